```python
import jax, jax.numpy as jnp
from jax import lax
import numpy as np

D_MODEL = 1024
BATCH = 8
SEQ = 8192
DEPTH = 1

MIX_WIDTH = D_MODEL
ATT_WIDTH = MIX_WIDTH // 2
GMLP_WIDTH = MIX_WIDTH - ATT_WIDTH
ATT_HEAD_DIM = 64
N_ATT_HEADS = ATT_WIDTH // ATT_HEAD_DIM
N_GMLP_GROUPS = 8
GMLP_GROUP_DIM = GMLP_WIDTH // N_GMLP_GROUPS
CHUNK = 128
Q_BLOCK = 128
D_FF = 2816
CONV_WIDTH = 3
EPS = 1e-6

Q_END = ATT_WIDTH
K_END = 2 * ATT_WIDTH
V_END = 3 * ATT_WIDTH
U_END = V_END + GMLP_WIDTH
G_END = U_END + GMLP_WIDTH
IN_COLS = G_END + N_ATT_HEADS

kernel_name = "fox_gmlp_convffn_hybrid"


def rmsnorm(x, gain):
    xf = x.astype(jnp.float32)
    y = xf * lax.rsqrt(jnp.mean(xf * xf, axis=-1, keepdims=True) + EPS)
    return y.astype(x.dtype) * gain


def forgetting_attention(q, k, v, log_f):
    B, S, H, Dh = q.shape
    nb = S // Q_BLOCK
    c = jnp.cumsum(log_f, axis=1)
    c_k = c.transpose(0, 2, 1)[:, :, None, :]
    key_pos = jnp.arange(S)
    qb = q.reshape(B, nb, Q_BLOCK, H, Dh).transpose(1, 0, 2, 3, 4)
    cb = c.reshape(B, nb, Q_BLOCK, H).transpose(1, 0, 3, 2)
    pb = key_pos.reshape(nb, Q_BLOCK)
    scale = Dh ** -0.5

    def block(args):
        qi, ci, pi = args
        logits = jnp.einsum('bqhd,bkhd->bhqk', qi, k).astype(jnp.float32) * scale
        logits = logits + ci[..., None] - c_k
        causal = pi[:, None] >= key_pos[None, :]
        logits = jnp.where(causal, logits, -jnp.inf)
        p = jax.nn.softmax(logits, axis=-1).astype(v.dtype)
        return jnp.einsum('bhqk,bkhd->bqhd', p, v)

    out = lax.map(block, (qb, cb, pb))
    return out.transpose(1, 0, 2, 3, 4).reshape(B, S, H * Dh)


def chunked_spatial_gating(u, vg, v_gain, w_s, b_s):
    B, S, G, Dg = u.shape
    nc = S // CHUNK
    vg = rmsnorm(vg, v_gain.reshape(G, Dg))
    vc = vg.reshape(B, nc, CHUNK, G, Dg)
    w = w_s * jnp.tril(jnp.ones((CHUNK, CHUNK), dtype=w_s.dtype))
    mixed = jnp.einsum('gij,bnjgd->bnigd', w, vc) + b_s.T[None, None, :, :, None]
    return (u * mixed.reshape(B, S, G, Dg)).reshape(B, S, G * Dg)


def conv_ffn(x, w_up, conv_w, conv_b, w_down):
    h = x @ w_up
    h = lax.conv_general_dilated(
        h, conv_w[:, None, :], window_strides=(1,),
        padding=[(CONV_WIDTH - 1, 0)],
        dimension_numbers=('NWC', 'WIO', 'NWC'),
        feature_group_count=h.shape[-1]) + conv_b
    a, g = jnp.split(h, 2, axis=-1)
    return (jax.nn.silu(g) * a) @ w_down


def _fwd_setup_inputs(seed: int = 0) -> dict:
    key = jax.random.key(seed)
    ks = jax.random.split(key, 16)
    f32 = jnp.float32
    x = jax.random.normal(ks[0], (BATCH, SEQ, D_MODEL), f32)
    norm_mix_g = 1.0 + 0.02 * jax.random.normal(ks[1], (DEPTH, D_MODEL), f32)
    w_in = jax.random.normal(ks[2], (DEPTH, D_MODEL, IN_COLS), f32) * D_MODEL ** -0.5
    b_forget = 2.0 + 0.5 * jax.random.normal(ks[3], (DEPTH, N_ATT_HEADS), f32)
    gmlp_norm_g = 1.0 + 0.02 * jax.random.normal(ks[4], (DEPTH, GMLP_WIDTH), f32)
    w_spatial = jax.random.normal(ks[5], (DEPTH, N_GMLP_GROUPS, CHUNK, CHUNK), f32) * CHUNK ** -0.5
    b_spatial = 1.0 + 0.1 * jax.random.normal(ks[6], (DEPTH, N_GMLP_GROUPS, CHUNK), f32)
    w_out = jax.random.normal(ks[7], (DEPTH, MIX_WIDTH, D_MODEL), f32) * MIX_WIDTH ** -0.5
    norm_ffn_g = 1.0 + 0.02 * jax.random.normal(ks[8], (DEPTH, D_MODEL), f32)
    w_up = jax.random.normal(ks[9], (DEPTH, D_MODEL, 2 * D_FF), f32) * D_MODEL ** -0.5
    conv_w = jax.random.normal(ks[10], (DEPTH, CONV_WIDTH, 2 * D_FF), f32) * CONV_WIDTH ** -0.5
    conv_b = 0.01 * jax.random.normal(ks[11], (DEPTH, 2 * D_FF), f32)
    w_down = jax.random.normal(ks[12], (DEPTH, D_FF, D_MODEL), f32) * D_FF ** -0.5
    norm_final_g = 1.0 + 0.02 * jax.random.normal(ks[13], (D_MODEL,), f32)
    return {"x": x, "norm_mix_g": norm_mix_g, "w_in": w_in, "b_forget": b_forget,
            "gmlp_norm_g": gmlp_norm_g, "w_spatial": w_spatial, "b_spatial": b_spatial,
            "w_out": w_out, "norm_ffn_g": norm_ffn_g, "w_up": w_up, "conv_w": conv_w,
            "conv_b": conv_b, "w_down": w_down, "norm_final_g": norm_final_g}


def _fwd_reference(x, norm_mix_g, w_in, b_forget, gmlp_norm_g, w_spatial, b_spatial,
              w_out, norm_ffn_g, w_up, conv_w, conv_b, w_down, norm_final_g):
    B, S, _ = x.shape
    h = x
    for layer in range(DEPTH):
        xn = rmsnorm(h, norm_mix_g[layer])
        proj = xn @ w_in[layer]
        q = proj[..., :Q_END].reshape(B, S, N_ATT_HEADS, ATT_HEAD_DIM)
        k = proj[..., Q_END:K_END].reshape(B, S, N_ATT_HEADS, ATT_HEAD_DIM)
        v = proj[..., K_END:V_END].reshape(B, S, N_ATT_HEADS, ATT_HEAD_DIM)
        log_f = jax.nn.log_sigmoid(
            (proj[..., G_END:] + b_forget[layer]).astype(jnp.float32))
        att = forgetting_attention(q, k, v, log_f)
        uv = jax.nn.gelu(proj[..., V_END:G_END])
        u = uv[..., :GMLP_WIDTH].reshape(B, S, N_GMLP_GROUPS, GMLP_GROUP_DIM)
        vg = uv[..., GMLP_WIDTH:].reshape(B, S, N_GMLP_GROUPS, GMLP_GROUP_DIM)
        sg = chunked_spatial_gating(u, vg, gmlp_norm_g[layer], w_spatial[layer], b_spatial[layer])
        mix = jnp.concatenate([att, sg], axis=-1)
        h = h + mix @ w_out[layer]
        h = h + conv_ffn(rmsnorm(h, norm_ffn_g[layer]), w_up[layer], conv_w[layer],
                         conv_b[layer], w_down[layer])
    return rmsnorm(h, norm_final_g)


import jax as _jax
import jax.numpy as _jnp

TWIN_FORMAT = 'train_step'
FWD_PARAMS = ['x', 'norm_mix_g', 'w_in', 'b_forget', 'gmlp_norm_g', 'w_spatial', 'b_spatial', 'w_out', 'norm_ffn_g', 'w_up', 'conv_w', 'conv_b', 'w_down', 'norm_final_g']
TWIN_WEIGHTS = ['norm_mix_g', 'w_in', 'b_forget', 'gmlp_norm_g', 'w_spatial', 'b_spatial', 'w_out', 'norm_ffn_g', 'w_up', 'conv_w', 'conv_b', 'w_down', 'norm_final_g']
TWIN_DIFF_INPUT = 'x'
TWIN_INPUTS = ['x', 'norm_mix_g', 'w_in', 'b_forget', 'gmlp_norm_g', 'w_spatial', 'b_spatial', 'w_out', 'norm_ffn_g', 'w_up', 'conv_w', 'conv_b', 'w_down', 'norm_final_g', 'loss_target', 'm_norm_mix_g', 'm_w_in', 'm_b_forget', 'm_gmlp_norm_g', 'm_w_spatial', 'm_b_spatial', 'm_w_out', 'm_norm_ffn_g', 'm_w_up', 'm_conv_w', 'm_conv_b', 'm_w_down', 'm_norm_final_g', 'v_norm_mix_g', 'v_w_in', 'v_b_forget', 'v_gmlp_norm_g', 'v_w_spatial', 'v_b_spatial', 'v_w_out', 'v_norm_ffn_g', 'v_w_up', 'v_conv_w', 'v_conv_b', 'v_w_down', 'v_norm_final_g']
TWIN_OUTPUTS = ['loss', 'grad_x', 'grad_norm_mix_g', 'grad_w_in', 'grad_b_forget', 'grad_gmlp_norm_g', 'grad_w_spatial', 'grad_b_spatial', 'grad_w_out', 'grad_norm_ffn_g', 'grad_w_up', 'grad_conv_w', 'grad_conv_b', 'grad_w_down', 'grad_norm_final_g', 'delta_norm_mix_g', 'delta_w_in', 'delta_b_forget', 'delta_gmlp_norm_g', 'delta_w_spatial', 'delta_b_spatial', 'delta_w_out', 'delta_norm_ffn_g', 'delta_w_up', 'delta_conv_w', 'delta_conv_b', 'delta_w_down', 'delta_norm_final_g', 'new_m_norm_mix_g', 'new_m_w_in', 'new_m_b_forget', 'new_m_gmlp_norm_g', 'new_m_w_spatial', 'new_m_b_spatial', 'new_m_w_out', 'new_m_norm_ffn_g', 'new_m_w_up', 'new_m_conv_w', 'new_m_conv_b', 'new_m_w_down', 'new_m_norm_final_g', 'new_v_norm_mix_g', 'new_v_w_in', 'new_v_b_forget', 'new_v_gmlp_norm_g', 'new_v_w_spatial', 'new_v_b_spatial', 'new_v_w_out', 'new_v_norm_ffn_g', 'new_v_w_up', 'new_v_conv_w', 'new_v_conv_b', 'new_v_w_down', 'new_v_norm_final_g']
TWIN_LEAF_KINDS = {'loss': 'loss', 'grad_x': 'grad_x', 'grad_norm_mix_g': 'grad_w', 'grad_w_in': 'grad_w', 'grad_b_forget': 'grad_w', 'grad_gmlp_norm_g': 'grad_w', 'grad_w_spatial': 'grad_w', 'grad_b_spatial': 'grad_w', 'grad_w_out': 'grad_w', 'grad_norm_ffn_g': 'grad_w', 'grad_w_up': 'grad_w', 'grad_conv_w': 'grad_w', 'grad_conv_b': 'grad_w', 'grad_w_down': 'grad_w', 'grad_norm_final_g': 'grad_w', 'delta_norm_mix_g': 'delta_w', 'delta_w_in': 'delta_w', 'delta_b_forget': 'delta_w', 'delta_gmlp_norm_g': 'delta_w', 'delta_w_spatial': 'delta_w', 'delta_b_spatial': 'delta_w', 'delta_w_out': 'delta_w', 'delta_norm_ffn_g': 'delta_w', 'delta_w_up': 'delta_w', 'delta_conv_w': 'delta_w', 'delta_conv_b': 'delta_w', 'delta_w_down': 'delta_w', 'delta_norm_final_g': 'delta_w', 'new_m_norm_mix_g': 'new_m', 'new_m_w_in': 'new_m', 'new_m_b_forget': 'new_m', 'new_m_gmlp_norm_g': 'new_m', 'new_m_w_spatial': 'new_m', 'new_m_b_spatial': 'new_m', 'new_m_w_out': 'new_m', 'new_m_norm_ffn_g': 'new_m', 'new_m_w_up': 'new_m', 'new_m_conv_w': 'new_m', 'new_m_conv_b': 'new_m', 'new_m_w_down': 'new_m', 'new_m_norm_final_g': 'new_m', 'new_v_norm_mix_g': 'new_v', 'new_v_w_in': 'new_v', 'new_v_b_forget': 'new_v', 'new_v_gmlp_norm_g': 'new_v', 'new_v_w_spatial': 'new_v', 'new_v_b_spatial': 'new_v', 'new_v_w_out': 'new_v', 'new_v_norm_ffn_g': 'new_v', 'new_v_w_up': 'new_v', 'new_v_conv_w': 'new_v', 'new_v_conv_b': 'new_v', 'new_v_w_down': 'new_v', 'new_v_norm_final_g': 'new_v'}


def _forward(args):
    return _fwd_reference(*[args[k] for k in FWD_PARAMS])


def _output_shape():
    out = _jax.eval_shape(lambda: _forward(_fwd_setup_inputs(0)))
    return out.shape, out.dtype

N_MICROBATCH = 1
ADAM_LR = 0.001
ADAM_B1 = 0.9
ADAM_B2 = 0.999
ADAM_EPS = 1e-08
ADAM_WD = 0.01
ADAM_STEP = 10
PER_EXAMPLE_BATCH_AXIS = {'x': 0, 'loss_target': 0}
SHARED_INPUTS = []
_WEIGHT_DTYPES = {'norm_mix_g': _jnp.float32, 'w_in': _jnp.float32, 'b_forget': _jnp.float32, 'gmlp_norm_g': _jnp.float32, 'w_spatial': _jnp.float32, 'b_spatial': _jnp.float32, 'w_out': _jnp.float32, 'norm_ffn_g': _jnp.float32, 'w_up': _jnp.float32, 'conv_w': _jnp.float32, 'conv_b': _jnp.float32, 'w_down': _jnp.float32, 'norm_final_g': _jnp.float32}
MOMENT_SCALE = {'norm_mix_g': 1.888421e-01, 'w_in': 1.181925e-01, 'b_forget': 7.169485e-01, 'gmlp_norm_g': 1.077996e-01, 'w_spatial': 7.530803e-02, 'b_spatial': 1.059116e-01, 'w_out': 1.505123e-01, 'norm_ffn_g': 1.688349e-01, 'w_up': 7.114314e-02, 'conv_w': 7.199057e-02, 'conv_b': 7.032040e-02, 'w_down': 1.154854e-01, 'norm_final_g': 6.401971e+01}


def _to_microbatches(a, axis):
    t = _jnp.moveaxis(a, axis, 0)
    t = t.reshape((N_MICROBATCH, t.shape[0] // N_MICROBATCH) + t.shape[1:])
    return _jnp.moveaxis(t, 1, axis + 1)


def setup_inputs(seed: int = 0) -> dict:
    inp = _fwd_setup_inputs(seed)
    key = _jax.random.fold_in(_jax.random.key(seed), 7919)
    shape, _ = _output_shape()
    out = dict(inp)
    out["loss_target"] = _jax.random.normal(_jax.random.fold_in(key, 0), shape, _jnp.float32)
    for i, name in enumerate(TWIN_WEIGHTS):
        w = inp[name].astype(_jnp.float32)
        if MOMENT_SCALE is None:
            s = _jnp.sqrt(_jnp.mean(_jnp.square(w)) + 1e-30)
        else:
            s = MOMENT_SCALE[name]
        km, kv = _jax.random.split(_jax.random.fold_in(key, i + 1))
        out[name] = w
        out["m_" + name] = s * _jax.random.normal(km, w.shape, _jnp.float32)
        out["v_" + name] = (s * s) * _jax.random.uniform(kv, w.shape, _jnp.float32, 0.5, 1.5)
    if N_MICROBATCH > 1:
        for name, axis in PER_EXAMPLE_BATCH_AXIS.items():
            out[name] = _to_microbatches(out[name], axis)
    return {'x': out['x'], 'norm_mix_g': out['norm_mix_g'], 'w_in': out['w_in'], 'b_forget': out['b_forget'], 'gmlp_norm_g': out['gmlp_norm_g'], 'w_spatial': out['w_spatial'], 'b_spatial': out['b_spatial'], 'w_out': out['w_out'], 'norm_ffn_g': out['norm_ffn_g'], 'w_up': out['w_up'], 'conv_w': out['conv_w'], 'conv_b': out['conv_b'], 'w_down': out['w_down'], 'norm_final_g': out['norm_final_g'], 'loss_target': out['loss_target'], 'm_norm_mix_g': out['m_norm_mix_g'], 'm_w_in': out['m_w_in'], 'm_b_forget': out['m_b_forget'], 'm_gmlp_norm_g': out['m_gmlp_norm_g'], 'm_w_spatial': out['m_w_spatial'], 'm_b_spatial': out['m_b_spatial'], 'm_w_out': out['m_w_out'], 'm_norm_ffn_g': out['m_norm_ffn_g'], 'm_w_up': out['m_w_up'], 'm_conv_w': out['m_conv_w'], 'm_conv_b': out['m_conv_b'], 'm_w_down': out['m_w_down'], 'm_norm_final_g': out['m_norm_final_g'], 'v_norm_mix_g': out['v_norm_mix_g'], 'v_w_in': out['v_w_in'], 'v_b_forget': out['v_b_forget'], 'v_gmlp_norm_g': out['v_gmlp_norm_g'], 'v_w_spatial': out['v_w_spatial'], 'v_b_spatial': out['v_b_spatial'], 'v_w_out': out['v_w_out'], 'v_norm_ffn_g': out['v_norm_ffn_g'], 'v_w_up': out['v_w_up'], 'v_conv_w': out['v_conv_w'], 'v_conv_b': out['v_conv_b'], 'v_w_down': out['v_w_down'], 'v_norm_final_g': out['v_norm_final_g']}


def _loss(weights, diff, rest, loss_target):
    with _jax.named_scope("forward"):
        args = {**rest, TWIN_DIFF_INPUT: diff, **{k: w.astype(_WEIGHT_DTYPES[k]) for k, w in weights.items()}}
        y = _forward(args)
    with _jax.named_scope("loss_head"):
        err = _jnp.square(y.astype(_jnp.float32) - loss_target)
        return 0.5 * _jnp.sum(_jnp.mean(err, axis=-1)) if err.ndim else 0.5 * err


def _adamw(w, g, m, v):
    m = ADAM_B1 * m + (1.0 - ADAM_B1) * g
    v = ADAM_B2 * v + (1.0 - ADAM_B2) * _jnp.square(g)
    m_hat = m / (1.0 - ADAM_B1 ** ADAM_STEP)
    v_hat = v / (1.0 - ADAM_B2 ** ADAM_STEP)
    delta = -ADAM_LR * (m_hat / (_jnp.sqrt(v_hat) + ADAM_EPS) + ADAM_WD * w)
    return delta, m, v


def reference(x, norm_mix_g, w_in, b_forget, gmlp_norm_g, w_spatial, b_spatial, w_out, norm_ffn_g, w_up, conv_w, conv_b, w_down, norm_final_g, loss_target, m_norm_mix_g, m_w_in, m_b_forget, m_gmlp_norm_g, m_w_spatial, m_b_spatial, m_w_out, m_norm_ffn_g, m_w_up, m_conv_w, m_conv_b, m_w_down, m_norm_final_g, v_norm_mix_g, v_w_in, v_b_forget, v_gmlp_norm_g, v_w_spatial, v_b_spatial, v_w_out, v_norm_ffn_g, v_w_up, v_conv_w, v_conv_b, v_w_down, v_norm_final_g):
    given = dict(x=x, norm_mix_g=norm_mix_g, w_in=w_in, b_forget=b_forget, gmlp_norm_g=gmlp_norm_g, w_spatial=w_spatial, b_spatial=b_spatial, w_out=w_out, norm_ffn_g=norm_ffn_g, w_up=w_up, conv_w=conv_w, conv_b=conv_b, w_down=w_down, norm_final_g=norm_final_g, loss_target=loss_target, m_norm_mix_g=m_norm_mix_g, m_w_in=m_w_in, m_b_forget=m_b_forget, m_gmlp_norm_g=m_gmlp_norm_g, m_w_spatial=m_w_spatial, m_b_spatial=m_b_spatial, m_w_out=m_w_out, m_norm_ffn_g=m_norm_ffn_g, m_w_up=m_w_up, m_conv_w=m_conv_w, m_conv_b=m_conv_b, m_w_down=m_w_down, m_norm_final_g=m_norm_final_g, v_norm_mix_g=v_norm_mix_g, v_w_in=v_w_in, v_b_forget=v_b_forget, v_gmlp_norm_g=v_gmlp_norm_g, v_w_spatial=v_w_spatial, v_b_spatial=v_b_spatial, v_w_out=v_w_out, v_norm_ffn_g=v_norm_ffn_g, v_w_up=v_w_up, v_conv_w=v_conv_w, v_conv_b=v_conv_b, v_w_down=v_w_down, v_norm_final_g=v_norm_final_g)
    weights = {n: given[n] for n in TWIN_WEIGHTS}
    shared = {n: given[n] for n in SHARED_INPUTS}
    per_example = {n: given[n] for n in ['x']}
    grad_fn = _jax.value_and_grad(_loss, argnums=(0, 1))

    def one_microbatch(ex, loss_target):
        ex = dict(ex)
        diff = ex.pop(TWIN_DIFF_INPUT)
        return grad_fn(weights, diff, {**shared, **ex}, loss_target)

    if N_MICROBATCH == 1:
        loss, (grad_w, grad_x) = one_microbatch(per_example, given["loss_target"])
    else:
        def body(carry, xs):
            loss_sum, grad_sum = carry
            l_k, (gw_k, gx_k) = one_microbatch(xs[0], xs[1])
            with _jax.named_scope("update"):
                return (loss_sum + l_k, _jax.tree.map(_jnp.add, grad_sum, gw_k)), gx_k

        init = (_jnp.zeros((), _jnp.float32), _jax.tree.map(_jnp.zeros_like, weights))
        (loss, grad_w), grad_x = _jax.lax.scan(body, init, (per_example, given["loss_target"]))
    with _jax.named_scope("update"):
        delta_w, new_m, new_v = {}, {}, {}
        for n in TWIN_WEIGHTS:
            delta_w[n], new_m[n], new_v[n] = _adamw(weights[n], grad_w[n], given["m_" + n], given["v_" + n])
    return (loss, grad_x, *[grad_w[n] for n in TWIN_WEIGHTS], *[delta_w[n] for n in TWIN_WEIGHTS],
            *[new_m[n] for n in TWIN_WEIGHTS], *[new_v[n] for n in TWIN_WEIGHTS])
```

```python
import functools
import math

import jax
import jax.numpy as jnp
from jax import lax
from jax.experimental import pallas as pl
from jax.experimental.pallas import tpu as pltpu

F32 = jnp.float32
BF16 = jnp.bfloat16

N_DEV = 8
D_MODEL = 1024
ATT_WIDTH = 512
GMLP_WIDTH = 512
HEAD_DIM = 64
N_HEADS = 8
N_PAIRS = 4
N_GROUPS = 8
GROUP_DIM = 64
CHUNK = 128
D_FF = 2816
IN_COLS = 2568
IN_PAD = 2688
QKV = 1536
UG_END = 2560
EPS = 1e-6
LANES = 128

ADAM_LR = 0.001
ADAM_B1 = 0.9
ADAM_B2 = 0.999
ADAM_EPS = 1e-08
ADAM_WD = 0.01
ADAM_STEP = 10

VMEM_LIMIT = 56 * 1024 * 1024
MESH = pl.DeviceIdType.MESH


def _cp(sem, vmem=None):
    return pltpu.CompilerParams(dimension_semantics=sem, vmem_limit_bytes=vmem)


def _pick(n, prefs):
    for p in prefs:
        if n % p == 0:
            return p
    return n


def _split3(x):
    hi = x.astype(BF16)
    r1 = x - hi.astype(F32)
    mid = r1.astype(BF16)
    lo = (r1 - mid.astype(F32)).astype(BF16)
    return hi, mid, lo


def _dot3(x, ones_bf):
    hi, mid, lo = _split3(x)
    d = functools.partial(jnp.dot, preferred_element_type=F32)
    return d(hi, ones_bf) + d(mid, ones_bf) + d(lo, ones_bf)


def _dot3l(ones_bf, x):
    hi, mid, lo = _split3(x)
    d = functools.partial(jnp.dot, preferred_element_type=F32)
    return d(ones_bf, hi) + d(ones_bf, mid) + d(ones_bf, lo)


def _gelu(x):
    k = math.sqrt(2.0 / math.pi)
    t = jnp.tanh(k * (x + 0.044715 * (x * x * x)))
    return 0.5 * x * (1.0 + t)


def _gelu_grad(x):
    k = math.sqrt(2.0 / math.pi)
    x2 = x * x
    t = jnp.tanh(k * (x + 0.044715 * (x2 * x)))
    return 0.5 * (1.0 + t) + 0.5 * x * (1.0 - t * t) * (k * (1.0 + 3.0 * 0.044715 * x2))


def _sigmoid(x):
    return 1.0 / (1.0 + jnp.exp(-x))


def _mm(a, b, *, mode, out_dtype, tm, tn, tk, name, res=None, a_halves=False, b_halves=False,
        out_halves=False, outer="i"):
    if mode == "tn":
        K, M = a.shape[-2], a.shape[-1] * (2 if a_halves else 1)
    else:
        M, K = a.shape[-2], a.shape[-1] * (2 if a_halves else 1)
    if mode == "nt":
        N = b.shape[-2]
        assert b.shape[-1] == K
    else:
        N = b.shape[-1] * (2 if b_halves else 1)
    tm, tn, tk = min(tm, M), min(tn, N), min(tk, K)
    assert M % tm == 0 and N % tn == 0 and K % tk == 0, (name, M, N, K, tm, tn, tk)
    nm, nn, nk = M // tm, N // tn, K // tk

    def ij(g0, g1):
        return (g0, g1) if outer == "i" else (g1, g0)

    if mode == "nn":
        dims = (((1,), (0,)), ((), ()))
        if a_halves:
            nkh = nk // 2
            a_spec = pl.BlockSpec((None, tm, tk), lambda g0, g1, k: (k // nkh, ij(g0, g1)[0], k % nkh))
        else:
            a_spec = pl.BlockSpec((tm, tk), lambda g0, g1, k: (ij(g0, g1)[0], k))
        b_spec = pl.BlockSpec((tk, tn), lambda g0, g1, k: (k, ij(g0, g1)[1]))
    elif mode == "nt":
        dims = (((1,), (1,)), ((), ()))
        if a_halves:
            nkh = nk // 2
            a_spec = pl.BlockSpec((None, tm, tk), lambda g0, g1, k: (k // nkh, ij(g0, g1)[0], k % nkh))
        else:
            a_spec = pl.BlockSpec((tm, tk), lambda g0, g1, k: (ij(g0, g1)[0], k))
        b_spec = pl.BlockSpec((tn, tk), lambda g0, g1, k: (ij(g0, g1)[1], k))
    else:
        dims = (((0,), (0,)), ((), ()))
        if a_halves:
            nmh = nm // 2
            a_spec = pl.BlockSpec((None, tk, tm), lambda g0, g1, k: (ij(g0, g1)[0] // nmh, k, ij(g0, g1)[0] % nmh))
        else:
            a_spec = pl.BlockSpec((tk, tm), lambda g0, g1, k: (k, ij(g0, g1)[0]))
        if b_halves:
            nnh = nn // 2
            b_spec = pl.BlockSpec((None, tk, tn), lambda g0, g1, k: (ij(g0, g1)[1] // nnh, k, ij(g0, g1)[1] % nnh))
        else:
            b_spec = pl.BlockSpec((tk, tn), lambda g0, g1, k: (k, ij(g0, g1)[1]))
    if out_halves:
        nnh = nn // 2
        o_spec = pl.BlockSpec((None, tm, tn), lambda g0, g1, k: (ij(g0, g1)[1] // nnh, ij(g0, g1)[0], ij(g0, g1)[1] % nnh))
        o_shape = jax.ShapeDtypeStruct((2, M, N // 2), out_dtype)
    else:
        o_spec = pl.BlockSpec((tm, tn), lambda g0, g1, k: ij(g0, g1))
        o_shape = jax.ShapeDtypeStruct((M, N), out_dtype)
    in_specs = [a_spec, b_spec]
    args = [a, b]
    if res is not None:
        in_specs.append(pl.BlockSpec((tm, tn), lambda g0, g1, k: ij(g0, g1)))
        args.append(res)

    def body(*refs):
        if res is not None:
            a_ref, b_ref, r_ref, o_ref = refs[:4]
        else:
            a_ref, b_ref, o_ref = refs[:3]
            r_ref = None
        part = lax.dot_general(a_ref[...], b_ref[...], dims, preferred_element_type=F32)
        if nk == 1:
            if r_ref is not None:
                part = part + r_ref[...]
            o_ref[...] = part.astype(out_dtype)
            return
        acc_ref = refs[-1]
        k = pl.program_id(2)

        @pl.when(k == 0)
        def _():
            acc_ref[...] = part

        @pl.when(k > 0)
        def _():
            acc_ref[...] += part

        @pl.when(k == nk - 1)
        def _():
            tot = acc_ref[...]
            if r_ref is not None:
                tot = tot + r_ref[...]
            o_ref[...] = tot.astype(out_dtype)

    grid = (nm, nn, nk) if outer == "i" else (nn, nm, nk)
    scratch = [] if nk == 1 else [pltpu.VMEM((tm, tn), F32)]
    return pl.pallas_call(
        body, out_shape=o_shape, grid=grid, in_specs=in_specs, out_specs=o_spec, scratch_shapes=scratch,
        name=name, compiler_params=_cp(("parallel", "parallel", "arbitrary"), VMEM_LIMIT),
    )(*args)


def _inproj_fwd(x, g_mix, w_pad, bf_pad):
    S = x.shape[0]
    tm = _pick(S, (256,))
    tri = (lax.broadcasted_iota(jnp.int32, (tm, tm), 0) >= lax.broadcasted_iota(jnp.int32, (tm, tm), 1)).astype(BF16)

    def body(x_ref, g_ref, w_ref, bf_ref, tri_ref, xn_ref, qkv_ref, ug_ref, zf_ref, c_ref, carry_ref):
        i = pl.program_id(0)

        @pl.when(i == 0)
        def _():
            carry_ref[...] = jnp.zeros_like(carry_ref)

        xf = x_ref[...]
        r = lax.rsqrt(jnp.mean(xf * xf, axis=-1, keepdims=True) + EPS)
        xn = ((xf * r) * g_ref[...]).astype(BF16)
        xn_ref[...] = xn
        proj = jnp.dot(xn, w_ref[...], preferred_element_type=F32)
        qkv_ref[:, :ATT_WIDTH] = (proj[:, :ATT_WIDTH] * (HEAD_DIM ** -0.5)).astype(BF16)
        qkv_ref[:, ATT_WIDTH:] = proj[:, ATT_WIDTH:QKV].astype(BF16)
        ug_ref[...] = proj[:, QKV:UG_END]
        zf = proj[:, UG_END:] + bf_ref[...]
        zf_ref[...] = zf
        lf = jnp.minimum(zf, 0.0) - jnp.log(1.0 + jnp.exp(-jnp.abs(zf)))
        c = _dot3l(tri_ref[...], lf) + carry_ref[0:1, :]
        c_ref[...] = c
        carry_ref[0:1, :] = c[tm - 1:tm, :]

    return pl.pallas_call(
        body,
        out_shape=(jax.ShapeDtypeStruct((S, D_MODEL), BF16), jax.ShapeDtypeStruct((S, QKV), BF16),
                   jax.ShapeDtypeStruct((S, 2 * GMLP_WIDTH), F32), jax.ShapeDtypeStruct((S, LANES), F32),
                   jax.ShapeDtypeStruct((S, LANES), F32)),
        grid=(S // tm,),
        in_specs=[pl.BlockSpec((tm, D_MODEL), lambda i: (i, 0)), pl.BlockSpec((1, D_MODEL), lambda i: (0, 0)),
                  pl.BlockSpec((D_MODEL, IN_PAD), lambda i: (0, 0)), pl.BlockSpec((1, LANES), lambda i: (0, 0)),
                  pl.BlockSpec((tm, tm), lambda i: (0, 0))],
        out_specs=(pl.BlockSpec((tm, D_MODEL), lambda i: (i, 0)), pl.BlockSpec((tm, QKV), lambda i: (i, 0)),
                   pl.BlockSpec((tm, 2 * GMLP_WIDTH), lambda i: (i, 0)), pl.BlockSpec((tm, LANES), lambda i: (i, 0)),
                   pl.BlockSpec((tm, LANES), lambda i: (i, 0))),
        scratch_shapes=[pltpu.VMEM((8, LANES), F32)],
        name="inproj_fwd", compiler_params=_cp(("arbitrary",), VMEM_LIMIT),
    )(x, g_mix, w_pad, bf_pad, tri)


def _group_ones():
    r = lax.broadcasted_iota(jnp.int32, (GMLP_WIDTH, GMLP_WIDTH), 0) // GROUP_DIM
    c = lax.broadcasted_iota(jnp.int32, (GMLP_WIDTH, GMLP_WIDTH), 1) // GROUP_DIM
    return (r == c).astype(BF16)


def _gmlp_mixed(vn_bf, w_ref, bias, n_chunks):
    lane = lax.broadcasted_iota(jnp.int32, (CHUNK, LANES), 1)
    row = lax.broadcasted_iota(jnp.int32, (CHUNK, CHUNK), 0)
    col = lax.broadcasted_iota(jnp.int32, (CHUNK, CHUNK), 1)
    ws = [jnp.where(row >= col, w_ref[g], 0.0).astype(BF16) for g in range(N_GROUPS)]
    rows = []
    for ci in range(n_chunks):
        cols = []
        for pp in range(N_GROUPS // 2):
            v = vn_bf[ci * CHUNK:(ci + 1) * CHUNK, pp * LANES:(pp + 1) * LANES]
            v_lo = jnp.where(lane < GROUP_DIM, v, jnp.zeros_like(v))
            v_hi = jnp.where(lane >= GROUP_DIM, v, jnp.zeros_like(v))
            m = (jnp.dot(ws[2 * pp], v_lo, preferred_element_type=F32)
                 + jnp.dot(ws[2 * pp + 1], v_hi, preferred_element_type=F32))
            cols.append(m + bias[:, pp * LANES:(pp + 1) * LANES])
        rows.append(jnp.concatenate(cols, axis=1))
    return jnp.concatenate(rows, axis=0)


def _gmlp_fwd(ug, gain, w_s, bias_full):
    S = ug.shape[0]
    tm = _pick(S, (512, 256, 128))
    ones = _group_ones()

    def body(ug_ref, gain_ref, w_ref, bias_ref, ones_ref, sg_ref):
        u = _gelu(ug_ref[:, :GMLP_WIDTH])
        vr = _gelu(ug_ref[:, GMLP_WIDTH:])
        ms = _dot3(vr * vr, ones_ref[...]) * (1.0 / GROUP_DIM)
        vn = ((vr * lax.rsqrt(ms + EPS)) * gain_ref[...]).astype(BF16)
        mixed = _gmlp_mixed(vn, w_ref, bias_ref[...], tm // CHUNK)
        sg_ref[...] = (u * mixed).astype(BF16)

    return pl.pallas_call(
        body, out_shape=jax.ShapeDtypeStruct((S, GMLP_WIDTH), BF16), grid=(S // tm,),
        in_specs=[pl.BlockSpec((tm, 2 * GMLP_WIDTH), lambda i: (i, 0)), pl.BlockSpec((1, GMLP_WIDTH), lambda i: (0, 0)),
                  pl.BlockSpec((N_GROUPS, CHUNK, CHUNK), lambda i: (0, 0, 0)),
                  pl.BlockSpec((CHUNK, GMLP_WIDTH), lambda i: (0, 0)),
                  pl.BlockSpec((GMLP_WIDTH, GMLP_WIDTH), lambda i: (0, 0))],
        out_specs=pl.BlockSpec((tm, GMLP_WIDTH), lambda i: (i, 0)),
        name="gmlp_fwd", compiler_params=_cp(("parallel",), VMEM_LIMIT),
    )(ug, gain, w_s, bias_full, ones)


def _attn_fwd(qkv, c_cols, c_rows):
    S = qkv.shape[0]
    tq = _pick(S, (256,))
    tk = tq
    nq = S // tq

    def body(q_ref, k_ref, v_ref, cc_ref, cr_ref, o_ref, lse_ref):
        qi = pl.program_id(1)
        lane = lax.broadcasted_iota(jnp.int32, (tq, LANES), 1)
        rid = lax.broadcasted_iota(jnp.int32, (tq, tk), 0)
        cid = lax.broadcasted_iota(jnp.int32, (tq, tk), 1)
        q = q_ref[...]
        outs, lses = [], []
        for hh in range(2):
            sel = (lane >= GROUP_DIM) if hh else (lane < GROUP_DIM)
            qm = jnp.where(sel, q, jnp.zeros_like(q))
            ct = cc_ref[:, hh * HEAD_DIM:hh * HEAD_DIM + 1]

            def step(kb, carry, qm=qm, ct=ct, hh=hh):
                m, l, acc = carry
                ks = pl.multiple_of(kb * tk, tk)
                kblk = k_ref[pl.ds(ks, tk), :]
                vblk = v_ref[pl.ds(ks, tk), :]
                s = lax.dot_general(qm, kblk, (((1,), (1,)), ((), ())), preferred_element_type=F32)
                s = s + (ct - cr_ref[hh:hh + 1, pl.ds(ks, tk)])
                s = jnp.where((kb < qi) | (rid >= cid), s, -jnp.inf)
                m_new = jnp.maximum(m, jnp.max(s, axis=-1, keepdims=True))
                alpha = jnp.exp(m - m_new)
                p = jnp.exp(s - m_new)
                l_new = alpha * l + jnp.sum(p, axis=-1, keepdims=True)
                acc_new = alpha * acc + jnp.dot(p.astype(BF16), vblk, preferred_element_type=F32)
                return m_new, l_new, acc_new

            init = (jnp.full((tq, 1), -jnp.inf, F32), jnp.zeros((tq, 1), F32), jnp.zeros((tq, LANES), F32))
            m, l, acc = lax.fori_loop(0, qi + 1, step, init)
            outs.append(acc / l)
            lses.append(m + jnp.log(l))
        o_ref[...] = jnp.where(lane < GROUP_DIM, outs[0], outs[1])
        lse_ref[...] = jnp.where(lane < GROUP_DIM, lses[0], lses[1])

    return pl.pallas_call(
        body,
        out_shape=(jax.ShapeDtypeStruct((S, ATT_WIDTH), F32), jax.ShapeDtypeStruct((S, ATT_WIDTH), F32)),
        grid=(N_PAIRS, nq),
        in_specs=[pl.BlockSpec((tq, LANES), lambda p, i: (i, p)),
                  pl.BlockSpec((S, LANES), lambda p, i: (0, N_PAIRS + p)),
                  pl.BlockSpec((S, LANES), lambda p, i: (0, 2 * N_PAIRS + p)),
                  pl.BlockSpec((None, tq, LANES), lambda p, i: (p, i, 0)),
                  pl.BlockSpec((None, 8, S), lambda p, i: (p, 0, 0))],
        out_specs=(pl.BlockSpec((tq, LANES), lambda p, i: (i, p)), pl.BlockSpec((tq, LANES), lambda p, i: (i, p))),
        name="attn_fwd", compiler_params=_cp(("parallel", "parallel"), VMEM_LIMIT),
    )(qkv, qkv, qkv, c_cols, c_rows)


def _rms_fwd(h, g):
    S = h.shape[0]
    tm = _pick(S, (512, 256))

    def body(h_ref, g_ref, o_ref):
        hf = h_ref[...]
        r = lax.rsqrt(jnp.mean(hf * hf, axis=-1, keepdims=True) + EPS)
        o_ref[...] = ((hf * r) * g_ref[...]).astype(BF16)

    return pl.pallas_call(
        body, out_shape=jax.ShapeDtypeStruct(h.shape, BF16), grid=(S // tm,),
        in_specs=[pl.BlockSpec((tm, D_MODEL), lambda i: (i, 0)), pl.BlockSpec((1, D_MODEL), lambda i: (0, 0))],
        out_specs=pl.BlockSpec((tm, D_MODEL), lambda i: (i, 0)),
        name="rms_fwd", compiler_params=_cp(("parallel",)),
    )(h, g)


def _shift_rows(x, prev, n):
    rid = lax.broadcasted_iota(jnp.int32, x.shape, 0)
    y = pltpu.roll(x, n, 0)
    if n == 1:
        return jnp.where(rid == 0, prev[7:8, :], y)
    return jnp.where(rid == 0, prev[6:7, :], jnp.where(rid == 1, prev[7:8, :], y))


def _shift_rows_up(x, nxt, n):
    rows = x.shape[0]
    rid = lax.broadcasted_iota(jnp.int32, x.shape, 0)
    y = pltpu.roll(x, rows - n, 0)
    if n == 1:
        return jnp.where(rid == rows - 1, nxt[0:1, :], y)
    return jnp.where(rid == rows - 2, nxt[0:1, :], jnp.where(rid == rows - 1, nxt[1:2, :], y))


def _conv3(cur, prev, w, b):
    return (w[0:1, :] * _shift_rows(cur, prev, 2) + w[1:2, :] * _shift_rows(cur, prev, 1)
            + w[2:3, :] * cur + b)


def _conv_act_fwd(hu, cw, cb):
    _, S, F = hu.shape
    tm = _pick(S, (512, 256))
    tn = _pick(F, (256, 128))
    r8 = tm // 8

    def body(cur_ref, prev_ref, w_ref, b_ref, o_ref):
        i = pl.program_id(1)
        halves = []
        for h in range(2):
            prev = jnp.where(i > 0, prev_ref[h], 0.0)
            halves.append(_conv3(cur_ref[h], prev, w_ref[h], b_ref[h]))
        a, g = halves
        o_ref[...] = (g * _sigmoid(g) * a).astype(BF16)

    return pl.pallas_call(
        body, out_shape=jax.ShapeDtypeStruct((S, F), BF16), grid=(F // tn, S // tm),
        in_specs=[pl.BlockSpec((2, tm, tn), lambda j, i: (0, i, j)),
                  pl.BlockSpec((2, 8, tn), lambda j, i: (0, jnp.maximum(i * r8 - 1, 0), j)),
                  pl.BlockSpec((2, 8, tn), lambda j, i: (0, 0, j)),
                  pl.BlockSpec((2, 1, tn), lambda j, i: (0, 0, j))],
        out_specs=pl.BlockSpec((tm, tn), lambda j, i: (i, j)),
        name="conv_act_fwd", compiler_params=_cp(("parallel", "parallel"), VMEM_LIMIT),
    )(hu, hu, cw, cb)


def _loss_head(h2, g_final, target):
    S = h2.shape[0]
    tm = _pick(S, (512, 256))

    def body(h_ref, g_ref, t_ref, loss_ref, dh_ref, dhb_ref, dg_ref):
        i = pl.program_id(0)

        @pl.when(i == 0)
        def _():
            loss_ref[...] = jnp.zeros_like(loss_ref)
            dg_ref[...] = jnp.zeros_like(dg_ref)

        hf = h_ref[...]
        g = g_ref[...]
        r = lax.rsqrt(jnp.mean(hf * hf, axis=-1, keepdims=True) + EPS)
        hhat = hf * r
        err = hhat * g - t_ref[...]
        loss_ref[...] += 0.5 * jnp.sum(jnp.mean(err * err, axis=-1, keepdims=True))
        dy = err * (1.0 / D_MODEL)
        dg_ref[0:1, :] += jnp.sum(dy * hhat, axis=0, keepdims=True)
        dhat = dy * g
        dh = r * (dhat - hhat * jnp.mean(dhat * hhat, axis=-1, keepdims=True))
        dh_ref[...] = dh
        dhb_ref[...] = dh.astype(BF16)

    return pl.pallas_call(
        body,
        out_shape=(jax.ShapeDtypeStruct((8, LANES), F32), jax.ShapeDtypeStruct((S, D_MODEL), F32),
                   jax.ShapeDtypeStruct((S, D_MODEL), BF16), jax.ShapeDtypeStruct((8, D_MODEL), F32)),
        grid=(S // tm,),
        in_specs=[pl.BlockSpec((tm, D_MODEL), lambda i: (i, 0)), pl.BlockSpec((1, D_MODEL), lambda i: (0, 0)),
                  pl.BlockSpec((tm, D_MODEL), lambda i: (i, 0))],
        out_specs=(pl.BlockSpec((8, LANES), lambda i: (0, 0)), pl.BlockSpec((tm, D_MODEL), lambda i: (i, 0)),
                   pl.BlockSpec((tm, D_MODEL), lambda i: (i, 0)), pl.BlockSpec((8, D_MODEL), lambda i: (0, 0))),
        name="loss_head", compiler_params=_cp(("arbitrary",), VMEM_LIMIT),
    )(h2, g_final, target)


def _rms_bwd(h, g, dy, res):
    S = h.shape[0]
    tm = _pick(S, (512, 256))

    def body(h_ref, g_ref, dy_ref, r_ref, dh_ref, dhb_ref, dg_ref):
        i = pl.program_id(0)

        @pl.when(i == 0)
        def _():
            dg_ref[...] = jnp.zeros_like(dg_ref)

        hf = h_ref[...]
        dyv = dy_ref[...]
        r = lax.rsqrt(jnp.mean(hf * hf, axis=-1, keepdims=True) + EPS)
        hhat = hf * r
        dg_ref[0:1, :] += jnp.sum(dyv * hhat, axis=0, keepdims=True)
        dhat = dyv * g_ref[...]
        dh = r_ref[...] + r * (dhat - hhat * jnp.mean(dhat * hhat, axis=-1, keepdims=True))
        dh_ref[...] = dh
        dhb_ref[...] = dh.astype(BF16)

    return pl.pallas_call(
        body,
        out_shape=(jax.ShapeDtypeStruct((S, D_MODEL), F32), jax.ShapeDtypeStruct((S, D_MODEL), BF16),
                   jax.ShapeDtypeStruct((8, D_MODEL), F32)),
        grid=(S // tm,),
        in_specs=[pl.BlockSpec((tm, D_MODEL), lambda i: (i, 0)), pl.BlockSpec((1, D_MODEL), lambda i: (0, 0)),
                  pl.BlockSpec((tm, D_MODEL), lambda i: (i, 0)), pl.BlockSpec((tm, D_MODEL), lambda i: (i, 0))],
        out_specs=(pl.BlockSpec((tm, D_MODEL), lambda i: (i, 0)), pl.BlockSpec((tm, D_MODEL), lambda i: (i, 0)),
                   pl.BlockSpec((8, D_MODEL), lambda i: (0, 0))),
        name="rms_bwd", compiler_params=_cp(("arbitrary",), VMEM_LIMIT),
    )(h, g, dy, res)


def _conv_act_bwd(hu, dact, cw, cb):
    _, S, F = hu.shape
    tm = _pick(S, (256,))
    tn = _pick(F, (256, 128))
    r8 = tm // 8
    n_i = S // tm
    last8 = S // 8 - 1

    def body(cur_ref, prev_ref, next_ref, da_ref, dan_ref, w_ref, b_ref, dhu_ref, dcw_ref):
        i = pl.program_id(1)

        @pl.when(i == 0)
        def _():
            dcw_ref[...] = jnp.zeros_like(dcw_ref)

        rid8 = lax.broadcasted_iota(jnp.int32, (8, tn), 0)
        cur = [cur_ref[0], cur_ref[1]]
        prev = [jnp.where(i > 0, prev_ref[h], 0.0) for h in range(2)]
        nxt = [next_ref[0], next_ref[1]]
        w = [w_ref[0], w_ref[1]]

        def gate_grads(a, g, d):
            sg = _sigmoid(g)
            return d * (g * sg), d * a * (sg * (1.0 + g * (1.0 - sg)))

        a = _conv3(cur[0], prev[0], w[0], b_ref[0])
        g = _conv3(cur[1], prev[1], w[1], b_ref[1])
        dhc = gate_grads(a, g, da_ref[...])
        a_n = _conv3(nxt[0], cur[0][tm - 8:, :], w[0], b_ref[0])
        g_n = _conv3(nxt[1], cur[1][tm - 8:, :], w[1], b_ref[1])
        dhc_n = gate_grads(a_n, g_n, dan_ref[...])
        for h in range(2):
            d = dhc[h]
            dn = jnp.where(i < n_i - 1, dhc_n[h], 0.0)
            dhu = (w[h][2:3, :] * d + w[h][1:2, :] * _shift_rows_up(d, dn, 1)
                   + w[h][0:1, :] * _shift_rows_up(d, dn, 2))
            dhu_ref[h] = dhu.astype(BF16)
            t0 = jnp.sum(d * _shift_rows(cur[h], prev[h], 2), axis=0, keepdims=True)
            t1 = jnp.sum(d * _shift_rows(cur[h], prev[h], 1), axis=0, keepdims=True)
            t2 = jnp.sum(d * cur[h], axis=0, keepdims=True)
            t3 = jnp.sum(d, axis=0, keepdims=True)
            dcw_ref[h] += jnp.where(rid8 == 0, t0, jnp.where(rid8 == 1, t1, jnp.where(rid8 == 2, t2, jnp.where(rid8 == 3, t3, 0.0))))

    return pl.pallas_call(
        body,
        out_shape=(jax.ShapeDtypeStruct((2, S, F), BF16), jax.ShapeDtypeStruct((2, 8, F), F32)),
        grid=(F // tn, n_i),
        in_specs=[pl.BlockSpec((2, tm, tn), lambda j, i: (0, i, j)),
                  pl.BlockSpec((2, 8, tn), lambda j, i: (0, jnp.maximum(i * r8 - 1, 0), j)),
                  pl.BlockSpec((2, 8, tn), lambda j, i: (0, jnp.minimum((i + 1) * r8, last8), j)),
                  pl.BlockSpec((tm, tn), lambda j, i: (i, j)),
                  pl.BlockSpec((8, tn), lambda j, i: (jnp.minimum((i + 1) * r8, last8), j)),
                  pl.BlockSpec((2, 8, tn), lambda j, i: (0, 0, j)),
                  pl.BlockSpec((2, 1, tn), lambda j, i: (0, 0, j))],
        out_specs=(pl.BlockSpec((2, tm, tn), lambda j, i: (0, i, j)), pl.BlockSpec((2, 8, tn), lambda j, i: (0, 0, j))),
        name="conv_act_bwd", compiler_params=_cp(("parallel", "arbitrary"), VMEM_LIMIT),
    )(hu, hu, hu, dact, dact, cw, cb)


def _attn_delta(o, do):
    S = o.shape[0]
    tm = _pick(S, (512, 256))
    ones = _group_ones()

    def body(o_ref, do_ref, ones_ref, d_ref):
        d_ref[...] = _dot3(o_ref[...] * do_ref[...], ones_ref[...])

    return pl.pallas_call(
        body, out_shape=jax.ShapeDtypeStruct((S, ATT_WIDTH), F32), grid=(S // tm,),
        in_specs=[pl.BlockSpec((tm, ATT_WIDTH), lambda i: (i, 0)), pl.BlockSpec((tm, ATT_WIDTH), lambda i: (i, 0)),
                  pl.BlockSpec((ATT_WIDTH, ATT_WIDTH), lambda i: (0, 0))],
        out_specs=pl.BlockSpec((tm, ATT_WIDTH), lambda i: (i, 0)),
        name="attn_delta", compiler_params=_cp(("parallel",)),
    )(o, do, ones)


def _attn_bwd(qkv, do_bf, c_cols, c_rows, lse_rows, dl_rows):
    S = qkv.shape[0]
    tk = _pick(S, (256,))
    tq = tk
    nq = S // tq
    nt = (((1,), (1,)), ((), ()))
    tn_dims = (((0,), (0,)), ((), ()))

    def body(q_ref, do_ref, k_ref, v_ref, cc_ref, cr_ref, lse_ref, dl_ref, dq_ref, dk_ref, dv_ref, dcs_ref, dcq_ref):
        kb = pl.program_id(1)

        @pl.when(kb == 0)
        def _():
            dq_ref[...] = jnp.zeros_like(dq_ref)
            dcq_ref[...] = jnp.zeros_like(dcq_ref)

        lane = lax.broadcasted_iota(jnp.int32, (tk, LANES), 1)
        rid = lax.broadcasted_iota(jnp.int32, (tk, tq), 0)
        cid = lax.broadcasted_iota(jnp.int32, (tk, tq), 1)
        k = k_ref[...]
        v = v_ref[...]
        dks, dvs, dcs = [], [], []
        for hh in range(2):
            sel = (lane >= GROUP_DIM) if hh else (lane < GROUP_DIM)
            km = jnp.where(sel, k, jnp.zeros_like(k))
            vm = jnp.where(sel, v, jnp.zeros_like(v))
            cs = cc_ref[:, hh * HEAD_DIM:hh * HEAD_DIM + 1]

            def step(qb, carry, km=km, vm=vm, cs=cs, hh=hh):
                dk_acc, dv_acc, dc_acc = carry
                qs = pl.multiple_of(qb * tq, tq)
                qblk = q_ref[pl.ds(qs, tq), :]
                doblk = do_ref[pl.ds(qs, tq), :]
                s = lax.dot_general(km, qblk, nt, preferred_element_type=F32)
                s = s + (cr_ref[hh:hh + 1, pl.ds(qs, tq)] - cs)
                p = jnp.exp(s - lse_ref[hh:hh + 1, pl.ds(qs, tq)])
                p = jnp.where((qb > kb) | (cid >= rid), p, 0.0)
                dp = lax.dot_general(vm, doblk, nt, preferred_element_type=F32)
                ds = p * (dp - dl_ref[hh:hh + 1, pl.ds(qs, tq)])
                ds_bf = ds.astype(BF16)
                dv_acc = dv_acc + jnp.dot(p.astype(BF16), doblk, preferred_element_type=F32)
                dk_acc = dk_acc + jnp.dot(ds_bf, qblk, preferred_element_type=F32)
                dc_acc = dc_acc + jnp.sum(ds, axis=-1, keepdims=True)
                dq_ref[pl.ds(qs, tq), :] += lax.dot_general(ds_bf, km, tn_dims, preferred_element_type=F32)
                dcq_ref[hh:hh + 1, pl.ds(qs, tq)] += jnp.sum(ds, axis=0, keepdims=True)
                return dk_acc, dv_acc, dc_acc

            init = (jnp.zeros((tk, LANES), F32), jnp.zeros((tk, LANES), F32), jnp.zeros((tk, 1), F32))
            dk_acc, dv_acc, dc_acc = lax.fori_loop(kb, nq, step, init)
            dks.append(dk_acc)
            dvs.append(dv_acc)
            dcs.append(dc_acc)
        dk_ref[...] = jnp.where(lane < GROUP_DIM, dks[0], dks[1])
        dv_ref[...] = jnp.where(lane < GROUP_DIM, dvs[0], dvs[1])
        dcs_ref[...] = jnp.where(lane == 0, -dcs[0], jnp.where(lane == 1, -dcs[1], 0.0))

    return pl.pallas_call(
        body,
        out_shape=(jax.ShapeDtypeStruct((S, ATT_WIDTH), F32), jax.ShapeDtypeStruct((S, ATT_WIDTH), F32),
                   jax.ShapeDtypeStruct((S, ATT_WIDTH), F32), jax.ShapeDtypeStruct((N_PAIRS, S, LANES), F32),
                   jax.ShapeDtypeStruct((N_PAIRS, 8, S), F32)),
        grid=(N_PAIRS, nq),
        in_specs=[pl.BlockSpec((S, LANES), lambda p, j: (0, p)),
                  pl.BlockSpec((S, LANES), lambda p, j: (0, p)),
                  pl.BlockSpec((tk, LANES), lambda p, j: (j, N_PAIRS + p)),
                  pl.BlockSpec((tk, LANES), lambda p, j: (j, 2 * N_PAIRS + p)),
                  pl.BlockSpec((None, tk, LANES), lambda p, j: (p, j, 0)),
                  pl.BlockSpec((None, 8, S), lambda p, j: (p, 0, 0)),
                  pl.BlockSpec((None, 8, S), lambda p, j: (p, 0, 0)),
                  pl.BlockSpec((None, 8, S), lambda p, j: (p, 0, 0))],
        out_specs=(pl.BlockSpec((S, LANES), lambda p, j: (0, p)),
                   pl.BlockSpec((tk, LANES), lambda p, j: (j, p)),
                   pl.BlockSpec((tk, LANES), lambda p, j: (j, p)),
                   pl.BlockSpec((None, tk, LANES), lambda p, j: (p, j, 0)),
                   pl.BlockSpec((None, 8, S), lambda p, j: (p, 0, 0))),
        name="attn_bwd", compiler_params=_cp(("parallel", "arbitrary"), VMEM_LIMIT),
    )(qkv, do_bf, qkv, qkv, c_cols, c_rows, lse_rows, dl_rows)


def _gmlp_bwd(ug, dsg, gain, w_s, wt_s, bias_full):
    S = ug.shape[0]
    tm = _pick(S, (512, 256, 128))
    n_chunks = tm // CHUNK
    n_i = S // tm
    ones = _group_ones()
    nt = (((1,), (1,)), ((), ()))

    def body(ug_ref, dsg_ref, gain_ref, w_ref, wt_ref, bias_ref, ones_ref, dug_ref, dw_ref, dgain_ref, dbias_ref,
             dbacc_ref):
        i = pl.program_id(0)

        @pl.when(i == 0)
        def _():
            dw_ref[...] = jnp.zeros_like(dw_ref)
            dgain_ref[...] = jnp.zeros_like(dgain_ref)
            dbacc_ref[...] = jnp.zeros_like(dbacc_ref)

        ones_m = ones_ref[...]
        pu = ug_ref[:, :GMLP_WIDTH]
        pg = ug_ref[:, GMLP_WIDTH:]
        u = _gelu(pu)
        vr = _gelu(pg)
        ms = _dot3(vr * vr, ones_m) * (1.0 / GROUP_DIM)
        rinv = lax.rsqrt(ms + EPS)
        vhat = vr * rinv
        gain_v = gain_ref[...]
        vn = (vhat * gain_v).astype(BF16)
        mixed = _gmlp_mixed(vn, w_ref, bias_ref[...], n_chunks)
        dsg_v = dsg_ref[...]
        du = dsg_v * mixed
        dmixed = dsg_v * u
        dm_bf = dmixed.astype(BF16)
        lane = lax.broadcasted_iota(jnp.int32, (CHUNK, LANES), 1)
        row = lax.broadcasted_iota(jnp.int32, (CHUNK, CHUNK), 0)
        col = lax.broadcasted_iota(jnp.int32, (CHUNK, CHUNK), 1)
        wts = [jnp.where(col >= row, wt_ref[g], 0.0).astype(BF16) for g in range(N_GROUPS)]
        dvn_rows = []
        dbsum = jnp.zeros((CHUNK, GMLP_WIDTH), F32)
        for ci in range(n_chunks):
            rs = slice(ci * CHUNK, (ci + 1) * CHUNK)
            dbsum = dbsum + dmixed[rs, :]
            cols = []
            for pp in range(N_GROUPS // 2):
                cs = slice(pp * LANES, (pp + 1) * LANES)
                dm = dm_bf[rs, cs]
                dm_lo = jnp.where(lane < GROUP_DIM, dm, jnp.zeros_like(dm))
                dm_hi = jnp.where(lane >= GROUP_DIM, dm, jnp.zeros_like(dm))
                vb = vn[rs, cs]
                dw_ref[2 * pp] += lax.dot_general(dm_lo, vb, nt, preferred_element_type=F32)
                dw_ref[2 * pp + 1] += lax.dot_general(dm_hi, vb, nt, preferred_element_type=F32)
                cols.append(jnp.dot(wts[2 * pp], dm_lo, preferred_element_type=F32)
                            + jnp.dot(wts[2 * pp + 1], dm_hi, preferred_element_type=F32))
            dvn_rows.append(jnp.concatenate(cols, axis=1))
        dvn = jnp.concatenate(dvn_rows, axis=0)
        dbacc_ref[...] += dbsum
        dgain_ref[0:1, :] += jnp.sum(dvn * vhat, axis=0, keepdims=True)
        dvhat = dvn * gain_v
        gm = _dot3(dvhat * vhat, ones_m) * (1.0 / GROUP_DIM)
        dvr = rinv * (dvhat - vhat * gm)
        dug_ref[:, :GMLP_WIDTH] = (du * _gelu_grad(pu)).astype(BF16)
        dug_ref[:, GMLP_WIDTH:] = (dvr * _gelu_grad(pg)).astype(BF16)

        @pl.when(i == n_i - 1)
        def _():
            for g in range(N_GROUPS):
                dw_ref[g] = jnp.where(row >= col, dw_ref[g], 0.0)
            dbias_ref[...] = _dot3(dbacc_ref[...], ones_m)

    return pl.pallas_call(
        body,
        out_shape=(jax.ShapeDtypeStruct((S, 2 * GMLP_WIDTH), BF16), jax.ShapeDtypeStruct((N_GROUPS, CHUNK, CHUNK), F32),
                   jax.ShapeDtypeStruct((8, GMLP_WIDTH), F32), jax.ShapeDtypeStruct((CHUNK, GMLP_WIDTH), F32)),
        grid=(n_i,),
        in_specs=[pl.BlockSpec((tm, 2 * GMLP_WIDTH), lambda i: (i, 0)), pl.BlockSpec((tm, GMLP_WIDTH), lambda i: (i, 0)),
                  pl.BlockSpec((1, GMLP_WIDTH), lambda i: (0, 0)),
                  pl.BlockSpec((N_GROUPS, CHUNK, CHUNK), lambda i: (0, 0, 0)),
                  pl.BlockSpec((N_GROUPS, CHUNK, CHUNK), lambda i: (0, 0, 0)),
                  pl.BlockSpec((CHUNK, GMLP_WIDTH), lambda i: (0, 0)),
                  pl.BlockSpec((GMLP_WIDTH, GMLP_WIDTH), lambda i: (0, 0))],
        out_specs=(pl.BlockSpec((tm, 2 * GMLP_WIDTH), lambda i: (i, 0)),
                   pl.BlockSpec((N_GROUPS, CHUNK, CHUNK), lambda i: (0, 0, 0)),
                   pl.BlockSpec((8, GMLP_WIDTH), lambda i: (0, 0)),
                   pl.BlockSpec((CHUNK, GMLP_WIDTH), lambda i: (0, 0))),
        scratch_shapes=[pltpu.VMEM((CHUNK, GMLP_WIDTH), F32)],
        name="gmlp_bwd", compiler_params=_cp(("arbitrary",), VMEM_LIMIT),
    )(ug, dsg, gain, w_s, wt_s, bias_full, ones)


def _gate_bwd(dcs, dcq, zf):
    S = zf.shape[0]
    tm = _pick(S, (256,))
    n_i = S // tm
    triu = (lax.broadcasted_iota(jnp.int32, (tm, tm), 0) <= lax.broadcasted_iota(jnp.int32, (tm, tm), 1)).astype(BF16)

    def body(dcs_ref, dcq_ref, zf_ref, tri_ref, dzf_ref, dbf_ref, carry_ref):
        i = pl.program_id(0)

        @pl.when(i == 0)
        def _():
            carry_ref[...] = jnp.zeros_like(carry_ref)
            dbf_ref[...] = jnp.zeros_like(dbf_ref)

        lane = lax.broadcasted_iota(jnp.int32, (tm, LANES), 1)
        dc = dcq_ref[...]
        for p in range(N_PAIRS):
            slab = dcs_ref[p]
            for hh in range(2):
                dc = dc + jnp.where(lane == 2 * p + hh, slab[:, hh:hh + 1], 0.0)
        dlf = _dot3l(tri_ref[...], dc) + carry_ref[0:1, :]
        carry_ref[0:1, :] = dlf[0:1, :]
        dz = jnp.where(lane < N_HEADS, dlf * _sigmoid(-zf_ref[...]), 0.0)
        dzf_ref[...] = dz.astype(BF16)
        dbf_ref[0:1, :] += jnp.sum(dz, axis=0, keepdims=True)

    return pl.pallas_call(
        body,
        out_shape=(jax.ShapeDtypeStruct((S, LANES), BF16), jax.ShapeDtypeStruct((8, LANES), F32)),
        grid=(n_i,),
        in_specs=[pl.BlockSpec((N_PAIRS, tm, LANES), lambda i: (0, n_i - 1 - i, 0)),
                  pl.BlockSpec((tm, LANES), lambda i: (n_i - 1 - i, 0)),
                  pl.BlockSpec((tm, LANES), lambda i: (n_i - 1 - i, 0)),
                  pl.BlockSpec((tm, tm), lambda i: (0, 0))],
        out_specs=(pl.BlockSpec((tm, LANES), lambda i: (n_i - 1 - i, 0)), pl.BlockSpec((8, LANES), lambda i: (0, 0))),
        scratch_shapes=[pltpu.VMEM((8, LANES), F32)],
        name="gate_bwd", compiler_params=_cp(("arbitrary",), VMEM_LIMIT),
    )(dcs, dcq, zf, triu)


def _adamw(w, m, v, parts, name):
    R, C = w.shape
    tr = R
    for cand in (256, 128, 64, 32, 16, 8):
        if R % cand == 0 and R > cand:
            tr = cand
            break
    c1 = 1.0 / (1.0 - ADAM_B1 ** ADAM_STEP)
    c2 = 1.0 / (1.0 - ADAM_B2 ** ADAM_STEP)

    def body(w_ref, m_ref, v_ref, p_ref, g_ref, d_ref, nm_ref, nv_ref):
        g = p_ref[0]
        for j in range(1, N_DEV):
            g = g + p_ref[j]
        g_ref[...] = g
        nm = ADAM_B1 * m_ref[...] + (1.0 - ADAM_B1) * g
        nv = ADAM_B2 * v_ref[...] + (1.0 - ADAM_B2) * (g * g)
        nm_ref[...] = nm
        nv_ref[...] = nv
        d_ref[...] = -ADAM_LR * ((nm * c1) / (jnp.sqrt(nv * c2) + ADAM_EPS) + ADAM_WD * w_ref[...])

    spec = pl.BlockSpec((tr, C), lambda i: (i, 0))
    shp = jax.ShapeDtypeStruct((R, C), F32)
    return pl.pallas_call(
        body, out_shape=(shp, shp, shp, shp), grid=(R // tr,),
        in_specs=[spec, spec, spec, pl.BlockSpec((N_DEV, tr, C), lambda i: (0, i, 0))],
        out_specs=(spec, spec, spec, spec),
        name=name, compiler_params=_cp(("parallel",), VMEM_LIMIT),
    )(w, m, v, parts)


def _place():
    x, y, c = lax.axis_index("x"), lax.axis_index("y"), lax.axis_index("c")
    return x, y, c


def _all_gather(blocks, name):
    n = len(blocks)

    def body(*refs):
        ins, outs = refs[:n], refs[n:2 * n]
        send_sems, recv_sems, local_sems = refs[2 * n:]
        x, y, c = _place()
        me, sibling = (x, y, c), (x, y, 1 - c)
        chips = [(1 - x, y), (x, 1 - y), (1 - x, 1 - y)]
        sends = []
        for a in range(n):
            out = outs[a]

            def slot(px, py, pc, out=out):
                return out.at[4 * px + 2 * py + pc]

            def copy(k, block, to, src=None, a=a, slot=slot):
                return pltpu.make_async_remote_copy(
                    src_ref=slot(*block) if src is None else src, dst_ref=slot(*block),
                    send_sem=send_sems.at[a, k], recv_sem=recv_sems.at[a, k], device_id=to, device_id_type=MESH)

            mine = pltpu.make_async_copy(ins[a], slot(*me), local_sems.at[a])
            mine.start()
            first = [copy(0, me, sibling, src=ins[a])]
            first += [copy(1 + j, me, (*chip, c), src=ins[a]) for j, chip in enumerate(chips)]
            for cp in first:
                cp.start()
            sends.append((mine, first, copy))
        for a in range(n):
            mine, first, copy = sends[a]
            passed = [copy(4 + j, (*chip, c), sibling) for j, chip in enumerate(chips)]
            for j, chip in enumerate(chips):
                copy(1 + j, (*chip, c), me).wait_recv()
                passed[j].start()
            copy(0, sibling, me).wait_recv()
            for j, chip in enumerate(chips):
                copy(4 + j, (*chip, 1 - c), me).wait_recv()
            for cp in first + passed:
                cp.wait_send()
            mine.wait()

    any_spec = pl.BlockSpec(memory_space=pl.ANY)
    return pl.pallas_call(
        body, out_shape=tuple(jax.ShapeDtypeStruct((N_DEV,) + b.shape, b.dtype) for b in blocks),
        in_specs=[any_spec] * n, out_specs=tuple([any_spec] * n),
        scratch_shapes=[pltpu.SemaphoreType.DMA((n, 7)), pltpu.SemaphoreType.DMA((n, 7)), pltpu.SemaphoreType.DMA((n,))],
        name=name,
    )(*blocks)


def _exchange_shards(parts, name):
    n = len(parts)

    def body(*refs):
        ins, outs = refs[:n], refs[n:2 * n]
        send_sems, recv_sems, local_sems = refs[2 * n:]
        x, y, c = _place()
        me = 4 * x + 2 * y + c
        started = []
        for a in range(n):
            mine = pltpu.make_async_copy(ins[a].at[me], outs[a].at[me], local_sems.at[a])
            mine.start()
            started.append(mine)
            for k in range(1, N_DEV):
                px, py, pc = x ^ ((k >> 2) & 1), y ^ ((k >> 1) & 1), c ^ (k & 1)
                cp = pltpu.make_async_remote_copy(
                    src_ref=ins[a].at[4 * px + 2 * py + pc], dst_ref=outs[a].at[me],
                    send_sem=send_sems.at[a, k - 1], recv_sem=recv_sems.at[a, k - 1],
                    device_id=(px, py, pc), device_id_type=MESH)
                cp.start()
                started.append(cp)
        for cp in started:
            cp.wait()

    any_spec = pl.BlockSpec(memory_space=pl.ANY)
    return pl.pallas_call(
        body, out_shape=tuple(jax.ShapeDtypeStruct(p.shape, p.dtype) for p in parts),
        in_specs=[any_spec] * n, out_specs=tuple([any_spec] * n),
        scratch_shapes=[pltpu.SemaphoreType.DMA((n, 7)), pltpu.SemaphoreType.DMA((n, 7)), pltpu.SemaphoreType.DMA((n,))],
        name=name,
    )(*parts)


def _rows128(a):
    flat = a.reshape(-1)
    rows = -(-flat.shape[0] // LANES)
    rows = -(-rows // 8) * 8
    return jnp.pad(flat, (0, rows * LANES - flat.shape[0])).reshape(rows, LANES)


def _local_step(x, target, norm_mix_g, w_in_bf, b_forget, gmlp_norm_g, w_spatial, b_spatial, w_out_bf, norm_ffn_g,
                w_up_bf, conv_w, conv_b, w_down_bf, norm_final_g):
    S = x.shape[0]
    f = D_FF
    w_pad = jnp.pad(w_in_bf, ((0, 0), (0, IN_PAD - IN_COLS)))
    bf_pad = jnp.pad(b_forget.reshape(1, N_HEADS), ((0, 0), (0, LANES - N_HEADS)))
    xn, qkv, ug, zf, c = _inproj_fwd(x, norm_mix_g.reshape(1, D_MODEL), w_pad, bf_pad)
    c8 = c[:, :N_HEADS]
    c_rows = jnp.pad(c8.T.reshape(N_PAIRS, 2, S), ((0, 0), (0, 6), (0, 0)))
    c_cols = jnp.repeat(c8.reshape(S, N_PAIRS, 2).transpose(1, 0, 2), HEAD_DIM, axis=2)
    bias_full = jnp.repeat(b_spatial.reshape(N_GROUPS, CHUNK).T, GROUP_DIM, axis=1)
    w_s = w_spatial.reshape(N_GROUPS, CHUNK, CHUNK)
    gain = gmlp_norm_g.reshape(1, GMLP_WIDTH)
    sg = _gmlp_fwd(ug, gain, w_s, bias_full)
    att, lse = _attn_fwd(qkv, c_cols, c_rows)
    mix = jnp.concatenate([att.astype(BF16), sg], axis=1)
    h1 = _mm(mix, w_out_bf, mode="nn", out_dtype=F32, tm=512, tn=1024, tk=1024, res=x, name="out_proj")
    g_ffn = norm_ffn_g.reshape(1, D_MODEL)
    hn = _rms_fwd(h1, g_ffn)
    hu = _mm(hn, w_up_bf, mode="nn", out_dtype=F32, tm=512, tn=256, tk=1024, out_halves=True, outer="j", name="ffn_up")
    cw = jnp.pad(conv_w.reshape(3, 2, f).transpose(1, 0, 2), ((0, 0), (0, 5), (0, 0)))
    cb = conv_b.reshape(2, 1, f)
    act = _conv_act_fwd(hu, cw, cb)
    h2 = _mm(act, w_down_bf, mode="nn", out_dtype=F32, tm=512, tn=1024, tk=1408, res=h1, name="ffn_down")
    loss_blk, dh2, dh2_bf, dg_final = _loss_head(h2, norm_final_g.reshape(1, D_MODEL), target)
    dact = _mm(dh2_bf, w_down_bf, mode="nt", out_dtype=F32, tm=512, tn=256, tk=1024, outer="j", name="ffn_down_dx")
    dw_down = _mm(act, dh2_bf, mode="tn", out_dtype=F32, tm=1408, tn=1024, tk=1024, name="ffn_down_dw")
    dhu, dcw = _conv_act_bwd(hu, dact, cw, cb)
    dhn = _mm(dhu, w_up_bf, mode="nt", out_dtype=F32, tm=512, tn=1024, tk=1408, a_halves=True, name="ffn_up_dx")
    dw_up = _mm(hn, dhu, mode="tn", out_dtype=F32, tm=1024, tn=1408, tk=1024, b_halves=True, outer="j", name="ffn_up_dw")
    dh1, dh1_bf, dg_ffn = _rms_bwd(h1, g_ffn, dhn, dh2)
    dmix = _mm(dh1_bf, w_out_bf, mode="nt", out_dtype=F32, tm=512, tn=1024, tk=1024, name="out_proj_dx")
    dw_out = _mm(mix, dh1_bf, mode="tn", out_dtype=F32, tm=1024, tn=1024, tk=1024, name="out_proj_dw")
    datt = dmix[:, :ATT_WIDTH]
    dsg = dmix[:, ATT_WIDTH:]
    dl = _attn_delta(att, datt)

    def rows_of(a):
        return jnp.pad(a[:, ::HEAD_DIM].T.reshape(N_PAIRS, 2, S), ((0, 0), (0, 6), (0, 0)))

    dq, dk, dv, dcs, dcq = _attn_bwd(qkv, datt.astype(BF16), c_cols, c_rows, rows_of(lse), rows_of(dl))
    wt_s = w_s.transpose(0, 2, 1)
    dug, dw_s, dgain, dbias = _gmlp_bwd(ug, dsg, gain, w_s, wt_s, bias_full)
    dcq_cols = jnp.pad(dcq[:, :2, :].reshape(N_HEADS, S).T, ((0, 0), (0, LANES - N_HEADS)))
    dzf, dbf = _gate_bwd(dcs, dcq_cols, zf)
    dproj = jnp.concatenate([(dq * (HEAD_DIM ** -0.5)).astype(BF16), dk.astype(BF16), dv.astype(BF16), dug, dzf], axis=1)
    dxn = _mm(dproj, w_pad, mode="nt", out_dtype=F32, tm=512, tn=1024, tk=896, name="in_proj_dx")
    dw_in = _mm(xn, dproj, mode="tn", out_dtype=F32, tm=1024, tn=896, tk=1024, outer="j", name="in_proj_dw")
    grad_x, _, dg_mix = _rms_bwd(x, norm_mix_g.reshape(1, D_MODEL), dxn, dh1)
    grads = dict(
        norm_mix_g=dg_mix[0:1, :],
        w_in=dw_in[:, :IN_COLS],
        b_forget=dbf[0:1, :N_HEADS],
        gmlp_norm_g=dgain[0:1, :],
        w_spatial=dw_s,
        b_spatial=dbias[:, ::GROUP_DIM].T,
        w_out=dw_out,
        norm_ffn_g=dg_ffn[0:1, :],
        w_up=dw_up,
        conv_w=dcw[:, 0:3, :].transpose(1, 0, 2).reshape(3, 2 * f),
        conv_b=dcw[:, 3, :].reshape(1, 2 * f),
        w_down=dw_down,
        norm_final_g=dg_final[0, :],
    )
    return loss_blk[0, 0], grad_x, grads


SMALL = ("norm_mix_g", "b_forget", "gmlp_norm_g", "w_spatial", "b_spatial", "norm_ffn_g", "conv_b", "norm_final_g")


def kernel(x, norm_mix_g, w_in, b_forget, gmlp_norm_g, w_spatial, b_spatial, w_out, norm_ffn_g, w_up, conv_w, conv_b, w_down, norm_final_g, loss_target, m_norm_mix_g, m_w_in, m_b_forget, m_gmlp_norm_g, m_w_spatial, m_b_spatial, m_w_out, m_norm_ffn_g, m_w_up, m_conv_w, m_conv_b, m_w_down, m_norm_final_g, v_norm_mix_g, v_w_in, v_b_forget, v_gmlp_norm_g, v_w_spatial, v_b_spatial, v_w_out, v_norm_ffn_g, v_w_up, v_conv_w, v_conv_b, v_w_down, v_norm_final_g):
    weights = dict(norm_mix_g=norm_mix_g, w_in=w_in, b_forget=b_forget, gmlp_norm_g=gmlp_norm_g, w_spatial=w_spatial,
                   b_spatial=b_spatial, w_out=w_out, norm_ffn_g=norm_ffn_g, w_up=w_up, conv_w=conv_w, conv_b=conv_b,
                   w_down=w_down, norm_final_g=norm_final_g)
    m_in = dict(norm_mix_g=m_norm_mix_g, w_in=m_w_in, b_forget=m_b_forget, gmlp_norm_g=m_gmlp_norm_g,
                w_spatial=m_w_spatial, b_spatial=m_b_spatial, w_out=m_w_out, norm_ffn_g=m_norm_ffn_g, w_up=m_w_up,
                conv_w=m_conv_w, conv_b=m_conv_b, w_down=m_w_down, norm_final_g=m_norm_final_g)
    v_in = dict(norm_mix_g=v_norm_mix_g, w_in=v_w_in, b_forget=v_b_forget, gmlp_norm_g=v_gmlp_norm_g,
                w_spatial=v_w_spatial, b_spatial=v_b_spatial, w_out=v_w_out, norm_ffn_g=v_norm_ffn_g, w_up=v_w_up,
                conv_w=v_conv_w, conv_b=v_conv_b, w_down=v_w_down, norm_final_g=v_norm_final_g)
    order = list(weights)
    me = 4 * lax.axis_index("x") + 2 * lax.axis_index("y") + lax.axis_index("c")
    n_in, n_up = w_in.shape[2], w_up.shape[2]
    r_out, r_down = w_out.shape[1], w_down.shape[1]

    cols_blk = jnp.concatenate([w_in[0].astype(BF16), w_up[0].astype(BF16)], axis=1)
    rows_blk = jnp.concatenate([w_out[0].astype(BF16), w_down[0].astype(BF16)], axis=0)
    taps_blk = jnp.pad(conv_w[0], ((0, 5), (0, 0)))
    cols_all, rows_all, taps_all = _all_gather([cols_blk, rows_blk, taps_blk], "gather_weights")
    w_in_bf = cols_all[:, :, :n_in].transpose(1, 0, 2).reshape(D_MODEL, N_DEV * n_in)
    w_up_bf = cols_all[:, :, n_in:].transpose(1, 0, 2).reshape(D_MODEL, N_DEV * n_up)
    w_out_bf = rows_all[:, :r_out, :].reshape(N_DEV * r_out, D_MODEL)
    w_down_bf = rows_all[:, r_out:, :].reshape(N_DEV * r_down, D_MODEL)
    conv_w_full = taps_all[:, :3, :].transpose(1, 0, 2).reshape(3, N_DEV * n_up)

    loss_local, grad_x, g = _local_step(
        x[0], loss_target[0], norm_mix_g, w_in_bf, b_forget, gmlp_norm_g, w_spatial, b_spatial, w_out_bf, norm_ffn_g,
        w_up_bf, conv_w_full, conv_b, w_down_bf, norm_final_g)
    loss = lax.psum(loss_local, ("x", "y", "c"))

    to_in = g["w_in"].reshape(D_MODEL, N_DEV, n_in).transpose(1, 0, 2)
    to_up = g["w_up"].reshape(D_MODEL, N_DEV, n_up).transpose(1, 0, 2)
    to_out = g["w_out"].reshape(N_DEV, r_out, D_MODEL)
    to_down = g["w_down"].reshape(N_DEV, r_down, D_MODEL)
    got_in, got_up, got_out, got_down = _exchange_shards([to_in, to_up, to_out, to_down], "scatter_grads")

    small_names = SMALL + ("conv_w",)
    packed = [_rows128(g[k]) for k in small_names]
    sizes = [p.shape[0] for p in packed]
    (small_all,) = _all_gather([jnp.concatenate(packed, axis=0)], "gather_small_grads")

    def pack(src):
        return jnp.concatenate([_rows128(src[k]) for k in SMALL], axis=0)

    n_small_rows = sum(sizes[:-1])
    sg_, sd_, sm_, sv_ = _adamw(pack(weights), pack(m_in), pack(v_in), small_all[:, :n_small_rows, :], "adamw_small")

    outs = {}
    off = 0
    for k, rows in zip(SMALL, sizes[:-1]):
        shp = weights[k].shape
        cnt = math.prod(shp)
        outs[k] = tuple(a[off:off + rows].reshape(-1)[:cnt].reshape(shp) for a in (sg_, sd_, sm_, sv_))
        off += rows
    taps_parts = small_all[:, n_small_rows:, :].reshape(N_DEV, -1)[:, :3 * N_DEV * n_up].reshape(N_DEV, 3, N_DEV * n_up)
    taps_mine = lax.dynamic_slice_in_dim(taps_parts, me * n_up, n_up, axis=2)
    taps_mine = jnp.pad(taps_mine, ((0, 0), (0, 5), (0, 0)))

    def pad8(a):
        return jnp.pad(a[0], ((0, 5), (0, 0)))

    res = _adamw(pad8(conv_w), pad8(m_conv_w), pad8(v_conv_w), taps_mine, "adamw_conv_w")
    outs["conv_w"] = tuple(a[:3][None] for a in res)
    for k, got in (("w_in", got_in), ("w_up", got_up), ("w_out", got_out), ("w_down", got_down)):
        res = _adamw(weights[k][0], m_in[k][0], v_in[k][0], got, "adamw_" + k)
        outs[k] = tuple(a[None] for a in res)

    return (loss, grad_x[None], *[outs[k][0] for k in order], *[outs[k][1] for k in order],
            *[outs[k][2] for k in order], *[outs[k][3] for k in order])
```

```python
import functools
import math

import jax
import jax.numpy as jnp
from jax import lax
from jax.experimental import pallas as pl
from jax.experimental.pallas import tpu as pltpu

F32 = jnp.float32
BF16 = jnp.bfloat16

N_DEV = 8
D_MODEL = 1024
ATT_WIDTH = 512
GMLP_WIDTH = 512
HEAD_DIM = 64
N_HEADS = 8
N_PAIRS = 4
N_GROUPS = 8
GROUP_DIM = 64
CHUNK = 128
D_FF = 2816
IN_COLS = 2568
IN_PAD = 2688
QKV = 1536
UG_END = 2560
EPS = 1e-6
LANES = 128

ADAM_LR = 0.001
ADAM_B1 = 0.9
ADAM_B2 = 0.999
ADAM_EPS = 1e-08
ADAM_WD = 0.01
ADAM_STEP = 10

VMEM_LIMIT = 56 * 1024 * 1024
MESH = pl.DeviceIdType.MESH


def _cp(sem, vmem=None):
    return pltpu.CompilerParams(dimension_semantics=sem, vmem_limit_bytes=vmem)


def _pick(n, prefs):
    for p in prefs:
        if n % p == 0:
            return p
    return n


def _split3(x):
    hi = x.astype(BF16)
    r1 = x - hi.astype(F32)
    mid = r1.astype(BF16)
    lo = (r1 - mid.astype(F32)).astype(BF16)
    return hi, mid, lo


def _dot3(x, ones_bf):
    hi, mid, lo = _split3(x)
    d = functools.partial(jnp.dot, preferred_element_type=F32)
    return d(hi, ones_bf) + d(mid, ones_bf) + d(lo, ones_bf)


def _dot3l(ones_bf, x):
    hi, mid, lo = _split3(x)
    d = functools.partial(jnp.dot, preferred_element_type=F32)
    return d(ones_bf, hi) + d(ones_bf, mid) + d(ones_bf, lo)


def _gelu(x):
    k = math.sqrt(2.0 / math.pi)
    t = jnp.tanh(k * (x + 0.044715 * (x * x * x)))
    return 0.5 * x * (1.0 + t)


def _gelu_grad(x):
    k = math.sqrt(2.0 / math.pi)
    x2 = x * x
    t = jnp.tanh(k * (x + 0.044715 * (x2 * x)))
    return 0.5 * (1.0 + t) + 0.5 * x * (1.0 - t * t) * (k * (1.0 + 3.0 * 0.044715 * x2))


def _sigmoid(x):
    return 1.0 / (1.0 + jnp.exp(-x))


def _mm(a, b, *, mode, out_dtype, tm, tn, tk, name, res=None, a_halves=False, b_halves=False,
        out_halves=False, outer="i"):
    if mode == "tn":
        K, M = a.shape[-2], a.shape[-1] * (2 if a_halves else 1)
    else:
        M, K = a.shape[-2], a.shape[-1] * (2 if a_halves else 1)
    if mode == "nt":
        N = b.shape[-2]
        assert b.shape[-1] == K
    else:
        N = b.shape[-1] * (2 if b_halves else 1)
    tm, tn, tk = min(tm, M), min(tn, N), min(tk, K)
    assert M % tm == 0 and N % tn == 0 and K % tk == 0, (name, M, N, K, tm, tn, tk)
    nm, nn, nk = M // tm, N // tn, K // tk

    def ij(g0, g1):
        return (g0, g1) if outer == "i" else (g1, g0)

    if mode == "nn":
        dims = (((1,), (0,)), ((), ()))
        if a_halves:
            nkh = nk // 2
            a_spec = pl.BlockSpec((None, tm, tk), lambda g0, g1, k: (k // nkh, ij(g0, g1)[0], k % nkh))
        else:
            a_spec = pl.BlockSpec((tm, tk), lambda g0, g1, k: (ij(g0, g1)[0], k))
        b_spec = pl.BlockSpec((tk, tn), lambda g0, g1, k: (k, ij(g0, g1)[1]))
    elif mode == "nt":
        dims = (((1,), (1,)), ((), ()))
        if a_halves:
            nkh = nk // 2
            a_spec = pl.BlockSpec((None, tm, tk), lambda g0, g1, k: (k // nkh, ij(g0, g1)[0], k % nkh))
        else:
            a_spec = pl.BlockSpec((tm, tk), lambda g0, g1, k: (ij(g0, g1)[0], k))
        b_spec = pl.BlockSpec((tn, tk), lambda g0, g1, k: (ij(g0, g1)[1], k))
    else:
        dims = (((0,), (0,)), ((), ()))
        if a_halves:
            nmh = nm // 2
            a_spec = pl.BlockSpec((None, tk, tm), lambda g0, g1, k: (ij(g0, g1)[0] // nmh, k, ij(g0, g1)[0] % nmh))
        else:
            a_spec = pl.BlockSpec((tk, tm), lambda g0, g1, k: (k, ij(g0, g1)[0]))
        if b_halves:
            nnh = nn // 2
            b_spec = pl.BlockSpec((None, tk, tn), lambda g0, g1, k: (ij(g0, g1)[1] // nnh, k, ij(g0, g1)[1] % nnh))
        else:
            b_spec = pl.BlockSpec((tk, tn), lambda g0, g1, k: (k, ij(g0, g1)[1]))
    if out_halves:
        nnh = nn // 2
        o_spec = pl.BlockSpec((None, tm, tn), lambda g0, g1, k: (ij(g0, g1)[1] // nnh, ij(g0, g1)[0], ij(g0, g1)[1] % nnh))
        o_shape = jax.ShapeDtypeStruct((2, M, N // 2), out_dtype)
    else:
        o_spec = pl.BlockSpec((tm, tn), lambda g0, g1, k: ij(g0, g1))
        o_shape = jax.ShapeDtypeStruct((M, N), out_dtype)
    in_specs = [a_spec, b_spec]
    args = [a, b]
    if res is not None:
        in_specs.append(pl.BlockSpec((tm, tn), lambda g0, g1, k: ij(g0, g1)))
        args.append(res)

    def body(*refs):
        if res is not None:
            a_ref, b_ref, r_ref, o_ref = refs[:4]
        else:
            a_ref, b_ref, o_ref = refs[:3]
            r_ref = None
        part = lax.dot_general(a_ref[...], b_ref[...], dims, preferred_element_type=F32)
        if nk == 1:
            if r_ref is not None:
                part = part + r_ref[...]
            o_ref[...] = part.astype(out_dtype)
            return
        acc_ref = refs[-1]
        k = pl.program_id(2)

        @pl.when(k == 0)
        def _():
            acc_ref[...] = part

        @pl.when(k > 0)
        def _():
            acc_ref[...] += part

        @pl.when(k == nk - 1)
        def _():
            tot = acc_ref[...]
            if r_ref is not None:
                tot = tot + r_ref[...]
            o_ref[...] = tot.astype(out_dtype)

    grid = (nm, nn, nk) if outer == "i" else (nn, nm, nk)
    scratch = [] if nk == 1 else [pltpu.VMEM((tm, tn), F32)]
    return pl.pallas_call(
        body, out_shape=o_shape, grid=grid, in_specs=in_specs, out_specs=o_spec, scratch_shapes=scratch,
        name=name, compiler_params=_cp(("parallel", "parallel", "arbitrary"), VMEM_LIMIT),
    )(*args)


def _aug(lane, terms):
    out = 0.0
    for j, t in enumerate(terms):
        out = jnp.where(lane == HEAD_DIM + j, t, out)
    return out


def _split3f(x):
    hi, mid, lo = _split3(x)
    return [hi.astype(F32), mid.astype(F32), lo.astype(F32)]


def _inproj_fwd(x, g_mix, w_pad, bf_pad):
    S = x.shape[0]
    tm = _pick(S, (256,))
    tri = (lax.broadcasted_iota(jnp.int32, (tm, tm), 0) >= lax.broadcasted_iota(jnp.int32, (tm, tm), 1)).astype(BF16)

    def body(x_ref, g_ref, w_ref, bf_ref, tri_ref, xn_ref, qa_ref, ka_ref, va_ref, ug_ref, zf_ref, carry_ref):
        i = pl.program_id(0)

        @pl.when(i == 0)
        def _():
            carry_ref[...] = jnp.zeros_like(carry_ref)

        xf = x_ref[...]
        r = lax.rsqrt(jnp.mean(xf * xf, axis=-1, keepdims=True) + EPS)
        xn = ((xf * r) * g_ref[...]).astype(BF16)
        xn_ref[...] = xn
        proj = jnp.dot(xn, w_ref[...], preferred_element_type=F32)
        ug_ref[...] = proj[:, QKV:UG_END]
        zf = proj[:, UG_END:] + bf_ref[...]
        zf_ref[...] = zf
        lf = jnp.minimum(zf, 0.0) - jnp.log(1.0 + jnp.exp(-jnp.abs(zf)))
        c = _dot3l(tri_ref[...], lf) + carry_ref[0:1, :]
        carry_ref[0:1, :] = c[tm - 1:tm, :]
        c3 = _split3f(c)
        lane = lax.broadcasted_iota(jnp.int32, (tm, LANES), 1)
        ones3 = [1.0, 1.0, 1.0]
        for h in range(N_HEADS):
            p, odd = h // 2, h % 2
            ch = [t[:, h:h + 1] for t in c3]

            def head(base, scale=None, p=p, odd=odd):
                blk = proj[:, base + p * LANES:base + (p + 1) * LANES]
                if scale is not None:
                    blk = blk * scale
                return pltpu.roll(blk, HEAD_DIM, 1) if odd else blk

            cols = slice(h * LANES, (h + 1) * LANES)
            qa_ref[:, cols] = jnp.where(lane < HEAD_DIM, head(0, HEAD_DIM ** -0.5), _aug(lane, ch + ones3)).astype(BF16)
            ka_ref[:, cols] = jnp.where(lane < HEAD_DIM, head(ATT_WIDTH),
                                        _aug(lane, ones3 + [-t for t in ch] + ones3)).astype(BF16)
            va_ref[:, cols] = jnp.where(lane < HEAD_DIM, head(2 * ATT_WIDTH), _aug(lane, ones3)).astype(BF16)

    wide = N_HEADS * LANES
    return pl.pallas_call(
        body,
        out_shape=(jax.ShapeDtypeStruct((S, D_MODEL), BF16), jax.ShapeDtypeStruct((S, wide), BF16),
                   jax.ShapeDtypeStruct((S, wide), BF16), jax.ShapeDtypeStruct((S, wide), BF16),
                   jax.ShapeDtypeStruct((S, 2 * GMLP_WIDTH), F32), jax.ShapeDtypeStruct((S, LANES), F32)),
        grid=(S // tm,),
        in_specs=[pl.BlockSpec((tm, D_MODEL), lambda i: (i, 0)), pl.BlockSpec((1, D_MODEL), lambda i: (0, 0)),
                  pl.BlockSpec((D_MODEL, IN_PAD), lambda i: (0, 0)), pl.BlockSpec((1, LANES), lambda i: (0, 0)),
                  pl.BlockSpec((tm, tm), lambda i: (0, 0))],
        out_specs=(pl.BlockSpec((tm, D_MODEL), lambda i: (i, 0)), pl.BlockSpec((tm, wide), lambda i: (i, 0)),
                   pl.BlockSpec((tm, wide), lambda i: (i, 0)), pl.BlockSpec((tm, wide), lambda i: (i, 0)),
                   pl.BlockSpec((tm, 2 * GMLP_WIDTH), lambda i: (i, 0)), pl.BlockSpec((tm, LANES), lambda i: (i, 0))),
        scratch_shapes=[pltpu.VMEM((8, LANES), F32)],
        name="inproj_fwd", compiler_params=_cp(("arbitrary",), VMEM_LIMIT),
    )(x, g_mix, w_pad, bf_pad, tri)


def _group_ones():
    r = lax.broadcasted_iota(jnp.int32, (GMLP_WIDTH, GMLP_WIDTH), 0) // GROUP_DIM
    c = lax.broadcasted_iota(jnp.int32, (GMLP_WIDTH, GMLP_WIDTH), 1) // GROUP_DIM
    return (r == c).astype(BF16)


def _gmlp_mixed(vn_bf, w_ref, bias, n_chunks):
    lane = lax.broadcasted_iota(jnp.int32, (CHUNK, LANES), 1)
    row = lax.broadcasted_iota(jnp.int32, (CHUNK, CHUNK), 0)
    col = lax.broadcasted_iota(jnp.int32, (CHUNK, CHUNK), 1)
    ws = [jnp.where(row >= col, w_ref[g], 0.0).astype(BF16) for g in range(N_GROUPS)]
    rows = []
    for ci in range(n_chunks):
        cols = []
        for pp in range(N_GROUPS // 2):
            v = vn_bf[ci * CHUNK:(ci + 1) * CHUNK, pp * LANES:(pp + 1) * LANES]
            v_lo = jnp.where(lane < GROUP_DIM, v, jnp.zeros_like(v))
            v_hi = jnp.where(lane >= GROUP_DIM, v, jnp.zeros_like(v))
            m = (jnp.dot(ws[2 * pp], v_lo, preferred_element_type=F32)
                 + jnp.dot(ws[2 * pp + 1], v_hi, preferred_element_type=F32))
            cols.append(m + bias[:, pp * LANES:(pp + 1) * LANES])
        rows.append(jnp.concatenate(cols, axis=1))
    return jnp.concatenate(rows, axis=0)


def _gmlp_fwd(ug, gain, w_s, bias_full):
    S = ug.shape[0]
    tm = _pick(S, (512, 256, 128))
    ones = _group_ones()

    def body(ug_ref, gain_ref, w_ref, bias_ref, ones_ref, sg_ref):
        u = _gelu(ug_ref[:, :GMLP_WIDTH])
        vr = _gelu(ug_ref[:, GMLP_WIDTH:])
        ms = _dot3(vr * vr, ones_ref[...]) * (1.0 / GROUP_DIM)
        vn = ((vr * lax.rsqrt(ms + EPS)) * gain_ref[...]).astype(BF16)
        mixed = _gmlp_mixed(vn, w_ref, bias_ref[...], tm // CHUNK)
        sg_ref[...] = (u * mixed).astype(BF16)

    return pl.pallas_call(
        body, out_shape=jax.ShapeDtypeStruct((S, GMLP_WIDTH), BF16), grid=(S // tm,),
        in_specs=[pl.BlockSpec((tm, 2 * GMLP_WIDTH), lambda i: (i, 0)), pl.BlockSpec((1, GMLP_WIDTH), lambda i: (0, 0)),
                  pl.BlockSpec((N_GROUPS, CHUNK, CHUNK), lambda i: (0, 0, 0)),
                  pl.BlockSpec((CHUNK, GMLP_WIDTH), lambda i: (0, 0)),
                  pl.BlockSpec((GMLP_WIDTH, GMLP_WIDTH), lambda i: (0, 0))],
        out_specs=pl.BlockSpec((tm, GMLP_WIDTH), lambda i: (i, 0)),
        name="gmlp_fwd", compiler_params=_cp(("parallel",), VMEM_LIMIT),
    )(ug, gain, w_s, bias_full, ones)


_NT = (((1,), (1,)), ((), ()))
_TN = (((0,), (0,)), ((), ()))


def _attn_fwd(qa, ka, va):
    S = qa.shape[0]
    tq = _pick(S, (512, 256))
    tk = tq
    nq = S // tq

    def body(q_ref, k_ref, v_ref, o_ref, lse_ref):
        qi = pl.program_id(1)
        lane = lax.broadcasted_iota(jnp.int32, (tq, LANES), 1)
        rid = lax.broadcasted_iota(jnp.int32, (tq, tk), 0)
        cid = lax.broadcasted_iota(jnp.int32, (tq, tk), 1)
        qs = [q_ref[:, :LANES], q_ref[:, LANES:]]

        def update(kb, h, m, acc, masked):
            ks = pl.multiple_of(kb * tk, tk)
            cols = slice(h * LANES, (h + 1) * LANES)
            s = lax.dot_general(qs[h], k_ref[pl.ds(ks, tk), cols], _NT, preferred_element_type=F32)
            if masked:
                s = jnp.where(rid >= cid, s, -jnp.inf)
            m_new = jnp.maximum(m, jnp.max(s, axis=-1, keepdims=True))
            p = jnp.exp(s - m_new).astype(BF16)
            acc = jnp.exp(m - m_new) * acc + jnp.dot(p, v_ref[pl.ds(ks, tk), cols], preferred_element_type=F32)
            return m_new, acc

        def step(kb, carry):
            return tuple(update(kb, h, *carry[h], False) for h in range(2))

        one = (jnp.full((tq, 1), -jnp.inf, F32), jnp.zeros((tq, LANES), F32))
        carry = lax.fori_loop(0, qi, step, (one, one))
        outs, lses = [], []
        for h in range(2):
            m, acc = update(qi, h, *carry[h], True)
            l = acc[:, HEAD_DIM:HEAD_DIM + 1]
            outs.append(acc / l)
            lses.append(m + jnp.log(l))
        o_ref[...] = jnp.where(lane < HEAD_DIM, outs[0], pltpu.roll(outs[1], HEAD_DIM, 1))
        lse_ref[...] = jnp.where(lane < HEAD_DIM, lses[0], lses[1])

    return pl.pallas_call(
        body,
        out_shape=(jax.ShapeDtypeStruct((S, ATT_WIDTH), F32), jax.ShapeDtypeStruct((S, ATT_WIDTH), F32)),
        grid=(N_PAIRS, nq),
        in_specs=[pl.BlockSpec((tq, 2 * LANES), lambda p, i: (i, p)),
                  pl.BlockSpec((S, 2 * LANES), lambda p, i: (0, p)),
                  pl.BlockSpec((S, 2 * LANES), lambda p, i: (0, p))],
        out_specs=(pl.BlockSpec((tq, LANES), lambda p, i: (i, p)), pl.BlockSpec((tq, LANES), lambda p, i: (i, p))),
        name="attn_fwd", compiler_params=_cp(("parallel", "parallel"), VMEM_LIMIT),
    )(qa, ka, va)


def _rms_fwd(h, g):
    S = h.shape[0]
    tm = _pick(S, (512, 256))

    def body(h_ref, g_ref, o_ref):
        hf = h_ref[...]
        r = lax.rsqrt(jnp.mean(hf * hf, axis=-1, keepdims=True) + EPS)
        o_ref[...] = ((hf * r) * g_ref[...]).astype(BF16)

    return pl.pallas_call(
        body, out_shape=jax.ShapeDtypeStruct(h.shape, BF16), grid=(S // tm,),
        in_specs=[pl.BlockSpec((tm, D_MODEL), lambda i: (i, 0)), pl.BlockSpec((1, D_MODEL), lambda i: (0, 0))],
        out_specs=pl.BlockSpec((tm, D_MODEL), lambda i: (i, 0)),
        name="rms_fwd", compiler_params=_cp(("parallel",)),
    )(h, g)


def _shift_rows(x, prev, n):
    rid = lax.broadcasted_iota(jnp.int32, x.shape, 0)
    y = pltpu.roll(x, n, 0)
    if n == 1:
        return jnp.where(rid == 0, prev[7:8, :], y)
    return jnp.where(rid == 0, prev[6:7, :], jnp.where(rid == 1, prev[7:8, :], y))


def _shift_rows_up(x, nxt, n):
    rows = x.shape[0]
    rid = lax.broadcasted_iota(jnp.int32, x.shape, 0)
    y = pltpu.roll(x, rows - n, 0)
    if n == 1:
        return jnp.where(rid == rows - 1, nxt[0:1, :], y)
    return jnp.where(rid == rows - 2, nxt[0:1, :], jnp.where(rid == rows - 1, nxt[1:2, :], y))


def _conv3(cur, prev, w, b):
    return (w[0:1, :] * _shift_rows(cur, prev, 2) + w[1:2, :] * _shift_rows(cur, prev, 1)
            + w[2:3, :] * cur + b)


def _conv_act_fwd(hu, cw, cb):
    _, S, F = hu.shape
    tm = _pick(S, (512, 256))
    tn = _pick(F, (256, 128))
    r8 = tm // 8

    def body(cur_ref, prev_ref, w_ref, b_ref, o_ref):
        i = pl.program_id(1)
        halves = []
        for h in range(2):
            prev = jnp.where(i > 0, prev_ref[h], 0.0)
            halves.append(_conv3(cur_ref[h], prev, w_ref[h], b_ref[h]))
        a, g = halves
        o_ref[...] = (g * _sigmoid(g) * a).astype(BF16)

    return pl.pallas_call(
        body, out_shape=jax.ShapeDtypeStruct((S, F), BF16), grid=(F // tn, S // tm),
        in_specs=[pl.BlockSpec((2, tm, tn), lambda j, i: (0, i, j)),
                  pl.BlockSpec((2, 8, tn), lambda j, i: (0, jnp.maximum(i * r8 - 1, 0), j)),
                  pl.BlockSpec((2, 8, tn), lambda j, i: (0, 0, j)),
                  pl.BlockSpec((2, 1, tn), lambda j, i: (0, 0, j))],
        out_specs=pl.BlockSpec((tm, tn), lambda j, i: (i, j)),
        name="conv_act_fwd", compiler_params=_cp(("parallel", "parallel"), VMEM_LIMIT),
    )(hu, hu, cw, cb)


def _loss_head(h2, g_final, target):
    S = h2.shape[0]
    tm = _pick(S, (512, 256))

    def body(h_ref, g_ref, t_ref, loss_ref, dh_ref, dhb_ref, dg_ref):
        i = pl.program_id(0)

        @pl.when(i == 0)
        def _():
            loss_ref[...] = jnp.zeros_like(loss_ref)
            dg_ref[...] = jnp.zeros_like(dg_ref)

        hf = h_ref[...]
        g = g_ref[...]
        r = lax.rsqrt(jnp.mean(hf * hf, axis=-1, keepdims=True) + EPS)
        hhat = hf * r
        err = hhat * g - t_ref[...]
        loss_ref[...] += 0.5 * jnp.sum(jnp.mean(err * err, axis=-1, keepdims=True))
        dy = err * (1.0 / D_MODEL)
        dg_ref[0:1, :] += jnp.sum(dy * hhat, axis=0, keepdims=True)
        dhat = dy * g
        dh = r * (dhat - hhat * jnp.mean(dhat * hhat, axis=-1, keepdims=True))
        dh_ref[...] = dh
        dhb_ref[...] = dh.astype(BF16)

    return pl.pallas_call(
        body,
        out_shape=(jax.ShapeDtypeStruct((8, LANES), F32), jax.ShapeDtypeStruct((S, D_MODEL), F32),
                   jax.ShapeDtypeStruct((S, D_MODEL), BF16), jax.ShapeDtypeStruct((8, D_MODEL), F32)),
        grid=(S // tm,),
        in_specs=[pl.BlockSpec((tm, D_MODEL), lambda i: (i, 0)), pl.BlockSpec((1, D_MODEL), lambda i: (0, 0)),
                  pl.BlockSpec((tm, D_MODEL), lambda i: (i, 0))],
        out_specs=(pl.BlockSpec((8, LANES), lambda i: (0, 0)), pl.BlockSpec((tm, D_MODEL), lambda i: (i, 0)),
                   pl.BlockSpec((tm, D_MODEL), lambda i: (i, 0)), pl.BlockSpec((8, D_MODEL), lambda i: (0, 0))),
        name="loss_head", compiler_params=_cp(("arbitrary",), VMEM_LIMIT),
    )(h2, g_final, target)


def _rms_bwd(h, g, dy, res):
    S = h.shape[0]
    tm = _pick(S, (512, 256))

    def body(h_ref, g_ref, dy_ref, r_ref, dh_ref, dhb_ref, dg_ref):
        i = pl.program_id(0)

        @pl.when(i == 0)
        def _():
            dg_ref[...] = jnp.zeros_like(dg_ref)

        hf = h_ref[...]
        dyv = dy_ref[...]
        r = lax.rsqrt(jnp.mean(hf * hf, axis=-1, keepdims=True) + EPS)
        hhat = hf * r
        dg_ref[0:1, :] += jnp.sum(dyv * hhat, axis=0, keepdims=True)
        dhat = dyv * g_ref[...]
        dh = r_ref[...] + r * (dhat - hhat * jnp.mean(dhat * hhat, axis=-1, keepdims=True))
        dh_ref[...] = dh
        dhb_ref[...] = dh.astype(BF16)

    return pl.pallas_call(
        body,
        out_shape=(jax.ShapeDtypeStruct((S, D_MODEL), F32), jax.ShapeDtypeStruct((S, D_MODEL), BF16),
                   jax.ShapeDtypeStruct((8, D_MODEL), F32)),
        grid=(S // tm,),
        in_specs=[pl.BlockSpec((tm, D_MODEL), lambda i: (i, 0)), pl.BlockSpec((1, D_MODEL), lambda i: (0, 0)),
                  pl.BlockSpec((tm, D_MODEL), lambda i: (i, 0)), pl.BlockSpec((tm, D_MODEL), lambda i: (i, 0))],
        out_specs=(pl.BlockSpec((tm, D_MODEL), lambda i: (i, 0)), pl.BlockSpec((tm, D_MODEL), lambda i: (i, 0)),
                   pl.BlockSpec((8, D_MODEL), lambda i: (0, 0))),
        name="rms_bwd", compiler_params=_cp(("arbitrary",), VMEM_LIMIT),
    )(h, g, dy, res)


def _conv_act_bwd(hu, dact, cw, cb):
    _, S, F = hu.shape
    tm = _pick(S, (256,))
    tn = _pick(F, (256, 128))
    r8 = tm // 8
    n_i = S // tm
    last8 = S // 8 - 1

    def body(cur_ref, prev_ref, next_ref, da_ref, dan_ref, w_ref, b_ref, dhu_ref, dcw_ref):
        i = pl.program_id(1)

        @pl.when(i == 0)
        def _():
            dcw_ref[...] = jnp.zeros_like(dcw_ref)

        rid8 = lax.broadcasted_iota(jnp.int32, (8, tn), 0)
        cur = [cur_ref[0], cur_ref[1]]
        prev = [jnp.where(i > 0, prev_ref[h], 0.0) for h in range(2)]
        nxt = [next_ref[0], next_ref[1]]
        w = [w_ref[0], w_ref[1]]

        def gate_grads(a, g, d):
            sg = _sigmoid(g)
            return d * (g * sg), d * a * (sg * (1.0 + g * (1.0 - sg)))

        a = _conv3(cur[0], prev[0], w[0], b_ref[0])
        g = _conv3(cur[1], prev[1], w[1], b_ref[1])
        dhc = gate_grads(a, g, da_ref[...])
        a_n = _conv3(nxt[0], cur[0][tm - 8:, :], w[0], b_ref[0])
        g_n = _conv3(nxt[1], cur[1][tm - 8:, :], w[1], b_ref[1])
        dhc_n = gate_grads(a_n, g_n, dan_ref[...])
        for h in range(2):
            d = dhc[h]
            dn = jnp.where(i < n_i - 1, dhc_n[h], 0.0)
            dhu = (w[h][2:3, :] * d + w[h][1:2, :] * _shift_rows_up(d, dn, 1)
                   + w[h][0:1, :] * _shift_rows_up(d, dn, 2))
            dhu_ref[h] = dhu.astype(BF16)
            t0 = jnp.sum(d * _shift_rows(cur[h], prev[h], 2), axis=0, keepdims=True)
            t1 = jnp.sum(d * _shift_rows(cur[h], prev[h], 1), axis=0, keepdims=True)
            t2 = jnp.sum(d * cur[h], axis=0, keepdims=True)
            t3 = jnp.sum(d, axis=0, keepdims=True)
            dcw_ref[h] += jnp.where(rid8 == 0, t0, jnp.where(rid8 == 1, t1, jnp.where(rid8 == 2, t2, jnp.where(rid8 == 3, t3, 0.0))))

    return pl.pallas_call(
        body,
        out_shape=(jax.ShapeDtypeStruct((2, S, F), BF16), jax.ShapeDtypeStruct((2, 8, F), F32)),
        grid=(F // tn, n_i),
        in_specs=[pl.BlockSpec((2, tm, tn), lambda j, i: (0, i, j)),
                  pl.BlockSpec((2, 8, tn), lambda j, i: (0, jnp.maximum(i * r8 - 1, 0), j)),
                  pl.BlockSpec((2, 8, tn), lambda j, i: (0, jnp.minimum((i + 1) * r8, last8), j)),
                  pl.BlockSpec((tm, tn), lambda j, i: (i, j)),
                  pl.BlockSpec((8, tn), lambda j, i: (jnp.minimum((i + 1) * r8, last8), j)),
                  pl.BlockSpec((2, 8, tn), lambda j, i: (0, 0, j)),
                  pl.BlockSpec((2, 1, tn), lambda j, i: (0, 0, j))],
        out_specs=(pl.BlockSpec((2, tm, tn), lambda j, i: (0, i, j)), pl.BlockSpec((2, 8, tn), lambda j, i: (0, 0, j))),
        name="conv_act_bwd", compiler_params=_cp(("parallel", "arbitrary"), VMEM_LIMIT),
    )(hu, hu, hu, dact, dact, cw, cb)


def _attn_prep(att, lse, dmix, qa):
    S = att.shape[0]
    tm = _pick(S, (512, 256))

    def body(o_ref, lse_ref, do_ref, q_ref, qb_ref, doa_ref):
        lane = lax.broadcasted_iota(jnp.int32, (tm, LANES), 1)
        do = do_ref[...]
        prod = o_ref[...] * do
        for hh in range(2):
            sel = (lane >= HEAD_DIM) if hh else (lane < HEAD_DIM)
            delta = jnp.sum(jnp.where(sel, prod, 0.0), axis=-1, keepdims=True)
            dod = pltpu.roll(do, HEAD_DIM, 1) if hh else do
            cols = slice(hh * LANES, (hh + 1) * LANES)
            doa_ref[:, cols] = jnp.where(lane < HEAD_DIM, dod, _aug(lane, _split3f(-delta))).astype(BF16)
            l3 = _split3f(-lse_ref[:, hh * HEAD_DIM:hh * HEAD_DIM + 1])
            augl = _aug(lane, [0.0] * 6 + l3).astype(BF16)
            qb_ref[:, cols] = jnp.where((lane >= HEAD_DIM + 6) & (lane < HEAD_DIM + 9), augl, q_ref[:, cols])

    return pl.pallas_call(
        body,
        out_shape=(jax.ShapeDtypeStruct(qa.shape, BF16), jax.ShapeDtypeStruct(qa.shape, BF16)),
        grid=(S // tm, N_PAIRS),
        in_specs=[pl.BlockSpec((tm, LANES), lambda i, p: (i, p)), pl.BlockSpec((tm, LANES), lambda i, p: (i, p)),
                  pl.BlockSpec((tm, LANES), lambda i, p: (i, p)), pl.BlockSpec((tm, 2 * LANES), lambda i, p: (i, p))],
        out_specs=(pl.BlockSpec((tm, 2 * LANES), lambda i, p: (i, p)), pl.BlockSpec((tm, 2 * LANES), lambda i, p: (i, p))),
        name="attn_prep", compiler_params=_cp(("parallel", "parallel")),
    )(att, lse, dmix, qa)


def _attn_bwd(qb, ka, va, doa):
    S = qb.shape[0]
    tk = _pick(S, (512, 256))
    tq = tk
    nq = S // tq

    def body(q_ref, do_ref, k_ref, v_ref, dq_ref, dk_ref, dv_ref, dka_ref, dva_ref):
        kb = pl.program_id(1)

        @pl.when(kb == 0)
        def _():
            dq_ref[...] = jnp.zeros_like(dq_ref)

        dka_ref[...] = jnp.zeros_like(dka_ref)
        dva_ref[...] = jnp.zeros_like(dva_ref)
        rid = lax.broadcasted_iota(jnp.int32, (tk, tq), 0)
        cid = lax.broadcasted_iota(jnp.int32, (tk, tq), 1)

        def tile(qi, masked):
            qs = pl.multiple_of(qi * tq, tq)
            for h in range(2):
                cols = slice(h * LANES, (h + 1) * LANES)
                qblk = q_ref[pl.ds(qs, tq), cols]
                doblk = do_ref[pl.ds(qs, tq), cols]
                kh = k_ref[:, cols]
                p = jnp.exp(lax.dot_general(kh, qblk, _NT, preferred_element_type=F32))
                if masked:
                    p = jnp.where(cid >= rid, p, 0.0)
                ds = (p * lax.dot_general(v_ref[:, cols], doblk, _NT, preferred_element_type=F32)).astype(BF16)
                dva_ref[:, cols] += jnp.dot(p.astype(BF16), doblk, preferred_element_type=F32)
                dka_ref[:, cols] += jnp.dot(ds, qblk, preferred_element_type=F32)
                dq_ref[pl.ds(qs, tq), cols] += lax.dot_general(ds, kh, _TN, preferred_element_type=F32)

        tile(kb, True)

        def step(qi, carry):
            tile(qi, False)
            return carry

        lax.fori_loop(kb + 1, nq, step, 0)
        dk_ref[...] = dka_ref[...]
        dv_ref[...] = dva_ref[...].astype(BF16)

    wide = 2 * LANES
    return pl.pallas_call(
        body,
        out_shape=(jax.ShapeDtypeStruct(qb.shape, F32), jax.ShapeDtypeStruct(qb.shape, F32),
                   jax.ShapeDtypeStruct(qb.shape, BF16)),
        grid=(N_PAIRS, nq),
        in_specs=[pl.BlockSpec((S, wide), lambda p, j: (0, p)), pl.BlockSpec((S, wide), lambda p, j: (0, p)),
                  pl.BlockSpec((tk, wide), lambda p, j: (j, p)), pl.BlockSpec((tk, wide), lambda p, j: (j, p))],
        out_specs=(pl.BlockSpec((S, wide), lambda p, j: (0, p)), pl.BlockSpec((tk, wide), lambda p, j: (j, p)),
                   pl.BlockSpec((tk, wide), lambda p, j: (j, p))),
        scratch_shapes=[pltpu.VMEM((tk, wide), F32), pltpu.VMEM((tk, wide), F32)],
        name="attn_bwd", compiler_params=_cp(("parallel", "arbitrary"), VMEM_LIMIT),
    )(qb, doa, ka, va)


def _attn_delta_old(o, do):
    S = o.shape[0]
    tm = _pick(S, (512, 256))
    ones = _group_ones()

    def body(o_ref, do_ref, ones_ref, d_ref):
        d_ref[...] = _dot3(o_ref[...] * do_ref[...], ones_ref[...])

    return pl.pallas_call(
        body, out_shape=jax.ShapeDtypeStruct((S, ATT_WIDTH), F32), grid=(S // tm,),
        in_specs=[pl.BlockSpec((tm, ATT_WIDTH), lambda i: (i, 0)), pl.BlockSpec((tm, ATT_WIDTH), lambda i: (i, 0)),
                  pl.BlockSpec((ATT_WIDTH, ATT_WIDTH), lambda i: (0, 0))],
        out_specs=pl.BlockSpec((tm, ATT_WIDTH), lambda i: (i, 0)),
        name="attn_delta", compiler_params=_cp(("parallel",)),
    )(o, do, ones)


def _attn_bwd_old(qkv, do_bf, c_cols, c_rows, lse_rows, dl_rows):
    S = qkv.shape[0]
    tk = _pick(S, (256,))
    tq = tk
    nq = S // tq
    nt = (((1,), (1,)), ((), ()))
    tn_dims = (((0,), (0,)), ((), ()))

    def body(q_ref, do_ref, k_ref, v_ref, cc_ref, cr_ref, lse_ref, dl_ref, dq_ref, dk_ref, dv_ref, dcs_ref, dcq_ref):
        kb = pl.program_id(1)

        @pl.when(kb == 0)
        def _():
            dq_ref[...] = jnp.zeros_like(dq_ref)
            dcq_ref[...] = jnp.zeros_like(dcq_ref)

        lane = lax.broadcasted_iota(jnp.int32, (tk, LANES), 1)
        rid = lax.broadcasted_iota(jnp.int32, (tk, tq), 0)
        cid = lax.broadcasted_iota(jnp.int32, (tk, tq), 1)
        k = k_ref[...]
        v = v_ref[...]
        dks, dvs, dcs = [], [], []
        for hh in range(2):
            sel = (lane >= GROUP_DIM) if hh else (lane < GROUP_DIM)
            km = jnp.where(sel, k, jnp.zeros_like(k))
            vm = jnp.where(sel, v, jnp.zeros_like(v))
            cs = cc_ref[:, hh * HEAD_DIM:hh * HEAD_DIM + 1]

            def step(qb, carry, km=km, vm=vm, cs=cs, hh=hh):
                dk_acc, dv_acc, dc_acc = carry
                qs = pl.multiple_of(qb * tq, tq)
                qblk = q_ref[pl.ds(qs, tq), :]
                doblk = do_ref[pl.ds(qs, tq), :]
                s = lax.dot_general(km, qblk, nt, preferred_element_type=F32)
                s = s + (cr_ref[hh:hh + 1, pl.ds(qs, tq)] - cs)
                p = jnp.exp(s - lse_ref[hh:hh + 1, pl.ds(qs, tq)])
                p = jnp.where((qb > kb) | (cid >= rid), p, 0.0)
                dp = lax.dot_general(vm, doblk, nt, preferred_element_type=F32)
                ds = p * (dp - dl_ref[hh:hh + 1, pl.ds(qs, tq)])
                ds_bf = ds.astype(BF16)
                dv_acc = dv_acc + jnp.dot(p.astype(BF16), doblk, preferred_element_type=F32)
                dk_acc = dk_acc + jnp.dot(ds_bf, qblk, preferred_element_type=F32)
                dc_acc = dc_acc + jnp.sum(ds, axis=-1, keepdims=True)
                dq_ref[pl.ds(qs, tq), :] += lax.dot_general(ds_bf, km, tn_dims, preferred_element_type=F32)
                dcq_ref[hh:hh + 1, pl.ds(qs, tq)] += jnp.sum(ds, axis=0, keepdims=True)
                return dk_acc, dv_acc, dc_acc

            init = (jnp.zeros((tk, LANES), F32), jnp.zeros((tk, LANES), F32), jnp.zeros((tk, 1), F32))
            dk_acc, dv_acc, dc_acc = lax.fori_loop(kb, nq, step, init)
            dks.append(dk_acc)
            dvs.append(dv_acc)
            dcs.append(dc_acc)
        dk_ref[...] = jnp.where(lane < GROUP_DIM, dks[0], dks[1])
        dv_ref[...] = jnp.where(lane < GROUP_DIM, dvs[0], dvs[1])
        dcs_ref[...] = jnp.where(lane == 0, -dcs[0], jnp.where(lane == 1, -dcs[1], 0.0))

    return pl.pallas_call(
        body,
        out_shape=(jax.ShapeDtypeStruct((S, ATT_WIDTH), F32), jax.ShapeDtypeStruct((S, ATT_WIDTH), F32),
                   jax.ShapeDtypeStruct((S, ATT_WIDTH), F32), jax.ShapeDtypeStruct((N_PAIRS, S, LANES), F32),
                   jax.ShapeDtypeStruct((N_PAIRS, 8, S), F32)),
        grid=(N_PAIRS, nq),
        in_specs=[pl.BlockSpec((S, LANES), lambda p, j: (0, p)),
                  pl.BlockSpec((S, LANES), lambda p, j: (0, p)),
                  pl.BlockSpec((tk, LANES), lambda p, j: (j, N_PAIRS + p)),
                  pl.BlockSpec((tk, LANES), lambda p, j: (j, 2 * N_PAIRS + p)),
                  pl.BlockSpec((None, tk, LANES), lambda p, j: (p, j, 0)),
                  pl.BlockSpec((None, 8, S), lambda p, j: (p, 0, 0)),
                  pl.BlockSpec((None, 8, S), lambda p, j: (p, 0, 0)),
                  pl.BlockSpec((None, 8, S), lambda p, j: (p, 0, 0))],
        out_specs=(pl.BlockSpec((S, LANES), lambda p, j: (0, p)),
                   pl.BlockSpec((tk, LANES), lambda p, j: (j, p)),
                   pl.BlockSpec((tk, LANES), lambda p, j: (j, p)),
                   pl.BlockSpec((None, tk, LANES), lambda p, j: (p, j, 0)),
                   pl.BlockSpec((None, 8, S), lambda p, j: (p, 0, 0))),
        name="attn_bwd", compiler_params=_cp(("parallel", "arbitrary"), VMEM_LIMIT),
    )(qkv, do_bf, qkv, qkv, c_cols, c_rows, lse_rows, dl_rows)


def _gmlp_bwd(ug, dsg, gain, w_s, wt_s, bias_full):
    S = ug.shape[0]
    tm = _pick(S, (512, 256, 128))
    n_chunks = tm // CHUNK
    n_i = S // tm
    ones = _group_ones()
    nt = (((1,), (1,)), ((), ()))

    def body(ug_ref, dsg_ref, gain_ref, w_ref, wt_ref, bias_ref, ones_ref, dug_ref, dw_ref, dgain_ref, dbias_ref,
             dbacc_ref):
        i = pl.program_id(0)

        @pl.when(i == 0)
        def _():
            dw_ref[...] = jnp.zeros_like(dw_ref)
            dgain_ref[...] = jnp.zeros_like(dgain_ref)
            dbacc_ref[...] = jnp.zeros_like(dbacc_ref)

        ones_m = ones_ref[...]
        pu = ug_ref[:, :GMLP_WIDTH]
        pg = ug_ref[:, GMLP_WIDTH:]
        u = _gelu(pu)
        vr = _gelu(pg)
        ms = _dot3(vr * vr, ones_m) * (1.0 / GROUP_DIM)
        rinv = lax.rsqrt(ms + EPS)
        vhat = vr * rinv
        gain_v = gain_ref[...]
        vn = (vhat * gain_v).astype(BF16)
        mixed = _gmlp_mixed(vn, w_ref, bias_ref[...], n_chunks)
        dsg_v = dsg_ref[...]
        du = dsg_v * mixed
        dmixed = dsg_v * u
        dm_bf = dmixed.astype(BF16)
        lane = lax.broadcasted_iota(jnp.int32, (CHUNK, LANES), 1)
        row = lax.broadcasted_iota(jnp.int32, (CHUNK, CHUNK), 0)
        col = lax.broadcasted_iota(jnp.int32, (CHUNK, CHUNK), 1)
        wts = [jnp.where(col >= row, wt_ref[g], 0.0).astype(BF16) for g in range(N_GROUPS)]
        dvn_rows = []
        dbsum = jnp.zeros((CHUNK, GMLP_WIDTH), F32)
        for ci in range(n_chunks):
            rs = slice(ci * CHUNK, (ci + 1) * CHUNK)
            dbsum = dbsum + dmixed[rs, :]
            cols = []
            for pp in range(N_GROUPS // 2):
                cs = slice(pp * LANES, (pp + 1) * LANES)
                dm = dm_bf[rs, cs]
                dm_lo = jnp.where(lane < GROUP_DIM, dm, jnp.zeros_like(dm))
                dm_hi = jnp.where(lane >= GROUP_DIM, dm, jnp.zeros_like(dm))
                vb = vn[rs, cs]
                dw_ref[2 * pp] += lax.dot_general(dm_lo, vb, nt, preferred_element_type=F32)
                dw_ref[2 * pp + 1] += lax.dot_general(dm_hi, vb, nt, preferred_element_type=F32)
                cols.append(jnp.dot(wts[2 * pp], dm_lo, preferred_element_type=F32)
                            + jnp.dot(wts[2 * pp + 1], dm_hi, preferred_element_type=F32))
            dvn_rows.append(jnp.concatenate(cols, axis=1))
        dvn = jnp.concatenate(dvn_rows, axis=0)
        dbacc_ref[...] += dbsum
        dgain_ref[0:1, :] += jnp.sum(dvn * vhat, axis=0, keepdims=True)
        dvhat = dvn * gain_v
        gm = _dot3(dvhat * vhat, ones_m) * (1.0 / GROUP_DIM)
        dvr = rinv * (dvhat - vhat * gm)
        dug_ref[:, :GMLP_WIDTH] = (du * _gelu_grad(pu)).astype(BF16)
        dug_ref[:, GMLP_WIDTH:] = (dvr * _gelu_grad(pg)).astype(BF16)

        @pl.when(i == n_i - 1)
        def _():
            for g in range(N_GROUPS):
                dw_ref[g] = jnp.where(row >= col, dw_ref[g], 0.0)
            dbias_ref[...] = _dot3(dbacc_ref[...], ones_m)

    return pl.pallas_call(
        body,
        out_shape=(jax.ShapeDtypeStruct((S, 2 * GMLP_WIDTH), BF16), jax.ShapeDtypeStruct((N_GROUPS, CHUNK, CHUNK), F32),
                   jax.ShapeDtypeStruct((8, GMLP_WIDTH), F32), jax.ShapeDtypeStruct((CHUNK, GMLP_WIDTH), F32)),
        grid=(n_i,),
        in_specs=[pl.BlockSpec((tm, 2 * GMLP_WIDTH), lambda i: (i, 0)), pl.BlockSpec((tm, GMLP_WIDTH), lambda i: (i, 0)),
                  pl.BlockSpec((1, GMLP_WIDTH), lambda i: (0, 0)),
                  pl.BlockSpec((N_GROUPS, CHUNK, CHUNK), lambda i: (0, 0, 0)),
                  pl.BlockSpec((N_GROUPS, CHUNK, CHUNK), lambda i: (0, 0, 0)),
                  pl.BlockSpec((CHUNK, GMLP_WIDTH), lambda i: (0, 0)),
                  pl.BlockSpec((GMLP_WIDTH, GMLP_WIDTH), lambda i: (0, 0))],
        out_specs=(pl.BlockSpec((tm, 2 * GMLP_WIDTH), lambda i: (i, 0)),
                   pl.BlockSpec((N_GROUPS, CHUNK, CHUNK), lambda i: (0, 0, 0)),
                   pl.BlockSpec((8, GMLP_WIDTH), lambda i: (0, 0)),
                   pl.BlockSpec((CHUNK, GMLP_WIDTH), lambda i: (0, 0))),
        scratch_shapes=[pltpu.VMEM((CHUNK, GMLP_WIDTH), F32)],
        name="gmlp_bwd", compiler_params=_cp(("arbitrary",), VMEM_LIMIT),
    )(ug, dsg, gain, w_s, wt_s, bias_full, ones)


def _gate_bwd(dc, zf):
    S = zf.shape[0]
    tm = _pick(S, (256,))
    n_i = S // tm
    triu = (lax.broadcasted_iota(jnp.int32, (tm, tm), 0) <= lax.broadcasted_iota(jnp.int32, (tm, tm), 1)).astype(BF16)

    def body(dc_ref, zf_ref, tri_ref, dzf_ref, dbf_ref, carry_ref):
        i = pl.program_id(0)

        @pl.when(i == 0)
        def _():
            carry_ref[...] = jnp.zeros_like(carry_ref)
            dbf_ref[...] = jnp.zeros_like(dbf_ref)

        lane = lax.broadcasted_iota(jnp.int32, (tm, LANES), 1)
        dlf = _dot3l(tri_ref[...], dc_ref[...]) + carry_ref[0:1, :]
        carry_ref[0:1, :] = dlf[0:1, :]
        dz = jnp.where(lane < N_HEADS, dlf * _sigmoid(-zf_ref[...]), 0.0)
        dzf_ref[...] = dz.astype(BF16)
        dbf_ref[0:1, :] += jnp.sum(dz, axis=0, keepdims=True)

    return pl.pallas_call(
        body,
        out_shape=(jax.ShapeDtypeStruct((S, LANES), BF16), jax.ShapeDtypeStruct((8, LANES), F32)),
        grid=(n_i,),
        in_specs=[pl.BlockSpec((tm, LANES), lambda i: (n_i - 1 - i, 0)),
                  pl.BlockSpec((tm, LANES), lambda i: (n_i - 1 - i, 0)),
                  pl.BlockSpec((tm, tm), lambda i: (0, 0))],
        out_specs=(pl.BlockSpec((tm, LANES), lambda i: (n_i - 1 - i, 0)), pl.BlockSpec((8, LANES), lambda i: (0, 0))),
        scratch_shapes=[pltpu.VMEM((8, LANES), F32)],
        name="gate_bwd", compiler_params=_cp(("arbitrary",), VMEM_LIMIT),
    )(dc, zf, triu)


def _adamw(w, m, v, parts, name):
    R, C = w.shape
    tr = R
    for cand in (256, 128, 64, 32, 16, 8):
        if R % cand == 0 and R > cand:
            tr = cand
            break
    c1 = 1.0 / (1.0 - ADAM_B1 ** ADAM_STEP)
    c2 = 1.0 / (1.0 - ADAM_B2 ** ADAM_STEP)

    def body(w_ref, m_ref, v_ref, p_ref, g_ref, d_ref, nm_ref, nv_ref):
        g = p_ref[0]
        for j in range(1, N_DEV):
            g = g + p_ref[j]
        g_ref[...] = g
        nm = ADAM_B1 * m_ref[...] + (1.0 - ADAM_B1) * g
        nv = ADAM_B2 * v_ref[...] + (1.0 - ADAM_B2) * (g * g)
        nm_ref[...] = nm
        nv_ref[...] = nv
        d_ref[...] = -ADAM_LR * ((nm * c1) / (jnp.sqrt(nv * c2) + ADAM_EPS) + ADAM_WD * w_ref[...])

    spec = pl.BlockSpec((tr, C), lambda i: (i, 0))
    shp = jax.ShapeDtypeStruct((R, C), F32)
    return pl.pallas_call(
        body, out_shape=(shp, shp, shp, shp), grid=(R // tr,),
        in_specs=[spec, spec, spec, pl.BlockSpec((N_DEV, tr, C), lambda i: (0, i, 0))],
        out_specs=(spec, spec, spec, spec),
        name=name, compiler_params=_cp(("parallel",), VMEM_LIMIT),
    )(w, m, v, parts)


def _place():
    x, y, c = lax.axis_index("x"), lax.axis_index("y"), lax.axis_index("c")
    return x, y, c


def _all_gather(blocks, name):
    n = len(blocks)

    def body(*refs):
        ins, outs = refs[:n], refs[n:2 * n]
        send_sems, recv_sems, local_sems = refs[2 * n:]
        x, y, c = _place()
        me, sibling = (x, y, c), (x, y, 1 - c)
        chips = [(1 - x, y), (x, 1 - y), (1 - x, 1 - y)]
        sends = []
        for a in range(n):
            out = outs[a]

            def slot(px, py, pc, out=out):
                return out.at[4 * px + 2 * py + pc]

            def copy(k, block, to, src=None, a=a, slot=slot):
                return pltpu.make_async_remote_copy(
                    src_ref=slot(*block) if src is None else src, dst_ref=slot(*block),
                    send_sem=send_sems.at[a, k], recv_sem=recv_sems.at[a, k], device_id=to, device_id_type=MESH)

            mine = pltpu.make_async_copy(ins[a], slot(*me), local_sems.at[a])
            mine.start()
            first = [copy(0, me, sibling, src=ins[a])]
            first += [copy(1 + j, me, (*chip, c), src=ins[a]) for j, chip in enumerate(chips)]
            for cp in first:
                cp.start()
            sends.append((mine, first, copy))
        for a in range(n):
            mine, first, copy = sends[a]
            passed = [copy(4 + j, (*chip, c), sibling) for j, chip in enumerate(chips)]
            for j, chip in enumerate(chips):
                copy(1 + j, (*chip, c), me).wait_recv()
                passed[j].start()
            copy(0, sibling, me).wait_recv()
            for j, chip in enumerate(chips):
                copy(4 + j, (*chip, 1 - c), me).wait_recv()
            for cp in first + passed:
                cp.wait_send()
            mine.wait()

    any_spec = pl.BlockSpec(memory_space=pl.ANY)
    return pl.pallas_call(
        body, out_shape=tuple(jax.ShapeDtypeStruct((N_DEV,) + b.shape, b.dtype) for b in blocks),
        in_specs=[any_spec] * n, out_specs=tuple([any_spec] * n),
        scratch_shapes=[pltpu.SemaphoreType.DMA((n, 7)), pltpu.SemaphoreType.DMA((n, 7)), pltpu.SemaphoreType.DMA((n,))],
        name=name,
    )(*blocks)


def _exchange_shards(parts, name):
    n = len(parts)

    def body(*refs):
        ins, outs = refs[:n], refs[n:2 * n]
        send_sems, recv_sems, local_sems = refs[2 * n:]
        x, y, c = _place()
        me = 4 * x + 2 * y + c
        started = []
        for a in range(n):
            mine = pltpu.make_async_copy(ins[a].at[me], outs[a].at[me], local_sems.at[a])
            mine.start()
            started.append(mine)
            for k in range(1, N_DEV):
                px, py, pc = x ^ ((k >> 2) & 1), y ^ ((k >> 1) & 1), c ^ (k & 1)
                cp = pltpu.make_async_remote_copy(
                    src_ref=ins[a].at[4 * px + 2 * py + pc], dst_ref=outs[a].at[me],
                    send_sem=send_sems.at[a, k - 1], recv_sem=recv_sems.at[a, k - 1],
                    device_id=(px, py, pc), device_id_type=MESH)
                cp.start()
                started.append(cp)
        for cp in started:
            cp.wait()

    any_spec = pl.BlockSpec(memory_space=pl.ANY)
    return pl.pallas_call(
        body, out_shape=tuple(jax.ShapeDtypeStruct(p.shape, p.dtype) for p in parts),
        in_specs=[any_spec] * n, out_specs=tuple([any_spec] * n),
        scratch_shapes=[pltpu.SemaphoreType.DMA((n, 7)), pltpu.SemaphoreType.DMA((n, 7)), pltpu.SemaphoreType.DMA((n,))],
        name=name,
    )(*parts)


def _rows128(a):
    flat = a.reshape(-1)
    rows = -(-flat.shape[0] // LANES)
    rows = -(-rows // 8) * 8
    return jnp.pad(flat, (0, rows * LANES - flat.shape[0])).reshape(rows, LANES)


def _local_step(x, target, norm_mix_g, w_in_bf, b_forget, gmlp_norm_g, w_spatial, b_spatial, w_out_bf, norm_ffn_g,
                w_up_bf, conv_w, conv_b, w_down_bf, norm_final_g):
    S = x.shape[0]
    f = D_FF
    w_pad = jnp.pad(w_in_bf, ((0, 0), (0, IN_PAD - IN_COLS)))
    bf_pad = jnp.pad(b_forget.reshape(1, N_HEADS), ((0, 0), (0, LANES - N_HEADS)))
    xn, qa, ka, va, ug, zf = _inproj_fwd(x, norm_mix_g.reshape(1, D_MODEL), w_pad, bf_pad)
    bias_full = jnp.repeat(b_spatial.reshape(N_GROUPS, CHUNK).T, GROUP_DIM, axis=1)
    w_s = w_spatial.reshape(N_GROUPS, CHUNK, CHUNK)
    gain = gmlp_norm_g.reshape(1, GMLP_WIDTH)
    sg = _gmlp_fwd(ug, gain, w_s, bias_full)
    att, lse = _attn_fwd(qa, ka, va)
    mix = jnp.concatenate([att.astype(BF16), sg], axis=1)
    h1 = _mm(mix, w_out_bf, mode="nn", out_dtype=F32, tm=512, tn=1024, tk=1024, res=x, name="out_proj")
    g_ffn = norm_ffn_g.reshape(1, D_MODEL)
    hn = _rms_fwd(h1, g_ffn)
    hu = _mm(hn, w_up_bf, mode="nn", out_dtype=F32, tm=512, tn=256, tk=1024, out_halves=True, outer="j", name="ffn_up")
    cw = jnp.pad(conv_w.reshape(3, 2, f).transpose(1, 0, 2), ((0, 0), (0, 5), (0, 0)))
    cb = conv_b.reshape(2, 1, f)
    act = _conv_act_fwd(hu, cw, cb)
    h2 = _mm(act, w_down_bf, mode="nn", out_dtype=F32, tm=512, tn=1024, tk=1408, res=h1, name="ffn_down")
    loss_blk, dh2, dh2_bf, dg_final = _loss_head(h2, norm_final_g.reshape(1, D_MODEL), target)
    dact = _mm(dh2_bf, w_down_bf, mode="nt", out_dtype=F32, tm=512, tn=256, tk=1024, outer="j", name="ffn_down_dx")
    dw_down = _mm(act, dh2_bf, mode="tn", out_dtype=F32, tm=1408, tn=1024, tk=1024, name="ffn_down_dw")
    dhu, dcw = _conv_act_bwd(hu, dact, cw, cb)
    dhn = _mm(dhu, w_up_bf, mode="nt", out_dtype=F32, tm=512, tn=1024, tk=1408, a_halves=True, name="ffn_up_dx")
    dw_up = _mm(hn, dhu, mode="tn", out_dtype=F32, tm=1024, tn=1408, tk=1024, b_halves=True, outer="j", name="ffn_up_dw")
    dh1, dh1_bf, dg_ffn = _rms_bwd(h1, g_ffn, dhn, dh2)
    dmix = _mm(dh1_bf, w_out_bf, mode="nt", out_dtype=F32, tm=512, tn=1024, tk=1024, name="out_proj_dx")
    dw_out = _mm(mix, dh1_bf, mode="tn", out_dtype=F32, tm=1024, tn=1024, tk=1024, name="out_proj_dw")
    dsg = dmix[:, ATT_WIDTH:]
    qb, doa = _attn_prep(att, lse, dmix, qa)
    dqa, dka, dva = _attn_bwd(qb, ka, va, doa)

    def heads(a):
        return a.reshape(S, N_HEADS, LANES)[:, :, :HEAD_DIM].reshape(S, ATT_WIDTH)

    wt_s = w_s.transpose(0, 2, 1)
    dug, dw_s, dgain, dbias = _gmlp_bwd(ug, dsg, gain, w_s, wt_s, bias_full)
    dc = dqa[:, HEAD_DIM::LANES] - dka[:, HEAD_DIM + 3::LANES]
    dzf, dbf = _gate_bwd(jnp.pad(dc, ((0, 0), (0, LANES - N_HEADS))), zf)
    dproj = jnp.concatenate([(heads(dqa) * (HEAD_DIM ** -0.5)).astype(BF16), heads(dka).astype(BF16), heads(dva),
                             dug, dzf], axis=1)
    dxn = _mm(dproj, w_pad, mode="nt", out_dtype=F32, tm=512, tn=1024, tk=896, name="in_proj_dx")
    dw_in = _mm(xn, dproj, mode="tn", out_dtype=F32, tm=1024, tn=896, tk=1024, outer="j", name="in_proj_dw")
    grad_x, _, dg_mix = _rms_bwd(x, norm_mix_g.reshape(1, D_MODEL), dxn, dh1)
    grads = dict(
        norm_mix_g=dg_mix[0:1, :],
        w_in=dw_in[:, :IN_COLS],
        b_forget=dbf[0:1, :N_HEADS],
        gmlp_norm_g=dgain[0:1, :],
        w_spatial=dw_s,
        b_spatial=dbias[:, ::GROUP_DIM].T,
        w_out=dw_out,
        norm_ffn_g=dg_ffn[0:1, :],
        w_up=dw_up,
        conv_w=dcw[:, 0:3, :].transpose(1, 0, 2).reshape(3, 2 * f),
        conv_b=dcw[:, 3, :].reshape(1, 2 * f),
        w_down=dw_down,
        norm_final_g=dg_final[0, :],
    )
    return loss_blk[0, 0], grad_x, grads


SMALL = ("norm_mix_g", "b_forget", "gmlp_norm_g", "w_spatial", "b_spatial", "norm_ffn_g", "conv_b", "norm_final_g")


def kernel(x, norm_mix_g, w_in, b_forget, gmlp_norm_g, w_spatial, b_spatial, w_out, norm_ffn_g, w_up, conv_w, conv_b, w_down, norm_final_g, loss_target, m_norm_mix_g, m_w_in, m_b_forget, m_gmlp_norm_g, m_w_spatial, m_b_spatial, m_w_out, m_norm_ffn_g, m_w_up, m_conv_w, m_conv_b, m_w_down, m_norm_final_g, v_norm_mix_g, v_w_in, v_b_forget, v_gmlp_norm_g, v_w_spatial, v_b_spatial, v_w_out, v_norm_ffn_g, v_w_up, v_conv_w, v_conv_b, v_w_down, v_norm_final_g):
    weights = dict(norm_mix_g=norm_mix_g, w_in=w_in, b_forget=b_forget, gmlp_norm_g=gmlp_norm_g, w_spatial=w_spatial,
                   b_spatial=b_spatial, w_out=w_out, norm_ffn_g=norm_ffn_g, w_up=w_up, conv_w=conv_w, conv_b=conv_b,
                   w_down=w_down, norm_final_g=norm_final_g)
    m_in = dict(norm_mix_g=m_norm_mix_g, w_in=m_w_in, b_forget=m_b_forget, gmlp_norm_g=m_gmlp_norm_g,
                w_spatial=m_w_spatial, b_spatial=m_b_spatial, w_out=m_w_out, norm_ffn_g=m_norm_ffn_g, w_up=m_w_up,
                conv_w=m_conv_w, conv_b=m_conv_b, w_down=m_w_down, norm_final_g=m_norm_final_g)
    v_in = dict(norm_mix_g=v_norm_mix_g, w_in=v_w_in, b_forget=v_b_forget, gmlp_norm_g=v_gmlp_norm_g,
                w_spatial=v_w_spatial, b_spatial=v_b_spatial, w_out=v_w_out, norm_ffn_g=v_norm_ffn_g, w_up=v_w_up,
                conv_w=v_conv_w, conv_b=v_conv_b, w_down=v_w_down, norm_final_g=v_norm_final_g)
    order = list(weights)
    me = 4 * lax.axis_index("x") + 2 * lax.axis_index("y") + lax.axis_index("c")
    n_in, n_up = w_in.shape[2], w_up.shape[2]
    r_out, r_down = w_out.shape[1], w_down.shape[1]

    cols_blk = jnp.concatenate([w_in[0].astype(BF16), w_up[0].astype(BF16)], axis=1)
    rows_blk = jnp.concatenate([w_out[0].astype(BF16), w_down[0].astype(BF16)], axis=0)
    taps_blk = jnp.pad(conv_w[0], ((0, 5), (0, 0)))
    cols_all, rows_all, taps_all = _all_gather([cols_blk, rows_blk, taps_blk], "gather_weights")
    w_in_bf = cols_all[:, :, :n_in].transpose(1, 0, 2).reshape(D_MODEL, N_DEV * n_in)
    w_up_bf = cols_all[:, :, n_in:].transpose(1, 0, 2).reshape(D_MODEL, N_DEV * n_up)
    w_out_bf = rows_all[:, :r_out, :].reshape(N_DEV * r_out, D_MODEL)
    w_down_bf = rows_all[:, r_out:, :].reshape(N_DEV * r_down, D_MODEL)
    conv_w_full = taps_all[:, :3, :].transpose(1, 0, 2).reshape(3, N_DEV * n_up)

    loss_local, grad_x, g = _local_step(
        x[0], loss_target[0], norm_mix_g, w_in_bf, b_forget, gmlp_norm_g, w_spatial, b_spatial, w_out_bf, norm_ffn_g,
        w_up_bf, conv_w_full, conv_b, w_down_bf, norm_final_g)
    loss = lax.psum(loss_local, ("x", "y", "c"))

    to_in = g["w_in"].reshape(D_MODEL, N_DEV, n_in).transpose(1, 0, 2)
    to_up = g["w_up"].reshape(D_MODEL, N_DEV, n_up).transpose(1, 0, 2)
    to_out = g["w_out"].reshape(N_DEV, r_out, D_MODEL)
    to_down = g["w_down"].reshape(N_DEV, r_down, D_MODEL)
    got_in, got_up, got_out, got_down = _exchange_shards([to_in, to_up, to_out, to_down], "scatter_grads")

    small_names = SMALL + ("conv_w",)
    packed = [_rows128(g[k]) for k in small_names]
    sizes = [p.shape[0] for p in packed]
    (small_all,) = _all_gather([jnp.concatenate(packed, axis=0)], "gather_small_grads")

    def pack(src):
        return jnp.concatenate([_rows128(src[k]) for k in SMALL], axis=0)

    n_small_rows = sum(sizes[:-1])
    sg_, sd_, sm_, sv_ = _adamw(pack(weights), pack(m_in), pack(v_in), small_all[:, :n_small_rows, :], "adamw_small")

    outs = {}
    off = 0
    for k, rows in zip(SMALL, sizes[:-1]):
        shp = weights[k].shape
        cnt = math.prod(shp)
        outs[k] = tuple(a[off:off + rows].reshape(-1)[:cnt].reshape(shp) for a in (sg_, sd_, sm_, sv_))
        off += rows
    taps_parts = small_all[:, n_small_rows:, :].reshape(N_DEV, -1)[:, :3 * N_DEV * n_up].reshape(N_DEV, 3, N_DEV * n_up)
    taps_mine = lax.dynamic_slice_in_dim(taps_parts, me * n_up, n_up, axis=2)
    taps_mine = jnp.pad(taps_mine, ((0, 0), (0, 5), (0, 0)))

    def pad8(a):
        return jnp.pad(a[0], ((0, 5), (0, 0)))

    res = _adamw(pad8(conv_w), pad8(m_conv_w), pad8(v_conv_w), taps_mine, "adamw_conv_w")
    outs["conv_w"] = tuple(a[:3][None] for a in res)
    for k, got in (("w_in", got_in), ("w_up", got_up), ("w_out", got_out), ("w_down", got_down)):
        res = _adamw(weights[k][0], m_in[k][0], v_in[k][0], got, "adamw_" + k)
        outs[k] = tuple(a[None] for a in res)

    return (loss, grad_x[None], *[outs[k][0] for k in order], *[outs[k][1] for k in order],
            *[outs[k][2] for k in order], *[outs[k][3] for k in order])
```

```python
import functools
import math

import jax
import jax.numpy as jnp
from jax import lax
from jax.experimental import pallas as pl
from jax.experimental.pallas import tpu as pltpu

F32 = jnp.float32
BF16 = jnp.bfloat16

N_DEV = 8
D_MODEL = 1024
ATT_WIDTH = 512
GMLP_WIDTH = 512
HEAD_DIM = 64
N_HEADS = 8
N_PAIRS = 4
N_GROUPS = 8
GROUP_DIM = 64
CHUNK = 128
D_FF = 2816
IN_COLS = 2568
IN_PAD = 2688
QKV = 1536
UG_END = 2560
EPS = 1e-6
LANES = 128

ADAM_LR = 0.001
ADAM_B1 = 0.9
ADAM_B2 = 0.999
ADAM_EPS = 1e-08
ADAM_WD = 0.01
ADAM_STEP = 10

VMEM_LIMIT = 56 * 1024 * 1024
MESH = pl.DeviceIdType.MESH


def _cp(sem, vmem=None):
    return pltpu.CompilerParams(dimension_semantics=sem, vmem_limit_bytes=vmem)


def _pick(n, prefs):
    for p in prefs:
        if n % p == 0:
            return p
    return n


def _split3(x):
    hi = x.astype(BF16)
    r1 = x - hi.astype(F32)
    mid = r1.astype(BF16)
    lo = (r1 - mid.astype(F32)).astype(BF16)
    return hi, mid, lo


def _dot3(x, ones_bf):
    hi, mid, lo = _split3(x)
    d = functools.partial(jnp.dot, preferred_element_type=F32)
    return d(hi, ones_bf) + d(mid, ones_bf) + d(lo, ones_bf)


def _dot3l(ones_bf, x):
    hi, mid, lo = _split3(x)
    d = functools.partial(jnp.dot, preferred_element_type=F32)
    return d(ones_bf, hi) + d(ones_bf, mid) + d(ones_bf, lo)


def _gelu(x):
    k = math.sqrt(2.0 / math.pi)
    t = jnp.tanh(k * (x + 0.044715 * (x * x * x)))
    return 0.5 * x * (1.0 + t)


def _gelu_grad(x):
    k = math.sqrt(2.0 / math.pi)
    x2 = x * x
    t = jnp.tanh(k * (x + 0.044715 * (x2 * x)))
    return 0.5 * (1.0 + t) + 0.5 * x * (1.0 - t * t) * (k * (1.0 + 3.0 * 0.044715 * x2))


def _sigmoid(x):
    return 1.0 / (1.0 + jnp.exp(-x))


def _mm(a, b, *, mode, out_dtype, tm, tn, tk, name, res=None, a_halves=False, b_halves=False,
        out_halves=False, outer="i"):
    if mode == "tn":
        K, M = a.shape[-2], a.shape[-1] * (2 if a_halves else 1)
    else:
        M, K = a.shape[-2], a.shape[-1] * (2 if a_halves else 1)
    if mode == "nt":
        N = b.shape[-2]
        assert b.shape[-1] == K
    else:
        N = b.shape[-1] * (2 if b_halves else 1)
    tm, tn, tk = min(tm, M), min(tn, N), min(tk, K)
    assert M % tm == 0 and N % tn == 0 and K % tk == 0, (name, M, N, K, tm, tn, tk)
    nm, nn, nk = M // tm, N // tn, K // tk

    def ij(g0, g1):
        return (g0, g1) if outer == "i" else (g1, g0)

    if mode == "nn":
        dims = (((1,), (0,)), ((), ()))
        if a_halves:
            nkh = nk // 2
            a_spec = pl.BlockSpec((None, tm, tk), lambda g0, g1, k: (k // nkh, ij(g0, g1)[0], k % nkh))
        else:
            a_spec = pl.BlockSpec((tm, tk), lambda g0, g1, k: (ij(g0, g1)[0], k))
        b_spec = pl.BlockSpec((tk, tn), lambda g0, g1, k: (k, ij(g0, g1)[1]))
    elif mode == "nt":
        dims = (((1,), (1,)), ((), ()))
        if a_halves:
            nkh = nk // 2
            a_spec = pl.BlockSpec((None, tm, tk), lambda g0, g1, k: (k // nkh, ij(g0, g1)[0], k % nkh))
        else:
            a_spec = pl.BlockSpec((tm, tk), lambda g0, g1, k: (ij(g0, g1)[0], k))
        b_spec = pl.BlockSpec((tn, tk), lambda g0, g1, k: (ij(g0, g1)[1], k))
    else:
        dims = (((0,), (0,)), ((), ()))
        if a_halves:
            nmh = nm // 2
            a_spec = pl.BlockSpec((None, tk, tm), lambda g0, g1, k: (ij(g0, g1)[0] // nmh, k, ij(g0, g1)[0] % nmh))
        else:
            a_spec = pl.BlockSpec((tk, tm), lambda g0, g1, k: (k, ij(g0, g1)[0]))
        if b_halves:
            nnh = nn // 2
            b_spec = pl.BlockSpec((None, tk, tn), lambda g0, g1, k: (ij(g0, g1)[1] // nnh, k, ij(g0, g1)[1] % nnh))
        else:
            b_spec = pl.BlockSpec((tk, tn), lambda g0, g1, k: (k, ij(g0, g1)[1]))
    if out_halves:
        nnh = nn // 2
        o_spec = pl.BlockSpec((None, tm, tn), lambda g0, g1, k: (ij(g0, g1)[1] // nnh, ij(g0, g1)[0], ij(g0, g1)[1] % nnh))
        o_shape = jax.ShapeDtypeStruct((2, M, N // 2), out_dtype)
    else:
        o_spec = pl.BlockSpec((tm, tn), lambda g0, g1, k: ij(g0, g1))
        o_shape = jax.ShapeDtypeStruct((M, N), out_dtype)
    in_specs = [a_spec, b_spec]
    args = [a, b]
    if res is not None:
        in_specs.append(pl.BlockSpec((tm, tn), lambda g0, g1, k: ij(g0, g1)))
        args.append(res)

    def body(*refs):
        if res is not None:
            a_ref, b_ref, r_ref, o_ref = refs[:4]
        else:
            a_ref, b_ref, o_ref = refs[:3]
            r_ref = None
        part = lax.dot_general(a_ref[...], b_ref[...], dims, preferred_element_type=F32)
        if nk == 1:
            if r_ref is not None:
                part = part + r_ref[...]
            o_ref[...] = part.astype(out_dtype)
            return
        acc_ref = refs[-1]
        k = pl.program_id(2)

        @pl.when(k == 0)
        def _():
            acc_ref[...] = part

        @pl.when(k > 0)
        def _():
            acc_ref[...] += part

        @pl.when(k == nk - 1)
        def _():
            tot = acc_ref[...]
            if r_ref is not None:
                tot = tot + r_ref[...]
            o_ref[...] = tot.astype(out_dtype)

    grid = (nm, nn, nk) if outer == "i" else (nn, nm, nk)
    scratch = [] if nk == 1 else [pltpu.VMEM((tm, tn), F32)]
    return pl.pallas_call(
        body, out_shape=o_shape, grid=grid, in_specs=in_specs, out_specs=o_spec, scratch_shapes=scratch,
        name=name, compiler_params=_cp(("parallel", "parallel", "arbitrary"), VMEM_LIMIT),
    )(*args)


def _aug(lane, terms):
    out = 0.0
    for j, t in enumerate(terms):
        out = jnp.where(lane == HEAD_DIM + j, t, out)
    return out


def _split3f(x):
    hi, mid, lo = _split3(x)
    return [hi.astype(F32), mid.astype(F32), lo.astype(F32)]


def _inproj_fwd(x, g_mix, w_pad, bf_pad):
    S = x.shape[0]
    tm = _pick(S, (256,))
    tri = (lax.broadcasted_iota(jnp.int32, (tm, tm), 0) >= lax.broadcasted_iota(jnp.int32, (tm, tm), 1)).astype(BF16)

    def body(x_ref, g_ref, w_ref, bf_ref, tri_ref, xn_ref, qa_ref, ka_ref, va_ref, ug_ref, zf_ref, carry_ref):
        i = pl.program_id(0)

        @pl.when(i == 0)
        def _():
            carry_ref[...] = jnp.zeros_like(carry_ref)

        xf = x_ref[...]
        r = lax.rsqrt(jnp.mean(xf * xf, axis=-1, keepdims=True) + EPS)
        xn = ((xf * r) * g_ref[...]).astype(BF16)
        xn_ref[...] = xn
        proj = jnp.dot(xn, w_ref[...], preferred_element_type=F32)
        ug_ref[...] = proj[:, QKV:UG_END]
        zf = proj[:, UG_END:] + bf_ref[...]
        zf_ref[...] = zf
        lf = jnp.minimum(zf, 0.0) - jnp.log(1.0 + jnp.exp(-jnp.abs(zf)))
        c = _dot3l(tri_ref[...], lf) + carry_ref[0:1, :]
        carry_ref[0:1, :] = c[tm - 1:tm, :]
        c3 = _split3f(c)
        lane = lax.broadcasted_iota(jnp.int32, (tm, LANES), 1)
        ones3 = [1.0, 1.0, 1.0]
        for h in range(N_HEADS):
            p, odd = h // 2, h % 2
            ch = [t[:, h:h + 1] for t in c3]

            def head(base, scale=None, p=p, odd=odd):
                blk = proj[:, base + p * LANES:base + (p + 1) * LANES]
                if scale is not None:
                    blk = blk * scale
                return pltpu.roll(blk, HEAD_DIM, 1) if odd else blk

            cols = slice(h * LANES, (h + 1) * LANES)
            qa_ref[:, cols] = jnp.where(lane < HEAD_DIM, head(0, HEAD_DIM ** -0.5), _aug(lane, ch + ones3)).astype(BF16)
            ka_ref[:, cols] = jnp.where(lane < HEAD_DIM, head(ATT_WIDTH),
                                        _aug(lane, ones3 + [-t for t in ch] + ones3)).astype(BF16)
            va_ref[:, cols] = jnp.where(lane < HEAD_DIM, head(2 * ATT_WIDTH), _aug(lane, ones3)).astype(BF16)

    wide = N_HEADS * LANES
    return pl.pallas_call(
        body,
        out_shape=(jax.ShapeDtypeStruct((S, D_MODEL), BF16), jax.ShapeDtypeStruct((S, wide), BF16),
                   jax.ShapeDtypeStruct((S, wide), BF16), jax.ShapeDtypeStruct((S, wide), BF16),
                   jax.ShapeDtypeStruct((S, 2 * GMLP_WIDTH), F32), jax.ShapeDtypeStruct((S, LANES), F32)),
        grid=(S // tm,),
        in_specs=[pl.BlockSpec((tm, D_MODEL), lambda i: (i, 0)), pl.BlockSpec((1, D_MODEL), lambda i: (0, 0)),
                  pl.BlockSpec((D_MODEL, IN_PAD), lambda i: (0, 0)), pl.BlockSpec((1, LANES), lambda i: (0, 0)),
                  pl.BlockSpec((tm, tm), lambda i: (0, 0))],
        out_specs=(pl.BlockSpec((tm, D_MODEL), lambda i: (i, 0)), pl.BlockSpec((tm, wide), lambda i: (i, 0)),
                   pl.BlockSpec((tm, wide), lambda i: (i, 0)), pl.BlockSpec((tm, wide), lambda i: (i, 0)),
                   pl.BlockSpec((tm, 2 * GMLP_WIDTH), lambda i: (i, 0)), pl.BlockSpec((tm, LANES), lambda i: (i, 0))),
        scratch_shapes=[pltpu.VMEM((8, LANES), F32)],
        name="inproj_fwd", compiler_params=_cp(("arbitrary",), VMEM_LIMIT),
    )(x, g_mix, w_pad, bf_pad, tri)


def _group_ones():
    r = lax.broadcasted_iota(jnp.int32, (GMLP_WIDTH, GMLP_WIDTH), 0) // GROUP_DIM
    c = lax.broadcasted_iota(jnp.int32, (GMLP_WIDTH, GMLP_WIDTH), 1) // GROUP_DIM
    return (r == c).astype(BF16)


def _gmlp_mixed(vn_bf, w_ref, bias, n_chunks):
    lane = lax.broadcasted_iota(jnp.int32, (CHUNK, LANES), 1)
    row = lax.broadcasted_iota(jnp.int32, (CHUNK, CHUNK), 0)
    col = lax.broadcasted_iota(jnp.int32, (CHUNK, CHUNK), 1)
    ws = [jnp.where(row >= col, w_ref[g], 0.0).astype(BF16) for g in range(N_GROUPS)]
    rows = []
    for ci in range(n_chunks):
        cols = []
        for pp in range(N_GROUPS // 2):
            v = vn_bf[ci * CHUNK:(ci + 1) * CHUNK, pp * LANES:(pp + 1) * LANES]
            v_lo = jnp.where(lane < GROUP_DIM, v, jnp.zeros_like(v))
            v_hi = jnp.where(lane >= GROUP_DIM, v, jnp.zeros_like(v))
            m = (jnp.dot(ws[2 * pp], v_lo, preferred_element_type=F32)
                 + jnp.dot(ws[2 * pp + 1], v_hi, preferred_element_type=F32))
            cols.append(m + bias[:, pp * LANES:(pp + 1) * LANES])
        rows.append(jnp.concatenate(cols, axis=1))
    return jnp.concatenate(rows, axis=0)


def _gmlp_fwd(ug, gain, w_s, bias_full):
    S = ug.shape[0]
    tm = _pick(S, (512, 256, 128))
    ones = _group_ones()

    def body(ug_ref, gain_ref, w_ref, bias_ref, ones_ref, sg_ref):
        u = _gelu(ug_ref[:, :GMLP_WIDTH])
        vr = _gelu(ug_ref[:, GMLP_WIDTH:])
        ms = _dot3(vr * vr, ones_ref[...]) * (1.0 / GROUP_DIM)
        vn = ((vr * lax.rsqrt(ms + EPS)) * gain_ref[...]).astype(BF16)
        mixed = _gmlp_mixed(vn, w_ref, bias_ref[...], tm // CHUNK)
        sg_ref[...] = (u * mixed).astype(BF16)

    return pl.pallas_call(
        body, out_shape=jax.ShapeDtypeStruct((S, GMLP_WIDTH), BF16), grid=(S // tm,),
        in_specs=[pl.BlockSpec((tm, 2 * GMLP_WIDTH), lambda i: (i, 0)), pl.BlockSpec((1, GMLP_WIDTH), lambda i: (0, 0)),
                  pl.BlockSpec((N_GROUPS, CHUNK, CHUNK), lambda i: (0, 0, 0)),
                  pl.BlockSpec((CHUNK, GMLP_WIDTH), lambda i: (0, 0)),
                  pl.BlockSpec((GMLP_WIDTH, GMLP_WIDTH), lambda i: (0, 0))],
        out_specs=pl.BlockSpec((tm, GMLP_WIDTH), lambda i: (i, 0)),
        name="gmlp_fwd", compiler_params=_cp(("parallel",), VMEM_LIMIT),
    )(ug, gain, w_s, bias_full, ones)


_NT = (((1,), (1,)), ((), ()))
_TN = (((0,), (0,)), ((), ()))


def _attn_fwd(qa, ka, va):
    S = qa.shape[0]
    tq = _pick(S, (512, 256))
    tk = tq
    nq = S // tq

    def body(q_ref, k_ref, v_ref, o_ref, lse_ref, ob_ref):
        qi = pl.program_id(1)
        lane = lax.broadcasted_iota(jnp.int32, (tq, LANES), 1)
        rid = lax.broadcasted_iota(jnp.int32, (tq, tk), 0)
        cid = lax.broadcasted_iota(jnp.int32, (tq, tk), 1)
        qs = [q_ref[:, :LANES], q_ref[:, LANES:]]

        def update(kb, h, m, acc, masked):
            ks = pl.multiple_of(kb * tk, tk)
            cols = slice(h * LANES, (h + 1) * LANES)
            s = lax.dot_general(qs[h], k_ref[pl.ds(ks, tk), cols], _NT, preferred_element_type=F32)
            if masked:
                s = jnp.where(rid >= cid, s, -jnp.inf)
            m_new = jnp.maximum(m, jnp.max(s, axis=-1, keepdims=True))
            p = jnp.exp(s - m_new).astype(BF16)
            acc = jnp.exp(m - m_new) * acc + jnp.dot(p, v_ref[pl.ds(ks, tk), cols], preferred_element_type=F32)
            return m_new, acc

        def step(kb, carry):
            return tuple(update(kb, h, *carry[h], False) for h in range(2))

        one = (jnp.full((tq, 1), -jnp.inf, F32), jnp.zeros((tq, LANES), F32))
        carry = lax.fori_loop(0, qi, step, (one, one))
        outs, lses = [], []
        for h in range(2):
            m, acc = update(qi, h, *carry[h], True)
            l = acc[:, HEAD_DIM:HEAD_DIM + 1]
            outs.append(acc / l)
            lses.append(m + jnp.log(l))
        o = jnp.where(lane < HEAD_DIM, outs[0], pltpu.roll(outs[1], HEAD_DIM, 1))
        o_ref[...] = o
        ob_ref[...] = o.astype(BF16)
        lse_ref[...] = jnp.where(lane < HEAD_DIM, lses[0], lses[1])

    return pl.pallas_call(
        body,
        out_shape=(jax.ShapeDtypeStruct((S, ATT_WIDTH), F32), jax.ShapeDtypeStruct((S, ATT_WIDTH), F32),
                   jax.ShapeDtypeStruct((S, ATT_WIDTH), BF16)),
        grid=(N_PAIRS, nq),
        in_specs=[pl.BlockSpec((tq, 2 * LANES), lambda p, i: (i, p)),
                  pl.BlockSpec((S, 2 * LANES), lambda p, i: (0, p)),
                  pl.BlockSpec((S, 2 * LANES), lambda p, i: (0, p))],
        out_specs=(pl.BlockSpec((tq, LANES), lambda p, i: (i, p)), pl.BlockSpec((tq, LANES), lambda p, i: (i, p)),
                   pl.BlockSpec((tq, LANES), lambda p, i: (i, p))),
        name="attn_fwd", compiler_params=_cp(("parallel", "parallel"), VMEM_LIMIT),
    )(qa, ka, va)


def _rms_fwd(h, g):
    S = h.shape[0]
    tm = _pick(S, (512, 256))

    def body(h_ref, g_ref, o_ref):
        hf = h_ref[...]
        r = lax.rsqrt(jnp.mean(hf * hf, axis=-1, keepdims=True) + EPS)
        o_ref[...] = ((hf * r) * g_ref[...]).astype(BF16)

    return pl.pallas_call(
        body, out_shape=jax.ShapeDtypeStruct(h.shape, BF16), grid=(S // tm,),
        in_specs=[pl.BlockSpec((tm, D_MODEL), lambda i: (i, 0)), pl.BlockSpec((1, D_MODEL), lambda i: (0, 0))],
        out_specs=pl.BlockSpec((tm, D_MODEL), lambda i: (i, 0)),
        name="rms_fwd", compiler_params=_cp(("parallel",)),
    )(h, g)


def _shift_rows(x, prev, n):
    rid = lax.broadcasted_iota(jnp.int32, x.shape, 0)
    y = pltpu.roll(x, n, 0)
    if n == 1:
        return jnp.where(rid == 0, prev[7:8, :], y)
    return jnp.where(rid == 0, prev[6:7, :], jnp.where(rid == 1, prev[7:8, :], y))


def _shift_rows_up(x, nxt, n):
    rows = x.shape[0]
    rid = lax.broadcasted_iota(jnp.int32, x.shape, 0)
    y = pltpu.roll(x, rows - n, 0)
    if n == 1:
        return jnp.where(rid == rows - 1, nxt[0:1, :], y)
    return jnp.where(rid == rows - 2, nxt[0:1, :], jnp.where(rid == rows - 1, nxt[1:2, :], y))


def _conv3(cur, prev, w, b):
    return (w[0:1, :] * _shift_rows(cur, prev, 2) + w[1:2, :] * _shift_rows(cur, prev, 1)
            + w[2:3, :] * cur + b)


def _conv_act_fwd(hu, cw, cb):
    _, S, F = hu.shape
    tm = _pick(S, (512, 256))
    tn = _pick(F, (256, 128))
    r8 = tm // 8

    def body(cur_ref, prev_ref, w_ref, b_ref, o_ref):
        i = pl.program_id(1)
        halves = []
        for h in range(2):
            prev = jnp.where(i > 0, prev_ref[h], 0.0)
            halves.append(_conv3(cur_ref[h], prev, w_ref[h], b_ref[h]))
        a, g = halves
        o_ref[...] = (g * _sigmoid(g) * a).astype(BF16)

    return pl.pallas_call(
        body, out_shape=jax.ShapeDtypeStruct((S, F), BF16), grid=(F // tn, S // tm),
        in_specs=[pl.BlockSpec((2, tm, tn), lambda j, i: (0, i, j)),
                  pl.BlockSpec((2, 8, tn), lambda j, i: (0, jnp.maximum(i * r8 - 1, 0), j)),
                  pl.BlockSpec((2, 8, tn), lambda j, i: (0, 0, j)),
                  pl.BlockSpec((2, 1, tn), lambda j, i: (0, 0, j))],
        out_specs=pl.BlockSpec((tm, tn), lambda j, i: (i, j)),
        name="conv_act_fwd", compiler_params=_cp(("parallel", "parallel"), VMEM_LIMIT),
    )(hu, hu, cw, cb)


def _loss_head(h2, g_final, target):
    S = h2.shape[0]
    tm = _pick(S, (512, 256))

    def body(h_ref, g_ref, t_ref, loss_ref, dh_ref, dhb_ref, dg_ref):
        i = pl.program_id(0)

        @pl.when(i == 0)
        def _():
            loss_ref[...] = jnp.zeros_like(loss_ref)
            dg_ref[...] = jnp.zeros_like(dg_ref)

        hf = h_ref[...]
        g = g_ref[...]
        r = lax.rsqrt(jnp.mean(hf * hf, axis=-1, keepdims=True) + EPS)
        hhat = hf * r
        err = hhat * g - t_ref[...]
        loss_ref[...] += 0.5 * jnp.sum(jnp.mean(err * err, axis=-1, keepdims=True))
        dy = err * (1.0 / D_MODEL)
        dg_ref[0:1, :] += jnp.sum(dy * hhat, axis=0, keepdims=True)
        dhat = dy * g
        dh = r * (dhat - hhat * jnp.mean(dhat * hhat, axis=-1, keepdims=True))
        dh_ref[...] = dh
        dhb_ref[...] = dh.astype(BF16)

    return pl.pallas_call(
        body,
        out_shape=(jax.ShapeDtypeStruct((8, LANES), F32), jax.ShapeDtypeStruct((S, D_MODEL), F32),
                   jax.ShapeDtypeStruct((S, D_MODEL), BF16), jax.ShapeDtypeStruct((8, D_MODEL), F32)),
        grid=(S // tm,),
        in_specs=[pl.BlockSpec((tm, D_MODEL), lambda i: (i, 0)), pl.BlockSpec((1, D_MODEL), lambda i: (0, 0)),
                  pl.BlockSpec((tm, D_MODEL), lambda i: (i, 0))],
        out_specs=(pl.BlockSpec((8, LANES), lambda i: (0, 0)), pl.BlockSpec((tm, D_MODEL), lambda i: (i, 0)),
                   pl.BlockSpec((tm, D_MODEL), lambda i: (i, 0)), pl.BlockSpec((8, D_MODEL), lambda i: (0, 0))),
        name="loss_head", compiler_params=_cp(("arbitrary",), VMEM_LIMIT),
    )(h2, g_final, target)


def _rms_bwd(h, g, dy, res):
    S = h.shape[0]
    tm = _pick(S, (512, 256))

    def body(h_ref, g_ref, dy_ref, r_ref, dh_ref, dhb_ref, dg_ref):
        i = pl.program_id(0)

        @pl.when(i == 0)
        def _():
            dg_ref[...] = jnp.zeros_like(dg_ref)

        hf = h_ref[...]
        dyv = dy_ref[...]
        r = lax.rsqrt(jnp.mean(hf * hf, axis=-1, keepdims=True) + EPS)
        hhat = hf * r
        dg_ref[0:1, :] += jnp.sum(dyv * hhat, axis=0, keepdims=True)
        dhat = dyv * g_ref[...]
        dh = r_ref[...] + r * (dhat - hhat * jnp.mean(dhat * hhat, axis=-1, keepdims=True))
        dh_ref[...] = dh
        dhb_ref[...] = dh.astype(BF16)

    return pl.pallas_call(
        body,
        out_shape=(jax.ShapeDtypeStruct((S, D_MODEL), F32), jax.ShapeDtypeStruct((S, D_MODEL), BF16),
                   jax.ShapeDtypeStruct((8, D_MODEL), F32)),
        grid=(S // tm,),
        in_specs=[pl.BlockSpec((tm, D_MODEL), lambda i: (i, 0)), pl.BlockSpec((1, D_MODEL), lambda i: (0, 0)),
                  pl.BlockSpec((tm, D_MODEL), lambda i: (i, 0)), pl.BlockSpec((tm, D_MODEL), lambda i: (i, 0))],
        out_specs=(pl.BlockSpec((tm, D_MODEL), lambda i: (i, 0)), pl.BlockSpec((tm, D_MODEL), lambda i: (i, 0)),
                   pl.BlockSpec((8, D_MODEL), lambda i: (0, 0))),
        name="rms_bwd", compiler_params=_cp(("arbitrary",), VMEM_LIMIT),
    )(h, g, dy, res)


def _conv_act_bwd(hu, dact, cw, cb):
    _, S, F = hu.shape
    tm = _pick(S, (256,))
    tn = _pick(F, (256, 128))
    r8 = tm // 8
    n_i = S // tm
    last8 = S // 8 - 1

    def body(cur_ref, prev_ref, next_ref, da_ref, dan_ref, w_ref, b_ref, dhu_ref, dcw_ref):
        i = pl.program_id(1)

        @pl.when(i == 0)
        def _():
            dcw_ref[...] = jnp.zeros_like(dcw_ref)

        rid8 = lax.broadcasted_iota(jnp.int32, (8, tn), 0)
        cur = [cur_ref[0], cur_ref[1]]
        prev = [jnp.where(i > 0, prev_ref[h], 0.0) for h in range(2)]
        nxt = [next_ref[0], next_ref[1]]
        w = [w_ref[0], w_ref[1]]

        def gate_grads(a, g, d):
            sg = _sigmoid(g)
            return d * (g * sg), d * a * (sg * (1.0 + g * (1.0 - sg)))

        a = _conv3(cur[0], prev[0], w[0], b_ref[0])
        g = _conv3(cur[1], prev[1], w[1], b_ref[1])
        dhc = gate_grads(a, g, da_ref[...])
        a_n = _conv3(nxt[0], cur[0][tm - 8:, :], w[0], b_ref[0])
        g_n = _conv3(nxt[1], cur[1][tm - 8:, :], w[1], b_ref[1])
        dhc_n = gate_grads(a_n, g_n, dan_ref[...])
        for h in range(2):
            d = dhc[h]
            dn = jnp.where(i < n_i - 1, dhc_n[h], 0.0)
            dhu = (w[h][2:3, :] * d + w[h][1:2, :] * _shift_rows_up(d, dn, 1)
                   + w[h][0:1, :] * _shift_rows_up(d, dn, 2))
            dhu_ref[h] = dhu.astype(BF16)
            t0 = jnp.sum(d * _shift_rows(cur[h], prev[h], 2), axis=0, keepdims=True)
            t1 = jnp.sum(d * _shift_rows(cur[h], prev[h], 1), axis=0, keepdims=True)
            t2 = jnp.sum(d * cur[h], axis=0, keepdims=True)
            t3 = jnp.sum(d, axis=0, keepdims=True)
            dcw_ref[h] += jnp.where(rid8 == 0, t0, jnp.where(rid8 == 1, t1, jnp.where(rid8 == 2, t2, jnp.where(rid8 == 3, t3, 0.0))))

    return pl.pallas_call(
        body,
        out_shape=(jax.ShapeDtypeStruct((2, S, F), BF16), jax.ShapeDtypeStruct((2, 8, F), F32)),
        grid=(F // tn, n_i),
        in_specs=[pl.BlockSpec((2, tm, tn), lambda j, i: (0, i, j)),
                  pl.BlockSpec((2, 8, tn), lambda j, i: (0, jnp.maximum(i * r8 - 1, 0), j)),
                  pl.BlockSpec((2, 8, tn), lambda j, i: (0, jnp.minimum((i + 1) * r8, last8), j)),
                  pl.BlockSpec((tm, tn), lambda j, i: (i, j)),
                  pl.BlockSpec((8, tn), lambda j, i: (jnp.minimum((i + 1) * r8, last8), j)),
                  pl.BlockSpec((2, 8, tn), lambda j, i: (0, 0, j)),
                  pl.BlockSpec((2, 1, tn), lambda j, i: (0, 0, j))],
        out_specs=(pl.BlockSpec((2, tm, tn), lambda j, i: (0, i, j)), pl.BlockSpec((2, 8, tn), lambda j, i: (0, 0, j))),
        name="conv_act_bwd", compiler_params=_cp(("parallel", "arbitrary"), VMEM_LIMIT),
    )(hu, hu, hu, dact, dact, cw, cb)


def _attn_prep(att, lse, dmix, qa):
    S = att.shape[0]
    tm = _pick(S, (512, 256))

    def body(o_ref, lse_ref, do_ref, q_ref, qb_ref, doa_ref):
        lane = lax.broadcasted_iota(jnp.int32, (tm, LANES), 1)
        do = do_ref[...]
        prod = o_ref[...] * do
        for hh in range(2):
            sel = (lane >= HEAD_DIM) if hh else (lane < HEAD_DIM)
            delta = jnp.sum(jnp.where(sel, prod, 0.0), axis=-1, keepdims=True)
            dod = pltpu.roll(do, HEAD_DIM, 1) if hh else do
            cols = slice(hh * LANES, (hh + 1) * LANES)
            doa_ref[:, cols] = jnp.where(lane < HEAD_DIM, dod, _aug(lane, _split3f(-delta))).astype(BF16)
            l3 = _split3f(-lse_ref[:, hh * HEAD_DIM:hh * HEAD_DIM + 1])
            augl = _aug(lane, [0.0] * 6 + l3).astype(BF16)
            qb_ref[:, cols] = jnp.where((lane >= HEAD_DIM + 6) & (lane < HEAD_DIM + 9), augl, q_ref[:, cols])

    return pl.pallas_call(
        body,
        out_shape=(jax.ShapeDtypeStruct(qa.shape, BF16), jax.ShapeDtypeStruct(qa.shape, BF16)),
        grid=(S // tm, N_PAIRS),
        in_specs=[pl.BlockSpec((tm, LANES), lambda i, p: (i, p)), pl.BlockSpec((tm, LANES), lambda i, p: (i, p)),
                  pl.BlockSpec((tm, LANES), lambda i, p: (i, p)), pl.BlockSpec((tm, 2 * LANES), lambda i, p: (i, p))],
        out_specs=(pl.BlockSpec((tm, 2 * LANES), lambda i, p: (i, p)), pl.BlockSpec((tm, 2 * LANES), lambda i, p: (i, p))),
        name="attn_prep", compiler_params=_cp(("parallel", "parallel")),
    )(att, lse, dmix, qa)


def _attn_bwd(qb, ka, va, doa):
    S = qb.shape[0]
    tk = _pick(S, (512, 256))
    tq = tk
    nq = S // tq

    def pair(a, scale=None):
        lane = lax.broadcasted_iota(jnp.int32, (a.shape[0], LANES), 1)
        out = jnp.where(lane < HEAD_DIM, a[:, :LANES], pltpu.roll(a[:, LANES:], HEAD_DIM, 1))
        return out if scale is None else out * scale

    def lanes01(a, col, sign):
        lane = lax.broadcasted_iota(jnp.int32, (a.shape[0], LANES), 1)
        return jnp.where(lane == 0, sign * a[:, col:col + 1], jnp.where(lane == 1, sign * a[:, LANES + col:LANES + col + 1], 0.0))

    def body(q_ref, do_ref, k_ref, v_ref, dqc_ref, dkc_ref, dvc_ref, dcq_ref, dck_ref, dq_ref, dka_ref, dva_ref):
        kb = pl.program_id(1)

        @pl.when(kb == 0)
        def _():
            dq_ref[...] = jnp.zeros_like(dq_ref)

        dka_ref[...] = jnp.zeros_like(dka_ref)
        dva_ref[...] = jnp.zeros_like(dva_ref)
        rid = lax.broadcasted_iota(jnp.int32, (tk, tq), 0)
        cid = lax.broadcasted_iota(jnp.int32, (tk, tq), 1)

        def tile(qi, masked):
            qs = pl.multiple_of(qi * tq, tq)
            for h in range(2):
                cols = slice(h * LANES, (h + 1) * LANES)
                qblk = q_ref[pl.ds(qs, tq), cols]
                doblk = do_ref[pl.ds(qs, tq), cols]
                kh = k_ref[:, cols]
                p = jnp.exp(lax.dot_general(kh, qblk, _NT, preferred_element_type=F32))
                if masked:
                    p = jnp.where(cid >= rid, p, 0.0)
                ds = (p * lax.dot_general(v_ref[:, cols], doblk, _NT, preferred_element_type=F32)).astype(BF16)
                dva_ref[:, cols] += jnp.dot(p.astype(BF16), doblk, preferred_element_type=F32)
                dka_ref[:, cols] += jnp.dot(ds, qblk, preferred_element_type=F32)
                dq_ref[pl.ds(qs, tq), cols] += lax.dot_general(ds, kh, _TN, preferred_element_type=F32)

        tile(kb, True)

        def step(qi, carry):
            tile(qi, False)
            return carry

        lax.fori_loop(kb + 1, nq, step, 0)
        dka = dka_ref[...]
        dkc_ref[...] = pair(dka).astype(BF16)
        dvc_ref[...] = pair(dva_ref[...]).astype(BF16)
        dck_ref[...] = lanes01(dka, HEAD_DIM + 3, -1.0)

        @pl.when(kb == nq - 1)
        def _():
            dqa = dq_ref[...]
            dqc_ref[...] = pair(dqa, HEAD_DIM ** -0.5).astype(BF16)
            dcq_ref[...] = lanes01(dqa, HEAD_DIM, 1.0)

    wide = 2 * LANES
    half = jax.ShapeDtypeStruct((S, ATT_WIDTH), BF16)
    slabs = jax.ShapeDtypeStruct((N_PAIRS, S, LANES), F32)
    return pl.pallas_call(
        body,
        out_shape=(half, half, half, slabs, slabs),
        grid=(N_PAIRS, nq),
        in_specs=[pl.BlockSpec((S, wide), lambda p, j: (0, p)), pl.BlockSpec((S, wide), lambda p, j: (0, p)),
                  pl.BlockSpec((tk, wide), lambda p, j: (j, p)), pl.BlockSpec((tk, wide), lambda p, j: (j, p))],
        out_specs=(pl.BlockSpec((S, LANES), lambda p, j: (0, p)), pl.BlockSpec((tk, LANES), lambda p, j: (j, p)),
                   pl.BlockSpec((tk, LANES), lambda p, j: (j, p)), pl.BlockSpec((None, S, LANES), lambda p, j: (p, 0, 0)),
                   pl.BlockSpec((None, tk, LANES), lambda p, j: (p, j, 0))),
        scratch_shapes=[pltpu.VMEM((S, wide), F32), pltpu.VMEM((tk, wide), F32), pltpu.VMEM((tk, wide), F32)],
        name="attn_bwd", compiler_params=_cp(("parallel", "arbitrary"), VMEM_LIMIT),
    )(qb, doa, ka, va)


def _attn_delta_old(o, do):
    S = o.shape[0]
    tm = _pick(S, (512, 256))
    ones = _group_ones()

    def body(o_ref, do_ref, ones_ref, d_ref):
        d_ref[...] = _dot3(o_ref[...] * do_ref[...], ones_ref[...])

    return pl.pallas_call(
        body, out_shape=jax.ShapeDtypeStruct((S, ATT_WIDTH), F32), grid=(S // tm,),
        in_specs=[pl.BlockSpec((tm, ATT_WIDTH), lambda i: (i, 0)), pl.BlockSpec((tm, ATT_WIDTH), lambda i: (i, 0)),
                  pl.BlockSpec((ATT_WIDTH, ATT_WIDTH), lambda i: (0, 0))],
        out_specs=pl.BlockSpec((tm, ATT_WIDTH), lambda i: (i, 0)),
        name="attn_delta", compiler_params=_cp(("parallel",)),
    )(o, do, ones)


def _attn_bwd_old(qkv, do_bf, c_cols, c_rows, lse_rows, dl_rows):
    S = qkv.shape[0]
    tk = _pick(S, (256,))
    tq = tk
    nq = S // tq
    nt = (((1,), (1,)), ((), ()))
    tn_dims = (((0,), (0,)), ((), ()))

    def body(q_ref, do_ref, k_ref, v_ref, cc_ref, cr_ref, lse_ref, dl_ref, dq_ref, dk_ref, dv_ref, dcs_ref, dcq_ref):
        kb = pl.program_id(1)

        @pl.when(kb == 0)
        def _():
            dq_ref[...] = jnp.zeros_like(dq_ref)
            dcq_ref[...] = jnp.zeros_like(dcq_ref)

        lane = lax.broadcasted_iota(jnp.int32, (tk, LANES), 1)
        rid = lax.broadcasted_iota(jnp.int32, (tk, tq), 0)
        cid = lax.broadcasted_iota(jnp.int32, (tk, tq), 1)
        k = k_ref[...]
        v = v_ref[...]
        dks, dvs, dcs = [], [], []
        for hh in range(2):
            sel = (lane >= GROUP_DIM) if hh else (lane < GROUP_DIM)
            km = jnp.where(sel, k, jnp.zeros_like(k))
            vm = jnp.where(sel, v, jnp.zeros_like(v))
            cs = cc_ref[:, hh * HEAD_DIM:hh * HEAD_DIM + 1]

            def step(qb, carry, km=km, vm=vm, cs=cs, hh=hh):
                dk_acc, dv_acc, dc_acc = carry
                qs = pl.multiple_of(qb * tq, tq)
                qblk = q_ref[pl.ds(qs, tq), :]
                doblk = do_ref[pl.ds(qs, tq), :]
                s = lax.dot_general(km, qblk, nt, preferred_element_type=F32)
                s = s + (cr_ref[hh:hh + 1, pl.ds(qs, tq)] - cs)
                p = jnp.exp(s - lse_ref[hh:hh + 1, pl.ds(qs, tq)])
                p = jnp.where((qb > kb) | (cid >= rid), p, 0.0)
                dp = lax.dot_general(vm, doblk, nt, preferred_element_type=F32)
                ds = p * (dp - dl_ref[hh:hh + 1, pl.ds(qs, tq)])
                ds_bf = ds.astype(BF16)
                dv_acc = dv_acc + jnp.dot(p.astype(BF16), doblk, preferred_element_type=F32)
                dk_acc = dk_acc + jnp.dot(ds_bf, qblk, preferred_element_type=F32)
                dc_acc = dc_acc + jnp.sum(ds, axis=-1, keepdims=True)
                dq_ref[pl.ds(qs, tq), :] += lax.dot_general(ds_bf, km, tn_dims, preferred_element_type=F32)
                dcq_ref[hh:hh + 1, pl.ds(qs, tq)] += jnp.sum(ds, axis=0, keepdims=True)
                return dk_acc, dv_acc, dc_acc

            init = (jnp.zeros((tk, LANES), F32), jnp.zeros((tk, LANES), F32), jnp.zeros((tk, 1), F32))
            dk_acc, dv_acc, dc_acc = lax.fori_loop(kb, nq, step, init)
            dks.append(dk_acc)
            dvs.append(dv_acc)
            dcs.append(dc_acc)
        dk_ref[...] = jnp.where(lane < GROUP_DIM, dks[0], dks[1])
        dv_ref[...] = jnp.where(lane < GROUP_DIM, dvs[0], dvs[1])
        dcs_ref[...] = jnp.where(lane == 0, -dcs[0], jnp.where(lane == 1, -dcs[1], 0.0))

    return pl.pallas_call(
        body,
        out_shape=(jax.ShapeDtypeStruct((S, ATT_WIDTH), F32), jax.ShapeDtypeStruct((S, ATT_WIDTH), F32),
                   jax.ShapeDtypeStruct((S, ATT_WIDTH), F32), jax.ShapeDtypeStruct((N_PAIRS, S, LANES), F32),
                   jax.ShapeDtypeStruct((N_PAIRS, 8, S), F32)),
        grid=(N_PAIRS, nq),
        in_specs=[pl.BlockSpec((S, LANES), lambda p, j: (0, p)),
                  pl.BlockSpec((S, LANES), lambda p, j: (0, p)),
                  pl.BlockSpec((tk, LANES), lambda p, j: (j, N_PAIRS + p)),
                  pl.BlockSpec((tk, LANES), lambda p, j: (j, 2 * N_PAIRS + p)),
                  pl.BlockSpec((None, tk, LANES), lambda p, j: (p, j, 0)),
                  pl.BlockSpec((None, 8, S), lambda p, j: (p, 0, 0)),
                  pl.BlockSpec((None, 8, S), lambda p, j: (p, 0, 0)),
                  pl.BlockSpec((None, 8, S), lambda p, j: (p, 0, 0))],
        out_specs=(pl.BlockSpec((S, LANES), lambda p, j: (0, p)),
                   pl.BlockSpec((tk, LANES), lambda p, j: (j, p)),
                   pl.BlockSpec((tk, LANES), lambda p, j: (j, p)),
                   pl.BlockSpec((None, tk, LANES), lambda p, j: (p, j, 0)),
                   pl.BlockSpec((None, 8, S), lambda p, j: (p, 0, 0))),
        name="attn_bwd", compiler_params=_cp(("parallel", "arbitrary"), VMEM_LIMIT),
    )(qkv, do_bf, qkv, qkv, c_cols, c_rows, lse_rows, dl_rows)


def _gmlp_bwd(ug, dsg, gain, w_s, wt_s, bias_full):
    S = ug.shape[0]
    tm = _pick(S, (512, 256, 128))
    n_chunks = tm // CHUNK
    n_i = S // tm
    ones = _group_ones()
    nt = (((1,), (1,)), ((), ()))

    def body(ug_ref, dsg_ref, gain_ref, w_ref, wt_ref, bias_ref, ones_ref, dug_ref, dw_ref, dgain_ref, dbias_ref,
             dbacc_ref):
        i = pl.program_id(0)

        @pl.when(i == 0)
        def _():
            dw_ref[...] = jnp.zeros_like(dw_ref)
            dgain_ref[...] = jnp.zeros_like(dgain_ref)
            dbacc_ref[...] = jnp.zeros_like(dbacc_ref)

        ones_m = ones_ref[...]
        pu = ug_ref[:, :GMLP_WIDTH]
        pg = ug_ref[:, GMLP_WIDTH:]
        u = _gelu(pu)
        vr = _gelu(pg)
        ms = _dot3(vr * vr, ones_m) * (1.0 / GROUP_DIM)
        rinv = lax.rsqrt(ms + EPS)
        vhat = vr * rinv
        gain_v = gain_ref[...]
        vn = (vhat * gain_v).astype(BF16)
        mixed = _gmlp_mixed(vn, w_ref, bias_ref[...], n_chunks)
        dsg_v = dsg_ref[...]
        du = dsg_v * mixed
        dmixed = dsg_v * u
        dm_bf = dmixed.astype(BF16)
        lane = lax.broadcasted_iota(jnp.int32, (CHUNK, LANES), 1)
        row = lax.broadcasted_iota(jnp.int32, (CHUNK, CHUNK), 0)
        col = lax.broadcasted_iota(jnp.int32, (CHUNK, CHUNK), 1)
        wts = [jnp.where(col >= row, wt_ref[g], 0.0).astype(BF16) for g in range(N_GROUPS)]
        dvn_rows = []
        dbsum = jnp.zeros((CHUNK, GMLP_WIDTH), F32)
        for ci in range(n_chunks):
            rs = slice(ci * CHUNK, (ci + 1) * CHUNK)
            dbsum = dbsum + dmixed[rs, :]
            cols = []
            for pp in range(N_GROUPS // 2):
                cs = slice(pp * LANES, (pp + 1) * LANES)
                dm = dm_bf[rs, cs]
                dm_lo = jnp.where(lane < GROUP_DIM, dm, jnp.zeros_like(dm))
                dm_hi = jnp.where(lane >= GROUP_DIM, dm, jnp.zeros_like(dm))
                vb = vn[rs, cs]
                dw_ref[2 * pp] += lax.dot_general(dm_lo, vb, nt, preferred_element_type=F32)
                dw_ref[2 * pp + 1] += lax.dot_general(dm_hi, vb, nt, preferred_element_type=F32)
                cols.append(jnp.dot(wts[2 * pp], dm_lo, preferred_element_type=F32)
                            + jnp.dot(wts[2 * pp + 1], dm_hi, preferred_element_type=F32))
            dvn_rows.append(jnp.concatenate(cols, axis=1))
        dvn = jnp.concatenate(dvn_rows, axis=0)
        dbacc_ref[...] += dbsum
        dgain_ref[0:1, :] += jnp.sum(dvn * vhat, axis=0, keepdims=True)
        dvhat = dvn * gain_v
        gm = _dot3(dvhat * vhat, ones_m) * (1.0 / GROUP_DIM)
        dvr = rinv * (dvhat - vhat * gm)
        dug_ref[:, :GMLP_WIDTH] = (du * _gelu_grad(pu)).astype(BF16)
        dug_ref[:, GMLP_WIDTH:] = (dvr * _gelu_grad(pg)).astype(BF16)

        @pl.when(i == n_i - 1)
        def _():
            for g in range(N_GROUPS):
                dw_ref[g] = jnp.where(row >= col, dw_ref[g], 0.0)
            dbias_ref[...] = _dot3(dbacc_ref[...], ones_m)

    return pl.pallas_call(
        body,
        out_shape=(jax.ShapeDtypeStruct((S, 2 * GMLP_WIDTH), BF16), jax.ShapeDtypeStruct((N_GROUPS, CHUNK, CHUNK), F32),
                   jax.ShapeDtypeStruct((8, GMLP_WIDTH), F32), jax.ShapeDtypeStruct((CHUNK, GMLP_WIDTH), F32)),
        grid=(n_i,),
        in_specs=[pl.BlockSpec((tm, 2 * GMLP_WIDTH), lambda i: (i, 0)), pl.BlockSpec((tm, GMLP_WIDTH), lambda i: (i, 1)),
                  pl.BlockSpec((1, GMLP_WIDTH), lambda i: (0, 0)),
                  pl.BlockSpec((N_GROUPS, CHUNK, CHUNK), lambda i: (0, 0, 0)),
                  pl.BlockSpec((N_GROUPS, CHUNK, CHUNK), lambda i: (0, 0, 0)),
                  pl.BlockSpec((CHUNK, GMLP_WIDTH), lambda i: (0, 0)),
                  pl.BlockSpec((GMLP_WIDTH, GMLP_WIDTH), lambda i: (0, 0))],
        out_specs=(pl.BlockSpec((tm, 2 * GMLP_WIDTH), lambda i: (i, 0)),
                   pl.BlockSpec((N_GROUPS, CHUNK, CHUNK), lambda i: (0, 0, 0)),
                   pl.BlockSpec((8, GMLP_WIDTH), lambda i: (0, 0)),
                   pl.BlockSpec((CHUNK, GMLP_WIDTH), lambda i: (0, 0))),
        scratch_shapes=[pltpu.VMEM((CHUNK, GMLP_WIDTH), F32)],
        name="gmlp_bwd", compiler_params=_cp(("arbitrary",), VMEM_LIMIT),
    )(ug, dsg, gain, w_s, wt_s, bias_full, ones)


def _gate_bwd(dcq, dck, zf):
    S = zf.shape[0]
    tm = _pick(S, (256,))
    n_i = S // tm
    triu = (lax.broadcasted_iota(jnp.int32, (tm, tm), 0) <= lax.broadcasted_iota(jnp.int32, (tm, tm), 1)).astype(BF16)

    def body(dcq_ref, dck_ref, zf_ref, tri_ref, dzf_ref, dbf_ref, carry_ref):
        i = pl.program_id(0)

        @pl.when(i == 0)
        def _():
            carry_ref[...] = jnp.zeros_like(carry_ref)
            dbf_ref[...] = jnp.zeros_like(dbf_ref)

        lane = lax.broadcasted_iota(jnp.int32, (tm, LANES), 1)
        dc = jnp.zeros((tm, LANES), F32)
        for p in range(N_PAIRS):
            slab = dcq_ref[p] + dck_ref[p]
            for hh in range(2):
                dc = dc + jnp.where(lane == 2 * p + hh, slab[:, hh:hh + 1], 0.0)
        dlf = _dot3l(tri_ref[...], dc) + carry_ref[0:1, :]
        carry_ref[0:1, :] = dlf[0:1, :]
        dz = jnp.where(lane < N_HEADS, dlf * _sigmoid(-zf_ref[...]), 0.0)
        dzf_ref[...] = dz.astype(BF16)
        dbf_ref[0:1, :] += jnp.sum(dz, axis=0, keepdims=True)

    return pl.pallas_call(
        body,
        out_shape=(jax.ShapeDtypeStruct((S, LANES), BF16), jax.ShapeDtypeStruct((8, LANES), F32)),
        grid=(n_i,),
        in_specs=[pl.BlockSpec((N_PAIRS, tm, LANES), lambda i: (0, n_i - 1 - i, 0)),
                  pl.BlockSpec((N_PAIRS, tm, LANES), lambda i: (0, n_i - 1 - i, 0)),
                  pl.BlockSpec((tm, LANES), lambda i: (n_i - 1 - i, 0)),
                  pl.BlockSpec((tm, tm), lambda i: (0, 0))],
        out_specs=(pl.BlockSpec((tm, LANES), lambda i: (n_i - 1 - i, 0)), pl.BlockSpec((8, LANES), lambda i: (0, 0))),
        scratch_shapes=[pltpu.VMEM((8, LANES), F32)],
        name="gate_bwd", compiler_params=_cp(("arbitrary",), VMEM_LIMIT),
    )(dcq, dck, zf, triu)


def _out_proj_fwd(att_bf, sg, w_out_bf, x, g_ffn):
    S = x.shape[0]
    tm = _pick(S, (512, 256))

    def body(a_ref, s_ref, w_ref, x_ref, g_ref, h_ref, hn_ref):
        h = (x_ref[...] + jnp.dot(a_ref[...], w_ref[:ATT_WIDTH, :], preferred_element_type=F32)
             + jnp.dot(s_ref[...], w_ref[ATT_WIDTH:, :], preferred_element_type=F32))
        h_ref[...] = h
        r = lax.rsqrt(jnp.mean(h * h, axis=-1, keepdims=True) + EPS)
        hn_ref[...] = ((h * r) * g_ref[...]).astype(BF16)

    row = pl.BlockSpec((tm, D_MODEL), lambda i: (i, 0))
    half = pl.BlockSpec((tm, ATT_WIDTH), lambda i: (i, 0))
    return pl.pallas_call(
        body, out_shape=(jax.ShapeDtypeStruct((S, D_MODEL), F32), jax.ShapeDtypeStruct((S, D_MODEL), BF16)),
        grid=(S // tm,),
        in_specs=[half, half, pl.BlockSpec((D_MODEL, D_MODEL), lambda i: (0, 0)), row,
                  pl.BlockSpec((1, D_MODEL), lambda i: (0, 0))],
        out_specs=(row, row), name="out_proj", compiler_params=_cp(("parallel",), VMEM_LIMIT),
    )(att_bf, sg, w_out_bf, x, g_ffn)


_IN_PIECES = ((0, ATT_WIDTH), (ATT_WIDTH, ATT_WIDTH), (2 * ATT_WIDTH, ATT_WIDTH), (QKV, 2 * GMLP_WIDTH), (UG_END, LANES))


def _inproj_bwd_dx(pieces, w_pad, x, g_mix, dh1):
    S = x.shape[0]
    tm = _pick(S, (512, 256))

    def body(*refs):
        p_refs, (w_ref, x_ref, g_ref, r_ref, dx_ref, dg_ref) = refs[:5], refs[5:]
        i = pl.program_id(0)

        @pl.when(i == 0)
        def _():
            dg_ref[...] = jnp.zeros_like(dg_ref)

        dxn = None
        for p_ref, (c0, width) in zip(p_refs, _IN_PIECES):
            part = lax.dot_general(p_ref[...], w_ref[:, c0:c0 + width], _NT, preferred_element_type=F32)
            dxn = part if dxn is None else dxn + part
        xf = x_ref[...]
        r = lax.rsqrt(jnp.mean(xf * xf, axis=-1, keepdims=True) + EPS)
        xhat = xf * r
        dg_ref[0:1, :] += jnp.sum(dxn * xhat, axis=0, keepdims=True)
        dhat = dxn * g_ref[...]
        dx_ref[...] = r_ref[...] + r * (dhat - xhat * jnp.mean(dhat * xhat, axis=-1, keepdims=True))

    row = pl.BlockSpec((tm, D_MODEL), lambda i: (i, 0))
    return pl.pallas_call(
        body, out_shape=(jax.ShapeDtypeStruct((S, D_MODEL), F32), jax.ShapeDtypeStruct((8, D_MODEL), F32)),
        grid=(S // tm,),
        in_specs=[pl.BlockSpec((tm, width), lambda i: (i, 0)) for _, width in _IN_PIECES]
        + [pl.BlockSpec((D_MODEL, IN_PAD), lambda i: (0, 0)), row, pl.BlockSpec((1, D_MODEL), lambda i: (0, 0)), row],
        out_specs=(row, pl.BlockSpec((8, D_MODEL), lambda i: (0, 0))),
        name="in_proj_dx", compiler_params=_cp(("arbitrary",), VMEM_LIMIT),
    )(*pieces, w_pad, x, g_mix, dh1)


def _inproj_bwd_dw(xn, pieces):
    S = xn.shape[0]
    tk = _pick(S, (512, 256))

    def body(*refs):
        x_ref, p_refs, o_ref = refs[0], refs[1:6], refs[6]
        k = pl.program_id(0)

        @pl.when(k == 0)
        def _():
            o_ref[...] = jnp.zeros_like(o_ref)

        xb = x_ref[...]
        for p_ref, (c0, width) in zip(p_refs, _IN_PIECES):
            o_ref[:, c0:c0 + width] += lax.dot_general(xb, p_ref[...], _TN, preferred_element_type=F32)

    return pl.pallas_call(
        body, out_shape=jax.ShapeDtypeStruct((D_MODEL, IN_PAD), F32), grid=(S // tk,),
        in_specs=[pl.BlockSpec((tk, D_MODEL), lambda k: (k, 0))]
        + [pl.BlockSpec((tk, width), lambda k: (k, 0)) for _, width in _IN_PIECES],
        out_specs=pl.BlockSpec((D_MODEL, IN_PAD), lambda k: (0, 0)),
        name="in_proj_dw", compiler_params=_cp(("arbitrary",), VMEM_LIMIT),
    )(xn, *pieces)


def _adamw(w, m, v, parts, name):
    R, C = w.shape
    tr = R
    for cand in (256, 128, 64, 32, 16, 8):
        if R % cand == 0 and R > cand:
            tr = cand
            break
    c1 = 1.0 / (1.0 - ADAM_B1 ** ADAM_STEP)
    c2 = 1.0 / (1.0 - ADAM_B2 ** ADAM_STEP)

    def body(w_ref, m_ref, v_ref, p_ref, g_ref, d_ref, nm_ref, nv_ref):
        g = p_ref[0]
        for j in range(1, N_DEV):
            g = g + p_ref[j]
        g_ref[...] = g
        nm = ADAM_B1 * m_ref[...] + (1.0 - ADAM_B1) * g
        nv = ADAM_B2 * v_ref[...] + (1.0 - ADAM_B2) * (g * g)
        nm_ref[...] = nm
        nv_ref[...] = nv
        d_ref[...] = -ADAM_LR * ((nm * c1) / (jnp.sqrt(nv * c2) + ADAM_EPS) + ADAM_WD * w_ref[...])

    spec = pl.BlockSpec((tr, C), lambda i: (i, 0))
    shp = jax.ShapeDtypeStruct((R, C), F32)
    return pl.pallas_call(
        body, out_shape=(shp, shp, shp, shp), grid=(R // tr,),
        in_specs=[spec, spec, spec, pl.BlockSpec((N_DEV, tr, C), lambda i: (0, i, 0))],
        out_specs=(spec, spec, spec, spec),
        name=name, compiler_params=_cp(("parallel",), VMEM_LIMIT),
    )(w, m, v, parts)


def _place():
    x, y, c = lax.axis_index("x"), lax.axis_index("y"), lax.axis_index("c")
    return x, y, c


def _all_gather(blocks, name):
    n = len(blocks)

    def body(*refs):
        ins, outs = refs[:n], refs[n:2 * n]
        send_sems, recv_sems, local_sems = refs[2 * n:]
        x, y, c = _place()
        me, sibling = (x, y, c), (x, y, 1 - c)
        chips = [(1 - x, y), (x, 1 - y), (1 - x, 1 - y)]
        sends = []
        for a in range(n):
            out = outs[a]

            def slot(px, py, pc, out=out):
                return out.at[4 * px + 2 * py + pc]

            def copy(k, block, to, src=None, a=a, slot=slot):
                return pltpu.make_async_remote_copy(
                    src_ref=slot(*block) if src is None else src, dst_ref=slot(*block),
                    send_sem=send_sems.at[a, k], recv_sem=recv_sems.at[a, k], device_id=to, device_id_type=MESH)

            mine = pltpu.make_async_copy(ins[a], slot(*me), local_sems.at[a])
            mine.start()
            first = [copy(0, me, sibling, src=ins[a])]
            first += [copy(1 + j, me, (*chip, c), src=ins[a]) for j, chip in enumerate(chips)]
            for cp in first:
                cp.start()
            sends.append((mine, first, copy))
        for a in range(n):
            mine, first, copy = sends[a]
            passed = [copy(4 + j, (*chip, c), sibling) for j, chip in enumerate(chips)]
            for j, chip in enumerate(chips):
                copy(1 + j, (*chip, c), me).wait_recv()
                passed[j].start()
            copy(0, sibling, me).wait_recv()
            for j, chip in enumerate(chips):
                copy(4 + j, (*chip, 1 - c), me).wait_recv()
            for cp in first + passed:
                cp.wait_send()
            mine.wait()

    any_spec = pl.BlockSpec(memory_space=pl.ANY)
    return pl.pallas_call(
        body, out_shape=tuple(jax.ShapeDtypeStruct((N_DEV,) + b.shape, b.dtype) for b in blocks),
        in_specs=[any_spec] * n, out_specs=tuple([any_spec] * n),
        scratch_shapes=[pltpu.SemaphoreType.DMA((n, 7)), pltpu.SemaphoreType.DMA((n, 7)), pltpu.SemaphoreType.DMA((n,))],
        name=name,
    )(*blocks)


def _exchange_shards(parts, name):
    n = len(parts)

    def body(*refs):
        ins, outs = refs[:n], refs[n:2 * n]
        send_sems, recv_sems, local_sems = refs[2 * n:]
        x, y, c = _place()
        me = 4 * x + 2 * y + c
        started = []
        for a in range(n):
            mine = pltpu.make_async_copy(ins[a].at[me], outs[a].at[me], local_sems.at[a])
            mine.start()
            started.append(mine)
            for k in range(1, N_DEV):
                px, py, pc = x ^ ((k >> 2) & 1), y ^ ((k >> 1) & 1), c ^ (k & 1)
                cp = pltpu.make_async_remote_copy(
                    src_ref=ins[a].at[4 * px + 2 * py + pc], dst_ref=outs[a].at[me],
                    send_sem=send_sems.at[a, k - 1], recv_sem=recv_sems.at[a, k - 1],
                    device_id=(px, py, pc), device_id_type=MESH)
                cp.start()
                started.append(cp)
        for cp in started:
            cp.wait()

    any_spec = pl.BlockSpec(memory_space=pl.ANY)
    return pl.pallas_call(
        body, out_shape=tuple(jax.ShapeDtypeStruct(p.shape, p.dtype) for p in parts),
        in_specs=[any_spec] * n, out_specs=tuple([any_spec] * n),
        scratch_shapes=[pltpu.SemaphoreType.DMA((n, 7)), pltpu.SemaphoreType.DMA((n, 7)), pltpu.SemaphoreType.DMA((n,))],
        name=name,
    )(*parts)


def _rows128(a):
    flat = a.reshape(-1)
    rows = -(-flat.shape[0] // LANES)
    rows = -(-rows // 8) * 8
    return jnp.pad(flat, (0, rows * LANES - flat.shape[0])).reshape(rows, LANES)


def _local_step(x, target, norm_mix_g, w_in_bf, b_forget, gmlp_norm_g, w_spatial, b_spatial, w_out_bf, norm_ffn_g,
                w_up_bf, conv_w, conv_b, w_down_bf, norm_final_g):
    S = x.shape[0]
    f = D_FF
    w_pad = jnp.pad(w_in_bf, ((0, 0), (0, IN_PAD - IN_COLS)))
    bf_pad = jnp.pad(b_forget.reshape(1, N_HEADS), ((0, 0), (0, LANES - N_HEADS)))
    xn, qa, ka, va, ug, zf = _inproj_fwd(x, norm_mix_g.reshape(1, D_MODEL), w_pad, bf_pad)
    bias_full = jnp.repeat(b_spatial.reshape(N_GROUPS, CHUNK).T, GROUP_DIM, axis=1)
    w_s = w_spatial.reshape(N_GROUPS, CHUNK, CHUNK)
    gain = gmlp_norm_g.reshape(1, GMLP_WIDTH)
    sg = _gmlp_fwd(ug, gain, w_s, bias_full)
    att, lse, att_bf = _attn_fwd(qa, ka, va)
    g_ffn = norm_ffn_g.reshape(1, D_MODEL)
    h1, hn = _out_proj_fwd(att_bf, sg, w_out_bf, x, g_ffn)
    hu = _mm(hn, w_up_bf, mode="nn", out_dtype=F32, tm=512, tn=1408, tk=1024, out_halves=True, outer="j", name="ffn_up")
    cw = jnp.pad(conv_w.reshape(3, 2, f).transpose(1, 0, 2), ((0, 0), (0, 5), (0, 0)))
    cb = conv_b.reshape(2, 1, f)
    act = _conv_act_fwd(hu, cw, cb)
    h2 = _mm(act, w_down_bf, mode="nn", out_dtype=F32, tm=512, tn=1024, tk=1408, res=h1, name="ffn_down")
    loss_blk, dh2, dh2_bf, dg_final = _loss_head(h2, norm_final_g.reshape(1, D_MODEL), target)
    dact = _mm(dh2_bf, w_down_bf, mode="nt", out_dtype=F32, tm=512, tn=1408, tk=1024, outer="j", name="ffn_down_dx")
    dw_down = _mm(act, dh2_bf, mode="tn", out_dtype=F32, tm=1408, tn=1024, tk=1024, name="ffn_down_dw")
    dhu, dcw = _conv_act_bwd(hu, dact, cw, cb)
    dhn = _mm(dhu, w_up_bf, mode="nt", out_dtype=F32, tm=1024, tn=1024, tk=1408, a_halves=True, name="ffn_up_dx")
    dw_up = _mm(hn, dhu, mode="tn", out_dtype=F32, tm=1024, tn=1408, tk=1024, b_halves=True, outer="j", name="ffn_up_dw")
    dh1, dh1_bf, dg_ffn = _rms_bwd(h1, g_ffn, dhn, dh2)
    dmix = _mm(dh1_bf, w_out_bf, mode="nt", out_dtype=F32, tm=512, tn=1024, tk=1024, name="out_proj_dx")
    dw_out = jnp.concatenate(
        [_mm(att_bf, dh1_bf, mode="tn", out_dtype=F32, tm=512, tn=1024, tk=1024, name="out_proj_dw_att"),
         _mm(sg, dh1_bf, mode="tn", out_dtype=F32, tm=512, tn=1024, tk=1024, name="out_proj_dw_sg")], axis=0)
    qb, doa = _attn_prep(att, lse, dmix, qa)
    dq, dk, dv, dcq, dck = _attn_bwd(qb, ka, va, doa)
    wt_s = w_s.transpose(0, 2, 1)
    dug, dw_s, dgain, dbias = _gmlp_bwd(ug, dmix, gain, w_s, wt_s, bias_full)
    dzf, dbf = _gate_bwd(dcq, dck, zf)
    pieces = (dq, dk, dv, dug, dzf)
    dw_in = _inproj_bwd_dw(xn, pieces)
    grad_x, dg_mix = _inproj_bwd_dx(pieces, w_pad, x, norm_mix_g.reshape(1, D_MODEL), dh1)
    grads = dict(
        norm_mix_g=dg_mix[0:1, :],
        w_in=dw_in[:, :IN_COLS],
        b_forget=dbf[0:1, :N_HEADS],
        gmlp_norm_g=dgain[0:1, :],
        w_spatial=dw_s,
        b_spatial=dbias[:, ::GROUP_DIM].T,
        w_out=dw_out,
        norm_ffn_g=dg_ffn[0:1, :],
        w_up=dw_up,
        conv_w=dcw[:, 0:3, :].transpose(1, 0, 2).reshape(3, 2 * f),
        conv_b=dcw[:, 3, :].reshape(1, 2 * f),
        w_down=dw_down,
        norm_final_g=dg_final[0, :],
    )
    return loss_blk[0, 0], grad_x, grads


SMALL = ("norm_mix_g", "b_forget", "gmlp_norm_g", "w_spatial", "b_spatial", "norm_ffn_g", "conv_b", "norm_final_g")


def kernel(x, norm_mix_g, w_in, b_forget, gmlp_norm_g, w_spatial, b_spatial, w_out, norm_ffn_g, w_up, conv_w, conv_b, w_down, norm_final_g, loss_target, m_norm_mix_g, m_w_in, m_b_forget, m_gmlp_norm_g, m_w_spatial, m_b_spatial, m_w_out, m_norm_ffn_g, m_w_up, m_conv_w, m_conv_b, m_w_down, m_norm_final_g, v_norm_mix_g, v_w_in, v_b_forget, v_gmlp_norm_g, v_w_spatial, v_b_spatial, v_w_out, v_norm_ffn_g, v_w_up, v_conv_w, v_conv_b, v_w_down, v_norm_final_g):
    weights = dict(norm_mix_g=norm_mix_g, w_in=w_in, b_forget=b_forget, gmlp_norm_g=gmlp_norm_g, w_spatial=w_spatial,
                   b_spatial=b_spatial, w_out=w_out, norm_ffn_g=norm_ffn_g, w_up=w_up, conv_w=conv_w, conv_b=conv_b,
                   w_down=w_down, norm_final_g=norm_final_g)
    m_in = dict(norm_mix_g=m_norm_mix_g, w_in=m_w_in, b_forget=m_b_forget, gmlp_norm_g=m_gmlp_norm_g,
                w_spatial=m_w_spatial, b_spatial=m_b_spatial, w_out=m_w_out, norm_ffn_g=m_norm_ffn_g, w_up=m_w_up,
                conv_w=m_conv_w, conv_b=m_conv_b, w_down=m_w_down, norm_final_g=m_norm_final_g)
    v_in = dict(norm_mix_g=v_norm_mix_g, w_in=v_w_in, b_forget=v_b_forget, gmlp_norm_g=v_gmlp_norm_g,
                w_spatial=v_w_spatial, b_spatial=v_b_spatial, w_out=v_w_out, norm_ffn_g=v_norm_ffn_g, w_up=v_w_up,
                conv_w=v_conv_w, conv_b=v_conv_b, w_down=v_w_down, norm_final_g=v_norm_final_g)
    order = list(weights)
    me = 4 * lax.axis_index("x") + 2 * lax.axis_index("y") + lax.axis_index("c")
    n_in, n_up = w_in.shape[2], w_up.shape[2]
    r_out, r_down = w_out.shape[1], w_down.shape[1]

    cols_blk = jnp.concatenate([w_in[0].astype(BF16), w_up[0].astype(BF16)], axis=1)
    rows_blk = jnp.concatenate([w_out[0].astype(BF16), w_down[0].astype(BF16)], axis=0)
    taps_blk = jnp.pad(conv_w[0], ((0, 5), (0, 0)))
    cols_all, rows_all, taps_all = _all_gather([cols_blk, rows_blk, taps_blk], "gather_weights")
    w_in_bf = cols_all[:, :, :n_in].transpose(1, 0, 2).reshape(D_MODEL, N_DEV * n_in)
    w_up_bf = cols_all[:, :, n_in:].transpose(1, 0, 2).reshape(D_MODEL, N_DEV * n_up)
    w_out_bf = rows_all[:, :r_out, :].reshape(N_DEV * r_out, D_MODEL)
    w_down_bf = rows_all[:, r_out:, :].reshape(N_DEV * r_down, D_MODEL)
    conv_w_full = taps_all[:, :3, :].transpose(1, 0, 2).reshape(3, N_DEV * n_up)

    loss_local, grad_x, g = _local_step(
        x[0], loss_target[0], norm_mix_g, w_in_bf, b_forget, gmlp_norm_g, w_spatial, b_spatial, w_out_bf, norm_ffn_g,
        w_up_bf, conv_w_full, conv_b, w_down_bf, norm_final_g)
    loss = lax.psum(loss_local, ("x", "y", "c"))

    to_in = g["w_in"].reshape(D_MODEL, N_DEV, n_in).transpose(1, 0, 2)
    to_up = g["w_up"].reshape(D_MODEL, N_DEV, n_up).transpose(1, 0, 2)
    to_out = g["w_out"].reshape(N_DEV, r_out, D_MODEL)
    to_down = g["w_down"].reshape(N_DEV, r_down, D_MODEL)
    got_in, got_up, got_out, got_down = _exchange_shards([to_in, to_up, to_out, to_down], "scatter_grads")

    small_names = SMALL + ("conv_w",)
    packed = [_rows128(g[k]) for k in small_names]
    sizes = [p.shape[0] for p in packed]
    (small_all,) = _all_gather([jnp.concatenate(packed, axis=0)], "gather_small_grads")

    def pack(src):
        return jnp.concatenate([_rows128(src[k]) for k in SMALL], axis=0)

    n_small_rows = sum(sizes[:-1])
    sg_, sd_, sm_, sv_ = _adamw(pack(weights), pack(m_in), pack(v_in), small_all[:, :n_small_rows, :], "adamw_small")

    outs = {}
    off = 0
    for k, rows in zip(SMALL, sizes[:-1]):
        shp = weights[k].shape
        cnt = math.prod(shp)
        outs[k] = tuple(a[off:off + rows].reshape(-1)[:cnt].reshape(shp) for a in (sg_, sd_, sm_, sv_))
        off += rows
    taps_parts = small_all[:, n_small_rows:, :].reshape(N_DEV, -1)[:, :3 * N_DEV * n_up].reshape(N_DEV, 3, N_DEV * n_up)
    taps_mine = lax.dynamic_slice_in_dim(taps_parts, me * n_up, n_up, axis=2)
    taps_mine = jnp.pad(taps_mine, ((0, 0), (0, 5), (0, 0)))

    def pad8(a):
        return jnp.pad(a[0], ((0, 5), (0, 0)))

    res = _adamw(pad8(conv_w), pad8(m_conv_w), pad8(v_conv_w), taps_mine, "adamw_conv_w")
    outs["conv_w"] = tuple(a[:3][None] for a in res)
    for k, got in (("w_in", got_in), ("w_up", got_up), ("w_out", got_out), ("w_down", got_down)):
        res = _adamw(weights[k][0], m_in[k][0], v_in[k][0], got, "adamw_" + k)
        outs[k] = tuple(a[None] for a in res)

    return (loss, grad_x[None], *[outs[k][0] for k in order], *[outs[k][1] for k in order],
            *[outs[k][2] for k in order], *[outs[k][3] for k in order])
```

```python
import functools
import math

import jax
import jax.numpy as jnp
from jax import lax
from jax.experimental import pallas as pl
from jax.experimental.pallas import tpu as pltpu

F32 = jnp.float32
BF16 = jnp.bfloat16

N_DEV = 8
D_MODEL = 1024
ATT_WIDTH = 512
GMLP_WIDTH = 512
HEAD_DIM = 64
N_HEADS = 8
N_PAIRS = 4
N_GROUPS = 8
GROUP_DIM = 64
CHUNK = 128
D_FF = 2816
IN_COLS = 2568
IN_PAD = 2688
QKV = 1536
UG_END = 2560
EPS = 1e-6
LANES = 128

ADAM_LR = 0.001
ADAM_B1 = 0.9
ADAM_B2 = 0.999
ADAM_EPS = 1e-08
ADAM_WD = 0.01
ADAM_STEP = 10

VMEM_LIMIT = 56 * 1024 * 1024
MESH = pl.DeviceIdType.MESH


def _cp(sem, vmem=None):
    return pltpu.CompilerParams(dimension_semantics=sem, vmem_limit_bytes=vmem)


def _pick(n, prefs):
    for p in prefs:
        if n % p == 0:
            return p
    return n


def _split3(x):
    hi = x.astype(BF16)
    r1 = x - hi.astype(F32)
    mid = r1.astype(BF16)
    lo = (r1 - mid.astype(F32)).astype(BF16)
    return hi, mid, lo


def _dot3(x, ones_bf):
    hi, mid, lo = _split3(x)
    d = functools.partial(jnp.dot, preferred_element_type=F32)
    return d(hi, ones_bf) + d(mid, ones_bf) + d(lo, ones_bf)


def _dot3l(ones_bf, x):
    hi, mid, lo = _split3(x)
    d = functools.partial(jnp.dot, preferred_element_type=F32)
    return d(ones_bf, hi) + d(ones_bf, mid) + d(ones_bf, lo)


def _gelu(x):
    k = math.sqrt(2.0 / math.pi)
    t = jnp.tanh(k * (x + 0.044715 * (x * x * x)))
    return 0.5 * x * (1.0 + t)


def _gelu_grad(x):
    k = math.sqrt(2.0 / math.pi)
    x2 = x * x
    t = jnp.tanh(k * (x + 0.044715 * (x2 * x)))
    return 0.5 * (1.0 + t) + 0.5 * x * (1.0 - t * t) * (k * (1.0 + 3.0 * 0.044715 * x2))


def _sigmoid(x):
    return 1.0 / (1.0 + jnp.exp(-x))


def _mm(a, b, *, mode, out_dtype, tm, tn, tk, name, res=None, a_halves=False, b_halves=False,
        out_halves=False, outer="i"):
    if mode == "tn":
        K, M = a.shape[-2], a.shape[-1] * (2 if a_halves else 1)
    else:
        M, K = a.shape[-2], a.shape[-1] * (2 if a_halves else 1)
    if mode == "nt":
        N = b.shape[-2]
        assert b.shape[-1] == K
    else:
        N = b.shape[-1] * (2 if b_halves else 1)
    tm, tn, tk = min(tm, M), min(tn, N), min(tk, K)
    assert M % tm == 0 and N % tn == 0 and K % tk == 0, (name, M, N, K, tm, tn, tk)
    nm, nn, nk = M // tm, N // tn, K // tk

    def ij(g0, g1):
        return (g0, g1) if outer == "i" else (g1, g0)

    if mode == "nn":
        dims = (((1,), (0,)), ((), ()))
        if a_halves:
            nkh = nk // 2
            a_spec = pl.BlockSpec((None, tm, tk), lambda g0, g1, k: (k // nkh, ij(g0, g1)[0], k % nkh))
        else:
            a_spec = pl.BlockSpec((tm, tk), lambda g0, g1, k: (ij(g0, g1)[0], k))
        b_spec = pl.BlockSpec((tk, tn), lambda g0, g1, k: (k, ij(g0, g1)[1]))
    elif mode == "nt":
        dims = (((1,), (1,)), ((), ()))
        if a_halves:
            nkh = nk // 2
            a_spec = pl.BlockSpec((None, tm, tk), lambda g0, g1, k: (k // nkh, ij(g0, g1)[0], k % nkh))
        else:
            a_spec = pl.BlockSpec((tm, tk), lambda g0, g1, k: (ij(g0, g1)[0], k))
        b_spec = pl.BlockSpec((tn, tk), lambda g0, g1, k: (ij(g0, g1)[1], k))
    else:
        dims = (((0,), (0,)), ((), ()))
        if a_halves:
            nmh = nm // 2
            a_spec = pl.BlockSpec((None, tk, tm), lambda g0, g1, k: (ij(g0, g1)[0] // nmh, k, ij(g0, g1)[0] % nmh))
        else:
            a_spec = pl.BlockSpec((tk, tm), lambda g0, g1, k: (k, ij(g0, g1)[0]))
        if b_halves:
            nnh = nn // 2
            b_spec = pl.BlockSpec((None, tk, tn), lambda g0, g1, k: (ij(g0, g1)[1] // nnh, k, ij(g0, g1)[1] % nnh))
        else:
            b_spec = pl.BlockSpec((tk, tn), lambda g0, g1, k: (k, ij(g0, g1)[1]))
    if out_halves:
        nnh = nn // 2
        o_spec = pl.BlockSpec((None, tm, tn), lambda g0, g1, k: (ij(g0, g1)[1] // nnh, ij(g0, g1)[0], ij(g0, g1)[1] % nnh))
        o_shape = jax.ShapeDtypeStruct((2, M, N // 2), out_dtype)
    else:
        o_spec = pl.BlockSpec((tm, tn), lambda g0, g1, k: ij(g0, g1))
        o_shape = jax.ShapeDtypeStruct((M, N), out_dtype)
    in_specs = [a_spec, b_spec]
    args = [a, b]
    if res is not None:
        in_specs.append(pl.BlockSpec((tm, tn), lambda g0, g1, k: ij(g0, g1)))
        args.append(res)

    def body(*refs):
        if res is not None:
            a_ref, b_ref, r_ref, o_ref = refs[:4]
        else:
            a_ref, b_ref, o_ref = refs[:3]
            r_ref = None
        part = lax.dot_general(a_ref[...], b_ref[...], dims, preferred_element_type=F32)
        if nk == 1:
            if r_ref is not None:
                part = part + r_ref[...]
            o_ref[...] = part.astype(out_dtype)
            return
        acc_ref = refs[-1]
        k = pl.program_id(2)

        @pl.when(k == 0)
        def _():
            acc_ref[...] = part

        @pl.when(k > 0)
        def _():
            acc_ref[...] += part

        @pl.when(k == nk - 1)
        def _():
            tot = acc_ref[...]
            if r_ref is not None:
                tot = tot + r_ref[...]
            o_ref[...] = tot.astype(out_dtype)

    grid = (nm, nn, nk) if outer == "i" else (nn, nm, nk)
    scratch = [] if nk == 1 else [pltpu.VMEM((tm, tn), F32)]
    return pl.pallas_call(
        body, out_shape=o_shape, grid=grid, in_specs=in_specs, out_specs=o_spec, scratch_shapes=scratch,
        name=name, compiler_params=_cp(("parallel", "parallel", "arbitrary"), VMEM_LIMIT),
    )(*args)


def _aug(lane, terms):
    out = 0.0
    for j, t in enumerate(terms):
        out = jnp.where(lane == HEAD_DIM + j, t, out)
    return out


def _split3f(x):
    hi, mid, lo = _split3(x)
    return [hi.astype(F32), mid.astype(F32), lo.astype(F32)]


def _inproj_fwd(x, g_mix, w_pad, bf_pad):
    S = x.shape[0]
    tm = _pick(S, (256,))
    tri = (lax.broadcasted_iota(jnp.int32, (tm, tm), 0) >= lax.broadcasted_iota(jnp.int32, (tm, tm), 1)).astype(BF16)

    def body(x_ref, g_ref, w_ref, bf_ref, tri_ref, xn_ref, qa_ref, ka_ref, va_ref, ug_ref, zf_ref, carry_ref):
        i = pl.program_id(0)

        @pl.when(i == 0)
        def _():
            carry_ref[...] = jnp.zeros_like(carry_ref)

        xf = x_ref[...]
        r = lax.rsqrt(jnp.mean(xf * xf, axis=-1, keepdims=True) + EPS)
        xn = ((xf * r) * g_ref[...]).astype(BF16)
        xn_ref[...] = xn
        proj = jnp.dot(xn, w_ref[...], preferred_element_type=F32)
        ug_ref[...] = proj[:, QKV:UG_END]
        zf = proj[:, UG_END:] + bf_ref[...]
        zf_ref[...] = zf
        lf = jnp.minimum(zf, 0.0) - jnp.log(1.0 + jnp.exp(-jnp.abs(zf)))
        c = _dot3l(tri_ref[...], lf) + carry_ref[0:1, :]
        carry_ref[0:1, :] = c[tm - 1:tm, :]
        c3 = _split3f(c)
        lane = lax.broadcasted_iota(jnp.int32, (tm, LANES), 1)
        ones3 = [1.0, 1.0, 1.0]
        for h in range(N_HEADS):
            p, odd = h // 2, h % 2
            ch = [t[:, h:h + 1] for t in c3]

            def head(base, scale=None, p=p, odd=odd):
                blk = proj[:, base + p * LANES:base + (p + 1) * LANES]
                if scale is not None:
                    blk = blk * scale
                return pltpu.roll(blk, HEAD_DIM, 1) if odd else blk

            cols = slice(h * LANES, (h + 1) * LANES)
            qa_ref[:, cols] = jnp.where(lane < HEAD_DIM, head(0, HEAD_DIM ** -0.5), _aug(lane, ch + ones3)).astype(BF16)
            ka_ref[:, cols] = jnp.where(lane < HEAD_DIM, head(ATT_WIDTH),
                                        _aug(lane, ones3 + [-t for t in ch] + ones3)).astype(BF16)
            va_ref[:, cols] = jnp.where(lane < HEAD_DIM, head(2 * ATT_WIDTH), _aug(lane, ones3)).astype(BF16)

    wide = N_HEADS * LANES
    return pl.pallas_call(
        body,
        out_shape=(jax.ShapeDtypeStruct((S, D_MODEL), BF16), jax.ShapeDtypeStruct((S, wide), BF16),
                   jax.ShapeDtypeStruct((S, wide), BF16), jax.ShapeDtypeStruct((S, wide), BF16),
                   jax.ShapeDtypeStruct((S, 2 * GMLP_WIDTH), F32), jax.ShapeDtypeStruct((S, LANES), F32)),
        grid=(S // tm,),
        in_specs=[pl.BlockSpec((tm, D_MODEL), lambda i: (i, 0)), pl.BlockSpec((1, D_MODEL), lambda i: (0, 0)),
                  pl.BlockSpec((D_MODEL, IN_PAD), lambda i: (0, 0)), pl.BlockSpec((1, LANES), lambda i: (0, 0)),
                  pl.BlockSpec((tm, tm), lambda i: (0, 0))],
        out_specs=(pl.BlockSpec((tm, D_MODEL), lambda i: (i, 0)), pl.BlockSpec((tm, wide), lambda i: (i, 0)),
                   pl.BlockSpec((tm, wide), lambda i: (i, 0)), pl.BlockSpec((tm, wide), lambda i: (i, 0)),
                   pl.BlockSpec((tm, 2 * GMLP_WIDTH), lambda i: (i, 0)), pl.BlockSpec((tm, LANES), lambda i: (i, 0))),
        scratch_shapes=[pltpu.VMEM((8, LANES), F32)],
        name="inproj_fwd", compiler_params=_cp(("arbitrary",), VMEM_LIMIT),
    )(x, g_mix, w_pad, bf_pad, tri)


def _group_ones():
    r = lax.broadcasted_iota(jnp.int32, (GMLP_WIDTH, GMLP_WIDTH), 0) // GROUP_DIM
    c = lax.broadcasted_iota(jnp.int32, (GMLP_WIDTH, GMLP_WIDTH), 1) // GROUP_DIM
    return (r == c).astype(BF16)


def _gmlp_mixed(vn_bf, w_ref, bias, n_chunks):
    lane = lax.broadcasted_iota(jnp.int32, (CHUNK, LANES), 1)
    row = lax.broadcasted_iota(jnp.int32, (CHUNK, CHUNK), 0)
    col = lax.broadcasted_iota(jnp.int32, (CHUNK, CHUNK), 1)
    ws = [jnp.where(row >= col, w_ref[g], 0.0).astype(BF16) for g in range(N_GROUPS)]
    rows = []
    for ci in range(n_chunks):
        cols = []
        for pp in range(N_GROUPS // 2):
            v = vn_bf[ci * CHUNK:(ci + 1) * CHUNK, pp * LANES:(pp + 1) * LANES]
            v_lo = jnp.where(lane < GROUP_DIM, v, jnp.zeros_like(v))
            v_hi = jnp.where(lane >= GROUP_DIM, v, jnp.zeros_like(v))
            m = (jnp.dot(ws[2 * pp], v_lo, preferred_element_type=F32)
                 + jnp.dot(ws[2 * pp + 1], v_hi, preferred_element_type=F32))
            cols.append(m + bias[:, pp * LANES:(pp + 1) * LANES])
        rows.append(jnp.concatenate(cols, axis=1))
    return jnp.concatenate(rows, axis=0)


def _gmlp_fwd(ug, gain, w_s, bias_full):
    S = ug.shape[0]
    tm = _pick(S, (512, 256, 128))
    ones = _group_ones()

    def body(ug_ref, gain_ref, w_ref, bias_ref, ones_ref, sg_ref):
        u = _gelu(ug_ref[:, :GMLP_WIDTH])
        vr = _gelu(ug_ref[:, GMLP_WIDTH:])
        ms = _dot3(vr * vr, ones_ref[...]) * (1.0 / GROUP_DIM)
        vn = ((vr * lax.rsqrt(ms + EPS)) * gain_ref[...]).astype(BF16)
        mixed = _gmlp_mixed(vn, w_ref, bias_ref[...], tm // CHUNK)
        sg_ref[...] = (u * mixed).astype(BF16)

    return pl.pallas_call(
        body, out_shape=jax.ShapeDtypeStruct((S, GMLP_WIDTH), BF16), grid=(S // tm,),
        in_specs=[pl.BlockSpec((tm, 2 * GMLP_WIDTH), lambda i: (i, 0)), pl.BlockSpec((1, GMLP_WIDTH), lambda i: (0, 0)),
                  pl.BlockSpec((N_GROUPS, CHUNK, CHUNK), lambda i: (0, 0, 0)),
                  pl.BlockSpec((CHUNK, GMLP_WIDTH), lambda i: (0, 0)),
                  pl.BlockSpec((GMLP_WIDTH, GMLP_WIDTH), lambda i: (0, 0))],
        out_specs=pl.BlockSpec((tm, GMLP_WIDTH), lambda i: (i, 0)),
        name="gmlp_fwd", compiler_params=_cp(("parallel",), VMEM_LIMIT),
    )(ug, gain, w_s, bias_full, ones)


_NT = (((1,), (1,)), ((), ()))
_TN = (((0,), (0,)), ((), ()))


def _attn_fwd(qa, ka, va):
    S = qa.shape[0]
    tq = _pick(S, (512, 256))
    tk = tq
    nq = S // tq

    def body(q_ref, k_ref, v_ref, o_ref, lse_ref, ob_ref):
        qi = pl.program_id(1)
        lane = lax.broadcasted_iota(jnp.int32, (tq, LANES), 1)
        rid = lax.broadcasted_iota(jnp.int32, (tq, tk), 0)
        cid = lax.broadcasted_iota(jnp.int32, (tq, tk), 1)
        qs = [q_ref[:, :LANES], q_ref[:, LANES:]]

        def update(kb, h, m, acc, masked):
            ks = pl.multiple_of(kb * tk, tk)
            cols = slice(h * LANES, (h + 1) * LANES)
            s = lax.dot_general(qs[h], k_ref[pl.ds(ks, tk), cols], _NT, preferred_element_type=F32)
            if masked:
                s = jnp.where(rid >= cid, s, -jnp.inf)
            m_new = jnp.maximum(m, jnp.max(s, axis=-1, keepdims=True))
            p = jnp.exp(s - m_new).astype(BF16)
            acc = jnp.exp(m - m_new) * acc + jnp.dot(p, v_ref[pl.ds(ks, tk), cols], preferred_element_type=F32)
            return m_new, acc

        def step(kb, carry):
            return tuple(update(kb, h, *carry[h], False) for h in range(2))

        one = (jnp.full((tq, 1), -jnp.inf, F32), jnp.zeros((tq, LANES), F32))
        carry = lax.fori_loop(0, qi, step, (one, one))
        outs, lses = [], []
        for h in range(2):
            m, acc = update(qi, h, *carry[h], True)
            l = acc[:, HEAD_DIM:HEAD_DIM + 1]
            outs.append(acc / l)
            lses.append(m + jnp.log(l))
        o = jnp.where(lane < HEAD_DIM, outs[0], pltpu.roll(outs[1], HEAD_DIM, 1))
        o_ref[...] = o
        ob_ref[...] = o.astype(BF16)
        lse_ref[...] = jnp.where(lane < HEAD_DIM, lses[0], lses[1])

    return pl.pallas_call(
        body,
        out_shape=(jax.ShapeDtypeStruct((S, ATT_WIDTH), F32), jax.ShapeDtypeStruct((S, ATT_WIDTH), F32),
                   jax.ShapeDtypeStruct((S, ATT_WIDTH), BF16)),
        grid=(N_PAIRS, nq),
        in_specs=[pl.BlockSpec((tq, 2 * LANES), lambda p, i: (i, p)),
                  pl.BlockSpec((S, 2 * LANES), lambda p, i: (0, p)),
                  pl.BlockSpec((S, 2 * LANES), lambda p, i: (0, p))],
        out_specs=(pl.BlockSpec((tq, LANES), lambda p, i: (i, p)), pl.BlockSpec((tq, LANES), lambda p, i: (i, p)),
                   pl.BlockSpec((tq, LANES), lambda p, i: (i, p))),
        name="attn_fwd", compiler_params=_cp(("parallel", "parallel"), VMEM_LIMIT),
    )(qa, ka, va)


def _rms_fwd(h, g):
    S = h.shape[0]
    tm = _pick(S, (512, 256))

    def body(h_ref, g_ref, o_ref):
        hf = h_ref[...]
        r = lax.rsqrt(jnp.mean(hf * hf, axis=-1, keepdims=True) + EPS)
        o_ref[...] = ((hf * r) * g_ref[...]).astype(BF16)

    return pl.pallas_call(
        body, out_shape=jax.ShapeDtypeStruct(h.shape, BF16), grid=(S // tm,),
        in_specs=[pl.BlockSpec((tm, D_MODEL), lambda i: (i, 0)), pl.BlockSpec((1, D_MODEL), lambda i: (0, 0))],
        out_specs=pl.BlockSpec((tm, D_MODEL), lambda i: (i, 0)),
        name="rms_fwd", compiler_params=_cp(("parallel",)),
    )(h, g)


def _shift_rows(x, prev, n):
    rid = lax.broadcasted_iota(jnp.int32, x.shape, 0)
    y = pltpu.roll(x, n, 0)
    if n == 1:
        return jnp.where(rid == 0, prev[7:8, :], y)
    return jnp.where(rid == 0, prev[6:7, :], jnp.where(rid == 1, prev[7:8, :], y))


def _shift_rows_up(x, nxt, n):
    rows = x.shape[0]
    rid = lax.broadcasted_iota(jnp.int32, x.shape, 0)
    y = pltpu.roll(x, rows - n, 0)
    if n == 1:
        return jnp.where(rid == rows - 1, nxt[0:1, :], y)
    return jnp.where(rid == rows - 2, nxt[0:1, :], jnp.where(rid == rows - 1, nxt[1:2, :], y))


def _conv3(cur, prev, w, b):
    return (w[0:1, :] * _shift_rows(cur, prev, 2) + w[1:2, :] * _shift_rows(cur, prev, 1)
            + w[2:3, :] * cur + b)


def _conv_act_fwd(hu, cw, cb):
    _, S, F = hu.shape
    tm = _pick(S, (512, 256))
    tn = _pick(F, (256, 128))
    r8 = tm // 8

    def body(cur_ref, prev_ref, w_ref, b_ref, o_ref):
        i = pl.program_id(1)
        halves = []
        for h in range(2):
            prev = jnp.where(i > 0, prev_ref[h], 0.0)
            halves.append(_conv3(cur_ref[h], prev, w_ref[h], b_ref[h]))
        a, g = halves
        o_ref[...] = (g * _sigmoid(g) * a).astype(BF16)

    return pl.pallas_call(
        body, out_shape=jax.ShapeDtypeStruct((S, F), BF16), grid=(F // tn, S // tm),
        in_specs=[pl.BlockSpec((2, tm, tn), lambda j, i: (0, i, j)),
                  pl.BlockSpec((2, 8, tn), lambda j, i: (0, jnp.maximum(i * r8 - 1, 0), j)),
                  pl.BlockSpec((2, 8, tn), lambda j, i: (0, 0, j)),
                  pl.BlockSpec((2, 1, tn), lambda j, i: (0, 0, j))],
        out_specs=pl.BlockSpec((tm, tn), lambda j, i: (i, j)),
        name="conv_act_fwd", compiler_params=_cp(("parallel", "parallel"), VMEM_LIMIT),
    )(hu, hu, cw, cb)


def _loss_head(h2, g_final, target):
    S = h2.shape[0]
    tm = _pick(S, (512, 256))

    def body(h_ref, g_ref, t_ref, loss_ref, dh_ref, dhb_ref, dg_ref):
        i = pl.program_id(0)

        @pl.when(i == 0)
        def _():
            loss_ref[...] = jnp.zeros_like(loss_ref)
            dg_ref[...] = jnp.zeros_like(dg_ref)

        hf = h_ref[...]
        g = g_ref[...]
        r = lax.rsqrt(jnp.mean(hf * hf, axis=-1, keepdims=True) + EPS)
        hhat = hf * r
        err = hhat * g - t_ref[...]
        loss_ref[...] += 0.5 * jnp.sum(jnp.mean(err * err, axis=-1, keepdims=True))
        dy = err * (1.0 / D_MODEL)
        dg_ref[0:1, :] += jnp.sum(dy * hhat, axis=0, keepdims=True)
        dhat = dy * g
        dh = r * (dhat - hhat * jnp.mean(dhat * hhat, axis=-1, keepdims=True))
        dh_ref[...] = dh
        dhb_ref[...] = dh.astype(BF16)

    return pl.pallas_call(
        body,
        out_shape=(jax.ShapeDtypeStruct((8, LANES), F32), jax.ShapeDtypeStruct((S, D_MODEL), F32),
                   jax.ShapeDtypeStruct((S, D_MODEL), BF16), jax.ShapeDtypeStruct((8, D_MODEL), F32)),
        grid=(S // tm,),
        in_specs=[pl.BlockSpec((tm, D_MODEL), lambda i: (i, 0)), pl.BlockSpec((1, D_MODEL), lambda i: (0, 0)),
                  pl.BlockSpec((tm, D_MODEL), lambda i: (i, 0))],
        out_specs=(pl.BlockSpec((8, LANES), lambda i: (0, 0)), pl.BlockSpec((tm, D_MODEL), lambda i: (i, 0)),
                   pl.BlockSpec((tm, D_MODEL), lambda i: (i, 0)), pl.BlockSpec((8, D_MODEL), lambda i: (0, 0))),
        name="loss_head", compiler_params=_cp(("arbitrary",), VMEM_LIMIT),
    )(h2, g_final, target)


def _rms_bwd(h, g, dy, res):
    S = h.shape[0]
    tm = _pick(S, (512, 256))

    def body(h_ref, g_ref, dy_ref, r_ref, dh_ref, dhb_ref, dg_ref):
        i = pl.program_id(0)

        @pl.when(i == 0)
        def _():
            dg_ref[...] = jnp.zeros_like(dg_ref)

        hf = h_ref[...]
        dyv = dy_ref[...]
        r = lax.rsqrt(jnp.mean(hf * hf, axis=-1, keepdims=True) + EPS)
        hhat = hf * r
        dg_ref[0:1, :] += jnp.sum(dyv * hhat, axis=0, keepdims=True)
        dhat = dyv * g_ref[...]
        dh = r_ref[...] + r * (dhat - hhat * jnp.mean(dhat * hhat, axis=-1, keepdims=True))
        dh_ref[...] = dh
        dhb_ref[...] = dh.astype(BF16)

    return pl.pallas_call(
        body,
        out_shape=(jax.ShapeDtypeStruct((S, D_MODEL), F32), jax.ShapeDtypeStruct((S, D_MODEL), BF16),
                   jax.ShapeDtypeStruct((8, D_MODEL), F32)),
        grid=(S // tm,),
        in_specs=[pl.BlockSpec((tm, D_MODEL), lambda i: (i, 0)), pl.BlockSpec((1, D_MODEL), lambda i: (0, 0)),
                  pl.BlockSpec((tm, D_MODEL), lambda i: (i, 0)), pl.BlockSpec((tm, D_MODEL), lambda i: (i, 0))],
        out_specs=(pl.BlockSpec((tm, D_MODEL), lambda i: (i, 0)), pl.BlockSpec((tm, D_MODEL), lambda i: (i, 0)),
                   pl.BlockSpec((8, D_MODEL), lambda i: (0, 0))),
        name="rms_bwd", compiler_params=_cp(("arbitrary",), VMEM_LIMIT),
    )(h, g, dy, res)


def _conv_act_bwd(hu, dact, cw, cb):
    _, S, F = hu.shape
    tm = _pick(S, (256,))
    tn = _pick(F, (256, 128))
    r8 = tm // 8
    n_i = S // tm
    last8 = S // 8 - 1

    def body(cur_ref, prev_ref, next_ref, da_ref, dan_ref, w_ref, b_ref, dhu_ref, dcw_ref):
        i = pl.program_id(1)

        @pl.when(i == 0)
        def _():
            dcw_ref[...] = jnp.zeros_like(dcw_ref)

        rid8 = lax.broadcasted_iota(jnp.int32, (8, tn), 0)
        cur = [cur_ref[0], cur_ref[1]]
        prev = [jnp.where(i > 0, prev_ref[h], 0.0) for h in range(2)]
        nxt = [next_ref[0], next_ref[1]]
        w = [w_ref[0], w_ref[1]]

        def gate_grads(a, g, d):
            sg = _sigmoid(g)
            return d * (g * sg), d * a * (sg * (1.0 + g * (1.0 - sg)))

        a = _conv3(cur[0], prev[0], w[0], b_ref[0])
        g = _conv3(cur[1], prev[1], w[1], b_ref[1])
        dhc = gate_grads(a, g, da_ref[...])
        a_n = _conv3(nxt[0], cur[0][tm - 8:, :], w[0], b_ref[0])
        g_n = _conv3(nxt[1], cur[1][tm - 8:, :], w[1], b_ref[1])
        dhc_n = gate_grads(a_n, g_n, dan_ref[...])
        for h in range(2):
            d = dhc[h]
            dn = jnp.where(i < n_i - 1, dhc_n[h], 0.0)
            dhu = (w[h][2:3, :] * d + w[h][1:2, :] * _shift_rows_up(d, dn, 1)
                   + w[h][0:1, :] * _shift_rows_up(d, dn, 2))
            dhu_ref[h] = dhu.astype(BF16)
            t0 = jnp.sum(d * _shift_rows(cur[h], prev[h], 2), axis=0, keepdims=True)
            t1 = jnp.sum(d * _shift_rows(cur[h], prev[h], 1), axis=0, keepdims=True)
            t2 = jnp.sum(d * cur[h], axis=0, keepdims=True)
            t3 = jnp.sum(d, axis=0, keepdims=True)
            dcw_ref[h] += jnp.where(rid8 == 0, t0, jnp.where(rid8 == 1, t1, jnp.where(rid8 == 2, t2, jnp.where(rid8 == 3, t3, 0.0))))

    return pl.pallas_call(
        body,
        out_shape=(jax.ShapeDtypeStruct((2, S, F), BF16), jax.ShapeDtypeStruct((2, 8, F), F32)),
        grid=(F // tn, n_i),
        in_specs=[pl.BlockSpec((2, tm, tn), lambda j, i: (0, i, j)),
                  pl.BlockSpec((2, 8, tn), lambda j, i: (0, jnp.maximum(i * r8 - 1, 0), j)),
                  pl.BlockSpec((2, 8, tn), lambda j, i: (0, jnp.minimum((i + 1) * r8, last8), j)),
                  pl.BlockSpec((tm, tn), lambda j, i: (i, j)),
                  pl.BlockSpec((8, tn), lambda j, i: (jnp.minimum((i + 1) * r8, last8), j)),
                  pl.BlockSpec((2, 8, tn), lambda j, i: (0, 0, j)),
                  pl.BlockSpec((2, 1, tn), lambda j, i: (0, 0, j))],
        out_specs=(pl.BlockSpec((2, tm, tn), lambda j, i: (0, i, j)), pl.BlockSpec((2, 8, tn), lambda j, i: (0, 0, j))),
        name="conv_act_bwd", compiler_params=_cp(("parallel", "arbitrary"), VMEM_LIMIT),
    )(hu, hu, hu, dact, dact, cw, cb)


def _attn_prep(att, lse, dmix, qa):
    S = att.shape[0]
    tm = _pick(S, (512, 256))

    def body(o_ref, lse_ref, do_ref, q_ref, qb_ref, doa_ref):
        lane = lax.broadcasted_iota(jnp.int32, (tm, LANES), 1)
        do = do_ref[...]
        prod = o_ref[...] * do
        for hh in range(2):
            sel = (lane >= HEAD_DIM) if hh else (lane < HEAD_DIM)
            delta = jnp.sum(jnp.where(sel, prod, 0.0), axis=-1, keepdims=True)
            dod = pltpu.roll(do, HEAD_DIM, 1) if hh else do
            cols = slice(hh * LANES, (hh + 1) * LANES)
            doa_ref[:, cols] = jnp.where(lane < HEAD_DIM, dod, _aug(lane, _split3f(-delta))).astype(BF16)
            l3 = _split3f(-lse_ref[:, hh * HEAD_DIM:hh * HEAD_DIM + 1])
            augl = _aug(lane, [0.0] * 6 + l3).astype(BF16)
            qb_ref[:, cols] = jnp.where((lane >= HEAD_DIM + 6) & (lane < HEAD_DIM + 9), augl, q_ref[:, cols])

    return pl.pallas_call(
        body,
        out_shape=(jax.ShapeDtypeStruct(qa.shape, BF16), jax.ShapeDtypeStruct(qa.shape, BF16)),
        grid=(S // tm, N_PAIRS),
        in_specs=[pl.BlockSpec((tm, LANES), lambda i, p: (i, p)), pl.BlockSpec((tm, LANES), lambda i, p: (i, p)),
                  pl.BlockSpec((tm, LANES), lambda i, p: (i, p)), pl.BlockSpec((tm, 2 * LANES), lambda i, p: (i, p))],
        out_specs=(pl.BlockSpec((tm, 2 * LANES), lambda i, p: (i, p)), pl.BlockSpec((tm, 2 * LANES), lambda i, p: (i, p))),
        name="attn_prep", compiler_params=_cp(("parallel", "parallel")),
    )(att, lse, dmix, qa)


def _attn_bwd(qb, ka, va, doa):
    S = qb.shape[0]
    tk = _pick(S, (512, 256))
    tq = tk
    nq = S // tq

    def pair(a, scale=None):
        lane = lax.broadcasted_iota(jnp.int32, (a.shape[0], LANES), 1)
        out = jnp.where(lane < HEAD_DIM, a[:, :LANES], pltpu.roll(a[:, LANES:], HEAD_DIM, 1))
        return out if scale is None else out * scale

    def lanes01(a, col, sign):
        lane = lax.broadcasted_iota(jnp.int32, (a.shape[0], LANES), 1)
        return jnp.where(lane == 0, sign * a[:, col:col + 1], jnp.where(lane == 1, sign * a[:, LANES + col:LANES + col + 1], 0.0))

    def body(q_ref, do_ref, k_ref, v_ref, dqc_ref, dkc_ref, dvc_ref, dcq_ref, dck_ref, dq_ref, dka_ref, dva_ref):
        kb = pl.program_id(1)

        @pl.when(kb == 0)
        def _():
            dq_ref[...] = jnp.zeros_like(dq_ref)

        dka_ref[...] = jnp.zeros_like(dka_ref)
        dva_ref[...] = jnp.zeros_like(dva_ref)
        rid = lax.broadcasted_iota(jnp.int32, (tk, tq), 0)
        cid = lax.broadcasted_iota(jnp.int32, (tk, tq), 1)

        def tile(qi, masked):
            qs = pl.multiple_of(qi * tq, tq)
            for h in range(2):
                cols = slice(h * LANES, (h + 1) * LANES)
                qblk = q_ref[pl.ds(qs, tq), cols]
                doblk = do_ref[pl.ds(qs, tq), cols]
                kh = k_ref[:, cols]
                p = jnp.exp(lax.dot_general(kh, qblk, _NT, preferred_element_type=F32))
                if masked:
                    p = jnp.where(cid >= rid, p, 0.0)
                ds = (p * lax.dot_general(v_ref[:, cols], doblk, _NT, preferred_element_type=F32)).astype(BF16)
                dva_ref[:, cols] += jnp.dot(p.astype(BF16), doblk, preferred_element_type=F32)
                dka_ref[:, cols] += jnp.dot(ds, qblk, preferred_element_type=F32)
                dq_ref[pl.ds(qs, tq), cols] += lax.dot_general(ds, kh, _TN, preferred_element_type=F32)

        tile(kb, True)

        def step(qi, carry):
            tile(qi, False)
            return carry

        lax.fori_loop(kb + 1, nq, step, 0)
        dka = dka_ref[...]
        dkc_ref[...] = pair(dka).astype(BF16)
        dvc_ref[...] = pair(dva_ref[...]).astype(BF16)
        dck_ref[...] = lanes01(dka, HEAD_DIM + 3, -1.0)

        @pl.when(kb == nq - 1)
        def _():
            dqa = dq_ref[...]
            dqc_ref[...] = pair(dqa, HEAD_DIM ** -0.5).astype(BF16)
            dcq_ref[...] = lanes01(dqa, HEAD_DIM, 1.0)

    wide = 2 * LANES
    half = jax.ShapeDtypeStruct((S, ATT_WIDTH), BF16)
    slabs = jax.ShapeDtypeStruct((N_PAIRS, S, LANES), F32)
    return pl.pallas_call(
        body,
        out_shape=(half, half, half, slabs, slabs),
        grid=(N_PAIRS, nq),
        in_specs=[pl.BlockSpec((S, wide), lambda p, j: (0, p)), pl.BlockSpec((S, wide), lambda p, j: (0, p)),
                  pl.BlockSpec((tk, wide), lambda p, j: (j, p)), pl.BlockSpec((tk, wide), lambda p, j: (j, p))],
        out_specs=(pl.BlockSpec((S, LANES), lambda p, j: (0, p)), pl.BlockSpec((tk, LANES), lambda p, j: (j, p)),
                   pl.BlockSpec((tk, LANES), lambda p, j: (j, p)), pl.BlockSpec((None, S, LANES), lambda p, j: (p, 0, 0)),
                   pl.BlockSpec((None, tk, LANES), lambda p, j: (p, j, 0))),
        scratch_shapes=[pltpu.VMEM((S, wide), F32), pltpu.VMEM((tk, wide), F32), pltpu.VMEM((tk, wide), F32)],
        name="attn_bwd", compiler_params=_cp(("parallel", "arbitrary"), VMEM_LIMIT),
    )(qb, doa, ka, va)


def _attn_delta_old(o, do):
    S = o.shape[0]
    tm = _pick(S, (512, 256))
    ones = _group_ones()

    def body(o_ref, do_ref, ones_ref, d_ref):
        d_ref[...] = _dot3(o_ref[...] * do_ref[...], ones_ref[...])

    return pl.pallas_call(
        body, out_shape=jax.ShapeDtypeStruct((S, ATT_WIDTH), F32), grid=(S // tm,),
        in_specs=[pl.BlockSpec((tm, ATT_WIDTH), lambda i: (i, 0)), pl.BlockSpec((tm, ATT_WIDTH), lambda i: (i, 0)),
                  pl.BlockSpec((ATT_WIDTH, ATT_WIDTH), lambda i: (0, 0))],
        out_specs=pl.BlockSpec((tm, ATT_WIDTH), lambda i: (i, 0)),
        name="attn_delta", compiler_params=_cp(("parallel",)),
    )(o, do, ones)


def _attn_bwd_old(qkv, do_bf, c_cols, c_rows, lse_rows, dl_rows):
    S = qkv.shape[0]
    tk = _pick(S, (256,))
    tq = tk
    nq = S // tq
    nt = (((1,), (1,)), ((), ()))
    tn_dims = (((0,), (0,)), ((), ()))

    def body(q_ref, do_ref, k_ref, v_ref, cc_ref, cr_ref, lse_ref, dl_ref, dq_ref, dk_ref, dv_ref, dcs_ref, dcq_ref):
        kb = pl.program_id(1)

        @pl.when(kb == 0)
        def _():
            dq_ref[...] = jnp.zeros_like(dq_ref)
            dcq_ref[...] = jnp.zeros_like(dcq_ref)

        lane = lax.broadcasted_iota(jnp.int32, (tk, LANES), 1)
        rid = lax.broadcasted_iota(jnp.int32, (tk, tq), 0)
        cid = lax.broadcasted_iota(jnp.int32, (tk, tq), 1)
        k = k_ref[...]
        v = v_ref[...]
        dks, dvs, dcs = [], [], []
        for hh in range(2):
            sel = (lane >= GROUP_DIM) if hh else (lane < GROUP_DIM)
            km = jnp.where(sel, k, jnp.zeros_like(k))
            vm = jnp.where(sel, v, jnp.zeros_like(v))
            cs = cc_ref[:, hh * HEAD_DIM:hh * HEAD_DIM + 1]

            def step(qb, carry, km=km, vm=vm, cs=cs, hh=hh):
                dk_acc, dv_acc, dc_acc = carry
                qs = pl.multiple_of(qb * tq, tq)
                qblk = q_ref[pl.ds(qs, tq), :]
                doblk = do_ref[pl.ds(qs, tq), :]
                s = lax.dot_general(km, qblk, nt, preferred_element_type=F32)
                s = s + (cr_ref[hh:hh + 1, pl.ds(qs, tq)] - cs)
                p = jnp.exp(s - lse_ref[hh:hh + 1, pl.ds(qs, tq)])
                p = jnp.where((qb > kb) | (cid >= rid), p, 0.0)
                dp = lax.dot_general(vm, doblk, nt, preferred_element_type=F32)
                ds = p * (dp - dl_ref[hh:hh + 1, pl.ds(qs, tq)])
                ds_bf = ds.astype(BF16)
                dv_acc = dv_acc + jnp.dot(p.astype(BF16), doblk, preferred_element_type=F32)
                dk_acc = dk_acc + jnp.dot(ds_bf, qblk, preferred_element_type=F32)
                dc_acc = dc_acc + jnp.sum(ds, axis=-1, keepdims=True)
                dq_ref[pl.ds(qs, tq), :] += lax.dot_general(ds_bf, km, tn_dims, preferred_element_type=F32)
                dcq_ref[hh:hh + 1, pl.ds(qs, tq)] += jnp.sum(ds, axis=0, keepdims=True)
                return dk_acc, dv_acc, dc_acc

            init = (jnp.zeros((tk, LANES), F32), jnp.zeros((tk, LANES), F32), jnp.zeros((tk, 1), F32))
            dk_acc, dv_acc, dc_acc = lax.fori_loop(kb, nq, step, init)
            dks.append(dk_acc)
            dvs.append(dv_acc)
            dcs.append(dc_acc)
        dk_ref[...] = jnp.where(lane < GROUP_DIM, dks[0], dks[1])
        dv_ref[...] = jnp.where(lane < GROUP_DIM, dvs[0], dvs[1])
        dcs_ref[...] = jnp.where(lane == 0, -dcs[0], jnp.where(lane == 1, -dcs[1], 0.0))

    return pl.pallas_call(
        body,
        out_shape=(jax.ShapeDtypeStruct((S, ATT_WIDTH), F32), jax.ShapeDtypeStruct((S, ATT_WIDTH), F32),
                   jax.ShapeDtypeStruct((S, ATT_WIDTH), F32), jax.ShapeDtypeStruct((N_PAIRS, S, LANES), F32),
                   jax.ShapeDtypeStruct((N_PAIRS, 8, S), F32)),
        grid=(N_PAIRS, nq),
        in_specs=[pl.BlockSpec((S, LANES), lambda p, j: (0, p)),
                  pl.BlockSpec((S, LANES), lambda p, j: (0, p)),
                  pl.BlockSpec((tk, LANES), lambda p, j: (j, N_PAIRS + p)),
                  pl.BlockSpec((tk, LANES), lambda p, j: (j, 2 * N_PAIRS + p)),
                  pl.BlockSpec((None, tk, LANES), lambda p, j: (p, j, 0)),
                  pl.BlockSpec((None, 8, S), lambda p, j: (p, 0, 0)),
                  pl.BlockSpec((None, 8, S), lambda p, j: (p, 0, 0)),
                  pl.BlockSpec((None, 8, S), lambda p, j: (p, 0, 0))],
        out_specs=(pl.BlockSpec((S, LANES), lambda p, j: (0, p)),
                   pl.BlockSpec((tk, LANES), lambda p, j: (j, p)),
                   pl.BlockSpec((tk, LANES), lambda p, j: (j, p)),
                   pl.BlockSpec((None, tk, LANES), lambda p, j: (p, j, 0)),
                   pl.BlockSpec((None, 8, S), lambda p, j: (p, 0, 0))),
        name="attn_bwd", compiler_params=_cp(("parallel", "arbitrary"), VMEM_LIMIT),
    )(qkv, do_bf, qkv, qkv, c_cols, c_rows, lse_rows, dl_rows)


def _gmlp_bwd(ug, dsg, gain, w_s, wt_s, bias_full):
    S = ug.shape[0]
    tm = _pick(S, (512, 256, 128))
    n_chunks = tm // CHUNK
    n_i = S // tm
    ones = _group_ones()
    nt = (((1,), (1,)), ((), ()))

    def body(ug_ref, dsg_ref, gain_ref, w_ref, wt_ref, bias_ref, ones_ref, dug_ref, dw_ref, dgain_ref, dbias_ref,
             dbacc_ref):
        i = pl.program_id(0)

        @pl.when(i == 0)
        def _():
            dw_ref[...] = jnp.zeros_like(dw_ref)
            dgain_ref[...] = jnp.zeros_like(dgain_ref)
            dbacc_ref[...] = jnp.zeros_like(dbacc_ref)

        ones_m = ones_ref[...]
        pu = ug_ref[:, :GMLP_WIDTH]
        pg = ug_ref[:, GMLP_WIDTH:]
        u = _gelu(pu)
        vr = _gelu(pg)
        ms = _dot3(vr * vr, ones_m) * (1.0 / GROUP_DIM)
        rinv = lax.rsqrt(ms + EPS)
        vhat = vr * rinv
        gain_v = gain_ref[...]
        vn = (vhat * gain_v).astype(BF16)
        mixed = _gmlp_mixed(vn, w_ref, bias_ref[...], n_chunks)
        dsg_v = dsg_ref[...]
        du = dsg_v * mixed
        dmixed = dsg_v * u
        dm_bf = dmixed.astype(BF16)
        lane = lax.broadcasted_iota(jnp.int32, (CHUNK, LANES), 1)
        row = lax.broadcasted_iota(jnp.int32, (CHUNK, CHUNK), 0)
        col = lax.broadcasted_iota(jnp.int32, (CHUNK, CHUNK), 1)
        wts = [jnp.where(col >= row, wt_ref[g], 0.0).astype(BF16) for g in range(N_GROUPS)]
        dvn_rows = []
        dbsum = jnp.zeros((CHUNK, GMLP_WIDTH), F32)
        for ci in range(n_chunks):
            rs = slice(ci * CHUNK, (ci + 1) * CHUNK)
            dbsum = dbsum + dmixed[rs, :]
            cols = []
            for pp in range(N_GROUPS // 2):
                cs = slice(pp * LANES, (pp + 1) * LANES)
                dm = dm_bf[rs, cs]
                dm_lo = jnp.where(lane < GROUP_DIM, dm, jnp.zeros_like(dm))
                dm_hi = jnp.where(lane >= GROUP_DIM, dm, jnp.zeros_like(dm))
                vb = vn[rs, cs]
                dw_ref[2 * pp] += lax.dot_general(dm_lo, vb, nt, preferred_element_type=F32)
                dw_ref[2 * pp + 1] += lax.dot_general(dm_hi, vb, nt, preferred_element_type=F32)
                cols.append(jnp.dot(wts[2 * pp], dm_lo, preferred_element_type=F32)
                            + jnp.dot(wts[2 * pp + 1], dm_hi, preferred_element_type=F32))
            dvn_rows.append(jnp.concatenate(cols, axis=1))
        dvn = jnp.concatenate(dvn_rows, axis=0)
        dbacc_ref[...] += dbsum
        dgain_ref[0:1, :] += jnp.sum(dvn * vhat, axis=0, keepdims=True)
        dvhat = dvn * gain_v
        gm = _dot3(dvhat * vhat, ones_m) * (1.0 / GROUP_DIM)
        dvr = rinv * (dvhat - vhat * gm)
        dug_ref[:, :GMLP_WIDTH] = (du * _gelu_grad(pu)).astype(BF16)
        dug_ref[:, GMLP_WIDTH:] = (dvr * _gelu_grad(pg)).astype(BF16)

        @pl.when(i == n_i - 1)
        def _():
            for g in range(N_GROUPS):
                dw_ref[g] = jnp.where(row >= col, dw_ref[g], 0.0)
            dbias_ref[...] = _dot3(dbacc_ref[...], ones_m)

    return pl.pallas_call(
        body,
        out_shape=(jax.ShapeDtypeStruct((S, 2 * GMLP_WIDTH), BF16), jax.ShapeDtypeStruct((N_GROUPS, CHUNK, CHUNK), F32),
                   jax.ShapeDtypeStruct((8, GMLP_WIDTH), F32), jax.ShapeDtypeStruct((CHUNK, GMLP_WIDTH), F32)),
        grid=(n_i,),
        in_specs=[pl.BlockSpec((tm, 2 * GMLP_WIDTH), lambda i: (i, 0)), pl.BlockSpec((tm, GMLP_WIDTH), lambda i: (i, 1)),
                  pl.BlockSpec((1, GMLP_WIDTH), lambda i: (0, 0)),
                  pl.BlockSpec((N_GROUPS, CHUNK, CHUNK), lambda i: (0, 0, 0)),
                  pl.BlockSpec((N_GROUPS, CHUNK, CHUNK), lambda i: (0, 0, 0)),
                  pl.BlockSpec((CHUNK, GMLP_WIDTH), lambda i: (0, 0)),
                  pl.BlockSpec((GMLP_WIDTH, GMLP_WIDTH), lambda i: (0, 0))],
        out_specs=(pl.BlockSpec((tm, 2 * GMLP_WIDTH), lambda i: (i, 0)),
                   pl.BlockSpec((N_GROUPS, CHUNK, CHUNK), lambda i: (0, 0, 0)),
                   pl.BlockSpec((8, GMLP_WIDTH), lambda i: (0, 0)),
                   pl.BlockSpec((CHUNK, GMLP_WIDTH), lambda i: (0, 0))),
        scratch_shapes=[pltpu.VMEM((CHUNK, GMLP_WIDTH), F32)],
        name="gmlp_bwd", compiler_params=_cp(("arbitrary",), VMEM_LIMIT),
    )(ug, dsg, gain, w_s, wt_s, bias_full, ones)


def _gate_bwd(dcq, dck, zf):
    S = zf.shape[0]
    tm = _pick(S, (256,))
    n_i = S // tm
    triu = (lax.broadcasted_iota(jnp.int32, (tm, tm), 0) <= lax.broadcasted_iota(jnp.int32, (tm, tm), 1)).astype(BF16)

    def body(dcq_ref, dck_ref, zf_ref, tri_ref, dzf_ref, dbf_ref, carry_ref):
        i = pl.program_id(0)

        @pl.when(i == 0)
        def _():
            carry_ref[...] = jnp.zeros_like(carry_ref)
            dbf_ref[...] = jnp.zeros_like(dbf_ref)

        lane = lax.broadcasted_iota(jnp.int32, (tm, LANES), 1)
        dc = jnp.zeros((tm, LANES), F32)
        for p in range(N_PAIRS):
            slab = dcq_ref[p] + dck_ref[p]
            for hh in range(2):
                dc = dc + jnp.where(lane == 2 * p + hh, slab[:, hh:hh + 1], 0.0)
        dlf = _dot3l(tri_ref[...], dc) + carry_ref[0:1, :]
        carry_ref[0:1, :] = dlf[0:1, :]
        dz = jnp.where(lane < N_HEADS, dlf * _sigmoid(-zf_ref[...]), 0.0)
        dzf_ref[...] = dz.astype(BF16)
        dbf_ref[0:1, :] += jnp.sum(dz, axis=0, keepdims=True)

    return pl.pallas_call(
        body,
        out_shape=(jax.ShapeDtypeStruct((S, LANES), BF16), jax.ShapeDtypeStruct((8, LANES), F32)),
        grid=(n_i,),
        in_specs=[pl.BlockSpec((N_PAIRS, tm, LANES), lambda i: (0, n_i - 1 - i, 0)),
                  pl.BlockSpec((N_PAIRS, tm, LANES), lambda i: (0, n_i - 1 - i, 0)),
                  pl.BlockSpec((tm, LANES), lambda i: (n_i - 1 - i, 0)),
                  pl.BlockSpec((tm, tm), lambda i: (0, 0))],
        out_specs=(pl.BlockSpec((tm, LANES), lambda i: (n_i - 1 - i, 0)), pl.BlockSpec((8, LANES), lambda i: (0, 0))),
        scratch_shapes=[pltpu.VMEM((8, LANES), F32)],
        name="gate_bwd", compiler_params=_cp(("arbitrary",), VMEM_LIMIT),
    )(dcq, dck, zf, triu)


def _out_proj_fwd(att_bf, sg, w_out_bf, x, g_ffn):
    S = x.shape[0]
    tm = _pick(S, (512, 256))

    def body(a_ref, s_ref, w_ref, x_ref, g_ref, h_ref, hn_ref):
        h = (x_ref[...] + jnp.dot(a_ref[...], w_ref[:ATT_WIDTH, :], preferred_element_type=F32)
             + jnp.dot(s_ref[...], w_ref[ATT_WIDTH:, :], preferred_element_type=F32))
        h_ref[...] = h
        r = lax.rsqrt(jnp.mean(h * h, axis=-1, keepdims=True) + EPS)
        hn_ref[...] = ((h * r) * g_ref[...]).astype(BF16)

    row = pl.BlockSpec((tm, D_MODEL), lambda i: (i, 0))
    half = pl.BlockSpec((tm, ATT_WIDTH), lambda i: (i, 0))
    return pl.pallas_call(
        body, out_shape=(jax.ShapeDtypeStruct((S, D_MODEL), F32), jax.ShapeDtypeStruct((S, D_MODEL), BF16)),
        grid=(S // tm,),
        in_specs=[half, half, pl.BlockSpec((D_MODEL, D_MODEL), lambda i: (0, 0)), row,
                  pl.BlockSpec((1, D_MODEL), lambda i: (0, 0))],
        out_specs=(row, row), name="out_proj", compiler_params=_cp(("parallel",), VMEM_LIMIT),
    )(att_bf, sg, w_out_bf, x, g_ffn)


_IN_PIECES = ((0, ATT_WIDTH), (ATT_WIDTH, ATT_WIDTH), (2 * ATT_WIDTH, ATT_WIDTH), (QKV, 2 * GMLP_WIDTH), (UG_END, LANES))


def _inproj_bwd_dx(pieces, w_pad, x, g_mix, dh1):
    S = x.shape[0]
    tm = _pick(S, (512, 256))

    def body(*refs):
        p_refs, (w_ref, x_ref, g_ref, r_ref, dx_ref, dg_ref) = refs[:5], refs[5:]
        i = pl.program_id(0)

        @pl.when(i == 0)
        def _():
            dg_ref[...] = jnp.zeros_like(dg_ref)

        dxn = None
        for p_ref, (c0, width) in zip(p_refs, _IN_PIECES):
            part = lax.dot_general(p_ref[...], w_ref[:, c0:c0 + width], _NT, preferred_element_type=F32)
            dxn = part if dxn is None else dxn + part
        xf = x_ref[...]
        r = lax.rsqrt(jnp.mean(xf * xf, axis=-1, keepdims=True) + EPS)
        xhat = xf * r
        dg_ref[0:1, :] += jnp.sum(dxn * xhat, axis=0, keepdims=True)
        dhat = dxn * g_ref[...]
        dx_ref[...] = r_ref[...] + r * (dhat - xhat * jnp.mean(dhat * xhat, axis=-1, keepdims=True))

    row = pl.BlockSpec((tm, D_MODEL), lambda i: (i, 0))
    return pl.pallas_call(
        body, out_shape=(jax.ShapeDtypeStruct((S, D_MODEL), F32), jax.ShapeDtypeStruct((8, D_MODEL), F32)),
        grid=(S // tm,),
        in_specs=[pl.BlockSpec((tm, width), lambda i: (i, 0)) for _, width in _IN_PIECES]
        + [pl.BlockSpec((D_MODEL, IN_PAD), lambda i: (0, 0)), row, pl.BlockSpec((1, D_MODEL), lambda i: (0, 0)), row],
        out_specs=(row, pl.BlockSpec((8, D_MODEL), lambda i: (0, 0))),
        name="in_proj_dx", compiler_params=_cp(("arbitrary",), VMEM_LIMIT),
    )(*pieces, w_pad, x, g_mix, dh1)


def _inproj_bwd_dw(xn, pieces):
    S = xn.shape[0]
    tk = _pick(S, (512, 256))

    def body(*refs):
        x_ref, p_refs, o_ref = refs[0], refs[1:6], refs[6]
        k = pl.program_id(0)

        @pl.when(k == 0)
        def _():
            o_ref[...] = jnp.zeros_like(o_ref)

        xb = x_ref[...]
        for p_ref, (c0, width) in zip(p_refs, _IN_PIECES):
            o_ref[:, c0:c0 + width] += lax.dot_general(xb, p_ref[...], _TN, preferred_element_type=F32)

    return pl.pallas_call(
        body, out_shape=jax.ShapeDtypeStruct((D_MODEL, IN_PAD), F32), grid=(S // tk,),
        in_specs=[pl.BlockSpec((tk, D_MODEL), lambda k: (k, 0))]
        + [pl.BlockSpec((tk, width), lambda k: (k, 0)) for _, width in _IN_PIECES],
        out_specs=pl.BlockSpec((D_MODEL, IN_PAD), lambda k: (0, 0)),
        name="in_proj_dw", compiler_params=_cp(("arbitrary",), VMEM_LIMIT),
    )(xn, *pieces)


def _adamw(w, m, v, parts, name):
    R, C = w.shape
    tr = R
    for cand in (256, 128, 64, 32, 16, 8):
        if R % cand == 0 and R > cand:
            tr = cand
            break
    c1 = 1.0 / (1.0 - ADAM_B1 ** ADAM_STEP)
    c2 = 1.0 / (1.0 - ADAM_B2 ** ADAM_STEP)

    def body(w_ref, m_ref, v_ref, p_ref, g_ref, d_ref, nm_ref, nv_ref):
        g = p_ref[0]
        for j in range(1, N_DEV):
            g = g + p_ref[j]
        g_ref[...] = g
        nm = ADAM_B1 * m_ref[...] + (1.0 - ADAM_B1) * g
        nv = ADAM_B2 * v_ref[...] + (1.0 - ADAM_B2) * (g * g)
        nm_ref[...] = nm
        nv_ref[...] = nv
        d_ref[...] = -ADAM_LR * ((nm * c1) / (jnp.sqrt(nv * c2) + ADAM_EPS) + ADAM_WD * w_ref[...])

    spec = pl.BlockSpec((tr, C), lambda i: (i, 0))
    shp = jax.ShapeDtypeStruct((R, C), F32)
    return pl.pallas_call(
        body, out_shape=(shp, shp, shp, shp), grid=(R // tr,),
        in_specs=[spec, spec, spec, pl.BlockSpec((N_DEV, tr, C), lambda i: (0, i, 0))],
        out_specs=(spec, spec, spec, spec),
        name=name, compiler_params=_cp(("parallel",), VMEM_LIMIT),
    )(w, m, v, parts)


def _place():
    x, y, c = lax.axis_index("x"), lax.axis_index("y"), lax.axis_index("c")
    return x, y, c


def _all_gather(blocks, name):
    n = len(blocks)

    def body(*refs):
        ins, outs = refs[:n], refs[n:2 * n]
        send_sems, recv_sems, local_sems = refs[2 * n:]
        x, y, c = _place()
        me, sibling = (x, y, c), (x, y, 1 - c)
        chips = [(1 - x, y), (x, 1 - y), (1 - x, 1 - y)]
        sends = []
        for a in range(n):
            out = outs[a]

            def slot(px, py, pc, out=out):
                return out.at[4 * px + 2 * py + pc]

            def copy(k, block, to, src=None, a=a, slot=slot):
                return pltpu.make_async_remote_copy(
                    src_ref=slot(*block) if src is None else src, dst_ref=slot(*block),
                    send_sem=send_sems.at[a, k], recv_sem=recv_sems.at[a, k], device_id=to, device_id_type=MESH)

            mine = pltpu.make_async_copy(ins[a], slot(*me), local_sems.at[a])
            mine.start()
            first = [copy(0, me, sibling, src=ins[a])]
            first += [copy(1 + j, me, (*chip, c), src=ins[a]) for j, chip in enumerate(chips)]
            for cp in first:
                cp.start()
            sends.append((mine, first, copy))
        for a in range(n):
            mine, first, copy = sends[a]
            passed = [copy(4 + j, (*chip, c), sibling) for j, chip in enumerate(chips)]
            for j, chip in enumerate(chips):
                copy(1 + j, (*chip, c), me).wait_recv()
                passed[j].start()
            copy(0, sibling, me).wait_recv()
            for j, chip in enumerate(chips):
                copy(4 + j, (*chip, 1 - c), me).wait_recv()
            for cp in first + passed:
                cp.wait_send()
            mine.wait()

    any_spec = pl.BlockSpec(memory_space=pl.ANY)
    return pl.pallas_call(
        body, out_shape=tuple(jax.ShapeDtypeStruct((N_DEV,) + b.shape, b.dtype) for b in blocks),
        in_specs=[any_spec] * n, out_specs=tuple([any_spec] * n),
        scratch_shapes=[pltpu.SemaphoreType.DMA((n, 7)), pltpu.SemaphoreType.DMA((n, 7)), pltpu.SemaphoreType.DMA((n,))],
        name=name,
    )(*blocks)


def _exchange_shards(parts, name):
    n = len(parts)

    def body(*refs):
        ins, outs = refs[:n], refs[n:2 * n]
        send_sems, recv_sems, local_sems = refs[2 * n:]
        x, y, c = _place()
        me = 4 * x + 2 * y + c
        started = []
        for a in range(n):
            mine = pltpu.make_async_copy(ins[a].at[me], outs[a].at[me], local_sems.at[a])
            mine.start()
            started.append(mine)
            for k in range(1, N_DEV):
                px, py, pc = x ^ ((k >> 2) & 1), y ^ ((k >> 1) & 1), c ^ (k & 1)
                cp = pltpu.make_async_remote_copy(
                    src_ref=ins[a].at[4 * px + 2 * py + pc], dst_ref=outs[a].at[me],
                    send_sem=send_sems.at[a, k - 1], recv_sem=recv_sems.at[a, k - 1],
                    device_id=(px, py, pc), device_id_type=MESH)
                cp.start()
                started.append(cp)
        for cp in started:
            cp.wait()

    any_spec = pl.BlockSpec(memory_space=pl.ANY)
    return pl.pallas_call(
        body, out_shape=tuple(jax.ShapeDtypeStruct(p.shape, p.dtype) for p in parts),
        in_specs=[any_spec] * n, out_specs=tuple([any_spec] * n),
        scratch_shapes=[pltpu.SemaphoreType.DMA((n, 7)), pltpu.SemaphoreType.DMA((n, 7)), pltpu.SemaphoreType.DMA((n,))],
        name=name,
    )(*parts)


_HBM = pl.BlockSpec(memory_space=pltpu.HBM)
_SEM = pl.BlockSpec(memory_space=pltpu.SEMAPHORE)
_EFFECT = pltpu.SideEffectType.DATAFLOW_SIDE_EFFECTING


def _peers(x, y, c):
    out = []
    for k in range(1, N_DEV):
        px, py, pc = x ^ ((k >> 2) & 1), y ^ ((k >> 1) & 1), c ^ (k & 1)
        out.append((k, (px, py, pc), 4 * px + 2 * py + pc))
    return out


def _xchg_copies(src_refs, land_refs, send_sems, recv_sems, scatter):
    x, y, c = _place()
    me = 4 * x + 2 * y + c
    copies = []
    for a, (src, land) in enumerate(zip(src_refs, land_refs)):
        for k, place, idx in _peers(x, y, c):
            j = a * (N_DEV - 1) + k - 1
            copies.append(pltpu.make_async_remote_copy(
                src_ref=src.at[idx] if scatter[a] else src, dst_ref=land.at[me],
                send_sem=send_sems[j], recv_sem=recv_sems[j], device_id=place, device_id_type=MESH))
    return copies


def _xchg_start(srcs, scatter, name):
    n = len(srcs)
    lands = [lax.empty((N_DEV,) + (s.shape[1:] if sc else s.shape), s.dtype) for s, sc in zip(srcs, scatter)]

    ns = n * (N_DEV - 1)

    def body(*refs):
        sems = refs[2 * n:2 * n + 2 * ns]
        for cp in _xchg_copies(refs[:n], refs[n:2 * n], sems[:ns], sems[ns:], scatter):
            cp.start()
        token = refs[-1]
        token[...] = jnp.zeros_like(token)

    both = list(srcs) + lands
    res = pl.pallas_call(
        body, name=name,
        out_shape=(*[pltpu.SemaphoreType.DMA(())] * (2 * ns),
                   *[pltpu.HBM(a.shape, a.dtype) for a in both], jax.ShapeDtypeStruct((8, LANES), F32)),
        in_specs=[_HBM] * (2 * n),
        out_specs=(*([_SEM] * (2 * ns)), *([_HBM] * (2 * n)), pl.BlockSpec(memory_space=pltpu.VMEM)),
        input_output_aliases={i: 2 * ns + i for i in range(2 * n)},
        compiler_params=pltpu.CompilerParams(has_side_effects=_EFFECT),
    )(*[pltpu.with_memory_space_constraint(a, pltpu.HBM) for a in both])
    return (tuple(res[:2 * ns]), tuple(res[2 * ns:2 * ns + 2 * n])), res[-1]


def _xchg_wait(handle, scatter, after, name):
    sems, thru = handle
    n = len(thru) // 2
    ns = len(sems) // 2

    def body(*refs):
        got = refs[2 * n:2 * n + 2 * ns]
        for cp in _xchg_copies(refs[:n], refs[n:2 * n], got[:ns], got[ns:], scatter):
            cp.wait_send()
            cp.wait_recv()

    outs = pl.pallas_call(
        body, name=name, out_shape=tuple(pltpu.HBM(a.shape, a.dtype) for a in thru),
        in_specs=[_HBM] * (2 * n) + [_SEM] * (2 * ns) + [pl.BlockSpec(memory_space=pl.ANY)],
        out_specs=tuple([_HBM] * (2 * n)), input_output_aliases={i: i for i in range(2 * n)},
        compiler_params=pltpu.CompilerParams(has_side_effects=_EFFECT),
    )(*thru, *sems, after)
    return outs[:n], outs[n:]


def _tie(a, token):
    return a if token is None else a + token[0, 0]


def _rows128(a):
    flat = a.reshape(-1)
    rows = -(-flat.shape[0] // LANES)
    rows = -(-rows // 8) * 8
    return jnp.pad(flat, (0, rows * LANES - flat.shape[0])).reshape(rows, LANES)


def _local_step(x, target, norm_mix_g, w_in_bf, b_forget, gmlp_norm_g, w_spatial, b_spatial, norm_ffn_g, conv_b,
                norm_final_g, rest_fn, send_fn, token=None):
    f = D_FF
    g_mix = norm_mix_g.reshape(1, D_MODEL)
    w_pad = jnp.pad(w_in_bf, ((0, 0), (0, IN_PAD - IN_COLS)))
    bf_pad = jnp.pad(b_forget.reshape(1, N_HEADS), ((0, 0), (0, LANES - N_HEADS)))
    xn, qa, ka, va, ug, zf = _inproj_fwd(x, _tie(g_mix, token), w_pad, bf_pad)
    bias_full = jnp.repeat(b_spatial.reshape(N_GROUPS, CHUNK).T, GROUP_DIM, axis=1)
    w_s = w_spatial.reshape(N_GROUPS, CHUNK, CHUNK)
    gain = gmlp_norm_g.reshape(1, GMLP_WIDTH)
    sg = _gmlp_fwd(ug, gain, w_s, bias_full)
    att, lse, att_bf = _attn_fwd(qa, ka, va)
    w_out_bf, w_up_bf, conv_w, w_down_bf = rest_fn(att_bf)
    g_ffn = norm_ffn_g.reshape(1, D_MODEL)
    h1, hn = _out_proj_fwd(att_bf, sg, w_out_bf, x, g_ffn)
    hu = _mm(hn, w_up_bf, mode="nn", out_dtype=F32, tm=512, tn=1408, tk=1024, out_halves=True, outer="j", name="ffn_up")
    cw = jnp.pad(conv_w.reshape(3, 2, f).transpose(1, 0, 2), ((0, 0), (0, 5), (0, 0)))
    cb = conv_b.reshape(2, 1, f)
    act = _conv_act_fwd(hu, cw, cb)
    h2 = _mm(act, w_down_bf, mode="nn", out_dtype=F32, tm=512, tn=1024, tk=1408, res=h1, name="ffn_down")
    loss_blk, dh2, dh2_bf, dg_final = _loss_head(h2, norm_final_g.reshape(1, D_MODEL), target)
    dact = _mm(dh2_bf, w_down_bf, mode="nt", out_dtype=F32, tm=512, tn=1408, tk=1024, outer="j", name="ffn_down_dx")
    dw_down = _mm(act, dh2_bf, mode="tn", out_dtype=F32, tm=1408, tn=1024, tk=1024, name="ffn_down_dw")
    dhu, dcw = _conv_act_bwd(hu, dact, cw, _tie(cb, send_fn("w_down", dw_down)))
    dhn = _mm(dhu, w_up_bf, mode="nt", out_dtype=F32, tm=1024, tn=1024, tk=1408, a_halves=True, name="ffn_up_dx")
    dw_up = _mm(hn, dhu, mode="tn", out_dtype=F32, tm=1024, tn=1408, tk=1024, b_halves=True, outer="j", name="ffn_up_dw")
    dh1, dh1_bf, dg_ffn = _rms_bwd(h1, _tie(g_ffn, send_fn("w_up", dw_up)), dhn, dh2)
    dmix = _mm(dh1_bf, w_out_bf, mode="nt", out_dtype=F32, tm=512, tn=1024, tk=1024, name="out_proj_dx")
    dw_out = jnp.concatenate(
        [_mm(att_bf, dh1_bf, mode="tn", out_dtype=F32, tm=512, tn=1024, tk=1024, name="out_proj_dw_att"),
         _mm(sg, dh1_bf, mode="tn", out_dtype=F32, tm=512, tn=1024, tk=1024, name="out_proj_dw_sg")], axis=0)
    qb, doa = _attn_prep(att, lse, dmix, qa)
    dq, dk, dv, dcq, dck = _attn_bwd(qb, ka, va, doa)
    wt_s = w_s.transpose(0, 2, 1)
    dug, dw_s, dgain, dbias = _gmlp_bwd(ug, dmix, _tie(gain, send_fn("w_out", dw_out)), w_s, wt_s, bias_full)
    dzf, dbf = _gate_bwd(dcq, dck, zf)
    pieces = (dq, dk, dv, dug, dzf)
    dw_in = _inproj_bwd_dw(xn, pieces)
    grad_x, dg_mix = _inproj_bwd_dx(pieces, w_pad, x, _tie(g_mix, send_fn("w_in", dw_in[:, :IN_COLS])), dh1)
    grads = dict(
        norm_mix_g=dg_mix[0:1, :],
        b_forget=dbf[0:1, :N_HEADS],
        gmlp_norm_g=dgain[0:1, :],
        w_spatial=dw_s,
        b_spatial=dbias[:, ::GROUP_DIM].T,
        norm_ffn_g=dg_ffn[0:1, :],
        conv_w=dcw[:, 0:3, :].transpose(1, 0, 2).reshape(3, 2 * f),
        conv_b=dcw[:, 3, :].reshape(1, 2 * f),
        norm_final_g=dg_final[0, :],
    )
    return loss_blk[0, 0], grad_x, grads


SMALL = ("norm_mix_g", "b_forget", "gmlp_norm_g", "w_spatial", "b_spatial", "norm_ffn_g", "conv_b", "norm_final_g")


def kernel(x, norm_mix_g, w_in, b_forget, gmlp_norm_g, w_spatial, b_spatial, w_out, norm_ffn_g, w_up, conv_w, conv_b, w_down, norm_final_g, loss_target, m_norm_mix_g, m_w_in, m_b_forget, m_gmlp_norm_g, m_w_spatial, m_b_spatial, m_w_out, m_norm_ffn_g, m_w_up, m_conv_w, m_conv_b, m_w_down, m_norm_final_g, v_norm_mix_g, v_w_in, v_b_forget, v_gmlp_norm_g, v_w_spatial, v_b_spatial, v_w_out, v_norm_ffn_g, v_w_up, v_conv_w, v_conv_b, v_w_down, v_norm_final_g):
    weights = dict(norm_mix_g=norm_mix_g, w_in=w_in, b_forget=b_forget, gmlp_norm_g=gmlp_norm_g, w_spatial=w_spatial,
                   b_spatial=b_spatial, w_out=w_out, norm_ffn_g=norm_ffn_g, w_up=w_up, conv_w=conv_w, conv_b=conv_b,
                   w_down=w_down, norm_final_g=norm_final_g)
    m_in = dict(norm_mix_g=m_norm_mix_g, w_in=m_w_in, b_forget=m_b_forget, gmlp_norm_g=m_gmlp_norm_g,
                w_spatial=m_w_spatial, b_spatial=m_b_spatial, w_out=m_w_out, norm_ffn_g=m_norm_ffn_g, w_up=m_w_up,
                conv_w=m_conv_w, conv_b=m_conv_b, w_down=m_w_down, norm_final_g=m_norm_final_g)
    v_in = dict(norm_mix_g=v_norm_mix_g, w_in=v_w_in, b_forget=v_b_forget, gmlp_norm_g=v_gmlp_norm_g,
                w_spatial=v_w_spatial, b_spatial=v_b_spatial, w_out=v_w_out, norm_ffn_g=v_norm_ffn_g, w_up=v_w_up,
                conv_w=v_conv_w, conv_b=v_conv_b, w_down=v_w_down, norm_final_g=v_norm_final_g)
    order = list(weights)
    me = 4 * lax.axis_index("x") + 2 * lax.axis_index("y") + lax.axis_index("c")
    n_in, n_up = w_in.shape[2], w_up.shape[2]
    r_out, r_down = w_out.shape[1], w_down.shape[1]

    def with_mine(landed, mine):
        return lax.dynamic_update_index_in_dim(landed, mine, me, 0)

    up_blk = w_up[0].astype(BF16)
    rows_blk = jnp.concatenate([w_out[0].astype(BF16), w_down[0].astype(BF16)], axis=0)
    taps_blk = jnp.pad(conv_w[0], ((0, 5), (0, 0)))
    rest_blocks = [up_blk, rows_blk, taps_blk]
    rest_handle, token = _xchg_start(rest_blocks, [False] * 3, "gather_rest_start")
    (in_all,) = _all_gather([_tie(w_in[0], token).astype(BF16)], "gather_w_in")
    w_in_bf = in_all.transpose(1, 0, 2).reshape(D_MODEL, N_DEV * n_in)

    def rest_fn(after):
        mine, landed = _xchg_wait(rest_handle, [False] * 3, after, "gather_rest_wait")
        up_all, rows_all, taps_all = [with_mine(l, b) for l, b in zip(landed, mine)]
        return (rows_all[:, :r_out, :].reshape(N_DEV * r_out, D_MODEL),
                up_all.transpose(1, 0, 2).reshape(D_MODEL, N_DEV * n_up),
                taps_all[:, :3, :].transpose(1, 0, 2).reshape(3, N_DEV * n_up),
                rows_all[:, r_out:, :].reshape(N_DEV * r_down, D_MODEL))

    sent = {}

    def send_fn(name, grad):
        if name in ("w_in", "w_up"):
            parts = grad.reshape(D_MODEL, N_DEV, -1).transpose(1, 0, 2)
        else:
            parts = grad.reshape(N_DEV, -1, D_MODEL)
        sent[name], tok = _xchg_start([parts], [True], "scatter_" + name + "_start")
        return tok

    loss_local, grad_x, g = _local_step(
        x[0], loss_target[0], norm_mix_g, w_in_bf, b_forget, gmlp_norm_g, w_spatial, b_spatial, norm_ffn_g, conv_b,
        norm_final_g, rest_fn, send_fn, token)
    loss = lax.psum(loss_local, ("x", "y", "c"))

    got = {}
    for name, handle in sent.items():
        (parts,), (landed,) = _xchg_wait(handle, [True], grad_x, "scatter_" + name + "_wait")
        got[name] = with_mine(landed, lax.dynamic_index_in_dim(parts, me, 0, keepdims=False))
    got_in, got_up, got_out, got_down = got["w_in"], got["w_up"], got["w_out"], got["w_down"]

    small_names = SMALL + ("conv_w",)
    packed = [_rows128(g[k]) for k in small_names]
    sizes = [p.shape[0] for p in packed]
    (small_all,) = _all_gather([jnp.concatenate(packed, axis=0)], "gather_small_grads")

    def pack(src):
        return jnp.concatenate([_rows128(src[k]) for k in SMALL], axis=0)

    n_small_rows = sum(sizes[:-1])
    sg_, sd_, sm_, sv_ = _adamw(pack(weights), pack(m_in), pack(v_in), small_all[:, :n_small_rows, :], "adamw_small")

    outs = {}
    off = 0
    for k, rows in zip(SMALL, sizes[:-1]):
        shp = weights[k].shape
        cnt = math.prod(shp)
        outs[k] = tuple(a[off:off + rows].reshape(-1)[:cnt].reshape(shp) for a in (sg_, sd_, sm_, sv_))
        off += rows
    taps_parts = small_all[:, n_small_rows:, :].reshape(N_DEV, -1)[:, :3 * N_DEV * n_up].reshape(N_DEV, 3, N_DEV * n_up)
    taps_mine = lax.dynamic_slice_in_dim(taps_parts, me * n_up, n_up, axis=2)
    taps_mine = jnp.pad(taps_mine, ((0, 0), (0, 5), (0, 0)))

    def pad8(a):
        return jnp.pad(a[0], ((0, 5), (0, 0)))

    res = _adamw(pad8(conv_w), pad8(m_conv_w), pad8(v_conv_w), taps_mine, "adamw_conv_w")
    outs["conv_w"] = tuple(a[:3][None] for a in res)
    for k, got in (("w_in", got_in), ("w_up", got_up), ("w_out", got_out), ("w_down", got_down)):
        res = _adamw(weights[k][0], m_in[k][0], v_in[k][0], got, "adamw_" + k)
        outs[k] = tuple(a[None] for a in res)

    return (loss, grad_x[None], *[outs[k][0] for k in order], *[outs[k][1] for k in order],
            *[outs[k][2] for k in order], *[outs[k][3] for k in order])
```

```python
import functools
import math

import jax
import jax.numpy as jnp
from jax import lax
from jax.experimental import pallas as pl
from jax.experimental.pallas import tpu as pltpu

F32 = jnp.float32
BF16 = jnp.bfloat16

N_DEV = 8
D_MODEL = 1024
ATT_WIDTH = 512
GMLP_WIDTH = 512
HEAD_DIM = 64
N_HEADS = 8
N_PAIRS = 4
N_GROUPS = 8
GROUP_DIM = 64
CHUNK = 128
D_FF = 2816
IN_COLS = 2568
IN_PAD = 2688
QKV = 1536
UG_END = 2560
EPS = 1e-6
LANES = 128

ADAM_LR = 0.001
ADAM_B1 = 0.9
ADAM_B2 = 0.999
ADAM_EPS = 1e-08
ADAM_WD = 0.01
ADAM_STEP = 10

VMEM_LIMIT = 56 * 1024 * 1024
MESH = pl.DeviceIdType.MESH


def _cp(sem, vmem=None):
    return pltpu.CompilerParams(dimension_semantics=sem, vmem_limit_bytes=vmem)


def _pick(n, prefs):
    for p in prefs:
        if n % p == 0:
            return p
    return n


def _split3(x):
    hi = x.astype(BF16)
    r1 = x - hi.astype(F32)
    mid = r1.astype(BF16)
    lo = (r1 - mid.astype(F32)).astype(BF16)
    return hi, mid, lo


def _dot3(x, ones_bf):
    hi, mid, lo = _split3(x)
    d = functools.partial(jnp.dot, preferred_element_type=F32)
    return d(hi, ones_bf) + d(mid, ones_bf) + d(lo, ones_bf)


def _dot3l(ones_bf, x):
    hi, mid, lo = _split3(x)
    d = functools.partial(jnp.dot, preferred_element_type=F32)
    return d(ones_bf, hi) + d(ones_bf, mid) + d(ones_bf, lo)


def _gelu(x):
    k = math.sqrt(2.0 / math.pi)
    t = jnp.tanh(k * (x + 0.044715 * (x * x * x)))
    return 0.5 * x * (1.0 + t)


def _gelu_grad(x):
    k = math.sqrt(2.0 / math.pi)
    x2 = x * x
    t = jnp.tanh(k * (x + 0.044715 * (x2 * x)))
    return 0.5 * (1.0 + t) + 0.5 * x * (1.0 - t * t) * (k * (1.0 + 3.0 * 0.044715 * x2))


def _sigmoid(x):
    return 1.0 / (1.0 + jnp.exp(-x))


def _mm(a, b, *, mode, out_dtype, tm, tn, tk, name, res=None, a_halves=False, b_halves=False,
        out_halves=False, outer="i"):
    if mode == "tn":
        K, M = a.shape[-2], a.shape[-1] * (2 if a_halves else 1)
    else:
        M, K = a.shape[-2], a.shape[-1] * (2 if a_halves else 1)
    if mode == "nt":
        N = b.shape[-2]
        assert b.shape[-1] == K
    else:
        N = b.shape[-1] * (2 if b_halves else 1)
    tm, tn, tk = min(tm, M), min(tn, N), min(tk, K)
    assert M % tm == 0 and N % tn == 0 and K % tk == 0, (name, M, N, K, tm, tn, tk)
    nm, nn, nk = M // tm, N // tn, K // tk

    def ij(g0, g1):
        return (g0, g1) if outer == "i" else (g1, g0)

    if mode == "nn":
        dims = (((1,), (0,)), ((), ()))
        if a_halves:
            nkh = nk // 2
            a_spec = pl.BlockSpec((None, tm, tk), lambda g0, g1, k: (k // nkh, ij(g0, g1)[0], k % nkh))
        else:
            a_spec = pl.BlockSpec((tm, tk), lambda g0, g1, k: (ij(g0, g1)[0], k))
        b_spec = pl.BlockSpec((tk, tn), lambda g0, g1, k: (k, ij(g0, g1)[1]))
    elif mode == "nt":
        dims = (((1,), (1,)), ((), ()))
        if a_halves:
            nkh = nk // 2
            a_spec = pl.BlockSpec((None, tm, tk), lambda g0, g1, k: (k // nkh, ij(g0, g1)[0], k % nkh))
        else:
            a_spec = pl.BlockSpec((tm, tk), lambda g0, g1, k: (ij(g0, g1)[0], k))
        b_spec = pl.BlockSpec((tn, tk), lambda g0, g1, k: (ij(g0, g1)[1], k))
    else:
        dims = (((0,), (0,)), ((), ()))
        if a_halves:
            nmh = nm // 2
            a_spec = pl.BlockSpec((None, tk, tm), lambda g0, g1, k: (ij(g0, g1)[0] // nmh, k, ij(g0, g1)[0] % nmh))
        else:
            a_spec = pl.BlockSpec((tk, tm), lambda g0, g1, k: (k, ij(g0, g1)[0]))
        if b_halves:
            nnh = nn // 2
            b_spec = pl.BlockSpec((None, tk, tn), lambda g0, g1, k: (ij(g0, g1)[1] // nnh, k, ij(g0, g1)[1] % nnh))
        else:
            b_spec = pl.BlockSpec((tk, tn), lambda g0, g1, k: (k, ij(g0, g1)[1]))
    if out_halves:
        nnh = nn // 2
        o_spec = pl.BlockSpec((None, tm, tn), lambda g0, g1, k: (ij(g0, g1)[1] // nnh, ij(g0, g1)[0], ij(g0, g1)[1] % nnh))
        o_shape = jax.ShapeDtypeStruct((2, M, N // 2), out_dtype)
    else:
        o_spec = pl.BlockSpec((tm, tn), lambda g0, g1, k: ij(g0, g1))
        o_shape = jax.ShapeDtypeStruct((M, N), out_dtype)
    in_specs = [a_spec, b_spec]
    args = [a, b]
    if res is not None:
        in_specs.append(pl.BlockSpec((tm, tn), lambda g0, g1, k: ij(g0, g1)))
        args.append(res)

    def body(*refs):
        if res is not None:
            a_ref, b_ref, r_ref, o_ref = refs[:4]
        else:
            a_ref, b_ref, o_ref = refs[:3]
            r_ref = None
        part = lax.dot_general(a_ref[...], b_ref[...], dims, preferred_element_type=F32)
        if nk == 1:
            if r_ref is not None:
                part = part + r_ref[...]
            o_ref[...] = part.astype(out_dtype)
            return
        acc_ref = refs[-1]
        k = pl.program_id(2)

        @pl.when(k == 0)
        def _():
            acc_ref[...] = part

        @pl.when(k > 0)
        def _():
            acc_ref[...] += part

        @pl.when(k == nk - 1)
        def _():
            tot = acc_ref[...]
            if r_ref is not None:
                tot = tot + r_ref[...]
            o_ref[...] = tot.astype(out_dtype)

    grid = (nm, nn, nk) if outer == "i" else (nn, nm, nk)
    scratch = [] if nk == 1 else [pltpu.VMEM((tm, tn), F32)]
    return pl.pallas_call(
        body, out_shape=o_shape, grid=grid, in_specs=in_specs, out_specs=o_spec, scratch_shapes=scratch,
        name=name, compiler_params=_cp(("parallel", "parallel", "arbitrary"), VMEM_LIMIT),
    )(*args)


def _aug(lane, terms):
    out = 0.0
    for j, t in enumerate(terms):
        out = jnp.where(lane == HEAD_DIM + j, t, out)
    return out


def _split3f(x):
    hi, mid, lo = _split3(x)
    return [hi.astype(F32), mid.astype(F32), lo.astype(F32)]


def _inproj_fwd(x, g_mix, w_pad, bf_pad):
    S = x.shape[0]
    tm = _pick(S, (256,))
    tri = (lax.broadcasted_iota(jnp.int32, (tm, tm), 0) >= lax.broadcasted_iota(jnp.int32, (tm, tm), 1)).astype(BF16)

    def body(x_ref, g_ref, w_ref, bf_ref, tri_ref, xn_ref, qa_ref, ka_ref, va_ref, ug_ref, zf_ref, carry_ref):
        i = pl.program_id(0)

        @pl.when(i == 0)
        def _():
            carry_ref[...] = jnp.zeros_like(carry_ref)

        xf = x_ref[...]
        r = lax.rsqrt(jnp.mean(xf * xf, axis=-1, keepdims=True) + EPS)
        xn = ((xf * r) * g_ref[...]).astype(BF16)
        xn_ref[...] = xn
        proj = jnp.dot(xn, w_ref[...], preferred_element_type=F32)
        ug_ref[...] = proj[:, QKV:UG_END]
        zf = proj[:, UG_END:] + bf_ref[...]
        zf_ref[...] = zf
        lf = jnp.minimum(zf, 0.0) - jnp.log(1.0 + jnp.exp(-jnp.abs(zf)))
        c = _dot3l(tri_ref[...], lf) + carry_ref[0:1, :]
        carry_ref[0:1, :] = c[tm - 1:tm, :]
        c3 = _split3f(c)
        lane = lax.broadcasted_iota(jnp.int32, (tm, LANES), 1)
        ones3 = [1.0, 1.0, 1.0]
        for h in range(N_HEADS):
            p, odd = h // 2, h % 2
            ch = [t[:, h:h + 1] for t in c3]

            def head(base, scale=None, p=p, odd=odd):
                blk = proj[:, base + p * LANES:base + (p + 1) * LANES]
                if scale is not None:
                    blk = blk * scale
                return pltpu.roll(blk, HEAD_DIM, 1) if odd else blk

            cols = slice(h * LANES, (h + 1) * LANES)
            qa_ref[:, cols] = jnp.where(lane < HEAD_DIM, head(0, HEAD_DIM ** -0.5), _aug(lane, ch + ones3)).astype(BF16)
            ka_ref[:, cols] = jnp.where(lane < HEAD_DIM, head(ATT_WIDTH),
                                        _aug(lane, ones3 + [-t for t in ch] + ones3)).astype(BF16)
            va_ref[:, cols] = jnp.where(lane < HEAD_DIM, head(2 * ATT_WIDTH), _aug(lane, ones3)).astype(BF16)

    wide = N_HEADS * LANES
    return pl.pallas_call(
        body,
        out_shape=(jax.ShapeDtypeStruct((S, D_MODEL), BF16), jax.ShapeDtypeStruct((S, wide), BF16),
                   jax.ShapeDtypeStruct((S, wide), BF16), jax.ShapeDtypeStruct((S, wide), BF16),
                   jax.ShapeDtypeStruct((S, 2 * GMLP_WIDTH), F32), jax.ShapeDtypeStruct((S, LANES), F32)),
        grid=(S // tm,),
        in_specs=[pl.BlockSpec((tm, D_MODEL), lambda i: (i, 0)), pl.BlockSpec((1, D_MODEL), lambda i: (0, 0)),
                  pl.BlockSpec((D_MODEL, IN_PAD), lambda i: (0, 0)), pl.BlockSpec((1, LANES), lambda i: (0, 0)),
                  pl.BlockSpec((tm, tm), lambda i: (0, 0))],
        out_specs=(pl.BlockSpec((tm, D_MODEL), lambda i: (i, 0)), pl.BlockSpec((tm, wide), lambda i: (i, 0)),
                   pl.BlockSpec((tm, wide), lambda i: (i, 0)), pl.BlockSpec((tm, wide), lambda i: (i, 0)),
                   pl.BlockSpec((tm, 2 * GMLP_WIDTH), lambda i: (i, 0)), pl.BlockSpec((tm, LANES), lambda i: (i, 0))),
        scratch_shapes=[pltpu.VMEM((8, LANES), F32)],
        name="inproj_fwd", compiler_params=_cp(("arbitrary",), VMEM_LIMIT),
    )(x, g_mix, w_pad, bf_pad, tri)


def _group_ones():
    r = lax.broadcasted_iota(jnp.int32, (GMLP_WIDTH, GMLP_WIDTH), 0) // GROUP_DIM
    c = lax.broadcasted_iota(jnp.int32, (GMLP_WIDTH, GMLP_WIDTH), 1) // GROUP_DIM
    return (r == c).astype(BF16)


def _gmlp_mixed(vn_bf, w_ref, bias, n_chunks):
    lane = lax.broadcasted_iota(jnp.int32, (CHUNK, LANES), 1)
    row = lax.broadcasted_iota(jnp.int32, (CHUNK, CHUNK), 0)
    col = lax.broadcasted_iota(jnp.int32, (CHUNK, CHUNK), 1)
    ws = [jnp.where(row >= col, w_ref[g], 0.0).astype(BF16) for g in range(N_GROUPS)]
    rows = []
    for ci in range(n_chunks):
        cols = []
        for pp in range(N_GROUPS // 2):
            v = vn_bf[ci * CHUNK:(ci + 1) * CHUNK, pp * LANES:(pp + 1) * LANES]
            v_lo = jnp.where(lane < GROUP_DIM, v, jnp.zeros_like(v))
            v_hi = jnp.where(lane >= GROUP_DIM, v, jnp.zeros_like(v))
            m = (jnp.dot(ws[2 * pp], v_lo, preferred_element_type=F32)
                 + jnp.dot(ws[2 * pp + 1], v_hi, preferred_element_type=F32))
            cols.append(m + bias[:, pp * LANES:(pp + 1) * LANES])
        rows.append(jnp.concatenate(cols, axis=1))
    return jnp.concatenate(rows, axis=0)


def _gmlp_fwd(ug, gain, w_s, bias_full):
    S = ug.shape[0]
    tm = _pick(S, (512, 256, 128))
    ones = _group_ones()

    def body(ug_ref, gain_ref, w_ref, bias_ref, ones_ref, sg_ref):
        u = _gelu(ug_ref[:, :GMLP_WIDTH])
        vr = _gelu(ug_ref[:, GMLP_WIDTH:])
        ms = _dot3(vr * vr, ones_ref[...]) * (1.0 / GROUP_DIM)
        vn = ((vr * lax.rsqrt(ms + EPS)) * gain_ref[...]).astype(BF16)
        mixed = _gmlp_mixed(vn, w_ref, bias_ref[...], tm // CHUNK)
        sg_ref[...] = (u * mixed).astype(BF16)

    return pl.pallas_call(
        body, out_shape=jax.ShapeDtypeStruct((S, GMLP_WIDTH), BF16), grid=(S // tm,),
        in_specs=[pl.BlockSpec((tm, 2 * GMLP_WIDTH), lambda i: (i, 0)), pl.BlockSpec((1, GMLP_WIDTH), lambda i: (0, 0)),
                  pl.BlockSpec((N_GROUPS, CHUNK, CHUNK), lambda i: (0, 0, 0)),
                  pl.BlockSpec((CHUNK, GMLP_WIDTH), lambda i: (0, 0)),
                  pl.BlockSpec((GMLP_WIDTH, GMLP_WIDTH), lambda i: (0, 0))],
        out_specs=pl.BlockSpec((tm, GMLP_WIDTH), lambda i: (i, 0)),
        name="gmlp_fwd", compiler_params=_cp(("parallel",), VMEM_LIMIT),
    )(ug, gain, w_s, bias_full, ones)


_NT = (((1,), (1,)), ((), ()))
_TN = (((0,), (0,)), ((), ()))


def _attn_fwd(qa, ka, va):
    S = qa.shape[0]
    tq = _pick(S, (512, 256))
    tk = tq
    nq = S // tq

    def body(q_ref, k_ref, v_ref, o_ref, lse_ref, ob_ref):
        qi = pl.program_id(1)
        lane = lax.broadcasted_iota(jnp.int32, (tq, LANES), 1)
        rid = lax.broadcasted_iota(jnp.int32, (tq, tk), 0)
        cid = lax.broadcasted_iota(jnp.int32, (tq, tk), 1)
        qs = [q_ref[:, :LANES], q_ref[:, LANES:]]

        def update(kb, h, m, acc, masked):
            ks = pl.multiple_of(kb * tk, tk)
            cols = slice(h * LANES, (h + 1) * LANES)
            s = lax.dot_general(qs[h], k_ref[pl.ds(ks, tk), cols], _NT, preferred_element_type=F32)
            if masked:
                s = jnp.where(rid >= cid, s, -jnp.inf)
            m_new = jnp.maximum(m, jnp.max(s, axis=-1, keepdims=True))
            p = jnp.exp(s - m_new).astype(BF16)
            acc = jnp.exp(m - m_new) * acc + jnp.dot(p, v_ref[pl.ds(ks, tk), cols], preferred_element_type=F32)
            return m_new, acc

        def step(kb, carry):
            return tuple(update(kb, h, *carry[h], False) for h in range(2))

        one = (jnp.full((tq, 1), -jnp.inf, F32), jnp.zeros((tq, LANES), F32))
        carry = lax.fori_loop(0, qi, step, (one, one))
        outs, lses = [], []
        for h in range(2):
            m, acc = update(qi, h, *carry[h], True)
            l = acc[:, HEAD_DIM:HEAD_DIM + 1]
            outs.append(acc / l)
            lses.append(m + jnp.log(l))
        o = jnp.where(lane < HEAD_DIM, outs[0], pltpu.roll(outs[1], HEAD_DIM, 1))
        o_ref[...] = o
        ob_ref[...] = o.astype(BF16)
        lse_ref[...] = jnp.where(lane < HEAD_DIM, lses[0], lses[1])

    return pl.pallas_call(
        body,
        out_shape=(jax.ShapeDtypeStruct((S, ATT_WIDTH), F32), jax.ShapeDtypeStruct((S, ATT_WIDTH), F32),
                   jax.ShapeDtypeStruct((S, ATT_WIDTH), BF16)),
        grid=(N_PAIRS, nq),
        in_specs=[pl.BlockSpec((tq, 2 * LANES), lambda p, i: (i, p)),
                  pl.BlockSpec((S, 2 * LANES), lambda p, i: (0, p)),
                  pl.BlockSpec((S, 2 * LANES), lambda p, i: (0, p))],
        out_specs=(pl.BlockSpec((tq, LANES), lambda p, i: (i, p)), pl.BlockSpec((tq, LANES), lambda p, i: (i, p)),
                   pl.BlockSpec((tq, LANES), lambda p, i: (i, p))),
        name="attn_fwd", compiler_params=_cp(("parallel", "parallel"), VMEM_LIMIT),
    )(qa, ka, va)


def _rms_fwd(h, g):
    S = h.shape[0]
    tm = _pick(S, (512, 256))

    def body(h_ref, g_ref, o_ref):
        hf = h_ref[...]
        r = lax.rsqrt(jnp.mean(hf * hf, axis=-1, keepdims=True) + EPS)
        o_ref[...] = ((hf * r) * g_ref[...]).astype(BF16)

    return pl.pallas_call(
        body, out_shape=jax.ShapeDtypeStruct(h.shape, BF16), grid=(S // tm,),
        in_specs=[pl.BlockSpec((tm, D_MODEL), lambda i: (i, 0)), pl.BlockSpec((1, D_MODEL), lambda i: (0, 0))],
        out_specs=pl.BlockSpec((tm, D_MODEL), lambda i: (i, 0)),
        name="rms_fwd", compiler_params=_cp(("parallel",)),
    )(h, g)


def _shift_rows(x, prev, n):
    rid = lax.broadcasted_iota(jnp.int32, x.shape, 0)
    y = pltpu.roll(x, n, 0)
    if n == 1:
        return jnp.where(rid == 0, prev[7:8, :], y)
    return jnp.where(rid == 0, prev[6:7, :], jnp.where(rid == 1, prev[7:8, :], y))


def _shift_rows_up(x, nxt, n):
    rows = x.shape[0]
    rid = lax.broadcasted_iota(jnp.int32, x.shape, 0)
    y = pltpu.roll(x, rows - n, 0)
    if n == 1:
        return jnp.where(rid == rows - 1, nxt[0:1, :], y)
    return jnp.where(rid == rows - 2, nxt[0:1, :], jnp.where(rid == rows - 1, nxt[1:2, :], y))


def _conv3(cur, prev, w, b):
    return (w[0:1, :] * _shift_rows(cur, prev, 2) + w[1:2, :] * _shift_rows(cur, prev, 1)
            + w[2:3, :] * cur + b)


def _conv_act_fwd(hu, cw, cb):
    _, S, F = hu.shape
    tm = _pick(S, (512, 256))
    tn = _pick(F, (256, 128))
    r8 = tm // 8

    def body(cur_ref, prev_ref, w_ref, b_ref, o_ref):
        i = pl.program_id(1)
        halves = []
        for h in range(2):
            prev = jnp.where(i > 0, prev_ref[h], 0.0)
            halves.append(_conv3(cur_ref[h], prev, w_ref[h], b_ref[h]))
        a, g = halves
        o_ref[...] = (g * _sigmoid(g) * a).astype(BF16)

    return pl.pallas_call(
        body, out_shape=jax.ShapeDtypeStruct((S, F), BF16), grid=(F // tn, S // tm),
        in_specs=[pl.BlockSpec((2, tm, tn), lambda j, i: (0, i, j)),
                  pl.BlockSpec((2, 8, tn), lambda j, i: (0, jnp.maximum(i * r8 - 1, 0), j)),
                  pl.BlockSpec((2, 8, tn), lambda j, i: (0, 0, j)),
                  pl.BlockSpec((2, 1, tn), lambda j, i: (0, 0, j))],
        out_specs=pl.BlockSpec((tm, tn), lambda j, i: (i, j)),
        name="conv_act_fwd", compiler_params=_cp(("parallel", "parallel"), VMEM_LIMIT),
    )(hu, hu, cw, cb)


def _ffn_up_conv(hn, w_up_bf, cw, cb):
    S = hn.shape[0]
    F = D_FF
    tm = _pick(S, (1024, 512))
    tn = _pick(F, (256, 128))
    nj = F // tn

    def body(hn_ref, wa_ref, wg_ref, cw_ref, cb_ref, hu_ref, act_ref, tail_ref):
        i = pl.program_id(1)

        @pl.when(i == 0)
        def _():
            tail_ref[...] = jnp.zeros_like(tail_ref)

        hn_v = hn_ref[...]
        halves = []
        for h, w_ref in enumerate((wa_ref, wg_ref)):
            hu = jnp.dot(hn_v, w_ref[...], preferred_element_type=F32)
            hu_ref[h] = hu
            halves.append(_conv3(hu, tail_ref[h], cw_ref[h], cb_ref[h]))
            tail_ref[h] = hu[tm - 8:, :]
        a, g = halves
        act_ref[...] = (g * _sigmoid(g) * a).astype(BF16)

    return pl.pallas_call(
        body, out_shape=(jax.ShapeDtypeStruct((2, S, F), F32), jax.ShapeDtypeStruct((S, F), BF16)),
        grid=(nj, S // tm),
        in_specs=[pl.BlockSpec((tm, D_MODEL), lambda j, i: (i, 0)),
                  pl.BlockSpec((D_MODEL, tn), lambda j, i: (0, j)),
                  pl.BlockSpec((D_MODEL, tn), lambda j, i: (0, nj + j)),
                  pl.BlockSpec((2, 8, tn), lambda j, i: (0, 0, j)),
                  pl.BlockSpec((2, 1, tn), lambda j, i: (0, 0, j))],
        out_specs=(pl.BlockSpec((2, tm, tn), lambda j, i: (0, i, j)), pl.BlockSpec((tm, tn), lambda j, i: (i, j))),
        scratch_shapes=[pltpu.VMEM((2, 8, tn), F32)],
        name="ffn_up_conv", compiler_params=_cp(("parallel", "arbitrary"), VMEM_LIMIT),
    )(hn, w_up_bf, w_up_bf, cw, cb)


def _loss_head(h2, g_final, target):
    S = h2.shape[0]
    tm = _pick(S, (512, 256))

    def body(h_ref, g_ref, t_ref, loss_ref, dh_ref, dhb_ref, dg_ref):
        i = pl.program_id(0)

        @pl.when(i == 0)
        def _():
            loss_ref[...] = jnp.zeros_like(loss_ref)
            dg_ref[...] = jnp.zeros_like(dg_ref)

        hf = h_ref[...]
        g = g_ref[...]
        r = lax.rsqrt(jnp.mean(hf * hf, axis=-1, keepdims=True) + EPS)
        hhat = hf * r
        err = hhat * g - t_ref[...]
        loss_ref[...] += 0.5 * jnp.sum(jnp.mean(err * err, axis=-1, keepdims=True))
        dy = err * (1.0 / D_MODEL)
        dg_ref[0:1, :] += jnp.sum(dy * hhat, axis=0, keepdims=True)
        dhat = dy * g
        dh = r * (dhat - hhat * jnp.mean(dhat * hhat, axis=-1, keepdims=True))
        dh_ref[...] = dh
        dhb_ref[...] = dh.astype(BF16)

    return pl.pallas_call(
        body,
        out_shape=(jax.ShapeDtypeStruct((8, LANES), F32), jax.ShapeDtypeStruct((S, D_MODEL), F32),
                   jax.ShapeDtypeStruct((S, D_MODEL), BF16), jax.ShapeDtypeStruct((8, D_MODEL), F32)),
        grid=(S // tm,),
        in_specs=[pl.BlockSpec((tm, D_MODEL), lambda i: (i, 0)), pl.BlockSpec((1, D_MODEL), lambda i: (0, 0)),
                  pl.BlockSpec((tm, D_MODEL), lambda i: (i, 0))],
        out_specs=(pl.BlockSpec((8, LANES), lambda i: (0, 0)), pl.BlockSpec((tm, D_MODEL), lambda i: (i, 0)),
                   pl.BlockSpec((tm, D_MODEL), lambda i: (i, 0)), pl.BlockSpec((8, D_MODEL), lambda i: (0, 0))),
        name="loss_head", compiler_params=_cp(("arbitrary",), VMEM_LIMIT),
    )(h2, g_final, target)


def _rms_bwd(h, g, dy, res):
    S = h.shape[0]
    tm = _pick(S, (512, 256))

    def body(h_ref, g_ref, dy_ref, r_ref, dh_ref, dhb_ref, dg_ref):
        i = pl.program_id(0)

        @pl.when(i == 0)
        def _():
            dg_ref[...] = jnp.zeros_like(dg_ref)

        hf = h_ref[...]
        dyv = dy_ref[...]
        r = lax.rsqrt(jnp.mean(hf * hf, axis=-1, keepdims=True) + EPS)
        hhat = hf * r
        dg_ref[0:1, :] += jnp.sum(dyv * hhat, axis=0, keepdims=True)
        dhat = dyv * g_ref[...]
        dh = r_ref[...] + r * (dhat - hhat * jnp.mean(dhat * hhat, axis=-1, keepdims=True))
        dh_ref[...] = dh
        dhb_ref[...] = dh.astype(BF16)

    return pl.pallas_call(
        body,
        out_shape=(jax.ShapeDtypeStruct((S, D_MODEL), F32), jax.ShapeDtypeStruct((S, D_MODEL), BF16),
                   jax.ShapeDtypeStruct((8, D_MODEL), F32)),
        grid=(S // tm,),
        in_specs=[pl.BlockSpec((tm, D_MODEL), lambda i: (i, 0)), pl.BlockSpec((1, D_MODEL), lambda i: (0, 0)),
                  pl.BlockSpec((tm, D_MODEL), lambda i: (i, 0)), pl.BlockSpec((tm, D_MODEL), lambda i: (i, 0))],
        out_specs=(pl.BlockSpec((tm, D_MODEL), lambda i: (i, 0)), pl.BlockSpec((tm, D_MODEL), lambda i: (i, 0)),
                   pl.BlockSpec((8, D_MODEL), lambda i: (0, 0))),
        name="rms_bwd", compiler_params=_cp(("arbitrary",), VMEM_LIMIT),
    )(h, g, dy, res)


def _conv_act_bwd(hu, dact, cw, cb):
    _, S, F = hu.shape
    tm = _pick(S, (256,))
    tn = _pick(F, (256, 128))
    r8 = tm // 8
    n_i = S // tm
    last8 = S // 8 - 1

    def body(cur_ref, prev_ref, next_ref, da_ref, dan_ref, w_ref, b_ref, dhu_ref, dcw_ref):
        i = pl.program_id(1)

        @pl.when(i == 0)
        def _():
            dcw_ref[...] = jnp.zeros_like(dcw_ref)

        rid8 = lax.broadcasted_iota(jnp.int32, (8, tn), 0)
        cur = [cur_ref[0], cur_ref[1]]
        prev = [jnp.where(i > 0, prev_ref[h], 0.0) for h in range(2)]
        nxt = [next_ref[0], next_ref[1]]
        w = [w_ref[0], w_ref[1]]

        def gate_grads(a, g, d):
            sg = _sigmoid(g)
            return d * (g * sg), d * a * (sg * (1.0 + g * (1.0 - sg)))

        taps = [(_shift_rows(cur[h], prev[h], 2), _shift_rows(cur[h], prev[h], 1), cur[h]) for h in range(2)]
        a, g = [w[h][0:1, :] * taps[h][0] + w[h][1:2, :] * taps[h][1] + w[h][2:3, :] * taps[h][2] + b_ref[h]
                for h in range(2)]
        dhc = gate_grads(a, g, da_ref[...])
        a_n = _conv3(nxt[0], cur[0][tm - 8:, :], w[0], b_ref[0])
        g_n = _conv3(nxt[1], cur[1][tm - 8:, :], w[1], b_ref[1])
        dhc_n = gate_grads(a_n, g_n, dan_ref[...])
        for h in range(2):
            d = dhc[h]
            dn = jnp.where(i < n_i - 1, dhc_n[h], 0.0)
            dhu = (w[h][2:3, :] * d + w[h][1:2, :] * _shift_rows_up(d, dn, 1)
                   + w[h][0:1, :] * _shift_rows_up(d, dn, 2))
            dhu_ref[h] = dhu.astype(BF16)
            t0, t1, t2 = [jnp.sum(d * t, axis=0, keepdims=True) for t in taps[h]]
            t3 = jnp.sum(d, axis=0, keepdims=True)
            dcw_ref[h] += jnp.where(rid8 == 0, t0, jnp.where(rid8 == 1, t1, jnp.where(rid8 == 2, t2, jnp.where(rid8 == 3, t3, 0.0))))

    return pl.pallas_call(
        body,
        out_shape=(jax.ShapeDtypeStruct((2, S, F), BF16), jax.ShapeDtypeStruct((2, 8, F), F32)),
        grid=(F // tn, n_i),
        in_specs=[pl.BlockSpec((2, tm, tn), lambda j, i: (0, i, j)),
                  pl.BlockSpec((2, 8, tn), lambda j, i: (0, jnp.maximum(i * r8 - 1, 0), j)),
                  pl.BlockSpec((2, 8, tn), lambda j, i: (0, jnp.minimum((i + 1) * r8, last8), j)),
                  pl.BlockSpec((tm, tn), lambda j, i: (i, j)),
                  pl.BlockSpec((8, tn), lambda j, i: (jnp.minimum((i + 1) * r8, last8), j)),
                  pl.BlockSpec((2, 8, tn), lambda j, i: (0, 0, j)),
                  pl.BlockSpec((2, 1, tn), lambda j, i: (0, 0, j))],
        out_specs=(pl.BlockSpec((2, tm, tn), lambda j, i: (0, i, j)), pl.BlockSpec((2, 8, tn), lambda j, i: (0, 0, j))),
        name="conv_act_bwd", compiler_params=_cp(("parallel", "arbitrary"), VMEM_LIMIT),
    )(hu, hu, hu, dact, dact, cw, cb)


def _attn_prep(att, lse, dmix, qa):
    S = att.shape[0]
    tm = _pick(S, (512, 256))

    def body(o_ref, lse_ref, do_ref, q_ref, qb_ref, doa_ref):
        lane = lax.broadcasted_iota(jnp.int32, (tm, LANES), 1)
        do = do_ref[...]
        prod = o_ref[...] * do
        for hh in range(2):
            sel = (lane >= HEAD_DIM) if hh else (lane < HEAD_DIM)
            delta = jnp.sum(jnp.where(sel, prod, 0.0), axis=-1, keepdims=True)
            dod = pltpu.roll(do, HEAD_DIM, 1) if hh else do
            cols = slice(hh * LANES, (hh + 1) * LANES)
            doa_ref[:, cols] = jnp.where(lane < HEAD_DIM, dod, _aug(lane, _split3f(-delta))).astype(BF16)
            l3 = _split3f(-lse_ref[:, hh * HEAD_DIM:hh * HEAD_DIM + 1])
            augl = _aug(lane, [0.0] * 6 + l3).astype(BF16)
            qb_ref[:, cols] = jnp.where((lane >= HEAD_DIM + 6) & (lane < HEAD_DIM + 9), augl, q_ref[:, cols])

    return pl.pallas_call(
        body,
        out_shape=(jax.ShapeDtypeStruct(qa.shape, BF16), jax.ShapeDtypeStruct(qa.shape, BF16)),
        grid=(S // tm, N_PAIRS),
        in_specs=[pl.BlockSpec((tm, LANES), lambda i, p: (i, p)), pl.BlockSpec((tm, LANES), lambda i, p: (i, p)),
                  pl.BlockSpec((tm, LANES), lambda i, p: (i, p)), pl.BlockSpec((tm, 2 * LANES), lambda i, p: (i, p))],
        out_specs=(pl.BlockSpec((tm, 2 * LANES), lambda i, p: (i, p)), pl.BlockSpec((tm, 2 * LANES), lambda i, p: (i, p))),
        name="attn_prep", compiler_params=_cp(("parallel", "parallel")),
    )(att, lse, dmix, qa)


def _attn_bwd(qb, ka, va, doa):
    S = qb.shape[0]
    tk = _pick(S, (512, 256))
    tq = tk
    nq = S // tq

    def pair(a, scale=None):
        lane = lax.broadcasted_iota(jnp.int32, (a.shape[0], LANES), 1)
        out = jnp.where(lane < HEAD_DIM, a[:, :LANES], pltpu.roll(a[:, LANES:], HEAD_DIM, 1))
        return out if scale is None else out * scale

    def lanes01(a, col, sign):
        lane = lax.broadcasted_iota(jnp.int32, (a.shape[0], LANES), 1)
        return jnp.where(lane == 0, sign * a[:, col:col + 1], jnp.where(lane == 1, sign * a[:, LANES + col:LANES + col + 1], 0.0))

    def body(q_ref, do_ref, k_ref, v_ref, dqc_ref, dkc_ref, dvc_ref, dcq_ref, dck_ref, dq_ref, dka_ref, dva_ref):
        kb = pl.program_id(1)

        @pl.when(kb == 0)
        def _():
            dq_ref[...] = jnp.zeros_like(dq_ref)

        dka_ref[...] = jnp.zeros_like(dka_ref)
        dva_ref[...] = jnp.zeros_like(dva_ref)
        rid = lax.broadcasted_iota(jnp.int32, (tk, tq), 0)
        cid = lax.broadcasted_iota(jnp.int32, (tk, tq), 1)

        def tile(qi, masked):
            qs = pl.multiple_of(qi * tq, tq)
            for h in range(2):
                cols = slice(h * LANES, (h + 1) * LANES)
                qblk = q_ref[pl.ds(qs, tq), cols]
                doblk = do_ref[pl.ds(qs, tq), cols]
                kh = k_ref[:, cols]
                p = jnp.exp(lax.dot_general(kh, qblk, _NT, preferred_element_type=F32))
                if masked:
                    p = jnp.where(cid >= rid, p, 0.0)
                ds = (p * lax.dot_general(v_ref[:, cols], doblk, _NT, preferred_element_type=F32)).astype(BF16)
                dva_ref[:, cols] += jnp.dot(p.astype(BF16), doblk, preferred_element_type=F32)
                dka_ref[:, cols] += jnp.dot(ds, qblk, preferred_element_type=F32)
                dq_ref[pl.ds(qs, tq), cols] += lax.dot_general(ds, kh, _TN, preferred_element_type=F32)

        tile(kb, True)

        def step(qi, carry):
            tile(qi, False)
            return carry

        lax.fori_loop(kb + 1, nq, step, 0)
        dka = dka_ref[...]
        dkc_ref[...] = pair(dka).astype(BF16)
        dvc_ref[...] = pair(dva_ref[...]).astype(BF16)
        dck_ref[...] = lanes01(dka, HEAD_DIM + 3, -1.0)

        @pl.when(kb == nq - 1)
        def _():
            dqa = dq_ref[...]
            dqc_ref[...] = pair(dqa, HEAD_DIM ** -0.5).astype(BF16)
            dcq_ref[...] = lanes01(dqa, HEAD_DIM, 1.0)

    wide = 2 * LANES
    half = jax.ShapeDtypeStruct((S, ATT_WIDTH), BF16)
    slabs = jax.ShapeDtypeStruct((N_PAIRS, S, LANES), F32)
    return pl.pallas_call(
        body,
        out_shape=(half, half, half, slabs, slabs),
        grid=(N_PAIRS, nq),
        in_specs=[pl.BlockSpec((S, wide), lambda p, j: (0, p)), pl.BlockSpec((S, wide), lambda p, j: (0, p)),
                  pl.BlockSpec((tk, wide), lambda p, j: (j, p)), pl.BlockSpec((tk, wide), lambda p, j: (j, p))],
        out_specs=(pl.BlockSpec((S, LANES), lambda p, j: (0, p)), pl.BlockSpec((tk, LANES), lambda p, j: (j, p)),
                   pl.BlockSpec((tk, LANES), lambda p, j: (j, p)), pl.BlockSpec((None, S, LANES), lambda p, j: (p, 0, 0)),
                   pl.BlockSpec((None, tk, LANES), lambda p, j: (p, j, 0))),
        scratch_shapes=[pltpu.VMEM((S, wide), F32), pltpu.VMEM((tk, wide), F32), pltpu.VMEM((tk, wide), F32)],
        name="attn_bwd", compiler_params=_cp(("parallel", "arbitrary"), VMEM_LIMIT),
    )(qb, doa, ka, va)


def _attn_delta_old(o, do):
    S = o.shape[0]
    tm = _pick(S, (512, 256))
    ones = _group_ones()

    def body(o_ref, do_ref, ones_ref, d_ref):
        d_ref[...] = _dot3(o_ref[...] * do_ref[...], ones_ref[...])

    return pl.pallas_call(
        body, out_shape=jax.ShapeDtypeStruct((S, ATT_WIDTH), F32), grid=(S // tm,),
        in_specs=[pl.BlockSpec((tm, ATT_WIDTH), lambda i: (i, 0)), pl.BlockSpec((tm, ATT_WIDTH), lambda i: (i, 0)),
                  pl.BlockSpec((ATT_WIDTH, ATT_WIDTH), lambda i: (0, 0))],
        out_specs=pl.BlockSpec((tm, ATT_WIDTH), lambda i: (i, 0)),
        name="attn_delta", compiler_params=_cp(("parallel",)),
    )(o, do, ones)


def _attn_bwd_old(qkv, do_bf, c_cols, c_rows, lse_rows, dl_rows):
    S = qkv.shape[0]
    tk = _pick(S, (256,))
    tq = tk
    nq = S // tq
    nt = (((1,), (1,)), ((), ()))
    tn_dims = (((0,), (0,)), ((), ()))

    def body(q_ref, do_ref, k_ref, v_ref, cc_ref, cr_ref, lse_ref, dl_ref, dq_ref, dk_ref, dv_ref, dcs_ref, dcq_ref):
        kb = pl.program_id(1)

        @pl.when(kb == 0)
        def _():
            dq_ref[...] = jnp.zeros_like(dq_ref)
            dcq_ref[...] = jnp.zeros_like(dcq_ref)

        lane = lax.broadcasted_iota(jnp.int32, (tk, LANES), 1)
        rid = lax.broadcasted_iota(jnp.int32, (tk, tq), 0)
        cid = lax.broadcasted_iota(jnp.int32, (tk, tq), 1)
        k = k_ref[...]
        v = v_ref[...]
        dks, dvs, dcs = [], [], []
        for hh in range(2):
            sel = (lane >= GROUP_DIM) if hh else (lane < GROUP_DIM)
            km = jnp.where(sel, k, jnp.zeros_like(k))
            vm = jnp.where(sel, v, jnp.zeros_like(v))
            cs = cc_ref[:, hh * HEAD_DIM:hh * HEAD_DIM + 1]

            def step(qb, carry, km=km, vm=vm, cs=cs, hh=hh):
                dk_acc, dv_acc, dc_acc = carry
                qs = pl.multiple_of(qb * tq, tq)
                qblk = q_ref[pl.ds(qs, tq), :]
                doblk = do_ref[pl.ds(qs, tq), :]
                s = lax.dot_general(km, qblk, nt, preferred_element_type=F32)
                s = s + (cr_ref[hh:hh + 1, pl.ds(qs, tq)] - cs)
                p = jnp.exp(s - lse_ref[hh:hh + 1, pl.ds(qs, tq)])
                p = jnp.where((qb > kb) | (cid >= rid), p, 0.0)
                dp = lax.dot_general(vm, doblk, nt, preferred_element_type=F32)
                ds = p * (dp - dl_ref[hh:hh + 1, pl.ds(qs, tq)])
                ds_bf = ds.astype(BF16)
                dv_acc = dv_acc + jnp.dot(p.astype(BF16), doblk, preferred_element_type=F32)
                dk_acc = dk_acc + jnp.dot(ds_bf, qblk, preferred_element_type=F32)
                dc_acc = dc_acc + jnp.sum(ds, axis=-1, keepdims=True)
                dq_ref[pl.ds(qs, tq), :] += lax.dot_general(ds_bf, km, tn_dims, preferred_element_type=F32)
                dcq_ref[hh:hh + 1, pl.ds(qs, tq)] += jnp.sum(ds, axis=0, keepdims=True)
                return dk_acc, dv_acc, dc_acc

            init = (jnp.zeros((tk, LANES), F32), jnp.zeros((tk, LANES), F32), jnp.zeros((tk, 1), F32))
            dk_acc, dv_acc, dc_acc = lax.fori_loop(kb, nq, step, init)
            dks.append(dk_acc)
            dvs.append(dv_acc)
            dcs.append(dc_acc)
        dk_ref[...] = jnp.where(lane < GROUP_DIM, dks[0], dks[1])
        dv_ref[...] = jnp.where(lane < GROUP_DIM, dvs[0], dvs[1])
        dcs_ref[...] = jnp.where(lane == 0, -dcs[0], jnp.where(lane == 1, -dcs[1], 0.0))

    return pl.pallas_call(
        body,
        out_shape=(jax.ShapeDtypeStruct((S, ATT_WIDTH), F32), jax.ShapeDtypeStruct((S, ATT_WIDTH), F32),
                   jax.ShapeDtypeStruct((S, ATT_WIDTH), F32), jax.ShapeDtypeStruct((N_PAIRS, S, LANES), F32),
                   jax.ShapeDtypeStruct((N_PAIRS, 8, S), F32)),
        grid=(N_PAIRS, nq),
        in_specs=[pl.BlockSpec((S, LANES), lambda p, j: (0, p)),
                  pl.BlockSpec((S, LANES), lambda p, j: (0, p)),
                  pl.BlockSpec((tk, LANES), lambda p, j: (j, N_PAIRS + p)),
                  pl.BlockSpec((tk, LANES), lambda p, j: (j, 2 * N_PAIRS + p)),
                  pl.BlockSpec((None, tk, LANES), lambda p, j: (p, j, 0)),
                  pl.BlockSpec((None, 8, S), lambda p, j: (p, 0, 0)),
                  pl.BlockSpec((None, 8, S), lambda p, j: (p, 0, 0)),
                  pl.BlockSpec((None, 8, S), lambda p, j: (p, 0, 0))],
        out_specs=(pl.BlockSpec((S, LANES), lambda p, j: (0, p)),
                   pl.BlockSpec((tk, LANES), lambda p, j: (j, p)),
                   pl.BlockSpec((tk, LANES), lambda p, j: (j, p)),
                   pl.BlockSpec((None, tk, LANES), lambda p, j: (p, j, 0)),
                   pl.BlockSpec((None, 8, S), lambda p, j: (p, 0, 0))),
        name="attn_bwd", compiler_params=_cp(("parallel", "arbitrary"), VMEM_LIMIT),
    )(qkv, do_bf, qkv, qkv, c_cols, c_rows, lse_rows, dl_rows)


def _gmlp_bwd(ug, dsg, gain, w_s, wt_s, bias_full):
    S = ug.shape[0]
    tm = _pick(S, (512, 256, 128))
    n_chunks = tm // CHUNK
    n_i = S // tm
    ones = _group_ones()
    nt = (((1,), (1,)), ((), ()))

    def body(ug_ref, dsg_ref, gain_ref, w_ref, wt_ref, bias_ref, ones_ref, dug_ref, dw_ref, dgain_ref, dbias_ref,
             dbacc_ref):
        i = pl.program_id(0)

        @pl.when(i == 0)
        def _():
            dw_ref[...] = jnp.zeros_like(dw_ref)
            dgain_ref[...] = jnp.zeros_like(dgain_ref)
            dbacc_ref[...] = jnp.zeros_like(dbacc_ref)

        ones_m = ones_ref[...]
        pu = ug_ref[:, :GMLP_WIDTH]
        pg = ug_ref[:, GMLP_WIDTH:]
        u = _gelu(pu)
        vr = _gelu(pg)
        ms = _dot3(vr * vr, ones_m) * (1.0 / GROUP_DIM)
        rinv = lax.rsqrt(ms + EPS)
        vhat = vr * rinv
        gain_v = gain_ref[...]
        vn = (vhat * gain_v).astype(BF16)
        mixed = _gmlp_mixed(vn, w_ref, bias_ref[...], n_chunks)
        dsg_v = dsg_ref[...]
        du = dsg_v * mixed
        dmixed = dsg_v * u
        dm_bf = dmixed.astype(BF16)
        lane = lax.broadcasted_iota(jnp.int32, (CHUNK, LANES), 1)
        row = lax.broadcasted_iota(jnp.int32, (CHUNK, CHUNK), 0)
        col = lax.broadcasted_iota(jnp.int32, (CHUNK, CHUNK), 1)
        wts = [jnp.where(col >= row, wt_ref[g], 0.0).astype(BF16) for g in range(N_GROUPS)]
        dvn_rows = []
        dbsum = jnp.zeros((CHUNK, GMLP_WIDTH), F32)
        for ci in range(n_chunks):
            rs = slice(ci * CHUNK, (ci + 1) * CHUNK)
            dbsum = dbsum + dmixed[rs, :]
            cols = []
            for pp in range(N_GROUPS // 2):
                cs = slice(pp * LANES, (pp + 1) * LANES)
                dm = dm_bf[rs, cs]
                dm_lo = jnp.where(lane < GROUP_DIM, dm, jnp.zeros_like(dm))
                dm_hi = jnp.where(lane >= GROUP_DIM, dm, jnp.zeros_like(dm))
                vb = vn[rs, cs]
                dw_ref[2 * pp] += lax.dot_general(dm_lo, vb, nt, preferred_element_type=F32)
                dw_ref[2 * pp + 1] += lax.dot_general(dm_hi, vb, nt, preferred_element_type=F32)
                cols.append(jnp.dot(wts[2 * pp], dm_lo, preferred_element_type=F32)
                            + jnp.dot(wts[2 * pp + 1], dm_hi, preferred_element_type=F32))
            dvn_rows.append(jnp.concatenate(cols, axis=1))
        dvn = jnp.concatenate(dvn_rows, axis=0)
        dbacc_ref[...] += dbsum
        dgain_ref[0:1, :] += jnp.sum(dvn * vhat, axis=0, keepdims=True)
        dvhat = dvn * gain_v
        gm = _dot3(dvhat * vhat, ones_m) * (1.0 / GROUP_DIM)
        dvr = rinv * (dvhat - vhat * gm)
        dug_ref[:, :GMLP_WIDTH] = (du * _gelu_grad(pu)).astype(BF16)
        dug_ref[:, GMLP_WIDTH:] = (dvr * _gelu_grad(pg)).astype(BF16)

        @pl.when(i == n_i - 1)
        def _():
            for g in range(N_GROUPS):
                dw_ref[g] = jnp.where(row >= col, dw_ref[g], 0.0)
            dbias_ref[...] = _dot3(dbacc_ref[...], ones_m)

    return pl.pallas_call(
        body,
        out_shape=(jax.ShapeDtypeStruct((S, 2 * GMLP_WIDTH), BF16), jax.ShapeDtypeStruct((N_GROUPS, CHUNK, CHUNK), F32),
                   jax.ShapeDtypeStruct((8, GMLP_WIDTH), F32), jax.ShapeDtypeStruct((CHUNK, GMLP_WIDTH), F32)),
        grid=(n_i,),
        in_specs=[pl.BlockSpec((tm, 2 * GMLP_WIDTH), lambda i: (i, 0)), pl.BlockSpec((tm, GMLP_WIDTH), lambda i: (i, 1)),
                  pl.BlockSpec((1, GMLP_WIDTH), lambda i: (0, 0)),
                  pl.BlockSpec((N_GROUPS, CHUNK, CHUNK), lambda i: (0, 0, 0)),
                  pl.BlockSpec((N_GROUPS, CHUNK, CHUNK), lambda i: (0, 0, 0)),
                  pl.BlockSpec((CHUNK, GMLP_WIDTH), lambda i: (0, 0)),
                  pl.BlockSpec((GMLP_WIDTH, GMLP_WIDTH), lambda i: (0, 0))],
        out_specs=(pl.BlockSpec((tm, 2 * GMLP_WIDTH), lambda i: (i, 0)),
                   pl.BlockSpec((N_GROUPS, CHUNK, CHUNK), lambda i: (0, 0, 0)),
                   pl.BlockSpec((8, GMLP_WIDTH), lambda i: (0, 0)),
                   pl.BlockSpec((CHUNK, GMLP_WIDTH), lambda i: (0, 0))),
        scratch_shapes=[pltpu.VMEM((CHUNK, GMLP_WIDTH), F32)],
        name="gmlp_bwd", compiler_params=_cp(("arbitrary",), VMEM_LIMIT),
    )(ug, dsg, gain, w_s, wt_s, bias_full, ones)


def _gate_bwd(dcq, dck, zf):
    S = zf.shape[0]
    tm = _pick(S, (256,))
    n_i = S // tm
    triu = (lax.broadcasted_iota(jnp.int32, (tm, tm), 0) <= lax.broadcasted_iota(jnp.int32, (tm, tm), 1)).astype(BF16)

    def body(dcq_ref, dck_ref, zf_ref, tri_ref, dzf_ref, dbf_ref, carry_ref):
        i = pl.program_id(0)

        @pl.when(i == 0)
        def _():
            carry_ref[...] = jnp.zeros_like(carry_ref)
            dbf_ref[...] = jnp.zeros_like(dbf_ref)

        lane = lax.broadcasted_iota(jnp.int32, (tm, LANES), 1)
        dc = jnp.zeros((tm, LANES), F32)
        for p in range(N_PAIRS):
            slab = dcq_ref[p] + dck_ref[p]
            for hh in range(2):
                dc = dc + jnp.where(lane == 2 * p + hh, slab[:, hh:hh + 1], 0.0)
        dlf = _dot3l(tri_ref[...], dc) + carry_ref[0:1, :]
        carry_ref[0:1, :] = dlf[0:1, :]
        dz = jnp.where(lane < N_HEADS, dlf * _sigmoid(-zf_ref[...]), 0.0)
        dzf_ref[...] = dz.astype(BF16)
        dbf_ref[0:1, :] += jnp.sum(dz, axis=0, keepdims=True)

    return pl.pallas_call(
        body,
        out_shape=(jax.ShapeDtypeStruct((S, LANES), BF16), jax.ShapeDtypeStruct((8, LANES), F32)),
        grid=(n_i,),
        in_specs=[pl.BlockSpec((N_PAIRS, tm, LANES), lambda i: (0, n_i - 1 - i, 0)),
                  pl.BlockSpec((N_PAIRS, tm, LANES), lambda i: (0, n_i - 1 - i, 0)),
                  pl.BlockSpec((tm, LANES), lambda i: (n_i - 1 - i, 0)),
                  pl.BlockSpec((tm, tm), lambda i: (0, 0))],
        out_specs=(pl.BlockSpec((tm, LANES), lambda i: (n_i - 1 - i, 0)), pl.BlockSpec((8, LANES), lambda i: (0, 0))),
        scratch_shapes=[pltpu.VMEM((8, LANES), F32)],
        name="gate_bwd", compiler_params=_cp(("arbitrary",), VMEM_LIMIT),
    )(dcq, dck, zf, triu)


def _out_proj_fwd(att_bf, sg, w_out_bf, x, g_ffn):
    S = x.shape[0]
    tm = _pick(S, (512, 256))

    def body(a_ref, s_ref, w_ref, x_ref, g_ref, h_ref, hn_ref):
        h = (x_ref[...] + jnp.dot(a_ref[...], w_ref[:ATT_WIDTH, :], preferred_element_type=F32)
             + jnp.dot(s_ref[...], w_ref[ATT_WIDTH:, :], preferred_element_type=F32))
        h_ref[...] = h
        r = lax.rsqrt(jnp.mean(h * h, axis=-1, keepdims=True) + EPS)
        hn_ref[...] = ((h * r) * g_ref[...]).astype(BF16)

    row = pl.BlockSpec((tm, D_MODEL), lambda i: (i, 0))
    half = pl.BlockSpec((tm, ATT_WIDTH), lambda i: (i, 0))
    return pl.pallas_call(
        body, out_shape=(jax.ShapeDtypeStruct((S, D_MODEL), F32), jax.ShapeDtypeStruct((S, D_MODEL), BF16)),
        grid=(S // tm,),
        in_specs=[half, half, pl.BlockSpec((D_MODEL, D_MODEL), lambda i: (0, 0)), row,
                  pl.BlockSpec((1, D_MODEL), lambda i: (0, 0))],
        out_specs=(row, row), name="out_proj", compiler_params=_cp(("parallel",), VMEM_LIMIT),
    )(att_bf, sg, w_out_bf, x, g_ffn)


_IN_PIECES = ((0, ATT_WIDTH), (ATT_WIDTH, ATT_WIDTH), (2 * ATT_WIDTH, ATT_WIDTH), (QKV, 2 * GMLP_WIDTH), (UG_END, LANES))


def _inproj_bwd_dx(pieces, w_pad, x, g_mix, dh1):
    S = x.shape[0]
    tm = _pick(S, (512, 256))

    def body(*refs):
        p_refs, (w_ref, x_ref, g_ref, r_ref, dx_ref, dg_ref) = refs[:5], refs[5:]
        i = pl.program_id(0)

        @pl.when(i == 0)
        def _():
            dg_ref[...] = jnp.zeros_like(dg_ref)

        dxn = None
        for p_ref, (c0, width) in zip(p_refs, _IN_PIECES):
            part = lax.dot_general(p_ref[...], w_ref[:, c0:c0 + width], _NT, preferred_element_type=F32)
            dxn = part if dxn is None else dxn + part
        xf = x_ref[...]
        r = lax.rsqrt(jnp.mean(xf * xf, axis=-1, keepdims=True) + EPS)
        xhat = xf * r
        dg_ref[0:1, :] += jnp.sum(dxn * xhat, axis=0, keepdims=True)
        dhat = dxn * g_ref[...]
        dx_ref[...] = r_ref[...] + r * (dhat - xhat * jnp.mean(dhat * xhat, axis=-1, keepdims=True))

    row = pl.BlockSpec((tm, D_MODEL), lambda i: (i, 0))
    return pl.pallas_call(
        body, out_shape=(jax.ShapeDtypeStruct((S, D_MODEL), F32), jax.ShapeDtypeStruct((8, D_MODEL), F32)),
        grid=(S // tm,),
        in_specs=[pl.BlockSpec((tm, width), lambda i: (i, 0)) for _, width in _IN_PIECES]
        + [pl.BlockSpec((D_MODEL, IN_PAD), lambda i: (0, 0)), row, pl.BlockSpec((1, D_MODEL), lambda i: (0, 0)), row],
        out_specs=(row, pl.BlockSpec((8, D_MODEL), lambda i: (0, 0))),
        name="in_proj_dx", compiler_params=_cp(("arbitrary",), VMEM_LIMIT),
    )(*pieces, w_pad, x, g_mix, dh1)


def _inproj_bwd_dw(xn, pieces):
    S = xn.shape[0]
    tk = _pick(S, (512, 256))

    def body(*refs):
        x_ref, p_refs, o_ref = refs[0], refs[1:6], refs[6]
        k = pl.program_id(0)

        @pl.when(k == 0)
        def _():
            o_ref[...] = jnp.zeros_like(o_ref)

        xb = x_ref[...]
        for p_ref, (c0, width) in zip(p_refs, _IN_PIECES):
            o_ref[:, c0:c0 + width] += lax.dot_general(xb, p_ref[...], _TN, preferred_element_type=F32)

    return pl.pallas_call(
        body, out_shape=jax.ShapeDtypeStruct((D_MODEL, IN_PAD), F32), grid=(S // tk,),
        in_specs=[pl.BlockSpec((tk, D_MODEL), lambda k: (k, 0))]
        + [pl.BlockSpec((tk, width), lambda k: (k, 0)) for _, width in _IN_PIECES],
        out_specs=pl.BlockSpec((D_MODEL, IN_PAD), lambda k: (0, 0)),
        name="in_proj_dw", compiler_params=_cp(("arbitrary",), VMEM_LIMIT),
    )(xn, *pieces)


def _adamw(w, m, v, parts, name):
    R, C = w.shape
    tr = R
    for cand in (256, 128, 64, 32, 16, 8):
        if R % cand == 0 and R > cand:
            tr = cand
            break
    c1 = 1.0 / (1.0 - ADAM_B1 ** ADAM_STEP)
    c2 = 1.0 / (1.0 - ADAM_B2 ** ADAM_STEP)

    def body(w_ref, m_ref, v_ref, p_ref, g_ref, d_ref, nm_ref, nv_ref):
        g = p_ref[0]
        for j in range(1, N_DEV):
            g = g + p_ref[j]
        g_ref[...] = g
        nm = ADAM_B1 * m_ref[...] + (1.0 - ADAM_B1) * g
        nv = ADAM_B2 * v_ref[...] + (1.0 - ADAM_B2) * (g * g)
        nm_ref[...] = nm
        nv_ref[...] = nv
        d_ref[...] = -ADAM_LR * ((nm * c1) / (jnp.sqrt(nv * c2) + ADAM_EPS) + ADAM_WD * w_ref[...])

    spec = pl.BlockSpec((tr, C), lambda i: (i, 0))
    shp = jax.ShapeDtypeStruct((R, C), F32)
    return pl.pallas_call(
        body, out_shape=(shp, shp, shp, shp), grid=(R // tr,),
        in_specs=[spec, spec, spec, pl.BlockSpec((N_DEV, tr, C), lambda i: (0, i, 0))],
        out_specs=(spec, spec, spec, spec),
        name=name, compiler_params=_cp(("parallel",), VMEM_LIMIT),
    )(w, m, v, parts)


def _place():
    x, y, c = lax.axis_index("x"), lax.axis_index("y"), lax.axis_index("c")
    return x, y, c


def _all_gather(blocks, name):
    n = len(blocks)

    def body(*refs):
        ins, outs = refs[:n], refs[n:2 * n]
        send_sems, recv_sems, local_sems = refs[2 * n:]
        x, y, c = _place()
        me, sibling = (x, y, c), (x, y, 1 - c)
        chips = [(1 - x, y), (x, 1 - y), (1 - x, 1 - y)]
        sends = []
        for a in range(n):
            out = outs[a]

            def slot(px, py, pc, out=out):
                return out.at[4 * px + 2 * py + pc]

            def copy(k, block, to, src=None, a=a, slot=slot):
                return pltpu.make_async_remote_copy(
                    src_ref=slot(*block) if src is None else src, dst_ref=slot(*block),
                    send_sem=send_sems.at[a, k], recv_sem=recv_sems.at[a, k], device_id=to, device_id_type=MESH)

            mine = pltpu.make_async_copy(ins[a], slot(*me), local_sems.at[a])
            mine.start()
            first = [copy(0, me, sibling, src=ins[a])]
            first += [copy(1 + j, me, (*chip, c), src=ins[a]) for j, chip in enumerate(chips)]
            for cp in first:
                cp.start()
            sends.append((mine, first, copy))
        for a in range(n):
            mine, first, copy = sends[a]
            passed = [copy(4 + j, (*chip, c), sibling) for j, chip in enumerate(chips)]
            for j, chip in enumerate(chips):
                copy(1 + j, (*chip, c), me).wait_recv()
                passed[j].start()
            copy(0, sibling, me).wait_recv()
            for j, chip in enumerate(chips):
                copy(4 + j, (*chip, 1 - c), me).wait_recv()
            for cp in first + passed:
                cp.wait_send()
            mine.wait()

    any_spec = pl.BlockSpec(memory_space=pl.ANY)
    return pl.pallas_call(
        body, out_shape=tuple(jax.ShapeDtypeStruct((N_DEV,) + b.shape, b.dtype) for b in blocks),
        in_specs=[any_spec] * n, out_specs=tuple([any_spec] * n),
        scratch_shapes=[pltpu.SemaphoreType.DMA((n, 7)), pltpu.SemaphoreType.DMA((n, 7)), pltpu.SemaphoreType.DMA((n,))],
        name=name,
    )(*blocks)


def _exchange_shards(parts, name):
    n = len(parts)

    def body(*refs):
        ins, outs = refs[:n], refs[n:2 * n]
        send_sems, recv_sems, local_sems = refs[2 * n:]
        x, y, c = _place()
        me = 4 * x + 2 * y + c
        started = []
        for a in range(n):
            mine = pltpu.make_async_copy(ins[a].at[me], outs[a].at[me], local_sems.at[a])
            mine.start()
            started.append(mine)
            for k in range(1, N_DEV):
                px, py, pc = x ^ ((k >> 2) & 1), y ^ ((k >> 1) & 1), c ^ (k & 1)
                cp = pltpu.make_async_remote_copy(
                    src_ref=ins[a].at[4 * px + 2 * py + pc], dst_ref=outs[a].at[me],
                    send_sem=send_sems.at[a, k - 1], recv_sem=recv_sems.at[a, k - 1],
                    device_id=(px, py, pc), device_id_type=MESH)
                cp.start()
                started.append(cp)
        for cp in started:
            cp.wait()

    any_spec = pl.BlockSpec(memory_space=pl.ANY)
    return pl.pallas_call(
        body, out_shape=tuple(jax.ShapeDtypeStruct(p.shape, p.dtype) for p in parts),
        in_specs=[any_spec] * n, out_specs=tuple([any_spec] * n),
        scratch_shapes=[pltpu.SemaphoreType.DMA((n, 7)), pltpu.SemaphoreType.DMA((n, 7)), pltpu.SemaphoreType.DMA((n,))],
        name=name,
    )(*parts)


_HBM = pl.BlockSpec(memory_space=pltpu.HBM)
_SEM = pl.BlockSpec(memory_space=pltpu.SEMAPHORE)
_EFFECT = pltpu.SideEffectType.DATAFLOW_SIDE_EFFECTING


def _peers(x, y, c):
    out = []
    for k in range(1, N_DEV):
        px, py, pc = x ^ ((k >> 2) & 1), y ^ ((k >> 1) & 1), c ^ (k & 1)
        out.append((k, (px, py, pc), 4 * px + 2 * py + pc))
    return out


def _xchg_copies(src_refs, land_refs, send_sems, recv_sems, scatter):
    x, y, c = _place()
    me = 4 * x + 2 * y + c
    copies = []
    for a, (src, land) in enumerate(zip(src_refs, land_refs)):
        for k, place, idx in _peers(x, y, c):
            j = a * (N_DEV - 1) + k - 1
            copies.append(pltpu.make_async_remote_copy(
                src_ref=src.at[idx] if scatter[a] else src, dst_ref=land.at[me],
                send_sem=send_sems[j], recv_sem=recv_sems[j], device_id=place, device_id_type=MESH))
    return copies


def _xchg_start(srcs, scatter, name):
    n = len(srcs)
    lands = [lax.empty((N_DEV,) + (s.shape[1:] if sc else s.shape), s.dtype) for s, sc in zip(srcs, scatter)]

    ns = n * (N_DEV - 1)

    def body(*refs):
        sems = refs[2 * n:2 * n + 2 * ns]
        for cp in _xchg_copies(refs[:n], refs[n:2 * n], sems[:ns], sems[ns:], scatter):
            cp.start()
        token = refs[-1]
        token[...] = jnp.zeros_like(token)

    both = list(srcs) + lands
    res = pl.pallas_call(
        body, name=name,
        out_shape=(*[pltpu.SemaphoreType.DMA(())] * (2 * ns),
                   *[pltpu.HBM(a.shape, a.dtype) for a in both], jax.ShapeDtypeStruct((8, LANES), F32)),
        in_specs=[_HBM] * (2 * n),
        out_specs=(*([_SEM] * (2 * ns)), *([_HBM] * (2 * n)), pl.BlockSpec(memory_space=pltpu.VMEM)),
        input_output_aliases={i: 2 * ns + i for i in range(2 * n)},
        compiler_params=pltpu.CompilerParams(has_side_effects=_EFFECT),
    )(*[pltpu.with_memory_space_constraint(a, pltpu.HBM) for a in both])
    return (tuple(res[:2 * ns]), tuple(res[2 * ns:2 * ns + 2 * n])), res[-1]


def _xchg_wait(handle, scatter, after, name):
    sems, thru = handle
    n = len(thru) // 2
    ns = len(sems) // 2

    def body(*refs):
        got = refs[2 * n:2 * n + 2 * ns]
        for cp in _xchg_copies(refs[:n], refs[n:2 * n], got[:ns], got[ns:], scatter):
            cp.wait_send()
            cp.wait_recv()

    outs = pl.pallas_call(
        body, name=name, out_shape=tuple(pltpu.HBM(a.shape, a.dtype) for a in thru),
        in_specs=[_HBM] * (2 * n) + [_SEM] * (2 * ns) + [pl.BlockSpec(memory_space=pl.ANY)],
        out_specs=tuple([_HBM] * (2 * n)), input_output_aliases={i: i for i in range(2 * n)},
        compiler_params=pltpu.CompilerParams(has_side_effects=_EFFECT),
    )(*thru, *sems, after)
    return outs[:n], outs[n:]


def _tie(a, token):
    return a if token is None else a + token[0, 0]


def _rows128(a):
    flat = a.reshape(-1)
    rows = -(-flat.shape[0] // LANES)
    rows = -(-rows // 8) * 8
    return jnp.pad(flat, (0, rows * LANES - flat.shape[0])).reshape(rows, LANES)


def _local_step(x, target, norm_mix_g, w_in_bf, b_forget, gmlp_norm_g, w_spatial, b_spatial, norm_ffn_g, conv_b,
                norm_final_g, rest_fn, send_fn, token=None):
    f = D_FF
    g_mix = norm_mix_g.reshape(1, D_MODEL)
    w_pad = jnp.pad(w_in_bf, ((0, 0), (0, IN_PAD - IN_COLS)))
    bf_pad = jnp.pad(b_forget.reshape(1, N_HEADS), ((0, 0), (0, LANES - N_HEADS)))
    xn, qa, ka, va, ug, zf = _inproj_fwd(x, _tie(g_mix, token), w_pad, bf_pad)
    bias_full = jnp.repeat(b_spatial.reshape(N_GROUPS, CHUNK).T, GROUP_DIM, axis=1)
    w_s = w_spatial.reshape(N_GROUPS, CHUNK, CHUNK)
    gain = gmlp_norm_g.reshape(1, GMLP_WIDTH)
    sg = _gmlp_fwd(ug, gain, w_s, bias_full)
    att, lse, att_bf = _attn_fwd(qa, ka, va)
    w_out_bf, w_up_bf, conv_w, w_down_bf = rest_fn(att_bf)
    g_ffn = norm_ffn_g.reshape(1, D_MODEL)
    h1, hn = _out_proj_fwd(att_bf, sg, w_out_bf, x, g_ffn)
    cw = jnp.pad(conv_w.reshape(3, 2, f).transpose(1, 0, 2), ((0, 0), (0, 5), (0, 0)))
    cb = conv_b.reshape(2, 1, f)
    hu, act = _ffn_up_conv(hn, w_up_bf, cw, cb)
    h2 = _mm(act, w_down_bf, mode="nn", out_dtype=F32, tm=512, tn=1024, tk=1408, res=h1, name="ffn_down")
    loss_blk, dh2, dh2_bf, dg_final = _loss_head(h2, norm_final_g.reshape(1, D_MODEL), target)
    dact = _mm(dh2_bf, w_down_bf, mode="nt", out_dtype=F32, tm=512, tn=1408, tk=1024, outer="j", name="ffn_down_dx")
    dw_down = _mm(act, dh2_bf, mode="tn", out_dtype=F32, tm=1408, tn=1024, tk=1024, name="ffn_down_dw")
    dhu, dcw = _conv_act_bwd(hu, dact, cw, _tie(cb, send_fn("w_down", dw_down)))
    dhn = _mm(dhu, w_up_bf, mode="nt", out_dtype=F32, tm=1024, tn=1024, tk=1408, a_halves=True, name="ffn_up_dx")
    dw_up = _mm(hn, dhu, mode="tn", out_dtype=F32, tm=1024, tn=1408, tk=1024, b_halves=True, outer="j", name="ffn_up_dw")
    dh1, dh1_bf, dg_ffn = _rms_bwd(h1, _tie(g_ffn, send_fn("w_up", dw_up)), dhn, dh2)
    dmix = _mm(dh1_bf, w_out_bf, mode="nt", out_dtype=F32, tm=512, tn=1024, tk=1024, name="out_proj_dx")
    dw_out = jnp.concatenate(
        [_mm(att_bf, dh1_bf, mode="tn", out_dtype=F32, tm=512, tn=1024, tk=1024, name="out_proj_dw_att"),
         _mm(sg, dh1_bf, mode="tn", out_dtype=F32, tm=512, tn=1024, tk=1024, name="out_proj_dw_sg")], axis=0)
    qb, doa = _attn_prep(att, lse, dmix, qa)
    dq, dk, dv, dcq, dck = _attn_bwd(qb, ka, va, doa)
    wt_s = w_s.transpose(0, 2, 1)
    dug, dw_s, dgain, dbias = _gmlp_bwd(ug, dmix, _tie(gain, send_fn("w_out", dw_out)), w_s, wt_s, bias_full)
    dzf, dbf = _gate_bwd(dcq, dck, zf)
    pieces = (dq, dk, dv, dug, dzf)
    dw_in = _inproj_bwd_dw(xn, pieces)
    grad_x, dg_mix = _inproj_bwd_dx(pieces, w_pad, x, _tie(g_mix, send_fn("w_in", dw_in[:, :IN_COLS])), dh1)
    grads = dict(
        norm_mix_g=dg_mix[0:1, :],
        b_forget=dbf[0:1, :N_HEADS],
        gmlp_norm_g=dgain[0:1, :],
        w_spatial=dw_s,
        b_spatial=dbias[:, ::GROUP_DIM].T,
        norm_ffn_g=dg_ffn[0:1, :],
        conv_w=dcw[:, 0:3, :].transpose(1, 0, 2).reshape(3, 2 * f),
        conv_b=dcw[:, 3, :].reshape(1, 2 * f),
        norm_final_g=dg_final[0, :],
    )
    return loss_blk[0, 0], grad_x, grads


SMALL = ("norm_mix_g", "b_forget", "gmlp_norm_g", "w_spatial", "b_spatial", "norm_ffn_g", "conv_b", "norm_final_g")


def kernel(x, norm_mix_g, w_in, b_forget, gmlp_norm_g, w_spatial, b_spatial, w_out, norm_ffn_g, w_up, conv_w, conv_b, w_down, norm_final_g, loss_target, m_norm_mix_g, m_w_in, m_b_forget, m_gmlp_norm_g, m_w_spatial, m_b_spatial, m_w_out, m_norm_ffn_g, m_w_up, m_conv_w, m_conv_b, m_w_down, m_norm_final_g, v_norm_mix_g, v_w_in, v_b_forget, v_gmlp_norm_g, v_w_spatial, v_b_spatial, v_w_out, v_norm_ffn_g, v_w_up, v_conv_w, v_conv_b, v_w_down, v_norm_final_g):
    weights = dict(norm_mix_g=norm_mix_g, w_in=w_in, b_forget=b_forget, gmlp_norm_g=gmlp_norm_g, w_spatial=w_spatial,
                   b_spatial=b_spatial, w_out=w_out, norm_ffn_g=norm_ffn_g, w_up=w_up, conv_w=conv_w, conv_b=conv_b,
                   w_down=w_down, norm_final_g=norm_final_g)
    m_in = dict(norm_mix_g=m_norm_mix_g, w_in=m_w_in, b_forget=m_b_forget, gmlp_norm_g=m_gmlp_norm_g,
                w_spatial=m_w_spatial, b_spatial=m_b_spatial, w_out=m_w_out, norm_ffn_g=m_norm_ffn_g, w_up=m_w_up,
                conv_w=m_conv_w, conv_b=m_conv_b, w_down=m_w_down, norm_final_g=m_norm_final_g)
    v_in = dict(norm_mix_g=v_norm_mix_g, w_in=v_w_in, b_forget=v_b_forget, gmlp_norm_g=v_gmlp_norm_g,
                w_spatial=v_w_spatial, b_spatial=v_b_spatial, w_out=v_w_out, norm_ffn_g=v_norm_ffn_g, w_up=v_w_up,
                conv_w=v_conv_w, conv_b=v_conv_b, w_down=v_w_down, norm_final_g=v_norm_final_g)
    order = list(weights)
    me = 4 * lax.axis_index("x") + 2 * lax.axis_index("y") + lax.axis_index("c")
    n_in, n_up = w_in.shape[2], w_up.shape[2]
    r_out, r_down = w_out.shape[1], w_down.shape[1]

    def with_mine(landed, mine):
        return lax.dynamic_update_index_in_dim(landed, mine, me, 0)

    up_blk = w_up[0].astype(BF16)
    rows_blk = jnp.concatenate([w_out[0].astype(BF16), w_down[0].astype(BF16)], axis=0)
    taps_blk = jnp.pad(conv_w[0], ((0, 5), (0, 0)))
    (in_all,) = _all_gather([w_in[0].astype(BF16)], "gather_w_in")
    in_all, rest_blocks = lax.optimization_barrier((in_all, [up_blk, rows_blk, taps_blk]))
    rest_handle, token = _xchg_start(rest_blocks, [False] * 3, "gather_rest_start")
    w_in_bf = in_all.transpose(1, 0, 2).reshape(D_MODEL, N_DEV * n_in)

    def rest_fn(after):
        mine, landed = _xchg_wait(rest_handle, [False] * 3, after, "gather_rest_wait")
        up_all, rows_all, taps_all = [with_mine(l, b) for l, b in zip(landed, mine)]
        return (rows_all[:, :r_out, :].reshape(N_DEV * r_out, D_MODEL),
                up_all.transpose(1, 0, 2).reshape(D_MODEL, N_DEV * n_up),
                taps_all[:, :3, :].transpose(1, 0, 2).reshape(3, N_DEV * n_up),
                rows_all[:, r_out:, :].reshape(N_DEV * r_down, D_MODEL))

    sent = {}

    def send_fn(name, grad):
        if name in ("w_in", "w_up"):
            parts = grad.reshape(D_MODEL, N_DEV, -1).transpose(1, 0, 2)
        else:
            parts = grad.reshape(N_DEV, -1, D_MODEL)
        sent[name], tok = _xchg_start([parts], [True], "scatter_" + name + "_start")
        return tok

    loss_local, grad_x, g = _local_step(
        x[0], loss_target[0], norm_mix_g, w_in_bf, b_forget, gmlp_norm_g, w_spatial, b_spatial, norm_ffn_g, conv_b,
        norm_final_g, rest_fn, send_fn, token)
    loss = lax.psum(loss_local, ("x", "y", "c"))

    got = {}
    for name, handle in sent.items():
        (parts,), (landed,) = _xchg_wait(handle, [True], grad_x, "scatter_" + name + "_wait")
        got[name] = with_mine(landed, lax.dynamic_index_in_dim(parts, me, 0, keepdims=False))
    got_in, got_up, got_out, got_down = got["w_in"], got["w_up"], got["w_out"], got["w_down"]

    small_names = SMALL + ("conv_w",)
    packed = [_rows128(g[k]) for k in small_names]
    sizes = [p.shape[0] for p in packed]
    (small_all,) = _all_gather([jnp.concatenate(packed, axis=0)], "gather_small_grads")

    def pack(src):
        return jnp.concatenate([_rows128(src[k]) for k in SMALL], axis=0)

    n_small_rows = sum(sizes[:-1])
    sg_, sd_, sm_, sv_ = _adamw(pack(weights), pack(m_in), pack(v_in), small_all[:, :n_small_rows, :], "adamw_small")

    outs = {}
    off = 0
    for k, rows in zip(SMALL, sizes[:-1]):
        shp = weights[k].shape
        cnt = math.prod(shp)
        outs[k] = tuple(a[off:off + rows].reshape(-1)[:cnt].reshape(shp) for a in (sg_, sd_, sm_, sv_))
        off += rows
    taps_parts = small_all[:, n_small_rows:, :].reshape(N_DEV, -1)[:, :3 * N_DEV * n_up].reshape(N_DEV, 3, N_DEV * n_up)
    taps_mine = lax.dynamic_slice_in_dim(taps_parts, me * n_up, n_up, axis=2)
    taps_mine = jnp.pad(taps_mine, ((0, 0), (0, 5), (0, 0)))

    def pad8(a):
        return jnp.pad(a[0], ((0, 5), (0, 0)))

    res = _adamw(pad8(conv_w), pad8(m_conv_w), pad8(v_conv_w), taps_mine, "adamw_conv_w")
    outs["conv_w"] = tuple(a[:3][None] for a in res)
    for k, got in (("w_in", got_in), ("w_up", got_up), ("w_out", got_out), ("w_down", got_down)):
        res = _adamw(weights[k][0], m_in[k][0], v_in[k][0], got, "adamw_" + k)
        outs[k] = tuple(a[None] for a in res)

    return (loss, grad_x[None], *[outs[k][0] for k in order], *[outs[k][1] for k in order],
            *[outs[k][2] for k in order], *[outs[k][3] for k in order])
```

```python
import functools
import math

import jax
import jax.numpy as jnp
from jax import lax
from jax.experimental import pallas as pl
from jax.experimental.pallas import tpu as pltpu

F32 = jnp.float32
BF16 = jnp.bfloat16

N_DEV = 8
D_MODEL = 1024
ATT_WIDTH = 512
GMLP_WIDTH = 512
HEAD_DIM = 64
N_HEADS = 8
N_PAIRS = 4
N_GROUPS = 8
GROUP_DIM = 64
CHUNK = 128
D_FF = 2816
IN_COLS = 2568
IN_PAD = 2688
QKV = 1536
UG_END = 2560
EPS = 1e-6
LANES = 128

ADAM_LR = 0.001
ADAM_B1 = 0.9
ADAM_B2 = 0.999
ADAM_EPS = 1e-08
ADAM_WD = 0.01
ADAM_STEP = 10

ATT_TQ = 1024
ATT_TK = 1024
VMEM_LIMIT = 56 * 1024 * 1024
MESH = pl.DeviceIdType.MESH


def _cp(sem, vmem=None):
    return pltpu.CompilerParams(dimension_semantics=sem, vmem_limit_bytes=vmem)


def _pick(n, prefs):
    for p in prefs:
        if n % p == 0:
            return p
    return n


def _split3(x):
    hi = x.astype(BF16)
    r1 = x - hi.astype(F32)
    mid = r1.astype(BF16)
    lo = (r1 - mid.astype(F32)).astype(BF16)
    return hi, mid, lo


def _dot3(x, ones_bf):
    hi, mid, lo = _split3(x)
    d = functools.partial(jnp.dot, preferred_element_type=F32)
    return d(hi, ones_bf) + d(mid, ones_bf) + d(lo, ones_bf)


def _dot3l(ones_bf, x):
    hi, mid, lo = _split3(x)
    d = functools.partial(jnp.dot, preferred_element_type=F32)
    return d(ones_bf, hi) + d(ones_bf, mid) + d(ones_bf, lo)


def _gelu(x):
    k = math.sqrt(2.0 / math.pi)
    t = jnp.tanh(k * (x + 0.044715 * (x * x * x)))
    return 0.5 * x * (1.0 + t)


def _gelu_grad(x):
    k = math.sqrt(2.0 / math.pi)
    x2 = x * x
    t = jnp.tanh(k * (x + 0.044715 * (x2 * x)))
    return 0.5 * (1.0 + t) + 0.5 * x * (1.0 - t * t) * (k * (1.0 + 3.0 * 0.044715 * x2))


def _sigmoid(x):
    return 1.0 / (1.0 + jnp.exp(-x))


def _mm(a, b, *, mode, out_dtype, tm, tn, tk, name, res=None, a_halves=False, b_halves=False,
        out_halves=False, outer="i"):
    if mode == "tn":
        K, M = a.shape[-2], a.shape[-1] * (2 if a_halves else 1)
    else:
        M, K = a.shape[-2], a.shape[-1] * (2 if a_halves else 1)
    if mode == "nt":
        N = b.shape[-2]
        assert b.shape[-1] == K
    else:
        N = b.shape[-1] * (2 if b_halves else 1)
    tm, tn, tk = min(tm, M), min(tn, N), min(tk, K)
    assert M % tm == 0 and N % tn == 0 and K % tk == 0, (name, M, N, K, tm, tn, tk)
    nm, nn, nk = M // tm, N // tn, K // tk

    def ij(g0, g1):
        return (g0, g1) if outer == "i" else (g1, g0)

    if mode == "nn":
        dims = (((1,), (0,)), ((), ()))
        if a_halves:
            nkh = nk // 2
            a_spec = pl.BlockSpec((None, tm, tk), lambda g0, g1, k: (k // nkh, ij(g0, g1)[0], k % nkh))
        else:
            a_spec = pl.BlockSpec((tm, tk), lambda g0, g1, k: (ij(g0, g1)[0], k))
        b_spec = pl.BlockSpec((tk, tn), lambda g0, g1, k: (k, ij(g0, g1)[1]))
    elif mode == "nt":
        dims = (((1,), (1,)), ((), ()))
        if a_halves:
            nkh = nk // 2
            a_spec = pl.BlockSpec((None, tm, tk), lambda g0, g1, k: (k // nkh, ij(g0, g1)[0], k % nkh))
        else:
            a_spec = pl.BlockSpec((tm, tk), lambda g0, g1, k: (ij(g0, g1)[0], k))
        b_spec = pl.BlockSpec((tn, tk), lambda g0, g1, k: (ij(g0, g1)[1], k))
    else:
        dims = (((0,), (0,)), ((), ()))
        if a_halves:
            nmh = nm // 2
            a_spec = pl.BlockSpec((None, tk, tm), lambda g0, g1, k: (ij(g0, g1)[0] // nmh, k, ij(g0, g1)[0] % nmh))
        else:
            a_spec = pl.BlockSpec((tk, tm), lambda g0, g1, k: (k, ij(g0, g1)[0]))
        if b_halves:
            nnh = nn // 2
            b_spec = pl.BlockSpec((None, tk, tn), lambda g0, g1, k: (ij(g0, g1)[1] // nnh, k, ij(g0, g1)[1] % nnh))
        else:
            b_spec = pl.BlockSpec((tk, tn), lambda g0, g1, k: (k, ij(g0, g1)[1]))
    if out_halves:
        nnh = nn // 2
        o_spec = pl.BlockSpec((None, tm, tn), lambda g0, g1, k: (ij(g0, g1)[1] // nnh, ij(g0, g1)[0], ij(g0, g1)[1] % nnh))
        o_shape = jax.ShapeDtypeStruct((2, M, N // 2), out_dtype)
    else:
        o_spec = pl.BlockSpec((tm, tn), lambda g0, g1, k: ij(g0, g1))
        o_shape = jax.ShapeDtypeStruct((M, N), out_dtype)
    in_specs = [a_spec, b_spec]
    args = [a, b]
    if res is not None:
        in_specs.append(pl.BlockSpec((tm, tn), lambda g0, g1, k: ij(g0, g1)))
        args.append(res)

    def body(*refs):
        if res is not None:
            a_ref, b_ref, r_ref, o_ref = refs[:4]
        else:
            a_ref, b_ref, o_ref = refs[:3]
            r_ref = None
        part = lax.dot_general(a_ref[...], b_ref[...], dims, preferred_element_type=F32)
        if nk == 1:
            if r_ref is not None:
                part = part + r_ref[...]
            o_ref[...] = part.astype(out_dtype)
            return
        acc_ref = refs[-1]
        k = pl.program_id(2)

        @pl.when(k == 0)
        def _():
            acc_ref[...] = part

        @pl.when(k > 0)
        def _():
            acc_ref[...] += part

        @pl.when(k == nk - 1)
        def _():
            tot = acc_ref[...]
            if r_ref is not None:
                tot = tot + r_ref[...]
            o_ref[...] = tot.astype(out_dtype)

    grid = (nm, nn, nk) if outer == "i" else (nn, nm, nk)
    scratch = [] if nk == 1 else [pltpu.VMEM((tm, tn), F32)]
    return pl.pallas_call(
        body, out_shape=o_shape, grid=grid, in_specs=in_specs, out_specs=o_spec, scratch_shapes=scratch,
        name=name, compiler_params=_cp(("parallel", "parallel", "arbitrary"), VMEM_LIMIT),
    )(*args)


def _aug(lane, terms):
    out = 0.0
    for j, t in enumerate(terms):
        out = jnp.where(lane == HEAD_DIM + j, t, out)
    return out


def _split3f(x):
    hi, mid, lo = _split3(x)
    return [hi.astype(F32), mid.astype(F32), lo.astype(F32)]


def _inproj_fwd(x, g_mix, w_pad, bf_pad):
    S = x.shape[0]
    tm = _pick(S, (256,))
    tri = (lax.broadcasted_iota(jnp.int32, (tm, tm), 0) >= lax.broadcasted_iota(jnp.int32, (tm, tm), 1)).astype(BF16)

    def body(x_ref, g_ref, w_ref, bf_ref, tri_ref, xn_ref, qa_ref, ka_ref, va_ref, ug_ref, zf_ref, carry_ref):
        i = pl.program_id(0)

        @pl.when(i == 0)
        def _():
            carry_ref[...] = jnp.zeros_like(carry_ref)

        xf = x_ref[...]
        r = lax.rsqrt(jnp.mean(xf * xf, axis=-1, keepdims=True) + EPS)
        xn = ((xf * r) * g_ref[...]).astype(BF16)
        xn_ref[...] = xn
        proj = jnp.dot(xn, w_ref[...], preferred_element_type=F32)
        ug_ref[...] = proj[:, QKV:UG_END]
        zf = proj[:, UG_END:] + bf_ref[...]
        zf_ref[...] = zf
        lf = jnp.minimum(zf, 0.0) - jnp.log(1.0 + jnp.exp(-jnp.abs(zf)))
        c = _dot3l(tri_ref[...], lf) + carry_ref[0:1, :]
        carry_ref[0:1, :] = c[tm - 1:tm, :]
        c3 = _split3f(c)
        lane = lax.broadcasted_iota(jnp.int32, (tm, LANES), 1)
        ones3 = [1.0, 1.0, 1.0]
        for h in range(N_HEADS):
            p, odd = h // 2, h % 2
            ch = [t[:, h:h + 1] for t in c3]

            def head(base, scale=None, p=p, odd=odd):
                blk = proj[:, base + p * LANES:base + (p + 1) * LANES]
                if scale is not None:
                    blk = blk * scale
                return pltpu.roll(blk, HEAD_DIM, 1) if odd else blk

            cols = slice(h * LANES, (h + 1) * LANES)
            qa_ref[:, cols] = jnp.where(lane < HEAD_DIM, head(0, HEAD_DIM ** -0.5), _aug(lane, ch + ones3)).astype(BF16)
            ka_ref[:, cols] = jnp.where(lane < HEAD_DIM, head(ATT_WIDTH),
                                        _aug(lane, ones3 + [-t for t in ch] + ones3)).astype(BF16)
            va_ref[:, cols] = jnp.where(lane < HEAD_DIM, head(2 * ATT_WIDTH), _aug(lane, ones3)).astype(BF16)

    wide = N_HEADS * LANES
    return pl.pallas_call(
        body,
        out_shape=(jax.ShapeDtypeStruct((S, D_MODEL), BF16), jax.ShapeDtypeStruct((S, wide), BF16),
                   jax.ShapeDtypeStruct((S, wide), BF16), jax.ShapeDtypeStruct((S, wide), BF16),
                   jax.ShapeDtypeStruct((S, 2 * GMLP_WIDTH), F32), jax.ShapeDtypeStruct((S, LANES), F32)),
        grid=(S // tm,),
        in_specs=[pl.BlockSpec((tm, D_MODEL), lambda i: (i, 0)), pl.BlockSpec((1, D_MODEL), lambda i: (0, 0)),
                  pl.BlockSpec((D_MODEL, IN_PAD), lambda i: (0, 0)), pl.BlockSpec((1, LANES), lambda i: (0, 0)),
                  pl.BlockSpec((tm, tm), lambda i: (0, 0))],
        out_specs=(pl.BlockSpec((tm, D_MODEL), lambda i: (i, 0)), pl.BlockSpec((tm, wide), lambda i: (i, 0)),
                   pl.BlockSpec((tm, wide), lambda i: (i, 0)), pl.BlockSpec((tm, wide), lambda i: (i, 0)),
                   pl.BlockSpec((tm, 2 * GMLP_WIDTH), lambda i: (i, 0)), pl.BlockSpec((tm, LANES), lambda i: (i, 0))),
        scratch_shapes=[pltpu.VMEM((8, LANES), F32)],
        name="inproj_fwd", compiler_params=_cp(("arbitrary",), VMEM_LIMIT),
    )(x, g_mix, w_pad, bf_pad, tri)


def _group_ones():
    r = lax.broadcasted_iota(jnp.int32, (GMLP_WIDTH, GMLP_WIDTH), 0) // GROUP_DIM
    c = lax.broadcasted_iota(jnp.int32, (GMLP_WIDTH, GMLP_WIDTH), 1) // GROUP_DIM
    return (r == c).astype(BF16)


def _gmlp_mixed(vn_bf, w_ref, bias, n_chunks):
    lane = lax.broadcasted_iota(jnp.int32, (CHUNK, LANES), 1)
    row = lax.broadcasted_iota(jnp.int32, (CHUNK, CHUNK), 0)
    col = lax.broadcasted_iota(jnp.int32, (CHUNK, CHUNK), 1)
    ws = [jnp.where(row >= col, w_ref[g], 0.0).astype(BF16) for g in range(N_GROUPS)]
    rows = []
    for ci in range(n_chunks):
        cols = []
        for pp in range(N_GROUPS // 2):
            v = vn_bf[ci * CHUNK:(ci + 1) * CHUNK, pp * LANES:(pp + 1) * LANES]
            v_lo = jnp.where(lane < GROUP_DIM, v, jnp.zeros_like(v))
            v_hi = jnp.where(lane >= GROUP_DIM, v, jnp.zeros_like(v))
            m = (jnp.dot(ws[2 * pp], v_lo, preferred_element_type=F32)
                 + jnp.dot(ws[2 * pp + 1], v_hi, preferred_element_type=F32))
            cols.append(m + bias[:, pp * LANES:(pp + 1) * LANES])
        rows.append(jnp.concatenate(cols, axis=1))
    return jnp.concatenate(rows, axis=0)


def _gmlp_fwd(ug, gain, w_s, bias_full):
    S = ug.shape[0]
    tm = _pick(S, (512, 256, 128))
    ones = _group_ones()

    def body(ug_ref, gain_ref, w_ref, bias_ref, ones_ref, sg_ref):
        u = _gelu(ug_ref[:, :GMLP_WIDTH])
        vr = _gelu(ug_ref[:, GMLP_WIDTH:])
        ms = _dot3(vr * vr, ones_ref[...]) * (1.0 / GROUP_DIM)
        vn = ((vr * lax.rsqrt(ms + EPS)) * gain_ref[...]).astype(BF16)
        mixed = _gmlp_mixed(vn, w_ref, bias_ref[...], tm // CHUNK)
        sg_ref[...] = (u * mixed).astype(BF16)

    return pl.pallas_call(
        body, out_shape=jax.ShapeDtypeStruct((S, GMLP_WIDTH), BF16), grid=(S // tm,),
        in_specs=[pl.BlockSpec((tm, 2 * GMLP_WIDTH), lambda i: (i, 0)), pl.BlockSpec((1, GMLP_WIDTH), lambda i: (0, 0)),
                  pl.BlockSpec((N_GROUPS, CHUNK, CHUNK), lambda i: (0, 0, 0)),
                  pl.BlockSpec((CHUNK, GMLP_WIDTH), lambda i: (0, 0)),
                  pl.BlockSpec((GMLP_WIDTH, GMLP_WIDTH), lambda i: (0, 0))],
        out_specs=pl.BlockSpec((tm, GMLP_WIDTH), lambda i: (i, 0)),
        name="gmlp_fwd", compiler_params=_cp(("parallel",), VMEM_LIMIT),
    )(ug, gain, w_s, bias_full, ones)


_NT = (((1,), (1,)), ((), ()))
_TN = (((0,), (0,)), ((), ()))


def _attn_fwd(qa, ka, va):
    S = qa.shape[0]
    tq = _pick(S, (ATT_TQ, 256))
    tk = min(ATT_TK, tq)
    nq = S // tq
    per_q = tq // tk

    def body(q_ref, k_ref, v_ref, o_ref, lse_ref, ob_ref):
        qi = pl.program_id(1)
        lane = lax.broadcasted_iota(jnp.int32, (tq, LANES), 1)
        rid = lax.broadcasted_iota(jnp.int32, (tq, tk), 0)
        cid = lax.broadcasted_iota(jnp.int32, (tq, tk), 1)
        qs = [q_ref[:, :LANES], q_ref[:, LANES:]]

        def update(kb, h, m, acc, diag):
            ks = pl.multiple_of(kb * tk, tk)
            cols = slice(h * LANES, (h + 1) * LANES)
            s = lax.dot_general(qs[h], k_ref[pl.ds(ks, tk), cols], _NT, preferred_element_type=F32)
            if diag is not None:
                s = jnp.where(rid >= cid + diag * tk, s, -jnp.inf)
            m_new = jnp.maximum(m, jnp.max(s, axis=-1, keepdims=True))
            p = jnp.exp(s - m_new).astype(BF16)
            acc = jnp.exp(m - m_new) * acc + jnp.dot(p, v_ref[pl.ds(ks, tk), cols], preferred_element_type=F32)
            return m_new, acc

        def step(kb, carry):
            return tuple(update(kb, h, *carry[h], None) for h in range(2))

        one = (jnp.full((tq, 1), -jnp.inf, F32), jnp.zeros((tq, LANES), F32))
        carry = lax.fori_loop(0, qi * per_q, step, (one, one))
        outs, lses = [], []
        for h in range(2):
            m, acc = carry[h]
            for d in range(per_q):
                m, acc = update(qi * per_q + d, h, m, acc, d)
            l = acc[:, HEAD_DIM:HEAD_DIM + 1]
            outs.append(acc / l)
            lses.append(m + jnp.log(l))
        o = jnp.where(lane < HEAD_DIM, outs[0], pltpu.roll(outs[1], HEAD_DIM, 1))
        o_ref[...] = o
        ob_ref[...] = o.astype(BF16)
        lse_ref[...] = jnp.where(lane < HEAD_DIM, lses[0], lses[1])

    return pl.pallas_call(
        body,
        out_shape=(jax.ShapeDtypeStruct((S, ATT_WIDTH), F32), jax.ShapeDtypeStruct((S, ATT_WIDTH), F32),
                   jax.ShapeDtypeStruct((S, ATT_WIDTH), BF16)),
        grid=(N_PAIRS, nq),
        in_specs=[pl.BlockSpec((tq, 2 * LANES), lambda p, i: (i, p)),
                  pl.BlockSpec((S, 2 * LANES), lambda p, i: (0, p)),
                  pl.BlockSpec((S, 2 * LANES), lambda p, i: (0, p))],
        out_specs=(pl.BlockSpec((tq, LANES), lambda p, i: (i, p)), pl.BlockSpec((tq, LANES), lambda p, i: (i, p)),
                   pl.BlockSpec((tq, LANES), lambda p, i: (i, p))),
        name="attn_fwd", compiler_params=_cp(("parallel", "parallel"), VMEM_LIMIT),
    )(qa, ka, va)


def _rms_fwd(h, g):
    S = h.shape[0]
    tm = _pick(S, (512, 256))

    def body(h_ref, g_ref, o_ref):
        hf = h_ref[...]
        r = lax.rsqrt(jnp.mean(hf * hf, axis=-1, keepdims=True) + EPS)
        o_ref[...] = ((hf * r) * g_ref[...]).astype(BF16)

    return pl.pallas_call(
        body, out_shape=jax.ShapeDtypeStruct(h.shape, BF16), grid=(S // tm,),
        in_specs=[pl.BlockSpec((tm, D_MODEL), lambda i: (i, 0)), pl.BlockSpec((1, D_MODEL), lambda i: (0, 0))],
        out_specs=pl.BlockSpec((tm, D_MODEL), lambda i: (i, 0)),
        name="rms_fwd", compiler_params=_cp(("parallel",)),
    )(h, g)


def _shift_rows(x, prev, n):
    rid = lax.broadcasted_iota(jnp.int32, x.shape, 0)
    y = pltpu.roll(x, n, 0)
    if n == 1:
        return jnp.where(rid == 0, prev[7:8, :], y)
    return jnp.where(rid == 0, prev[6:7, :], jnp.where(rid == 1, prev[7:8, :], y))


def _shift_rows_up(x, nxt, n):
    rows = x.shape[0]
    rid = lax.broadcasted_iota(jnp.int32, x.shape, 0)
    y = pltpu.roll(x, rows - n, 0)
    if n == 1:
        return jnp.where(rid == rows - 1, nxt[0:1, :], y)
    return jnp.where(rid == rows - 2, nxt[0:1, :], jnp.where(rid == rows - 1, nxt[1:2, :], y))


def _conv3(cur, prev, w, b):
    return (w[0:1, :] * _shift_rows(cur, prev, 2) + w[1:2, :] * _shift_rows(cur, prev, 1)
            + w[2:3, :] * cur + b)


def _conv_act_fwd(hu, cw, cb):
    _, S, F = hu.shape
    tm = _pick(S, (512, 256))
    tn = _pick(F, (256, 128))
    r8 = tm // 8

    def body(cur_ref, prev_ref, w_ref, b_ref, o_ref):
        i = pl.program_id(1)
        halves = []
        for h in range(2):
            prev = jnp.where(i > 0, prev_ref[h], 0.0)
            halves.append(_conv3(cur_ref[h], prev, w_ref[h], b_ref[h]))
        a, g = halves
        o_ref[...] = (g * _sigmoid(g) * a).astype(BF16)

    return pl.pallas_call(
        body, out_shape=jax.ShapeDtypeStruct((S, F), BF16), grid=(F // tn, S // tm),
        in_specs=[pl.BlockSpec((2, tm, tn), lambda j, i: (0, i, j)),
                  pl.BlockSpec((2, 8, tn), lambda j, i: (0, jnp.maximum(i * r8 - 1, 0), j)),
                  pl.BlockSpec((2, 8, tn), lambda j, i: (0, 0, j)),
                  pl.BlockSpec((2, 1, tn), lambda j, i: (0, 0, j))],
        out_specs=pl.BlockSpec((tm, tn), lambda j, i: (i, j)),
        name="conv_act_fwd", compiler_params=_cp(("parallel", "parallel"), VMEM_LIMIT),
    )(hu, hu, cw, cb)


def _ffn_up_conv(hn, w_up_bf, cw, cb):
    S = hn.shape[0]
    F = D_FF
    tm = _pick(S, (1024, 512))
    tn = _pick(F, (256, 128))
    nj = F // tn

    def body(hn_ref, wa_ref, wg_ref, cw_ref, cb_ref, hu_ref, act_ref, tail_ref):
        i = pl.program_id(1)

        @pl.when(i == 0)
        def _():
            tail_ref[...] = jnp.zeros_like(tail_ref)

        hn_v = hn_ref[...]
        halves = []
        for h, w_ref in enumerate((wa_ref, wg_ref)):
            hu = jnp.dot(hn_v, w_ref[...], preferred_element_type=F32)
            hu_ref[h] = hu
            halves.append(_conv3(hu, tail_ref[h], cw_ref[h], cb_ref[h]))
            tail_ref[h] = hu[tm - 8:, :]
        a, g = halves
        act_ref[...] = (g * _sigmoid(g) * a).astype(BF16)

    return pl.pallas_call(
        body, out_shape=(jax.ShapeDtypeStruct((2, S, F), F32), jax.ShapeDtypeStruct((S, F), BF16)),
        grid=(nj, S // tm),
        in_specs=[pl.BlockSpec((tm, D_MODEL), lambda j, i: (i, 0)),
                  pl.BlockSpec((D_MODEL, tn), lambda j, i: (0, j)),
                  pl.BlockSpec((D_MODEL, tn), lambda j, i: (0, nj + j)),
                  pl.BlockSpec((2, 8, tn), lambda j, i: (0, 0, j)),
                  pl.BlockSpec((2, 1, tn), lambda j, i: (0, 0, j))],
        out_specs=(pl.BlockSpec((2, tm, tn), lambda j, i: (0, i, j)), pl.BlockSpec((tm, tn), lambda j, i: (i, j))),
        scratch_shapes=[pltpu.VMEM((2, 8, tn), F32)],
        name="ffn_up_conv", compiler_params=_cp(("parallel", "arbitrary"), VMEM_LIMIT),
    )(hn, w_up_bf, w_up_bf, cw, cb)


def _loss_head(h2, g_final, target):
    S = h2.shape[0]
    tm = _pick(S, (512, 256))

    def body(h_ref, g_ref, t_ref, loss_ref, dh_ref, dhb_ref, dg_ref):
        i = pl.program_id(0)

        @pl.when(i == 0)
        def _():
            loss_ref[...] = jnp.zeros_like(loss_ref)
            dg_ref[...] = jnp.zeros_like(dg_ref)

        hf = h_ref[...]
        g = g_ref[...]
        r = lax.rsqrt(jnp.mean(hf * hf, axis=-1, keepdims=True) + EPS)
        hhat = hf * r
        err = hhat * g - t_ref[...]
        loss_ref[...] += 0.5 * jnp.sum(jnp.mean(err * err, axis=-1, keepdims=True))
        dy = err * (1.0 / D_MODEL)
        dg_ref[0:1, :] += jnp.sum(dy * hhat, axis=0, keepdims=True)
        dhat = dy * g
        dh = r * (dhat - hhat * jnp.mean(dhat * hhat, axis=-1, keepdims=True))
        dh_ref[...] = dh
        dhb_ref[...] = dh.astype(BF16)

    return pl.pallas_call(
        body,
        out_shape=(jax.ShapeDtypeStruct((8, LANES), F32), jax.ShapeDtypeStruct((S, D_MODEL), F32),
                   jax.ShapeDtypeStruct((S, D_MODEL), BF16), jax.ShapeDtypeStruct((8, D_MODEL), F32)),
        grid=(S // tm,),
        in_specs=[pl.BlockSpec((tm, D_MODEL), lambda i: (i, 0)), pl.BlockSpec((1, D_MODEL), lambda i: (0, 0)),
                  pl.BlockSpec((tm, D_MODEL), lambda i: (i, 0))],
        out_specs=(pl.BlockSpec((8, LANES), lambda i: (0, 0)), pl.BlockSpec((tm, D_MODEL), lambda i: (i, 0)),
                   pl.BlockSpec((tm, D_MODEL), lambda i: (i, 0)), pl.BlockSpec((8, D_MODEL), lambda i: (0, 0))),
        name="loss_head", compiler_params=_cp(("arbitrary",), VMEM_LIMIT),
    )(h2, g_final, target)


def _rms_bwd(h, g, dy, res):
    S = h.shape[0]
    tm = _pick(S, (512, 256))

    def body(h_ref, g_ref, dy_ref, r_ref, dh_ref, dhb_ref, dg_ref):
        i = pl.program_id(0)

        @pl.when(i == 0)
        def _():
            dg_ref[...] = jnp.zeros_like(dg_ref)

        hf = h_ref[...]
        dyv = dy_ref[...]
        r = lax.rsqrt(jnp.mean(hf * hf, axis=-1, keepdims=True) + EPS)
        hhat = hf * r
        dg_ref[0:1, :] += jnp.sum(dyv * hhat, axis=0, keepdims=True)
        dhat = dyv * g_ref[...]
        dh = r_ref[...] + r * (dhat - hhat * jnp.mean(dhat * hhat, axis=-1, keepdims=True))
        dh_ref[...] = dh
        dhb_ref[...] = dh.astype(BF16)

    return pl.pallas_call(
        body,
        out_shape=(jax.ShapeDtypeStruct((S, D_MODEL), F32), jax.ShapeDtypeStruct((S, D_MODEL), BF16),
                   jax.ShapeDtypeStruct((8, D_MODEL), F32)),
        grid=(S // tm,),
        in_specs=[pl.BlockSpec((tm, D_MODEL), lambda i: (i, 0)), pl.BlockSpec((1, D_MODEL), lambda i: (0, 0)),
                  pl.BlockSpec((tm, D_MODEL), lambda i: (i, 0)), pl.BlockSpec((tm, D_MODEL), lambda i: (i, 0))],
        out_specs=(pl.BlockSpec((tm, D_MODEL), lambda i: (i, 0)), pl.BlockSpec((tm, D_MODEL), lambda i: (i, 0)),
                   pl.BlockSpec((8, D_MODEL), lambda i: (0, 0))),
        name="rms_bwd", compiler_params=_cp(("arbitrary",), VMEM_LIMIT),
    )(h, g, dy, res)


def _conv_act_bwd(hu, dh2_bf, w_down_bf, cw, cb):
    _, S, F = hu.shape
    tm = _pick(S, (256,))
    tn = _pick(F, (256, 128))
    r8 = tm // 8
    r16 = tm // 16
    n_i = S // tm
    last8 = S // 8 - 1
    last16 = S // 16 - 1

    def body(cur_ref, prev_ref, next_ref, dy_ref, dyn_ref, wd_ref, w_ref, b_ref, dhu_ref, dcw_ref):
        i = pl.program_id(1)
        wd = wd_ref[...]
        dact = lax.dot_general(dy_ref[...], wd, _NT, preferred_element_type=F32)
        dact_n = lax.dot_general(dyn_ref[...], wd, _NT, preferred_element_type=F32)[0:8, :]

        @pl.when(i == 0)
        def _():
            dcw_ref[...] = jnp.zeros_like(dcw_ref)

        rid8 = lax.broadcasted_iota(jnp.int32, (8, tn), 0)
        cur = [cur_ref[0], cur_ref[1]]
        prev = [jnp.where(i > 0, prev_ref[h], 0.0) for h in range(2)]
        nxt = [next_ref[0], next_ref[1]]
        w = [w_ref[0], w_ref[1]]

        def gate_grads(a, g, d):
            sg = _sigmoid(g)
            return d * (g * sg), d * a * (sg * (1.0 + g * (1.0 - sg)))

        taps = [(_shift_rows(cur[h], prev[h], 2), _shift_rows(cur[h], prev[h], 1), cur[h]) for h in range(2)]
        a, g = [w[h][0:1, :] * taps[h][0] + w[h][1:2, :] * taps[h][1] + w[h][2:3, :] * taps[h][2] + b_ref[h]
                for h in range(2)]
        dhc = gate_grads(a, g, dact)
        a_n = _conv3(nxt[0], cur[0][tm - 8:, :], w[0], b_ref[0])
        g_n = _conv3(nxt[1], cur[1][tm - 8:, :], w[1], b_ref[1])
        dhc_n = gate_grads(a_n, g_n, dact_n)
        for h in range(2):
            d = dhc[h]
            dn = jnp.where(i < n_i - 1, dhc_n[h], 0.0)
            dhu = (w[h][2:3, :] * d + w[h][1:2, :] * _shift_rows_up(d, dn, 1)
                   + w[h][0:1, :] * _shift_rows_up(d, dn, 2))
            dhu_ref[h] = dhu.astype(BF16)
            t0, t1, t2 = [jnp.sum(d * t, axis=0, keepdims=True) for t in taps[h]]
            t3 = jnp.sum(d, axis=0, keepdims=True)
            dcw_ref[h] += jnp.where(rid8 == 0, t0, jnp.where(rid8 == 1, t1, jnp.where(rid8 == 2, t2, jnp.where(rid8 == 3, t3, 0.0))))

    return pl.pallas_call(
        body,
        out_shape=(jax.ShapeDtypeStruct((2, S, F), BF16), jax.ShapeDtypeStruct((2, 8, F), F32)),
        grid=(F // tn, n_i),
        in_specs=[pl.BlockSpec((2, tm, tn), lambda j, i: (0, i, j)),
                  pl.BlockSpec((2, 8, tn), lambda j, i: (0, jnp.maximum(i * r8 - 1, 0), j)),
                  pl.BlockSpec((2, 8, tn), lambda j, i: (0, jnp.minimum((i + 1) * r8, last8), j)),
                  pl.BlockSpec((tm, D_MODEL), lambda j, i: (i, 0)),
                  pl.BlockSpec((16, D_MODEL), lambda j, i: (jnp.minimum((i + 1) * r16, last16), 0)),
                  pl.BlockSpec((tn, D_MODEL), lambda j, i: (j, 0)),
                  pl.BlockSpec((2, 8, tn), lambda j, i: (0, 0, j)),
                  pl.BlockSpec((2, 1, tn), lambda j, i: (0, 0, j))],
        out_specs=(pl.BlockSpec((2, tm, tn), lambda j, i: (0, i, j)), pl.BlockSpec((2, 8, tn), lambda j, i: (0, 0, j))),
        name="conv_act_bwd", compiler_params=_cp(("parallel", "arbitrary"), VMEM_LIMIT),
    )(hu, hu, hu, dh2_bf, dh2_bf, w_down_bf, cw, cb)


def _attn_prep(att, lse, dmix, qa):
    S = att.shape[0]
    tm = _pick(S, (512, 256))

    def body(o_ref, lse_ref, do_ref, q_ref, qb_ref, doa_ref):
        lane = lax.broadcasted_iota(jnp.int32, (tm, LANES), 1)
        do = do_ref[...]
        prod = o_ref[...] * do
        for hh in range(2):
            sel = (lane >= HEAD_DIM) if hh else (lane < HEAD_DIM)
            delta = jnp.sum(jnp.where(sel, prod, 0.0), axis=-1, keepdims=True)
            dod = pltpu.roll(do, HEAD_DIM, 1) if hh else do
            cols = slice(hh * LANES, (hh + 1) * LANES)
            doa_ref[:, cols] = jnp.where(lane < HEAD_DIM, dod, _aug(lane, _split3f(-delta))).astype(BF16)
            l3 = _split3f(-lse_ref[:, hh * HEAD_DIM:hh * HEAD_DIM + 1])
            augl = _aug(lane, [0.0] * 6 + l3).astype(BF16)
            qb_ref[:, cols] = jnp.where((lane >= HEAD_DIM + 6) & (lane < HEAD_DIM + 9), augl, q_ref[:, cols])

    return pl.pallas_call(
        body,
        out_shape=(jax.ShapeDtypeStruct(qa.shape, BF16), jax.ShapeDtypeStruct(qa.shape, BF16)),
        grid=(S // tm, N_PAIRS),
        in_specs=[pl.BlockSpec((tm, LANES), lambda i, p: (i, p)), pl.BlockSpec((tm, LANES), lambda i, p: (i, p)),
                  pl.BlockSpec((tm, LANES), lambda i, p: (i, p)), pl.BlockSpec((tm, 2 * LANES), lambda i, p: (i, p))],
        out_specs=(pl.BlockSpec((tm, 2 * LANES), lambda i, p: (i, p)), pl.BlockSpec((tm, 2 * LANES), lambda i, p: (i, p))),
        name="attn_prep", compiler_params=_cp(("parallel", "parallel")),
    )(att, lse, dmix, qa)


def _attn_bwd(qb, ka, va, doa):
    S = qb.shape[0]
    tk = _pick(S, (512, 256))
    tq = tk
    nq = S // tq

    def pair(a, scale=None):
        lane = lax.broadcasted_iota(jnp.int32, (a.shape[0], LANES), 1)
        out = jnp.where(lane < HEAD_DIM, a[:, :LANES], pltpu.roll(a[:, LANES:], HEAD_DIM, 1))
        return out if scale is None else out * scale

    def lanes01(a, col, sign):
        lane = lax.broadcasted_iota(jnp.int32, (a.shape[0], LANES), 1)
        return jnp.where(lane == 0, sign * a[:, col:col + 1], jnp.where(lane == 1, sign * a[:, LANES + col:LANES + col + 1], 0.0))

    def body(q_ref, do_ref, k_ref, v_ref, dqc_ref, dkc_ref, dvc_ref, dcq_ref, dck_ref, dq_ref, dka_ref, dva_ref):
        kb = pl.program_id(1)

        @pl.when(kb == 0)
        def _():
            dq_ref[...] = jnp.zeros_like(dq_ref)

        dka_ref[...] = jnp.zeros_like(dka_ref)
        dva_ref[...] = jnp.zeros_like(dva_ref)
        rid = lax.broadcasted_iota(jnp.int32, (tk, tq), 0)
        cid = lax.broadcasted_iota(jnp.int32, (tk, tq), 1)

        def tile(qi, masked):
            qs = pl.multiple_of(qi * tq, tq)
            for h in range(2):
                cols = slice(h * LANES, (h + 1) * LANES)
                qblk = q_ref[pl.ds(qs, tq), cols]
                doblk = do_ref[pl.ds(qs, tq), cols]
                kh = k_ref[:, cols]
                p = jnp.exp(lax.dot_general(kh, qblk, _NT, preferred_element_type=F32))
                if masked:
                    p = jnp.where(cid >= rid, p, 0.0)
                ds = (p * lax.dot_general(v_ref[:, cols], doblk, _NT, preferred_element_type=F32)).astype(BF16)
                dva_ref[:, cols] += jnp.dot(p.astype(BF16), doblk, preferred_element_type=F32)
                dka_ref[:, cols] += jnp.dot(ds, qblk, preferred_element_type=F32)
                dq_ref[pl.ds(qs, tq), cols] += lax.dot_general(ds, kh, _TN, preferred_element_type=F32)

        tile(kb, True)

        def step(qi, carry):
            tile(qi, False)
            return carry

        lax.fori_loop(kb + 1, nq, step, 0)
        dka = dka_ref[...]
        dkc_ref[...] = pair(dka).astype(BF16)
        dvc_ref[...] = pair(dva_ref[...]).astype(BF16)
        dck_ref[...] = lanes01(dka, HEAD_DIM + 3, -1.0)

        @pl.when(kb == nq - 1)
        def _():
            dqa = dq_ref[...]
            dqc_ref[...] = pair(dqa, HEAD_DIM ** -0.5).astype(BF16)
            dcq_ref[...] = lanes01(dqa, HEAD_DIM, 1.0)

    wide = 2 * LANES
    half = jax.ShapeDtypeStruct((S, ATT_WIDTH), BF16)
    slabs = jax.ShapeDtypeStruct((N_PAIRS, S, LANES), F32)
    return pl.pallas_call(
        body,
        out_shape=(half, half, half, slabs, slabs),
        grid=(N_PAIRS, nq),
        in_specs=[pl.BlockSpec((S, wide), lambda p, j: (0, p)), pl.BlockSpec((S, wide), lambda p, j: (0, p)),
                  pl.BlockSpec((tk, wide), lambda p, j: (j, p)), pl.BlockSpec((tk, wide), lambda p, j: (j, p))],
        out_specs=(pl.BlockSpec((S, LANES), lambda p, j: (0, p)), pl.BlockSpec((tk, LANES), lambda p, j: (j, p)),
                   pl.BlockSpec((tk, LANES), lambda p, j: (j, p)), pl.BlockSpec((None, S, LANES), lambda p, j: (p, 0, 0)),
                   pl.BlockSpec((None, tk, LANES), lambda p, j: (p, j, 0))),
        scratch_shapes=[pltpu.VMEM((S, wide), F32), pltpu.VMEM((tk, wide), F32), pltpu.VMEM((tk, wide), F32)],
        name="attn_bwd", compiler_params=_cp(("parallel", "arbitrary"), VMEM_LIMIT),
    )(qb, doa, ka, va)


def _attn_delta_old(o, do):
    S = o.shape[0]
    tm = _pick(S, (512, 256))
    ones = _group_ones()

    def body(o_ref, do_ref, ones_ref, d_ref):
        d_ref[...] = _dot3(o_ref[...] * do_ref[...], ones_ref[...])

    return pl.pallas_call(
        body, out_shape=jax.ShapeDtypeStruct((S, ATT_WIDTH), F32), grid=(S // tm,),
        in_specs=[pl.BlockSpec((tm, ATT_WIDTH), lambda i: (i, 0)), pl.BlockSpec((tm, ATT_WIDTH), lambda i: (i, 0)),
                  pl.BlockSpec((ATT_WIDTH, ATT_WIDTH), lambda i: (0, 0))],
        out_specs=pl.BlockSpec((tm, ATT_WIDTH), lambda i: (i, 0)),
        name="attn_delta", compiler_params=_cp(("parallel",)),
    )(o, do, ones)


def _attn_bwd_old(qkv, do_bf, c_cols, c_rows, lse_rows, dl_rows):
    S = qkv.shape[0]
    tk = _pick(S, (256,))
    tq = tk
    nq = S // tq
    nt = (((1,), (1,)), ((), ()))
    tn_dims = (((0,), (0,)), ((), ()))

    def body(q_ref, do_ref, k_ref, v_ref, cc_ref, cr_ref, lse_ref, dl_ref, dq_ref, dk_ref, dv_ref, dcs_ref, dcq_ref):
        kb = pl.program_id(1)

        @pl.when(kb == 0)
        def _():
            dq_ref[...] = jnp.zeros_like(dq_ref)
            dcq_ref[...] = jnp.zeros_like(dcq_ref)

        lane = lax.broadcasted_iota(jnp.int32, (tk, LANES), 1)
        rid = lax.broadcasted_iota(jnp.int32, (tk, tq), 0)
        cid = lax.broadcasted_iota(jnp.int32, (tk, tq), 1)
        k = k_ref[...]
        v = v_ref[...]
        dks, dvs, dcs = [], [], []
        for hh in range(2):
            sel = (lane >= GROUP_DIM) if hh else (lane < GROUP_DIM)
            km = jnp.where(sel, k, jnp.zeros_like(k))
            vm = jnp.where(sel, v, jnp.zeros_like(v))
            cs = cc_ref[:, hh * HEAD_DIM:hh * HEAD_DIM + 1]

            def step(qb, carry, km=km, vm=vm, cs=cs, hh=hh):
                dk_acc, dv_acc, dc_acc = carry
                qs = pl.multiple_of(qb * tq, tq)
                qblk = q_ref[pl.ds(qs, tq), :]
                doblk = do_ref[pl.ds(qs, tq), :]
                s = lax.dot_general(km, qblk, nt, preferred_element_type=F32)
                s = s + (cr_ref[hh:hh + 1, pl.ds(qs, tq)] - cs)
                p = jnp.exp(s - lse_ref[hh:hh + 1, pl.ds(qs, tq)])
                p = jnp.where((qb > kb) | (cid >= rid), p, 0.0)
                dp = lax.dot_general(vm, doblk, nt, preferred_element_type=F32)
                ds = p * (dp - dl_ref[hh:hh + 1, pl.ds(qs, tq)])
                ds_bf = ds.astype(BF16)
                dv_acc = dv_acc + jnp.dot(p.astype(BF16), doblk, preferred_element_type=F32)
                dk_acc = dk_acc + jnp.dot(ds_bf, qblk, preferred_element_type=F32)
                dc_acc = dc_acc + jnp.sum(ds, axis=-1, keepdims=True)
                dq_ref[pl.ds(qs, tq), :] += lax.dot_general(ds_bf, km, tn_dims, preferred_element_type=F32)
                dcq_ref[hh:hh + 1, pl.ds(qs, tq)] += jnp.sum(ds, axis=0, keepdims=True)
                return dk_acc, dv_acc, dc_acc

            init = (jnp.zeros((tk, LANES), F32), jnp.zeros((tk, LANES), F32), jnp.zeros((tk, 1), F32))
            dk_acc, dv_acc, dc_acc = lax.fori_loop(kb, nq, step, init)
            dks.append(dk_acc)
            dvs.append(dv_acc)
            dcs.append(dc_acc)
        dk_ref[...] = jnp.where(lane < GROUP_DIM, dks[0], dks[1])
        dv_ref[...] = jnp.where(lane < GROUP_DIM, dvs[0], dvs[1])
        dcs_ref[...] = jnp.where(lane == 0, -dcs[0], jnp.where(lane == 1, -dcs[1], 0.0))

    return pl.pallas_call(
        body,
        out_shape=(jax.ShapeDtypeStruct((S, ATT_WIDTH), F32), jax.ShapeDtypeStruct((S, ATT_WIDTH), F32),
                   jax.ShapeDtypeStruct((S, ATT_WIDTH), F32), jax.ShapeDtypeStruct((N_PAIRS, S, LANES), F32),
                   jax.ShapeDtypeStruct((N_PAIRS, 8, S), F32)),
        grid=(N_PAIRS, nq),
        in_specs=[pl.BlockSpec((S, LANES), lambda p, j: (0, p)),
                  pl.BlockSpec((S, LANES), lambda p, j: (0, p)),
                  pl.BlockSpec((tk, LANES), lambda p, j: (j, N_PAIRS + p)),
                  pl.BlockSpec((tk, LANES), lambda p, j: (j, 2 * N_PAIRS + p)),
                  pl.BlockSpec((None, tk, LANES), lambda p, j: (p, j, 0)),
                  pl.BlockSpec((None, 8, S), lambda p, j: (p, 0, 0)),
                  pl.BlockSpec((None, 8, S), lambda p, j: (p, 0, 0)),
                  pl.BlockSpec((None, 8, S), lambda p, j: (p, 0, 0))],
        out_specs=(pl.BlockSpec((S, LANES), lambda p, j: (0, p)),
                   pl.BlockSpec((tk, LANES), lambda p, j: (j, p)),
                   pl.BlockSpec((tk, LANES), lambda p, j: (j, p)),
                   pl.BlockSpec((None, tk, LANES), lambda p, j: (p, j, 0)),
                   pl.BlockSpec((None, 8, S), lambda p, j: (p, 0, 0))),
        name="attn_bwd", compiler_params=_cp(("parallel", "arbitrary"), VMEM_LIMIT),
    )(qkv, do_bf, qkv, qkv, c_cols, c_rows, lse_rows, dl_rows)


def _gmlp_bwd(ug, dsg, gain, w_s, wt_s, bias_full):
    S = ug.shape[0]
    tm = _pick(S, (512, 256, 128))
    n_chunks = tm // CHUNK
    n_i = S // tm
    ones = _group_ones()
    nt = (((1,), (1,)), ((), ()))

    def body(ug_ref, dsg_ref, gain_ref, w_ref, wt_ref, bias_ref, ones_ref, dug_ref, dw_ref, dgain_ref, dbias_ref,
             dbacc_ref):
        i = pl.program_id(0)

        @pl.when(i == 0)
        def _():
            dw_ref[...] = jnp.zeros_like(dw_ref)
            dgain_ref[...] = jnp.zeros_like(dgain_ref)
            dbacc_ref[...] = jnp.zeros_like(dbacc_ref)

        ones_m = ones_ref[...]
        pu = ug_ref[:, :GMLP_WIDTH]
        pg = ug_ref[:, GMLP_WIDTH:]
        u = _gelu(pu)
        vr = _gelu(pg)
        ms = _dot3(vr * vr, ones_m) * (1.0 / GROUP_DIM)
        rinv = lax.rsqrt(ms + EPS)
        vhat = vr * rinv
        gain_v = gain_ref[...]
        vn = (vhat * gain_v).astype(BF16)
        mixed = _gmlp_mixed(vn, w_ref, bias_ref[...], n_chunks)
        dsg_v = dsg_ref[...]
        du = dsg_v * mixed
        dmixed = dsg_v * u
        dm_bf = dmixed.astype(BF16)
        lane = lax.broadcasted_iota(jnp.int32, (CHUNK, LANES), 1)
        row = lax.broadcasted_iota(jnp.int32, (CHUNK, CHUNK), 0)
        col = lax.broadcasted_iota(jnp.int32, (CHUNK, CHUNK), 1)
        wts = [jnp.where(col >= row, wt_ref[g], 0.0).astype(BF16) for g in range(N_GROUPS)]
        dvn_rows = []
        dbsum = jnp.zeros((CHUNK, GMLP_WIDTH), F32)
        for ci in range(n_chunks):
            rs = slice(ci * CHUNK, (ci + 1) * CHUNK)
            dbsum = dbsum + dmixed[rs, :]
            cols = []
            for pp in range(N_GROUPS // 2):
                cs = slice(pp * LANES, (pp + 1) * LANES)
                dm = dm_bf[rs, cs]
                dm_lo = jnp.where(lane < GROUP_DIM, dm, jnp.zeros_like(dm))
                dm_hi = jnp.where(lane >= GROUP_DIM, dm, jnp.zeros_like(dm))
                vb = vn[rs, cs]
                dw_ref[2 * pp] += lax.dot_general(dm_lo, vb, nt, preferred_element_type=F32)
                dw_ref[2 * pp + 1] += lax.dot_general(dm_hi, vb, nt, preferred_element_type=F32)
                cols.append(jnp.dot(wts[2 * pp], dm_lo, preferred_element_type=F32)
                            + jnp.dot(wts[2 * pp + 1], dm_hi, preferred_element_type=F32))
            dvn_rows.append(jnp.concatenate(cols, axis=1))
        dvn = jnp.concatenate(dvn_rows, axis=0)
        dbacc_ref[...] += dbsum
        dgain_ref[0:1, :] += jnp.sum(dvn * vhat, axis=0, keepdims=True)
        dvhat = dvn * gain_v
        gm = _dot3(dvhat * vhat, ones_m) * (1.0 / GROUP_DIM)
        dvr = rinv * (dvhat - vhat * gm)
        dug_ref[:, :GMLP_WIDTH] = (du * _gelu_grad(pu)).astype(BF16)
        dug_ref[:, GMLP_WIDTH:] = (dvr * _gelu_grad(pg)).astype(BF16)

        @pl.when(i == n_i - 1)
        def _():
            for g in range(N_GROUPS):
                dw_ref[g] = jnp.where(row >= col, dw_ref[g], 0.0)
            dbias_ref[...] = _dot3(dbacc_ref[...], ones_m)

    return pl.pallas_call(
        body,
        out_shape=(jax.ShapeDtypeStruct((S, 2 * GMLP_WIDTH), BF16), jax.ShapeDtypeStruct((N_GROUPS, CHUNK, CHUNK), F32),
                   jax.ShapeDtypeStruct((8, GMLP_WIDTH), F32), jax.ShapeDtypeStruct((CHUNK, GMLP_WIDTH), F32)),
        grid=(n_i,),
        in_specs=[pl.BlockSpec((tm, 2 * GMLP_WIDTH), lambda i: (i, 0)), pl.BlockSpec((tm, GMLP_WIDTH), lambda i: (i, 1)),
                  pl.BlockSpec((1, GMLP_WIDTH), lambda i: (0, 0)),
                  pl.BlockSpec((N_GROUPS, CHUNK, CHUNK), lambda i: (0, 0, 0)),
                  pl.BlockSpec((N_GROUPS, CHUNK, CHUNK), lambda i: (0, 0, 0)),
                  pl.BlockSpec((CHUNK, GMLP_WIDTH), lambda i: (0, 0)),
                  pl.BlockSpec((GMLP_WIDTH, GMLP_WIDTH), lambda i: (0, 0))],
        out_specs=(pl.BlockSpec((tm, 2 * GMLP_WIDTH), lambda i: (i, 0)),
                   pl.BlockSpec((N_GROUPS, CHUNK, CHUNK), lambda i: (0, 0, 0)),
                   pl.BlockSpec((8, GMLP_WIDTH), lambda i: (0, 0)),
                   pl.BlockSpec((CHUNK, GMLP_WIDTH), lambda i: (0, 0))),
        scratch_shapes=[pltpu.VMEM((CHUNK, GMLP_WIDTH), F32)],
        name="gmlp_bwd", compiler_params=_cp(("arbitrary",), VMEM_LIMIT),
    )(ug, dsg, gain, w_s, wt_s, bias_full, ones)


def _gate_bwd(dcq, dck, zf):
    S = zf.shape[0]
    tm = _pick(S, (256,))
    n_i = S // tm
    triu = (lax.broadcasted_iota(jnp.int32, (tm, tm), 0) <= lax.broadcasted_iota(jnp.int32, (tm, tm), 1)).astype(BF16)

    def body(dcq_ref, dck_ref, zf_ref, tri_ref, dzf_ref, dbf_ref, carry_ref):
        i = pl.program_id(0)

        @pl.when(i == 0)
        def _():
            carry_ref[...] = jnp.zeros_like(carry_ref)
            dbf_ref[...] = jnp.zeros_like(dbf_ref)

        lane = lax.broadcasted_iota(jnp.int32, (tm, LANES), 1)
        dc = jnp.zeros((tm, LANES), F32)
        for p in range(N_PAIRS):
            slab = dcq_ref[p] + dck_ref[p]
            for hh in range(2):
                dc = dc + jnp.where(lane == 2 * p + hh, slab[:, hh:hh + 1], 0.0)
        dlf = _dot3l(tri_ref[...], dc) + carry_ref[0:1, :]
        carry_ref[0:1, :] = dlf[0:1, :]
        dz = jnp.where(lane < N_HEADS, dlf * _sigmoid(-zf_ref[...]), 0.0)
        dzf_ref[...] = dz.astype(BF16)
        dbf_ref[0:1, :] += jnp.sum(dz, axis=0, keepdims=True)

    return pl.pallas_call(
        body,
        out_shape=(jax.ShapeDtypeStruct((S, LANES), BF16), jax.ShapeDtypeStruct((8, LANES), F32)),
        grid=(n_i,),
        in_specs=[pl.BlockSpec((N_PAIRS, tm, LANES), lambda i: (0, n_i - 1 - i, 0)),
                  pl.BlockSpec((N_PAIRS, tm, LANES), lambda i: (0, n_i - 1 - i, 0)),
                  pl.BlockSpec((tm, LANES), lambda i: (n_i - 1 - i, 0)),
                  pl.BlockSpec((tm, tm), lambda i: (0, 0))],
        out_specs=(pl.BlockSpec((tm, LANES), lambda i: (n_i - 1 - i, 0)), pl.BlockSpec((8, LANES), lambda i: (0, 0))),
        scratch_shapes=[pltpu.VMEM((8, LANES), F32)],
        name="gate_bwd", compiler_params=_cp(("arbitrary",), VMEM_LIMIT),
    )(dcq, dck, zf, triu)


def _out_proj_fwd(att_bf, sg, w_out_bf, x, g_ffn):
    S = x.shape[0]
    tm = _pick(S, (512, 256))

    def body(a_ref, s_ref, w_ref, x_ref, g_ref, h_ref, hn_ref):
        h = (x_ref[...] + jnp.dot(a_ref[...], w_ref[:ATT_WIDTH, :], preferred_element_type=F32)
             + jnp.dot(s_ref[...], w_ref[ATT_WIDTH:, :], preferred_element_type=F32))
        h_ref[...] = h
        r = lax.rsqrt(jnp.mean(h * h, axis=-1, keepdims=True) + EPS)
        hn_ref[...] = ((h * r) * g_ref[...]).astype(BF16)

    row = pl.BlockSpec((tm, D_MODEL), lambda i: (i, 0))
    half = pl.BlockSpec((tm, ATT_WIDTH), lambda i: (i, 0))
    return pl.pallas_call(
        body, out_shape=(jax.ShapeDtypeStruct((S, D_MODEL), F32), jax.ShapeDtypeStruct((S, D_MODEL), BF16)),
        grid=(S // tm,),
        in_specs=[half, half, pl.BlockSpec((D_MODEL, D_MODEL), lambda i: (0, 0)), row,
                  pl.BlockSpec((1, D_MODEL), lambda i: (0, 0))],
        out_specs=(row, row), name="out_proj", compiler_params=_cp(("parallel",), VMEM_LIMIT),
    )(att_bf, sg, w_out_bf, x, g_ffn)


_IN_PIECES = ((0, ATT_WIDTH), (ATT_WIDTH, ATT_WIDTH), (2 * ATT_WIDTH, ATT_WIDTH), (QKV, 2 * GMLP_WIDTH), (UG_END, LANES))


def _inproj_bwd_dx(pieces, w_pad, x, g_mix, dh1):
    S = x.shape[0]
    tm = _pick(S, (512, 256))

    def body(*refs):
        p_refs, (w_ref, x_ref, g_ref, r_ref, dx_ref, dg_ref) = refs[:5], refs[5:]
        i = pl.program_id(0)

        @pl.when(i == 0)
        def _():
            dg_ref[...] = jnp.zeros_like(dg_ref)

        dxn = None
        for p_ref, (c0, width) in zip(p_refs, _IN_PIECES):
            part = lax.dot_general(p_ref[...], w_ref[:, c0:c0 + width], _NT, preferred_element_type=F32)
            dxn = part if dxn is None else dxn + part
        xf = x_ref[...]
        r = lax.rsqrt(jnp.mean(xf * xf, axis=-1, keepdims=True) + EPS)
        xhat = xf * r
        dg_ref[0:1, :] += jnp.sum(dxn * xhat, axis=0, keepdims=True)
        dhat = dxn * g_ref[...]
        dx_ref[...] = r_ref[...] + r * (dhat - xhat * jnp.mean(dhat * xhat, axis=-1, keepdims=True))

    row = pl.BlockSpec((tm, D_MODEL), lambda i: (i, 0))
    return pl.pallas_call(
        body, out_shape=(jax.ShapeDtypeStruct((S, D_MODEL), F32), jax.ShapeDtypeStruct((8, D_MODEL), F32)),
        grid=(S // tm,),
        in_specs=[pl.BlockSpec((tm, width), lambda i: (i, 0)) for _, width in _IN_PIECES]
        + [pl.BlockSpec((D_MODEL, IN_PAD), lambda i: (0, 0)), row, pl.BlockSpec((1, D_MODEL), lambda i: (0, 0)), row],
        out_specs=(row, pl.BlockSpec((8, D_MODEL), lambda i: (0, 0))),
        name="in_proj_dx", compiler_params=_cp(("arbitrary",), VMEM_LIMIT),
    )(*pieces, w_pad, x, g_mix, dh1)


def _inproj_bwd_dw(xn, pieces):
    S = xn.shape[0]
    tk = _pick(S, (512, 256))

    def body(*refs):
        x_ref, p_refs, o_ref = refs[0], refs[1:6], refs[6]
        k = pl.program_id(0)

        @pl.when(k == 0)
        def _():
            o_ref[...] = jnp.zeros_like(o_ref)

        xb = x_ref[...]
        for p_ref, (c0, width) in zip(p_refs, _IN_PIECES):
            o_ref[:, c0:c0 + width] += lax.dot_general(xb, p_ref[...], _TN, preferred_element_type=F32)

    return pl.pallas_call(
        body, out_shape=jax.ShapeDtypeStruct((D_MODEL, IN_PAD), F32), grid=(S // tk,),
        in_specs=[pl.BlockSpec((tk, D_MODEL), lambda k: (k, 0))]
        + [pl.BlockSpec((tk, width), lambda k: (k, 0)) for _, width in _IN_PIECES],
        out_specs=pl.BlockSpec((D_MODEL, IN_PAD), lambda k: (0, 0)),
        name="in_proj_dw", compiler_params=_cp(("arbitrary",), VMEM_LIMIT),
    )(xn, *pieces)


def _adamw(w, m, v, parts, name):
    R, C = w.shape
    tr = R
    for cand in (256, 128, 64, 32, 16, 8):
        if R % cand == 0 and R > cand:
            tr = cand
            break
    c1 = 1.0 / (1.0 - ADAM_B1 ** ADAM_STEP)
    c2 = 1.0 / (1.0 - ADAM_B2 ** ADAM_STEP)

    def body(w_ref, m_ref, v_ref, p_ref, g_ref, d_ref, nm_ref, nv_ref):
        g = p_ref[0].astype(F32)
        for j in range(1, N_DEV):
            g = g + p_ref[j].astype(F32)
        g_ref[...] = g
        nm = ADAM_B1 * m_ref[...] + (1.0 - ADAM_B1) * g
        nv = ADAM_B2 * v_ref[...] + (1.0 - ADAM_B2) * (g * g)
        nm_ref[...] = nm
        nv_ref[...] = nv
        d_ref[...] = -ADAM_LR * ((nm * c1) / (jnp.sqrt(nv * c2) + ADAM_EPS) + ADAM_WD * w_ref[...])

    spec = pl.BlockSpec((tr, C), lambda i: (i, 0))
    shp = jax.ShapeDtypeStruct((R, C), F32)
    return pl.pallas_call(
        body, out_shape=(shp, shp, shp, shp), grid=(R // tr,),
        in_specs=[spec, spec, spec, pl.BlockSpec((N_DEV, tr, C), lambda i: (0, i, 0))],
        out_specs=(spec, spec, spec, spec),
        name=name, compiler_params=_cp(("parallel",), VMEM_LIMIT),
    )(w, m, v, parts)


def _place():
    x, y, c = lax.axis_index("x"), lax.axis_index("y"), lax.axis_index("c")
    return x, y, c


def _all_gather(blocks, name):
    n = len(blocks)

    def body(*refs):
        ins, outs = refs[:n], refs[n:2 * n]
        send_sems, recv_sems, local_sems = refs[2 * n:]
        x, y, c = _place()
        me, sibling = (x, y, c), (x, y, 1 - c)
        chips = [(1 - x, y), (x, 1 - y), (1 - x, 1 - y)]
        sends = []
        for a in range(n):
            out = outs[a]

            def slot(px, py, pc, out=out):
                return out.at[4 * px + 2 * py + pc]

            def copy(k, block, to, src=None, a=a, slot=slot):
                return pltpu.make_async_remote_copy(
                    src_ref=slot(*block) if src is None else src, dst_ref=slot(*block),
                    send_sem=send_sems.at[a, k], recv_sem=recv_sems.at[a, k], device_id=to, device_id_type=MESH)

            mine = pltpu.make_async_copy(ins[a], slot(*me), local_sems.at[a])
            mine.start()
            first = [copy(0, me, sibling, src=ins[a])]
            first += [copy(1 + j, me, (*chip, c), src=ins[a]) for j, chip in enumerate(chips)]
            for cp in first:
                cp.start()
            sends.append((mine, first, copy))
        for a in range(n):
            mine, first, copy = sends[a]
            passed = [copy(4 + j, (*chip, c), sibling) for j, chip in enumerate(chips)]
            for j, chip in enumerate(chips):
                copy(1 + j, (*chip, c), me).wait_recv()
                passed[j].start()
            copy(0, sibling, me).wait_recv()
            for j, chip in enumerate(chips):
                copy(4 + j, (*chip, 1 - c), me).wait_recv()
            for cp in first + passed:
                cp.wait_send()
            mine.wait()

    any_spec = pl.BlockSpec(memory_space=pl.ANY)
    return pl.pallas_call(
        body, out_shape=tuple(jax.ShapeDtypeStruct((N_DEV,) + b.shape, b.dtype) for b in blocks),
        in_specs=[any_spec] * n, out_specs=tuple([any_spec] * n),
        scratch_shapes=[pltpu.SemaphoreType.DMA((n, 7)), pltpu.SemaphoreType.DMA((n, 7)), pltpu.SemaphoreType.DMA((n,))],
        name=name,
    )(*blocks)


def _exchange_shards(parts, name):
    n = len(parts)

    def body(*refs):
        ins, outs = refs[:n], refs[n:2 * n]
        send_sems, recv_sems, local_sems = refs[2 * n:]
        x, y, c = _place()
        me = 4 * x + 2 * y + c
        started = []
        for a in range(n):
            mine = pltpu.make_async_copy(ins[a].at[me], outs[a].at[me], local_sems.at[a])
            mine.start()
            started.append(mine)
            for k in range(1, N_DEV):
                px, py, pc = x ^ ((k >> 2) & 1), y ^ ((k >> 1) & 1), c ^ (k & 1)
                cp = pltpu.make_async_remote_copy(
                    src_ref=ins[a].at[4 * px + 2 * py + pc], dst_ref=outs[a].at[me],
                    send_sem=send_sems.at[a, k - 1], recv_sem=recv_sems.at[a, k - 1],
                    device_id=(px, py, pc), device_id_type=MESH)
                cp.start()
                started.append(cp)
        for cp in started:
            cp.wait()

    any_spec = pl.BlockSpec(memory_space=pl.ANY)
    return pl.pallas_call(
        body, out_shape=tuple(jax.ShapeDtypeStruct(p.shape, p.dtype) for p in parts),
        in_specs=[any_spec] * n, out_specs=tuple([any_spec] * n),
        scratch_shapes=[pltpu.SemaphoreType.DMA((n, 7)), pltpu.SemaphoreType.DMA((n, 7)), pltpu.SemaphoreType.DMA((n,))],
        name=name,
    )(*parts)


_HBM = pl.BlockSpec(memory_space=pltpu.HBM)
_SEM = pl.BlockSpec(memory_space=pltpu.SEMAPHORE)
_EFFECT = pltpu.SideEffectType.DATAFLOW_SIDE_EFFECTING


def _peers(x, y, c):
    out = []
    for k in range(1, N_DEV):
        px, py, pc = x ^ ((k >> 2) & 1), y ^ ((k >> 1) & 1), c ^ (k & 1)
        out.append((k, (px, py, pc), 4 * px + 2 * py + pc))
    return out


def _xchg_copies(src_refs, land_refs, send_sems, recv_sems, scatter):
    x, y, c = _place()
    me = 4 * x + 2 * y + c
    copies = []
    for a, (src, land) in enumerate(zip(src_refs, land_refs)):
        for k, place, idx in _peers(x, y, c):
            j = a * (N_DEV - 1) + k - 1
            copies.append(pltpu.make_async_remote_copy(
                src_ref=src.at[idx] if scatter[a] else src, dst_ref=land.at[me],
                send_sem=send_sems[j], recv_sem=recv_sems[j], device_id=place, device_id_type=MESH))
    return copies


def _xchg_start(srcs, scatter, name):
    n = len(srcs)
    lands = [lax.empty((N_DEV,) + (s.shape[1:] if sc else s.shape), s.dtype) for s, sc in zip(srcs, scatter)]

    ns = n * (N_DEV - 1)

    def body(*refs):
        sems = refs[2 * n:2 * n + 2 * ns]
        for cp in _xchg_copies(refs[:n], refs[n:2 * n], sems[:ns], sems[ns:], scatter):
            cp.start()
        token = refs[-1]
        token[...] = jnp.zeros_like(token)

    both = list(srcs) + lands
    res = pl.pallas_call(
        body, name=name,
        out_shape=(*[pltpu.SemaphoreType.DMA(())] * (2 * ns),
                   *[pltpu.HBM(a.shape, a.dtype) for a in both], jax.ShapeDtypeStruct((8, LANES), F32)),
        in_specs=[_HBM] * (2 * n),
        out_specs=(*([_SEM] * (2 * ns)), *([_HBM] * (2 * n)), pl.BlockSpec(memory_space=pltpu.VMEM)),
        input_output_aliases={i: 2 * ns + i for i in range(2 * n)},
        compiler_params=pltpu.CompilerParams(has_side_effects=_EFFECT),
    )(*[pltpu.with_memory_space_constraint(a, pltpu.HBM) for a in both])
    return (tuple(res[:2 * ns]), tuple(res[2 * ns:2 * ns + 2 * n])), res[-1]


def _xchg_wait(handle, scatter, after, name):
    sems, thru = handle
    n = len(thru) // 2
    ns = len(sems) // 2

    def body(*refs):
        got = refs[2 * n:2 * n + 2 * ns]
        for cp in _xchg_copies(refs[:n], refs[n:2 * n], got[:ns], got[ns:], scatter):
            cp.wait_send()
            cp.wait_recv()

    outs = pl.pallas_call(
        body, name=name, out_shape=tuple(pltpu.HBM(a.shape, a.dtype) for a in thru),
        in_specs=[_HBM] * (2 * n) + [_SEM] * (2 * ns) + [pl.BlockSpec(memory_space=pl.ANY)],
        out_specs=tuple([_HBM] * (2 * n)), input_output_aliases={i: i for i in range(2 * n)},
        compiler_params=pltpu.CompilerParams(has_side_effects=_EFFECT),
    )(*thru, *sems, after)
    return outs[:n], outs[n:]


def _tie(a, token):
    return a if token is None else a + token[0, 0]


def _rows128(a):
    flat = a.reshape(-1)
    rows = -(-flat.shape[0] // LANES)
    rows = -(-rows // 8) * 8
    return jnp.pad(flat, (0, rows * LANES - flat.shape[0])).reshape(rows, LANES)


def _local_step(x, target, norm_mix_g, w_in_bf, b_forget, gmlp_norm_g, w_spatial, b_spatial, norm_ffn_g, conv_b,
                norm_final_g, rest_fn, send_fn, token=None):
    f = D_FF
    g_mix = norm_mix_g.reshape(1, D_MODEL)
    w_pad = jnp.pad(w_in_bf, ((0, 0), (0, IN_PAD - IN_COLS)))
    bf_pad = jnp.pad(b_forget.reshape(1, N_HEADS), ((0, 0), (0, LANES - N_HEADS)))
    xn, qa, ka, va, ug, zf = _inproj_fwd(x, _tie(g_mix, token), w_pad, bf_pad)
    bias_full = jnp.repeat(b_spatial.reshape(N_GROUPS, CHUNK).T, GROUP_DIM, axis=1)
    w_s = w_spatial.reshape(N_GROUPS, CHUNK, CHUNK)
    gain = gmlp_norm_g.reshape(1, GMLP_WIDTH)
    sg = _gmlp_fwd(ug, gain, w_s, bias_full)
    att, lse, att_bf = _attn_fwd(qa, ka, va)
    w_out_bf, w_up_bf, conv_w, w_down_bf = rest_fn(att_bf)
    g_ffn = norm_ffn_g.reshape(1, D_MODEL)
    h1, hn = _out_proj_fwd(att_bf, sg, w_out_bf, x, g_ffn)
    cw = jnp.pad(conv_w.reshape(3, 2, f).transpose(1, 0, 2), ((0, 0), (0, 5), (0, 0)))
    cb = conv_b.reshape(2, 1, f)
    hu, act = _ffn_up_conv(hn, w_up_bf, cw, cb)
    h2 = _mm(act, w_down_bf, mode="nn", out_dtype=F32, tm=512, tn=1024, tk=2816, res=h1, name="ffn_down")
    loss_blk, dh2, dh2_bf, dg_final = _loss_head(h2, norm_final_g.reshape(1, D_MODEL), target)
    dw_down = _mm(act, dh2_bf, mode="tn", out_dtype=F32, tm=1408, tn=1024, tk=2048, name="ffn_down_dw")
    dhu, dcw = _conv_act_bwd(hu, dh2_bf, w_down_bf, cw, _tie(cb, send_fn("w_down", dw_down)))
    dhn = _mm(dhu, w_up_bf, mode="nt", out_dtype=F32, tm=1024, tn=1024, tk=2816, a_halves=True, name="ffn_up_dx")
    dw_up = _mm(hn, dhu, mode="tn", out_dtype=F32, tm=1024, tn=1408, tk=2048, b_halves=True, outer="j", name="ffn_up_dw")
    dh1, dh1_bf, dg_ffn = _rms_bwd(h1, _tie(g_ffn, send_fn("w_up", dw_up)), dhn, dh2)
    dmix = _mm(dh1_bf, w_out_bf, mode="nt", out_dtype=F32, tm=512, tn=1024, tk=1024, name="out_proj_dx")
    dw_out = jnp.concatenate(
        [_mm(att_bf, dh1_bf, mode="tn", out_dtype=F32, tm=512, tn=1024, tk=1024, name="out_proj_dw_att"),
         _mm(sg, dh1_bf, mode="tn", out_dtype=F32, tm=512, tn=1024, tk=1024, name="out_proj_dw_sg")], axis=0)
    qb, doa = _attn_prep(att, lse, dmix, qa)
    dq, dk, dv, dcq, dck = _attn_bwd(qb, ka, va, doa)
    wt_s = w_s.transpose(0, 2, 1)
    dug, dw_s, dgain, dbias = _gmlp_bwd(ug, dmix, _tie(gain, send_fn("w_out", dw_out)), w_s, wt_s, bias_full)
    dzf, dbf = _gate_bwd(dcq, dck, zf)
    pieces = (dq, dk, dv, dug, dzf)
    dw_in = _inproj_bwd_dw(xn, pieces)
    grad_x, dg_mix = _inproj_bwd_dx(pieces, w_pad, x, _tie(g_mix, send_fn("w_in", dw_in[:, :IN_COLS])), dh1)
    grads = dict(
        norm_mix_g=dg_mix[0:1, :],
        b_forget=dbf[0:1, :N_HEADS],
        gmlp_norm_g=dgain[0:1, :],
        w_spatial=dw_s,
        b_spatial=dbias[:, ::GROUP_DIM].T,
        norm_ffn_g=dg_ffn[0:1, :],
        conv_w=dcw[:, 0:3, :].transpose(1, 0, 2).reshape(3, 2 * f),
        conv_b=dcw[:, 3, :].reshape(1, 2 * f),
        norm_final_g=dg_final[0, :],
    )
    return loss_blk[0, 0], grad_x, grads


SMALL = ("norm_mix_g", "b_forget", "gmlp_norm_g", "w_spatial", "b_spatial", "norm_ffn_g", "conv_b", "norm_final_g")


def kernel(x, norm_mix_g, w_in, b_forget, gmlp_norm_g, w_spatial, b_spatial, w_out, norm_ffn_g, w_up, conv_w, conv_b, w_down, norm_final_g, loss_target, m_norm_mix_g, m_w_in, m_b_forget, m_gmlp_norm_g, m_w_spatial, m_b_spatial, m_w_out, m_norm_ffn_g, m_w_up, m_conv_w, m_conv_b, m_w_down, m_norm_final_g, v_norm_mix_g, v_w_in, v_b_forget, v_gmlp_norm_g, v_w_spatial, v_b_spatial, v_w_out, v_norm_ffn_g, v_w_up, v_conv_w, v_conv_b, v_w_down, v_norm_final_g):
    weights = dict(norm_mix_g=norm_mix_g, w_in=w_in, b_forget=b_forget, gmlp_norm_g=gmlp_norm_g, w_spatial=w_spatial,
                   b_spatial=b_spatial, w_out=w_out, norm_ffn_g=norm_ffn_g, w_up=w_up, conv_w=conv_w, conv_b=conv_b,
                   w_down=w_down, norm_final_g=norm_final_g)
    m_in = dict(norm_mix_g=m_norm_mix_g, w_in=m_w_in, b_forget=m_b_forget, gmlp_norm_g=m_gmlp_norm_g,
                w_spatial=m_w_spatial, b_spatial=m_b_spatial, w_out=m_w_out, norm_ffn_g=m_norm_ffn_g, w_up=m_w_up,
                conv_w=m_conv_w, conv_b=m_conv_b, w_down=m_w_down, norm_final_g=m_norm_final_g)
    v_in = dict(norm_mix_g=v_norm_mix_g, w_in=v_w_in, b_forget=v_b_forget, gmlp_norm_g=v_gmlp_norm_g,
                w_spatial=v_w_spatial, b_spatial=v_b_spatial, w_out=v_w_out, norm_ffn_g=v_norm_ffn_g, w_up=v_w_up,
                conv_w=v_conv_w, conv_b=v_conv_b, w_down=v_w_down, norm_final_g=v_norm_final_g)
    order = list(weights)
    me = 4 * lax.axis_index("x") + 2 * lax.axis_index("y") + lax.axis_index("c")
    n_in, n_up = w_in.shape[2], w_up.shape[2]
    r_out, r_down = w_out.shape[1], w_down.shape[1]

    def with_mine(landed, mine):
        return lax.dynamic_update_index_in_dim(landed, mine, me, 0)

    up_blk = w_up[0].astype(BF16)
    rows_blk = jnp.concatenate([w_out[0].astype(BF16), w_down[0].astype(BF16)], axis=0)
    taps_blk = jnp.pad(conv_w[0], ((0, 5), (0, 0)))
    (in_all,) = _all_gather([w_in[0].astype(BF16)], "gather_w_in")
    in_all, rest_blocks = lax.optimization_barrier((in_all, [up_blk, rows_blk, taps_blk]))
    rest_handle, token = _xchg_start(rest_blocks, [False] * 3, "gather_rest_start")
    w_in_bf = in_all.transpose(1, 0, 2).reshape(D_MODEL, N_DEV * n_in)

    def rest_fn(after):
        mine, landed = _xchg_wait(rest_handle, [False] * 3, after, "gather_rest_wait")
        up_all, rows_all, taps_all = [with_mine(l, b) for l, b in zip(landed, mine)]
        return (rows_all[:, :r_out, :].reshape(N_DEV * r_out, D_MODEL),
                up_all.transpose(1, 0, 2).reshape(D_MODEL, N_DEV * n_up),
                taps_all[:, :3, :].transpose(1, 0, 2).reshape(3, N_DEV * n_up),
                rows_all[:, r_out:, :].reshape(N_DEV * r_down, D_MODEL))

    sent = {}

    def send_fn(name, grad):
        if name == "w_in":
            parts = grad.reshape(D_MODEL, N_DEV, -1).transpose(1, 0, 2).astype(BF16)
        elif name == "w_up":
            parts = grad.reshape(D_MODEL, N_DEV, -1).transpose(1, 0, 2)
        else:
            parts = grad.reshape(N_DEV, -1, D_MODEL)
        sent[name], tok = _xchg_start([parts], [True], "scatter_" + name + "_start")
        return tok

    loss_local, grad_x, g = _local_step(
        x[0], loss_target[0], norm_mix_g, w_in_bf, b_forget, gmlp_norm_g, w_spatial, b_spatial, norm_ffn_g, conv_b,
        norm_final_g, rest_fn, send_fn, token)
    loss = lax.psum(loss_local, ("x", "y", "c"))

    got = {}
    for name, handle in sent.items():
        (parts,), (landed,) = _xchg_wait(handle, [True], grad_x, "scatter_" + name + "_wait")
        got[name] = with_mine(landed, lax.dynamic_index_in_dim(parts, me, 0, keepdims=False))
    got_in, got_up, got_out, got_down = got["w_in"], got["w_up"], got["w_out"], got["w_down"]

    small_names = SMALL + ("conv_w",)
    packed = [_rows128(g[k]) for k in small_names]
    sizes = [p.shape[0] for p in packed]
    (small_all,) = _all_gather([jnp.concatenate(packed, axis=0)], "gather_small_grads")

    def pack(src):
        return jnp.concatenate([_rows128(src[k]) for k in SMALL], axis=0)

    n_small_rows = sum(sizes[:-1])
    sg_, sd_, sm_, sv_ = _adamw(pack(weights), pack(m_in), pack(v_in), small_all[:, :n_small_rows, :], "adamw_small")

    outs = {}
    off = 0
    for k, rows in zip(SMALL, sizes[:-1]):
        shp = weights[k].shape
        cnt = math.prod(shp)
        outs[k] = tuple(a[off:off + rows].reshape(-1)[:cnt].reshape(shp) for a in (sg_, sd_, sm_, sv_))
        off += rows
    taps_parts = small_all[:, n_small_rows:, :].reshape(N_DEV, -1)[:, :3 * N_DEV * n_up].reshape(N_DEV, 3, N_DEV * n_up)
    taps_mine = lax.dynamic_slice_in_dim(taps_parts, me * n_up, n_up, axis=2)
    taps_mine = jnp.pad(taps_mine, ((0, 0), (0, 5), (0, 0)))

    def pad8(a):
        return jnp.pad(a[0], ((0, 5), (0, 0)))

    res = _adamw(pad8(conv_w), pad8(m_conv_w), pad8(v_conv_w), taps_mine, "adamw_conv_w")
    outs["conv_w"] = tuple(a[:3][None] for a in res)
    for k, got in (("w_in", got_in), ("w_up", got_up), ("w_out", got_out), ("w_down", got_down)):
        res = _adamw(weights[k][0], m_in[k][0], v_in[k][0], got, "adamw_" + k)
        outs[k] = tuple(a[None] for a in res)

    return (loss, grad_x[None], *[outs[k][0] for k in order], *[outs[k][1] for k in order],
            *[outs[k][2] for k in order], *[outs[k][3] for k in order])
```

```python
import functools
import math

import jax
import jax.numpy as jnp
from jax import lax
from jax.experimental import pallas as pl
from jax.experimental.pallas import tpu as pltpu

F32 = jnp.float32
BF16 = jnp.bfloat16

N_DEV = 8
D_MODEL = 1024
ATT_WIDTH = 512
GMLP_WIDTH = 512
HEAD_DIM = 64
N_HEADS = 8
N_PAIRS = 4
N_GROUPS = 8
GROUP_DIM = 64
CHUNK = 128
D_FF = 2816
IN_COLS = 2568
IN_PAD = 2688
QKV = 1536
UG_END = 2560
EPS = 1e-6
LANES = 128

ADAM_LR = 0.001
ADAM_B1 = 0.9
ADAM_B2 = 0.999
ADAM_EPS = 1e-08
ADAM_WD = 0.01
ADAM_STEP = 10

ATT_TQ = 1024
ATT_TK = 1024
VMEM_LIMIT = 56 * 1024 * 1024
MESH = pl.DeviceIdType.MESH


def _cp(sem, vmem=None):
    return pltpu.CompilerParams(dimension_semantics=sem, vmem_limit_bytes=vmem)


def _pick(n, prefs):
    for p in prefs:
        if n % p == 0:
            return p
    return n


def _split3(x):
    hi = x.astype(BF16)
    r1 = x - hi.astype(F32)
    mid = r1.astype(BF16)
    lo = (r1 - mid.astype(F32)).astype(BF16)
    return hi, mid, lo


def _dot3(x, ones_bf):
    hi, mid, lo = _split3(x)
    d = functools.partial(jnp.dot, preferred_element_type=F32)
    return d(hi, ones_bf) + d(mid, ones_bf) + d(lo, ones_bf)


def _dot3l(ones_bf, x):
    hi, mid, lo = _split3(x)
    d = functools.partial(jnp.dot, preferred_element_type=F32)
    return d(ones_bf, hi) + d(ones_bf, mid) + d(ones_bf, lo)


def _gelu(x):
    k = math.sqrt(2.0 / math.pi)
    t = jnp.tanh(k * (x + 0.044715 * (x * x * x)))
    return 0.5 * x * (1.0 + t)


def _gelu_grad(x):
    k = math.sqrt(2.0 / math.pi)
    x2 = x * x
    t = jnp.tanh(k * (x + 0.044715 * (x2 * x)))
    return 0.5 * (1.0 + t) + 0.5 * x * (1.0 - t * t) * (k * (1.0 + 3.0 * 0.044715 * x2))


def _sigmoid(x):
    return 1.0 / (1.0 + jnp.exp(-x))


def _mm(a, b, *, mode, out_dtype, tm, tn, tk, name, res=None, a_halves=False, b_halves=False,
        out_halves=False, outer="i"):
    if mode == "tn":
        K, M = a.shape[-2], a.shape[-1] * (2 if a_halves else 1)
    else:
        M, K = a.shape[-2], a.shape[-1] * (2 if a_halves else 1)
    if mode == "nt":
        N = b.shape[-2]
        assert b.shape[-1] == K
    else:
        N = b.shape[-1] * (2 if b_halves else 1)
    tm, tn, tk = min(tm, M), min(tn, N), min(tk, K)
    assert M % tm == 0 and N % tn == 0 and K % tk == 0, (name, M, N, K, tm, tn, tk)
    nm, nn, nk = M // tm, N // tn, K // tk

    def ij(g0, g1):
        return (g0, g1) if outer == "i" else (g1, g0)

    if mode == "nn":
        dims = (((1,), (0,)), ((), ()))
        if a_halves:
            nkh = nk // 2
            a_spec = pl.BlockSpec((None, tm, tk), lambda g0, g1, k: (k // nkh, ij(g0, g1)[0], k % nkh))
        else:
            a_spec = pl.BlockSpec((tm, tk), lambda g0, g1, k: (ij(g0, g1)[0], k))
        b_spec = pl.BlockSpec((tk, tn), lambda g0, g1, k: (k, ij(g0, g1)[1]))
    elif mode == "nt":
        dims = (((1,), (1,)), ((), ()))
        if a_halves:
            nkh = nk // 2
            a_spec = pl.BlockSpec((None, tm, tk), lambda g0, g1, k: (k // nkh, ij(g0, g1)[0], k % nkh))
        else:
            a_spec = pl.BlockSpec((tm, tk), lambda g0, g1, k: (ij(g0, g1)[0], k))
        b_spec = pl.BlockSpec((tn, tk), lambda g0, g1, k: (ij(g0, g1)[1], k))
    else:
        dims = (((0,), (0,)), ((), ()))
        if a_halves:
            nmh = nm // 2
            a_spec = pl.BlockSpec((None, tk, tm), lambda g0, g1, k: (ij(g0, g1)[0] // nmh, k, ij(g0, g1)[0] % nmh))
        else:
            a_spec = pl.BlockSpec((tk, tm), lambda g0, g1, k: (k, ij(g0, g1)[0]))
        if b_halves:
            nnh = nn // 2
            b_spec = pl.BlockSpec((None, tk, tn), lambda g0, g1, k: (ij(g0, g1)[1] // nnh, k, ij(g0, g1)[1] % nnh))
        else:
            b_spec = pl.BlockSpec((tk, tn), lambda g0, g1, k: (k, ij(g0, g1)[1]))
    if out_halves:
        nnh = nn // 2
        o_spec = pl.BlockSpec((None, tm, tn), lambda g0, g1, k: (ij(g0, g1)[1] // nnh, ij(g0, g1)[0], ij(g0, g1)[1] % nnh))
        o_shape = jax.ShapeDtypeStruct((2, M, N // 2), out_dtype)
    else:
        o_spec = pl.BlockSpec((tm, tn), lambda g0, g1, k: ij(g0, g1))
        o_shape = jax.ShapeDtypeStruct((M, N), out_dtype)
    in_specs = [a_spec, b_spec]
    args = [a, b]
    if res is not None:
        in_specs.append(pl.BlockSpec((tm, tn), lambda g0, g1, k: ij(g0, g1)))
        args.append(res)

    def body(*refs):
        if res is not None:
            a_ref, b_ref, r_ref, o_ref = refs[:4]
        else:
            a_ref, b_ref, o_ref = refs[:3]
            r_ref = None
        part = lax.dot_general(a_ref[...], b_ref[...], dims, preferred_element_type=F32)
        if nk == 1:
            if r_ref is not None:
                part = part + r_ref[...]
            o_ref[...] = part.astype(out_dtype)
            return
        acc_ref = refs[-1]
        k = pl.program_id(2)

        @pl.when(k == 0)
        def _():
            acc_ref[...] = part

        @pl.when(k > 0)
        def _():
            acc_ref[...] += part

        @pl.when(k == nk - 1)
        def _():
            tot = acc_ref[...]
            if r_ref is not None:
                tot = tot + r_ref[...]
            o_ref[...] = tot.astype(out_dtype)

    grid = (nm, nn, nk) if outer == "i" else (nn, nm, nk)
    scratch = [] if nk == 1 else [pltpu.VMEM((tm, tn), F32)]
    return pl.pallas_call(
        body, out_shape=o_shape, grid=grid, in_specs=in_specs, out_specs=o_spec, scratch_shapes=scratch,
        name=name, compiler_params=_cp(("parallel", "parallel", "arbitrary"), VMEM_LIMIT),
    )(*args)


def _aug(lane, terms):
    out = 0.0
    for j, t in enumerate(terms):
        out = jnp.where(lane == HEAD_DIM + j, t, out)
    return out


def _split3f(x):
    hi, mid, lo = _split3(x)
    return [hi.astype(F32), mid.astype(F32), lo.astype(F32)]


def _inproj_fwd(x, g_mix, w_pad, bf_pad):
    S = x.shape[0]
    tm = _pick(S, (512, 256))
    tri = (lax.broadcasted_iota(jnp.int32, (tm, tm), 0) >= lax.broadcasted_iota(jnp.int32, (tm, tm), 1)).astype(BF16)

    def body(x_ref, g_ref, w_ref, bf_ref, tri_ref, xn_ref, qa_ref, ka_ref, va_ref, ug_ref, zf_ref, carry_ref):
        i = pl.program_id(0)

        @pl.when(i == 0)
        def _():
            carry_ref[...] = jnp.zeros_like(carry_ref)

        xf = x_ref[...]
        r = lax.rsqrt(jnp.mean(xf * xf, axis=-1, keepdims=True) + EPS)
        xn = ((xf * r) * g_ref[...]).astype(BF16)
        xn_ref[...] = xn
        proj = jnp.dot(xn, w_ref[...], preferred_element_type=F32)
        ug_ref[...] = proj[:, QKV:UG_END]
        zf = proj[:, UG_END:] + bf_ref[...]
        zf_ref[...] = zf
        lf = jnp.minimum(zf, 0.0) - jnp.log(1.0 + jnp.exp(-jnp.abs(zf)))
        c = _dot3l(tri_ref[...], lf) + carry_ref[0:1, :]
        carry_ref[0:1, :] = c[tm - 1:tm, :]
        c3 = _split3f(c)
        lane = lax.broadcasted_iota(jnp.int32, (tm, LANES), 1)
        ones3 = [1.0, 1.0, 1.0]
        for h in range(N_HEADS):
            p, odd = h // 2, h % 2
            ch = [t[:, h:h + 1] for t in c3]

            def head(base, scale=None, p=p, odd=odd):
                blk = proj[:, base + p * LANES:base + (p + 1) * LANES]
                if scale is not None:
                    blk = blk * scale
                return pltpu.roll(blk, HEAD_DIM, 1) if odd else blk

            cols = slice(h * LANES, (h + 1) * LANES)
            qa_ref[:, cols] = jnp.where(lane < HEAD_DIM, head(0, HEAD_DIM ** -0.5), _aug(lane, ch + ones3)).astype(BF16)
            ka_ref[:, cols] = jnp.where(lane < HEAD_DIM, head(ATT_WIDTH),
                                        _aug(lane, ones3 + [-t for t in ch] + ones3)).astype(BF16)
            va_ref[:, cols] = jnp.where(lane < HEAD_DIM, head(2 * ATT_WIDTH), _aug(lane, ones3)).astype(BF16)

    wide = N_HEADS * LANES
    return pl.pallas_call(
        body,
        out_shape=(jax.ShapeDtypeStruct((S, D_MODEL), BF16), jax.ShapeDtypeStruct((S, wide), BF16),
                   jax.ShapeDtypeStruct((S, wide), BF16), jax.ShapeDtypeStruct((S, wide), BF16),
                   jax.ShapeDtypeStruct((S, 2 * GMLP_WIDTH), F32), jax.ShapeDtypeStruct((S, LANES), F32)),
        grid=(S // tm,),
        in_specs=[pl.BlockSpec((tm, D_MODEL), lambda i: (i, 0)), pl.BlockSpec((1, D_MODEL), lambda i: (0, 0)),
                  pl.BlockSpec((D_MODEL, IN_PAD), lambda i: (0, 0)), pl.BlockSpec((1, LANES), lambda i: (0, 0)),
                  pl.BlockSpec((tm, tm), lambda i: (0, 0))],
        out_specs=(pl.BlockSpec((tm, D_MODEL), lambda i: (i, 0)), pl.BlockSpec((tm, wide), lambda i: (i, 0)),
                   pl.BlockSpec((tm, wide), lambda i: (i, 0)), pl.BlockSpec((tm, wide), lambda i: (i, 0)),
                   pl.BlockSpec((tm, 2 * GMLP_WIDTH), lambda i: (i, 0)), pl.BlockSpec((tm, LANES), lambda i: (i, 0))),
        scratch_shapes=[pltpu.VMEM((8, LANES), F32)],
        name="inproj_fwd", compiler_params=_cp(("arbitrary",), VMEM_LIMIT),
    )(x, g_mix, w_pad, bf_pad, tri)


def _group_ones():
    r = lax.broadcasted_iota(jnp.int32, (GMLP_WIDTH, GMLP_WIDTH), 0) // GROUP_DIM
    c = lax.broadcasted_iota(jnp.int32, (GMLP_WIDTH, GMLP_WIDTH), 1) // GROUP_DIM
    return (r == c).astype(BF16)


def _gmlp_mixed(vn_bf, w_ref, bias, n_chunks):
    lane = lax.broadcasted_iota(jnp.int32, (CHUNK, LANES), 1)
    row = lax.broadcasted_iota(jnp.int32, (CHUNK, CHUNK), 0)
    col = lax.broadcasted_iota(jnp.int32, (CHUNK, CHUNK), 1)
    ws = [jnp.where(row >= col, w_ref[g], 0.0).astype(BF16) for g in range(N_GROUPS)]
    rows = []
    for ci in range(n_chunks):
        cols = []
        for pp in range(N_GROUPS // 2):
            v = vn_bf[ci * CHUNK:(ci + 1) * CHUNK, pp * LANES:(pp + 1) * LANES]
            v_lo = jnp.where(lane < GROUP_DIM, v, jnp.zeros_like(v))
            v_hi = jnp.where(lane >= GROUP_DIM, v, jnp.zeros_like(v))
            m = (jnp.dot(ws[2 * pp], v_lo, preferred_element_type=F32)
                 + jnp.dot(ws[2 * pp + 1], v_hi, preferred_element_type=F32))
            cols.append(m + bias[:, pp * LANES:(pp + 1) * LANES])
        rows.append(jnp.concatenate(cols, axis=1))
    return jnp.concatenate(rows, axis=0)


def _gmlp_fwd(ug, gain, w_s, bias_full):
    S = ug.shape[0]
    tm = _pick(S, (512, 256, 128))
    ones = _group_ones()

    def body(ug_ref, gain_ref, w_ref, bias_ref, ones_ref, sg_ref):
        u = _gelu(ug_ref[:, :GMLP_WIDTH])
        vr = _gelu(ug_ref[:, GMLP_WIDTH:])
        ms = _dot3(vr * vr, ones_ref[...]) * (1.0 / GROUP_DIM)
        vn = ((vr * lax.rsqrt(ms + EPS)) * gain_ref[...]).astype(BF16)
        mixed = _gmlp_mixed(vn, w_ref, bias_ref[...], tm // CHUNK)
        sg_ref[...] = (u * mixed).astype(BF16)

    return pl.pallas_call(
        body, out_shape=jax.ShapeDtypeStruct((S, GMLP_WIDTH), BF16), grid=(S // tm,),
        in_specs=[pl.BlockSpec((tm, 2 * GMLP_WIDTH), lambda i: (i, 0)), pl.BlockSpec((1, GMLP_WIDTH), lambda i: (0, 0)),
                  pl.BlockSpec((N_GROUPS, CHUNK, CHUNK), lambda i: (0, 0, 0)),
                  pl.BlockSpec((CHUNK, GMLP_WIDTH), lambda i: (0, 0)),
                  pl.BlockSpec((GMLP_WIDTH, GMLP_WIDTH), lambda i: (0, 0))],
        out_specs=pl.BlockSpec((tm, GMLP_WIDTH), lambda i: (i, 0)),
        name="gmlp_fwd", compiler_params=_cp(("parallel",), VMEM_LIMIT),
    )(ug, gain, w_s, bias_full, ones)


_NT = (((1,), (1,)), ((), ()))
_TN = (((0,), (0,)), ((), ()))


def _attn_fwd(qa, ka, va):
    S = qa.shape[0]
    tq = _pick(S, (ATT_TQ, 256))
    tk = min(ATT_TK, tq)
    nq = S // tq
    per_q = tq // tk

    def body(q_ref, k_ref, v_ref, o_ref, lse_ref, ob_ref):
        qi = pl.program_id(1)
        lane = lax.broadcasted_iota(jnp.int32, (tq, LANES), 1)
        rid = lax.broadcasted_iota(jnp.int32, (tq, tk), 0)
        cid = lax.broadcasted_iota(jnp.int32, (tq, tk), 1)
        qs = [q_ref[:, :LANES], q_ref[:, LANES:]]

        def update(kb, h, m, acc, diag):
            ks = pl.multiple_of(kb * tk, tk)
            cols = slice(h * LANES, (h + 1) * LANES)
            s = lax.dot_general(qs[h], k_ref[pl.ds(ks, tk), cols], _NT, preferred_element_type=F32)
            if diag is not None:
                s = jnp.where(rid >= cid + diag * tk, s, -jnp.inf)
            m_new = jnp.maximum(m, jnp.max(s, axis=-1, keepdims=True))
            p = jnp.exp(s - m_new).astype(BF16)
            acc = jnp.exp(m - m_new) * acc + jnp.dot(p, v_ref[pl.ds(ks, tk), cols], preferred_element_type=F32)
            return m_new, acc

        def step(kb, carry):
            return tuple(update(kb, h, *carry[h], None) for h in range(2))

        one = (jnp.full((tq, 1), -jnp.inf, F32), jnp.zeros((tq, LANES), F32))
        carry = lax.fori_loop(0, qi * per_q, step, (one, one))
        outs, lses = [], []
        for h in range(2):
            m, acc = carry[h]
            for d in range(per_q):
                m, acc = update(qi * per_q + d, h, m, acc, d)
            l = acc[:, HEAD_DIM:HEAD_DIM + 1]
            outs.append(acc / l)
            lses.append(m + jnp.log(l))
        o = jnp.where(lane < HEAD_DIM, outs[0], pltpu.roll(outs[1], HEAD_DIM, 1))
        o_ref[...] = o
        ob_ref[...] = o.astype(BF16)
        lse_ref[...] = jnp.where(lane < HEAD_DIM, lses[0], lses[1])

    return pl.pallas_call(
        body,
        out_shape=(jax.ShapeDtypeStruct((S, ATT_WIDTH), F32), jax.ShapeDtypeStruct((S, ATT_WIDTH), F32),
                   jax.ShapeDtypeStruct((S, ATT_WIDTH), BF16)),
        grid=(N_PAIRS, nq),
        in_specs=[pl.BlockSpec((tq, 2 * LANES), lambda p, i: (i, p)),
                  pl.BlockSpec((S, 2 * LANES), lambda p, i: (0, p)),
                  pl.BlockSpec((S, 2 * LANES), lambda p, i: (0, p))],
        out_specs=(pl.BlockSpec((tq, LANES), lambda p, i: (i, p)), pl.BlockSpec((tq, LANES), lambda p, i: (i, p)),
                   pl.BlockSpec((tq, LANES), lambda p, i: (i, p))),
        name="attn_fwd", compiler_params=_cp(("parallel", "parallel"), VMEM_LIMIT),
    )(qa, ka, va)


def _rms_fwd(h, g):
    S = h.shape[0]
    tm = _pick(S, (512, 256))

    def body(h_ref, g_ref, o_ref):
        hf = h_ref[...]
        r = lax.rsqrt(jnp.mean(hf * hf, axis=-1, keepdims=True) + EPS)
        o_ref[...] = ((hf * r) * g_ref[...]).astype(BF16)

    return pl.pallas_call(
        body, out_shape=jax.ShapeDtypeStruct(h.shape, BF16), grid=(S // tm,),
        in_specs=[pl.BlockSpec((tm, D_MODEL), lambda i: (i, 0)), pl.BlockSpec((1, D_MODEL), lambda i: (0, 0))],
        out_specs=pl.BlockSpec((tm, D_MODEL), lambda i: (i, 0)),
        name="rms_fwd", compiler_params=_cp(("parallel",)),
    )(h, g)


def _shift_rows(x, prev, n):
    rid = lax.broadcasted_iota(jnp.int32, x.shape, 0)
    y = pltpu.roll(x, n, 0)
    if n == 1:
        return jnp.where(rid == 0, prev[7:8, :], y)
    return jnp.where(rid == 0, prev[6:7, :], jnp.where(rid == 1, prev[7:8, :], y))


def _shift_rows_up(x, nxt, n):
    rows = x.shape[0]
    rid = lax.broadcasted_iota(jnp.int32, x.shape, 0)
    y = pltpu.roll(x, rows - n, 0)
    if n == 1:
        return jnp.where(rid == rows - 1, nxt[0:1, :], y)
    return jnp.where(rid == rows - 2, nxt[0:1, :], jnp.where(rid == rows - 1, nxt[1:2, :], y))


def _conv3(cur, prev, w, b):
    return (w[0:1, :] * _shift_rows(cur, prev, 2) + w[1:2, :] * _shift_rows(cur, prev, 1)
            + w[2:3, :] * cur + b)


def _conv_act_fwd(hu, cw, cb):
    _, S, F = hu.shape
    tm = _pick(S, (512, 256))
    tn = _pick(F, (256, 128))
    r8 = tm // 8

    def body(cur_ref, prev_ref, w_ref, b_ref, o_ref):
        i = pl.program_id(1)
        halves = []
        for h in range(2):
            prev = jnp.where(i > 0, prev_ref[h], 0.0)
            halves.append(_conv3(cur_ref[h], prev, w_ref[h], b_ref[h]))
        a, g = halves
        o_ref[...] = (g * _sigmoid(g) * a).astype(BF16)

    return pl.pallas_call(
        body, out_shape=jax.ShapeDtypeStruct((S, F), BF16), grid=(F // tn, S // tm),
        in_specs=[pl.BlockSpec((2, tm, tn), lambda j, i: (0, i, j)),
                  pl.BlockSpec((2, 8, tn), lambda j, i: (0, jnp.maximum(i * r8 - 1, 0), j)),
                  pl.BlockSpec((2, 8, tn), lambda j, i: (0, 0, j)),
                  pl.BlockSpec((2, 1, tn), lambda j, i: (0, 0, j))],
        out_specs=pl.BlockSpec((tm, tn), lambda j, i: (i, j)),
        name="conv_act_fwd", compiler_params=_cp(("parallel", "parallel"), VMEM_LIMIT),
    )(hu, hu, cw, cb)


def _ffn_up_conv(hn, w_up_bf, cw, cb):
    S = hn.shape[0]
    F = D_FF
    tm = _pick(S, (1024, 512))
    tn = _pick(F, (256, 128))
    nj = F // tn

    def body(hn_ref, wa_ref, wg_ref, cw_ref, cb_ref, hu_ref, hc_ref, act_ref, tail_ref):
        i = pl.program_id(1)

        @pl.when(i == 0)
        def _():
            tail_ref[...] = jnp.zeros_like(tail_ref)

        hn_v = hn_ref[...]
        halves = []
        for h, w_ref in enumerate((wa_ref, wg_ref)):
            hu = jnp.dot(hn_v, w_ref[...], preferred_element_type=F32)
            hu_ref[h] = hu
            hc = _conv3(hu, tail_ref[h], cw_ref[h], cb_ref[h])
            hc_ref[h] = hc
            halves.append(hc)
            tail_ref[h] = hu[tm - 8:, :]
        a, g = halves
        act_ref[...] = (g * _sigmoid(g) * a).astype(BF16)

    both = pl.BlockSpec((2, tm, tn), lambda j, i: (0, i, j))
    return pl.pallas_call(
        body, out_shape=(jax.ShapeDtypeStruct((2, S, F), F32), jax.ShapeDtypeStruct((2, S, F), F32),
                         jax.ShapeDtypeStruct((S, F), BF16)),
        grid=(nj, S // tm),
        in_specs=[pl.BlockSpec((tm, D_MODEL), lambda j, i: (i, 0)),
                  pl.BlockSpec((D_MODEL, tn), lambda j, i: (0, j)),
                  pl.BlockSpec((D_MODEL, tn), lambda j, i: (0, nj + j)),
                  pl.BlockSpec((2, 8, tn), lambda j, i: (0, 0, j)),
                  pl.BlockSpec((2, 1, tn), lambda j, i: (0, 0, j))],
        out_specs=(both, both, pl.BlockSpec((tm, tn), lambda j, i: (i, j))),
        scratch_shapes=[pltpu.VMEM((2, 8, tn), F32)],
        name="ffn_up_conv", compiler_params=_cp(("parallel", "arbitrary"), VMEM_LIMIT),
    )(hn, w_up_bf, w_up_bf, cw, cb)


def _ffn_down_loss(act, w_down_bf, h1, g_final, target):
    S = h1.shape[0]
    tm = _pick(S, (512, 256))

    def body(a_ref, w_ref, h1_ref, g_ref, t_ref, loss_ref, dh_ref, dhb_ref, dg_ref):
        i = pl.program_id(0)

        @pl.when(i == 0)
        def _():
            loss_ref[...] = jnp.zeros_like(loss_ref)
            dg_ref[...] = jnp.zeros_like(dg_ref)

        hf = h1_ref[...] + jnp.dot(a_ref[...], w_ref[...], preferred_element_type=F32)
        g = g_ref[...]
        r = lax.rsqrt(jnp.mean(hf * hf, axis=-1, keepdims=True) + EPS)
        hhat = hf * r
        err = hhat * g - t_ref[...]
        loss_ref[...] += 0.5 * jnp.sum(jnp.mean(err * err, axis=-1, keepdims=True))
        dy = err * (1.0 / D_MODEL)
        dg_ref[0:1, :] += jnp.sum(dy * hhat, axis=0, keepdims=True)
        dhat = dy * g
        dh = r * (dhat - hhat * jnp.mean(dhat * hhat, axis=-1, keepdims=True))
        dh_ref[...] = dh
        dhb_ref[...] = dh.astype(BF16)

    row = pl.BlockSpec((tm, D_MODEL), lambda i: (i, 0))
    return pl.pallas_call(
        body,
        out_shape=(jax.ShapeDtypeStruct((8, LANES), F32), jax.ShapeDtypeStruct((S, D_MODEL), F32),
                   jax.ShapeDtypeStruct((S, D_MODEL), BF16), jax.ShapeDtypeStruct((8, D_MODEL), F32)),
        grid=(S // tm,),
        in_specs=[pl.BlockSpec((tm, D_FF), lambda i: (i, 0)), pl.BlockSpec((D_FF, D_MODEL), lambda i: (0, 0)), row,
                  pl.BlockSpec((1, D_MODEL), lambda i: (0, 0)), row],
        out_specs=(pl.BlockSpec((8, LANES), lambda i: (0, 0)), row, row, pl.BlockSpec((8, D_MODEL), lambda i: (0, 0))),
        name="ffn_down_loss", compiler_params=_cp(("arbitrary",), VMEM_LIMIT),
    )(act, w_down_bf, h1, g_final, target)


def _loss_head(h2, g_final, target):
    S = h2.shape[0]
    tm = _pick(S, (512, 256))

    def body(h_ref, g_ref, t_ref, loss_ref, dh_ref, dhb_ref, dg_ref):
        i = pl.program_id(0)

        @pl.when(i == 0)
        def _():
            loss_ref[...] = jnp.zeros_like(loss_ref)
            dg_ref[...] = jnp.zeros_like(dg_ref)

        hf = h_ref[...]
        g = g_ref[...]
        r = lax.rsqrt(jnp.mean(hf * hf, axis=-1, keepdims=True) + EPS)
        hhat = hf * r
        err = hhat * g - t_ref[...]
        loss_ref[...] += 0.5 * jnp.sum(jnp.mean(err * err, axis=-1, keepdims=True))
        dy = err * (1.0 / D_MODEL)
        dg_ref[0:1, :] += jnp.sum(dy * hhat, axis=0, keepdims=True)
        dhat = dy * g
        dh = r * (dhat - hhat * jnp.mean(dhat * hhat, axis=-1, keepdims=True))
        dh_ref[...] = dh
        dhb_ref[...] = dh.astype(BF16)

    return pl.pallas_call(
        body,
        out_shape=(jax.ShapeDtypeStruct((8, LANES), F32), jax.ShapeDtypeStruct((S, D_MODEL), F32),
                   jax.ShapeDtypeStruct((S, D_MODEL), BF16), jax.ShapeDtypeStruct((8, D_MODEL), F32)),
        grid=(S // tm,),
        in_specs=[pl.BlockSpec((tm, D_MODEL), lambda i: (i, 0)), pl.BlockSpec((1, D_MODEL), lambda i: (0, 0)),
                  pl.BlockSpec((tm, D_MODEL), lambda i: (i, 0))],
        out_specs=(pl.BlockSpec((8, LANES), lambda i: (0, 0)), pl.BlockSpec((tm, D_MODEL), lambda i: (i, 0)),
                   pl.BlockSpec((tm, D_MODEL), lambda i: (i, 0)), pl.BlockSpec((8, D_MODEL), lambda i: (0, 0))),
        name="loss_head", compiler_params=_cp(("arbitrary",), VMEM_LIMIT),
    )(h2, g_final, target)


def _rms_bwd(h, g, dy, res):
    S = h.shape[0]
    tm = _pick(S, (512, 256))

    def body(h_ref, g_ref, dy_ref, r_ref, dh_ref, dhb_ref, dg_ref):
        i = pl.program_id(0)

        @pl.when(i == 0)
        def _():
            dg_ref[...] = jnp.zeros_like(dg_ref)

        hf = h_ref[...]
        dyv = dy_ref[...]
        r = lax.rsqrt(jnp.mean(hf * hf, axis=-1, keepdims=True) + EPS)
        hhat = hf * r
        dg_ref[0:1, :] += jnp.sum(dyv * hhat, axis=0, keepdims=True)
        dhat = dyv * g_ref[...]
        dh = r_ref[...] + r * (dhat - hhat * jnp.mean(dhat * hhat, axis=-1, keepdims=True))
        dh_ref[...] = dh
        dhb_ref[...] = dh.astype(BF16)

    return pl.pallas_call(
        body,
        out_shape=(jax.ShapeDtypeStruct((S, D_MODEL), F32), jax.ShapeDtypeStruct((S, D_MODEL), BF16),
                   jax.ShapeDtypeStruct((8, D_MODEL), F32)),
        grid=(S // tm,),
        in_specs=[pl.BlockSpec((tm, D_MODEL), lambda i: (i, 0)), pl.BlockSpec((1, D_MODEL), lambda i: (0, 0)),
                  pl.BlockSpec((tm, D_MODEL), lambda i: (i, 0)), pl.BlockSpec((tm, D_MODEL), lambda i: (i, 0))],
        out_specs=(pl.BlockSpec((tm, D_MODEL), lambda i: (i, 0)), pl.BlockSpec((tm, D_MODEL), lambda i: (i, 0)),
                   pl.BlockSpec((8, D_MODEL), lambda i: (0, 0))),
        name="rms_bwd", compiler_params=_cp(("arbitrary",), VMEM_LIMIT),
    )(h, g, dy, res)


def _conv_act_bwd(hu, dh2_bf, w_down_bf, cw, cb):
    _, S, F = hu.shape
    tm = _pick(S, (256,))
    tn = _pick(F, (256, 128))
    r8 = tm // 8
    r16 = tm // 16
    n_i = S // tm
    last8 = S // 8 - 1
    last16 = S // 16 - 1

    def body(cur_ref, prev_ref, next_ref, dy_ref, dyn_ref, wd_ref, w_ref, b_ref, dhu_ref, dcw_ref):
        i = pl.program_id(1)
        wd = wd_ref[...]
        dact = lax.dot_general(dy_ref[...], wd, _NT, preferred_element_type=F32)
        dact_n = lax.dot_general(dyn_ref[...], wd, _NT, preferred_element_type=F32)[0:8, :]

        @pl.when(i == 0)
        def _():
            dcw_ref[...] = jnp.zeros_like(dcw_ref)

        rid8 = lax.broadcasted_iota(jnp.int32, (8, tn), 0)
        cur = [cur_ref[0], cur_ref[1]]
        prev = [jnp.where(i > 0, prev_ref[h], 0.0) for h in range(2)]
        nxt = [next_ref[0], next_ref[1]]
        w = [w_ref[0], w_ref[1]]

        def gate_grads(a, g, d):
            sg = _sigmoid(g)
            return d * (g * sg), d * a * (sg * (1.0 + g * (1.0 - sg)))

        taps = [(_shift_rows(cur[h], prev[h], 2), _shift_rows(cur[h], prev[h], 1), cur[h]) for h in range(2)]
        a, g = [w[h][0:1, :] * taps[h][0] + w[h][1:2, :] * taps[h][1] + w[h][2:3, :] * taps[h][2] + b_ref[h]
                for h in range(2)]
        dhc = gate_grads(a, g, dact)
        a_n = _conv3(nxt[0], cur[0][tm - 8:, :], w[0], b_ref[0])
        g_n = _conv3(nxt[1], cur[1][tm - 8:, :], w[1], b_ref[1])
        dhc_n = gate_grads(a_n, g_n, dact_n)
        for h in range(2):
            d = dhc[h]
            dn = jnp.where(i < n_i - 1, dhc_n[h], 0.0)
            dhu = (w[h][2:3, :] * d + w[h][1:2, :] * _shift_rows_up(d, dn, 1)
                   + w[h][0:1, :] * _shift_rows_up(d, dn, 2))
            dhu_ref[h] = dhu.astype(BF16)
            t0, t1, t2 = [jnp.sum(d * t, axis=0, keepdims=True) for t in taps[h]]
            t3 = jnp.sum(d, axis=0, keepdims=True)
            dcw_ref[h] += jnp.where(rid8 == 0, t0, jnp.where(rid8 == 1, t1, jnp.where(rid8 == 2, t2, jnp.where(rid8 == 3, t3, 0.0))))

    return pl.pallas_call(
        body,
        out_shape=(jax.ShapeDtypeStruct((2, S, F), BF16), jax.ShapeDtypeStruct((2, 8, F), F32)),
        grid=(F // tn, n_i),
        in_specs=[pl.BlockSpec((2, tm, tn), lambda j, i: (0, i, j)),
                  pl.BlockSpec((2, 8, tn), lambda j, i: (0, jnp.maximum(i * r8 - 1, 0), j)),
                  pl.BlockSpec((2, 8, tn), lambda j, i: (0, jnp.minimum((i + 1) * r8, last8), j)),
                  pl.BlockSpec((tm, D_MODEL), lambda j, i: (i, 0)),
                  pl.BlockSpec((16, D_MODEL), lambda j, i: (jnp.minimum((i + 1) * r16, last16), 0)),
                  pl.BlockSpec((tn, D_MODEL), lambda j, i: (j, 0)),
                  pl.BlockSpec((2, 8, tn), lambda j, i: (0, 0, j)),
                  pl.BlockSpec((2, 1, tn), lambda j, i: (0, 0, j))],
        out_specs=(pl.BlockSpec((2, tm, tn), lambda j, i: (0, i, j)), pl.BlockSpec((2, 8, tn), lambda j, i: (0, 0, j))),
        name="conv_act_bwd", compiler_params=_cp(("parallel", "arbitrary"), VMEM_LIMIT),
    )(hu, hu, hu, dh2_bf, dh2_bf, w_down_bf, cw, cb)


def _conv_gate_bwd(hc, hu, dact, cw):
    _, S, F = hu.shape
    tm = _pick(S, (256,))
    tn = _pick(F, (256, 128))
    r8 = tm // 8
    n_i = S // tm
    last8 = S // 8 - 1

    def body(hc_ref, hcn_ref, hu_ref, da_ref, dan_ref, w_ref, dhu_ref, dcw_ref):
        i = pl.program_id(1)

        @pl.when(i == 0)
        def _():
            dcw_ref[...] = jnp.zeros_like(dcw_ref)

        rid8 = lax.broadcasted_iota(jnp.int32, (8, tn), 0)

        def gate_grads(a, g, d):
            sg = _sigmoid(g)
            return d * (g * sg), d * a * (sg * (1.0 + g * (1.0 - sg)))

        dhc = gate_grads(hc_ref[0], hc_ref[1], da_ref[...])
        dhc_n = gate_grads(hcn_ref[0], hcn_ref[1], dan_ref[...])
        for h in range(2):
            w = w_ref[h]
            d = dhc[h]
            dn = jnp.where(i < n_i - 1, dhc_n[h], 0.0)
            u1 = _shift_rows_up(d, dn, 1)
            u2 = _shift_rows_up(d, dn, 2)
            dhu_ref[h] = (w[2:3, :] * d + w[1:2, :] * u1 + w[0:1, :] * u2).astype(BF16)
            x = hu_ref[h]
            t0, t1, t2, t3 = [jnp.sum(t, axis=0, keepdims=True) for t in (u2 * x, u1 * x, d * x, d)]
            dcw_ref[h] += jnp.where(rid8 == 0, t0, jnp.where(rid8 == 1, t1, jnp.where(rid8 == 2, t2, jnp.where(rid8 == 3, t3, 0.0))))

    cur = pl.BlockSpec((2, tm, tn), lambda j, i: (0, i, j))
    return pl.pallas_call(
        body,
        out_shape=(jax.ShapeDtypeStruct((2, S, F), BF16), jax.ShapeDtypeStruct((2, 8, F), F32)),
        grid=(F // tn, n_i),
        in_specs=[cur, pl.BlockSpec((2, 8, tn), lambda j, i: (0, jnp.minimum((i + 1) * r8, last8), j)), cur,
                  pl.BlockSpec((tm, tn), lambda j, i: (i, j)),
                  pl.BlockSpec((8, tn), lambda j, i: (jnp.minimum((i + 1) * r8, last8), j)),
                  pl.BlockSpec((2, 8, tn), lambda j, i: (0, 0, j))],
        out_specs=(cur, pl.BlockSpec((2, 8, tn), lambda j, i: (0, 0, j))),
        name="conv_gate_bwd", compiler_params=_cp(("parallel", "arbitrary"), VMEM_LIMIT),
    )(hc, hc, hu, dact, dact, cw)


def _attn_prep(att, lse, dmix, qa):
    S = att.shape[0]
    tm = _pick(S, (512, 256))

    def body(o_ref, lse_ref, do_ref, q_ref, qb_ref, doa_ref):
        lane = lax.broadcasted_iota(jnp.int32, (tm, LANES), 1)
        do = do_ref[...]
        prod = o_ref[...] * do
        for hh in range(2):
            sel = (lane >= HEAD_DIM) if hh else (lane < HEAD_DIM)
            delta = jnp.sum(jnp.where(sel, prod, 0.0), axis=-1, keepdims=True)
            dod = pltpu.roll(do, HEAD_DIM, 1) if hh else do
            cols = slice(hh * LANES, (hh + 1) * LANES)
            doa_ref[:, cols] = jnp.where(lane < HEAD_DIM, dod, _aug(lane, _split3f(-delta))).astype(BF16)
            l3 = _split3f(-lse_ref[:, hh * HEAD_DIM:hh * HEAD_DIM + 1])
            augl = _aug(lane, [0.0] * 6 + l3).astype(BF16)
            qb_ref[:, cols] = jnp.where((lane >= HEAD_DIM + 6) & (lane < HEAD_DIM + 9), augl, q_ref[:, cols])

    return pl.pallas_call(
        body,
        out_shape=(jax.ShapeDtypeStruct(qa.shape, BF16), jax.ShapeDtypeStruct(qa.shape, BF16)),
        grid=(S // tm, N_PAIRS),
        in_specs=[pl.BlockSpec((tm, LANES), lambda i, p: (i, p)), pl.BlockSpec((tm, LANES), lambda i, p: (i, p)),
                  pl.BlockSpec((tm, LANES), lambda i, p: (i, p)), pl.BlockSpec((tm, 2 * LANES), lambda i, p: (i, p))],
        out_specs=(pl.BlockSpec((tm, 2 * LANES), lambda i, p: (i, p)), pl.BlockSpec((tm, 2 * LANES), lambda i, p: (i, p))),
        name="attn_prep", compiler_params=_cp(("parallel", "parallel")),
    )(att, lse, dmix, qa)


def _attn_bwd(qb, ka, va, doa):
    S = qb.shape[0]
    tk = _pick(S, (512, 256))
    tq = tk
    nq = S // tq

    def pair(a, scale=None):
        lane = lax.broadcasted_iota(jnp.int32, (a.shape[0], LANES), 1)
        out = jnp.where(lane < HEAD_DIM, a[:, :LANES], pltpu.roll(a[:, LANES:], HEAD_DIM, 1))
        return out if scale is None else out * scale

    def lanes01(a, col, sign):
        lane = lax.broadcasted_iota(jnp.int32, (a.shape[0], LANES), 1)
        return jnp.where(lane == 0, sign * a[:, col:col + 1], jnp.where(lane == 1, sign * a[:, LANES + col:LANES + col + 1], 0.0))

    def body(q_ref, do_ref, k_ref, v_ref, dqc_ref, dkc_ref, dvc_ref, dcq_ref, dck_ref, dq_ref, dka_ref, dva_ref):
        kb = pl.program_id(1)

        @pl.when(kb == 0)
        def _():
            dq_ref[...] = jnp.zeros_like(dq_ref)

        dka_ref[...] = jnp.zeros_like(dka_ref)
        dva_ref[...] = jnp.zeros_like(dva_ref)
        rid = lax.broadcasted_iota(jnp.int32, (tk, tq), 0)
        cid = lax.broadcasted_iota(jnp.int32, (tk, tq), 1)

        def tile(qi, masked):
            qs = pl.multiple_of(qi * tq, tq)
            for h in range(2):
                cols = slice(h * LANES, (h + 1) * LANES)
                qblk = q_ref[pl.ds(qs, tq), cols]
                doblk = do_ref[pl.ds(qs, tq), cols]
                kh = k_ref[:, cols]
                p = jnp.exp(lax.dot_general(kh, qblk, _NT, preferred_element_type=F32))
                if masked:
                    p = jnp.where(cid >= rid, p, 0.0)
                ds = (p * lax.dot_general(v_ref[:, cols], doblk, _NT, preferred_element_type=F32)).astype(BF16)
                dva_ref[:, cols] += jnp.dot(p.astype(BF16), doblk, preferred_element_type=F32)
                dka_ref[:, cols] += jnp.dot(ds, qblk, preferred_element_type=F32)
                dq_ref[pl.ds(qs, tq), cols] += lax.dot_general(ds, kh, _TN, preferred_element_type=F32)

        tile(kb, True)

        def step(qi, carry):
            tile(qi, False)
            return carry

        lax.fori_loop(kb + 1, nq, step, 0)
        dka = dka_ref[...]
        dkc_ref[...] = pair(dka).astype(BF16)
        dvc_ref[...] = pair(dva_ref[...]).astype(BF16)
        dck_ref[...] = lanes01(dka, HEAD_DIM + 3, -1.0)

        @pl.when(kb == nq - 1)
        def _():
            dqa = dq_ref[...]
            dqc_ref[...] = pair(dqa, HEAD_DIM ** -0.5).astype(BF16)
            dcq_ref[...] = lanes01(dqa, HEAD_DIM, 1.0)

    wide = 2 * LANES
    half = jax.ShapeDtypeStruct((S, ATT_WIDTH), BF16)
    slabs = jax.ShapeDtypeStruct((N_PAIRS, S, LANES), F32)
    return pl.pallas_call(
        body,
        out_shape=(half, half, half, slabs, slabs),
        grid=(N_PAIRS, nq),
        in_specs=[pl.BlockSpec((S, wide), lambda p, j: (0, p)), pl.BlockSpec((S, wide), lambda p, j: (0, p)),
                  pl.BlockSpec((tk, wide), lambda p, j: (j, p)), pl.BlockSpec((tk, wide), lambda p, j: (j, p))],
        out_specs=(pl.BlockSpec((S, LANES), lambda p, j: (0, p)), pl.BlockSpec((tk, LANES), lambda p, j: (j, p)),
                   pl.BlockSpec((tk, LANES), lambda p, j: (j, p)), pl.BlockSpec((None, S, LANES), lambda p, j: (p, 0, 0)),
                   pl.BlockSpec((None, tk, LANES), lambda p, j: (p, j, 0))),
        scratch_shapes=[pltpu.VMEM((S, wide), F32), pltpu.VMEM((tk, wide), F32), pltpu.VMEM((tk, wide), F32)],
        name="attn_bwd", compiler_params=_cp(("parallel", "arbitrary"), VMEM_LIMIT),
    )(qb, doa, ka, va)


def _attn_delta_old(o, do):
    S = o.shape[0]
    tm = _pick(S, (512, 256))
    ones = _group_ones()

    def body(o_ref, do_ref, ones_ref, d_ref):
        d_ref[...] = _dot3(o_ref[...] * do_ref[...], ones_ref[...])

    return pl.pallas_call(
        body, out_shape=jax.ShapeDtypeStruct((S, ATT_WIDTH), F32), grid=(S // tm,),
        in_specs=[pl.BlockSpec((tm, ATT_WIDTH), lambda i: (i, 0)), pl.BlockSpec((tm, ATT_WIDTH), lambda i: (i, 0)),
                  pl.BlockSpec((ATT_WIDTH, ATT_WIDTH), lambda i: (0, 0))],
        out_specs=pl.BlockSpec((tm, ATT_WIDTH), lambda i: (i, 0)),
        name="attn_delta", compiler_params=_cp(("parallel",)),
    )(o, do, ones)


def _attn_bwd_old(qkv, do_bf, c_cols, c_rows, lse_rows, dl_rows):
    S = qkv.shape[0]
    tk = _pick(S, (256,))
    tq = tk
    nq = S // tq
    nt = (((1,), (1,)), ((), ()))
    tn_dims = (((0,), (0,)), ((), ()))

    def body(q_ref, do_ref, k_ref, v_ref, cc_ref, cr_ref, lse_ref, dl_ref, dq_ref, dk_ref, dv_ref, dcs_ref, dcq_ref):
        kb = pl.program_id(1)

        @pl.when(kb == 0)
        def _():
            dq_ref[...] = jnp.zeros_like(dq_ref)
            dcq_ref[...] = jnp.zeros_like(dcq_ref)

        lane = lax.broadcasted_iota(jnp.int32, (tk, LANES), 1)
        rid = lax.broadcasted_iota(jnp.int32, (tk, tq), 0)
        cid = lax.broadcasted_iota(jnp.int32, (tk, tq), 1)
        k = k_ref[...]
        v = v_ref[...]
        dks, dvs, dcs = [], [], []
        for hh in range(2):
            sel = (lane >= GROUP_DIM) if hh else (lane < GROUP_DIM)
            km = jnp.where(sel, k, jnp.zeros_like(k))
            vm = jnp.where(sel, v, jnp.zeros_like(v))
            cs = cc_ref[:, hh * HEAD_DIM:hh * HEAD_DIM + 1]

            def step(qb, carry, km=km, vm=vm, cs=cs, hh=hh):
                dk_acc, dv_acc, dc_acc = carry
                qs = pl.multiple_of(qb * tq, tq)
                qblk = q_ref[pl.ds(qs, tq), :]
                doblk = do_ref[pl.ds(qs, tq), :]
                s = lax.dot_general(km, qblk, nt, preferred_element_type=F32)
                s = s + (cr_ref[hh:hh + 1, pl.ds(qs, tq)] - cs)
                p = jnp.exp(s - lse_ref[hh:hh + 1, pl.ds(qs, tq)])
                p = jnp.where((qb > kb) | (cid >= rid), p, 0.0)
                dp = lax.dot_general(vm, doblk, nt, preferred_element_type=F32)
                ds = p * (dp - dl_ref[hh:hh + 1, pl.ds(qs, tq)])
                ds_bf = ds.astype(BF16)
                dv_acc = dv_acc + jnp.dot(p.astype(BF16), doblk, preferred_element_type=F32)
                dk_acc = dk_acc + jnp.dot(ds_bf, qblk, preferred_element_type=F32)
                dc_acc = dc_acc + jnp.sum(ds, axis=-1, keepdims=True)
                dq_ref[pl.ds(qs, tq), :] += lax.dot_general(ds_bf, km, tn_dims, preferred_element_type=F32)
                dcq_ref[hh:hh + 1, pl.ds(qs, tq)] += jnp.sum(ds, axis=0, keepdims=True)
                return dk_acc, dv_acc, dc_acc

            init = (jnp.zeros((tk, LANES), F32), jnp.zeros((tk, LANES), F32), jnp.zeros((tk, 1), F32))
            dk_acc, dv_acc, dc_acc = lax.fori_loop(kb, nq, step, init)
            dks.append(dk_acc)
            dvs.append(dv_acc)
            dcs.append(dc_acc)
        dk_ref[...] = jnp.where(lane < GROUP_DIM, dks[0], dks[1])
        dv_ref[...] = jnp.where(lane < GROUP_DIM, dvs[0], dvs[1])
        dcs_ref[...] = jnp.where(lane == 0, -dcs[0], jnp.where(lane == 1, -dcs[1], 0.0))

    return pl.pallas_call(
        body,
        out_shape=(jax.ShapeDtypeStruct((S, ATT_WIDTH), F32), jax.ShapeDtypeStruct((S, ATT_WIDTH), F32),
                   jax.ShapeDtypeStruct((S, ATT_WIDTH), F32), jax.ShapeDtypeStruct((N_PAIRS, S, LANES), F32),
                   jax.ShapeDtypeStruct((N_PAIRS, 8, S), F32)),
        grid=(N_PAIRS, nq),
        in_specs=[pl.BlockSpec((S, LANES), lambda p, j: (0, p)),
                  pl.BlockSpec((S, LANES), lambda p, j: (0, p)),
                  pl.BlockSpec((tk, LANES), lambda p, j: (j, N_PAIRS + p)),
                  pl.BlockSpec((tk, LANES), lambda p, j: (j, 2 * N_PAIRS + p)),
                  pl.BlockSpec((None, tk, LANES), lambda p, j: (p, j, 0)),
                  pl.BlockSpec((None, 8, S), lambda p, j: (p, 0, 0)),
                  pl.BlockSpec((None, 8, S), lambda p, j: (p, 0, 0)),
                  pl.BlockSpec((None, 8, S), lambda p, j: (p, 0, 0))],
        out_specs=(pl.BlockSpec((S, LANES), lambda p, j: (0, p)),
                   pl.BlockSpec((tk, LANES), lambda p, j: (j, p)),
                   pl.BlockSpec((tk, LANES), lambda p, j: (j, p)),
                   pl.BlockSpec((None, tk, LANES), lambda p, j: (p, j, 0)),
                   pl.BlockSpec((None, 8, S), lambda p, j: (p, 0, 0))),
        name="attn_bwd", compiler_params=_cp(("parallel", "arbitrary"), VMEM_LIMIT),
    )(qkv, do_bf, qkv, qkv, c_cols, c_rows, lse_rows, dl_rows)


def _gmlp_bwd(ug, dsg, gain, w_s, wt_s, bias_full):
    S = ug.shape[0]
    tm = _pick(S, (512, 256, 128))
    n_chunks = tm // CHUNK
    n_i = S // tm
    ones = _group_ones()
    nt = (((1,), (1,)), ((), ()))

    def body(ug_ref, dsg_ref, gain_ref, w_ref, wt_ref, bias_ref, ones_ref, dug_ref, dw_ref, dgain_ref, dbias_ref,
             dbacc_ref):
        i = pl.program_id(0)

        @pl.when(i == 0)
        def _():
            dw_ref[...] = jnp.zeros_like(dw_ref)
            dgain_ref[...] = jnp.zeros_like(dgain_ref)
            dbacc_ref[...] = jnp.zeros_like(dbacc_ref)

        ones_m = ones_ref[...]
        pu = ug_ref[:, :GMLP_WIDTH]
        pg = ug_ref[:, GMLP_WIDTH:]
        u = _gelu(pu)
        vr = _gelu(pg)
        ms = _dot3(vr * vr, ones_m) * (1.0 / GROUP_DIM)
        rinv = lax.rsqrt(ms + EPS)
        vhat = vr * rinv
        gain_v = gain_ref[...]
        vn = (vhat * gain_v).astype(BF16)
        mixed = _gmlp_mixed(vn, w_ref, bias_ref[...], n_chunks)
        dsg_v = dsg_ref[...]
        du = dsg_v * mixed
        dmixed = dsg_v * u
        dm_bf = dmixed.astype(BF16)
        lane = lax.broadcasted_iota(jnp.int32, (CHUNK, LANES), 1)
        row = lax.broadcasted_iota(jnp.int32, (CHUNK, CHUNK), 0)
        col = lax.broadcasted_iota(jnp.int32, (CHUNK, CHUNK), 1)
        wts = [jnp.where(col >= row, wt_ref[g], 0.0).astype(BF16) for g in range(N_GROUPS)]
        dvn_rows = []
        dbsum = jnp.zeros((CHUNK, GMLP_WIDTH), F32)
        for ci in range(n_chunks):
            rs = slice(ci * CHUNK, (ci + 1) * CHUNK)
            dbsum = dbsum + dmixed[rs, :]
            cols = []
            for pp in range(N_GROUPS // 2):
                cs = slice(pp * LANES, (pp + 1) * LANES)
                dm = dm_bf[rs, cs]
                dm_lo = jnp.where(lane < GROUP_DIM, dm, jnp.zeros_like(dm))
                dm_hi = jnp.where(lane >= GROUP_DIM, dm, jnp.zeros_like(dm))
                vb = vn[rs, cs]
                dw_ref[2 * pp] += lax.dot_general(dm_lo, vb, nt, preferred_element_type=F32)
                dw_ref[2 * pp + 1] += lax.dot_general(dm_hi, vb, nt, preferred_element_type=F32)
                cols.append(jnp.dot(wts[2 * pp], dm_lo, preferred_element_type=F32)
                            + jnp.dot(wts[2 * pp + 1], dm_hi, preferred_element_type=F32))
            dvn_rows.append(jnp.concatenate(cols, axis=1))
        dvn = jnp.concatenate(dvn_rows, axis=0)
        dbacc_ref[...] += dbsum
        dgain_ref[0:1, :] += jnp.sum(dvn * vhat, axis=0, keepdims=True)
        dvhat = dvn * gain_v
        gm = _dot3(dvhat * vhat, ones_m) * (1.0 / GROUP_DIM)
        dvr = rinv * (dvhat - vhat * gm)
        dug_ref[:, :GMLP_WIDTH] = (du * _gelu_grad(pu)).astype(BF16)
        dug_ref[:, GMLP_WIDTH:] = (dvr * _gelu_grad(pg)).astype(BF16)

        @pl.when(i == n_i - 1)
        def _():
            for g in range(N_GROUPS):
                dw_ref[g] = jnp.where(row >= col, dw_ref[g], 0.0)
            dbias_ref[...] = _dot3(dbacc_ref[...], ones_m)

    return pl.pallas_call(
        body,
        out_shape=(jax.ShapeDtypeStruct((S, 2 * GMLP_WIDTH), BF16), jax.ShapeDtypeStruct((N_GROUPS, CHUNK, CHUNK), F32),
                   jax.ShapeDtypeStruct((8, GMLP_WIDTH), F32), jax.ShapeDtypeStruct((CHUNK, GMLP_WIDTH), F32)),
        grid=(n_i,),
        in_specs=[pl.BlockSpec((tm, 2 * GMLP_WIDTH), lambda i: (i, 0)), pl.BlockSpec((tm, GMLP_WIDTH), lambda i: (i, 1)),
                  pl.BlockSpec((1, GMLP_WIDTH), lambda i: (0, 0)),
                  pl.BlockSpec((N_GROUPS, CHUNK, CHUNK), lambda i: (0, 0, 0)),
                  pl.BlockSpec((N_GROUPS, CHUNK, CHUNK), lambda i: (0, 0, 0)),
                  pl.BlockSpec((CHUNK, GMLP_WIDTH), lambda i: (0, 0)),
                  pl.BlockSpec((GMLP_WIDTH, GMLP_WIDTH), lambda i: (0, 0))],
        out_specs=(pl.BlockSpec((tm, 2 * GMLP_WIDTH), lambda i: (i, 0)),
                   pl.BlockSpec((N_GROUPS, CHUNK, CHUNK), lambda i: (0, 0, 0)),
                   pl.BlockSpec((8, GMLP_WIDTH), lambda i: (0, 0)),
                   pl.BlockSpec((CHUNK, GMLP_WIDTH), lambda i: (0, 0))),
        scratch_shapes=[pltpu.VMEM((CHUNK, GMLP_WIDTH), F32)],
        name="gmlp_bwd", compiler_params=_cp(("arbitrary",), VMEM_LIMIT),
    )(ug, dsg, gain, w_s, wt_s, bias_full, ones)


def _gate_bwd(dcq, dck, zf):
    S = zf.shape[0]
    tm = _pick(S, (256,))
    n_i = S // tm
    triu = (lax.broadcasted_iota(jnp.int32, (tm, tm), 0) <= lax.broadcasted_iota(jnp.int32, (tm, tm), 1)).astype(BF16)

    def body(dcq_ref, dck_ref, zf_ref, tri_ref, dzf_ref, dbf_ref, carry_ref):
        i = pl.program_id(0)

        @pl.when(i == 0)
        def _():
            carry_ref[...] = jnp.zeros_like(carry_ref)
            dbf_ref[...] = jnp.zeros_like(dbf_ref)

        lane = lax.broadcasted_iota(jnp.int32, (tm, LANES), 1)
        dc = jnp.zeros((tm, LANES), F32)
        for p in range(N_PAIRS):
            slab = dcq_ref[p] + dck_ref[p]
            for hh in range(2):
                dc = dc + jnp.where(lane == 2 * p + hh, slab[:, hh:hh + 1], 0.0)
        dlf = _dot3l(tri_ref[...], dc) + carry_ref[0:1, :]
        carry_ref[0:1, :] = dlf[0:1, :]
        dz = jnp.where(lane < N_HEADS, dlf * _sigmoid(-zf_ref[...]), 0.0)
        dzf_ref[...] = dz.astype(BF16)
        dbf_ref[0:1, :] += jnp.sum(dz, axis=0, keepdims=True)

    return pl.pallas_call(
        body,
        out_shape=(jax.ShapeDtypeStruct((S, LANES), BF16), jax.ShapeDtypeStruct((8, LANES), F32)),
        grid=(n_i,),
        in_specs=[pl.BlockSpec((N_PAIRS, tm, LANES), lambda i: (0, n_i - 1 - i, 0)),
                  pl.BlockSpec((N_PAIRS, tm, LANES), lambda i: (0, n_i - 1 - i, 0)),
                  pl.BlockSpec((tm, LANES), lambda i: (n_i - 1 - i, 0)),
                  pl.BlockSpec((tm, tm), lambda i: (0, 0))],
        out_specs=(pl.BlockSpec((tm, LANES), lambda i: (n_i - 1 - i, 0)), pl.BlockSpec((8, LANES), lambda i: (0, 0))),
        scratch_shapes=[pltpu.VMEM((8, LANES), F32)],
        name="gate_bwd", compiler_params=_cp(("arbitrary",), VMEM_LIMIT),
    )(dcq, dck, zf, triu)


def _out_proj_fwd(att_bf, sg, w_out_bf, x, g_ffn):
    S = x.shape[0]
    tm = _pick(S, (512, 256))

    def body(a_ref, s_ref, w_ref, x_ref, g_ref, h_ref, hn_ref):
        h = (x_ref[...] + jnp.dot(a_ref[...], w_ref[:ATT_WIDTH, :], preferred_element_type=F32)
             + jnp.dot(s_ref[...], w_ref[ATT_WIDTH:, :], preferred_element_type=F32))
        h_ref[...] = h
        r = lax.rsqrt(jnp.mean(h * h, axis=-1, keepdims=True) + EPS)
        hn_ref[...] = ((h * r) * g_ref[...]).astype(BF16)

    row = pl.BlockSpec((tm, D_MODEL), lambda i: (i, 0))
    half = pl.BlockSpec((tm, ATT_WIDTH), lambda i: (i, 0))
    return pl.pallas_call(
        body, out_shape=(jax.ShapeDtypeStruct((S, D_MODEL), F32), jax.ShapeDtypeStruct((S, D_MODEL), BF16)),
        grid=(S // tm,),
        in_specs=[half, half, pl.BlockSpec((D_MODEL, D_MODEL), lambda i: (0, 0)), row,
                  pl.BlockSpec((1, D_MODEL), lambda i: (0, 0))],
        out_specs=(row, row), name="out_proj", compiler_params=_cp(("parallel",), VMEM_LIMIT),
    )(att_bf, sg, w_out_bf, x, g_ffn)


_IN_PIECES = ((0, ATT_WIDTH), (ATT_WIDTH, ATT_WIDTH), (2 * ATT_WIDTH, ATT_WIDTH), (QKV, 2 * GMLP_WIDTH), (UG_END, LANES))


def _inproj_bwd_dx(pieces, w_pad, x, g_mix, dh1):
    S = x.shape[0]
    tm = _pick(S, (512, 256))

    def body(*refs):
        p_refs, (w_ref, x_ref, g_ref, r_ref, dx_ref, dg_ref) = refs[:5], refs[5:]
        i = pl.program_id(0)

        @pl.when(i == 0)
        def _():
            dg_ref[...] = jnp.zeros_like(dg_ref)

        dxn = None
        for p_ref, (c0, width) in zip(p_refs, _IN_PIECES):
            part = lax.dot_general(p_ref[...], w_ref[:, c0:c0 + width], _NT, preferred_element_type=F32)
            dxn = part if dxn is None else dxn + part
        xf = x_ref[...]
        r = lax.rsqrt(jnp.mean(xf * xf, axis=-1, keepdims=True) + EPS)
        xhat = xf * r
        dg_ref[0:1, :] += jnp.sum(dxn * xhat, axis=0, keepdims=True)
        dhat = dxn * g_ref[...]
        dx_ref[...] = r_ref[...] + r * (dhat - xhat * jnp.mean(dhat * xhat, axis=-1, keepdims=True))

    row = pl.BlockSpec((tm, D_MODEL), lambda i: (i, 0))
    return pl.pallas_call(
        body, out_shape=(jax.ShapeDtypeStruct((S, D_MODEL), F32), jax.ShapeDtypeStruct((8, D_MODEL), F32)),
        grid=(S // tm,),
        in_specs=[pl.BlockSpec((tm, width), lambda i: (i, 0)) for _, width in _IN_PIECES]
        + [pl.BlockSpec((D_MODEL, IN_PAD), lambda i: (0, 0)), row, pl.BlockSpec((1, D_MODEL), lambda i: (0, 0)), row],
        out_specs=(row, pl.BlockSpec((8, D_MODEL), lambda i: (0, 0))),
        name="in_proj_dx", compiler_params=_cp(("arbitrary",), VMEM_LIMIT),
    )(*pieces, w_pad, x, g_mix, dh1)


def _inproj_bwd_dw(xn, pieces):
    S = xn.shape[0]
    tk = _pick(S, (512, 256))

    def body(*refs):
        x_ref, p_refs, o_ref = refs[0], refs[1:6], refs[6]
        k = pl.program_id(0)

        @pl.when(k == 0)
        def _():
            o_ref[...] = jnp.zeros_like(o_ref)

        xb = x_ref[...]
        for p_ref, (c0, width) in zip(p_refs, _IN_PIECES):
            o_ref[:, c0:c0 + width] += lax.dot_general(xb, p_ref[...], _TN, preferred_element_type=F32)

    return pl.pallas_call(
        body, out_shape=jax.ShapeDtypeStruct((D_MODEL, IN_PAD), F32), grid=(S // tk,),
        in_specs=[pl.BlockSpec((tk, D_MODEL), lambda k: (k, 0))]
        + [pl.BlockSpec((tk, width), lambda k: (k, 0)) for _, width in _IN_PIECES],
        out_specs=pl.BlockSpec((D_MODEL, IN_PAD), lambda k: (0, 0)),
        name="in_proj_dw", compiler_params=_cp(("arbitrary",), VMEM_LIMIT),
    )(xn, *pieces)


def _adamw(w, m, v, parts, name):
    R, C = w.shape
    tr = R
    for cand in (256, 128, 64, 32, 16, 8):
        if R % cand == 0 and R > cand:
            tr = cand
            break
    c1 = 1.0 / (1.0 - ADAM_B1 ** ADAM_STEP)
    c2 = 1.0 / (1.0 - ADAM_B2 ** ADAM_STEP)

    def body(w_ref, m_ref, v_ref, p_ref, g_ref, d_ref, nm_ref, nv_ref):
        g = p_ref[0].astype(F32)
        for j in range(1, N_DEV):
            g = g + p_ref[j].astype(F32)
        g_ref[...] = g
        nm = ADAM_B1 * m_ref[...] + (1.0 - ADAM_B1) * g
        nv = ADAM_B2 * v_ref[...] + (1.0 - ADAM_B2) * (g * g)
        nm_ref[...] = nm
        nv_ref[...] = nv
        d_ref[...] = -ADAM_LR * ((nm * c1) / (jnp.sqrt(nv * c2) + ADAM_EPS) + ADAM_WD * w_ref[...])

    spec = pl.BlockSpec((tr, C), lambda i: (i, 0))
    shp = jax.ShapeDtypeStruct((R, C), F32)
    return pl.pallas_call(
        body, out_shape=(shp, shp, shp, shp), grid=(R // tr,),
        in_specs=[spec, spec, spec, pl.BlockSpec((N_DEV, tr, C), lambda i: (0, i, 0))],
        out_specs=(spec, spec, spec, spec),
        name=name, compiler_params=_cp(("parallel",), VMEM_LIMIT),
    )(w, m, v, parts)


def _place():
    x, y, c = lax.axis_index("x"), lax.axis_index("y"), lax.axis_index("c")
    return x, y, c


def _all_gather(blocks, name):
    n = len(blocks)

    def body(*refs):
        ins, outs = refs[:n], refs[n:2 * n]
        send_sems, recv_sems, local_sems = refs[2 * n:]
        x, y, c = _place()
        me, sibling = (x, y, c), (x, y, 1 - c)
        chips = [(1 - x, y), (x, 1 - y), (1 - x, 1 - y)]
        sends = []
        for a in range(n):
            out = outs[a]

            def slot(px, py, pc, out=out):
                return out.at[4 * px + 2 * py + pc]

            def copy(k, block, to, src=None, a=a, slot=slot):
                return pltpu.make_async_remote_copy(
                    src_ref=slot(*block) if src is None else src, dst_ref=slot(*block),
                    send_sem=send_sems.at[a, k], recv_sem=recv_sems.at[a, k], device_id=to, device_id_type=MESH)

            mine = pltpu.make_async_copy(ins[a], slot(*me), local_sems.at[a])
            mine.start()
            first = [copy(0, me, sibling, src=ins[a])]
            first += [copy(1 + j, me, (*chip, c), src=ins[a]) for j, chip in enumerate(chips)]
            for cp in first:
                cp.start()
            sends.append((mine, first, copy))
        for a in range(n):
            mine, first, copy = sends[a]
            passed = [copy(4 + j, (*chip, c), sibling) for j, chip in enumerate(chips)]
            for j, chip in enumerate(chips):
                copy(1 + j, (*chip, c), me).wait_recv()
                passed[j].start()
            copy(0, sibling, me).wait_recv()
            for j, chip in enumerate(chips):
                copy(4 + j, (*chip, 1 - c), me).wait_recv()
            for cp in first + passed:
                cp.wait_send()
            mine.wait()

    any_spec = pl.BlockSpec(memory_space=pl.ANY)
    return pl.pallas_call(
        body, out_shape=tuple(jax.ShapeDtypeStruct((N_DEV,) + b.shape, b.dtype) for b in blocks),
        in_specs=[any_spec] * n, out_specs=tuple([any_spec] * n),
        scratch_shapes=[pltpu.SemaphoreType.DMA((n, 7)), pltpu.SemaphoreType.DMA((n, 7)), pltpu.SemaphoreType.DMA((n,))],
        name=name,
    )(*blocks)


def _exchange_shards(parts, name):
    n = len(parts)

    def body(*refs):
        ins, outs = refs[:n], refs[n:2 * n]
        send_sems, recv_sems, local_sems = refs[2 * n:]
        x, y, c = _place()
        me = 4 * x + 2 * y + c
        started = []
        for a in range(n):
            mine = pltpu.make_async_copy(ins[a].at[me], outs[a].at[me], local_sems.at[a])
            mine.start()
            started.append(mine)
            for k in range(1, N_DEV):
                px, py, pc = x ^ ((k >> 2) & 1), y ^ ((k >> 1) & 1), c ^ (k & 1)
                cp = pltpu.make_async_remote_copy(
                    src_ref=ins[a].at[4 * px + 2 * py + pc], dst_ref=outs[a].at[me],
                    send_sem=send_sems.at[a, k - 1], recv_sem=recv_sems.at[a, k - 1],
                    device_id=(px, py, pc), device_id_type=MESH)
                cp.start()
                started.append(cp)
        for cp in started:
            cp.wait()

    any_spec = pl.BlockSpec(memory_space=pl.ANY)
    return pl.pallas_call(
        body, out_shape=tuple(jax.ShapeDtypeStruct(p.shape, p.dtype) for p in parts),
        in_specs=[any_spec] * n, out_specs=tuple([any_spec] * n),
        scratch_shapes=[pltpu.SemaphoreType.DMA((n, 7)), pltpu.SemaphoreType.DMA((n, 7)), pltpu.SemaphoreType.DMA((n,))],
        name=name,
    )(*parts)


_HBM = pl.BlockSpec(memory_space=pltpu.HBM)
_SEM = pl.BlockSpec(memory_space=pltpu.SEMAPHORE)
_EFFECT = pltpu.SideEffectType.DATAFLOW_SIDE_EFFECTING


def _peers(x, y, c):
    out = []
    for k in range(1, N_DEV):
        px, py, pc = x ^ ((k >> 2) & 1), y ^ ((k >> 1) & 1), c ^ (k & 1)
        out.append((k, (px, py, pc), 4 * px + 2 * py + pc))
    return out


def _xchg_copies(src_refs, land_refs, send_sems, recv_sems, scatter):
    x, y, c = _place()
    me = 4 * x + 2 * y + c
    copies = []
    for a, (src, land) in enumerate(zip(src_refs, land_refs)):
        for k, place, idx in _peers(x, y, c):
            j = a * (N_DEV - 1) + k - 1
            copies.append(pltpu.make_async_remote_copy(
                src_ref=src.at[idx] if scatter[a] else src, dst_ref=land.at[me],
                send_sem=send_sems[j], recv_sem=recv_sems[j], device_id=place, device_id_type=MESH))
    return copies


def _xchg_start(srcs, scatter, name):
    n = len(srcs)
    lands = [lax.empty((N_DEV,) + (s.shape[1:] if sc else s.shape), s.dtype) for s, sc in zip(srcs, scatter)]

    ns = n * (N_DEV - 1)

    def body(*refs):
        sems = refs[2 * n:2 * n + 2 * ns]
        for cp in _xchg_copies(refs[:n], refs[n:2 * n], sems[:ns], sems[ns:], scatter):
            cp.start()
        token = refs[-1]
        token[...] = jnp.zeros_like(token)

    both = list(srcs) + lands
    res = pl.pallas_call(
        body, name=name,
        out_shape=(*[pltpu.SemaphoreType.DMA(())] * (2 * ns),
                   *[pltpu.HBM(a.shape, a.dtype) for a in both], jax.ShapeDtypeStruct((8, LANES), F32)),
        in_specs=[_HBM] * (2 * n),
        out_specs=(*([_SEM] * (2 * ns)), *([_HBM] * (2 * n)), pl.BlockSpec(memory_space=pltpu.VMEM)),
        input_output_aliases={i: 2 * ns + i for i in range(2 * n)},
        compiler_params=pltpu.CompilerParams(has_side_effects=_EFFECT),
    )(*[pltpu.with_memory_space_constraint(a, pltpu.HBM) for a in both])
    return (tuple(res[:2 * ns]), tuple(res[2 * ns:2 * ns + 2 * n])), res[-1]


def _xchg_wait(handle, scatter, after, name):
    sems, thru = handle
    n = len(thru) // 2
    ns = len(sems) // 2

    def body(*refs):
        got = refs[2 * n:2 * n + 2 * ns]
        for cp in _xchg_copies(refs[:n], refs[n:2 * n], got[:ns], got[ns:], scatter):
            cp.wait_send()
            cp.wait_recv()

    outs = pl.pallas_call(
        body, name=name, out_shape=tuple(pltpu.HBM(a.shape, a.dtype) for a in thru),
        in_specs=[_HBM] * (2 * n) + [_SEM] * (2 * ns) + [pl.BlockSpec(memory_space=pl.ANY)],
        out_specs=tuple([_HBM] * (2 * n)), input_output_aliases={i: i for i in range(2 * n)},
        compiler_params=pltpu.CompilerParams(has_side_effects=_EFFECT),
    )(*thru, *sems, after)
    return outs[:n], outs[n:]


def _tie(a, token):
    return a if token is None else a + token[0, 0]


def _rows128(a):
    flat = a.reshape(-1)
    rows = -(-flat.shape[0] // LANES)
    rows = -(-rows // 8) * 8
    return jnp.pad(flat, (0, rows * LANES - flat.shape[0])).reshape(rows, LANES)


def _local_step(x, target, norm_mix_g, w_in_bf, b_forget, gmlp_norm_g, w_spatial, b_spatial, norm_ffn_g, conv_b,
                norm_final_g, rest_fn, send_fn, token=None):
    f = D_FF
    g_mix = norm_mix_g.reshape(1, D_MODEL)
    w_pad = jnp.pad(w_in_bf, ((0, 0), (0, IN_PAD - IN_COLS)))
    bf_pad = jnp.pad(b_forget.reshape(1, N_HEADS), ((0, 0), (0, LANES - N_HEADS)))
    xn, qa, ka, va, ug, zf = _inproj_fwd(x, _tie(g_mix, token), w_pad, bf_pad)
    bias_full = jnp.repeat(b_spatial.reshape(N_GROUPS, CHUNK).T, GROUP_DIM, axis=1)
    w_s = w_spatial.reshape(N_GROUPS, CHUNK, CHUNK)
    gain = gmlp_norm_g.reshape(1, GMLP_WIDTH)
    sg = _gmlp_fwd(ug, gain, w_s, bias_full)
    att, lse, att_bf = _attn_fwd(qa, ka, va)
    w_out_bf, w_up_bf, conv_w, w_down_bf = rest_fn(att_bf)
    g_ffn = norm_ffn_g.reshape(1, D_MODEL)
    h1, hn = _out_proj_fwd(att_bf, sg, w_out_bf, x, g_ffn)
    cw = jnp.pad(conv_w.reshape(3, 2, f).transpose(1, 0, 2), ((0, 0), (0, 5), (0, 0)))
    cb = conv_b.reshape(2, 1, f)
    hu, hc, act = _ffn_up_conv(hn, w_up_bf, cw, cb)
    loss_blk, dh2, dh2_bf, dg_final = _ffn_down_loss(act, w_down_bf, h1, norm_final_g.reshape(1, D_MODEL), target)
    dw_down = _mm(act, dh2_bf, mode="tn", out_dtype=F32, tm=1408, tn=1024, tk=2048, name="ffn_down_dw")
    dact = _mm(dh2_bf, w_down_bf, mode="nt", out_dtype=F32, tm=512, tn=1408, tk=1024, outer="j", name="ffn_down_dx")
    dhu, dcw = _conv_gate_bwd(hc, hu, dact, _tie(cw, send_fn("w_down", dw_down)))
    dhn = _mm(dhu, w_up_bf, mode="nt", out_dtype=F32, tm=1024, tn=1024, tk=2816, a_halves=True, name="ffn_up_dx")
    dw_up = _mm(hn, dhu, mode="tn", out_dtype=F32, tm=1024, tn=1408, tk=2048, b_halves=True, outer="j", name="ffn_up_dw")
    dh1, dh1_bf, dg_ffn = _rms_bwd(h1, _tie(g_ffn, send_fn("w_up", dw_up)), dhn, dh2)
    dmix = _mm(dh1_bf, w_out_bf, mode="nt", out_dtype=F32, tm=512, tn=1024, tk=1024, name="out_proj_dx")
    dw_out = jnp.concatenate(
        [_mm(att_bf, dh1_bf, mode="tn", out_dtype=F32, tm=512, tn=1024, tk=1024, name="out_proj_dw_att"),
         _mm(sg, dh1_bf, mode="tn", out_dtype=F32, tm=512, tn=1024, tk=1024, name="out_proj_dw_sg")], axis=0)
    qb, doa = _attn_prep(att, lse, dmix, qa)
    dq, dk, dv, dcq, dck = _attn_bwd(qb, ka, va, doa)
    wt_s = w_s.transpose(0, 2, 1)
    dug, dw_s, dgain, dbias = _gmlp_bwd(ug, dmix, _tie(gain, send_fn("w_out", dw_out)), w_s, wt_s, bias_full)
    dzf, dbf = _gate_bwd(dcq, dck, zf)
    pieces = (dq, dk, dv, dug, dzf)
    dw_in = _inproj_bwd_dw(xn, pieces)
    grad_x, dg_mix = _inproj_bwd_dx(pieces, w_pad, x, _tie(g_mix, send_fn("w_in", dw_in[:, :IN_COLS])), dh1)
    grads = dict(
        norm_mix_g=dg_mix[0:1, :],
        b_forget=dbf[0:1, :N_HEADS],
        gmlp_norm_g=dgain[0:1, :],
        w_spatial=dw_s,
        b_spatial=dbias[:, ::GROUP_DIM].T,
        norm_ffn_g=dg_ffn[0:1, :],
        conv_w=dcw[:, 0:3, :].transpose(1, 0, 2).reshape(3, 2 * f),
        conv_b=dcw[:, 3, :].reshape(1, 2 * f),
        norm_final_g=dg_final[0, :],
    )
    return loss_blk[0, 0], grad_x, grads


SMALL = ("norm_mix_g", "b_forget", "gmlp_norm_g", "w_spatial", "b_spatial", "norm_ffn_g", "conv_b", "norm_final_g")


def kernel(x, norm_mix_g, w_in, b_forget, gmlp_norm_g, w_spatial, b_spatial, w_out, norm_ffn_g, w_up, conv_w, conv_b, w_down, norm_final_g, loss_target, m_norm_mix_g, m_w_in, m_b_forget, m_gmlp_norm_g, m_w_spatial, m_b_spatial, m_w_out, m_norm_ffn_g, m_w_up, m_conv_w, m_conv_b, m_w_down, m_norm_final_g, v_norm_mix_g, v_w_in, v_b_forget, v_gmlp_norm_g, v_w_spatial, v_b_spatial, v_w_out, v_norm_ffn_g, v_w_up, v_conv_w, v_conv_b, v_w_down, v_norm_final_g):
    weights = dict(norm_mix_g=norm_mix_g, w_in=w_in, b_forget=b_forget, gmlp_norm_g=gmlp_norm_g, w_spatial=w_spatial,
                   b_spatial=b_spatial, w_out=w_out, norm_ffn_g=norm_ffn_g, w_up=w_up, conv_w=conv_w, conv_b=conv_b,
                   w_down=w_down, norm_final_g=norm_final_g)
    m_in = dict(norm_mix_g=m_norm_mix_g, w_in=m_w_in, b_forget=m_b_forget, gmlp_norm_g=m_gmlp_norm_g,
                w_spatial=m_w_spatial, b_spatial=m_b_spatial, w_out=m_w_out, norm_ffn_g=m_norm_ffn_g, w_up=m_w_up,
                conv_w=m_conv_w, conv_b=m_conv_b, w_down=m_w_down, norm_final_g=m_norm_final_g)
    v_in = dict(norm_mix_g=v_norm_mix_g, w_in=v_w_in, b_forget=v_b_forget, gmlp_norm_g=v_gmlp_norm_g,
                w_spatial=v_w_spatial, b_spatial=v_b_spatial, w_out=v_w_out, norm_ffn_g=v_norm_ffn_g, w_up=v_w_up,
                conv_w=v_conv_w, conv_b=v_conv_b, w_down=v_w_down, norm_final_g=v_norm_final_g)
    order = list(weights)
    me = 4 * lax.axis_index("x") + 2 * lax.axis_index("y") + lax.axis_index("c")
    n_in, n_up = w_in.shape[2], w_up.shape[2]
    r_out, r_down = w_out.shape[1], w_down.shape[1]

    def with_mine(landed, mine):
        return lax.dynamic_update_index_in_dim(landed, mine, me, 0)

    up_blk = w_up[0].astype(BF16)
    rows_blk = jnp.concatenate([w_out[0].astype(BF16), w_down[0].astype(BF16)], axis=0)
    taps_blk = jnp.pad(conv_w[0], ((0, 5), (0, 0)))
    (in_all,) = _all_gather([w_in[0].astype(BF16)], "gather_w_in")
    in_all, rest_blocks = lax.optimization_barrier((in_all, [up_blk, rows_blk, taps_blk]))
    rest_handle, token = _xchg_start(rest_blocks, [False] * 3, "gather_rest_start")
    w_in_bf = in_all.transpose(1, 0, 2).reshape(D_MODEL, N_DEV * n_in)

    def rest_fn(after):
        mine, landed = _xchg_wait(rest_handle, [False] * 3, after, "gather_rest_wait")
        up_all, rows_all, taps_all = [with_mine(l, b) for l, b in zip(landed, mine)]
        return (rows_all[:, :r_out, :].reshape(N_DEV * r_out, D_MODEL),
                up_all.transpose(1, 0, 2).reshape(D_MODEL, N_DEV * n_up),
                taps_all[:, :3, :].transpose(1, 0, 2).reshape(3, N_DEV * n_up),
                rows_all[:, r_out:, :].reshape(N_DEV * r_down, D_MODEL))

    sent = {}

    def send_fn(name, grad):
        if name == "w_in":
            parts = grad.reshape(D_MODEL, N_DEV, -1).transpose(1, 0, 2).astype(BF16)
        elif name == "w_up":
            parts = grad.reshape(D_MODEL, N_DEV, -1).transpose(1, 0, 2)
        else:
            parts = grad.reshape(N_DEV, -1, D_MODEL)
        sent[name], tok = _xchg_start([parts], [True], "scatter_" + name + "_start")
        return tok

    loss_local, grad_x, g = _local_step(
        x[0], loss_target[0], norm_mix_g, w_in_bf, b_forget, gmlp_norm_g, w_spatial, b_spatial, norm_ffn_g, conv_b,
        norm_final_g, rest_fn, send_fn, token)
    loss = lax.psum(loss_local, ("x", "y", "c"))

    got = {}
    for name, handle in sent.items():
        (parts,), (landed,) = _xchg_wait(handle, [True], grad_x, "scatter_" + name + "_wait")
        got[name] = with_mine(landed, lax.dynamic_index_in_dim(parts, me, 0, keepdims=False))
    got_in, got_up, got_out, got_down = got["w_in"], got["w_up"], got["w_out"], got["w_down"]

    small_names = SMALL + ("conv_w",)
    packed = [_rows128(g[k]) for k in small_names]
    sizes = [p.shape[0] for p in packed]
    (small_all,) = _all_gather([jnp.concatenate(packed, axis=0)], "gather_small_grads")

    def pack(src):
        return jnp.concatenate([_rows128(src[k]) for k in SMALL], axis=0)

    n_small_rows = sum(sizes[:-1])
    sg_, sd_, sm_, sv_ = _adamw(pack(weights), pack(m_in), pack(v_in), small_all[:, :n_small_rows, :], "adamw_small")

    outs = {}
    off = 0
    for k, rows in zip(SMALL, sizes[:-1]):
        shp = weights[k].shape
        cnt = math.prod(shp)
        outs[k] = tuple(a[off:off + rows].reshape(-1)[:cnt].reshape(shp) for a in (sg_, sd_, sm_, sv_))
        off += rows
    taps_parts = small_all[:, n_small_rows:, :].reshape(N_DEV, -1)[:, :3 * N_DEV * n_up].reshape(N_DEV, 3, N_DEV * n_up)
    taps_mine = lax.dynamic_slice_in_dim(taps_parts, me * n_up, n_up, axis=2)
    taps_mine = jnp.pad(taps_mine, ((0, 0), (0, 5), (0, 0)))

    def pad8(a):
        return jnp.pad(a[0], ((0, 5), (0, 0)))

    res = _adamw(pad8(conv_w), pad8(m_conv_w), pad8(v_conv_w), taps_mine, "adamw_conv_w")
    outs["conv_w"] = tuple(a[:3][None] for a in res)
    for k, got in (("w_in", got_in), ("w_up", got_up), ("w_out", got_out), ("w_down", got_down)):
        res = _adamw(weights[k][0], m_in[k][0], v_in[k][0], got, "adamw_" + k)
        outs[k] = tuple(a[None] for a in res)

    return (loss, grad_x[None], *[outs[k][0] for k in order], *[outs[k][1] for k in order],
            *[outs[k][2] for k in order], *[outs[k][3] for k in order])
```

```python
import functools
import math

import jax
import jax.numpy as jnp
from jax import lax
from jax.experimental import pallas as pl
from jax.experimental.pallas import tpu as pltpu

F32 = jnp.float32
BF16 = jnp.bfloat16

N_DEV = 8
D_MODEL = 1024
ATT_WIDTH = 512
GMLP_WIDTH = 512
HEAD_DIM = 64
N_HEADS = 8
N_PAIRS = 4
N_GROUPS = 8
GROUP_DIM = 64
CHUNK = 128
D_FF = 2816
IN_COLS = 2568
IN_PAD = 2688
QKV = 1536
UG_END = 2560
EPS = 1e-6
LANES = 128

ADAM_LR = 0.001
ADAM_B1 = 0.9
ADAM_B2 = 0.999
ADAM_EPS = 1e-08
ADAM_WD = 0.01
ADAM_STEP = 10

ATT_TQ = 1024
ATT_TK = 1024
FFN_TM, FFN_TN = 512, 1408
CONV_TM, CONV_TN = 256, 1408
VMEM_LIMIT = 56 * 1024 * 1024
MESH = pl.DeviceIdType.MESH


def _cp(sem, vmem=None):
    return pltpu.CompilerParams(dimension_semantics=sem, vmem_limit_bytes=vmem)


def _pick(n, prefs):
    for p in prefs:
        if n % p == 0:
            return p
    return n


def _split3(x):
    hi = x.astype(BF16)
    r1 = x - hi.astype(F32)
    mid = r1.astype(BF16)
    lo = (r1 - mid.astype(F32)).astype(BF16)
    return hi, mid, lo


def _dot3(x, ones_bf):
    hi, mid, lo = _split3(x)
    d = functools.partial(jnp.dot, preferred_element_type=F32)
    return d(hi, ones_bf) + d(mid, ones_bf) + d(lo, ones_bf)


def _dot3l(ones_bf, x):
    hi, mid, lo = _split3(x)
    d = functools.partial(jnp.dot, preferred_element_type=F32)
    return d(ones_bf, hi) + d(ones_bf, mid) + d(ones_bf, lo)


def _gelu(x):
    k = math.sqrt(2.0 / math.pi)
    t = jnp.tanh(k * (x + 0.044715 * (x * x * x)))
    return 0.5 * x * (1.0 + t)


def _gelu_grad(x):
    k = math.sqrt(2.0 / math.pi)
    x2 = x * x
    t = jnp.tanh(k * (x + 0.044715 * (x2 * x)))
    return 0.5 * (1.0 + t) + 0.5 * x * (1.0 - t * t) * (k * (1.0 + 3.0 * 0.044715 * x2))


def _sigmoid(x):
    return 1.0 / (1.0 + jnp.exp(-x))


def _mm(a, b, *, mode, out_dtype, tm, tn, tk, name, res=None, a_halves=False, b_halves=False,
        out_halves=False, outer="i"):
    if mode == "tn":
        K, M = a.shape[-2], a.shape[-1] * (2 if a_halves else 1)
    else:
        M, K = a.shape[-2], a.shape[-1] * (2 if a_halves else 1)
    if mode == "nt":
        N = b.shape[-2]
        assert b.shape[-1] == K
    else:
        N = b.shape[-1] * (2 if b_halves else 1)
    tm, tn, tk = min(tm, M), min(tn, N), min(tk, K)
    assert M % tm == 0 and N % tn == 0 and K % tk == 0, (name, M, N, K, tm, tn, tk)
    nm, nn, nk = M // tm, N // tn, K // tk

    def ij(g0, g1):
        return (g0, g1) if outer == "i" else (g1, g0)

    if mode == "nn":
        dims = (((1,), (0,)), ((), ()))
        if a_halves:
            nkh = nk // 2
            a_spec = pl.BlockSpec((None, tm, tk), lambda g0, g1, k: (k // nkh, ij(g0, g1)[0], k % nkh))
        else:
            a_spec = pl.BlockSpec((tm, tk), lambda g0, g1, k: (ij(g0, g1)[0], k))
        b_spec = pl.BlockSpec((tk, tn), lambda g0, g1, k: (k, ij(g0, g1)[1]))
    elif mode == "nt":
        dims = (((1,), (1,)), ((), ()))
        if a_halves:
            nkh = nk // 2
            a_spec = pl.BlockSpec((None, tm, tk), lambda g0, g1, k: (k // nkh, ij(g0, g1)[0], k % nkh))
        else:
            a_spec = pl.BlockSpec((tm, tk), lambda g0, g1, k: (ij(g0, g1)[0], k))
        b_spec = pl.BlockSpec((tn, tk), lambda g0, g1, k: (ij(g0, g1)[1], k))
    else:
        dims = (((0,), (0,)), ((), ()))
        if a_halves:
            nmh = nm // 2
            a_spec = pl.BlockSpec((None, tk, tm), lambda g0, g1, k: (ij(g0, g1)[0] // nmh, k, ij(g0, g1)[0] % nmh))
        else:
            a_spec = pl.BlockSpec((tk, tm), lambda g0, g1, k: (k, ij(g0, g1)[0]))
        if b_halves:
            nnh = nn // 2
            b_spec = pl.BlockSpec((None, tk, tn), lambda g0, g1, k: (ij(g0, g1)[1] // nnh, k, ij(g0, g1)[1] % nnh))
        else:
            b_spec = pl.BlockSpec((tk, tn), lambda g0, g1, k: (k, ij(g0, g1)[1]))
    if out_halves:
        nnh = nn // 2
        o_spec = pl.BlockSpec((None, tm, tn), lambda g0, g1, k: (ij(g0, g1)[1] // nnh, ij(g0, g1)[0], ij(g0, g1)[1] % nnh))
        o_shape = jax.ShapeDtypeStruct((2, M, N // 2), out_dtype)
    else:
        o_spec = pl.BlockSpec((tm, tn), lambda g0, g1, k: ij(g0, g1))
        o_shape = jax.ShapeDtypeStruct((M, N), out_dtype)
    in_specs = [a_spec, b_spec]
    args = [a, b]
    if res is not None:
        in_specs.append(pl.BlockSpec((tm, tn), lambda g0, g1, k: ij(g0, g1)))
        args.append(res)

    def body(*refs):
        if res is not None:
            a_ref, b_ref, r_ref, o_ref = refs[:4]
        else:
            a_ref, b_ref, o_ref = refs[:3]
            r_ref = None
        part = lax.dot_general(a_ref[...], b_ref[...], dims, preferred_element_type=F32)
        if nk == 1:
            if r_ref is not None:
                part = part + r_ref[...]
            o_ref[...] = part.astype(out_dtype)
            return
        acc_ref = refs[-1]
        k = pl.program_id(2)

        @pl.when(k == 0)
        def _():
            acc_ref[...] = part

        @pl.when(k > 0)
        def _():
            acc_ref[...] += part

        @pl.when(k == nk - 1)
        def _():
            tot = acc_ref[...]
            if r_ref is not None:
                tot = tot + r_ref[...]
            o_ref[...] = tot.astype(out_dtype)

    grid = (nm, nn, nk) if outer == "i" else (nn, nm, nk)
    scratch = [] if nk == 1 else [pltpu.VMEM((tm, tn), F32)]
    return pl.pallas_call(
        body, out_shape=o_shape, grid=grid, in_specs=in_specs, out_specs=o_spec, scratch_shapes=scratch,
        name=name, compiler_params=_cp(("parallel", "parallel", "arbitrary"), VMEM_LIMIT),
    )(*args)


def _aug(lane, terms):
    out = 0.0
    for j, t in enumerate(terms):
        out = jnp.where(lane == HEAD_DIM + j, t, out)
    return out


def _split3f(x):
    hi, mid, lo = _split3(x)
    return [hi.astype(F32), mid.astype(F32), lo.astype(F32)]


def _inproj_fwd(x, g_mix, w_pad, bf_pad):
    S = x.shape[0]
    tm = _pick(S, (512, 256))
    tri = (lax.broadcasted_iota(jnp.int32, (tm, tm), 0) >= lax.broadcasted_iota(jnp.int32, (tm, tm), 1)).astype(BF16)

    def body(x_ref, g_ref, w_ref, bf_ref, tri_ref, xn_ref, qa_ref, ka_ref, va_ref, ug_ref, zf_ref, carry_ref):
        i = pl.program_id(0)

        @pl.when(i == 0)
        def _():
            carry_ref[...] = jnp.zeros_like(carry_ref)

        xf = x_ref[...]
        r = lax.rsqrt(jnp.mean(xf * xf, axis=-1, keepdims=True) + EPS)
        xn = ((xf * r) * g_ref[...]).astype(BF16)
        xn_ref[...] = xn
        proj = jnp.dot(xn, w_ref[...], preferred_element_type=F32)
        ug_ref[...] = proj[:, QKV:UG_END]
        zf = proj[:, UG_END:] + bf_ref[...]
        zf_ref[...] = zf
        lf = jnp.minimum(zf, 0.0) - jnp.log(1.0 + jnp.exp(-jnp.abs(zf)))
        c = _dot3l(tri_ref[...], lf) + carry_ref[0:1, :]
        carry_ref[0:1, :] = c[tm - 1:tm, :]
        c3 = _split3f(c)
        lane = lax.broadcasted_iota(jnp.int32, (tm, LANES), 1)
        ones3 = [1.0, 1.0, 1.0]
        for h in range(N_HEADS):
            p, odd = h // 2, h % 2
            ch = [t[:, h:h + 1] for t in c3]

            def head(base, scale=None, p=p, odd=odd):
                blk = proj[:, base + p * LANES:base + (p + 1) * LANES]
                if scale is not None:
                    blk = blk * scale
                return pltpu.roll(blk, HEAD_DIM, 1) if odd else blk

            cols = slice(h * LANES, (h + 1) * LANES)
            qa_ref[:, cols] = jnp.where(lane < HEAD_DIM, head(0, HEAD_DIM ** -0.5), _aug(lane, ch + ones3)).astype(BF16)
            ka_ref[:, cols] = jnp.where(lane < HEAD_DIM, head(ATT_WIDTH),
                                        _aug(lane, ones3 + [-t for t in ch] + ones3)).astype(BF16)
            va_ref[:, cols] = jnp.where(lane < HEAD_DIM, head(2 * ATT_WIDTH), _aug(lane, ones3)).astype(BF16)

    wide = N_HEADS * LANES
    return pl.pallas_call(
        body,
        out_shape=(jax.ShapeDtypeStruct((S, D_MODEL), BF16), jax.ShapeDtypeStruct((S, wide), BF16),
                   jax.ShapeDtypeStruct((S, wide), BF16), jax.ShapeDtypeStruct((S, wide), BF16),
                   jax.ShapeDtypeStruct((S, 2 * GMLP_WIDTH), F32), jax.ShapeDtypeStruct((S, LANES), F32)),
        grid=(S // tm,),
        in_specs=[pl.BlockSpec((tm, D_MODEL), lambda i: (i, 0)), pl.BlockSpec((1, D_MODEL), lambda i: (0, 0)),
                  pl.BlockSpec((D_MODEL, IN_PAD), lambda i: (0, 0)), pl.BlockSpec((1, LANES), lambda i: (0, 0)),
                  pl.BlockSpec((tm, tm), lambda i: (0, 0))],
        out_specs=(pl.BlockSpec((tm, D_MODEL), lambda i: (i, 0)), pl.BlockSpec((tm, wide), lambda i: (i, 0)),
                   pl.BlockSpec((tm, wide), lambda i: (i, 0)), pl.BlockSpec((tm, wide), lambda i: (i, 0)),
                   pl.BlockSpec((tm, 2 * GMLP_WIDTH), lambda i: (i, 0)), pl.BlockSpec((tm, LANES), lambda i: (i, 0))),
        scratch_shapes=[pltpu.VMEM((8, LANES), F32)],
        name="inproj_fwd", compiler_params=_cp(("arbitrary",), VMEM_LIMIT),
    )(x, g_mix, w_pad, bf_pad, tri)


def _group_ones():
    r = lax.broadcasted_iota(jnp.int32, (GMLP_WIDTH, GMLP_WIDTH), 0) // GROUP_DIM
    c = lax.broadcasted_iota(jnp.int32, (GMLP_WIDTH, GMLP_WIDTH), 1) // GROUP_DIM
    return (r == c).astype(BF16)


def _gmlp_mixed(vn_bf, w_ref, bias, n_chunks):
    lane = lax.broadcasted_iota(jnp.int32, (CHUNK, LANES), 1)
    row = lax.broadcasted_iota(jnp.int32, (CHUNK, CHUNK), 0)
    col = lax.broadcasted_iota(jnp.int32, (CHUNK, CHUNK), 1)
    ws = [jnp.where(row >= col, w_ref[g], 0.0).astype(BF16) for g in range(N_GROUPS)]
    rows = []
    for ci in range(n_chunks):
        cols = []
        for pp in range(N_GROUPS // 2):
            v = vn_bf[ci * CHUNK:(ci + 1) * CHUNK, pp * LANES:(pp + 1) * LANES]
            v_lo = jnp.where(lane < GROUP_DIM, v, jnp.zeros_like(v))
            v_hi = jnp.where(lane >= GROUP_DIM, v, jnp.zeros_like(v))
            m = (jnp.dot(ws[2 * pp], v_lo, preferred_element_type=F32)
                 + jnp.dot(ws[2 * pp + 1], v_hi, preferred_element_type=F32))
            cols.append(m + bias[:, pp * LANES:(pp + 1) * LANES])
        rows.append(jnp.concatenate(cols, axis=1))
    return jnp.concatenate(rows, axis=0)


def _gmlp_fwd(ug, gain, w_s, bias_full):
    S = ug.shape[0]
    tm = _pick(S, (512, 256, 128))
    ones = _group_ones()

    def body(ug_ref, gain_ref, w_ref, bias_ref, ones_ref, sg_ref):
        u = _gelu(ug_ref[:, :GMLP_WIDTH])
        vr = _gelu(ug_ref[:, GMLP_WIDTH:])
        ms = _dot3(vr * vr, ones_ref[...]) * (1.0 / GROUP_DIM)
        vn = ((vr * lax.rsqrt(ms + EPS)) * gain_ref[...]).astype(BF16)
        mixed = _gmlp_mixed(vn, w_ref, bias_ref[...], tm // CHUNK)
        sg_ref[...] = (u * mixed).astype(BF16)

    return pl.pallas_call(
        body, out_shape=jax.ShapeDtypeStruct((S, GMLP_WIDTH), BF16), grid=(S // tm,),
        in_specs=[pl.BlockSpec((tm, 2 * GMLP_WIDTH), lambda i: (i, 0)), pl.BlockSpec((1, GMLP_WIDTH), lambda i: (0, 0)),
                  pl.BlockSpec((N_GROUPS, CHUNK, CHUNK), lambda i: (0, 0, 0)),
                  pl.BlockSpec((CHUNK, GMLP_WIDTH), lambda i: (0, 0)),
                  pl.BlockSpec((GMLP_WIDTH, GMLP_WIDTH), lambda i: (0, 0))],
        out_specs=pl.BlockSpec((tm, GMLP_WIDTH), lambda i: (i, 0)),
        name="gmlp_fwd", compiler_params=_cp(("parallel",), VMEM_LIMIT),
    )(ug, gain, w_s, bias_full, ones)


_NT = (((1,), (1,)), ((), ()))
_TN = (((0,), (0,)), ((), ()))


def _attn_fwd(qa, ka, va):
    S = qa.shape[0]
    tq = _pick(S, (ATT_TQ, 256))
    tk = min(ATT_TK, tq)
    nq = S // tq
    per_q = tq // tk

    def body(q_ref, k_ref, v_ref, o_ref, lse_ref, ob_ref):
        qi = pl.program_id(1)
        lane = lax.broadcasted_iota(jnp.int32, (tq, LANES), 1)
        rid = lax.broadcasted_iota(jnp.int32, (tq, tk), 0)
        cid = lax.broadcasted_iota(jnp.int32, (tq, tk), 1)
        qs = [q_ref[:, :LANES], q_ref[:, LANES:]]

        def update(kb, h, m, acc, diag):
            ks = pl.multiple_of(kb * tk, tk)
            cols = slice(h * LANES, (h + 1) * LANES)
            s = lax.dot_general(qs[h], k_ref[pl.ds(ks, tk), cols], _NT, preferred_element_type=F32)
            if diag is not None:
                s = jnp.where(rid >= cid + diag * tk, s, -jnp.inf)
            m_new = jnp.maximum(m, jnp.max(s, axis=-1, keepdims=True))
            p = jnp.exp(s - m_new).astype(BF16)
            acc = jnp.exp(m - m_new) * acc + jnp.dot(p, v_ref[pl.ds(ks, tk), cols], preferred_element_type=F32)
            return m_new, acc

        def step(kb, carry):
            return tuple(update(kb, h, *carry[h], None) for h in range(2))

        one = (jnp.full((tq, 1), -jnp.inf, F32), jnp.zeros((tq, LANES), F32))
        carry = lax.fori_loop(0, qi * per_q, step, (one, one))
        outs, lses = [], []
        for h in range(2):
            m, acc = carry[h]
            for d in range(per_q):
                m, acc = update(qi * per_q + d, h, m, acc, d)
            l = acc[:, HEAD_DIM:HEAD_DIM + 1]
            outs.append(acc / l)
            lses.append(m + jnp.log(l))
        o = jnp.where(lane < HEAD_DIM, outs[0], pltpu.roll(outs[1], HEAD_DIM, 1))
        o_ref[...] = o
        ob_ref[...] = o.astype(BF16)
        lse_ref[...] = jnp.where(lane < HEAD_DIM, lses[0], lses[1])

    return pl.pallas_call(
        body,
        out_shape=(jax.ShapeDtypeStruct((S, ATT_WIDTH), F32), jax.ShapeDtypeStruct((S, ATT_WIDTH), F32),
                   jax.ShapeDtypeStruct((S, ATT_WIDTH), BF16)),
        grid=(N_PAIRS, nq),
        in_specs=[pl.BlockSpec((tq, 2 * LANES), lambda p, i: (i, p)),
                  pl.BlockSpec((S, 2 * LANES), lambda p, i: (0, p)),
                  pl.BlockSpec((S, 2 * LANES), lambda p, i: (0, p))],
        out_specs=(pl.BlockSpec((tq, LANES), lambda p, i: (i, p)), pl.BlockSpec((tq, LANES), lambda p, i: (i, p)),
                   pl.BlockSpec((tq, LANES), lambda p, i: (i, p))),
        name="attn_fwd", compiler_params=_cp(("parallel", "parallel"), VMEM_LIMIT),
    )(qa, ka, va)


def _rms_fwd(h, g):
    S = h.shape[0]
    tm = _pick(S, (512, 256))

    def body(h_ref, g_ref, o_ref):
        hf = h_ref[...]
        r = lax.rsqrt(jnp.mean(hf * hf, axis=-1, keepdims=True) + EPS)
        o_ref[...] = ((hf * r) * g_ref[...]).astype(BF16)

    return pl.pallas_call(
        body, out_shape=jax.ShapeDtypeStruct(h.shape, BF16), grid=(S // tm,),
        in_specs=[pl.BlockSpec((tm, D_MODEL), lambda i: (i, 0)), pl.BlockSpec((1, D_MODEL), lambda i: (0, 0))],
        out_specs=pl.BlockSpec((tm, D_MODEL), lambda i: (i, 0)),
        name="rms_fwd", compiler_params=_cp(("parallel",)),
    )(h, g)


def _shift_rows(x, prev, n):
    rid = lax.broadcasted_iota(jnp.int32, x.shape, 0)
    y = pltpu.roll(x, n, 0)
    if n == 1:
        return jnp.where(rid == 0, prev[7:8, :], y)
    return jnp.where(rid == 0, prev[6:7, :], jnp.where(rid == 1, prev[7:8, :], y))


def _shift_rows_up(x, nxt, n):
    rows = x.shape[0]
    rid = lax.broadcasted_iota(jnp.int32, x.shape, 0)
    y = pltpu.roll(x, rows - n, 0)
    if n == 1:
        return jnp.where(rid == rows - 1, nxt[0:1, :], y)
    return jnp.where(rid == rows - 2, nxt[0:1, :], jnp.where(rid == rows - 1, nxt[1:2, :], y))


def _conv3(cur, prev, w, b):
    return (w[0:1, :] * _shift_rows(cur, prev, 2) + w[1:2, :] * _shift_rows(cur, prev, 1)
            + w[2:3, :] * cur + b)


def _conv_act_fwd(hu, cw, cb):
    _, S, F = hu.shape
    tm = _pick(S, (512, 256))
    tn = _pick(F, (256, 128))
    r8 = tm // 8

    def body(cur_ref, prev_ref, w_ref, b_ref, o_ref):
        i = pl.program_id(1)
        halves = []
        for h in range(2):
            prev = jnp.where(i > 0, prev_ref[h], 0.0)
            halves.append(_conv3(cur_ref[h], prev, w_ref[h], b_ref[h]))
        a, g = halves
        o_ref[...] = (g * _sigmoid(g) * a).astype(BF16)

    return pl.pallas_call(
        body, out_shape=jax.ShapeDtypeStruct((S, F), BF16), grid=(F // tn, S // tm),
        in_specs=[pl.BlockSpec((2, tm, tn), lambda j, i: (0, i, j)),
                  pl.BlockSpec((2, 8, tn), lambda j, i: (0, jnp.maximum(i * r8 - 1, 0), j)),
                  pl.BlockSpec((2, 8, tn), lambda j, i: (0, 0, j)),
                  pl.BlockSpec((2, 1, tn), lambda j, i: (0, 0, j))],
        out_specs=pl.BlockSpec((tm, tn), lambda j, i: (i, j)),
        name="conv_act_fwd", compiler_params=_cp(("parallel", "parallel"), VMEM_LIMIT),
    )(hu, hu, cw, cb)


def _ffn_up_conv(hn, w_up_bf, cw, cb):
    S = hn.shape[0]
    F = D_FF
    tm = _pick(S, (FFN_TM, 256))
    tn = _pick(F, (FFN_TN, 256, 128))
    nj = F // tn

    def body(hn_ref, wa_ref, wg_ref, cw_ref, cb_ref, hu_ref, hc_ref, act_ref, tail_ref):
        i = pl.program_id(1)

        @pl.when(i == 0)
        def _():
            tail_ref[...] = jnp.zeros_like(tail_ref)

        hn_v = hn_ref[...]
        halves = []
        for h, w_ref in enumerate((wa_ref, wg_ref)):
            hu = jnp.dot(hn_v, w_ref[...], preferred_element_type=F32)
            hu_ref[h] = hu
            hc = _conv3(hu, tail_ref[h], cw_ref[h], cb_ref[h])
            hc_ref[h] = hc
            halves.append(hc)
            tail_ref[h] = hu[tm - 8:, :]
        a, g = halves
        act_ref[...] = (g * _sigmoid(g) * a).astype(BF16)

    both = pl.BlockSpec((2, tm, tn), lambda j, i: (0, i, j))
    return pl.pallas_call(
        body, out_shape=(jax.ShapeDtypeStruct((2, S, F), F32), jax.ShapeDtypeStruct((2, S, F), F32),
                         jax.ShapeDtypeStruct((S, F), BF16)),
        grid=(nj, S // tm),
        in_specs=[pl.BlockSpec((tm, D_MODEL), lambda j, i: (i, 0)),
                  pl.BlockSpec((D_MODEL, tn), lambda j, i: (0, j)),
                  pl.BlockSpec((D_MODEL, tn), lambda j, i: (0, nj + j)),
                  pl.BlockSpec((2, 8, tn), lambda j, i: (0, 0, j)),
                  pl.BlockSpec((2, 1, tn), lambda j, i: (0, 0, j))],
        out_specs=(both, both, pl.BlockSpec((tm, tn), lambda j, i: (i, j))),
        scratch_shapes=[pltpu.VMEM((2, 8, tn), F32)],
        name="ffn_up_conv", compiler_params=_cp(("parallel", "arbitrary"), VMEM_LIMIT),
    )(hn, w_up_bf, w_up_bf, cw, cb)


def _ffn_down_loss(act, w_down_bf, h1, g_final, target):
    S = h1.shape[0]
    tm = _pick(S, (512, 256))

    def body(a_ref, w_ref, h1_ref, g_ref, t_ref, loss_ref, dh_ref, dhb_ref, dg_ref):
        i = pl.program_id(0)

        @pl.when(i == 0)
        def _():
            loss_ref[...] = jnp.zeros_like(loss_ref)
            dg_ref[...] = jnp.zeros_like(dg_ref)

        hf = h1_ref[...] + jnp.dot(a_ref[...], w_ref[...], preferred_element_type=F32)
        g = g_ref[...]
        r = lax.rsqrt(jnp.mean(hf * hf, axis=-1, keepdims=True) + EPS)
        hhat = hf * r
        err = hhat * g - t_ref[...]
        loss_ref[...] += 0.5 * jnp.sum(jnp.mean(err * err, axis=-1, keepdims=True))
        dy = err * (1.0 / D_MODEL)
        dg_ref[0:1, :] += jnp.sum(dy * hhat, axis=0, keepdims=True)
        dhat = dy * g
        dh = r * (dhat - hhat * jnp.mean(dhat * hhat, axis=-1, keepdims=True))
        dh_ref[...] = dh
        dhb_ref[...] = dh.astype(BF16)

    row = pl.BlockSpec((tm, D_MODEL), lambda i: (i, 0))
    return pl.pallas_call(
        body,
        out_shape=(jax.ShapeDtypeStruct((8, LANES), F32), jax.ShapeDtypeStruct((S, D_MODEL), F32),
                   jax.ShapeDtypeStruct((S, D_MODEL), BF16), jax.ShapeDtypeStruct((8, D_MODEL), F32)),
        grid=(S // tm,),
        in_specs=[pl.BlockSpec((tm, D_FF), lambda i: (i, 0)), pl.BlockSpec((D_FF, D_MODEL), lambda i: (0, 0)), row,
                  pl.BlockSpec((1, D_MODEL), lambda i: (0, 0)), row],
        out_specs=(pl.BlockSpec((8, LANES), lambda i: (0, 0)), row, row, pl.BlockSpec((8, D_MODEL), lambda i: (0, 0))),
        name="ffn_down_loss", compiler_params=_cp(("arbitrary",), VMEM_LIMIT),
    )(act, w_down_bf, h1, g_final, target)


def _loss_head(h2, g_final, target):
    S = h2.shape[0]
    tm = _pick(S, (512, 256))

    def body(h_ref, g_ref, t_ref, loss_ref, dh_ref, dhb_ref, dg_ref):
        i = pl.program_id(0)

        @pl.when(i == 0)
        def _():
            loss_ref[...] = jnp.zeros_like(loss_ref)
            dg_ref[...] = jnp.zeros_like(dg_ref)

        hf = h_ref[...]
        g = g_ref[...]
        r = lax.rsqrt(jnp.mean(hf * hf, axis=-1, keepdims=True) + EPS)
        hhat = hf * r
        err = hhat * g - t_ref[...]
        loss_ref[...] += 0.5 * jnp.sum(jnp.mean(err * err, axis=-1, keepdims=True))
        dy = err * (1.0 / D_MODEL)
        dg_ref[0:1, :] += jnp.sum(dy * hhat, axis=0, keepdims=True)
        dhat = dy * g
        dh = r * (dhat - hhat * jnp.mean(dhat * hhat, axis=-1, keepdims=True))
        dh_ref[...] = dh
        dhb_ref[...] = dh.astype(BF16)

    return pl.pallas_call(
        body,
        out_shape=(jax.ShapeDtypeStruct((8, LANES), F32), jax.ShapeDtypeStruct((S, D_MODEL), F32),
                   jax.ShapeDtypeStruct((S, D_MODEL), BF16), jax.ShapeDtypeStruct((8, D_MODEL), F32)),
        grid=(S // tm,),
        in_specs=[pl.BlockSpec((tm, D_MODEL), lambda i: (i, 0)), pl.BlockSpec((1, D_MODEL), lambda i: (0, 0)),
                  pl.BlockSpec((tm, D_MODEL), lambda i: (i, 0))],
        out_specs=(pl.BlockSpec((8, LANES), lambda i: (0, 0)), pl.BlockSpec((tm, D_MODEL), lambda i: (i, 0)),
                   pl.BlockSpec((tm, D_MODEL), lambda i: (i, 0)), pl.BlockSpec((8, D_MODEL), lambda i: (0, 0))),
        name="loss_head", compiler_params=_cp(("arbitrary",), VMEM_LIMIT),
    )(h2, g_final, target)


def _rms_bwd(h, g, dy, res):
    S = h.shape[0]
    tm = _pick(S, (512, 256))

    def body(h_ref, g_ref, dy_ref, r_ref, dh_ref, dhb_ref, dg_ref):
        i = pl.program_id(0)

        @pl.when(i == 0)
        def _():
            dg_ref[...] = jnp.zeros_like(dg_ref)

        hf = h_ref[...]
        dyv = dy_ref[...]
        r = lax.rsqrt(jnp.mean(hf * hf, axis=-1, keepdims=True) + EPS)
        hhat = hf * r
        dg_ref[0:1, :] += jnp.sum(dyv * hhat, axis=0, keepdims=True)
        dhat = dyv * g_ref[...]
        dh = r_ref[...] + r * (dhat - hhat * jnp.mean(dhat * hhat, axis=-1, keepdims=True))
        dh_ref[...] = dh
        dhb_ref[...] = dh.astype(BF16)

    return pl.pallas_call(
        body,
        out_shape=(jax.ShapeDtypeStruct((S, D_MODEL), F32), jax.ShapeDtypeStruct((S, D_MODEL), BF16),
                   jax.ShapeDtypeStruct((8, D_MODEL), F32)),
        grid=(S // tm,),
        in_specs=[pl.BlockSpec((tm, D_MODEL), lambda i: (i, 0)), pl.BlockSpec((1, D_MODEL), lambda i: (0, 0)),
                  pl.BlockSpec((tm, D_MODEL), lambda i: (i, 0)), pl.BlockSpec((tm, D_MODEL), lambda i: (i, 0))],
        out_specs=(pl.BlockSpec((tm, D_MODEL), lambda i: (i, 0)), pl.BlockSpec((tm, D_MODEL), lambda i: (i, 0)),
                   pl.BlockSpec((8, D_MODEL), lambda i: (0, 0))),
        name="rms_bwd", compiler_params=_cp(("arbitrary",), VMEM_LIMIT),
    )(h, g, dy, res)


def _conv_act_bwd(hu, dh2_bf, w_down_bf, cw, cb):
    _, S, F = hu.shape
    tm = _pick(S, (256,))
    tn = _pick(F, (256, 128))
    r8 = tm // 8
    r16 = tm // 16
    n_i = S // tm
    last8 = S // 8 - 1
    last16 = S // 16 - 1

    def body(cur_ref, prev_ref, next_ref, dy_ref, dyn_ref, wd_ref, w_ref, b_ref, dhu_ref, dcw_ref):
        i = pl.program_id(1)
        wd = wd_ref[...]
        dact = lax.dot_general(dy_ref[...], wd, _NT, preferred_element_type=F32)
        dact_n = lax.dot_general(dyn_ref[...], wd, _NT, preferred_element_type=F32)[0:8, :]

        @pl.when(i == 0)
        def _():
            dcw_ref[...] = jnp.zeros_like(dcw_ref)

        rid8 = lax.broadcasted_iota(jnp.int32, (8, tn), 0)
        cur = [cur_ref[0], cur_ref[1]]
        prev = [jnp.where(i > 0, prev_ref[h], 0.0) for h in range(2)]
        nxt = [next_ref[0], next_ref[1]]
        w = [w_ref[0], w_ref[1]]

        def gate_grads(a, g, d):
            sg = _sigmoid(g)
            return d * (g * sg), d * a * (sg * (1.0 + g * (1.0 - sg)))

        taps = [(_shift_rows(cur[h], prev[h], 2), _shift_rows(cur[h], prev[h], 1), cur[h]) for h in range(2)]
        a, g = [w[h][0:1, :] * taps[h][0] + w[h][1:2, :] * taps[h][1] + w[h][2:3, :] * taps[h][2] + b_ref[h]
                for h in range(2)]
        dhc = gate_grads(a, g, dact)
        a_n = _conv3(nxt[0], cur[0][tm - 8:, :], w[0], b_ref[0])
        g_n = _conv3(nxt[1], cur[1][tm - 8:, :], w[1], b_ref[1])
        dhc_n = gate_grads(a_n, g_n, dact_n)
        for h in range(2):
            d = dhc[h]
            dn = jnp.where(i < n_i - 1, dhc_n[h], 0.0)
            dhu = (w[h][2:3, :] * d + w[h][1:2, :] * _shift_rows_up(d, dn, 1)
                   + w[h][0:1, :] * _shift_rows_up(d, dn, 2))
            dhu_ref[h] = dhu.astype(BF16)
            t0, t1, t2 = [jnp.sum(d * t, axis=0, keepdims=True) for t in taps[h]]
            t3 = jnp.sum(d, axis=0, keepdims=True)
            dcw_ref[h] += jnp.where(rid8 == 0, t0, jnp.where(rid8 == 1, t1, jnp.where(rid8 == 2, t2, jnp.where(rid8 == 3, t3, 0.0))))

    return pl.pallas_call(
        body,
        out_shape=(jax.ShapeDtypeStruct((2, S, F), BF16), jax.ShapeDtypeStruct((2, 8, F), F32)),
        grid=(F // tn, n_i),
        in_specs=[pl.BlockSpec((2, tm, tn), lambda j, i: (0, i, j)),
                  pl.BlockSpec((2, 8, tn), lambda j, i: (0, jnp.maximum(i * r8 - 1, 0), j)),
                  pl.BlockSpec((2, 8, tn), lambda j, i: (0, jnp.minimum((i + 1) * r8, last8), j)),
                  pl.BlockSpec((tm, D_MODEL), lambda j, i: (i, 0)),
                  pl.BlockSpec((16, D_MODEL), lambda j, i: (jnp.minimum((i + 1) * r16, last16), 0)),
                  pl.BlockSpec((tn, D_MODEL), lambda j, i: (j, 0)),
                  pl.BlockSpec((2, 8, tn), lambda j, i: (0, 0, j)),
                  pl.BlockSpec((2, 1, tn), lambda j, i: (0, 0, j))],
        out_specs=(pl.BlockSpec((2, tm, tn), lambda j, i: (0, i, j)), pl.BlockSpec((2, 8, tn), lambda j, i: (0, 0, j))),
        name="conv_act_bwd", compiler_params=_cp(("parallel", "arbitrary"), VMEM_LIMIT),
    )(hu, hu, hu, dh2_bf, dh2_bf, w_down_bf, cw, cb)


def _ffn_up_dx_rms(dhu, w_up_bf, h1, g_ffn, dh2):
    _, S, F = dhu.shape
    tm = _pick(S, (512, 256))

    def body(a_ref, b_ref, h_ref, g_ref, r_ref, dh_ref, dhb_ref, dg_ref, acc_ref):
        i, k = pl.program_id(0), pl.program_id(1)

        @pl.when((i == 0) & (k == 0))
        def _():
            dg_ref[...] = jnp.zeros_like(dg_ref)

        part = lax.dot_general(a_ref[...], b_ref[...], _NT, preferred_element_type=F32)

        @pl.when(k == 0)
        def _():
            acc_ref[...] = part

        @pl.when(k == 1)
        def _():
            dyv = acc_ref[...] + part
            hf = h_ref[...]
            r = lax.rsqrt(jnp.mean(hf * hf, axis=-1, keepdims=True) + EPS)
            hhat = hf * r
            dg_ref[0:1, :] += jnp.sum(dyv * hhat, axis=0, keepdims=True)
            dhat = dyv * g_ref[...]
            dh = r_ref[...] + r * (dhat - hhat * jnp.mean(dhat * hhat, axis=-1, keepdims=True))
            dh_ref[...] = dh
            dhb_ref[...] = dh.astype(BF16)

    row = pl.BlockSpec((tm, D_MODEL), lambda i, k: (i, 0))
    return pl.pallas_call(
        body,
        out_shape=(jax.ShapeDtypeStruct((S, D_MODEL), F32), jax.ShapeDtypeStruct((S, D_MODEL), BF16),
                   jax.ShapeDtypeStruct((8, D_MODEL), F32)),
        grid=(S // tm, 2),
        in_specs=[pl.BlockSpec((None, tm, F), lambda i, k: (k, i, 0)), pl.BlockSpec((D_MODEL, F), lambda i, k: (0, k)),
                  row, pl.BlockSpec((1, D_MODEL), lambda i, k: (0, 0)), row],
        out_specs=(row, row, pl.BlockSpec((8, D_MODEL), lambda i, k: (0, 0))),
        scratch_shapes=[pltpu.VMEM((tm, D_MODEL), F32)],
        name="ffn_up_dx_rms", compiler_params=_cp(("arbitrary", "arbitrary"), VMEM_LIMIT),
    )(dhu, w_up_bf, h1, g_ffn, dh2)


def _conv_gate_bwd(hc, hu, dact, cw):
    _, S, F = hu.shape
    tm = _pick(S, (CONV_TM, 128))
    tn = _pick(F, (CONV_TN, 256, 128))
    r8 = tm // 8
    n_i = S // tm
    last8 = S // 8 - 1

    def body(hc_ref, hcn_ref, hu_ref, da_ref, dan_ref, w_ref, dhu_ref, dcw_ref):
        i = pl.program_id(1)

        @pl.when(i == 0)
        def _():
            dcw_ref[...] = jnp.zeros_like(dcw_ref)

        rid8 = lax.broadcasted_iota(jnp.int32, (8, tn), 0)

        def gate_grads(a, g, d):
            sg = _sigmoid(g)
            return d * (g * sg), d * a * (sg * (1.0 + g * (1.0 - sg)))

        dhc = gate_grads(hc_ref[0], hc_ref[1], da_ref[...])
        dhc_n = gate_grads(hcn_ref[0], hcn_ref[1], dan_ref[...])
        for h in range(2):
            w = w_ref[h]
            d = dhc[h]
            dn = jnp.where(i < n_i - 1, dhc_n[h], 0.0)
            u1 = _shift_rows_up(d, dn, 1)
            u2 = _shift_rows_up(d, dn, 2)
            dhu_ref[h] = (w[2:3, :] * d + w[1:2, :] * u1 + w[0:1, :] * u2).astype(BF16)
            x = hu_ref[h]
            t0, t1, t2, t3 = [jnp.sum(t, axis=0, keepdims=True) for t in (u2 * x, u1 * x, d * x, d)]
            dcw_ref[h] += jnp.where(rid8 == 0, t0, jnp.where(rid8 == 1, t1, jnp.where(rid8 == 2, t2, jnp.where(rid8 == 3, t3, 0.0))))

    cur = pl.BlockSpec((2, tm, tn), lambda j, i: (0, i, j))
    return pl.pallas_call(
        body,
        out_shape=(jax.ShapeDtypeStruct((2, S, F), BF16), jax.ShapeDtypeStruct((2, 8, F), F32)),
        grid=(F // tn, n_i),
        in_specs=[cur, pl.BlockSpec((2, 8, tn), lambda j, i: (0, jnp.minimum((i + 1) * r8, last8), j)), cur,
                  pl.BlockSpec((tm, tn), lambda j, i: (i, j)),
                  pl.BlockSpec((8, tn), lambda j, i: (jnp.minimum((i + 1) * r8, last8), j)),
                  pl.BlockSpec((2, 8, tn), lambda j, i: (0, 0, j))],
        out_specs=(cur, pl.BlockSpec((2, 8, tn), lambda j, i: (0, 0, j))),
        name="conv_gate_bwd", compiler_params=_cp(("parallel", "arbitrary"), VMEM_LIMIT),
    )(hc, hc, hu, dact, dact, cw)


def _attn_prep(att, lse, dmix, qa):
    S = att.shape[0]
    tm = _pick(S, (256,))

    def body(o_ref, lse_ref, do_ref, q_ref, qb_ref, doa_ref):
        lane = lax.broadcasted_iota(jnp.int32, (tm, LANES), 1)
        for p in range(N_PAIRS):
            pc = slice(p * LANES, (p + 1) * LANES)
            do = do_ref[:, pc]
            prod = o_ref[:, pc] * do
            for hh in range(2):
                sel = (lane >= HEAD_DIM) if hh else (lane < HEAD_DIM)
                delta = jnp.sum(jnp.where(sel, prod, 0.0), axis=-1, keepdims=True)
                dod = pltpu.roll(do, HEAD_DIM, 1) if hh else do
                cols = slice((2 * p + hh) * LANES, (2 * p + hh + 1) * LANES)
                doa_ref[:, cols] = jnp.where(lane < HEAD_DIM, dod, _aug(lane, _split3f(-delta))).astype(BF16)
                lcol = p * LANES + hh * HEAD_DIM
                l3 = _split3f(-lse_ref[:, lcol:lcol + 1])
                augl = _aug(lane, [0.0] * 6 + l3).astype(BF16)
                qb_ref[:, cols] = jnp.where((lane >= HEAD_DIM + 6) & (lane < HEAD_DIM + 9), augl, q_ref[:, cols])

    half = pl.BlockSpec((tm, ATT_WIDTH), lambda i: (i, 0))
    wide = pl.BlockSpec((tm, N_HEADS * LANES), lambda i: (i, 0))
    return pl.pallas_call(
        body,
        out_shape=(jax.ShapeDtypeStruct(qa.shape, BF16), jax.ShapeDtypeStruct(qa.shape, BF16)),
        grid=(S // tm,), in_specs=[half, half, half, wide], out_specs=(wide, wide),
        name="attn_prep", compiler_params=_cp(("parallel",), VMEM_LIMIT),
    )(att, lse, dmix, qa)


def _attn_bwd(qb, ka, va, doa):
    S = qb.shape[0]
    tk = _pick(S, (512, 256))
    tq = tk
    nq = S // tq

    def pair(a, scale=None):
        lane = lax.broadcasted_iota(jnp.int32, (a.shape[0], LANES), 1)
        out = jnp.where(lane < HEAD_DIM, a[:, :LANES], pltpu.roll(a[:, LANES:], HEAD_DIM, 1))
        return out if scale is None else out * scale

    def lanes01(a, col, sign):
        lane = lax.broadcasted_iota(jnp.int32, (a.shape[0], LANES), 1)
        return jnp.where(lane == 0, sign * a[:, col:col + 1], jnp.where(lane == 1, sign * a[:, LANES + col:LANES + col + 1], 0.0))

    def body(q_ref, do_ref, k_ref, v_ref, dqc_ref, dkc_ref, dvc_ref, dcq_ref, dck_ref, dq_ref, dka_ref, dva_ref):
        kb = pl.program_id(1)

        @pl.when(kb == 0)
        def _():
            dq_ref[...] = jnp.zeros_like(dq_ref)

        dka_ref[...] = jnp.zeros_like(dka_ref)
        dva_ref[...] = jnp.zeros_like(dva_ref)
        rid = lax.broadcasted_iota(jnp.int32, (tk, tq), 0)
        cid = lax.broadcasted_iota(jnp.int32, (tk, tq), 1)

        def tile(qi, masked):
            qs = pl.multiple_of(qi * tq, tq)
            for h in range(2):
                cols = slice(h * LANES, (h + 1) * LANES)
                qblk = q_ref[pl.ds(qs, tq), cols]
                doblk = do_ref[pl.ds(qs, tq), cols]
                kh = k_ref[:, cols]
                p = jnp.exp(lax.dot_general(kh, qblk, _NT, preferred_element_type=F32))
                if masked:
                    p = jnp.where(cid >= rid, p, 0.0)
                ds = (p * lax.dot_general(v_ref[:, cols], doblk, _NT, preferred_element_type=F32)).astype(BF16)
                dva_ref[:, cols] += jnp.dot(p.astype(BF16), doblk, preferred_element_type=F32)
                dka_ref[:, cols] += jnp.dot(ds, qblk, preferred_element_type=F32)
                dq_ref[pl.ds(qs, tq), cols] += lax.dot_general(ds, kh, _TN, preferred_element_type=F32)

        tile(kb, True)

        def step(qi, carry):
            tile(qi, False)
            return carry

        lax.fori_loop(kb + 1, nq, step, 0)
        dka = dka_ref[...]
        dkc_ref[...] = pair(dka).astype(BF16)
        dvc_ref[...] = pair(dva_ref[...]).astype(BF16)
        dck_ref[...] = lanes01(dka, HEAD_DIM + 3, -1.0)

        @pl.when(kb == nq - 1)
        def _():
            dqa = dq_ref[...]
            dqc_ref[...] = pair(dqa, HEAD_DIM ** -0.5).astype(BF16)
            dcq_ref[...] = lanes01(dqa, HEAD_DIM, 1.0)

    wide = 2 * LANES
    half = jax.ShapeDtypeStruct((S, ATT_WIDTH), BF16)
    slabs = jax.ShapeDtypeStruct((N_PAIRS, S, LANES), F32)
    return pl.pallas_call(
        body,
        out_shape=(half, half, half, slabs, slabs),
        grid=(N_PAIRS, nq),
        in_specs=[pl.BlockSpec((S, wide), lambda p, j: (0, p)), pl.BlockSpec((S, wide), lambda p, j: (0, p)),
                  pl.BlockSpec((tk, wide), lambda p, j: (j, p)), pl.BlockSpec((tk, wide), lambda p, j: (j, p))],
        out_specs=(pl.BlockSpec((S, LANES), lambda p, j: (0, p)), pl.BlockSpec((tk, LANES), lambda p, j: (j, p)),
                   pl.BlockSpec((tk, LANES), lambda p, j: (j, p)), pl.BlockSpec((None, S, LANES), lambda p, j: (p, 0, 0)),
                   pl.BlockSpec((None, tk, LANES), lambda p, j: (p, j, 0))),
        scratch_shapes=[pltpu.VMEM((S, wide), F32), pltpu.VMEM((tk, wide), F32), pltpu.VMEM((tk, wide), F32)],
        name="attn_bwd", compiler_params=_cp(("parallel", "arbitrary"), VMEM_LIMIT),
    )(qb, doa, ka, va)


def _attn_delta_old(o, do):
    S = o.shape[0]
    tm = _pick(S, (512, 256))
    ones = _group_ones()

    def body(o_ref, do_ref, ones_ref, d_ref):
        d_ref[...] = _dot3(o_ref[...] * do_ref[...], ones_ref[...])

    return pl.pallas_call(
        body, out_shape=jax.ShapeDtypeStruct((S, ATT_WIDTH), F32), grid=(S // tm,),
        in_specs=[pl.BlockSpec((tm, ATT_WIDTH), lambda i: (i, 0)), pl.BlockSpec((tm, ATT_WIDTH), lambda i: (i, 0)),
                  pl.BlockSpec((ATT_WIDTH, ATT_WIDTH), lambda i: (0, 0))],
        out_specs=pl.BlockSpec((tm, ATT_WIDTH), lambda i: (i, 0)),
        name="attn_delta", compiler_params=_cp(("parallel",)),
    )(o, do, ones)


def _attn_bwd_old(qkv, do_bf, c_cols, c_rows, lse_rows, dl_rows):
    S = qkv.shape[0]
    tk = _pick(S, (256,))
    tq = tk
    nq = S // tq
    nt = (((1,), (1,)), ((), ()))
    tn_dims = (((0,), (0,)), ((), ()))

    def body(q_ref, do_ref, k_ref, v_ref, cc_ref, cr_ref, lse_ref, dl_ref, dq_ref, dk_ref, dv_ref, dcs_ref, dcq_ref):
        kb = pl.program_id(1)

        @pl.when(kb == 0)
        def _():
            dq_ref[...] = jnp.zeros_like(dq_ref)
            dcq_ref[...] = jnp.zeros_like(dcq_ref)

        lane = lax.broadcasted_iota(jnp.int32, (tk, LANES), 1)
        rid = lax.broadcasted_iota(jnp.int32, (tk, tq), 0)
        cid = lax.broadcasted_iota(jnp.int32, (tk, tq), 1)
        k = k_ref[...]
        v = v_ref[...]
        dks, dvs, dcs = [], [], []
        for hh in range(2):
            sel = (lane >= GROUP_DIM) if hh else (lane < GROUP_DIM)
            km = jnp.where(sel, k, jnp.zeros_like(k))
            vm = jnp.where(sel, v, jnp.zeros_like(v))
            cs = cc_ref[:, hh * HEAD_DIM:hh * HEAD_DIM + 1]

            def step(qb, carry, km=km, vm=vm, cs=cs, hh=hh):
                dk_acc, dv_acc, dc_acc = carry
                qs = pl.multiple_of(qb * tq, tq)
                qblk = q_ref[pl.ds(qs, tq), :]
                doblk = do_ref[pl.ds(qs, tq), :]
                s = lax.dot_general(km, qblk, nt, preferred_element_type=F32)
                s = s + (cr_ref[hh:hh + 1, pl.ds(qs, tq)] - cs)
                p = jnp.exp(s - lse_ref[hh:hh + 1, pl.ds(qs, tq)])
                p = jnp.where((qb > kb) | (cid >= rid), p, 0.0)
                dp = lax.dot_general(vm, doblk, nt, preferred_element_type=F32)
                ds = p * (dp - dl_ref[hh:hh + 1, pl.ds(qs, tq)])
                ds_bf = ds.astype(BF16)
                dv_acc = dv_acc + jnp.dot(p.astype(BF16), doblk, preferred_element_type=F32)
                dk_acc = dk_acc + jnp.dot(ds_bf, qblk, preferred_element_type=F32)
                dc_acc = dc_acc + jnp.sum(ds, axis=-1, keepdims=True)
                dq_ref[pl.ds(qs, tq), :] += lax.dot_general(ds_bf, km, tn_dims, preferred_element_type=F32)
                dcq_ref[hh:hh + 1, pl.ds(qs, tq)] += jnp.sum(ds, axis=0, keepdims=True)
                return dk_acc, dv_acc, dc_acc

            init = (jnp.zeros((tk, LANES), F32), jnp.zeros((tk, LANES), F32), jnp.zeros((tk, 1), F32))
            dk_acc, dv_acc, dc_acc = lax.fori_loop(kb, nq, step, init)
            dks.append(dk_acc)
            dvs.append(dv_acc)
            dcs.append(dc_acc)
        dk_ref[...] = jnp.where(lane < GROUP_DIM, dks[0], dks[1])
        dv_ref[...] = jnp.where(lane < GROUP_DIM, dvs[0], dvs[1])
        dcs_ref[...] = jnp.where(lane == 0, -dcs[0], jnp.where(lane == 1, -dcs[1], 0.0))

    return pl.pallas_call(
        body,
        out_shape=(jax.ShapeDtypeStruct((S, ATT_WIDTH), F32), jax.ShapeDtypeStruct((S, ATT_WIDTH), F32),
                   jax.ShapeDtypeStruct((S, ATT_WIDTH), F32), jax.ShapeDtypeStruct((N_PAIRS, S, LANES), F32),
                   jax.ShapeDtypeStruct((N_PAIRS, 8, S), F32)),
        grid=(N_PAIRS, nq),
        in_specs=[pl.BlockSpec((S, LANES), lambda p, j: (0, p)),
                  pl.BlockSpec((S, LANES), lambda p, j: (0, p)),
                  pl.BlockSpec((tk, LANES), lambda p, j: (j, N_PAIRS + p)),
                  pl.BlockSpec((tk, LANES), lambda p, j: (j, 2 * N_PAIRS + p)),
                  pl.BlockSpec((None, tk, LANES), lambda p, j: (p, j, 0)),
                  pl.BlockSpec((None, 8, S), lambda p, j: (p, 0, 0)),
                  pl.BlockSpec((None, 8, S), lambda p, j: (p, 0, 0)),
                  pl.BlockSpec((None, 8, S), lambda p, j: (p, 0, 0))],
        out_specs=(pl.BlockSpec((S, LANES), lambda p, j: (0, p)),
                   pl.BlockSpec((tk, LANES), lambda p, j: (j, p)),
                   pl.BlockSpec((tk, LANES), lambda p, j: (j, p)),
                   pl.BlockSpec((None, tk, LANES), lambda p, j: (p, j, 0)),
                   pl.BlockSpec((None, 8, S), lambda p, j: (p, 0, 0))),
        name="attn_bwd", compiler_params=_cp(("parallel", "arbitrary"), VMEM_LIMIT),
    )(qkv, do_bf, qkv, qkv, c_cols, c_rows, lse_rows, dl_rows)


def _gmlp_bwd(ug, dsg, gain, w_s, wt_s, bias_full):
    S = ug.shape[0]
    tm = _pick(S, (512, 256, 128))
    n_chunks = tm // CHUNK
    n_i = S // tm
    ones = _group_ones()
    nt = (((1,), (1,)), ((), ()))

    def body(ug_ref, dsg_ref, gain_ref, w_ref, wt_ref, bias_ref, ones_ref, dug_ref, dw_ref, dgain_ref, dbias_ref,
             dbacc_ref):
        i = pl.program_id(0)

        @pl.when(i == 0)
        def _():
            dw_ref[...] = jnp.zeros_like(dw_ref)
            dgain_ref[...] = jnp.zeros_like(dgain_ref)
            dbacc_ref[...] = jnp.zeros_like(dbacc_ref)

        ones_m = ones_ref[...]
        pu = ug_ref[:, :GMLP_WIDTH]
        pg = ug_ref[:, GMLP_WIDTH:]
        u = _gelu(pu)
        vr = _gelu(pg)
        ms = _dot3(vr * vr, ones_m) * (1.0 / GROUP_DIM)
        rinv = lax.rsqrt(ms + EPS)
        vhat = vr * rinv
        gain_v = gain_ref[...]
        vn = (vhat * gain_v).astype(BF16)
        mixed = _gmlp_mixed(vn, w_ref, bias_ref[...], n_chunks)
        dsg_v = dsg_ref[...]
        du = dsg_v * mixed
        dmixed = dsg_v * u
        dm_bf = dmixed.astype(BF16)
        lane = lax.broadcasted_iota(jnp.int32, (CHUNK, LANES), 1)
        row = lax.broadcasted_iota(jnp.int32, (CHUNK, CHUNK), 0)
        col = lax.broadcasted_iota(jnp.int32, (CHUNK, CHUNK), 1)
        wts = [jnp.where(col >= row, wt_ref[g], 0.0).astype(BF16) for g in range(N_GROUPS)]
        dvn_rows = []
        dbsum = jnp.zeros((CHUNK, GMLP_WIDTH), F32)
        for ci in range(n_chunks):
            rs = slice(ci * CHUNK, (ci + 1) * CHUNK)
            dbsum = dbsum + dmixed[rs, :]
            cols = []
            for pp in range(N_GROUPS // 2):
                cs = slice(pp * LANES, (pp + 1) * LANES)
                dm = dm_bf[rs, cs]
                dm_lo = jnp.where(lane < GROUP_DIM, dm, jnp.zeros_like(dm))
                dm_hi = jnp.where(lane >= GROUP_DIM, dm, jnp.zeros_like(dm))
                vb = vn[rs, cs]
                dw_ref[2 * pp] += lax.dot_general(dm_lo, vb, nt, preferred_element_type=F32)
                dw_ref[2 * pp + 1] += lax.dot_general(dm_hi, vb, nt, preferred_element_type=F32)
                cols.append(jnp.dot(wts[2 * pp], dm_lo, preferred_element_type=F32)
                            + jnp.dot(wts[2 * pp + 1], dm_hi, preferred_element_type=F32))
            dvn_rows.append(jnp.concatenate(cols, axis=1))
        dvn = jnp.concatenate(dvn_rows, axis=0)
        dbacc_ref[...] += dbsum
        dgain_ref[0:1, :] += jnp.sum(dvn * vhat, axis=0, keepdims=True)
        dvhat = dvn * gain_v
        gm = _dot3(dvhat * vhat, ones_m) * (1.0 / GROUP_DIM)
        dvr = rinv * (dvhat - vhat * gm)
        dug_ref[:, :GMLP_WIDTH] = (du * _gelu_grad(pu)).astype(BF16)
        dug_ref[:, GMLP_WIDTH:] = (dvr * _gelu_grad(pg)).astype(BF16)

        @pl.when(i == n_i - 1)
        def _():
            for g in range(N_GROUPS):
                dw_ref[g] = jnp.where(row >= col, dw_ref[g], 0.0)
            dbias_ref[...] = _dot3(dbacc_ref[...], ones_m)

    return pl.pallas_call(
        body,
        out_shape=(jax.ShapeDtypeStruct((S, 2 * GMLP_WIDTH), BF16), jax.ShapeDtypeStruct((N_GROUPS, CHUNK, CHUNK), F32),
                   jax.ShapeDtypeStruct((8, GMLP_WIDTH), F32), jax.ShapeDtypeStruct((CHUNK, GMLP_WIDTH), F32)),
        grid=(n_i,),
        in_specs=[pl.BlockSpec((tm, 2 * GMLP_WIDTH), lambda i: (i, 0)), pl.BlockSpec((tm, GMLP_WIDTH), lambda i: (i, 1)),
                  pl.BlockSpec((1, GMLP_WIDTH), lambda i: (0, 0)),
                  pl.BlockSpec((N_GROUPS, CHUNK, CHUNK), lambda i: (0, 0, 0)),
                  pl.BlockSpec((N_GROUPS, CHUNK, CHUNK), lambda i: (0, 0, 0)),
                  pl.BlockSpec((CHUNK, GMLP_WIDTH), lambda i: (0, 0)),
                  pl.BlockSpec((GMLP_WIDTH, GMLP_WIDTH), lambda i: (0, 0))],
        out_specs=(pl.BlockSpec((tm, 2 * GMLP_WIDTH), lambda i: (i, 0)),
                   pl.BlockSpec((N_GROUPS, CHUNK, CHUNK), lambda i: (0, 0, 0)),
                   pl.BlockSpec((8, GMLP_WIDTH), lambda i: (0, 0)),
                   pl.BlockSpec((CHUNK, GMLP_WIDTH), lambda i: (0, 0))),
        scratch_shapes=[pltpu.VMEM((CHUNK, GMLP_WIDTH), F32)],
        name="gmlp_bwd", compiler_params=_cp(("arbitrary",), VMEM_LIMIT),
    )(ug, dsg, gain, w_s, wt_s, bias_full, ones)


def _gate_bwd(dcq, dck, zf):
    S = zf.shape[0]
    tm = _pick(S, (256,))
    n_i = S // tm
    triu = (lax.broadcasted_iota(jnp.int32, (tm, tm), 0) <= lax.broadcasted_iota(jnp.int32, (tm, tm), 1)).astype(BF16)

    def body(dcq_ref, dck_ref, zf_ref, tri_ref, dzf_ref, dbf_ref, carry_ref):
        i = pl.program_id(0)

        @pl.when(i == 0)
        def _():
            carry_ref[...] = jnp.zeros_like(carry_ref)
            dbf_ref[...] = jnp.zeros_like(dbf_ref)

        lane = lax.broadcasted_iota(jnp.int32, (tm, LANES), 1)
        dc = jnp.zeros((tm, LANES), F32)
        for p in range(N_PAIRS):
            slab = dcq_ref[p] + dck_ref[p]
            for hh in range(2):
                dc = dc + jnp.where(lane == 2 * p + hh, slab[:, hh:hh + 1], 0.0)
        dlf = _dot3l(tri_ref[...], dc) + carry_ref[0:1, :]
        carry_ref[0:1, :] = dlf[0:1, :]
        dz = jnp.where(lane < N_HEADS, dlf * _sigmoid(-zf_ref[...]), 0.0)
        dzf_ref[...] = dz.astype(BF16)
        dbf_ref[0:1, :] += jnp.sum(dz, axis=0, keepdims=True)

    return pl.pallas_call(
        body,
        out_shape=(jax.ShapeDtypeStruct((S, LANES), BF16), jax.ShapeDtypeStruct((8, LANES), F32)),
        grid=(n_i,),
        in_specs=[pl.BlockSpec((N_PAIRS, tm, LANES), lambda i: (0, n_i - 1 - i, 0)),
                  pl.BlockSpec((N_PAIRS, tm, LANES), lambda i: (0, n_i - 1 - i, 0)),
                  pl.BlockSpec((tm, LANES), lambda i: (n_i - 1 - i, 0)),
                  pl.BlockSpec((tm, tm), lambda i: (0, 0))],
        out_specs=(pl.BlockSpec((tm, LANES), lambda i: (n_i - 1 - i, 0)), pl.BlockSpec((8, LANES), lambda i: (0, 0))),
        scratch_shapes=[pltpu.VMEM((8, LANES), F32)],
        name="gate_bwd", compiler_params=_cp(("arbitrary",), VMEM_LIMIT),
    )(dcq, dck, zf, triu)


def _out_proj_fwd(att_bf, sg, w_out_bf, x, g_ffn):
    S = x.shape[0]
    tm = _pick(S, (512, 256))

    def body(a_ref, s_ref, w_ref, x_ref, g_ref, h_ref, hn_ref):
        h = (x_ref[...] + jnp.dot(a_ref[...], w_ref[:ATT_WIDTH, :], preferred_element_type=F32)
             + jnp.dot(s_ref[...], w_ref[ATT_WIDTH:, :], preferred_element_type=F32))
        h_ref[...] = h
        r = lax.rsqrt(jnp.mean(h * h, axis=-1, keepdims=True) + EPS)
        hn_ref[...] = ((h * r) * g_ref[...]).astype(BF16)

    row = pl.BlockSpec((tm, D_MODEL), lambda i: (i, 0))
    half = pl.BlockSpec((tm, ATT_WIDTH), lambda i: (i, 0))
    return pl.pallas_call(
        body, out_shape=(jax.ShapeDtypeStruct((S, D_MODEL), F32), jax.ShapeDtypeStruct((S, D_MODEL), BF16)),
        grid=(S // tm,),
        in_specs=[half, half, pl.BlockSpec((D_MODEL, D_MODEL), lambda i: (0, 0)), row,
                  pl.BlockSpec((1, D_MODEL), lambda i: (0, 0))],
        out_specs=(row, row), name="out_proj", compiler_params=_cp(("parallel",), VMEM_LIMIT),
    )(att_bf, sg, w_out_bf, x, g_ffn)


_IN_PIECES = ((0, ATT_WIDTH), (ATT_WIDTH, ATT_WIDTH), (2 * ATT_WIDTH, ATT_WIDTH), (QKV, 2 * GMLP_WIDTH), (UG_END, LANES))


def _inproj_bwd_dx(pieces, w_pad, x, g_mix, dh1):
    S = x.shape[0]
    tm = _pick(S, (512, 256))

    def body(*refs):
        p_refs, (w_ref, x_ref, g_ref, r_ref, dx_ref, dg_ref) = refs[:5], refs[5:]
        i = pl.program_id(0)

        @pl.when(i == 0)
        def _():
            dg_ref[...] = jnp.zeros_like(dg_ref)

        dxn = None
        for p_ref, (c0, width) in zip(p_refs, _IN_PIECES):
            part = lax.dot_general(p_ref[...], w_ref[:, c0:c0 + width], _NT, preferred_element_type=F32)
            dxn = part if dxn is None else dxn + part
        xf = x_ref[...]
        r = lax.rsqrt(jnp.mean(xf * xf, axis=-1, keepdims=True) + EPS)
        xhat = xf * r
        dg_ref[0:1, :] += jnp.sum(dxn * xhat, axis=0, keepdims=True)
        dhat = dxn * g_ref[...]
        dx_ref[...] = r_ref[...] + r * (dhat - xhat * jnp.mean(dhat * xhat, axis=-1, keepdims=True))

    row = pl.BlockSpec((tm, D_MODEL), lambda i: (i, 0))
    return pl.pallas_call(
        body, out_shape=(jax.ShapeDtypeStruct((S, D_MODEL), F32), jax.ShapeDtypeStruct((8, D_MODEL), F32)),
        grid=(S // tm,),
        in_specs=[pl.BlockSpec((tm, width), lambda i: (i, 0)) for _, width in _IN_PIECES]
        + [pl.BlockSpec((D_MODEL, IN_PAD), lambda i: (0, 0)), row, pl.BlockSpec((1, D_MODEL), lambda i: (0, 0)), row],
        out_specs=(row, pl.BlockSpec((8, D_MODEL), lambda i: (0, 0))),
        name="in_proj_dx", compiler_params=_cp(("arbitrary",), VMEM_LIMIT),
    )(*pieces, w_pad, x, g_mix, dh1)


def _inproj_bwd_dw(xn, pieces):
    S = xn.shape[0]
    tk = _pick(S, (512, 256))

    def body(*refs):
        x_ref, p_refs, o_ref = refs[0], refs[1:6], refs[6]
        k = pl.program_id(0)

        @pl.when(k == 0)
        def _():
            o_ref[...] = jnp.zeros_like(o_ref)

        xb = x_ref[...]
        for p_ref, (c0, width) in zip(p_refs, _IN_PIECES):
            o_ref[:, c0:c0 + width] += lax.dot_general(xb, p_ref[...], _TN, preferred_element_type=F32)

    return pl.pallas_call(
        body, out_shape=jax.ShapeDtypeStruct((D_MODEL, IN_PAD), F32), grid=(S // tk,),
        in_specs=[pl.BlockSpec((tk, D_MODEL), lambda k: (k, 0))]
        + [pl.BlockSpec((tk, width), lambda k: (k, 0)) for _, width in _IN_PIECES],
        out_specs=pl.BlockSpec((D_MODEL, IN_PAD), lambda k: (0, 0)),
        name="in_proj_dw", compiler_params=_cp(("arbitrary",), VMEM_LIMIT),
    )(xn, *pieces)


def _adamw(w, m, v, parts, name):
    R, C = w.shape
    tr = R
    for cand in (256, 128, 64, 32, 16, 8):
        if R % cand == 0 and R > cand:
            tr = cand
            break
    c1 = 1.0 / (1.0 - ADAM_B1 ** ADAM_STEP)
    c2 = 1.0 / (1.0 - ADAM_B2 ** ADAM_STEP)

    def body(w_ref, m_ref, v_ref, p_ref, g_ref, d_ref, nm_ref, nv_ref):
        g = p_ref[0].astype(F32)
        for j in range(1, N_DEV):
            g = g + p_ref[j].astype(F32)
        g_ref[...] = g
        nm = ADAM_B1 * m_ref[...] + (1.0 - ADAM_B1) * g
        nv = ADAM_B2 * v_ref[...] + (1.0 - ADAM_B2) * (g * g)
        nm_ref[...] = nm
        nv_ref[...] = nv
        d_ref[...] = -ADAM_LR * ((nm * c1) / (jnp.sqrt(nv * c2) + ADAM_EPS) + ADAM_WD * w_ref[...])

    spec = pl.BlockSpec((tr, C), lambda i: (i, 0))
    shp = jax.ShapeDtypeStruct((R, C), F32)
    return pl.pallas_call(
        body, out_shape=(shp, shp, shp, shp), grid=(R // tr,),
        in_specs=[spec, spec, spec, pl.BlockSpec((N_DEV, tr, C), lambda i: (0, i, 0))],
        out_specs=(spec, spec, spec, spec),
        name=name, compiler_params=_cp(("parallel",), VMEM_LIMIT),
    )(w, m, v, parts)


def _place():
    x, y, c = lax.axis_index("x"), lax.axis_index("y"), lax.axis_index("c")
    return x, y, c


def _all_gather(blocks, name):
    n = len(blocks)

    def body(*refs):
        ins, outs = refs[:n], refs[n:2 * n]
        send_sems, recv_sems, local_sems = refs[2 * n:]
        x, y, c = _place()
        me, sibling = (x, y, c), (x, y, 1 - c)
        chips = [(1 - x, y), (x, 1 - y), (1 - x, 1 - y)]
        sends = []
        for a in range(n):
            out = outs[a]

            def slot(px, py, pc, out=out):
                return out.at[4 * px + 2 * py + pc]

            def copy(k, block, to, src=None, a=a, slot=slot):
                return pltpu.make_async_remote_copy(
                    src_ref=slot(*block) if src is None else src, dst_ref=slot(*block),
                    send_sem=send_sems.at[a, k], recv_sem=recv_sems.at[a, k], device_id=to, device_id_type=MESH)

            mine = pltpu.make_async_copy(ins[a], slot(*me), local_sems.at[a])
            mine.start()
            first = [copy(0, me, sibling, src=ins[a])]
            first += [copy(1 + j, me, (*chip, c), src=ins[a]) for j, chip in enumerate(chips)]
            for cp in first:
                cp.start()
            sends.append((mine, first, copy))
        for a in range(n):
            mine, first, copy = sends[a]
            passed = [copy(4 + j, (*chip, c), sibling) for j, chip in enumerate(chips)]
            for j, chip in enumerate(chips):
                copy(1 + j, (*chip, c), me).wait_recv()
                passed[j].start()
            copy(0, sibling, me).wait_recv()
            for j, chip in enumerate(chips):
                copy(4 + j, (*chip, 1 - c), me).wait_recv()
            for cp in first + passed:
                cp.wait_send()
            mine.wait()

    any_spec = pl.BlockSpec(memory_space=pl.ANY)
    return pl.pallas_call(
        body, out_shape=tuple(jax.ShapeDtypeStruct((N_DEV,) + b.shape, b.dtype) for b in blocks),
        in_specs=[any_spec] * n, out_specs=tuple([any_spec] * n),
        scratch_shapes=[pltpu.SemaphoreType.DMA((n, 7)), pltpu.SemaphoreType.DMA((n, 7)), pltpu.SemaphoreType.DMA((n,))],
        name=name,
    )(*blocks)


def _exchange_shards(parts, name):
    n = len(parts)

    def body(*refs):
        ins, outs = refs[:n], refs[n:2 * n]
        send_sems, recv_sems, local_sems = refs[2 * n:]
        x, y, c = _place()
        me = 4 * x + 2 * y + c
        started = []
        for a in range(n):
            mine = pltpu.make_async_copy(ins[a].at[me], outs[a].at[me], local_sems.at[a])
            mine.start()
            started.append(mine)
            for k in range(1, N_DEV):
                px, py, pc = x ^ ((k >> 2) & 1), y ^ ((k >> 1) & 1), c ^ (k & 1)
                cp = pltpu.make_async_remote_copy(
                    src_ref=ins[a].at[4 * px + 2 * py + pc], dst_ref=outs[a].at[me],
                    send_sem=send_sems.at[a, k - 1], recv_sem=recv_sems.at[a, k - 1],
                    device_id=(px, py, pc), device_id_type=MESH)
                cp.start()
                started.append(cp)
        for cp in started:
            cp.wait()

    any_spec = pl.BlockSpec(memory_space=pl.ANY)
    return pl.pallas_call(
        body, out_shape=tuple(jax.ShapeDtypeStruct(p.shape, p.dtype) for p in parts),
        in_specs=[any_spec] * n, out_specs=tuple([any_spec] * n),
        scratch_shapes=[pltpu.SemaphoreType.DMA((n, 7)), pltpu.SemaphoreType.DMA((n, 7)), pltpu.SemaphoreType.DMA((n,))],
        name=name,
    )(*parts)


_HBM = pl.BlockSpec(memory_space=pltpu.HBM)
_SEM = pl.BlockSpec(memory_space=pltpu.SEMAPHORE)
_EFFECT = pltpu.SideEffectType.DATAFLOW_SIDE_EFFECTING


def _peers(x, y, c):
    out = []
    for k in range(1, N_DEV):
        px, py, pc = x ^ ((k >> 2) & 1), y ^ ((k >> 1) & 1), c ^ (k & 1)
        out.append((k, (px, py, pc), 4 * px + 2 * py + pc))
    return out


def _xchg_copies(src_refs, land_refs, send_sems, recv_sems, scatter):
    x, y, c = _place()
    me = 4 * x + 2 * y + c
    copies = []
    for a, (src, land) in enumerate(zip(src_refs, land_refs)):
        for k, place, idx in _peers(x, y, c):
            j = a * (N_DEV - 1) + k - 1
            copies.append(pltpu.make_async_remote_copy(
                src_ref=src.at[idx] if scatter[a] else src, dst_ref=land.at[me],
                send_sem=send_sems[j], recv_sem=recv_sems[j], device_id=place, device_id_type=MESH))
    return copies


def _xchg_start(srcs, scatter, name):
    n = len(srcs)
    lands = [lax.empty((N_DEV,) + (s.shape[1:] if sc else s.shape), s.dtype) for s, sc in zip(srcs, scatter)]

    ns = n * (N_DEV - 1)

    def body(*refs):
        sems = refs[2 * n:2 * n + 2 * ns]
        for cp in _xchg_copies(refs[:n], refs[n:2 * n], sems[:ns], sems[ns:], scatter):
            cp.start()
        token = refs[-1]
        token[...] = jnp.zeros_like(token)

    both = list(srcs) + lands
    res = pl.pallas_call(
        body, name=name,
        out_shape=(*[pltpu.SemaphoreType.DMA(())] * (2 * ns),
                   *[pltpu.HBM(a.shape, a.dtype) for a in both], jax.ShapeDtypeStruct((8, LANES), F32)),
        in_specs=[_HBM] * (2 * n),
        out_specs=(*([_SEM] * (2 * ns)), *([_HBM] * (2 * n)), pl.BlockSpec(memory_space=pltpu.VMEM)),
        input_output_aliases={i: 2 * ns + i for i in range(2 * n)},
        compiler_params=pltpu.CompilerParams(has_side_effects=_EFFECT),
    )(*[pltpu.with_memory_space_constraint(a, pltpu.HBM) for a in both])
    return (tuple(res[:2 * ns]), tuple(res[2 * ns:2 * ns + 2 * n])), res[-1]


def _xchg_wait(handle, scatter, after, name):
    sems, thru = handle
    n = len(thru) // 2
    ns = len(sems) // 2

    def body(*refs):
        got = refs[2 * n:2 * n + 2 * ns]
        for cp in _xchg_copies(refs[:n], refs[n:2 * n], got[:ns], got[ns:], scatter):
            cp.wait_send()
            cp.wait_recv()

    outs = pl.pallas_call(
        body, name=name, out_shape=tuple(pltpu.HBM(a.shape, a.dtype) for a in thru),
        in_specs=[_HBM] * (2 * n) + [_SEM] * (2 * ns) + [pl.BlockSpec(memory_space=pl.ANY)],
        out_specs=tuple([_HBM] * (2 * n)), input_output_aliases={i: i for i in range(2 * n)},
        compiler_params=pltpu.CompilerParams(has_side_effects=_EFFECT),
    )(*thru, *sems, after)
    return outs[:n], outs[n:]


def _tie(a, token):
    return a if token is None else a + token[0, 0]


def _rows128(a):
    flat = a.reshape(-1)
    rows = -(-flat.shape[0] // LANES)
    rows = -(-rows // 8) * 8
    return jnp.pad(flat, (0, rows * LANES - flat.shape[0])).reshape(rows, LANES)


def _local_step(x, target, norm_mix_g, w_in_bf, b_forget, gmlp_norm_g, w_spatial, b_spatial, norm_ffn_g, conv_b,
                norm_final_g, rest_fn, send_fn, token=None):
    f = D_FF
    g_mix = norm_mix_g.reshape(1, D_MODEL)
    w_pad = jnp.pad(w_in_bf, ((0, 0), (0, IN_PAD - IN_COLS)))
    bf_pad = jnp.pad(b_forget.reshape(1, N_HEADS), ((0, 0), (0, LANES - N_HEADS)))
    xn, qa, ka, va, ug, zf = _inproj_fwd(x, _tie(g_mix, token), w_pad, bf_pad)
    bias_full = jnp.repeat(b_spatial.reshape(N_GROUPS, CHUNK).T, GROUP_DIM, axis=1)
    w_s = w_spatial.reshape(N_GROUPS, CHUNK, CHUNK)
    gain = gmlp_norm_g.reshape(1, GMLP_WIDTH)
    sg = _gmlp_fwd(ug, gain, w_s, bias_full)
    att, lse, att_bf = _attn_fwd(qa, ka, va)
    w_out_bf, w_up_bf, conv_w, w_down_bf = rest_fn(att_bf)
    g_ffn = norm_ffn_g.reshape(1, D_MODEL)
    h1, hn = _out_proj_fwd(att_bf, sg, w_out_bf, x, g_ffn)
    cw = jnp.pad(conv_w.reshape(3, 2, f).transpose(1, 0, 2), ((0, 0), (0, 5), (0, 0)))
    cb = conv_b.reshape(2, 1, f)
    hu, hc, act = _ffn_up_conv(hn, w_up_bf, cw, cb)
    loss_blk, dh2, dh2_bf, dg_final = _ffn_down_loss(act, w_down_bf, h1, norm_final_g.reshape(1, D_MODEL), target)
    dw_down = _mm(act, dh2_bf, mode="tn", out_dtype=F32, tm=1408, tn=1024, tk=2048, name="ffn_down_dw")
    dact = _mm(dh2_bf, w_down_bf, mode="nt", out_dtype=F32, tm=512, tn=1408, tk=1024, outer="j", name="ffn_down_dx")
    dhu, dcw = _conv_gate_bwd(hc, hu, dact, _tie(cw, send_fn("w_down", dw_down)))
    dw_up = _mm(hn, dhu, mode="tn", out_dtype=F32, tm=1024, tn=1408, tk=2048, b_halves=True, outer="j", name="ffn_up_dw")
    dh1, dh1_bf, dg_ffn = _ffn_up_dx_rms(dhu, w_up_bf, h1, _tie(g_ffn, send_fn("w_up", dw_up)), dh2)
    dmix = _mm(dh1_bf, w_out_bf, mode="nt", out_dtype=F32, tm=512, tn=1024, tk=1024, name="out_proj_dx")
    dw_out = jnp.concatenate(
        [_mm(att_bf, dh1_bf, mode="tn", out_dtype=F32, tm=512, tn=1024, tk=1024, name="out_proj_dw_att"),
         _mm(sg, dh1_bf, mode="tn", out_dtype=F32, tm=512, tn=1024, tk=1024, name="out_proj_dw_sg")], axis=0)
    qb, doa = _attn_prep(att, lse, dmix, qa)
    dq, dk, dv, dcq, dck = _attn_bwd(qb, ka, va, doa)
    wt_s = w_s.transpose(0, 2, 1)
    dug, dw_s, dgain, dbias = _gmlp_bwd(ug, dmix, _tie(gain, send_fn("w_out", dw_out)), w_s, wt_s, bias_full)
    dzf, dbf = _gate_bwd(dcq, dck, zf)
    pieces = (dq, dk, dv, dug, dzf)
    dw_in = _inproj_bwd_dw(xn, pieces)
    grad_x, dg_mix = _inproj_bwd_dx(pieces, w_pad, x, _tie(g_mix, send_fn("w_in", dw_in[:, :IN_COLS])), dh1)
    grads = dict(
        norm_mix_g=dg_mix[0:1, :],
        b_forget=dbf[0:1, :N_HEADS],
        gmlp_norm_g=dgain[0:1, :],
        w_spatial=dw_s,
        b_spatial=dbias[:, ::GROUP_DIM].T,
        norm_ffn_g=dg_ffn[0:1, :],
        conv_w=dcw[:, 0:3, :].transpose(1, 0, 2).reshape(3, 2 * f),
        conv_b=dcw[:, 3, :].reshape(1, 2 * f),
        norm_final_g=dg_final[0, :],
    )
    return loss_blk[0, 0], grad_x, grads


SMALL = ("norm_mix_g", "b_forget", "gmlp_norm_g", "w_spatial", "b_spatial", "norm_ffn_g", "conv_b", "norm_final_g")


def kernel(x, norm_mix_g, w_in, b_forget, gmlp_norm_g, w_spatial, b_spatial, w_out, norm_ffn_g, w_up, conv_w, conv_b, w_down, norm_final_g, loss_target, m_norm_mix_g, m_w_in, m_b_forget, m_gmlp_norm_g, m_w_spatial, m_b_spatial, m_w_out, m_norm_ffn_g, m_w_up, m_conv_w, m_conv_b, m_w_down, m_norm_final_g, v_norm_mix_g, v_w_in, v_b_forget, v_gmlp_norm_g, v_w_spatial, v_b_spatial, v_w_out, v_norm_ffn_g, v_w_up, v_conv_w, v_conv_b, v_w_down, v_norm_final_g):
    weights = dict(norm_mix_g=norm_mix_g, w_in=w_in, b_forget=b_forget, gmlp_norm_g=gmlp_norm_g, w_spatial=w_spatial,
                   b_spatial=b_spatial, w_out=w_out, norm_ffn_g=norm_ffn_g, w_up=w_up, conv_w=conv_w, conv_b=conv_b,
                   w_down=w_down, norm_final_g=norm_final_g)
    m_in = dict(norm_mix_g=m_norm_mix_g, w_in=m_w_in, b_forget=m_b_forget, gmlp_norm_g=m_gmlp_norm_g,
                w_spatial=m_w_spatial, b_spatial=m_b_spatial, w_out=m_w_out, norm_ffn_g=m_norm_ffn_g, w_up=m_w_up,
                conv_w=m_conv_w, conv_b=m_conv_b, w_down=m_w_down, norm_final_g=m_norm_final_g)
    v_in = dict(norm_mix_g=v_norm_mix_g, w_in=v_w_in, b_forget=v_b_forget, gmlp_norm_g=v_gmlp_norm_g,
                w_spatial=v_w_spatial, b_spatial=v_b_spatial, w_out=v_w_out, norm_ffn_g=v_norm_ffn_g, w_up=v_w_up,
                conv_w=v_conv_w, conv_b=v_conv_b, w_down=v_w_down, norm_final_g=v_norm_final_g)
    order = list(weights)
    me = 4 * lax.axis_index("x") + 2 * lax.axis_index("y") + lax.axis_index("c")
    n_in, n_up = w_in.shape[2], w_up.shape[2]
    r_out, r_down = w_out.shape[1], w_down.shape[1]

    def with_mine(landed, mine):
        return lax.dynamic_update_index_in_dim(landed, mine, me, 0)

    up_blk = w_up[0].astype(BF16)
    rows_blk = jnp.concatenate([w_out[0].astype(BF16), w_down[0].astype(BF16)], axis=0)
    taps_blk = jnp.pad(conv_w[0], ((0, 5), (0, 0)))
    (in_all,) = _all_gather([w_in[0].astype(BF16)], "gather_w_in")
    in_all, rest_blocks = lax.optimization_barrier((in_all, [up_blk, rows_blk, taps_blk]))
    rest_handle, token = _xchg_start(rest_blocks, [False] * 3, "gather_rest_start")
    w_in_bf = in_all.transpose(1, 0, 2).reshape(D_MODEL, N_DEV * n_in)

    def rest_fn(after):
        mine, landed = _xchg_wait(rest_handle, [False] * 3, after, "gather_rest_wait")
        up_all, rows_all, taps_all = [with_mine(l, b) for l, b in zip(landed, mine)]
        return (rows_all[:, :r_out, :].reshape(N_DEV * r_out, D_MODEL),
                up_all.transpose(1, 0, 2).reshape(D_MODEL, N_DEV * n_up),
                taps_all[:, :3, :].transpose(1, 0, 2).reshape(3, N_DEV * n_up),
                rows_all[:, r_out:, :].reshape(N_DEV * r_down, D_MODEL))

    sent = {}

    def send_fn(name, grad):
        if name == "w_in":
            parts = grad.reshape(D_MODEL, N_DEV, -1).transpose(1, 0, 2).astype(BF16)
        elif name == "w_up":
            parts = grad.reshape(D_MODEL, N_DEV, -1).transpose(1, 0, 2)
        else:
            parts = grad.reshape(N_DEV, -1, D_MODEL)
        sent[name], tok = _xchg_start([parts], [True], "scatter_" + name + "_start")
        return tok

    loss_local, grad_x, g = _local_step(
        x[0], loss_target[0], norm_mix_g, w_in_bf, b_forget, gmlp_norm_g, w_spatial, b_spatial, norm_ffn_g, conv_b,
        norm_final_g, rest_fn, send_fn, token)
    loss = lax.psum(loss_local, ("x", "y", "c"))

    got = {}
    for name, handle in sent.items():
        (parts,), (landed,) = _xchg_wait(handle, [True], grad_x, "scatter_" + name + "_wait")
        got[name] = with_mine(landed, lax.dynamic_index_in_dim(parts, me, 0, keepdims=False))
    got_in, got_up, got_out, got_down = got["w_in"], got["w_up"], got["w_out"], got["w_down"]

    small_names = SMALL + ("conv_w",)
    packed = [_rows128(g[k]) for k in small_names]
    sizes = [p.shape[0] for p in packed]
    (small_all,) = _all_gather([jnp.concatenate(packed, axis=0)], "gather_small_grads")

    def pack(src):
        return jnp.concatenate([_rows128(src[k]) for k in SMALL], axis=0)

    n_small_rows = sum(sizes[:-1])
    sg_, sd_, sm_, sv_ = _adamw(pack(weights), pack(m_in), pack(v_in), small_all[:, :n_small_rows, :], "adamw_small")

    outs = {}
    off = 0
    for k, rows in zip(SMALL, sizes[:-1]):
        shp = weights[k].shape
        cnt = math.prod(shp)
        outs[k] = tuple(a[off:off + rows].reshape(-1)[:cnt].reshape(shp) for a in (sg_, sd_, sm_, sv_))
        off += rows
    taps_parts = small_all[:, n_small_rows:, :].reshape(N_DEV, -1)[:, :3 * N_DEV * n_up].reshape(N_DEV, 3, N_DEV * n_up)
    taps_mine = lax.dynamic_slice_in_dim(taps_parts, me * n_up, n_up, axis=2)
    taps_mine = jnp.pad(taps_mine, ((0, 0), (0, 5), (0, 0)))

    def pad8(a):
        return jnp.pad(a[0], ((0, 5), (0, 0)))

    res = _adamw(pad8(conv_w), pad8(m_conv_w), pad8(v_conv_w), taps_mine, "adamw_conv_w")
    outs["conv_w"] = tuple(a[:3][None] for a in res)
    for k, got in (("w_in", got_in), ("w_up", got_up), ("w_out", got_out), ("w_down", got_down)):
        res = _adamw(weights[k][0], m_in[k][0], v_in[k][0], got, "adamw_" + k)
        outs[k] = tuple(a[None] for a in res)

    return (loss, grad_x[None], *[outs[k][0] for k in order], *[outs[k][1] for k in order],
            *[outs[k][2] for k in order], *[outs[k][3] for k in order])
```

```python
import functools
import math

import jax
import jax.numpy as jnp
from jax import lax
from jax.experimental import pallas as pl
from jax.experimental.pallas import tpu as pltpu

F32 = jnp.float32
BF16 = jnp.bfloat16

N_DEV = 8
D_MODEL = 1024
ATT_WIDTH = 512
GMLP_WIDTH = 512
HEAD_DIM = 64
N_HEADS = 8
N_PAIRS = 4
N_GROUPS = 8
GROUP_DIM = 64
CHUNK = 128
D_FF = 2816
IN_COLS = 2568
IN_PAD = 2688
QKV = 1536
UG_END = 2560
EPS = 1e-6
LANES = 128

ADAM_LR = 0.001
ADAM_B1 = 0.9
ADAM_B2 = 0.999
ADAM_EPS = 1e-08
ADAM_WD = 0.01
ADAM_STEP = 10

ATT_TQ = 1024
ATT_TK = 1024
FFN_TM, FFN_TN = 512, 1408
CONV_TM, CONV_TN = 256, 1408
VMEM_LIMIT = 56 * 1024 * 1024
MESH = pl.DeviceIdType.MESH


def _cp(sem, vmem=None):
    return pltpu.CompilerParams(dimension_semantics=sem, vmem_limit_bytes=vmem)


def _pick(n, prefs):
    for p in prefs:
        if n % p == 0:
            return p
    return n


def _split3(x):
    hi = x.astype(BF16)
    r1 = x - hi.astype(F32)
    mid = r1.astype(BF16)
    lo = (r1 - mid.astype(F32)).astype(BF16)
    return hi, mid, lo


def _dot3(x, ones_bf):
    hi, mid, lo = _split3(x)
    d = functools.partial(jnp.dot, preferred_element_type=F32)
    return d(hi, ones_bf) + d(mid, ones_bf) + d(lo, ones_bf)


def _dot3l(ones_bf, x):
    hi, mid, lo = _split3(x)
    d = functools.partial(jnp.dot, preferred_element_type=F32)
    return d(ones_bf, hi) + d(ones_bf, mid) + d(ones_bf, lo)


def _gelu(x):
    k = math.sqrt(2.0 / math.pi)
    t = jnp.tanh(k * (x + 0.044715 * (x * x * x)))
    return 0.5 * x * (1.0 + t)


def _gelu_grad(x):
    k = math.sqrt(2.0 / math.pi)
    x2 = x * x
    t = jnp.tanh(k * (x + 0.044715 * (x2 * x)))
    return 0.5 * (1.0 + t) + 0.5 * x * (1.0 - t * t) * (k * (1.0 + 3.0 * 0.044715 * x2))


def _sigmoid(x):
    return 1.0 / (1.0 + jnp.exp(-x))


def _mm(a, b, *, mode, out_dtype, tm, tn, tk, name, res=None, a_halves=False, b_halves=False,
        out_halves=False, outer="i"):
    if mode == "tn":
        K, M = a.shape[-2], a.shape[-1] * (2 if a_halves else 1)
    else:
        M, K = a.shape[-2], a.shape[-1] * (2 if a_halves else 1)
    if mode == "nt":
        N = b.shape[-2]
        assert b.shape[-1] == K
    else:
        N = b.shape[-1] * (2 if b_halves else 1)
    tm, tn, tk = min(tm, M), min(tn, N), min(tk, K)
    assert M % tm == 0 and N % tn == 0 and K % tk == 0, (name, M, N, K, tm, tn, tk)
    nm, nn, nk = M // tm, N // tn, K // tk

    def ij(g0, g1):
        return (g0, g1) if outer == "i" else (g1, g0)

    if mode == "nn":
        dims = (((1,), (0,)), ((), ()))
        if a_halves:
            nkh = nk // 2
            a_spec = pl.BlockSpec((None, tm, tk), lambda g0, g1, k: (k // nkh, ij(g0, g1)[0], k % nkh))
        else:
            a_spec = pl.BlockSpec((tm, tk), lambda g0, g1, k: (ij(g0, g1)[0], k))
        b_spec = pl.BlockSpec((tk, tn), lambda g0, g1, k: (k, ij(g0, g1)[1]))
    elif mode == "nt":
        dims = (((1,), (1,)), ((), ()))
        if a_halves:
            nkh = nk // 2
            a_spec = pl.BlockSpec((None, tm, tk), lambda g0, g1, k: (k // nkh, ij(g0, g1)[0], k % nkh))
        else:
            a_spec = pl.BlockSpec((tm, tk), lambda g0, g1, k: (ij(g0, g1)[0], k))
        b_spec = pl.BlockSpec((tn, tk), lambda g0, g1, k: (ij(g0, g1)[1], k))
    else:
        dims = (((0,), (0,)), ((), ()))
        if a_halves:
            nmh = nm // 2
            a_spec = pl.BlockSpec((None, tk, tm), lambda g0, g1, k: (ij(g0, g1)[0] // nmh, k, ij(g0, g1)[0] % nmh))
        else:
            a_spec = pl.BlockSpec((tk, tm), lambda g0, g1, k: (k, ij(g0, g1)[0]))
        if b_halves:
            nnh = nn // 2
            b_spec = pl.BlockSpec((None, tk, tn), lambda g0, g1, k: (ij(g0, g1)[1] // nnh, k, ij(g0, g1)[1] % nnh))
        else:
            b_spec = pl.BlockSpec((tk, tn), lambda g0, g1, k: (k, ij(g0, g1)[1]))
    if out_halves:
        nnh = nn // 2
        o_spec = pl.BlockSpec((None, tm, tn), lambda g0, g1, k: (ij(g0, g1)[1] // nnh, ij(g0, g1)[0], ij(g0, g1)[1] % nnh))
        o_shape = jax.ShapeDtypeStruct((2, M, N // 2), out_dtype)
    else:
        o_spec = pl.BlockSpec((tm, tn), lambda g0, g1, k: ij(g0, g1))
        o_shape = jax.ShapeDtypeStruct((M, N), out_dtype)
    in_specs = [a_spec, b_spec]
    args = [a, b]
    if res is not None:
        in_specs.append(pl.BlockSpec((tm, tn), lambda g0, g1, k: ij(g0, g1)))
        args.append(res)

    def body(*refs):
        if res is not None:
            a_ref, b_ref, r_ref, o_ref = refs[:4]
        else:
            a_ref, b_ref, o_ref = refs[:3]
            r_ref = None
        part = lax.dot_general(a_ref[...], b_ref[...], dims, preferred_element_type=F32)
        if nk == 1:
            if r_ref is not None:
                part = part + r_ref[...]
            o_ref[...] = part.astype(out_dtype)
            return
        acc_ref = refs[-1]
        k = pl.program_id(2)

        @pl.when(k == 0)
        def _():
            acc_ref[...] = part

        @pl.when(k > 0)
        def _():
            acc_ref[...] += part

        @pl.when(k == nk - 1)
        def _():
            tot = acc_ref[...]
            if r_ref is not None:
                tot = tot + r_ref[...]
            o_ref[...] = tot.astype(out_dtype)

    grid = (nm, nn, nk) if outer == "i" else (nn, nm, nk)
    scratch = [] if nk == 1 else [pltpu.VMEM((tm, tn), F32)]
    return pl.pallas_call(
        body, out_shape=o_shape, grid=grid, in_specs=in_specs, out_specs=o_spec, scratch_shapes=scratch,
        name=name, compiler_params=_cp(("parallel", "parallel", "arbitrary"), VMEM_LIMIT),
    )(*args)


def _aug(lane, terms):
    out = 0.0
    for j, t in enumerate(terms):
        out = jnp.where(lane == HEAD_DIM + j, t, out)
    return out


def _split3f(x):
    hi, mid, lo = _split3(x)
    return [hi.astype(F32), mid.astype(F32), lo.astype(F32)]


def _inproj_fwd(x, g_mix, w_pad, bf_pad):
    S = x.shape[0]
    tm = _pick(S, (512, 256))
    tri = (lax.broadcasted_iota(jnp.int32, (tm, tm), 0) >= lax.broadcasted_iota(jnp.int32, (tm, tm), 1)).astype(BF16)

    def body(x_ref, g_ref, w_ref, bf_ref, tri_ref, xn_ref, qa_ref, ka_ref, va_ref, ug_ref, zf_ref, carry_ref):
        i = pl.program_id(0)

        @pl.when(i == 0)
        def _():
            carry_ref[...] = jnp.zeros_like(carry_ref)

        xf = x_ref[...]
        r = lax.rsqrt(jnp.mean(xf * xf, axis=-1, keepdims=True) + EPS)
        xn = ((xf * r) * g_ref[...]).astype(BF16)
        xn_ref[...] = xn
        proj = lax.dot_general(xn, w_ref[...], _NT, preferred_element_type=F32)
        ug_ref[...] = proj[:, QKV:UG_END]
        zf = proj[:, UG_END:] + bf_ref[...]
        zf_ref[...] = zf
        lf = jnp.minimum(zf, 0.0) - jnp.log(1.0 + jnp.exp(-jnp.abs(zf)))
        c = _dot3l(tri_ref[...], lf) + carry_ref[0:1, :]
        carry_ref[0:1, :] = c[tm - 1:tm, :]
        c3 = _split3f(c)
        lane = lax.broadcasted_iota(jnp.int32, (tm, LANES), 1)
        ones3 = [1.0, 1.0, 1.0]
        for h in range(N_HEADS):
            p, odd = h // 2, h % 2
            ch = [t[:, h:h + 1] for t in c3]

            def head(base, scale=None, p=p, odd=odd):
                blk = proj[:, base + p * LANES:base + (p + 1) * LANES]
                if scale is not None:
                    blk = blk * scale
                return pltpu.roll(blk, HEAD_DIM, 1) if odd else blk

            cols = slice(h * LANES, (h + 1) * LANES)
            qa_ref[:, cols] = jnp.where(lane < HEAD_DIM, head(0, HEAD_DIM ** -0.5), _aug(lane, ch + ones3)).astype(BF16)
            ka_ref[:, cols] = jnp.where(lane < HEAD_DIM, head(ATT_WIDTH),
                                        _aug(lane, ones3 + [-t for t in ch] + ones3)).astype(BF16)
            va_ref[:, cols] = jnp.where(lane < HEAD_DIM, head(2 * ATT_WIDTH), _aug(lane, ones3)).astype(BF16)

    wide = N_HEADS * LANES
    return pl.pallas_call(
        body,
        out_shape=(jax.ShapeDtypeStruct((S, D_MODEL), BF16), jax.ShapeDtypeStruct((S, wide), BF16),
                   jax.ShapeDtypeStruct((S, wide), BF16), jax.ShapeDtypeStruct((S, wide), BF16),
                   jax.ShapeDtypeStruct((S, 2 * GMLP_WIDTH), F32), jax.ShapeDtypeStruct((S, LANES), F32)),
        grid=(S // tm,),
        in_specs=[pl.BlockSpec((tm, D_MODEL), lambda i: (i, 0)), pl.BlockSpec((1, D_MODEL), lambda i: (0, 0)),
                  pl.BlockSpec((IN_PAD, D_MODEL), lambda i: (0, 0)), pl.BlockSpec((1, LANES), lambda i: (0, 0)),
                  pl.BlockSpec((tm, tm), lambda i: (0, 0))],
        out_specs=(pl.BlockSpec((tm, D_MODEL), lambda i: (i, 0)), pl.BlockSpec((tm, wide), lambda i: (i, 0)),
                   pl.BlockSpec((tm, wide), lambda i: (i, 0)), pl.BlockSpec((tm, wide), lambda i: (i, 0)),
                   pl.BlockSpec((tm, 2 * GMLP_WIDTH), lambda i: (i, 0)), pl.BlockSpec((tm, LANES), lambda i: (i, 0))),
        scratch_shapes=[pltpu.VMEM((8, LANES), F32)],
        name="inproj_fwd", compiler_params=_cp(("arbitrary",), VMEM_LIMIT),
    )(x, g_mix, w_pad, bf_pad, tri)


def _group_ones():
    r = lax.broadcasted_iota(jnp.int32, (GMLP_WIDTH, GMLP_WIDTH), 0) // GROUP_DIM
    c = lax.broadcasted_iota(jnp.int32, (GMLP_WIDTH, GMLP_WIDTH), 1) // GROUP_DIM
    return (r == c).astype(BF16)


def _gmlp_mixed(vn_bf, w_ref, bias, n_chunks):
    lane = lax.broadcasted_iota(jnp.int32, (CHUNK, LANES), 1)
    row = lax.broadcasted_iota(jnp.int32, (CHUNK, CHUNK), 0)
    col = lax.broadcasted_iota(jnp.int32, (CHUNK, CHUNK), 1)
    ws = [jnp.where(row >= col, w_ref[g], 0.0).astype(BF16) for g in range(N_GROUPS)]
    rows = []
    for ci in range(n_chunks):
        cols = []
        for pp in range(N_GROUPS // 2):
            v = vn_bf[ci * CHUNK:(ci + 1) * CHUNK, pp * LANES:(pp + 1) * LANES]
            v_lo = jnp.where(lane < GROUP_DIM, v, jnp.zeros_like(v))
            v_hi = jnp.where(lane >= GROUP_DIM, v, jnp.zeros_like(v))
            m = (jnp.dot(ws[2 * pp], v_lo, preferred_element_type=F32)
                 + jnp.dot(ws[2 * pp + 1], v_hi, preferred_element_type=F32))
            cols.append(m + bias[:, pp * LANES:(pp + 1) * LANES])
        rows.append(jnp.concatenate(cols, axis=1))
    return jnp.concatenate(rows, axis=0)


def _gmlp_fwd(ug, gain, w_s, bias_full):
    S = ug.shape[0]
    tm = _pick(S, (512, 256, 128))
    ones = _group_ones()

    def body(ug_ref, gain_ref, w_ref, bias_ref, ones_ref, sg_ref):
        u = _gelu(ug_ref[:, :GMLP_WIDTH])
        vr = _gelu(ug_ref[:, GMLP_WIDTH:])
        ms = _dot3(vr * vr, ones_ref[...]) * (1.0 / GROUP_DIM)
        vn = ((vr * lax.rsqrt(ms + EPS)) * gain_ref[...]).astype(BF16)
        mixed = _gmlp_mixed(vn, w_ref, bias_ref[...], tm // CHUNK)
        sg_ref[...] = (u * mixed).astype(BF16)

    return pl.pallas_call(
        body, out_shape=jax.ShapeDtypeStruct((S, GMLP_WIDTH), BF16), grid=(S // tm,),
        in_specs=[pl.BlockSpec((tm, 2 * GMLP_WIDTH), lambda i: (i, 0)), pl.BlockSpec((1, GMLP_WIDTH), lambda i: (0, 0)),
                  pl.BlockSpec((N_GROUPS, CHUNK, CHUNK), lambda i: (0, 0, 0)),
                  pl.BlockSpec((CHUNK, GMLP_WIDTH), lambda i: (0, 0)),
                  pl.BlockSpec((GMLP_WIDTH, GMLP_WIDTH), lambda i: (0, 0))],
        out_specs=pl.BlockSpec((tm, GMLP_WIDTH), lambda i: (i, 0)),
        name="gmlp_fwd", compiler_params=_cp(("parallel",), VMEM_LIMIT),
    )(ug, gain, w_s, bias_full, ones)


_NT = (((1,), (1,)), ((), ()))
_TN = (((0,), (0,)), ((), ()))


def _attn_fwd(qa, ka, va):
    S = qa.shape[0]
    tq = _pick(S, (ATT_TQ, 256))
    tk = min(ATT_TK, tq)
    nq = S // tq
    per_q = tq // tk

    def body(q_ref, k_ref, v_ref, o_ref, lse_ref, ob_ref):
        qi = pl.program_id(1)
        lane = lax.broadcasted_iota(jnp.int32, (tq, LANES), 1)
        sub = tk
        rid = lax.broadcasted_iota(jnp.int32, (sub, sub), 0)
        cid = lax.broadcasted_iota(jnp.int32, (sub, sub), 1)
        qs = [q_ref[:, :LANES], q_ref[:, LANES:]]

        def update(q, ks, k_len, h, m, acc, masked):
            cols = slice(h * LANES, (h + 1) * LANES)
            s = lax.dot_general(q, k_ref[pl.ds(ks, k_len), cols], _NT, preferred_element_type=F32)
            if masked:
                s = jnp.where(rid >= cid, s, -jnp.inf)
            m_new = jnp.maximum(m, jnp.max(s, axis=-1, keepdims=True))
            p = jnp.exp(s - m_new).astype(BF16)
            acc = jnp.exp(m - m_new) * acc + jnp.dot(p, v_ref[pl.ds(ks, k_len), cols], preferred_element_type=F32)
            return m_new, acc

        def step(kb, carry):
            ks = pl.multiple_of(kb * tk, tk)
            return tuple(update(qs[h], ks, tk, h, *carry[h], False) for h in range(2))

        one = (jnp.full((tq, 1), -jnp.inf, F32), jnp.zeros((tq, LANES), F32))
        carry = lax.fori_loop(0, qi * per_q, step, (one, one))
        outs, lses = [], []
        for h in range(2):
            ms, accs = [], []
            for r in range(tq // sub):
                rows = slice(r * sub, (r + 1) * sub)
                m, acc = carry[h][0][rows], carry[h][1][rows]
                for c in range(r + 1):
                    ks = pl.multiple_of(qi * tq + c * sub, sub)
                    m, acc = update(qs[h][rows], ks, sub, h, m, acc, c == r)
                ms.append(m)
                accs.append(acc)
            m, acc = jnp.concatenate(ms, axis=0), jnp.concatenate(accs, axis=0)
            l = acc[:, HEAD_DIM:HEAD_DIM + 1]
            outs.append(acc / l)
            lses.append(m + jnp.log(l))
        o = jnp.where(lane < HEAD_DIM, outs[0], pltpu.roll(outs[1], HEAD_DIM, 1))
        o_ref[...] = o
        ob_ref[...] = o.astype(BF16)
        lse_ref[...] = jnp.where(lane < HEAD_DIM, lses[0], lses[1])

    return pl.pallas_call(
        body,
        out_shape=(jax.ShapeDtypeStruct((S, ATT_WIDTH), F32), jax.ShapeDtypeStruct((S, ATT_WIDTH), F32),
                   jax.ShapeDtypeStruct((S, ATT_WIDTH), BF16)),
        grid=(N_PAIRS, nq),
        in_specs=[pl.BlockSpec((tq, 2 * LANES), lambda p, i: (i, p)),
                  pl.BlockSpec((S, 2 * LANES), lambda p, i: (0, p)),
                  pl.BlockSpec((S, 2 * LANES), lambda p, i: (0, p))],
        out_specs=(pl.BlockSpec((tq, LANES), lambda p, i: (i, p)), pl.BlockSpec((tq, LANES), lambda p, i: (i, p)),
                   pl.BlockSpec((tq, LANES), lambda p, i: (i, p))),
        name="attn_fwd", compiler_params=_cp(("parallel", "parallel"), VMEM_LIMIT),
    )(qa, ka, va)


def _rms_fwd(h, g):
    S = h.shape[0]
    tm = _pick(S, (512, 256))

    def body(h_ref, g_ref, o_ref):
        hf = h_ref[...]
        r = lax.rsqrt(jnp.mean(hf * hf, axis=-1, keepdims=True) + EPS)
        o_ref[...] = ((hf * r) * g_ref[...]).astype(BF16)

    return pl.pallas_call(
        body, out_shape=jax.ShapeDtypeStruct(h.shape, BF16), grid=(S // tm,),
        in_specs=[pl.BlockSpec((tm, D_MODEL), lambda i: (i, 0)), pl.BlockSpec((1, D_MODEL), lambda i: (0, 0))],
        out_specs=pl.BlockSpec((tm, D_MODEL), lambda i: (i, 0)),
        name="rms_fwd", compiler_params=_cp(("parallel",)),
    )(h, g)


def _shift_rows(x, prev, n):
    rid = lax.broadcasted_iota(jnp.int32, x.shape, 0)
    y = pltpu.roll(x, n, 0)
    if n == 1:
        return jnp.where(rid == 0, prev[7:8, :], y)
    return jnp.where(rid == 0, prev[6:7, :], jnp.where(rid == 1, prev[7:8, :], y))


def _shift_rows_up(x, nxt, n):
    rows = x.shape[0]
    rid = lax.broadcasted_iota(jnp.int32, x.shape, 0)
    y = pltpu.roll(x, rows - n, 0)
    if n == 1:
        return jnp.where(rid == rows - 1, nxt[0:1, :], y)
    return jnp.where(rid == rows - 2, nxt[0:1, :], jnp.where(rid == rows - 1, nxt[1:2, :], y))


def _conv3(cur, prev, w, b):
    return (w[0:1, :] * _shift_rows(cur, prev, 2) + w[1:2, :] * _shift_rows(cur, prev, 1)
            + w[2:3, :] * cur + b)


def _conv_act_fwd(hu, cw, cb):
    _, S, F = hu.shape
    tm = _pick(S, (512, 256))
    tn = _pick(F, (256, 128))
    r8 = tm // 8

    def body(cur_ref, prev_ref, w_ref, b_ref, o_ref):
        i = pl.program_id(1)
        halves = []
        for h in range(2):
            prev = jnp.where(i > 0, prev_ref[h], 0.0)
            halves.append(_conv3(cur_ref[h], prev, w_ref[h], b_ref[h]))
        a, g = halves
        o_ref[...] = (g * _sigmoid(g) * a).astype(BF16)

    return pl.pallas_call(
        body, out_shape=jax.ShapeDtypeStruct((S, F), BF16), grid=(F // tn, S // tm),
        in_specs=[pl.BlockSpec((2, tm, tn), lambda j, i: (0, i, j)),
                  pl.BlockSpec((2, 8, tn), lambda j, i: (0, jnp.maximum(i * r8 - 1, 0), j)),
                  pl.BlockSpec((2, 8, tn), lambda j, i: (0, 0, j)),
                  pl.BlockSpec((2, 1, tn), lambda j, i: (0, 0, j))],
        out_specs=pl.BlockSpec((tm, tn), lambda j, i: (i, j)),
        name="conv_act_fwd", compiler_params=_cp(("parallel", "parallel"), VMEM_LIMIT),
    )(hu, hu, cw, cb)


def _ffn_up_conv(hn, w_up_bf, cw, cb):
    S = hn.shape[0]
    F = D_FF
    tm = _pick(S, (FFN_TM, 256))
    tn = _pick(F, (FFN_TN, 256, 128))
    nj = F // tn

    def body(hn_ref, wa_ref, wg_ref, cw_ref, cb_ref, hu_ref, hc_ref, act_ref, tail_ref):
        i = pl.program_id(1)

        @pl.when(i == 0)
        def _():
            tail_ref[...] = jnp.zeros_like(tail_ref)

        hn_v = hn_ref[...]
        halves = []
        for h, w_ref in enumerate((wa_ref, wg_ref)):
            hu = lax.dot_general(hn_v, w_ref[...], _NT, preferred_element_type=F32)
            hu_ref[h] = hu
            hc = _conv3(hu, tail_ref[h], cw_ref[h], cb_ref[h])
            hc_ref[h] = hc
            halves.append(hc)
            tail_ref[h] = hu[tm - 8:, :]
        a, g = halves
        act_ref[...] = (g * _sigmoid(g) * a).astype(BF16)

    both = pl.BlockSpec((2, tm, tn), lambda j, i: (0, i, j))
    return pl.pallas_call(
        body, out_shape=(jax.ShapeDtypeStruct((2, S, F), F32), jax.ShapeDtypeStruct((2, S, F), F32),
                         jax.ShapeDtypeStruct((S, F), BF16)),
        grid=(nj, S // tm),
        in_specs=[pl.BlockSpec((tm, D_MODEL), lambda j, i: (i, 0)),
                  pl.BlockSpec((tn, D_MODEL), lambda j, i: (j, 0)),
                  pl.BlockSpec((tn, D_MODEL), lambda j, i: (nj + j, 0)),
                  pl.BlockSpec((2, 8, tn), lambda j, i: (0, 0, j)),
                  pl.BlockSpec((2, 1, tn), lambda j, i: (0, 0, j))],
        out_specs=(both, both, pl.BlockSpec((tm, tn), lambda j, i: (i, j))),
        scratch_shapes=[pltpu.VMEM((2, 8, tn), F32)],
        name="ffn_up_conv", compiler_params=_cp(("parallel", "arbitrary"), VMEM_LIMIT),
    )(hn, w_up_bf, w_up_bf, cw, cb)


def _ffn_down_loss(act, w_down_bf, h1, g_final, target):
    S = h1.shape[0]
    tm = _pick(S, (512, 256))

    def body(a_ref, w_ref, h1_ref, g_ref, t_ref, loss_ref, dh_ref, dhb_ref, dg_ref):
        i = pl.program_id(0)

        @pl.when(i == 0)
        def _():
            loss_ref[...] = jnp.zeros_like(loss_ref)
            dg_ref[...] = jnp.zeros_like(dg_ref)

        hf = h1_ref[...] + jnp.dot(a_ref[...], w_ref[...], preferred_element_type=F32)
        g = g_ref[...]
        r = lax.rsqrt(jnp.mean(hf * hf, axis=-1, keepdims=True) + EPS)
        hhat = hf * r
        err = hhat * g - t_ref[...]
        loss_ref[...] += 0.5 * jnp.sum(jnp.mean(err * err, axis=-1, keepdims=True))
        dy = err * (1.0 / D_MODEL)
        dg_ref[0:1, :] += jnp.sum(dy * hhat, axis=0, keepdims=True)
        dhat = dy * g
        dh = r * (dhat - hhat * jnp.mean(dhat * hhat, axis=-1, keepdims=True))
        dh_ref[...] = dh
        dhb_ref[...] = dh.astype(BF16)

    row = pl.BlockSpec((tm, D_MODEL), lambda i: (i, 0))
    return pl.pallas_call(
        body,
        out_shape=(jax.ShapeDtypeStruct((8, LANES), F32), jax.ShapeDtypeStruct((S, D_MODEL), F32),
                   jax.ShapeDtypeStruct((S, D_MODEL), BF16), jax.ShapeDtypeStruct((8, D_MODEL), F32)),
        grid=(S // tm,),
        in_specs=[pl.BlockSpec((tm, D_FF), lambda i: (i, 0)), pl.BlockSpec((D_FF, D_MODEL), lambda i: (0, 0)), row,
                  pl.BlockSpec((1, D_MODEL), lambda i: (0, 0)), row],
        out_specs=(pl.BlockSpec((8, LANES), lambda i: (0, 0)), row, row, pl.BlockSpec((8, D_MODEL), lambda i: (0, 0))),
        name="ffn_down_loss", compiler_params=_cp(("arbitrary",), VMEM_LIMIT),
    )(act, w_down_bf, h1, g_final, target)


def _loss_head(h2, g_final, target):
    S = h2.shape[0]
    tm = _pick(S, (512, 256))

    def body(h_ref, g_ref, t_ref, loss_ref, dh_ref, dhb_ref, dg_ref):
        i = pl.program_id(0)

        @pl.when(i == 0)
        def _():
            loss_ref[...] = jnp.zeros_like(loss_ref)
            dg_ref[...] = jnp.zeros_like(dg_ref)

        hf = h_ref[...]
        g = g_ref[...]
        r = lax.rsqrt(jnp.mean(hf * hf, axis=-1, keepdims=True) + EPS)
        hhat = hf * r
        err = hhat * g - t_ref[...]
        loss_ref[...] += 0.5 * jnp.sum(jnp.mean(err * err, axis=-1, keepdims=True))
        dy = err * (1.0 / D_MODEL)
        dg_ref[0:1, :] += jnp.sum(dy * hhat, axis=0, keepdims=True)
        dhat = dy * g
        dh = r * (dhat - hhat * jnp.mean(dhat * hhat, axis=-1, keepdims=True))
        dh_ref[...] = dh
        dhb_ref[...] = dh.astype(BF16)

    return pl.pallas_call(
        body,
        out_shape=(jax.ShapeDtypeStruct((8, LANES), F32), jax.ShapeDtypeStruct((S, D_MODEL), F32),
                   jax.ShapeDtypeStruct((S, D_MODEL), BF16), jax.ShapeDtypeStruct((8, D_MODEL), F32)),
        grid=(S // tm,),
        in_specs=[pl.BlockSpec((tm, D_MODEL), lambda i: (i, 0)), pl.BlockSpec((1, D_MODEL), lambda i: (0, 0)),
                  pl.BlockSpec((tm, D_MODEL), lambda i: (i, 0))],
        out_specs=(pl.BlockSpec((8, LANES), lambda i: (0, 0)), pl.BlockSpec((tm, D_MODEL), lambda i: (i, 0)),
                   pl.BlockSpec((tm, D_MODEL), lambda i: (i, 0)), pl.BlockSpec((8, D_MODEL), lambda i: (0, 0))),
        name="loss_head", compiler_params=_cp(("arbitrary",), VMEM_LIMIT),
    )(h2, g_final, target)


def _rms_bwd(h, g, dy, res):
    S = h.shape[0]
    tm = _pick(S, (512, 256))

    def body(h_ref, g_ref, dy_ref, r_ref, dh_ref, dhb_ref, dg_ref):
        i = pl.program_id(0)

        @pl.when(i == 0)
        def _():
            dg_ref[...] = jnp.zeros_like(dg_ref)

        hf = h_ref[...]
        dyv = dy_ref[...]
        r = lax.rsqrt(jnp.mean(hf * hf, axis=-1, keepdims=True) + EPS)
        hhat = hf * r
        dg_ref[0:1, :] += jnp.sum(dyv * hhat, axis=0, keepdims=True)
        dhat = dyv * g_ref[...]
        dh = r_ref[...] + r * (dhat - hhat * jnp.mean(dhat * hhat, axis=-1, keepdims=True))
        dh_ref[...] = dh
        dhb_ref[...] = dh.astype(BF16)

    return pl.pallas_call(
        body,
        out_shape=(jax.ShapeDtypeStruct((S, D_MODEL), F32), jax.ShapeDtypeStruct((S, D_MODEL), BF16),
                   jax.ShapeDtypeStruct((8, D_MODEL), F32)),
        grid=(S // tm,),
        in_specs=[pl.BlockSpec((tm, D_MODEL), lambda i: (i, 0)), pl.BlockSpec((1, D_MODEL), lambda i: (0, 0)),
                  pl.BlockSpec((tm, D_MODEL), lambda i: (i, 0)), pl.BlockSpec((tm, D_MODEL), lambda i: (i, 0))],
        out_specs=(pl.BlockSpec((tm, D_MODEL), lambda i: (i, 0)), pl.BlockSpec((tm, D_MODEL), lambda i: (i, 0)),
                   pl.BlockSpec((8, D_MODEL), lambda i: (0, 0))),
        name="rms_bwd", compiler_params=_cp(("arbitrary",), VMEM_LIMIT),
    )(h, g, dy, res)


def _conv_act_bwd(hu, dh2_bf, w_down_bf, cw, cb):
    _, S, F = hu.shape
    tm = _pick(S, (256,))
    tn = _pick(F, (256, 128))
    r8 = tm // 8
    r16 = tm // 16
    n_i = S // tm
    last8 = S // 8 - 1
    last16 = S // 16 - 1

    def body(cur_ref, prev_ref, next_ref, dy_ref, dyn_ref, wd_ref, w_ref, b_ref, dhu_ref, dcw_ref):
        i = pl.program_id(1)
        wd = wd_ref[...]
        dact = lax.dot_general(dy_ref[...], wd, _NT, preferred_element_type=F32)
        dact_n = lax.dot_general(dyn_ref[...], wd, _NT, preferred_element_type=F32)[0:8, :]

        @pl.when(i == 0)
        def _():
            dcw_ref[...] = jnp.zeros_like(dcw_ref)

        rid8 = lax.broadcasted_iota(jnp.int32, (8, tn), 0)
        cur = [cur_ref[0], cur_ref[1]]
        prev = [jnp.where(i > 0, prev_ref[h], 0.0) for h in range(2)]
        nxt = [next_ref[0], next_ref[1]]
        w = [w_ref[0], w_ref[1]]

        def gate_grads(a, g, d):
            sg = _sigmoid(g)
            return d * (g * sg), d * a * (sg * (1.0 + g * (1.0 - sg)))

        taps = [(_shift_rows(cur[h], prev[h], 2), _shift_rows(cur[h], prev[h], 1), cur[h]) for h in range(2)]
        a, g = [w[h][0:1, :] * taps[h][0] + w[h][1:2, :] * taps[h][1] + w[h][2:3, :] * taps[h][2] + b_ref[h]
                for h in range(2)]
        dhc = gate_grads(a, g, dact)
        a_n = _conv3(nxt[0], cur[0][tm - 8:, :], w[0], b_ref[0])
        g_n = _conv3(nxt[1], cur[1][tm - 8:, :], w[1], b_ref[1])
        dhc_n = gate_grads(a_n, g_n, dact_n)
        for h in range(2):
            d = dhc[h]
            dn = jnp.where(i < n_i - 1, dhc_n[h], 0.0)
            dhu = (w[h][2:3, :] * d + w[h][1:2, :] * _shift_rows_up(d, dn, 1)
                   + w[h][0:1, :] * _shift_rows_up(d, dn, 2))
            dhu_ref[h] = dhu.astype(BF16)
            t0, t1, t2 = [jnp.sum(d * t, axis=0, keepdims=True) for t in taps[h]]
            t3 = jnp.sum(d, axis=0, keepdims=True)
            dcw_ref[h] += jnp.where(rid8 == 0, t0, jnp.where(rid8 == 1, t1, jnp.where(rid8 == 2, t2, jnp.where(rid8 == 3, t3, 0.0))))

    return pl.pallas_call(
        body,
        out_shape=(jax.ShapeDtypeStruct((2, S, F), BF16), jax.ShapeDtypeStruct((2, 8, F), F32)),
        grid=(F // tn, n_i),
        in_specs=[pl.BlockSpec((2, tm, tn), lambda j, i: (0, i, j)),
                  pl.BlockSpec((2, 8, tn), lambda j, i: (0, jnp.maximum(i * r8 - 1, 0), j)),
                  pl.BlockSpec((2, 8, tn), lambda j, i: (0, jnp.minimum((i + 1) * r8, last8), j)),
                  pl.BlockSpec((tm, D_MODEL), lambda j, i: (i, 0)),
                  pl.BlockSpec((16, D_MODEL), lambda j, i: (jnp.minimum((i + 1) * r16, last16), 0)),
                  pl.BlockSpec((tn, D_MODEL), lambda j, i: (j, 0)),
                  pl.BlockSpec((2, 8, tn), lambda j, i: (0, 0, j)),
                  pl.BlockSpec((2, 1, tn), lambda j, i: (0, 0, j))],
        out_specs=(pl.BlockSpec((2, tm, tn), lambda j, i: (0, i, j)), pl.BlockSpec((2, 8, tn), lambda j, i: (0, 0, j))),
        name="conv_act_bwd", compiler_params=_cp(("parallel", "arbitrary"), VMEM_LIMIT),
    )(hu, hu, hu, dh2_bf, dh2_bf, w_down_bf, cw, cb)


def _ffn_up_dx_rms(dhu, w_up_bf, h1, g_ffn, dh2):
    _, S, F = dhu.shape
    tm = _pick(S, (512, 256))

    def body(a_ref, b_ref, h_ref, g_ref, r_ref, dh_ref, dhb_ref, dg_ref, acc_ref):
        i, k = pl.program_id(0), pl.program_id(1)

        @pl.when((i == 0) & (k == 0))
        def _():
            dg_ref[...] = jnp.zeros_like(dg_ref)

        part = jnp.dot(a_ref[...], b_ref[...], preferred_element_type=F32)

        @pl.when(k == 0)
        def _():
            acc_ref[...] = part

        @pl.when(k == 1)
        def _():
            dyv = acc_ref[...] + part
            hf = h_ref[...]
            r = lax.rsqrt(jnp.mean(hf * hf, axis=-1, keepdims=True) + EPS)
            hhat = hf * r
            dg_ref[0:1, :] += jnp.sum(dyv * hhat, axis=0, keepdims=True)
            dhat = dyv * g_ref[...]
            dh = r_ref[...] + r * (dhat - hhat * jnp.mean(dhat * hhat, axis=-1, keepdims=True))
            dh_ref[...] = dh
            dhb_ref[...] = dh.astype(BF16)

    row = pl.BlockSpec((tm, D_MODEL), lambda i, k: (i, 0))
    return pl.pallas_call(
        body,
        out_shape=(jax.ShapeDtypeStruct((S, D_MODEL), F32), jax.ShapeDtypeStruct((S, D_MODEL), BF16),
                   jax.ShapeDtypeStruct((8, D_MODEL), F32)),
        grid=(S // tm, 2),
        in_specs=[pl.BlockSpec((None, tm, F), lambda i, k: (k, i, 0)), pl.BlockSpec((F, D_MODEL), lambda i, k: (k, 0)),
                  row, pl.BlockSpec((1, D_MODEL), lambda i, k: (0, 0)), row],
        out_specs=(row, row, pl.BlockSpec((8, D_MODEL), lambda i, k: (0, 0))),
        scratch_shapes=[pltpu.VMEM((tm, D_MODEL), F32)],
        name="ffn_up_dx_rms", compiler_params=_cp(("arbitrary", "arbitrary"), VMEM_LIMIT),
    )(dhu, w_up_bf, h1, g_ffn, dh2)


def _conv_gate_bwd(hc, hu, dact, cw):
    _, S, F = hu.shape
    tm = _pick(S, (CONV_TM, 128))
    tn = _pick(F, (CONV_TN, 256, 128))
    r8 = tm // 8
    n_i = S // tm
    last8 = S // 8 - 1

    def body(hc_ref, hcn_ref, hu_ref, da_ref, dan_ref, w_ref, dhu_ref, dcw_ref):
        i = pl.program_id(1)

        @pl.when(i == 0)
        def _():
            dcw_ref[...] = jnp.zeros_like(dcw_ref)

        rid8 = lax.broadcasted_iota(jnp.int32, (8, tn), 0)

        def gate_grads(a, g, d):
            sg = _sigmoid(g)
            return d * (g * sg), d * a * (sg * (1.0 + g * (1.0 - sg)))

        dhc = gate_grads(hc_ref[0], hc_ref[1], da_ref[...])
        dhc_n = gate_grads(hcn_ref[0], hcn_ref[1], dan_ref[...])
        for h in range(2):
            w = w_ref[h]
            d = dhc[h]
            dn = jnp.where(i < n_i - 1, dhc_n[h], 0.0)
            u1 = _shift_rows_up(d, dn, 1)
            u2 = _shift_rows_up(d, dn, 2)
            dhu_ref[h] = (w[2:3, :] * d + w[1:2, :] * u1 + w[0:1, :] * u2).astype(BF16)
            x = hu_ref[h]
            t0, t1, t2, t3 = [jnp.sum(t, axis=0, keepdims=True) for t in (u2 * x, u1 * x, d * x, d)]
            dcw_ref[h] += jnp.where(rid8 == 0, t0, jnp.where(rid8 == 1, t1, jnp.where(rid8 == 2, t2, jnp.where(rid8 == 3, t3, 0.0))))

    cur = pl.BlockSpec((2, tm, tn), lambda j, i: (0, i, j))
    return pl.pallas_call(
        body,
        out_shape=(jax.ShapeDtypeStruct((2, S, F), BF16), jax.ShapeDtypeStruct((2, 8, F), F32)),
        grid=(F // tn, n_i),
        in_specs=[cur, pl.BlockSpec((2, 8, tn), lambda j, i: (0, jnp.minimum((i + 1) * r8, last8), j)), cur,
                  pl.BlockSpec((tm, tn), lambda j, i: (i, j)),
                  pl.BlockSpec((8, tn), lambda j, i: (jnp.minimum((i + 1) * r8, last8), j)),
                  pl.BlockSpec((2, 8, tn), lambda j, i: (0, 0, j))],
        out_specs=(cur, pl.BlockSpec((2, 8, tn), lambda j, i: (0, 0, j))),
        name="conv_gate_bwd", compiler_params=_cp(("parallel", "arbitrary"), VMEM_LIMIT),
    )(hc, hc, hu, dact, dact, cw)


def _attn_prep(att, lse, dmix, qa):
    S = att.shape[0]
    tm = _pick(S, (256,))

    def body(o_ref, lse_ref, do_ref, q_ref, qb_ref, doa_ref):
        lane = lax.broadcasted_iota(jnp.int32, (tm, LANES), 1)
        for p in range(N_PAIRS):
            pc = slice(p * LANES, (p + 1) * LANES)
            do = do_ref[:, pc]
            prod = o_ref[:, pc] * do
            for hh in range(2):
                sel = (lane >= HEAD_DIM) if hh else (lane < HEAD_DIM)
                delta = jnp.sum(jnp.where(sel, prod, 0.0), axis=-1, keepdims=True)
                dod = pltpu.roll(do, HEAD_DIM, 1) if hh else do
                cols = slice((2 * p + hh) * LANES, (2 * p + hh + 1) * LANES)
                doa_ref[:, cols] = jnp.where(lane < HEAD_DIM, dod, _aug(lane, _split3f(-delta))).astype(BF16)
                lcol = p * LANES + hh * HEAD_DIM
                l3 = _split3f(-lse_ref[:, lcol:lcol + 1])
                augl = _aug(lane, [0.0] * 6 + l3).astype(BF16)
                qb_ref[:, cols] = jnp.where((lane >= HEAD_DIM + 6) & (lane < HEAD_DIM + 9), augl, q_ref[:, cols])

    half = pl.BlockSpec((tm, ATT_WIDTH), lambda i: (i, 0))
    wide = pl.BlockSpec((tm, N_HEADS * LANES), lambda i: (i, 0))
    return pl.pallas_call(
        body,
        out_shape=(jax.ShapeDtypeStruct(qa.shape, BF16), jax.ShapeDtypeStruct(qa.shape, BF16)),
        grid=(S // tm,), in_specs=[half, half, half, wide], out_specs=(wide, wide),
        name="attn_prep", compiler_params=_cp(("parallel",), VMEM_LIMIT),
    )(att, lse, dmix, qa)


def _attn_bwd(qb, ka, va, doa):
    S = qb.shape[0]
    tk = _pick(S, (512, 256))
    tq = tk
    nq = S // tq

    def pair(a, scale=None):
        lane = lax.broadcasted_iota(jnp.int32, (a.shape[0], LANES), 1)
        out = jnp.where(lane < HEAD_DIM, a[:, :LANES], pltpu.roll(a[:, LANES:], HEAD_DIM, 1))
        return out if scale is None else out * scale

    def lanes01(a, col, sign):
        lane = lax.broadcasted_iota(jnp.int32, (a.shape[0], LANES), 1)
        return jnp.where(lane == 0, sign * a[:, col:col + 1], jnp.where(lane == 1, sign * a[:, LANES + col:LANES + col + 1], 0.0))

    def body(q_ref, do_ref, k_ref, v_ref, dqc_ref, dkc_ref, dvc_ref, dcq_ref, dck_ref, dq_ref, dka_ref, dva_ref):
        kb = pl.program_id(1)

        @pl.when(kb == 0)
        def _():
            dq_ref[...] = jnp.zeros_like(dq_ref)

        dka_ref[...] = jnp.zeros_like(dka_ref)
        dva_ref[...] = jnp.zeros_like(dva_ref)
        rid = lax.broadcasted_iota(jnp.int32, (tk, tq), 0)
        cid = lax.broadcasted_iota(jnp.int32, (tk, tq), 1)

        def sub_tile(qs, q_len, k_off, k_len, masked):
            keys = slice(k_off, k_off + k_len)
            for h in range(2):
                cols = slice(h * LANES, (h + 1) * LANES)
                qblk = q_ref[pl.ds(qs, q_len), cols]
                doblk = do_ref[pl.ds(qs, q_len), cols]
                kh = k_ref[keys, cols]
                p = jnp.exp(lax.dot_general(kh, qblk, _NT, preferred_element_type=F32))
                if masked:
                    p = jnp.where(cid >= rid, p, 0.0)
                ds = (p * lax.dot_general(v_ref[keys, cols], doblk, _NT, preferred_element_type=F32)).astype(BF16)
                dva_ref[keys, cols] += jnp.dot(p.astype(BF16), doblk, preferred_element_type=F32)
                dka_ref[keys, cols] += jnp.dot(ds, qblk, preferred_element_type=F32)
                dq_ref[pl.ds(qs, q_len), cols] += lax.dot_general(ds, kh, _TN, preferred_element_type=F32)

        sub_tile(pl.multiple_of(kb * tq, tq), tq, 0, tk, True)

        def step(qi, carry):
            sub_tile(pl.multiple_of(qi * tq, tq), tq, 0, tk, False)
            return carry

        lax.fori_loop(kb + 1, nq, step, 0)
        dka = dka_ref[...]
        dkc_ref[...] = pair(dka).astype(BF16)
        dvc_ref[...] = pair(dva_ref[...]).astype(BF16)
        dck_ref[...] = lanes01(dka, HEAD_DIM + 3, -1.0)

        @pl.when(kb == nq - 1)
        def _():
            dqa = dq_ref[...]
            dqc_ref[...] = pair(dqa, HEAD_DIM ** -0.5).astype(BF16)
            dcq_ref[...] = lanes01(dqa, HEAD_DIM, 1.0)

    wide = 2 * LANES
    half = jax.ShapeDtypeStruct((S, ATT_WIDTH), BF16)
    slabs = jax.ShapeDtypeStruct((N_PAIRS, S, LANES), F32)
    return pl.pallas_call(
        body,
        out_shape=(half, half, half, slabs, slabs),
        grid=(N_PAIRS, nq),
        in_specs=[pl.BlockSpec((S, wide), lambda p, j: (0, p)), pl.BlockSpec((S, wide), lambda p, j: (0, p)),
                  pl.BlockSpec((tk, wide), lambda p, j: (j, p)), pl.BlockSpec((tk, wide), lambda p, j: (j, p))],
        out_specs=(pl.BlockSpec((S, LANES), lambda p, j: (0, p)), pl.BlockSpec((tk, LANES), lambda p, j: (j, p)),
                   pl.BlockSpec((tk, LANES), lambda p, j: (j, p)), pl.BlockSpec((None, S, LANES), lambda p, j: (p, 0, 0)),
                   pl.BlockSpec((None, tk, LANES), lambda p, j: (p, j, 0))),
        scratch_shapes=[pltpu.VMEM((S, wide), F32), pltpu.VMEM((tk, wide), F32), pltpu.VMEM((tk, wide), F32)],
        name="attn_bwd", compiler_params=_cp(("parallel", "arbitrary"), VMEM_LIMIT),
    )(qb, doa, ka, va)


def _attn_delta_old(o, do):
    S = o.shape[0]
    tm = _pick(S, (512, 256))
    ones = _group_ones()

    def body(o_ref, do_ref, ones_ref, d_ref):
        d_ref[...] = _dot3(o_ref[...] * do_ref[...], ones_ref[...])

    return pl.pallas_call(
        body, out_shape=jax.ShapeDtypeStruct((S, ATT_WIDTH), F32), grid=(S // tm,),
        in_specs=[pl.BlockSpec((tm, ATT_WIDTH), lambda i: (i, 0)), pl.BlockSpec((tm, ATT_WIDTH), lambda i: (i, 0)),
                  pl.BlockSpec((ATT_WIDTH, ATT_WIDTH), lambda i: (0, 0))],
        out_specs=pl.BlockSpec((tm, ATT_WIDTH), lambda i: (i, 0)),
        name="attn_delta", compiler_params=_cp(("parallel",)),
    )(o, do, ones)


def _attn_bwd_old(qkv, do_bf, c_cols, c_rows, lse_rows, dl_rows):
    S = qkv.shape[0]
    tk = _pick(S, (256,))
    tq = tk
    nq = S // tq
    nt = (((1,), (1,)), ((), ()))
    tn_dims = (((0,), (0,)), ((), ()))

    def body(q_ref, do_ref, k_ref, v_ref, cc_ref, cr_ref, lse_ref, dl_ref, dq_ref, dk_ref, dv_ref, dcs_ref, dcq_ref):
        kb = pl.program_id(1)

        @pl.when(kb == 0)
        def _():
            dq_ref[...] = jnp.zeros_like(dq_ref)
            dcq_ref[...] = jnp.zeros_like(dcq_ref)

        lane = lax.broadcasted_iota(jnp.int32, (tk, LANES), 1)
        rid = lax.broadcasted_iota(jnp.int32, (tk, tq), 0)
        cid = lax.broadcasted_iota(jnp.int32, (tk, tq), 1)
        k = k_ref[...]
        v = v_ref[...]
        dks, dvs, dcs = [], [], []
        for hh in range(2):
            sel = (lane >= GROUP_DIM) if hh else (lane < GROUP_DIM)
            km = jnp.where(sel, k, jnp.zeros_like(k))
            vm = jnp.where(sel, v, jnp.zeros_like(v))
            cs = cc_ref[:, hh * HEAD_DIM:hh * HEAD_DIM + 1]

            def step(qb, carry, km=km, vm=vm, cs=cs, hh=hh):
                dk_acc, dv_acc, dc_acc = carry
                qs = pl.multiple_of(qb * tq, tq)
                qblk = q_ref[pl.ds(qs, tq), :]
                doblk = do_ref[pl.ds(qs, tq), :]
                s = lax.dot_general(km, qblk, nt, preferred_element_type=F32)
                s = s + (cr_ref[hh:hh + 1, pl.ds(qs, tq)] - cs)
                p = jnp.exp(s - lse_ref[hh:hh + 1, pl.ds(qs, tq)])
                p = jnp.where((qb > kb) | (cid >= rid), p, 0.0)
                dp = lax.dot_general(vm, doblk, nt, preferred_element_type=F32)
                ds = p * (dp - dl_ref[hh:hh + 1, pl.ds(qs, tq)])
                ds_bf = ds.astype(BF16)
                dv_acc = dv_acc + jnp.dot(p.astype(BF16), doblk, preferred_element_type=F32)
                dk_acc = dk_acc + jnp.dot(ds_bf, qblk, preferred_element_type=F32)
                dc_acc = dc_acc + jnp.sum(ds, axis=-1, keepdims=True)
                dq_ref[pl.ds(qs, tq), :] += lax.dot_general(ds_bf, km, tn_dims, preferred_element_type=F32)
                dcq_ref[hh:hh + 1, pl.ds(qs, tq)] += jnp.sum(ds, axis=0, keepdims=True)
                return dk_acc, dv_acc, dc_acc

            init = (jnp.zeros((tk, LANES), F32), jnp.zeros((tk, LANES), F32), jnp.zeros((tk, 1), F32))
            dk_acc, dv_acc, dc_acc = lax.fori_loop(kb, nq, step, init)
            dks.append(dk_acc)
            dvs.append(dv_acc)
            dcs.append(dc_acc)
        dk_ref[...] = jnp.where(lane < GROUP_DIM, dks[0], dks[1])
        dv_ref[...] = jnp.where(lane < GROUP_DIM, dvs[0], dvs[1])
        dcs_ref[...] = jnp.where(lane == 0, -dcs[0], jnp.where(lane == 1, -dcs[1], 0.0))

    return pl.pallas_call(
        body,
        out_shape=(jax.ShapeDtypeStruct((S, ATT_WIDTH), F32), jax.ShapeDtypeStruct((S, ATT_WIDTH), F32),
                   jax.ShapeDtypeStruct((S, ATT_WIDTH), F32), jax.ShapeDtypeStruct((N_PAIRS, S, LANES), F32),
                   jax.ShapeDtypeStruct((N_PAIRS, 8, S), F32)),
        grid=(N_PAIRS, nq),
        in_specs=[pl.BlockSpec((S, LANES), lambda p, j: (0, p)),
                  pl.BlockSpec((S, LANES), lambda p, j: (0, p)),
                  pl.BlockSpec((tk, LANES), lambda p, j: (j, N_PAIRS + p)),
                  pl.BlockSpec((tk, LANES), lambda p, j: (j, 2 * N_PAIRS + p)),
                  pl.BlockSpec((None, tk, LANES), lambda p, j: (p, j, 0)),
                  pl.BlockSpec((None, 8, S), lambda p, j: (p, 0, 0)),
                  pl.BlockSpec((None, 8, S), lambda p, j: (p, 0, 0)),
                  pl.BlockSpec((None, 8, S), lambda p, j: (p, 0, 0))],
        out_specs=(pl.BlockSpec((S, LANES), lambda p, j: (0, p)),
                   pl.BlockSpec((tk, LANES), lambda p, j: (j, p)),
                   pl.BlockSpec((tk, LANES), lambda p, j: (j, p)),
                   pl.BlockSpec((None, tk, LANES), lambda p, j: (p, j, 0)),
                   pl.BlockSpec((None, 8, S), lambda p, j: (p, 0, 0))),
        name="attn_bwd", compiler_params=_cp(("parallel", "arbitrary"), VMEM_LIMIT),
    )(qkv, do_bf, qkv, qkv, c_cols, c_rows, lse_rows, dl_rows)


def _gmlp_bwd(ug, dsg, gain, w_s, wt_s, bias_full):
    S = ug.shape[0]
    tm = _pick(S, (512, 256, 128))
    n_chunks = tm // CHUNK
    n_i = S // tm
    ones = _group_ones()
    nt = (((1,), (1,)), ((), ()))

    def body(ug_ref, dsg_ref, gain_ref, w_ref, wt_ref, bias_ref, ones_ref, dug_ref, dw_ref, dgain_ref, dbias_ref,
             dbacc_ref):
        i = pl.program_id(0)

        @pl.when(i == 0)
        def _():
            dw_ref[...] = jnp.zeros_like(dw_ref)
            dgain_ref[...] = jnp.zeros_like(dgain_ref)
            dbacc_ref[...] = jnp.zeros_like(dbacc_ref)

        ones_m = ones_ref[...]
        pu = ug_ref[:, :GMLP_WIDTH]
        pg = ug_ref[:, GMLP_WIDTH:]
        u = _gelu(pu)
        vr = _gelu(pg)
        ms = _dot3(vr * vr, ones_m) * (1.0 / GROUP_DIM)
        rinv = lax.rsqrt(ms + EPS)
        vhat = vr * rinv
        gain_v = gain_ref[...]
        vn = (vhat * gain_v).astype(BF16)
        mixed = _gmlp_mixed(vn, w_ref, bias_ref[...], n_chunks)
        dsg_v = dsg_ref[...]
        du = dsg_v * mixed
        dmixed = dsg_v * u
        dm_bf = dmixed.astype(BF16)
        lane = lax.broadcasted_iota(jnp.int32, (CHUNK, LANES), 1)
        row = lax.broadcasted_iota(jnp.int32, (CHUNK, CHUNK), 0)
        col = lax.broadcasted_iota(jnp.int32, (CHUNK, CHUNK), 1)
        wts = [jnp.where(col >= row, wt_ref[g], 0.0).astype(BF16) for g in range(N_GROUPS)]
        dvn_rows = []
        dbsum = jnp.zeros((CHUNK, GMLP_WIDTH), F32)
        for ci in range(n_chunks):
            rs = slice(ci * CHUNK, (ci + 1) * CHUNK)
            dbsum = dbsum + dmixed[rs, :]
            cols = []
            for pp in range(N_GROUPS // 2):
                cs = slice(pp * LANES, (pp + 1) * LANES)
                dm = dm_bf[rs, cs]
                dm_lo = jnp.where(lane < GROUP_DIM, dm, jnp.zeros_like(dm))
                dm_hi = jnp.where(lane >= GROUP_DIM, dm, jnp.zeros_like(dm))
                vb = vn[rs, cs]
                dw_ref[2 * pp] += lax.dot_general(dm_lo, vb, nt, preferred_element_type=F32)
                dw_ref[2 * pp + 1] += lax.dot_general(dm_hi, vb, nt, preferred_element_type=F32)
                cols.append(jnp.dot(wts[2 * pp], dm_lo, preferred_element_type=F32)
                            + jnp.dot(wts[2 * pp + 1], dm_hi, preferred_element_type=F32))
            dvn_rows.append(jnp.concatenate(cols, axis=1))
        dvn = jnp.concatenate(dvn_rows, axis=0)
        dbacc_ref[...] += dbsum
        dgain_ref[0:1, :] += jnp.sum(dvn * vhat, axis=0, keepdims=True)
        dvhat = dvn * gain_v
        gm = _dot3(dvhat * vhat, ones_m) * (1.0 / GROUP_DIM)
        dvr = rinv * (dvhat - vhat * gm)
        dug_ref[:, :GMLP_WIDTH] = (du * _gelu_grad(pu)).astype(BF16)
        dug_ref[:, GMLP_WIDTH:] = (dvr * _gelu_grad(pg)).astype(BF16)

        @pl.when(i == n_i - 1)
        def _():
            for g in range(N_GROUPS):
                dw_ref[g] = jnp.where(row >= col, dw_ref[g], 0.0)
            dbias_ref[...] = _dot3(dbacc_ref[...], ones_m)

    return pl.pallas_call(
        body,
        out_shape=(jax.ShapeDtypeStruct((S, 2 * GMLP_WIDTH), BF16), jax.ShapeDtypeStruct((N_GROUPS, CHUNK, CHUNK), F32),
                   jax.ShapeDtypeStruct((8, GMLP_WIDTH), F32), jax.ShapeDtypeStruct((CHUNK, GMLP_WIDTH), F32)),
        grid=(n_i,),
        in_specs=[pl.BlockSpec((tm, 2 * GMLP_WIDTH), lambda i: (i, 0)), pl.BlockSpec((tm, GMLP_WIDTH), lambda i: (i, 1)),
                  pl.BlockSpec((1, GMLP_WIDTH), lambda i: (0, 0)),
                  pl.BlockSpec((N_GROUPS, CHUNK, CHUNK), lambda i: (0, 0, 0)),
                  pl.BlockSpec((N_GROUPS, CHUNK, CHUNK), lambda i: (0, 0, 0)),
                  pl.BlockSpec((CHUNK, GMLP_WIDTH), lambda i: (0, 0)),
                  pl.BlockSpec((GMLP_WIDTH, GMLP_WIDTH), lambda i: (0, 0))],
        out_specs=(pl.BlockSpec((tm, 2 * GMLP_WIDTH), lambda i: (i, 0)),
                   pl.BlockSpec((N_GROUPS, CHUNK, CHUNK), lambda i: (0, 0, 0)),
                   pl.BlockSpec((8, GMLP_WIDTH), lambda i: (0, 0)),
                   pl.BlockSpec((CHUNK, GMLP_WIDTH), lambda i: (0, 0))),
        scratch_shapes=[pltpu.VMEM((CHUNK, GMLP_WIDTH), F32)],
        name="gmlp_bwd", compiler_params=_cp(("arbitrary",), VMEM_LIMIT),
    )(ug, dsg, gain, w_s, wt_s, bias_full, ones)


def _gate_bwd(dcq, dck, zf):
    S = zf.shape[0]
    tm = _pick(S, (256,))
    n_i = S // tm
    triu = (lax.broadcasted_iota(jnp.int32, (tm, tm), 0) <= lax.broadcasted_iota(jnp.int32, (tm, tm), 1)).astype(BF16)

    def body(dcq_ref, dck_ref, zf_ref, tri_ref, dzf_ref, dbf_ref, carry_ref):
        i = pl.program_id(0)

        @pl.when(i == 0)
        def _():
            carry_ref[...] = jnp.zeros_like(carry_ref)
            dbf_ref[...] = jnp.zeros_like(dbf_ref)

        lane = lax.broadcasted_iota(jnp.int32, (tm, LANES), 1)
        dc = jnp.zeros((tm, LANES), F32)
        for p in range(N_PAIRS):
            slab = dcq_ref[p] + dck_ref[p]
            for hh in range(2):
                dc = dc + jnp.where(lane == 2 * p + hh, slab[:, hh:hh + 1], 0.0)
        dlf = _dot3l(tri_ref[...], dc) + carry_ref[0:1, :]
        carry_ref[0:1, :] = dlf[0:1, :]
        dz = jnp.where(lane < N_HEADS, dlf * _sigmoid(-zf_ref[...]), 0.0)
        dzf_ref[...] = dz.astype(BF16)
        dbf_ref[0:1, :] += jnp.sum(dz, axis=0, keepdims=True)

    return pl.pallas_call(
        body,
        out_shape=(jax.ShapeDtypeStruct((S, LANES), BF16), jax.ShapeDtypeStruct((8, LANES), F32)),
        grid=(n_i,),
        in_specs=[pl.BlockSpec((N_PAIRS, tm, LANES), lambda i: (0, n_i - 1 - i, 0)),
                  pl.BlockSpec((N_PAIRS, tm, LANES), lambda i: (0, n_i - 1 - i, 0)),
                  pl.BlockSpec((tm, LANES), lambda i: (n_i - 1 - i, 0)),
                  pl.BlockSpec((tm, tm), lambda i: (0, 0))],
        out_specs=(pl.BlockSpec((tm, LANES), lambda i: (n_i - 1 - i, 0)), pl.BlockSpec((8, LANES), lambda i: (0, 0))),
        scratch_shapes=[pltpu.VMEM((8, LANES), F32)],
        name="gate_bwd", compiler_params=_cp(("arbitrary",), VMEM_LIMIT),
    )(dcq, dck, zf, triu)


def _out_proj_fwd(att_bf, sg, w_out_bf, x, g_ffn):
    S = x.shape[0]
    tm = _pick(S, (512, 256))

    def body(a_ref, s_ref, w_ref, x_ref, g_ref, h_ref, hn_ref):
        h = (x_ref[...] + jnp.dot(a_ref[...], w_ref[:ATT_WIDTH, :], preferred_element_type=F32)
             + jnp.dot(s_ref[...], w_ref[ATT_WIDTH:, :], preferred_element_type=F32))
        h_ref[...] = h
        r = lax.rsqrt(jnp.mean(h * h, axis=-1, keepdims=True) + EPS)
        hn_ref[...] = ((h * r) * g_ref[...]).astype(BF16)

    row = pl.BlockSpec((tm, D_MODEL), lambda i: (i, 0))
    half = pl.BlockSpec((tm, ATT_WIDTH), lambda i: (i, 0))
    return pl.pallas_call(
        body, out_shape=(jax.ShapeDtypeStruct((S, D_MODEL), F32), jax.ShapeDtypeStruct((S, D_MODEL), BF16)),
        grid=(S // tm,),
        in_specs=[half, half, pl.BlockSpec((D_MODEL, D_MODEL), lambda i: (0, 0)), row,
                  pl.BlockSpec((1, D_MODEL), lambda i: (0, 0))],
        out_specs=(row, row), name="out_proj", compiler_params=_cp(("parallel",), VMEM_LIMIT),
    )(att_bf, sg, w_out_bf, x, g_ffn)


_IN_PIECES = ((0, ATT_WIDTH), (ATT_WIDTH, ATT_WIDTH), (2 * ATT_WIDTH, ATT_WIDTH), (QKV, 2 * GMLP_WIDTH), (UG_END, LANES))


def _inproj_bwd_dx(pieces, w_pad, x, g_mix, dh1):
    S = x.shape[0]
    tm = _pick(S, (512, 256))

    def body(*refs):
        p_refs, (w_ref, x_ref, g_ref, r_ref, dx_ref, dg_ref) = refs[:5], refs[5:]
        i = pl.program_id(0)

        @pl.when(i == 0)
        def _():
            dg_ref[...] = jnp.zeros_like(dg_ref)

        dxn = None
        for p_ref, (c0, width) in zip(p_refs, _IN_PIECES):
            part = jnp.dot(p_ref[...], w_ref[c0:c0 + width, :], preferred_element_type=F32)
            dxn = part if dxn is None else dxn + part
        xf = x_ref[...]
        r = lax.rsqrt(jnp.mean(xf * xf, axis=-1, keepdims=True) + EPS)
        xhat = xf * r
        dg_ref[0:1, :] += jnp.sum(dxn * xhat, axis=0, keepdims=True)
        dhat = dxn * g_ref[...]
        dx_ref[...] = r_ref[...] + r * (dhat - xhat * jnp.mean(dhat * xhat, axis=-1, keepdims=True))

    row = pl.BlockSpec((tm, D_MODEL), lambda i: (i, 0))
    return pl.pallas_call(
        body, out_shape=(jax.ShapeDtypeStruct((S, D_MODEL), F32), jax.ShapeDtypeStruct((8, D_MODEL), F32)),
        grid=(S // tm,),
        in_specs=[pl.BlockSpec((tm, width), lambda i: (i, 0)) for _, width in _IN_PIECES]
        + [pl.BlockSpec((IN_PAD, D_MODEL), lambda i: (0, 0)), row, pl.BlockSpec((1, D_MODEL), lambda i: (0, 0)), row],
        out_specs=(row, pl.BlockSpec((8, D_MODEL), lambda i: (0, 0))),
        name="in_proj_dx", compiler_params=_cp(("arbitrary",), VMEM_LIMIT),
    )(*pieces, w_pad, x, g_mix, dh1)


def _inproj_bwd_dw(xn, pieces):
    S = xn.shape[0]
    tk = _pick(S, (512, 256))

    def body(*refs):
        x_ref, p_refs, o_ref = refs[0], refs[1:6], refs[6]
        k = pl.program_id(0)

        @pl.when(k == 0)
        def _():
            o_ref[...] = jnp.zeros_like(o_ref)

        xb = x_ref[...]
        for p_ref, (c0, width) in zip(p_refs, _IN_PIECES):
            o_ref[:, c0:c0 + width] += lax.dot_general(xb, p_ref[...], _TN, preferred_element_type=F32)

    return pl.pallas_call(
        body, out_shape=jax.ShapeDtypeStruct((D_MODEL, IN_PAD), F32), grid=(S // tk,),
        in_specs=[pl.BlockSpec((tk, D_MODEL), lambda k: (k, 0))]
        + [pl.BlockSpec((tk, width), lambda k: (k, 0)) for _, width in _IN_PIECES],
        out_specs=pl.BlockSpec((D_MODEL, IN_PAD), lambda k: (0, 0)),
        name="in_proj_dw", compiler_params=_cp(("arbitrary",), VMEM_LIMIT),
    )(xn, *pieces)


def _adamw(w, m, v, parts, name):
    R, C = w.shape[-2:]
    tr = R
    for cand in (256, 128, 64, 32, 16, 8):
        if R % cand == 0 and R > cand:
            tr = cand
            break
    c1 = 1.0 / (1.0 - ADAM_B1 ** ADAM_STEP)
    c2 = 1.0 / (1.0 - ADAM_B2 ** ADAM_STEP)

    def body(w_ref, m_ref, v_ref, p_ref, g_ref, d_ref, nm_ref, nv_ref):
        g = p_ref[0].astype(F32)
        for j in range(1, N_DEV):
            g = g + p_ref[j].astype(F32)
        g_ref[...] = g
        nm = ADAM_B1 * m_ref[...] + (1.0 - ADAM_B1) * g
        nv = ADAM_B2 * v_ref[...] + (1.0 - ADAM_B2) * (g * g)
        nm_ref[...] = nm
        nv_ref[...] = nv
        d_ref[...] = -ADAM_LR * ((nm * c1) / (jnp.sqrt(nv * c2) + ADAM_EPS) + ADAM_WD * w_ref[...])

    if w.ndim == 3:
        spec = pl.BlockSpec((None, tr, C), lambda i: (0, i, 0))
    else:
        spec = pl.BlockSpec((tr, C), lambda i: (i, 0))
    shp = jax.ShapeDtypeStruct(w.shape, F32)
    return pl.pallas_call(
        body, out_shape=(shp, shp, shp, shp), grid=(R // tr,),
        in_specs=[spec, spec, spec, pl.BlockSpec((N_DEV, tr, C), lambda i: (0, i, 0))],
        out_specs=(spec, spec, spec, spec),
        name=name, compiler_params=_cp(("parallel",), VMEM_LIMIT),
    )(w, m, v, parts)


def _place():
    x, y, c = lax.axis_index("x"), lax.axis_index("y"), lax.axis_index("c")
    return x, y, c


def _all_gather(blocks, name):
    n = len(blocks)

    def body(*refs):
        ins, outs = refs[:n], refs[n:2 * n]
        send_sems, recv_sems, local_sems = refs[2 * n:]
        x, y, c = _place()
        me, sibling = (x, y, c), (x, y, 1 - c)
        chips = [(1 - x, y), (x, 1 - y), (1 - x, 1 - y)]
        sends = []
        for a in range(n):
            out = outs[a]

            def slot(px, py, pc, out=out):
                return out.at[4 * px + 2 * py + pc]

            def copy(k, block, to, src=None, a=a, slot=slot):
                return pltpu.make_async_remote_copy(
                    src_ref=slot(*block) if src is None else src, dst_ref=slot(*block),
                    send_sem=send_sems.at[a, k], recv_sem=recv_sems.at[a, k], device_id=to, device_id_type=MESH)

            mine = pltpu.make_async_copy(ins[a], slot(*me), local_sems.at[a])
            mine.start()
            first = [copy(0, me, sibling, src=ins[a])]
            first += [copy(1 + j, me, (*chip, c), src=ins[a]) for j, chip in enumerate(chips)]
            for cp in first:
                cp.start()
            sends.append((mine, first, copy))
        for a in range(n):
            mine, first, copy = sends[a]
            passed = [copy(4 + j, (*chip, c), sibling) for j, chip in enumerate(chips)]
            for j, chip in enumerate(chips):
                copy(1 + j, (*chip, c), me).wait_recv()
                passed[j].start()
            copy(0, sibling, me).wait_recv()
            for j, chip in enumerate(chips):
                copy(4 + j, (*chip, 1 - c), me).wait_recv()
            for cp in first + passed:
                cp.wait_send()
            mine.wait()

    any_spec = pl.BlockSpec(memory_space=pl.ANY)
    return pl.pallas_call(
        body, out_shape=tuple(jax.ShapeDtypeStruct((N_DEV,) + b.shape, b.dtype) for b in blocks),
        in_specs=[any_spec] * n, out_specs=tuple([any_spec] * n),
        scratch_shapes=[pltpu.SemaphoreType.DMA((n, 7)), pltpu.SemaphoreType.DMA((n, 7)), pltpu.SemaphoreType.DMA((n,))],
        name=name,
    )(*blocks)


def _exchange_shards(parts, name):
    n = len(parts)

    def body(*refs):
        ins, outs = refs[:n], refs[n:2 * n]
        send_sems, recv_sems, local_sems = refs[2 * n:]
        x, y, c = _place()
        me = 4 * x + 2 * y + c
        started = []
        for a in range(n):
            mine = pltpu.make_async_copy(ins[a].at[me], outs[a].at[me], local_sems.at[a])
            mine.start()
            started.append(mine)
            for k in range(1, N_DEV):
                px, py, pc = x ^ ((k >> 2) & 1), y ^ ((k >> 1) & 1), c ^ (k & 1)
                cp = pltpu.make_async_remote_copy(
                    src_ref=ins[a].at[4 * px + 2 * py + pc], dst_ref=outs[a].at[me],
                    send_sem=send_sems.at[a, k - 1], recv_sem=recv_sems.at[a, k - 1],
                    device_id=(px, py, pc), device_id_type=MESH)
                cp.start()
                started.append(cp)
        for cp in started:
            cp.wait()

    any_spec = pl.BlockSpec(memory_space=pl.ANY)
    return pl.pallas_call(
        body, out_shape=tuple(jax.ShapeDtypeStruct(p.shape, p.dtype) for p in parts),
        in_specs=[any_spec] * n, out_specs=tuple([any_spec] * n),
        scratch_shapes=[pltpu.SemaphoreType.DMA((n, 7)), pltpu.SemaphoreType.DMA((n, 7)), pltpu.SemaphoreType.DMA((n,))],
        name=name,
    )(*parts)


_HBM = pl.BlockSpec(memory_space=pltpu.HBM)
_SEM = pl.BlockSpec(memory_space=pltpu.SEMAPHORE)
_EFFECT = pltpu.SideEffectType.DATAFLOW_SIDE_EFFECTING


def _peers(x, y, c):
    out = []
    for k in range(1, N_DEV):
        px, py, pc = x ^ ((k >> 2) & 1), y ^ ((k >> 1) & 1), c ^ (k & 1)
        out.append((k, (px, py, pc), 4 * px + 2 * py + pc))
    return out


def _xchg_copies(src_refs, land_refs, send_sems, recv_sems, scatter):
    x, y, c = _place()
    me = 4 * x + 2 * y + c
    copies = []
    for a, (src, land) in enumerate(zip(src_refs, land_refs)):
        for k, place, idx in _peers(x, y, c):
            j = a * (N_DEV - 1) + k - 1
            copies.append(pltpu.make_async_remote_copy(
                src_ref=src.at[idx] if scatter[a] else src, dst_ref=land.at[me],
                send_sem=send_sems[j], recv_sem=recv_sems[j], device_id=place, device_id_type=MESH))
    return copies


def _xchg_start(srcs, scatter, name):
    n = len(srcs)
    lands = [lax.empty((N_DEV,) + (s.shape[1:] if sc else s.shape), s.dtype) for s, sc in zip(srcs, scatter)]

    ns = n * (N_DEV - 1)

    def body(*refs):
        sems = refs[2 * n:2 * n + 2 * ns]
        for cp in _xchg_copies(refs[:n], refs[n:2 * n], sems[:ns], sems[ns:], scatter):
            cp.start()
        token = refs[-1]
        token[...] = jnp.zeros_like(token)

    both = list(srcs) + lands
    res = pl.pallas_call(
        body, name=name,
        out_shape=(*[pltpu.SemaphoreType.DMA(())] * (2 * ns),
                   *[pltpu.HBM(a.shape, a.dtype) for a in both], jax.ShapeDtypeStruct((8, LANES), F32)),
        in_specs=[_HBM] * (2 * n),
        out_specs=(*([_SEM] * (2 * ns)), *([_HBM] * (2 * n)), pl.BlockSpec(memory_space=pltpu.VMEM)),
        input_output_aliases={i: 2 * ns + i for i in range(2 * n)},
        compiler_params=pltpu.CompilerParams(has_side_effects=_EFFECT),
    )(*[pltpu.with_memory_space_constraint(a, pltpu.HBM) for a in both])
    return (tuple(res[:2 * ns]), tuple(res[2 * ns:2 * ns + 2 * n])), res[-1]


def _xchg_wait(handle, scatter, after, name):
    sems, thru = handle
    n = len(thru) // 2
    ns = len(sems) // 2

    def body(*refs):
        got = refs[2 * n:2 * n + 2 * ns]
        for cp in _xchg_copies(refs[:n], refs[n:2 * n], got[:ns], got[ns:], scatter):
            cp.wait_send()
            cp.wait_recv()

    outs = pl.pallas_call(
        body, name=name, out_shape=tuple(pltpu.HBM(a.shape, a.dtype) for a in thru),
        in_specs=[_HBM] * (2 * n) + [_SEM] * (2 * ns) + [pl.BlockSpec(memory_space=pl.ANY)],
        out_specs=tuple([_HBM] * (2 * n)), input_output_aliases={i: i for i in range(2 * n)},
        compiler_params=pltpu.CompilerParams(has_side_effects=_EFFECT),
    )(*thru, *sems, after)
    return outs[:n], outs[n:]


def _tie(a, token):
    return a if token is None else a + token[0, 0]


def _rows128(a):
    flat = a.reshape(-1)
    rows = -(-flat.shape[0] // LANES)
    rows = -(-rows // 8) * 8
    return jnp.pad(flat, (0, rows * LANES - flat.shape[0])).reshape(rows, LANES)


def _local_step(x, target, norm_mix_g, w_in_t, b_forget, gmlp_norm_g, w_spatial, b_spatial, norm_ffn_g, conv_b,
                norm_final_g, rest_fn, send_fn, token=None):
    f = D_FF
    g_mix = norm_mix_g.reshape(1, D_MODEL)
    w_pad = jnp.pad(w_in_t, ((0, IN_PAD - IN_COLS), (0, 0)))
    bf_pad = jnp.pad(b_forget.reshape(1, N_HEADS), ((0, 0), (0, LANES - N_HEADS)))
    xn, qa, ka, va, ug, zf = _inproj_fwd(x, _tie(g_mix, token), w_pad, bf_pad)
    bias_full = jnp.repeat(b_spatial.reshape(N_GROUPS, CHUNK).T, GROUP_DIM, axis=1)
    w_s = w_spatial.reshape(N_GROUPS, CHUNK, CHUNK)
    gain = gmlp_norm_g.reshape(1, GMLP_WIDTH)
    sg = _gmlp_fwd(ug, gain, w_s, bias_full)
    att, lse, att_bf = _attn_fwd(qa, ka, va)
    w_out_bf, w_up_bf, conv_w, w_down_bf = rest_fn(att_bf)
    g_ffn = norm_ffn_g.reshape(1, D_MODEL)
    h1, hn = _out_proj_fwd(att_bf, sg, w_out_bf, x, g_ffn)
    cw = jnp.pad(conv_w.reshape(3, 2, f).transpose(1, 0, 2), ((0, 0), (0, 5), (0, 0)))
    cb = conv_b.reshape(2, 1, f)
    hu, hc, act = _ffn_up_conv(hn, w_up_bf, cw, cb)
    loss_blk, dh2, dh2_bf, dg_final = _ffn_down_loss(act, w_down_bf, h1, norm_final_g.reshape(1, D_MODEL), target)
    dw_down = _mm(act, dh2_bf, mode="tn", out_dtype=F32, tm=1408, tn=1024, tk=2048, name="ffn_down_dw")
    dact = _mm(dh2_bf, w_down_bf, mode="nt", out_dtype=F32, tm=512, tn=1408, tk=1024, outer="j", name="ffn_down_dx")
    dhu, dcw = _conv_gate_bwd(hc, hu, dact, _tie(cw, send_fn("w_down", dw_down)))
    dw_up = _mm(hn, dhu, mode="tn", out_dtype=F32, tm=1024, tn=1408, tk=2048, b_halves=True, outer="j", name="ffn_up_dw")
    dh1, dh1_bf, dg_ffn = _ffn_up_dx_rms(dhu, w_up_bf, h1, _tie(g_ffn, send_fn("w_up", dw_up)), dh2)
    dmix = _mm(dh1_bf, w_out_bf, mode="nt", out_dtype=F32, tm=512, tn=1024, tk=1024, name="out_proj_dx")
    dw_out = jnp.concatenate(
        [_mm(att_bf, dh1_bf, mode="tn", out_dtype=F32, tm=512, tn=1024, tk=1024, name="out_proj_dw_att"),
         _mm(sg, dh1_bf, mode="tn", out_dtype=F32, tm=512, tn=1024, tk=1024, name="out_proj_dw_sg")], axis=0)
    qb, doa = _attn_prep(att, lse, dmix, qa)
    dq, dk, dv, dcq, dck = _attn_bwd(qb, ka, va, doa)
    wt_s = w_s.transpose(0, 2, 1)
    dug, dw_s, dgain, dbias = _gmlp_bwd(ug, dmix, _tie(gain, send_fn("w_out", dw_out)), w_s, wt_s, bias_full)
    dzf, dbf = _gate_bwd(dcq, dck, zf)
    pieces = (dq, dk, dv, dug, dzf)
    dw_in = _inproj_bwd_dw(xn, pieces)
    grad_x, dg_mix = _inproj_bwd_dx(pieces, w_pad, x, _tie(g_mix, send_fn("w_in", dw_in[:, :IN_COLS])), dh1)
    grads = dict(
        norm_mix_g=dg_mix[0:1, :],
        b_forget=dbf[0:1, :N_HEADS],
        gmlp_norm_g=dgain[0:1, :],
        w_spatial=dw_s,
        b_spatial=dbias[:, ::GROUP_DIM].T,
        norm_ffn_g=dg_ffn[0:1, :],
        conv_w=dcw[:, 0:3, :].transpose(1, 0, 2).reshape(3, 2 * f),
        conv_b=dcw[:, 3, :].reshape(1, 2 * f),
        norm_final_g=dg_final[0, :],
    )
    return loss_blk[0, 0], grad_x, grads


SMALL = ("norm_mix_g", "b_forget", "gmlp_norm_g", "w_spatial", "b_spatial", "norm_ffn_g", "conv_b", "norm_final_g")


def kernel(x, norm_mix_g, w_in, b_forget, gmlp_norm_g, w_spatial, b_spatial, w_out, norm_ffn_g, w_up, conv_w, conv_b, w_down, norm_final_g, loss_target, m_norm_mix_g, m_w_in, m_b_forget, m_gmlp_norm_g, m_w_spatial, m_b_spatial, m_w_out, m_norm_ffn_g, m_w_up, m_conv_w, m_conv_b, m_w_down, m_norm_final_g, v_norm_mix_g, v_w_in, v_b_forget, v_gmlp_norm_g, v_w_spatial, v_b_spatial, v_w_out, v_norm_ffn_g, v_w_up, v_conv_w, v_conv_b, v_w_down, v_norm_final_g):
    weights = dict(norm_mix_g=norm_mix_g, w_in=w_in, b_forget=b_forget, gmlp_norm_g=gmlp_norm_g, w_spatial=w_spatial,
                   b_spatial=b_spatial, w_out=w_out, norm_ffn_g=norm_ffn_g, w_up=w_up, conv_w=conv_w, conv_b=conv_b,
                   w_down=w_down, norm_final_g=norm_final_g)
    m_in = dict(norm_mix_g=m_norm_mix_g, w_in=m_w_in, b_forget=m_b_forget, gmlp_norm_g=m_gmlp_norm_g,
                w_spatial=m_w_spatial, b_spatial=m_b_spatial, w_out=m_w_out, norm_ffn_g=m_norm_ffn_g, w_up=m_w_up,
                conv_w=m_conv_w, conv_b=m_conv_b, w_down=m_w_down, norm_final_g=m_norm_final_g)
    v_in = dict(norm_mix_g=v_norm_mix_g, w_in=v_w_in, b_forget=v_b_forget, gmlp_norm_g=v_gmlp_norm_g,
                w_spatial=v_w_spatial, b_spatial=v_b_spatial, w_out=v_w_out, norm_ffn_g=v_norm_ffn_g, w_up=v_w_up,
                conv_w=v_conv_w, conv_b=v_conv_b, w_down=v_w_down, norm_final_g=v_norm_final_g)
    order = list(weights)
    me = 4 * lax.axis_index("x") + 2 * lax.axis_index("y") + lax.axis_index("c")
    n_in, n_up = w_in.shape[2], w_up.shape[2]
    r_out, r_down = w_out.shape[1], w_down.shape[1]

    def with_mine(landed, mine):
        return lax.dynamic_update_index_in_dim(landed, mine, me, 0)

    up_blk = w_up[0].T.astype(BF16)
    out_blk = w_out[0].astype(BF16)
    down_blk = w_down[0].astype(BF16)
    taps_blk = jnp.pad(conv_w[0], ((0, 5), (0, 0)))
    (in_all,) = _all_gather([w_in[0].T.astype(BF16)], "gather_w_in")
    in_all, rest_blocks = lax.optimization_barrier((in_all, [up_blk, out_blk, down_blk, taps_blk]))
    rest_handle, token = _xchg_start(rest_blocks, [False] * 4, "gather_rest_start")
    w_in_t = in_all.reshape(N_DEV * n_in, D_MODEL)

    def rest_fn(after):
        mine, landed = _xchg_wait(rest_handle, [False] * 4, after, "gather_rest_wait")
        up_all, out_all, down_all, taps_all = [with_mine(l, b) for l, b in zip(landed, mine)]
        return (out_all.reshape(N_DEV * r_out, D_MODEL), up_all.reshape(N_DEV * n_up, D_MODEL),
                taps_all[:, :3, :].transpose(1, 0, 2).reshape(3, N_DEV * n_up),
                down_all.reshape(N_DEV * r_down, D_MODEL))

    sent = {}

    def send_fn(name, grad):
        if name == "w_in":
            parts = grad.reshape(D_MODEL, N_DEV, -1).transpose(1, 0, 2).astype(BF16)
        elif name == "w_up":
            parts = grad.reshape(D_MODEL, N_DEV, -1).transpose(1, 0, 2)
        else:
            parts = grad.reshape(N_DEV, -1, D_MODEL)
        sent[name], tok = _xchg_start([parts], [True], "scatter_" + name + "_start")
        return tok

    loss_local, grad_x, g = _local_step(
        x[0], loss_target[0], norm_mix_g, w_in_t, b_forget, gmlp_norm_g, w_spatial, b_spatial, norm_ffn_g, conv_b,
        norm_final_g, rest_fn, send_fn, token)
    loss = lax.psum(loss_local, ("x", "y", "c"))

    got = {}
    for name, handle in sent.items():
        (parts,), (landed,) = _xchg_wait(handle, [True], grad_x, "scatter_" + name + "_wait")
        got[name] = with_mine(landed, lax.dynamic_index_in_dim(parts, me, 0, keepdims=False))
    got_in, got_up, got_out, got_down = got["w_in"], got["w_up"], got["w_out"], got["w_down"]

    small_names = SMALL + ("conv_w",)
    packed = [_rows128(g[k]) for k in small_names]
    sizes = [p.shape[0] for p in packed]
    (small_all,) = _all_gather([jnp.concatenate(packed, axis=0)], "gather_small_grads")

    def pack(src):
        return jnp.concatenate([_rows128(src[k]) for k in SMALL], axis=0)

    n_small_rows = sum(sizes[:-1])
    sg_, sd_, sm_, sv_ = _adamw(pack(weights), pack(m_in), pack(v_in), small_all[:, :n_small_rows, :], "adamw_small")

    outs = {}
    off = 0
    for k, rows in zip(SMALL, sizes[:-1]):
        shp = weights[k].shape
        cnt = math.prod(shp)
        outs[k] = tuple(a[off:off + rows].reshape(-1)[:cnt].reshape(shp) for a in (sg_, sd_, sm_, sv_))
        off += rows
    taps_parts = small_all[:, n_small_rows:, :].reshape(N_DEV, -1)[:, :3 * N_DEV * n_up].reshape(N_DEV, 3, N_DEV * n_up)
    taps_mine = lax.dynamic_slice_in_dim(taps_parts, me * n_up, n_up, axis=2)
    taps_mine = jnp.pad(taps_mine, ((0, 0), (0, 5), (0, 0)))

    def pad8(a):
        return jnp.pad(a[0], ((0, 5), (0, 0)))

    res = _adamw(pad8(conv_w), pad8(m_conv_w), pad8(v_conv_w), taps_mine, "adamw_conv_w")
    outs["conv_w"] = tuple(a[:3][None] for a in res)
    for k, got in (("w_in", got_in), ("w_up", got_up), ("w_out", got_out), ("w_down", got_down)):
        outs[k] = tuple(_adamw(weights[k], m_in[k], v_in[k], got, "adamw_" + k))

    return (loss, grad_x[None], *[outs[k][0] for k in order], *[outs[k][1] for k in order],
            *[outs[k][2] for k in order], *[outs[k][3] for k in order])
```

```python
import functools
import math

import jax
import jax.numpy as jnp
from jax import lax
from jax.experimental import pallas as pl
from jax.experimental.pallas import tpu as pltpu

F32 = jnp.float32
BF16 = jnp.bfloat16

N_DEV = 8
D_MODEL = 1024
ATT_WIDTH = 512
GMLP_WIDTH = 512
HEAD_DIM = 64
N_HEADS = 8
N_PAIRS = 4
N_GROUPS = 8
GROUP_DIM = 64
CHUNK = 128
D_FF = 2816
IN_COLS = 2568
IN_PAD = 2688
QKV = 1536
UG_END = 2560
EPS = 1e-6
LANES = 128

ADAM_LR = 0.001
ADAM_B1 = 0.9
ADAM_B2 = 0.999
ADAM_EPS = 1e-08
ADAM_WD = 0.01
ADAM_STEP = 10

ATT_TQ = 1024
ATT_TK = 1024
FFN_TM, FFN_TN = 512, 1408
CONV_TM, CONV_TN = 256, 1408
VMEM_LIMIT = 56 * 1024 * 1024
MESH = pl.DeviceIdType.MESH


def _cp(sem, vmem=None):
    return pltpu.CompilerParams(dimension_semantics=sem, vmem_limit_bytes=vmem)


def _pick(n, prefs):
    for p in prefs:
        if n % p == 0:
            return p
    return n


def _split3(x):
    hi = x.astype(BF16)
    r1 = x - hi.astype(F32)
    mid = r1.astype(BF16)
    lo = (r1 - mid.astype(F32)).astype(BF16)
    return hi, mid, lo


def _dot3(x, ones_bf):
    hi, mid, lo = _split3(x)
    d = functools.partial(jnp.dot, preferred_element_type=F32)
    return d(hi, ones_bf) + d(mid, ones_bf) + d(lo, ones_bf)


def _dot3l(ones_bf, x):
    hi, mid, lo = _split3(x)
    d = functools.partial(jnp.dot, preferred_element_type=F32)
    return d(ones_bf, hi) + d(ones_bf, mid) + d(ones_bf, lo)


def _gelu(x):
    k = math.sqrt(2.0 / math.pi)
    t = jnp.tanh(k * (x + 0.044715 * (x * x * x)))
    return 0.5 * x * (1.0 + t)


def _gelu_grad(x):
    k = math.sqrt(2.0 / math.pi)
    x2 = x * x
    t = jnp.tanh(k * (x + 0.044715 * (x2 * x)))
    return 0.5 * (1.0 + t) + 0.5 * x * (1.0 - t * t) * (k * (1.0 + 3.0 * 0.044715 * x2))


def _sigmoid(x):
    return 1.0 / (1.0 + jnp.exp(-x))


def _mm(a, b, *, mode, out_dtype, tm, tn, tk, name, res=None, a_halves=False, b_halves=False,
        out_halves=False, outer="i"):
    if mode == "tn":
        K, M = a.shape[-2], a.shape[-1] * (2 if a_halves else 1)
    else:
        M, K = a.shape[-2], a.shape[-1] * (2 if a_halves else 1)
    if mode == "nt":
        N = b.shape[-2]
        assert b.shape[-1] == K
    else:
        N = b.shape[-1] * (2 if b_halves else 1)
    tm, tn, tk = min(tm, M), min(tn, N), min(tk, K)
    assert M % tm == 0 and N % tn == 0 and K % tk == 0, (name, M, N, K, tm, tn, tk)
    nm, nn, nk = M // tm, N // tn, K // tk

    def ij(g0, g1):
        return (g0, g1) if outer == "i" else (g1, g0)

    if mode == "nn":
        dims = (((1,), (0,)), ((), ()))
        if a_halves:
            nkh = nk // 2
            a_spec = pl.BlockSpec((None, tm, tk), lambda g0, g1, k: (k // nkh, ij(g0, g1)[0], k % nkh))
        else:
            a_spec = pl.BlockSpec((tm, tk), lambda g0, g1, k: (ij(g0, g1)[0], k))
        b_spec = pl.BlockSpec((tk, tn), lambda g0, g1, k: (k, ij(g0, g1)[1]))
    elif mode == "nt":
        dims = (((1,), (1,)), ((), ()))
        if a_halves:
            nkh = nk // 2
            a_spec = pl.BlockSpec((None, tm, tk), lambda g0, g1, k: (k // nkh, ij(g0, g1)[0], k % nkh))
        else:
            a_spec = pl.BlockSpec((tm, tk), lambda g0, g1, k: (ij(g0, g1)[0], k))
        b_spec = pl.BlockSpec((tn, tk), lambda g0, g1, k: (ij(g0, g1)[1], k))
    else:
        dims = (((0,), (0,)), ((), ()))
        if a_halves:
            nmh = nm // 2
            a_spec = pl.BlockSpec((None, tk, tm), lambda g0, g1, k: (ij(g0, g1)[0] // nmh, k, ij(g0, g1)[0] % nmh))
        else:
            a_spec = pl.BlockSpec((tk, tm), lambda g0, g1, k: (k, ij(g0, g1)[0]))
        if b_halves:
            nnh = nn // 2
            b_spec = pl.BlockSpec((None, tk, tn), lambda g0, g1, k: (ij(g0, g1)[1] // nnh, k, ij(g0, g1)[1] % nnh))
        else:
            b_spec = pl.BlockSpec((tk, tn), lambda g0, g1, k: (k, ij(g0, g1)[1]))
    if out_halves:
        nnh = nn // 2
        o_spec = pl.BlockSpec((None, tm, tn), lambda g0, g1, k: (ij(g0, g1)[1] // nnh, ij(g0, g1)[0], ij(g0, g1)[1] % nnh))
        o_shape = jax.ShapeDtypeStruct((2, M, N // 2), out_dtype)
    else:
        o_spec = pl.BlockSpec((tm, tn), lambda g0, g1, k: ij(g0, g1))
        o_shape = jax.ShapeDtypeStruct((M, N), out_dtype)
    in_specs = [a_spec, b_spec]
    args = [a, b]
    if res is not None:
        in_specs.append(pl.BlockSpec((tm, tn), lambda g0, g1, k: ij(g0, g1)))
        args.append(res)

    def body(*refs):
        if res is not None:
            a_ref, b_ref, r_ref, o_ref = refs[:4]
        else:
            a_ref, b_ref, o_ref = refs[:3]
            r_ref = None
        part = lax.dot_general(a_ref[...], b_ref[...], dims, preferred_element_type=F32)
        if nk == 1:
            if r_ref is not None:
                part = part + r_ref[...]
            o_ref[...] = part.astype(out_dtype)
            return
        acc_ref = refs[-1]
        k = pl.program_id(2)

        @pl.when(k == 0)
        def _():
            acc_ref[...] = part

        @pl.when(k > 0)
        def _():
            acc_ref[...] += part

        @pl.when(k == nk - 1)
        def _():
            tot = acc_ref[...]
            if r_ref is not None:
                tot = tot + r_ref[...]
            o_ref[...] = tot.astype(out_dtype)

    grid = (nm, nn, nk) if outer == "i" else (nn, nm, nk)
    scratch = [] if nk == 1 else [pltpu.VMEM((tm, tn), F32)]
    return pl.pallas_call(
        body, out_shape=o_shape, grid=grid, in_specs=in_specs, out_specs=o_spec, scratch_shapes=scratch,
        name=name, compiler_params=_cp(("parallel", "parallel", "arbitrary"), VMEM_LIMIT),
    )(*args)


def _aug(lane, terms):
    out = 0.0
    for j, t in enumerate(terms):
        out = jnp.where(lane == HEAD_DIM + j, t, out)
    return out


def _split3f(x):
    hi, mid, lo = _split3(x)
    return [hi.astype(F32), mid.astype(F32), lo.astype(F32)]


def _inproj_fwd(x, g_mix, w_pad, bf_pad):
    S = x.shape[0]
    tm = _pick(S, (512, 256))
    tri = (lax.broadcasted_iota(jnp.int32, (tm, tm), 0) >= lax.broadcasted_iota(jnp.int32, (tm, tm), 1)).astype(BF16)

    def body(x_ref, g_ref, w_ref, bf_ref, tri_ref, xn_ref, qa_ref, ka_ref, va_ref, ug_ref, zf_ref, carry_ref):
        i = pl.program_id(0)

        @pl.when(i == 0)
        def _():
            carry_ref[...] = jnp.zeros_like(carry_ref)

        xf = x_ref[...]
        r = lax.rsqrt(jnp.mean(xf * xf, axis=-1, keepdims=True) + EPS)
        xn = ((xf * r) * g_ref[...]).astype(BF16)
        xn_ref[...] = xn
        proj = lax.dot_general(xn, w_ref[...], _NT, preferred_element_type=F32)
        ug_ref[...] = proj[:, QKV:UG_END]
        zf = proj[:, UG_END:] + bf_ref[...]
        zf_ref[...] = zf
        lf = jnp.minimum(zf, 0.0) - jnp.log(1.0 + jnp.exp(-jnp.abs(zf)))
        c = _dot3l(tri_ref[...], lf) + carry_ref[0:1, :]
        carry_ref[0:1, :] = c[tm - 1:tm, :]
        c3 = _split3f(c)
        lane = lax.broadcasted_iota(jnp.int32, (tm, LANES), 1)
        ones3 = [1.0, 1.0, 1.0]
        for h in range(N_HEADS):
            p, odd = h // 2, h % 2
            ch = [t[:, h:h + 1] for t in c3]

            def head(base, scale=None, p=p, odd=odd):
                blk = proj[:, base + p * LANES:base + (p + 1) * LANES]
                if scale is not None:
                    blk = blk * scale
                return pltpu.roll(blk, HEAD_DIM, 1) if odd else blk

            cols = slice(h * LANES, (h + 1) * LANES)
            qa_ref[:, cols] = jnp.where(lane < HEAD_DIM, head(0, HEAD_DIM ** -0.5), _aug(lane, ch + ones3)).astype(BF16)
            ka_ref[:, cols] = jnp.where(lane < HEAD_DIM, head(ATT_WIDTH),
                                        _aug(lane, ones3 + [-t for t in ch] + ones3)).astype(BF16)
            va_ref[:, cols] = jnp.where(lane < HEAD_DIM, head(2 * ATT_WIDTH), _aug(lane, ones3)).astype(BF16)

    wide = N_HEADS * LANES
    return pl.pallas_call(
        body,
        out_shape=(jax.ShapeDtypeStruct((S, D_MODEL), BF16), jax.ShapeDtypeStruct((S, wide), BF16),
                   jax.ShapeDtypeStruct((S, wide), BF16), jax.ShapeDtypeStruct((S, wide), BF16),
                   jax.ShapeDtypeStruct((S, 2 * GMLP_WIDTH), F32), jax.ShapeDtypeStruct((S, LANES), F32)),
        grid=(S // tm,),
        in_specs=[pl.BlockSpec((tm, D_MODEL), lambda i: (i, 0)), pl.BlockSpec((1, D_MODEL), lambda i: (0, 0)),
                  pl.BlockSpec((IN_PAD, D_MODEL), lambda i: (0, 0)), pl.BlockSpec((1, LANES), lambda i: (0, 0)),
                  pl.BlockSpec((tm, tm), lambda i: (0, 0))],
        out_specs=(pl.BlockSpec((tm, D_MODEL), lambda i: (i, 0)), pl.BlockSpec((tm, wide), lambda i: (i, 0)),
                   pl.BlockSpec((tm, wide), lambda i: (i, 0)), pl.BlockSpec((tm, wide), lambda i: (i, 0)),
                   pl.BlockSpec((tm, 2 * GMLP_WIDTH), lambda i: (i, 0)), pl.BlockSpec((tm, LANES), lambda i: (i, 0))),
        scratch_shapes=[pltpu.VMEM((8, LANES), F32)],
        name="inproj_fwd", compiler_params=_cp(("arbitrary",), VMEM_LIMIT),
    )(x, g_mix, w_pad, bf_pad, tri)


def _group_ones():
    r = lax.broadcasted_iota(jnp.int32, (GMLP_WIDTH, GMLP_WIDTH), 0) // GROUP_DIM
    c = lax.broadcasted_iota(jnp.int32, (GMLP_WIDTH, GMLP_WIDTH), 1) // GROUP_DIM
    return (r == c).astype(BF16)


def _gmlp_mixed(vn_bf, w_ref, bias, n_chunks):
    lane = lax.broadcasted_iota(jnp.int32, (CHUNK, LANES), 1)
    row = lax.broadcasted_iota(jnp.int32, (CHUNK, CHUNK), 0)
    col = lax.broadcasted_iota(jnp.int32, (CHUNK, CHUNK), 1)
    ws = [jnp.where(row >= col, w_ref[g], 0.0).astype(BF16) for g in range(N_GROUPS)]
    rows = []
    for ci in range(n_chunks):
        cols = []
        for pp in range(N_GROUPS // 2):
            v = vn_bf[ci * CHUNK:(ci + 1) * CHUNK, pp * LANES:(pp + 1) * LANES]
            v_lo = jnp.where(lane < GROUP_DIM, v, jnp.zeros_like(v))
            v_hi = jnp.where(lane >= GROUP_DIM, v, jnp.zeros_like(v))
            m = (jnp.dot(ws[2 * pp], v_lo, preferred_element_type=F32)
                 + jnp.dot(ws[2 * pp + 1], v_hi, preferred_element_type=F32))
            cols.append(m + bias[:, pp * LANES:(pp + 1) * LANES])
        rows.append(jnp.concatenate(cols, axis=1))
    return jnp.concatenate(rows, axis=0)


def _gmlp_fwd(ug, gain, w_s, bias_full):
    S = ug.shape[0]
    tm = _pick(S, (512, 256, 128))
    ones = _group_ones()

    def body(ug_ref, gain_ref, w_ref, bias_ref, ones_ref, sg_ref):
        u = _gelu(ug_ref[:, :GMLP_WIDTH])
        vr = _gelu(ug_ref[:, GMLP_WIDTH:])
        ms = _dot3(vr * vr, ones_ref[...]) * (1.0 / GROUP_DIM)
        vn = ((vr * lax.rsqrt(ms + EPS)) * gain_ref[...]).astype(BF16)
        mixed = _gmlp_mixed(vn, w_ref, bias_ref[...], tm // CHUNK)
        sg_ref[...] = (u * mixed).astype(BF16)

    return pl.pallas_call(
        body, out_shape=jax.ShapeDtypeStruct((S, GMLP_WIDTH), BF16), grid=(S // tm,),
        in_specs=[pl.BlockSpec((tm, 2 * GMLP_WIDTH), lambda i: (i, 0)), pl.BlockSpec((1, GMLP_WIDTH), lambda i: (0, 0)),
                  pl.BlockSpec((N_GROUPS, CHUNK, CHUNK), lambda i: (0, 0, 0)),
                  pl.BlockSpec((CHUNK, GMLP_WIDTH), lambda i: (0, 0)),
                  pl.BlockSpec((GMLP_WIDTH, GMLP_WIDTH), lambda i: (0, 0))],
        out_specs=pl.BlockSpec((tm, GMLP_WIDTH), lambda i: (i, 0)),
        name="gmlp_fwd", compiler_params=_cp(("parallel",), VMEM_LIMIT),
    )(ug, gain, w_s, bias_full, ones)


_NT = (((1,), (1,)), ((), ()))
_TN = (((0,), (0,)), ((), ()))


def _attn_fwd(qa, ka, va):
    S = qa.shape[0]
    tq = _pick(S, (ATT_TQ, 256))
    tk = min(ATT_TK, tq)
    nq = S // tq
    per_q = tq // tk

    def body(q_ref, k_ref, v_ref, o_ref, lse_ref, ob_ref):
        qi = pl.program_id(1)
        lane = lax.broadcasted_iota(jnp.int32, (tq, LANES), 1)
        sub = tk
        rid = lax.broadcasted_iota(jnp.int32, (sub, sub), 0)
        cid = lax.broadcasted_iota(jnp.int32, (sub, sub), 1)
        qs = [q_ref[:, :LANES], q_ref[:, LANES:]]

        def update(q, ks, k_len, h, m, acc, masked):
            cols = slice(h * LANES, (h + 1) * LANES)
            s = lax.dot_general(q, k_ref[pl.ds(ks, k_len), cols], _NT, preferred_element_type=F32)
            if masked:
                s = jnp.where(rid >= cid, s, -jnp.inf)
            m_new = jnp.maximum(m, jnp.max(s, axis=-1, keepdims=True))
            p = jnp.exp(s - m_new).astype(BF16)
            acc = jnp.exp(m - m_new) * acc + jnp.dot(p, v_ref[pl.ds(ks, k_len), cols], preferred_element_type=F32)
            return m_new, acc

        def step(kb, carry):
            ks = pl.multiple_of(kb * tk, tk)
            return tuple(update(qs[h], ks, tk, h, *carry[h], False) for h in range(2))

        one = (jnp.full((tq, 1), -jnp.inf, F32), jnp.zeros((tq, LANES), F32))
        carry = lax.fori_loop(0, qi * per_q, step, (one, one))
        outs, lses = [], []
        for h in range(2):
            ms, accs = [], []
            for r in range(tq // sub):
                rows = slice(r * sub, (r + 1) * sub)
                m, acc = carry[h][0][rows], carry[h][1][rows]
                for c in range(r + 1):
                    ks = pl.multiple_of(qi * tq + c * sub, sub)
                    m, acc = update(qs[h][rows], ks, sub, h, m, acc, c == r)
                ms.append(m)
                accs.append(acc)
            m, acc = jnp.concatenate(ms, axis=0), jnp.concatenate(accs, axis=0)
            l = acc[:, HEAD_DIM:HEAD_DIM + 1]
            outs.append(acc / l)
            lses.append(m + jnp.log(l))
        o = jnp.where(lane < HEAD_DIM, outs[0], pltpu.roll(outs[1], HEAD_DIM, 1))
        o_ref[...] = o
        ob_ref[...] = o.astype(BF16)
        lse_ref[...] = jnp.where(lane < HEAD_DIM, lses[0], lses[1])

    return pl.pallas_call(
        body,
        out_shape=(jax.ShapeDtypeStruct((S, ATT_WIDTH), F32), jax.ShapeDtypeStruct((S, ATT_WIDTH), F32),
                   jax.ShapeDtypeStruct((S, ATT_WIDTH), BF16)),
        grid=(N_PAIRS, nq),
        in_specs=[pl.BlockSpec((tq, 2 * LANES), lambda p, i: (i, p)),
                  pl.BlockSpec((S, 2 * LANES), lambda p, i: (0, p)),
                  pl.BlockSpec((S, 2 * LANES), lambda p, i: (0, p))],
        out_specs=(pl.BlockSpec((tq, LANES), lambda p, i: (i, p)), pl.BlockSpec((tq, LANES), lambda p, i: (i, p)),
                   pl.BlockSpec((tq, LANES), lambda p, i: (i, p))),
        name="attn_fwd", compiler_params=_cp(("parallel", "parallel"), VMEM_LIMIT),
    )(qa, ka, va)


def _rms_fwd(h, g):
    S = h.shape[0]
    tm = _pick(S, (512, 256))

    def body(h_ref, g_ref, o_ref):
        hf = h_ref[...]
        r = lax.rsqrt(jnp.mean(hf * hf, axis=-1, keepdims=True) + EPS)
        o_ref[...] = ((hf * r) * g_ref[...]).astype(BF16)

    return pl.pallas_call(
        body, out_shape=jax.ShapeDtypeStruct(h.shape, BF16), grid=(S // tm,),
        in_specs=[pl.BlockSpec((tm, D_MODEL), lambda i: (i, 0)), pl.BlockSpec((1, D_MODEL), lambda i: (0, 0))],
        out_specs=pl.BlockSpec((tm, D_MODEL), lambda i: (i, 0)),
        name="rms_fwd", compiler_params=_cp(("parallel",)),
    )(h, g)


def _shift_rows(x, prev, n):
    rid = lax.broadcasted_iota(jnp.int32, x.shape, 0)
    y = pltpu.roll(x, n, 0)
    if n == 1:
        return jnp.where(rid == 0, prev[7:8, :], y)
    return jnp.where(rid == 0, prev[6:7, :], jnp.where(rid == 1, prev[7:8, :], y))


def _shift_rows_up(x, nxt, n):
    rows = x.shape[0]
    rid = lax.broadcasted_iota(jnp.int32, x.shape, 0)
    y = pltpu.roll(x, rows - n, 0)
    if n == 1:
        return jnp.where(rid == rows - 1, nxt[0:1, :], y)
    return jnp.where(rid == rows - 2, nxt[0:1, :], jnp.where(rid == rows - 1, nxt[1:2, :], y))


def _conv3(cur, prev, w, b):
    return (w[0:1, :] * _shift_rows(cur, prev, 2) + w[1:2, :] * _shift_rows(cur, prev, 1)
            + w[2:3, :] * cur + b)


def _conv_act_fwd(hu, cw, cb):
    _, S, F = hu.shape
    tm = _pick(S, (512, 256))
    tn = _pick(F, (256, 128))
    r8 = tm // 8

    def body(cur_ref, prev_ref, w_ref, b_ref, o_ref):
        i = pl.program_id(1)
        halves = []
        for h in range(2):
            prev = jnp.where(i > 0, prev_ref[h], 0.0)
            halves.append(_conv3(cur_ref[h], prev, w_ref[h], b_ref[h]))
        a, g = halves
        o_ref[...] = (g * _sigmoid(g) * a).astype(BF16)

    return pl.pallas_call(
        body, out_shape=jax.ShapeDtypeStruct((S, F), BF16), grid=(F // tn, S // tm),
        in_specs=[pl.BlockSpec((2, tm, tn), lambda j, i: (0, i, j)),
                  pl.BlockSpec((2, 8, tn), lambda j, i: (0, jnp.maximum(i * r8 - 1, 0), j)),
                  pl.BlockSpec((2, 8, tn), lambda j, i: (0, 0, j)),
                  pl.BlockSpec((2, 1, tn), lambda j, i: (0, 0, j))],
        out_specs=pl.BlockSpec((tm, tn), lambda j, i: (i, j)),
        name="conv_act_fwd", compiler_params=_cp(("parallel", "parallel"), VMEM_LIMIT),
    )(hu, hu, cw, cb)


def _ffn_up_conv(hn, w_up_bf, cw, cb):
    S = hn.shape[0]
    F = D_FF
    tm = _pick(S, (FFN_TM, 256))
    tn = _pick(F, (FFN_TN, 256, 128))
    nj = F // tn

    def body(hn_ref, wa_ref, wg_ref, cw_ref, cb_ref, hu_ref, hc_ref, act_ref, tail_ref):
        i = pl.program_id(1)

        @pl.when(i == 0)
        def _():
            tail_ref[...] = jnp.zeros_like(tail_ref)

        hn_v = hn_ref[...]
        halves = []
        for h, w_ref in enumerate((wa_ref, wg_ref)):
            hu = lax.dot_general(hn_v, w_ref[...], _NT, preferred_element_type=F32)
            hu_ref[h] = hu
            hc = _conv3(hu, tail_ref[h], cw_ref[h], cb_ref[h])
            hc_ref[h] = hc
            halves.append(hc)
            tail_ref[h] = hu[tm - 8:, :]
        a, g = halves
        act_ref[...] = (g * _sigmoid(g) * a).astype(BF16)

    both = pl.BlockSpec((2, tm, tn), lambda j, i: (0, i, j))
    return pl.pallas_call(
        body, out_shape=(jax.ShapeDtypeStruct((2, S, F), F32), jax.ShapeDtypeStruct((2, S, F), F32),
                         jax.ShapeDtypeStruct((S, F), BF16)),
        grid=(nj, S // tm),
        in_specs=[pl.BlockSpec((tm, D_MODEL), lambda j, i: (i, 0)),
                  pl.BlockSpec((tn, D_MODEL), lambda j, i: (j, 0)),
                  pl.BlockSpec((tn, D_MODEL), lambda j, i: (nj + j, 0)),
                  pl.BlockSpec((2, 8, tn), lambda j, i: (0, 0, j)),
                  pl.BlockSpec((2, 1, tn), lambda j, i: (0, 0, j))],
        out_specs=(both, both, pl.BlockSpec((tm, tn), lambda j, i: (i, j))),
        scratch_shapes=[pltpu.VMEM((2, 8, tn), F32)],
        name="ffn_up_conv", compiler_params=_cp(("parallel", "arbitrary"), VMEM_LIMIT),
    )(hn, w_up_bf, w_up_bf, cw, cb)


def _ffn_down_loss(act, w_down_bf, h1, g_final, target):
    S = h1.shape[0]
    tm = _pick(S, (512, 256))

    def body(a_ref, w_ref, h1_ref, g_ref, t_ref, loss_ref, dh_ref, dhb_ref, dg_ref):
        i = pl.program_id(0)

        @pl.when(i == 0)
        def _():
            loss_ref[...] = jnp.zeros_like(loss_ref)
            dg_ref[...] = jnp.zeros_like(dg_ref)

        hf = h1_ref[...] + jnp.dot(a_ref[...], w_ref[...], preferred_element_type=F32)
        g = g_ref[...]
        r = lax.rsqrt(jnp.mean(hf * hf, axis=-1, keepdims=True) + EPS)
        hhat = hf * r
        err = hhat * g - t_ref[...]
        loss_ref[...] += 0.5 * jnp.sum(jnp.mean(err * err, axis=-1, keepdims=True))
        dy = err * (1.0 / D_MODEL)
        dg_ref[0:1, :] += jnp.sum(dy * hhat, axis=0, keepdims=True)
        dhat = dy * g
        dh = r * (dhat - hhat * jnp.mean(dhat * hhat, axis=-1, keepdims=True))
        dh_ref[...] = dh
        dhb_ref[...] = dh.astype(BF16)

    row = pl.BlockSpec((tm, D_MODEL), lambda i: (i, 0))
    return pl.pallas_call(
        body,
        out_shape=(jax.ShapeDtypeStruct((8, LANES), F32), jax.ShapeDtypeStruct((S, D_MODEL), F32),
                   jax.ShapeDtypeStruct((S, D_MODEL), BF16), jax.ShapeDtypeStruct((8, D_MODEL), F32)),
        grid=(S // tm,),
        in_specs=[pl.BlockSpec((tm, D_FF), lambda i: (i, 0)), pl.BlockSpec((D_FF, D_MODEL), lambda i: (0, 0)), row,
                  pl.BlockSpec((1, D_MODEL), lambda i: (0, 0)), row],
        out_specs=(pl.BlockSpec((8, LANES), lambda i: (0, 0)), row, row, pl.BlockSpec((8, D_MODEL), lambda i: (0, 0))),
        name="ffn_down_loss", compiler_params=_cp(("arbitrary",), VMEM_LIMIT),
    )(act, w_down_bf, h1, g_final, target)


def _loss_head(h2, g_final, target):
    S = h2.shape[0]
    tm = _pick(S, (512, 256))

    def body(h_ref, g_ref, t_ref, loss_ref, dh_ref, dhb_ref, dg_ref):
        i = pl.program_id(0)

        @pl.when(i == 0)
        def _():
            loss_ref[...] = jnp.zeros_like(loss_ref)
            dg_ref[...] = jnp.zeros_like(dg_ref)

        hf = h_ref[...]
        g = g_ref[...]
        r = lax.rsqrt(jnp.mean(hf * hf, axis=-1, keepdims=True) + EPS)
        hhat = hf * r
        err = hhat * g - t_ref[...]
        loss_ref[...] += 0.5 * jnp.sum(jnp.mean(err * err, axis=-1, keepdims=True))
        dy = err * (1.0 / D_MODEL)
        dg_ref[0:1, :] += jnp.sum(dy * hhat, axis=0, keepdims=True)
        dhat = dy * g
        dh = r * (dhat - hhat * jnp.mean(dhat * hhat, axis=-1, keepdims=True))
        dh_ref[...] = dh
        dhb_ref[...] = dh.astype(BF16)

    return pl.pallas_call(
        body,
        out_shape=(jax.ShapeDtypeStruct((8, LANES), F32), jax.ShapeDtypeStruct((S, D_MODEL), F32),
                   jax.ShapeDtypeStruct((S, D_MODEL), BF16), jax.ShapeDtypeStruct((8, D_MODEL), F32)),
        grid=(S // tm,),
        in_specs=[pl.BlockSpec((tm, D_MODEL), lambda i: (i, 0)), pl.BlockSpec((1, D_MODEL), lambda i: (0, 0)),
                  pl.BlockSpec((tm, D_MODEL), lambda i: (i, 0))],
        out_specs=(pl.BlockSpec((8, LANES), lambda i: (0, 0)), pl.BlockSpec((tm, D_MODEL), lambda i: (i, 0)),
                   pl.BlockSpec((tm, D_MODEL), lambda i: (i, 0)), pl.BlockSpec((8, D_MODEL), lambda i: (0, 0))),
        name="loss_head", compiler_params=_cp(("arbitrary",), VMEM_LIMIT),
    )(h2, g_final, target)


def _rms_bwd(h, g, dy, res):
    S = h.shape[0]
    tm = _pick(S, (512, 256))

    def body(h_ref, g_ref, dy_ref, r_ref, dh_ref, dhb_ref, dg_ref):
        i = pl.program_id(0)

        @pl.when(i == 0)
        def _():
            dg_ref[...] = jnp.zeros_like(dg_ref)

        hf = h_ref[...]
        dyv = dy_ref[...]
        r = lax.rsqrt(jnp.mean(hf * hf, axis=-1, keepdims=True) + EPS)
        hhat = hf * r
        dg_ref[0:1, :] += jnp.sum(dyv * hhat, axis=0, keepdims=True)
        dhat = dyv * g_ref[...]
        dh = r_ref[...] + r * (dhat - hhat * jnp.mean(dhat * hhat, axis=-1, keepdims=True))
        dh_ref[...] = dh
        dhb_ref[...] = dh.astype(BF16)

    return pl.pallas_call(
        body,
        out_shape=(jax.ShapeDtypeStruct((S, D_MODEL), F32), jax.ShapeDtypeStruct((S, D_MODEL), BF16),
                   jax.ShapeDtypeStruct((8, D_MODEL), F32)),
        grid=(S // tm,),
        in_specs=[pl.BlockSpec((tm, D_MODEL), lambda i: (i, 0)), pl.BlockSpec((1, D_MODEL), lambda i: (0, 0)),
                  pl.BlockSpec((tm, D_MODEL), lambda i: (i, 0)), pl.BlockSpec((tm, D_MODEL), lambda i: (i, 0))],
        out_specs=(pl.BlockSpec((tm, D_MODEL), lambda i: (i, 0)), pl.BlockSpec((tm, D_MODEL), lambda i: (i, 0)),
                   pl.BlockSpec((8, D_MODEL), lambda i: (0, 0))),
        name="rms_bwd", compiler_params=_cp(("arbitrary",), VMEM_LIMIT),
    )(h, g, dy, res)


def _conv_act_bwd(hu, dh2_bf, w_down_bf, cw, cb):
    _, S, F = hu.shape
    tm = _pick(S, (256,))
    tn = _pick(F, (256, 128))
    r8 = tm // 8
    r16 = tm // 16
    n_i = S // tm
    last8 = S // 8 - 1
    last16 = S // 16 - 1

    def body(cur_ref, prev_ref, next_ref, dy_ref, dyn_ref, wd_ref, w_ref, b_ref, dhu_ref, dcw_ref):
        i = pl.program_id(1)
        wd = wd_ref[...]
        dact = lax.dot_general(dy_ref[...], wd, _NT, preferred_element_type=F32)
        dact_n = lax.dot_general(dyn_ref[...], wd, _NT, preferred_element_type=F32)[0:8, :]

        @pl.when(i == 0)
        def _():
            dcw_ref[...] = jnp.zeros_like(dcw_ref)

        rid8 = lax.broadcasted_iota(jnp.int32, (8, tn), 0)
        cur = [cur_ref[0], cur_ref[1]]
        prev = [jnp.where(i > 0, prev_ref[h], 0.0) for h in range(2)]
        nxt = [next_ref[0], next_ref[1]]
        w = [w_ref[0], w_ref[1]]

        def gate_grads(a, g, d):
            sg = _sigmoid(g)
            return d * (g * sg), d * a * (sg * (1.0 + g * (1.0 - sg)))

        taps = [(_shift_rows(cur[h], prev[h], 2), _shift_rows(cur[h], prev[h], 1), cur[h]) for h in range(2)]
        a, g = [w[h][0:1, :] * taps[h][0] + w[h][1:2, :] * taps[h][1] + w[h][2:3, :] * taps[h][2] + b_ref[h]
                for h in range(2)]
        dhc = gate_grads(a, g, dact)
        a_n = _conv3(nxt[0], cur[0][tm - 8:, :], w[0], b_ref[0])
        g_n = _conv3(nxt[1], cur[1][tm - 8:, :], w[1], b_ref[1])
        dhc_n = gate_grads(a_n, g_n, dact_n)
        for h in range(2):
            d = dhc[h]
            dn = jnp.where(i < n_i - 1, dhc_n[h], 0.0)
            dhu = (w[h][2:3, :] * d + w[h][1:2, :] * _shift_rows_up(d, dn, 1)
                   + w[h][0:1, :] * _shift_rows_up(d, dn, 2))
            dhu_ref[h] = dhu.astype(BF16)
            t0, t1, t2 = [jnp.sum(d * t, axis=0, keepdims=True) for t in taps[h]]
            t3 = jnp.sum(d, axis=0, keepdims=True)
            dcw_ref[h] += jnp.where(rid8 == 0, t0, jnp.where(rid8 == 1, t1, jnp.where(rid8 == 2, t2, jnp.where(rid8 == 3, t3, 0.0))))

    return pl.pallas_call(
        body,
        out_shape=(jax.ShapeDtypeStruct((2, S, F), BF16), jax.ShapeDtypeStruct((2, 8, F), F32)),
        grid=(F // tn, n_i),
        in_specs=[pl.BlockSpec((2, tm, tn), lambda j, i: (0, i, j)),
                  pl.BlockSpec((2, 8, tn), lambda j, i: (0, jnp.maximum(i * r8 - 1, 0), j)),
                  pl.BlockSpec((2, 8, tn), lambda j, i: (0, jnp.minimum((i + 1) * r8, last8), j)),
                  pl.BlockSpec((tm, D_MODEL), lambda j, i: (i, 0)),
                  pl.BlockSpec((16, D_MODEL), lambda j, i: (jnp.minimum((i + 1) * r16, last16), 0)),
                  pl.BlockSpec((tn, D_MODEL), lambda j, i: (j, 0)),
                  pl.BlockSpec((2, 8, tn), lambda j, i: (0, 0, j)),
                  pl.BlockSpec((2, 1, tn), lambda j, i: (0, 0, j))],
        out_specs=(pl.BlockSpec((2, tm, tn), lambda j, i: (0, i, j)), pl.BlockSpec((2, 8, tn), lambda j, i: (0, 0, j))),
        name="conv_act_bwd", compiler_params=_cp(("parallel", "arbitrary"), VMEM_LIMIT),
    )(hu, hu, hu, dh2_bf, dh2_bf, w_down_bf, cw, cb)


def _ffn_up_dx_rms(dhu, w_up_bf, h1, g_ffn, dh2):
    _, S, F = dhu.shape
    tm = _pick(S, (512, 256))

    def body(a_ref, b_ref, h_ref, g_ref, r_ref, dh_ref, dhb_ref, dg_ref, acc_ref):
        i, k = pl.program_id(0), pl.program_id(1)

        @pl.when((i == 0) & (k == 0))
        def _():
            dg_ref[...] = jnp.zeros_like(dg_ref)

        part = jnp.dot(a_ref[...], b_ref[...], preferred_element_type=F32)

        @pl.when(k == 0)
        def _():
            acc_ref[...] = part

        @pl.when(k == 1)
        def _():
            dyv = acc_ref[...] + part
            hf = h_ref[...]
            r = lax.rsqrt(jnp.mean(hf * hf, axis=-1, keepdims=True) + EPS)
            hhat = hf * r
            dg_ref[0:1, :] += jnp.sum(dyv * hhat, axis=0, keepdims=True)
            dhat = dyv * g_ref[...]
            dh = r_ref[...] + r * (dhat - hhat * jnp.mean(dhat * hhat, axis=-1, keepdims=True))
            dh_ref[...] = dh
            dhb_ref[...] = dh.astype(BF16)

    row = pl.BlockSpec((tm, D_MODEL), lambda i, k: (i, 0))
    return pl.pallas_call(
        body,
        out_shape=(jax.ShapeDtypeStruct((S, D_MODEL), F32), jax.ShapeDtypeStruct((S, D_MODEL), BF16),
                   jax.ShapeDtypeStruct((8, D_MODEL), F32)),
        grid=(S // tm, 2),
        in_specs=[pl.BlockSpec((None, tm, F), lambda i, k: (k, i, 0)), pl.BlockSpec((F, D_MODEL), lambda i, k: (k, 0)),
                  row, pl.BlockSpec((1, D_MODEL), lambda i, k: (0, 0)), row],
        out_specs=(row, row, pl.BlockSpec((8, D_MODEL), lambda i, k: (0, 0))),
        scratch_shapes=[pltpu.VMEM((tm, D_MODEL), F32)],
        name="ffn_up_dx_rms", compiler_params=_cp(("arbitrary", "arbitrary"), VMEM_LIMIT),
    )(dhu, w_up_bf, h1, g_ffn, dh2)


def _conv_gate_bwd(hc, hu, dact, cw):
    _, S, F = hu.shape
    tm = _pick(S, (CONV_TM, 128))
    tn = _pick(F, (CONV_TN, 256, 128))
    r8 = tm // 8
    n_i = S // tm
    last8 = S // 8 - 1

    def body(hc_ref, hcn_ref, hu_ref, da_ref, dan_ref, w_ref, dhu_ref, dcw_ref):
        i = pl.program_id(1)

        @pl.when(i == 0)
        def _():
            dcw_ref[...] = jnp.zeros_like(dcw_ref)

        rid8 = lax.broadcasted_iota(jnp.int32, (8, tn), 0)

        def gate_grads(a, g, d):
            sg = _sigmoid(g)
            return d * (g * sg), d * a * (sg * (1.0 + g * (1.0 - sg)))

        dhc = gate_grads(hc_ref[0], hc_ref[1], da_ref[...])
        dhc_n = gate_grads(hcn_ref[0], hcn_ref[1], dan_ref[...])
        for h in range(2):
            w = w_ref[h]
            d = dhc[h]
            dn = jnp.where(i < n_i - 1, dhc_n[h], 0.0)
            u1 = _shift_rows_up(d, dn, 1)
            u2 = _shift_rows_up(d, dn, 2)
            dhu_ref[h] = (w[2:3, :] * d + w[1:2, :] * u1 + w[0:1, :] * u2).astype(BF16)
            x = hu_ref[h]
            t0, t1, t2, t3 = [jnp.sum(t, axis=0, keepdims=True) for t in (u2 * x, u1 * x, d * x, d)]
            dcw_ref[h] += jnp.where(rid8 == 0, t0, jnp.where(rid8 == 1, t1, jnp.where(rid8 == 2, t2, jnp.where(rid8 == 3, t3, 0.0))))

    cur = pl.BlockSpec((2, tm, tn), lambda j, i: (0, i, j))
    return pl.pallas_call(
        body,
        out_shape=(jax.ShapeDtypeStruct((2, S, F), BF16), jax.ShapeDtypeStruct((2, 8, F), F32)),
        grid=(F // tn, n_i),
        in_specs=[cur, pl.BlockSpec((2, 8, tn), lambda j, i: (0, jnp.minimum((i + 1) * r8, last8), j)), cur,
                  pl.BlockSpec((tm, tn), lambda j, i: (i, j)),
                  pl.BlockSpec((8, tn), lambda j, i: (jnp.minimum((i + 1) * r8, last8), j)),
                  pl.BlockSpec((2, 8, tn), lambda j, i: (0, 0, j))],
        out_specs=(cur, pl.BlockSpec((2, 8, tn), lambda j, i: (0, 0, j))),
        name="conv_gate_bwd", compiler_params=_cp(("parallel", "arbitrary"), VMEM_LIMIT),
    )(hc, hc, hu, dact, dact, cw)


def _attn_prep(att, lse, dmix, qa):
    S = att.shape[0]
    tm = _pick(S, (256,))

    def body(o_ref, lse_ref, do_ref, q_ref, qb_ref, doa_ref):
        lane = lax.broadcasted_iota(jnp.int32, (tm, LANES), 1)
        for p in range(N_PAIRS):
            pc = slice(p * LANES, (p + 1) * LANES)
            do = do_ref[:, pc]
            prod = o_ref[:, pc] * do
            for hh in range(2):
                sel = (lane >= HEAD_DIM) if hh else (lane < HEAD_DIM)
                delta = jnp.sum(jnp.where(sel, prod, 0.0), axis=-1, keepdims=True)
                dod = pltpu.roll(do, HEAD_DIM, 1) if hh else do
                cols = slice((2 * p + hh) * LANES, (2 * p + hh + 1) * LANES)
                doa_ref[:, cols] = jnp.where(lane < HEAD_DIM, dod, _aug(lane, _split3f(-delta))).astype(BF16)
                lcol = p * LANES + hh * HEAD_DIM
                l3 = _split3f(-lse_ref[:, lcol:lcol + 1])
                augl = _aug(lane, [0.0] * 6 + l3).astype(BF16)
                qb_ref[:, cols] = jnp.where((lane >= HEAD_DIM + 6) & (lane < HEAD_DIM + 9), augl, q_ref[:, cols])

    half = pl.BlockSpec((tm, ATT_WIDTH), lambda i: (i, 0))
    wide = pl.BlockSpec((tm, N_HEADS * LANES), lambda i: (i, 0))
    return pl.pallas_call(
        body,
        out_shape=(jax.ShapeDtypeStruct(qa.shape, BF16), jax.ShapeDtypeStruct(qa.shape, BF16)),
        grid=(S // tm,), in_specs=[half, half, half, wide], out_specs=(wide, wide),
        name="attn_prep", compiler_params=_cp(("parallel",), VMEM_LIMIT),
    )(att, lse, dmix, qa)


def _attn_bwd(qb, ka, va, doa):
    S = qb.shape[0]
    tk = _pick(S, (512, 256))
    tq = tk
    nq = S // tq

    def pair(a, scale=None):
        lane = lax.broadcasted_iota(jnp.int32, (a.shape[0], LANES), 1)
        out = jnp.where(lane < HEAD_DIM, a[:, :LANES], pltpu.roll(a[:, LANES:], HEAD_DIM, 1))
        return out if scale is None else out * scale

    def lanes01(a, col, sign):
        lane = lax.broadcasted_iota(jnp.int32, (a.shape[0], LANES), 1)
        return jnp.where(lane == 0, sign * a[:, col:col + 1], jnp.where(lane == 1, sign * a[:, LANES + col:LANES + col + 1], 0.0))

    def body(q_ref, do_ref, k_ref, v_ref, dqc_ref, dkc_ref, dvc_ref, dcq_ref, dck_ref, dq_ref, dka_ref, dva_ref):
        kb = pl.program_id(1)

        @pl.when(kb == 0)
        def _():
            dq_ref[...] = jnp.zeros_like(dq_ref)

        dka_ref[...] = jnp.zeros_like(dka_ref)
        dva_ref[...] = jnp.zeros_like(dva_ref)
        rid = lax.broadcasted_iota(jnp.int32, (tk, tq), 0)
        cid = lax.broadcasted_iota(jnp.int32, (tk, tq), 1)

        def sub_tile(qs, q_len, k_off, k_len, masked):
            keys = slice(k_off, k_off + k_len)
            for h in range(2):
                cols = slice(h * LANES, (h + 1) * LANES)
                qblk = q_ref[pl.ds(qs, q_len), cols]
                doblk = do_ref[pl.ds(qs, q_len), cols]
                kh = k_ref[keys, cols]
                p = jnp.exp(lax.dot_general(kh, qblk, _NT, preferred_element_type=F32))
                if masked:
                    p = jnp.where(cid >= rid, p, 0.0)
                ds = (p * lax.dot_general(v_ref[keys, cols], doblk, _NT, preferred_element_type=F32)).astype(BF16)
                dva_ref[keys, cols] += jnp.dot(p.astype(BF16), doblk, preferred_element_type=F32)
                dka_ref[keys, cols] += jnp.dot(ds, qblk, preferred_element_type=F32)
                dq_ref[pl.ds(qs, q_len), cols] += lax.dot_general(ds, kh, _TN, preferred_element_type=F32)

        sub_tile(pl.multiple_of(kb * tq, tq), tq, 0, tk, True)

        def step(qi, carry):
            sub_tile(pl.multiple_of(qi * tq, tq), tq, 0, tk, False)
            return carry

        lax.fori_loop(kb + 1, nq, step, 0)
        dka = dka_ref[...]
        dkc_ref[...] = pair(dka).astype(BF16)
        dvc_ref[...] = pair(dva_ref[...]).astype(BF16)
        dck_ref[...] = lanes01(dka, HEAD_DIM + 3, -1.0)

        @pl.when(kb == nq - 1)
        def _():
            dqa = dq_ref[...]
            dqc_ref[...] = pair(dqa, HEAD_DIM ** -0.5).astype(BF16)
            dcq_ref[...] = lanes01(dqa, HEAD_DIM, 1.0)

    wide = 2 * LANES
    half = jax.ShapeDtypeStruct((S, ATT_WIDTH), BF16)
    slabs = jax.ShapeDtypeStruct((N_PAIRS, S, LANES), F32)
    return pl.pallas_call(
        body,
        out_shape=(half, half, half, slabs, slabs),
        grid=(N_PAIRS, nq),
        in_specs=[pl.BlockSpec((S, wide), lambda p, j: (0, p)), pl.BlockSpec((S, wide), lambda p, j: (0, p)),
                  pl.BlockSpec((tk, wide), lambda p, j: (j, p)), pl.BlockSpec((tk, wide), lambda p, j: (j, p))],
        out_specs=(pl.BlockSpec((S, LANES), lambda p, j: (0, p)), pl.BlockSpec((tk, LANES), lambda p, j: (j, p)),
                   pl.BlockSpec((tk, LANES), lambda p, j: (j, p)), pl.BlockSpec((None, S, LANES), lambda p, j: (p, 0, 0)),
                   pl.BlockSpec((None, tk, LANES), lambda p, j: (p, j, 0))),
        scratch_shapes=[pltpu.VMEM((S, wide), F32), pltpu.VMEM((tk, wide), F32), pltpu.VMEM((tk, wide), F32)],
        name="attn_bwd", compiler_params=_cp(("parallel", "arbitrary"), VMEM_LIMIT),
    )(qb, doa, ka, va)


def _attn_delta_old(o, do):
    S = o.shape[0]
    tm = _pick(S, (512, 256))
    ones = _group_ones()

    def body(o_ref, do_ref, ones_ref, d_ref):
        d_ref[...] = _dot3(o_ref[...] * do_ref[...], ones_ref[...])

    return pl.pallas_call(
        body, out_shape=jax.ShapeDtypeStruct((S, ATT_WIDTH), F32), grid=(S // tm,),
        in_specs=[pl.BlockSpec((tm, ATT_WIDTH), lambda i: (i, 0)), pl.BlockSpec((tm, ATT_WIDTH), lambda i: (i, 0)),
                  pl.BlockSpec((ATT_WIDTH, ATT_WIDTH), lambda i: (0, 0))],
        out_specs=pl.BlockSpec((tm, ATT_WIDTH), lambda i: (i, 0)),
        name="attn_delta", compiler_params=_cp(("parallel",)),
    )(o, do, ones)


def _attn_bwd_old(qkv, do_bf, c_cols, c_rows, lse_rows, dl_rows):
    S = qkv.shape[0]
    tk = _pick(S, (256,))
    tq = tk
    nq = S // tq
    nt = (((1,), (1,)), ((), ()))
    tn_dims = (((0,), (0,)), ((), ()))

    def body(q_ref, do_ref, k_ref, v_ref, cc_ref, cr_ref, lse_ref, dl_ref, dq_ref, dk_ref, dv_ref, dcs_ref, dcq_ref):
        kb = pl.program_id(1)

        @pl.when(kb == 0)
        def _():
            dq_ref[...] = jnp.zeros_like(dq_ref)
            dcq_ref[...] = jnp.zeros_like(dcq_ref)

        lane = lax.broadcasted_iota(jnp.int32, (tk, LANES), 1)
        rid = lax.broadcasted_iota(jnp.int32, (tk, tq), 0)
        cid = lax.broadcasted_iota(jnp.int32, (tk, tq), 1)
        k = k_ref[...]
        v = v_ref[...]
        dks, dvs, dcs = [], [], []
        for hh in range(2):
            sel = (lane >= GROUP_DIM) if hh else (lane < GROUP_DIM)
            km = jnp.where(sel, k, jnp.zeros_like(k))
            vm = jnp.where(sel, v, jnp.zeros_like(v))
            cs = cc_ref[:, hh * HEAD_DIM:hh * HEAD_DIM + 1]

            def step(qb, carry, km=km, vm=vm, cs=cs, hh=hh):
                dk_acc, dv_acc, dc_acc = carry
                qs = pl.multiple_of(qb * tq, tq)
                qblk = q_ref[pl.ds(qs, tq), :]
                doblk = do_ref[pl.ds(qs, tq), :]
                s = lax.dot_general(km, qblk, nt, preferred_element_type=F32)
                s = s + (cr_ref[hh:hh + 1, pl.ds(qs, tq)] - cs)
                p = jnp.exp(s - lse_ref[hh:hh + 1, pl.ds(qs, tq)])
                p = jnp.where((qb > kb) | (cid >= rid), p, 0.0)
                dp = lax.dot_general(vm, doblk, nt, preferred_element_type=F32)
                ds = p * (dp - dl_ref[hh:hh + 1, pl.ds(qs, tq)])
                ds_bf = ds.astype(BF16)
                dv_acc = dv_acc + jnp.dot(p.astype(BF16), doblk, preferred_element_type=F32)
                dk_acc = dk_acc + jnp.dot(ds_bf, qblk, preferred_element_type=F32)
                dc_acc = dc_acc + jnp.sum(ds, axis=-1, keepdims=True)
                dq_ref[pl.ds(qs, tq), :] += lax.dot_general(ds_bf, km, tn_dims, preferred_element_type=F32)
                dcq_ref[hh:hh + 1, pl.ds(qs, tq)] += jnp.sum(ds, axis=0, keepdims=True)
                return dk_acc, dv_acc, dc_acc

            init = (jnp.zeros((tk, LANES), F32), jnp.zeros((tk, LANES), F32), jnp.zeros((tk, 1), F32))
            dk_acc, dv_acc, dc_acc = lax.fori_loop(kb, nq, step, init)
            dks.append(dk_acc)
            dvs.append(dv_acc)
            dcs.append(dc_acc)
        dk_ref[...] = jnp.where(lane < GROUP_DIM, dks[0], dks[1])
        dv_ref[...] = jnp.where(lane < GROUP_DIM, dvs[0], dvs[1])
        dcs_ref[...] = jnp.where(lane == 0, -dcs[0], jnp.where(lane == 1, -dcs[1], 0.0))

    return pl.pallas_call(
        body,
        out_shape=(jax.ShapeDtypeStruct((S, ATT_WIDTH), F32), jax.ShapeDtypeStruct((S, ATT_WIDTH), F32),
                   jax.ShapeDtypeStruct((S, ATT_WIDTH), F32), jax.ShapeDtypeStruct((N_PAIRS, S, LANES), F32),
                   jax.ShapeDtypeStruct((N_PAIRS, 8, S), F32)),
        grid=(N_PAIRS, nq),
        in_specs=[pl.BlockSpec((S, LANES), lambda p, j: (0, p)),
                  pl.BlockSpec((S, LANES), lambda p, j: (0, p)),
                  pl.BlockSpec((tk, LANES), lambda p, j: (j, N_PAIRS + p)),
                  pl.BlockSpec((tk, LANES), lambda p, j: (j, 2 * N_PAIRS + p)),
                  pl.BlockSpec((None, tk, LANES), lambda p, j: (p, j, 0)),
                  pl.BlockSpec((None, 8, S), lambda p, j: (p, 0, 0)),
                  pl.BlockSpec((None, 8, S), lambda p, j: (p, 0, 0)),
                  pl.BlockSpec((None, 8, S), lambda p, j: (p, 0, 0))],
        out_specs=(pl.BlockSpec((S, LANES), lambda p, j: (0, p)),
                   pl.BlockSpec((tk, LANES), lambda p, j: (j, p)),
                   pl.BlockSpec((tk, LANES), lambda p, j: (j, p)),
                   pl.BlockSpec((None, tk, LANES), lambda p, j: (p, j, 0)),
                   pl.BlockSpec((None, 8, S), lambda p, j: (p, 0, 0))),
        name="attn_bwd", compiler_params=_cp(("parallel", "arbitrary"), VMEM_LIMIT),
    )(qkv, do_bf, qkv, qkv, c_cols, c_rows, lse_rows, dl_rows)


def _gmlp_bwd(ug, dsg, gain, w_s, wt_s, bias_full):
    S = ug.shape[0]
    tm = _pick(S, (512, 256, 128))
    n_chunks = tm // CHUNK
    n_i = S // tm
    ones = _group_ones()
    nt = (((1,), (1,)), ((), ()))

    def body(ug_ref, dsg_ref, gain_ref, w_ref, wt_ref, bias_ref, ones_ref, dug_ref, dw_ref, dgain_ref, dbias_ref,
             dbacc_ref):
        i = pl.program_id(0)

        @pl.when(i == 0)
        def _():
            dw_ref[...] = jnp.zeros_like(dw_ref)
            dgain_ref[...] = jnp.zeros_like(dgain_ref)
            dbacc_ref[...] = jnp.zeros_like(dbacc_ref)

        ones_m = ones_ref[...]
        pu = ug_ref[:, :GMLP_WIDTH]
        pg = ug_ref[:, GMLP_WIDTH:]
        u = _gelu(pu)
        vr = _gelu(pg)
        ms = _dot3(vr * vr, ones_m) * (1.0 / GROUP_DIM)
        rinv = lax.rsqrt(ms + EPS)
        vhat = vr * rinv
        gain_v = gain_ref[...]
        vn = (vhat * gain_v).astype(BF16)
        mixed = _gmlp_mixed(vn, w_ref, bias_ref[...], n_chunks)
        dsg_v = dsg_ref[...]
        du = dsg_v * mixed
        dmixed = dsg_v * u
        dm_bf = dmixed.astype(BF16)
        lane = lax.broadcasted_iota(jnp.int32, (CHUNK, LANES), 1)
        row = lax.broadcasted_iota(jnp.int32, (CHUNK, CHUNK), 0)
        col = lax.broadcasted_iota(jnp.int32, (CHUNK, CHUNK), 1)
        wts = [jnp.where(col >= row, wt_ref[g], 0.0).astype(BF16) for g in range(N_GROUPS)]
        dvn_rows = []
        dbsum = jnp.zeros((CHUNK, GMLP_WIDTH), F32)
        for ci in range(n_chunks):
            rs = slice(ci * CHUNK, (ci + 1) * CHUNK)
            dbsum = dbsum + dmixed[rs, :]
            cols = []
            for pp in range(N_GROUPS // 2):
                cs = slice(pp * LANES, (pp + 1) * LANES)
                dm = dm_bf[rs, cs]
                dm_lo = jnp.where(lane < GROUP_DIM, dm, jnp.zeros_like(dm))
                dm_hi = jnp.where(lane >= GROUP_DIM, dm, jnp.zeros_like(dm))
                vb = vn[rs, cs]
                dw_ref[2 * pp] += lax.dot_general(dm_lo, vb, nt, preferred_element_type=F32)
                dw_ref[2 * pp + 1] += lax.dot_general(dm_hi, vb, nt, preferred_element_type=F32)
                cols.append(jnp.dot(wts[2 * pp], dm_lo, preferred_element_type=F32)
                            + jnp.dot(wts[2 * pp + 1], dm_hi, preferred_element_type=F32))
            dvn_rows.append(jnp.concatenate(cols, axis=1))
        dvn = jnp.concatenate(dvn_rows, axis=0)
        dbacc_ref[...] += dbsum
        dgain_ref[0:1, :] += jnp.sum(dvn * vhat, axis=0, keepdims=True)
        dvhat = dvn * gain_v
        gm = _dot3(dvhat * vhat, ones_m) * (1.0 / GROUP_DIM)
        dvr = rinv * (dvhat - vhat * gm)
        dug_ref[:, :GMLP_WIDTH] = (du * _gelu_grad(pu)).astype(BF16)
        dug_ref[:, GMLP_WIDTH:] = (dvr * _gelu_grad(pg)).astype(BF16)

        @pl.when(i == n_i - 1)
        def _():
            for g in range(N_GROUPS):
                dw_ref[g] = jnp.where(row >= col, dw_ref[g], 0.0)
            dbias_ref[...] = _dot3(dbacc_ref[...], ones_m)

    return pl.pallas_call(
        body,
        out_shape=(jax.ShapeDtypeStruct((S, 2 * GMLP_WIDTH), BF16), jax.ShapeDtypeStruct((N_GROUPS, CHUNK, CHUNK), F32),
                   jax.ShapeDtypeStruct((8, GMLP_WIDTH), F32), jax.ShapeDtypeStruct((CHUNK, GMLP_WIDTH), F32)),
        grid=(n_i,),
        in_specs=[pl.BlockSpec((tm, 2 * GMLP_WIDTH), lambda i: (i, 0)), pl.BlockSpec((tm, GMLP_WIDTH), lambda i: (i, 1)),
                  pl.BlockSpec((1, GMLP_WIDTH), lambda i: (0, 0)),
                  pl.BlockSpec((N_GROUPS, CHUNK, CHUNK), lambda i: (0, 0, 0)),
                  pl.BlockSpec((N_GROUPS, CHUNK, CHUNK), lambda i: (0, 0, 0)),
                  pl.BlockSpec((CHUNK, GMLP_WIDTH), lambda i: (0, 0)),
                  pl.BlockSpec((GMLP_WIDTH, GMLP_WIDTH), lambda i: (0, 0))],
        out_specs=(pl.BlockSpec((tm, 2 * GMLP_WIDTH), lambda i: (i, 0)),
                   pl.BlockSpec((N_GROUPS, CHUNK, CHUNK), lambda i: (0, 0, 0)),
                   pl.BlockSpec((8, GMLP_WIDTH), lambda i: (0, 0)),
                   pl.BlockSpec((CHUNK, GMLP_WIDTH), lambda i: (0, 0))),
        scratch_shapes=[pltpu.VMEM((CHUNK, GMLP_WIDTH), F32)],
        name="gmlp_bwd", compiler_params=_cp(("arbitrary",), VMEM_LIMIT),
    )(ug, dsg, gain, w_s, wt_s, bias_full, ones)


def _gate_bwd(dcq, dck, zf):
    S = zf.shape[0]
    tm = _pick(S, (256,))
    n_i = S // tm
    triu = (lax.broadcasted_iota(jnp.int32, (tm, tm), 0) <= lax.broadcasted_iota(jnp.int32, (tm, tm), 1)).astype(BF16)

    def body(dcq_ref, dck_ref, zf_ref, tri_ref, dzf_ref, dbf_ref, carry_ref):
        i = pl.program_id(0)

        @pl.when(i == 0)
        def _():
            carry_ref[...] = jnp.zeros_like(carry_ref)
            dbf_ref[...] = jnp.zeros_like(dbf_ref)

        lane = lax.broadcasted_iota(jnp.int32, (tm, LANES), 1)
        dc = jnp.zeros((tm, LANES), F32)
        for p in range(N_PAIRS):
            slab = dcq_ref[p] + dck_ref[p]
            for hh in range(2):
                dc = dc + jnp.where(lane == 2 * p + hh, slab[:, hh:hh + 1], 0.0)
        dlf = _dot3l(tri_ref[...], dc) + carry_ref[0:1, :]
        carry_ref[0:1, :] = dlf[0:1, :]
        dz = jnp.where(lane < N_HEADS, dlf * _sigmoid(-zf_ref[...]), 0.0)
        dzf_ref[...] = dz.astype(BF16)
        dbf_ref[0:1, :] += jnp.sum(dz, axis=0, keepdims=True)

    return pl.pallas_call(
        body,
        out_shape=(jax.ShapeDtypeStruct((S, LANES), BF16), jax.ShapeDtypeStruct((8, LANES), F32)),
        grid=(n_i,),
        in_specs=[pl.BlockSpec((N_PAIRS, tm, LANES), lambda i: (0, n_i - 1 - i, 0)),
                  pl.BlockSpec((N_PAIRS, tm, LANES), lambda i: (0, n_i - 1 - i, 0)),
                  pl.BlockSpec((tm, LANES), lambda i: (n_i - 1 - i, 0)),
                  pl.BlockSpec((tm, tm), lambda i: (0, 0))],
        out_specs=(pl.BlockSpec((tm, LANES), lambda i: (n_i - 1 - i, 0)), pl.BlockSpec((8, LANES), lambda i: (0, 0))),
        scratch_shapes=[pltpu.VMEM((8, LANES), F32)],
        name="gate_bwd", compiler_params=_cp(("arbitrary",), VMEM_LIMIT),
    )(dcq, dck, zf, triu)


def _out_proj_fwd(att_bf, sg, w_out_bf, x, g_ffn):
    S = x.shape[0]
    tm = _pick(S, (512, 256))

    def body(a_ref, s_ref, w_ref, x_ref, g_ref, h_ref, hn_ref):
        h = (x_ref[...] + jnp.dot(a_ref[...], w_ref[:ATT_WIDTH, :], preferred_element_type=F32)
             + jnp.dot(s_ref[...], w_ref[ATT_WIDTH:, :], preferred_element_type=F32))
        h_ref[...] = h
        r = lax.rsqrt(jnp.mean(h * h, axis=-1, keepdims=True) + EPS)
        hn_ref[...] = ((h * r) * g_ref[...]).astype(BF16)

    row = pl.BlockSpec((tm, D_MODEL), lambda i: (i, 0))
    half = pl.BlockSpec((tm, ATT_WIDTH), lambda i: (i, 0))
    return pl.pallas_call(
        body, out_shape=(jax.ShapeDtypeStruct((S, D_MODEL), F32), jax.ShapeDtypeStruct((S, D_MODEL), BF16)),
        grid=(S // tm,),
        in_specs=[half, half, pl.BlockSpec((D_MODEL, D_MODEL), lambda i: (0, 0)), row,
                  pl.BlockSpec((1, D_MODEL), lambda i: (0, 0))],
        out_specs=(row, row), name="out_proj", compiler_params=_cp(("parallel",), VMEM_LIMIT),
    )(att_bf, sg, w_out_bf, x, g_ffn)


_IN_PIECES = ((0, ATT_WIDTH), (ATT_WIDTH, ATT_WIDTH), (2 * ATT_WIDTH, ATT_WIDTH), (QKV, 2 * GMLP_WIDTH), (UG_END, LANES))


def _inproj_bwd_dx(pieces, w_pad, x, g_mix, dh1):
    S = x.shape[0]
    tm = _pick(S, (512, 256))

    def body(*refs):
        p_refs, (w_ref, x_ref, g_ref, r_ref, dx_ref, dg_ref) = refs[:5], refs[5:]
        i = pl.program_id(0)

        @pl.when(i == 0)
        def _():
            dg_ref[...] = jnp.zeros_like(dg_ref)

        dxn = None
        for p_ref, (c0, width) in zip(p_refs, _IN_PIECES):
            part = jnp.dot(p_ref[...], w_ref[c0:c0 + width, :], preferred_element_type=F32)
            dxn = part if dxn is None else dxn + part
        xf = x_ref[...]
        r = lax.rsqrt(jnp.mean(xf * xf, axis=-1, keepdims=True) + EPS)
        xhat = xf * r
        dg_ref[0:1, :] += jnp.sum(dxn * xhat, axis=0, keepdims=True)
        dhat = dxn * g_ref[...]
        dx_ref[...] = r_ref[...] + r * (dhat - xhat * jnp.mean(dhat * xhat, axis=-1, keepdims=True))

    row = pl.BlockSpec((tm, D_MODEL), lambda i: (i, 0))
    return pl.pallas_call(
        body, out_shape=(jax.ShapeDtypeStruct((S, D_MODEL), F32), jax.ShapeDtypeStruct((8, D_MODEL), F32)),
        grid=(S // tm,),
        in_specs=[pl.BlockSpec((tm, width), lambda i: (i, 0)) for _, width in _IN_PIECES]
        + [pl.BlockSpec((IN_PAD, D_MODEL), lambda i: (0, 0)), row, pl.BlockSpec((1, D_MODEL), lambda i: (0, 0)), row],
        out_specs=(row, pl.BlockSpec((8, D_MODEL), lambda i: (0, 0))),
        name="in_proj_dx", compiler_params=_cp(("arbitrary",), VMEM_LIMIT),
    )(*pieces, w_pad, x, g_mix, dh1)


def _inproj_bwd_dw(xn, pieces):
    S = xn.shape[0]
    tk = _pick(S, (512, 256))

    def body(*refs):
        x_ref, p_refs, o_ref = refs[0], refs[1:6], refs[6]
        k = pl.program_id(0)

        @pl.when(k == 0)
        def _():
            o_ref[...] = jnp.zeros_like(o_ref)

        xb = x_ref[...]
        for p_ref, (c0, width) in zip(p_refs, _IN_PIECES):
            o_ref[:, c0:c0 + width] += lax.dot_general(xb, p_ref[...], _TN, preferred_element_type=F32)

    return pl.pallas_call(
        body, out_shape=jax.ShapeDtypeStruct((D_MODEL, IN_PAD), F32), grid=(S // tk,),
        in_specs=[pl.BlockSpec((tk, D_MODEL), lambda k: (k, 0))]
        + [pl.BlockSpec((tk, width), lambda k: (k, 0)) for _, width in _IN_PIECES],
        out_specs=pl.BlockSpec((D_MODEL, IN_PAD), lambda k: (0, 0)),
        name="in_proj_dw", compiler_params=_cp(("arbitrary",), VMEM_LIMIT),
    )(xn, *pieces)


def _adamw(w, m, v, parts, name):
    R, C = w.shape[-2:]
    tr = R
    for cand in (256, 128, 64, 32, 16, 8):
        if R % cand == 0 and R > cand:
            tr = cand
            break
    c1 = 1.0 / (1.0 - ADAM_B1 ** ADAM_STEP)
    c2 = 1.0 / (1.0 - ADAM_B2 ** ADAM_STEP)

    def body(w_ref, m_ref, v_ref, p_ref, g_ref, d_ref, nm_ref, nv_ref):
        g = p_ref[0].astype(F32)
        for j in range(1, N_DEV):
            g = g + p_ref[j].astype(F32)
        g_ref[...] = g
        nm = ADAM_B1 * m_ref[...] + (1.0 - ADAM_B1) * g
        nv = ADAM_B2 * v_ref[...] + (1.0 - ADAM_B2) * (g * g)
        nm_ref[...] = nm
        nv_ref[...] = nv
        d_ref[...] = -ADAM_LR * ((nm * c1) / (jnp.sqrt(nv * c2) + ADAM_EPS) + ADAM_WD * w_ref[...])

    if w.ndim == 3:
        spec = pl.BlockSpec((None, tr, C), lambda i: (0, i, 0))
    else:
        spec = pl.BlockSpec((tr, C), lambda i: (i, 0))
    shp = jax.ShapeDtypeStruct(w.shape, F32)
    return pl.pallas_call(
        body, out_shape=(shp, shp, shp, shp), grid=(R // tr,),
        in_specs=[spec, spec, spec, pl.BlockSpec((N_DEV, tr, C), lambda i: (0, i, 0))],
        out_specs=(spec, spec, spec, spec),
        name=name, compiler_params=_cp(("parallel",), VMEM_LIMIT),
    )(w, m, v, parts)


def _place():
    x, y, c = lax.axis_index("x"), lax.axis_index("y"), lax.axis_index("c")
    return x, y, c


def _all_gather(blocks, name):
    n = len(blocks)

    def body(*refs):
        ins, outs = refs[:n], refs[n:2 * n]
        send_sems, recv_sems, local_sems = refs[2 * n:]
        x, y, c = _place()
        me, sibling = (x, y, c), (x, y, 1 - c)
        chips = [(1 - x, y), (x, 1 - y), (1 - x, 1 - y)]
        sends = []
        for a in range(n):
            out = outs[a]

            def slot(px, py, pc, out=out):
                return out.at[4 * px + 2 * py + pc]

            def copy(k, block, to, src=None, a=a, slot=slot):
                return pltpu.make_async_remote_copy(
                    src_ref=slot(*block) if src is None else src, dst_ref=slot(*block),
                    send_sem=send_sems.at[a, k], recv_sem=recv_sems.at[a, k], device_id=to, device_id_type=MESH)

            mine = pltpu.make_async_copy(ins[a], slot(*me), local_sems.at[a])
            mine.start()
            first = [copy(0, me, sibling, src=ins[a])]
            first += [copy(1 + j, me, (*chip, c), src=ins[a]) for j, chip in enumerate(chips)]
            for cp in first:
                cp.start()
            sends.append((mine, first, copy))
        for a in range(n):
            mine, first, copy = sends[a]
            passed = [copy(4 + j, (*chip, c), sibling) for j, chip in enumerate(chips)]
            for j, chip in enumerate(chips):
                copy(1 + j, (*chip, c), me).wait_recv()
                passed[j].start()
            copy(0, sibling, me).wait_recv()
            for j, chip in enumerate(chips):
                copy(4 + j, (*chip, 1 - c), me).wait_recv()
            for cp in first + passed:
                cp.wait_send()
            mine.wait()

    any_spec = pl.BlockSpec(memory_space=pl.ANY)
    return pl.pallas_call(
        body, out_shape=tuple(jax.ShapeDtypeStruct((N_DEV,) + b.shape, b.dtype) for b in blocks),
        in_specs=[any_spec] * n, out_specs=tuple([any_spec] * n),
        scratch_shapes=[pltpu.SemaphoreType.DMA((n, 7)), pltpu.SemaphoreType.DMA((n, 7)), pltpu.SemaphoreType.DMA((n,))],
        name=name,
    )(*blocks)


def _exchange_shards(parts, name):
    n = len(parts)

    def body(*refs):
        ins, outs = refs[:n], refs[n:2 * n]
        send_sems, recv_sems, local_sems = refs[2 * n:]
        x, y, c = _place()
        me = 4 * x + 2 * y + c
        started = []
        for a in range(n):
            mine = pltpu.make_async_copy(ins[a].at[me], outs[a].at[me], local_sems.at[a])
            mine.start()
            started.append(mine)
            for k in range(1, N_DEV):
                px, py, pc = x ^ ((k >> 2) & 1), y ^ ((k >> 1) & 1), c ^ (k & 1)
                cp = pltpu.make_async_remote_copy(
                    src_ref=ins[a].at[4 * px + 2 * py + pc], dst_ref=outs[a].at[me],
                    send_sem=send_sems.at[a, k - 1], recv_sem=recv_sems.at[a, k - 1],
                    device_id=(px, py, pc), device_id_type=MESH)
                cp.start()
                started.append(cp)
        for cp in started:
            cp.wait()

    any_spec = pl.BlockSpec(memory_space=pl.ANY)
    return pl.pallas_call(
        body, out_shape=tuple(jax.ShapeDtypeStruct(p.shape, p.dtype) for p in parts),
        in_specs=[any_spec] * n, out_specs=tuple([any_spec] * n),
        scratch_shapes=[pltpu.SemaphoreType.DMA((n, 7)), pltpu.SemaphoreType.DMA((n, 7)), pltpu.SemaphoreType.DMA((n,))],
        name=name,
    )(*parts)


_HBM = pl.BlockSpec(memory_space=pltpu.HBM)
_SEM = pl.BlockSpec(memory_space=pltpu.SEMAPHORE)
_EFFECT = pltpu.SideEffectType.DATAFLOW_SIDE_EFFECTING


def _peers(x, y, c):
    out = []
    for k in range(1, N_DEV):
        px, py, pc = x ^ ((k >> 2) & 1), y ^ ((k >> 1) & 1), c ^ (k & 1)
        out.append((k, (px, py, pc), 4 * px + 2 * py + pc))
    return out


def _xchg_copies(src_refs, land_refs, send_sems, recv_sems, scatter):
    x, y, c = _place()
    me = 4 * x + 2 * y + c
    copies = []
    for a, (src, land) in enumerate(zip(src_refs, land_refs)):
        for k, place, idx in _peers(x, y, c):
            j = a * (N_DEV - 1) + k - 1
            copies.append(pltpu.make_async_remote_copy(
                src_ref=src.at[idx] if scatter[a] else src, dst_ref=land.at[me],
                send_sem=send_sems[j], recv_sem=recv_sems[j], device_id=place, device_id_type=MESH))
    return copies


def _xchg_start(srcs, scatter, name):
    n = len(srcs)
    lands = [lax.empty((N_DEV,) + (s.shape[1:] if sc else s.shape), s.dtype) for s, sc in zip(srcs, scatter)]

    ns = n * (N_DEV - 1)

    def body(*refs):
        sems = refs[2 * n:2 * n + 2 * ns]
        for cp in _xchg_copies(refs[:n], refs[n:2 * n], sems[:ns], sems[ns:], scatter):
            cp.start()
        token = refs[-1]
        token[...] = jnp.zeros_like(token)

    both = list(srcs) + lands
    res = pl.pallas_call(
        body, name=name,
        out_shape=(*[pltpu.SemaphoreType.DMA(())] * (2 * ns),
                   *[pltpu.HBM(a.shape, a.dtype) for a in both], jax.ShapeDtypeStruct((8, LANES), F32)),
        in_specs=[_HBM] * (2 * n),
        out_specs=(*([_SEM] * (2 * ns)), *([_HBM] * (2 * n)), pl.BlockSpec(memory_space=pltpu.VMEM)),
        input_output_aliases={i: 2 * ns + i for i in range(2 * n)},
        compiler_params=pltpu.CompilerParams(has_side_effects=_EFFECT),
    )(*[pltpu.with_memory_space_constraint(a, pltpu.HBM) for a in both])
    return (tuple(res[:2 * ns]), tuple(res[2 * ns:2 * ns + 2 * n])), res[-1]


def _xchg_wait(handle, scatter, after, name):
    sems, thru = handle
    n = len(thru) // 2
    ns = len(sems) // 2

    def body(*refs):
        got = refs[2 * n:2 * n + 2 * ns]
        for cp in _xchg_copies(refs[:n], refs[n:2 * n], got[:ns], got[ns:], scatter):
            cp.wait_send()
            cp.wait_recv()

    outs = pl.pallas_call(
        body, name=name, out_shape=tuple(pltpu.HBM(a.shape, a.dtype) for a in thru),
        in_specs=[_HBM] * (2 * n) + [_SEM] * (2 * ns) + [pl.BlockSpec(memory_space=pl.ANY)],
        out_specs=tuple([_HBM] * (2 * n)), input_output_aliases={i: i for i in range(2 * n)},
        compiler_params=pltpu.CompilerParams(has_side_effects=_EFFECT),
    )(*thru, *sems, after)
    return outs[:n], outs[n:]


def _tie(a, token):
    return a if token is None else a + token[0, 0].astype(a.dtype)


def _rows128(a):
    flat = a.reshape(-1)
    rows = -(-flat.shape[0] // LANES)
    rows = -(-rows // 8) * 8
    return jnp.pad(flat, (0, rows * LANES - flat.shape[0])).reshape(rows, LANES)


def _local_step(x, target, norm_mix_g, w_in_t, b_forget, gmlp_norm_g, w_spatial, b_spatial, norm_ffn_g, conv_b,
                norm_final_g, rest_fn, send_fn, small_fn, token=None):
    f = D_FF
    g_mix = norm_mix_g.reshape(1, D_MODEL)
    w_pad = jnp.pad(w_in_t, ((0, IN_PAD - IN_COLS), (0, 0)))
    bf_pad = jnp.pad(b_forget.reshape(1, N_HEADS), ((0, 0), (0, LANES - N_HEADS)))
    xn, qa, ka, va, ug, zf = _inproj_fwd(x, _tie(g_mix, token), w_pad, bf_pad)
    bias_full = jnp.repeat(b_spatial.reshape(N_GROUPS, CHUNK).T, GROUP_DIM, axis=1)
    w_s = w_spatial.reshape(N_GROUPS, CHUNK, CHUNK)
    gain = gmlp_norm_g.reshape(1, GMLP_WIDTH)
    sg = _gmlp_fwd(ug, gain, w_s, bias_full)
    att, lse, att_bf = _attn_fwd(qa, ka, va)
    w_out_bf, w_up_bf, conv_w, w_down_bf = rest_fn(att_bf)
    g_ffn = norm_ffn_g.reshape(1, D_MODEL)
    h1, hn = _out_proj_fwd(att_bf, sg, w_out_bf, x, g_ffn)
    cw = jnp.pad(conv_w.reshape(3, 2, f).transpose(1, 0, 2), ((0, 0), (0, 5), (0, 0)))
    cb = conv_b.reshape(2, 1, f)
    hu, hc, act = _ffn_up_conv(hn, w_up_bf, cw, cb)
    loss_blk, dh2, dh2_bf, dg_final = _ffn_down_loss(act, w_down_bf, h1, norm_final_g.reshape(1, D_MODEL), target)
    dw_down = _mm(act, dh2_bf, mode="tn", out_dtype=F32, tm=1408, tn=1024, tk=2048, name="ffn_down_dw")
    dact = _mm(dh2_bf, w_down_bf, mode="nt", out_dtype=F32, tm=512, tn=1408, tk=1024, outer="j", name="ffn_down_dx")
    dhu, dcw = _conv_gate_bwd(hc, hu, dact, _tie(cw, send_fn("w_down", dw_down)))
    dw_up = _mm(hn, dhu, mode="tn", out_dtype=F32, tm=1024, tn=1408, tk=2048, b_halves=True, outer="j", name="ffn_up_dw")
    dh1, dh1_bf, dg_ffn = _ffn_up_dx_rms(dhu, w_up_bf, h1, _tie(g_ffn, send_fn("w_up", dw_up)), dh2)
    dmix = _mm(dh1_bf, w_out_bf, mode="nt", out_dtype=F32, tm=512, tn=1024, tk=1024, name="out_proj_dx")
    dw_out = jnp.concatenate(
        [_mm(att_bf, dh1_bf, mode="tn", out_dtype=F32, tm=512, tn=1024, tk=1024, name="out_proj_dw_att"),
         _mm(sg, dh1_bf, mode="tn", out_dtype=F32, tm=512, tn=1024, tk=1024, name="out_proj_dw_sg")], axis=0)
    qb, doa = _attn_prep(att, lse, dmix, qa)
    dq, dk, dv, dcq, dck = _attn_bwd(qb, ka, va, doa)
    wt_s = w_s.transpose(0, 2, 1)
    dug, dw_s, dgain, dbias = _gmlp_bwd(ug, dmix, _tie(gain, send_fn("w_out", dw_out)), w_s, wt_s, bias_full)
    dzf, dbf = _gate_bwd(dcq, dck, zf)
    grad_x, dg_mix = _inproj_bwd_dx((dq, dk, dv, dug, dzf), w_pad, x, g_mix, dh1)
    grads = dict(
        norm_mix_g=dg_mix[0:1, :],
        b_forget=dbf[0:1, :N_HEADS],
        gmlp_norm_g=dgain[0:1, :],
        w_spatial=dw_s,
        b_spatial=dbias[:, ::GROUP_DIM].T,
        norm_ffn_g=dg_ffn[0:1, :],
        conv_w=dcw[:, 0:3, :].transpose(1, 0, 2).reshape(3, 2 * f),
        conv_b=dcw[:, 3, :].reshape(1, 2 * f),
        norm_final_g=dg_final[0, :],
    )
    token = small_fn(loss_blk[0, 0], grads)
    dw_in = _inproj_bwd_dw(xn, (dq, dk, dv, dug, _tie(dzf, token)))
    return grad_x, send_fn("w_in", dw_in[:, :IN_COLS])


SMALL = ("norm_mix_g", "b_forget", "gmlp_norm_g", "w_spatial", "b_spatial", "norm_ffn_g", "conv_b", "norm_final_g")


def kernel(x, norm_mix_g, w_in, b_forget, gmlp_norm_g, w_spatial, b_spatial, w_out, norm_ffn_g, w_up, conv_w, conv_b, w_down, norm_final_g, loss_target, m_norm_mix_g, m_w_in, m_b_forget, m_gmlp_norm_g, m_w_spatial, m_b_spatial, m_w_out, m_norm_ffn_g, m_w_up, m_conv_w, m_conv_b, m_w_down, m_norm_final_g, v_norm_mix_g, v_w_in, v_b_forget, v_gmlp_norm_g, v_w_spatial, v_b_spatial, v_w_out, v_norm_ffn_g, v_w_up, v_conv_w, v_conv_b, v_w_down, v_norm_final_g):
    weights = dict(norm_mix_g=norm_mix_g, w_in=w_in, b_forget=b_forget, gmlp_norm_g=gmlp_norm_g, w_spatial=w_spatial,
                   b_spatial=b_spatial, w_out=w_out, norm_ffn_g=norm_ffn_g, w_up=w_up, conv_w=conv_w, conv_b=conv_b,
                   w_down=w_down, norm_final_g=norm_final_g)
    m_in = dict(norm_mix_g=m_norm_mix_g, w_in=m_w_in, b_forget=m_b_forget, gmlp_norm_g=m_gmlp_norm_g,
                w_spatial=m_w_spatial, b_spatial=m_b_spatial, w_out=m_w_out, norm_ffn_g=m_norm_ffn_g, w_up=m_w_up,
                conv_w=m_conv_w, conv_b=m_conv_b, w_down=m_w_down, norm_final_g=m_norm_final_g)
    v_in = dict(norm_mix_g=v_norm_mix_g, w_in=v_w_in, b_forget=v_b_forget, gmlp_norm_g=v_gmlp_norm_g,
                w_spatial=v_w_spatial, b_spatial=v_b_spatial, w_out=v_w_out, norm_ffn_g=v_norm_ffn_g, w_up=v_w_up,
                conv_w=v_conv_w, conv_b=v_conv_b, w_down=v_w_down, norm_final_g=v_norm_final_g)
    order = list(weights)
    me = 4 * lax.axis_index("x") + 2 * lax.axis_index("y") + lax.axis_index("c")
    n_in, n_up = w_in.shape[2], w_up.shape[2]
    r_out, r_down = w_out.shape[1], w_down.shape[1]

    def with_mine(landed, mine):
        return lax.dynamic_update_index_in_dim(landed, mine, me, 0)

    up_blk = w_up[0].T.astype(BF16)
    out_blk = w_out[0].astype(BF16)
    down_blk = w_down[0].astype(BF16)
    taps_blk = jnp.pad(conv_w[0], ((0, 5), (0, 0)))
    (in_all,) = _all_gather([w_in[0].T.astype(BF16)], "gather_w_in")
    in_all, rest_blocks = lax.optimization_barrier((in_all, [up_blk, out_blk, down_blk, taps_blk]))
    rest_handle, token = _xchg_start(rest_blocks, [False] * 4, "gather_rest_start")
    w_in_t = in_all.reshape(N_DEV * n_in, D_MODEL)

    def rest_fn(after):
        mine, landed = _xchg_wait(rest_handle, [False] * 4, after, "gather_rest_wait")
        up_all, out_all, down_all, taps_all = [with_mine(l, b) for l, b in zip(landed, mine)]
        return (out_all.reshape(N_DEV * r_out, D_MODEL), up_all.reshape(N_DEV * n_up, D_MODEL),
                taps_all[:, :3, :].transpose(1, 0, 2).reshape(3, N_DEV * n_up),
                down_all.reshape(N_DEV * r_down, D_MODEL))

    sent = {}

    def send_fn(name, grad):
        if name == "w_in":
            parts = grad.reshape(D_MODEL, N_DEV, -1).transpose(1, 0, 2).astype(BF16)
        elif name == "w_up":
            parts = grad.reshape(D_MODEL, N_DEV, -1).transpose(1, 0, 2)
        else:
            parts = grad.reshape(N_DEV, -1, D_MODEL)
        sent[name], tok = _xchg_start([parts], [True], "scatter_" + name + "_start")
        return tok

    small = {}

    def small_fn(loss_local, g):
        packed = [_rows128(g[k]) for k in SMALL] + [_rows128(loss_local.reshape(1)), _rows128(g["conv_w"])]
        small["sizes"] = [p.shape[0] for p in packed]
        small["handle"], tok = _xchg_start([jnp.concatenate(packed, axis=0)], [False], "gather_small_start")
        return tok

    grad_x, after = _local_step(
        x[0], loss_target[0], norm_mix_g, w_in_t, b_forget, gmlp_norm_g, w_spatial, b_spatial, norm_ffn_g, conv_b,
        norm_final_g, rest_fn, send_fn, small_fn, token)

    outs = {}

    def update_big(name, after):
        (parts,), (landed,) = _xchg_wait(sent[name], [True], after, "scatter_" + name + "_wait")
        got = with_mine(landed, lax.dynamic_index_in_dim(parts, me, 0, keepdims=False))
        outs[name] = tuple(_adamw(weights[name], m_in[name], v_in[name], got, "adamw_" + name))
        return outs[name][0]

    for name in ("w_down", "w_up", "w_out"):
        after = update_big(name, after)

    (mine,), (landed,) = _xchg_wait(small["handle"], [False], after, "gather_small_wait")
    small_all = with_mine(landed, mine)
    sizes = small["sizes"]
    n_small_rows = sum(sizes[:-2])

    def pack(src):
        return jnp.concatenate([_rows128(src[k]) for k in SMALL] + [jnp.zeros((sizes[-2], LANES), F32)], axis=0)

    n_adam_rows = n_small_rows + sizes[-2]
    sg_, sd_, sm_, sv_ = _adamw(pack(weights), pack(m_in), pack(v_in), small_all[:, :n_adam_rows, :], "adamw_small")
    loss = sg_[n_small_rows, 0]
    off = 0
    for k, rows in zip(SMALL, sizes[:-2]):
        shp = weights[k].shape
        cnt = math.prod(shp)
        outs[k] = tuple(a[off:off + rows].reshape(-1)[:cnt].reshape(shp) for a in (sg_, sd_, sm_, sv_))
        off += rows
    taps_parts = small_all[:, n_adam_rows:, :].reshape(N_DEV, -1)[:, :3 * N_DEV * n_up].reshape(N_DEV, 3, N_DEV * n_up)
    taps_mine = lax.dynamic_slice_in_dim(taps_parts, me * n_up, n_up, axis=2)
    taps_mine = jnp.pad(taps_mine, ((0, 0), (0, 5), (0, 0)))

    def pad8(a):
        return jnp.pad(a[0], ((0, 5), (0, 0)))

    res = _adamw(pad8(conv_w), pad8(m_conv_w), pad8(v_conv_w), taps_mine, "adamw_conv_w")
    outs["conv_w"] = tuple(a[:3][None] for a in res)
    update_big("w_in", sg_)

    return (loss, grad_x[None], *[outs[k][0] for k in order], *[outs[k][1] for k in order],
            *[outs[k][2] for k in order], *[outs[k][3] for k in order])
```

```python
import functools
import math

import jax
import jax.numpy as jnp
from jax import lax
from jax.experimental import pallas as pl
from jax.experimental.pallas import tpu as pltpu

F32 = jnp.float32
BF16 = jnp.bfloat16

N_DEV = 8
D_MODEL = 1024
ATT_WIDTH = 512
GMLP_WIDTH = 512
HEAD_DIM = 64
N_HEADS = 8
N_PAIRS = 4
N_GROUPS = 8
GROUP_DIM = 64
CHUNK = 128
D_FF = 2816
IN_COLS = 2568
IN_PAD = 2688
QKV = 1536
UG_END = 2560
EPS = 1e-6
LANES = 128

ADAM_LR = 0.001
ADAM_B1 = 0.9
ADAM_B2 = 0.999
ADAM_EPS = 1e-08
ADAM_WD = 0.01
ADAM_STEP = 10

ATT_TQ = 1024
ATT_TK = 1024
FFN_TM, FFN_TN = 512, 1408
CONV_TM, CONV_TN = 256, 1408
VMEM_LIMIT = 56 * 1024 * 1024
MESH = pl.DeviceIdType.MESH


def _cp(sem, vmem=None):
    return pltpu.CompilerParams(dimension_semantics=sem, vmem_limit_bytes=vmem)


def _pick(n, prefs):
    for p in prefs:
        if n % p == 0:
            return p
    return n


def _split3(x):
    hi = x.astype(BF16)
    r1 = x - hi.astype(F32)
    mid = r1.astype(BF16)
    lo = (r1 - mid.astype(F32)).astype(BF16)
    return hi, mid, lo


def _dot3(x, ones_bf):
    hi, mid, lo = _split3(x)
    d = functools.partial(jnp.dot, preferred_element_type=F32)
    return d(hi, ones_bf) + d(mid, ones_bf) + d(lo, ones_bf)


def _dot3l(ones_bf, x):
    hi, mid, lo = _split3(x)
    d = functools.partial(jnp.dot, preferred_element_type=F32)
    return d(ones_bf, hi) + d(ones_bf, mid) + d(ones_bf, lo)


def _gelu(x):
    k = math.sqrt(2.0 / math.pi)
    t = jnp.tanh(k * (x + 0.044715 * (x * x * x)))
    return 0.5 * x * (1.0 + t)


def _gelu_grad(x):
    k = math.sqrt(2.0 / math.pi)
    x2 = x * x
    t = jnp.tanh(k * (x + 0.044715 * (x2 * x)))
    return 0.5 * (1.0 + t) + 0.5 * x * (1.0 - t * t) * (k * (1.0 + 3.0 * 0.044715 * x2))


def _sigmoid(x):
    return 1.0 / (1.0 + jnp.exp(-x))


def _mm(a, b, *, mode, out_dtype, tm, tn, tk, name, res=None, a_halves=False, b_halves=False,
        out_halves=False, outer="i"):
    if mode == "tn":
        K, M = a.shape[-2], a.shape[-1] * (2 if a_halves else 1)
    else:
        M, K = a.shape[-2], a.shape[-1] * (2 if a_halves else 1)
    if mode == "nt":
        N = b.shape[-2]
        assert b.shape[-1] == K
    else:
        N = b.shape[-1] * (2 if b_halves else 1)
    tm, tn, tk = min(tm, M), min(tn, N), min(tk, K)
    assert M % tm == 0 and N % tn == 0 and K % tk == 0, (name, M, N, K, tm, tn, tk)
    nm, nn, nk = M // tm, N // tn, K // tk

    def ij(g0, g1):
        return (g0, g1) if outer == "i" else (g1, g0)

    if mode == "nn":
        dims = (((1,), (0,)), ((), ()))
        if a_halves:
            nkh = nk // 2
            a_spec = pl.BlockSpec((None, tm, tk), lambda g0, g1, k: (k // nkh, ij(g0, g1)[0], k % nkh))
        else:
            a_spec = pl.BlockSpec((tm, tk), lambda g0, g1, k: (ij(g0, g1)[0], k))
        b_spec = pl.BlockSpec((tk, tn), lambda g0, g1, k: (k, ij(g0, g1)[1]))
    elif mode == "nt":
        dims = (((1,), (1,)), ((), ()))
        if a_halves:
            nkh = nk // 2
            a_spec = pl.BlockSpec((None, tm, tk), lambda g0, g1, k: (k // nkh, ij(g0, g1)[0], k % nkh))
        else:
            a_spec = pl.BlockSpec((tm, tk), lambda g0, g1, k: (ij(g0, g1)[0], k))
        b_spec = pl.BlockSpec((tn, tk), lambda g0, g1, k: (ij(g0, g1)[1], k))
    else:
        dims = (((0,), (0,)), ((), ()))
        if a_halves:
            nmh = nm // 2
            a_spec = pl.BlockSpec((None, tk, tm), lambda g0, g1, k: (ij(g0, g1)[0] // nmh, k, ij(g0, g1)[0] % nmh))
        else:
            a_spec = pl.BlockSpec((tk, tm), lambda g0, g1, k: (k, ij(g0, g1)[0]))
        if b_halves:
            nnh = nn // 2
            b_spec = pl.BlockSpec((None, tk, tn), lambda g0, g1, k: (ij(g0, g1)[1] // nnh, k, ij(g0, g1)[1] % nnh))
        else:
            b_spec = pl.BlockSpec((tk, tn), lambda g0, g1, k: (k, ij(g0, g1)[1]))
    if out_halves:
        nnh = nn // 2
        o_spec = pl.BlockSpec((None, tm, tn), lambda g0, g1, k: (ij(g0, g1)[1] // nnh, ij(g0, g1)[0], ij(g0, g1)[1] % nnh))
        o_shape = jax.ShapeDtypeStruct((2, M, N // 2), out_dtype)
    else:
        o_spec = pl.BlockSpec((tm, tn), lambda g0, g1, k: ij(g0, g1))
        o_shape = jax.ShapeDtypeStruct((M, N), out_dtype)
    in_specs = [a_spec, b_spec]
    args = [a, b]
    if res is not None:
        in_specs.append(pl.BlockSpec((tm, tn), lambda g0, g1, k: ij(g0, g1)))
        args.append(res)

    def body(*refs):
        if res is not None:
            a_ref, b_ref, r_ref, o_ref = refs[:4]
        else:
            a_ref, b_ref, o_ref = refs[:3]
            r_ref = None
        part = lax.dot_general(a_ref[...], b_ref[...], dims, preferred_element_type=F32)
        if nk == 1:
            if r_ref is not None:
                part = part + r_ref[...]
            o_ref[...] = part.astype(out_dtype)
            return
        acc_ref = refs[-1]
        k = pl.program_id(2)

        @pl.when(k == 0)
        def _():
            acc_ref[...] = part

        @pl.when(k > 0)
        def _():
            acc_ref[...] += part

        @pl.when(k == nk - 1)
        def _():
            tot = acc_ref[...]
            if r_ref is not None:
                tot = tot + r_ref[...]
            o_ref[...] = tot.astype(out_dtype)

    grid = (nm, nn, nk) if outer == "i" else (nn, nm, nk)
    scratch = [] if nk == 1 else [pltpu.VMEM((tm, tn), F32)]
    return pl.pallas_call(
        body, out_shape=o_shape, grid=grid, in_specs=in_specs, out_specs=o_spec, scratch_shapes=scratch,
        name=name, compiler_params=_cp(("parallel", "parallel", "arbitrary"), VMEM_LIMIT),
    )(*args)


def _aug(lane, terms):
    out = 0.0
    for j, t in enumerate(terms):
        out = jnp.where(lane == HEAD_DIM + j, t, out)
    return out


def _split3f(x):
    hi, mid, lo = _split3(x)
    return [hi.astype(F32), mid.astype(F32), lo.astype(F32)]


def _inproj_fwd(x, g_mix, w_pad, bf_pad):
    S = x.shape[0]
    tm = _pick(S, (512, 256))
    tri = (lax.broadcasted_iota(jnp.int32, (tm, tm), 0) >= lax.broadcasted_iota(jnp.int32, (tm, tm), 1)).astype(BF16)

    def body(x_ref, g_ref, w_ref, bf_ref, tri_ref, xn_ref, qa_ref, ka_ref, va_ref, ug_ref, zf_ref, carry_ref):
        i = pl.program_id(0)

        @pl.when(i == 0)
        def _():
            carry_ref[...] = jnp.zeros_like(carry_ref)

        xf = x_ref[...]
        r = lax.rsqrt(jnp.mean(xf * xf, axis=-1, keepdims=True) + EPS)
        xn = ((xf * r) * g_ref[...]).astype(BF16)
        xn_ref[...] = xn
        proj = lax.dot_general(xn, w_ref[...], _NT, preferred_element_type=F32)
        ug_ref[...] = proj[:, QKV:UG_END]
        zf = proj[:, UG_END:] + bf_ref[...]
        zf_ref[...] = zf
        lf = jnp.minimum(zf, 0.0) - jnp.log(1.0 + jnp.exp(-jnp.abs(zf)))
        c = _dot3l(tri_ref[...], lf) + carry_ref[0:1, :]
        carry_ref[0:1, :] = c[tm - 1:tm, :]
        c3 = _split3f(c)
        lane = lax.broadcasted_iota(jnp.int32, (tm, LANES), 1)
        ones3 = [1.0, 1.0, 1.0]
        for h in range(N_HEADS):
            p, odd = h // 2, h % 2
            ch = [t[:, h:h + 1] for t in c3]

            def head(base, scale=None, p=p, odd=odd):
                blk = proj[:, base + p * LANES:base + (p + 1) * LANES]
                if scale is not None:
                    blk = blk * scale
                return pltpu.roll(blk, HEAD_DIM, 1) if odd else blk

            cols = slice(h * LANES, (h + 1) * LANES)
            qa_ref[:, cols] = jnp.where(lane < HEAD_DIM, head(0, HEAD_DIM ** -0.5), _aug(lane, ch + ones3)).astype(BF16)
            ka_ref[:, cols] = jnp.where(lane < HEAD_DIM, head(ATT_WIDTH),
                                        _aug(lane, ones3 + [-t for t in ch] + ones3)).astype(BF16)
            va_ref[:, cols] = jnp.where(lane < HEAD_DIM, head(2 * ATT_WIDTH), _aug(lane, ones3)).astype(BF16)

    wide = N_HEADS * LANES
    return pl.pallas_call(
        body,
        out_shape=(jax.ShapeDtypeStruct((S, D_MODEL), BF16), jax.ShapeDtypeStruct((S, wide), BF16),
                   jax.ShapeDtypeStruct((S, wide), BF16), jax.ShapeDtypeStruct((S, wide), BF16),
                   jax.ShapeDtypeStruct((S, 2 * GMLP_WIDTH), F32), jax.ShapeDtypeStruct((S, LANES), F32)),
        grid=(S // tm,),
        in_specs=[pl.BlockSpec((tm, D_MODEL), lambda i: (i, 0)), pl.BlockSpec((1, D_MODEL), lambda i: (0, 0)),
                  pl.BlockSpec((IN_PAD, D_MODEL), lambda i: (0, 0)), pl.BlockSpec((1, LANES), lambda i: (0, 0)),
                  pl.BlockSpec((tm, tm), lambda i: (0, 0))],
        out_specs=(pl.BlockSpec((tm, D_MODEL), lambda i: (i, 0)), pl.BlockSpec((tm, wide), lambda i: (i, 0)),
                   pl.BlockSpec((tm, wide), lambda i: (i, 0)), pl.BlockSpec((tm, wide), lambda i: (i, 0)),
                   pl.BlockSpec((tm, 2 * GMLP_WIDTH), lambda i: (i, 0)), pl.BlockSpec((tm, LANES), lambda i: (i, 0))),
        scratch_shapes=[pltpu.VMEM((8, LANES), F32)],
        name="inproj_fwd", compiler_params=_cp(("arbitrary",), VMEM_LIMIT),
    )(x, g_mix, w_pad, bf_pad, tri)


def _group_ones():
    r = lax.broadcasted_iota(jnp.int32, (GMLP_WIDTH, GMLP_WIDTH), 0) // GROUP_DIM
    c = lax.broadcasted_iota(jnp.int32, (GMLP_WIDTH, GMLP_WIDTH), 1) // GROUP_DIM
    return (r == c).astype(BF16)


def _gmlp_mixed(vn_bf, w_ref, bias, n_chunks):
    lane = lax.broadcasted_iota(jnp.int32, (CHUNK, LANES), 1)
    row = lax.broadcasted_iota(jnp.int32, (CHUNK, CHUNK), 0)
    col = lax.broadcasted_iota(jnp.int32, (CHUNK, CHUNK), 1)
    ws = [jnp.where(row >= col, w_ref[g], 0.0).astype(BF16) for g in range(N_GROUPS)]
    rows = []
    for ci in range(n_chunks):
        cols = []
        for pp in range(N_GROUPS // 2):
            v = vn_bf[ci * CHUNK:(ci + 1) * CHUNK, pp * LANES:(pp + 1) * LANES]
            v_lo = jnp.where(lane < GROUP_DIM, v, jnp.zeros_like(v))
            v_hi = jnp.where(lane >= GROUP_DIM, v, jnp.zeros_like(v))
            m = (jnp.dot(ws[2 * pp], v_lo, preferred_element_type=F32)
                 + jnp.dot(ws[2 * pp + 1], v_hi, preferred_element_type=F32))
            cols.append(m + bias[:, pp * LANES:(pp + 1) * LANES])
        rows.append(jnp.concatenate(cols, axis=1))
    return jnp.concatenate(rows, axis=0)


def _gmlp_fwd(ug, gain, w_s, bias_full):
    S = ug.shape[0]
    tm = _pick(S, (512, 256, 128))
    ones = _group_ones()

    def body(ug_ref, gain_ref, w_ref, bias_ref, ones_ref, sg_ref):
        u = _gelu(ug_ref[:, :GMLP_WIDTH])
        vr = _gelu(ug_ref[:, GMLP_WIDTH:])
        ms = _dot3(vr * vr, ones_ref[...]) * (1.0 / GROUP_DIM)
        vn = ((vr * lax.rsqrt(ms + EPS)) * gain_ref[...]).astype(BF16)
        mixed = _gmlp_mixed(vn, w_ref, bias_ref[...], tm // CHUNK)
        sg_ref[...] = (u * mixed).astype(BF16)

    return pl.pallas_call(
        body, out_shape=jax.ShapeDtypeStruct((S, GMLP_WIDTH), BF16), grid=(S // tm,),
        in_specs=[pl.BlockSpec((tm, 2 * GMLP_WIDTH), lambda i: (i, 0)), pl.BlockSpec((1, GMLP_WIDTH), lambda i: (0, 0)),
                  pl.BlockSpec((N_GROUPS, CHUNK, CHUNK), lambda i: (0, 0, 0)),
                  pl.BlockSpec((CHUNK, GMLP_WIDTH), lambda i: (0, 0)),
                  pl.BlockSpec((GMLP_WIDTH, GMLP_WIDTH), lambda i: (0, 0))],
        out_specs=pl.BlockSpec((tm, GMLP_WIDTH), lambda i: (i, 0)),
        name="gmlp_fwd", compiler_params=_cp(("parallel",), VMEM_LIMIT),
    )(ug, gain, w_s, bias_full, ones)


_NT = (((1,), (1,)), ((), ()))
_TN = (((0,), (0,)), ((), ()))


def _attn_fwd(qa, ka, va):
    S = qa.shape[0]
    tq = _pick(S, (ATT_TQ, 256))
    tk = min(ATT_TK, tq)
    nq = S // tq
    per_q = tq // tk

    def body(q_ref, k_ref, v_ref, o_ref, lse_ref, ob_ref):
        qi = pl.program_id(1)
        lane = lax.broadcasted_iota(jnp.int32, (tq, LANES), 1)
        sub = tk
        rid = lax.broadcasted_iota(jnp.int32, (sub, sub), 0)
        cid = lax.broadcasted_iota(jnp.int32, (sub, sub), 1)
        qs = [q_ref[:, :LANES], q_ref[:, LANES:]]

        def update(q, ks, k_len, h, m, acc, masked):
            cols = slice(h * LANES, (h + 1) * LANES)
            s = lax.dot_general(q, k_ref[pl.ds(ks, k_len), cols], _NT, preferred_element_type=F32)
            if masked:
                s = jnp.where(rid >= cid, s, -jnp.inf)
            m_new = jnp.maximum(m, jnp.max(s, axis=-1, keepdims=True))
            p = jnp.exp(s - m_new).astype(BF16)
            acc = jnp.exp(m - m_new) * acc + jnp.dot(p, v_ref[pl.ds(ks, k_len), cols], preferred_element_type=F32)
            return m_new, acc

        def step(kb, carry):
            ks = pl.multiple_of(kb * tk, tk)
            return tuple(update(qs[h], ks, tk, h, *carry[h], False) for h in range(2))

        one = (jnp.full((tq, 1), -jnp.inf, F32), jnp.zeros((tq, LANES), F32))
        carry = lax.fori_loop(0, qi * per_q, step, (one, one))
        outs, lses = [], []
        for h in range(2):
            ms, accs = [], []
            for r in range(tq // sub):
                rows = slice(r * sub, (r + 1) * sub)
                m, acc = carry[h][0][rows], carry[h][1][rows]
                for c in range(r + 1):
                    ks = pl.multiple_of(qi * tq + c * sub, sub)
                    m, acc = update(qs[h][rows], ks, sub, h, m, acc, c == r)
                ms.append(m)
                accs.append(acc)
            m, acc = jnp.concatenate(ms, axis=0), jnp.concatenate(accs, axis=0)
            l = acc[:, HEAD_DIM:HEAD_DIM + 1]
            outs.append(acc / l)
            lses.append(m + jnp.log(l))
        o = jnp.where(lane < HEAD_DIM, outs[0], pltpu.roll(outs[1], HEAD_DIM, 1))
        o_ref[...] = o
        ob_ref[...] = o.astype(BF16)
        lse_ref[...] = jnp.where(lane < HEAD_DIM, lses[0], lses[1])

    return pl.pallas_call(
        body,
        out_shape=(jax.ShapeDtypeStruct((S, ATT_WIDTH), F32), jax.ShapeDtypeStruct((S, ATT_WIDTH), F32),
                   jax.ShapeDtypeStruct((S, ATT_WIDTH), BF16)),
        grid=(N_PAIRS, nq),
        in_specs=[pl.BlockSpec((tq, 2 * LANES), lambda p, i: (i, p)),
                  pl.BlockSpec((S, 2 * LANES), lambda p, i: (0, p)),
                  pl.BlockSpec((S, 2 * LANES), lambda p, i: (0, p))],
        out_specs=(pl.BlockSpec((tq, LANES), lambda p, i: (i, p)), pl.BlockSpec((tq, LANES), lambda p, i: (i, p)),
                   pl.BlockSpec((tq, LANES), lambda p, i: (i, p))),
        name="attn_fwd", compiler_params=_cp(("parallel", "parallel"), VMEM_LIMIT),
    )(qa, ka, va)


def _rms_fwd(h, g):
    S = h.shape[0]
    tm = _pick(S, (512, 256))

    def body(h_ref, g_ref, o_ref):
        hf = h_ref[...]
        r = lax.rsqrt(jnp.mean(hf * hf, axis=-1, keepdims=True) + EPS)
        o_ref[...] = ((hf * r) * g_ref[...]).astype(BF16)

    return pl.pallas_call(
        body, out_shape=jax.ShapeDtypeStruct(h.shape, BF16), grid=(S // tm,),
        in_specs=[pl.BlockSpec((tm, D_MODEL), lambda i: (i, 0)), pl.BlockSpec((1, D_MODEL), lambda i: (0, 0))],
        out_specs=pl.BlockSpec((tm, D_MODEL), lambda i: (i, 0)),
        name="rms_fwd", compiler_params=_cp(("parallel",)),
    )(h, g)


def _shift_rows(x, prev, n):
    rid = lax.broadcasted_iota(jnp.int32, x.shape, 0)
    y = pltpu.roll(x, n, 0)
    if n == 1:
        return jnp.where(rid == 0, prev[7:8, :], y)
    return jnp.where(rid == 0, prev[6:7, :], jnp.where(rid == 1, prev[7:8, :], y))


def _shift_rows_up(x, nxt, n):
    rows = x.shape[0]
    rid = lax.broadcasted_iota(jnp.int32, x.shape, 0)
    y = pltpu.roll(x, rows - n, 0)
    if n == 1:
        return jnp.where(rid == rows - 1, nxt[0:1, :], y)
    return jnp.where(rid == rows - 2, nxt[0:1, :], jnp.where(rid == rows - 1, nxt[1:2, :], y))


def _conv3(cur, prev, w, b):
    return (w[0:1, :] * _shift_rows(cur, prev, 2) + w[1:2, :] * _shift_rows(cur, prev, 1)
            + w[2:3, :] * cur + b)


def _conv_act_fwd(hu, cw, cb):
    _, S, F = hu.shape
    tm = _pick(S, (512, 256))
    tn = _pick(F, (256, 128))
    r8 = tm // 8

    def body(cur_ref, prev_ref, w_ref, b_ref, o_ref):
        i = pl.program_id(1)
        halves = []
        for h in range(2):
            prev = jnp.where(i > 0, prev_ref[h], 0.0)
            halves.append(_conv3(cur_ref[h], prev, w_ref[h], b_ref[h]))
        a, g = halves
        o_ref[...] = (g * _sigmoid(g) * a).astype(BF16)

    return pl.pallas_call(
        body, out_shape=jax.ShapeDtypeStruct((S, F), BF16), grid=(F // tn, S // tm),
        in_specs=[pl.BlockSpec((2, tm, tn), lambda j, i: (0, i, j)),
                  pl.BlockSpec((2, 8, tn), lambda j, i: (0, jnp.maximum(i * r8 - 1, 0), j)),
                  pl.BlockSpec((2, 8, tn), lambda j, i: (0, 0, j)),
                  pl.BlockSpec((2, 1, tn), lambda j, i: (0, 0, j))],
        out_specs=pl.BlockSpec((tm, tn), lambda j, i: (i, j)),
        name="conv_act_fwd", compiler_params=_cp(("parallel", "parallel"), VMEM_LIMIT),
    )(hu, hu, cw, cb)


def _ffn_up_conv(hn, w_up_bf, cw, cb):
    S = hn.shape[0]
    F = D_FF
    tm = _pick(S, (FFN_TM, 256))
    tn = _pick(F, (FFN_TN, 256, 128))
    nj = F // tn

    def body(hn_ref, wa_ref, wg_ref, cw_ref, cb_ref, hu_ref, hc_ref, act_ref, tail_ref):
        i = pl.program_id(1)

        @pl.when(i == 0)
        def _():
            tail_ref[...] = jnp.zeros_like(tail_ref)

        hn_v = hn_ref[...]
        halves = []
        for h, w_ref in enumerate((wa_ref, wg_ref)):
            hu = lax.dot_general(hn_v, w_ref[...], _NT, preferred_element_type=F32)
            hu_ref[h] = hu
            hc = _conv3(hu, tail_ref[h], cw_ref[h], cb_ref[h])
            hc_ref[h] = hc
            halves.append(hc)
            tail_ref[h] = hu[tm - 8:, :]
        a, g = halves
        act_ref[...] = (g * _sigmoid(g) * a).astype(BF16)

    both = pl.BlockSpec((2, tm, tn), lambda j, i: (0, i, j))
    return pl.pallas_call(
        body, out_shape=(jax.ShapeDtypeStruct((2, S, F), F32), jax.ShapeDtypeStruct((2, S, F), F32),
                         jax.ShapeDtypeStruct((S, F), BF16)),
        grid=(nj, S // tm),
        in_specs=[pl.BlockSpec((tm, D_MODEL), lambda j, i: (i, 0)),
                  pl.BlockSpec((tn, D_MODEL), lambda j, i: (j, 0)),
                  pl.BlockSpec((tn, D_MODEL), lambda j, i: (nj + j, 0)),
                  pl.BlockSpec((2, 8, tn), lambda j, i: (0, 0, j)),
                  pl.BlockSpec((2, 1, tn), lambda j, i: (0, 0, j))],
        out_specs=(both, both, pl.BlockSpec((tm, tn), lambda j, i: (i, j))),
        scratch_shapes=[pltpu.VMEM((2, 8, tn), F32)],
        name="ffn_up_conv", compiler_params=_cp(("parallel", "arbitrary"), VMEM_LIMIT),
    )(hn, w_up_bf, w_up_bf, cw, cb)


def _ffn_down_loss(act, w_down_bf, h1, g_final, target):
    S = h1.shape[0]
    tm = _pick(S, (512, 256))

    def body(a_ref, w_ref, h1_ref, g_ref, t_ref, loss_ref, dh_ref, dhb_ref, dg_ref):
        i = pl.program_id(0)

        @pl.when(i == 0)
        def _():
            loss_ref[...] = jnp.zeros_like(loss_ref)
            dg_ref[...] = jnp.zeros_like(dg_ref)

        hf = h1_ref[...] + jnp.dot(a_ref[...], w_ref[...], preferred_element_type=F32)
        g = g_ref[...]
        r = lax.rsqrt(jnp.mean(hf * hf, axis=-1, keepdims=True) + EPS)
        hhat = hf * r
        err = hhat * g - t_ref[...]
        loss_ref[...] += 0.5 * jnp.sum(jnp.mean(err * err, axis=-1, keepdims=True))
        dy = err * (1.0 / D_MODEL)
        dg_ref[0:1, :] += jnp.sum(dy * hhat, axis=0, keepdims=True)
        dhat = dy * g
        dh = r * (dhat - hhat * jnp.mean(dhat * hhat, axis=-1, keepdims=True))
        dh_ref[...] = dh
        dhb_ref[...] = dh.astype(BF16)

    row = pl.BlockSpec((tm, D_MODEL), lambda i: (i, 0))
    return pl.pallas_call(
        body,
        out_shape=(jax.ShapeDtypeStruct((8, LANES), F32), jax.ShapeDtypeStruct((S, D_MODEL), F32),
                   jax.ShapeDtypeStruct((S, D_MODEL), BF16), jax.ShapeDtypeStruct((8, D_MODEL), F32)),
        grid=(S // tm,),
        in_specs=[pl.BlockSpec((tm, D_FF), lambda i: (i, 0)), pl.BlockSpec((D_FF, D_MODEL), lambda i: (0, 0)), row,
                  pl.BlockSpec((1, D_MODEL), lambda i: (0, 0)), row],
        out_specs=(pl.BlockSpec((8, LANES), lambda i: (0, 0)), row, row, pl.BlockSpec((8, D_MODEL), lambda i: (0, 0))),
        name="ffn_down_loss", compiler_params=_cp(("arbitrary",), VMEM_LIMIT),
    )(act, w_down_bf, h1, g_final, target)


def _loss_head(h2, g_final, target):
    S = h2.shape[0]
    tm = _pick(S, (512, 256))

    def body(h_ref, g_ref, t_ref, loss_ref, dh_ref, dhb_ref, dg_ref):
        i = pl.program_id(0)

        @pl.when(i == 0)
        def _():
            loss_ref[...] = jnp.zeros_like(loss_ref)
            dg_ref[...] = jnp.zeros_like(dg_ref)

        hf = h_ref[...]
        g = g_ref[...]
        r = lax.rsqrt(jnp.mean(hf * hf, axis=-1, keepdims=True) + EPS)
        hhat = hf * r
        err = hhat * g - t_ref[...]
        loss_ref[...] += 0.5 * jnp.sum(jnp.mean(err * err, axis=-1, keepdims=True))
        dy = err * (1.0 / D_MODEL)
        dg_ref[0:1, :] += jnp.sum(dy * hhat, axis=0, keepdims=True)
        dhat = dy * g
        dh = r * (dhat - hhat * jnp.mean(dhat * hhat, axis=-1, keepdims=True))
        dh_ref[...] = dh
        dhb_ref[...] = dh.astype(BF16)

    return pl.pallas_call(
        body,
        out_shape=(jax.ShapeDtypeStruct((8, LANES), F32), jax.ShapeDtypeStruct((S, D_MODEL), F32),
                   jax.ShapeDtypeStruct((S, D_MODEL), BF16), jax.ShapeDtypeStruct((8, D_MODEL), F32)),
        grid=(S // tm,),
        in_specs=[pl.BlockSpec((tm, D_MODEL), lambda i: (i, 0)), pl.BlockSpec((1, D_MODEL), lambda i: (0, 0)),
                  pl.BlockSpec((tm, D_MODEL), lambda i: (i, 0))],
        out_specs=(pl.BlockSpec((8, LANES), lambda i: (0, 0)), pl.BlockSpec((tm, D_MODEL), lambda i: (i, 0)),
                   pl.BlockSpec((tm, D_MODEL), lambda i: (i, 0)), pl.BlockSpec((8, D_MODEL), lambda i: (0, 0))),
        name="loss_head", compiler_params=_cp(("arbitrary",), VMEM_LIMIT),
    )(h2, g_final, target)


def _rms_bwd(h, g, dy, res):
    S = h.shape[0]
    tm = _pick(S, (512, 256))

    def body(h_ref, g_ref, dy_ref, r_ref, dh_ref, dhb_ref, dg_ref):
        i = pl.program_id(0)

        @pl.when(i == 0)
        def _():
            dg_ref[...] = jnp.zeros_like(dg_ref)

        hf = h_ref[...]
        dyv = dy_ref[...]
        r = lax.rsqrt(jnp.mean(hf * hf, axis=-1, keepdims=True) + EPS)
        hhat = hf * r
        dg_ref[0:1, :] += jnp.sum(dyv * hhat, axis=0, keepdims=True)
        dhat = dyv * g_ref[...]
        dh = r_ref[...] + r * (dhat - hhat * jnp.mean(dhat * hhat, axis=-1, keepdims=True))
        dh_ref[...] = dh
        dhb_ref[...] = dh.astype(BF16)

    return pl.pallas_call(
        body,
        out_shape=(jax.ShapeDtypeStruct((S, D_MODEL), F32), jax.ShapeDtypeStruct((S, D_MODEL), BF16),
                   jax.ShapeDtypeStruct((8, D_MODEL), F32)),
        grid=(S // tm,),
        in_specs=[pl.BlockSpec((tm, D_MODEL), lambda i: (i, 0)), pl.BlockSpec((1, D_MODEL), lambda i: (0, 0)),
                  pl.BlockSpec((tm, D_MODEL), lambda i: (i, 0)), pl.BlockSpec((tm, D_MODEL), lambda i: (i, 0))],
        out_specs=(pl.BlockSpec((tm, D_MODEL), lambda i: (i, 0)), pl.BlockSpec((tm, D_MODEL), lambda i: (i, 0)),
                   pl.BlockSpec((8, D_MODEL), lambda i: (0, 0))),
        name="rms_bwd", compiler_params=_cp(("arbitrary",), VMEM_LIMIT),
    )(h, g, dy, res)


def _conv_act_bwd(hu, dh2_bf, w_down_bf, cw, cb):
    _, S, F = hu.shape
    tm = _pick(S, (256,))
    tn = _pick(F, (256, 128))
    r8 = tm // 8
    r16 = tm // 16
    n_i = S // tm
    last8 = S // 8 - 1
    last16 = S // 16 - 1

    def body(cur_ref, prev_ref, next_ref, dy_ref, dyn_ref, wd_ref, w_ref, b_ref, dhu_ref, dcw_ref):
        i = pl.program_id(1)
        wd = wd_ref[...]
        dact = lax.dot_general(dy_ref[...], wd, _NT, preferred_element_type=F32)
        dact_n = lax.dot_general(dyn_ref[...], wd, _NT, preferred_element_type=F32)[0:8, :]

        @pl.when(i == 0)
        def _():
            dcw_ref[...] = jnp.zeros_like(dcw_ref)

        rid8 = lax.broadcasted_iota(jnp.int32, (8, tn), 0)
        cur = [cur_ref[0], cur_ref[1]]
        prev = [jnp.where(i > 0, prev_ref[h], 0.0) for h in range(2)]
        nxt = [next_ref[0], next_ref[1]]
        w = [w_ref[0], w_ref[1]]

        def gate_grads(a, g, d):
            sg = _sigmoid(g)
            return d * (g * sg), d * a * (sg * (1.0 + g * (1.0 - sg)))

        taps = [(_shift_rows(cur[h], prev[h], 2), _shift_rows(cur[h], prev[h], 1), cur[h]) for h in range(2)]
        a, g = [w[h][0:1, :] * taps[h][0] + w[h][1:2, :] * taps[h][1] + w[h][2:3, :] * taps[h][2] + b_ref[h]
                for h in range(2)]
        dhc = gate_grads(a, g, dact)
        a_n = _conv3(nxt[0], cur[0][tm - 8:, :], w[0], b_ref[0])
        g_n = _conv3(nxt[1], cur[1][tm - 8:, :], w[1], b_ref[1])
        dhc_n = gate_grads(a_n, g_n, dact_n)
        for h in range(2):
            d = dhc[h]
            dn = jnp.where(i < n_i - 1, dhc_n[h], 0.0)
            dhu = (w[h][2:3, :] * d + w[h][1:2, :] * _shift_rows_up(d, dn, 1)
                   + w[h][0:1, :] * _shift_rows_up(d, dn, 2))
            dhu_ref[h] = dhu.astype(BF16)
            t0, t1, t2 = [jnp.sum(d * t, axis=0, keepdims=True) for t in taps[h]]
            t3 = jnp.sum(d, axis=0, keepdims=True)
            dcw_ref[h] += jnp.where(rid8 == 0, t0, jnp.where(rid8 == 1, t1, jnp.where(rid8 == 2, t2, jnp.where(rid8 == 3, t3, 0.0))))

    return pl.pallas_call(
        body,
        out_shape=(jax.ShapeDtypeStruct((2, S, F), BF16), jax.ShapeDtypeStruct((2, 8, F), F32)),
        grid=(F // tn, n_i),
        in_specs=[pl.BlockSpec((2, tm, tn), lambda j, i: (0, i, j)),
                  pl.BlockSpec((2, 8, tn), lambda j, i: (0, jnp.maximum(i * r8 - 1, 0), j)),
                  pl.BlockSpec((2, 8, tn), lambda j, i: (0, jnp.minimum((i + 1) * r8, last8), j)),
                  pl.BlockSpec((tm, D_MODEL), lambda j, i: (i, 0)),
                  pl.BlockSpec((16, D_MODEL), lambda j, i: (jnp.minimum((i + 1) * r16, last16), 0)),
                  pl.BlockSpec((tn, D_MODEL), lambda j, i: (j, 0)),
                  pl.BlockSpec((2, 8, tn), lambda j, i: (0, 0, j)),
                  pl.BlockSpec((2, 1, tn), lambda j, i: (0, 0, j))],
        out_specs=(pl.BlockSpec((2, tm, tn), lambda j, i: (0, i, j)), pl.BlockSpec((2, 8, tn), lambda j, i: (0, 0, j))),
        name="conv_act_bwd", compiler_params=_cp(("parallel", "arbitrary"), VMEM_LIMIT),
    )(hu, hu, hu, dh2_bf, dh2_bf, w_down_bf, cw, cb)


def _ffn_up_dx_rms(dhu, w_up_bf, h1, g_ffn, dh2):
    _, S, F = dhu.shape
    tm = _pick(S, (512, 256))

    def body(a_ref, b_ref, h_ref, g_ref, r_ref, dh_ref, dhb_ref, dg_ref, acc_ref):
        i, k = pl.program_id(0), pl.program_id(1)

        @pl.when((i == 0) & (k == 0))
        def _():
            dg_ref[...] = jnp.zeros_like(dg_ref)

        part = jnp.dot(a_ref[...], b_ref[...], preferred_element_type=F32)

        @pl.when(k == 0)
        def _():
            acc_ref[...] = part

        @pl.when(k == 1)
        def _():
            dyv = acc_ref[...] + part
            hf = h_ref[...]
            r = lax.rsqrt(jnp.mean(hf * hf, axis=-1, keepdims=True) + EPS)
            hhat = hf * r
            dg_ref[0:1, :] += jnp.sum(dyv * hhat, axis=0, keepdims=True)
            dhat = dyv * g_ref[...]
            dh = r_ref[...] + r * (dhat - hhat * jnp.mean(dhat * hhat, axis=-1, keepdims=True))
            dh_ref[...] = dh
            dhb_ref[...] = dh.astype(BF16)

    row = pl.BlockSpec((tm, D_MODEL), lambda i, k: (i, 0))
    return pl.pallas_call(
        body,
        out_shape=(jax.ShapeDtypeStruct((S, D_MODEL), F32), jax.ShapeDtypeStruct((S, D_MODEL), BF16),
                   jax.ShapeDtypeStruct((8, D_MODEL), F32)),
        grid=(S // tm, 2),
        in_specs=[pl.BlockSpec((None, tm, F), lambda i, k: (k, i, 0)), pl.BlockSpec((F, D_MODEL), lambda i, k: (k, 0)),
                  row, pl.BlockSpec((1, D_MODEL), lambda i, k: (0, 0)), row],
        out_specs=(row, row, pl.BlockSpec((8, D_MODEL), lambda i, k: (0, 0))),
        scratch_shapes=[pltpu.VMEM((tm, D_MODEL), F32)],
        name="ffn_up_dx_rms", compiler_params=_cp(("arbitrary", "arbitrary"), VMEM_LIMIT),
    )(dhu, w_up_bf, h1, g_ffn, dh2)


def _conv_gate_bwd(hc, hu, dact, cw):
    _, S, F = hu.shape
    tm = _pick(S, (CONV_TM, 128))
    tn = _pick(F, (CONV_TN, 256, 128))
    r8 = tm // 8
    n_i = S // tm
    last8 = S // 8 - 1

    def body(hc_ref, hcn_ref, hu_ref, da_ref, dan_ref, w_ref, dhu_ref, dcw_ref):
        i = pl.program_id(1)

        @pl.when(i == 0)
        def _():
            dcw_ref[...] = jnp.zeros_like(dcw_ref)

        rid8 = lax.broadcasted_iota(jnp.int32, (8, tn), 0)

        def gate_grads(a, g, d):
            sg = _sigmoid(g)
            return d * (g * sg), d * a * (sg * (1.0 + g * (1.0 - sg)))

        dhc = gate_grads(hc_ref[0], hc_ref[1], da_ref[...])
        dhc_n = gate_grads(hcn_ref[0], hcn_ref[1], dan_ref[...])
        for h in range(2):
            w = w_ref[h]
            d = dhc[h]
            dn = jnp.where(i < n_i - 1, dhc_n[h], 0.0)
            u1 = _shift_rows_up(d, dn, 1)
            u2 = _shift_rows_up(d, dn, 2)
            dhu_ref[h] = (w[2:3, :] * d + w[1:2, :] * u1 + w[0:1, :] * u2).astype(BF16)
            x = hu_ref[h]
            t0, t1, t2, t3 = [jnp.sum(t, axis=0, keepdims=True) for t in (u2 * x, u1 * x, d * x, d)]
            dcw_ref[h] += jnp.where(rid8 == 0, t0, jnp.where(rid8 == 1, t1, jnp.where(rid8 == 2, t2, jnp.where(rid8 == 3, t3, 0.0))))

    cur = pl.BlockSpec((2, tm, tn), lambda j, i: (0, i, j))
    return pl.pallas_call(
        body,
        out_shape=(jax.ShapeDtypeStruct((2, S, F), BF16), jax.ShapeDtypeStruct((2, 8, F), F32)),
        grid=(F // tn, n_i),
        in_specs=[cur, pl.BlockSpec((2, 8, tn), lambda j, i: (0, jnp.minimum((i + 1) * r8, last8), j)), cur,
                  pl.BlockSpec((tm, tn), lambda j, i: (i, j)),
                  pl.BlockSpec((8, tn), lambda j, i: (jnp.minimum((i + 1) * r8, last8), j)),
                  pl.BlockSpec((2, 8, tn), lambda j, i: (0, 0, j))],
        out_specs=(cur, pl.BlockSpec((2, 8, tn), lambda j, i: (0, 0, j))),
        name="conv_gate_bwd", compiler_params=_cp(("parallel", "arbitrary"), VMEM_LIMIT),
    )(hc, hc, hu, dact, dact, cw)


def _attn_prep(att, lse, dmix, qa):
    S = att.shape[0]
    tm = _pick(S, (256,))

    def body(o_ref, lse_ref, do_ref, q_ref, qb_ref, doa_ref):
        lane = lax.broadcasted_iota(jnp.int32, (tm, LANES), 1)
        for p in range(N_PAIRS):
            pc = slice(p * LANES, (p + 1) * LANES)
            do = do_ref[:, pc]
            prod = o_ref[:, pc] * do
            for hh in range(2):
                sel = (lane >= HEAD_DIM) if hh else (lane < HEAD_DIM)
                delta = jnp.sum(jnp.where(sel, prod, 0.0), axis=-1, keepdims=True)
                dod = pltpu.roll(do, HEAD_DIM, 1) if hh else do
                cols = slice((2 * p + hh) * LANES, (2 * p + hh + 1) * LANES)
                doa_ref[:, cols] = jnp.where(lane < HEAD_DIM, dod, _aug(lane, _split3f(-delta))).astype(BF16)
                lcol = p * LANES + hh * HEAD_DIM
                l3 = _split3f(-lse_ref[:, lcol:lcol + 1])
                augl = _aug(lane, [0.0] * 6 + l3).astype(BF16)
                qb_ref[:, cols] = jnp.where((lane >= HEAD_DIM + 6) & (lane < HEAD_DIM + 9), augl, q_ref[:, cols])

    half = pl.BlockSpec((tm, ATT_WIDTH), lambda i: (i, 0))
    wide = pl.BlockSpec((tm, N_HEADS * LANES), lambda i: (i, 0))
    return pl.pallas_call(
        body,
        out_shape=(jax.ShapeDtypeStruct(qa.shape, BF16), jax.ShapeDtypeStruct(qa.shape, BF16)),
        grid=(S // tm,), in_specs=[half, half, half, wide], out_specs=(wide, wide),
        name="attn_prep", compiler_params=_cp(("parallel",), VMEM_LIMIT),
    )(att, lse, dmix, qa)


def _attn_bwd(qb, ka, va, doa):
    S = qb.shape[0]
    tk = _pick(S, (512, 256))
    tq = tk
    nq = S // tq

    def pair(a, scale=None):
        lane = lax.broadcasted_iota(jnp.int32, (a.shape[0], LANES), 1)
        out = jnp.where(lane < HEAD_DIM, a[:, :LANES], pltpu.roll(a[:, LANES:], HEAD_DIM, 1))
        return out if scale is None else out * scale

    def lanes01(a, col, sign):
        lane = lax.broadcasted_iota(jnp.int32, (a.shape[0], LANES), 1)
        return jnp.where(lane == 0, sign * a[:, col:col + 1], jnp.where(lane == 1, sign * a[:, LANES + col:LANES + col + 1], 0.0))

    def body(q_ref, do_ref, k_ref, v_ref, dqc_ref, dkc_ref, dvc_ref, dcq_ref, dck_ref, dq_ref, dka_ref, dva_ref):
        kb = pl.program_id(1)

        @pl.when(kb == 0)
        def _():
            dq_ref[...] = jnp.zeros_like(dq_ref)

        dka_ref[...] = jnp.zeros_like(dka_ref)
        dva_ref[...] = jnp.zeros_like(dva_ref)
        rid = lax.broadcasted_iota(jnp.int32, (tk, tq), 0)
        cid = lax.broadcasted_iota(jnp.int32, (tk, tq), 1)

        def sub_tile(qs, q_len, k_off, k_len, masked):
            keys = slice(k_off, k_off + k_len)
            for h in range(2):
                cols = slice(h * LANES, (h + 1) * LANES)
                qblk = q_ref[pl.ds(qs, q_len), cols]
                doblk = do_ref[pl.ds(qs, q_len), cols]
                kh = k_ref[keys, cols]
                p = jnp.exp(lax.dot_general(kh, qblk, _NT, preferred_element_type=F32))
                if masked:
                    p = jnp.where(cid >= rid, p, 0.0)
                ds = (p * lax.dot_general(v_ref[keys, cols], doblk, _NT, preferred_element_type=F32)).astype(BF16)
                dva_ref[keys, cols] += jnp.dot(p.astype(BF16), doblk, preferred_element_type=F32)
                dka_ref[keys, cols] += jnp.dot(ds, qblk, preferred_element_type=F32)
                dq_ref[pl.ds(qs, q_len), cols] += lax.dot_general(ds, kh, _TN, preferred_element_type=F32)

        sub_tile(pl.multiple_of(kb * tq, tq), tq, 0, tk, True)

        def step(qi, carry):
            sub_tile(pl.multiple_of(qi * tq, tq), tq, 0, tk, False)
            return carry

        lax.fori_loop(kb + 1, nq, step, 0)
        dka = dka_ref[...]
        dkc_ref[...] = pair(dka).astype(BF16)
        dvc_ref[...] = pair(dva_ref[...]).astype(BF16)
        dck_ref[...] = lanes01(dka, HEAD_DIM + 3, -1.0)

        @pl.when(kb == nq - 1)
        def _():
            dqa = dq_ref[...]
            dqc_ref[...] = pair(dqa, HEAD_DIM ** -0.5).astype(BF16)
            dcq_ref[...] = lanes01(dqa, HEAD_DIM, 1.0)

    wide = 2 * LANES
    half = jax.ShapeDtypeStruct((S, ATT_WIDTH), BF16)
    slabs = jax.ShapeDtypeStruct((N_PAIRS, S, LANES), F32)
    return pl.pallas_call(
        body,
        out_shape=(half, half, half, slabs, slabs),
        grid=(N_PAIRS, nq),
        in_specs=[pl.BlockSpec((S, wide), lambda p, j: (0, p)), pl.BlockSpec((S, wide), lambda p, j: (0, p)),
                  pl.BlockSpec((tk, wide), lambda p, j: (j, p)), pl.BlockSpec((tk, wide), lambda p, j: (j, p))],
        out_specs=(pl.BlockSpec((S, LANES), lambda p, j: (0, p)), pl.BlockSpec((tk, LANES), lambda p, j: (j, p)),
                   pl.BlockSpec((tk, LANES), lambda p, j: (j, p)), pl.BlockSpec((None, S, LANES), lambda p, j: (p, 0, 0)),
                   pl.BlockSpec((None, tk, LANES), lambda p, j: (p, j, 0))),
        scratch_shapes=[pltpu.VMEM((S, wide), F32), pltpu.VMEM((tk, wide), F32), pltpu.VMEM((tk, wide), F32)],
        name="attn_bwd", compiler_params=_cp(("parallel", "arbitrary"), VMEM_LIMIT),
    )(qb, doa, ka, va)


def _attn_delta_old(o, do):
    S = o.shape[0]
    tm = _pick(S, (512, 256))
    ones = _group_ones()

    def body(o_ref, do_ref, ones_ref, d_ref):
        d_ref[...] = _dot3(o_ref[...] * do_ref[...], ones_ref[...])

    return pl.pallas_call(
        body, out_shape=jax.ShapeDtypeStruct((S, ATT_WIDTH), F32), grid=(S // tm,),
        in_specs=[pl.BlockSpec((tm, ATT_WIDTH), lambda i: (i, 0)), pl.BlockSpec((tm, ATT_WIDTH), lambda i: (i, 0)),
                  pl.BlockSpec((ATT_WIDTH, ATT_WIDTH), lambda i: (0, 0))],
        out_specs=pl.BlockSpec((tm, ATT_WIDTH), lambda i: (i, 0)),
        name="attn_delta", compiler_params=_cp(("parallel",)),
    )(o, do, ones)


def _attn_bwd_old(qkv, do_bf, c_cols, c_rows, lse_rows, dl_rows):
    S = qkv.shape[0]
    tk = _pick(S, (256,))
    tq = tk
    nq = S // tq
    nt = (((1,), (1,)), ((), ()))
    tn_dims = (((0,), (0,)), ((), ()))

    def body(q_ref, do_ref, k_ref, v_ref, cc_ref, cr_ref, lse_ref, dl_ref, dq_ref, dk_ref, dv_ref, dcs_ref, dcq_ref):
        kb = pl.program_id(1)

        @pl.when(kb == 0)
        def _():
            dq_ref[...] = jnp.zeros_like(dq_ref)
            dcq_ref[...] = jnp.zeros_like(dcq_ref)

        lane = lax.broadcasted_iota(jnp.int32, (tk, LANES), 1)
        rid = lax.broadcasted_iota(jnp.int32, (tk, tq), 0)
        cid = lax.broadcasted_iota(jnp.int32, (tk, tq), 1)
        k = k_ref[...]
        v = v_ref[...]
        dks, dvs, dcs = [], [], []
        for hh in range(2):
            sel = (lane >= GROUP_DIM) if hh else (lane < GROUP_DIM)
            km = jnp.where(sel, k, jnp.zeros_like(k))
            vm = jnp.where(sel, v, jnp.zeros_like(v))
            cs = cc_ref[:, hh * HEAD_DIM:hh * HEAD_DIM + 1]

            def step(qb, carry, km=km, vm=vm, cs=cs, hh=hh):
                dk_acc, dv_acc, dc_acc = carry
                qs = pl.multiple_of(qb * tq, tq)
                qblk = q_ref[pl.ds(qs, tq), :]
                doblk = do_ref[pl.ds(qs, tq), :]
                s = lax.dot_general(km, qblk, nt, preferred_element_type=F32)
                s = s + (cr_ref[hh:hh + 1, pl.ds(qs, tq)] - cs)
                p = jnp.exp(s - lse_ref[hh:hh + 1, pl.ds(qs, tq)])
                p = jnp.where((qb > kb) | (cid >= rid), p, 0.0)
                dp = lax.dot_general(vm, doblk, nt, preferred_element_type=F32)
                ds = p * (dp - dl_ref[hh:hh + 1, pl.ds(qs, tq)])
                ds_bf = ds.astype(BF16)
                dv_acc = dv_acc + jnp.dot(p.astype(BF16), doblk, preferred_element_type=F32)
                dk_acc = dk_acc + jnp.dot(ds_bf, qblk, preferred_element_type=F32)
                dc_acc = dc_acc + jnp.sum(ds, axis=-1, keepdims=True)
                dq_ref[pl.ds(qs, tq), :] += lax.dot_general(ds_bf, km, tn_dims, preferred_element_type=F32)
                dcq_ref[hh:hh + 1, pl.ds(qs, tq)] += jnp.sum(ds, axis=0, keepdims=True)
                return dk_acc, dv_acc, dc_acc

            init = (jnp.zeros((tk, LANES), F32), jnp.zeros((tk, LANES), F32), jnp.zeros((tk, 1), F32))
            dk_acc, dv_acc, dc_acc = lax.fori_loop(kb, nq, step, init)
            dks.append(dk_acc)
            dvs.append(dv_acc)
            dcs.append(dc_acc)
        dk_ref[...] = jnp.where(lane < GROUP_DIM, dks[0], dks[1])
        dv_ref[...] = jnp.where(lane < GROUP_DIM, dvs[0], dvs[1])
        dcs_ref[...] = jnp.where(lane == 0, -dcs[0], jnp.where(lane == 1, -dcs[1], 0.0))

    return pl.pallas_call(
        body,
        out_shape=(jax.ShapeDtypeStruct((S, ATT_WIDTH), F32), jax.ShapeDtypeStruct((S, ATT_WIDTH), F32),
                   jax.ShapeDtypeStruct((S, ATT_WIDTH), F32), jax.ShapeDtypeStruct((N_PAIRS, S, LANES), F32),
                   jax.ShapeDtypeStruct((N_PAIRS, 8, S), F32)),
        grid=(N_PAIRS, nq),
        in_specs=[pl.BlockSpec((S, LANES), lambda p, j: (0, p)),
                  pl.BlockSpec((S, LANES), lambda p, j: (0, p)),
                  pl.BlockSpec((tk, LANES), lambda p, j: (j, N_PAIRS + p)),
                  pl.BlockSpec((tk, LANES), lambda p, j: (j, 2 * N_PAIRS + p)),
                  pl.BlockSpec((None, tk, LANES), lambda p, j: (p, j, 0)),
                  pl.BlockSpec((None, 8, S), lambda p, j: (p, 0, 0)),
                  pl.BlockSpec((None, 8, S), lambda p, j: (p, 0, 0)),
                  pl.BlockSpec((None, 8, S), lambda p, j: (p, 0, 0))],
        out_specs=(pl.BlockSpec((S, LANES), lambda p, j: (0, p)),
                   pl.BlockSpec((tk, LANES), lambda p, j: (j, p)),
                   pl.BlockSpec((tk, LANES), lambda p, j: (j, p)),
                   pl.BlockSpec((None, tk, LANES), lambda p, j: (p, j, 0)),
                   pl.BlockSpec((None, 8, S), lambda p, j: (p, 0, 0))),
        name="attn_bwd", compiler_params=_cp(("parallel", "arbitrary"), VMEM_LIMIT),
    )(qkv, do_bf, qkv, qkv, c_cols, c_rows, lse_rows, dl_rows)


def _gmlp_bwd(ug, dsg, gain, w_s, wt_s, bias_full):
    S = ug.shape[0]
    tm = _pick(S, (512, 256, 128))
    n_chunks = tm // CHUNK
    n_i = S // tm
    ones = _group_ones()
    nt = (((1,), (1,)), ((), ()))

    def body(ug_ref, dsg_ref, gain_ref, w_ref, wt_ref, bias_ref, ones_ref, dug_ref, dw_ref, dgain_ref, dbias_ref,
             dbacc_ref):
        i = pl.program_id(0)

        @pl.when(i == 0)
        def _():
            dw_ref[...] = jnp.zeros_like(dw_ref)
            dgain_ref[...] = jnp.zeros_like(dgain_ref)
            dbacc_ref[...] = jnp.zeros_like(dbacc_ref)

        ones_m = ones_ref[...]
        pu = ug_ref[:, :GMLP_WIDTH]
        pg = ug_ref[:, GMLP_WIDTH:]
        u = _gelu(pu)
        vr = _gelu(pg)
        ms = _dot3(vr * vr, ones_m) * (1.0 / GROUP_DIM)
        rinv = lax.rsqrt(ms + EPS)
        vhat = vr * rinv
        gain_v = gain_ref[...]
        vn = (vhat * gain_v).astype(BF16)
        mixed = _gmlp_mixed(vn, w_ref, bias_ref[...], n_chunks)
        dsg_v = dsg_ref[...]
        du = dsg_v * mixed
        dmixed = dsg_v * u
        dm_bf = dmixed.astype(BF16)
        lane = lax.broadcasted_iota(jnp.int32, (CHUNK, LANES), 1)
        row = lax.broadcasted_iota(jnp.int32, (CHUNK, CHUNK), 0)
        col = lax.broadcasted_iota(jnp.int32, (CHUNK, CHUNK), 1)
        wts = [jnp.where(col >= row, wt_ref[g], 0.0).astype(BF16) for g in range(N_GROUPS)]
        dvn_rows = []
        dbsum = jnp.zeros((CHUNK, GMLP_WIDTH), F32)
        for ci in range(n_chunks):
            rs = slice(ci * CHUNK, (ci + 1) * CHUNK)
            dbsum = dbsum + dmixed[rs, :]
            cols = []
            for pp in range(N_GROUPS // 2):
                cs = slice(pp * LANES, (pp + 1) * LANES)
                dm = dm_bf[rs, cs]
                dm_lo = jnp.where(lane < GROUP_DIM, dm, jnp.zeros_like(dm))
                dm_hi = jnp.where(lane >= GROUP_DIM, dm, jnp.zeros_like(dm))
                vb = vn[rs, cs]
                dw_ref[2 * pp] += lax.dot_general(dm_lo, vb, nt, preferred_element_type=F32)
                dw_ref[2 * pp + 1] += lax.dot_general(dm_hi, vb, nt, preferred_element_type=F32)
                cols.append(jnp.dot(wts[2 * pp], dm_lo, preferred_element_type=F32)
                            + jnp.dot(wts[2 * pp + 1], dm_hi, preferred_element_type=F32))
            dvn_rows.append(jnp.concatenate(cols, axis=1))
        dvn = jnp.concatenate(dvn_rows, axis=0)
        dbacc_ref[...] += dbsum
        dgain_ref[0:1, :] += jnp.sum(dvn * vhat, axis=0, keepdims=True)
        dvhat = dvn * gain_v
        gm = _dot3(dvhat * vhat, ones_m) * (1.0 / GROUP_DIM)
        dvr = rinv * (dvhat - vhat * gm)
        dug_ref[:, :GMLP_WIDTH] = (du * _gelu_grad(pu)).astype(BF16)
        dug_ref[:, GMLP_WIDTH:] = (dvr * _gelu_grad(pg)).astype(BF16)

        @pl.when(i == n_i - 1)
        def _():
            for g in range(N_GROUPS):
                dw_ref[g] = jnp.where(row >= col, dw_ref[g], 0.0)
            dbias_ref[...] = _dot3(dbacc_ref[...], ones_m)

    return pl.pallas_call(
        body,
        out_shape=(jax.ShapeDtypeStruct((S, 2 * GMLP_WIDTH), BF16), jax.ShapeDtypeStruct((N_GROUPS, CHUNK, CHUNK), F32),
                   jax.ShapeDtypeStruct((8, GMLP_WIDTH), F32), jax.ShapeDtypeStruct((CHUNK, GMLP_WIDTH), F32)),
        grid=(n_i,),
        in_specs=[pl.BlockSpec((tm, 2 * GMLP_WIDTH), lambda i: (i, 0)), pl.BlockSpec((tm, GMLP_WIDTH), lambda i: (i, 1)),
                  pl.BlockSpec((1, GMLP_WIDTH), lambda i: (0, 0)),
                  pl.BlockSpec((N_GROUPS, CHUNK, CHUNK), lambda i: (0, 0, 0)),
                  pl.BlockSpec((N_GROUPS, CHUNK, CHUNK), lambda i: (0, 0, 0)),
                  pl.BlockSpec((CHUNK, GMLP_WIDTH), lambda i: (0, 0)),
                  pl.BlockSpec((GMLP_WIDTH, GMLP_WIDTH), lambda i: (0, 0))],
        out_specs=(pl.BlockSpec((tm, 2 * GMLP_WIDTH), lambda i: (i, 0)),
                   pl.BlockSpec((N_GROUPS, CHUNK, CHUNK), lambda i: (0, 0, 0)),
                   pl.BlockSpec((8, GMLP_WIDTH), lambda i: (0, 0)),
                   pl.BlockSpec((CHUNK, GMLP_WIDTH), lambda i: (0, 0))),
        scratch_shapes=[pltpu.VMEM((CHUNK, GMLP_WIDTH), F32)],
        name="gmlp_bwd", compiler_params=_cp(("arbitrary",), VMEM_LIMIT),
    )(ug, dsg, gain, w_s, wt_s, bias_full, ones)


def _gate_bwd(dcq, dck, zf):
    S = zf.shape[0]
    tm = _pick(S, (256,))
    n_i = S // tm
    triu = (lax.broadcasted_iota(jnp.int32, (tm, tm), 0) <= lax.broadcasted_iota(jnp.int32, (tm, tm), 1)).astype(BF16)

    def body(dcq_ref, dck_ref, zf_ref, tri_ref, dzf_ref, dbf_ref, carry_ref):
        i = pl.program_id(0)

        @pl.when(i == 0)
        def _():
            carry_ref[...] = jnp.zeros_like(carry_ref)
            dbf_ref[...] = jnp.zeros_like(dbf_ref)

        lane = lax.broadcasted_iota(jnp.int32, (tm, LANES), 1)
        dc = jnp.zeros((tm, LANES), F32)
        for p in range(N_PAIRS):
            slab = dcq_ref[p] + dck_ref[p]
            for hh in range(2):
                dc = dc + jnp.where(lane == 2 * p + hh, slab[:, hh:hh + 1], 0.0)
        dlf = _dot3l(tri_ref[...], dc) + carry_ref[0:1, :]
        carry_ref[0:1, :] = dlf[0:1, :]
        dz = jnp.where(lane < N_HEADS, dlf * _sigmoid(-zf_ref[...]), 0.0)
        dzf_ref[...] = dz.astype(BF16)
        dbf_ref[0:1, :] += jnp.sum(dz, axis=0, keepdims=True)

    return pl.pallas_call(
        body,
        out_shape=(jax.ShapeDtypeStruct((S, LANES), BF16), jax.ShapeDtypeStruct((8, LANES), F32)),
        grid=(n_i,),
        in_specs=[pl.BlockSpec((N_PAIRS, tm, LANES), lambda i: (0, n_i - 1 - i, 0)),
                  pl.BlockSpec((N_PAIRS, tm, LANES), lambda i: (0, n_i - 1 - i, 0)),
                  pl.BlockSpec((tm, LANES), lambda i: (n_i - 1 - i, 0)),
                  pl.BlockSpec((tm, tm), lambda i: (0, 0))],
        out_specs=(pl.BlockSpec((tm, LANES), lambda i: (n_i - 1 - i, 0)), pl.BlockSpec((8, LANES), lambda i: (0, 0))),
        scratch_shapes=[pltpu.VMEM((8, LANES), F32)],
        name="gate_bwd", compiler_params=_cp(("arbitrary",), VMEM_LIMIT),
    )(dcq, dck, zf, triu)


def _out_proj_fwd(att_bf, sg, w_out_bf, x, g_ffn):
    S = x.shape[0]
    tm = _pick(S, (512, 256))

    def body(a_ref, s_ref, w_ref, x_ref, g_ref, h_ref, hn_ref):
        h = (x_ref[...] + jnp.dot(a_ref[...], w_ref[:ATT_WIDTH, :], preferred_element_type=F32)
             + jnp.dot(s_ref[...], w_ref[ATT_WIDTH:, :], preferred_element_type=F32))
        h_ref[...] = h
        r = lax.rsqrt(jnp.mean(h * h, axis=-1, keepdims=True) + EPS)
        hn_ref[...] = ((h * r) * g_ref[...]).astype(BF16)

    row = pl.BlockSpec((tm, D_MODEL), lambda i: (i, 0))
    half = pl.BlockSpec((tm, ATT_WIDTH), lambda i: (i, 0))
    return pl.pallas_call(
        body, out_shape=(jax.ShapeDtypeStruct((S, D_MODEL), F32), jax.ShapeDtypeStruct((S, D_MODEL), BF16)),
        grid=(S // tm,),
        in_specs=[half, half, pl.BlockSpec((D_MODEL, D_MODEL), lambda i: (0, 0)), row,
                  pl.BlockSpec((1, D_MODEL), lambda i: (0, 0))],
        out_specs=(row, row), name="out_proj", compiler_params=_cp(("parallel",), VMEM_LIMIT),
    )(att_bf, sg, w_out_bf, x, g_ffn)


_IN_PIECES = ((0, ATT_WIDTH), (ATT_WIDTH, ATT_WIDTH), (2 * ATT_WIDTH, ATT_WIDTH), (QKV, 2 * GMLP_WIDTH), (UG_END, LANES))


def _inproj_bwd_dx(pieces, w_pad, x, g_mix, dh1):
    S = x.shape[0]
    tm = _pick(S, (512, 256))

    def body(*refs):
        p_refs, (w_ref, x_ref, g_ref, r_ref, dx_ref, dg_ref) = refs[:5], refs[5:]
        i = pl.program_id(0)

        @pl.when(i == 0)
        def _():
            dg_ref[...] = jnp.zeros_like(dg_ref)

        dxn = None
        for p_ref, (c0, width) in zip(p_refs, _IN_PIECES):
            part = jnp.dot(p_ref[...], w_ref[c0:c0 + width, :], preferred_element_type=F32)
            dxn = part if dxn is None else dxn + part
        xf = x_ref[...]
        r = lax.rsqrt(jnp.mean(xf * xf, axis=-1, keepdims=True) + EPS)
        xhat = xf * r
        dg_ref[0:1, :] += jnp.sum(dxn * xhat, axis=0, keepdims=True)
        dhat = dxn * g_ref[...]
        dx_ref[...] = r_ref[...] + r * (dhat - xhat * jnp.mean(dhat * xhat, axis=-1, keepdims=True))

    row = pl.BlockSpec((tm, D_MODEL), lambda i: (i, 0))
    return pl.pallas_call(
        body, out_shape=(jax.ShapeDtypeStruct((S, D_MODEL), F32), jax.ShapeDtypeStruct((8, D_MODEL), F32)),
        grid=(S // tm,),
        in_specs=[pl.BlockSpec((tm, width), lambda i: (i, 0)) for _, width in _IN_PIECES]
        + [pl.BlockSpec((IN_PAD, D_MODEL), lambda i: (0, 0)), row, pl.BlockSpec((1, D_MODEL), lambda i: (0, 0)), row],
        out_specs=(row, pl.BlockSpec((8, D_MODEL), lambda i: (0, 0))),
        name="in_proj_dx", compiler_params=_cp(("arbitrary",), VMEM_LIMIT),
    )(*pieces, w_pad, x, g_mix, dh1)


def _inproj_bwd_dw(xn, pieces):
    S = xn.shape[0]
    tk = _pick(S, (512, 256))

    def body(*refs):
        x_ref, p_refs, o_ref = refs[0], refs[1:6], refs[6]
        k = pl.program_id(0)

        @pl.when(k == 0)
        def _():
            o_ref[...] = jnp.zeros_like(o_ref)

        xb = x_ref[...]
        for p_ref, (c0, width) in zip(p_refs, _IN_PIECES):
            o_ref[:, c0:c0 + width] += lax.dot_general(xb, p_ref[...], _TN, preferred_element_type=F32)

    return pl.pallas_call(
        body, out_shape=jax.ShapeDtypeStruct((D_MODEL, IN_PAD), F32), grid=(S // tk,),
        in_specs=[pl.BlockSpec((tk, D_MODEL), lambda k: (k, 0))]
        + [pl.BlockSpec((tk, width), lambda k: (k, 0)) for _, width in _IN_PIECES],
        out_specs=pl.BlockSpec((D_MODEL, IN_PAD), lambda k: (0, 0)),
        name="in_proj_dw", compiler_params=_cp(("arbitrary",), VMEM_LIMIT),
    )(xn, *pieces)


def _adamw(w, m, v, parts, name):
    R, C = w.shape[-2:]
    tr = R
    for cand in (256, 128, 64, 32, 16, 8):
        if R % cand == 0 and R > cand:
            tr = cand
            break
    c1 = 1.0 / (1.0 - ADAM_B1 ** ADAM_STEP)
    c2 = 1.0 / (1.0 - ADAM_B2 ** ADAM_STEP)

    def body(w_ref, m_ref, v_ref, p_ref, g_ref, d_ref, nm_ref, nv_ref):
        g = p_ref[0].astype(F32)
        for j in range(1, N_DEV):
            g = g + p_ref[j].astype(F32)
        g_ref[...] = g
        nm = ADAM_B1 * m_ref[...] + (1.0 - ADAM_B1) * g
        nv = ADAM_B2 * v_ref[...] + (1.0 - ADAM_B2) * (g * g)
        nm_ref[...] = nm
        nv_ref[...] = nv
        d_ref[...] = -ADAM_LR * ((nm * c1) / (jnp.sqrt(nv * c2) + ADAM_EPS) + ADAM_WD * w_ref[...])

    if w.ndim == 3:
        spec = pl.BlockSpec((None, tr, C), lambda i: (0, i, 0))
    else:
        spec = pl.BlockSpec((tr, C), lambda i: (i, 0))
    shp = jax.ShapeDtypeStruct(w.shape, F32)
    return pl.pallas_call(
        body, out_shape=(shp, shp, shp, shp), grid=(R // tr,),
        in_specs=[spec, spec, spec, pl.BlockSpec((N_DEV, tr, C), lambda i: (0, i, 0))],
        out_specs=(spec, spec, spec, spec),
        name=name, compiler_params=_cp(("parallel",), VMEM_LIMIT),
    )(w, m, v, parts)


def _place():
    x, y, c = lax.axis_index("x"), lax.axis_index("y"), lax.axis_index("c")
    return x, y, c


def _all_gather(blocks, name):
    n = len(blocks)

    def body(*refs):
        ins, outs = refs[:n], refs[n:2 * n]
        send_sems, recv_sems, local_sems = refs[2 * n:]
        x, y, c = _place()
        me, sibling = (x, y, c), (x, y, 1 - c)
        chips = [(1 - x, y), (x, 1 - y), (1 - x, 1 - y)]
        sends = []
        for a in range(n):
            out = outs[a]

            def slot(px, py, pc, out=out):
                return out.at[4 * px + 2 * py + pc]

            def copy(k, block, to, src=None, a=a, slot=slot):
                return pltpu.make_async_remote_copy(
                    src_ref=slot(*block) if src is None else src, dst_ref=slot(*block),
                    send_sem=send_sems.at[a, k], recv_sem=recv_sems.at[a, k], device_id=to, device_id_type=MESH)

            mine = pltpu.make_async_copy(ins[a], slot(*me), local_sems.at[a])
            mine.start()
            first = [copy(0, me, sibling, src=ins[a])]
            first += [copy(1 + j, me, (*chip, c), src=ins[a]) for j, chip in enumerate(chips)]
            for cp in first:
                cp.start()
            sends.append((mine, first, copy))
        for a in range(n):
            mine, first, copy = sends[a]
            passed = [copy(4 + j, (*chip, c), sibling) for j, chip in enumerate(chips)]
            for j, chip in enumerate(chips):
                copy(1 + j, (*chip, c), me).wait_recv()
                passed[j].start()
            copy(0, sibling, me).wait_recv()
            for j, chip in enumerate(chips):
                copy(4 + j, (*chip, 1 - c), me).wait_recv()
            for cp in first + passed:
                cp.wait_send()
            mine.wait()

    any_spec = pl.BlockSpec(memory_space=pl.ANY)
    return pl.pallas_call(
        body, out_shape=tuple(jax.ShapeDtypeStruct((N_DEV,) + b.shape, b.dtype) for b in blocks),
        in_specs=[any_spec] * n, out_specs=tuple([any_spec] * n),
        scratch_shapes=[pltpu.SemaphoreType.DMA((n, 7)), pltpu.SemaphoreType.DMA((n, 7)), pltpu.SemaphoreType.DMA((n,))],
        name=name,
    )(*blocks)


def _exchange_shards(parts, name):
    n = len(parts)

    def body(*refs):
        ins, outs = refs[:n], refs[n:2 * n]
        send_sems, recv_sems, local_sems = refs[2 * n:]
        x, y, c = _place()
        me = 4 * x + 2 * y + c
        started = []
        for a in range(n):
            mine = pltpu.make_async_copy(ins[a].at[me], outs[a].at[me], local_sems.at[a])
            mine.start()
            started.append(mine)
            for k in range(1, N_DEV):
                px, py, pc = x ^ ((k >> 2) & 1), y ^ ((k >> 1) & 1), c ^ (k & 1)
                cp = pltpu.make_async_remote_copy(
                    src_ref=ins[a].at[4 * px + 2 * py + pc], dst_ref=outs[a].at[me],
                    send_sem=send_sems.at[a, k - 1], recv_sem=recv_sems.at[a, k - 1],
                    device_id=(px, py, pc), device_id_type=MESH)
                cp.start()
                started.append(cp)
        for cp in started:
            cp.wait()

    any_spec = pl.BlockSpec(memory_space=pl.ANY)
    return pl.pallas_call(
        body, out_shape=tuple(jax.ShapeDtypeStruct(p.shape, p.dtype) for p in parts),
        in_specs=[any_spec] * n, out_specs=tuple([any_spec] * n),
        scratch_shapes=[pltpu.SemaphoreType.DMA((n, 7)), pltpu.SemaphoreType.DMA((n, 7)), pltpu.SemaphoreType.DMA((n,))],
        name=name,
    )(*parts)


_HBM = pl.BlockSpec(memory_space=pltpu.HBM)
_SEM = pl.BlockSpec(memory_space=pltpu.SEMAPHORE)
_EFFECT = pltpu.SideEffectType.DATAFLOW_SIDE_EFFECTING


def _peers(x, y, c):
    out = []
    for k in range(1, N_DEV):
        px, py, pc = x ^ ((k >> 2) & 1), y ^ ((k >> 1) & 1), c ^ (k & 1)
        out.append((k, (px, py, pc), 4 * px + 2 * py + pc))
    return out


def _xchg_copies(src_refs, land_refs, send_sems, recv_sems, scatter):
    x, y, c = _place()
    me = 4 * x + 2 * y + c
    copies = []
    for a, (src, land) in enumerate(zip(src_refs, land_refs)):
        for k, place, idx in _peers(x, y, c):
            j = a * (N_DEV - 1) + k - 1
            copies.append(pltpu.make_async_remote_copy(
                src_ref=src.at[idx] if scatter[a] else src, dst_ref=land.at[me],
                send_sem=send_sems[j], recv_sem=recv_sems[j], device_id=place, device_id_type=MESH))
    return copies


def _xchg_start(srcs, scatter, name):
    n = len(srcs)
    lands = [lax.empty((N_DEV,) + (s.shape[1:] if sc else s.shape), s.dtype) for s, sc in zip(srcs, scatter)]

    ns = n * (N_DEV - 1)

    def body(*refs):
        sems = refs[2 * n:2 * n + 2 * ns]
        for cp in _xchg_copies(refs[:n], refs[n:2 * n], sems[:ns], sems[ns:], scatter):
            cp.start()
        token = refs[-1]
        token[...] = jnp.zeros_like(token)

    both = list(srcs) + lands
    res = pl.pallas_call(
        body, name=name,
        out_shape=(*[pltpu.SemaphoreType.DMA(())] * (2 * ns),
                   *[pltpu.HBM(a.shape, a.dtype) for a in both], jax.ShapeDtypeStruct((8, LANES), F32)),
        in_specs=[_HBM] * (2 * n),
        out_specs=(*([_SEM] * (2 * ns)), *([_HBM] * (2 * n)), pl.BlockSpec(memory_space=pltpu.VMEM)),
        input_output_aliases={i: 2 * ns + i for i in range(2 * n)},
        compiler_params=pltpu.CompilerParams(has_side_effects=_EFFECT),
    )(*[pltpu.with_memory_space_constraint(a, pltpu.HBM) for a in both])
    return (tuple(res[:2 * ns]), tuple(res[2 * ns:2 * ns + 2 * n])), res[-1]


def _xchg_wait(handle, scatter, after, name):
    sems, thru = handle
    n = len(thru) // 2
    ns = len(sems) // 2

    def body(*refs):
        got = refs[2 * n:2 * n + 2 * ns]
        for cp in _xchg_copies(refs[:n], refs[n:2 * n], got[:ns], got[ns:], scatter):
            cp.wait_send()
            cp.wait_recv()

    outs = pl.pallas_call(
        body, name=name, out_shape=tuple(pltpu.HBM(a.shape, a.dtype) for a in thru),
        in_specs=[_HBM] * (2 * n) + [_SEM] * (2 * ns) + [pl.BlockSpec(memory_space=pl.ANY)],
        out_specs=tuple([_HBM] * (2 * n)), input_output_aliases={i: i for i in range(2 * n)},
        compiler_params=pltpu.CompilerParams(has_side_effects=_EFFECT),
    )(*thru, *sems, after)
    return outs[:n], outs[n:]


def _tie(a, token):
    return a if token is None else a + token[0, 0].astype(a.dtype)


def _rows128(a):
    flat = a.reshape(-1)
    rows = -(-flat.shape[0] // LANES)
    rows = -(-rows // 8) * 8
    return jnp.pad(flat, (0, rows * LANES - flat.shape[0])).reshape(rows, LANES)


def _local_step(x, target, norm_mix_g, w_in_t, b_forget, gmlp_norm_g, w_spatial, b_spatial, norm_ffn_g, conv_b,
                norm_final_g, rest_fn, send_fn, small_fn, token=None):
    f = D_FF
    g_mix = norm_mix_g.reshape(1, D_MODEL)
    w_pad = jnp.pad(w_in_t, ((0, IN_PAD - IN_COLS), (0, 0)))
    bf_pad = jnp.pad(b_forget.reshape(1, N_HEADS), ((0, 0), (0, LANES - N_HEADS)))
    xn, qa, ka, va, ug, zf = _inproj_fwd(x, _tie(g_mix, token), w_pad, bf_pad)
    bias_full = jnp.repeat(b_spatial.reshape(N_GROUPS, CHUNK).T, GROUP_DIM, axis=1)
    w_s = w_spatial.reshape(N_GROUPS, CHUNK, CHUNK)
    gain = gmlp_norm_g.reshape(1, GMLP_WIDTH)
    sg = _gmlp_fwd(ug, gain, w_s, bias_full)
    att, lse, att_bf = _attn_fwd(qa, ka, va)
    w_out_bf, w_up_bf, conv_w, w_down_bf = rest_fn(att_bf)
    g_ffn = norm_ffn_g.reshape(1, D_MODEL)
    h1, hn = _out_proj_fwd(att_bf, sg, w_out_bf, x, g_ffn)
    cw = jnp.pad(conv_w.reshape(3, 2, f).transpose(1, 0, 2), ((0, 0), (0, 5), (0, 0)))
    cb = conv_b.reshape(2, 1, f)
    hu, hc, act = _ffn_up_conv(hn, w_up_bf, cw, cb)
    loss_blk, dh2, dh2_bf, dg_final = _ffn_down_loss(act, w_down_bf, h1, norm_final_g.reshape(1, D_MODEL), target)
    dw_down = _mm(act, dh2_bf, mode="tn", out_dtype=F32, tm=1408, tn=1024, tk=2048, name="ffn_down_dw")
    dact = _mm(dh2_bf, w_down_bf, mode="nt", out_dtype=F32, tm=512, tn=1408, tk=1024, outer="j", name="ffn_down_dx")
    dhu, dcw = _conv_gate_bwd(hc, hu, dact, _tie(cw, send_fn("w_down", dw_down)))
    dw_up = _mm(hn, dhu, mode="tn", out_dtype=F32, tm=1024, tn=1408, tk=2048, b_halves=True, outer="j", name="ffn_up_dw")
    dh1, dh1_bf, dg_ffn = _ffn_up_dx_rms(dhu, w_up_bf, h1, _tie(g_ffn, send_fn("w_up", dw_up)), dh2)
    dmix = _mm(dh1_bf, w_out_bf, mode="nt", out_dtype=F32, tm=512, tn=1024, tk=1024, name="out_proj_dx")
    dw_out = jnp.concatenate(
        [_mm(att_bf, dh1_bf, mode="tn", out_dtype=F32, tm=512, tn=1024, tk=1024, name="out_proj_dw_att"),
         _mm(sg, dh1_bf, mode="tn", out_dtype=F32, tm=512, tn=1024, tk=1024, name="out_proj_dw_sg")], axis=0)
    qb, doa = _attn_prep(att, lse, dmix, qa)
    dq, dk, dv, dcq, dck = _attn_bwd(qb, ka, va, doa)
    wt_s = w_s.transpose(0, 2, 1)
    dug, dw_s, dgain, dbias = _gmlp_bwd(ug, dmix, _tie(gain, send_fn("w_out", dw_out)), w_s, wt_s, bias_full)
    dzf, dbf = _gate_bwd(dcq, dck, zf)
    grad_x, dg_mix = _inproj_bwd_dx((dq, dk, dv, dug, dzf), w_pad, x, g_mix, dh1)
    grads = dict(
        norm_mix_g=dg_mix[0:1, :],
        b_forget=dbf[0:1, :N_HEADS],
        gmlp_norm_g=dgain[0:1, :],
        w_spatial=dw_s,
        b_spatial=dbias[:, ::GROUP_DIM].T,
        norm_ffn_g=dg_ffn[0:1, :],
        conv_w=dcw[:, 0:3, :].transpose(1, 0, 2).reshape(3, 2 * f),
        conv_b=dcw[:, 3, :].reshape(1, 2 * f),
        norm_final_g=dg_final[0, :],
    )
    token = small_fn(loss_blk[0, 0], grads)
    dw_in = _inproj_bwd_dw(xn, (dq, dk, dv, dug, _tie(dzf, token)))
    return grad_x, send_fn("w_in", dw_in[:, :IN_COLS])


SMALL = ("norm_mix_g", "b_forget", "gmlp_norm_g", "w_spatial", "b_spatial", "norm_ffn_g", "conv_b", "norm_final_g")


def kernel(x, norm_mix_g, w_in, b_forget, gmlp_norm_g, w_spatial, b_spatial, w_out, norm_ffn_g, w_up, conv_w, conv_b, w_down, norm_final_g, loss_target, m_norm_mix_g, m_w_in, m_b_forget, m_gmlp_norm_g, m_w_spatial, m_b_spatial, m_w_out, m_norm_ffn_g, m_w_up, m_conv_w, m_conv_b, m_w_down, m_norm_final_g, v_norm_mix_g, v_w_in, v_b_forget, v_gmlp_norm_g, v_w_spatial, v_b_spatial, v_w_out, v_norm_ffn_g, v_w_up, v_conv_w, v_conv_b, v_w_down, v_norm_final_g):
    weights = dict(norm_mix_g=norm_mix_g, w_in=w_in, b_forget=b_forget, gmlp_norm_g=gmlp_norm_g, w_spatial=w_spatial,
                   b_spatial=b_spatial, w_out=w_out, norm_ffn_g=norm_ffn_g, w_up=w_up, conv_w=conv_w, conv_b=conv_b,
                   w_down=w_down, norm_final_g=norm_final_g)
    m_in = dict(norm_mix_g=m_norm_mix_g, w_in=m_w_in, b_forget=m_b_forget, gmlp_norm_g=m_gmlp_norm_g,
                w_spatial=m_w_spatial, b_spatial=m_b_spatial, w_out=m_w_out, norm_ffn_g=m_norm_ffn_g, w_up=m_w_up,
                conv_w=m_conv_w, conv_b=m_conv_b, w_down=m_w_down, norm_final_g=m_norm_final_g)
    v_in = dict(norm_mix_g=v_norm_mix_g, w_in=v_w_in, b_forget=v_b_forget, gmlp_norm_g=v_gmlp_norm_g,
                w_spatial=v_w_spatial, b_spatial=v_b_spatial, w_out=v_w_out, norm_ffn_g=v_norm_ffn_g, w_up=v_w_up,
                conv_w=v_conv_w, conv_b=v_conv_b, w_down=v_w_down, norm_final_g=v_norm_final_g)
    order = list(weights)
    me = 4 * lax.axis_index("x") + 2 * lax.axis_index("y") + lax.axis_index("c")
    n_in, n_up = w_in.shape[2], w_up.shape[2]
    r_out, r_down = w_out.shape[1], w_down.shape[1]

    def with_mine(landed, mine):
        return lax.dynamic_update_index_in_dim(landed, mine, me, 0)

    up_blk = w_up[0].T.astype(BF16)
    out_blk = w_out[0].astype(BF16)
    down_blk = w_down[0].astype(BF16)
    taps_blk = jnp.pad(conv_w[0], ((0, 5), (0, 0)))
    (in_all,) = _all_gather([w_in[0].T.astype(BF16)], "gather_w_in")
    in_all, rest_blocks = lax.optimization_barrier((in_all, [up_blk, out_blk, down_blk, taps_blk]))
    rest_handle, token = _xchg_start(rest_blocks, [False] * 4, "gather_rest_start")
    w_in_t = in_all.reshape(N_DEV * n_in, D_MODEL)

    def rest_fn(after):
        mine, landed = _xchg_wait(rest_handle, [False] * 4, after, "gather_rest_wait")
        up_all, out_all, down_all, taps_all = [with_mine(l, b) for l, b in zip(landed, mine)]
        return (out_all.reshape(N_DEV * r_out, D_MODEL), up_all.reshape(N_DEV * n_up, D_MODEL),
                taps_all[:, :3, :].transpose(1, 0, 2).reshape(3, N_DEV * n_up),
                down_all.reshape(N_DEV * r_down, D_MODEL))

    sent = {}

    def send_fn(name, grad):
        if name == "w_in":
            parts = grad.reshape(D_MODEL, N_DEV, -1).transpose(1, 0, 2).astype(BF16)
        elif name == "w_up":
            parts = grad.reshape(D_MODEL, N_DEV, -1).transpose(1, 0, 2)
        else:
            parts = grad.reshape(N_DEV, -1, D_MODEL)
        sent[name], tok = _xchg_start([parts], [True], "scatter_" + name + "_start")
        return tok

    small = {}

    def small_fn(loss_local, g):
        loss_rows = jnp.pad(loss_local.reshape(1, 1), ((0, 31), (0, LANES - 1)))
        packed = [_rows128(g[k]) for k in SMALL] + [loss_rows, _rows128(g["conv_w"])]
        small["sizes"] = [p.shape[0] for p in packed]
        small["handle"], tok = _xchg_start([jnp.concatenate(packed, axis=0)], [False], "gather_small_start")
        return tok

    grad_x, after = _local_step(
        x[0], loss_target[0], norm_mix_g, w_in_t, b_forget, gmlp_norm_g, w_spatial, b_spatial, norm_ffn_g, conv_b,
        norm_final_g, rest_fn, send_fn, small_fn, token)

    outs = {}

    def update_big(name, after):
        (parts,), (landed,) = _xchg_wait(sent[name], [True], after, "scatter_" + name + "_wait")
        got = with_mine(landed, lax.dynamic_index_in_dim(parts, me, 0, keepdims=False))
        outs[name] = tuple(_adamw(weights[name], m_in[name], v_in[name], got, "adamw_" + name))
        return outs[name][0]

    for name in ("w_down", "w_up", "w_out"):
        after = update_big(name, after)

    (mine,), (landed,) = _xchg_wait(small["handle"], [False], after, "gather_small_wait")
    small_all = with_mine(landed, mine)
    sizes = small["sizes"]
    n_small_rows = sum(sizes[:-2])

    def pack(src):
        return jnp.concatenate([_rows128(src[k]) for k in SMALL] + [jnp.zeros((sizes[-2], LANES), F32)], axis=0)

    n_adam_rows = n_small_rows + sizes[-2]
    sg_, sd_, sm_, sv_ = _adamw(pack(weights), pack(m_in), pack(v_in), small_all[:, :n_adam_rows, :], "adamw_small")
    loss = sg_[n_small_rows, 0]
    off = 0
    for k, rows in zip(SMALL, sizes[:-2]):
        shp = weights[k].shape
        cnt = math.prod(shp)
        outs[k] = tuple(a[off:off + rows].reshape(-1)[:cnt].reshape(shp) for a in (sg_, sd_, sm_, sv_))
        off += rows
    taps_parts = small_all[:, n_adam_rows:, :].reshape(N_DEV, -1)[:, :3 * N_DEV * n_up].reshape(N_DEV, 3, N_DEV * n_up)
    taps_mine = lax.dynamic_slice_in_dim(taps_parts, me * n_up, n_up, axis=2)
    taps_mine = jnp.pad(taps_mine, ((0, 0), (0, 5), (0, 0)))

    def pad8(a):
        return jnp.pad(a[0], ((0, 5), (0, 0)))

    res = _adamw(pad8(conv_w), pad8(m_conv_w), pad8(v_conv_w), taps_mine, "adamw_conv_w")
    outs["conv_w"] = tuple(a[:3][None] for a in res)
    update_big("w_in", sg_)

    return (loss, grad_x[None], *[outs[k][0] for k in order], *[outs[k][1] for k in order],
            *[outs[k][2] for k in order], *[outs[k][3] for k in order])
```

```python
import functools
import math

import jax
import jax.numpy as jnp
from jax import lax
from jax.experimental import pallas as pl
from jax.experimental.pallas import tpu as pltpu

F32 = jnp.float32
BF16 = jnp.bfloat16

N_DEV = 8
D_MODEL = 1024
ATT_WIDTH = 512
GMLP_WIDTH = 512
HEAD_DIM = 64
N_HEADS = 8
N_PAIRS = 4
N_GROUPS = 8
GROUP_DIM = 64
CHUNK = 128
D_FF = 2816
IN_COLS = 2568
IN_PAD = 2688
QKV = 1536
UG_END = 2560
EPS = 1e-6
LANES = 128

ADAM_LR = 0.001
ADAM_B1 = 0.9
ADAM_B2 = 0.999
ADAM_EPS = 1e-08
ADAM_WD = 0.01
ADAM_STEP = 10

ATT_TQ = 1024
ATT_TK = 1024
FFN_TM, FFN_TN = 512, 1408
CONV_TM, CONV_TN = 256, 1408
VMEM_LIMIT = 56 * 1024 * 1024
MESH = pl.DeviceIdType.MESH


def _cp(sem, vmem=None):
    return pltpu.CompilerParams(dimension_semantics=sem, vmem_limit_bytes=vmem)


def _pick(n, prefs):
    for p in prefs:
        if n % p == 0:
            return p
    return n


def _split3(x):
    hi = x.astype(BF16)
    r1 = x - hi.astype(F32)
    mid = r1.astype(BF16)
    lo = (r1 - mid.astype(F32)).astype(BF16)
    return hi, mid, lo


def _dot3(x, ones_bf):
    hi, mid, lo = _split3(x)
    d = functools.partial(jnp.dot, preferred_element_type=F32)
    return d(hi, ones_bf) + d(mid, ones_bf) + d(lo, ones_bf)


def _dot3l(ones_bf, x):
    hi, mid, lo = _split3(x)
    d = functools.partial(jnp.dot, preferred_element_type=F32)
    return d(ones_bf, hi) + d(ones_bf, mid) + d(ones_bf, lo)


def _gelu(x):
    k = math.sqrt(2.0 / math.pi)
    t = jnp.tanh(k * (x + 0.044715 * (x * x * x)))
    return 0.5 * x * (1.0 + t)


def _gelu_grad(x):
    k = math.sqrt(2.0 / math.pi)
    x2 = x * x
    t = jnp.tanh(k * (x + 0.044715 * (x2 * x)))
    return 0.5 * (1.0 + t) + 0.5 * x * (1.0 - t * t) * (k * (1.0 + 3.0 * 0.044715 * x2))


def _sigmoid(x):
    return 1.0 / (1.0 + jnp.exp(-x))


def _mm(a, b, *, mode, out_dtype, tm, tn, tk, name, res=None, a_halves=False, b_halves=False,
        out_halves=False, outer="i"):
    if mode == "tn":
        K, M = a.shape[-2], a.shape[-1] * (2 if a_halves else 1)
    else:
        M, K = a.shape[-2], a.shape[-1] * (2 if a_halves else 1)
    if mode == "nt":
        N = b.shape[-2]
        assert b.shape[-1] == K
    else:
        N = b.shape[-1] * (2 if b_halves else 1)
    tm, tn, tk = min(tm, M), min(tn, N), min(tk, K)
    assert M % tm == 0 and N % tn == 0 and K % tk == 0, (name, M, N, K, tm, tn, tk)
    nm, nn, nk = M // tm, N // tn, K // tk

    def ij(g0, g1):
        return (g0, g1) if outer == "i" else (g1, g0)

    if mode == "nn":
        dims = (((1,), (0,)), ((), ()))
        if a_halves:
            nkh = nk // 2
            a_spec = pl.BlockSpec((None, tm, tk), lambda g0, g1, k: (k // nkh, ij(g0, g1)[0], k % nkh))
        else:
            a_spec = pl.BlockSpec((tm, tk), lambda g0, g1, k: (ij(g0, g1)[0], k))
        b_spec = pl.BlockSpec((tk, tn), lambda g0, g1, k: (k, ij(g0, g1)[1]))
    elif mode == "nt":
        dims = (((1,), (1,)), ((), ()))
        if a_halves:
            nkh = nk // 2
            a_spec = pl.BlockSpec((None, tm, tk), lambda g0, g1, k: (k // nkh, ij(g0, g1)[0], k % nkh))
        else:
            a_spec = pl.BlockSpec((tm, tk), lambda g0, g1, k: (ij(g0, g1)[0], k))
        b_spec = pl.BlockSpec((tn, tk), lambda g0, g1, k: (ij(g0, g1)[1], k))
    else:
        dims = (((0,), (0,)), ((), ()))
        if a_halves:
            nmh = nm // 2
            a_spec = pl.BlockSpec((None, tk, tm), lambda g0, g1, k: (ij(g0, g1)[0] // nmh, k, ij(g0, g1)[0] % nmh))
        else:
            a_spec = pl.BlockSpec((tk, tm), lambda g0, g1, k: (k, ij(g0, g1)[0]))
        if b_halves:
            nnh = nn // 2
            b_spec = pl.BlockSpec((None, tk, tn), lambda g0, g1, k: (ij(g0, g1)[1] // nnh, k, ij(g0, g1)[1] % nnh))
        else:
            b_spec = pl.BlockSpec((tk, tn), lambda g0, g1, k: (k, ij(g0, g1)[1]))
    if out_halves:
        nnh = nn // 2
        o_spec = pl.BlockSpec((None, tm, tn), lambda g0, g1, k: (ij(g0, g1)[1] // nnh, ij(g0, g1)[0], ij(g0, g1)[1] % nnh))
        o_shape = jax.ShapeDtypeStruct((2, M, N // 2), out_dtype)
    else:
        o_spec = pl.BlockSpec((tm, tn), lambda g0, g1, k: ij(g0, g1))
        o_shape = jax.ShapeDtypeStruct((M, N), out_dtype)
    in_specs = [a_spec, b_spec]
    args = [a, b]
    if res is not None:
        in_specs.append(pl.BlockSpec((tm, tn), lambda g0, g1, k: ij(g0, g1)))
        args.append(res)

    def body(*refs):
        if res is not None:
            a_ref, b_ref, r_ref, o_ref = refs[:4]
        else:
            a_ref, b_ref, o_ref = refs[:3]
            r_ref = None
        part = lax.dot_general(a_ref[...], b_ref[...], dims, preferred_element_type=F32)
        if nk == 1:
            if r_ref is not None:
                part = part + r_ref[...]
            o_ref[...] = part.astype(out_dtype)
            return
        acc_ref = refs[-1]
        k = pl.program_id(2)

        @pl.when(k == 0)
        def _():
            acc_ref[...] = part

        @pl.when(k > 0)
        def _():
            acc_ref[...] += part

        @pl.when(k == nk - 1)
        def _():
            tot = acc_ref[...]
            if r_ref is not None:
                tot = tot + r_ref[...]
            o_ref[...] = tot.astype(out_dtype)

    grid = (nm, nn, nk) if outer == "i" else (nn, nm, nk)
    scratch = [] if nk == 1 else [pltpu.VMEM((tm, tn), F32)]
    return pl.pallas_call(
        body, out_shape=o_shape, grid=grid, in_specs=in_specs, out_specs=o_spec, scratch_shapes=scratch,
        name=name, compiler_params=_cp(("parallel", "parallel", "arbitrary"), VMEM_LIMIT),
    )(*args)


def _aug(lane, terms):
    out = 0.0
    for j, t in enumerate(terms):
        out = jnp.where(lane == HEAD_DIM + j, t, out)
    return out


def _split3f(x):
    hi, mid, lo = _split3(x)
    return [hi.astype(F32), mid.astype(F32), lo.astype(F32)]


def _inproj_fwd(x, g_mix, w_pad, bf_pad):
    S = x.shape[0]
    tm = _pick(S, (512, 256))
    tri = (lax.broadcasted_iota(jnp.int32, (tm, tm), 0) >= lax.broadcasted_iota(jnp.int32, (tm, tm), 1)).astype(BF16)

    def body(x_ref, g_ref, w_ref, bf_ref, tri_ref, xn_ref, qa_ref, ka_ref, va_ref, ug_ref, zf_ref, carry_ref):
        i = pl.program_id(0)

        @pl.when(i == 0)
        def _():
            carry_ref[...] = jnp.zeros_like(carry_ref)

        xf = x_ref[...]
        r = lax.rsqrt(jnp.mean(xf * xf, axis=-1, keepdims=True) + EPS)
        xn = ((xf * r) * g_ref[...]).astype(BF16)
        xn_ref[...] = xn
        proj = lax.dot_general(xn, w_ref[...], _NT, preferred_element_type=F32)
        ug_ref[...] = proj[:, QKV:UG_END]
        zf = proj[:, UG_END:] + bf_ref[...]
        zf_ref[...] = zf
        lf = jnp.minimum(zf, 0.0) - jnp.log(1.0 + jnp.exp(-jnp.abs(zf)))
        c = _dot3l(tri_ref[...], lf) + carry_ref[0:1, :]
        carry_ref[0:1, :] = c[tm - 1:tm, :]
        c3 = _split3f(c)
        lane = lax.broadcasted_iota(jnp.int32, (tm, LANES), 1)
        ones3 = [1.0, 1.0, 1.0]
        for h in range(N_HEADS):
            p, odd = h // 2, h % 2
            ch = [t[:, h:h + 1] for t in c3]

            def head(base, scale=None, p=p, odd=odd):
                blk = proj[:, base + p * LANES:base + (p + 1) * LANES]
                if scale is not None:
                    blk = blk * scale
                return pltpu.roll(blk, HEAD_DIM, 1) if odd else blk

            cols = slice(h * LANES, (h + 1) * LANES)
            qa_ref[:, cols] = jnp.where(lane < HEAD_DIM, head(0, HEAD_DIM ** -0.5), _aug(lane, ch + ones3)).astype(BF16)
            ka_ref[:, cols] = jnp.where(lane < HEAD_DIM, head(ATT_WIDTH),
                                        _aug(lane, ones3 + [-t for t in ch] + ones3)).astype(BF16)
            va_ref[:, cols] = jnp.where(lane < HEAD_DIM, head(2 * ATT_WIDTH), _aug(lane, ones3)).astype(BF16)

    wide = N_HEADS * LANES
    return pl.pallas_call(
        body,
        out_shape=(jax.ShapeDtypeStruct((S, D_MODEL), BF16), jax.ShapeDtypeStruct((S, wide), BF16),
                   jax.ShapeDtypeStruct((S, wide), BF16), jax.ShapeDtypeStruct((S, wide), BF16),
                   jax.ShapeDtypeStruct((S, 2 * GMLP_WIDTH), F32), jax.ShapeDtypeStruct((S, LANES), F32)),
        grid=(S // tm,),
        in_specs=[pl.BlockSpec((tm, D_MODEL), lambda i: (i, 0)), pl.BlockSpec((1, D_MODEL), lambda i: (0, 0)),
                  pl.BlockSpec((IN_PAD, D_MODEL), lambda i: (0, 0)), pl.BlockSpec((1, LANES), lambda i: (0, 0)),
                  pl.BlockSpec((tm, tm), lambda i: (0, 0))],
        out_specs=(pl.BlockSpec((tm, D_MODEL), lambda i: (i, 0)), pl.BlockSpec((tm, wide), lambda i: (i, 0)),
                   pl.BlockSpec((tm, wide), lambda i: (i, 0)), pl.BlockSpec((tm, wide), lambda i: (i, 0)),
                   pl.BlockSpec((tm, 2 * GMLP_WIDTH), lambda i: (i, 0)), pl.BlockSpec((tm, LANES), lambda i: (i, 0))),
        scratch_shapes=[pltpu.VMEM((8, LANES), F32)],
        name="inproj_fwd", compiler_params=_cp(("arbitrary",), VMEM_LIMIT),
    )(x, g_mix, w_pad, bf_pad, tri)


def _group_ones():
    r = lax.broadcasted_iota(jnp.int32, (GMLP_WIDTH, GMLP_WIDTH), 0) // GROUP_DIM
    c = lax.broadcasted_iota(jnp.int32, (GMLP_WIDTH, GMLP_WIDTH), 1) // GROUP_DIM
    return (r == c).astype(BF16)


def _gmlp_mixed(vn_bf, w_ref, bias, n_chunks):
    lane = lax.broadcasted_iota(jnp.int32, (CHUNK, LANES), 1)
    row = lax.broadcasted_iota(jnp.int32, (CHUNK, CHUNK), 0)
    col = lax.broadcasted_iota(jnp.int32, (CHUNK, CHUNK), 1)
    ws = [jnp.where(row >= col, w_ref[g], 0.0).astype(BF16) for g in range(N_GROUPS)]
    rows = []
    for ci in range(n_chunks):
        cols = []
        for pp in range(N_GROUPS // 2):
            v = vn_bf[ci * CHUNK:(ci + 1) * CHUNK, pp * LANES:(pp + 1) * LANES]
            v_lo = jnp.where(lane < GROUP_DIM, v, jnp.zeros_like(v))
            v_hi = jnp.where(lane >= GROUP_DIM, v, jnp.zeros_like(v))
            m = (jnp.dot(ws[2 * pp], v_lo, preferred_element_type=F32)
                 + jnp.dot(ws[2 * pp + 1], v_hi, preferred_element_type=F32))
            cols.append(m + bias[:, pp * LANES:(pp + 1) * LANES])
        rows.append(jnp.concatenate(cols, axis=1))
    return jnp.concatenate(rows, axis=0)


def _gmlp_fwd(ug, gain, w_s, bias_full):
    S = ug.shape[0]
    tm = _pick(S, (512, 256, 128))
    ones = _group_ones()

    def body(ug_ref, gain_ref, w_ref, bias_ref, ones_ref, sg_ref):
        u = _gelu(ug_ref[:, :GMLP_WIDTH])
        vr = _gelu(ug_ref[:, GMLP_WIDTH:])
        ms = _dot3(vr * vr, ones_ref[...]) * (1.0 / GROUP_DIM)
        vn = ((vr * lax.rsqrt(ms + EPS)) * gain_ref[...]).astype(BF16)
        mixed = _gmlp_mixed(vn, w_ref, bias_ref[...], tm // CHUNK)
        sg_ref[...] = (u * mixed).astype(BF16)

    return pl.pallas_call(
        body, out_shape=jax.ShapeDtypeStruct((S, GMLP_WIDTH), BF16), grid=(S // tm,),
        in_specs=[pl.BlockSpec((tm, 2 * GMLP_WIDTH), lambda i: (i, 0)), pl.BlockSpec((1, GMLP_WIDTH), lambda i: (0, 0)),
                  pl.BlockSpec((N_GROUPS, CHUNK, CHUNK), lambda i: (0, 0, 0)),
                  pl.BlockSpec((CHUNK, GMLP_WIDTH), lambda i: (0, 0)),
                  pl.BlockSpec((GMLP_WIDTH, GMLP_WIDTH), lambda i: (0, 0))],
        out_specs=pl.BlockSpec((tm, GMLP_WIDTH), lambda i: (i, 0)),
        name="gmlp_fwd", compiler_params=_cp(("parallel",), VMEM_LIMIT),
    )(ug, gain, w_s, bias_full, ones)


_NT = (((1,), (1,)), ((), ()))
_TN = (((0,), (0,)), ((), ()))


def _attn_fwd(qa, ka, va):
    S = qa.shape[0]
    tq = _pick(S, (ATT_TQ, 256))
    tk = min(ATT_TK, tq)
    nq = S // tq
    per_q = tq // tk

    def body(q_ref, k_ref, v_ref, o_ref, lse_ref, ob_ref):
        qi = pl.program_id(1)
        lane = lax.broadcasted_iota(jnp.int32, (tq, LANES), 1)
        sub = tk
        rid = lax.broadcasted_iota(jnp.int32, (sub, sub), 0)
        cid = lax.broadcasted_iota(jnp.int32, (sub, sub), 1)
        qs = [q_ref[:, :LANES], q_ref[:, LANES:]]

        def update(q, ks, k_len, h, m, acc, masked):
            cols = slice(h * LANES, (h + 1) * LANES)
            s = lax.dot_general(q, k_ref[pl.ds(ks, k_len), cols], _NT, preferred_element_type=F32)
            if masked:
                s = jnp.where(rid >= cid, s, -jnp.inf)
            m_new = jnp.maximum(m, jnp.max(s, axis=-1, keepdims=True))
            p = jnp.exp(s - m_new).astype(BF16)
            acc = jnp.exp(m - m_new) * acc + jnp.dot(p, v_ref[pl.ds(ks, k_len), cols], preferred_element_type=F32)
            return m_new, acc

        def step(kb, carry):
            ks = pl.multiple_of(kb * tk, tk)
            return tuple(update(qs[h], ks, tk, h, *carry[h], False) for h in range(2))

        one = (jnp.full((tq, 1), -jnp.inf, F32), jnp.zeros((tq, LANES), F32))
        carry = lax.fori_loop(0, qi * per_q, step, (one, one))
        outs, lses = [], []
        for h in range(2):
            ms, accs = [], []
            for r in range(tq // sub):
                rows = slice(r * sub, (r + 1) * sub)
                m, acc = carry[h][0][rows], carry[h][1][rows]
                for c in range(r + 1):
                    ks = pl.multiple_of(qi * tq + c * sub, sub)
                    m, acc = update(qs[h][rows], ks, sub, h, m, acc, c == r)
                ms.append(m)
                accs.append(acc)
            m, acc = jnp.concatenate(ms, axis=0), jnp.concatenate(accs, axis=0)
            l = acc[:, HEAD_DIM:HEAD_DIM + 1]
            outs.append(acc / l)
            lses.append(m + jnp.log(l))
        o = jnp.where(lane < HEAD_DIM, outs[0], pltpu.roll(outs[1], HEAD_DIM, 1))
        o_ref[...] = o
        ob_ref[...] = o.astype(BF16)
        lse_ref[...] = jnp.where(lane < HEAD_DIM, lses[0], lses[1])

    return pl.pallas_call(
        body,
        out_shape=(jax.ShapeDtypeStruct((S, ATT_WIDTH), F32), jax.ShapeDtypeStruct((S, ATT_WIDTH), F32),
                   jax.ShapeDtypeStruct((S, ATT_WIDTH), BF16)),
        grid=(N_PAIRS, nq),
        in_specs=[pl.BlockSpec((tq, 2 * LANES), lambda p, i: (i, p)),
                  pl.BlockSpec((S, 2 * LANES), lambda p, i: (0, p)),
                  pl.BlockSpec((S, 2 * LANES), lambda p, i: (0, p))],
        out_specs=(pl.BlockSpec((tq, LANES), lambda p, i: (i, p)), pl.BlockSpec((tq, LANES), lambda p, i: (i, p)),
                   pl.BlockSpec((tq, LANES), lambda p, i: (i, p))),
        name="attn_fwd", compiler_params=_cp(("parallel", "parallel"), VMEM_LIMIT),
    )(qa, ka, va)


def _shift_rows(x, prev, n):
    rid = lax.broadcasted_iota(jnp.int32, x.shape, 0)
    y = pltpu.roll(x, n, 0)
    if n == 1:
        return jnp.where(rid == 0, prev[7:8, :], y)
    return jnp.where(rid == 0, prev[6:7, :], jnp.where(rid == 1, prev[7:8, :], y))


def _shift_rows_up(x, nxt, n):
    rows = x.shape[0]
    rid = lax.broadcasted_iota(jnp.int32, x.shape, 0)
    y = pltpu.roll(x, rows - n, 0)
    if n == 1:
        return jnp.where(rid == rows - 1, nxt[0:1, :], y)
    return jnp.where(rid == rows - 2, nxt[0:1, :], jnp.where(rid == rows - 1, nxt[1:2, :], y))


def _conv3(cur, prev, w, b):
    return (w[0:1, :] * _shift_rows(cur, prev, 2) + w[1:2, :] * _shift_rows(cur, prev, 1)
            + w[2:3, :] * cur + b)


def _ffn_up_conv(hn, w_up_bf, cw, cb):
    S = hn.shape[0]
    F = D_FF
    tm = _pick(S, (FFN_TM, 256))
    tn = _pick(F, (FFN_TN, 256, 128))
    nj = F // tn

    def body(hn_ref, wa_ref, wg_ref, cw_ref, cb_ref, hu_ref, hc_ref, act_ref, tail_ref):
        i = pl.program_id(1)

        @pl.when(i == 0)
        def _():
            tail_ref[...] = jnp.zeros_like(tail_ref)

        hn_v = hn_ref[...]
        halves = []
        for h, w_ref in enumerate((wa_ref, wg_ref)):
            hu = lax.dot_general(hn_v, w_ref[...], _NT, preferred_element_type=F32)
            hu_ref[h] = hu
            hc = _conv3(hu, tail_ref[h], cw_ref[h], cb_ref[h])
            hc_ref[h] = hc
            halves.append(hc)
            tail_ref[h] = hu[tm - 8:, :]
        a, g = halves
        act_ref[...] = (g * _sigmoid(g) * a).astype(BF16)

    both = pl.BlockSpec((2, tm, tn), lambda j, i: (0, i, j))
    return pl.pallas_call(
        body, out_shape=(jax.ShapeDtypeStruct((2, S, F), F32), jax.ShapeDtypeStruct((2, S, F), F32),
                         jax.ShapeDtypeStruct((S, F), BF16)),
        grid=(nj, S // tm),
        in_specs=[pl.BlockSpec((tm, D_MODEL), lambda j, i: (i, 0)),
                  pl.BlockSpec((tn, D_MODEL), lambda j, i: (j, 0)),
                  pl.BlockSpec((tn, D_MODEL), lambda j, i: (nj + j, 0)),
                  pl.BlockSpec((2, 8, tn), lambda j, i: (0, 0, j)),
                  pl.BlockSpec((2, 1, tn), lambda j, i: (0, 0, j))],
        out_specs=(both, both, pl.BlockSpec((tm, tn), lambda j, i: (i, j))),
        scratch_shapes=[pltpu.VMEM((2, 8, tn), F32)],
        name="ffn_up_conv", compiler_params=_cp(("parallel", "arbitrary"), VMEM_LIMIT),
    )(hn, w_up_bf, w_up_bf, cw, cb)


def _ffn_down_loss(act, w_down_bf, h1, g_final, target):
    S = h1.shape[0]
    tm = _pick(S, (512, 256))

    def body(a_ref, w_ref, h1_ref, g_ref, t_ref, loss_ref, dh_ref, dhb_ref, dg_ref):
        i = pl.program_id(0)

        @pl.when(i == 0)
        def _():
            loss_ref[...] = jnp.zeros_like(loss_ref)
            dg_ref[...] = jnp.zeros_like(dg_ref)

        hf = h1_ref[...] + jnp.dot(a_ref[...], w_ref[...], preferred_element_type=F32)
        g = g_ref[...]
        r = lax.rsqrt(jnp.mean(hf * hf, axis=-1, keepdims=True) + EPS)
        hhat = hf * r
        err = hhat * g - t_ref[...]
        loss_ref[...] += 0.5 * jnp.sum(jnp.mean(err * err, axis=-1, keepdims=True))
        dy = err * (1.0 / D_MODEL)
        dg_ref[0:1, :] += jnp.sum(dy * hhat, axis=0, keepdims=True)
        dhat = dy * g
        dh = r * (dhat - hhat * jnp.mean(dhat * hhat, axis=-1, keepdims=True))
        dh_ref[...] = dh
        dhb_ref[...] = dh.astype(BF16)

    row = pl.BlockSpec((tm, D_MODEL), lambda i: (i, 0))
    return pl.pallas_call(
        body,
        out_shape=(jax.ShapeDtypeStruct((8, LANES), F32), jax.ShapeDtypeStruct((S, D_MODEL), F32),
                   jax.ShapeDtypeStruct((S, D_MODEL), BF16), jax.ShapeDtypeStruct((8, D_MODEL), F32)),
        grid=(S // tm,),
        in_specs=[pl.BlockSpec((tm, D_FF), lambda i: (i, 0)), pl.BlockSpec((D_FF, D_MODEL), lambda i: (0, 0)), row,
                  pl.BlockSpec((1, D_MODEL), lambda i: (0, 0)), row],
        out_specs=(pl.BlockSpec((8, LANES), lambda i: (0, 0)), row, row, pl.BlockSpec((8, D_MODEL), lambda i: (0, 0))),
        name="ffn_down_loss", compiler_params=_cp(("arbitrary",), VMEM_LIMIT),
    )(act, w_down_bf, h1, g_final, target)


def _ffn_up_dx_rms(dhu, w_up_bf, h1, g_ffn, dh2):
    _, S, F = dhu.shape
    tm = _pick(S, (512, 256))

    def body(a_ref, b_ref, h_ref, g_ref, r_ref, dh_ref, dhb_ref, dg_ref, acc_ref):
        i, k = pl.program_id(0), pl.program_id(1)

        @pl.when((i == 0) & (k == 0))
        def _():
            dg_ref[...] = jnp.zeros_like(dg_ref)

        part = jnp.dot(a_ref[...], b_ref[...], preferred_element_type=F32)

        @pl.when(k == 0)
        def _():
            acc_ref[...] = part

        @pl.when(k == 1)
        def _():
            dyv = acc_ref[...] + part
            hf = h_ref[...]
            r = lax.rsqrt(jnp.mean(hf * hf, axis=-1, keepdims=True) + EPS)
            hhat = hf * r
            dg_ref[0:1, :] += jnp.sum(dyv * hhat, axis=0, keepdims=True)
            dhat = dyv * g_ref[...]
            dh = r_ref[...] + r * (dhat - hhat * jnp.mean(dhat * hhat, axis=-1, keepdims=True))
            dh_ref[...] = dh
            dhb_ref[...] = dh.astype(BF16)

    row = pl.BlockSpec((tm, D_MODEL), lambda i, k: (i, 0))
    return pl.pallas_call(
        body,
        out_shape=(jax.ShapeDtypeStruct((S, D_MODEL), F32), jax.ShapeDtypeStruct((S, D_MODEL), BF16),
                   jax.ShapeDtypeStruct((8, D_MODEL), F32)),
        grid=(S // tm, 2),
        in_specs=[pl.BlockSpec((None, tm, F), lambda i, k: (k, i, 0)), pl.BlockSpec((F, D_MODEL), lambda i, k: (k, 0)),
                  row, pl.BlockSpec((1, D_MODEL), lambda i, k: (0, 0)), row],
        out_specs=(row, row, pl.BlockSpec((8, D_MODEL), lambda i, k: (0, 0))),
        scratch_shapes=[pltpu.VMEM((tm, D_MODEL), F32)],
        name="ffn_up_dx_rms", compiler_params=_cp(("arbitrary", "arbitrary"), VMEM_LIMIT),
    )(dhu, w_up_bf, h1, g_ffn, dh2)


def _conv_gate_bwd(hc, hu, dact, cw):
    _, S, F = hu.shape
    tm = _pick(S, (CONV_TM, 128))
    tn = _pick(F, (CONV_TN, 256, 128))
    r8 = tm // 8
    n_i = S // tm
    last8 = S // 8 - 1

    def body(hc_ref, hcn_ref, hu_ref, da_ref, dan_ref, w_ref, dhu_ref, dcw_ref):
        i = pl.program_id(1)

        @pl.when(i == 0)
        def _():
            dcw_ref[...] = jnp.zeros_like(dcw_ref)

        rid8 = lax.broadcasted_iota(jnp.int32, (8, tn), 0)

        def gate_grads(a, g, d):
            sg = _sigmoid(g)
            return d * (g * sg), d * a * (sg * (1.0 + g * (1.0 - sg)))

        dhc = gate_grads(hc_ref[0], hc_ref[1], da_ref[...])
        dhc_n = gate_grads(hcn_ref[0], hcn_ref[1], dan_ref[...])
        for h in range(2):
            w = w_ref[h]
            d = dhc[h]
            dn = jnp.where(i < n_i - 1, dhc_n[h], 0.0)
            u1 = _shift_rows_up(d, dn, 1)
            u2 = _shift_rows_up(d, dn, 2)
            dhu_ref[h] = (w[2:3, :] * d + w[1:2, :] * u1 + w[0:1, :] * u2).astype(BF16)
            x = hu_ref[h]
            t0, t1, t2, t3 = [jnp.sum(t, axis=0, keepdims=True) for t in (u2 * x, u1 * x, d * x, d)]
            dcw_ref[h] += jnp.where(rid8 == 0, t0, jnp.where(rid8 == 1, t1, jnp.where(rid8 == 2, t2, jnp.where(rid8 == 3, t3, 0.0))))

    cur = pl.BlockSpec((2, tm, tn), lambda j, i: (0, i, j))
    return pl.pallas_call(
        body,
        out_shape=(jax.ShapeDtypeStruct((2, S, F), BF16), jax.ShapeDtypeStruct((2, 8, F), F32)),
        grid=(F // tn, n_i),
        in_specs=[cur, pl.BlockSpec((2, 8, tn), lambda j, i: (0, jnp.minimum((i + 1) * r8, last8), j)), cur,
                  pl.BlockSpec((tm, tn), lambda j, i: (i, j)),
                  pl.BlockSpec((8, tn), lambda j, i: (jnp.minimum((i + 1) * r8, last8), j)),
                  pl.BlockSpec((2, 8, tn), lambda j, i: (0, 0, j))],
        out_specs=(cur, pl.BlockSpec((2, 8, tn), lambda j, i: (0, 0, j))),
        name="conv_gate_bwd", compiler_params=_cp(("parallel", "arbitrary"), VMEM_LIMIT),
    )(hc, hc, hu, dact, dact, cw)


def _attn_prep(att, lse, dmix, qa):
    S = att.shape[0]
    tm = _pick(S, (256,))

    def body(o_ref, lse_ref, do_ref, q_ref, qb_ref, doa_ref):
        lane = lax.broadcasted_iota(jnp.int32, (tm, LANES), 1)
        for p in range(N_PAIRS):
            pc = slice(p * LANES, (p + 1) * LANES)
            do = do_ref[:, pc]
            prod = o_ref[:, pc] * do
            for hh in range(2):
                sel = (lane >= HEAD_DIM) if hh else (lane < HEAD_DIM)
                delta = jnp.sum(jnp.where(sel, prod, 0.0), axis=-1, keepdims=True)
                dod = pltpu.roll(do, HEAD_DIM, 1) if hh else do
                cols = slice((2 * p + hh) * LANES, (2 * p + hh + 1) * LANES)
                doa_ref[:, cols] = jnp.where(lane < HEAD_DIM, dod, _aug(lane, _split3f(-delta))).astype(BF16)
                lcol = p * LANES + hh * HEAD_DIM
                l3 = _split3f(-lse_ref[:, lcol:lcol + 1])
                augl = _aug(lane, [0.0] * 6 + l3).astype(BF16)
                qb_ref[:, cols] = jnp.where((lane >= HEAD_DIM + 6) & (lane < HEAD_DIM + 9), augl, q_ref[:, cols])

    half = pl.BlockSpec((tm, ATT_WIDTH), lambda i: (i, 0))
    wide = pl.BlockSpec((tm, N_HEADS * LANES), lambda i: (i, 0))
    return pl.pallas_call(
        body,
        out_shape=(jax.ShapeDtypeStruct(qa.shape, BF16), jax.ShapeDtypeStruct(qa.shape, BF16)),
        grid=(S // tm,), in_specs=[half, half, half, wide], out_specs=(wide, wide),
        name="attn_prep", compiler_params=_cp(("parallel",), VMEM_LIMIT),
    )(att, lse, dmix, qa)


def _attn_bwd(qb, ka, va, doa):
    S = qb.shape[0]
    tk = _pick(S, (512, 256))
    tq = tk
    nq = S // tq

    def pair(a, scale=None):
        lane = lax.broadcasted_iota(jnp.int32, (a.shape[0], LANES), 1)
        out = jnp.where(lane < HEAD_DIM, a[:, :LANES], pltpu.roll(a[:, LANES:], HEAD_DIM, 1))
        return out if scale is None else out * scale

    def lanes01(a, col, sign):
        lane = lax.broadcasted_iota(jnp.int32, (a.shape[0], LANES), 1)
        return jnp.where(lane == 0, sign * a[:, col:col + 1], jnp.where(lane == 1, sign * a[:, LANES + col:LANES + col + 1], 0.0))

    def body(q_ref, do_ref, k_ref, v_ref, dqc_ref, dkc_ref, dvc_ref, dcq_ref, dck_ref, dq_ref, dka_ref, dva_ref):
        kb = pl.program_id(1)

        @pl.when(kb == 0)
        def _():
            dq_ref[...] = jnp.zeros_like(dq_ref)

        dka_ref[...] = jnp.zeros_like(dka_ref)
        dva_ref[...] = jnp.zeros_like(dva_ref)
        rid = lax.broadcasted_iota(jnp.int32, (tk, tq), 0)
        cid = lax.broadcasted_iota(jnp.int32, (tk, tq), 1)

        def sub_tile(qs, q_len, k_off, k_len, masked):
            keys = slice(k_off, k_off + k_len)
            for h in range(2):
                cols = slice(h * LANES, (h + 1) * LANES)
                qblk = q_ref[pl.ds(qs, q_len), cols]
                doblk = do_ref[pl.ds(qs, q_len), cols]
                kh = k_ref[keys, cols]
                p = jnp.exp(lax.dot_general(kh, qblk, _NT, preferred_element_type=F32))
                if masked:
                    p = jnp.where(cid >= rid, p, 0.0)
                ds = (p * lax.dot_general(v_ref[keys, cols], doblk, _NT, preferred_element_type=F32)).astype(BF16)
                dva_ref[keys, cols] += jnp.dot(p.astype(BF16), doblk, preferred_element_type=F32)
                dka_ref[keys, cols] += jnp.dot(ds, qblk, preferred_element_type=F32)
                dq_ref[pl.ds(qs, q_len), cols] += lax.dot_general(ds, kh, _TN, preferred_element_type=F32)

        sub_tile(pl.multiple_of(kb * tq, tq), tq, 0, tk, True)

        def step(qi, carry):
            sub_tile(pl.multiple_of(qi * tq, tq), tq, 0, tk, False)
            return carry

        lax.fori_loop(kb + 1, nq, step, 0)
        dka = dka_ref[...]
        dkc_ref[...] = pair(dka).astype(BF16)
        dvc_ref[...] = pair(dva_ref[...]).astype(BF16)
        dck_ref[...] = lanes01(dka, HEAD_DIM + 3, -1.0)

        @pl.when(kb == nq - 1)
        def _():
            dqa = dq_ref[...]
            dqc_ref[...] = pair(dqa, HEAD_DIM ** -0.5).astype(BF16)
            dcq_ref[...] = lanes01(dqa, HEAD_DIM, 1.0)

    wide = 2 * LANES
    half = jax.ShapeDtypeStruct((S, ATT_WIDTH), BF16)
    slabs = jax.ShapeDtypeStruct((N_PAIRS, S, LANES), F32)
    return pl.pallas_call(
        body,
        out_shape=(half, half, half, slabs, slabs),
        grid=(N_PAIRS, nq),
        in_specs=[pl.BlockSpec((S, wide), lambda p, j: (0, p)), pl.BlockSpec((S, wide), lambda p, j: (0, p)),
                  pl.BlockSpec((tk, wide), lambda p, j: (j, p)), pl.BlockSpec((tk, wide), lambda p, j: (j, p))],
        out_specs=(pl.BlockSpec((S, LANES), lambda p, j: (0, p)), pl.BlockSpec((tk, LANES), lambda p, j: (j, p)),
                   pl.BlockSpec((tk, LANES), lambda p, j: (j, p)), pl.BlockSpec((None, S, LANES), lambda p, j: (p, 0, 0)),
                   pl.BlockSpec((None, tk, LANES), lambda p, j: (p, j, 0))),
        scratch_shapes=[pltpu.VMEM((S, wide), F32), pltpu.VMEM((tk, wide), F32), pltpu.VMEM((tk, wide), F32)],
        name="attn_bwd", compiler_params=_cp(("parallel", "arbitrary"), VMEM_LIMIT),
    )(qb, doa, ka, va)


def _gmlp_bwd(ug, dsg, gain, w_s, wt_s, bias_full):
    S = ug.shape[0]
    tm = _pick(S, (512, 256, 128))
    n_chunks = tm // CHUNK
    n_i = S // tm
    ones = _group_ones()
    nt = (((1,), (1,)), ((), ()))

    def body(ug_ref, dsg_ref, gain_ref, w_ref, wt_ref, bias_ref, ones_ref, dug_ref, dw_ref, dgain_ref, dbias_ref,
             dbacc_ref):
        i = pl.program_id(0)

        @pl.when(i == 0)
        def _():
            dw_ref[...] = jnp.zeros_like(dw_ref)
            dgain_ref[...] = jnp.zeros_like(dgain_ref)
            dbacc_ref[...] = jnp.zeros_like(dbacc_ref)

        ones_m = ones_ref[...]
        pu = ug_ref[:, :GMLP_WIDTH]
        pg = ug_ref[:, GMLP_WIDTH:]
        u = _gelu(pu)
        vr = _gelu(pg)
        ms = _dot3(vr * vr, ones_m) * (1.0 / GROUP_DIM)
        rinv = lax.rsqrt(ms + EPS)
        vhat = vr * rinv
        gain_v = gain_ref[...]
        vn = (vhat * gain_v).astype(BF16)
        mixed = _gmlp_mixed(vn, w_ref, bias_ref[...], n_chunks)
        dsg_v = dsg_ref[...]
        du = dsg_v * mixed
        dmixed = dsg_v * u
        dm_bf = dmixed.astype(BF16)
        lane = lax.broadcasted_iota(jnp.int32, (CHUNK, LANES), 1)
        row = lax.broadcasted_iota(jnp.int32, (CHUNK, CHUNK), 0)
        col = lax.broadcasted_iota(jnp.int32, (CHUNK, CHUNK), 1)
        wts = [jnp.where(col >= row, wt_ref[g], 0.0).astype(BF16) for g in range(N_GROUPS)]
        dvn_rows = []
        dbsum = jnp.zeros((CHUNK, GMLP_WIDTH), F32)
        for ci in range(n_chunks):
            rs = slice(ci * CHUNK, (ci + 1) * CHUNK)
            dbsum = dbsum + dmixed[rs, :]
            cols = []
            for pp in range(N_GROUPS // 2):
                cs = slice(pp * LANES, (pp + 1) * LANES)
                dm = dm_bf[rs, cs]
                dm_lo = jnp.where(lane < GROUP_DIM, dm, jnp.zeros_like(dm))
                dm_hi = jnp.where(lane >= GROUP_DIM, dm, jnp.zeros_like(dm))
                vb = vn[rs, cs]
                dw_ref[2 * pp] += lax.dot_general(dm_lo, vb, nt, preferred_element_type=F32)
                dw_ref[2 * pp + 1] += lax.dot_general(dm_hi, vb, nt, preferred_element_type=F32)
                cols.append(jnp.dot(wts[2 * pp], dm_lo, preferred_element_type=F32)
                            + jnp.dot(wts[2 * pp + 1], dm_hi, preferred_element_type=F32))
            dvn_rows.append(jnp.concatenate(cols, axis=1))
        dvn = jnp.concatenate(dvn_rows, axis=0)
        dbacc_ref[...] += dbsum
        dgain_ref[0:1, :] += jnp.sum(dvn * vhat, axis=0, keepdims=True)
        dvhat = dvn * gain_v
        gm = _dot3(dvhat * vhat, ones_m) * (1.0 / GROUP_DIM)
        dvr = rinv * (dvhat - vhat * gm)
        dug_ref[:, :GMLP_WIDTH] = (du * _gelu_grad(pu)).astype(BF16)
        dug_ref[:, GMLP_WIDTH:] = (dvr * _gelu_grad(pg)).astype(BF16)

        @pl.when(i == n_i - 1)
        def _():
            for g in range(N_GROUPS):
                dw_ref[g] = jnp.where(row >= col, dw_ref[g], 0.0)
            dbias_ref[...] = _dot3(dbacc_ref[...], ones_m)

    return pl.pallas_call(
        body,
        out_shape=(jax.ShapeDtypeStruct((S, 2 * GMLP_WIDTH), BF16), jax.ShapeDtypeStruct((N_GROUPS, CHUNK, CHUNK), F32),
                   jax.ShapeDtypeStruct((8, GMLP_WIDTH), F32), jax.ShapeDtypeStruct((CHUNK, GMLP_WIDTH), F32)),
        grid=(n_i,),
        in_specs=[pl.BlockSpec((tm, 2 * GMLP_WIDTH), lambda i: (i, 0)), pl.BlockSpec((tm, GMLP_WIDTH), lambda i: (i, 1)),
                  pl.BlockSpec((1, GMLP_WIDTH), lambda i: (0, 0)),
                  pl.BlockSpec((N_GROUPS, CHUNK, CHUNK), lambda i: (0, 0, 0)),
                  pl.BlockSpec((N_GROUPS, CHUNK, CHUNK), lambda i: (0, 0, 0)),
                  pl.BlockSpec((CHUNK, GMLP_WIDTH), lambda i: (0, 0)),
                  pl.BlockSpec((GMLP_WIDTH, GMLP_WIDTH), lambda i: (0, 0))],
        out_specs=(pl.BlockSpec((tm, 2 * GMLP_WIDTH), lambda i: (i, 0)),
                   pl.BlockSpec((N_GROUPS, CHUNK, CHUNK), lambda i: (0, 0, 0)),
                   pl.BlockSpec((8, GMLP_WIDTH), lambda i: (0, 0)),
                   pl.BlockSpec((CHUNK, GMLP_WIDTH), lambda i: (0, 0))),
        scratch_shapes=[pltpu.VMEM((CHUNK, GMLP_WIDTH), F32)],
        name="gmlp_bwd", compiler_params=_cp(("arbitrary",), VMEM_LIMIT),
    )(ug, dsg, gain, w_s, wt_s, bias_full, ones)


def _gate_bwd(dcq, dck, zf):
    S = zf.shape[0]
    tm = _pick(S, (256,))
    n_i = S // tm
    triu = (lax.broadcasted_iota(jnp.int32, (tm, tm), 0) <= lax.broadcasted_iota(jnp.int32, (tm, tm), 1)).astype(BF16)

    def body(dcq_ref, dck_ref, zf_ref, tri_ref, dzf_ref, dbf_ref, carry_ref):
        i = pl.program_id(0)

        @pl.when(i == 0)
        def _():
            carry_ref[...] = jnp.zeros_like(carry_ref)
            dbf_ref[...] = jnp.zeros_like(dbf_ref)

        lane = lax.broadcasted_iota(jnp.int32, (tm, LANES), 1)
        dc = jnp.zeros((tm, LANES), F32)
        for p in range(N_PAIRS):
            slab = dcq_ref[p] + dck_ref[p]
            for hh in range(2):
                dc = dc + jnp.where(lane == 2 * p + hh, slab[:, hh:hh + 1], 0.0)
        dlf = _dot3l(tri_ref[...], dc) + carry_ref[0:1, :]
        carry_ref[0:1, :] = dlf[0:1, :]
        dz = jnp.where(lane < N_HEADS, dlf * _sigmoid(-zf_ref[...]), 0.0)
        dzf_ref[...] = dz.astype(BF16)
        dbf_ref[0:1, :] += jnp.sum(dz, axis=0, keepdims=True)

    return pl.pallas_call(
        body,
        out_shape=(jax.ShapeDtypeStruct((S, LANES), BF16), jax.ShapeDtypeStruct((8, LANES), F32)),
        grid=(n_i,),
        in_specs=[pl.BlockSpec((N_PAIRS, tm, LANES), lambda i: (0, n_i - 1 - i, 0)),
                  pl.BlockSpec((N_PAIRS, tm, LANES), lambda i: (0, n_i - 1 - i, 0)),
                  pl.BlockSpec((tm, LANES), lambda i: (n_i - 1 - i, 0)),
                  pl.BlockSpec((tm, tm), lambda i: (0, 0))],
        out_specs=(pl.BlockSpec((tm, LANES), lambda i: (n_i - 1 - i, 0)), pl.BlockSpec((8, LANES), lambda i: (0, 0))),
        scratch_shapes=[pltpu.VMEM((8, LANES), F32)],
        name="gate_bwd", compiler_params=_cp(("arbitrary",), VMEM_LIMIT),
    )(dcq, dck, zf, triu)


def _out_proj_fwd(att_bf, sg, w_out_bf, x, g_ffn):
    S = x.shape[0]
    tm = _pick(S, (512, 256))

    def body(a_ref, s_ref, w_ref, x_ref, g_ref, h_ref, hn_ref):
        h = (x_ref[...] + jnp.dot(a_ref[...], w_ref[:ATT_WIDTH, :], preferred_element_type=F32)
             + jnp.dot(s_ref[...], w_ref[ATT_WIDTH:, :], preferred_element_type=F32))
        h_ref[...] = h
        r = lax.rsqrt(jnp.mean(h * h, axis=-1, keepdims=True) + EPS)
        hn_ref[...] = ((h * r) * g_ref[...]).astype(BF16)

    row = pl.BlockSpec((tm, D_MODEL), lambda i: (i, 0))
    half = pl.BlockSpec((tm, ATT_WIDTH), lambda i: (i, 0))
    return pl.pallas_call(
        body, out_shape=(jax.ShapeDtypeStruct((S, D_MODEL), F32), jax.ShapeDtypeStruct((S, D_MODEL), BF16)),
        grid=(S // tm,),
        in_specs=[half, half, pl.BlockSpec((D_MODEL, D_MODEL), lambda i: (0, 0)), row,
                  pl.BlockSpec((1, D_MODEL), lambda i: (0, 0))],
        out_specs=(row, row), name="out_proj", compiler_params=_cp(("parallel",), VMEM_LIMIT),
    )(att_bf, sg, w_out_bf, x, g_ffn)


def _out_proj_dw(att_bf, sg, dh1_bf):
    S = att_bf.shape[0]
    tk = _pick(S, (1024, 512))

    def body(a_ref, s_ref, d_ref, o_ref):
        k = pl.program_id(0)

        @pl.when(k == 0)
        def _():
            o_ref[...] = jnp.zeros_like(o_ref)

        d = d_ref[...]
        o_ref[:ATT_WIDTH, :] += lax.dot_general(a_ref[...], d, _TN, preferred_element_type=F32)
        o_ref[ATT_WIDTH:, :] += lax.dot_general(s_ref[...], d, _TN, preferred_element_type=F32)

    half = pl.BlockSpec((tk, ATT_WIDTH), lambda k: (k, 0))
    return pl.pallas_call(
        body, out_shape=jax.ShapeDtypeStruct((D_MODEL, D_MODEL), F32), grid=(S // tk,),
        in_specs=[half, half, pl.BlockSpec((tk, D_MODEL), lambda k: (k, 0))],
        out_specs=pl.BlockSpec((D_MODEL, D_MODEL), lambda k: (0, 0)),
        name="out_proj_dw", compiler_params=_cp(("arbitrary",), VMEM_LIMIT),
    )(att_bf, sg, dh1_bf)


_IN_PIECES = ((0, ATT_WIDTH), (ATT_WIDTH, ATT_WIDTH), (2 * ATT_WIDTH, ATT_WIDTH), (QKV, 2 * GMLP_WIDTH), (UG_END, LANES))


def _inproj_bwd_dx(pieces, w_pad, x, g_mix, dh1):
    S = x.shape[0]
    tm = _pick(S, (512, 256))

    def body(*refs):
        p_refs, (w_ref, x_ref, g_ref, r_ref, dx_ref, dg_ref) = refs[:5], refs[5:]
        i = pl.program_id(0)

        @pl.when(i == 0)
        def _():
            dg_ref[...] = jnp.zeros_like(dg_ref)

        dxn = None
        for p_ref, (c0, width) in zip(p_refs, _IN_PIECES):
            part = jnp.dot(p_ref[...], w_ref[c0:c0 + width, :], preferred_element_type=F32)
            dxn = part if dxn is None else dxn + part
        xf = x_ref[...]
        r = lax.rsqrt(jnp.mean(xf * xf, axis=-1, keepdims=True) + EPS)
        xhat = xf * r
        dg_ref[0:1, :] += jnp.sum(dxn * xhat, axis=0, keepdims=True)
        dhat = dxn * g_ref[...]
        dx_ref[...] = r_ref[...] + r * (dhat - xhat * jnp.mean(dhat * xhat, axis=-1, keepdims=True))

    row = pl.BlockSpec((tm, D_MODEL), lambda i: (i, 0))
    return pl.pallas_call(
        body, out_shape=(jax.ShapeDtypeStruct((S, D_MODEL), F32), jax.ShapeDtypeStruct((8, D_MODEL), F32)),
        grid=(S // tm,),
        in_specs=[pl.BlockSpec((tm, width), lambda i: (i, 0)) for _, width in _IN_PIECES]
        + [pl.BlockSpec((IN_PAD, D_MODEL), lambda i: (0, 0)), row, pl.BlockSpec((1, D_MODEL), lambda i: (0, 0)), row],
        out_specs=(row, pl.BlockSpec((8, D_MODEL), lambda i: (0, 0))),
        name="in_proj_dx", compiler_params=_cp(("arbitrary",), VMEM_LIMIT),
    )(*pieces, w_pad, x, g_mix, dh1)


def _inproj_bwd_dw(xn, pieces):
    S = xn.shape[0]
    tk = _pick(S, (512, 256))

    def body(*refs):
        x_ref, p_refs, o_ref = refs[0], refs[1:6], refs[6]
        k = pl.program_id(0)

        @pl.when(k == 0)
        def _():
            o_ref[...] = jnp.zeros_like(o_ref)

        xb = x_ref[...]
        for p_ref, (c0, width) in zip(p_refs, _IN_PIECES):
            o_ref[:, c0:c0 + width] += lax.dot_general(xb, p_ref[...], _TN, preferred_element_type=F32)

    return pl.pallas_call(
        body, out_shape=jax.ShapeDtypeStruct((D_MODEL, IN_PAD), F32), grid=(S // tk,),
        in_specs=[pl.BlockSpec((tk, D_MODEL), lambda k: (k, 0))]
        + [pl.BlockSpec((tk, width), lambda k: (k, 0)) for _, width in _IN_PIECES],
        out_specs=pl.BlockSpec((D_MODEL, IN_PAD), lambda k: (0, 0)),
        name="in_proj_dw", compiler_params=_cp(("arbitrary",), VMEM_LIMIT),
    )(xn, *pieces)


def _adamw(w, m, v, parts, name):
    R, C = w.shape[-2:]
    tr = R
    for cand in (256, 128, 64, 32, 16, 8):
        if R % cand == 0 and R > cand:
            tr = cand
            break
    c1 = 1.0 / (1.0 - ADAM_B1 ** ADAM_STEP)
    c2 = 1.0 / (1.0 - ADAM_B2 ** ADAM_STEP)

    def body(w_ref, m_ref, v_ref, p_ref, g_ref, d_ref, nm_ref, nv_ref):
        g = p_ref[0].astype(F32)
        for j in range(1, N_DEV):
            g = g + p_ref[j].astype(F32)
        g_ref[...] = g
        nm = ADAM_B1 * m_ref[...] + (1.0 - ADAM_B1) * g
        nv = ADAM_B2 * v_ref[...] + (1.0 - ADAM_B2) * (g * g)
        nm_ref[...] = nm
        nv_ref[...] = nv
        d_ref[...] = -ADAM_LR * ((nm * c1) / (jnp.sqrt(nv * c2) + ADAM_EPS) + ADAM_WD * w_ref[...])

    if w.ndim == 3:
        spec = pl.BlockSpec((None, tr, C), lambda i: (0, i, 0))
    else:
        spec = pl.BlockSpec((tr, C), lambda i: (i, 0))
    shp = jax.ShapeDtypeStruct(w.shape, F32)
    return pl.pallas_call(
        body, out_shape=(shp, shp, shp, shp), grid=(R // tr,),
        in_specs=[spec, spec, spec, pl.BlockSpec((N_DEV, tr, C), lambda i: (0, i, 0))],
        out_specs=(spec, spec, spec, spec),
        name=name, compiler_params=_cp(("parallel",), VMEM_LIMIT),
    )(w, m, v, parts)


def _place():
    x, y, c = lax.axis_index("x"), lax.axis_index("y"), lax.axis_index("c")
    return x, y, c


def _all_gather(blocks, name):
    n = len(blocks)

    def body(*refs):
        ins, outs = refs[:n], refs[n:2 * n]
        send_sems, recv_sems, local_sems = refs[2 * n:]
        x, y, c = _place()
        me, sibling = (x, y, c), (x, y, 1 - c)
        chips = [(1 - x, y), (x, 1 - y), (1 - x, 1 - y)]
        sends = []
        for a in range(n):
            out = outs[a]

            def slot(px, py, pc, out=out):
                return out.at[4 * px + 2 * py + pc]

            def copy(k, block, to, src=None, a=a, slot=slot):
                return pltpu.make_async_remote_copy(
                    src_ref=slot(*block) if src is None else src, dst_ref=slot(*block),
                    send_sem=send_sems.at[a, k], recv_sem=recv_sems.at[a, k], device_id=to, device_id_type=MESH)

            mine = pltpu.make_async_copy(ins[a], slot(*me), local_sems.at[a])
            mine.start()
            first = [copy(0, me, sibling, src=ins[a])]
            first += [copy(1 + j, me, (*chip, c), src=ins[a]) for j, chip in enumerate(chips)]
            for cp in first:
                cp.start()
            sends.append((mine, first, copy))
        for a in range(n):
            mine, first, copy = sends[a]
            passed = [copy(4 + j, (*chip, c), sibling) for j, chip in enumerate(chips)]
            for j, chip in enumerate(chips):
                copy(1 + j, (*chip, c), me).wait_recv()
                passed[j].start()
            copy(0, sibling, me).wait_recv()
            for j, chip in enumerate(chips):
                copy(4 + j, (*chip, 1 - c), me).wait_recv()
            for cp in first + passed:
                cp.wait_send()
            mine.wait()

    any_spec = pl.BlockSpec(memory_space=pl.ANY)
    return pl.pallas_call(
        body, out_shape=tuple(jax.ShapeDtypeStruct((N_DEV,) + b.shape, b.dtype) for b in blocks),
        in_specs=[any_spec] * n, out_specs=tuple([any_spec] * n),
        scratch_shapes=[pltpu.SemaphoreType.DMA((n, 7)), pltpu.SemaphoreType.DMA((n, 7)), pltpu.SemaphoreType.DMA((n,))],
        name=name,
    )(*blocks)


_HBM = pl.BlockSpec(memory_space=pltpu.HBM)
_SEM = pl.BlockSpec(memory_space=pltpu.SEMAPHORE)
_EFFECT = pltpu.SideEffectType.DATAFLOW_SIDE_EFFECTING


def _peers(x, y, c):
    out = []
    for k in range(1, N_DEV):
        px, py, pc = x ^ ((k >> 2) & 1), y ^ ((k >> 1) & 1), c ^ (k & 1)
        out.append((k, (px, py, pc), 4 * px + 2 * py + pc))
    return out


def _xchg_copies(src_refs, land_refs, send_sems, recv_sems, scatter):
    x, y, c = _place()
    me = 4 * x + 2 * y + c
    copies = []
    for a, (src, land) in enumerate(zip(src_refs, land_refs)):
        for k, place, idx in _peers(x, y, c):
            j = a * (N_DEV - 1) + k - 1
            copies.append(pltpu.make_async_remote_copy(
                src_ref=src.at[idx] if scatter[a] else src, dst_ref=land.at[me],
                send_sem=send_sems[j], recv_sem=recv_sems[j], device_id=place, device_id_type=MESH))
    return copies


def _xchg_start(srcs, scatter, name):
    n = len(srcs)
    lands = [lax.empty((N_DEV,) + (s.shape[1:] if sc else s.shape), s.dtype) for s, sc in zip(srcs, scatter)]

    ns = n * (N_DEV - 1)

    def body(*refs):
        sems = refs[2 * n:2 * n + 2 * ns]
        for cp in _xchg_copies(refs[:n], refs[n:2 * n], sems[:ns], sems[ns:], scatter):
            cp.start()
        token = refs[-1]
        token[...] = jnp.zeros_like(token)

    both = list(srcs) + lands
    res = pl.pallas_call(
        body, name=name,
        out_shape=(*[pltpu.SemaphoreType.DMA(())] * (2 * ns),
                   *[pltpu.HBM(a.shape, a.dtype) for a in both], jax.ShapeDtypeStruct((8, LANES), F32)),
        in_specs=[_HBM] * (2 * n),
        out_specs=(*([_SEM] * (2 * ns)), *([_HBM] * (2 * n)), pl.BlockSpec(memory_space=pltpu.VMEM)),
        input_output_aliases={i: 2 * ns + i for i in range(2 * n)},
        compiler_params=pltpu.CompilerParams(has_side_effects=_EFFECT),
    )(*[pltpu.with_memory_space_constraint(a, pltpu.HBM) for a in both])
    return (tuple(res[:2 * ns]), tuple(res[2 * ns:2 * ns + 2 * n])), res[-1]


def _xchg_wait(handle, scatter, after, name):
    sems, thru = handle
    n = len(thru) // 2
    ns = len(sems) // 2

    def body(*refs):
        got = refs[2 * n:2 * n + 2 * ns]
        for cp in _xchg_copies(refs[:n], refs[n:2 * n], got[:ns], got[ns:], scatter):
            cp.wait_send()
            cp.wait_recv()

    outs = pl.pallas_call(
        body, name=name, out_shape=tuple(pltpu.HBM(a.shape, a.dtype) for a in thru),
        in_specs=[_HBM] * (2 * n) + [_SEM] * (2 * ns) + [pl.BlockSpec(memory_space=pl.ANY)],
        out_specs=tuple([_HBM] * (2 * n)), input_output_aliases={i: i for i in range(2 * n)},
        compiler_params=pltpu.CompilerParams(has_side_effects=_EFFECT),
    )(*thru, *sems, after)
    return outs[:n], outs[n:]


def _tie(a, token):
    return a if token is None else a + token[0, 0].astype(a.dtype)


def _rows128(a):
    flat = a.reshape(-1)
    rows = -(-flat.shape[0] // LANES)
    rows = -(-rows // 8) * 8
    return jnp.pad(flat, (0, rows * LANES - flat.shape[0])).reshape(rows, LANES)


def _local_step(x, target, norm_mix_g, w_in_t, b_forget, gmlp_norm_g, w_spatial, b_spatial, norm_ffn_g, conv_b,
                norm_final_g, rest_fn, send_fn, small_fn, token=None):
    f = D_FF
    g_mix = norm_mix_g.reshape(1, D_MODEL)
    w_pad = jnp.pad(w_in_t, ((0, IN_PAD - IN_COLS), (0, 0)))
    bf_pad = jnp.pad(b_forget.reshape(1, N_HEADS), ((0, 0), (0, LANES - N_HEADS)))
    xn, qa, ka, va, ug, zf = _inproj_fwd(x, _tie(g_mix, token), w_pad, bf_pad)
    bias_full = jnp.repeat(b_spatial.reshape(N_GROUPS, CHUNK).T, GROUP_DIM, axis=1)
    w_s = w_spatial.reshape(N_GROUPS, CHUNK, CHUNK)
    gain = gmlp_norm_g.reshape(1, GMLP_WIDTH)
    sg = _gmlp_fwd(ug, gain, w_s, bias_full)
    att, lse, att_bf = _attn_fwd(qa, ka, va)
    w_out_bf, w_up_bf, conv_w, w_down_bf = rest_fn(att_bf)
    g_ffn = norm_ffn_g.reshape(1, D_MODEL)
    h1, hn = _out_proj_fwd(att_bf, sg, w_out_bf, x, g_ffn)
    cw = jnp.pad(conv_w.reshape(3, 2, f).transpose(1, 0, 2), ((0, 0), (0, 5), (0, 0)))
    cb = conv_b.reshape(2, 1, f)
    hu, hc, act = _ffn_up_conv(hn, w_up_bf, cw, cb)
    loss_blk, dh2, dh2_bf, dg_final = _ffn_down_loss(act, w_down_bf, h1, norm_final_g.reshape(1, D_MODEL), target)
    dw_down = _mm(act, dh2_bf, mode="tn", out_dtype=F32, tm=1408, tn=1024, tk=2048, name="ffn_down_dw")
    dact = _mm(dh2_bf, w_down_bf, mode="nt", out_dtype=F32, tm=512, tn=1408, tk=1024, outer="j", name="ffn_down_dx")
    dhu, dcw = _conv_gate_bwd(hc, hu, dact, _tie(cw, send_fn("w_down", dw_down)))
    dw_up = _mm(hn, dhu, mode="tn", out_dtype=F32, tm=1024, tn=1408, tk=2048, b_halves=True, outer="j", name="ffn_up_dw")
    dh1, dh1_bf, dg_ffn = _ffn_up_dx_rms(dhu, w_up_bf, h1, _tie(g_ffn, send_fn("w_up", dw_up)), dh2)
    dmix = _mm(dh1_bf, w_out_bf, mode="nt", out_dtype=F32, tm=512, tn=1024, tk=1024, name="out_proj_dx")
    dw_out = _out_proj_dw(att_bf, sg, dh1_bf)
    qb, doa = _attn_prep(att, lse, dmix, qa)
    dq, dk, dv, dcq, dck = _attn_bwd(qb, ka, va, doa)
    wt_s = w_s.transpose(0, 2, 1)
    dug, dw_s, dgain, dbias = _gmlp_bwd(ug, dmix, _tie(gain, send_fn("w_out", dw_out)), w_s, wt_s, bias_full)
    dzf, dbf = _gate_bwd(dcq, dck, zf)
    grad_x, dg_mix = _inproj_bwd_dx((dq, dk, dv, dug, dzf), w_pad, x, g_mix, dh1)
    grads = dict(
        norm_mix_g=dg_mix[0:1, :],
        b_forget=dbf[0:1, :N_HEADS],
        gmlp_norm_g=dgain[0:1, :],
        w_spatial=dw_s,
        b_spatial=dbias[:, ::GROUP_DIM].T,
        norm_ffn_g=dg_ffn[0:1, :],
        conv_w=dcw[:, 0:3, :].transpose(1, 0, 2).reshape(3, 2 * f),
        conv_b=dcw[:, 3, :].reshape(1, 2 * f),
        norm_final_g=dg_final[0, :],
    )
    token = small_fn(loss_blk[0, 0], grads)
    dw_in = _inproj_bwd_dw(xn, (dq, dk, dv, dug, _tie(dzf, token)))
    return grad_x, send_fn("w_in", dw_in[:, :IN_COLS])


SMALL = ("norm_mix_g", "b_forget", "gmlp_norm_g", "w_spatial", "b_spatial", "norm_ffn_g", "conv_b", "norm_final_g")


def kernel(x, norm_mix_g, w_in, b_forget, gmlp_norm_g, w_spatial, b_spatial, w_out, norm_ffn_g, w_up, conv_w, conv_b, w_down, norm_final_g, loss_target, m_norm_mix_g, m_w_in, m_b_forget, m_gmlp_norm_g, m_w_spatial, m_b_spatial, m_w_out, m_norm_ffn_g, m_w_up, m_conv_w, m_conv_b, m_w_down, m_norm_final_g, v_norm_mix_g, v_w_in, v_b_forget, v_gmlp_norm_g, v_w_spatial, v_b_spatial, v_w_out, v_norm_ffn_g, v_w_up, v_conv_w, v_conv_b, v_w_down, v_norm_final_g):
    weights = dict(norm_mix_g=norm_mix_g, w_in=w_in, b_forget=b_forget, gmlp_norm_g=gmlp_norm_g, w_spatial=w_spatial,
                   b_spatial=b_spatial, w_out=w_out, norm_ffn_g=norm_ffn_g, w_up=w_up, conv_w=conv_w, conv_b=conv_b,
                   w_down=w_down, norm_final_g=norm_final_g)
    m_in = dict(norm_mix_g=m_norm_mix_g, w_in=m_w_in, b_forget=m_b_forget, gmlp_norm_g=m_gmlp_norm_g,
                w_spatial=m_w_spatial, b_spatial=m_b_spatial, w_out=m_w_out, norm_ffn_g=m_norm_ffn_g, w_up=m_w_up,
                conv_w=m_conv_w, conv_b=m_conv_b, w_down=m_w_down, norm_final_g=m_norm_final_g)
    v_in = dict(norm_mix_g=v_norm_mix_g, w_in=v_w_in, b_forget=v_b_forget, gmlp_norm_g=v_gmlp_norm_g,
                w_spatial=v_w_spatial, b_spatial=v_b_spatial, w_out=v_w_out, norm_ffn_g=v_norm_ffn_g, w_up=v_w_up,
                conv_w=v_conv_w, conv_b=v_conv_b, w_down=v_w_down, norm_final_g=v_norm_final_g)
    order = list(weights)
    me = 4 * lax.axis_index("x") + 2 * lax.axis_index("y") + lax.axis_index("c")
    n_in, n_up = w_in.shape[2], w_up.shape[2]
    r_out, r_down = w_out.shape[1], w_down.shape[1]

    def with_mine(landed, mine):
        return lax.dynamic_update_index_in_dim(landed, mine, me, 0)

    up_blk = w_up[0].T.astype(BF16)
    out_blk = w_out[0].astype(BF16)
    down_blk = w_down[0].astype(BF16)
    taps_blk = jnp.pad(conv_w[0], ((0, 5), (0, 0)))
    (in_all,) = _all_gather([w_in[0].T.astype(BF16)], "gather_w_in")
    in_all, rest_blocks = lax.optimization_barrier((in_all, [up_blk, out_blk, down_blk, taps_blk]))
    rest_handle, token = _xchg_start(rest_blocks, [False] * 4, "gather_rest_start")
    w_in_t = in_all.reshape(N_DEV * n_in, D_MODEL)

    def rest_fn(after):
        mine, landed = _xchg_wait(rest_handle, [False] * 4, after, "gather_rest_wait")
        up_all, out_all, down_all, taps_all = [with_mine(l, b) for l, b in zip(landed, mine)]
        return (out_all.reshape(N_DEV * r_out, D_MODEL), up_all.reshape(N_DEV * n_up, D_MODEL),
                taps_all[:, :3, :].transpose(1, 0, 2).reshape(3, N_DEV * n_up),
                down_all.reshape(N_DEV * r_down, D_MODEL))

    sent = {}

    def send_fn(name, grad):
        if name == "w_in":
            parts = grad.reshape(D_MODEL, N_DEV, -1).transpose(1, 0, 2).astype(BF16)
        elif name == "w_up":
            parts = grad.reshape(D_MODEL, N_DEV, -1).transpose(1, 0, 2)
        else:
            parts = grad.reshape(N_DEV, -1, D_MODEL)
        sent[name], tok = _xchg_start([parts], [True], "scatter_" + name + "_start")
        return tok

    small = {}

    def small_fn(loss_local, g):
        loss_rows = jnp.pad(loss_local.reshape(1, 1), ((0, 31), (0, LANES - 1)))
        packed = [_rows128(g[k]) for k in SMALL] + [loss_rows, _rows128(g["conv_w"])]
        small["sizes"] = [p.shape[0] for p in packed]
        small["handle"], tok = _xchg_start([jnp.concatenate(packed, axis=0)], [False], "gather_small_start")
        return tok

    grad_x, after = _local_step(
        x[0], loss_target[0], norm_mix_g, w_in_t, b_forget, gmlp_norm_g, w_spatial, b_spatial, norm_ffn_g, conv_b,
        norm_final_g, rest_fn, send_fn, small_fn, token)

    outs = {}

    def update_big(name, after):
        (parts,), (landed,) = _xchg_wait(sent[name], [True], after, "scatter_" + name + "_wait")
        got = with_mine(landed, lax.dynamic_index_in_dim(parts, me, 0, keepdims=False))
        outs[name] = tuple(_adamw(weights[name], m_in[name], v_in[name], got, "adamw_" + name))
        return outs[name][0]

    for name in ("w_down", "w_up", "w_out"):
        after = update_big(name, after)

    (mine,), (landed,) = _xchg_wait(small["handle"], [False], after, "gather_small_wait")
    small_all = with_mine(landed, mine)
    sizes = small["sizes"]
    n_small_rows = sum(sizes[:-2])

    def pack(src):
        return jnp.concatenate([_rows128(src[k]) for k in SMALL] + [jnp.zeros((sizes[-2], LANES), F32)], axis=0)

    n_adam_rows = n_small_rows + sizes[-2]
    sg_, sd_, sm_, sv_ = _adamw(pack(weights), pack(m_in), pack(v_in), small_all[:, :n_adam_rows, :], "adamw_small")
    loss = sg_[n_small_rows, 0]
    off = 0
    for k, rows in zip(SMALL, sizes[:-2]):
        shp = weights[k].shape
        cnt = math.prod(shp)
        outs[k] = tuple(a[off:off + rows].reshape(-1)[:cnt].reshape(shp) for a in (sg_, sd_, sm_, sv_))
        off += rows
    taps_parts = small_all[:, n_adam_rows:, :].reshape(N_DEV, -1)[:, :3 * N_DEV * n_up].reshape(N_DEV, 3, N_DEV * n_up)
    taps_mine = lax.dynamic_slice_in_dim(taps_parts, me * n_up, n_up, axis=2)
    taps_mine = jnp.pad(taps_mine, ((0, 0), (0, 5), (0, 0)))

    def pad8(a):
        return jnp.pad(a[0], ((0, 5), (0, 0)))

    res = _adamw(pad8(conv_w), pad8(m_conv_w), pad8(v_conv_w), taps_mine, "adamw_conv_w")
    outs["conv_w"] = tuple(a[:3][None] for a in res)
    update_big("w_in", sg_)

    return (loss, grad_x[None], *[outs[k][0] for k in order], *[outs[k][1] for k in order],
            *[outs[k][2] for k in order], *[outs[k][3] for k in order])
```

```python
import functools
import math

import jax
import jax.numpy as jnp
from jax import lax
from jax.experimental import pallas as pl
from jax.experimental.pallas import tpu as pltpu

F32 = jnp.float32
BF16 = jnp.bfloat16

N_DEV = 8
D_MODEL = 1024
ATT_WIDTH = 512
GMLP_WIDTH = 512
HEAD_DIM = 64
N_HEADS = 8
N_PAIRS = 4
N_GROUPS = 8
GROUP_DIM = 64
CHUNK = 128
D_FF = 2816
IN_COLS = 2568
IN_PAD = 2688
QKV = 1536
UG_END = 2560
EPS = 1e-6
LANES = 128

ADAM_LR = 0.001
ADAM_B1 = 0.9
ADAM_B2 = 0.999
ADAM_EPS = 1e-08
ADAM_WD = 0.01
ADAM_STEP = 10

ATT_TQ = 1024
ATT_TK = 1024
FFN_TM, FFN_TN = 512, 1408
CONV_TM, CONV_TN = 256, 1408
VMEM_LIMIT = 56 * 1024 * 1024
MESH = pl.DeviceIdType.MESH


def _cp(sem, vmem=None):
    return pltpu.CompilerParams(dimension_semantics=sem, vmem_limit_bytes=vmem)


def _pick(n, prefs):
    for p in prefs:
        if n % p == 0:
            return p
    return n


def _split3(x):
    hi = x.astype(BF16)
    r1 = x - hi.astype(F32)
    mid = r1.astype(BF16)
    lo = (r1 - mid.astype(F32)).astype(BF16)
    return hi, mid, lo


def _dot3(x, ones_bf):
    hi, mid, lo = _split3(x)
    d = functools.partial(jnp.dot, preferred_element_type=F32)
    return d(hi, ones_bf) + d(mid, ones_bf) + d(lo, ones_bf)


def _dot3l(ones_bf, x):
    hi, mid, lo = _split3(x)
    d = functools.partial(jnp.dot, preferred_element_type=F32)
    return d(ones_bf, hi) + d(ones_bf, mid) + d(ones_bf, lo)


def _gelu(x):
    k = math.sqrt(2.0 / math.pi)
    t = jnp.tanh(k * (x + 0.044715 * (x * x * x)))
    return 0.5 * x * (1.0 + t)


def _gelu_grad(x):
    k = math.sqrt(2.0 / math.pi)
    x2 = x * x
    t = jnp.tanh(k * (x + 0.044715 * (x2 * x)))
    return 0.5 * (1.0 + t) + 0.5 * x * (1.0 - t * t) * (k * (1.0 + 3.0 * 0.044715 * x2))


def _sigmoid(x):
    return 1.0 / (1.0 + jnp.exp(-x))


def _mm(a, b, *, mode, out_dtype, tm, tn, tk, name, res=None, a_halves=False, b_halves=False,
        out_halves=False, outer="i"):
    if mode == "tn":
        K, M = a.shape[-2], a.shape[-1] * (2 if a_halves else 1)
    else:
        M, K = a.shape[-2], a.shape[-1] * (2 if a_halves else 1)
    if mode == "nt":
        N = b.shape[-2]
        assert b.shape[-1] == K
    else:
        N = b.shape[-1] * (2 if b_halves else 1)
    tm, tn, tk = min(tm, M), min(tn, N), min(tk, K)
    assert M % tm == 0 and N % tn == 0 and K % tk == 0, (name, M, N, K, tm, tn, tk)
    nm, nn, nk = M // tm, N // tn, K // tk

    def ij(g0, g1):
        return (g0, g1) if outer == "i" else (g1, g0)

    if mode == "nn":
        dims = (((1,), (0,)), ((), ()))
        if a_halves:
            nkh = nk // 2
            a_spec = pl.BlockSpec((None, tm, tk), lambda g0, g1, k: (k // nkh, ij(g0, g1)[0], k % nkh))
        else:
            a_spec = pl.BlockSpec((tm, tk), lambda g0, g1, k: (ij(g0, g1)[0], k))
        b_spec = pl.BlockSpec((tk, tn), lambda g0, g1, k: (k, ij(g0, g1)[1]))
    elif mode == "nt":
        dims = (((1,), (1,)), ((), ()))
        if a_halves:
            nkh = nk // 2
            a_spec = pl.BlockSpec((None, tm, tk), lambda g0, g1, k: (k // nkh, ij(g0, g1)[0], k % nkh))
        else:
            a_spec = pl.BlockSpec((tm, tk), lambda g0, g1, k: (ij(g0, g1)[0], k))
        b_spec = pl.BlockSpec((tn, tk), lambda g0, g1, k: (ij(g0, g1)[1], k))
    else:
        dims = (((0,), (0,)), ((), ()))
        if a_halves:
            nmh = nm // 2
            a_spec = pl.BlockSpec((None, tk, tm), lambda g0, g1, k: (ij(g0, g1)[0] // nmh, k, ij(g0, g1)[0] % nmh))
        else:
            a_spec = pl.BlockSpec((tk, tm), lambda g0, g1, k: (k, ij(g0, g1)[0]))
        if b_halves:
            nnh = nn // 2
            b_spec = pl.BlockSpec((None, tk, tn), lambda g0, g1, k: (ij(g0, g1)[1] // nnh, k, ij(g0, g1)[1] % nnh))
        else:
            b_spec = pl.BlockSpec((tk, tn), lambda g0, g1, k: (k, ij(g0, g1)[1]))
    if out_halves:
        nnh = nn // 2
        o_spec = pl.BlockSpec((None, tm, tn), lambda g0, g1, k: (ij(g0, g1)[1] // nnh, ij(g0, g1)[0], ij(g0, g1)[1] % nnh))
        o_shape = jax.ShapeDtypeStruct((2, M, N // 2), out_dtype)
    else:
        o_spec = pl.BlockSpec((tm, tn), lambda g0, g1, k: ij(g0, g1))
        o_shape = jax.ShapeDtypeStruct((M, N), out_dtype)
    in_specs = [a_spec, b_spec]
    args = [a, b]
    if res is not None:
        in_specs.append(pl.BlockSpec((tm, tn), lambda g0, g1, k: ij(g0, g1)))
        args.append(res)

    def body(*refs):
        if res is not None:
            a_ref, b_ref, r_ref, o_ref = refs[:4]
        else:
            a_ref, b_ref, o_ref = refs[:3]
            r_ref = None
        part = lax.dot_general(a_ref[...], b_ref[...], dims, preferred_element_type=F32)
        if nk == 1:
            if r_ref is not None:
                part = part + r_ref[...]
            o_ref[...] = part.astype(out_dtype)
            return
        acc_ref = refs[-1]
        k = pl.program_id(2)

        @pl.when(k == 0)
        def _():
            acc_ref[...] = part

        @pl.when(k > 0)
        def _():
            acc_ref[...] += part

        @pl.when(k == nk - 1)
        def _():
            tot = acc_ref[...]
            if r_ref is not None:
                tot = tot + r_ref[...]
            o_ref[...] = tot.astype(out_dtype)

    grid = (nm, nn, nk) if outer == "i" else (nn, nm, nk)
    scratch = [] if nk == 1 else [pltpu.VMEM((tm, tn), F32)]
    return pl.pallas_call(
        body, out_shape=o_shape, grid=grid, in_specs=in_specs, out_specs=o_spec, scratch_shapes=scratch,
        name=name, compiler_params=_cp(("parallel", "parallel", "arbitrary"), VMEM_LIMIT),
    )(*args)


def _aug(lane, terms):
    out = 0.0
    for j, t in enumerate(terms):
        out = jnp.where(lane == HEAD_DIM + j, t, out)
    return out


def _split3f(x):
    hi, mid, lo = _split3(x)
    return [hi.astype(F32), mid.astype(F32), lo.astype(F32)]


def _inproj_fwd(x, g_mix, w_pad, bf_pad):
    S = x.shape[0]
    tm = _pick(S, (512, 256))
    tri = (lax.broadcasted_iota(jnp.int32, (tm, tm), 0) >= lax.broadcasted_iota(jnp.int32, (tm, tm), 1)).astype(BF16)

    def body(x_ref, g_ref, w_ref, bf_ref, tri_ref, xn_ref, qa_ref, ka_ref, va_ref, ug_ref, zf_ref, carry_ref):
        i = pl.program_id(0)

        @pl.when(i == 0)
        def _():
            carry_ref[...] = jnp.zeros_like(carry_ref)

        xf = x_ref[...]
        r = lax.rsqrt(jnp.mean(xf * xf, axis=-1, keepdims=True) + EPS)
        xn = ((xf * r) * g_ref[...]).astype(BF16)
        xn_ref[...] = xn
        proj = lax.dot_general(xn, w_ref[...], _NT, preferred_element_type=F32)
        ug_ref[...] = proj[:, QKV:UG_END]
        zf = proj[:, UG_END:] + bf_ref[...]
        zf_ref[...] = zf
        lf = jnp.minimum(zf, 0.0) - jnp.log(1.0 + jnp.exp(-jnp.abs(zf)))
        c = _dot3l(tri_ref[...], lf) + carry_ref[0:1, :]
        carry_ref[0:1, :] = c[tm - 1:tm, :]
        c3 = _split3f(c)
        lane = lax.broadcasted_iota(jnp.int32, (tm, LANES), 1)
        ones3 = [1.0, 1.0, 1.0]
        for h in range(N_HEADS):
            p, odd = h // 2, h % 2
            ch = [t[:, h:h + 1] for t in c3]

            def head(base, scale=None, p=p, odd=odd):
                blk = proj[:, base + p * LANES:base + (p + 1) * LANES]
                if scale is not None:
                    blk = blk * scale
                return pltpu.roll(blk, HEAD_DIM, 1) if odd else blk

            cols = slice(h * LANES, (h + 1) * LANES)
            qa_ref[:, cols] = jnp.where(lane < HEAD_DIM, head(0, HEAD_DIM ** -0.5), _aug(lane, ch + ones3)).astype(BF16)
            ka_ref[:, cols] = jnp.where(lane < HEAD_DIM, head(ATT_WIDTH),
                                        _aug(lane, ones3 + [-t for t in ch] + ones3)).astype(BF16)
            va_ref[:, cols] = jnp.where(lane < HEAD_DIM, head(2 * ATT_WIDTH), _aug(lane, ones3)).astype(BF16)

    wide = N_HEADS * LANES
    return pl.pallas_call(
        body,
        out_shape=(jax.ShapeDtypeStruct((S, D_MODEL), BF16), jax.ShapeDtypeStruct((S, wide), BF16),
                   jax.ShapeDtypeStruct((S, wide), BF16), jax.ShapeDtypeStruct((S, wide), BF16),
                   jax.ShapeDtypeStruct((S, 2 * GMLP_WIDTH), F32), jax.ShapeDtypeStruct((S, LANES), F32)),
        grid=(S // tm,),
        in_specs=[pl.BlockSpec((tm, D_MODEL), lambda i: (i, 0)), pl.BlockSpec((1, D_MODEL), lambda i: (0, 0)),
                  pl.BlockSpec((IN_PAD, D_MODEL), lambda i: (0, 0)), pl.BlockSpec((1, LANES), lambda i: (0, 0)),
                  pl.BlockSpec((tm, tm), lambda i: (0, 0))],
        out_specs=(pl.BlockSpec((tm, D_MODEL), lambda i: (i, 0)), pl.BlockSpec((tm, wide), lambda i: (i, 0)),
                   pl.BlockSpec((tm, wide), lambda i: (i, 0)), pl.BlockSpec((tm, wide), lambda i: (i, 0)),
                   pl.BlockSpec((tm, 2 * GMLP_WIDTH), lambda i: (i, 0)), pl.BlockSpec((tm, LANES), lambda i: (i, 0))),
        scratch_shapes=[pltpu.VMEM((8, LANES), F32)],
        name="inproj_fwd", compiler_params=_cp(("arbitrary",), VMEM_LIMIT),
    )(x, g_mix, w_pad, bf_pad, tri)


def _group_ones():
    r = lax.broadcasted_iota(jnp.int32, (GMLP_WIDTH, GMLP_WIDTH), 0) // GROUP_DIM
    c = lax.broadcasted_iota(jnp.int32, (GMLP_WIDTH, GMLP_WIDTH), 1) // GROUP_DIM
    return (r == c).astype(BF16)


def _gmlp_mixed(vn_bf, w_ref, bias, n_chunks):
    lane = lax.broadcasted_iota(jnp.int32, (CHUNK, LANES), 1)
    row = lax.broadcasted_iota(jnp.int32, (CHUNK, CHUNK), 0)
    col = lax.broadcasted_iota(jnp.int32, (CHUNK, CHUNK), 1)
    ws = [jnp.where(row >= col, w_ref[g], 0.0).astype(BF16) for g in range(N_GROUPS)]
    rows = []
    for ci in range(n_chunks):
        cols = []
        for pp in range(N_GROUPS // 2):
            v = vn_bf[ci * CHUNK:(ci + 1) * CHUNK, pp * LANES:(pp + 1) * LANES]
            v_lo = jnp.where(lane < GROUP_DIM, v, jnp.zeros_like(v))
            v_hi = jnp.where(lane >= GROUP_DIM, v, jnp.zeros_like(v))
            m = (jnp.dot(ws[2 * pp], v_lo, preferred_element_type=F32)
                 + jnp.dot(ws[2 * pp + 1], v_hi, preferred_element_type=F32))
            cols.append(m + bias[:, pp * LANES:(pp + 1) * LANES])
        rows.append(jnp.concatenate(cols, axis=1))
    return jnp.concatenate(rows, axis=0)


def _gmlp_fwd(ug, gain, w_s, bias_full):
    S = ug.shape[0]
    tm = _pick(S, (512, 256, 128))
    ones = _group_ones()

    def body(ug_ref, gain_ref, w_ref, bias_ref, ones_ref, sg_ref):
        u = _gelu(ug_ref[:, :GMLP_WIDTH])
        vr = _gelu(ug_ref[:, GMLP_WIDTH:])
        ms = _dot3(vr * vr, ones_ref[...]) * (1.0 / GROUP_DIM)
        vn = ((vr * lax.rsqrt(ms + EPS)) * gain_ref[...]).astype(BF16)
        mixed = _gmlp_mixed(vn, w_ref, bias_ref[...], tm // CHUNK)
        sg_ref[...] = (u * mixed).astype(BF16)

    return pl.pallas_call(
        body, out_shape=jax.ShapeDtypeStruct((S, GMLP_WIDTH), BF16), grid=(S // tm,),
        in_specs=[pl.BlockSpec((tm, 2 * GMLP_WIDTH), lambda i: (i, 0)), pl.BlockSpec((1, GMLP_WIDTH), lambda i: (0, 0)),
                  pl.BlockSpec((N_GROUPS, CHUNK, CHUNK), lambda i: (0, 0, 0)),
                  pl.BlockSpec((CHUNK, GMLP_WIDTH), lambda i: (0, 0)),
                  pl.BlockSpec((GMLP_WIDTH, GMLP_WIDTH), lambda i: (0, 0))],
        out_specs=pl.BlockSpec((tm, GMLP_WIDTH), lambda i: (i, 0)),
        name="gmlp_fwd", compiler_params=_cp(("parallel",), VMEM_LIMIT),
    )(ug, gain, w_s, bias_full, ones)


_NT = (((1,), (1,)), ((), ()))
_TN = (((0,), (0,)), ((), ()))


def _attn_fwd(qa, ka, va):
    S = qa.shape[0]
    tq = _pick(S, (ATT_TQ, 256))
    tk = min(ATT_TK, tq)
    nq = S // tq
    per_q = tq // tk

    def body(q_ref, k_ref, v_ref, o_ref, lse_ref, ob_ref):
        qi = pl.program_id(1)
        lane = lax.broadcasted_iota(jnp.int32, (tq, LANES), 1)
        sub = tk
        rid = lax.broadcasted_iota(jnp.int32, (sub, sub), 0)
        cid = lax.broadcasted_iota(jnp.int32, (sub, sub), 1)
        qs = [q_ref[:, :LANES], q_ref[:, LANES:]]

        def update(q, ks, k_len, h, m, acc, masked):
            cols = slice(h * LANES, (h + 1) * LANES)
            s = lax.dot_general(q, k_ref[pl.ds(ks, k_len), cols], _NT, preferred_element_type=F32)
            if masked:
                s = jnp.where(rid >= cid, s, -jnp.inf)
            m_new = jnp.maximum(m, jnp.max(s, axis=-1, keepdims=True))
            p = jnp.exp(s - m_new).astype(BF16)
            acc = jnp.exp(m - m_new) * acc + jnp.dot(p, v_ref[pl.ds(ks, k_len), cols], preferred_element_type=F32)
            return m_new, acc

        def step(kb, carry):
            ks = pl.multiple_of(kb * tk, tk)
            return tuple(update(qs[h], ks, tk, h, *carry[h], False) for h in range(2))

        one = (jnp.full((tq, 1), -jnp.inf, F32), jnp.zeros((tq, LANES), F32))
        carry = lax.fori_loop(0, qi * per_q, step, (one, one))
        outs, lses = [], []
        for h in range(2):
            ms, accs = [], []
            for r in range(tq // sub):
                rows = slice(r * sub, (r + 1) * sub)
                m, acc = carry[h][0][rows], carry[h][1][rows]
                for c in range(r + 1):
                    ks = pl.multiple_of(qi * tq + c * sub, sub)
                    m, acc = update(qs[h][rows], ks, sub, h, m, acc, c == r)
                ms.append(m)
                accs.append(acc)
            m, acc = jnp.concatenate(ms, axis=0), jnp.concatenate(accs, axis=0)
            l = acc[:, HEAD_DIM:HEAD_DIM + 1]
            outs.append(acc / l)
            lses.append(m + jnp.log(l))
        o = jnp.where(lane < HEAD_DIM, outs[0], pltpu.roll(outs[1], HEAD_DIM, 1))
        o_ref[...] = o
        ob_ref[...] = o.astype(BF16)
        lse_ref[...] = jnp.where(lane < HEAD_DIM, lses[0], lses[1])

    return pl.pallas_call(
        body,
        out_shape=(jax.ShapeDtypeStruct((S, ATT_WIDTH), F32), jax.ShapeDtypeStruct((S, ATT_WIDTH), F32),
                   jax.ShapeDtypeStruct((S, ATT_WIDTH), BF16)),
        grid=(N_PAIRS, nq),
        in_specs=[pl.BlockSpec((tq, 2 * LANES), lambda p, i: (i, p)),
                  pl.BlockSpec((S, 2 * LANES), lambda p, i: (0, p)),
                  pl.BlockSpec((S, 2 * LANES), lambda p, i: (0, p))],
        out_specs=(pl.BlockSpec((tq, LANES), lambda p, i: (i, p)), pl.BlockSpec((tq, LANES), lambda p, i: (i, p)),
                   pl.BlockSpec((tq, LANES), lambda p, i: (i, p))),
        name="attn_fwd", compiler_params=_cp(("parallel", "parallel"), VMEM_LIMIT),
    )(qa, ka, va)


def _shift_rows(x, prev, n):
    rid = lax.broadcasted_iota(jnp.int32, x.shape, 0)
    y = pltpu.roll(x, n, 0)
    if n == 1:
        return jnp.where(rid == 0, prev[7:8, :], y)
    return jnp.where(rid == 0, prev[6:7, :], jnp.where(rid == 1, prev[7:8, :], y))


def _shift_rows_up(x, nxt, n):
    rows = x.shape[0]
    rid = lax.broadcasted_iota(jnp.int32, x.shape, 0)
    y = pltpu.roll(x, rows - n, 0)
    if n == 1:
        return jnp.where(rid == rows - 1, nxt[0:1, :], y)
    return jnp.where(rid == rows - 2, nxt[0:1, :], jnp.where(rid == rows - 1, nxt[1:2, :], y))


def _conv3(cur, prev, w, b):
    return (w[0:1, :] * _shift_rows(cur, prev, 2) + w[1:2, :] * _shift_rows(cur, prev, 1)
            + w[2:3, :] * cur + b)


def _ffn_up_conv(hn, w_up_bf, cw, cb):
    S = hn.shape[0]
    F = D_FF
    tm = _pick(S, (FFN_TM, 256))
    tn = _pick(F, (FFN_TN, 256, 128))
    nj = F // tn

    def body(hn_ref, wa_ref, wg_ref, cw_ref, cb_ref, hu_ref, hc_ref, act_ref, tail_ref):
        i = pl.program_id(1)

        @pl.when(i == 0)
        def _():
            tail_ref[...] = jnp.zeros_like(tail_ref)

        hn_v = hn_ref[...]
        halves = []
        for h, w_ref in enumerate((wa_ref, wg_ref)):
            hu = lax.dot_general(hn_v, w_ref[...], _NT, preferred_element_type=F32)
            hu_ref[h] = hu
            hc = _conv3(hu, tail_ref[h], cw_ref[h], cb_ref[h])
            hc_ref[h] = hc
            halves.append(hc)
            tail_ref[h] = hu[tm - 8:, :]
        a, g = halves
        act_ref[...] = (g * _sigmoid(g) * a).astype(BF16)

    both = pl.BlockSpec((2, tm, tn), lambda j, i: (0, i, j))
    return pl.pallas_call(
        body, out_shape=(jax.ShapeDtypeStruct((2, S, F), F32), jax.ShapeDtypeStruct((2, S, F), F32),
                         jax.ShapeDtypeStruct((S, F), BF16)),
        grid=(nj, S // tm),
        in_specs=[pl.BlockSpec((tm, D_MODEL), lambda j, i: (i, 0)),
                  pl.BlockSpec((tn, D_MODEL), lambda j, i: (j, 0)),
                  pl.BlockSpec((tn, D_MODEL), lambda j, i: (nj + j, 0)),
                  pl.BlockSpec((2, 8, tn), lambda j, i: (0, 0, j)),
                  pl.BlockSpec((2, 1, tn), lambda j, i: (0, 0, j))],
        out_specs=(both, both, pl.BlockSpec((tm, tn), lambda j, i: (i, j))),
        scratch_shapes=[pltpu.VMEM((2, 8, tn), F32)],
        name="ffn_up_conv", compiler_params=_cp(("parallel", "arbitrary"), VMEM_LIMIT),
    )(hn, w_up_bf, w_up_bf, cw, cb)


def _ffn_down_loss(act, w_down_bf, h1, g_final, target):
    S = h1.shape[0]
    tm = _pick(S, (512, 256))

    def body(a_ref, w_ref, h1_ref, g_ref, t_ref, loss_ref, dh_ref, dhb_ref, dg_ref):
        i = pl.program_id(0)

        @pl.when(i == 0)
        def _():
            loss_ref[...] = jnp.zeros_like(loss_ref)
            dg_ref[...] = jnp.zeros_like(dg_ref)

        hf = h1_ref[...] + jnp.dot(a_ref[...], w_ref[...], preferred_element_type=F32)
        g = g_ref[...]
        r = lax.rsqrt(jnp.mean(hf * hf, axis=-1, keepdims=True) + EPS)
        hhat = hf * r
        err = hhat * g - t_ref[...]
        loss_ref[...] += 0.5 * jnp.sum(jnp.mean(err * err, axis=-1, keepdims=True))
        dy = err * (1.0 / D_MODEL)
        dg_ref[0:1, :] += jnp.sum(dy * hhat, axis=0, keepdims=True)
        dhat = dy * g
        dh = r * (dhat - hhat * jnp.mean(dhat * hhat, axis=-1, keepdims=True))
        dh_ref[...] = dh
        dhb_ref[...] = dh.astype(BF16)

    row = pl.BlockSpec((tm, D_MODEL), lambda i: (i, 0))
    return pl.pallas_call(
        body,
        out_shape=(jax.ShapeDtypeStruct((8, LANES), F32), jax.ShapeDtypeStruct((S, D_MODEL), F32),
                   jax.ShapeDtypeStruct((S, D_MODEL), BF16), jax.ShapeDtypeStruct((8, D_MODEL), F32)),
        grid=(S // tm,),
        in_specs=[pl.BlockSpec((tm, D_FF), lambda i: (i, 0)), pl.BlockSpec((D_FF, D_MODEL), lambda i: (0, 0)), row,
                  pl.BlockSpec((1, D_MODEL), lambda i: (0, 0)), row],
        out_specs=(pl.BlockSpec((8, LANES), lambda i: (0, 0)), row, row, pl.BlockSpec((8, D_MODEL), lambda i: (0, 0))),
        name="ffn_down_loss", compiler_params=_cp(("arbitrary",), VMEM_LIMIT),
    )(act, w_down_bf, h1, g_final, target)


def _ffn_up_dx_rms(dhu, w_up_bf, h1, g_ffn, dh2):
    _, S, F = dhu.shape
    tm = _pick(S, (512, 256))

    def body(a_ref, b_ref, h_ref, g_ref, r_ref, dh_ref, dhb_ref, dg_ref, acc_ref):
        i, k = pl.program_id(0), pl.program_id(1)

        @pl.when((i == 0) & (k == 0))
        def _():
            dg_ref[...] = jnp.zeros_like(dg_ref)

        part = jnp.dot(a_ref[...], b_ref[...], preferred_element_type=F32)

        @pl.when(k == 0)
        def _():
            acc_ref[...] = part

        @pl.when(k == 1)
        def _():
            dyv = acc_ref[...] + part
            hf = h_ref[...]
            r = lax.rsqrt(jnp.mean(hf * hf, axis=-1, keepdims=True) + EPS)
            hhat = hf * r
            dg_ref[0:1, :] += jnp.sum(dyv * hhat, axis=0, keepdims=True)
            dhat = dyv * g_ref[...]
            dh = r_ref[...] + r * (dhat - hhat * jnp.mean(dhat * hhat, axis=-1, keepdims=True))
            dh_ref[...] = dh
            dhb_ref[...] = dh.astype(BF16)

    row = pl.BlockSpec((tm, D_MODEL), lambda i, k: (i, 0))
    return pl.pallas_call(
        body,
        out_shape=(jax.ShapeDtypeStruct((S, D_MODEL), F32), jax.ShapeDtypeStruct((S, D_MODEL), BF16),
                   jax.ShapeDtypeStruct((8, D_MODEL), F32)),
        grid=(S // tm, 2),
        in_specs=[pl.BlockSpec((None, tm, F), lambda i, k: (k, i, 0)), pl.BlockSpec((F, D_MODEL), lambda i, k: (k, 0)),
                  row, pl.BlockSpec((1, D_MODEL), lambda i, k: (0, 0)), row],
        out_specs=(row, row, pl.BlockSpec((8, D_MODEL), lambda i, k: (0, 0))),
        scratch_shapes=[pltpu.VMEM((tm, D_MODEL), F32)],
        name="ffn_up_dx_rms", compiler_params=_cp(("arbitrary", "arbitrary"), VMEM_LIMIT),
    )(dhu, w_up_bf, h1, g_ffn, dh2)


def _conv_gate_bwd(hc, hu, dact, cw):
    _, S, F = hu.shape
    tm = _pick(S, (CONV_TM, 128))
    tn = _pick(F, (CONV_TN, 256, 128))
    r8 = tm // 8
    n_i = S // tm
    last8 = S // 8 - 1

    def body(hc_ref, hcn_ref, hu_ref, da_ref, dan_ref, w_ref, dhu_ref, dcw_ref):
        i = pl.program_id(1)

        @pl.when(i == 0)
        def _():
            dcw_ref[...] = jnp.zeros_like(dcw_ref)

        rid8 = lax.broadcasted_iota(jnp.int32, (8, tn), 0)

        def gate_grads(a, g, d):
            sg = _sigmoid(g)
            return d * (g * sg), d * a * (sg * (1.0 + g * (1.0 - sg)))

        dhc = gate_grads(hc_ref[0], hc_ref[1], da_ref[...])
        dhc_n = gate_grads(hcn_ref[0], hcn_ref[1], dan_ref[...])
        for h in range(2):
            w = w_ref[h]
            d = dhc[h]
            dn = jnp.where(i < n_i - 1, dhc_n[h], 0.0)
            u1 = _shift_rows_up(d, dn, 1)
            u2 = _shift_rows_up(d, dn, 2)
            dhu_ref[h] = (w[2:3, :] * d + w[1:2, :] * u1 + w[0:1, :] * u2).astype(BF16)
            x = hu_ref[h]
            t0, t1, t2, t3 = [jnp.sum(t, axis=0, keepdims=True) for t in (u2 * x, u1 * x, d * x, d)]
            dcw_ref[h] += jnp.where(rid8 == 0, t0, jnp.where(rid8 == 1, t1, jnp.where(rid8 == 2, t2, jnp.where(rid8 == 3, t3, 0.0))))

    cur = pl.BlockSpec((2, tm, tn), lambda j, i: (0, i, j))
    return pl.pallas_call(
        body,
        out_shape=(jax.ShapeDtypeStruct((2, S, F), BF16), jax.ShapeDtypeStruct((2, 8, F), F32)),
        grid=(F // tn, n_i),
        in_specs=[cur, pl.BlockSpec((2, 8, tn), lambda j, i: (0, jnp.minimum((i + 1) * r8, last8), j)), cur,
                  pl.BlockSpec((tm, tn), lambda j, i: (i, j)),
                  pl.BlockSpec((8, tn), lambda j, i: (jnp.minimum((i + 1) * r8, last8), j)),
                  pl.BlockSpec((2, 8, tn), lambda j, i: (0, 0, j))],
        out_specs=(cur, pl.BlockSpec((2, 8, tn), lambda j, i: (0, 0, j))),
        name="conv_gate_bwd", compiler_params=_cp(("parallel", "arbitrary"), VMEM_LIMIT),
    )(hc, hc, hu, dact, dact, cw)


def _out_proj_dx_prep(dh1_bf, w_out_bf, att, lse, qa):
    S = att.shape[0]
    tm = _pick(S, (256,))

    def body(dh_ref, w_ref, o_ref, lse_ref, q_ref, dsg_ref, qb_ref, doa_ref):
        lane = lax.broadcasted_iota(jnp.int32, (tm, LANES), 1)
        dh = dh_ref[...]
        dsg_ref[...] = lax.dot_general(dh, w_ref[ATT_WIDTH:, :], _NT, preferred_element_type=F32)
        datt = lax.dot_general(dh, w_ref[:ATT_WIDTH, :], _NT, preferred_element_type=F32)
        for p in range(N_PAIRS):
            pc = slice(p * LANES, (p + 1) * LANES)
            do = datt[:, pc]
            prod = o_ref[:, pc] * do
            for hh in range(2):
                sel = (lane >= HEAD_DIM) if hh else (lane < HEAD_DIM)
                delta = jnp.sum(jnp.where(sel, prod, 0.0), axis=-1, keepdims=True)
                dod = pltpu.roll(do, HEAD_DIM, 1) if hh else do
                cols = slice((2 * p + hh) * LANES, (2 * p + hh + 1) * LANES)
                doa_ref[:, cols] = jnp.where(lane < HEAD_DIM, dod, _aug(lane, _split3f(-delta))).astype(BF16)
                lcol = p * LANES + hh * HEAD_DIM
                l3 = _split3f(-lse_ref[:, lcol:lcol + 1])
                augl = _aug(lane, [0.0] * 6 + l3).astype(BF16)
                qb_ref[:, cols] = jnp.where((lane >= HEAD_DIM + 6) & (lane < HEAD_DIM + 9), augl, q_ref[:, cols])

    half = pl.BlockSpec((tm, ATT_WIDTH), lambda i: (i, 0))
    wide = pl.BlockSpec((tm, N_HEADS * LANES), lambda i: (i, 0))
    return pl.pallas_call(
        body,
        out_shape=(jax.ShapeDtypeStruct((S, GMLP_WIDTH), F32), jax.ShapeDtypeStruct(qa.shape, BF16),
                   jax.ShapeDtypeStruct(qa.shape, BF16)),
        grid=(S // tm,),
        in_specs=[pl.BlockSpec((tm, D_MODEL), lambda i: (i, 0)), pl.BlockSpec((D_MODEL, D_MODEL), lambda i: (0, 0)),
                  half, half, wide],
        out_specs=(half, wide, wide),
        name="out_proj_dx_prep", compiler_params=_cp(("parallel",), VMEM_LIMIT),
    )(dh1_bf, w_out_bf, att, lse, qa)


def _attn_bwd(qb, ka, va, doa):
    S = qb.shape[0]
    tk = _pick(S, (512, 256))
    tq = tk
    nq = S // tq

    def pair(a, scale=None):
        lane = lax.broadcasted_iota(jnp.int32, (a.shape[0], LANES), 1)
        out = jnp.where(lane < HEAD_DIM, a[:, :LANES], pltpu.roll(a[:, LANES:], HEAD_DIM, 1))
        return out if scale is None else out * scale

    def lanes01(a, col, sign):
        lane = lax.broadcasted_iota(jnp.int32, (a.shape[0], LANES), 1)
        return jnp.where(lane == 0, sign * a[:, col:col + 1], jnp.where(lane == 1, sign * a[:, LANES + col:LANES + col + 1], 0.0))

    def body(q_ref, do_ref, k_ref, v_ref, dqc_ref, dkc_ref, dvc_ref, dcq_ref, dck_ref, dq_ref, dka_ref, dva_ref):
        kb = pl.program_id(1)

        @pl.when(kb == 0)
        def _():
            dq_ref[...] = jnp.zeros_like(dq_ref)

        dka_ref[...] = jnp.zeros_like(dka_ref)
        dva_ref[...] = jnp.zeros_like(dva_ref)
        rid = lax.broadcasted_iota(jnp.int32, (tk, tq), 0)
        cid = lax.broadcasted_iota(jnp.int32, (tk, tq), 1)

        def sub_tile(qs, q_len, k_off, k_len, masked):
            keys = slice(k_off, k_off + k_len)
            for h in range(2):
                cols = slice(h * LANES, (h + 1) * LANES)
                qblk = q_ref[pl.ds(qs, q_len), cols]
                doblk = do_ref[pl.ds(qs, q_len), cols]
                kh = k_ref[keys, cols]
                p = jnp.exp(lax.dot_general(kh, qblk, _NT, preferred_element_type=F32))
                if masked:
                    p = jnp.where(cid >= rid, p, 0.0)
                ds = (p * lax.dot_general(v_ref[keys, cols], doblk, _NT, preferred_element_type=F32)).astype(BF16)
                dva_ref[keys, cols] += jnp.dot(p.astype(BF16), doblk, preferred_element_type=F32)
                dka_ref[keys, cols] += jnp.dot(ds, qblk, preferred_element_type=F32)
                dq_ref[pl.ds(qs, q_len), cols] += lax.dot_general(ds, kh, _TN, preferred_element_type=F32)

        sub_tile(pl.multiple_of(kb * tq, tq), tq, 0, tk, True)

        def step(qi, carry):
            sub_tile(pl.multiple_of(qi * tq, tq), tq, 0, tk, False)
            return carry

        lax.fori_loop(kb + 1, nq, step, 0)
        dka = dka_ref[...]
        dkc_ref[...] = pair(dka).astype(BF16)
        dvc_ref[...] = pair(dva_ref[...]).astype(BF16)
        dck_ref[...] = lanes01(dka, HEAD_DIM + 3, -1.0)

        @pl.when(kb == nq - 1)
        def _():
            dqa = dq_ref[...]
            dqc_ref[...] = pair(dqa, HEAD_DIM ** -0.5).astype(BF16)
            dcq_ref[...] = lanes01(dqa, HEAD_DIM, 1.0)

    wide = 2 * LANES
    half = jax.ShapeDtypeStruct((S, ATT_WIDTH), BF16)
    slabs = jax.ShapeDtypeStruct((N_PAIRS, S, LANES), F32)
    return pl.pallas_call(
        body,
        out_shape=(half, half, half, slabs, slabs),
        grid=(N_PAIRS, nq),
        in_specs=[pl.BlockSpec((S, wide), lambda p, j: (0, p)), pl.BlockSpec((S, wide), lambda p, j: (0, p)),
                  pl.BlockSpec((tk, wide), lambda p, j: (j, p)), pl.BlockSpec((tk, wide), lambda p, j: (j, p))],
        out_specs=(pl.BlockSpec((S, LANES), lambda p, j: (0, p)), pl.BlockSpec((tk, LANES), lambda p, j: (j, p)),
                   pl.BlockSpec((tk, LANES), lambda p, j: (j, p)), pl.BlockSpec((None, S, LANES), lambda p, j: (p, 0, 0)),
                   pl.BlockSpec((None, tk, LANES), lambda p, j: (p, j, 0))),
        scratch_shapes=[pltpu.VMEM((S, wide), F32), pltpu.VMEM((tk, wide), F32), pltpu.VMEM((tk, wide), F32)],
        name="attn_bwd", compiler_params=_cp(("parallel", "arbitrary"), VMEM_LIMIT),
    )(qb, doa, ka, va)


def _gmlp_bwd(ug, dsg, gain, w_s, wt_s, bias_full):
    S = ug.shape[0]
    tm = _pick(S, (512, 256, 128))
    n_chunks = tm // CHUNK
    n_i = S // tm
    ones = _group_ones()
    nt = (((1,), (1,)), ((), ()))

    def body(ug_ref, dsg_ref, gain_ref, w_ref, wt_ref, bias_ref, ones_ref, dug_ref, dw_ref, dgain_ref, dbias_ref,
             dbacc_ref):
        i = pl.program_id(0)

        @pl.when(i == 0)
        def _():
            dw_ref[...] = jnp.zeros_like(dw_ref)
            dgain_ref[...] = jnp.zeros_like(dgain_ref)
            dbacc_ref[...] = jnp.zeros_like(dbacc_ref)

        ones_m = ones_ref[...]
        pu = ug_ref[:, :GMLP_WIDTH]
        pg = ug_ref[:, GMLP_WIDTH:]
        u = _gelu(pu)
        vr = _gelu(pg)
        ms = _dot3(vr * vr, ones_m) * (1.0 / GROUP_DIM)
        rinv = lax.rsqrt(ms + EPS)
        vhat = vr * rinv
        gain_v = gain_ref[...]
        vn = (vhat * gain_v).astype(BF16)
        mixed = _gmlp_mixed(vn, w_ref, bias_ref[...], n_chunks)
        dsg_v = dsg_ref[...]
        du = dsg_v * mixed
        dmixed = dsg_v * u
        dm_bf = dmixed.astype(BF16)
        lane = lax.broadcasted_iota(jnp.int32, (CHUNK, LANES), 1)
        row = lax.broadcasted_iota(jnp.int32, (CHUNK, CHUNK), 0)
        col = lax.broadcasted_iota(jnp.int32, (CHUNK, CHUNK), 1)
        wts = [jnp.where(col >= row, wt_ref[g], 0.0).astype(BF16) for g in range(N_GROUPS)]
        dvn_rows = []
        dbsum = jnp.zeros((CHUNK, GMLP_WIDTH), F32)
        for ci in range(n_chunks):
            rs = slice(ci * CHUNK, (ci + 1) * CHUNK)
            dbsum = dbsum + dmixed[rs, :]
            cols = []
            for pp in range(N_GROUPS // 2):
                cs = slice(pp * LANES, (pp + 1) * LANES)
                dm = dm_bf[rs, cs]
                dm_lo = jnp.where(lane < GROUP_DIM, dm, jnp.zeros_like(dm))
                dm_hi = jnp.where(lane >= GROUP_DIM, dm, jnp.zeros_like(dm))
                vb = vn[rs, cs]
                dw_ref[2 * pp] += lax.dot_general(dm_lo, vb, nt, preferred_element_type=F32)
                dw_ref[2 * pp + 1] += lax.dot_general(dm_hi, vb, nt, preferred_element_type=F32)
                cols.append(jnp.dot(wts[2 * pp], dm_lo, preferred_element_type=F32)
                            + jnp.dot(wts[2 * pp + 1], dm_hi, preferred_element_type=F32))
            dvn_rows.append(jnp.concatenate(cols, axis=1))
        dvn = jnp.concatenate(dvn_rows, axis=0)
        dbacc_ref[...] += dbsum
        dgain_ref[0:1, :] += jnp.sum(dvn * vhat, axis=0, keepdims=True)
        dvhat = dvn * gain_v
        gm = _dot3(dvhat * vhat, ones_m) * (1.0 / GROUP_DIM)
        dvr = rinv * (dvhat - vhat * gm)
        dug_ref[:, :GMLP_WIDTH] = (du * _gelu_grad(pu)).astype(BF16)
        dug_ref[:, GMLP_WIDTH:] = (dvr * _gelu_grad(pg)).astype(BF16)

        @pl.when(i == n_i - 1)
        def _():
            for g in range(N_GROUPS):
                dw_ref[g] = jnp.where(row >= col, dw_ref[g], 0.0)
            dbias_ref[...] = _dot3(dbacc_ref[...], ones_m)

    return pl.pallas_call(
        body,
        out_shape=(jax.ShapeDtypeStruct((S, 2 * GMLP_WIDTH), BF16), jax.ShapeDtypeStruct((N_GROUPS, CHUNK, CHUNK), F32),
                   jax.ShapeDtypeStruct((8, GMLP_WIDTH), F32), jax.ShapeDtypeStruct((CHUNK, GMLP_WIDTH), F32)),
        grid=(n_i,),
        in_specs=[pl.BlockSpec((tm, 2 * GMLP_WIDTH), lambda i: (i, 0)), pl.BlockSpec((tm, GMLP_WIDTH), lambda i: (i, 0)),
                  pl.BlockSpec((1, GMLP_WIDTH), lambda i: (0, 0)),
                  pl.BlockSpec((N_GROUPS, CHUNK, CHUNK), lambda i: (0, 0, 0)),
                  pl.BlockSpec((N_GROUPS, CHUNK, CHUNK), lambda i: (0, 0, 0)),
                  pl.BlockSpec((CHUNK, GMLP_WIDTH), lambda i: (0, 0)),
                  pl.BlockSpec((GMLP_WIDTH, GMLP_WIDTH), lambda i: (0, 0))],
        out_specs=(pl.BlockSpec((tm, 2 * GMLP_WIDTH), lambda i: (i, 0)),
                   pl.BlockSpec((N_GROUPS, CHUNK, CHUNK), lambda i: (0, 0, 0)),
                   pl.BlockSpec((8, GMLP_WIDTH), lambda i: (0, 0)),
                   pl.BlockSpec((CHUNK, GMLP_WIDTH), lambda i: (0, 0))),
        scratch_shapes=[pltpu.VMEM((CHUNK, GMLP_WIDTH), F32)],
        name="gmlp_bwd", compiler_params=_cp(("arbitrary",), VMEM_LIMIT),
    )(ug, dsg, gain, w_s, wt_s, bias_full, ones)


def _gate_bwd(dcq, dck, zf):
    S = zf.shape[0]
    tm = _pick(S, (256,))
    n_i = S // tm
    triu = (lax.broadcasted_iota(jnp.int32, (tm, tm), 0) <= lax.broadcasted_iota(jnp.int32, (tm, tm), 1)).astype(BF16)

    def body(dcq_ref, dck_ref, zf_ref, tri_ref, dzf_ref, dbf_ref, carry_ref):
        i = pl.program_id(0)

        @pl.when(i == 0)
        def _():
            carry_ref[...] = jnp.zeros_like(carry_ref)
            dbf_ref[...] = jnp.zeros_like(dbf_ref)

        lane = lax.broadcasted_iota(jnp.int32, (tm, LANES), 1)
        dc = jnp.zeros((tm, LANES), F32)
        for p in range(N_PAIRS):
            slab = dcq_ref[p] + dck_ref[p]
            for hh in range(2):
                dc = dc + jnp.where(lane == 2 * p + hh, slab[:, hh:hh + 1], 0.0)
        dlf = _dot3l(tri_ref[...], dc) + carry_ref[0:1, :]
        carry_ref[0:1, :] = dlf[0:1, :]
        dz = jnp.where(lane < N_HEADS, dlf * _sigmoid(-zf_ref[...]), 0.0)
        dzf_ref[...] = dz.astype(BF16)
        dbf_ref[0:1, :] += jnp.sum(dz, axis=0, keepdims=True)

    return pl.pallas_call(
        body,
        out_shape=(jax.ShapeDtypeStruct((S, LANES), BF16), jax.ShapeDtypeStruct((8, LANES), F32)),
        grid=(n_i,),
        in_specs=[pl.BlockSpec((N_PAIRS, tm, LANES), lambda i: (0, n_i - 1 - i, 0)),
                  pl.BlockSpec((N_PAIRS, tm, LANES), lambda i: (0, n_i - 1 - i, 0)),
                  pl.BlockSpec((tm, LANES), lambda i: (n_i - 1 - i, 0)),
                  pl.BlockSpec((tm, tm), lambda i: (0, 0))],
        out_specs=(pl.BlockSpec((tm, LANES), lambda i: (n_i - 1 - i, 0)), pl.BlockSpec((8, LANES), lambda i: (0, 0))),
        scratch_shapes=[pltpu.VMEM((8, LANES), F32)],
        name="gate_bwd", compiler_params=_cp(("arbitrary",), VMEM_LIMIT),
    )(dcq, dck, zf, triu)


def _out_proj_fwd(att_bf, sg, w_out_bf, x, g_ffn):
    S = x.shape[0]
    tm = _pick(S, (512, 256))

    def body(a_ref, s_ref, w_ref, x_ref, g_ref, h_ref, hn_ref):
        h = (x_ref[...] + jnp.dot(a_ref[...], w_ref[:ATT_WIDTH, :], preferred_element_type=F32)
             + jnp.dot(s_ref[...], w_ref[ATT_WIDTH:, :], preferred_element_type=F32))
        h_ref[...] = h
        r = lax.rsqrt(jnp.mean(h * h, axis=-1, keepdims=True) + EPS)
        hn_ref[...] = ((h * r) * g_ref[...]).astype(BF16)

    row = pl.BlockSpec((tm, D_MODEL), lambda i: (i, 0))
    half = pl.BlockSpec((tm, ATT_WIDTH), lambda i: (i, 0))
    return pl.pallas_call(
        body, out_shape=(jax.ShapeDtypeStruct((S, D_MODEL), F32), jax.ShapeDtypeStruct((S, D_MODEL), BF16)),
        grid=(S // tm,),
        in_specs=[half, half, pl.BlockSpec((D_MODEL, D_MODEL), lambda i: (0, 0)), row,
                  pl.BlockSpec((1, D_MODEL), lambda i: (0, 0))],
        out_specs=(row, row), name="out_proj", compiler_params=_cp(("parallel",), VMEM_LIMIT),
    )(att_bf, sg, w_out_bf, x, g_ffn)


def _out_proj_dw(att_bf, sg, dh1_bf):
    S = att_bf.shape[0]
    tk = _pick(S, (1024, 512))

    def body(a_ref, s_ref, d_ref, o_ref):
        k = pl.program_id(0)

        @pl.when(k == 0)
        def _():
            o_ref[...] = jnp.zeros_like(o_ref)

        d = d_ref[...]
        o_ref[:ATT_WIDTH, :] += lax.dot_general(a_ref[...], d, _TN, preferred_element_type=F32)
        o_ref[ATT_WIDTH:, :] += lax.dot_general(s_ref[...], d, _TN, preferred_element_type=F32)

    half = pl.BlockSpec((tk, ATT_WIDTH), lambda k: (k, 0))
    return pl.pallas_call(
        body, out_shape=jax.ShapeDtypeStruct((D_MODEL, D_MODEL), F32), grid=(S // tk,),
        in_specs=[half, half, pl.BlockSpec((tk, D_MODEL), lambda k: (k, 0))],
        out_specs=pl.BlockSpec((D_MODEL, D_MODEL), lambda k: (0, 0)),
        name="out_proj_dw", compiler_params=_cp(("arbitrary",), VMEM_LIMIT),
    )(att_bf, sg, dh1_bf)


_IN_PIECES = ((0, ATT_WIDTH), (ATT_WIDTH, ATT_WIDTH), (2 * ATT_WIDTH, ATT_WIDTH), (QKV, 2 * GMLP_WIDTH), (UG_END, LANES))


def _inproj_bwd_dx(pieces, w_pad, x, g_mix, dh1):
    S = x.shape[0]
    tm = _pick(S, (512, 256))

    def body(*refs):
        p_refs, (w_ref, x_ref, g_ref, r_ref, dx_ref, dg_ref) = refs[:5], refs[5:]
        i = pl.program_id(0)

        @pl.when(i == 0)
        def _():
            dg_ref[...] = jnp.zeros_like(dg_ref)

        dxn = None
        for p_ref, (c0, width) in zip(p_refs, _IN_PIECES):
            part = jnp.dot(p_ref[...], w_ref[c0:c0 + width, :], preferred_element_type=F32)
            dxn = part if dxn is None else dxn + part
        xf = x_ref[...]
        r = lax.rsqrt(jnp.mean(xf * xf, axis=-1, keepdims=True) + EPS)
        xhat = xf * r
        dg_ref[0:1, :] += jnp.sum(dxn * xhat, axis=0, keepdims=True)
        dhat = dxn * g_ref[...]
        dx_ref[...] = r_ref[...] + r * (dhat - xhat * jnp.mean(dhat * xhat, axis=-1, keepdims=True))

    row = pl.BlockSpec((tm, D_MODEL), lambda i: (i, 0))
    return pl.pallas_call(
        body, out_shape=(jax.ShapeDtypeStruct((S, D_MODEL), F32), jax.ShapeDtypeStruct((8, D_MODEL), F32)),
        grid=(S // tm,),
        in_specs=[pl.BlockSpec((tm, width), lambda i: (i, 0)) for _, width in _IN_PIECES]
        + [pl.BlockSpec((IN_PAD, D_MODEL), lambda i: (0, 0)), row, pl.BlockSpec((1, D_MODEL), lambda i: (0, 0)), row],
        out_specs=(row, pl.BlockSpec((8, D_MODEL), lambda i: (0, 0))),
        name="in_proj_dx", compiler_params=_cp(("arbitrary",), VMEM_LIMIT),
    )(*pieces, w_pad, x, g_mix, dh1)


def _inproj_bwd_dw(xn, pieces):
    S = xn.shape[0]
    tk = _pick(S, (512, 256))

    def body(*refs):
        x_ref, p_refs, o_ref = refs[0], refs[1:6], refs[6]
        k = pl.program_id(0)

        @pl.when(k == 0)
        def _():
            o_ref[...] = jnp.zeros_like(o_ref)

        xb = x_ref[...]
        for p_ref, (c0, width) in zip(p_refs, _IN_PIECES):
            o_ref[:, c0:c0 + width] += lax.dot_general(xb, p_ref[...], _TN, preferred_element_type=F32)

    return pl.pallas_call(
        body, out_shape=jax.ShapeDtypeStruct((D_MODEL, IN_PAD), F32), grid=(S // tk,),
        in_specs=[pl.BlockSpec((tk, D_MODEL), lambda k: (k, 0))]
        + [pl.BlockSpec((tk, width), lambda k: (k, 0)) for _, width in _IN_PIECES],
        out_specs=pl.BlockSpec((D_MODEL, IN_PAD), lambda k: (0, 0)),
        name="in_proj_dw", compiler_params=_cp(("arbitrary",), VMEM_LIMIT),
    )(xn, *pieces)


def _adamw(w, m, v, parts, name):
    R, C = w.shape[-2:]
    tr = R
    for cand in (256, 128, 64, 32, 16, 8):
        if R % cand == 0 and R > cand:
            tr = cand
            break
    c1 = 1.0 / (1.0 - ADAM_B1 ** ADAM_STEP)
    c2 = 1.0 / (1.0 - ADAM_B2 ** ADAM_STEP)

    def body(w_ref, m_ref, v_ref, p_ref, g_ref, d_ref, nm_ref, nv_ref):
        g = p_ref[0].astype(F32)
        for j in range(1, N_DEV):
            g = g + p_ref[j].astype(F32)
        g_ref[...] = g
        nm = ADAM_B1 * m_ref[...] + (1.0 - ADAM_B1) * g
        nv = ADAM_B2 * v_ref[...] + (1.0 - ADAM_B2) * (g * g)
        nm_ref[...] = nm
        nv_ref[...] = nv
        d_ref[...] = -ADAM_LR * ((nm * c1) / (jnp.sqrt(nv * c2) + ADAM_EPS) + ADAM_WD * w_ref[...])

    if w.ndim == 3:
        spec = pl.BlockSpec((None, tr, C), lambda i: (0, i, 0))
    else:
        spec = pl.BlockSpec((tr, C), lambda i: (i, 0))
    shp = jax.ShapeDtypeStruct(w.shape, F32)
    return pl.pallas_call(
        body, out_shape=(shp, shp, shp, shp), grid=(R // tr,),
        in_specs=[spec, spec, spec, pl.BlockSpec((N_DEV, tr, C), lambda i: (0, i, 0))],
        out_specs=(spec, spec, spec, spec),
        name=name, compiler_params=_cp(("parallel",), VMEM_LIMIT),
    )(w, m, v, parts)


def _place():
    x, y, c = lax.axis_index("x"), lax.axis_index("y"), lax.axis_index("c")
    return x, y, c


def _all_gather(blocks, name):
    n = len(blocks)

    def body(*refs):
        ins, outs = refs[:n], refs[n:2 * n]
        send_sems, recv_sems, local_sems = refs[2 * n:]
        x, y, c = _place()
        me, sibling = (x, y, c), (x, y, 1 - c)
        chips = [(1 - x, y), (x, 1 - y), (1 - x, 1 - y)]
        sends = []
        for a in range(n):
            out = outs[a]

            def slot(px, py, pc, out=out):
                return out.at[4 * px + 2 * py + pc]

            def copy(k, block, to, src=None, a=a, slot=slot):
                return pltpu.make_async_remote_copy(
                    src_ref=slot(*block) if src is None else src, dst_ref=slot(*block),
                    send_sem=send_sems.at[a, k], recv_sem=recv_sems.at[a, k], device_id=to, device_id_type=MESH)

            mine = pltpu.make_async_copy(ins[a], slot(*me), local_sems.at[a])
            mine.start()
            first = [copy(0, me, sibling, src=ins[a])]
            first += [copy(1 + j, me, (*chip, c), src=ins[a]) for j, chip in enumerate(chips)]
            for cp in first:
                cp.start()
            sends.append((mine, first, copy))
        for a in range(n):
            mine, first, copy = sends[a]
            passed = [copy(4 + j, (*chip, c), sibling) for j, chip in enumerate(chips)]
            for j, chip in enumerate(chips):
                copy(1 + j, (*chip, c), me).wait_recv()
                passed[j].start()
            copy(0, sibling, me).wait_recv()
            for j, chip in enumerate(chips):
                copy(4 + j, (*chip, 1 - c), me).wait_recv()
            for cp in first + passed:
                cp.wait_send()
            mine.wait()

    any_spec = pl.BlockSpec(memory_space=pl.ANY)
    return pl.pallas_call(
        body, out_shape=tuple(jax.ShapeDtypeStruct((N_DEV,) + b.shape, b.dtype) for b in blocks),
        in_specs=[any_spec] * n, out_specs=tuple([any_spec] * n),
        scratch_shapes=[pltpu.SemaphoreType.DMA((n, 7)), pltpu.SemaphoreType.DMA((n, 7)), pltpu.SemaphoreType.DMA((n,))],
        name=name,
    )(*blocks)


_HBM = pl.BlockSpec(memory_space=pltpu.HBM)
_SEM = pl.BlockSpec(memory_space=pltpu.SEMAPHORE)
_EFFECT = pltpu.SideEffectType.DATAFLOW_SIDE_EFFECTING


def _peers(x, y, c):
    out = []
    for k in range(1, N_DEV):
        px, py, pc = x ^ ((k >> 2) & 1), y ^ ((k >> 1) & 1), c ^ (k & 1)
        out.append((k, (px, py, pc), 4 * px + 2 * py + pc))
    return out


def _xchg_copies(src_refs, land_refs, send_sems, recv_sems, scatter):
    x, y, c = _place()
    me = 4 * x + 2 * y + c
    copies = []
    for a, (src, land) in enumerate(zip(src_refs, land_refs)):
        for k, place, idx in _peers(x, y, c):
            j = a * (N_DEV - 1) + k - 1
            copies.append(pltpu.make_async_remote_copy(
                src_ref=src.at[idx] if scatter[a] else src, dst_ref=land.at[me],
                send_sem=send_sems[j], recv_sem=recv_sems[j], device_id=place, device_id_type=MESH))
    return copies


def _xchg_start(srcs, scatter, name):
    n = len(srcs)
    lands = [lax.empty((N_DEV,) + (s.shape[1:] if sc else s.shape), s.dtype) for s, sc in zip(srcs, scatter)]

    ns = n * (N_DEV - 1)

    def body(*refs):
        sems = refs[2 * n:2 * n + 2 * ns]
        for cp in _xchg_copies(refs[:n], refs[n:2 * n], sems[:ns], sems[ns:], scatter):
            cp.start()
        token = refs[-1]
        token[...] = jnp.zeros_like(token)

    both = list(srcs) + lands
    res = pl.pallas_call(
        body, name=name,
        out_shape=(*[pltpu.SemaphoreType.DMA(())] * (2 * ns),
                   *[pltpu.HBM(a.shape, a.dtype) for a in both], jax.ShapeDtypeStruct((8, LANES), F32)),
        in_specs=[_HBM] * (2 * n),
        out_specs=(*([_SEM] * (2 * ns)), *([_HBM] * (2 * n)), pl.BlockSpec(memory_space=pltpu.VMEM)),
        input_output_aliases={i: 2 * ns + i for i in range(2 * n)},
        compiler_params=pltpu.CompilerParams(has_side_effects=_EFFECT),
    )(*[pltpu.with_memory_space_constraint(a, pltpu.HBM) for a in both])
    return (tuple(res[:2 * ns]), tuple(res[2 * ns:2 * ns + 2 * n])), res[-1]


def _xchg_wait(handle, scatter, after, name):
    sems, thru = handle
    n = len(thru) // 2
    ns = len(sems) // 2

    def body(*refs):
        got = refs[2 * n:2 * n + 2 * ns]
        for cp in _xchg_copies(refs[:n], refs[n:2 * n], got[:ns], got[ns:], scatter):
            cp.wait_send()
            cp.wait_recv()

    outs = pl.pallas_call(
        body, name=name, out_shape=tuple(pltpu.HBM(a.shape, a.dtype) for a in thru),
        in_specs=[_HBM] * (2 * n) + [_SEM] * (2 * ns) + [pl.BlockSpec(memory_space=pl.ANY)],
        out_specs=tuple([_HBM] * (2 * n)), input_output_aliases={i: i for i in range(2 * n)},
        compiler_params=pltpu.CompilerParams(has_side_effects=_EFFECT),
    )(*thru, *sems, after)
    return outs[:n], outs[n:]


def _tie(a, token):
    return a if token is None else a + token[0, 0].astype(a.dtype)


def _rows128(a):
    flat = a.reshape(-1)
    rows = -(-flat.shape[0] // LANES)
    rows = -(-rows // 8) * 8
    return jnp.pad(flat, (0, rows * LANES - flat.shape[0])).reshape(rows, LANES)


def _local_step(x, target, norm_mix_g, w_in_t, b_forget, gmlp_norm_g, w_spatial, b_spatial, norm_ffn_g, conv_b,
                norm_final_g, rest_fn, send_fn, small_fn, token=None):
    f = D_FF
    g_mix = norm_mix_g.reshape(1, D_MODEL)
    w_pad = jnp.pad(w_in_t, ((0, IN_PAD - IN_COLS), (0, 0)))
    bf_pad = jnp.pad(b_forget.reshape(1, N_HEADS), ((0, 0), (0, LANES - N_HEADS)))
    xn, qa, ka, va, ug, zf = _inproj_fwd(x, _tie(g_mix, token), w_pad, bf_pad)
    bias_full = jnp.repeat(b_spatial.reshape(N_GROUPS, CHUNK).T, GROUP_DIM, axis=1)
    w_s = w_spatial.reshape(N_GROUPS, CHUNK, CHUNK)
    gain = gmlp_norm_g.reshape(1, GMLP_WIDTH)
    sg = _gmlp_fwd(ug, gain, w_s, bias_full)
    att, lse, att_bf = _attn_fwd(qa, ka, va)
    w_out_bf, w_up_bf, conv_w, w_down_bf = rest_fn(att_bf)
    g_ffn = norm_ffn_g.reshape(1, D_MODEL)
    h1, hn = _out_proj_fwd(att_bf, sg, w_out_bf, x, g_ffn)
    cw = jnp.pad(conv_w.reshape(3, 2, f).transpose(1, 0, 2), ((0, 0), (0, 5), (0, 0)))
    cb = conv_b.reshape(2, 1, f)
    hu, hc, act = _ffn_up_conv(hn, w_up_bf, cw, cb)
    loss_blk, dh2, dh2_bf, dg_final = _ffn_down_loss(act, w_down_bf, h1, norm_final_g.reshape(1, D_MODEL), target)
    dw_down = _mm(act, dh2_bf, mode="tn", out_dtype=F32, tm=1408, tn=1024, tk=2048, name="ffn_down_dw")
    dact = _mm(dh2_bf, w_down_bf, mode="nt", out_dtype=F32, tm=1024, tn=1408, tk=1024, outer="j", name="ffn_down_dx")
    dhu, dcw = _conv_gate_bwd(hc, hu, dact, _tie(cw, send_fn("w_down", dw_down)))
    dw_up = _mm(hn, dhu, mode="tn", out_dtype=F32, tm=1024, tn=1408, tk=2048, b_halves=True, outer="j", name="ffn_up_dw")
    dh1, dh1_bf, dg_ffn = _ffn_up_dx_rms(dhu, w_up_bf, h1, _tie(g_ffn, send_fn("w_up", dw_up)), dh2)
    dsg, qb, doa = _out_proj_dx_prep(dh1_bf, w_out_bf, att, lse, qa)
    dw_out = _out_proj_dw(att_bf, sg, dh1_bf)
    dq, dk, dv, dcq, dck = _attn_bwd(qb, ka, va, doa)
    wt_s = w_s.transpose(0, 2, 1)
    dug, dw_s, dgain, dbias = _gmlp_bwd(ug, dsg, _tie(gain, send_fn("w_out", dw_out)), w_s, wt_s, bias_full)
    dzf, dbf = _gate_bwd(dcq, dck, zf)
    grad_x, dg_mix = _inproj_bwd_dx((dq, dk, dv, dug, dzf), w_pad, x, g_mix, dh1)
    grads = dict(
        norm_mix_g=dg_mix[0:1, :],
        b_forget=dbf[0:1, :N_HEADS],
        gmlp_norm_g=dgain[0:1, :],
        w_spatial=dw_s,
        b_spatial=dbias[:, ::GROUP_DIM].T,
        norm_ffn_g=dg_ffn[0:1, :],
        conv_w=dcw[:, 0:3, :].transpose(1, 0, 2).reshape(3, 2 * f),
        conv_b=dcw[:, 3, :].reshape(1, 2 * f),
        norm_final_g=dg_final[0, :],
    )
    token = small_fn(loss_blk[0, 0], grads)
    dw_in = _inproj_bwd_dw(xn, (dq, dk, dv, dug, _tie(dzf, token)))
    return grad_x, send_fn("w_in", dw_in[:, :IN_COLS])


SMALL = ("norm_mix_g", "b_forget", "gmlp_norm_g", "w_spatial", "b_spatial", "norm_ffn_g", "conv_b", "norm_final_g")


def kernel(x, norm_mix_g, w_in, b_forget, gmlp_norm_g, w_spatial, b_spatial, w_out, norm_ffn_g, w_up, conv_w, conv_b, w_down, norm_final_g, loss_target, m_norm_mix_g, m_w_in, m_b_forget, m_gmlp_norm_g, m_w_spatial, m_b_spatial, m_w_out, m_norm_ffn_g, m_w_up, m_conv_w, m_conv_b, m_w_down, m_norm_final_g, v_norm_mix_g, v_w_in, v_b_forget, v_gmlp_norm_g, v_w_spatial, v_b_spatial, v_w_out, v_norm_ffn_g, v_w_up, v_conv_w, v_conv_b, v_w_down, v_norm_final_g):
    weights = dict(norm_mix_g=norm_mix_g, w_in=w_in, b_forget=b_forget, gmlp_norm_g=gmlp_norm_g, w_spatial=w_spatial,
                   b_spatial=b_spatial, w_out=w_out, norm_ffn_g=norm_ffn_g, w_up=w_up, conv_w=conv_w, conv_b=conv_b,
                   w_down=w_down, norm_final_g=norm_final_g)
    m_in = dict(norm_mix_g=m_norm_mix_g, w_in=m_w_in, b_forget=m_b_forget, gmlp_norm_g=m_gmlp_norm_g,
                w_spatial=m_w_spatial, b_spatial=m_b_spatial, w_out=m_w_out, norm_ffn_g=m_norm_ffn_g, w_up=m_w_up,
                conv_w=m_conv_w, conv_b=m_conv_b, w_down=m_w_down, norm_final_g=m_norm_final_g)
    v_in = dict(norm_mix_g=v_norm_mix_g, w_in=v_w_in, b_forget=v_b_forget, gmlp_norm_g=v_gmlp_norm_g,
                w_spatial=v_w_spatial, b_spatial=v_b_spatial, w_out=v_w_out, norm_ffn_g=v_norm_ffn_g, w_up=v_w_up,
                conv_w=v_conv_w, conv_b=v_conv_b, w_down=v_w_down, norm_final_g=v_norm_final_g)
    order = list(weights)
    me = 4 * lax.axis_index("x") + 2 * lax.axis_index("y") + lax.axis_index("c")
    n_in, n_up = w_in.shape[2], w_up.shape[2]
    r_out, r_down = w_out.shape[1], w_down.shape[1]

    def with_mine(landed, mine):
        return lax.dynamic_update_index_in_dim(landed, mine, me, 0)

    up_blk = w_up[0].T.astype(BF16)
    out_blk = w_out[0].astype(BF16)
    down_blk = w_down[0].astype(BF16)
    taps_blk = jnp.pad(conv_w[0], ((0, 5), (0, 0)))
    (in_all,) = _all_gather([w_in[0].T.astype(BF16)], "gather_w_in")
    in_all, rest_blocks = lax.optimization_barrier((in_all, [up_blk, out_blk, down_blk, taps_blk]))
    rest_handle, token = _xchg_start(rest_blocks, [False] * 4, "gather_rest_start")
    w_in_t = in_all.reshape(N_DEV * n_in, D_MODEL)

    def rest_fn(after):
        mine, landed = _xchg_wait(rest_handle, [False] * 4, after, "gather_rest_wait")
        up_all, out_all, down_all, taps_all = [with_mine(l, b) for l, b in zip(landed, mine)]
        return (out_all.reshape(N_DEV * r_out, D_MODEL), up_all.reshape(N_DEV * n_up, D_MODEL),
                taps_all[:, :3, :].transpose(1, 0, 2).reshape(3, N_DEV * n_up),
                down_all.reshape(N_DEV * r_down, D_MODEL))

    sent = {}

    def send_fn(name, grad):
        if name == "w_in":
            parts = grad.reshape(D_MODEL, N_DEV, -1).transpose(1, 0, 2).astype(BF16)
        elif name == "w_up":
            parts = grad.reshape(D_MODEL, N_DEV, -1).transpose(1, 0, 2)
        else:
            parts = grad.reshape(N_DEV, -1, D_MODEL)
        sent[name], tok = _xchg_start([parts], [True], "scatter_" + name + "_start")
        return tok

    small = {}

    def small_fn(loss_local, g):
        loss_rows = jnp.pad(loss_local.reshape(1, 1), ((0, 31), (0, LANES - 1)))
        packed = [_rows128(g[k]) for k in SMALL] + [loss_rows, _rows128(g["conv_w"])]
        small["sizes"] = [p.shape[0] for p in packed]
        small["handle"], tok = _xchg_start([jnp.concatenate(packed, axis=0)], [False], "gather_small_start")
        return tok

    grad_x, after = _local_step(
        x[0], loss_target[0], norm_mix_g, w_in_t, b_forget, gmlp_norm_g, w_spatial, b_spatial, norm_ffn_g, conv_b,
        norm_final_g, rest_fn, send_fn, small_fn, token)

    outs = {}

    def update_big(name, after):
        (parts,), (landed,) = _xchg_wait(sent[name], [True], after, "scatter_" + name + "_wait")
        got = with_mine(landed, lax.dynamic_index_in_dim(parts, me, 0, keepdims=False))
        outs[name] = tuple(_adamw(weights[name], m_in[name], v_in[name], got, "adamw_" + name))
        return outs[name][0]

    for name in ("w_down", "w_up", "w_out"):
        after = update_big(name, after)

    (mine,), (landed,) = _xchg_wait(small["handle"], [False], after, "gather_small_wait")
    small_all = with_mine(landed, mine)
    sizes = small["sizes"]
    n_small_rows = sum(sizes[:-2])

    def pack(src):
        return jnp.concatenate([_rows128(src[k]) for k in SMALL] + [jnp.zeros((sizes[-2], LANES), F32)], axis=0)

    n_adam_rows = n_small_rows + sizes[-2]
    sg_, sd_, sm_, sv_ = _adamw(pack(weights), pack(m_in), pack(v_in), small_all[:, :n_adam_rows, :], "adamw_small")
    loss = sg_[n_small_rows, 0]
    off = 0
    for k, rows in zip(SMALL, sizes[:-2]):
        shp = weights[k].shape
        cnt = math.prod(shp)
        outs[k] = tuple(a[off:off + rows].reshape(-1)[:cnt].reshape(shp) for a in (sg_, sd_, sm_, sv_))
        off += rows
    taps_parts = small_all[:, n_adam_rows:, :].reshape(N_DEV, -1)[:, :3 * N_DEV * n_up].reshape(N_DEV, 3, N_DEV * n_up)
    taps_mine = lax.dynamic_slice_in_dim(taps_parts, me * n_up, n_up, axis=2)
    taps_mine = jnp.pad(taps_mine, ((0, 0), (0, 5), (0, 0)))

    def pad8(a):
        return jnp.pad(a[0], ((0, 5), (0, 0)))

    res = _adamw(pad8(conv_w), pad8(m_conv_w), pad8(v_conv_w), taps_mine, "adamw_conv_w")
    outs["conv_w"] = tuple(a[:3][None] for a in res)
    update_big("w_in", sg_)

    return (loss, grad_x[None], *[outs[k][0] for k in order], *[outs[k][1] for k in order],
            *[outs[k][2] for k in order], *[outs[k][3] for k in order])
```

```python
import functools
import math

import jax
import jax.numpy as jnp
from jax import lax
from jax.experimental import pallas as pl
from jax.experimental.pallas import tpu as pltpu

F32 = jnp.float32
BF16 = jnp.bfloat16

N_DEV = 8
D_MODEL = 1024
ATT_WIDTH = 512
GMLP_WIDTH = 512
HEAD_DIM = 64
N_HEADS = 8
N_PAIRS = 4
N_GROUPS = 8
GROUP_DIM = 64
CHUNK = 128
D_FF = 2816
IN_COLS = 2568
IN_PAD = 2688
QKV = 1536
UG_END = 2560
EPS = 1e-6
LANES = 128

ADAM_LR = 0.001
ADAM_B1 = 0.9
ADAM_B2 = 0.999
ADAM_EPS = 1e-08
ADAM_WD = 0.01
ADAM_STEP = 10

ATT_TQ = 1024
ATT_TK = 1024
FFN_TM, FFN_TN = 512, 1408
CONV_TM, CONV_TN = 256, 1408
VMEM_LIMIT = 56 * 1024 * 1024
MESH = pl.DeviceIdType.MESH


def _cp(sem, vmem=None):
    return pltpu.CompilerParams(dimension_semantics=sem, vmem_limit_bytes=vmem)


def _pick(n, prefs):
    for p in prefs:
        if n % p == 0:
            return p
    return n


def _split3(x):
    hi = x.astype(BF16)
    r1 = x - hi.astype(F32)
    mid = r1.astype(BF16)
    lo = (r1 - mid.astype(F32)).astype(BF16)
    return hi, mid, lo


def _dot3(x, ones_bf):
    hi, mid, lo = _split3(x)
    d = functools.partial(jnp.dot, preferred_element_type=F32)
    return d(hi, ones_bf) + d(mid, ones_bf) + d(lo, ones_bf)


def _dot3l(ones_bf, x):
    hi, mid, lo = _split3(x)
    d = functools.partial(jnp.dot, preferred_element_type=F32)
    return d(ones_bf, hi) + d(ones_bf, mid) + d(ones_bf, lo)


def _gelu(x):
    k = math.sqrt(2.0 / math.pi)
    t = jnp.tanh(k * (x + 0.044715 * (x * x * x)))
    return 0.5 * x * (1.0 + t)


def _gelu_grad(x):
    k = math.sqrt(2.0 / math.pi)
    x2 = x * x
    t = jnp.tanh(k * (x + 0.044715 * (x2 * x)))
    return 0.5 * (1.0 + t) + 0.5 * x * (1.0 - t * t) * (k * (1.0 + 3.0 * 0.044715 * x2))


def _sigmoid(x):
    return 1.0 / (1.0 + jnp.exp(-x))


def _mm(a, b, *, mode, out_dtype, tm, tn, tk, name, res=None, a_halves=False, b_halves=False,
        out_halves=False, outer="i"):
    if mode == "tn":
        K, M = a.shape[-2], a.shape[-1] * (2 if a_halves else 1)
    else:
        M, K = a.shape[-2], a.shape[-1] * (2 if a_halves else 1)
    if mode == "nt":
        N = b.shape[-2]
        assert b.shape[-1] == K
    else:
        N = b.shape[-1] * (2 if b_halves else 1)
    tm, tn, tk = min(tm, M), min(tn, N), min(tk, K)
    assert M % tm == 0 and N % tn == 0 and K % tk == 0, (name, M, N, K, tm, tn, tk)
    nm, nn, nk = M // tm, N // tn, K // tk

    def ij(g0, g1):
        return (g0, g1) if outer == "i" else (g1, g0)

    if mode == "nn":
        dims = (((1,), (0,)), ((), ()))
        if a_halves:
            nkh = nk // 2
            a_spec = pl.BlockSpec((None, tm, tk), lambda g0, g1, k: (k // nkh, ij(g0, g1)[0], k % nkh))
        else:
            a_spec = pl.BlockSpec((tm, tk), lambda g0, g1, k: (ij(g0, g1)[0], k))
        b_spec = pl.BlockSpec((tk, tn), lambda g0, g1, k: (k, ij(g0, g1)[1]))
    elif mode == "nt":
        dims = (((1,), (1,)), ((), ()))
        if a_halves:
            nkh = nk // 2
            a_spec = pl.BlockSpec((None, tm, tk), lambda g0, g1, k: (k // nkh, ij(g0, g1)[0], k % nkh))
        else:
            a_spec = pl.BlockSpec((tm, tk), lambda g0, g1, k: (ij(g0, g1)[0], k))
        b_spec = pl.BlockSpec((tn, tk), lambda g0, g1, k: (ij(g0, g1)[1], k))
    else:
        dims = (((0,), (0,)), ((), ()))
        if a_halves:
            nmh = nm // 2
            a_spec = pl.BlockSpec((None, tk, tm), lambda g0, g1, k: (ij(g0, g1)[0] // nmh, k, ij(g0, g1)[0] % nmh))
        else:
            a_spec = pl.BlockSpec((tk, tm), lambda g0, g1, k: (k, ij(g0, g1)[0]))
        if b_halves:
            nnh = nn // 2
            b_spec = pl.BlockSpec((None, tk, tn), lambda g0, g1, k: (ij(g0, g1)[1] // nnh, k, ij(g0, g1)[1] % nnh))
        else:
            b_spec = pl.BlockSpec((tk, tn), lambda g0, g1, k: (k, ij(g0, g1)[1]))
    if out_halves:
        nnh = nn // 2
        o_spec = pl.BlockSpec((None, tm, tn), lambda g0, g1, k: (ij(g0, g1)[1] // nnh, ij(g0, g1)[0], ij(g0, g1)[1] % nnh))
        o_shape = jax.ShapeDtypeStruct((2, M, N // 2), out_dtype)
    else:
        o_spec = pl.BlockSpec((tm, tn), lambda g0, g1, k: ij(g0, g1))
        o_shape = jax.ShapeDtypeStruct((M, N), out_dtype)
    in_specs = [a_spec, b_spec]
    args = [a, b]
    if res is not None:
        in_specs.append(pl.BlockSpec((tm, tn), lambda g0, g1, k: ij(g0, g1)))
        args.append(res)

    def body(*refs):
        if res is not None:
            a_ref, b_ref, r_ref, o_ref = refs[:4]
        else:
            a_ref, b_ref, o_ref = refs[:3]
            r_ref = None
        part = lax.dot_general(a_ref[...], b_ref[...], dims, preferred_element_type=F32)
        if nk == 1:
            if r_ref is not None:
                part = part + r_ref[...]
            o_ref[...] = part.astype(out_dtype)
            return
        acc_ref = refs[-1]
        k = pl.program_id(2)

        @pl.when(k == 0)
        def _():
            acc_ref[...] = part

        @pl.when(k > 0)
        def _():
            acc_ref[...] += part

        @pl.when(k == nk - 1)
        def _():
            tot = acc_ref[...]
            if r_ref is not None:
                tot = tot + r_ref[...]
            o_ref[...] = tot.astype(out_dtype)

    grid = (nm, nn, nk) if outer == "i" else (nn, nm, nk)
    scratch = [] if nk == 1 else [pltpu.VMEM((tm, tn), F32)]
    return pl.pallas_call(
        body, out_shape=o_shape, grid=grid, in_specs=in_specs, out_specs=o_spec, scratch_shapes=scratch,
        name=name, compiler_params=_cp(("parallel", "parallel", "arbitrary"), VMEM_LIMIT),
    )(*args)


def _aug(lane, terms):
    out = 0.0
    for j, t in enumerate(terms):
        out = jnp.where(lane == HEAD_DIM + j, t, out)
    return out


def _split3f(x):
    hi, mid, lo = _split3(x)
    return [hi.astype(F32), mid.astype(F32), lo.astype(F32)]


def _inproj_fwd(x, g_mix, w_pad, bf_pad):
    S = x.shape[0]
    tm = _pick(S, (512, 256))
    tri = (lax.broadcasted_iota(jnp.int32, (tm, tm), 0) >= lax.broadcasted_iota(jnp.int32, (tm, tm), 1)).astype(BF16)

    def body(x_ref, g_ref, w_ref, bf_ref, tri_ref, xn_ref, qa_ref, ka_ref, va_ref, ug_ref, zf_ref, carry_ref):
        i = pl.program_id(0)

        @pl.when(i == 0)
        def _():
            carry_ref[...] = jnp.zeros_like(carry_ref)

        xf = x_ref[...]
        r = lax.rsqrt(jnp.mean(xf * xf, axis=-1, keepdims=True) + EPS)
        xn = ((xf * r) * g_ref[...]).astype(BF16)
        xn_ref[...] = xn
        proj = lax.dot_general(xn, w_ref[...], _NT, preferred_element_type=F32)
        ug_ref[...] = proj[:, QKV:UG_END]
        zf = proj[:, UG_END:] + bf_ref[...]
        zf_ref[...] = zf
        lf = jnp.minimum(zf, 0.0) - jnp.log(1.0 + jnp.exp(-jnp.abs(zf)))
        c = _dot3l(tri_ref[...], lf) + carry_ref[0:1, :]
        carry_ref[0:1, :] = c[tm - 1:tm, :]
        c3 = _split3f(c)
        lane = lax.broadcasted_iota(jnp.int32, (tm, LANES), 1)
        ones3 = [1.0, 1.0, 1.0]
        for h in range(N_HEADS):
            p, odd = h // 2, h % 2
            ch = [t[:, h:h + 1] for t in c3]

            def head(base, scale=None, p=p, odd=odd):
                blk = proj[:, base + p * LANES:base + (p + 1) * LANES]
                if scale is not None:
                    blk = blk * scale
                return pltpu.roll(blk, HEAD_DIM, 1) if odd else blk

            cols = slice(h * LANES, (h + 1) * LANES)
            qa_ref[:, cols] = jnp.where(lane < HEAD_DIM, head(0, HEAD_DIM ** -0.5), _aug(lane, ch + ones3)).astype(BF16)
            ka_ref[:, cols] = jnp.where(lane < HEAD_DIM, head(ATT_WIDTH),
                                        _aug(lane, ones3 + [-t for t in ch] + ones3)).astype(BF16)
            va_ref[:, cols] = jnp.where(lane < HEAD_DIM, head(2 * ATT_WIDTH), _aug(lane, ones3)).astype(BF16)

    wide = N_HEADS * LANES
    return pl.pallas_call(
        body,
        out_shape=(jax.ShapeDtypeStruct((S, D_MODEL), BF16), jax.ShapeDtypeStruct((S, wide), BF16),
                   jax.ShapeDtypeStruct((S, wide), BF16), jax.ShapeDtypeStruct((S, wide), BF16),
                   jax.ShapeDtypeStruct((S, 2 * GMLP_WIDTH), F32), jax.ShapeDtypeStruct((S, LANES), F32)),
        grid=(S // tm,),
        in_specs=[pl.BlockSpec((tm, D_MODEL), lambda i: (i, 0)), pl.BlockSpec((1, D_MODEL), lambda i: (0, 0)),
                  pl.BlockSpec((IN_PAD, D_MODEL), lambda i: (0, 0)), pl.BlockSpec((1, LANES), lambda i: (0, 0)),
                  pl.BlockSpec((tm, tm), lambda i: (0, 0))],
        out_specs=(pl.BlockSpec((tm, D_MODEL), lambda i: (i, 0)), pl.BlockSpec((tm, wide), lambda i: (i, 0)),
                   pl.BlockSpec((tm, wide), lambda i: (i, 0)), pl.BlockSpec((tm, wide), lambda i: (i, 0)),
                   pl.BlockSpec((tm, 2 * GMLP_WIDTH), lambda i: (i, 0)), pl.BlockSpec((tm, LANES), lambda i: (i, 0))),
        scratch_shapes=[pltpu.VMEM((8, LANES), F32)],
        name="inproj_fwd", compiler_params=_cp(("arbitrary",), VMEM_LIMIT),
    )(x, g_mix, w_pad, bf_pad, tri)


def _group_ones():
    r = lax.broadcasted_iota(jnp.int32, (GMLP_WIDTH, GMLP_WIDTH), 0) // GROUP_DIM
    c = lax.broadcasted_iota(jnp.int32, (GMLP_WIDTH, GMLP_WIDTH), 1) // GROUP_DIM
    return (r == c).astype(BF16)


def _gmlp_mixed(vn_bf, w_ref, bias, n_chunks):
    lane = lax.broadcasted_iota(jnp.int32, (CHUNK, LANES), 1)
    row = lax.broadcasted_iota(jnp.int32, (CHUNK, CHUNK), 0)
    col = lax.broadcasted_iota(jnp.int32, (CHUNK, CHUNK), 1)
    ws = [jnp.where(row >= col, w_ref[g], 0.0).astype(BF16) for g in range(N_GROUPS)]
    rows = []
    for ci in range(n_chunks):
        cols = []
        for pp in range(N_GROUPS // 2):
            v = vn_bf[ci * CHUNK:(ci + 1) * CHUNK, pp * LANES:(pp + 1) * LANES]
            v_lo = jnp.where(lane < GROUP_DIM, v, jnp.zeros_like(v))
            v_hi = jnp.where(lane >= GROUP_DIM, v, jnp.zeros_like(v))
            m = (jnp.dot(ws[2 * pp], v_lo, preferred_element_type=F32)
                 + jnp.dot(ws[2 * pp + 1], v_hi, preferred_element_type=F32))
            cols.append(m + bias[:, pp * LANES:(pp + 1) * LANES])
        rows.append(jnp.concatenate(cols, axis=1))
    return jnp.concatenate(rows, axis=0)


def _gmlp_fwd(ug, gain, w_s, bias_full):
    S = ug.shape[0]
    tm = _pick(S, (512, 256, 128))
    ones = _group_ones()

    def body(ug_ref, gain_ref, w_ref, bias_ref, ones_ref, sg_ref):
        u = _gelu(ug_ref[:, :GMLP_WIDTH])
        vr = _gelu(ug_ref[:, GMLP_WIDTH:])
        ms = _dot3(vr * vr, ones_ref[...]) * (1.0 / GROUP_DIM)
        vn = ((vr * lax.rsqrt(ms + EPS)) * gain_ref[...]).astype(BF16)
        mixed = _gmlp_mixed(vn, w_ref, bias_ref[...], tm // CHUNK)
        sg_ref[...] = (u * mixed).astype(BF16)

    return pl.pallas_call(
        body, out_shape=jax.ShapeDtypeStruct((S, GMLP_WIDTH), BF16), grid=(S // tm,),
        in_specs=[pl.BlockSpec((tm, 2 * GMLP_WIDTH), lambda i: (i, 0)), pl.BlockSpec((1, GMLP_WIDTH), lambda i: (0, 0)),
                  pl.BlockSpec((N_GROUPS, CHUNK, CHUNK), lambda i: (0, 0, 0)),
                  pl.BlockSpec((CHUNK, GMLP_WIDTH), lambda i: (0, 0)),
                  pl.BlockSpec((GMLP_WIDTH, GMLP_WIDTH), lambda i: (0, 0))],
        out_specs=pl.BlockSpec((tm, GMLP_WIDTH), lambda i: (i, 0)),
        name="gmlp_fwd", compiler_params=_cp(("parallel",), VMEM_LIMIT),
    )(ug, gain, w_s, bias_full, ones)


_NT = (((1,), (1,)), ((), ()))
_TN = (((0,), (0,)), ((), ()))


def _attn_fwd(qa, ka, va):
    S = qa.shape[0]
    tq = _pick(S, (ATT_TQ, 256))
    tk = min(ATT_TK, tq)
    nq = S // tq
    per_q = tq // tk

    def body(q_ref, k_ref, v_ref, o_ref, lse_ref, ob_ref):
        qi = pl.program_id(1)
        lane = lax.broadcasted_iota(jnp.int32, (tq, LANES), 1)
        sub = tk
        rid = lax.broadcasted_iota(jnp.int32, (sub, sub), 0)
        cid = lax.broadcasted_iota(jnp.int32, (sub, sub), 1)
        qs = [q_ref[:, :LANES], q_ref[:, LANES:]]

        def update(q, ks, k_len, h, m, acc, masked):
            cols = slice(h * LANES, (h + 1) * LANES)
            s = lax.dot_general(q, k_ref[pl.ds(ks, k_len), cols], _NT, preferred_element_type=F32)
            if masked:
                s = jnp.where(rid >= cid, s, -jnp.inf)
            m_new = jnp.maximum(m, jnp.max(s, axis=-1, keepdims=True))
            p = jnp.exp(s - m_new).astype(BF16)
            acc = jnp.exp(m - m_new) * acc + jnp.dot(p, v_ref[pl.ds(ks, k_len), cols], preferred_element_type=F32)
            return m_new, acc

        def step(kb, carry):
            ks = pl.multiple_of(kb * tk, tk)
            return tuple(update(qs[h], ks, tk, h, *carry[h], False) for h in range(2))

        one = (jnp.full((tq, 1), -jnp.inf, F32), jnp.zeros((tq, LANES), F32))
        carry = lax.fori_loop(0, qi * per_q, step, (one, one))
        outs, lses = [], []
        for h in range(2):
            ms, accs = [], []
            for r in range(tq // sub):
                rows = slice(r * sub, (r + 1) * sub)
                m, acc = carry[h][0][rows], carry[h][1][rows]
                for c in range(r + 1):
                    ks = pl.multiple_of(qi * tq + c * sub, sub)
                    m, acc = update(qs[h][rows], ks, sub, h, m, acc, c == r)
                ms.append(m)
                accs.append(acc)
            m, acc = jnp.concatenate(ms, axis=0), jnp.concatenate(accs, axis=0)
            l = acc[:, HEAD_DIM:HEAD_DIM + 1]
            outs.append(acc / l)
            lses.append(m + jnp.log(l))
        o = jnp.where(lane < HEAD_DIM, outs[0], pltpu.roll(outs[1], HEAD_DIM, 1))
        o_ref[...] = o
        ob_ref[...] = o.astype(BF16)
        lse_ref[...] = jnp.where(lane < HEAD_DIM, lses[0], lses[1])

    return pl.pallas_call(
        body,
        out_shape=(jax.ShapeDtypeStruct((S, ATT_WIDTH), F32), jax.ShapeDtypeStruct((S, ATT_WIDTH), F32),
                   jax.ShapeDtypeStruct((S, ATT_WIDTH), BF16)),
        grid=(N_PAIRS, nq),
        in_specs=[pl.BlockSpec((tq, 2 * LANES), lambda p, i: (i, p)),
                  pl.BlockSpec((S, 2 * LANES), lambda p, i: (0, p)),
                  pl.BlockSpec((S, 2 * LANES), lambda p, i: (0, p))],
        out_specs=(pl.BlockSpec((tq, LANES), lambda p, i: (i, p)), pl.BlockSpec((tq, LANES), lambda p, i: (i, p)),
                   pl.BlockSpec((tq, LANES), lambda p, i: (i, p))),
        name="attn_fwd", compiler_params=_cp(("parallel", "parallel"), VMEM_LIMIT),
    )(qa, ka, va)


def _shift_rows(x, prev, n):
    rid = lax.broadcasted_iota(jnp.int32, x.shape, 0)
    y = pltpu.roll(x, n, 0)
    if n == 1:
        return jnp.where(rid == 0, prev[7:8, :], y)
    return jnp.where(rid == 0, prev[6:7, :], jnp.where(rid == 1, prev[7:8, :], y))


def _shift_rows_up(x, nxt, n):
    rows = x.shape[0]
    rid = lax.broadcasted_iota(jnp.int32, x.shape, 0)
    y = pltpu.roll(x, rows - n, 0)
    if n == 1:
        return jnp.where(rid == rows - 1, nxt[0:1, :], y)
    return jnp.where(rid == rows - 2, nxt[0:1, :], jnp.where(rid == rows - 1, nxt[1:2, :], y))


def _conv3(cur, prev, w, b):
    return (w[0:1, :] * _shift_rows(cur, prev, 2) + w[1:2, :] * _shift_rows(cur, prev, 1)
            + w[2:3, :] * cur + b)


def _ffn_up_conv(hn, w_up_bf, cw, cb):
    S = hn.shape[0]
    F = D_FF
    tm = _pick(S, (FFN_TM, 256))
    tn = _pick(F, (FFN_TN, 256, 128))
    nj = F // tn

    def body(hn_ref, wa_ref, wg_ref, cw_ref, cb_ref, hu_ref, hc_ref, act_ref, tail_ref):
        i = pl.program_id(1)

        @pl.when(i == 0)
        def _():
            tail_ref[...] = jnp.zeros_like(tail_ref)

        hn_v = hn_ref[...]
        halves = []
        for h, w_ref in enumerate((wa_ref, wg_ref)):
            hu = lax.dot_general(hn_v, w_ref[...], _NT, preferred_element_type=F32)
            hu_ref[h] = hu
            hc = _conv3(hu, tail_ref[h], cw_ref[h], cb_ref[h])
            hc_ref[h] = hc
            halves.append(hc)
            tail_ref[h] = hu[tm - 8:, :]
        a, g = halves
        act_ref[...] = (g * _sigmoid(g) * a).astype(BF16)

    both = pl.BlockSpec((2, tm, tn), lambda j, i: (0, i, j))
    return pl.pallas_call(
        body, out_shape=(jax.ShapeDtypeStruct((2, S, F), F32), jax.ShapeDtypeStruct((2, S, F), F32),
                         jax.ShapeDtypeStruct((S, F), BF16)),
        grid=(nj, S // tm),
        in_specs=[pl.BlockSpec((tm, D_MODEL), lambda j, i: (i, 0)),
                  pl.BlockSpec((tn, D_MODEL), lambda j, i: (j, 0)),
                  pl.BlockSpec((tn, D_MODEL), lambda j, i: (nj + j, 0)),
                  pl.BlockSpec((2, 8, tn), lambda j, i: (0, 0, j)),
                  pl.BlockSpec((2, 1, tn), lambda j, i: (0, 0, j))],
        out_specs=(both, both, pl.BlockSpec((tm, tn), lambda j, i: (i, j))),
        scratch_shapes=[pltpu.VMEM((2, 8, tn), F32)],
        name="ffn_up_conv", compiler_params=_cp(("parallel", "arbitrary"), VMEM_LIMIT),
    )(hn, w_up_bf, w_up_bf, cw, cb)


def _ffn_down_loss(act, w_down_bf, h1, g_final, target):
    S = h1.shape[0]
    tm = _pick(S, (512, 256))

    def body(a_ref, w_ref, h1_ref, g_ref, t_ref, loss_ref, dh_ref, dhb_ref, dg_ref):
        i = pl.program_id(0)

        @pl.when(i == 0)
        def _():
            loss_ref[...] = jnp.zeros_like(loss_ref)
            dg_ref[...] = jnp.zeros_like(dg_ref)

        hf = h1_ref[...] + jnp.dot(a_ref[...], w_ref[...], preferred_element_type=F32)
        g = g_ref[...]
        r = lax.rsqrt(jnp.mean(hf * hf, axis=-1, keepdims=True) + EPS)
        hhat = hf * r
        err = hhat * g - t_ref[...]
        loss_ref[...] += 0.5 * jnp.sum(jnp.mean(err * err, axis=-1, keepdims=True))
        dy = err * (1.0 / D_MODEL)
        dg_ref[0:1, :] += jnp.sum(dy * hhat, axis=0, keepdims=True)
        dhat = dy * g
        dh = r * (dhat - hhat * jnp.mean(dhat * hhat, axis=-1, keepdims=True))
        dh_ref[...] = dh
        dhb_ref[...] = dh.astype(BF16)

    row = pl.BlockSpec((tm, D_MODEL), lambda i: (i, 0))
    return pl.pallas_call(
        body,
        out_shape=(jax.ShapeDtypeStruct((8, LANES), F32), jax.ShapeDtypeStruct((S, D_MODEL), F32),
                   jax.ShapeDtypeStruct((S, D_MODEL), BF16), jax.ShapeDtypeStruct((8, D_MODEL), F32)),
        grid=(S // tm,),
        in_specs=[pl.BlockSpec((tm, D_FF), lambda i: (i, 0)), pl.BlockSpec((D_FF, D_MODEL), lambda i: (0, 0)), row,
                  pl.BlockSpec((1, D_MODEL), lambda i: (0, 0)), row],
        out_specs=(pl.BlockSpec((8, LANES), lambda i: (0, 0)), row, row, pl.BlockSpec((8, D_MODEL), lambda i: (0, 0))),
        name="ffn_down_loss", compiler_params=_cp(("arbitrary",), VMEM_LIMIT),
    )(act, w_down_bf, h1, g_final, target)


def _ffn_up_dx_rms(dhu, w_up_bf, h1, g_ffn, dh2):
    _, S, F = dhu.shape
    tm = _pick(S, (512, 256))

    def body(a_ref, b_ref, h_ref, g_ref, r_ref, dh_ref, dhb_ref, dg_ref, acc_ref):
        i, k = pl.program_id(0), pl.program_id(1)

        @pl.when((i == 0) & (k == 0))
        def _():
            dg_ref[...] = jnp.zeros_like(dg_ref)

        part = jnp.dot(a_ref[...], b_ref[...], preferred_element_type=F32)

        @pl.when(k == 0)
        def _():
            acc_ref[...] = part

        @pl.when(k == 1)
        def _():
            dyv = acc_ref[...] + part
            hf = h_ref[...]
            r = lax.rsqrt(jnp.mean(hf * hf, axis=-1, keepdims=True) + EPS)
            hhat = hf * r
            dg_ref[0:1, :] += jnp.sum(dyv * hhat, axis=0, keepdims=True)
            dhat = dyv * g_ref[...]
            dh = r_ref[...] + r * (dhat - hhat * jnp.mean(dhat * hhat, axis=-1, keepdims=True))
            dh_ref[...] = dh
            dhb_ref[...] = dh.astype(BF16)

    row = pl.BlockSpec((tm, D_MODEL), lambda i, k: (i, 0))
    return pl.pallas_call(
        body,
        out_shape=(jax.ShapeDtypeStruct((S, D_MODEL), F32), jax.ShapeDtypeStruct((S, D_MODEL), BF16),
                   jax.ShapeDtypeStruct((8, D_MODEL), F32)),
        grid=(S // tm, 2),
        in_specs=[pl.BlockSpec((None, tm, F), lambda i, k: (k, i, 0)), pl.BlockSpec((F, D_MODEL), lambda i, k: (k, 0)),
                  row, pl.BlockSpec((1, D_MODEL), lambda i, k: (0, 0)), row],
        out_specs=(row, row, pl.BlockSpec((8, D_MODEL), lambda i, k: (0, 0))),
        scratch_shapes=[pltpu.VMEM((tm, D_MODEL), F32)],
        name="ffn_up_dx_rms", compiler_params=_cp(("arbitrary", "arbitrary"), VMEM_LIMIT),
    )(dhu, w_up_bf, h1, g_ffn, dh2)


def _conv_gate_bwd(hc, hu, dact, cw):
    _, S, F = hu.shape
    tm = _pick(S, (CONV_TM, 128))
    tn = _pick(F, (CONV_TN, 256, 128))
    r8 = tm // 8
    n_i = S // tm
    last8 = S // 8 - 1

    def body(hc_ref, hcn_ref, hu_ref, da_ref, dan_ref, w_ref, dhu_ref, dcw_ref):
        i = pl.program_id(1)

        @pl.when(i == 0)
        def _():
            dcw_ref[...] = jnp.zeros_like(dcw_ref)

        rid8 = lax.broadcasted_iota(jnp.int32, (8, tn), 0)

        def gate_grads(a, g, d):
            sg = _sigmoid(g)
            return d * (g * sg), d * a * (sg * (1.0 + g * (1.0 - sg)))

        dhc = gate_grads(hc_ref[0], hc_ref[1], da_ref[...])
        dhc_n = gate_grads(hcn_ref[0], hcn_ref[1], dan_ref[...])
        for h in range(2):
            w = w_ref[h]
            d = dhc[h]
            dn = jnp.where(i < n_i - 1, dhc_n[h], 0.0)
            u1 = _shift_rows_up(d, dn, 1)
            u2 = _shift_rows_up(d, dn, 2)
            dhu_ref[h] = (w[2:3, :] * d + w[1:2, :] * u1 + w[0:1, :] * u2).astype(BF16)
            x = hu_ref[h]
            t0, t1, t2, t3 = [jnp.sum(t, axis=0, keepdims=True) for t in (u2 * x, u1 * x, d * x, d)]
            dcw_ref[h] += jnp.where(rid8 == 0, t0, jnp.where(rid8 == 1, t1, jnp.where(rid8 == 2, t2, jnp.where(rid8 == 3, t3, 0.0))))

    cur = pl.BlockSpec((2, tm, tn), lambda j, i: (0, i, j))
    return pl.pallas_call(
        body,
        out_shape=(jax.ShapeDtypeStruct((2, S, F), BF16), jax.ShapeDtypeStruct((2, 8, F), F32)),
        grid=(F // tn, n_i),
        in_specs=[cur, pl.BlockSpec((2, 8, tn), lambda j, i: (0, jnp.minimum((i + 1) * r8, last8), j)), cur,
                  pl.BlockSpec((tm, tn), lambda j, i: (i, j)),
                  pl.BlockSpec((8, tn), lambda j, i: (jnp.minimum((i + 1) * r8, last8), j)),
                  pl.BlockSpec((2, 8, tn), lambda j, i: (0, 0, j))],
        out_specs=(cur, pl.BlockSpec((2, 8, tn), lambda j, i: (0, 0, j))),
        name="conv_gate_bwd", compiler_params=_cp(("parallel", "arbitrary"), VMEM_LIMIT),
    )(hc, hc, hu, dact, dact, cw)


def _out_proj_dx_prep(dh1_bf, w_out_bf, att, lse, qa):
    S = att.shape[0]
    tm = _pick(S, (256,))

    def body(dh_ref, w_ref, o_ref, lse_ref, q_ref, dsg_ref, qb_ref, doa_ref):
        lane = lax.broadcasted_iota(jnp.int32, (tm, LANES), 1)
        dh = dh_ref[...]
        dsg_ref[...] = lax.dot_general(dh, w_ref[ATT_WIDTH:, :], _NT, preferred_element_type=F32)
        datt = lax.dot_general(dh, w_ref[:ATT_WIDTH, :], _NT, preferred_element_type=F32)
        for p in range(N_PAIRS):
            pc = slice(p * LANES, (p + 1) * LANES)
            do = datt[:, pc]
            prod = o_ref[:, pc] * do
            for hh in range(2):
                sel = (lane >= HEAD_DIM) if hh else (lane < HEAD_DIM)
                delta = jnp.sum(jnp.where(sel, prod, 0.0), axis=-1, keepdims=True)
                dod = pltpu.roll(do, HEAD_DIM, 1) if hh else do
                cols = slice((2 * p + hh) * LANES, (2 * p + hh + 1) * LANES)
                doa_ref[:, cols] = jnp.where(lane < HEAD_DIM, dod, _aug(lane, _split3f(-delta))).astype(BF16)
                lcol = p * LANES + hh * HEAD_DIM
                l3 = _split3f(-lse_ref[:, lcol:lcol + 1])
                augl = _aug(lane, [0.0] * 6 + l3).astype(BF16)
                qb_ref[:, cols] = jnp.where((lane >= HEAD_DIM + 6) & (lane < HEAD_DIM + 9), augl, q_ref[:, cols])

    half = pl.BlockSpec((tm, ATT_WIDTH), lambda i: (i, 0))
    wide = pl.BlockSpec((tm, N_HEADS * LANES), lambda i: (i, 0))
    return pl.pallas_call(
        body,
        out_shape=(jax.ShapeDtypeStruct((S, GMLP_WIDTH), F32), jax.ShapeDtypeStruct(qa.shape, BF16),
                   jax.ShapeDtypeStruct(qa.shape, BF16)),
        grid=(S // tm,),
        in_specs=[pl.BlockSpec((tm, D_MODEL), lambda i: (i, 0)), pl.BlockSpec((D_MODEL, D_MODEL), lambda i: (0, 0)),
                  half, half, wide],
        out_specs=(half, wide, wide),
        name="out_proj_dx_prep", compiler_params=_cp(("parallel",), VMEM_LIMIT),
    )(dh1_bf, w_out_bf, att, lse, qa)


def _attn_bwd(qb, ka, va, doa):
    S = qb.shape[0]
    tk = _pick(S, (512, 256))
    tq = tk
    nq = S // tq

    def pair(a, scale=None):
        lane = lax.broadcasted_iota(jnp.int32, (a.shape[0], LANES), 1)
        out = jnp.where(lane < HEAD_DIM, a[:, :LANES], pltpu.roll(a[:, LANES:], HEAD_DIM, 1))
        return out if scale is None else out * scale

    def head_lanes(a, col, sign, first):
        lane = lax.broadcasted_iota(jnp.int32, (a.shape[0], LANES), 1)
        return jnp.where(lane == first, sign * a[:, col:col + 1],
                         jnp.where(lane == first + 1, sign * a[:, LANES + col:LANES + col + 1], 0.0))

    def body(q_ref, do_ref, k_ref, v_ref, dqc_ref, dkc_ref, dvc_ref, dcq_ref, dck_ref, dq_ref, dka_ref, dva_ref):
        kb = pl.program_id(1)

        @pl.when(kb == 0)
        def _():
            dq_ref[...] = jnp.zeros_like(dq_ref)

        dka_ref[...] = jnp.zeros_like(dka_ref)
        dva_ref[...] = jnp.zeros_like(dva_ref)
        rid = lax.broadcasted_iota(jnp.int32, (tk, tq), 0)
        cid = lax.broadcasted_iota(jnp.int32, (tk, tq), 1)

        def sub_tile(qs, q_len, k_off, k_len, masked):
            keys = slice(k_off, k_off + k_len)
            for h in range(2):
                cols = slice(h * LANES, (h + 1) * LANES)
                qblk = q_ref[pl.ds(qs, q_len), cols]
                doblk = do_ref[pl.ds(qs, q_len), cols]
                kh = k_ref[keys, cols]
                p = jnp.exp(lax.dot_general(kh, qblk, _NT, preferred_element_type=F32))
                if masked:
                    p = jnp.where(cid >= rid, p, 0.0)
                ds = (p * lax.dot_general(v_ref[keys, cols], doblk, _NT, preferred_element_type=F32)).astype(BF16)
                dva_ref[keys, cols] += jnp.dot(p.astype(BF16), doblk, preferred_element_type=F32)
                dka_ref[keys, cols] += jnp.dot(ds, qblk, preferred_element_type=F32)
                dq_ref[pl.ds(qs, q_len), cols] += lax.dot_general(ds, kh, _TN, preferred_element_type=F32)

        sub_tile(pl.multiple_of(kb * tq, tq), tq, 0, tk, True)

        def step(qi, carry):
            sub_tile(pl.multiple_of(qi * tq, tq), tq, 0, tk, False)
            return carry

        lax.fori_loop(kb + 1, nq, step, 0)
        dka = dka_ref[...]
        dkc_ref[...] = pair(dka).astype(BF16)
        dvc_ref[...] = pair(dva_ref[...]).astype(BF16)
        first = 2 * pl.program_id(0)
        dck_ref[...] = head_lanes(dka, HEAD_DIM + 3, -1.0, first)

        @pl.when(kb == nq - 1)
        def _():
            dqa = dq_ref[...]
            dqc_ref[...] = pair(dqa, HEAD_DIM ** -0.5).astype(BF16)
            dcq_ref[...] = head_lanes(dqa, HEAD_DIM, 1.0, first)

    wide = 2 * LANES
    half = jax.ShapeDtypeStruct((S, ATT_WIDTH), BF16)
    slabs = jax.ShapeDtypeStruct((N_PAIRS, S, LANES), F32)
    return pl.pallas_call(
        body,
        out_shape=(half, half, half, slabs, slabs),
        grid=(N_PAIRS, nq),
        in_specs=[pl.BlockSpec((S, wide), lambda p, j: (0, p)), pl.BlockSpec((S, wide), lambda p, j: (0, p)),
                  pl.BlockSpec((tk, wide), lambda p, j: (j, p)), pl.BlockSpec((tk, wide), lambda p, j: (j, p))],
        out_specs=(pl.BlockSpec((S, LANES), lambda p, j: (0, p)), pl.BlockSpec((tk, LANES), lambda p, j: (j, p)),
                   pl.BlockSpec((tk, LANES), lambda p, j: (j, p)), pl.BlockSpec((None, S, LANES), lambda p, j: (p, 0, 0)),
                   pl.BlockSpec((None, tk, LANES), lambda p, j: (p, j, 0))),
        scratch_shapes=[pltpu.VMEM((S, wide), F32), pltpu.VMEM((tk, wide), F32), pltpu.VMEM((tk, wide), F32)],
        name="attn_bwd", compiler_params=_cp(("parallel", "arbitrary"), VMEM_LIMIT),
    )(qb, doa, ka, va)


def _gmlp_bwd(ug, dsg, gain, w_s, wt_s, bias_full):
    S = ug.shape[0]
    tm = _pick(S, (512, 256, 128))
    n_chunks = tm // CHUNK
    n_i = S // tm
    ones = _group_ones()
    nt = (((1,), (1,)), ((), ()))

    def body(ug_ref, dsg_ref, gain_ref, w_ref, wt_ref, bias_ref, ones_ref, dug_ref, dw_ref, dgain_ref, dbias_ref,
             dbacc_ref):
        i = pl.program_id(0)

        @pl.when(i == 0)
        def _():
            dw_ref[...] = jnp.zeros_like(dw_ref)
            dgain_ref[...] = jnp.zeros_like(dgain_ref)
            dbacc_ref[...] = jnp.zeros_like(dbacc_ref)

        ones_m = ones_ref[...]
        pu = ug_ref[:, :GMLP_WIDTH]
        pg = ug_ref[:, GMLP_WIDTH:]
        u = _gelu(pu)
        vr = _gelu(pg)
        ms = _dot3(vr * vr, ones_m) * (1.0 / GROUP_DIM)
        rinv = lax.rsqrt(ms + EPS)
        vhat = vr * rinv
        gain_v = gain_ref[...]
        vn = (vhat * gain_v).astype(BF16)
        mixed = _gmlp_mixed(vn, w_ref, bias_ref[...], n_chunks)
        dsg_v = dsg_ref[...]
        du = dsg_v * mixed
        dmixed = dsg_v * u
        dm_bf = dmixed.astype(BF16)
        lane = lax.broadcasted_iota(jnp.int32, (CHUNK, LANES), 1)
        row = lax.broadcasted_iota(jnp.int32, (CHUNK, CHUNK), 0)
        col = lax.broadcasted_iota(jnp.int32, (CHUNK, CHUNK), 1)
        wts = [jnp.where(col >= row, wt_ref[g], 0.0).astype(BF16) for g in range(N_GROUPS)]
        dvn_rows = []
        dbsum = jnp.zeros((CHUNK, GMLP_WIDTH), F32)
        for ci in range(n_chunks):
            rs = slice(ci * CHUNK, (ci + 1) * CHUNK)
            dbsum = dbsum + dmixed[rs, :]
            cols = []
            for pp in range(N_GROUPS // 2):
                cs = slice(pp * LANES, (pp + 1) * LANES)
                dm = dm_bf[rs, cs]
                dm_lo = jnp.where(lane < GROUP_DIM, dm, jnp.zeros_like(dm))
                dm_hi = jnp.where(lane >= GROUP_DIM, dm, jnp.zeros_like(dm))
                vb = vn[rs, cs]
                dw_ref[2 * pp] += lax.dot_general(dm_lo, vb, nt, preferred_element_type=F32)
                dw_ref[2 * pp + 1] += lax.dot_general(dm_hi, vb, nt, preferred_element_type=F32)
                cols.append(jnp.dot(wts[2 * pp], dm_lo, preferred_element_type=F32)
                            + jnp.dot(wts[2 * pp + 1], dm_hi, preferred_element_type=F32))
            dvn_rows.append(jnp.concatenate(cols, axis=1))
        dvn = jnp.concatenate(dvn_rows, axis=0)
        dbacc_ref[...] += dbsum
        dgain_ref[0:1, :] += jnp.sum(dvn * vhat, axis=0, keepdims=True)
        dvhat = dvn * gain_v
        gm = _dot3(dvhat * vhat, ones_m) * (1.0 / GROUP_DIM)
        dvr = rinv * (dvhat - vhat * gm)
        dug_ref[:, :GMLP_WIDTH] = (du * _gelu_grad(pu)).astype(BF16)
        dug_ref[:, GMLP_WIDTH:] = (dvr * _gelu_grad(pg)).astype(BF16)

        @pl.when(i == n_i - 1)
        def _():
            for g in range(N_GROUPS):
                dw_ref[g] = jnp.where(row >= col, dw_ref[g], 0.0)
            dbias_ref[...] = _dot3(dbacc_ref[...], ones_m)

    return pl.pallas_call(
        body,
        out_shape=(jax.ShapeDtypeStruct((S, 2 * GMLP_WIDTH), BF16), jax.ShapeDtypeStruct((N_GROUPS, CHUNK, CHUNK), F32),
                   jax.ShapeDtypeStruct((8, GMLP_WIDTH), F32), jax.ShapeDtypeStruct((CHUNK, GMLP_WIDTH), F32)),
        grid=(n_i,),
        in_specs=[pl.BlockSpec((tm, 2 * GMLP_WIDTH), lambda i: (i, 0)), pl.BlockSpec((tm, GMLP_WIDTH), lambda i: (i, 0)),
                  pl.BlockSpec((1, GMLP_WIDTH), lambda i: (0, 0)),
                  pl.BlockSpec((N_GROUPS, CHUNK, CHUNK), lambda i: (0, 0, 0)),
                  pl.BlockSpec((N_GROUPS, CHUNK, CHUNK), lambda i: (0, 0, 0)),
                  pl.BlockSpec((CHUNK, GMLP_WIDTH), lambda i: (0, 0)),
                  pl.BlockSpec((GMLP_WIDTH, GMLP_WIDTH), lambda i: (0, 0))],
        out_specs=(pl.BlockSpec((tm, 2 * GMLP_WIDTH), lambda i: (i, 0)),
                   pl.BlockSpec((N_GROUPS, CHUNK, CHUNK), lambda i: (0, 0, 0)),
                   pl.BlockSpec((8, GMLP_WIDTH), lambda i: (0, 0)),
                   pl.BlockSpec((CHUNK, GMLP_WIDTH), lambda i: (0, 0))),
        scratch_shapes=[pltpu.VMEM((CHUNK, GMLP_WIDTH), F32)],
        name="gmlp_bwd", compiler_params=_cp(("arbitrary",), VMEM_LIMIT),
    )(ug, dsg, gain, w_s, wt_s, bias_full, ones)


def _gate_bwd(dcq, dck, zf):
    S = zf.shape[0]
    tm = _pick(S, (256,))
    n_i = S // tm
    triu = (lax.broadcasted_iota(jnp.int32, (tm, tm), 0) <= lax.broadcasted_iota(jnp.int32, (tm, tm), 1)).astype(BF16)

    def body(dcq_ref, dck_ref, zf_ref, tri_ref, dzf_ref, dbf_ref, carry_ref):
        i = pl.program_id(0)

        @pl.when(i == 0)
        def _():
            carry_ref[...] = jnp.zeros_like(carry_ref)
            dbf_ref[...] = jnp.zeros_like(dbf_ref)

        lane = lax.broadcasted_iota(jnp.int32, (tm, LANES), 1)
        dc = dcq_ref[0] + dck_ref[0]
        for p in range(1, N_PAIRS):
            dc = dc + (dcq_ref[p] + dck_ref[p])
        dlf = _dot3l(tri_ref[...], dc) + carry_ref[0:1, :]
        carry_ref[0:1, :] = dlf[0:1, :]
        dz = jnp.where(lane < N_HEADS, dlf * _sigmoid(-zf_ref[...]), 0.0)
        dzf_ref[...] = dz.astype(BF16)
        dbf_ref[0:1, :] += jnp.sum(dz, axis=0, keepdims=True)

    return pl.pallas_call(
        body,
        out_shape=(jax.ShapeDtypeStruct((S, LANES), BF16), jax.ShapeDtypeStruct((8, LANES), F32)),
        grid=(n_i,),
        in_specs=[pl.BlockSpec((N_PAIRS, tm, LANES), lambda i: (0, n_i - 1 - i, 0)),
                  pl.BlockSpec((N_PAIRS, tm, LANES), lambda i: (0, n_i - 1 - i, 0)),
                  pl.BlockSpec((tm, LANES), lambda i: (n_i - 1 - i, 0)),
                  pl.BlockSpec((tm, tm), lambda i: (0, 0))],
        out_specs=(pl.BlockSpec((tm, LANES), lambda i: (n_i - 1 - i, 0)), pl.BlockSpec((8, LANES), lambda i: (0, 0))),
        scratch_shapes=[pltpu.VMEM((8, LANES), F32)],
        name="gate_bwd", compiler_params=_cp(("arbitrary",), VMEM_LIMIT),
    )(dcq, dck, zf, triu)


def _out_proj_fwd(att_bf, sg, w_out_bf, x, g_ffn):
    S = x.shape[0]
    tm = _pick(S, (512, 256))

    def body(a_ref, s_ref, w_ref, x_ref, g_ref, h_ref, hn_ref):
        h = (x_ref[...] + jnp.dot(a_ref[...], w_ref[:ATT_WIDTH, :], preferred_element_type=F32)
             + jnp.dot(s_ref[...], w_ref[ATT_WIDTH:, :], preferred_element_type=F32))
        h_ref[...] = h
        r = lax.rsqrt(jnp.mean(h * h, axis=-1, keepdims=True) + EPS)
        hn_ref[...] = ((h * r) * g_ref[...]).astype(BF16)

    row = pl.BlockSpec((tm, D_MODEL), lambda i: (i, 0))
    half = pl.BlockSpec((tm, ATT_WIDTH), lambda i: (i, 0))
    return pl.pallas_call(
        body, out_shape=(jax.ShapeDtypeStruct((S, D_MODEL), F32), jax.ShapeDtypeStruct((S, D_MODEL), BF16)),
        grid=(S // tm,),
        in_specs=[half, half, pl.BlockSpec((D_MODEL, D_MODEL), lambda i: (0, 0)), row,
                  pl.BlockSpec((1, D_MODEL), lambda i: (0, 0))],
        out_specs=(row, row), name="out_proj", compiler_params=_cp(("parallel",), VMEM_LIMIT),
    )(att_bf, sg, w_out_bf, x, g_ffn)


def _out_proj_dw(att_bf, sg, dh1_bf):
    S = att_bf.shape[0]
    tk = _pick(S, (1024, 512))

    def body(a_ref, s_ref, d_ref, o_ref):
        k = pl.program_id(0)

        @pl.when(k == 0)
        def _():
            o_ref[...] = jnp.zeros_like(o_ref)

        d = d_ref[...]
        o_ref[:ATT_WIDTH, :] += lax.dot_general(a_ref[...], d, _TN, preferred_element_type=F32)
        o_ref[ATT_WIDTH:, :] += lax.dot_general(s_ref[...], d, _TN, preferred_element_type=F32)

    half = pl.BlockSpec((tk, ATT_WIDTH), lambda k: (k, 0))
    return pl.pallas_call(
        body, out_shape=jax.ShapeDtypeStruct((D_MODEL, D_MODEL), F32), grid=(S // tk,),
        in_specs=[half, half, pl.BlockSpec((tk, D_MODEL), lambda k: (k, 0))],
        out_specs=pl.BlockSpec((D_MODEL, D_MODEL), lambda k: (0, 0)),
        name="out_proj_dw", compiler_params=_cp(("arbitrary",), VMEM_LIMIT),
    )(att_bf, sg, dh1_bf)


_IN_PIECES = ((0, ATT_WIDTH), (ATT_WIDTH, ATT_WIDTH), (2 * ATT_WIDTH, ATT_WIDTH), (QKV, 2 * GMLP_WIDTH), (UG_END, LANES))


def _inproj_bwd_dx(pieces, w_pad, x, g_mix, dh1):
    S = x.shape[0]
    tm = _pick(S, (512, 256))

    def body(*refs):
        p_refs, (w_ref, x_ref, g_ref, r_ref, dx_ref, dg_ref) = refs[:5], refs[5:]
        i = pl.program_id(0)

        @pl.when(i == 0)
        def _():
            dg_ref[...] = jnp.zeros_like(dg_ref)

        dxn = None
        for p_ref, (c0, width) in zip(p_refs, _IN_PIECES):
            part = jnp.dot(p_ref[...], w_ref[c0:c0 + width, :], preferred_element_type=F32)
            dxn = part if dxn is None else dxn + part
        xf = x_ref[...]
        r = lax.rsqrt(jnp.mean(xf * xf, axis=-1, keepdims=True) + EPS)
        xhat = xf * r
        dg_ref[0:1, :] += jnp.sum(dxn * xhat, axis=0, keepdims=True)
        dhat = dxn * g_ref[...]
        dx_ref[...] = r_ref[...] + r * (dhat - xhat * jnp.mean(dhat * xhat, axis=-1, keepdims=True))

    row = pl.BlockSpec((tm, D_MODEL), lambda i: (i, 0))
    return pl.pallas_call(
        body, out_shape=(jax.ShapeDtypeStruct((S, D_MODEL), F32), jax.ShapeDtypeStruct((8, D_MODEL), F32)),
        grid=(S // tm,),
        in_specs=[pl.BlockSpec((tm, width), lambda i: (i, 0)) for _, width in _IN_PIECES]
        + [pl.BlockSpec((IN_PAD, D_MODEL), lambda i: (0, 0)), row, pl.BlockSpec((1, D_MODEL), lambda i: (0, 0)), row],
        out_specs=(row, pl.BlockSpec((8, D_MODEL), lambda i: (0, 0))),
        name="in_proj_dx", compiler_params=_cp(("arbitrary",), VMEM_LIMIT),
    )(*pieces, w_pad, x, g_mix, dh1)


def _inproj_bwd_dw(xn, pieces):
    S = xn.shape[0]
    tk = _pick(S, (1024, 512))

    def body(*refs):
        x_ref, p_refs, o_ref = refs[0], refs[1:6], refs[6]
        k = pl.program_id(0)

        @pl.when(k == 0)
        def _():
            o_ref[...] = jnp.zeros_like(o_ref)

        xb = x_ref[...]
        for p_ref, (c0, width) in zip(p_refs, _IN_PIECES):
            o_ref[:, c0:c0 + width] += lax.dot_general(xb, p_ref[...], _TN, preferred_element_type=F32)

    return pl.pallas_call(
        body, out_shape=jax.ShapeDtypeStruct((D_MODEL, IN_PAD), F32), grid=(S // tk,),
        in_specs=[pl.BlockSpec((tk, D_MODEL), lambda k: (k, 0))]
        + [pl.BlockSpec((tk, width), lambda k: (k, 0)) for _, width in _IN_PIECES],
        out_specs=pl.BlockSpec((D_MODEL, IN_PAD), lambda k: (0, 0)),
        name="in_proj_dw", compiler_params=_cp(("arbitrary",), VMEM_LIMIT),
    )(xn, *pieces)


def _adamw(w, m, v, parts, name):
    R, C = w.shape[-2:]
    tr = R
    for cand in (256, 128, 64, 32, 16, 8):
        if R % cand == 0 and R > cand:
            tr = cand
            break
    c1 = 1.0 / (1.0 - ADAM_B1 ** ADAM_STEP)
    c2 = 1.0 / (1.0 - ADAM_B2 ** ADAM_STEP)

    def body(w_ref, m_ref, v_ref, p_ref, g_ref, d_ref, nm_ref, nv_ref):
        g = p_ref[0].astype(F32)
        for j in range(1, N_DEV):
            g = g + p_ref[j].astype(F32)
        g_ref[...] = g
        nm = ADAM_B1 * m_ref[...] + (1.0 - ADAM_B1) * g
        nv = ADAM_B2 * v_ref[...] + (1.0 - ADAM_B2) * (g * g)
        nm_ref[...] = nm
        nv_ref[...] = nv
        d_ref[...] = -ADAM_LR * ((nm * c1) / (jnp.sqrt(nv * c2) + ADAM_EPS) + ADAM_WD * w_ref[...])

    if w.ndim == 3:
        spec = pl.BlockSpec((None, tr, C), lambda i: (0, i, 0))
    else:
        spec = pl.BlockSpec((tr, C), lambda i: (i, 0))
    shp = jax.ShapeDtypeStruct(w.shape, F32)
    return pl.pallas_call(
        body, out_shape=(shp, shp, shp, shp), grid=(R // tr,),
        in_specs=[spec, spec, spec, pl.BlockSpec((N_DEV, tr, C), lambda i: (0, i, 0))],
        out_specs=(spec, spec, spec, spec),
        name=name, compiler_params=_cp(("parallel",), VMEM_LIMIT),
    )(w, m, v, parts)


def _place():
    x, y, c = lax.axis_index("x"), lax.axis_index("y"), lax.axis_index("c")
    return x, y, c


def _all_gather(blocks, name):
    n = len(blocks)

    def body(*refs):
        ins, outs = refs[:n], refs[n:2 * n]
        send_sems, recv_sems, local_sems = refs[2 * n:]
        x, y, c = _place()
        me, sibling = (x, y, c), (x, y, 1 - c)
        chips = [(1 - x, y), (x, 1 - y), (1 - x, 1 - y)]
        sends = []
        for a in range(n):
            out = outs[a]

            def slot(px, py, pc, out=out):
                return out.at[4 * px + 2 * py + pc]

            def copy(k, block, to, src=None, a=a, slot=slot):
                return pltpu.make_async_remote_copy(
                    src_ref=slot(*block) if src is None else src, dst_ref=slot(*block),
                    send_sem=send_sems.at[a, k], recv_sem=recv_sems.at[a, k], device_id=to, device_id_type=MESH)

            mine = pltpu.make_async_copy(ins[a], slot(*me), local_sems.at[a])
            mine.start()
            first = [copy(0, me, sibling, src=ins[a])]
            first += [copy(1 + j, me, (*chip, c), src=ins[a]) for j, chip in enumerate(chips)]
            for cp in first:
                cp.start()
            sends.append((mine, first, copy))
        for a in range(n):
            mine, first, copy = sends[a]
            passed = [copy(4 + j, (*chip, c), sibling) for j, chip in enumerate(chips)]
            for j, chip in enumerate(chips):
                copy(1 + j, (*chip, c), me).wait_recv()
                passed[j].start()
            copy(0, sibling, me).wait_recv()
            for j, chip in enumerate(chips):
                copy(4 + j, (*chip, 1 - c), me).wait_recv()
            for cp in first + passed:
                cp.wait_send()
            mine.wait()

    any_spec = pl.BlockSpec(memory_space=pl.ANY)
    return pl.pallas_call(
        body, out_shape=tuple(jax.ShapeDtypeStruct((N_DEV,) + b.shape, b.dtype) for b in blocks),
        in_specs=[any_spec] * n, out_specs=tuple([any_spec] * n),
        scratch_shapes=[pltpu.SemaphoreType.DMA((n, 7)), pltpu.SemaphoreType.DMA((n, 7)), pltpu.SemaphoreType.DMA((n,))],
        name=name,
    )(*blocks)


_HBM = pl.BlockSpec(memory_space=pltpu.HBM)
_SEM = pl.BlockSpec(memory_space=pltpu.SEMAPHORE)
_EFFECT = pltpu.SideEffectType.DATAFLOW_SIDE_EFFECTING


def _peers(x, y, c):
    out = []
    for k in range(1, N_DEV):
        px, py, pc = x ^ ((k >> 2) & 1), y ^ ((k >> 1) & 1), c ^ (k & 1)
        out.append((k, (px, py, pc), 4 * px + 2 * py + pc))
    return out


def _xchg_copies(src_refs, land_refs, send_sems, recv_sems, scatter):
    x, y, c = _place()
    me = 4 * x + 2 * y + c
    copies = []
    for a, (src, land) in enumerate(zip(src_refs, land_refs)):
        for k, place, idx in _peers(x, y, c):
            j = a * (N_DEV - 1) + k - 1
            copies.append(pltpu.make_async_remote_copy(
                src_ref=src.at[idx] if scatter[a] else src, dst_ref=land.at[me],
                send_sem=send_sems[j], recv_sem=recv_sems[j], device_id=place, device_id_type=MESH))
    return copies


def _xchg_start(srcs, scatter, name):
    n = len(srcs)
    lands = [lax.empty((N_DEV,) + (s.shape[1:] if sc else s.shape), s.dtype) for s, sc in zip(srcs, scatter)]

    ns = n * (N_DEV - 1)

    def body(*refs):
        sems = refs[2 * n:2 * n + 2 * ns]
        for cp in _xchg_copies(refs[:n], refs[n:2 * n], sems[:ns], sems[ns:], scatter):
            cp.start()
        token = refs[-1]
        token[...] = jnp.zeros_like(token)

    both = list(srcs) + lands
    res = pl.pallas_call(
        body, name=name,
        out_shape=(*[pltpu.SemaphoreType.DMA(())] * (2 * ns),
                   *[pltpu.HBM(a.shape, a.dtype) for a in both], jax.ShapeDtypeStruct((8, LANES), F32)),
        in_specs=[_HBM] * (2 * n),
        out_specs=(*([_SEM] * (2 * ns)), *([_HBM] * (2 * n)), pl.BlockSpec(memory_space=pltpu.VMEM)),
        input_output_aliases={i: 2 * ns + i for i in range(2 * n)},
        compiler_params=pltpu.CompilerParams(has_side_effects=_EFFECT),
    )(*[pltpu.with_memory_space_constraint(a, pltpu.HBM) for a in both])
    return (tuple(res[:2 * ns]), tuple(res[2 * ns:2 * ns + 2 * n])), res[-1]


def _xchg_wait(handle, scatter, after, name):
    sems, thru = handle
    n = len(thru) // 2
    ns = len(sems) // 2

    def body(*refs):
        got = refs[2 * n:2 * n + 2 * ns]
        for cp in _xchg_copies(refs[:n], refs[n:2 * n], got[:ns], got[ns:], scatter):
            cp.wait_send()
            cp.wait_recv()

    outs = pl.pallas_call(
        body, name=name, out_shape=tuple(pltpu.HBM(a.shape, a.dtype) for a in thru),
        in_specs=[_HBM] * (2 * n) + [_SEM] * (2 * ns) + [pl.BlockSpec(memory_space=pl.ANY)],
        out_specs=tuple([_HBM] * (2 * n)), input_output_aliases={i: i for i in range(2 * n)},
        compiler_params=pltpu.CompilerParams(has_side_effects=_EFFECT),
    )(*thru, *sems, after)
    return outs[:n], outs[n:]


def _tie(a, token):
    return a if token is None else a + token[0, 0].astype(a.dtype)


def _rows128(a):
    flat = a.reshape(-1)
    rows = -(-flat.shape[0] // LANES)
    rows = -(-rows // 8) * 8
    return jnp.pad(flat, (0, rows * LANES - flat.shape[0])).reshape(rows, LANES)


def _local_step(x, target, norm_mix_g, w_in_t, b_forget, gmlp_norm_g, w_spatial, b_spatial, norm_ffn_g, conv_b,
                norm_final_g, rest_fn, send_fn, small_fn, token=None):
    f = D_FF
    g_mix = norm_mix_g.reshape(1, D_MODEL)
    w_pad = jnp.pad(w_in_t, ((0, IN_PAD - IN_COLS), (0, 0)))
    bf_pad = jnp.pad(b_forget.reshape(1, N_HEADS), ((0, 0), (0, LANES - N_HEADS)))
    xn, qa, ka, va, ug, zf = _inproj_fwd(x, _tie(g_mix, token), w_pad, bf_pad)
    bias_full = jnp.repeat(b_spatial.reshape(N_GROUPS, CHUNK).T, GROUP_DIM, axis=1)
    w_s = w_spatial.reshape(N_GROUPS, CHUNK, CHUNK)
    gain = gmlp_norm_g.reshape(1, GMLP_WIDTH)
    sg = _gmlp_fwd(ug, gain, w_s, bias_full)
    att, lse, att_bf = _attn_fwd(qa, ka, va)
    w_out_bf, w_up_bf, conv_w, w_down_bf = rest_fn(att_bf)
    g_ffn = norm_ffn_g.reshape(1, D_MODEL)
    h1, hn = _out_proj_fwd(att_bf, sg, w_out_bf, x, g_ffn)
    cw = jnp.pad(conv_w.reshape(3, 2, f).transpose(1, 0, 2), ((0, 0), (0, 5), (0, 0)))
    cb = conv_b.reshape(2, 1, f)
    hu, hc, act = _ffn_up_conv(hn, w_up_bf, cw, cb)
    loss_blk, dh2, dh2_bf, dg_final = _ffn_down_loss(act, w_down_bf, h1, norm_final_g.reshape(1, D_MODEL), target)
    dw_down = _mm(act, dh2_bf, mode="tn", out_dtype=F32, tm=1408, tn=1024, tk=2048, name="ffn_down_dw")
    dact = _mm(dh2_bf, w_down_bf, mode="nt", out_dtype=F32, tm=1024, tn=1408, tk=1024, outer="j", name="ffn_down_dx")
    dhu, dcw = _conv_gate_bwd(hc, hu, dact, _tie(cw, send_fn("w_down", dw_down)))
    dw_up = _mm(hn, dhu, mode="tn", out_dtype=F32, tm=1024, tn=1408, tk=2048, b_halves=True, outer="j", name="ffn_up_dw")
    dh1, dh1_bf, dg_ffn = _ffn_up_dx_rms(dhu, w_up_bf, h1, _tie(g_ffn, send_fn("w_up", dw_up)), dh2)
    dsg, qb, doa = _out_proj_dx_prep(dh1_bf, w_out_bf, att, lse, qa)
    dw_out = _out_proj_dw(att_bf, sg, dh1_bf)
    dq, dk, dv, dcq, dck = _attn_bwd(qb, ka, va, doa)
    wt_s = w_s.transpose(0, 2, 1)
    dug, dw_s, dgain, dbias = _gmlp_bwd(ug, dsg, _tie(gain, send_fn("w_out", dw_out)), w_s, wt_s, bias_full)
    dzf, dbf = _gate_bwd(dcq, dck, zf)
    grad_x, dg_mix = _inproj_bwd_dx((dq, dk, dv, dug, dzf), w_pad, x, g_mix, dh1)
    grads = dict(
        norm_mix_g=dg_mix[0:1, :],
        b_forget=dbf[0:1, :N_HEADS],
        gmlp_norm_g=dgain[0:1, :],
        w_spatial=dw_s,
        b_spatial=dbias[:, ::GROUP_DIM].T,
        norm_ffn_g=dg_ffn[0:1, :],
        conv_w=dcw[:, 0:3, :].transpose(1, 0, 2).reshape(3, 2 * f),
        conv_b=dcw[:, 3, :].reshape(1, 2 * f),
        norm_final_g=dg_final[0, :],
    )
    token = small_fn(loss_blk[0, 0], grads)
    dw_in = _inproj_bwd_dw(xn, (dq, dk, dv, dug, _tie(dzf, token)))
    return grad_x, send_fn("w_in", dw_in[:, :IN_COLS])


SMALL = ("norm_mix_g", "b_forget", "gmlp_norm_g", "w_spatial", "b_spatial", "norm_ffn_g", "conv_b", "norm_final_g")


def kernel(x, norm_mix_g, w_in, b_forget, gmlp_norm_g, w_spatial, b_spatial, w_out, norm_ffn_g, w_up, conv_w, conv_b, w_down, norm_final_g, loss_target, m_norm_mix_g, m_w_in, m_b_forget, m_gmlp_norm_g, m_w_spatial, m_b_spatial, m_w_out, m_norm_ffn_g, m_w_up, m_conv_w, m_conv_b, m_w_down, m_norm_final_g, v_norm_mix_g, v_w_in, v_b_forget, v_gmlp_norm_g, v_w_spatial, v_b_spatial, v_w_out, v_norm_ffn_g, v_w_up, v_conv_w, v_conv_b, v_w_down, v_norm_final_g):
    weights = dict(norm_mix_g=norm_mix_g, w_in=w_in, b_forget=b_forget, gmlp_norm_g=gmlp_norm_g, w_spatial=w_spatial,
                   b_spatial=b_spatial, w_out=w_out, norm_ffn_g=norm_ffn_g, w_up=w_up, conv_w=conv_w, conv_b=conv_b,
                   w_down=w_down, norm_final_g=norm_final_g)
    m_in = dict(norm_mix_g=m_norm_mix_g, w_in=m_w_in, b_forget=m_b_forget, gmlp_norm_g=m_gmlp_norm_g,
                w_spatial=m_w_spatial, b_spatial=m_b_spatial, w_out=m_w_out, norm_ffn_g=m_norm_ffn_g, w_up=m_w_up,
                conv_w=m_conv_w, conv_b=m_conv_b, w_down=m_w_down, norm_final_g=m_norm_final_g)
    v_in = dict(norm_mix_g=v_norm_mix_g, w_in=v_w_in, b_forget=v_b_forget, gmlp_norm_g=v_gmlp_norm_g,
                w_spatial=v_w_spatial, b_spatial=v_b_spatial, w_out=v_w_out, norm_ffn_g=v_norm_ffn_g, w_up=v_w_up,
                conv_w=v_conv_w, conv_b=v_conv_b, w_down=v_w_down, norm_final_g=v_norm_final_g)
    order = list(weights)
    me = 4 * lax.axis_index("x") + 2 * lax.axis_index("y") + lax.axis_index("c")
    n_in, n_up = w_in.shape[2], w_up.shape[2]
    r_out, r_down = w_out.shape[1], w_down.shape[1]

    def with_mine(landed, mine):
        return lax.dynamic_update_index_in_dim(landed, mine, me, 0)

    up_blk = w_up[0].T.astype(BF16)
    out_blk = w_out[0].astype(BF16)
    down_blk = w_down[0].astype(BF16)
    taps_blk = jnp.pad(conv_w[0], ((0, 5), (0, 0)))
    (in_all,) = _all_gather([w_in[0].T.astype(BF16)], "gather_w_in")
    in_all, rest_blocks = lax.optimization_barrier((in_all, [up_blk, out_blk, down_blk, taps_blk]))
    rest_handle, token = _xchg_start(rest_blocks, [False] * 4, "gather_rest_start")
    w_in_t = in_all.reshape(N_DEV * n_in, D_MODEL)

    def rest_fn(after):
        mine, landed = _xchg_wait(rest_handle, [False] * 4, after, "gather_rest_wait")
        up_all, out_all, down_all, taps_all = [with_mine(l, b) for l, b in zip(landed, mine)]
        return (out_all.reshape(N_DEV * r_out, D_MODEL), up_all.reshape(N_DEV * n_up, D_MODEL),
                taps_all[:, :3, :].transpose(1, 0, 2).reshape(3, N_DEV * n_up),
                down_all.reshape(N_DEV * r_down, D_MODEL))

    sent = {}

    def send_fn(name, grad):
        if name == "w_in":
            parts = grad.reshape(D_MODEL, N_DEV, -1).transpose(1, 0, 2).astype(BF16)
        elif name == "w_up":
            parts = grad.reshape(D_MODEL, N_DEV, -1).transpose(1, 0, 2)
        else:
            parts = grad.reshape(N_DEV, -1, D_MODEL)
        sent[name], tok = _xchg_start([parts], [True], "scatter_" + name + "_start")
        return tok

    small = {}

    def small_fn(loss_local, g):
        loss_rows = jnp.pad(loss_local.reshape(1, 1), ((0, 31), (0, LANES - 1)))
        packed = [_rows128(g[k]) for k in SMALL] + [loss_rows, _rows128(g["conv_w"])]
        small["sizes"] = [p.shape[0] for p in packed]
        small["handle"], tok = _xchg_start([jnp.concatenate(packed, axis=0)], [False], "gather_small_start")
        return tok

    grad_x, after = _local_step(
        x[0], loss_target[0], norm_mix_g, w_in_t, b_forget, gmlp_norm_g, w_spatial, b_spatial, norm_ffn_g, conv_b,
        norm_final_g, rest_fn, send_fn, small_fn, token)

    outs = {}

    def update_big(name, after):
        (parts,), (landed,) = _xchg_wait(sent[name], [True], after, "scatter_" + name + "_wait")
        got = with_mine(landed, lax.dynamic_index_in_dim(parts, me, 0, keepdims=False))
        outs[name] = tuple(_adamw(weights[name], m_in[name], v_in[name], got, "adamw_" + name))
        return outs[name][0]

    for name in ("w_down", "w_up", "w_out"):
        after = update_big(name, after)

    (mine,), (landed,) = _xchg_wait(small["handle"], [False], after, "gather_small_wait")
    small_all = with_mine(landed, mine)
    sizes = small["sizes"]
    n_small_rows = sum(sizes[:-2])

    def pack(src):
        return jnp.concatenate([_rows128(src[k]) for k in SMALL] + [jnp.zeros((sizes[-2], LANES), F32)], axis=0)

    n_adam_rows = n_small_rows + sizes[-2]
    sg_, sd_, sm_, sv_ = _adamw(pack(weights), pack(m_in), pack(v_in), small_all[:, :n_adam_rows, :], "adamw_small")
    loss = sg_[n_small_rows, 0]
    off = 0
    for k, rows in zip(SMALL, sizes[:-2]):
        shp = weights[k].shape
        cnt = math.prod(shp)
        outs[k] = tuple(a[off:off + rows].reshape(-1)[:cnt].reshape(shp) for a in (sg_, sd_, sm_, sv_))
        off += rows
    taps_parts = small_all[:, n_adam_rows:, :].reshape(N_DEV, -1)[:, :3 * N_DEV * n_up].reshape(N_DEV, 3, N_DEV * n_up)
    taps_mine = lax.dynamic_slice_in_dim(taps_parts, me * n_up, n_up, axis=2)
    taps_mine = jnp.pad(taps_mine, ((0, 0), (0, 5), (0, 0)))

    def pad8(a):
        return jnp.pad(a[0], ((0, 5), (0, 0)))

    res = _adamw(pad8(conv_w), pad8(m_conv_w), pad8(v_conv_w), taps_mine, "adamw_conv_w")
    outs["conv_w"] = tuple(a[:3][None] for a in res)
    update_big("w_in", sg_)

    return (loss, grad_x[None], *[outs[k][0] for k in order], *[outs[k][1] for k in order],
            *[outs[k][2] for k in order], *[outs[k][3] for k in order])
```

```python
import functools
import math

import jax
import jax.numpy as jnp
from jax import lax
from jax.experimental import pallas as pl
from jax.experimental.pallas import tpu as pltpu

F32 = jnp.float32
BF16 = jnp.bfloat16

N_DEV = 8
D_MODEL = 1024
ATT_WIDTH = 512
GMLP_WIDTH = 512
HEAD_DIM = 64
N_HEADS = 8
N_PAIRS = 4
N_GROUPS = 8
GROUP_DIM = 64
CHUNK = 128
D_FF = 2816
IN_COLS = 2568
IN_PAD = 2688
QKV = 1536
UG_END = 2560
EPS = 1e-6
LANES = 128

ADAM_LR = 0.001
ADAM_B1 = 0.9
ADAM_B2 = 0.999
ADAM_EPS = 1e-08
ADAM_WD = 0.01
ADAM_STEP = 10

ATT_TQ = 1024
ATT_TK = 1024
FFN_TM, FFN_TN = 512, 1408
CONV_TM, CONV_TN = 256, 1408
VMEM_LIMIT = 56 * 1024 * 1024
MESH = pl.DeviceIdType.MESH


def _cp(sem, vmem=None):
    return pltpu.CompilerParams(dimension_semantics=sem, vmem_limit_bytes=vmem)


def _pick(n, prefs):
    for p in prefs:
        if n % p == 0:
            return p
    return n


def _split3(x):
    hi = x.astype(BF16)
    r1 = x - hi.astype(F32)
    mid = r1.astype(BF16)
    lo = (r1 - mid.astype(F32)).astype(BF16)
    return hi, mid, lo


def _dot3(x, ones_bf):
    hi, mid, lo = _split3(x)
    d = functools.partial(jnp.dot, preferred_element_type=F32)
    return d(hi, ones_bf) + d(mid, ones_bf) + d(lo, ones_bf)


def _dot3l(ones_bf, x):
    n = x.shape[1]
    y = jnp.dot(ones_bf, jnp.concatenate(_split3(x), axis=1), preferred_element_type=F32)
    return y[:, :n] + y[:, n:2 * n] + y[:, 2 * n:]


def _gelu(x):
    k = math.sqrt(2.0 / math.pi)
    t = jnp.tanh(k * (x + 0.044715 * (x * x * x)))
    return 0.5 * x * (1.0 + t)


def _gelu_grad(x):
    k = math.sqrt(2.0 / math.pi)
    x2 = x * x
    t = jnp.tanh(k * (x + 0.044715 * (x2 * x)))
    return 0.5 * (1.0 + t) + 0.5 * x * (1.0 - t * t) * (k * (1.0 + 3.0 * 0.044715 * x2))


def _sigmoid(x):
    return 1.0 / (1.0 + jnp.exp(-x))


def _mm(a, b, *, mode, out_dtype, tm, tn, tk, name, res=None, a_halves=False, b_halves=False,
        out_halves=False, outer="i"):
    if mode == "tn":
        K, M = a.shape[-2], a.shape[-1] * (2 if a_halves else 1)
    else:
        M, K = a.shape[-2], a.shape[-1] * (2 if a_halves else 1)
    if mode == "nt":
        N = b.shape[-2]
        assert b.shape[-1] == K
    else:
        N = b.shape[-1] * (2 if b_halves else 1)
    tm, tn, tk = min(tm, M), min(tn, N), min(tk, K)
    assert M % tm == 0 and N % tn == 0 and K % tk == 0, (name, M, N, K, tm, tn, tk)
    nm, nn, nk = M // tm, N // tn, K // tk

    def ij(g0, g1):
        return (g0, g1) if outer == "i" else (g1, g0)

    if mode == "nn":
        dims = (((1,), (0,)), ((), ()))
        if a_halves:
            nkh = nk // 2
            a_spec = pl.BlockSpec((None, tm, tk), lambda g0, g1, k: (k // nkh, ij(g0, g1)[0], k % nkh))
        else:
            a_spec = pl.BlockSpec((tm, tk), lambda g0, g1, k: (ij(g0, g1)[0], k))
        b_spec = pl.BlockSpec((tk, tn), lambda g0, g1, k: (k, ij(g0, g1)[1]))
    elif mode == "nt":
        dims = (((1,), (1,)), ((), ()))
        if a_halves:
            nkh = nk // 2
            a_spec = pl.BlockSpec((None, tm, tk), lambda g0, g1, k: (k // nkh, ij(g0, g1)[0], k % nkh))
        else:
            a_spec = pl.BlockSpec((tm, tk), lambda g0, g1, k: (ij(g0, g1)[0], k))
        b_spec = pl.BlockSpec((tn, tk), lambda g0, g1, k: (ij(g0, g1)[1], k))
    else:
        dims = (((0,), (0,)), ((), ()))
        if a_halves:
            nmh = nm // 2
            a_spec = pl.BlockSpec((None, tk, tm), lambda g0, g1, k: (ij(g0, g1)[0] // nmh, k, ij(g0, g1)[0] % nmh))
        else:
            a_spec = pl.BlockSpec((tk, tm), lambda g0, g1, k: (k, ij(g0, g1)[0]))
        if b_halves:
            nnh = nn // 2
            b_spec = pl.BlockSpec((None, tk, tn), lambda g0, g1, k: (ij(g0, g1)[1] // nnh, k, ij(g0, g1)[1] % nnh))
        else:
            b_spec = pl.BlockSpec((tk, tn), lambda g0, g1, k: (k, ij(g0, g1)[1]))
    if out_halves:
        nnh = nn // 2
        o_spec = pl.BlockSpec((None, tm, tn), lambda g0, g1, k: (ij(g0, g1)[1] // nnh, ij(g0, g1)[0], ij(g0, g1)[1] % nnh))
        o_shape = jax.ShapeDtypeStruct((2, M, N // 2), out_dtype)
    else:
        o_spec = pl.BlockSpec((tm, tn), lambda g0, g1, k: ij(g0, g1))
        o_shape = jax.ShapeDtypeStruct((M, N), out_dtype)
    in_specs = [a_spec, b_spec]
    args = [a, b]
    if res is not None:
        in_specs.append(pl.BlockSpec((tm, tn), lambda g0, g1, k: ij(g0, g1)))
        args.append(res)

    def body(*refs):
        if res is not None:
            a_ref, b_ref, r_ref, o_ref = refs[:4]
        else:
            a_ref, b_ref, o_ref = refs[:3]
            r_ref = None
        part = lax.dot_general(a_ref[...], b_ref[...], dims, preferred_element_type=F32)
        if nk == 1:
            if r_ref is not None:
                part = part + r_ref[...]
            o_ref[...] = part.astype(out_dtype)
            return
        acc_ref = refs[-1]
        k = pl.program_id(2)

        @pl.when(k == 0)
        def _():
            acc_ref[...] = part

        @pl.when(k > 0)
        def _():
            acc_ref[...] += part

        @pl.when(k == nk - 1)
        def _():
            tot = acc_ref[...]
            if r_ref is not None:
                tot = tot + r_ref[...]
            o_ref[...] = tot.astype(out_dtype)

    grid = (nm, nn, nk) if outer == "i" else (nn, nm, nk)
    scratch = [] if nk == 1 else [pltpu.VMEM((tm, tn), F32)]
    return pl.pallas_call(
        body, out_shape=o_shape, grid=grid, in_specs=in_specs, out_specs=o_spec, scratch_shapes=scratch,
        name=name, compiler_params=_cp(("parallel", "parallel", "arbitrary"), VMEM_LIMIT),
    )(*args)


def _aug(lane, terms):
    out = 0.0
    for j, t in enumerate(terms):
        out = jnp.where(lane == HEAD_DIM + j, t, out)
    return out


def _split3f(x):
    hi, mid, lo = _split3(x)
    return [hi.astype(F32), mid.astype(F32), lo.astype(F32)]


def _inproj_fwd(x, g_mix, w_pad, bf_pad):
    S = x.shape[0]
    tm = _pick(S, (512, 256))
    tri = (lax.broadcasted_iota(jnp.int32, (tm, tm), 0) >= lax.broadcasted_iota(jnp.int32, (tm, tm), 1)).astype(BF16)

    def body(x_ref, g_ref, w_ref, bf_ref, tri_ref, put_ref, one_ref, xn_ref, qa_ref, ka_ref, va_ref, ug_ref, zf_ref,
             carry_ref):
        i = pl.program_id(0)

        @pl.when(i == 0)
        def _():
            carry_ref[...] = jnp.zeros_like(carry_ref)

        xf = x_ref[...]
        r = lax.rsqrt(jnp.mean(xf * xf, axis=-1, keepdims=True) + EPS)
        xn = ((xf * r) * g_ref[...]).astype(BF16)
        xn_ref[...] = xn
        proj = lax.dot_general(xn, w_ref[...], _NT, preferred_element_type=F32)
        ug_ref[...] = proj[:, QKV:UG_END]
        zf = proj[:, UG_END:] + bf_ref[...]
        zf_ref[...] = zf
        lf = jnp.minimum(zf, 0.0) - jnp.log(1.0 + jnp.exp(-jnp.abs(zf)))
        c = _dot3l(tri_ref[...], lf) + carry_ref[0:1, :]
        carry_ref[0:1, :] = c[tm - 1:tm, :]
        c3 = jnp.concatenate(_split3(c), axis=1)
        aug_q = jnp.dot(c3, put_ref[0], preferred_element_type=F32) + one_ref[0:1, :]
        aug_k = jnp.dot(c3, put_ref[1], preferred_element_type=F32) + one_ref[1:2, :]
        lane = lax.broadcasted_iota(jnp.int32, (tm, LANES), 1)
        for h in range(N_HEADS):
            p, odd = h // 2, h % 2

            def head(base, scale=None, p=p, odd=odd):
                blk = proj[:, base + p * LANES:base + (p + 1) * LANES]
                if scale is not None:
                    blk = blk * scale
                return pltpu.roll(blk, HEAD_DIM, 1) if odd else blk

            cols = slice(h * LANES, (h + 1) * LANES)
            qa_ref[:, cols] = jnp.where(lane < HEAD_DIM, head(0, HEAD_DIM ** -0.5), aug_q[:, cols]).astype(BF16)
            ka_ref[:, cols] = jnp.where(lane < HEAD_DIM, head(ATT_WIDTH), aug_k[:, cols]).astype(BF16)
            va_ref[:, cols] = jnp.where(lane < HEAD_DIM, head(2 * ATT_WIDTH), one_ref[2:3, cols]).astype(BF16)

    wide = N_HEADS * LANES
    src = lax.broadcasted_iota(jnp.int32, (3 * LANES, wide), 0)
    col = lax.broadcasted_iota(jnp.int32, (3 * LANES, wide), 1)
    hd, term = src % LANES, src // LANES
    to_q = (col == hd * LANES + HEAD_DIM + term) & (hd < N_HEADS)
    to_k = (col == hd * LANES + HEAD_DIM + 3 + term) & (hd < N_HEADS)
    put = jnp.stack([to_q.astype(BF16), -to_k.astype(BF16)])
    off = lax.broadcasted_iota(jnp.int32, (8, wide), 1) % LANES - HEAD_DIM
    row = lax.broadcasted_iota(jnp.int32, (8, wide), 0)
    q_one = (off >= 3) & (off < 6)
    k_one = ((off >= 0) & (off < 3)) | ((off >= 6) & (off < 9))
    v_one = (off >= 0) & (off < 3)
    ones = jnp.where(row == 0, q_one, jnp.where(row == 1, k_one, (row == 2) & v_one)).astype(F32)
    return pl.pallas_call(
        body,
        out_shape=(jax.ShapeDtypeStruct((S, D_MODEL), BF16), jax.ShapeDtypeStruct((S, wide), BF16),
                   jax.ShapeDtypeStruct((S, wide), BF16), jax.ShapeDtypeStruct((S, wide), BF16),
                   jax.ShapeDtypeStruct((S, 2 * GMLP_WIDTH), F32), jax.ShapeDtypeStruct((S, LANES), F32)),
        grid=(S // tm,),
        in_specs=[pl.BlockSpec((tm, D_MODEL), lambda i: (i, 0)), pl.BlockSpec((1, D_MODEL), lambda i: (0, 0)),
                  pl.BlockSpec((IN_PAD, D_MODEL), lambda i: (0, 0)), pl.BlockSpec((1, LANES), lambda i: (0, 0)),
                  pl.BlockSpec((tm, tm), lambda i: (0, 0)), pl.BlockSpec((2, 3 * LANES, wide), lambda i: (0, 0, 0)),
                  pl.BlockSpec((8, wide), lambda i: (0, 0))],
        out_specs=(pl.BlockSpec((tm, D_MODEL), lambda i: (i, 0)), pl.BlockSpec((tm, wide), lambda i: (i, 0)),
                   pl.BlockSpec((tm, wide), lambda i: (i, 0)), pl.BlockSpec((tm, wide), lambda i: (i, 0)),
                   pl.BlockSpec((tm, 2 * GMLP_WIDTH), lambda i: (i, 0)), pl.BlockSpec((tm, LANES), lambda i: (i, 0))),
        scratch_shapes=[pltpu.VMEM((8, LANES), F32)],
        name="inproj_fwd", compiler_params=_cp(("arbitrary",), VMEM_LIMIT),
    )(x, g_mix, w_pad, bf_pad, tri, put, ones)


def _group_ones():
    r = lax.broadcasted_iota(jnp.int32, (GMLP_WIDTH, GMLP_WIDTH), 0) // GROUP_DIM
    c = lax.broadcasted_iota(jnp.int32, (GMLP_WIDTH, GMLP_WIDTH), 1) // GROUP_DIM
    return (r == c).astype(BF16)


def _gmlp_mixed(vn_bf, w_ref, bias, n_chunks):
    lane = lax.broadcasted_iota(jnp.int32, (CHUNK, LANES), 1)
    row = lax.broadcasted_iota(jnp.int32, (CHUNK, CHUNK), 0)
    col = lax.broadcasted_iota(jnp.int32, (CHUNK, CHUNK), 1)
    ws = [jnp.where(row >= col, w_ref[g], 0.0).astype(BF16) for g in range(N_GROUPS)]
    rows = []
    for ci in range(n_chunks):
        cols = []
        for pp in range(N_GROUPS // 2):
            v = vn_bf[ci * CHUNK:(ci + 1) * CHUNK, pp * LANES:(pp + 1) * LANES]
            v_lo = jnp.where(lane < GROUP_DIM, v, jnp.zeros_like(v))
            v_hi = jnp.where(lane >= GROUP_DIM, v, jnp.zeros_like(v))
            m = (jnp.dot(ws[2 * pp], v_lo, preferred_element_type=F32)
                 + jnp.dot(ws[2 * pp + 1], v_hi, preferred_element_type=F32))
            cols.append(m + bias[:, pp * LANES:(pp + 1) * LANES])
        rows.append(jnp.concatenate(cols, axis=1))
    return jnp.concatenate(rows, axis=0)


def _gmlp_fwd(ug, gain, w_s, bias_full):
    S = ug.shape[0]
    tm = _pick(S, (512, 256, 128))
    ones = _group_ones()

    def body(ug_ref, gain_ref, w_ref, bias_ref, ones_ref, sg_ref):
        u = _gelu(ug_ref[:, :GMLP_WIDTH])
        vr = _gelu(ug_ref[:, GMLP_WIDTH:])
        ms = _dot3(vr * vr, ones_ref[...]) * (1.0 / GROUP_DIM)
        vn = ((vr * lax.rsqrt(ms + EPS)) * gain_ref[...]).astype(BF16)
        mixed = _gmlp_mixed(vn, w_ref, bias_ref[...], tm // CHUNK)
        sg_ref[...] = (u * mixed).astype(BF16)

    return pl.pallas_call(
        body, out_shape=jax.ShapeDtypeStruct((S, GMLP_WIDTH), BF16), grid=(S // tm,),
        in_specs=[pl.BlockSpec((tm, 2 * GMLP_WIDTH), lambda i: (i, 0)), pl.BlockSpec((1, GMLP_WIDTH), lambda i: (0, 0)),
                  pl.BlockSpec((N_GROUPS, CHUNK, CHUNK), lambda i: (0, 0, 0)),
                  pl.BlockSpec((CHUNK, GMLP_WIDTH), lambda i: (0, 0)),
                  pl.BlockSpec((GMLP_WIDTH, GMLP_WIDTH), lambda i: (0, 0))],
        out_specs=pl.BlockSpec((tm, GMLP_WIDTH), lambda i: (i, 0)),
        name="gmlp_fwd", compiler_params=_cp(("parallel",), VMEM_LIMIT),
    )(ug, gain, w_s, bias_full, ones)


_NT = (((1,), (1,)), ((), ()))
_TN = (((0,), (0,)), ((), ()))


def _attn_fwd(qa, ka, va):
    S = qa.shape[0]
    tq = _pick(S, (ATT_TQ, 256))
    tk = min(ATT_TK, tq)
    nq = S // tq
    per_q = tq // tk

    def body(q_ref, k_ref, v_ref, o_ref, lse_ref, ob_ref):
        qi = pl.program_id(1)
        lane = lax.broadcasted_iota(jnp.int32, (tq, LANES), 1)
        sub = tk
        rid = lax.broadcasted_iota(jnp.int32, (sub, sub), 0)
        cid = lax.broadcasted_iota(jnp.int32, (sub, sub), 1)
        qs = [q_ref[:, :LANES], q_ref[:, LANES:]]

        def update(q, ks, k_len, h, m, acc, masked):
            cols = slice(h * LANES, (h + 1) * LANES)
            s = lax.dot_general(q, k_ref[pl.ds(ks, k_len), cols], _NT, preferred_element_type=F32)
            if masked:
                s = jnp.where(rid >= cid, s, -jnp.inf)
            m_new = jnp.maximum(m, jnp.max(s, axis=-1, keepdims=True))
            p = jnp.exp(s - m_new).astype(BF16)
            acc = jnp.exp(m - m_new) * acc + jnp.dot(p, v_ref[pl.ds(ks, k_len), cols], preferred_element_type=F32)
            return m_new, acc

        def step(kb, carry):
            ks = pl.multiple_of(kb * tk, tk)
            return tuple(update(qs[h], ks, tk, h, *carry[h], False) for h in range(2))

        one = (jnp.full((tq, 1), -jnp.inf, F32), jnp.zeros((tq, LANES), F32))
        carry = lax.fori_loop(0, qi * per_q, step, (one, one))
        outs, lses = [], []
        for h in range(2):
            ms, accs = [], []
            for r in range(tq // sub):
                rows = slice(r * sub, (r + 1) * sub)
                m, acc = carry[h][0][rows], carry[h][1][rows]
                for c in range(r + 1):
                    ks = pl.multiple_of(qi * tq + c * sub, sub)
                    m, acc = update(qs[h][rows], ks, sub, h, m, acc, c == r)
                ms.append(m)
                accs.append(acc)
            m, acc = jnp.concatenate(ms, axis=0), jnp.concatenate(accs, axis=0)
            l = acc[:, HEAD_DIM:HEAD_DIM + 1]
            outs.append(acc / l)
            lses.append(m + jnp.log(l))
        o = jnp.where(lane < HEAD_DIM, outs[0], pltpu.roll(outs[1], HEAD_DIM, 1))
        o_ref[...] = o
        ob_ref[...] = o.astype(BF16)
        lse_ref[...] = jnp.where(lane < HEAD_DIM, lses[0], lses[1])

    return pl.pallas_call(
        body,
        out_shape=(jax.ShapeDtypeStruct((S, ATT_WIDTH), F32), jax.ShapeDtypeStruct((S, ATT_WIDTH), F32),
                   jax.ShapeDtypeStruct((S, ATT_WIDTH), BF16)),
        grid=(N_PAIRS, nq),
        in_specs=[pl.BlockSpec((tq, 2 * LANES), lambda p, i: (i, p)),
                  pl.BlockSpec((S, 2 * LANES), lambda p, i: (0, p)),
                  pl.BlockSpec((S, 2 * LANES), lambda p, i: (0, p))],
        out_specs=(pl.BlockSpec((tq, LANES), lambda p, i: (i, p)), pl.BlockSpec((tq, LANES), lambda p, i: (i, p)),
                   pl.BlockSpec((tq, LANES), lambda p, i: (i, p))),
        name="attn_fwd", compiler_params=_cp(("parallel", "parallel"), VMEM_LIMIT),
    )(qa, ka, va)


def _shift_rows(x, prev, n):
    rid = lax.broadcasted_iota(jnp.int32, x.shape, 0)
    y = pltpu.roll(x, n, 0)
    if n == 1:
        return jnp.where(rid == 0, prev[7:8, :], y)
    return jnp.where(rid == 0, prev[6:7, :], jnp.where(rid == 1, prev[7:8, :], y))


def _shift_rows_up(x, nxt, n):
    rows = x.shape[0]
    rid = lax.broadcasted_iota(jnp.int32, x.shape, 0)
    y = pltpu.roll(x, rows - n, 0)
    if n == 1:
        return jnp.where(rid == rows - 1, nxt[0:1, :], y)
    return jnp.where(rid == rows - 2, nxt[0:1, :], jnp.where(rid == rows - 1, nxt[1:2, :], y))


def _conv3(cur, prev, w, b):
    return (w[0:1, :] * _shift_rows(cur, prev, 2) + w[1:2, :] * _shift_rows(cur, prev, 1)
            + w[2:3, :] * cur + b)


def _ffn_up_conv(hn, w_up_bf, cw, cb):
    S = hn.shape[0]
    F = D_FF
    tm = _pick(S, (FFN_TM, 256))
    tn = _pick(F, (FFN_TN, 256, 128))
    nj = F // tn

    def body(hn_ref, wa_ref, wg_ref, cw_ref, cb_ref, hu_ref, hc_ref, act_ref, tail_ref):
        i = pl.program_id(1)

        @pl.when(i == 0)
        def _():
            tail_ref[...] = jnp.zeros_like(tail_ref)

        hn_v = hn_ref[...]
        halves = []
        for h, w_ref in enumerate((wa_ref, wg_ref)):
            hu = lax.dot_general(hn_v, w_ref[...], _NT, preferred_element_type=F32)
            hu_ref[h] = hu
            hc = _conv3(hu, tail_ref[h], cw_ref[h], cb_ref[h])
            hc_ref[h] = hc
            halves.append(hc)
            tail_ref[h] = hu[tm - 8:, :]
        a, g = halves
        act_ref[...] = (g * _sigmoid(g) * a).astype(BF16)

    both = pl.BlockSpec((2, tm, tn), lambda j, i: (0, i, j))
    return pl.pallas_call(
        body, out_shape=(jax.ShapeDtypeStruct((2, S, F), F32), jax.ShapeDtypeStruct((2, S, F), F32),
                         jax.ShapeDtypeStruct((S, F), BF16)),
        grid=(nj, S // tm),
        in_specs=[pl.BlockSpec((tm, D_MODEL), lambda j, i: (i, 0)),
                  pl.BlockSpec((tn, D_MODEL), lambda j, i: (j, 0)),
                  pl.BlockSpec((tn, D_MODEL), lambda j, i: (nj + j, 0)),
                  pl.BlockSpec((2, 8, tn), lambda j, i: (0, 0, j)),
                  pl.BlockSpec((2, 1, tn), lambda j, i: (0, 0, j))],
        out_specs=(both, both, pl.BlockSpec((tm, tn), lambda j, i: (i, j))),
        scratch_shapes=[pltpu.VMEM((2, 8, tn), F32)],
        name="ffn_up_conv", compiler_params=_cp(("parallel", "arbitrary"), VMEM_LIMIT),
    )(hn, w_up_bf, w_up_bf, cw, cb)


def _ffn_down_loss(act, w_down_bf, h1, g_final, target):
    S = h1.shape[0]
    tm = _pick(S, (512, 256))

    def body(a_ref, w_ref, h1_ref, g_ref, t_ref, loss_ref, dh_ref, dhb_ref, dg_ref):
        i = pl.program_id(0)

        @pl.when(i == 0)
        def _():
            loss_ref[...] = jnp.zeros_like(loss_ref)
            dg_ref[...] = jnp.zeros_like(dg_ref)

        hf = h1_ref[...] + jnp.dot(a_ref[...], w_ref[...], preferred_element_type=F32)
        g = g_ref[...]
        r = lax.rsqrt(jnp.mean(hf * hf, axis=-1, keepdims=True) + EPS)
        hhat = hf * r
        err = hhat * g - t_ref[...]
        loss_ref[...] += 0.5 * jnp.sum(jnp.mean(err * err, axis=-1, keepdims=True))
        dy = err * (1.0 / D_MODEL)
        dg_ref[0:1, :] += jnp.sum(dy * hhat, axis=0, keepdims=True)
        dhat = dy * g
        dh = r * (dhat - hhat * jnp.mean(dhat * hhat, axis=-1, keepdims=True))
        dh_ref[...] = dh
        dhb_ref[...] = dh.astype(BF16)

    row = pl.BlockSpec((tm, D_MODEL), lambda i: (i, 0))
    return pl.pallas_call(
        body,
        out_shape=(jax.ShapeDtypeStruct((8, LANES), F32), jax.ShapeDtypeStruct((S, D_MODEL), F32),
                   jax.ShapeDtypeStruct((S, D_MODEL), BF16), jax.ShapeDtypeStruct((8, D_MODEL), F32)),
        grid=(S // tm,),
        in_specs=[pl.BlockSpec((tm, D_FF), lambda i: (i, 0)), pl.BlockSpec((D_FF, D_MODEL), lambda i: (0, 0)), row,
                  pl.BlockSpec((1, D_MODEL), lambda i: (0, 0)), row],
        out_specs=(pl.BlockSpec((8, LANES), lambda i: (0, 0)), row, row, pl.BlockSpec((8, D_MODEL), lambda i: (0, 0))),
        name="ffn_down_loss", compiler_params=_cp(("arbitrary",), VMEM_LIMIT),
    )(act, w_down_bf, h1, g_final, target)


def _ffn_up_dx_rms(dhu, w_up_bf, h1, g_ffn, dh2):
    _, S, F = dhu.shape
    tm = _pick(S, (512, 256))

    def body(a_ref, b_ref, h_ref, g_ref, r_ref, dh_ref, dhb_ref, dg_ref, acc_ref):
        i, k = pl.program_id(0), pl.program_id(1)

        @pl.when((i == 0) & (k == 0))
        def _():
            dg_ref[...] = jnp.zeros_like(dg_ref)

        part = jnp.dot(a_ref[...], b_ref[...], preferred_element_type=F32)

        @pl.when(k == 0)
        def _():
            acc_ref[...] = part

        @pl.when(k == 1)
        def _():
            dyv = acc_ref[...] + part
            hf = h_ref[...]
            r = lax.rsqrt(jnp.mean(hf * hf, axis=-1, keepdims=True) + EPS)
            hhat = hf * r
            dg_ref[0:1, :] += jnp.sum(dyv * hhat, axis=0, keepdims=True)
            dhat = dyv * g_ref[...]
            dh = r_ref[...] + r * (dhat - hhat * jnp.mean(dhat * hhat, axis=-1, keepdims=True))
            dh_ref[...] = dh
            dhb_ref[...] = dh.astype(BF16)

    row = pl.BlockSpec((tm, D_MODEL), lambda i, k: (i, 0))
    return pl.pallas_call(
        body,
        out_shape=(jax.ShapeDtypeStruct((S, D_MODEL), F32), jax.ShapeDtypeStruct((S, D_MODEL), BF16),
                   jax.ShapeDtypeStruct((8, D_MODEL), F32)),
        grid=(S // tm, 2),
        in_specs=[pl.BlockSpec((None, tm, F), lambda i, k: (k, i, 0)), pl.BlockSpec((F, D_MODEL), lambda i, k: (k, 0)),
                  row, pl.BlockSpec((1, D_MODEL), lambda i, k: (0, 0)), row],
        out_specs=(row, row, pl.BlockSpec((8, D_MODEL), lambda i, k: (0, 0))),
        scratch_shapes=[pltpu.VMEM((tm, D_MODEL), F32)],
        name="ffn_up_dx_rms", compiler_params=_cp(("arbitrary", "arbitrary"), VMEM_LIMIT),
    )(dhu, w_up_bf, h1, g_ffn, dh2)


def _conv_gate_bwd(hc, hu, dact, cw):
    _, S, F = hu.shape
    tm = _pick(S, (CONV_TM, 128))
    tn = _pick(F, (CONV_TN, 256, 128))
    r8 = tm // 8
    n_i = S // tm
    last8 = S // 8 - 1

    def body(hc_ref, hcn_ref, hu_ref, da_ref, dan_ref, w_ref, dhu_ref, dcw_ref):
        i = pl.program_id(1)

        @pl.when(i == 0)
        def _():
            dcw_ref[...] = jnp.zeros_like(dcw_ref)

        rid8 = lax.broadcasted_iota(jnp.int32, (8, tn), 0)

        def gate_grads(a, g, d):
            sg = _sigmoid(g)
            return d * (g * sg), d * a * (sg * (1.0 + g * (1.0 - sg)))

        dhc = gate_grads(hc_ref[0], hc_ref[1], da_ref[...])
        dhc_n = gate_grads(hcn_ref[0], hcn_ref[1], dan_ref[...])
        for h in range(2):
            w = w_ref[h]
            d = dhc[h]
            dn = jnp.where(i < n_i - 1, dhc_n[h], 0.0)
            u1 = _shift_rows_up(d, dn, 1)
            u2 = _shift_rows_up(d, dn, 2)
            dhu_ref[h] = (w[2:3, :] * d + w[1:2, :] * u1 + w[0:1, :] * u2).astype(BF16)
            x = hu_ref[h]
            t0, t1, t2, t3 = [jnp.sum(t, axis=0, keepdims=True) for t in (u2 * x, u1 * x, d * x, d)]
            dcw_ref[h] += jnp.where(rid8 == 0, t0, jnp.where(rid8 == 1, t1, jnp.where(rid8 == 2, t2, jnp.where(rid8 == 3, t3, 0.0))))

    cur = pl.BlockSpec((2, tm, tn), lambda j, i: (0, i, j))
    return pl.pallas_call(
        body,
        out_shape=(jax.ShapeDtypeStruct((2, S, F), BF16), jax.ShapeDtypeStruct((2, 8, F), F32)),
        grid=(F // tn, n_i),
        in_specs=[cur, pl.BlockSpec((2, 8, tn), lambda j, i: (0, jnp.minimum((i + 1) * r8, last8), j)), cur,
                  pl.BlockSpec((tm, tn), lambda j, i: (i, j)),
                  pl.BlockSpec((8, tn), lambda j, i: (jnp.minimum((i + 1) * r8, last8), j)),
                  pl.BlockSpec((2, 8, tn), lambda j, i: (0, 0, j))],
        out_specs=(cur, pl.BlockSpec((2, 8, tn), lambda j, i: (0, 0, j))),
        name="conv_gate_bwd", compiler_params=_cp(("parallel", "arbitrary"), VMEM_LIMIT),
    )(hc, hc, hu, dact, dact, cw)


def _out_proj_dx_prep(dh1_bf, w_out_bf, att, lse, qa):
    S = att.shape[0]
    tm = _pick(S, (256,))

    def body(dh_ref, w_ref, o_ref, lse_ref, q_ref, dsg_ref, qb_ref, doa_ref):
        lane = lax.broadcasted_iota(jnp.int32, (tm, LANES), 1)
        dh = dh_ref[...]
        dsg_ref[...] = lax.dot_general(dh, w_ref[ATT_WIDTH:, :], _NT, preferred_element_type=F32)
        datt = lax.dot_general(dh, w_ref[:ATT_WIDTH, :], _NT, preferred_element_type=F32)
        for p in range(N_PAIRS):
            pc = slice(p * LANES, (p + 1) * LANES)
            do = datt[:, pc]
            prod = o_ref[:, pc] * do
            for hh in range(2):
                sel = (lane >= HEAD_DIM) if hh else (lane < HEAD_DIM)
                delta = jnp.sum(jnp.where(sel, prod, 0.0), axis=-1, keepdims=True)
                dod = pltpu.roll(do, HEAD_DIM, 1) if hh else do
                cols = slice((2 * p + hh) * LANES, (2 * p + hh + 1) * LANES)
                doa_ref[:, cols] = jnp.where(lane < HEAD_DIM, dod, _aug(lane, _split3f(-delta))).astype(BF16)
                lcol = p * LANES + hh * HEAD_DIM
                l3 = _split3f(-lse_ref[:, lcol:lcol + 1])
                augl = _aug(lane, [0.0] * 6 + l3).astype(BF16)
                qb_ref[:, cols] = jnp.where((lane >= HEAD_DIM + 6) & (lane < HEAD_DIM + 9), augl, q_ref[:, cols])

    half = pl.BlockSpec((tm, ATT_WIDTH), lambda i: (i, 0))
    wide = pl.BlockSpec((tm, N_HEADS * LANES), lambda i: (i, 0))
    return pl.pallas_call(
        body,
        out_shape=(jax.ShapeDtypeStruct((S, GMLP_WIDTH), F32), jax.ShapeDtypeStruct(qa.shape, BF16),
                   jax.ShapeDtypeStruct(qa.shape, BF16)),
        grid=(S // tm,),
        in_specs=[pl.BlockSpec((tm, D_MODEL), lambda i: (i, 0)), pl.BlockSpec((D_MODEL, D_MODEL), lambda i: (0, 0)),
                  half, half, wide],
        out_specs=(half, wide, wide),
        name="out_proj_dx_prep", compiler_params=_cp(("parallel",), VMEM_LIMIT),
    )(dh1_bf, w_out_bf, att, lse, qa)


def _attn_bwd(qb, ka, va, doa):
    S = qb.shape[0]
    tk = _pick(S, (512, 256))
    tq = tk
    nq = S // tq

    def pair(a, scale=None):
        lane = lax.broadcasted_iota(jnp.int32, (a.shape[0], LANES), 1)
        out = jnp.where(lane < HEAD_DIM, a[:, :LANES], pltpu.roll(a[:, LANES:], HEAD_DIM, 1))
        return out if scale is None else out * scale

    def head_lanes(a, col, sign, first):
        lane = lax.broadcasted_iota(jnp.int32, (a.shape[0], LANES), 1)
        return jnp.where(lane == first, sign * a[:, col:col + 1],
                         jnp.where(lane == first + 1, sign * a[:, LANES + col:LANES + col + 1], 0.0))

    def body(q_ref, do_ref, k_ref, v_ref, dqc_ref, dkc_ref, dvc_ref, dcq_ref, dck_ref, dq_ref, dka_ref, dva_ref):
        kb = pl.program_id(1)

        @pl.when(kb == 0)
        def _():
            dq_ref[...] = jnp.zeros_like(dq_ref)

        dka_ref[...] = jnp.zeros_like(dka_ref)
        dva_ref[...] = jnp.zeros_like(dva_ref)
        rid = lax.broadcasted_iota(jnp.int32, (tk, tq), 0)
        cid = lax.broadcasted_iota(jnp.int32, (tk, tq), 1)

        def sub_tile(qs, q_len, k_off, k_len, masked):
            keys = slice(k_off, k_off + k_len)
            for h in range(2):
                cols = slice(h * LANES, (h + 1) * LANES)
                qblk = q_ref[pl.ds(qs, q_len), cols]
                doblk = do_ref[pl.ds(qs, q_len), cols]
                kh = k_ref[keys, cols]
                p = jnp.exp(lax.dot_general(kh, qblk, _NT, preferred_element_type=F32))
                if masked:
                    p = jnp.where(cid >= rid, p, 0.0)
                ds = (p * lax.dot_general(v_ref[keys, cols], doblk, _NT, preferred_element_type=F32)).astype(BF16)
                dva_ref[keys, cols] += jnp.dot(p.astype(BF16), doblk, preferred_element_type=F32)
                dka_ref[keys, cols] += jnp.dot(ds, qblk, preferred_element_type=F32)
                dq_ref[pl.ds(qs, q_len), cols] += lax.dot_general(ds, kh, _TN, preferred_element_type=F32)

        sub_tile(pl.multiple_of(kb * tq, tq), tq, 0, tk, True)

        def step(qi, carry):
            sub_tile(pl.multiple_of(qi * tq, tq), tq, 0, tk, False)
            return carry

        lax.fori_loop(kb + 1, nq, step, 0)
        dka = dka_ref[...]
        dkc_ref[...] = pair(dka).astype(BF16)
        dvc_ref[...] = pair(dva_ref[...]).astype(BF16)
        first = 2 * pl.program_id(0)
        dck_ref[...] = head_lanes(dka, HEAD_DIM + 3, -1.0, first)

        @pl.when(kb == nq - 1)
        def _():
            dqa = dq_ref[...]
            dqc_ref[...] = pair(dqa, HEAD_DIM ** -0.5).astype(BF16)
            dcq_ref[...] = head_lanes(dqa, HEAD_DIM, 1.0, first)

    wide = 2 * LANES
    half = jax.ShapeDtypeStruct((S, ATT_WIDTH), BF16)
    slabs = jax.ShapeDtypeStruct((N_PAIRS, S, LANES), F32)
    return pl.pallas_call(
        body,
        out_shape=(half, half, half, slabs, slabs),
        grid=(N_PAIRS, nq),
        in_specs=[pl.BlockSpec((S, wide), lambda p, j: (0, p)), pl.BlockSpec((S, wide), lambda p, j: (0, p)),
                  pl.BlockSpec((tk, wide), lambda p, j: (j, p)), pl.BlockSpec((tk, wide), lambda p, j: (j, p))],
        out_specs=(pl.BlockSpec((S, LANES), lambda p, j: (0, p)), pl.BlockSpec((tk, LANES), lambda p, j: (j, p)),
                   pl.BlockSpec((tk, LANES), lambda p, j: (j, p)), pl.BlockSpec((None, S, LANES), lambda p, j: (p, 0, 0)),
                   pl.BlockSpec((None, tk, LANES), lambda p, j: (p, j, 0))),
        scratch_shapes=[pltpu.VMEM((S, wide), F32), pltpu.VMEM((tk, wide), F32), pltpu.VMEM((tk, wide), F32)],
        name="attn_bwd", compiler_params=_cp(("parallel", "arbitrary"), VMEM_LIMIT),
    )(qb, doa, ka, va)


def _gmlp_bwd(ug, dsg, gain, w_s, wt_s, bias_full):
    S = ug.shape[0]
    tm = _pick(S, (512, 256, 128))
    n_chunks = tm // CHUNK
    n_i = S // tm
    ones = _group_ones()
    nt = (((1,), (1,)), ((), ()))

    def body(ug_ref, dsg_ref, gain_ref, w_ref, wt_ref, bias_ref, ones_ref, dug_ref, dw_ref, dgain_ref, dbias_ref,
             dbacc_ref):
        i = pl.program_id(0)

        @pl.when(i == 0)
        def _():
            dw_ref[...] = jnp.zeros_like(dw_ref)
            dgain_ref[...] = jnp.zeros_like(dgain_ref)
            dbacc_ref[...] = jnp.zeros_like(dbacc_ref)

        ones_m = ones_ref[...]
        pu = ug_ref[:, :GMLP_WIDTH]
        pg = ug_ref[:, GMLP_WIDTH:]
        u = _gelu(pu)
        vr = _gelu(pg)
        ms = _dot3(vr * vr, ones_m) * (1.0 / GROUP_DIM)
        rinv = lax.rsqrt(ms + EPS)
        vhat = vr * rinv
        gain_v = gain_ref[...]
        vn = (vhat * gain_v).astype(BF16)
        mixed = _gmlp_mixed(vn, w_ref, bias_ref[...], n_chunks)
        dsg_v = dsg_ref[...]
        du = dsg_v * mixed
        dmixed = dsg_v * u
        dm_bf = dmixed.astype(BF16)
        lane = lax.broadcasted_iota(jnp.int32, (CHUNK, LANES), 1)
        row = lax.broadcasted_iota(jnp.int32, (CHUNK, CHUNK), 0)
        col = lax.broadcasted_iota(jnp.int32, (CHUNK, CHUNK), 1)
        wts = [jnp.where(col >= row, wt_ref[g], 0.0).astype(BF16) for g in range(N_GROUPS)]
        dvn_rows = []
        dbsum = jnp.zeros((CHUNK, GMLP_WIDTH), F32)
        for ci in range(n_chunks):
            rs = slice(ci * CHUNK, (ci + 1) * CHUNK)
            dbsum = dbsum + dmixed[rs, :]
            cols = []
            for pp in range(N_GROUPS // 2):
                cs = slice(pp * LANES, (pp + 1) * LANES)
                dm = dm_bf[rs, cs]
                dm_lo = jnp.where(lane < GROUP_DIM, dm, jnp.zeros_like(dm))
                dm_hi = jnp.where(lane >= GROUP_DIM, dm, jnp.zeros_like(dm))
                vb = vn[rs, cs]
                dw_ref[2 * pp] += lax.dot_general(dm_lo, vb, nt, preferred_element_type=F32)
                dw_ref[2 * pp + 1] += lax.dot_general(dm_hi, vb, nt, preferred_element_type=F32)
                cols.append(jnp.dot(wts[2 * pp], dm_lo, preferred_element_type=F32)
                            + jnp.dot(wts[2 * pp + 1], dm_hi, preferred_element_type=F32))
            dvn_rows.append(jnp.concatenate(cols, axis=1))
        dvn = jnp.concatenate(dvn_rows, axis=0)
        dbacc_ref[...] += dbsum
        dgain_ref[0:1, :] += jnp.sum(dvn * vhat, axis=0, keepdims=True)
        dvhat = dvn * gain_v
        gm = _dot3(dvhat * vhat, ones_m) * (1.0 / GROUP_DIM)
        dvr = rinv * (dvhat - vhat * gm)
        dug_ref[:, :GMLP_WIDTH] = (du * _gelu_grad(pu)).astype(BF16)
        dug_ref[:, GMLP_WIDTH:] = (dvr * _gelu_grad(pg)).astype(BF16)

        @pl.when(i == n_i - 1)
        def _():
            for g in range(N_GROUPS):
                dw_ref[g] = jnp.where(row >= col, dw_ref[g], 0.0)
            dbias_ref[...] = _dot3(dbacc_ref[...], ones_m)

    return pl.pallas_call(
        body,
        out_shape=(jax.ShapeDtypeStruct((S, 2 * GMLP_WIDTH), BF16), jax.ShapeDtypeStruct((N_GROUPS, CHUNK, CHUNK), F32),
                   jax.ShapeDtypeStruct((8, GMLP_WIDTH), F32), jax.ShapeDtypeStruct((CHUNK, GMLP_WIDTH), F32)),
        grid=(n_i,),
        in_specs=[pl.BlockSpec((tm, 2 * GMLP_WIDTH), lambda i: (i, 0)), pl.BlockSpec((tm, GMLP_WIDTH), lambda i: (i, 0)),
                  pl.BlockSpec((1, GMLP_WIDTH), lambda i: (0, 0)),
                  pl.BlockSpec((N_GROUPS, CHUNK, CHUNK), lambda i: (0, 0, 0)),
                  pl.BlockSpec((N_GROUPS, CHUNK, CHUNK), lambda i: (0, 0, 0)),
                  pl.BlockSpec((CHUNK, GMLP_WIDTH), lambda i: (0, 0)),
                  pl.BlockSpec((GMLP_WIDTH, GMLP_WIDTH), lambda i: (0, 0))],
        out_specs=(pl.BlockSpec((tm, 2 * GMLP_WIDTH), lambda i: (i, 0)),
                   pl.BlockSpec((N_GROUPS, CHUNK, CHUNK), lambda i: (0, 0, 0)),
                   pl.BlockSpec((8, GMLP_WIDTH), lambda i: (0, 0)),
                   pl.BlockSpec((CHUNK, GMLP_WIDTH), lambda i: (0, 0))),
        scratch_shapes=[pltpu.VMEM((CHUNK, GMLP_WIDTH), F32)],
        name="gmlp_bwd", compiler_params=_cp(("arbitrary",), VMEM_LIMIT),
    )(ug, dsg, gain, w_s, wt_s, bias_full, ones)


def _gate_bwd(dcq, dck, zf):
    S = zf.shape[0]
    tm = _pick(S, (256,))
    n_i = S // tm
    triu = (lax.broadcasted_iota(jnp.int32, (tm, tm), 0) <= lax.broadcasted_iota(jnp.int32, (tm, tm), 1)).astype(BF16)

    def body(dcq_ref, dck_ref, zf_ref, tri_ref, dzf_ref, dbf_ref, carry_ref):
        i = pl.program_id(0)

        @pl.when(i == 0)
        def _():
            carry_ref[...] = jnp.zeros_like(carry_ref)
            dbf_ref[...] = jnp.zeros_like(dbf_ref)

        lane = lax.broadcasted_iota(jnp.int32, (tm, LANES), 1)
        dc = dcq_ref[0] + dck_ref[0]
        for p in range(1, N_PAIRS):
            dc = dc + (dcq_ref[p] + dck_ref[p])
        dlf = _dot3l(tri_ref[...], dc) + carry_ref[0:1, :]
        carry_ref[0:1, :] = dlf[0:1, :]
        dz = jnp.where(lane < N_HEADS, dlf * _sigmoid(-zf_ref[...]), 0.0)
        dzf_ref[...] = dz.astype(BF16)
        dbf_ref[0:1, :] += jnp.sum(dz, axis=0, keepdims=True)

    return pl.pallas_call(
        body,
        out_shape=(jax.ShapeDtypeStruct((S, LANES), BF16), jax.ShapeDtypeStruct((8, LANES), F32)),
        grid=(n_i,),
        in_specs=[pl.BlockSpec((N_PAIRS, tm, LANES), lambda i: (0, n_i - 1 - i, 0)),
                  pl.BlockSpec((N_PAIRS, tm, LANES), lambda i: (0, n_i - 1 - i, 0)),
                  pl.BlockSpec((tm, LANES), lambda i: (n_i - 1 - i, 0)),
                  pl.BlockSpec((tm, tm), lambda i: (0, 0))],
        out_specs=(pl.BlockSpec((tm, LANES), lambda i: (n_i - 1 - i, 0)), pl.BlockSpec((8, LANES), lambda i: (0, 0))),
        scratch_shapes=[pltpu.VMEM((8, LANES), F32)],
        name="gate_bwd", compiler_params=_cp(("arbitrary",), VMEM_LIMIT),
    )(dcq, dck, zf, triu)


def _out_proj_fwd(att_bf, sg, w_out_bf, x, g_ffn):
    S = x.shape[0]
    tm = _pick(S, (512, 256))

    def body(a_ref, s_ref, w_ref, x_ref, g_ref, h_ref, hn_ref):
        h = (x_ref[...] + jnp.dot(a_ref[...], w_ref[:ATT_WIDTH, :], preferred_element_type=F32)
             + jnp.dot(s_ref[...], w_ref[ATT_WIDTH:, :], preferred_element_type=F32))
        h_ref[...] = h
        r = lax.rsqrt(jnp.mean(h * h, axis=-1, keepdims=True) + EPS)
        hn_ref[...] = ((h * r) * g_ref[...]).astype(BF16)

    row = pl.BlockSpec((tm, D_MODEL), lambda i: (i, 0))
    half = pl.BlockSpec((tm, ATT_WIDTH), lambda i: (i, 0))
    return pl.pallas_call(
        body, out_shape=(jax.ShapeDtypeStruct((S, D_MODEL), F32), jax.ShapeDtypeStruct((S, D_MODEL), BF16)),
        grid=(S // tm,),
        in_specs=[half, half, pl.BlockSpec((D_MODEL, D_MODEL), lambda i: (0, 0)), row,
                  pl.BlockSpec((1, D_MODEL), lambda i: (0, 0))],
        out_specs=(row, row), name="out_proj", compiler_params=_cp(("parallel",), VMEM_LIMIT),
    )(att_bf, sg, w_out_bf, x, g_ffn)


def _out_proj_dw(att_bf, sg, dh1_bf):
    S = att_bf.shape[0]
    tk = _pick(S, (1024, 512))

    def body(a_ref, s_ref, d_ref, o_ref):
        k = pl.program_id(0)

        @pl.when(k == 0)
        def _():
            o_ref[...] = jnp.zeros_like(o_ref)

        d = d_ref[...]
        o_ref[:ATT_WIDTH, :] += lax.dot_general(a_ref[...], d, _TN, preferred_element_type=F32)
        o_ref[ATT_WIDTH:, :] += lax.dot_general(s_ref[...], d, _TN, preferred_element_type=F32)

    half = pl.BlockSpec((tk, ATT_WIDTH), lambda k: (k, 0))
    return pl.pallas_call(
        body, out_shape=jax.ShapeDtypeStruct((D_MODEL, D_MODEL), F32), grid=(S // tk,),
        in_specs=[half, half, pl.BlockSpec((tk, D_MODEL), lambda k: (k, 0))],
        out_specs=pl.BlockSpec((D_MODEL, D_MODEL), lambda k: (0, 0)),
        name="out_proj_dw", compiler_params=_cp(("arbitrary",), VMEM_LIMIT),
    )(att_bf, sg, dh1_bf)


_IN_PIECES = ((0, ATT_WIDTH), (ATT_WIDTH, ATT_WIDTH), (2 * ATT_WIDTH, ATT_WIDTH), (QKV, 2 * GMLP_WIDTH), (UG_END, LANES))


def _inproj_bwd_dx(pieces, w_pad, x, g_mix, dh1):
    S = x.shape[0]
    tm = _pick(S, (512, 256))

    def body(*refs):
        p_refs, (w_ref, x_ref, g_ref, r_ref, dx_ref, dg_ref) = refs[:5], refs[5:]
        i = pl.program_id(0)

        @pl.when(i == 0)
        def _():
            dg_ref[...] = jnp.zeros_like(dg_ref)

        dxn = None
        for p_ref, (c0, width) in zip(p_refs, _IN_PIECES):
            part = jnp.dot(p_ref[...], w_ref[c0:c0 + width, :], preferred_element_type=F32)
            dxn = part if dxn is None else dxn + part
        xf = x_ref[...]
        r = lax.rsqrt(jnp.mean(xf * xf, axis=-1, keepdims=True) + EPS)
        xhat = xf * r
        dg_ref[0:1, :] += jnp.sum(dxn * xhat, axis=0, keepdims=True)
        dhat = dxn * g_ref[...]
        dx_ref[...] = r_ref[...] + r * (dhat - xhat * jnp.mean(dhat * xhat, axis=-1, keepdims=True))

    row = pl.BlockSpec((tm, D_MODEL), lambda i: (i, 0))
    return pl.pallas_call(
        body, out_shape=(jax.ShapeDtypeStruct((S, D_MODEL), F32), jax.ShapeDtypeStruct((8, D_MODEL), F32)),
        grid=(S // tm,),
        in_specs=[pl.BlockSpec((tm, width), lambda i: (i, 0)) for _, width in _IN_PIECES]
        + [pl.BlockSpec((IN_PAD, D_MODEL), lambda i: (0, 0)), row, pl.BlockSpec((1, D_MODEL), lambda i: (0, 0)), row],
        out_specs=(row, pl.BlockSpec((8, D_MODEL), lambda i: (0, 0))),
        name="in_proj_dx", compiler_params=_cp(("arbitrary",), VMEM_LIMIT),
    )(*pieces, w_pad, x, g_mix, dh1)


def _inproj_bwd_dw(xn, pieces):
    S = xn.shape[0]
    tk = _pick(S, (1024, 512))

    def body(*refs):
        x_ref, p_refs, o_ref = refs[0], refs[1:6], refs[6]
        k = pl.program_id(0)

        @pl.when(k == 0)
        def _():
            o_ref[...] = jnp.zeros_like(o_ref)

        xb = x_ref[...]
        for p_ref, (c0, width) in zip(p_refs, _IN_PIECES):
            o_ref[:, c0:c0 + width] += lax.dot_general(xb, p_ref[...], _TN, preferred_element_type=F32)

    return pl.pallas_call(
        body, out_shape=jax.ShapeDtypeStruct((D_MODEL, IN_PAD), F32), grid=(S // tk,),
        in_specs=[pl.BlockSpec((tk, D_MODEL), lambda k: (k, 0))]
        + [pl.BlockSpec((tk, width), lambda k: (k, 0)) for _, width in _IN_PIECES],
        out_specs=pl.BlockSpec((D_MODEL, IN_PAD), lambda k: (0, 0)),
        name="in_proj_dw", compiler_params=_cp(("arbitrary",), VMEM_LIMIT),
    )(xn, *pieces)


def _adamw(w, m, v, parts, name):
    R, C = w.shape[-2:]
    tr = R
    for cand in (256, 128, 64, 32, 16, 8):
        if R % cand == 0 and R > cand:
            tr = cand
            break
    c1 = 1.0 / (1.0 - ADAM_B1 ** ADAM_STEP)
    c2 = 1.0 / (1.0 - ADAM_B2 ** ADAM_STEP)

    def body(w_ref, m_ref, v_ref, p_ref, g_ref, d_ref, nm_ref, nv_ref):
        g = p_ref[0].astype(F32)
        for j in range(1, N_DEV):
            g = g + p_ref[j].astype(F32)
        g_ref[...] = g
        nm = ADAM_B1 * m_ref[...] + (1.0 - ADAM_B1) * g
        nv = ADAM_B2 * v_ref[...] + (1.0 - ADAM_B2) * (g * g)
        nm_ref[...] = nm
        nv_ref[...] = nv
        d_ref[...] = -ADAM_LR * ((nm * c1) / (jnp.sqrt(nv * c2) + ADAM_EPS) + ADAM_WD * w_ref[...])

    if w.ndim == 3:
        spec = pl.BlockSpec((None, tr, C), lambda i: (0, i, 0))
    else:
        spec = pl.BlockSpec((tr, C), lambda i: (i, 0))
    shp = jax.ShapeDtypeStruct(w.shape, F32)
    return pl.pallas_call(
        body, out_shape=(shp, shp, shp, shp), grid=(R // tr,),
        in_specs=[spec, spec, spec, pl.BlockSpec((N_DEV, tr, C), lambda i: (0, i, 0))],
        out_specs=(spec, spec, spec, spec),
        name=name, compiler_params=_cp(("parallel",), VMEM_LIMIT),
    )(w, m, v, parts)


def _place():
    x, y, c = lax.axis_index("x"), lax.axis_index("y"), lax.axis_index("c")
    return x, y, c


def _all_gather(blocks, name):
    n = len(blocks)

    def body(*refs):
        ins, outs = refs[:n], refs[n:2 * n]
        send_sems, recv_sems, local_sems = refs[2 * n:]
        x, y, c = _place()
        me, sibling = (x, y, c), (x, y, 1 - c)
        chips = [(1 - x, y), (x, 1 - y), (1 - x, 1 - y)]
        sends = []
        for a in range(n):
            out = outs[a]

            def slot(px, py, pc, out=out):
                return out.at[4 * px + 2 * py + pc]

            def copy(k, block, to, src=None, a=a, slot=slot):
                return pltpu.make_async_remote_copy(
                    src_ref=slot(*block) if src is None else src, dst_ref=slot(*block),
                    send_sem=send_sems.at[a, k], recv_sem=recv_sems.at[a, k], device_id=to, device_id_type=MESH)

            mine = pltpu.make_async_copy(ins[a], slot(*me), local_sems.at[a])
            mine.start()
            first = [copy(0, me, sibling, src=ins[a])]
            first += [copy(1 + j, me, (*chip, c), src=ins[a]) for j, chip in enumerate(chips)]
            for cp in first:
                cp.start()
            sends.append((mine, first, copy))
        for a in range(n):
            mine, first, copy = sends[a]
            passed = [copy(4 + j, (*chip, c), sibling) for j, chip in enumerate(chips)]
            for j, chip in enumerate(chips):
                copy(1 + j, (*chip, c), me).wait_recv()
                passed[j].start()
            copy(0, sibling, me).wait_recv()
            for j, chip in enumerate(chips):
                copy(4 + j, (*chip, 1 - c), me).wait_recv()
            for cp in first + passed:
                cp.wait_send()
            mine.wait()

    any_spec = pl.BlockSpec(memory_space=pl.ANY)
    return pl.pallas_call(
        body, out_shape=tuple(jax.ShapeDtypeStruct((N_DEV,) + b.shape, b.dtype) for b in blocks),
        in_specs=[any_spec] * n, out_specs=tuple([any_spec] * n),
        scratch_shapes=[pltpu.SemaphoreType.DMA((n, 7)), pltpu.SemaphoreType.DMA((n, 7)), pltpu.SemaphoreType.DMA((n,))],
        name=name,
    )(*blocks)


_HBM = pl.BlockSpec(memory_space=pltpu.HBM)
_SEM = pl.BlockSpec(memory_space=pltpu.SEMAPHORE)
_EFFECT = pltpu.SideEffectType.DATAFLOW_SIDE_EFFECTING


def _peers(x, y, c):
    out = []
    for k in range(1, N_DEV):
        px, py, pc = x ^ ((k >> 2) & 1), y ^ ((k >> 1) & 1), c ^ (k & 1)
        out.append((k, (px, py, pc), 4 * px + 2 * py + pc))
    return out


def _xchg_copies(src_refs, land_refs, send_sems, recv_sems, scatter):
    x, y, c = _place()
    me = 4 * x + 2 * y + c
    copies = []
    for a, (src, land) in enumerate(zip(src_refs, land_refs)):
        for k, place, idx in _peers(x, y, c):
            j = a * (N_DEV - 1) + k - 1
            copies.append(pltpu.make_async_remote_copy(
                src_ref=src.at[idx] if scatter[a] else src, dst_ref=land.at[me],
                send_sem=send_sems[j], recv_sem=recv_sems[j], device_id=place, device_id_type=MESH))
    return copies


def _xchg_start(srcs, scatter, name):
    n = len(srcs)
    lands = [lax.empty((N_DEV,) + (s.shape[1:] if sc else s.shape), s.dtype) for s, sc in zip(srcs, scatter)]

    ns = n * (N_DEV - 1)

    def body(*refs):
        sems = refs[2 * n:2 * n + 2 * ns]
        for cp in _xchg_copies(refs[:n], refs[n:2 * n], sems[:ns], sems[ns:], scatter):
            cp.start()
        token = refs[-1]
        token[...] = jnp.zeros_like(token)

    both = list(srcs) + lands
    res = pl.pallas_call(
        body, name=name,
        out_shape=(*[pltpu.SemaphoreType.DMA(())] * (2 * ns),
                   *[pltpu.HBM(a.shape, a.dtype) for a in both], jax.ShapeDtypeStruct((8, LANES), F32)),
        in_specs=[_HBM] * (2 * n),
        out_specs=(*([_SEM] * (2 * ns)), *([_HBM] * (2 * n)), pl.BlockSpec(memory_space=pltpu.VMEM)),
        input_output_aliases={i: 2 * ns + i for i in range(2 * n)},
        compiler_params=pltpu.CompilerParams(has_side_effects=_EFFECT),
    )(*[pltpu.with_memory_space_constraint(a, pltpu.HBM) for a in both])
    return (tuple(res[:2 * ns]), tuple(res[2 * ns:2 * ns + 2 * n])), res[-1]


def _xchg_wait(handle, scatter, after, name):
    sems, thru = handle
    n = len(thru) // 2
    ns = len(sems) // 2

    def body(*refs):
        got = refs[2 * n:2 * n + 2 * ns]
        for cp in _xchg_copies(refs[:n], refs[n:2 * n], got[:ns], got[ns:], scatter):
            cp.wait_send()
            cp.wait_recv()

    outs = pl.pallas_call(
        body, name=name, out_shape=tuple(pltpu.HBM(a.shape, a.dtype) for a in thru),
        in_specs=[_HBM] * (2 * n) + [_SEM] * (2 * ns) + [pl.BlockSpec(memory_space=pl.ANY)],
        out_specs=tuple([_HBM] * (2 * n)), input_output_aliases={i: i for i in range(2 * n)},
        compiler_params=pltpu.CompilerParams(has_side_effects=_EFFECT),
    )(*thru, *sems, after)
    return outs[:n], outs[n:]


def _tie(a, token):
    return a if token is None else a + token[0, 0].astype(a.dtype)


def _rows128(a):
    flat = a.reshape(-1)
    rows = -(-flat.shape[0] // LANES)
    rows = -(-rows // 8) * 8
    return jnp.pad(flat, (0, rows * LANES - flat.shape[0])).reshape(rows, LANES)


def _local_step(x, target, norm_mix_g, w_in_t, b_forget, gmlp_norm_g, w_spatial, b_spatial, norm_ffn_g, conv_b,
                norm_final_g, rest_fn, send_fn, small_fn, token=None):
    f = D_FF
    g_mix = norm_mix_g.reshape(1, D_MODEL)
    w_pad = jnp.pad(w_in_t, ((0, IN_PAD - IN_COLS), (0, 0)))
    bf_pad = jnp.pad(b_forget.reshape(1, N_HEADS), ((0, 0), (0, LANES - N_HEADS)))
    xn, qa, ka, va, ug, zf = _inproj_fwd(x, _tie(g_mix, token), w_pad, bf_pad)
    bias_full = jnp.repeat(b_spatial.reshape(N_GROUPS, CHUNK).T, GROUP_DIM, axis=1)
    w_s = w_spatial.reshape(N_GROUPS, CHUNK, CHUNK)
    gain = gmlp_norm_g.reshape(1, GMLP_WIDTH)
    sg = _gmlp_fwd(ug, gain, w_s, bias_full)
    att, lse, att_bf = _attn_fwd(qa, ka, va)
    w_out_bf, w_up_bf, conv_w, w_down_bf = rest_fn(att_bf)
    g_ffn = norm_ffn_g.reshape(1, D_MODEL)
    h1, hn = _out_proj_fwd(att_bf, sg, w_out_bf, x, g_ffn)
    cw = jnp.pad(conv_w.reshape(3, 2, f).transpose(1, 0, 2), ((0, 0), (0, 5), (0, 0)))
    cb = conv_b.reshape(2, 1, f)
    hu, hc, act = _ffn_up_conv(hn, w_up_bf, cw, cb)
    loss_blk, dh2, dh2_bf, dg_final = _ffn_down_loss(act, w_down_bf, h1, norm_final_g.reshape(1, D_MODEL), target)
    dw_down = _mm(act, dh2_bf, mode="tn", out_dtype=F32, tm=1408, tn=1024, tk=2048, name="ffn_down_dw")
    dact = _mm(dh2_bf, w_down_bf, mode="nt", out_dtype=F32, tm=1024, tn=1408, tk=1024, outer="j", name="ffn_down_dx")
    dhu, dcw = _conv_gate_bwd(hc, hu, dact, _tie(cw, send_fn("w_down", dw_down)))
    dw_up = _mm(hn, dhu, mode="tn", out_dtype=F32, tm=1024, tn=1408, tk=2048, b_halves=True, outer="j", name="ffn_up_dw")
    dh1, dh1_bf, dg_ffn = _ffn_up_dx_rms(dhu, w_up_bf, h1, _tie(g_ffn, send_fn("w_up", dw_up)), dh2)
    dsg, qb, doa = _out_proj_dx_prep(dh1_bf, w_out_bf, att, lse, qa)
    dw_out = _out_proj_dw(att_bf, sg, dh1_bf)
    dq, dk, dv, dcq, dck = _attn_bwd(qb, ka, va, doa)
    wt_s = w_s.transpose(0, 2, 1)
    dug, dw_s, dgain, dbias = _gmlp_bwd(ug, dsg, _tie(gain, send_fn("w_out", dw_out)), w_s, wt_s, bias_full)
    dzf, dbf = _gate_bwd(dcq, dck, zf)
    grad_x, dg_mix = _inproj_bwd_dx((dq, dk, dv, dug, dzf), w_pad, x, g_mix, dh1)
    grads = dict(
        norm_mix_g=dg_mix[0:1, :],
        b_forget=dbf[0:1, :N_HEADS],
        gmlp_norm_g=dgain[0:1, :],
        w_spatial=dw_s,
        b_spatial=dbias[:, ::GROUP_DIM].T,
        norm_ffn_g=dg_ffn[0:1, :],
        conv_w=dcw[:, 0:3, :].transpose(1, 0, 2).reshape(3, 2 * f),
        conv_b=dcw[:, 3, :].reshape(1, 2 * f),
        norm_final_g=dg_final[0, :],
    )
    token = small_fn(loss_blk[0, 0], grads)
    dw_in = _inproj_bwd_dw(xn, (dq, dk, dv, dug, _tie(dzf, token)))
    return grad_x, send_fn("w_in", dw_in[:, :IN_COLS])


SMALL = ("norm_mix_g", "b_forget", "gmlp_norm_g", "w_spatial", "b_spatial", "norm_ffn_g", "conv_b", "norm_final_g")


def kernel(x, norm_mix_g, w_in, b_forget, gmlp_norm_g, w_spatial, b_spatial, w_out, norm_ffn_g, w_up, conv_w, conv_b, w_down, norm_final_g, loss_target, m_norm_mix_g, m_w_in, m_b_forget, m_gmlp_norm_g, m_w_spatial, m_b_spatial, m_w_out, m_norm_ffn_g, m_w_up, m_conv_w, m_conv_b, m_w_down, m_norm_final_g, v_norm_mix_g, v_w_in, v_b_forget, v_gmlp_norm_g, v_w_spatial, v_b_spatial, v_w_out, v_norm_ffn_g, v_w_up, v_conv_w, v_conv_b, v_w_down, v_norm_final_g):
    weights = dict(norm_mix_g=norm_mix_g, w_in=w_in, b_forget=b_forget, gmlp_norm_g=gmlp_norm_g, w_spatial=w_spatial,
                   b_spatial=b_spatial, w_out=w_out, norm_ffn_g=norm_ffn_g, w_up=w_up, conv_w=conv_w, conv_b=conv_b,
                   w_down=w_down, norm_final_g=norm_final_g)
    m_in = dict(norm_mix_g=m_norm_mix_g, w_in=m_w_in, b_forget=m_b_forget, gmlp_norm_g=m_gmlp_norm_g,
                w_spatial=m_w_spatial, b_spatial=m_b_spatial, w_out=m_w_out, norm_ffn_g=m_norm_ffn_g, w_up=m_w_up,
                conv_w=m_conv_w, conv_b=m_conv_b, w_down=m_w_down, norm_final_g=m_norm_final_g)
    v_in = dict(norm_mix_g=v_norm_mix_g, w_in=v_w_in, b_forget=v_b_forget, gmlp_norm_g=v_gmlp_norm_g,
                w_spatial=v_w_spatial, b_spatial=v_b_spatial, w_out=v_w_out, norm_ffn_g=v_norm_ffn_g, w_up=v_w_up,
                conv_w=v_conv_w, conv_b=v_conv_b, w_down=v_w_down, norm_final_g=v_norm_final_g)
    order = list(weights)
    me = 4 * lax.axis_index("x") + 2 * lax.axis_index("y") + lax.axis_index("c")
    n_in, n_up = w_in.shape[2], w_up.shape[2]
    r_out, r_down = w_out.shape[1], w_down.shape[1]

    def with_mine(landed, mine):
        return lax.dynamic_update_index_in_dim(landed, mine, me, 0)

    up_blk = w_up[0].T.astype(BF16)
    out_blk = w_out[0].astype(BF16)
    down_blk = w_down[0].astype(BF16)
    taps_blk = jnp.pad(conv_w[0], ((0, 5), (0, 0)))
    (in_all,) = _all_gather([w_in[0].T.astype(BF16)], "gather_w_in")
    in_all, rest_blocks = lax.optimization_barrier((in_all, [up_blk, out_blk, down_blk, taps_blk]))
    rest_handle, token = _xchg_start(rest_blocks, [False] * 4, "gather_rest_start")
    w_in_t = in_all.reshape(N_DEV * n_in, D_MODEL)

    def rest_fn(after):
        mine, landed = _xchg_wait(rest_handle, [False] * 4, after, "gather_rest_wait")
        up_all, out_all, down_all, taps_all = [with_mine(l, b) for l, b in zip(landed, mine)]
        return (out_all.reshape(N_DEV * r_out, D_MODEL), up_all.reshape(N_DEV * n_up, D_MODEL),
                taps_all[:, :3, :].transpose(1, 0, 2).reshape(3, N_DEV * n_up),
                down_all.reshape(N_DEV * r_down, D_MODEL))

    sent = {}

    def send_fn(name, grad):
        if name == "w_in":
            parts = grad.reshape(D_MODEL, N_DEV, -1).transpose(1, 0, 2).astype(BF16)
        elif name == "w_up":
            parts = grad.reshape(D_MODEL, N_DEV, -1).transpose(1, 0, 2)
        else:
            parts = grad.reshape(N_DEV, -1, D_MODEL)
        sent[name], tok = _xchg_start([parts], [True], "scatter_" + name + "_start")
        return tok

    small = {}

    def small_fn(loss_local, g):
        loss_rows = jnp.pad(loss_local.reshape(1, 1), ((0, 31), (0, LANES - 1)))
        packed = [_rows128(g[k]) for k in SMALL] + [loss_rows, _rows128(g["conv_w"])]
        small["sizes"] = [p.shape[0] for p in packed]
        small["handle"], tok = _xchg_start([jnp.concatenate(packed, axis=0)], [False], "gather_small_start")
        return tok

    grad_x, after = _local_step(
        x[0], loss_target[0], norm_mix_g, w_in_t, b_forget, gmlp_norm_g, w_spatial, b_spatial, norm_ffn_g, conv_b,
        norm_final_g, rest_fn, send_fn, small_fn, token)

    outs = {}

    def update_big(name, after):
        (parts,), (landed,) = _xchg_wait(sent[name], [True], after, "scatter_" + name + "_wait")
        got = with_mine(landed, lax.dynamic_index_in_dim(parts, me, 0, keepdims=False))
        outs[name] = tuple(_adamw(weights[name], m_in[name], v_in[name], got, "adamw_" + name))
        return outs[name][0]

    for name in ("w_down", "w_up", "w_out"):
        after = update_big(name, after)

    (mine,), (landed,) = _xchg_wait(small["handle"], [False], after, "gather_small_wait")
    small_all = with_mine(landed, mine)
    sizes = small["sizes"]
    n_small_rows = sum(sizes[:-2])

    def pack(src):
        return jnp.concatenate([_rows128(src[k]) for k in SMALL] + [jnp.zeros((sizes[-2], LANES), F32)], axis=0)

    n_adam_rows = n_small_rows + sizes[-2]
    sg_, sd_, sm_, sv_ = _adamw(pack(weights), pack(m_in), pack(v_in), small_all[:, :n_adam_rows, :], "adamw_small")
    loss = sg_[n_small_rows, 0]
    off = 0
    for k, rows in zip(SMALL, sizes[:-2]):
        shp = weights[k].shape
        cnt = math.prod(shp)
        outs[k] = tuple(a[off:off + rows].reshape(-1)[:cnt].reshape(shp) for a in (sg_, sd_, sm_, sv_))
        off += rows
    taps_parts = small_all[:, n_adam_rows:, :].reshape(N_DEV, -1)[:, :3 * N_DEV * n_up].reshape(N_DEV, 3, N_DEV * n_up)
    taps_mine = lax.dynamic_slice_in_dim(taps_parts, me * n_up, n_up, axis=2)
    taps_mine = jnp.pad(taps_mine, ((0, 0), (0, 5), (0, 0)))

    def pad8(a):
        return jnp.pad(a[0], ((0, 5), (0, 0)))

    res = _adamw(pad8(conv_w), pad8(m_conv_w), pad8(v_conv_w), taps_mine, "adamw_conv_w")
    outs["conv_w"] = tuple(a[:3][None] for a in res)
    update_big("w_in", sg_)

    return (loss, grad_x[None], *[outs[k][0] for k in order], *[outs[k][1] for k in order],
            *[outs[k][2] for k in order], *[outs[k][3] for k in order])
```

```python
import functools
import math

import jax
import jax.numpy as jnp
from jax import lax
from jax.experimental import pallas as pl
from jax.experimental.pallas import tpu as pltpu

F32 = jnp.float32
BF16 = jnp.bfloat16

N_DEV = 8
D_MODEL = 1024
ATT_WIDTH = 512
GMLP_WIDTH = 512
HEAD_DIM = 64
N_HEADS = 8
N_PAIRS = 4
N_GROUPS = 8
GROUP_DIM = 64
CHUNK = 128
D_FF = 2816
IN_COLS = 2568
IN_PAD = 2688
QKV = 1536
UG_END = 2560
EPS = 1e-6
LANES = 128

ADAM_LR = 0.001
ADAM_B1 = 0.9
ADAM_B2 = 0.999
ADAM_EPS = 1e-08
ADAM_WD = 0.01
ADAM_STEP = 10

ATT_TQ = 1024
ATT_TK = 1024
FFN_TM, FFN_TN = 512, 1408
CONV_TM, CONV_TN = 256, 1408
VMEM_LIMIT = 56 * 1024 * 1024
MESH = pl.DeviceIdType.MESH


def _cp(sem, vmem=None):
    return pltpu.CompilerParams(dimension_semantics=sem, vmem_limit_bytes=vmem)


def _pick(n, prefs):
    for p in prefs:
        if n % p == 0:
            return p
    return n


def _split3(x):
    hi = x.astype(BF16)
    r1 = x - hi.astype(F32)
    mid = r1.astype(BF16)
    lo = (r1 - mid.astype(F32)).astype(BF16)
    return hi, mid, lo


def _dot3(x, ones_bf):
    d = functools.partial(jnp.dot, preferred_element_type=F32)
    out = []
    for c in range(0, x.shape[1], 2 * LANES):
        blk = ones_bf[c:c + 2 * LANES, c:c + 2 * LANES]
        hi, mid, lo = _split3(x[:, c:c + 2 * LANES])
        out.append(d(hi, blk) + d(mid, blk) + d(lo, blk))
    return jnp.concatenate(out, axis=1)


def _dot3l(ones_bf, x):
    n = x.shape[1]
    y = jnp.dot(ones_bf, jnp.concatenate(_split3(x), axis=1), preferred_element_type=F32)
    return y[:, :n] + y[:, n:2 * n] + y[:, 2 * n:]


def _gelu(x):
    k = math.sqrt(2.0 / math.pi)
    t = jnp.tanh(k * (x + 0.044715 * (x * x * x)))
    return 0.5 * x * (1.0 + t)


def _gelu_grad(x):
    k = math.sqrt(2.0 / math.pi)
    x2 = x * x
    t = jnp.tanh(k * (x + 0.044715 * (x2 * x)))
    return 0.5 * (1.0 + t) + 0.5 * x * (1.0 - t * t) * (k * (1.0 + 3.0 * 0.044715 * x2))


def _sigmoid(x):
    return 1.0 / (1.0 + jnp.exp(-x))


def _mm(a, b, *, mode, out_dtype, tm, tn, tk, name, res=None, a_halves=False, b_halves=False,
        out_halves=False, outer="i"):
    if mode == "tn":
        K, M = a.shape[-2], a.shape[-1] * (2 if a_halves else 1)
    else:
        M, K = a.shape[-2], a.shape[-1] * (2 if a_halves else 1)
    if mode == "nt":
        N = b.shape[-2]
        assert b.shape[-1] == K
    else:
        N = b.shape[-1] * (2 if b_halves else 1)
    tm, tn, tk = min(tm, M), min(tn, N), min(tk, K)
    assert M % tm == 0 and N % tn == 0 and K % tk == 0, (name, M, N, K, tm, tn, tk)
    nm, nn, nk = M // tm, N // tn, K // tk

    def ij(g0, g1):
        return (g0, g1) if outer == "i" else (g1, g0)

    if mode == "nn":
        dims = (((1,), (0,)), ((), ()))
        if a_halves:
            nkh = nk // 2
            a_spec = pl.BlockSpec((None, tm, tk), lambda g0, g1, k: (k // nkh, ij(g0, g1)[0], k % nkh))
        else:
            a_spec = pl.BlockSpec((tm, tk), lambda g0, g1, k: (ij(g0, g1)[0], k))
        b_spec = pl.BlockSpec((tk, tn), lambda g0, g1, k: (k, ij(g0, g1)[1]))
    elif mode == "nt":
        dims = (((1,), (1,)), ((), ()))
        if a_halves:
            nkh = nk // 2
            a_spec = pl.BlockSpec((None, tm, tk), lambda g0, g1, k: (k // nkh, ij(g0, g1)[0], k % nkh))
        else:
            a_spec = pl.BlockSpec((tm, tk), lambda g0, g1, k: (ij(g0, g1)[0], k))
        b_spec = pl.BlockSpec((tn, tk), lambda g0, g1, k: (ij(g0, g1)[1], k))
    else:
        dims = (((0,), (0,)), ((), ()))
        if a_halves:
            nmh = nm // 2
            a_spec = pl.BlockSpec((None, tk, tm), lambda g0, g1, k: (ij(g0, g1)[0] // nmh, k, ij(g0, g1)[0] % nmh))
        else:
            a_spec = pl.BlockSpec((tk, tm), lambda g0, g1, k: (k, ij(g0, g1)[0]))
        if b_halves:
            nnh = nn // 2
            b_spec = pl.BlockSpec((None, tk, tn), lambda g0, g1, k: (ij(g0, g1)[1] // nnh, k, ij(g0, g1)[1] % nnh))
        else:
            b_spec = pl.BlockSpec((tk, tn), lambda g0, g1, k: (k, ij(g0, g1)[1]))
    if out_halves:
        nnh = nn // 2
        o_spec = pl.BlockSpec((None, tm, tn), lambda g0, g1, k: (ij(g0, g1)[1] // nnh, ij(g0, g1)[0], ij(g0, g1)[1] % nnh))
        o_shape = jax.ShapeDtypeStruct((2, M, N // 2), out_dtype)
    else:
        o_spec = pl.BlockSpec((tm, tn), lambda g0, g1, k: ij(g0, g1))
        o_shape = jax.ShapeDtypeStruct((M, N), out_dtype)
    in_specs = [a_spec, b_spec]
    args = [a, b]
    if res is not None:
        in_specs.append(pl.BlockSpec((tm, tn), lambda g0, g1, k: ij(g0, g1)))
        args.append(res)

    def body(*refs):
        if res is not None:
            a_ref, b_ref, r_ref, o_ref = refs[:4]
        else:
            a_ref, b_ref, o_ref = refs[:3]
            r_ref = None
        part = lax.dot_general(a_ref[...], b_ref[...], dims, preferred_element_type=F32)
        if nk == 1:
            if r_ref is not None:
                part = part + r_ref[...]
            o_ref[...] = part.astype(out_dtype)
            return
        acc_ref = refs[-1]
        k = pl.program_id(2)

        @pl.when(k == 0)
        def _():
            acc_ref[...] = part

        @pl.when(k > 0)
        def _():
            acc_ref[...] += part

        @pl.when(k == nk - 1)
        def _():
            tot = acc_ref[...]
            if r_ref is not None:
                tot = tot + r_ref[...]
            o_ref[...] = tot.astype(out_dtype)

    grid = (nm, nn, nk) if outer == "i" else (nn, nm, nk)
    scratch = [] if nk == 1 else [pltpu.VMEM((tm, tn), F32)]
    return pl.pallas_call(
        body, out_shape=o_shape, grid=grid, in_specs=in_specs, out_specs=o_spec, scratch_shapes=scratch,
        name=name, compiler_params=_cp(("parallel", "parallel", "arbitrary"), VMEM_LIMIT),
    )(*args)


def _aug(lane, terms):
    out = 0.0
    for j, t in enumerate(terms):
        out = jnp.where(lane == HEAD_DIM + j, t, out)
    return out


def _split3f(x):
    hi, mid, lo = _split3(x)
    return [hi.astype(F32), mid.astype(F32), lo.astype(F32)]


def _inproj_fwd(x, g_mix, w_pad, bf_pad):
    S = x.shape[0]
    tm = _pick(S, (512, 256))
    tri = (lax.broadcasted_iota(jnp.int32, (tm, tm), 0) >= lax.broadcasted_iota(jnp.int32, (tm, tm), 1)).astype(BF16)

    def body(x_ref, g_ref, w_ref, bf_ref, tri_ref, put_ref, one_ref, xn_ref, qa_ref, ka_ref, va_ref, ug_ref, zf_ref,
             carry_ref):
        i = pl.program_id(0)

        @pl.when(i == 0)
        def _():
            carry_ref[...] = jnp.zeros_like(carry_ref)

        xf = x_ref[...]
        r = lax.rsqrt(jnp.mean(xf * xf, axis=-1, keepdims=True) + EPS)
        xn = ((xf * r) * g_ref[...]).astype(BF16)
        xn_ref[...] = xn
        proj = lax.dot_general(xn, w_ref[...], _NT, preferred_element_type=F32)
        ug_ref[...] = proj[:, QKV:UG_END]
        zf = proj[:, UG_END:] + bf_ref[...]
        zf_ref[...] = zf
        lf = jnp.minimum(zf, 0.0) - jnp.log(1.0 + jnp.exp(-jnp.abs(zf)))
        c = _dot3l(tri_ref[...], lf) + carry_ref[0:1, :]
        carry_ref[0:1, :] = c[tm - 1:tm, :]
        c3 = jnp.concatenate(_split3(c), axis=1)
        aug_q = jnp.dot(c3, put_ref[0], preferred_element_type=F32) + one_ref[0:1, :]
        aug_k = jnp.dot(c3, put_ref[1], preferred_element_type=F32) + one_ref[1:2, :]
        lane = lax.broadcasted_iota(jnp.int32, (tm, LANES), 1)
        for h in range(N_HEADS):
            p, odd = h // 2, h % 2

            def head(base, scale=None, p=p, odd=odd):
                blk = proj[:, base + p * LANES:base + (p + 1) * LANES]
                if scale is not None:
                    blk = blk * scale
                return pltpu.roll(blk, HEAD_DIM, 1) if odd else blk

            cols = slice(h * LANES, (h + 1) * LANES)
            qa_ref[:, cols] = jnp.where(lane < HEAD_DIM, head(0, HEAD_DIM ** -0.5), aug_q[:, cols]).astype(BF16)
            ka_ref[:, cols] = jnp.where(lane < HEAD_DIM, head(ATT_WIDTH), aug_k[:, cols]).astype(BF16)
            va_ref[:, cols] = jnp.where(lane < HEAD_DIM, head(2 * ATT_WIDTH), one_ref[2:3, cols]).astype(BF16)

    wide = N_HEADS * LANES
    src = lax.broadcasted_iota(jnp.int32, (3 * LANES, wide), 0)
    col = lax.broadcasted_iota(jnp.int32, (3 * LANES, wide), 1)
    hd, term = src % LANES, src // LANES
    to_q = (col == hd * LANES + HEAD_DIM + term) & (hd < N_HEADS)
    to_k = (col == hd * LANES + HEAD_DIM + 3 + term) & (hd < N_HEADS)
    put = jnp.stack([to_q.astype(BF16), -to_k.astype(BF16)])
    off = lax.broadcasted_iota(jnp.int32, (8, wide), 1) % LANES - HEAD_DIM
    row = lax.broadcasted_iota(jnp.int32, (8, wide), 0)
    q_one = (off >= 3) & (off < 6)
    k_one = ((off >= 0) & (off < 3)) | ((off >= 6) & (off < 9))
    v_one = (off >= 0) & (off < 3)
    ones = jnp.where(row == 0, q_one, jnp.where(row == 1, k_one, (row == 2) & v_one)).astype(F32)
    return pl.pallas_call(
        body,
        out_shape=(jax.ShapeDtypeStruct((S, D_MODEL), BF16), jax.ShapeDtypeStruct((S, wide), BF16),
                   jax.ShapeDtypeStruct((S, wide), BF16), jax.ShapeDtypeStruct((S, wide), BF16),
                   jax.ShapeDtypeStruct((S, 2 * GMLP_WIDTH), F32), jax.ShapeDtypeStruct((S, LANES), F32)),
        grid=(S // tm,),
        in_specs=[pl.BlockSpec((tm, D_MODEL), lambda i: (i, 0)), pl.BlockSpec((1, D_MODEL), lambda i: (0, 0)),
                  pl.BlockSpec((IN_PAD, D_MODEL), lambda i: (0, 0)), pl.BlockSpec((1, LANES), lambda i: (0, 0)),
                  pl.BlockSpec((tm, tm), lambda i: (0, 0)), pl.BlockSpec((2, 3 * LANES, wide), lambda i: (0, 0, 0)),
                  pl.BlockSpec((8, wide), lambda i: (0, 0))],
        out_specs=(pl.BlockSpec((tm, D_MODEL), lambda i: (i, 0)), pl.BlockSpec((tm, wide), lambda i: (i, 0)),
                   pl.BlockSpec((tm, wide), lambda i: (i, 0)), pl.BlockSpec((tm, wide), lambda i: (i, 0)),
                   pl.BlockSpec((tm, 2 * GMLP_WIDTH), lambda i: (i, 0)), pl.BlockSpec((tm, LANES), lambda i: (i, 0))),
        scratch_shapes=[pltpu.VMEM((8, LANES), F32)],
        name="inproj_fwd", compiler_params=_cp(("arbitrary",), VMEM_LIMIT),
    )(x, g_mix, w_pad, bf_pad, tri, put, ones)


def _group_ones():
    r = lax.broadcasted_iota(jnp.int32, (GMLP_WIDTH, GMLP_WIDTH), 0) // GROUP_DIM
    c = lax.broadcasted_iota(jnp.int32, (GMLP_WIDTH, GMLP_WIDTH), 1) // GROUP_DIM
    return (r == c).astype(BF16)


def _gmlp_mixed(vn_bf, w_ref, bias, n_chunks):
    lane = lax.broadcasted_iota(jnp.int32, (CHUNK, LANES), 1)
    row = lax.broadcasted_iota(jnp.int32, (CHUNK, CHUNK), 0)
    col = lax.broadcasted_iota(jnp.int32, (CHUNK, CHUNK), 1)
    ws = [jnp.where(row >= col, w_ref[g], 0.0).astype(BF16) for g in range(N_GROUPS)]
    rows = []
    for ci in range(n_chunks):
        cols = []
        for pp in range(N_GROUPS // 2):
            v = vn_bf[ci * CHUNK:(ci + 1) * CHUNK, pp * LANES:(pp + 1) * LANES]
            v_lo = jnp.where(lane < GROUP_DIM, v, jnp.zeros_like(v))
            v_hi = jnp.where(lane >= GROUP_DIM, v, jnp.zeros_like(v))
            m = (jnp.dot(ws[2 * pp], v_lo, preferred_element_type=F32)
                 + jnp.dot(ws[2 * pp + 1], v_hi, preferred_element_type=F32))
            cols.append(m + bias[:, pp * LANES:(pp + 1) * LANES])
        rows.append(jnp.concatenate(cols, axis=1))
    return jnp.concatenate(rows, axis=0)


def _gmlp_fwd(ug, gain, w_s, bias_full):
    S = ug.shape[0]
    tm = _pick(S, (512, 256, 128))
    ones = _group_ones()

    def body(ug_ref, gain_ref, w_ref, bias_ref, ones_ref, sg_ref):
        u = _gelu(ug_ref[:, :GMLP_WIDTH])
        vr = _gelu(ug_ref[:, GMLP_WIDTH:])
        ms = _dot3(vr * vr, ones_ref[...]) * (1.0 / GROUP_DIM)
        vn = ((vr * lax.rsqrt(ms + EPS)) * gain_ref[...]).astype(BF16)
        mixed = _gmlp_mixed(vn, w_ref, bias_ref[...], tm // CHUNK)
        sg_ref[...] = (u * mixed).astype(BF16)

    return pl.pallas_call(
        body, out_shape=jax.ShapeDtypeStruct((S, GMLP_WIDTH), BF16), grid=(S // tm,),
        in_specs=[pl.BlockSpec((tm, 2 * GMLP_WIDTH), lambda i: (i, 0)), pl.BlockSpec((1, GMLP_WIDTH), lambda i: (0, 0)),
                  pl.BlockSpec((N_GROUPS, CHUNK, CHUNK), lambda i: (0, 0, 0)),
                  pl.BlockSpec((CHUNK, GMLP_WIDTH), lambda i: (0, 0)),
                  pl.BlockSpec((GMLP_WIDTH, GMLP_WIDTH), lambda i: (0, 0))],
        out_specs=pl.BlockSpec((tm, GMLP_WIDTH), lambda i: (i, 0)),
        name="gmlp_fwd", compiler_params=_cp(("parallel",), VMEM_LIMIT),
    )(ug, gain, w_s, bias_full, ones)


_NT = (((1,), (1,)), ((), ()))
_TN = (((0,), (0,)), ((), ()))


def _attn_fwd(qa, ka, va):
    S = qa.shape[0]
    tq = _pick(S, (ATT_TQ, 256))
    tk = min(ATT_TK, tq)
    nq = S // tq
    per_q = tq // tk

    def body(q_ref, k_ref, v_ref, o_ref, lse_ref, ob_ref):
        qi = pl.program_id(1)
        lane = lax.broadcasted_iota(jnp.int32, (tq, LANES), 1)
        sub = tk
        rid = lax.broadcasted_iota(jnp.int32, (sub, sub), 0)
        cid = lax.broadcasted_iota(jnp.int32, (sub, sub), 1)
        qs = [q_ref[:, :LANES], q_ref[:, LANES:]]

        def update(q, ks, k_len, h, m, acc, masked):
            cols = slice(h * LANES, (h + 1) * LANES)
            s = lax.dot_general(q, k_ref[pl.ds(ks, k_len), cols], _NT, preferred_element_type=F32)
            if masked:
                s = jnp.where(rid >= cid, s, -jnp.inf)
            m_new = jnp.maximum(m, jnp.max(s, axis=-1, keepdims=True))
            p = jnp.exp(s - m_new).astype(BF16)
            acc = jnp.exp(m - m_new) * acc + jnp.dot(p, v_ref[pl.ds(ks, k_len), cols], preferred_element_type=F32)
            return m_new, acc

        def step(kb, carry):
            ks = pl.multiple_of(kb * tk, tk)
            return tuple(update(qs[h], ks, tk, h, *carry[h], False) for h in range(2))

        one = (jnp.full((tq, 1), -jnp.inf, F32), jnp.zeros((tq, LANES), F32))
        carry = lax.fori_loop(0, qi * per_q, step, (one, one))
        outs, lses = [], []
        for h in range(2):
            ms, accs = [], []
            for r in range(tq // sub):
                rows = slice(r * sub, (r + 1) * sub)
                m, acc = carry[h][0][rows], carry[h][1][rows]
                for c in range(r + 1):
                    ks = pl.multiple_of(qi * tq + c * sub, sub)
                    m, acc = update(qs[h][rows], ks, sub, h, m, acc, c == r)
                ms.append(m)
                accs.append(acc)
            m, acc = jnp.concatenate(ms, axis=0), jnp.concatenate(accs, axis=0)
            l = acc[:, HEAD_DIM:HEAD_DIM + 1]
            outs.append(acc / l)
            lses.append(m + jnp.log(l))
        o = jnp.where(lane < HEAD_DIM, outs[0], pltpu.roll(outs[1], HEAD_DIM, 1))
        o_ref[...] = o
        ob_ref[...] = o.astype(BF16)
        lse_ref[...] = jnp.where(lane < HEAD_DIM, lses[0], lses[1])

    return pl.pallas_call(
        body,
        out_shape=(jax.ShapeDtypeStruct((S, ATT_WIDTH), F32), jax.ShapeDtypeStruct((S, ATT_WIDTH), F32),
                   jax.ShapeDtypeStruct((S, ATT_WIDTH), BF16)),
        grid=(N_PAIRS, nq),
        in_specs=[pl.BlockSpec((tq, 2 * LANES), lambda p, i: (i, p)),
                  pl.BlockSpec((S, 2 * LANES), lambda p, i: (0, p)),
                  pl.BlockSpec((S, 2 * LANES), lambda p, i: (0, p))],
        out_specs=(pl.BlockSpec((tq, LANES), lambda p, i: (i, p)), pl.BlockSpec((tq, LANES), lambda p, i: (i, p)),
                   pl.BlockSpec((tq, LANES), lambda p, i: (i, p))),
        name="attn_fwd", compiler_params=_cp(("parallel", "parallel"), VMEM_LIMIT),
    )(qa, ka, va)


def _shift_rows(x, prev, n):
    rid = lax.broadcasted_iota(jnp.int32, x.shape, 0)
    y = pltpu.roll(x, n, 0)
    if n == 1:
        return jnp.where(rid == 0, prev[7:8, :], y)
    return jnp.where(rid == 0, prev[6:7, :], jnp.where(rid == 1, prev[7:8, :], y))


def _shift_rows_up(x, nxt, n):
    rows = x.shape[0]
    rid = lax.broadcasted_iota(jnp.int32, x.shape, 0)
    y = pltpu.roll(x, rows - n, 0)
    if n == 1:
        return jnp.where(rid == rows - 1, nxt[0:1, :], y)
    return jnp.where(rid == rows - 2, nxt[0:1, :], jnp.where(rid == rows - 1, nxt[1:2, :], y))


def _conv3(cur, prev, w, b):
    return (w[0:1, :] * _shift_rows(cur, prev, 2) + w[1:2, :] * _shift_rows(cur, prev, 1)
            + w[2:3, :] * cur + b)


def _ffn_up_conv(hn, w_up_bf, cw, cb):
    S = hn.shape[0]
    F = D_FF
    tm = _pick(S, (FFN_TM, 256))
    tn = _pick(F, (FFN_TN, 256, 128))
    nj = F // tn

    def body(hn_ref, wa_ref, wg_ref, cw_ref, cb_ref, hu_ref, hc_ref, act_ref, tail_ref):
        i = pl.program_id(1)

        @pl.when(i == 0)
        def _():
            tail_ref[...] = jnp.zeros_like(tail_ref)

        hn_v = hn_ref[...]
        halves = []
        for h, w_ref in enumerate((wa_ref, wg_ref)):
            hu = lax.dot_general(hn_v, w_ref[...], _NT, preferred_element_type=F32)
            hu_ref[h] = hu
            hc = _conv3(hu, tail_ref[h], cw_ref[h], cb_ref[h])
            hc_ref[h] = hc
            halves.append(hc)
            tail_ref[h] = hu[tm - 8:, :]
        a, g = halves
        act_ref[...] = (g * _sigmoid(g) * a).astype(BF16)

    both = pl.BlockSpec((2, tm, tn), lambda j, i: (0, i, j))
    return pl.pallas_call(
        body, out_shape=(jax.ShapeDtypeStruct((2, S, F), F32), jax.ShapeDtypeStruct((2, S, F), F32),
                         jax.ShapeDtypeStruct((S, F), BF16)),
        grid=(nj, S // tm),
        in_specs=[pl.BlockSpec((tm, D_MODEL), lambda j, i: (i, 0)),
                  pl.BlockSpec((tn, D_MODEL), lambda j, i: (j, 0)),
                  pl.BlockSpec((tn, D_MODEL), lambda j, i: (nj + j, 0)),
                  pl.BlockSpec((2, 8, tn), lambda j, i: (0, 0, j)),
                  pl.BlockSpec((2, 1, tn), lambda j, i: (0, 0, j))],
        out_specs=(both, both, pl.BlockSpec((tm, tn), lambda j, i: (i, j))),
        scratch_shapes=[pltpu.VMEM((2, 8, tn), F32)],
        name="ffn_up_conv", compiler_params=_cp(("parallel", "arbitrary"), VMEM_LIMIT),
    )(hn, w_up_bf, w_up_bf, cw, cb)


def _ffn_down_loss(act, w_down_bf, h1, g_final, target):
    S = h1.shape[0]
    tm = _pick(S, (512, 256))

    def body(a_ref, w_ref, h1_ref, g_ref, t_ref, loss_ref, dh_ref, dhb_ref, dg_ref):
        i = pl.program_id(0)

        @pl.when(i == 0)
        def _():
            loss_ref[...] = jnp.zeros_like(loss_ref)
            dg_ref[...] = jnp.zeros_like(dg_ref)

        hf = h1_ref[...] + jnp.dot(a_ref[...], w_ref[...], preferred_element_type=F32)
        g = g_ref[...]
        r = lax.rsqrt(jnp.mean(hf * hf, axis=-1, keepdims=True) + EPS)
        hhat = hf * r
        err = hhat * g - t_ref[...]
        loss_ref[...] += 0.5 * jnp.sum(jnp.mean(err * err, axis=-1, keepdims=True))
        dy = err * (1.0 / D_MODEL)
        dg_ref[0:1, :] += jnp.sum(dy * hhat, axis=0, keepdims=True)
        dhat = dy * g
        dh = r * (dhat - hhat * jnp.mean(dhat * hhat, axis=-1, keepdims=True))
        dh_ref[...] = dh
        dhb_ref[...] = dh.astype(BF16)

    row = pl.BlockSpec((tm, D_MODEL), lambda i: (i, 0))
    return pl.pallas_call(
        body,
        out_shape=(jax.ShapeDtypeStruct((8, LANES), F32), jax.ShapeDtypeStruct((S, D_MODEL), F32),
                   jax.ShapeDtypeStruct((S, D_MODEL), BF16), jax.ShapeDtypeStruct((8, D_MODEL), F32)),
        grid=(S // tm,),
        in_specs=[pl.BlockSpec((tm, D_FF), lambda i: (i, 0)), pl.BlockSpec((D_FF, D_MODEL), lambda i: (0, 0)), row,
                  pl.BlockSpec((1, D_MODEL), lambda i: (0, 0)), row],
        out_specs=(pl.BlockSpec((8, LANES), lambda i: (0, 0)), row, row, pl.BlockSpec((8, D_MODEL), lambda i: (0, 0))),
        name="ffn_down_loss", compiler_params=_cp(("arbitrary",), VMEM_LIMIT),
    )(act, w_down_bf, h1, g_final, target)


def _ffn_up_dx_rms(dhu, w_up_bf, h1, g_ffn, dh2):
    _, S, F = dhu.shape
    tm = _pick(S, (512, 256))

    def body(a_ref, b_ref, h_ref, g_ref, r_ref, dh_ref, dhb_ref, dg_ref, acc_ref):
        i, k = pl.program_id(0), pl.program_id(1)

        @pl.when((i == 0) & (k == 0))
        def _():
            dg_ref[...] = jnp.zeros_like(dg_ref)

        part = jnp.dot(a_ref[...], b_ref[...], preferred_element_type=F32)

        @pl.when(k == 0)
        def _():
            acc_ref[...] = part

        @pl.when(k == 1)
        def _():
            dyv = acc_ref[...] + part
            hf = h_ref[...]
            r = lax.rsqrt(jnp.mean(hf * hf, axis=-1, keepdims=True) + EPS)
            hhat = hf * r
            dg_ref[0:1, :] += jnp.sum(dyv * hhat, axis=0, keepdims=True)
            dhat = dyv * g_ref[...]
            dh = r_ref[...] + r * (dhat - hhat * jnp.mean(dhat * hhat, axis=-1, keepdims=True))
            dh_ref[...] = dh
            dhb_ref[...] = dh.astype(BF16)

    row = pl.BlockSpec((tm, D_MODEL), lambda i, k: (i, 0))
    return pl.pallas_call(
        body,
        out_shape=(jax.ShapeDtypeStruct((S, D_MODEL), F32), jax.ShapeDtypeStruct((S, D_MODEL), BF16),
                   jax.ShapeDtypeStruct((8, D_MODEL), F32)),
        grid=(S // tm, 2),
        in_specs=[pl.BlockSpec((None, tm, F), lambda i, k: (k, i, 0)), pl.BlockSpec((F, D_MODEL), lambda i, k: (k, 0)),
                  row, pl.BlockSpec((1, D_MODEL), lambda i, k: (0, 0)), row],
        out_specs=(row, row, pl.BlockSpec((8, D_MODEL), lambda i, k: (0, 0))),
        scratch_shapes=[pltpu.VMEM((tm, D_MODEL), F32)],
        name="ffn_up_dx_rms", compiler_params=_cp(("arbitrary", "arbitrary"), VMEM_LIMIT),
    )(dhu, w_up_bf, h1, g_ffn, dh2)


def _conv_gate_bwd(hc, hu, dact, cw):
    _, S, F = hu.shape
    tm = _pick(S, (CONV_TM, 128))
    tn = _pick(F, (CONV_TN, 256, 128))
    r8 = tm // 8
    n_i = S // tm
    last8 = S // 8 - 1

    def body(hc_ref, hcn_ref, hu_ref, da_ref, dan_ref, w_ref, dhu_ref, dcw_ref):
        i = pl.program_id(1)

        @pl.when(i == 0)
        def _():
            dcw_ref[...] = jnp.zeros_like(dcw_ref)

        rid8 = lax.broadcasted_iota(jnp.int32, (8, tn), 0)

        def gate_grads(a, g, d):
            sg = _sigmoid(g)
            return d * (g * sg), d * a * (sg * (1.0 + g * (1.0 - sg)))

        dhc = gate_grads(hc_ref[0], hc_ref[1], da_ref[...])
        dhc_n = gate_grads(hcn_ref[0], hcn_ref[1], dan_ref[...])
        for h in range(2):
            w = w_ref[h]
            d = dhc[h]
            dn = jnp.where(i < n_i - 1, dhc_n[h], 0.0)
            u1 = _shift_rows_up(d, dn, 1)
            u2 = _shift_rows_up(d, dn, 2)
            dhu_ref[h] = (w[2:3, :] * d + w[1:2, :] * u1 + w[0:1, :] * u2).astype(BF16)
            x = hu_ref[h]
            t0, t1, t2, t3 = [jnp.sum(t, axis=0, keepdims=True) for t in (u2 * x, u1 * x, d * x, d)]
            dcw_ref[h] += jnp.where(rid8 == 0, t0, jnp.where(rid8 == 1, t1, jnp.where(rid8 == 2, t2, jnp.where(rid8 == 3, t3, 0.0))))

    cur = pl.BlockSpec((2, tm, tn), lambda j, i: (0, i, j))
    return pl.pallas_call(
        body,
        out_shape=(jax.ShapeDtypeStruct((2, S, F), BF16), jax.ShapeDtypeStruct((2, 8, F), F32)),
        grid=(F // tn, n_i),
        in_specs=[cur, pl.BlockSpec((2, 8, tn), lambda j, i: (0, jnp.minimum((i + 1) * r8, last8), j)), cur,
                  pl.BlockSpec((tm, tn), lambda j, i: (i, j)),
                  pl.BlockSpec((8, tn), lambda j, i: (jnp.minimum((i + 1) * r8, last8), j)),
                  pl.BlockSpec((2, 8, tn), lambda j, i: (0, 0, j))],
        out_specs=(cur, pl.BlockSpec((2, 8, tn), lambda j, i: (0, 0, j))),
        name="conv_gate_bwd", compiler_params=_cp(("parallel", "arbitrary"), VMEM_LIMIT),
    )(hc, hc, hu, dact, dact, cw)


def _out_proj_dx_prep(dh1_bf, w_out_bf, att, lse, qa):
    S = att.shape[0]
    tm = _pick(S, (256,))

    def body(dh_ref, w_ref, o_ref, lse_ref, q_ref, dsg_ref, qb_ref, doa_ref):
        lane = lax.broadcasted_iota(jnp.int32, (tm, LANES), 1)
        dh = dh_ref[...]
        dsg_ref[...] = lax.dot_general(dh, w_ref[ATT_WIDTH:, :], _NT, preferred_element_type=F32)
        datt = lax.dot_general(dh, w_ref[:ATT_WIDTH, :], _NT, preferred_element_type=F32)
        for p in range(N_PAIRS):
            pc = slice(p * LANES, (p + 1) * LANES)
            do = datt[:, pc]
            prod = o_ref[:, pc] * do
            for hh in range(2):
                sel = (lane >= HEAD_DIM) if hh else (lane < HEAD_DIM)
                delta = jnp.sum(jnp.where(sel, prod, 0.0), axis=-1, keepdims=True)
                dod = pltpu.roll(do, HEAD_DIM, 1) if hh else do
                cols = slice((2 * p + hh) * LANES, (2 * p + hh + 1) * LANES)
                doa_ref[:, cols] = jnp.where(lane < HEAD_DIM, dod, _aug(lane, _split3f(-delta))).astype(BF16)
                lcol = p * LANES + hh * HEAD_DIM
                l3 = _split3f(-lse_ref[:, lcol:lcol + 1])
                augl = jnp.where(lane == HEAD_DIM + 6, l3[0], jnp.where(lane == HEAD_DIM + 7, l3[1], l3[2])).astype(BF16)
                qb_ref[:, cols] = jnp.where((lane >= HEAD_DIM + 6) & (lane < HEAD_DIM + 9), augl, q_ref[:, cols])

    half = pl.BlockSpec((tm, ATT_WIDTH), lambda i: (i, 0))
    wide = pl.BlockSpec((tm, N_HEADS * LANES), lambda i: (i, 0))
    return pl.pallas_call(
        body,
        out_shape=(jax.ShapeDtypeStruct((S, GMLP_WIDTH), F32), jax.ShapeDtypeStruct(qa.shape, BF16),
                   jax.ShapeDtypeStruct(qa.shape, BF16)),
        grid=(S // tm,),
        in_specs=[pl.BlockSpec((tm, D_MODEL), lambda i: (i, 0)), pl.BlockSpec((D_MODEL, D_MODEL), lambda i: (0, 0)),
                  half, half, wide],
        out_specs=(half, wide, wide),
        name="out_proj_dx_prep", compiler_params=_cp(("parallel",), VMEM_LIMIT),
    )(dh1_bf, w_out_bf, att, lse, qa)


def _attn_bwd(qb, ka, va, doa):
    S = qb.shape[0]
    tk = _pick(S, (512, 256))
    tq = tk
    nq = S // tq

    def pair(a, scale=None):
        lane = lax.broadcasted_iota(jnp.int32, (a.shape[0], LANES), 1)
        out = jnp.where(lane < HEAD_DIM, a[:, :LANES], pltpu.roll(a[:, LANES:], HEAD_DIM, 1))
        return out if scale is None else out * scale

    def head_lanes(a, col, sign, first):
        lane = lax.broadcasted_iota(jnp.int32, (a.shape[0], LANES), 1)
        return jnp.where(lane == first, sign * a[:, col:col + 1],
                         jnp.where(lane == first + 1, sign * a[:, LANES + col:LANES + col + 1], 0.0))

    def body(q_ref, do_ref, k_ref, v_ref, dqc_ref, dkc_ref, dvc_ref, dcq_ref, dck_ref, dq_ref, dka_ref, dva_ref):
        kb = pl.program_id(1)

        @pl.when(kb == 0)
        def _():
            dq_ref[...] = jnp.zeros_like(dq_ref)

        dka_ref[...] = jnp.zeros_like(dka_ref)
        dva_ref[...] = jnp.zeros_like(dva_ref)
        rid = lax.broadcasted_iota(jnp.int32, (tk, tq), 0)
        cid = lax.broadcasted_iota(jnp.int32, (tk, tq), 1)

        def sub_tile(qs, q_len, k_off, k_len, masked):
            keys = slice(k_off, k_off + k_len)
            for h in range(2):
                cols = slice(h * LANES, (h + 1) * LANES)
                qblk = q_ref[pl.ds(qs, q_len), cols]
                doblk = do_ref[pl.ds(qs, q_len), cols]
                kh = k_ref[keys, cols]
                p = jnp.exp(lax.dot_general(kh, qblk, _NT, preferred_element_type=F32))
                if masked:
                    p = jnp.where(cid >= rid, p, 0.0)
                ds = (p * lax.dot_general(v_ref[keys, cols], doblk, _NT, preferred_element_type=F32)).astype(BF16)
                dva_ref[keys, cols] += jnp.dot(p.astype(BF16), doblk, preferred_element_type=F32)
                dka_ref[keys, cols] += jnp.dot(ds, qblk, preferred_element_type=F32)
                dq_ref[pl.ds(qs, q_len), cols] += lax.dot_general(ds, kh, _TN, preferred_element_type=F32)

        sub_tile(pl.multiple_of(kb * tq, tq), tq, 0, tk, True)

        def step(qi, carry):
            sub_tile(pl.multiple_of(qi * tq, tq), tq, 0, tk, False)
            return carry

        lax.fori_loop(kb + 1, nq, step, 0)
        dka = dka_ref[...]
        dkc_ref[...] = pair(dka).astype(BF16)
        dvc_ref[...] = pair(dva_ref[...]).astype(BF16)
        first = 2 * pl.program_id(0)
        dck_ref[...] = head_lanes(dka, HEAD_DIM + 3, -1.0, first)

        @pl.when(kb == nq - 1)
        def _():
            dqa = dq_ref[...]
            dqc_ref[...] = pair(dqa, HEAD_DIM ** -0.5).astype(BF16)
            dcq_ref[...] = head_lanes(dqa, HEAD_DIM, 1.0, first)

    wide = 2 * LANES
    half = jax.ShapeDtypeStruct((S, ATT_WIDTH), BF16)
    slabs = jax.ShapeDtypeStruct((N_PAIRS, S, LANES), F32)
    return pl.pallas_call(
        body,
        out_shape=(half, half, half, slabs, slabs),
        grid=(N_PAIRS, nq),
        in_specs=[pl.BlockSpec((S, wide), lambda p, j: (0, p)), pl.BlockSpec((S, wide), lambda p, j: (0, p)),
                  pl.BlockSpec((tk, wide), lambda p, j: (j, p)), pl.BlockSpec((tk, wide), lambda p, j: (j, p))],
        out_specs=(pl.BlockSpec((S, LANES), lambda p, j: (0, p)), pl.BlockSpec((tk, LANES), lambda p, j: (j, p)),
                   pl.BlockSpec((tk, LANES), lambda p, j: (j, p)), pl.BlockSpec((None, S, LANES), lambda p, j: (p, 0, 0)),
                   pl.BlockSpec((None, tk, LANES), lambda p, j: (p, j, 0))),
        scratch_shapes=[pltpu.VMEM((S, wide), F32), pltpu.VMEM((tk, wide), F32), pltpu.VMEM((tk, wide), F32)],
        name="attn_bwd", compiler_params=_cp(("parallel", "arbitrary"), VMEM_LIMIT),
    )(qb, doa, ka, va)


def _gmlp_bwd(ug, dsg, gain, w_s, wt_s, bias_full):
    S = ug.shape[0]
    tm = _pick(S, (512, 256, 128))
    n_chunks = tm // CHUNK
    n_i = S // tm
    ones = _group_ones()
    nt = (((1,), (1,)), ((), ()))

    def body(ug_ref, dsg_ref, gain_ref, w_ref, wt_ref, bias_ref, ones_ref, dug_ref, dw_ref, dgain_ref, dbias_ref,
             dbacc_ref):
        i = pl.program_id(0)

        @pl.when(i == 0)
        def _():
            dw_ref[...] = jnp.zeros_like(dw_ref)
            dgain_ref[...] = jnp.zeros_like(dgain_ref)
            dbacc_ref[...] = jnp.zeros_like(dbacc_ref)

        ones_m = ones_ref[...]
        pu = ug_ref[:, :GMLP_WIDTH]
        pg = ug_ref[:, GMLP_WIDTH:]
        u = _gelu(pu)
        vr = _gelu(pg)
        ms = _dot3(vr * vr, ones_m) * (1.0 / GROUP_DIM)
        rinv = lax.rsqrt(ms + EPS)
        vhat = vr * rinv
        gain_v = gain_ref[...]
        vn = (vhat * gain_v).astype(BF16)
        mixed = _gmlp_mixed(vn, w_ref, bias_ref[...], n_chunks)
        dsg_v = dsg_ref[...]
        du = dsg_v * mixed
        dmixed = dsg_v * u
        dm_bf = dmixed.astype(BF16)
        lane = lax.broadcasted_iota(jnp.int32, (CHUNK, LANES), 1)
        row = lax.broadcasted_iota(jnp.int32, (CHUNK, CHUNK), 0)
        col = lax.broadcasted_iota(jnp.int32, (CHUNK, CHUNK), 1)
        wts = [jnp.where(col >= row, wt_ref[g], 0.0).astype(BF16) for g in range(N_GROUPS)]
        dvn_rows = []
        dbsum = jnp.zeros((CHUNK, GMLP_WIDTH), F32)
        for ci in range(n_chunks):
            rs = slice(ci * CHUNK, (ci + 1) * CHUNK)
            dbsum = dbsum + dmixed[rs, :]
            cols = []
            for pp in range(N_GROUPS // 2):
                cs = slice(pp * LANES, (pp + 1) * LANES)
                dm = dm_bf[rs, cs]
                dm_lo = jnp.where(lane < GROUP_DIM, dm, jnp.zeros_like(dm))
                dm_hi = jnp.where(lane >= GROUP_DIM, dm, jnp.zeros_like(dm))
                vb = vn[rs, cs]
                dw_ref[2 * pp] += lax.dot_general(dm_lo, vb, nt, preferred_element_type=F32)
                dw_ref[2 * pp + 1] += lax.dot_general(dm_hi, vb, nt, preferred_element_type=F32)
                cols.append(jnp.dot(wts[2 * pp], dm_lo, preferred_element_type=F32)
                            + jnp.dot(wts[2 * pp + 1], dm_hi, preferred_element_type=F32))
            dvn_rows.append(jnp.concatenate(cols, axis=1))
        dvn = jnp.concatenate(dvn_rows, axis=0)
        dbacc_ref[...] += dbsum
        dgain_ref[0:1, :] += jnp.sum(dvn * vhat, axis=0, keepdims=True)
        dvhat = dvn * gain_v
        gm = _dot3(dvhat * vhat, ones_m) * (1.0 / GROUP_DIM)
        dvr = rinv * (dvhat - vhat * gm)
        dug_ref[:, :GMLP_WIDTH] = (du * _gelu_grad(pu)).astype(BF16)
        dug_ref[:, GMLP_WIDTH:] = (dvr * _gelu_grad(pg)).astype(BF16)

        @pl.when(i == n_i - 1)
        def _():
            for g in range(N_GROUPS):
                dw_ref[g] = jnp.where(row >= col, dw_ref[g], 0.0)
            dbias_ref[...] = _dot3(dbacc_ref[...], ones_m)

    return pl.pallas_call(
        body,
        out_shape=(jax.ShapeDtypeStruct((S, 2 * GMLP_WIDTH), BF16), jax.ShapeDtypeStruct((N_GROUPS, CHUNK, CHUNK), F32),
                   jax.ShapeDtypeStruct((8, GMLP_WIDTH), F32), jax.ShapeDtypeStruct((CHUNK, GMLP_WIDTH), F32)),
        grid=(n_i,),
        in_specs=[pl.BlockSpec((tm, 2 * GMLP_WIDTH), lambda i: (i, 0)), pl.BlockSpec((tm, GMLP_WIDTH), lambda i: (i, 0)),
                  pl.BlockSpec((1, GMLP_WIDTH), lambda i: (0, 0)),
                  pl.BlockSpec((N_GROUPS, CHUNK, CHUNK), lambda i: (0, 0, 0)),
                  pl.BlockSpec((N_GROUPS, CHUNK, CHUNK), lambda i: (0, 0, 0)),
                  pl.BlockSpec((CHUNK, GMLP_WIDTH), lambda i: (0, 0)),
                  pl.BlockSpec((GMLP_WIDTH, GMLP_WIDTH), lambda i: (0, 0))],
        out_specs=(pl.BlockSpec((tm, 2 * GMLP_WIDTH), lambda i: (i, 0)),
                   pl.BlockSpec((N_GROUPS, CHUNK, CHUNK), lambda i: (0, 0, 0)),
                   pl.BlockSpec((8, GMLP_WIDTH), lambda i: (0, 0)),
                   pl.BlockSpec((CHUNK, GMLP_WIDTH), lambda i: (0, 0))),
        scratch_shapes=[pltpu.VMEM((CHUNK, GMLP_WIDTH), F32)],
        name="gmlp_bwd", compiler_params=_cp(("arbitrary",), VMEM_LIMIT),
    )(ug, dsg, gain, w_s, wt_s, bias_full, ones)


def _gate_bwd(dcq, dck, zf):
    S = zf.shape[0]
    tm = _pick(S, (256,))
    n_i = S // tm
    triu = (lax.broadcasted_iota(jnp.int32, (tm, tm), 0) <= lax.broadcasted_iota(jnp.int32, (tm, tm), 1)).astype(BF16)

    def body(dcq_ref, dck_ref, zf_ref, tri_ref, dzf_ref, dbf_ref, carry_ref):
        i = pl.program_id(0)

        @pl.when(i == 0)
        def _():
            carry_ref[...] = jnp.zeros_like(carry_ref)
            dbf_ref[...] = jnp.zeros_like(dbf_ref)

        lane = lax.broadcasted_iota(jnp.int32, (tm, LANES), 1)
        dc = dcq_ref[0] + dck_ref[0]
        for p in range(1, N_PAIRS):
            dc = dc + (dcq_ref[p] + dck_ref[p])
        dlf = _dot3l(tri_ref[...], dc) + carry_ref[0:1, :]
        carry_ref[0:1, :] = dlf[0:1, :]
        dz = jnp.where(lane < N_HEADS, dlf * _sigmoid(-zf_ref[...]), 0.0)
        dzf_ref[...] = dz.astype(BF16)
        dbf_ref[0:1, :] += jnp.sum(dz, axis=0, keepdims=True)

    return pl.pallas_call(
        body,
        out_shape=(jax.ShapeDtypeStruct((S, LANES), BF16), jax.ShapeDtypeStruct((8, LANES), F32)),
        grid=(n_i,),
        in_specs=[pl.BlockSpec((N_PAIRS, tm, LANES), lambda i: (0, n_i - 1 - i, 0)),
                  pl.BlockSpec((N_PAIRS, tm, LANES), lambda i: (0, n_i - 1 - i, 0)),
                  pl.BlockSpec((tm, LANES), lambda i: (n_i - 1 - i, 0)),
                  pl.BlockSpec((tm, tm), lambda i: (0, 0))],
        out_specs=(pl.BlockSpec((tm, LANES), lambda i: (n_i - 1 - i, 0)), pl.BlockSpec((8, LANES), lambda i: (0, 0))),
        scratch_shapes=[pltpu.VMEM((8, LANES), F32)],
        name="gate_bwd", compiler_params=_cp(("arbitrary",), VMEM_LIMIT),
    )(dcq, dck, zf, triu)


def _out_proj_fwd(att_bf, sg, w_out_bf, x, g_ffn):
    S = x.shape[0]
    tm = _pick(S, (512, 256))

    def body(a_ref, s_ref, w_ref, x_ref, g_ref, h_ref, hn_ref):
        h = (x_ref[...] + jnp.dot(a_ref[...], w_ref[:ATT_WIDTH, :], preferred_element_type=F32)
             + jnp.dot(s_ref[...], w_ref[ATT_WIDTH:, :], preferred_element_type=F32))
        h_ref[...] = h
        r = lax.rsqrt(jnp.mean(h * h, axis=-1, keepdims=True) + EPS)
        hn_ref[...] = ((h * r) * g_ref[...]).astype(BF16)

    row = pl.BlockSpec((tm, D_MODEL), lambda i: (i, 0))
    half = pl.BlockSpec((tm, ATT_WIDTH), lambda i: (i, 0))
    return pl.pallas_call(
        body, out_shape=(jax.ShapeDtypeStruct((S, D_MODEL), F32), jax.ShapeDtypeStruct((S, D_MODEL), BF16)),
        grid=(S // tm,),
        in_specs=[half, half, pl.BlockSpec((D_MODEL, D_MODEL), lambda i: (0, 0)), row,
                  pl.BlockSpec((1, D_MODEL), lambda i: (0, 0))],
        out_specs=(row, row), name="out_proj", compiler_params=_cp(("parallel",), VMEM_LIMIT),
    )(att_bf, sg, w_out_bf, x, g_ffn)


def _out_proj_dw(att_bf, sg, dh1_bf):
    S = att_bf.shape[0]
    tk = _pick(S, (1024, 512))

    def body(a_ref, s_ref, d_ref, o_ref):
        k = pl.program_id(0)

        @pl.when(k == 0)
        def _():
            o_ref[...] = jnp.zeros_like(o_ref)

        d = d_ref[...]
        o_ref[:ATT_WIDTH, :] += lax.dot_general(a_ref[...], d, _TN, preferred_element_type=F32)
        o_ref[ATT_WIDTH:, :] += lax.dot_general(s_ref[...], d, _TN, preferred_element_type=F32)

    half = pl.BlockSpec((tk, ATT_WIDTH), lambda k: (k, 0))
    return pl.pallas_call(
        body, out_shape=jax.ShapeDtypeStruct((D_MODEL, D_MODEL), F32), grid=(S // tk,),
        in_specs=[half, half, pl.BlockSpec((tk, D_MODEL), lambda k: (k, 0))],
        out_specs=pl.BlockSpec((D_MODEL, D_MODEL), lambda k: (0, 0)),
        name="out_proj_dw", compiler_params=_cp(("arbitrary",), VMEM_LIMIT),
    )(att_bf, sg, dh1_bf)


_IN_PIECES = ((0, ATT_WIDTH), (ATT_WIDTH, ATT_WIDTH), (2 * ATT_WIDTH, ATT_WIDTH), (QKV, 2 * GMLP_WIDTH), (UG_END, LANES))


def _inproj_bwd_dx(pieces, w_pad, x, g_mix, dh1):
    S = x.shape[0]
    tm = _pick(S, (512, 256))

    def body(*refs):
        p_refs, (w_ref, x_ref, g_ref, r_ref, dx_ref, dg_ref) = refs[:5], refs[5:]
        i = pl.program_id(0)

        @pl.when(i == 0)
        def _():
            dg_ref[...] = jnp.zeros_like(dg_ref)

        dxn = None
        for p_ref, (c0, width) in zip(p_refs, _IN_PIECES):
            part = jnp.dot(p_ref[...], w_ref[c0:c0 + width, :], preferred_element_type=F32)
            dxn = part if dxn is None else dxn + part
        xf = x_ref[...]
        r = lax.rsqrt(jnp.mean(xf * xf, axis=-1, keepdims=True) + EPS)
        xhat = xf * r
        dg_ref[0:1, :] += jnp.sum(dxn * xhat, axis=0, keepdims=True)
        dhat = dxn * g_ref[...]
        dx_ref[...] = r_ref[...] + r * (dhat - xhat * jnp.mean(dhat * xhat, axis=-1, keepdims=True))

    row = pl.BlockSpec((tm, D_MODEL), lambda i: (i, 0))
    return pl.pallas_call(
        body, out_shape=(jax.ShapeDtypeStruct((S, D_MODEL), F32), jax.ShapeDtypeStruct((8, D_MODEL), F32)),
        grid=(S // tm,),
        in_specs=[pl.BlockSpec((tm, width), lambda i: (i, 0)) for _, width in _IN_PIECES]
        + [pl.BlockSpec((IN_PAD, D_MODEL), lambda i: (0, 0)), row, pl.BlockSpec((1, D_MODEL), lambda i: (0, 0)), row],
        out_specs=(row, pl.BlockSpec((8, D_MODEL), lambda i: (0, 0))),
        name="in_proj_dx", compiler_params=_cp(("arbitrary",), VMEM_LIMIT),
    )(*pieces, w_pad, x, g_mix, dh1)


def _inproj_bwd_dw(xn, pieces):
    S = xn.shape[0]
    tk = _pick(S, (1024, 512))

    def body(*refs):
        x_ref, p_refs, o_ref = refs[0], refs[1:6], refs[6]
        k = pl.program_id(0)

        @pl.when(k == 0)
        def _():
            o_ref[...] = jnp.zeros_like(o_ref)

        xb = x_ref[...]
        for p_ref, (c0, width) in zip(p_refs, _IN_PIECES):
            o_ref[:, c0:c0 + width] += lax.dot_general(xb, p_ref[...], _TN, preferred_element_type=F32)

    return pl.pallas_call(
        body, out_shape=jax.ShapeDtypeStruct((D_MODEL, IN_PAD), F32), grid=(S // tk,),
        in_specs=[pl.BlockSpec((tk, D_MODEL), lambda k: (k, 0))]
        + [pl.BlockSpec((tk, width), lambda k: (k, 0)) for _, width in _IN_PIECES],
        out_specs=pl.BlockSpec((D_MODEL, IN_PAD), lambda k: (0, 0)),
        name="in_proj_dw", compiler_params=_cp(("arbitrary",), VMEM_LIMIT),
    )(xn, *pieces)


def _adamw(w, m, v, parts, name):
    R, C = w.shape[-2:]
    tr = R
    for cand in (256, 128, 64, 32, 16, 8):
        if R % cand == 0 and R > cand:
            tr = cand
            break
    c1 = 1.0 / (1.0 - ADAM_B1 ** ADAM_STEP)
    c2 = 1.0 / (1.0 - ADAM_B2 ** ADAM_STEP)

    def body(w_ref, m_ref, v_ref, p_ref, g_ref, d_ref, nm_ref, nv_ref):
        g = p_ref[0].astype(F32)
        for j in range(1, N_DEV):
            g = g + p_ref[j].astype(F32)
        g_ref[...] = g
        nm = ADAM_B1 * m_ref[...] + (1.0 - ADAM_B1) * g
        nv = ADAM_B2 * v_ref[...] + (1.0 - ADAM_B2) * (g * g)
        nm_ref[...] = nm
        nv_ref[...] = nv
        d_ref[...] = -ADAM_LR * ((nm * c1) / (jnp.sqrt(nv * c2) + ADAM_EPS) + ADAM_WD * w_ref[...])

    if w.ndim == 3:
        spec = pl.BlockSpec((None, tr, C), lambda i: (0, i, 0))
    else:
        spec = pl.BlockSpec((tr, C), lambda i: (i, 0))
    shp = jax.ShapeDtypeStruct(w.shape, F32)
    return pl.pallas_call(
        body, out_shape=(shp, shp, shp, shp), grid=(R // tr,),
        in_specs=[spec, spec, spec, pl.BlockSpec((N_DEV, tr, C), lambda i: (0, i, 0))],
        out_specs=(spec, spec, spec, spec),
        name=name, compiler_params=_cp(("parallel",), VMEM_LIMIT),
    )(w, m, v, parts)


def _place():
    x, y, c = lax.axis_index("x"), lax.axis_index("y"), lax.axis_index("c")
    return x, y, c


def _all_gather(blocks, name):
    n = len(blocks)

    def body(*refs):
        ins, outs = refs[:n], refs[n:2 * n]
        send_sems, recv_sems, local_sems = refs[2 * n:]
        x, y, c = _place()
        me, sibling = (x, y, c), (x, y, 1 - c)
        chips = [(1 - x, y), (x, 1 - y), (1 - x, 1 - y)]
        sends = []
        for a in range(n):
            out = outs[a]

            def slot(px, py, pc, out=out):
                return out.at[4 * px + 2 * py + pc]

            def copy(k, block, to, src=None, a=a, slot=slot):
                return pltpu.make_async_remote_copy(
                    src_ref=slot(*block) if src is None else src, dst_ref=slot(*block),
                    send_sem=send_sems.at[a, k], recv_sem=recv_sems.at[a, k], device_id=to, device_id_type=MESH)

            mine = pltpu.make_async_copy(ins[a], slot(*me), local_sems.at[a])
            mine.start()
            first = [copy(0, me, sibling, src=ins[a])]
            first += [copy(1 + j, me, (*chip, c), src=ins[a]) for j, chip in enumerate(chips)]
            for cp in first:
                cp.start()
            sends.append((mine, first, copy))
        for a in range(n):
            mine, first, copy = sends[a]
            passed = [copy(4 + j, (*chip, c), sibling) for j, chip in enumerate(chips)]
            for j, chip in enumerate(chips):
                copy(1 + j, (*chip, c), me).wait_recv()
                passed[j].start()
            copy(0, sibling, me).wait_recv()
            for j, chip in enumerate(chips):
                copy(4 + j, (*chip, 1 - c), me).wait_recv()
            for cp in first + passed:
                cp.wait_send()
            mine.wait()

    any_spec = pl.BlockSpec(memory_space=pl.ANY)
    return pl.pallas_call(
        body, out_shape=tuple(jax.ShapeDtypeStruct((N_DEV,) + b.shape, b.dtype) for b in blocks),
        in_specs=[any_spec] * n, out_specs=tuple([any_spec] * n),
        scratch_shapes=[pltpu.SemaphoreType.DMA((n, 7)), pltpu.SemaphoreType.DMA((n, 7)), pltpu.SemaphoreType.DMA((n,))],
        name=name,
    )(*blocks)


_HBM = pl.BlockSpec(memory_space=pltpu.HBM)
_SEM = pl.BlockSpec(memory_space=pltpu.SEMAPHORE)
_EFFECT = pltpu.SideEffectType.DATAFLOW_SIDE_EFFECTING


def _peers(x, y, c):
    out = []
    for k in range(1, N_DEV):
        px, py, pc = x ^ ((k >> 2) & 1), y ^ ((k >> 1) & 1), c ^ (k & 1)
        out.append((k, (px, py, pc), 4 * px + 2 * py + pc))
    return out


def _xchg_copies(src_refs, land_refs, send_sems, recv_sems, scatter):
    x, y, c = _place()
    me = 4 * x + 2 * y + c
    copies = []
    for a, (src, land) in enumerate(zip(src_refs, land_refs)):
        for k, place, idx in _peers(x, y, c):
            j = a * (N_DEV - 1) + k - 1
            copies.append(pltpu.make_async_remote_copy(
                src_ref=src.at[idx] if scatter[a] else src, dst_ref=land.at[me],
                send_sem=send_sems[j], recv_sem=recv_sems[j], device_id=place, device_id_type=MESH))
    return copies


def _xchg_start(srcs, scatter, name):
    n = len(srcs)
    lands = [lax.empty((N_DEV,) + (s.shape[1:] if sc else s.shape), s.dtype) for s, sc in zip(srcs, scatter)]

    ns = n * (N_DEV - 1)

    def body(*refs):
        sems = refs[2 * n:2 * n + 2 * ns]
        for cp in _xchg_copies(refs[:n], refs[n:2 * n], sems[:ns], sems[ns:], scatter):
            cp.start()
        token = refs[-1]
        token[...] = jnp.zeros_like(token)

    both = list(srcs) + lands
    res = pl.pallas_call(
        body, name=name,
        out_shape=(*[pltpu.SemaphoreType.DMA(())] * (2 * ns),
                   *[pltpu.HBM(a.shape, a.dtype) for a in both], jax.ShapeDtypeStruct((8, LANES), F32)),
        in_specs=[_HBM] * (2 * n),
        out_specs=(*([_SEM] * (2 * ns)), *([_HBM] * (2 * n)), pl.BlockSpec(memory_space=pltpu.VMEM)),
        input_output_aliases={i: 2 * ns + i for i in range(2 * n)},
        compiler_params=pltpu.CompilerParams(has_side_effects=_EFFECT),
    )(*[pltpu.with_memory_space_constraint(a, pltpu.HBM) for a in both])
    return (tuple(res[:2 * ns]), tuple(res[2 * ns:2 * ns + 2 * n])), res[-1]


def _xchg_wait(handle, scatter, after, name):
    sems, thru = handle
    n = len(thru) // 2
    ns = len(sems) // 2

    def body(*refs):
        got = refs[2 * n:2 * n + 2 * ns]
        for cp in _xchg_copies(refs[:n], refs[n:2 * n], got[:ns], got[ns:], scatter):
            cp.wait_send()
            cp.wait_recv()

    outs = pl.pallas_call(
        body, name=name, out_shape=tuple(pltpu.HBM(a.shape, a.dtype) for a in thru),
        in_specs=[_HBM] * (2 * n) + [_SEM] * (2 * ns) + [pl.BlockSpec(memory_space=pl.ANY)],
        out_specs=tuple([_HBM] * (2 * n)), input_output_aliases={i: i for i in range(2 * n)},
        compiler_params=pltpu.CompilerParams(has_side_effects=_EFFECT),
    )(*thru, *sems, after)
    return outs[:n], outs[n:]


def _tie(a, token):
    return a if token is None else a + token[0, 0].astype(a.dtype)


def _rows128(a):
    flat = a.reshape(-1)
    rows = -(-flat.shape[0] // LANES)
    rows = -(-rows // 8) * 8
    return jnp.pad(flat, (0, rows * LANES - flat.shape[0])).reshape(rows, LANES)


def _local_step(x, target, norm_mix_g, w_in_t, b_forget, gmlp_norm_g, w_spatial, b_spatial, norm_ffn_g, conv_b,
                norm_final_g, rest_fn, send_fn, small_fn, token=None):
    f = D_FF
    g_mix = norm_mix_g.reshape(1, D_MODEL)
    w_pad = jnp.pad(w_in_t, ((0, IN_PAD - IN_COLS), (0, 0)))
    bf_pad = jnp.pad(b_forget.reshape(1, N_HEADS), ((0, 0), (0, LANES - N_HEADS)))
    xn, qa, ka, va, ug, zf = _inproj_fwd(x, _tie(g_mix, token), w_pad, bf_pad)
    bias_full = jnp.repeat(b_spatial.reshape(N_GROUPS, CHUNK).T, GROUP_DIM, axis=1)
    w_s = w_spatial.reshape(N_GROUPS, CHUNK, CHUNK)
    gain = gmlp_norm_g.reshape(1, GMLP_WIDTH)
    sg = _gmlp_fwd(ug, gain, w_s, bias_full)
    att, lse, att_bf = _attn_fwd(qa, ka, va)
    w_out_bf, w_up_bf, conv_w, w_down_bf = rest_fn(att_bf)
    g_ffn = norm_ffn_g.reshape(1, D_MODEL)
    h1, hn = _out_proj_fwd(att_bf, sg, w_out_bf, x, g_ffn)
    cw = jnp.pad(conv_w.reshape(3, 2, f).transpose(1, 0, 2), ((0, 0), (0, 5), (0, 0)))
    cb = conv_b.reshape(2, 1, f)
    hu, hc, act = _ffn_up_conv(hn, w_up_bf, cw, cb)
    loss_blk, dh2, dh2_bf, dg_final = _ffn_down_loss(act, w_down_bf, h1, norm_final_g.reshape(1, D_MODEL), target)
    dw_down = _mm(act, dh2_bf, mode="tn", out_dtype=F32, tm=1408, tn=1024, tk=2048, name="ffn_down_dw")
    dact = _mm(dh2_bf, w_down_bf, mode="nt", out_dtype=F32, tm=1024, tn=1408, tk=1024, outer="j", name="ffn_down_dx")
    dhu, dcw = _conv_gate_bwd(hc, hu, dact, _tie(cw, send_fn("w_down", dw_down)))
    dw_up = _mm(hn, dhu, mode="tn", out_dtype=F32, tm=1024, tn=1408, tk=2048, b_halves=True, outer="j", name="ffn_up_dw")
    dh1, dh1_bf, dg_ffn = _ffn_up_dx_rms(dhu, w_up_bf, h1, _tie(g_ffn, send_fn("w_up", dw_up)), dh2)
    dsg, qb, doa = _out_proj_dx_prep(dh1_bf, w_out_bf, att, lse, qa)
    dw_out = _out_proj_dw(att_bf, sg, dh1_bf)
    dq, dk, dv, dcq, dck = _attn_bwd(qb, ka, va, doa)
    wt_s = w_s.transpose(0, 2, 1)
    dug, dw_s, dgain, dbias = _gmlp_bwd(ug, dsg, _tie(gain, send_fn("w_out", dw_out)), w_s, wt_s, bias_full)
    dzf, dbf = _gate_bwd(dcq, dck, zf)
    grad_x, dg_mix = _inproj_bwd_dx((dq, dk, dv, dug, dzf), w_pad, x, g_mix, dh1)
    grads = dict(
        norm_mix_g=dg_mix[0:1, :],
        b_forget=dbf[0:1, :N_HEADS],
        gmlp_norm_g=dgain[0:1, :],
        w_spatial=dw_s,
        b_spatial=dbias[:, ::GROUP_DIM].T,
        norm_ffn_g=dg_ffn[0:1, :],
        conv_w=dcw[:, 0:3, :].transpose(1, 0, 2).reshape(3, 2 * f),
        conv_b=dcw[:, 3, :].reshape(1, 2 * f),
        norm_final_g=dg_final[0, :],
    )
    token = small_fn(loss_blk[0, 0], grads)
    dw_in = _inproj_bwd_dw(xn, (dq, dk, dv, dug, _tie(dzf, token)))
    return grad_x, send_fn("w_in", dw_in[:, :IN_COLS])


SMALL = ("norm_mix_g", "b_forget", "gmlp_norm_g", "w_spatial", "b_spatial", "norm_ffn_g", "conv_b", "norm_final_g")


def kernel(x, norm_mix_g, w_in, b_forget, gmlp_norm_g, w_spatial, b_spatial, w_out, norm_ffn_g, w_up, conv_w, conv_b, w_down, norm_final_g, loss_target, m_norm_mix_g, m_w_in, m_b_forget, m_gmlp_norm_g, m_w_spatial, m_b_spatial, m_w_out, m_norm_ffn_g, m_w_up, m_conv_w, m_conv_b, m_w_down, m_norm_final_g, v_norm_mix_g, v_w_in, v_b_forget, v_gmlp_norm_g, v_w_spatial, v_b_spatial, v_w_out, v_norm_ffn_g, v_w_up, v_conv_w, v_conv_b, v_w_down, v_norm_final_g):
    weights = dict(norm_mix_g=norm_mix_g, w_in=w_in, b_forget=b_forget, gmlp_norm_g=gmlp_norm_g, w_spatial=w_spatial,
                   b_spatial=b_spatial, w_out=w_out, norm_ffn_g=norm_ffn_g, w_up=w_up, conv_w=conv_w, conv_b=conv_b,
                   w_down=w_down, norm_final_g=norm_final_g)
    m_in = dict(norm_mix_g=m_norm_mix_g, w_in=m_w_in, b_forget=m_b_forget, gmlp_norm_g=m_gmlp_norm_g,
                w_spatial=m_w_spatial, b_spatial=m_b_spatial, w_out=m_w_out, norm_ffn_g=m_norm_ffn_g, w_up=m_w_up,
                conv_w=m_conv_w, conv_b=m_conv_b, w_down=m_w_down, norm_final_g=m_norm_final_g)
    v_in = dict(norm_mix_g=v_norm_mix_g, w_in=v_w_in, b_forget=v_b_forget, gmlp_norm_g=v_gmlp_norm_g,
                w_spatial=v_w_spatial, b_spatial=v_b_spatial, w_out=v_w_out, norm_ffn_g=v_norm_ffn_g, w_up=v_w_up,
                conv_w=v_conv_w, conv_b=v_conv_b, w_down=v_w_down, norm_final_g=v_norm_final_g)
    order = list(weights)
    me = 4 * lax.axis_index("x") + 2 * lax.axis_index("y") + lax.axis_index("c")
    n_in, n_up = w_in.shape[2], w_up.shape[2]
    r_out, r_down = w_out.shape[1], w_down.shape[1]

    def with_mine(landed, mine):
        return lax.dynamic_update_index_in_dim(landed, mine, me, 0)

    up_blk = w_up[0].T.astype(BF16)
    out_blk = w_out[0].astype(BF16)
    down_blk = w_down[0].astype(BF16)
    taps_blk = jnp.pad(conv_w[0], ((0, 5), (0, 0)))
    (in_all,) = _all_gather([w_in[0].T.astype(BF16)], "gather_w_in")
    in_all, rest_blocks = lax.optimization_barrier((in_all, [up_blk, out_blk, down_blk, taps_blk]))
    rest_handle, token = _xchg_start(rest_blocks, [False] * 4, "gather_rest_start")
    w_in_t = in_all.reshape(N_DEV * n_in, D_MODEL)

    def rest_fn(after):
        mine, landed = _xchg_wait(rest_handle, [False] * 4, after, "gather_rest_wait")
        up_all, out_all, down_all, taps_all = [with_mine(l, b) for l, b in zip(landed, mine)]
        return (out_all.reshape(N_DEV * r_out, D_MODEL), up_all.reshape(N_DEV * n_up, D_MODEL),
                taps_all[:, :3, :].transpose(1, 0, 2).reshape(3, N_DEV * n_up),
                down_all.reshape(N_DEV * r_down, D_MODEL))

    sent = {}

    def send_fn(name, grad):
        if name == "w_in":
            parts = grad.reshape(D_MODEL, N_DEV, -1).transpose(1, 0, 2).astype(BF16)
        elif name == "w_up":
            parts = grad.reshape(D_MODEL, N_DEV, -1).transpose(1, 0, 2)
        else:
            parts = grad.reshape(N_DEV, -1, D_MODEL)
        sent[name], tok = _xchg_start([parts], [True], "scatter_" + name + "_start")
        return tok

    small = {}

    def small_fn(loss_local, g):
        loss_rows = jnp.pad(loss_local.reshape(1, 1), ((0, 31), (0, LANES - 1)))
        packed = [_rows128(g[k]) for k in SMALL] + [loss_rows, _rows128(g["conv_w"])]
        small["sizes"] = [p.shape[0] for p in packed]
        small["handle"], tok = _xchg_start([jnp.concatenate(packed, axis=0)], [False], "gather_small_start")
        return tok

    grad_x, after = _local_step(
        x[0], loss_target[0], norm_mix_g, w_in_t, b_forget, gmlp_norm_g, w_spatial, b_spatial, norm_ffn_g, conv_b,
        norm_final_g, rest_fn, send_fn, small_fn, token)

    outs = {}

    def update_big(name, after):
        (parts,), (landed,) = _xchg_wait(sent[name], [True], after, "scatter_" + name + "_wait")
        got = with_mine(landed, lax.dynamic_index_in_dim(parts, me, 0, keepdims=False))
        outs[name] = tuple(_adamw(weights[name], m_in[name], v_in[name], got, "adamw_" + name))
        return outs[name][0]

    for name in ("w_down", "w_up", "w_out"):
        after = update_big(name, after)

    (mine,), (landed,) = _xchg_wait(small["handle"], [False], after, "gather_small_wait")
    small_all = with_mine(landed, mine)
    sizes = small["sizes"]
    n_small_rows = sum(sizes[:-2])

    def pack(src):
        return jnp.concatenate([_rows128(src[k]) for k in SMALL] + [jnp.zeros((sizes[-2], LANES), F32)], axis=0)

    n_adam_rows = n_small_rows + sizes[-2]
    sg_, sd_, sm_, sv_ = _adamw(pack(weights), pack(m_in), pack(v_in), small_all[:, :n_adam_rows, :], "adamw_small")
    loss = sg_[n_small_rows, 0]
    off = 0
    for k, rows in zip(SMALL, sizes[:-2]):
        shp = weights[k].shape
        cnt = math.prod(shp)
        outs[k] = tuple(a[off:off + rows].reshape(-1)[:cnt].reshape(shp) for a in (sg_, sd_, sm_, sv_))
        off += rows
    taps_parts = small_all[:, n_adam_rows:, :].reshape(N_DEV, -1)[:, :3 * N_DEV * n_up].reshape(N_DEV, 3, N_DEV * n_up)
    taps_mine = lax.dynamic_slice_in_dim(taps_parts, me * n_up, n_up, axis=2)
    taps_mine = jnp.pad(taps_mine, ((0, 0), (0, 5), (0, 0)))

    def pad8(a):
        return jnp.pad(a[0], ((0, 5), (0, 0)))

    res = _adamw(pad8(conv_w), pad8(m_conv_w), pad8(v_conv_w), taps_mine, "adamw_conv_w")
    outs["conv_w"] = tuple(a[:3][None] for a in res)
    update_big("w_in", sg_)

    return (loss, grad_x[None], *[outs[k][0] for k in order], *[outs[k][1] for k in order],
            *[outs[k][2] for k in order], *[outs[k][3] for k in order])
```

```python
import functools
import math

import jax
import jax.numpy as jnp
from jax import lax
from jax.experimental import pallas as pl
from jax.experimental.pallas import tpu as pltpu

F32 = jnp.float32
BF16 = jnp.bfloat16

N_DEV = 8
D_MODEL = 1024
ATT_WIDTH = 512
GMLP_WIDTH = 512
HEAD_DIM = 64
N_HEADS = 8
N_PAIRS = 4
N_GROUPS = 8
GROUP_DIM = 64
CHUNK = 128
D_FF = 2816
IN_COLS = 2568
IN_PAD = 2688
QKV = 1536
UG_END = 2560
EPS = 1e-6
LANES = 128

ADAM_LR = 0.001
ADAM_B1 = 0.9
ADAM_B2 = 0.999
ADAM_EPS = 1e-08
ADAM_WD = 0.01
ADAM_STEP = 10

ATT_TQ = 1024
ATT_TK = 1024
FFN_TM, FFN_TN = 512, 1408
CONV_TM, CONV_TN = 256, 1408
VMEM_LIMIT = 56 * 1024 * 1024
MESH = pl.DeviceIdType.MESH


def _cp(sem, vmem=None):
    return pltpu.CompilerParams(dimension_semantics=sem, vmem_limit_bytes=vmem)


def _pick(n, prefs):
    for p in prefs:
        if n % p == 0:
            return p
    return n


def _split3(x):
    hi = x.astype(BF16)
    r1 = x - hi.astype(F32)
    mid = r1.astype(BF16)
    lo = (r1 - mid.astype(F32)).astype(BF16)
    return hi, mid, lo


def _dot3(x, ones_bf):
    d = functools.partial(jnp.dot, preferred_element_type=F32)
    out = []
    for c in range(0, x.shape[1], 2 * LANES):
        blk = ones_bf[c:c + 2 * LANES, c:c + 2 * LANES]
        hi, mid, lo = _split3(x[:, c:c + 2 * LANES])
        out.append(d(hi, blk) + d(mid, blk) + d(lo, blk))
    return jnp.concatenate(out, axis=1)


def _dot3l(ones_bf, x):
    n = x.shape[1]
    y = jnp.dot(ones_bf, jnp.concatenate(_split3(x), axis=1), preferred_element_type=F32)
    return y[:, :n] + y[:, n:2 * n] + y[:, 2 * n:]


def _gelu(x):
    k = math.sqrt(2.0 / math.pi)
    t = jnp.tanh(k * (x + 0.044715 * (x * x * x)))
    return 0.5 * x * (1.0 + t)


def _gelu_grad(x):
    k = math.sqrt(2.0 / math.pi)
    x2 = x * x
    t = jnp.tanh(k * (x + 0.044715 * (x2 * x)))
    return 0.5 * (1.0 + t) + 0.5 * x * (1.0 - t * t) * (k * (1.0 + 3.0 * 0.044715 * x2))


def _sigmoid(x):
    return 1.0 / (1.0 + jnp.exp(-x))


def _mm(a, b, *, mode, out_dtype, tm, tn, tk, name, res=None, a_halves=False, b_halves=False,
        out_halves=False, outer="i"):
    if mode == "tn":
        K, M = a.shape[-2], a.shape[-1] * (2 if a_halves else 1)
    else:
        M, K = a.shape[-2], a.shape[-1] * (2 if a_halves else 1)
    if mode == "nt":
        N = b.shape[-2]
        assert b.shape[-1] == K
    else:
        N = b.shape[-1] * (2 if b_halves else 1)
    tm, tn, tk = min(tm, M), min(tn, N), min(tk, K)
    assert M % tm == 0 and N % tn == 0 and K % tk == 0, (name, M, N, K, tm, tn, tk)
    nm, nn, nk = M // tm, N // tn, K // tk

    def ij(g0, g1):
        return (g0, g1) if outer == "i" else (g1, g0)

    if mode == "nn":
        dims = (((1,), (0,)), ((), ()))
        if a_halves:
            nkh = nk // 2
            a_spec = pl.BlockSpec((None, tm, tk), lambda g0, g1, k: (k // nkh, ij(g0, g1)[0], k % nkh))
        else:
            a_spec = pl.BlockSpec((tm, tk), lambda g0, g1, k: (ij(g0, g1)[0], k))
        b_spec = pl.BlockSpec((tk, tn), lambda g0, g1, k: (k, ij(g0, g1)[1]))
    elif mode == "nt":
        dims = (((1,), (1,)), ((), ()))
        if a_halves:
            nkh = nk // 2
            a_spec = pl.BlockSpec((None, tm, tk), lambda g0, g1, k: (k // nkh, ij(g0, g1)[0], k % nkh))
        else:
            a_spec = pl.BlockSpec((tm, tk), lambda g0, g1, k: (ij(g0, g1)[0], k))
        b_spec = pl.BlockSpec((tn, tk), lambda g0, g1, k: (ij(g0, g1)[1], k))
    else:
        dims = (((0,), (0,)), ((), ()))
        if a_halves:
            nmh = nm // 2
            a_spec = pl.BlockSpec((None, tk, tm), lambda g0, g1, k: (ij(g0, g1)[0] // nmh, k, ij(g0, g1)[0] % nmh))
        else:
            a_spec = pl.BlockSpec((tk, tm), lambda g0, g1, k: (k, ij(g0, g1)[0]))
        if b_halves:
            nnh = nn // 2
            b_spec = pl.BlockSpec((None, tk, tn), lambda g0, g1, k: (ij(g0, g1)[1] // nnh, k, ij(g0, g1)[1] % nnh))
        else:
            b_spec = pl.BlockSpec((tk, tn), lambda g0, g1, k: (k, ij(g0, g1)[1]))
    if out_halves:
        nnh = nn // 2
        o_spec = pl.BlockSpec((None, tm, tn), lambda g0, g1, k: (ij(g0, g1)[1] // nnh, ij(g0, g1)[0], ij(g0, g1)[1] % nnh))
        o_shape = jax.ShapeDtypeStruct((2, M, N // 2), out_dtype)
    else:
        o_spec = pl.BlockSpec((tm, tn), lambda g0, g1, k: ij(g0, g1))
        o_shape = jax.ShapeDtypeStruct((M, N), out_dtype)
    in_specs = [a_spec, b_spec]
    args = [a, b]
    if res is not None:
        in_specs.append(pl.BlockSpec((tm, tn), lambda g0, g1, k: ij(g0, g1)))
        args.append(res)

    def body(*refs):
        if res is not None:
            a_ref, b_ref, r_ref, o_ref = refs[:4]
        else:
            a_ref, b_ref, o_ref = refs[:3]
            r_ref = None
        part = lax.dot_general(a_ref[...], b_ref[...], dims, preferred_element_type=F32)
        if nk == 1:
            if r_ref is not None:
                part = part + r_ref[...]
            o_ref[...] = part.astype(out_dtype)
            return
        acc_ref = refs[-1]
        k = pl.program_id(2)

        @pl.when(k == 0)
        def _():
            acc_ref[...] = part

        @pl.when(k > 0)
        def _():
            acc_ref[...] += part

        @pl.when(k == nk - 1)
        def _():
            tot = acc_ref[...]
            if r_ref is not None:
                tot = tot + r_ref[...]
            o_ref[...] = tot.astype(out_dtype)

    grid = (nm, nn, nk) if outer == "i" else (nn, nm, nk)
    scratch = [] if nk == 1 else [pltpu.VMEM((tm, tn), F32)]
    return pl.pallas_call(
        body, out_shape=o_shape, grid=grid, in_specs=in_specs, out_specs=o_spec, scratch_shapes=scratch,
        name=name, compiler_params=_cp(("parallel", "parallel", "arbitrary"), VMEM_LIMIT),
    )(*args)


def _aug(lane, terms):
    out = 0.0
    for j, t in enumerate(terms):
        out = jnp.where(lane == HEAD_DIM + j, t, out)
    return out


def _split3f(x):
    hi, mid, lo = _split3(x)
    return [hi.astype(F32), mid.astype(F32), lo.astype(F32)]


def _inproj_fwd(x, g_mix, w_pad, bf_pad):
    S = x.shape[0]
    tm = _pick(S, (512, 256))
    tri = (lax.broadcasted_iota(jnp.int32, (tm, tm), 0) >= lax.broadcasted_iota(jnp.int32, (tm, tm), 1)).astype(BF16)

    def body(x_ref, g_ref, w_ref, bf_ref, tri_ref, put_ref, one_ref, xn_ref, qa_ref, ka_ref, va_ref, ug_ref, zf_ref,
             carry_ref):
        i = pl.program_id(0)

        @pl.when(i == 0)
        def _():
            carry_ref[...] = jnp.zeros_like(carry_ref)

        xf = x_ref[...]
        r = lax.rsqrt(jnp.mean(xf * xf, axis=-1, keepdims=True) + EPS)
        xn = ((xf * r) * g_ref[...]).astype(BF16)
        xn_ref[...] = xn
        proj = lax.dot_general(xn, w_ref[...], _NT, preferred_element_type=F32)
        ug_ref[...] = proj[:, QKV:UG_END]
        zf = proj[:, UG_END:] + bf_ref[...]
        zf_ref[...] = zf
        lf = jnp.minimum(zf, 0.0) - jnp.log(1.0 + jnp.exp(-jnp.abs(zf)))
        c = _dot3l(tri_ref[...], lf) + carry_ref[0:1, :]
        carry_ref[0:1, :] = c[tm - 1:tm, :]
        c3 = jnp.concatenate(_split3(c), axis=1)
        aug_q = jnp.dot(c3, put_ref[0], preferred_element_type=F32) + one_ref[0:1, :]
        aug_k = jnp.dot(c3, put_ref[1], preferred_element_type=F32) + one_ref[1:2, :]
        lane = lax.broadcasted_iota(jnp.int32, (tm, LANES), 1)
        for h in range(N_HEADS):
            p, odd = h // 2, h % 2

            def head(base, scale=None, p=p, odd=odd):
                blk = proj[:, base + p * LANES:base + (p + 1) * LANES]
                if scale is not None:
                    blk = blk * scale
                return pltpu.roll(blk, HEAD_DIM, 1) if odd else blk

            cols = slice(h * LANES, (h + 1) * LANES)
            qa_ref[:, cols] = jnp.where(lane < HEAD_DIM, head(0, HEAD_DIM ** -0.5), aug_q[:, cols]).astype(BF16)
            ka_ref[:, cols] = jnp.where(lane < HEAD_DIM, head(ATT_WIDTH), aug_k[:, cols]).astype(BF16)
            va_ref[:, cols] = jnp.where(lane < HEAD_DIM, head(2 * ATT_WIDTH), one_ref[2:3, cols]).astype(BF16)

    wide = N_HEADS * LANES
    src = lax.broadcasted_iota(jnp.int32, (3 * LANES, wide), 0)
    col = lax.broadcasted_iota(jnp.int32, (3 * LANES, wide), 1)
    hd, term = src % LANES, src // LANES
    to_q = (col == hd * LANES + HEAD_DIM + term) & (hd < N_HEADS)
    to_k = (col == hd * LANES + HEAD_DIM + 3 + term) & (hd < N_HEADS)
    put = jnp.stack([to_q.astype(BF16), -to_k.astype(BF16)])
    off = lax.broadcasted_iota(jnp.int32, (8, wide), 1) % LANES - HEAD_DIM
    row = lax.broadcasted_iota(jnp.int32, (8, wide), 0)
    q_one = (off >= 3) & (off < 6)
    k_one = ((off >= 0) & (off < 3)) | ((off >= 6) & (off < 9))
    v_one = (off >= 0) & (off < 3)
    ones = jnp.where(row == 0, q_one, jnp.where(row == 1, k_one, (row == 2) & v_one)).astype(F32)
    return pl.pallas_call(
        body,
        out_shape=(jax.ShapeDtypeStruct((S, D_MODEL), BF16), jax.ShapeDtypeStruct((S, wide), BF16),
                   jax.ShapeDtypeStruct((S, wide), BF16), jax.ShapeDtypeStruct((S, wide), BF16),
                   jax.ShapeDtypeStruct((S, 2 * GMLP_WIDTH), F32), jax.ShapeDtypeStruct((S, LANES), F32)),
        grid=(S // tm,),
        in_specs=[pl.BlockSpec((tm, D_MODEL), lambda i: (i, 0)), pl.BlockSpec((1, D_MODEL), lambda i: (0, 0)),
                  pl.BlockSpec((IN_PAD, D_MODEL), lambda i: (0, 0)), pl.BlockSpec((1, LANES), lambda i: (0, 0)),
                  pl.BlockSpec((tm, tm), lambda i: (0, 0)), pl.BlockSpec((2, 3 * LANES, wide), lambda i: (0, 0, 0)),
                  pl.BlockSpec((8, wide), lambda i: (0, 0))],
        out_specs=(pl.BlockSpec((tm, D_MODEL), lambda i: (i, 0)), pl.BlockSpec((tm, wide), lambda i: (i, 0)),
                   pl.BlockSpec((tm, wide), lambda i: (i, 0)), pl.BlockSpec((tm, wide), lambda i: (i, 0)),
                   pl.BlockSpec((tm, 2 * GMLP_WIDTH), lambda i: (i, 0)), pl.BlockSpec((tm, LANES), lambda i: (i, 0))),
        scratch_shapes=[pltpu.VMEM((8, LANES), F32)],
        name="inproj_fwd", compiler_params=_cp(("arbitrary",), VMEM_LIMIT),
    )(x, g_mix, w_pad, bf_pad, tri, put, ones)


def _group_ones():
    r = lax.broadcasted_iota(jnp.int32, (GMLP_WIDTH, GMLP_WIDTH), 0) // GROUP_DIM
    c = lax.broadcasted_iota(jnp.int32, (GMLP_WIDTH, GMLP_WIDTH), 1) // GROUP_DIM
    return (r == c).astype(BF16)


def _gmlp_mixed(vn_bf, w_ref, bias, n_chunks):
    lane = lax.broadcasted_iota(jnp.int32, (CHUNK, LANES), 1)
    row = lax.broadcasted_iota(jnp.int32, (CHUNK, CHUNK), 0)
    col = lax.broadcasted_iota(jnp.int32, (CHUNK, CHUNK), 1)
    ws = [jnp.where(row >= col, w_ref[g], 0.0).astype(BF16) for g in range(N_GROUPS)]
    rows = []
    for ci in range(n_chunks):
        cols = []
        for pp in range(N_GROUPS // 2):
            v = vn_bf[ci * CHUNK:(ci + 1) * CHUNK, pp * LANES:(pp + 1) * LANES]
            v_lo = jnp.where(lane < GROUP_DIM, v, jnp.zeros_like(v))
            v_hi = jnp.where(lane >= GROUP_DIM, v, jnp.zeros_like(v))
            m = (jnp.dot(ws[2 * pp], v_lo, preferred_element_type=F32)
                 + jnp.dot(ws[2 * pp + 1], v_hi, preferred_element_type=F32))
            cols.append(m + bias[:, pp * LANES:(pp + 1) * LANES])
        rows.append(jnp.concatenate(cols, axis=1))
    return jnp.concatenate(rows, axis=0)


def _gmlp_fwd(ug, gain, w_s, bias_full):
    S = ug.shape[0]
    tm = _pick(S, (512, 256, 128))
    ones = _group_ones()

    def body(ug_ref, gain_ref, w_ref, bias_ref, ones_ref, sg_ref):
        u = _gelu(ug_ref[:, :GMLP_WIDTH])
        vr = _gelu(ug_ref[:, GMLP_WIDTH:])
        ms = _dot3(vr * vr, ones_ref[...]) * (1.0 / GROUP_DIM)
        vn = ((vr * lax.rsqrt(ms + EPS)) * gain_ref[...]).astype(BF16)
        mixed = _gmlp_mixed(vn, w_ref, bias_ref[...], tm // CHUNK)
        sg_ref[...] = (u * mixed).astype(BF16)

    return pl.pallas_call(
        body, out_shape=jax.ShapeDtypeStruct((S, GMLP_WIDTH), BF16), grid=(S // tm,),
        in_specs=[pl.BlockSpec((tm, 2 * GMLP_WIDTH), lambda i: (i, 0)), pl.BlockSpec((1, GMLP_WIDTH), lambda i: (0, 0)),
                  pl.BlockSpec((N_GROUPS, CHUNK, CHUNK), lambda i: (0, 0, 0)),
                  pl.BlockSpec((CHUNK, GMLP_WIDTH), lambda i: (0, 0)),
                  pl.BlockSpec((GMLP_WIDTH, GMLP_WIDTH), lambda i: (0, 0))],
        out_specs=pl.BlockSpec((tm, GMLP_WIDTH), lambda i: (i, 0)),
        name="gmlp_fwd", compiler_params=_cp(("parallel",), VMEM_LIMIT),
    )(ug, gain, w_s, bias_full, ones)


_NT = (((1,), (1,)), ((), ()))
_TN = (((0,), (0,)), ((), ()))


def _attn_fwd(qa, ka, va):
    S = qa.shape[0]
    tq = _pick(S, (ATT_TQ, 256))
    tk = min(ATT_TK, tq)
    nq = S // tq
    per_q = tq // tk

    def body(q_ref, k_ref, v_ref, o_ref, lse_ref, ob_ref):
        qi = pl.program_id(1)
        lane = lax.broadcasted_iota(jnp.int32, (tq, LANES), 1)
        sub = tk
        rid = lax.broadcasted_iota(jnp.int32, (sub, sub), 0)
        cid = lax.broadcasted_iota(jnp.int32, (sub, sub), 1)
        qs = [q_ref[:, :LANES], q_ref[:, LANES:]]

        def update(q, ks, k_len, h, m, acc, masked):
            cols = slice(h * LANES, (h + 1) * LANES)
            s = lax.dot_general(q, k_ref[pl.ds(ks, k_len), cols], _NT, preferred_element_type=F32)
            if masked:
                s = jnp.where(rid >= cid, s, -jnp.inf)
            m_new = jnp.maximum(m, jnp.max(s, axis=-1, keepdims=True))
            p = jnp.exp(s - m_new).astype(BF16)
            acc = jnp.exp(m - m_new) * acc + jnp.dot(p, v_ref[pl.ds(ks, k_len), cols], preferred_element_type=F32)
            return m_new, acc

        def step(kb, carry):
            ks = pl.multiple_of(kb * tk, tk)
            return tuple(update(qs[h], ks, tk, h, *carry[h], False) for h in range(2))

        one = (jnp.full((tq, 1), -jnp.inf, F32), jnp.zeros((tq, LANES), F32))
        carry = lax.fori_loop(0, qi * per_q, step, (one, one))
        outs, lses = [], []
        for h in range(2):
            ms, accs = [], []
            for r in range(tq // sub):
                rows = slice(r * sub, (r + 1) * sub)
                m, acc = carry[h][0][rows], carry[h][1][rows]
                for c in range(r + 1):
                    ks = pl.multiple_of(qi * tq + c * sub, sub)
                    m, acc = update(qs[h][rows], ks, sub, h, m, acc, c == r)
                ms.append(m)
                accs.append(acc)
            m, acc = jnp.concatenate(ms, axis=0), jnp.concatenate(accs, axis=0)
            l = acc[:, HEAD_DIM:HEAD_DIM + 1]
            outs.append(acc / l)
            lses.append(m + jnp.log(l))
        o = jnp.where(lane < HEAD_DIM, outs[0], pltpu.roll(outs[1], HEAD_DIM, 1))
        o_ref[...] = o
        ob_ref[...] = o.astype(BF16)
        lse_ref[...] = jnp.where(lane < HEAD_DIM, lses[0], lses[1])

    return pl.pallas_call(
        body,
        out_shape=(jax.ShapeDtypeStruct((S, ATT_WIDTH), F32), jax.ShapeDtypeStruct((S, ATT_WIDTH), F32),
                   jax.ShapeDtypeStruct((S, ATT_WIDTH), BF16)),
        grid=(N_PAIRS, nq),
        in_specs=[pl.BlockSpec((tq, 2 * LANES), lambda p, i: (i, p)),
                  pl.BlockSpec((S, 2 * LANES), lambda p, i: (0, p)),
                  pl.BlockSpec((S, 2 * LANES), lambda p, i: (0, p))],
        out_specs=(pl.BlockSpec((tq, LANES), lambda p, i: (i, p)), pl.BlockSpec((tq, LANES), lambda p, i: (i, p)),
                   pl.BlockSpec((tq, LANES), lambda p, i: (i, p))),
        name="attn_fwd", compiler_params=_cp(("parallel", "parallel"), VMEM_LIMIT),
    )(qa, ka, va)


def _shift_rows(x, prev, n):
    rid = lax.broadcasted_iota(jnp.int32, x.shape, 0)
    y = pltpu.roll(x, n, 0)
    if n == 1:
        return jnp.where(rid == 0, prev[7:8, :], y)
    return jnp.where(rid == 0, prev[6:7, :], jnp.where(rid == 1, prev[7:8, :], y))


def _shift_rows_up(x, nxt, n):
    rows = x.shape[0]
    rid = lax.broadcasted_iota(jnp.int32, x.shape, 0)
    y = pltpu.roll(x, rows - n, 0)
    if n == 1:
        return jnp.where(rid == rows - 1, nxt[0:1, :], y)
    return jnp.where(rid == rows - 2, nxt[0:1, :], jnp.where(rid == rows - 1, nxt[1:2, :], y))


def _conv3(cur, prev, w, b):
    return (w[0:1, :] * _shift_rows(cur, prev, 2) + w[1:2, :] * _shift_rows(cur, prev, 1)
            + w[2:3, :] * cur + b)


def _ffn_up_conv(hn, w_up_bf, cw, cb):
    S = hn.shape[0]
    F = D_FF
    tm = _pick(S, (FFN_TM, 256))
    tn = _pick(F, (FFN_TN, 256, 128))
    nj = F // tn

    def body(hn_ref, wa_ref, wg_ref, cw_ref, cb_ref, hu_ref, hc_ref, act_ref, tail_ref):
        i = pl.program_id(1)

        @pl.when(i == 0)
        def _():
            tail_ref[...] = jnp.zeros_like(tail_ref)

        hn_v = hn_ref[...]
        halves = []
        for h, w_ref in enumerate((wa_ref, wg_ref)):
            hu = lax.dot_general(hn_v, w_ref[...], _NT, preferred_element_type=F32)
            hu_ref[h] = hu.astype(BF16)
            hc = _conv3(hu, tail_ref[h], cw_ref[h], cb_ref[h])
            hc_ref[h] = hc
            halves.append(hc)
            tail_ref[h] = hu[tm - 8:, :]
        a, g = halves
        act_ref[...] = (g * _sigmoid(g) * a).astype(BF16)

    both = pl.BlockSpec((2, tm, tn), lambda j, i: (0, i, j))
    return pl.pallas_call(
        body, out_shape=(jax.ShapeDtypeStruct((2, S, F), BF16), jax.ShapeDtypeStruct((2, S, F), F32),
                         jax.ShapeDtypeStruct((S, F), BF16)),
        grid=(nj, S // tm),
        in_specs=[pl.BlockSpec((tm, D_MODEL), lambda j, i: (i, 0)),
                  pl.BlockSpec((tn, D_MODEL), lambda j, i: (j, 0)),
                  pl.BlockSpec((tn, D_MODEL), lambda j, i: (nj + j, 0)),
                  pl.BlockSpec((2, 8, tn), lambda j, i: (0, 0, j)),
                  pl.BlockSpec((2, 1, tn), lambda j, i: (0, 0, j))],
        out_specs=(both, both, pl.BlockSpec((tm, tn), lambda j, i: (i, j))),
        scratch_shapes=[pltpu.VMEM((2, 8, tn), F32)],
        name="ffn_up_conv", compiler_params=_cp(("parallel", "arbitrary"), VMEM_LIMIT),
    )(hn, w_up_bf, w_up_bf, cw, cb)


def _ffn_down_loss(act, w_down_bf, h1, g_final, target):
    S = h1.shape[0]
    tm = _pick(S, (512, 256))

    def body(a_ref, w_ref, h1_ref, g_ref, t_ref, loss_ref, dh_ref, dhb_ref, dg_ref):
        i = pl.program_id(0)

        @pl.when(i == 0)
        def _():
            loss_ref[...] = jnp.zeros_like(loss_ref)
            dg_ref[...] = jnp.zeros_like(dg_ref)

        hf = h1_ref[...] + jnp.dot(a_ref[...], w_ref[...], preferred_element_type=F32)
        g = g_ref[...]
        r = lax.rsqrt(jnp.mean(hf * hf, axis=-1, keepdims=True) + EPS)
        hhat = hf * r
        err = hhat * g - t_ref[...]
        loss_ref[...] += 0.5 * jnp.sum(jnp.mean(err * err, axis=-1, keepdims=True))
        dy = err * (1.0 / D_MODEL)
        dg_ref[0:1, :] += jnp.sum(dy * hhat, axis=0, keepdims=True)
        dhat = dy * g
        dh = r * (dhat - hhat * jnp.mean(dhat * hhat, axis=-1, keepdims=True))
        dh_ref[...] = dh
        dhb_ref[...] = dh.astype(BF16)

    row = pl.BlockSpec((tm, D_MODEL), lambda i: (i, 0))
    return pl.pallas_call(
        body,
        out_shape=(jax.ShapeDtypeStruct((8, LANES), F32), jax.ShapeDtypeStruct((S, D_MODEL), F32),
                   jax.ShapeDtypeStruct((S, D_MODEL), BF16), jax.ShapeDtypeStruct((8, D_MODEL), F32)),
        grid=(S // tm,),
        in_specs=[pl.BlockSpec((tm, D_FF), lambda i: (i, 0)), pl.BlockSpec((D_FF, D_MODEL), lambda i: (0, 0)), row,
                  pl.BlockSpec((1, D_MODEL), lambda i: (0, 0)), row],
        out_specs=(pl.BlockSpec((8, LANES), lambda i: (0, 0)), row, row, pl.BlockSpec((8, D_MODEL), lambda i: (0, 0))),
        name="ffn_down_loss", compiler_params=_cp(("arbitrary",), VMEM_LIMIT),
    )(act, w_down_bf, h1, g_final, target)


def _ffn_up_dx_rms(dhu, w_up_bf, h1, g_ffn, dh2):
    _, S, F = dhu.shape
    tm = _pick(S, (512, 256))

    def body(a_ref, b_ref, h_ref, g_ref, r_ref, dh_ref, dhb_ref, dg_ref):
        i = pl.program_id(0)

        @pl.when(i == 0)
        def _():
            dg_ref[...] = jnp.zeros_like(dg_ref)

        dyv = (jnp.dot(a_ref[0], b_ref[:F, :], preferred_element_type=F32)
               + jnp.dot(a_ref[1], b_ref[F:, :], preferred_element_type=F32))
        hf = h_ref[...]
        r = lax.rsqrt(jnp.mean(hf * hf, axis=-1, keepdims=True) + EPS)
        hhat = hf * r
        dg_ref[0:1, :] += jnp.sum(dyv * hhat, axis=0, keepdims=True)
        dhat = dyv * g_ref[...]
        dh = r_ref[...] + r * (dhat - hhat * jnp.mean(dhat * hhat, axis=-1, keepdims=True))
        dh_ref[...] = dh
        dhb_ref[...] = dh.astype(BF16)

    row = pl.BlockSpec((tm, D_MODEL), lambda i: (i, 0))
    return pl.pallas_call(
        body,
        out_shape=(jax.ShapeDtypeStruct((S, D_MODEL), F32), jax.ShapeDtypeStruct((S, D_MODEL), BF16),
                   jax.ShapeDtypeStruct((8, D_MODEL), F32)),
        grid=(S // tm,),
        in_specs=[pl.BlockSpec((2, tm, F), lambda i: (0, i, 0)), pl.BlockSpec((2 * F, D_MODEL), lambda i: (0, 0)),
                  row, pl.BlockSpec((1, D_MODEL), lambda i: (0, 0)), row],
        out_specs=(row, row, pl.BlockSpec((8, D_MODEL), lambda i: (0, 0))),
        name="ffn_up_dx_rms", compiler_params=_cp(("arbitrary",), VMEM_LIMIT),
    )(dhu, w_up_bf, h1, g_ffn, dh2)


def _conv_gate_bwd(hc, hu, dact, cw):
    _, S, F = hu.shape
    tm = _pick(S, (CONV_TM, 128))
    tn = _pick(F, (CONV_TN, 256, 128))
    r8 = tm // 8
    n_i = S // tm
    last8 = S // 8 - 1

    def body(hc_ref, hcn_ref, hu_ref, da_ref, dan_ref, w_ref, dhu_ref, dcw_ref):
        i = pl.program_id(1)

        @pl.when(i == 0)
        def _():
            dcw_ref[...] = jnp.zeros_like(dcw_ref)

        rid8 = lax.broadcasted_iota(jnp.int32, (8, tn), 0)

        def gate_grads(a, g, d):
            sg = _sigmoid(g)
            return d * (g * sg), d * a * (sg * (1.0 + g * (1.0 - sg)))

        dhc = gate_grads(hc_ref[0], hc_ref[1], da_ref[...])
        dhc_n = gate_grads(hcn_ref[0], hcn_ref[1], dan_ref[...])
        for h in range(2):
            w = w_ref[h]
            d = dhc[h]
            dn = jnp.where(i < n_i - 1, dhc_n[h], 0.0)
            u1 = _shift_rows_up(d, dn, 1)
            u2 = _shift_rows_up(d, dn, 2)
            dhu_ref[h] = (w[2:3, :] * d + w[1:2, :] * u1 + w[0:1, :] * u2).astype(BF16)
            x = hu_ref[h].astype(F32)
            t0, t1, t2, t3 = [jnp.sum(t, axis=0, keepdims=True) for t in (u2 * x, u1 * x, d * x, d)]
            dcw_ref[h] += jnp.where(rid8 == 0, t0, jnp.where(rid8 == 1, t1, jnp.where(rid8 == 2, t2, jnp.where(rid8 == 3, t3, 0.0))))

    cur = pl.BlockSpec((2, tm, tn), lambda j, i: (0, i, j))
    return pl.pallas_call(
        body,
        out_shape=(jax.ShapeDtypeStruct((2, S, F), BF16), jax.ShapeDtypeStruct((2, 8, F), F32)),
        grid=(F // tn, n_i),
        in_specs=[cur, pl.BlockSpec((2, 8, tn), lambda j, i: (0, jnp.minimum((i + 1) * r8, last8), j)), cur,
                  pl.BlockSpec((tm, tn), lambda j, i: (i, j)),
                  pl.BlockSpec((8, tn), lambda j, i: (jnp.minimum((i + 1) * r8, last8), j)),
                  pl.BlockSpec((2, 8, tn), lambda j, i: (0, 0, j))],
        out_specs=(cur, pl.BlockSpec((2, 8, tn), lambda j, i: (0, 0, j))),
        name="conv_gate_bwd", compiler_params=_cp(("parallel", "arbitrary"), VMEM_LIMIT),
    )(hc, hc, hu, dact, dact, cw)


def _out_proj_dx_prep(dh1_bf, w_out_bf, att, lse, qa):
    S = att.shape[0]
    tm = _pick(S, (256,))

    def body(dh_ref, w_ref, o_ref, lse_ref, q_ref, dsg_ref, qb_ref, doa_ref):
        lane = lax.broadcasted_iota(jnp.int32, (tm, LANES), 1)
        dh = dh_ref[...]
        dsg_ref[...] = lax.dot_general(dh, w_ref[ATT_WIDTH:, :], _NT, preferred_element_type=F32)
        datt = lax.dot_general(dh, w_ref[:ATT_WIDTH, :], _NT, preferred_element_type=F32)
        for p in range(N_PAIRS):
            pc = slice(p * LANES, (p + 1) * LANES)
            do = datt[:, pc]
            prod = o_ref[:, pc] * do
            for hh in range(2):
                sel = (lane >= HEAD_DIM) if hh else (lane < HEAD_DIM)
                delta = jnp.sum(jnp.where(sel, prod, 0.0), axis=-1, keepdims=True)
                dod = pltpu.roll(do, HEAD_DIM, 1) if hh else do
                cols = slice((2 * p + hh) * LANES, (2 * p + hh + 1) * LANES)
                doa_ref[:, cols] = jnp.where(lane < HEAD_DIM, dod, _aug(lane, _split3f(-delta))).astype(BF16)
                lcol = p * LANES + hh * HEAD_DIM
                l3 = _split3f(-lse_ref[:, lcol:lcol + 1])
                augl = jnp.where(lane == HEAD_DIM + 6, l3[0], jnp.where(lane == HEAD_DIM + 7, l3[1], l3[2])).astype(BF16)
                qb_ref[:, cols] = jnp.where((lane >= HEAD_DIM + 6) & (lane < HEAD_DIM + 9), augl, q_ref[:, cols])

    half = pl.BlockSpec((tm, ATT_WIDTH), lambda i: (i, 0))
    wide = pl.BlockSpec((tm, N_HEADS * LANES), lambda i: (i, 0))
    return pl.pallas_call(
        body,
        out_shape=(jax.ShapeDtypeStruct((S, GMLP_WIDTH), F32), jax.ShapeDtypeStruct(qa.shape, BF16),
                   jax.ShapeDtypeStruct(qa.shape, BF16)),
        grid=(S // tm,),
        in_specs=[pl.BlockSpec((tm, D_MODEL), lambda i: (i, 0)), pl.BlockSpec((D_MODEL, D_MODEL), lambda i: (0, 0)),
                  half, half, wide],
        out_specs=(half, wide, wide),
        name="out_proj_dx_prep", compiler_params=_cp(("parallel",), VMEM_LIMIT),
    )(dh1_bf, w_out_bf, att, lse, qa)


def _attn_bwd(qb, ka, va, doa):
    S = qb.shape[0]
    tk = _pick(S, (512, 256))
    tq = tk
    nq = S // tq

    def pair(a, scale=None):
        lane = lax.broadcasted_iota(jnp.int32, (a.shape[0], LANES), 1)
        out = jnp.where(lane < HEAD_DIM, a[:, :LANES], pltpu.roll(a[:, LANES:], HEAD_DIM, 1))
        return out if scale is None else out * scale

    def head_lanes(a, col, sign, first):
        lane = lax.broadcasted_iota(jnp.int32, (a.shape[0], LANES), 1)
        return jnp.where(lane == first, sign * a[:, col:col + 1],
                         jnp.where(lane == first + 1, sign * a[:, LANES + col:LANES + col + 1], 0.0))

    def body(q_ref, do_ref, k_ref, v_ref, dqc_ref, dkc_ref, dvc_ref, dcq_ref, dck_ref, dq_ref, dka_ref, dva_ref):
        kb = pl.program_id(1)

        @pl.when(kb == 0)
        def _():
            dq_ref[...] = jnp.zeros_like(dq_ref)

        dka_ref[...] = jnp.zeros_like(dka_ref)
        dva_ref[...] = jnp.zeros_like(dva_ref)
        rid = lax.broadcasted_iota(jnp.int32, (tk, tq), 0)
        cid = lax.broadcasted_iota(jnp.int32, (tk, tq), 1)

        def sub_tile(qs, q_len, k_off, k_len, masked):
            keys = slice(k_off, k_off + k_len)
            for h in range(2):
                cols = slice(h * LANES, (h + 1) * LANES)
                qblk = q_ref[pl.ds(qs, q_len), cols]
                doblk = do_ref[pl.ds(qs, q_len), cols]
                kh = k_ref[keys, cols]
                p = jnp.exp(lax.dot_general(kh, qblk, _NT, preferred_element_type=F32))
                if masked:
                    p = jnp.where(cid >= rid, p, 0.0)
                ds = (p * lax.dot_general(v_ref[keys, cols], doblk, _NT, preferred_element_type=F32)).astype(BF16)
                dva_ref[keys, cols] += jnp.dot(p.astype(BF16), doblk, preferred_element_type=F32)
                dka_ref[keys, cols] += jnp.dot(ds, qblk, preferred_element_type=F32)
                dq_ref[pl.ds(qs, q_len), cols] += lax.dot_general(ds, kh, _TN, preferred_element_type=F32)

        sub_tile(pl.multiple_of(kb * tq, tq), tq, 0, tk, True)

        def step(qi, carry):
            sub_tile(pl.multiple_of(qi * tq, tq), tq, 0, tk, False)
            return carry

        lax.fori_loop(kb + 1, nq, step, 0)
        dka = dka_ref[...]
        dkc_ref[...] = pair(dka).astype(BF16)
        dvc_ref[...] = pair(dva_ref[...]).astype(BF16)
        first = 2 * pl.program_id(0)
        dck_ref[...] = head_lanes(dka, HEAD_DIM + 3, -1.0, first)

        @pl.when(kb == nq - 1)
        def _():
            dqa = dq_ref[...]
            dqc_ref[...] = pair(dqa, HEAD_DIM ** -0.5).astype(BF16)
            dcq_ref[...] = head_lanes(dqa, HEAD_DIM, 1.0, first)

    wide = 2 * LANES
    half = jax.ShapeDtypeStruct((S, ATT_WIDTH), BF16)
    slabs = jax.ShapeDtypeStruct((N_PAIRS, S, LANES), F32)
    return pl.pallas_call(
        body,
        out_shape=(half, half, half, slabs, slabs),
        grid=(N_PAIRS, nq),
        in_specs=[pl.BlockSpec((S, wide), lambda p, j: (0, p)), pl.BlockSpec((S, wide), lambda p, j: (0, p)),
                  pl.BlockSpec((tk, wide), lambda p, j: (j, p)), pl.BlockSpec((tk, wide), lambda p, j: (j, p))],
        out_specs=(pl.BlockSpec((S, LANES), lambda p, j: (0, p)), pl.BlockSpec((tk, LANES), lambda p, j: (j, p)),
                   pl.BlockSpec((tk, LANES), lambda p, j: (j, p)), pl.BlockSpec((None, S, LANES), lambda p, j: (p, 0, 0)),
                   pl.BlockSpec((None, tk, LANES), lambda p, j: (p, j, 0))),
        scratch_shapes=[pltpu.VMEM((S, wide), F32), pltpu.VMEM((tk, wide), F32), pltpu.VMEM((tk, wide), F32)],
        name="attn_bwd", compiler_params=_cp(("parallel", "arbitrary"), VMEM_LIMIT),
    )(qb, doa, ka, va)


def _gmlp_bwd(ug, dsg, gain, w_s, wt_s, bias_full):
    S = ug.shape[0]
    tm = _pick(S, (512, 256, 128))
    n_chunks = tm // CHUNK
    n_i = S // tm
    ones = _group_ones()
    nt = (((1,), (1,)), ((), ()))

    def body(ug_ref, dsg_ref, gain_ref, w_ref, wt_ref, bias_ref, ones_ref, dug_ref, dw_ref, dgain_ref, dbias_ref,
             dbacc_ref):
        i = pl.program_id(0)

        @pl.when(i == 0)
        def _():
            dw_ref[...] = jnp.zeros_like(dw_ref)
            dgain_ref[...] = jnp.zeros_like(dgain_ref)
            dbacc_ref[...] = jnp.zeros_like(dbacc_ref)

        ones_m = ones_ref[...]
        pu = ug_ref[:, :GMLP_WIDTH]
        pg = ug_ref[:, GMLP_WIDTH:]
        u = _gelu(pu)
        vr = _gelu(pg)
        ms = _dot3(vr * vr, ones_m) * (1.0 / GROUP_DIM)
        rinv = lax.rsqrt(ms + EPS)
        vhat = vr * rinv
        gain_v = gain_ref[...]
        vn = (vhat * gain_v).astype(BF16)
        mixed = _gmlp_mixed(vn, w_ref, bias_ref[...], n_chunks)
        dsg_v = dsg_ref[...]
        du = dsg_v * mixed
        dmixed = dsg_v * u
        dm_bf = dmixed.astype(BF16)
        lane = lax.broadcasted_iota(jnp.int32, (CHUNK, LANES), 1)
        row = lax.broadcasted_iota(jnp.int32, (CHUNK, CHUNK), 0)
        col = lax.broadcasted_iota(jnp.int32, (CHUNK, CHUNK), 1)
        wts = [jnp.where(col >= row, wt_ref[g], 0.0).astype(BF16) for g in range(N_GROUPS)]
        dvn_rows = []
        dbsum = jnp.zeros((CHUNK, GMLP_WIDTH), F32)
        for ci in range(n_chunks):
            rs = slice(ci * CHUNK, (ci + 1) * CHUNK)
            dbsum = dbsum + dmixed[rs, :]
            cols = []
            for pp in range(N_GROUPS // 2):
                cs = slice(pp * LANES, (pp + 1) * LANES)
                dm = dm_bf[rs, cs]
                dm_lo = jnp.where(lane < GROUP_DIM, dm, jnp.zeros_like(dm))
                dm_hi = jnp.where(lane >= GROUP_DIM, dm, jnp.zeros_like(dm))
                vb = vn[rs, cs]
                dw_ref[2 * pp] += lax.dot_general(dm_lo, vb, nt, preferred_element_type=F32)
                dw_ref[2 * pp + 1] += lax.dot_general(dm_hi, vb, nt, preferred_element_type=F32)
                cols.append(jnp.dot(wts[2 * pp], dm_lo, preferred_element_type=F32)
                            + jnp.dot(wts[2 * pp + 1], dm_hi, preferred_element_type=F32))
            dvn_rows.append(jnp.concatenate(cols, axis=1))
        dvn = jnp.concatenate(dvn_rows, axis=0)
        dbacc_ref[...] += dbsum
        dgain_ref[0:1, :] += jnp.sum(dvn * vhat, axis=0, keepdims=True)
        dvhat = dvn * gain_v
        gm = _dot3(dvhat * vhat, ones_m) * (1.0 / GROUP_DIM)
        dvr = rinv * (dvhat - vhat * gm)
        dug_ref[:, :GMLP_WIDTH] = (du * _gelu_grad(pu)).astype(BF16)
        dug_ref[:, GMLP_WIDTH:] = (dvr * _gelu_grad(pg)).astype(BF16)

        @pl.when(i == n_i - 1)
        def _():
            for g in range(N_GROUPS):
                dw_ref[g] = jnp.where(row >= col, dw_ref[g], 0.0)
            dbias_ref[...] = _dot3(dbacc_ref[...], ones_m)

    return pl.pallas_call(
        body,
        out_shape=(jax.ShapeDtypeStruct((S, 2 * GMLP_WIDTH), BF16), jax.ShapeDtypeStruct((N_GROUPS, CHUNK, CHUNK), F32),
                   jax.ShapeDtypeStruct((8, GMLP_WIDTH), F32), jax.ShapeDtypeStruct((CHUNK, GMLP_WIDTH), F32)),
        grid=(n_i,),
        in_specs=[pl.BlockSpec((tm, 2 * GMLP_WIDTH), lambda i: (i, 0)), pl.BlockSpec((tm, GMLP_WIDTH), lambda i: (i, 0)),
                  pl.BlockSpec((1, GMLP_WIDTH), lambda i: (0, 0)),
                  pl.BlockSpec((N_GROUPS, CHUNK, CHUNK), lambda i: (0, 0, 0)),
                  pl.BlockSpec((N_GROUPS, CHUNK, CHUNK), lambda i: (0, 0, 0)),
                  pl.BlockSpec((CHUNK, GMLP_WIDTH), lambda i: (0, 0)),
                  pl.BlockSpec((GMLP_WIDTH, GMLP_WIDTH), lambda i: (0, 0))],
        out_specs=(pl.BlockSpec((tm, 2 * GMLP_WIDTH), lambda i: (i, 0)),
                   pl.BlockSpec((N_GROUPS, CHUNK, CHUNK), lambda i: (0, 0, 0)),
                   pl.BlockSpec((8, GMLP_WIDTH), lambda i: (0, 0)),
                   pl.BlockSpec((CHUNK, GMLP_WIDTH), lambda i: (0, 0))),
        scratch_shapes=[pltpu.VMEM((CHUNK, GMLP_WIDTH), F32)],
        name="gmlp_bwd", compiler_params=_cp(("arbitrary",), VMEM_LIMIT),
    )(ug, dsg, gain, w_s, wt_s, bias_full, ones)


def _gate_bwd(dcq, dck, zf):
    S = zf.shape[0]
    tm = _pick(S, (256,))
    n_i = S // tm
    triu = (lax.broadcasted_iota(jnp.int32, (tm, tm), 0) <= lax.broadcasted_iota(jnp.int32, (tm, tm), 1)).astype(BF16)

    def body(dcq_ref, dck_ref, zf_ref, tri_ref, dzf_ref, dbf_ref, carry_ref):
        i = pl.program_id(0)

        @pl.when(i == 0)
        def _():
            carry_ref[...] = jnp.zeros_like(carry_ref)
            dbf_ref[...] = jnp.zeros_like(dbf_ref)

        lane = lax.broadcasted_iota(jnp.int32, (tm, LANES), 1)
        dc = dcq_ref[0] + dck_ref[0]
        for p in range(1, N_PAIRS):
            dc = dc + (dcq_ref[p] + dck_ref[p])
        dlf = _dot3l(tri_ref[...], dc) + carry_ref[0:1, :]
        carry_ref[0:1, :] = dlf[0:1, :]
        dz = jnp.where(lane < N_HEADS, dlf * _sigmoid(-zf_ref[...]), 0.0)
        dzf_ref[...] = dz.astype(BF16)
        dbf_ref[0:1, :] += jnp.sum(dz, axis=0, keepdims=True)

    return pl.pallas_call(
        body,
        out_shape=(jax.ShapeDtypeStruct((S, LANES), BF16), jax.ShapeDtypeStruct((8, LANES), F32)),
        grid=(n_i,),
        in_specs=[pl.BlockSpec((N_PAIRS, tm, LANES), lambda i: (0, n_i - 1 - i, 0)),
                  pl.BlockSpec((N_PAIRS, tm, LANES), lambda i: (0, n_i - 1 - i, 0)),
                  pl.BlockSpec((tm, LANES), lambda i: (n_i - 1 - i, 0)),
                  pl.BlockSpec((tm, tm), lambda i: (0, 0))],
        out_specs=(pl.BlockSpec((tm, LANES), lambda i: (n_i - 1 - i, 0)), pl.BlockSpec((8, LANES), lambda i: (0, 0))),
        scratch_shapes=[pltpu.VMEM((8, LANES), F32)],
        name="gate_bwd", compiler_params=_cp(("arbitrary",), VMEM_LIMIT),
    )(dcq, dck, zf, triu)


def _out_proj_fwd(att_bf, sg, w_out_bf, x, g_ffn):
    S = x.shape[0]
    tm = _pick(S, (512, 256))

    def body(a_ref, s_ref, w_ref, x_ref, g_ref, h_ref, hn_ref):
        h = (x_ref[...] + jnp.dot(a_ref[...], w_ref[:ATT_WIDTH, :], preferred_element_type=F32)
             + jnp.dot(s_ref[...], w_ref[ATT_WIDTH:, :], preferred_element_type=F32))
        h_ref[...] = h
        r = lax.rsqrt(jnp.mean(h * h, axis=-1, keepdims=True) + EPS)
        hn_ref[...] = ((h * r) * g_ref[...]).astype(BF16)

    row = pl.BlockSpec((tm, D_MODEL), lambda i: (i, 0))
    half = pl.BlockSpec((tm, ATT_WIDTH), lambda i: (i, 0))
    return pl.pallas_call(
        body, out_shape=(jax.ShapeDtypeStruct((S, D_MODEL), F32), jax.ShapeDtypeStruct((S, D_MODEL), BF16)),
        grid=(S // tm,),
        in_specs=[half, half, pl.BlockSpec((D_MODEL, D_MODEL), lambda i: (0, 0)), row,
                  pl.BlockSpec((1, D_MODEL), lambda i: (0, 0))],
        out_specs=(row, row), name="out_proj", compiler_params=_cp(("parallel",), VMEM_LIMIT),
    )(att_bf, sg, w_out_bf, x, g_ffn)


def _out_proj_dw(att_bf, sg, dh1_bf):
    S = att_bf.shape[0]
    tk = _pick(S, (1024, 512))

    def body(a_ref, s_ref, d_ref, o_ref):
        k = pl.program_id(0)

        @pl.when(k == 0)
        def _():
            o_ref[...] = jnp.zeros_like(o_ref)

        d = d_ref[...]
        o_ref[:ATT_WIDTH, :] += lax.dot_general(a_ref[...], d, _TN, preferred_element_type=F32)
        o_ref[ATT_WIDTH:, :] += lax.dot_general(s_ref[...], d, _TN, preferred_element_type=F32)

    half = pl.BlockSpec((tk, ATT_WIDTH), lambda k: (k, 0))
    return pl.pallas_call(
        body, out_shape=jax.ShapeDtypeStruct((D_MODEL, D_MODEL), F32), grid=(S // tk,),
        in_specs=[half, half, pl.BlockSpec((tk, D_MODEL), lambda k: (k, 0))],
        out_specs=pl.BlockSpec((D_MODEL, D_MODEL), lambda k: (0, 0)),
        name="out_proj_dw", compiler_params=_cp(("arbitrary",), VMEM_LIMIT),
    )(att_bf, sg, dh1_bf)


_IN_PIECES = ((0, ATT_WIDTH), (ATT_WIDTH, ATT_WIDTH), (2 * ATT_WIDTH, ATT_WIDTH), (QKV, 2 * GMLP_WIDTH), (UG_END, LANES))


def _inproj_bwd_dx(pieces, w_pad, x, g_mix, dh1):
    S = x.shape[0]
    tm = _pick(S, (512, 256))

    def body(*refs):
        p_refs, (w_ref, x_ref, g_ref, r_ref, dx_ref, dg_ref) = refs[:5], refs[5:]
        i = pl.program_id(0)

        @pl.when(i == 0)
        def _():
            dg_ref[...] = jnp.zeros_like(dg_ref)

        dxn = None
        for p_ref, (c0, width) in zip(p_refs, _IN_PIECES):
            part = jnp.dot(p_ref[...], w_ref[c0:c0 + width, :], preferred_element_type=F32)
            dxn = part if dxn is None else dxn + part
        xf = x_ref[...]
        r = lax.rsqrt(jnp.mean(xf * xf, axis=-1, keepdims=True) + EPS)
        xhat = xf * r
        dg_ref[0:1, :] += jnp.sum(dxn * xhat, axis=0, keepdims=True)
        dhat = dxn * g_ref[...]
        dx_ref[...] = r_ref[...] + r * (dhat - xhat * jnp.mean(dhat * xhat, axis=-1, keepdims=True))

    row = pl.BlockSpec((tm, D_MODEL), lambda i: (i, 0))
    return pl.pallas_call(
        body, out_shape=(jax.ShapeDtypeStruct((S, D_MODEL), F32), jax.ShapeDtypeStruct((8, D_MODEL), F32)),
        grid=(S // tm,),
        in_specs=[pl.BlockSpec((tm, width), lambda i: (i, 0)) for _, width in _IN_PIECES]
        + [pl.BlockSpec((IN_PAD, D_MODEL), lambda i: (0, 0)), row, pl.BlockSpec((1, D_MODEL), lambda i: (0, 0)), row],
        out_specs=(row, pl.BlockSpec((8, D_MODEL), lambda i: (0, 0))),
        name="in_proj_dx", compiler_params=_cp(("arbitrary",), VMEM_LIMIT),
    )(*pieces, w_pad, x, g_mix, dh1)


def _inproj_bwd_dw(xn, pieces):
    S = xn.shape[0]
    tk = _pick(S, (1024, 512))

    def body(*refs):
        x_ref, p_refs, o_ref = refs[0], refs[1:6], refs[6]
        k = pl.program_id(0)

        @pl.when(k == 0)
        def _():
            o_ref[...] = jnp.zeros_like(o_ref)

        xb = x_ref[...]
        for p_ref, (c0, width) in zip(p_refs, _IN_PIECES):
            o_ref[:, c0:c0 + width] += lax.dot_general(xb, p_ref[...], _TN, preferred_element_type=F32)

    return pl.pallas_call(
        body, out_shape=jax.ShapeDtypeStruct((D_MODEL, IN_PAD), F32), grid=(S // tk,),
        in_specs=[pl.BlockSpec((tk, D_MODEL), lambda k: (k, 0))]
        + [pl.BlockSpec((tk, width), lambda k: (k, 0)) for _, width in _IN_PIECES],
        out_specs=pl.BlockSpec((D_MODEL, IN_PAD), lambda k: (0, 0)),
        name="in_proj_dw", compiler_params=_cp(("arbitrary",), VMEM_LIMIT),
    )(xn, *pieces)


def _adamw(w, m, v, parts, name):
    R, C = w.shape[-2:]
    tr = R
    for cand in (256, 128, 64, 32, 16, 8):
        if R % cand == 0 and R > cand:
            tr = cand
            break
    c1 = 1.0 / (1.0 - ADAM_B1 ** ADAM_STEP)
    c2 = 1.0 / (1.0 - ADAM_B2 ** ADAM_STEP)

    def body(w_ref, m_ref, v_ref, p_ref, g_ref, d_ref, nm_ref, nv_ref):
        g = p_ref[0].astype(F32)
        for j in range(1, N_DEV):
            g = g + p_ref[j].astype(F32)
        g_ref[...] = g
        nm = ADAM_B1 * m_ref[...] + (1.0 - ADAM_B1) * g
        nv = ADAM_B2 * v_ref[...] + (1.0 - ADAM_B2) * (g * g)
        nm_ref[...] = nm
        nv_ref[...] = nv
        d_ref[...] = -ADAM_LR * ((nm * c1) / (jnp.sqrt(nv * c2) + ADAM_EPS) + ADAM_WD * w_ref[...])

    if w.ndim == 3:
        spec = pl.BlockSpec((None, tr, C), lambda i: (0, i, 0))
    else:
        spec = pl.BlockSpec((tr, C), lambda i: (i, 0))
    shp = jax.ShapeDtypeStruct(w.shape, F32)
    return pl.pallas_call(
        body, out_shape=(shp, shp, shp, shp), grid=(R // tr,),
        in_specs=[spec, spec, spec, pl.BlockSpec((N_DEV, tr, C), lambda i: (0, i, 0))],
        out_specs=(spec, spec, spec, spec),
        name=name, compiler_params=_cp(("parallel",), VMEM_LIMIT),
    )(w, m, v, parts)


def _place():
    x, y, c = lax.axis_index("x"), lax.axis_index("y"), lax.axis_index("c")
    return x, y, c


def _all_gather(blocks, name):
    n = len(blocks)

    def body(*refs):
        ins, outs = refs[:n], refs[n:2 * n]
        send_sems, recv_sems, local_sems = refs[2 * n:]
        x, y, c = _place()
        me, sibling = (x, y, c), (x, y, 1 - c)
        chips = [(1 - x, y), (x, 1 - y), (1 - x, 1 - y)]
        sends = []
        for a in range(n):
            out = outs[a]

            def slot(px, py, pc, out=out):
                return out.at[4 * px + 2 * py + pc]

            def copy(k, block, to, src=None, a=a, slot=slot):
                return pltpu.make_async_remote_copy(
                    src_ref=slot(*block) if src is None else src, dst_ref=slot(*block),
                    send_sem=send_sems.at[a, k], recv_sem=recv_sems.at[a, k], device_id=to, device_id_type=MESH)

            mine = pltpu.make_async_copy(ins[a], slot(*me), local_sems.at[a])
            mine.start()
            first = [copy(0, me, sibling, src=ins[a])]
            first += [copy(1 + j, me, (*chip, c), src=ins[a]) for j, chip in enumerate(chips)]
            for cp in first:
                cp.start()
            sends.append((mine, first, copy))
        for a in range(n):
            mine, first, copy = sends[a]
            passed = [copy(4 + j, (*chip, c), sibling) for j, chip in enumerate(chips)]
            for j, chip in enumerate(chips):
                copy(1 + j, (*chip, c), me).wait_recv()
                passed[j].start()
            copy(0, sibling, me).wait_recv()
            for j, chip in enumerate(chips):
                copy(4 + j, (*chip, 1 - c), me).wait_recv()
            for cp in first + passed:
                cp.wait_send()
            mine.wait()

    any_spec = pl.BlockSpec(memory_space=pl.ANY)
    return pl.pallas_call(
        body, out_shape=tuple(jax.ShapeDtypeStruct((N_DEV,) + b.shape, b.dtype) for b in blocks),
        in_specs=[any_spec] * n, out_specs=tuple([any_spec] * n),
        scratch_shapes=[pltpu.SemaphoreType.DMA((n, 7)), pltpu.SemaphoreType.DMA((n, 7)), pltpu.SemaphoreType.DMA((n,))],
        name=name,
    )(*blocks)


_HBM = pl.BlockSpec(memory_space=pltpu.HBM)
_SEM = pl.BlockSpec(memory_space=pltpu.SEMAPHORE)
_EFFECT = pltpu.SideEffectType.DATAFLOW_SIDE_EFFECTING


def _peers(x, y, c):
    out = []
    for k in range(1, N_DEV):
        px, py, pc = x ^ ((k >> 2) & 1), y ^ ((k >> 1) & 1), c ^ (k & 1)
        out.append((k, (px, py, pc), 4 * px + 2 * py + pc))
    return out


def _xchg_copies(src_refs, land_refs, send_sems, recv_sems, scatter):
    x, y, c = _place()
    me = 4 * x + 2 * y + c
    copies = []
    for a, (src, land) in enumerate(zip(src_refs, land_refs)):
        for k, place, idx in _peers(x, y, c):
            j = a * (N_DEV - 1) + k - 1
            copies.append(pltpu.make_async_remote_copy(
                src_ref=src.at[idx] if scatter[a] else src, dst_ref=land.at[me],
                send_sem=send_sems[j], recv_sem=recv_sems[j], device_id=place, device_id_type=MESH))
    return copies


def _xchg_start(srcs, scatter, name):
    n = len(srcs)
    lands = [lax.empty((N_DEV,) + (s.shape[1:] if sc else s.shape), s.dtype) for s, sc in zip(srcs, scatter)]

    ns = n * (N_DEV - 1)

    def body(*refs):
        sems = refs[2 * n:2 * n + 2 * ns]
        for cp in _xchg_copies(refs[:n], refs[n:2 * n], sems[:ns], sems[ns:], scatter):
            cp.start()
        token = refs[-1]
        token[...] = jnp.zeros_like(token)

    both = list(srcs) + lands
    res = pl.pallas_call(
        body, name=name,
        out_shape=(*[pltpu.SemaphoreType.DMA(())] * (2 * ns),
                   *[pltpu.HBM(a.shape, a.dtype) for a in both], jax.ShapeDtypeStruct((8, LANES), F32)),
        in_specs=[_HBM] * (2 * n),
        out_specs=(*([_SEM] * (2 * ns)), *([_HBM] * (2 * n)), pl.BlockSpec(memory_space=pltpu.VMEM)),
        input_output_aliases={i: 2 * ns + i for i in range(2 * n)},
        compiler_params=pltpu.CompilerParams(has_side_effects=_EFFECT),
    )(*[pltpu.with_memory_space_constraint(a, pltpu.HBM) for a in both])
    return (tuple(res[:2 * ns]), tuple(res[2 * ns:2 * ns + 2 * n])), res[-1]


def _xchg_wait(handle, scatter, after, name):
    sems, thru = handle
    n = len(thru) // 2
    ns = len(sems) // 2

    def body(*refs):
        got = refs[2 * n:2 * n + 2 * ns]
        for cp in _xchg_copies(refs[:n], refs[n:2 * n], got[:ns], got[ns:], scatter):
            cp.wait_send()
            cp.wait_recv()

    outs = pl.pallas_call(
        body, name=name, out_shape=tuple(pltpu.HBM(a.shape, a.dtype) for a in thru),
        in_specs=[_HBM] * (2 * n) + [_SEM] * (2 * ns) + [pl.BlockSpec(memory_space=pl.ANY)],
        out_specs=tuple([_HBM] * (2 * n)), input_output_aliases={i: i for i in range(2 * n)},
        compiler_params=pltpu.CompilerParams(has_side_effects=_EFFECT),
    )(*thru, *sems, after)
    return outs[:n], outs[n:]


def _tie(a, token):
    return a if token is None else a + token[0, 0].astype(a.dtype)


def _rows128(a):
    flat = a.reshape(-1)
    rows = -(-flat.shape[0] // LANES)
    rows = -(-rows // 8) * 8
    return jnp.pad(flat, (0, rows * LANES - flat.shape[0])).reshape(rows, LANES)


def _local_step(x, target, norm_mix_g, w_in_t, b_forget, gmlp_norm_g, w_spatial, b_spatial, norm_ffn_g, conv_b,
                norm_final_g, rest_fn, send_fn, small_fn, token=None):
    f = D_FF
    g_mix = norm_mix_g.reshape(1, D_MODEL)
    w_pad = jnp.pad(w_in_t, ((0, IN_PAD - IN_COLS), (0, 0)))
    bf_pad = jnp.pad(b_forget.reshape(1, N_HEADS), ((0, 0), (0, LANES - N_HEADS)))
    xn, qa, ka, va, ug, zf = _inproj_fwd(x, _tie(g_mix, token), w_pad, bf_pad)
    bias_full = jnp.repeat(b_spatial.reshape(N_GROUPS, CHUNK).T, GROUP_DIM, axis=1)
    w_s = w_spatial.reshape(N_GROUPS, CHUNK, CHUNK)
    gain = gmlp_norm_g.reshape(1, GMLP_WIDTH)
    sg = _gmlp_fwd(ug, gain, w_s, bias_full)
    att, lse, att_bf = _attn_fwd(qa, ka, va)
    w_out_bf, w_up_bf, conv_w, w_down_bf = rest_fn(att_bf)
    g_ffn = norm_ffn_g.reshape(1, D_MODEL)
    h1, hn = _out_proj_fwd(att_bf, sg, w_out_bf, x, g_ffn)
    cw = jnp.pad(conv_w.reshape(3, 2, f).transpose(1, 0, 2), ((0, 0), (0, 5), (0, 0)))
    cb = conv_b.reshape(2, 1, f)
    hu, hc, act = _ffn_up_conv(hn, w_up_bf, cw, cb)
    loss_blk, dh2, dh2_bf, dg_final = _ffn_down_loss(act, w_down_bf, h1, norm_final_g.reshape(1, D_MODEL), target)
    dw_down = _mm(act, dh2_bf, mode="tn", out_dtype=F32, tm=1408, tn=1024, tk=2048, name="ffn_down_dw")
    dact = _mm(dh2_bf, w_down_bf, mode="nt", out_dtype=F32, tm=1024, tn=1408, tk=1024, outer="j", name="ffn_down_dx")
    dhu, dcw = _conv_gate_bwd(hc, hu, dact, _tie(cw, send_fn("w_down", dw_down)))
    dw_up = _mm(hn, dhu, mode="tn", out_dtype=F32, tm=1024, tn=1408, tk=2048, b_halves=True, outer="j", name="ffn_up_dw")
    dh1, dh1_bf, dg_ffn = _ffn_up_dx_rms(dhu, w_up_bf, h1, _tie(g_ffn, send_fn("w_up", dw_up)), dh2)
    dsg, qb, doa = _out_proj_dx_prep(dh1_bf, w_out_bf, att, lse, qa)
    dw_out = _out_proj_dw(att_bf, sg, dh1_bf)
    dq, dk, dv, dcq, dck = _attn_bwd(qb, ka, va, doa)
    wt_s = w_s.transpose(0, 2, 1)
    dug, dw_s, dgain, dbias = _gmlp_bwd(ug, dsg, _tie(gain, send_fn("w_out", dw_out)), w_s, wt_s, bias_full)
    dzf, dbf = _gate_bwd(dcq, dck, zf)
    grad_x, dg_mix = _inproj_bwd_dx((dq, dk, dv, dug, dzf), w_pad, x, g_mix, dh1)
    grads = dict(
        norm_mix_g=dg_mix[0:1, :],
        b_forget=dbf[0:1, :N_HEADS],
        gmlp_norm_g=dgain[0:1, :],
        w_spatial=dw_s,
        b_spatial=dbias[:, ::GROUP_DIM].T,
        norm_ffn_g=dg_ffn[0:1, :],
        conv_w=dcw[:, 0:3, :].transpose(1, 0, 2).reshape(3, 2 * f),
        conv_b=dcw[:, 3, :].reshape(1, 2 * f),
        norm_final_g=dg_final[0, :],
    )
    token = small_fn(loss_blk[0, 0], grads)
    dw_in = _inproj_bwd_dw(xn, (dq, dk, dv, dug, _tie(dzf, token)))
    return grad_x, send_fn("w_in", dw_in[:, :IN_COLS])


SMALL = ("norm_mix_g", "b_forget", "gmlp_norm_g", "w_spatial", "b_spatial", "norm_ffn_g", "conv_b", "norm_final_g")


def kernel(x, norm_mix_g, w_in, b_forget, gmlp_norm_g, w_spatial, b_spatial, w_out, norm_ffn_g, w_up, conv_w, conv_b, w_down, norm_final_g, loss_target, m_norm_mix_g, m_w_in, m_b_forget, m_gmlp_norm_g, m_w_spatial, m_b_spatial, m_w_out, m_norm_ffn_g, m_w_up, m_conv_w, m_conv_b, m_w_down, m_norm_final_g, v_norm_mix_g, v_w_in, v_b_forget, v_gmlp_norm_g, v_w_spatial, v_b_spatial, v_w_out, v_norm_ffn_g, v_w_up, v_conv_w, v_conv_b, v_w_down, v_norm_final_g):
    weights = dict(norm_mix_g=norm_mix_g, w_in=w_in, b_forget=b_forget, gmlp_norm_g=gmlp_norm_g, w_spatial=w_spatial,
                   b_spatial=b_spatial, w_out=w_out, norm_ffn_g=norm_ffn_g, w_up=w_up, conv_w=conv_w, conv_b=conv_b,
                   w_down=w_down, norm_final_g=norm_final_g)
    m_in = dict(norm_mix_g=m_norm_mix_g, w_in=m_w_in, b_forget=m_b_forget, gmlp_norm_g=m_gmlp_norm_g,
                w_spatial=m_w_spatial, b_spatial=m_b_spatial, w_out=m_w_out, norm_ffn_g=m_norm_ffn_g, w_up=m_w_up,
                conv_w=m_conv_w, conv_b=m_conv_b, w_down=m_w_down, norm_final_g=m_norm_final_g)
    v_in = dict(norm_mix_g=v_norm_mix_g, w_in=v_w_in, b_forget=v_b_forget, gmlp_norm_g=v_gmlp_norm_g,
                w_spatial=v_w_spatial, b_spatial=v_b_spatial, w_out=v_w_out, norm_ffn_g=v_norm_ffn_g, w_up=v_w_up,
                conv_w=v_conv_w, conv_b=v_conv_b, w_down=v_w_down, norm_final_g=v_norm_final_g)
    order = list(weights)
    me = 4 * lax.axis_index("x") + 2 * lax.axis_index("y") + lax.axis_index("c")
    n_in, n_up = w_in.shape[2], w_up.shape[2]
    r_out, r_down = w_out.shape[1], w_down.shape[1]

    def with_mine(landed, mine):
        return lax.dynamic_update_index_in_dim(landed, mine, me, 0)

    up_blk = w_up[0].T.astype(BF16)
    out_blk = w_out[0].astype(BF16)
    down_blk = w_down[0].astype(BF16)
    taps_blk = jnp.pad(conv_w[0], ((0, 5), (0, 0)))
    (in_all,) = _all_gather([w_in[0].T.astype(BF16)], "gather_w_in")
    in_all, rest_blocks = lax.optimization_barrier((in_all, [up_blk, out_blk, down_blk, taps_blk]))
    rest_handle, token = _xchg_start(rest_blocks, [False] * 4, "gather_rest_start")
    w_in_t = in_all.reshape(N_DEV * n_in, D_MODEL)

    def rest_fn(after):
        mine, landed = _xchg_wait(rest_handle, [False] * 4, after, "gather_rest_wait")
        up_all, out_all, down_all, taps_all = [with_mine(l, b) for l, b in zip(landed, mine)]
        return (out_all.reshape(N_DEV * r_out, D_MODEL), up_all.reshape(N_DEV * n_up, D_MODEL),
                taps_all[:, :3, :].transpose(1, 0, 2).reshape(3, N_DEV * n_up),
                down_all.reshape(N_DEV * r_down, D_MODEL))

    sent = {}

    def send_fn(name, grad):
        if name == "w_in":
            parts = grad.reshape(D_MODEL, N_DEV, -1).transpose(1, 0, 2).astype(BF16)
        elif name == "w_up":
            parts = grad.reshape(D_MODEL, N_DEV, -1).transpose(1, 0, 2)
        else:
            parts = grad.reshape(N_DEV, -1, D_MODEL)
        sent[name], tok = _xchg_start([parts], [True], "scatter_" + name + "_start")
        return tok

    small = {}

    def small_fn(loss_local, g):
        loss_rows = jnp.pad(loss_local.reshape(1, 1), ((0, 31), (0, LANES - 1)))
        packed = [_rows128(g[k]) for k in SMALL] + [loss_rows, _rows128(g["conv_w"])]
        small["sizes"] = [p.shape[0] for p in packed]
        small["handle"], tok = _xchg_start([jnp.concatenate(packed, axis=0)], [False], "gather_small_start")
        return tok

    grad_x, after = _local_step(
        x[0], loss_target[0], norm_mix_g, w_in_t, b_forget, gmlp_norm_g, w_spatial, b_spatial, norm_ffn_g, conv_b,
        norm_final_g, rest_fn, send_fn, small_fn, token)

    outs = {}

    def update_big(name, after):
        (parts,), (landed,) = _xchg_wait(sent[name], [True], after, "scatter_" + name + "_wait")
        got = with_mine(landed, lax.dynamic_index_in_dim(parts, me, 0, keepdims=False))
        outs[name] = tuple(_adamw(weights[name], m_in[name], v_in[name], got, "adamw_" + name))
        return outs[name][0]

    for name in ("w_down", "w_up", "w_out"):
        after = update_big(name, after)

    (mine,), (landed,) = _xchg_wait(small["handle"], [False], after, "gather_small_wait")
    small_all = with_mine(landed, mine)
    sizes = small["sizes"]
    n_small_rows = sum(sizes[:-2])

    def pack(src):
        return jnp.concatenate([_rows128(src[k]) for k in SMALL] + [jnp.zeros((sizes[-2], LANES), F32)], axis=0)

    n_adam_rows = n_small_rows + sizes[-2]
    sg_, sd_, sm_, sv_ = _adamw(pack(weights), pack(m_in), pack(v_in), small_all[:, :n_adam_rows, :], "adamw_small")
    loss = sg_[n_small_rows, 0]
    off = 0
    for k, rows in zip(SMALL, sizes[:-2]):
        shp = weights[k].shape
        cnt = math.prod(shp)
        outs[k] = tuple(a[off:off + rows].reshape(-1)[:cnt].reshape(shp) for a in (sg_, sd_, sm_, sv_))
        off += rows
    taps_parts = small_all[:, n_adam_rows:, :].reshape(N_DEV, -1)[:, :3 * N_DEV * n_up].reshape(N_DEV, 3, N_DEV * n_up)
    taps_mine = lax.dynamic_slice_in_dim(taps_parts, me * n_up, n_up, axis=2)
    taps_mine = jnp.pad(taps_mine, ((0, 0), (0, 5), (0, 0)))

    def pad8(a):
        return jnp.pad(a[0], ((0, 5), (0, 0)))

    res = _adamw(pad8(conv_w), pad8(m_conv_w), pad8(v_conv_w), taps_mine, "adamw_conv_w")
    outs["conv_w"] = tuple(a[:3][None] for a in res)
    update_big("w_in", sg_)

    return (loss, grad_x[None], *[outs[k][0] for k in order], *[outs[k][1] for k in order],
            *[outs[k][2] for k in order], *[outs[k][3] for k in order])
```

```python
import functools
import math

import jax
import jax.numpy as jnp
from jax import lax
from jax.experimental import pallas as pl
from jax.experimental.pallas import tpu as pltpu

F32 = jnp.float32
BF16 = jnp.bfloat16

N_DEV = 8
D_MODEL = 1024
ATT_WIDTH = 512
GMLP_WIDTH = 512
HEAD_DIM = 64
N_HEADS = 8
N_PAIRS = 4
N_GROUPS = 8
GROUP_DIM = 64
CHUNK = 128
D_FF = 2816
IN_COLS = 2568
IN_PAD = 2688
QKV = 1536
UG_END = 2560
EPS = 1e-6
LANES = 128

ADAM_LR = 0.001
ADAM_B1 = 0.9
ADAM_B2 = 0.999
ADAM_EPS = 1e-08
ADAM_WD = 0.01
ADAM_STEP = 10

ATT_TQ = 1024
ATT_TK = 1024
FFN_TM, FFN_TN = 512, 1408
CONV_TM, CONV_TN = 256, 1408
VMEM_LIMIT = 56 * 1024 * 1024
MESH = pl.DeviceIdType.MESH


def _cp(sem, vmem=None):
    return pltpu.CompilerParams(dimension_semantics=sem, vmem_limit_bytes=vmem)


def _pick(n, prefs):
    for p in prefs:
        if n % p == 0:
            return p
    return n


def _split3(x):
    hi = x.astype(BF16)
    r1 = x - hi.astype(F32)
    mid = r1.astype(BF16)
    lo = (r1 - mid.astype(F32)).astype(BF16)
    return hi, mid, lo


def _dot3(x, ones_bf):
    d = functools.partial(jnp.dot, preferred_element_type=F32)
    out = []
    for c in range(0, x.shape[1], 2 * LANES):
        blk = ones_bf[c:c + 2 * LANES, c:c + 2 * LANES]
        hi, mid, lo = _split3(x[:, c:c + 2 * LANES])
        out.append(d(hi, blk) + d(mid, blk) + d(lo, blk))
    return jnp.concatenate(out, axis=1)


def _dot3l(ones_bf, x):
    n = x.shape[1]
    y = jnp.dot(ones_bf, jnp.concatenate(_split3(x), axis=1), preferred_element_type=F32)
    return y[:, :n] + y[:, n:2 * n] + y[:, 2 * n:]


def _gelu(x):
    k = math.sqrt(2.0 / math.pi)
    t = jnp.tanh(k * (x + 0.044715 * (x * x * x)))
    return 0.5 * x * (1.0 + t)


def _gelu_grad(x):
    k = math.sqrt(2.0 / math.pi)
    x2 = x * x
    t = jnp.tanh(k * (x + 0.044715 * (x2 * x)))
    return 0.5 * (1.0 + t) + 0.5 * x * (1.0 - t * t) * (k * (1.0 + 3.0 * 0.044715 * x2))


def _sigmoid(x):
    return 1.0 / (1.0 + jnp.exp(-x))


def _mm(a, b, *, mode, out_dtype, tm, tn, tk, name, res=None, a_halves=False, b_halves=False,
        out_halves=False, outer="i"):
    if mode == "tn":
        K, M = a.shape[-2], a.shape[-1] * (2 if a_halves else 1)
    else:
        M, K = a.shape[-2], a.shape[-1] * (2 if a_halves else 1)
    if mode == "nt":
        N = b.shape[-2]
        assert b.shape[-1] == K
    else:
        N = b.shape[-1] * (2 if b_halves else 1)
    tm, tn, tk = min(tm, M), min(tn, N), min(tk, K)
    assert M % tm == 0 and N % tn == 0 and K % tk == 0, (name, M, N, K, tm, tn, tk)
    nm, nn, nk = M // tm, N // tn, K // tk

    def ij(g0, g1):
        return (g0, g1) if outer == "i" else (g1, g0)

    if mode == "nn":
        dims = (((1,), (0,)), ((), ()))
        if a_halves:
            nkh = nk // 2
            a_spec = pl.BlockSpec((None, tm, tk), lambda g0, g1, k: (k // nkh, ij(g0, g1)[0], k % nkh))
        else:
            a_spec = pl.BlockSpec((tm, tk), lambda g0, g1, k: (ij(g0, g1)[0], k))
        b_spec = pl.BlockSpec((tk, tn), lambda g0, g1, k: (k, ij(g0, g1)[1]))
    elif mode == "nt":
        dims = (((1,), (1,)), ((), ()))
        if a_halves:
            nkh = nk // 2
            a_spec = pl.BlockSpec((None, tm, tk), lambda g0, g1, k: (k // nkh, ij(g0, g1)[0], k % nkh))
        else:
            a_spec = pl.BlockSpec((tm, tk), lambda g0, g1, k: (ij(g0, g1)[0], k))
        b_spec = pl.BlockSpec((tn, tk), lambda g0, g1, k: (ij(g0, g1)[1], k))
    else:
        dims = (((0,), (0,)), ((), ()))
        if a_halves:
            nmh = nm // 2
            a_spec = pl.BlockSpec((None, tk, tm), lambda g0, g1, k: (ij(g0, g1)[0] // nmh, k, ij(g0, g1)[0] % nmh))
        else:
            a_spec = pl.BlockSpec((tk, tm), lambda g0, g1, k: (k, ij(g0, g1)[0]))
        if b_halves:
            nnh = nn // 2
            b_spec = pl.BlockSpec((None, tk, tn), lambda g0, g1, k: (ij(g0, g1)[1] // nnh, k, ij(g0, g1)[1] % nnh))
        else:
            b_spec = pl.BlockSpec((tk, tn), lambda g0, g1, k: (k, ij(g0, g1)[1]))
    if out_halves:
        nnh = nn // 2
        o_spec = pl.BlockSpec((None, tm, tn), lambda g0, g1, k: (ij(g0, g1)[1] // nnh, ij(g0, g1)[0], ij(g0, g1)[1] % nnh))
        o_shape = jax.ShapeDtypeStruct((2, M, N // 2), out_dtype)
    else:
        o_spec = pl.BlockSpec((tm, tn), lambda g0, g1, k: ij(g0, g1))
        o_shape = jax.ShapeDtypeStruct((M, N), out_dtype)
    in_specs = [a_spec, b_spec]
    args = [a, b]
    if res is not None:
        in_specs.append(pl.BlockSpec((tm, tn), lambda g0, g1, k: ij(g0, g1)))
        args.append(res)

    def body(*refs):
        if res is not None:
            a_ref, b_ref, r_ref, o_ref = refs[:4]
        else:
            a_ref, b_ref, o_ref = refs[:3]
            r_ref = None
        part = lax.dot_general(a_ref[...], b_ref[...], dims, preferred_element_type=F32)
        if nk == 1:
            if r_ref is not None:
                part = part + r_ref[...]
            o_ref[...] = part.astype(out_dtype)
            return
        acc_ref = refs[-1]
        k = pl.program_id(2)

        @pl.when(k == 0)
        def _():
            acc_ref[...] = part

        @pl.when(k > 0)
        def _():
            acc_ref[...] += part

        @pl.when(k == nk - 1)
        def _():
            tot = acc_ref[...]
            if r_ref is not None:
                tot = tot + r_ref[...]
            o_ref[...] = tot.astype(out_dtype)

    grid = (nm, nn, nk) if outer == "i" else (nn, nm, nk)
    scratch = [] if nk == 1 else [pltpu.VMEM((tm, tn), F32)]
    return pl.pallas_call(
        body, out_shape=o_shape, grid=grid, in_specs=in_specs, out_specs=o_spec, scratch_shapes=scratch,
        name=name, compiler_params=_cp(("parallel", "parallel", "arbitrary"), VMEM_LIMIT),
    )(*args)


def _aug(lane, terms):
    out = 0.0
    for j, t in enumerate(terms):
        out = jnp.where(lane == HEAD_DIM + j, t, out)
    return out


def _split3f(x):
    hi, mid, lo = _split3(x)
    return [hi.astype(F32), mid.astype(F32), lo.astype(F32)]


def _inproj_fwd(x, g_mix, w_pad, bf_pad):
    S = x.shape[0]
    tm = _pick(S, (512, 256))
    tri = (lax.broadcasted_iota(jnp.int32, (tm, tm), 0) >= lax.broadcasted_iota(jnp.int32, (tm, tm), 1)).astype(BF16)

    def body(x_ref, g_ref, w_ref, bf_ref, tri_ref, put_ref, one_ref, xn_ref, qa_ref, ka_ref, va_ref, ug_ref, zf_ref,
             carry_ref):
        i = pl.program_id(0)

        @pl.when(i == 0)
        def _():
            carry_ref[...] = jnp.zeros_like(carry_ref)

        xf = x_ref[...]
        r = lax.rsqrt(jnp.mean(xf * xf, axis=-1, keepdims=True) + EPS)
        xn = ((xf * r) * g_ref[...]).astype(BF16)
        xn_ref[...] = xn
        proj = lax.dot_general(xn, w_ref[...], _NT, preferred_element_type=F32)
        ug_ref[...] = proj[:, QKV:UG_END]
        zf = proj[:, UG_END:] + bf_ref[...]
        zf_ref[...] = zf
        lf = jnp.minimum(zf, 0.0) - jnp.log(1.0 + jnp.exp(-jnp.abs(zf)))
        c = _dot3l(tri_ref[...], lf) + carry_ref[0:1, :]
        carry_ref[0:1, :] = c[tm - 1:tm, :]
        c3 = jnp.concatenate(_split3(c), axis=1)
        aug_q = jnp.dot(c3, put_ref[0], preferred_element_type=F32) + one_ref[0:1, :]
        aug_k = jnp.dot(c3, put_ref[1], preferred_element_type=F32) + one_ref[1:2, :]
        lane = lax.broadcasted_iota(jnp.int32, (tm, LANES), 1)
        for h in range(N_HEADS):
            p, odd = h // 2, h % 2

            def head(base, scale=None, p=p, odd=odd):
                blk = proj[:, base + p * LANES:base + (p + 1) * LANES]
                if scale is not None:
                    blk = blk * scale
                return pltpu.roll(blk, HEAD_DIM, 1) if odd else blk

            cols = slice(h * LANES, (h + 1) * LANES)
            qa_ref[:, cols] = jnp.where(lane < HEAD_DIM, head(0, HEAD_DIM ** -0.5), aug_q[:, cols]).astype(BF16)
            ka_ref[:, cols] = jnp.where(lane < HEAD_DIM, head(ATT_WIDTH), aug_k[:, cols]).astype(BF16)
            va_ref[:, cols] = jnp.where(lane < HEAD_DIM, head(2 * ATT_WIDTH), one_ref[2:3, cols]).astype(BF16)

    wide = N_HEADS * LANES
    src = lax.broadcasted_iota(jnp.int32, (3 * LANES, wide), 0)
    col = lax.broadcasted_iota(jnp.int32, (3 * LANES, wide), 1)
    hd, term = src % LANES, src // LANES
    to_q = (col == hd * LANES + HEAD_DIM + term) & (hd < N_HEADS)
    to_k = (col == hd * LANES + HEAD_DIM + 3 + term) & (hd < N_HEADS)
    put = jnp.stack([to_q.astype(BF16), -to_k.astype(BF16)])
    off = lax.broadcasted_iota(jnp.int32, (8, wide), 1) % LANES - HEAD_DIM
    row = lax.broadcasted_iota(jnp.int32, (8, wide), 0)
    q_one = (off >= 3) & (off < 6)
    k_one = ((off >= 0) & (off < 3)) | ((off >= 6) & (off < 9))
    v_one = (off >= 0) & (off < 3)
    ones = jnp.where(row == 0, q_one, jnp.where(row == 1, k_one, (row == 2) & v_one)).astype(F32)
    return pl.pallas_call(
        body,
        out_shape=(jax.ShapeDtypeStruct((S, D_MODEL), BF16), jax.ShapeDtypeStruct((S, wide), BF16),
                   jax.ShapeDtypeStruct((S, wide), BF16), jax.ShapeDtypeStruct((S, wide), BF16),
                   jax.ShapeDtypeStruct((S, 2 * GMLP_WIDTH), F32), jax.ShapeDtypeStruct((S, LANES), F32)),
        grid=(S // tm,),
        in_specs=[pl.BlockSpec((tm, D_MODEL), lambda i: (i, 0)), pl.BlockSpec((1, D_MODEL), lambda i: (0, 0)),
                  pl.BlockSpec((IN_PAD, D_MODEL), lambda i: (0, 0)), pl.BlockSpec((1, LANES), lambda i: (0, 0)),
                  pl.BlockSpec((tm, tm), lambda i: (0, 0)), pl.BlockSpec((2, 3 * LANES, wide), lambda i: (0, 0, 0)),
                  pl.BlockSpec((8, wide), lambda i: (0, 0))],
        out_specs=(pl.BlockSpec((tm, D_MODEL), lambda i: (i, 0)), pl.BlockSpec((tm, wide), lambda i: (i, 0)),
                   pl.BlockSpec((tm, wide), lambda i: (i, 0)), pl.BlockSpec((tm, wide), lambda i: (i, 0)),
                   pl.BlockSpec((tm, 2 * GMLP_WIDTH), lambda i: (i, 0)), pl.BlockSpec((tm, LANES), lambda i: (i, 0))),
        scratch_shapes=[pltpu.VMEM((8, LANES), F32)],
        name="inproj_fwd", compiler_params=_cp(("arbitrary",), VMEM_LIMIT),
    )(x, g_mix, w_pad, bf_pad, tri, put, ones)


def _group_ones():
    r = lax.broadcasted_iota(jnp.int32, (GMLP_WIDTH, GMLP_WIDTH), 0) // GROUP_DIM
    c = lax.broadcasted_iota(jnp.int32, (GMLP_WIDTH, GMLP_WIDTH), 1) // GROUP_DIM
    return (r == c).astype(BF16)


def _gmlp_mixed(vn_bf, w_ref, bias, n_chunks):
    lane = lax.broadcasted_iota(jnp.int32, (CHUNK, LANES), 1)
    row = lax.broadcasted_iota(jnp.int32, (CHUNK, CHUNK), 0)
    col = lax.broadcasted_iota(jnp.int32, (CHUNK, CHUNK), 1)
    ws = [jnp.where(row >= col, w_ref[g], 0.0).astype(BF16) for g in range(N_GROUPS)]
    rows = []
    for ci in range(n_chunks):
        cols = []
        for pp in range(N_GROUPS // 2):
            v = vn_bf[ci * CHUNK:(ci + 1) * CHUNK, pp * LANES:(pp + 1) * LANES]
            v_lo = jnp.where(lane < GROUP_DIM, v, jnp.zeros_like(v))
            v_hi = jnp.where(lane >= GROUP_DIM, v, jnp.zeros_like(v))
            m = (jnp.dot(ws[2 * pp], v_lo, preferred_element_type=F32)
                 + jnp.dot(ws[2 * pp + 1], v_hi, preferred_element_type=F32))
            cols.append(m + bias[:, pp * LANES:(pp + 1) * LANES])
        rows.append(jnp.concatenate(cols, axis=1))
    return jnp.concatenate(rows, axis=0)


def _gmlp_fwd(ug, gain, w_s, bias_full):
    S = ug.shape[0]
    tm = _pick(S, (512, 256, 128))
    ones = _group_ones()

    def body(ug_ref, gain_ref, w_ref, bias_ref, ones_ref, sg_ref):
        u = _gelu(ug_ref[:, :GMLP_WIDTH])
        vr = _gelu(ug_ref[:, GMLP_WIDTH:])
        ms = _dot3(vr * vr, ones_ref[...]) * (1.0 / GROUP_DIM)
        vn = ((vr * lax.rsqrt(ms + EPS)) * gain_ref[...]).astype(BF16)
        mixed = _gmlp_mixed(vn, w_ref, bias_ref[...], tm // CHUNK)
        sg_ref[...] = (u * mixed).astype(BF16)

    return pl.pallas_call(
        body, out_shape=jax.ShapeDtypeStruct((S, GMLP_WIDTH), BF16), grid=(S // tm,),
        in_specs=[pl.BlockSpec((tm, 2 * GMLP_WIDTH), lambda i: (i, 0)), pl.BlockSpec((1, GMLP_WIDTH), lambda i: (0, 0)),
                  pl.BlockSpec((N_GROUPS, CHUNK, CHUNK), lambda i: (0, 0, 0)),
                  pl.BlockSpec((CHUNK, GMLP_WIDTH), lambda i: (0, 0)),
                  pl.BlockSpec((GMLP_WIDTH, GMLP_WIDTH), lambda i: (0, 0))],
        out_specs=pl.BlockSpec((tm, GMLP_WIDTH), lambda i: (i, 0)),
        name="gmlp_fwd", compiler_params=_cp(("parallel",), VMEM_LIMIT),
    )(ug, gain, w_s, bias_full, ones)


_NT = (((1,), (1,)), ((), ()))
_TN = (((0,), (0,)), ((), ()))


def _attn_fwd(qa, ka, va):
    S = qa.shape[0]
    tq = _pick(S, (ATT_TQ, 256))
    tk = min(ATT_TK, tq)
    nq = S // tq
    assert tq == tk, "the diagonal block is handled as one tq x tq tile"
    per_q = 1

    def body(q_ref, k_ref, v_ref, o_ref, lse_ref, ob_ref):
        qi = pl.program_id(1)
        lane = lax.broadcasted_iota(jnp.int32, (tq, LANES), 1)
        qs = [q_ref[:, :LANES], q_ref[:, LANES:]]

        def update(q, ks, k_len, h, m, acc, first_row):
            cols = slice(h * LANES, (h + 1) * LANES)
            s = lax.dot_general(q, k_ref[pl.ds(ks, k_len), cols], _NT, preferred_element_type=F32)
            if first_row is not None:
                rid = lax.broadcasted_iota(jnp.int32, s.shape, 0) + first_row
                s = jnp.where(rid >= lax.broadcasted_iota(jnp.int32, s.shape, 1), s, -jnp.inf)
            m_new = jnp.maximum(m, jnp.max(s, axis=-1, keepdims=True))
            p = jnp.exp(s - m_new).astype(BF16)
            acc = jnp.exp(m - m_new) * acc + jnp.dot(p, v_ref[pl.ds(ks, k_len), cols], preferred_element_type=F32)
            return m_new, acc

        def step(kb, carry):
            ks = pl.multiple_of(kb * tk, tk)
            return tuple(update(qs[h], ks, tk, h, *carry[h], None) for h in range(2))

        one = (jnp.full((tq, 1), -jnp.inf, F32), jnp.zeros((tq, LANES), F32))
        carry = lax.fori_loop(0, qi * per_q, step, (one, one))
        outs, lses = [], []
        strip = tq // 2
        diag = pl.multiple_of(qi * tq, tq)
        for h in range(2):
            ms, accs = [], []
            for r in range(2):
                rows = slice(r * strip, (r + 1) * strip)
                m, acc = update(qs[h][rows], diag, (r + 1) * strip, h, carry[h][0][rows], carry[h][1][rows], r * strip)
                ms.append(m)
                accs.append(acc)
            m, acc = jnp.concatenate(ms, axis=0), jnp.concatenate(accs, axis=0)
            l = acc[:, HEAD_DIM:HEAD_DIM + 1]
            outs.append(acc / l)
            lses.append(m + jnp.log(l))
        o = jnp.where(lane < HEAD_DIM, outs[0], pltpu.roll(outs[1], HEAD_DIM, 1))
        o_ref[...] = o
        ob_ref[...] = o.astype(BF16)
        lse_ref[...] = jnp.where(lane < HEAD_DIM, lses[0], lses[1])

    return pl.pallas_call(
        body,
        out_shape=(jax.ShapeDtypeStruct((S, ATT_WIDTH), F32), jax.ShapeDtypeStruct((S, ATT_WIDTH), F32),
                   jax.ShapeDtypeStruct((S, ATT_WIDTH), BF16)),
        grid=(N_PAIRS, nq),
        in_specs=[pl.BlockSpec((tq, 2 * LANES), lambda p, i: (i, p)),
                  pl.BlockSpec((S, 2 * LANES), lambda p, i: (0, p)),
                  pl.BlockSpec((S, 2 * LANES), lambda p, i: (0, p))],
        out_specs=(pl.BlockSpec((tq, LANES), lambda p, i: (i, p)), pl.BlockSpec((tq, LANES), lambda p, i: (i, p)),
                   pl.BlockSpec((tq, LANES), lambda p, i: (i, p))),
        name="attn_fwd", compiler_params=_cp(("parallel", "parallel"), VMEM_LIMIT),
    )(qa, ka, va)


def _shift_rows(x, prev, n):
    rid = lax.broadcasted_iota(jnp.int32, x.shape, 0)
    y = pltpu.roll(x, n, 0)
    if n == 1:
        return jnp.where(rid == 0, prev[7:8, :], y)
    return jnp.where(rid == 0, prev[6:7, :], jnp.where(rid == 1, prev[7:8, :], y))


def _shift_rows_up(x, nxt, n):
    rows = x.shape[0]
    rid = lax.broadcasted_iota(jnp.int32, x.shape, 0)
    y = pltpu.roll(x, rows - n, 0)
    if n == 1:
        return jnp.where(rid == rows - 1, nxt[0:1, :], y)
    return jnp.where(rid == rows - 2, nxt[0:1, :], jnp.where(rid == rows - 1, nxt[1:2, :], y))


def _conv3(cur, prev, w, b):
    return (w[0:1, :] * _shift_rows(cur, prev, 2) + w[1:2, :] * _shift_rows(cur, prev, 1)
            + w[2:3, :] * cur + b)


def _ffn_up_conv(hn, w_up_bf, cw, cb):
    S = hn.shape[0]
    F = D_FF
    tm = _pick(S, (FFN_TM, 256))
    tn = _pick(F, (FFN_TN, 256, 128))
    nj = F // tn

    def body(hn_ref, wa_ref, wg_ref, cw_ref, cb_ref, hu_ref, hc_ref, act_ref, tail_ref):
        i = pl.program_id(1)

        @pl.when(i == 0)
        def _():
            tail_ref[...] = jnp.zeros_like(tail_ref)

        hn_v = hn_ref[...]
        halves = []
        for h, w_ref in enumerate((wa_ref, wg_ref)):
            hu = lax.dot_general(hn_v, w_ref[...], _NT, preferred_element_type=F32)
            hu_ref[h] = hu.astype(BF16)
            hc = _conv3(hu, tail_ref[h], cw_ref[h], cb_ref[h])
            hc_ref[h] = hc
            halves.append(hc)
            tail_ref[h] = hu[tm - 8:, :]
        a, g = halves
        act_ref[...] = (g * _sigmoid(g) * a).astype(BF16)

    both = pl.BlockSpec((2, tm, tn), lambda j, i: (0, i, j))
    return pl.pallas_call(
        body, out_shape=(jax.ShapeDtypeStruct((2, S, F), BF16), jax.ShapeDtypeStruct((2, S, F), F32),
                         jax.ShapeDtypeStruct((S, F), BF16)),
        grid=(nj, S // tm),
        in_specs=[pl.BlockSpec((tm, D_MODEL), lambda j, i: (i, 0)),
                  pl.BlockSpec((tn, D_MODEL), lambda j, i: (j, 0)),
                  pl.BlockSpec((tn, D_MODEL), lambda j, i: (nj + j, 0)),
                  pl.BlockSpec((2, 8, tn), lambda j, i: (0, 0, j)),
                  pl.BlockSpec((2, 1, tn), lambda j, i: (0, 0, j))],
        out_specs=(both, both, pl.BlockSpec((tm, tn), lambda j, i: (i, j))),
        scratch_shapes=[pltpu.VMEM((2, 8, tn), F32)],
        name="ffn_up_conv", compiler_params=_cp(("parallel", "arbitrary"), VMEM_LIMIT),
    )(hn, w_up_bf, w_up_bf, cw, cb)


def _ffn_down_loss(act, w_down_bf, h1, g_final, target):
    S = h1.shape[0]
    tm = _pick(S, (512, 256))

    def body(a_ref, w_ref, h1_ref, g_ref, t_ref, loss_ref, dh_ref, dhb_ref, dg_ref):
        i = pl.program_id(0)

        @pl.when(i == 0)
        def _():
            loss_ref[...] = jnp.zeros_like(loss_ref)
            dg_ref[...] = jnp.zeros_like(dg_ref)

        hf = h1_ref[...] + jnp.dot(a_ref[...], w_ref[...], preferred_element_type=F32)
        g = g_ref[...]
        r = lax.rsqrt(jnp.mean(hf * hf, axis=-1, keepdims=True) + EPS)
        hhat = hf * r
        err = hhat * g - t_ref[...]
        loss_ref[...] += 0.5 * jnp.sum(jnp.mean(err * err, axis=-1, keepdims=True))
        dy = err * (1.0 / D_MODEL)
        dg_ref[0:1, :] += jnp.sum(dy * hhat, axis=0, keepdims=True)
        dhat = dy * g
        dh = r * (dhat - hhat * jnp.mean(dhat * hhat, axis=-1, keepdims=True))
        dh_ref[...] = dh
        dhb_ref[...] = dh.astype(BF16)

    row = pl.BlockSpec((tm, D_MODEL), lambda i: (i, 0))
    return pl.pallas_call(
        body,
        out_shape=(jax.ShapeDtypeStruct((8, LANES), F32), jax.ShapeDtypeStruct((S, D_MODEL), F32),
                   jax.ShapeDtypeStruct((S, D_MODEL), BF16), jax.ShapeDtypeStruct((8, D_MODEL), F32)),
        grid=(S // tm,),
        in_specs=[pl.BlockSpec((tm, D_FF), lambda i: (i, 0)), pl.BlockSpec((D_FF, D_MODEL), lambda i: (0, 0)), row,
                  pl.BlockSpec((1, D_MODEL), lambda i: (0, 0)), row],
        out_specs=(pl.BlockSpec((8, LANES), lambda i: (0, 0)), row, row, pl.BlockSpec((8, D_MODEL), lambda i: (0, 0))),
        name="ffn_down_loss", compiler_params=_cp(("arbitrary",), VMEM_LIMIT),
    )(act, w_down_bf, h1, g_final, target)


def _ffn_up_dx_rms(dhu, w_up_bf, h1, g_ffn, dh2):
    _, S, F = dhu.shape
    tm = _pick(S, (512, 256))

    def body(a_ref, b_ref, h_ref, g_ref, r_ref, dh_ref, dhb_ref, dg_ref):
        i = pl.program_id(0)

        @pl.when(i == 0)
        def _():
            dg_ref[...] = jnp.zeros_like(dg_ref)

        dyv = (jnp.dot(a_ref[0], b_ref[:F, :], preferred_element_type=F32)
               + jnp.dot(a_ref[1], b_ref[F:, :], preferred_element_type=F32))
        hf = h_ref[...]
        r = lax.rsqrt(jnp.mean(hf * hf, axis=-1, keepdims=True) + EPS)
        hhat = hf * r
        dg_ref[0:1, :] += jnp.sum(dyv * hhat, axis=0, keepdims=True)
        dhat = dyv * g_ref[...]
        dh = r_ref[...] + r * (dhat - hhat * jnp.mean(dhat * hhat, axis=-1, keepdims=True))
        dh_ref[...] = dh
        dhb_ref[...] = dh.astype(BF16)

    row = pl.BlockSpec((tm, D_MODEL), lambda i: (i, 0))
    return pl.pallas_call(
        body,
        out_shape=(jax.ShapeDtypeStruct((S, D_MODEL), F32), jax.ShapeDtypeStruct((S, D_MODEL), BF16),
                   jax.ShapeDtypeStruct((8, D_MODEL), F32)),
        grid=(S // tm,),
        in_specs=[pl.BlockSpec((2, tm, F), lambda i: (0, i, 0)), pl.BlockSpec((2 * F, D_MODEL), lambda i: (0, 0)),
                  row, pl.BlockSpec((1, D_MODEL), lambda i: (0, 0)), row],
        out_specs=(row, row, pl.BlockSpec((8, D_MODEL), lambda i: (0, 0))),
        name="ffn_up_dx_rms", compiler_params=_cp(("arbitrary",), VMEM_LIMIT),
    )(dhu, w_up_bf, h1, g_ffn, dh2)


def _conv_gate_bwd(hc, hu, dact, cw):
    _, S, F = hu.shape
    tm = _pick(S, (CONV_TM, 128))
    tn = _pick(F, (CONV_TN, 256, 128))
    r8 = tm // 8
    n_i = S // tm
    last8 = S // 8 - 1

    def body(hc_ref, hcn_ref, hu_ref, da_ref, dan_ref, w_ref, dhu_ref, dcw_ref):
        i = pl.program_id(1)

        @pl.when(i == 0)
        def _():
            dcw_ref[...] = jnp.zeros_like(dcw_ref)

        rid8 = lax.broadcasted_iota(jnp.int32, (8, tn), 0)

        def gate_grads(a, g, d):
            sg = _sigmoid(g)
            return d * (g * sg), d * a * (sg * (1.0 + g * (1.0 - sg)))

        dhc = gate_grads(hc_ref[0], hc_ref[1], da_ref[...])
        dhc_n = gate_grads(hcn_ref[0], hcn_ref[1], dan_ref[...])
        for h in range(2):
            w = w_ref[h]
            d = dhc[h]
            dn = jnp.where(i < n_i - 1, dhc_n[h], 0.0)
            u1 = _shift_rows_up(d, dn, 1)
            u2 = _shift_rows_up(d, dn, 2)
            dhu_ref[h] = (w[2:3, :] * d + w[1:2, :] * u1 + w[0:1, :] * u2).astype(BF16)
            x = hu_ref[h].astype(F32)
            t0, t1, t2, t3 = [jnp.sum(t, axis=0, keepdims=True) for t in (u2 * x, u1 * x, d * x, d)]
            dcw_ref[h] += jnp.where(rid8 == 0, t0, jnp.where(rid8 == 1, t1, jnp.where(rid8 == 2, t2, jnp.where(rid8 == 3, t3, 0.0))))

    cur = pl.BlockSpec((2, tm, tn), lambda j, i: (0, i, j))
    return pl.pallas_call(
        body,
        out_shape=(jax.ShapeDtypeStruct((2, S, F), BF16), jax.ShapeDtypeStruct((2, 8, F), F32)),
        grid=(F // tn, n_i),
        in_specs=[cur, pl.BlockSpec((2, 8, tn), lambda j, i: (0, jnp.minimum((i + 1) * r8, last8), j)), cur,
                  pl.BlockSpec((tm, tn), lambda j, i: (i, j)),
                  pl.BlockSpec((8, tn), lambda j, i: (jnp.minimum((i + 1) * r8, last8), j)),
                  pl.BlockSpec((2, 8, tn), lambda j, i: (0, 0, j))],
        out_specs=(cur, pl.BlockSpec((2, 8, tn), lambda j, i: (0, 0, j))),
        name="conv_gate_bwd", compiler_params=_cp(("parallel", "arbitrary"), VMEM_LIMIT),
    )(hc, hc, hu, dact, dact, cw)


def _out_proj_dx_prep(dh1_bf, w_out_bf, att, lse, qa):
    S = att.shape[0]
    tm = _pick(S, (256,))

    def body(dh_ref, w_ref, o_ref, lse_ref, q_ref, dsg_ref, qb_ref, doa_ref):
        lane = lax.broadcasted_iota(jnp.int32, (tm, LANES), 1)
        dh = dh_ref[...]
        dsg_ref[...] = lax.dot_general(dh, w_ref[ATT_WIDTH:, :], _NT, preferred_element_type=F32)
        datt = lax.dot_general(dh, w_ref[:ATT_WIDTH, :], _NT, preferred_element_type=F32)
        for p in range(N_PAIRS):
            pc = slice(p * LANES, (p + 1) * LANES)
            do = datt[:, pc]
            prod = o_ref[:, pc] * do
            for hh in range(2):
                sel = (lane >= HEAD_DIM) if hh else (lane < HEAD_DIM)
                delta = jnp.sum(jnp.where(sel, prod, 0.0), axis=-1, keepdims=True)
                dod = pltpu.roll(do, HEAD_DIM, 1) if hh else do
                cols = slice((2 * p + hh) * LANES, (2 * p + hh + 1) * LANES)
                doa_ref[:, cols] = jnp.where(lane < HEAD_DIM, dod, _aug(lane, _split3f(-delta))).astype(BF16)
                lcol = p * LANES + hh * HEAD_DIM
                l3 = _split3f(-lse_ref[:, lcol:lcol + 1])
                augl = jnp.where(lane == HEAD_DIM + 6, l3[0], jnp.where(lane == HEAD_DIM + 7, l3[1], l3[2])).astype(BF16)
                qb_ref[:, cols] = jnp.where((lane >= HEAD_DIM + 6) & (lane < HEAD_DIM + 9), augl, q_ref[:, cols])

    half = pl.BlockSpec((tm, ATT_WIDTH), lambda i: (i, 0))
    wide = pl.BlockSpec((tm, N_HEADS * LANES), lambda i: (i, 0))
    return pl.pallas_call(
        body,
        out_shape=(jax.ShapeDtypeStruct((S, GMLP_WIDTH), F32), jax.ShapeDtypeStruct(qa.shape, BF16),
                   jax.ShapeDtypeStruct(qa.shape, BF16)),
        grid=(S // tm,),
        in_specs=[pl.BlockSpec((tm, D_MODEL), lambda i: (i, 0)), pl.BlockSpec((D_MODEL, D_MODEL), lambda i: (0, 0)),
                  half, half, wide],
        out_specs=(half, wide, wide),
        name="out_proj_dx_prep", compiler_params=_cp(("parallel",), VMEM_LIMIT),
    )(dh1_bf, w_out_bf, att, lse, qa)


def _attn_bwd(qb, ka, va, doa):
    S = qb.shape[0]
    tk = _pick(S, (512, 256))
    tq = tk
    nq = S // tq

    def pair(a, scale=None):
        lane = lax.broadcasted_iota(jnp.int32, (a.shape[0], LANES), 1)
        out = jnp.where(lane < HEAD_DIM, a[:, :LANES], pltpu.roll(a[:, LANES:], HEAD_DIM, 1))
        return out if scale is None else out * scale

    def head_lanes(a, col, sign, first):
        lane = lax.broadcasted_iota(jnp.int32, (a.shape[0], LANES), 1)
        return jnp.where(lane == first, sign * a[:, col:col + 1],
                         jnp.where(lane == first + 1, sign * a[:, LANES + col:LANES + col + 1], 0.0))

    def body(q_ref, do_ref, k_ref, v_ref, dqc_ref, dkc_ref, dvc_ref, dcq_ref, dck_ref, dq_ref, dka_ref, dva_ref):
        kb = pl.program_id(1)

        @pl.when(kb == 0)
        def _():
            dq_ref[...] = jnp.zeros_like(dq_ref)

        dka_ref[...] = jnp.zeros_like(dka_ref)
        dva_ref[...] = jnp.zeros_like(dva_ref)
        def sub_tile(qs, q_len, k_off, k_len, masked):
            keys = slice(k_off, k_off + k_len)
            for h in range(2):
                cols = slice(h * LANES, (h + 1) * LANES)
                qblk = q_ref[pl.ds(qs, q_len), cols]
                doblk = do_ref[pl.ds(qs, q_len), cols]
                kh = k_ref[keys, cols]
                p = jnp.exp(lax.dot_general(kh, qblk, _NT, preferred_element_type=F32))
                if masked:
                    p = jnp.where(lax.broadcasted_iota(jnp.int32, p.shape, 1) >= lax.broadcasted_iota(jnp.int32, p.shape, 0),
                                  p, 0.0)
                ds = (p * lax.dot_general(v_ref[keys, cols], doblk, _NT, preferred_element_type=F32)).astype(BF16)
                dva_ref[keys, cols] += jnp.dot(p.astype(BF16), doblk, preferred_element_type=F32)
                dka_ref[keys, cols] += jnp.dot(ds, qblk, preferred_element_type=F32)
                dq_ref[pl.ds(qs, q_len), cols] += lax.dot_general(ds, kh, _TN, preferred_element_type=F32)

        half = tk // 2
        sub_tile(pl.multiple_of(kb * tq, tq), tq, 0, half, True)
        sub_tile(pl.multiple_of(kb * tq + half, half), half, half, half, True)

        def step(qi, carry):
            sub_tile(pl.multiple_of(qi * tq, tq), tq, 0, tk, False)
            return carry

        lax.fori_loop(kb + 1, nq, step, 0)
        dka = dka_ref[...]
        dkc_ref[...] = pair(dka).astype(BF16)
        dvc_ref[...] = pair(dva_ref[...]).astype(BF16)
        first = 2 * pl.program_id(0)
        dck_ref[...] = head_lanes(dka, HEAD_DIM + 3, -1.0, first)

        @pl.when(kb == nq - 1)
        def _():
            dqa = dq_ref[...]
            dqc_ref[...] = pair(dqa, HEAD_DIM ** -0.5).astype(BF16)
            dcq_ref[...] = head_lanes(dqa, HEAD_DIM, 1.0, first)

    wide = 2 * LANES
    half = jax.ShapeDtypeStruct((S, ATT_WIDTH), BF16)
    slabs = jax.ShapeDtypeStruct((N_PAIRS, S, LANES), F32)
    return pl.pallas_call(
        body,
        out_shape=(half, half, half, slabs, slabs),
        grid=(N_PAIRS, nq),
        in_specs=[pl.BlockSpec((S, wide), lambda p, j: (0, p)), pl.BlockSpec((S, wide), lambda p, j: (0, p)),
                  pl.BlockSpec((tk, wide), lambda p, j: (j, p)), pl.BlockSpec((tk, wide), lambda p, j: (j, p))],
        out_specs=(pl.BlockSpec((S, LANES), lambda p, j: (0, p)), pl.BlockSpec((tk, LANES), lambda p, j: (j, p)),
                   pl.BlockSpec((tk, LANES), lambda p, j: (j, p)), pl.BlockSpec((None, S, LANES), lambda p, j: (p, 0, 0)),
                   pl.BlockSpec((None, tk, LANES), lambda p, j: (p, j, 0))),
        scratch_shapes=[pltpu.VMEM((S, wide), F32), pltpu.VMEM((tk, wide), F32), pltpu.VMEM((tk, wide), F32)],
        name="attn_bwd", compiler_params=_cp(("parallel", "arbitrary"), VMEM_LIMIT),
    )(qb, doa, ka, va)


def _gmlp_bwd(ug, dsg, gain, w_s, wt_s, bias_full):
    S = ug.shape[0]
    tm = _pick(S, (512, 256, 128))
    n_chunks = tm // CHUNK
    n_i = S // tm
    ones = _group_ones()
    nt = (((1,), (1,)), ((), ()))

    def body(ug_ref, dsg_ref, gain_ref, w_ref, wt_ref, bias_ref, ones_ref, dug_ref, dw_ref, dgain_ref, dbias_ref,
             dbacc_ref):
        i = pl.program_id(0)

        @pl.when(i == 0)
        def _():
            dw_ref[...] = jnp.zeros_like(dw_ref)
            dgain_ref[...] = jnp.zeros_like(dgain_ref)
            dbacc_ref[...] = jnp.zeros_like(dbacc_ref)

        ones_m = ones_ref[...]
        pu = ug_ref[:, :GMLP_WIDTH]
        pg = ug_ref[:, GMLP_WIDTH:]
        u = _gelu(pu)
        vr = _gelu(pg)
        ms = _dot3(vr * vr, ones_m) * (1.0 / GROUP_DIM)
        rinv = lax.rsqrt(ms + EPS)
        vhat = vr * rinv
        gain_v = gain_ref[...]
        vn = (vhat * gain_v).astype(BF16)
        mixed = _gmlp_mixed(vn, w_ref, bias_ref[...], n_chunks)
        dsg_v = dsg_ref[...]
        du = dsg_v * mixed
        dmixed = dsg_v * u
        dm_bf = dmixed.astype(BF16)
        lane = lax.broadcasted_iota(jnp.int32, (CHUNK, LANES), 1)
        row = lax.broadcasted_iota(jnp.int32, (CHUNK, CHUNK), 0)
        col = lax.broadcasted_iota(jnp.int32, (CHUNK, CHUNK), 1)
        wts = [jnp.where(col >= row, wt_ref[g], 0.0).astype(BF16) for g in range(N_GROUPS)]
        dvn_rows = []
        dbsum = jnp.zeros((CHUNK, GMLP_WIDTH), F32)
        for ci in range(n_chunks):
            rs = slice(ci * CHUNK, (ci + 1) * CHUNK)
            dbsum = dbsum + dmixed[rs, :]
            cols = []
            for pp in range(N_GROUPS // 2):
                cs = slice(pp * LANES, (pp + 1) * LANES)
                dm = dm_bf[rs, cs]
                dm_lo = jnp.where(lane < GROUP_DIM, dm, jnp.zeros_like(dm))
                dm_hi = jnp.where(lane >= GROUP_DIM, dm, jnp.zeros_like(dm))
                vb = vn[rs, cs]
                dw_ref[2 * pp] += lax.dot_general(dm_lo, vb, nt, preferred_element_type=F32)
                dw_ref[2 * pp + 1] += lax.dot_general(dm_hi, vb, nt, preferred_element_type=F32)
                cols.append(jnp.dot(wts[2 * pp], dm_lo, preferred_element_type=F32)
                            + jnp.dot(wts[2 * pp + 1], dm_hi, preferred_element_type=F32))
            dvn_rows.append(jnp.concatenate(cols, axis=1))
        dvn = jnp.concatenate(dvn_rows, axis=0)
        dbacc_ref[...] += dbsum
        dgain_ref[0:1, :] += jnp.sum(dvn * vhat, axis=0, keepdims=True)
        dvhat = dvn * gain_v
        gm = _dot3(dvhat * vhat, ones_m) * (1.0 / GROUP_DIM)
        dvr = rinv * (dvhat - vhat * gm)
        dug_ref[:, :GMLP_WIDTH] = (du * _gelu_grad(pu)).astype(BF16)
        dug_ref[:, GMLP_WIDTH:] = (dvr * _gelu_grad(pg)).astype(BF16)

        @pl.when(i == n_i - 1)
        def _():
            for g in range(N_GROUPS):
                dw_ref[g] = jnp.where(row >= col, dw_ref[g], 0.0)
            dbias_ref[...] = _dot3(dbacc_ref[...], ones_m)

    return pl.pallas_call(
        body,
        out_shape=(jax.ShapeDtypeStruct((S, 2 * GMLP_WIDTH), BF16), jax.ShapeDtypeStruct((N_GROUPS, CHUNK, CHUNK), F32),
                   jax.ShapeDtypeStruct((8, GMLP_WIDTH), F32), jax.ShapeDtypeStruct((CHUNK, GMLP_WIDTH), F32)),
        grid=(n_i,),
        in_specs=[pl.BlockSpec((tm, 2 * GMLP_WIDTH), lambda i: (i, 0)), pl.BlockSpec((tm, GMLP_WIDTH), lambda i: (i, 0)),
                  pl.BlockSpec((1, GMLP_WIDTH), lambda i: (0, 0)),
                  pl.BlockSpec((N_GROUPS, CHUNK, CHUNK), lambda i: (0, 0, 0)),
                  pl.BlockSpec((N_GROUPS, CHUNK, CHUNK), lambda i: (0, 0, 0)),
                  pl.BlockSpec((CHUNK, GMLP_WIDTH), lambda i: (0, 0)),
                  pl.BlockSpec((GMLP_WIDTH, GMLP_WIDTH), lambda i: (0, 0))],
        out_specs=(pl.BlockSpec((tm, 2 * GMLP_WIDTH), lambda i: (i, 0)),
                   pl.BlockSpec((N_GROUPS, CHUNK, CHUNK), lambda i: (0, 0, 0)),
                   pl.BlockSpec((8, GMLP_WIDTH), lambda i: (0, 0)),
                   pl.BlockSpec((CHUNK, GMLP_WIDTH), lambda i: (0, 0))),
        scratch_shapes=[pltpu.VMEM((CHUNK, GMLP_WIDTH), F32)],
        name="gmlp_bwd", compiler_params=_cp(("arbitrary",), VMEM_LIMIT),
    )(ug, dsg, gain, w_s, wt_s, bias_full, ones)


def _gate_bwd(dcq, dck, zf):
    S = zf.shape[0]
    tm = _pick(S, (256,))
    n_i = S // tm
    triu = (lax.broadcasted_iota(jnp.int32, (tm, tm), 0) <= lax.broadcasted_iota(jnp.int32, (tm, tm), 1)).astype(BF16)

    def body(dcq_ref, dck_ref, zf_ref, tri_ref, dzf_ref, dbf_ref, carry_ref):
        i = pl.program_id(0)

        @pl.when(i == 0)
        def _():
            carry_ref[...] = jnp.zeros_like(carry_ref)
            dbf_ref[...] = jnp.zeros_like(dbf_ref)

        lane = lax.broadcasted_iota(jnp.int32, (tm, LANES), 1)
        dc = dcq_ref[0] + dck_ref[0]
        for p in range(1, N_PAIRS):
            dc = dc + (dcq_ref[p] + dck_ref[p])
        dlf = _dot3l(tri_ref[...], dc) + carry_ref[0:1, :]
        carry_ref[0:1, :] = dlf[0:1, :]
        dz = jnp.where(lane < N_HEADS, dlf * _sigmoid(-zf_ref[...]), 0.0)
        dzf_ref[...] = dz.astype(BF16)
        dbf_ref[0:1, :] += jnp.sum(dz, axis=0, keepdims=True)

    return pl.pallas_call(
        body,
        out_shape=(jax.ShapeDtypeStruct((S, LANES), BF16), jax.ShapeDtypeStruct((8, LANES), F32)),
        grid=(n_i,),
        in_specs=[pl.BlockSpec((N_PAIRS, tm, LANES), lambda i: (0, n_i - 1 - i, 0)),
                  pl.BlockSpec((N_PAIRS, tm, LANES), lambda i: (0, n_i - 1 - i, 0)),
                  pl.BlockSpec((tm, LANES), lambda i: (n_i - 1 - i, 0)),
                  pl.BlockSpec((tm, tm), lambda i: (0, 0))],
        out_specs=(pl.BlockSpec((tm, LANES), lambda i: (n_i - 1 - i, 0)), pl.BlockSpec((8, LANES), lambda i: (0, 0))),
        scratch_shapes=[pltpu.VMEM((8, LANES), F32)],
        name="gate_bwd", compiler_params=_cp(("arbitrary",), VMEM_LIMIT),
    )(dcq, dck, zf, triu)


def _out_proj_fwd(att_bf, sg, w_out_bf, x, g_ffn):
    S = x.shape[0]
    tm = _pick(S, (512, 256))

    def body(a_ref, s_ref, w_ref, x_ref, g_ref, h_ref, hn_ref):
        h = (x_ref[...] + jnp.dot(a_ref[...], w_ref[:ATT_WIDTH, :], preferred_element_type=F32)
             + jnp.dot(s_ref[...], w_ref[ATT_WIDTH:, :], preferred_element_type=F32))
        h_ref[...] = h
        r = lax.rsqrt(jnp.mean(h * h, axis=-1, keepdims=True) + EPS)
        hn_ref[...] = ((h * r) * g_ref[...]).astype(BF16)

    row = pl.BlockSpec((tm, D_MODEL), lambda i: (i, 0))
    half = pl.BlockSpec((tm, ATT_WIDTH), lambda i: (i, 0))
    return pl.pallas_call(
        body, out_shape=(jax.ShapeDtypeStruct((S, D_MODEL), F32), jax.ShapeDtypeStruct((S, D_MODEL), BF16)),
        grid=(S // tm,),
        in_specs=[half, half, pl.BlockSpec((D_MODEL, D_MODEL), lambda i: (0, 0)), row,
                  pl.BlockSpec((1, D_MODEL), lambda i: (0, 0))],
        out_specs=(row, row), name="out_proj", compiler_params=_cp(("parallel",), VMEM_LIMIT),
    )(att_bf, sg, w_out_bf, x, g_ffn)


def _out_proj_dw(att_bf, sg, dh1_bf):
    S = att_bf.shape[0]
    tk = _pick(S, (1024, 512))

    def body(a_ref, s_ref, d_ref, o_ref):
        k = pl.program_id(0)

        @pl.when(k == 0)
        def _():
            o_ref[...] = jnp.zeros_like(o_ref)

        d = d_ref[...]
        o_ref[:ATT_WIDTH, :] += lax.dot_general(a_ref[...], d, _TN, preferred_element_type=F32)
        o_ref[ATT_WIDTH:, :] += lax.dot_general(s_ref[...], d, _TN, preferred_element_type=F32)

    half = pl.BlockSpec((tk, ATT_WIDTH), lambda k: (k, 0))
    return pl.pallas_call(
        body, out_shape=jax.ShapeDtypeStruct((D_MODEL, D_MODEL), F32), grid=(S // tk,),
        in_specs=[half, half, pl.BlockSpec((tk, D_MODEL), lambda k: (k, 0))],
        out_specs=pl.BlockSpec((D_MODEL, D_MODEL), lambda k: (0, 0)),
        name="out_proj_dw", compiler_params=_cp(("arbitrary",), VMEM_LIMIT),
    )(att_bf, sg, dh1_bf)


_IN_PIECES = ((0, ATT_WIDTH), (ATT_WIDTH, ATT_WIDTH), (2 * ATT_WIDTH, ATT_WIDTH), (QKV, 2 * GMLP_WIDTH), (UG_END, LANES))


def _inproj_bwd_dx(pieces, w_pad, x, g_mix, dh1):
    S = x.shape[0]
    tm = _pick(S, (512, 256))

    def body(*refs):
        p_refs, (w_ref, x_ref, g_ref, r_ref, dx_ref, dg_ref) = refs[:5], refs[5:]
        i = pl.program_id(0)

        @pl.when(i == 0)
        def _():
            dg_ref[...] = jnp.zeros_like(dg_ref)

        dxn = None
        for p_ref, (c0, width) in zip(p_refs, _IN_PIECES):
            part = jnp.dot(p_ref[...], w_ref[c0:c0 + width, :], preferred_element_type=F32)
            dxn = part if dxn is None else dxn + part
        xf = x_ref[...]
        r = lax.rsqrt(jnp.mean(xf * xf, axis=-1, keepdims=True) + EPS)
        xhat = xf * r
        dg_ref[0:1, :] += jnp.sum(dxn * xhat, axis=0, keepdims=True)
        dhat = dxn * g_ref[...]
        dx_ref[...] = r_ref[...] + r * (dhat - xhat * jnp.mean(dhat * xhat, axis=-1, keepdims=True))

    row = pl.BlockSpec((tm, D_MODEL), lambda i: (i, 0))
    return pl.pallas_call(
        body, out_shape=(jax.ShapeDtypeStruct((S, D_MODEL), F32), jax.ShapeDtypeStruct((8, D_MODEL), F32)),
        grid=(S // tm,),
        in_specs=[pl.BlockSpec((tm, width), lambda i: (i, 0)) for _, width in _IN_PIECES]
        + [pl.BlockSpec((IN_PAD, D_MODEL), lambda i: (0, 0)), row, pl.BlockSpec((1, D_MODEL), lambda i: (0, 0)), row],
        out_specs=(row, pl.BlockSpec((8, D_MODEL), lambda i: (0, 0))),
        name="in_proj_dx", compiler_params=_cp(("arbitrary",), VMEM_LIMIT),
    )(*pieces, w_pad, x, g_mix, dh1)


def _inproj_bwd_dw(xn, pieces):
    S = xn.shape[0]
    tk = _pick(S, (1024, 512))

    def body(*refs):
        x_ref, p_refs, o_ref = refs[0], refs[1:6], refs[6]
        k = pl.program_id(0)

        @pl.when(k == 0)
        def _():
            o_ref[...] = jnp.zeros_like(o_ref)

        xb = x_ref[...]
        for p_ref, (c0, width) in zip(p_refs, _IN_PIECES):
            o_ref[:, c0:c0 + width] += lax.dot_general(xb, p_ref[...], _TN, preferred_element_type=F32)

    return pl.pallas_call(
        body, out_shape=jax.ShapeDtypeStruct((D_MODEL, IN_PAD), F32), grid=(S // tk,),
        in_specs=[pl.BlockSpec((tk, D_MODEL), lambda k: (k, 0))]
        + [pl.BlockSpec((tk, width), lambda k: (k, 0)) for _, width in _IN_PIECES],
        out_specs=pl.BlockSpec((D_MODEL, IN_PAD), lambda k: (0, 0)),
        name="in_proj_dw", compiler_params=_cp(("arbitrary",), VMEM_LIMIT),
    )(xn, *pieces)


def _adamw(w, m, v, parts, name):
    R, C = w.shape[-2:]
    tr = R
    for cand in (256, 128, 64, 32, 16, 8):
        if R % cand == 0 and R > cand:
            tr = cand
            break
    c1 = 1.0 / (1.0 - ADAM_B1 ** ADAM_STEP)
    c2 = 1.0 / (1.0 - ADAM_B2 ** ADAM_STEP)

    def body(w_ref, m_ref, v_ref, p_ref, g_ref, d_ref, nm_ref, nv_ref):
        g = p_ref[0].astype(F32)
        for j in range(1, N_DEV):
            g = g + p_ref[j].astype(F32)
        g_ref[...] = g
        nm = ADAM_B1 * m_ref[...] + (1.0 - ADAM_B1) * g
        nv = ADAM_B2 * v_ref[...] + (1.0 - ADAM_B2) * (g * g)
        nm_ref[...] = nm
        nv_ref[...] = nv
        d_ref[...] = -ADAM_LR * ((nm * c1) / (jnp.sqrt(nv * c2) + ADAM_EPS) + ADAM_WD * w_ref[...])

    if w.ndim == 3:
        spec = pl.BlockSpec((None, tr, C), lambda i: (0, i, 0))
    else:
        spec = pl.BlockSpec((tr, C), lambda i: (i, 0))
    shp = jax.ShapeDtypeStruct(w.shape, F32)
    return pl.pallas_call(
        body, out_shape=(shp, shp, shp, shp), grid=(R // tr,),
        in_specs=[spec, spec, spec, pl.BlockSpec((N_DEV, tr, C), lambda i: (0, i, 0))],
        out_specs=(spec, spec, spec, spec),
        name=name, compiler_params=_cp(("parallel",), VMEM_LIMIT),
    )(w, m, v, parts)


def _place():
    x, y, c = lax.axis_index("x"), lax.axis_index("y"), lax.axis_index("c")
    return x, y, c


def _all_gather(blocks, name):
    n = len(blocks)

    def body(*refs):
        ins, outs = refs[:n], refs[n:2 * n]
        send_sems, recv_sems, local_sems = refs[2 * n:]
        x, y, c = _place()
        me, sibling = (x, y, c), (x, y, 1 - c)
        chips = [(1 - x, y), (x, 1 - y), (1 - x, 1 - y)]
        sends = []
        for a in range(n):
            out = outs[a]

            def slot(px, py, pc, out=out):
                return out.at[4 * px + 2 * py + pc]

            def copy(k, block, to, src=None, a=a, slot=slot):
                return pltpu.make_async_remote_copy(
                    src_ref=slot(*block) if src is None else src, dst_ref=slot(*block),
                    send_sem=send_sems.at[a, k], recv_sem=recv_sems.at[a, k], device_id=to, device_id_type=MESH)

            mine = pltpu.make_async_copy(ins[a], slot(*me), local_sems.at[a])
            mine.start()
            first = [copy(0, me, sibling, src=ins[a])]
            first += [copy(1 + j, me, (*chip, c), src=ins[a]) for j, chip in enumerate(chips)]
            for cp in first:
                cp.start()
            sends.append((mine, first, copy))
        for a in range(n):
            mine, first, copy = sends[a]
            passed = [copy(4 + j, (*chip, c), sibling) for j, chip in enumerate(chips)]
            for j, chip in enumerate(chips):
                copy(1 + j, (*chip, c), me).wait_recv()
                passed[j].start()
            copy(0, sibling, me).wait_recv()
            for j, chip in enumerate(chips):
                copy(4 + j, (*chip, 1 - c), me).wait_recv()
            for cp in first + passed:
                cp.wait_send()
            mine.wait()

    any_spec = pl.BlockSpec(memory_space=pl.ANY)
    return pl.pallas_call(
        body, out_shape=tuple(jax.ShapeDtypeStruct((N_DEV,) + b.shape, b.dtype) for b in blocks),
        in_specs=[any_spec] * n, out_specs=tuple([any_spec] * n),
        scratch_shapes=[pltpu.SemaphoreType.DMA((n, 7)), pltpu.SemaphoreType.DMA((n, 7)), pltpu.SemaphoreType.DMA((n,))],
        name=name,
    )(*blocks)


_HBM = pl.BlockSpec(memory_space=pltpu.HBM)
_SEM = pl.BlockSpec(memory_space=pltpu.SEMAPHORE)
_EFFECT = pltpu.SideEffectType.DATAFLOW_SIDE_EFFECTING


def _peers(x, y, c):
    out = []
    for k in range(1, N_DEV):
        px, py, pc = x ^ ((k >> 2) & 1), y ^ ((k >> 1) & 1), c ^ (k & 1)
        out.append((k, (px, py, pc), 4 * px + 2 * py + pc))
    return out


def _xchg_copies(src_refs, land_refs, send_sems, recv_sems, scatter):
    x, y, c = _place()
    me = 4 * x + 2 * y + c
    copies = []
    for a, (src, land) in enumerate(zip(src_refs, land_refs)):
        for k, place, idx in _peers(x, y, c):
            j = a * (N_DEV - 1) + k - 1
            copies.append(pltpu.make_async_remote_copy(
                src_ref=src.at[idx] if scatter[a] else src, dst_ref=land.at[me],
                send_sem=send_sems[j], recv_sem=recv_sems[j], device_id=place, device_id_type=MESH))
    return copies


def _xchg_start(srcs, scatter, name):
    n = len(srcs)
    lands = [lax.empty((N_DEV,) + (s.shape[1:] if sc else s.shape), s.dtype) for s, sc in zip(srcs, scatter)]

    ns = n * (N_DEV - 1)

    def body(*refs):
        sems = refs[2 * n:2 * n + 2 * ns]
        for cp in _xchg_copies(refs[:n], refs[n:2 * n], sems[:ns], sems[ns:], scatter):
            cp.start()
        token = refs[-1]
        token[...] = jnp.zeros_like(token)

    both = list(srcs) + lands
    res = pl.pallas_call(
        body, name=name,
        out_shape=(*[pltpu.SemaphoreType.DMA(())] * (2 * ns),
                   *[pltpu.HBM(a.shape, a.dtype) for a in both], jax.ShapeDtypeStruct((8, LANES), F32)),
        in_specs=[_HBM] * (2 * n),
        out_specs=(*([_SEM] * (2 * ns)), *([_HBM] * (2 * n)), pl.BlockSpec(memory_space=pltpu.VMEM)),
        input_output_aliases={i: 2 * ns + i for i in range(2 * n)},
        compiler_params=pltpu.CompilerParams(has_side_effects=_EFFECT),
    )(*[pltpu.with_memory_space_constraint(a, pltpu.HBM) for a in both])
    return (tuple(res[:2 * ns]), tuple(res[2 * ns:2 * ns + 2 * n])), res[-1]


def _xchg_wait(handle, scatter, after, name):
    sems, thru = handle
    n = len(thru) // 2
    ns = len(sems) // 2

    def body(*refs):
        got = refs[2 * n:2 * n + 2 * ns]
        for cp in _xchg_copies(refs[:n], refs[n:2 * n], got[:ns], got[ns:], scatter):
            cp.wait_send()
            cp.wait_recv()

    outs = pl.pallas_call(
        body, name=name, out_shape=tuple(pltpu.HBM(a.shape, a.dtype) for a in thru),
        in_specs=[_HBM] * (2 * n) + [_SEM] * (2 * ns) + [pl.BlockSpec(memory_space=pl.ANY)],
        out_specs=tuple([_HBM] * (2 * n)), input_output_aliases={i: i for i in range(2 * n)},
        compiler_params=pltpu.CompilerParams(has_side_effects=_EFFECT),
    )(*thru, *sems, after)
    return outs[:n], outs[n:]


def _tie(a, token):
    return a if token is None else a + token[0, 0].astype(a.dtype)


def _rows128(a):
    flat = a.reshape(-1)
    rows = -(-flat.shape[0] // LANES)
    rows = -(-rows // 8) * 8
    return jnp.pad(flat, (0, rows * LANES - flat.shape[0])).reshape(rows, LANES)


def _local_step(x, target, norm_mix_g, w_in_t, b_forget, gmlp_norm_g, w_spatial, b_spatial, norm_ffn_g, conv_b,
                norm_final_g, rest_fn, send_fn, small_fn, token=None):
    f = D_FF
    g_mix = norm_mix_g.reshape(1, D_MODEL)
    w_pad = jnp.pad(w_in_t, ((0, IN_PAD - IN_COLS), (0, 0)))
    bf_pad = jnp.pad(b_forget.reshape(1, N_HEADS), ((0, 0), (0, LANES - N_HEADS)))
    xn, qa, ka, va, ug, zf = _inproj_fwd(x, _tie(g_mix, token), w_pad, bf_pad)
    bias_full = jnp.repeat(b_spatial.reshape(N_GROUPS, CHUNK).T, GROUP_DIM, axis=1)
    w_s = w_spatial.reshape(N_GROUPS, CHUNK, CHUNK)
    gain = gmlp_norm_g.reshape(1, GMLP_WIDTH)
    sg = _gmlp_fwd(ug, gain, w_s, bias_full)
    att, lse, att_bf = _attn_fwd(qa, ka, va)
    w_out_bf, w_up_bf, conv_w, w_down_bf = rest_fn(att_bf)
    g_ffn = norm_ffn_g.reshape(1, D_MODEL)
    h1, hn = _out_proj_fwd(att_bf, sg, w_out_bf, x, g_ffn)
    cw = jnp.pad(conv_w.reshape(3, 2, f).transpose(1, 0, 2), ((0, 0), (0, 5), (0, 0)))
    cb = conv_b.reshape(2, 1, f)
    hu, hc, act = _ffn_up_conv(hn, w_up_bf, cw, cb)
    loss_blk, dh2, dh2_bf, dg_final = _ffn_down_loss(act, w_down_bf, h1, norm_final_g.reshape(1, D_MODEL), target)
    dw_down = _mm(act, dh2_bf, mode="tn", out_dtype=F32, tm=1408, tn=1024, tk=2048, name="ffn_down_dw")
    dact = _mm(dh2_bf, w_down_bf, mode="nt", out_dtype=F32, tm=1024, tn=1408, tk=1024, outer="j", name="ffn_down_dx")
    dhu, dcw = _conv_gate_bwd(hc, hu, dact, _tie(cw, send_fn("w_down", dw_down)))
    dw_up = _mm(hn, dhu, mode="tn", out_dtype=F32, tm=1024, tn=1408, tk=2048, b_halves=True, outer="j", name="ffn_up_dw")
    dh1, dh1_bf, dg_ffn = _ffn_up_dx_rms(dhu, w_up_bf, h1, _tie(g_ffn, send_fn("w_up", dw_up)), dh2)
    dsg, qb, doa = _out_proj_dx_prep(dh1_bf, w_out_bf, att, lse, qa)
    dw_out = _out_proj_dw(att_bf, sg, dh1_bf)
    dq, dk, dv, dcq, dck = _attn_bwd(qb, ka, va, doa)
    wt_s = w_s.transpose(0, 2, 1)
    dug, dw_s, dgain, dbias = _gmlp_bwd(ug, dsg, _tie(gain, send_fn("w_out", dw_out)), w_s, wt_s, bias_full)
    dzf, dbf = _gate_bwd(dcq, dck, zf)
    grad_x, dg_mix = _inproj_bwd_dx((dq, dk, dv, dug, dzf), w_pad, x, g_mix, dh1)
    grads = dict(
        norm_mix_g=dg_mix[0:1, :],
        b_forget=dbf[0:1, :N_HEADS],
        gmlp_norm_g=dgain[0:1, :],
        w_spatial=dw_s,
        b_spatial=dbias[:, ::GROUP_DIM].T,
        norm_ffn_g=dg_ffn[0:1, :],
        conv_w=dcw[:, 0:3, :].transpose(1, 0, 2).reshape(3, 2 * f),
        conv_b=dcw[:, 3, :].reshape(1, 2 * f),
        norm_final_g=dg_final[0, :],
    )
    token = small_fn(loss_blk[0, 0], grads)
    dw_in = _inproj_bwd_dw(xn, (dq, dk, dv, dug, _tie(dzf, token)))
    return grad_x, send_fn("w_in", dw_in[:, :IN_COLS])


SMALL = ("norm_mix_g", "b_forget", "gmlp_norm_g", "w_spatial", "b_spatial", "norm_ffn_g", "conv_b", "norm_final_g")


def kernel(x, norm_mix_g, w_in, b_forget, gmlp_norm_g, w_spatial, b_spatial, w_out, norm_ffn_g, w_up, conv_w, conv_b, w_down, norm_final_g, loss_target, m_norm_mix_g, m_w_in, m_b_forget, m_gmlp_norm_g, m_w_spatial, m_b_spatial, m_w_out, m_norm_ffn_g, m_w_up, m_conv_w, m_conv_b, m_w_down, m_norm_final_g, v_norm_mix_g, v_w_in, v_b_forget, v_gmlp_norm_g, v_w_spatial, v_b_spatial, v_w_out, v_norm_ffn_g, v_w_up, v_conv_w, v_conv_b, v_w_down, v_norm_final_g):
    weights = dict(norm_mix_g=norm_mix_g, w_in=w_in, b_forget=b_forget, gmlp_norm_g=gmlp_norm_g, w_spatial=w_spatial,
                   b_spatial=b_spatial, w_out=w_out, norm_ffn_g=norm_ffn_g, w_up=w_up, conv_w=conv_w, conv_b=conv_b,
                   w_down=w_down, norm_final_g=norm_final_g)
    m_in = dict(norm_mix_g=m_norm_mix_g, w_in=m_w_in, b_forget=m_b_forget, gmlp_norm_g=m_gmlp_norm_g,
                w_spatial=m_w_spatial, b_spatial=m_b_spatial, w_out=m_w_out, norm_ffn_g=m_norm_ffn_g, w_up=m_w_up,
                conv_w=m_conv_w, conv_b=m_conv_b, w_down=m_w_down, norm_final_g=m_norm_final_g)
    v_in = dict(norm_mix_g=v_norm_mix_g, w_in=v_w_in, b_forget=v_b_forget, gmlp_norm_g=v_gmlp_norm_g,
                w_spatial=v_w_spatial, b_spatial=v_b_spatial, w_out=v_w_out, norm_ffn_g=v_norm_ffn_g, w_up=v_w_up,
                conv_w=v_conv_w, conv_b=v_conv_b, w_down=v_w_down, norm_final_g=v_norm_final_g)
    order = list(weights)
    me = 4 * lax.axis_index("x") + 2 * lax.axis_index("y") + lax.axis_index("c")
    n_in, n_up = w_in.shape[2], w_up.shape[2]
    r_out, r_down = w_out.shape[1], w_down.shape[1]

    def with_mine(landed, mine):
        return lax.dynamic_update_index_in_dim(landed, mine, me, 0)

    up_blk = w_up[0].T.astype(BF16)
    out_blk = w_out[0].astype(BF16)
    down_blk = w_down[0].astype(BF16)
    taps_blk = jnp.pad(conv_w[0], ((0, 5), (0, 0)))
    (in_all,) = _all_gather([w_in[0].T.astype(BF16)], "gather_w_in")
    in_all, rest_blocks = lax.optimization_barrier((in_all, [up_blk, out_blk, down_blk, taps_blk]))
    rest_handle, token = _xchg_start(rest_blocks, [False] * 4, "gather_rest_start")
    w_in_t = in_all.reshape(N_DEV * n_in, D_MODEL)

    def rest_fn(after):
        mine, landed = _xchg_wait(rest_handle, [False] * 4, after, "gather_rest_wait")
        up_all, out_all, down_all, taps_all = [with_mine(l, b) for l, b in zip(landed, mine)]
        return (out_all.reshape(N_DEV * r_out, D_MODEL), up_all.reshape(N_DEV * n_up, D_MODEL),
                taps_all[:, :3, :].transpose(1, 0, 2).reshape(3, N_DEV * n_up),
                down_all.reshape(N_DEV * r_down, D_MODEL))

    sent = {}

    def send_fn(name, grad):
        if name == "w_in":
            parts = grad.reshape(D_MODEL, N_DEV, -1).transpose(1, 0, 2).astype(BF16)
        elif name == "w_up":
            parts = grad.reshape(D_MODEL, N_DEV, -1).transpose(1, 0, 2)
        else:
            parts = grad.reshape(N_DEV, -1, D_MODEL)
        sent[name], tok = _xchg_start([parts], [True], "scatter_" + name + "_start")
        return tok

    small = {}

    def small_fn(loss_local, g):
        loss_rows = jnp.pad(loss_local.reshape(1, 1), ((0, 31), (0, LANES - 1)))
        packed = [_rows128(g[k]) for k in SMALL] + [loss_rows, _rows128(g["conv_w"])]
        small["sizes"] = [p.shape[0] for p in packed]
        small["handle"], tok = _xchg_start([jnp.concatenate(packed, axis=0)], [False], "gather_small_start")
        return tok

    grad_x, after = _local_step(
        x[0], loss_target[0], norm_mix_g, w_in_t, b_forget, gmlp_norm_g, w_spatial, b_spatial, norm_ffn_g, conv_b,
        norm_final_g, rest_fn, send_fn, small_fn, token)

    outs = {}

    def update_big(name, after):
        (parts,), (landed,) = _xchg_wait(sent[name], [True], after, "scatter_" + name + "_wait")
        got = with_mine(landed, lax.dynamic_index_in_dim(parts, me, 0, keepdims=False))
        outs[name] = tuple(_adamw(weights[name], m_in[name], v_in[name], got, "adamw_" + name))
        return outs[name][0]

    for name in ("w_down", "w_up", "w_out"):
        after = update_big(name, after)

    (mine,), (landed,) = _xchg_wait(small["handle"], [False], after, "gather_small_wait")
    small_all = with_mine(landed, mine)
    sizes = small["sizes"]
    n_small_rows = sum(sizes[:-2])

    def pack(src):
        return jnp.concatenate([_rows128(src[k]) for k in SMALL] + [jnp.zeros((sizes[-2], LANES), F32)], axis=0)

    n_adam_rows = n_small_rows + sizes[-2]
    sg_, sd_, sm_, sv_ = _adamw(pack(weights), pack(m_in), pack(v_in), small_all[:, :n_adam_rows, :], "adamw_small")
    loss = sg_[n_small_rows, 0]
    off = 0
    for k, rows in zip(SMALL, sizes[:-2]):
        shp = weights[k].shape
        cnt = math.prod(shp)
        outs[k] = tuple(a[off:off + rows].reshape(-1)[:cnt].reshape(shp) for a in (sg_, sd_, sm_, sv_))
        off += rows
    taps_parts = small_all[:, n_adam_rows:, :].reshape(N_DEV, -1)[:, :3 * N_DEV * n_up].reshape(N_DEV, 3, N_DEV * n_up)
    taps_mine = lax.dynamic_slice_in_dim(taps_parts, me * n_up, n_up, axis=2)
    taps_mine = jnp.pad(taps_mine, ((0, 0), (0, 5), (0, 0)))

    def pad8(a):
        return jnp.pad(a[0], ((0, 5), (0, 0)))

    res = _adamw(pad8(conv_w), pad8(m_conv_w), pad8(v_conv_w), taps_mine, "adamw_conv_w")
    outs["conv_w"] = tuple(a[:3][None] for a in res)
    update_big("w_in", sg_)

    return (loss, grad_x[None], *[outs[k][0] for k in order], *[outs[k][1] for k in order],
            *[outs[k][2] for k in order], *[outs[k][3] for k in order])
```

```python
import functools
import math

import jax
import jax.numpy as jnp
from jax import lax
from jax.experimental import pallas as pl
from jax.experimental.pallas import tpu as pltpu

F32 = jnp.float32
BF16 = jnp.bfloat16

N_DEV = 8
D_MODEL = 1024
ATT_WIDTH = 512
GMLP_WIDTH = 512
HEAD_DIM = 64
N_HEADS = 8
N_PAIRS = 4
N_GROUPS = 8
GROUP_DIM = 64
CHUNK = 128
D_FF = 2816
IN_COLS = 2568
IN_PAD = 2688
QKV = 1536
UG_END = 2560
EPS = 1e-6
LANES = 128

ADAM_LR = 0.001
ADAM_B1 = 0.9
ADAM_B2 = 0.999
ADAM_EPS = 1e-08
ADAM_WD = 0.01
ADAM_STEP = 10

ATT_TQ = 1024
ATT_TK = 1024
FFN_TM, FFN_TN = 256, 2816
CONV_TM, CONV_TN = 512, 1408
VMEM_LIMIT = 56 * 1024 * 1024
MESH = pl.DeviceIdType.MESH


def _cp(sem, vmem=None):
    return pltpu.CompilerParams(dimension_semantics=sem, vmem_limit_bytes=vmem)


def _pick(n, prefs):
    for p in prefs:
        if n % p == 0:
            return p
    return n


def _split3(x):
    hi = x.astype(BF16)
    r1 = x - hi.astype(F32)
    mid = r1.astype(BF16)
    lo = (r1 - mid.astype(F32)).astype(BF16)
    return hi, mid, lo


def _dot3(x, ones_bf):
    d = functools.partial(jnp.dot, preferred_element_type=F32)
    out = []
    for c in range(0, x.shape[1], 2 * LANES):
        blk = ones_bf[c:c + 2 * LANES, c:c + 2 * LANES]
        hi, mid, lo = _split3(x[:, c:c + 2 * LANES])
        out.append(d(hi, blk) + d(mid, blk) + d(lo, blk))
    return jnp.concatenate(out, axis=1)


def _dot3l(ones_bf, x):
    n = x.shape[1]
    y = jnp.dot(ones_bf, jnp.concatenate(_split3(x), axis=1), preferred_element_type=F32)
    return y[:, :n] + y[:, n:2 * n] + y[:, 2 * n:]


def _gelu(x):
    k = math.sqrt(2.0 / math.pi)
    t = jnp.tanh(k * (x + 0.044715 * (x * x * x)))
    return 0.5 * x * (1.0 + t)


def _gelu_grad(x):
    k = math.sqrt(2.0 / math.pi)
    x2 = x * x
    t = jnp.tanh(k * (x + 0.044715 * (x2 * x)))
    return 0.5 * (1.0 + t) + 0.5 * x * (1.0 - t * t) * (k * (1.0 + 3.0 * 0.044715 * x2))


def _sigmoid(x):
    return 1.0 / (1.0 + jnp.exp(-x))


def _mm(a, b, *, mode, out_dtype, tm, tn, tk, name, res=None, a_halves=False, b_halves=False,
        out_halves=False, outer="i"):
    if mode == "tn":
        K, M = a.shape[-2], a.shape[-1] * (2 if a_halves else 1)
    else:
        M, K = a.shape[-2], a.shape[-1] * (2 if a_halves else 1)
    if mode == "nt":
        N = b.shape[-2]
        assert b.shape[-1] == K
    else:
        N = b.shape[-1] * (2 if b_halves else 1)
    tm, tn, tk = min(tm, M), min(tn, N), min(tk, K)
    assert M % tm == 0 and N % tn == 0 and K % tk == 0, (name, M, N, K, tm, tn, tk)
    nm, nn, nk = M // tm, N // tn, K // tk

    def ij(g0, g1):
        return (g0, g1) if outer == "i" else (g1, g0)

    if mode == "nn":
        dims = (((1,), (0,)), ((), ()))
        if a_halves:
            nkh = nk // 2
            a_spec = pl.BlockSpec((None, tm, tk), lambda g0, g1, k: (k // nkh, ij(g0, g1)[0], k % nkh))
        else:
            a_spec = pl.BlockSpec((tm, tk), lambda g0, g1, k: (ij(g0, g1)[0], k))
        b_spec = pl.BlockSpec((tk, tn), lambda g0, g1, k: (k, ij(g0, g1)[1]))
    elif mode == "nt":
        dims = (((1,), (1,)), ((), ()))
        if a_halves:
            nkh = nk // 2
            a_spec = pl.BlockSpec((None, tm, tk), lambda g0, g1, k: (k // nkh, ij(g0, g1)[0], k % nkh))
        else:
            a_spec = pl.BlockSpec((tm, tk), lambda g0, g1, k: (ij(g0, g1)[0], k))
        b_spec = pl.BlockSpec((tn, tk), lambda g0, g1, k: (ij(g0, g1)[1], k))
    else:
        dims = (((0,), (0,)), ((), ()))
        if a_halves:
            nmh = nm // 2
            a_spec = pl.BlockSpec((None, tk, tm), lambda g0, g1, k: (ij(g0, g1)[0] // nmh, k, ij(g0, g1)[0] % nmh))
        else:
            a_spec = pl.BlockSpec((tk, tm), lambda g0, g1, k: (k, ij(g0, g1)[0]))
        if b_halves:
            nnh = nn // 2
            b_spec = pl.BlockSpec((None, tk, tn), lambda g0, g1, k: (ij(g0, g1)[1] // nnh, k, ij(g0, g1)[1] % nnh))
        else:
            b_spec = pl.BlockSpec((tk, tn), lambda g0, g1, k: (k, ij(g0, g1)[1]))
    if out_halves:
        nnh = nn // 2
        o_spec = pl.BlockSpec((None, tm, tn), lambda g0, g1, k: (ij(g0, g1)[1] // nnh, ij(g0, g1)[0], ij(g0, g1)[1] % nnh))
        o_shape = jax.ShapeDtypeStruct((2, M, N // 2), out_dtype)
    else:
        o_spec = pl.BlockSpec((tm, tn), lambda g0, g1, k: ij(g0, g1))
        o_shape = jax.ShapeDtypeStruct((M, N), out_dtype)
    in_specs = [a_spec, b_spec]
    args = [a, b]
    if res is not None:
        in_specs.append(pl.BlockSpec((tm, tn), lambda g0, g1, k: ij(g0, g1)))
        args.append(res)

    def body(*refs):
        if res is not None:
            a_ref, b_ref, r_ref, o_ref = refs[:4]
        else:
            a_ref, b_ref, o_ref = refs[:3]
            r_ref = None
        part = lax.dot_general(a_ref[...], b_ref[...], dims, preferred_element_type=F32)
        if nk == 1:
            if r_ref is not None:
                part = part + r_ref[...]
            o_ref[...] = part.astype(out_dtype)
            return
        acc_ref = refs[-1]
        k = pl.program_id(2)

        @pl.when(k == 0)
        def _():
            acc_ref[...] = part

        @pl.when(k > 0)
        def _():
            acc_ref[...] += part

        @pl.when(k == nk - 1)
        def _():
            tot = acc_ref[...]
            if r_ref is not None:
                tot = tot + r_ref[...]
            o_ref[...] = tot.astype(out_dtype)

    grid = (nm, nn, nk) if outer == "i" else (nn, nm, nk)
    scratch = [] if nk == 1 else [pltpu.VMEM((tm, tn), F32)]
    return pl.pallas_call(
        body, out_shape=o_shape, grid=grid, in_specs=in_specs, out_specs=o_spec, scratch_shapes=scratch,
        name=name, compiler_params=_cp(("parallel", "parallel", "arbitrary"), VMEM_LIMIT),
    )(*args)


def _aug(lane, terms):
    out = 0.0
    for j, t in enumerate(terms):
        out = jnp.where(lane == HEAD_DIM + j, t, out)
    return out


def _split3f(x):
    hi, mid, lo = _split3(x)
    return [hi.astype(F32), mid.astype(F32), lo.astype(F32)]


def _inproj_fwd(x, g_mix, w_pad, bf_pad):
    S = x.shape[0]
    tm = _pick(S, (512, 256))
    tri = (lax.broadcasted_iota(jnp.int32, (tm, tm), 0) >= lax.broadcasted_iota(jnp.int32, (tm, tm), 1)).astype(BF16)

    def body(x_ref, g_ref, w_ref, bf_ref, tri_ref, put_ref, one_ref, xn_ref, qa_ref, ka_ref, va_ref, ug_ref, zf_ref,
             carry_ref):
        i = pl.program_id(0)

        @pl.when(i == 0)
        def _():
            carry_ref[...] = jnp.zeros_like(carry_ref)

        xf = x_ref[...]
        r = lax.rsqrt(jnp.mean(xf * xf, axis=-1, keepdims=True) + EPS)
        xn = ((xf * r) * g_ref[...]).astype(BF16)
        xn_ref[...] = xn
        proj = lax.dot_general(xn, w_ref[...], _NT, preferred_element_type=F32)
        ug_ref[...] = proj[:, QKV:UG_END]
        zf = proj[:, UG_END:] + bf_ref[...]
        zf_ref[...] = zf
        lf = jnp.minimum(zf, 0.0) - jnp.log(1.0 + jnp.exp(-jnp.abs(zf)))
        c = _dot3l(tri_ref[...], lf) + carry_ref[0:1, :]
        carry_ref[0:1, :] = c[tm - 1:tm, :]
        c3 = jnp.concatenate(_split3(c), axis=1)
        aug_q = jnp.dot(c3, put_ref[0], preferred_element_type=F32) + one_ref[0:1, :]
        aug_k = jnp.dot(c3, put_ref[1], preferred_element_type=F32) + one_ref[1:2, :]
        lane = lax.broadcasted_iota(jnp.int32, (tm, LANES), 1)
        for h in range(N_HEADS):
            p, odd = h // 2, h % 2

            def head(base, scale=None, p=p, odd=odd):
                blk = proj[:, base + p * LANES:base + (p + 1) * LANES]
                if scale is not None:
                    blk = blk * scale
                return pltpu.roll(blk, HEAD_DIM, 1) if odd else blk

            cols = slice(h * LANES, (h + 1) * LANES)
            qa_ref[:, cols] = jnp.where(lane < HEAD_DIM, head(0, HEAD_DIM ** -0.5), aug_q[:, cols]).astype(BF16)
            ka_ref[:, cols] = jnp.where(lane < HEAD_DIM, head(ATT_WIDTH), aug_k[:, cols]).astype(BF16)
            va_ref[:, cols] = jnp.where(lane < HEAD_DIM, head(2 * ATT_WIDTH), one_ref[2:3, cols]).astype(BF16)

    wide = N_HEADS * LANES
    src = lax.broadcasted_iota(jnp.int32, (3 * LANES, wide), 0)
    col = lax.broadcasted_iota(jnp.int32, (3 * LANES, wide), 1)
    hd, term = src % LANES, src // LANES
    to_q = (col == hd * LANES + HEAD_DIM + term) & (hd < N_HEADS)
    to_k = (col == hd * LANES + HEAD_DIM + 3 + term) & (hd < N_HEADS)
    put = jnp.stack([to_q.astype(BF16), -to_k.astype(BF16)])
    off = lax.broadcasted_iota(jnp.int32, (8, wide), 1) % LANES - HEAD_DIM
    row = lax.broadcasted_iota(jnp.int32, (8, wide), 0)
    q_one = (off >= 3) & (off < 6)
    k_one = ((off >= 0) & (off < 3)) | ((off >= 6) & (off < 9))
    v_one = (off >= 0) & (off < 3)
    ones = jnp.where(row == 0, q_one, jnp.where(row == 1, k_one, (row == 2) & v_one)).astype(F32)
    return pl.pallas_call(
        body,
        out_shape=(jax.ShapeDtypeStruct((S, D_MODEL), BF16), jax.ShapeDtypeStruct((S, wide), BF16),
                   jax.ShapeDtypeStruct((S, wide), BF16), jax.ShapeDtypeStruct((S, wide), BF16),
                   jax.ShapeDtypeStruct((S, 2 * GMLP_WIDTH), F32), jax.ShapeDtypeStruct((S, LANES), F32)),
        grid=(S // tm,),
        in_specs=[pl.BlockSpec((tm, D_MODEL), lambda i: (i, 0)), pl.BlockSpec((1, D_MODEL), lambda i: (0, 0)),
                  pl.BlockSpec((IN_PAD, D_MODEL), lambda i: (0, 0)), pl.BlockSpec((1, LANES), lambda i: (0, 0)),
                  pl.BlockSpec((tm, tm), lambda i: (0, 0)), pl.BlockSpec((2, 3 * LANES, wide), lambda i: (0, 0, 0)),
                  pl.BlockSpec((8, wide), lambda i: (0, 0))],
        out_specs=(pl.BlockSpec((tm, D_MODEL), lambda i: (i, 0)), pl.BlockSpec((tm, wide), lambda i: (i, 0)),
                   pl.BlockSpec((tm, wide), lambda i: (i, 0)), pl.BlockSpec((tm, wide), lambda i: (i, 0)),
                   pl.BlockSpec((tm, 2 * GMLP_WIDTH), lambda i: (i, 0)), pl.BlockSpec((tm, LANES), lambda i: (i, 0))),
        scratch_shapes=[pltpu.VMEM((8, LANES), F32)],
        name="inproj_fwd", compiler_params=_cp(("arbitrary",), VMEM_LIMIT),
    )(x, g_mix, w_pad, bf_pad, tri, put, ones)


def _group_ones():
    r = lax.broadcasted_iota(jnp.int32, (GMLP_WIDTH, GMLP_WIDTH), 0) // GROUP_DIM
    c = lax.broadcasted_iota(jnp.int32, (GMLP_WIDTH, GMLP_WIDTH), 1) // GROUP_DIM
    return (r == c).astype(BF16)


def _gmlp_mixed(vn_bf, w_ref, bias, n_chunks):
    lane = lax.broadcasted_iota(jnp.int32, (CHUNK, LANES), 1)
    row = lax.broadcasted_iota(jnp.int32, (CHUNK, CHUNK), 0)
    col = lax.broadcasted_iota(jnp.int32, (CHUNK, CHUNK), 1)
    ws = [jnp.where(row >= col, w_ref[g], 0.0).astype(BF16) for g in range(N_GROUPS)]
    rows = []
    for ci in range(n_chunks):
        cols = []
        for pp in range(N_GROUPS // 2):
            v = vn_bf[ci * CHUNK:(ci + 1) * CHUNK, pp * LANES:(pp + 1) * LANES]
            v_lo = jnp.where(lane < GROUP_DIM, v, jnp.zeros_like(v))
            v_hi = jnp.where(lane >= GROUP_DIM, v, jnp.zeros_like(v))
            m = (jnp.dot(ws[2 * pp], v_lo, preferred_element_type=F32)
                 + jnp.dot(ws[2 * pp + 1], v_hi, preferred_element_type=F32))
            cols.append(m + bias[:, pp * LANES:(pp + 1) * LANES])
        rows.append(jnp.concatenate(cols, axis=1))
    return jnp.concatenate(rows, axis=0)


def _gmlp_fwd(ug, gain, w_s, bias_full):
    S = ug.shape[0]
    tm = _pick(S, (512, 256, 128))
    ones = _group_ones()

    def body(ug_ref, gain_ref, w_ref, bias_ref, ones_ref, sg_ref):
        u = _gelu(ug_ref[:, :GMLP_WIDTH])
        vr = _gelu(ug_ref[:, GMLP_WIDTH:])
        ms = _dot3(vr * vr, ones_ref[...]) * (1.0 / GROUP_DIM)
        vn = ((vr * lax.rsqrt(ms + EPS)) * gain_ref[...]).astype(BF16)
        mixed = _gmlp_mixed(vn, w_ref, bias_ref[...], tm // CHUNK)
        sg_ref[...] = (u * mixed).astype(BF16)

    return pl.pallas_call(
        body, out_shape=jax.ShapeDtypeStruct((S, GMLP_WIDTH), BF16), grid=(S // tm,),
        in_specs=[pl.BlockSpec((tm, 2 * GMLP_WIDTH), lambda i: (i, 0)), pl.BlockSpec((1, GMLP_WIDTH), lambda i: (0, 0)),
                  pl.BlockSpec((N_GROUPS, CHUNK, CHUNK), lambda i: (0, 0, 0)),
                  pl.BlockSpec((CHUNK, GMLP_WIDTH), lambda i: (0, 0)),
                  pl.BlockSpec((GMLP_WIDTH, GMLP_WIDTH), lambda i: (0, 0))],
        out_specs=pl.BlockSpec((tm, GMLP_WIDTH), lambda i: (i, 0)),
        name="gmlp_fwd", compiler_params=_cp(("parallel",), VMEM_LIMIT),
    )(ug, gain, w_s, bias_full, ones)


_NT = (((1,), (1,)), ((), ()))
_TN = (((0,), (0,)), ((), ()))


def _attn_fwd(qa, ka, va):
    S = qa.shape[0]
    tq = _pick(S, (ATT_TQ, 256))
    tk = min(ATT_TK, tq)
    nq = S // tq
    assert tq == tk, "the diagonal block is handled as one tq x tq tile"
    per_q = 1

    def body(q_ref, k_ref, v_ref, o_ref, lse_ref, ob_ref):
        qi = pl.program_id(1)
        lane = lax.broadcasted_iota(jnp.int32, (tq, LANES), 1)
        qs = [q_ref[:, :LANES], q_ref[:, LANES:]]

        def update(q, ks, k_len, h, m, acc, first_row):
            cols = slice(h * LANES, (h + 1) * LANES)
            s = lax.dot_general(q, k_ref[pl.ds(ks, k_len), cols], _NT, preferred_element_type=F32)
            if first_row is not None:
                rid = lax.broadcasted_iota(jnp.int32, s.shape, 0) + first_row
                s = jnp.where(rid >= lax.broadcasted_iota(jnp.int32, s.shape, 1), s, -jnp.inf)
            m_new = jnp.maximum(m, jnp.max(s, axis=-1, keepdims=True))
            p = jnp.exp(s - m_new).astype(BF16)
            acc = jnp.exp(m - m_new) * acc + jnp.dot(p, v_ref[pl.ds(ks, k_len), cols], preferred_element_type=F32)
            return m_new, acc

        def step(kb, carry):
            ks = pl.multiple_of(kb * tk, tk)
            return tuple(update(qs[h], ks, tk, h, *carry[h], None) for h in range(2))

        one = (jnp.full((tq, 1), -jnp.inf, F32), jnp.zeros((tq, LANES), F32))
        carry = lax.fori_loop(0, qi * per_q, step, (one, one))
        outs, lses = [], []
        strip = tq // 2
        diag = pl.multiple_of(qi * tq, tq)
        for h in range(2):
            ms, accs = [], []
            for r in range(2):
                rows = slice(r * strip, (r + 1) * strip)
                m, acc = update(qs[h][rows], diag, (r + 1) * strip, h, carry[h][0][rows], carry[h][1][rows], r * strip)
                ms.append(m)
                accs.append(acc)
            m, acc = jnp.concatenate(ms, axis=0), jnp.concatenate(accs, axis=0)
            l = acc[:, HEAD_DIM:HEAD_DIM + 1]
            outs.append(acc / l)
            lses.append(m + jnp.log(l))
        o = jnp.where(lane < HEAD_DIM, outs[0], pltpu.roll(outs[1], HEAD_DIM, 1))
        o_ref[...] = o
        ob_ref[...] = o.astype(BF16)
        lse_ref[...] = jnp.where(lane < HEAD_DIM, lses[0], lses[1])

    return pl.pallas_call(
        body,
        out_shape=(jax.ShapeDtypeStruct((S, ATT_WIDTH), F32), jax.ShapeDtypeStruct((S, ATT_WIDTH), F32),
                   jax.ShapeDtypeStruct((S, ATT_WIDTH), BF16)),
        grid=(N_PAIRS, nq),
        in_specs=[pl.BlockSpec((tq, 2 * LANES), lambda p, i: (i, p)),
                  pl.BlockSpec((S, 2 * LANES), lambda p, i: (0, p)),
                  pl.BlockSpec((S, 2 * LANES), lambda p, i: (0, p))],
        out_specs=(pl.BlockSpec((tq, LANES), lambda p, i: (i, p)), pl.BlockSpec((tq, LANES), lambda p, i: (i, p)),
                   pl.BlockSpec((tq, LANES), lambda p, i: (i, p))),
        name="attn_fwd", compiler_params=_cp(("parallel", "parallel"), VMEM_LIMIT),
    )(qa, ka, va)


def _shift_rows(x, prev, n):
    rid = lax.broadcasted_iota(jnp.int32, x.shape, 0)
    y = pltpu.roll(x, n, 0)
    if n == 1:
        return jnp.where(rid == 0, prev[7:8, :], y)
    return jnp.where(rid == 0, prev[6:7, :], jnp.where(rid == 1, prev[7:8, :], y))


def _shift_rows_up(x, nxt, n):
    rows = x.shape[0]
    rid = lax.broadcasted_iota(jnp.int32, x.shape, 0)
    y = pltpu.roll(x, rows - n, 0)
    if n == 1:
        return jnp.where(rid == rows - 1, nxt[0:1, :], y)
    return jnp.where(rid == rows - 2, nxt[0:1, :], jnp.where(rid == rows - 1, nxt[1:2, :], y))


def _conv3(cur, prev, w, b):
    return (w[0:1, :] * _shift_rows(cur, prev, 2) + w[1:2, :] * _shift_rows(cur, prev, 1)
            + w[2:3, :] * cur + b)


def _ffn_up_conv(hn, w_up_bf, cw, cb):
    S = hn.shape[0]
    F = D_FF
    tm = _pick(S, (FFN_TM, 256))
    tn = _pick(F, (FFN_TN, 256, 128))
    nj = F // tn

    def body(hn_ref, wa_ref, wg_ref, cw_ref, cb_ref, hu_ref, hc_ref, act_ref, tail_ref):
        i = pl.program_id(1)

        @pl.when(i == 0)
        def _():
            tail_ref[...] = jnp.zeros_like(tail_ref)

        hn_v = hn_ref[...]
        halves = []
        for h, w_ref in enumerate((wa_ref, wg_ref)):
            hu = lax.dot_general(hn_v, w_ref[...], _NT, preferred_element_type=F32)
            hu_ref[h] = hu.astype(BF16)
            hc = _conv3(hu, tail_ref[h], cw_ref[h], cb_ref[h])
            hc_ref[h] = hc
            halves.append(hc)
            tail_ref[h] = hu[tm - 8:, :]
        a, g = halves
        act_ref[...] = (g * _sigmoid(g) * a).astype(BF16)

    both = pl.BlockSpec((2, tm, tn), lambda j, i: (0, i, j))
    return pl.pallas_call(
        body, out_shape=(jax.ShapeDtypeStruct((2, S, F), BF16), jax.ShapeDtypeStruct((2, S, F), F32),
                         jax.ShapeDtypeStruct((S, F), BF16)),
        grid=(nj, S // tm),
        in_specs=[pl.BlockSpec((tm, D_MODEL), lambda j, i: (i, 0)),
                  pl.BlockSpec((tn, D_MODEL), lambda j, i: (j, 0)),
                  pl.BlockSpec((tn, D_MODEL), lambda j, i: (nj + j, 0)),
                  pl.BlockSpec((2, 8, tn), lambda j, i: (0, 0, j)),
                  pl.BlockSpec((2, 1, tn), lambda j, i: (0, 0, j))],
        out_specs=(both, both, pl.BlockSpec((tm, tn), lambda j, i: (i, j))),
        scratch_shapes=[pltpu.VMEM((2, 8, tn), F32)],
        name="ffn_up_conv", compiler_params=_cp(("parallel", "arbitrary"), VMEM_LIMIT),
    )(hn, w_up_bf, w_up_bf, cw, cb)


def _ffn_down_loss(act, w_down_bf, h1, g_final, target):
    S = h1.shape[0]
    tm = _pick(S, (512, 256))

    def body(a_ref, w_ref, h1_ref, g_ref, t_ref, loss_ref, dh_ref, dhb_ref, dg_ref):
        i = pl.program_id(0)

        @pl.when(i == 0)
        def _():
            loss_ref[...] = jnp.zeros_like(loss_ref)
            dg_ref[...] = jnp.zeros_like(dg_ref)

        hf = h1_ref[...] + jnp.dot(a_ref[...], w_ref[...], preferred_element_type=F32)
        g = g_ref[...]
        r = lax.rsqrt(jnp.mean(hf * hf, axis=-1, keepdims=True) + EPS)
        hhat = hf * r
        err = hhat * g - t_ref[...]
        loss_ref[...] += 0.5 * jnp.sum(jnp.mean(err * err, axis=-1, keepdims=True))
        dy = err * (1.0 / D_MODEL)
        dg_ref[0:1, :] += jnp.sum(dy * hhat, axis=0, keepdims=True)
        dhat = dy * g
        dh = r * (dhat - hhat * jnp.mean(dhat * hhat, axis=-1, keepdims=True))
        dh_ref[...] = dh
        dhb_ref[...] = dh.astype(BF16)

    row = pl.BlockSpec((tm, D_MODEL), lambda i: (i, 0))
    return pl.pallas_call(
        body,
        out_shape=(jax.ShapeDtypeStruct((8, LANES), F32), jax.ShapeDtypeStruct((S, D_MODEL), F32),
                   jax.ShapeDtypeStruct((S, D_MODEL), BF16), jax.ShapeDtypeStruct((8, D_MODEL), F32)),
        grid=(S // tm,),
        in_specs=[pl.BlockSpec((tm, D_FF), lambda i: (i, 0)), pl.BlockSpec((D_FF, D_MODEL), lambda i: (0, 0)), row,
                  pl.BlockSpec((1, D_MODEL), lambda i: (0, 0)), row],
        out_specs=(pl.BlockSpec((8, LANES), lambda i: (0, 0)), row, row, pl.BlockSpec((8, D_MODEL), lambda i: (0, 0))),
        name="ffn_down_loss", compiler_params=_cp(("arbitrary",), VMEM_LIMIT),
    )(act, w_down_bf, h1, g_final, target)


def _ffn_up_dx_rms(dhu, w_up_bf, h1, g_ffn, dh2):
    _, S, F = dhu.shape
    tm = _pick(S, (512, 256))

    def body(a_ref, b_ref, h_ref, g_ref, r_ref, dh_ref, dhb_ref, dg_ref):
        i = pl.program_id(0)

        @pl.when(i == 0)
        def _():
            dg_ref[...] = jnp.zeros_like(dg_ref)

        dyv = (jnp.dot(a_ref[0], b_ref[:F, :], preferred_element_type=F32)
               + jnp.dot(a_ref[1], b_ref[F:, :], preferred_element_type=F32))
        hf = h_ref[...]
        r = lax.rsqrt(jnp.mean(hf * hf, axis=-1, keepdims=True) + EPS)
        hhat = hf * r
        dg_ref[0:1, :] += jnp.sum(dyv * hhat, axis=0, keepdims=True)
        dhat = dyv * g_ref[...]
        dh = r_ref[...] + r * (dhat - hhat * jnp.mean(dhat * hhat, axis=-1, keepdims=True))
        dh_ref[...] = dh
        dhb_ref[...] = dh.astype(BF16)

    row = pl.BlockSpec((tm, D_MODEL), lambda i: (i, 0))
    return pl.pallas_call(
        body,
        out_shape=(jax.ShapeDtypeStruct((S, D_MODEL), F32), jax.ShapeDtypeStruct((S, D_MODEL), BF16),
                   jax.ShapeDtypeStruct((8, D_MODEL), F32)),
        grid=(S // tm,),
        in_specs=[pl.BlockSpec((2, tm, F), lambda i: (0, i, 0)), pl.BlockSpec((2 * F, D_MODEL), lambda i: (0, 0)),
                  row, pl.BlockSpec((1, D_MODEL), lambda i: (0, 0)), row],
        out_specs=(row, row, pl.BlockSpec((8, D_MODEL), lambda i: (0, 0))),
        name="ffn_up_dx_rms", compiler_params=_cp(("arbitrary",), VMEM_LIMIT),
    )(dhu, w_up_bf, h1, g_ffn, dh2)


def _conv_gate_bwd(hc, hu, dact, cw):
    _, S, F = hu.shape
    tm = _pick(S, (CONV_TM, 128))
    tn = _pick(F, (CONV_TN, 256, 128))
    r8 = tm // 8
    n_i = S // tm
    last8 = S // 8 - 1

    def body(hc_ref, hcn_ref, hu_ref, da_ref, dan_ref, w_ref, dhu_ref, dcw_ref):
        i = pl.program_id(1)

        @pl.when(i == 0)
        def _():
            dcw_ref[...] = jnp.zeros_like(dcw_ref)

        rid8 = lax.broadcasted_iota(jnp.int32, (8, tn), 0)

        def gate_grads(a, g, d):
            sg = _sigmoid(g)
            return d * (g * sg), d * a * (sg * (1.0 + g * (1.0 - sg)))

        dhc = gate_grads(hc_ref[0], hc_ref[1], da_ref[...])
        dhc_n = gate_grads(hcn_ref[0], hcn_ref[1], dan_ref[...])
        for h in range(2):
            w = w_ref[h]
            d = dhc[h]
            dn = jnp.where(i < n_i - 1, dhc_n[h], 0.0)
            u1 = _shift_rows_up(d, dn, 1)
            u2 = _shift_rows_up(d, dn, 2)
            dhu_ref[h] = (w[2:3, :] * d + w[1:2, :] * u1 + w[0:1, :] * u2).astype(BF16)
            x = hu_ref[h].astype(F32)
            t0, t1, t2, t3 = [jnp.sum(t, axis=0, keepdims=True) for t in (u2 * x, u1 * x, d * x, d)]
            dcw_ref[h] += jnp.where(rid8 == 0, t0, jnp.where(rid8 == 1, t1, jnp.where(rid8 == 2, t2, jnp.where(rid8 == 3, t3, 0.0))))

    cur = pl.BlockSpec((2, tm, tn), lambda j, i: (0, i, j))
    return pl.pallas_call(
        body,
        out_shape=(jax.ShapeDtypeStruct((2, S, F), BF16), jax.ShapeDtypeStruct((2, 8, F), F32)),
        grid=(F // tn, n_i),
        in_specs=[cur, pl.BlockSpec((2, 8, tn), lambda j, i: (0, jnp.minimum((i + 1) * r8, last8), j)), cur,
                  pl.BlockSpec((tm, tn), lambda j, i: (i, j)),
                  pl.BlockSpec((8, tn), lambda j, i: (jnp.minimum((i + 1) * r8, last8), j)),
                  pl.BlockSpec((2, 8, tn), lambda j, i: (0, 0, j))],
        out_specs=(cur, pl.BlockSpec((2, 8, tn), lambda j, i: (0, 0, j))),
        name="conv_gate_bwd", compiler_params=_cp(("parallel", "arbitrary"), VMEM_LIMIT),
    )(hc, hc, hu, dact, dact, cw)


def _out_proj_dx_prep(dh1_bf, w_out_bf, att, lse, qa):
    S = att.shape[0]
    tm = _pick(S, (256,))

    def body(dh_ref, w_ref, o_ref, lse_ref, q_ref, dsg_ref, qb_ref, doa_ref):
        lane = lax.broadcasted_iota(jnp.int32, (tm, LANES), 1)
        dh = dh_ref[...]
        dsg_ref[...] = lax.dot_general(dh, w_ref[ATT_WIDTH:, :], _NT, preferred_element_type=F32)
        datt = lax.dot_general(dh, w_ref[:ATT_WIDTH, :], _NT, preferred_element_type=F32)
        for p in range(N_PAIRS):
            pc = slice(p * LANES, (p + 1) * LANES)
            do = datt[:, pc]
            prod = o_ref[:, pc] * do
            for hh in range(2):
                sel = (lane >= HEAD_DIM) if hh else (lane < HEAD_DIM)
                delta = jnp.sum(jnp.where(sel, prod, 0.0), axis=-1, keepdims=True)
                dod = pltpu.roll(do, HEAD_DIM, 1) if hh else do
                cols = slice((2 * p + hh) * LANES, (2 * p + hh + 1) * LANES)
                doa_ref[:, cols] = jnp.where(lane < HEAD_DIM, dod, _aug(lane, _split3f(-delta))).astype(BF16)
                lcol = p * LANES + hh * HEAD_DIM
                l3 = _split3f(-lse_ref[:, lcol:lcol + 1])
                augl = jnp.where(lane == HEAD_DIM + 6, l3[0], jnp.where(lane == HEAD_DIM + 7, l3[1], l3[2])).astype(BF16)
                qb_ref[:, cols] = jnp.where((lane >= HEAD_DIM + 6) & (lane < HEAD_DIM + 9), augl, q_ref[:, cols])

    half = pl.BlockSpec((tm, ATT_WIDTH), lambda i: (i, 0))
    wide = pl.BlockSpec((tm, N_HEADS * LANES), lambda i: (i, 0))
    return pl.pallas_call(
        body,
        out_shape=(jax.ShapeDtypeStruct((S, GMLP_WIDTH), F32), jax.ShapeDtypeStruct(qa.shape, BF16),
                   jax.ShapeDtypeStruct(qa.shape, BF16)),
        grid=(S // tm,),
        in_specs=[pl.BlockSpec((tm, D_MODEL), lambda i: (i, 0)), pl.BlockSpec((D_MODEL, D_MODEL), lambda i: (0, 0)),
                  half, half, wide],
        out_specs=(half, wide, wide),
        name="out_proj_dx_prep", compiler_params=_cp(("parallel",), VMEM_LIMIT),
    )(dh1_bf, w_out_bf, att, lse, qa)


def _attn_bwd(qb, ka, va, doa):
    S = qb.shape[0]
    tk = _pick(S, (512, 256))
    tq = tk
    nq = S // tq

    def pair(a, scale=None):
        lane = lax.broadcasted_iota(jnp.int32, (a.shape[0], LANES), 1)
        out = jnp.where(lane < HEAD_DIM, a[:, :LANES], pltpu.roll(a[:, LANES:], HEAD_DIM, 1))
        return out if scale is None else out * scale

    def head_lanes(a, col, sign, first):
        lane = lax.broadcasted_iota(jnp.int32, (a.shape[0], LANES), 1)
        return jnp.where(lane == first, sign * a[:, col:col + 1],
                         jnp.where(lane == first + 1, sign * a[:, LANES + col:LANES + col + 1], 0.0))

    def body(q_ref, do_ref, k_ref, v_ref, dqc_ref, dkc_ref, dvc_ref, dcq_ref, dck_ref, dq_ref, dka_ref, dva_ref):
        kb = pl.program_id(1)

        @pl.when(kb == 0)
        def _():
            dq_ref[...] = jnp.zeros_like(dq_ref)

        dka_ref[...] = jnp.zeros_like(dka_ref)
        dva_ref[...] = jnp.zeros_like(dva_ref)
        def sub_tile(qs, q_len, k_off, k_len, masked):
            keys = slice(k_off, k_off + k_len)
            for h in range(2):
                cols = slice(h * LANES, (h + 1) * LANES)
                qblk = q_ref[pl.ds(qs, q_len), cols]
                doblk = do_ref[pl.ds(qs, q_len), cols]
                kh = k_ref[keys, cols]
                p = jnp.exp(lax.dot_general(kh, qblk, _NT, preferred_element_type=F32))
                if masked:
                    p = jnp.where(lax.broadcasted_iota(jnp.int32, p.shape, 1) >= lax.broadcasted_iota(jnp.int32, p.shape, 0),
                                  p, 0.0)
                ds = (p * lax.dot_general(v_ref[keys, cols], doblk, _NT, preferred_element_type=F32)).astype(BF16)
                dva_ref[keys, cols] += jnp.dot(p.astype(BF16), doblk, preferred_element_type=F32)
                dka_ref[keys, cols] += jnp.dot(ds, qblk, preferred_element_type=F32)
                dq_ref[pl.ds(qs, q_len), cols] += lax.dot_general(ds, kh, _TN, preferred_element_type=F32)

        half = tk // 2
        sub_tile(pl.multiple_of(kb * tq, tq), tq, 0, half, True)
        sub_tile(pl.multiple_of(kb * tq + half, half), half, half, half, True)

        def step(qi, carry):
            sub_tile(pl.multiple_of(qi * tq, tq), tq, 0, tk, False)
            return carry

        lax.fori_loop(kb + 1, nq, step, 0)
        dka = dka_ref[...]
        dkc_ref[...] = pair(dka).astype(BF16)
        dvc_ref[...] = pair(dva_ref[...]).astype(BF16)
        first = 2 * pl.program_id(0)
        dck_ref[...] = head_lanes(dka, HEAD_DIM + 3, -1.0, first)

        @pl.when(kb == nq - 1)
        def _():
            dqa = dq_ref[...]
            dqc_ref[...] = pair(dqa, HEAD_DIM ** -0.5).astype(BF16)
            dcq_ref[...] = head_lanes(dqa, HEAD_DIM, 1.0, first)

    wide = 2 * LANES
    half = jax.ShapeDtypeStruct((S, ATT_WIDTH), BF16)
    slabs = jax.ShapeDtypeStruct((N_PAIRS, S, LANES), F32)
    return pl.pallas_call(
        body,
        out_shape=(half, half, half, slabs, slabs),
        grid=(N_PAIRS, nq),
        in_specs=[pl.BlockSpec((S, wide), lambda p, j: (0, p)), pl.BlockSpec((S, wide), lambda p, j: (0, p)),
                  pl.BlockSpec((tk, wide), lambda p, j: (j, p)), pl.BlockSpec((tk, wide), lambda p, j: (j, p))],
        out_specs=(pl.BlockSpec((S, LANES), lambda p, j: (0, p)), pl.BlockSpec((tk, LANES), lambda p, j: (j, p)),
                   pl.BlockSpec((tk, LANES), lambda p, j: (j, p)), pl.BlockSpec((None, S, LANES), lambda p, j: (p, 0, 0)),
                   pl.BlockSpec((None, tk, LANES), lambda p, j: (p, j, 0))),
        scratch_shapes=[pltpu.VMEM((S, wide), F32), pltpu.VMEM((tk, wide), F32), pltpu.VMEM((tk, wide), F32)],
        name="attn_bwd", compiler_params=_cp(("parallel", "arbitrary"), VMEM_LIMIT),
    )(qb, doa, ka, va)


def _gmlp_bwd(ug, dsg, gain, w_s, wt_s, bias_full):
    S = ug.shape[0]
    tm = _pick(S, (512, 256, 128))
    n_chunks = tm // CHUNK
    n_i = S // tm
    ones = _group_ones()
    nt = (((1,), (1,)), ((), ()))

    def body(ug_ref, dsg_ref, gain_ref, w_ref, wt_ref, bias_ref, ones_ref, dug_ref, dw_ref, dgain_ref, dbias_ref,
             dbacc_ref):
        i = pl.program_id(0)

        @pl.when(i == 0)
        def _():
            dw_ref[...] = jnp.zeros_like(dw_ref)
            dgain_ref[...] = jnp.zeros_like(dgain_ref)
            dbacc_ref[...] = jnp.zeros_like(dbacc_ref)

        ones_m = ones_ref[...]
        pu = ug_ref[:, :GMLP_WIDTH]
        pg = ug_ref[:, GMLP_WIDTH:]
        u = _gelu(pu)
        vr = _gelu(pg)
        ms = _dot3(vr * vr, ones_m) * (1.0 / GROUP_DIM)
        rinv = lax.rsqrt(ms + EPS)
        vhat = vr * rinv
        gain_v = gain_ref[...]
        vn = (vhat * gain_v).astype(BF16)
        mixed = _gmlp_mixed(vn, w_ref, bias_ref[...], n_chunks)
        dsg_v = dsg_ref[...]
        du = dsg_v * mixed
        dmixed = dsg_v * u
        dm_bf = dmixed.astype(BF16)
        lane = lax.broadcasted_iota(jnp.int32, (CHUNK, LANES), 1)
        row = lax.broadcasted_iota(jnp.int32, (CHUNK, CHUNK), 0)
        col = lax.broadcasted_iota(jnp.int32, (CHUNK, CHUNK), 1)
        wts = [jnp.where(col >= row, wt_ref[g], 0.0).astype(BF16) for g in range(N_GROUPS)]
        dvn_rows = []
        dbsum = jnp.zeros((CHUNK, GMLP_WIDTH), F32)
        for ci in range(n_chunks):
            rs = slice(ci * CHUNK, (ci + 1) * CHUNK)
            dbsum = dbsum + dmixed[rs, :]
            cols = []
            for pp in range(N_GROUPS // 2):
                cs = slice(pp * LANES, (pp + 1) * LANES)
                dm = dm_bf[rs, cs]
                dm_lo = jnp.where(lane < GROUP_DIM, dm, jnp.zeros_like(dm))
                dm_hi = jnp.where(lane >= GROUP_DIM, dm, jnp.zeros_like(dm))
                vb = vn[rs, cs]
                dw_ref[2 * pp] += lax.dot_general(dm_lo, vb, nt, preferred_element_type=F32)
                dw_ref[2 * pp + 1] += lax.dot_general(dm_hi, vb, nt, preferred_element_type=F32)
                cols.append(jnp.dot(wts[2 * pp], dm_lo, preferred_element_type=F32)
                            + jnp.dot(wts[2 * pp + 1], dm_hi, preferred_element_type=F32))
            dvn_rows.append(jnp.concatenate(cols, axis=1))
        dvn = jnp.concatenate(dvn_rows, axis=0)
        dbacc_ref[...] += dbsum
        dgain_ref[0:1, :] += jnp.sum(dvn * vhat, axis=0, keepdims=True)
        dvhat = dvn * gain_v
        gm = _dot3(dvhat * vhat, ones_m) * (1.0 / GROUP_DIM)
        dvr = rinv * (dvhat - vhat * gm)
        dug_ref[:, :GMLP_WIDTH] = (du * _gelu_grad(pu)).astype(BF16)
        dug_ref[:, GMLP_WIDTH:] = (dvr * _gelu_grad(pg)).astype(BF16)

        @pl.when(i == n_i - 1)
        def _():
            for g in range(N_GROUPS):
                dw_ref[g] = jnp.where(row >= col, dw_ref[g], 0.0)
            dbias_ref[...] = _dot3(dbacc_ref[...], ones_m)

    return pl.pallas_call(
        body,
        out_shape=(jax.ShapeDtypeStruct((S, 2 * GMLP_WIDTH), BF16), jax.ShapeDtypeStruct((N_GROUPS, CHUNK, CHUNK), F32),
                   jax.ShapeDtypeStruct((8, GMLP_WIDTH), F32), jax.ShapeDtypeStruct((CHUNK, GMLP_WIDTH), F32)),
        grid=(n_i,),
        in_specs=[pl.BlockSpec((tm, 2 * GMLP_WIDTH), lambda i: (i, 0)), pl.BlockSpec((tm, GMLP_WIDTH), lambda i: (i, 0)),
                  pl.BlockSpec((1, GMLP_WIDTH), lambda i: (0, 0)),
                  pl.BlockSpec((N_GROUPS, CHUNK, CHUNK), lambda i: (0, 0, 0)),
                  pl.BlockSpec((N_GROUPS, CHUNK, CHUNK), lambda i: (0, 0, 0)),
                  pl.BlockSpec((CHUNK, GMLP_WIDTH), lambda i: (0, 0)),
                  pl.BlockSpec((GMLP_WIDTH, GMLP_WIDTH), lambda i: (0, 0))],
        out_specs=(pl.BlockSpec((tm, 2 * GMLP_WIDTH), lambda i: (i, 0)),
                   pl.BlockSpec((N_GROUPS, CHUNK, CHUNK), lambda i: (0, 0, 0)),
                   pl.BlockSpec((8, GMLP_WIDTH), lambda i: (0, 0)),
                   pl.BlockSpec((CHUNK, GMLP_WIDTH), lambda i: (0, 0))),
        scratch_shapes=[pltpu.VMEM((CHUNK, GMLP_WIDTH), F32)],
        name="gmlp_bwd", compiler_params=_cp(("arbitrary",), VMEM_LIMIT),
    )(ug, dsg, gain, w_s, wt_s, bias_full, ones)


def _gate_bwd(dcq, dck, zf):
    S = zf.shape[0]
    tm = _pick(S, (256,))
    n_i = S // tm
    triu = (lax.broadcasted_iota(jnp.int32, (tm, tm), 0) <= lax.broadcasted_iota(jnp.int32, (tm, tm), 1)).astype(BF16)

    def body(dcq_ref, dck_ref, zf_ref, tri_ref, dzf_ref, dbf_ref, carry_ref):
        i = pl.program_id(0)

        @pl.when(i == 0)
        def _():
            carry_ref[...] = jnp.zeros_like(carry_ref)
            dbf_ref[...] = jnp.zeros_like(dbf_ref)

        lane = lax.broadcasted_iota(jnp.int32, (tm, LANES), 1)
        dc = dcq_ref[0] + dck_ref[0]
        for p in range(1, N_PAIRS):
            dc = dc + (dcq_ref[p] + dck_ref[p])
        dlf = _dot3l(tri_ref[...], dc) + carry_ref[0:1, :]
        carry_ref[0:1, :] = dlf[0:1, :]
        dz = jnp.where(lane < N_HEADS, dlf * _sigmoid(-zf_ref[...]), 0.0)
        dzf_ref[...] = dz.astype(BF16)
        dbf_ref[0:1, :] += jnp.sum(dz, axis=0, keepdims=True)

    return pl.pallas_call(
        body,
        out_shape=(jax.ShapeDtypeStruct((S, LANES), BF16), jax.ShapeDtypeStruct((8, LANES), F32)),
        grid=(n_i,),
        in_specs=[pl.BlockSpec((N_PAIRS, tm, LANES), lambda i: (0, n_i - 1 - i, 0)),
                  pl.BlockSpec((N_PAIRS, tm, LANES), lambda i: (0, n_i - 1 - i, 0)),
                  pl.BlockSpec((tm, LANES), lambda i: (n_i - 1 - i, 0)),
                  pl.BlockSpec((tm, tm), lambda i: (0, 0))],
        out_specs=(pl.BlockSpec((tm, LANES), lambda i: (n_i - 1 - i, 0)), pl.BlockSpec((8, LANES), lambda i: (0, 0))),
        scratch_shapes=[pltpu.VMEM((8, LANES), F32)],
        name="gate_bwd", compiler_params=_cp(("arbitrary",), VMEM_LIMIT),
    )(dcq, dck, zf, triu)


def _out_proj_fwd(att_bf, sg, w_out_bf, x, g_ffn):
    S = x.shape[0]
    tm = _pick(S, (512, 256))

    def body(a_ref, s_ref, w_ref, x_ref, g_ref, h_ref, hn_ref):
        h = (x_ref[...] + jnp.dot(a_ref[...], w_ref[:ATT_WIDTH, :], preferred_element_type=F32)
             + jnp.dot(s_ref[...], w_ref[ATT_WIDTH:, :], preferred_element_type=F32))
        h_ref[...] = h
        r = lax.rsqrt(jnp.mean(h * h, axis=-1, keepdims=True) + EPS)
        hn_ref[...] = ((h * r) * g_ref[...]).astype(BF16)

    row = pl.BlockSpec((tm, D_MODEL), lambda i: (i, 0))
    half = pl.BlockSpec((tm, ATT_WIDTH), lambda i: (i, 0))
    return pl.pallas_call(
        body, out_shape=(jax.ShapeDtypeStruct((S, D_MODEL), F32), jax.ShapeDtypeStruct((S, D_MODEL), BF16)),
        grid=(S // tm,),
        in_specs=[half, half, pl.BlockSpec((D_MODEL, D_MODEL), lambda i: (0, 0)), row,
                  pl.BlockSpec((1, D_MODEL), lambda i: (0, 0))],
        out_specs=(row, row), name="out_proj", compiler_params=_cp(("parallel",), VMEM_LIMIT),
    )(att_bf, sg, w_out_bf, x, g_ffn)


def _out_proj_dw(att_bf, sg, dh1_bf):
    S = att_bf.shape[0]
    tk = _pick(S, (1024, 512))

    def body(a_ref, s_ref, d_ref, o_ref):
        k = pl.program_id(0)

        @pl.when(k == 0)
        def _():
            o_ref[...] = jnp.zeros_like(o_ref)

        d = d_ref[...]
        o_ref[:ATT_WIDTH, :] += lax.dot_general(a_ref[...], d, _TN, preferred_element_type=F32)
        o_ref[ATT_WIDTH:, :] += lax.dot_general(s_ref[...], d, _TN, preferred_element_type=F32)

    half = pl.BlockSpec((tk, ATT_WIDTH), lambda k: (k, 0))
    return pl.pallas_call(
        body, out_shape=jax.ShapeDtypeStruct((D_MODEL, D_MODEL), F32), grid=(S // tk,),
        in_specs=[half, half, pl.BlockSpec((tk, D_MODEL), lambda k: (k, 0))],
        out_specs=pl.BlockSpec((D_MODEL, D_MODEL), lambda k: (0, 0)),
        name="out_proj_dw", compiler_params=_cp(("arbitrary",), VMEM_LIMIT),
    )(att_bf, sg, dh1_bf)


_IN_PIECES = ((0, ATT_WIDTH), (ATT_WIDTH, ATT_WIDTH), (2 * ATT_WIDTH, ATT_WIDTH), (QKV, 2 * GMLP_WIDTH), (UG_END, LANES))


def _inproj_bwd_dx(pieces, w_pad, x, g_mix, dh1):
    S = x.shape[0]
    tm = _pick(S, (512, 256))

    def body(*refs):
        p_refs, (w_ref, x_ref, g_ref, r_ref, dx_ref, dg_ref) = refs[:5], refs[5:]
        i = pl.program_id(0)

        @pl.when(i == 0)
        def _():
            dg_ref[...] = jnp.zeros_like(dg_ref)

        dxn = None
        for p_ref, (c0, width) in zip(p_refs, _IN_PIECES):
            part = jnp.dot(p_ref[...], w_ref[c0:c0 + width, :], preferred_element_type=F32)
            dxn = part if dxn is None else dxn + part
        xf = x_ref[...]
        r = lax.rsqrt(jnp.mean(xf * xf, axis=-1, keepdims=True) + EPS)
        xhat = xf * r
        dg_ref[0:1, :] += jnp.sum(dxn * xhat, axis=0, keepdims=True)
        dhat = dxn * g_ref[...]
        dx_ref[...] = r_ref[...] + r * (dhat - xhat * jnp.mean(dhat * xhat, axis=-1, keepdims=True))

    row = pl.BlockSpec((tm, D_MODEL), lambda i: (i, 0))
    return pl.pallas_call(
        body, out_shape=(jax.ShapeDtypeStruct((S, D_MODEL), F32), jax.ShapeDtypeStruct((8, D_MODEL), F32)),
        grid=(S // tm,),
        in_specs=[pl.BlockSpec((tm, width), lambda i: (i, 0)) for _, width in _IN_PIECES]
        + [pl.BlockSpec((IN_PAD, D_MODEL), lambda i: (0, 0)), row, pl.BlockSpec((1, D_MODEL), lambda i: (0, 0)), row],
        out_specs=(row, pl.BlockSpec((8, D_MODEL), lambda i: (0, 0))),
        name="in_proj_dx", compiler_params=_cp(("arbitrary",), VMEM_LIMIT),
    )(*pieces, w_pad, x, g_mix, dh1)


def _inproj_bwd_dw(xn, pieces):
    S = xn.shape[0]
    tk = _pick(S, (1024, 512))

    def body(*refs):
        x_ref, p_refs, o_ref = refs[0], refs[1:6], refs[6]
        k = pl.program_id(0)

        @pl.when(k == 0)
        def _():
            o_ref[...] = jnp.zeros_like(o_ref)

        xb = x_ref[...]
        for p_ref, (c0, width) in zip(p_refs, _IN_PIECES):
            o_ref[:, c0:c0 + width] += lax.dot_general(xb, p_ref[...], _TN, preferred_element_type=F32)

    return pl.pallas_call(
        body, out_shape=jax.ShapeDtypeStruct((D_MODEL, IN_PAD), F32), grid=(S // tk,),
        in_specs=[pl.BlockSpec((tk, D_MODEL), lambda k: (k, 0))]
        + [pl.BlockSpec((tk, width), lambda k: (k, 0)) for _, width in _IN_PIECES],
        out_specs=pl.BlockSpec((D_MODEL, IN_PAD), lambda k: (0, 0)),
        name="in_proj_dw", compiler_params=_cp(("arbitrary",), VMEM_LIMIT),
    )(xn, *pieces)


def _adamw(w, m, v, parts, name):
    R, C = w.shape[-2:]
    tr = R
    for cand in (256, 128, 64, 32, 16, 8):
        if R % cand == 0 and R > cand:
            tr = cand
            break
    c1 = 1.0 / (1.0 - ADAM_B1 ** ADAM_STEP)
    c2 = 1.0 / (1.0 - ADAM_B2 ** ADAM_STEP)

    def body(w_ref, m_ref, v_ref, p_ref, g_ref, d_ref, nm_ref, nv_ref):
        g = p_ref[0].astype(F32)
        for j in range(1, N_DEV):
            g = g + p_ref[j].astype(F32)
        g_ref[...] = g
        nm = ADAM_B1 * m_ref[...] + (1.0 - ADAM_B1) * g
        nv = ADAM_B2 * v_ref[...] + (1.0 - ADAM_B2) * (g * g)
        nm_ref[...] = nm
        nv_ref[...] = nv
        d_ref[...] = -ADAM_LR * ((nm * c1) / (jnp.sqrt(nv * c2) + ADAM_EPS) + ADAM_WD * w_ref[...])

    if w.ndim == 3:
        spec = pl.BlockSpec((None, tr, C), lambda i: (0, i, 0))
    else:
        spec = pl.BlockSpec((tr, C), lambda i: (i, 0))
    shp = jax.ShapeDtypeStruct(w.shape, F32)
    return pl.pallas_call(
        body, out_shape=(shp, shp, shp, shp), grid=(R // tr,),
        in_specs=[spec, spec, spec, pl.BlockSpec((N_DEV, tr, C), lambda i: (0, i, 0))],
        out_specs=(spec, spec, spec, spec),
        name=name, compiler_params=_cp(("parallel",), VMEM_LIMIT),
    )(w, m, v, parts)


def _place():
    x, y, c = lax.axis_index("x"), lax.axis_index("y"), lax.axis_index("c")
    return x, y, c


def _all_gather(blocks, name):
    n = len(blocks)

    def body(*refs):
        ins, outs = refs[:n], refs[n:2 * n]
        send_sems, recv_sems, local_sems = refs[2 * n:]
        x, y, c = _place()
        me, sibling = (x, y, c), (x, y, 1 - c)
        chips = [(1 - x, y), (x, 1 - y), (1 - x, 1 - y)]
        sends = []
        for a in range(n):
            out = outs[a]

            def slot(px, py, pc, out=out):
                return out.at[4 * px + 2 * py + pc]

            def copy(k, block, to, src=None, a=a, slot=slot):
                return pltpu.make_async_remote_copy(
                    src_ref=slot(*block) if src is None else src, dst_ref=slot(*block),
                    send_sem=send_sems.at[a, k], recv_sem=recv_sems.at[a, k], device_id=to, device_id_type=MESH)

            mine = pltpu.make_async_copy(ins[a], slot(*me), local_sems.at[a])
            mine.start()
            first = [copy(0, me, sibling, src=ins[a])]
            first += [copy(1 + j, me, (*chip, c), src=ins[a]) for j, chip in enumerate(chips)]
            for cp in first:
                cp.start()
            sends.append((mine, first, copy))
        for a in range(n):
            mine, first, copy = sends[a]
            passed = [copy(4 + j, (*chip, c), sibling) for j, chip in enumerate(chips)]
            for j, chip in enumerate(chips):
                copy(1 + j, (*chip, c), me).wait_recv()
                passed[j].start()
            copy(0, sibling, me).wait_recv()
            for j, chip in enumerate(chips):
                copy(4 + j, (*chip, 1 - c), me).wait_recv()
            for cp in first + passed:
                cp.wait_send()
            mine.wait()

    any_spec = pl.BlockSpec(memory_space=pl.ANY)
    return pl.pallas_call(
        body, out_shape=tuple(jax.ShapeDtypeStruct((N_DEV,) + b.shape, b.dtype) for b in blocks),
        in_specs=[any_spec] * n, out_specs=tuple([any_spec] * n),
        scratch_shapes=[pltpu.SemaphoreType.DMA((n, 7)), pltpu.SemaphoreType.DMA((n, 7)), pltpu.SemaphoreType.DMA((n,))],
        name=name,
    )(*blocks)


_HBM = pl.BlockSpec(memory_space=pltpu.HBM)
_SEM = pl.BlockSpec(memory_space=pltpu.SEMAPHORE)
_EFFECT = pltpu.SideEffectType.DATAFLOW_SIDE_EFFECTING


def _peers(x, y, c):
    out = []
    for k in range(1, N_DEV):
        px, py, pc = x ^ ((k >> 2) & 1), y ^ ((k >> 1) & 1), c ^ (k & 1)
        out.append((k, (px, py, pc), 4 * px + 2 * py + pc))
    return out


def _xchg_copies(src_refs, land_refs, send_sems, recv_sems, scatter):
    x, y, c = _place()
    me = 4 * x + 2 * y + c
    copies = []
    for a, (src, land) in enumerate(zip(src_refs, land_refs)):
        for k, place, idx in _peers(x, y, c):
            j = a * (N_DEV - 1) + k - 1
            copies.append(pltpu.make_async_remote_copy(
                src_ref=src.at[idx] if scatter[a] else src, dst_ref=land.at[me],
                send_sem=send_sems[j], recv_sem=recv_sems[j], device_id=place, device_id_type=MESH))
    return copies


def _xchg_start(srcs, scatter, name):
    n = len(srcs)
    lands = [lax.empty((N_DEV,) + (s.shape[1:] if sc else s.shape), s.dtype) for s, sc in zip(srcs, scatter)]

    ns = n * (N_DEV - 1)

    def body(*refs):
        sems = refs[2 * n:2 * n + 2 * ns]
        for cp in _xchg_copies(refs[:n], refs[n:2 * n], sems[:ns], sems[ns:], scatter):
            cp.start()
        token = refs[-1]
        token[...] = jnp.zeros_like(token)

    both = list(srcs) + lands
    res = pl.pallas_call(
        body, name=name,
        out_shape=(*[pltpu.SemaphoreType.DMA(())] * (2 * ns),
                   *[pltpu.HBM(a.shape, a.dtype) for a in both], jax.ShapeDtypeStruct((8, LANES), F32)),
        in_specs=[_HBM] * (2 * n),
        out_specs=(*([_SEM] * (2 * ns)), *([_HBM] * (2 * n)), pl.BlockSpec(memory_space=pltpu.VMEM)),
        input_output_aliases={i: 2 * ns + i for i in range(2 * n)},
        compiler_params=pltpu.CompilerParams(has_side_effects=_EFFECT),
    )(*[pltpu.with_memory_space_constraint(a, pltpu.HBM) for a in both])
    return (tuple(res[:2 * ns]), tuple(res[2 * ns:2 * ns + 2 * n])), res[-1]


def _xchg_wait(handle, scatter, after, name):
    sems, thru = handle
    n = len(thru) // 2
    ns = len(sems) // 2

    def body(*refs):
        got = refs[2 * n:2 * n + 2 * ns]
        for cp in _xchg_copies(refs[:n], refs[n:2 * n], got[:ns], got[ns:], scatter):
            cp.wait_send()
            cp.wait_recv()

    outs = pl.pallas_call(
        body, name=name, out_shape=tuple(pltpu.HBM(a.shape, a.dtype) for a in thru),
        in_specs=[_HBM] * (2 * n) + [_SEM] * (2 * ns) + [pl.BlockSpec(memory_space=pl.ANY)],
        out_specs=tuple([_HBM] * (2 * n)), input_output_aliases={i: i for i in range(2 * n)},
        compiler_params=pltpu.CompilerParams(has_side_effects=_EFFECT),
    )(*thru, *sems, after)
    return outs[:n], outs[n:]


def _tie(a, token):
    return a if token is None else a + token[0, 0].astype(a.dtype)


def _rows128(a):
    flat = a.reshape(-1)
    rows = -(-flat.shape[0] // LANES)
    rows = -(-rows // 8) * 8
    return jnp.pad(flat, (0, rows * LANES - flat.shape[0])).reshape(rows, LANES)


def _local_step(x, target, norm_mix_g, w_in_t, b_forget, gmlp_norm_g, w_spatial, b_spatial, norm_ffn_g, conv_b,
                norm_final_g, rest_fn, send_fn, small_fn, token=None):
    f = D_FF
    g_mix = norm_mix_g.reshape(1, D_MODEL)
    w_pad = jnp.pad(w_in_t, ((0, IN_PAD - IN_COLS), (0, 0)))
    bf_pad = jnp.pad(b_forget.reshape(1, N_HEADS), ((0, 0), (0, LANES - N_HEADS)))
    xn, qa, ka, va, ug, zf = _inproj_fwd(x, _tie(g_mix, token), w_pad, bf_pad)
    bias_full = jnp.repeat(b_spatial.reshape(N_GROUPS, CHUNK).T, GROUP_DIM, axis=1)
    w_s = w_spatial.reshape(N_GROUPS, CHUNK, CHUNK)
    gain = gmlp_norm_g.reshape(1, GMLP_WIDTH)
    sg = _gmlp_fwd(ug, gain, w_s, bias_full)
    att, lse, att_bf = _attn_fwd(qa, ka, va)
    w_out_bf, w_up_bf, conv_w, w_down_bf = rest_fn(att_bf)
    g_ffn = norm_ffn_g.reshape(1, D_MODEL)
    h1, hn = _out_proj_fwd(att_bf, sg, w_out_bf, x, g_ffn)
    cw = jnp.pad(conv_w.reshape(3, 2, f).transpose(1, 0, 2), ((0, 0), (0, 5), (0, 0)))
    cb = conv_b.reshape(2, 1, f)
    hu, hc, act = _ffn_up_conv(hn, w_up_bf, cw, cb)
    loss_blk, dh2, dh2_bf, dg_final = _ffn_down_loss(act, w_down_bf, h1, norm_final_g.reshape(1, D_MODEL), target)
    dw_down = _mm(act, dh2_bf, mode="tn", out_dtype=F32, tm=1408, tn=1024, tk=2048, name="ffn_down_dw")
    dact = _mm(dh2_bf, w_down_bf, mode="nt", out_dtype=F32, tm=1024, tn=1408, tk=1024, outer="j", name="ffn_down_dx")
    dhu, dcw = _conv_gate_bwd(hc, hu, dact, _tie(cw, send_fn("w_down", dw_down)))
    dw_up = _mm(hn, dhu, mode="tn", out_dtype=F32, tm=1024, tn=1408, tk=2048, b_halves=True, outer="j", name="ffn_up_dw")
    dh1, dh1_bf, dg_ffn = _ffn_up_dx_rms(dhu, w_up_bf, h1, _tie(g_ffn, send_fn("w_up", dw_up)), dh2)
    dsg, qb, doa = _out_proj_dx_prep(dh1_bf, w_out_bf, att, lse, qa)
    dw_out = _out_proj_dw(att_bf, sg, dh1_bf)
    dq, dk, dv, dcq, dck = _attn_bwd(qb, ka, va, doa)
    wt_s = w_s.transpose(0, 2, 1)
    dug, dw_s, dgain, dbias = _gmlp_bwd(ug, dsg, _tie(gain, send_fn("w_out", dw_out)), w_s, wt_s, bias_full)
    dzf, dbf = _gate_bwd(dcq, dck, zf)
    grad_x, dg_mix = _inproj_bwd_dx((dq, dk, dv, dug, dzf), w_pad, x, g_mix, dh1)
    grads = dict(
        norm_mix_g=dg_mix[0:1, :],
        b_forget=dbf[0:1, :N_HEADS],
        gmlp_norm_g=dgain[0:1, :],
        w_spatial=dw_s,
        b_spatial=dbias[:, ::GROUP_DIM].T,
        norm_ffn_g=dg_ffn[0:1, :],
        conv_w=dcw[:, 0:3, :].transpose(1, 0, 2).reshape(3, 2 * f),
        conv_b=dcw[:, 3, :].reshape(1, 2 * f),
        norm_final_g=dg_final[0, :],
    )
    token = small_fn(loss_blk[0, 0], grads)
    dw_in = _inproj_bwd_dw(xn, (dq, dk, dv, dug, _tie(dzf, token)))
    return grad_x, send_fn("w_in", dw_in[:, :IN_COLS])


SMALL = ("norm_mix_g", "b_forget", "gmlp_norm_g", "w_spatial", "b_spatial", "norm_ffn_g", "conv_b", "norm_final_g")


def kernel(x, norm_mix_g, w_in, b_forget, gmlp_norm_g, w_spatial, b_spatial, w_out, norm_ffn_g, w_up, conv_w, conv_b, w_down, norm_final_g, loss_target, m_norm_mix_g, m_w_in, m_b_forget, m_gmlp_norm_g, m_w_spatial, m_b_spatial, m_w_out, m_norm_ffn_g, m_w_up, m_conv_w, m_conv_b, m_w_down, m_norm_final_g, v_norm_mix_g, v_w_in, v_b_forget, v_gmlp_norm_g, v_w_spatial, v_b_spatial, v_w_out, v_norm_ffn_g, v_w_up, v_conv_w, v_conv_b, v_w_down, v_norm_final_g):
    weights = dict(norm_mix_g=norm_mix_g, w_in=w_in, b_forget=b_forget, gmlp_norm_g=gmlp_norm_g, w_spatial=w_spatial,
                   b_spatial=b_spatial, w_out=w_out, norm_ffn_g=norm_ffn_g, w_up=w_up, conv_w=conv_w, conv_b=conv_b,
                   w_down=w_down, norm_final_g=norm_final_g)
    m_in = dict(norm_mix_g=m_norm_mix_g, w_in=m_w_in, b_forget=m_b_forget, gmlp_norm_g=m_gmlp_norm_g,
                w_spatial=m_w_spatial, b_spatial=m_b_spatial, w_out=m_w_out, norm_ffn_g=m_norm_ffn_g, w_up=m_w_up,
                conv_w=m_conv_w, conv_b=m_conv_b, w_down=m_w_down, norm_final_g=m_norm_final_g)
    v_in = dict(norm_mix_g=v_norm_mix_g, w_in=v_w_in, b_forget=v_b_forget, gmlp_norm_g=v_gmlp_norm_g,
                w_spatial=v_w_spatial, b_spatial=v_b_spatial, w_out=v_w_out, norm_ffn_g=v_norm_ffn_g, w_up=v_w_up,
                conv_w=v_conv_w, conv_b=v_conv_b, w_down=v_w_down, norm_final_g=v_norm_final_g)
    order = list(weights)
    me = 4 * lax.axis_index("x") + 2 * lax.axis_index("y") + lax.axis_index("c")
    n_in, n_up = w_in.shape[2], w_up.shape[2]
    r_out, r_down = w_out.shape[1], w_down.shape[1]

    def with_mine(landed, mine):
        return lax.dynamic_update_index_in_dim(landed, mine, me, 0)

    up_blk = w_up[0].T.astype(BF16)
    out_blk = w_out[0].astype(BF16)
    down_blk = w_down[0].astype(BF16)
    taps_blk = jnp.pad(conv_w[0], ((0, 5), (0, 0)))
    (in_all,) = _all_gather([w_in[0].T.astype(BF16)], "gather_w_in")
    in_all, rest_blocks = lax.optimization_barrier((in_all, [up_blk, out_blk, down_blk, taps_blk]))
    rest_handle, token = _xchg_start(rest_blocks, [False] * 4, "gather_rest_start")
    w_in_t = in_all.reshape(N_DEV * n_in, D_MODEL)

    def rest_fn(after):
        mine, landed = _xchg_wait(rest_handle, [False] * 4, after, "gather_rest_wait")
        up_all, out_all, down_all, taps_all = [with_mine(l, b) for l, b in zip(landed, mine)]
        return (out_all.reshape(N_DEV * r_out, D_MODEL), up_all.reshape(N_DEV * n_up, D_MODEL),
                taps_all[:, :3, :].transpose(1, 0, 2).reshape(3, N_DEV * n_up),
                down_all.reshape(N_DEV * r_down, D_MODEL))

    sent = {}

    def send_fn(name, grad):
        if name == "w_in":
            parts = grad.reshape(D_MODEL, N_DEV, -1).transpose(1, 0, 2).astype(BF16)
        elif name == "w_up":
            parts = grad.reshape(D_MODEL, N_DEV, -1).transpose(1, 0, 2)
        else:
            parts = grad.reshape(N_DEV, -1, D_MODEL)
        sent[name], tok = _xchg_start([parts], [True], "scatter_" + name + "_start")
        return tok

    small = {}

    def small_fn(loss_local, g):
        loss_rows = jnp.pad(loss_local.reshape(1, 1), ((0, 31), (0, LANES - 1)))
        packed = [_rows128(g[k]) for k in SMALL] + [loss_rows, _rows128(g["conv_w"])]
        small["sizes"] = [p.shape[0] for p in packed]
        small["handle"], tok = _xchg_start([jnp.concatenate(packed, axis=0)], [False], "gather_small_start")
        return tok

    grad_x, after = _local_step(
        x[0], loss_target[0], norm_mix_g, w_in_t, b_forget, gmlp_norm_g, w_spatial, b_spatial, norm_ffn_g, conv_b,
        norm_final_g, rest_fn, send_fn, small_fn, token)

    outs = {}

    def update_big(name, after):
        (parts,), (landed,) = _xchg_wait(sent[name], [True], after, "scatter_" + name + "_wait")
        got = with_mine(landed, lax.dynamic_index_in_dim(parts, me, 0, keepdims=False))
        outs[name] = tuple(_adamw(weights[name], m_in[name], v_in[name], got, "adamw_" + name))
        return outs[name][0]

    for name in ("w_down", "w_up", "w_out"):
        after = update_big(name, after)

    (mine,), (landed,) = _xchg_wait(small["handle"], [False], after, "gather_small_wait")
    small_all = with_mine(landed, mine)
    sizes = small["sizes"]
    n_small_rows = sum(sizes[:-2])

    def pack(src):
        return jnp.concatenate([_rows128(src[k]) for k in SMALL] + [jnp.zeros((sizes[-2], LANES), F32)], axis=0)

    n_adam_rows = n_small_rows + sizes[-2]
    sg_, sd_, sm_, sv_ = _adamw(pack(weights), pack(m_in), pack(v_in), small_all[:, :n_adam_rows, :], "adamw_small")
    loss = sg_[n_small_rows, 0]
    off = 0
    for k, rows in zip(SMALL, sizes[:-2]):
        shp = weights[k].shape
        cnt = math.prod(shp)
        outs[k] = tuple(a[off:off + rows].reshape(-1)[:cnt].reshape(shp) for a in (sg_, sd_, sm_, sv_))
        off += rows
    taps_parts = small_all[:, n_adam_rows:, :].reshape(N_DEV, -1)[:, :3 * N_DEV * n_up].reshape(N_DEV, 3, N_DEV * n_up)
    taps_mine = lax.dynamic_slice_in_dim(taps_parts, me * n_up, n_up, axis=2)
    taps_mine = jnp.pad(taps_mine, ((0, 0), (0, 5), (0, 0)))

    def pad8(a):
        return jnp.pad(a[0], ((0, 5), (0, 0)))

    res = _adamw(pad8(conv_w), pad8(m_conv_w), pad8(v_conv_w), taps_mine, "adamw_conv_w")
    outs["conv_w"] = tuple(a[:3][None] for a in res)
    update_big("w_in", sg_)

    return (loss, grad_x[None], *[outs[k][0] for k in order], *[outs[k][1] for k in order],
            *[outs[k][2] for k in order], *[outs[k][3] for k in order])
```

```python
import functools
import math

import jax
import jax.numpy as jnp
from jax import lax
from jax.experimental import pallas as pl
from jax.experimental.pallas import tpu as pltpu

F32 = jnp.float32
BF16 = jnp.bfloat16

N_DEV = 8
D_MODEL = 1024
ATT_WIDTH = 512
GMLP_WIDTH = 512
HEAD_DIM = 64
N_HEADS = 8
N_PAIRS = 4
N_GROUPS = 8
GROUP_DIM = 64
CHUNK = 128
D_FF = 2816
IN_COLS = 2568
IN_PAD = 2688
QKV = 1536
UG_END = 2560
EPS = 1e-6
LANES = 128

ADAM_LR = 0.001
ADAM_B1 = 0.9
ADAM_B2 = 0.999
ADAM_EPS = 1e-08
ADAM_WD = 0.01
ADAM_STEP = 10

ATT_TQ = 1024
ATT_TK = 1024
FFN_TM, FFN_TN = 512, 1408
CONV_TM, CONV_TN = 512, 1408
VMEM_LIMIT = 56 * 1024 * 1024
MESH = pl.DeviceIdType.MESH


def _cp(sem, vmem=None):
    return pltpu.CompilerParams(dimension_semantics=sem, vmem_limit_bytes=vmem)


def _pick(n, prefs):
    for p in prefs:
        if n % p == 0:
            return p
    return n


def _split3(x):
    hi = x.astype(BF16)
    r1 = x - hi.astype(F32)
    mid = r1.astype(BF16)
    lo = (r1 - mid.astype(F32)).astype(BF16)
    return hi, mid, lo


def _dot3(x, ones_bf):
    d = functools.partial(jnp.dot, preferred_element_type=F32)
    out = []
    for c in range(0, x.shape[1], 2 * LANES):
        blk = ones_bf[c:c + 2 * LANES, c:c + 2 * LANES]
        hi, mid, lo = _split3(x[:, c:c + 2 * LANES])
        out.append(d(hi, blk) + d(mid, blk) + d(lo, blk))
    return jnp.concatenate(out, axis=1)


def _dot3l(ones_bf, x):
    n = x.shape[1]
    y = jnp.dot(ones_bf, jnp.concatenate(_split3(x), axis=1), preferred_element_type=F32)
    return y[:, :n] + y[:, n:2 * n] + y[:, 2 * n:]


def _gelu(x):
    k = math.sqrt(2.0 / math.pi)
    t = jnp.tanh(k * (x + 0.044715 * (x * x * x)))
    return 0.5 * x * (1.0 + t)


def _gelu_grad(x):
    k = math.sqrt(2.0 / math.pi)
    x2 = x * x
    t = jnp.tanh(k * (x + 0.044715 * (x2 * x)))
    return 0.5 * (1.0 + t) + 0.5 * x * (1.0 - t * t) * (k * (1.0 + 3.0 * 0.044715 * x2))


def _sigmoid(x):
    return 1.0 / (1.0 + jnp.exp(-x))


def _mm(a, b, *, mode, out_dtype, tm, tn, tk, name, res=None, a_halves=False, b_halves=False,
        out_halves=False, outer="i"):
    if mode == "tn":
        K, M = a.shape[-2], a.shape[-1] * (2 if a_halves else 1)
    else:
        M, K = a.shape[-2], a.shape[-1] * (2 if a_halves else 1)
    if mode == "nt":
        N = b.shape[-2]
        assert b.shape[-1] == K
    else:
        N = b.shape[-1] * (2 if b_halves else 1)
    tm, tn, tk = min(tm, M), min(tn, N), min(tk, K)
    assert M % tm == 0 and N % tn == 0 and K % tk == 0, (name, M, N, K, tm, tn, tk)
    nm, nn, nk = M // tm, N // tn, K // tk

    def ij(g0, g1):
        return (g0, g1) if outer == "i" else (g1, g0)

    if mode == "nn":
        dims = (((1,), (0,)), ((), ()))
        if a_halves:
            nkh = nk // 2
            a_spec = pl.BlockSpec((None, tm, tk), lambda g0, g1, k: (k // nkh, ij(g0, g1)[0], k % nkh))
        else:
            a_spec = pl.BlockSpec((tm, tk), lambda g0, g1, k: (ij(g0, g1)[0], k))
        b_spec = pl.BlockSpec((tk, tn), lambda g0, g1, k: (k, ij(g0, g1)[1]))
    elif mode == "nt":
        dims = (((1,), (1,)), ((), ()))
        if a_halves:
            nkh = nk // 2
            a_spec = pl.BlockSpec((None, tm, tk), lambda g0, g1, k: (k // nkh, ij(g0, g1)[0], k % nkh))
        else:
            a_spec = pl.BlockSpec((tm, tk), lambda g0, g1, k: (ij(g0, g1)[0], k))
        b_spec = pl.BlockSpec((tn, tk), lambda g0, g1, k: (ij(g0, g1)[1], k))
    else:
        dims = (((0,), (0,)), ((), ()))
        if a_halves:
            nmh = nm // 2
            a_spec = pl.BlockSpec((None, tk, tm), lambda g0, g1, k: (ij(g0, g1)[0] // nmh, k, ij(g0, g1)[0] % nmh))
        else:
            a_spec = pl.BlockSpec((tk, tm), lambda g0, g1, k: (k, ij(g0, g1)[0]))
        if b_halves:
            nnh = nn // 2
            b_spec = pl.BlockSpec((None, tk, tn), lambda g0, g1, k: (ij(g0, g1)[1] // nnh, k, ij(g0, g1)[1] % nnh))
        else:
            b_spec = pl.BlockSpec((tk, tn), lambda g0, g1, k: (k, ij(g0, g1)[1]))
    if out_halves:
        nnh = nn // 2
        o_spec = pl.BlockSpec((None, tm, tn), lambda g0, g1, k: (ij(g0, g1)[1] // nnh, ij(g0, g1)[0], ij(g0, g1)[1] % nnh))
        o_shape = jax.ShapeDtypeStruct((2, M, N // 2), out_dtype)
    else:
        o_spec = pl.BlockSpec((tm, tn), lambda g0, g1, k: ij(g0, g1))
        o_shape = jax.ShapeDtypeStruct((M, N), out_dtype)
    in_specs = [a_spec, b_spec]
    args = [a, b]
    if res is not None:
        in_specs.append(pl.BlockSpec((tm, tn), lambda g0, g1, k: ij(g0, g1)))
        args.append(res)

    def body(*refs):
        if res is not None:
            a_ref, b_ref, r_ref, o_ref = refs[:4]
        else:
            a_ref, b_ref, o_ref = refs[:3]
            r_ref = None
        part = lax.dot_general(a_ref[...], b_ref[...], dims, preferred_element_type=F32)
        if nk == 1:
            if r_ref is not None:
                part = part + r_ref[...]
            o_ref[...] = part.astype(out_dtype)
            return
        acc_ref = refs[-1]
        k = pl.program_id(2)

        @pl.when(k == 0)
        def _():
            acc_ref[...] = part

        @pl.when(k > 0)
        def _():
            acc_ref[...] += part

        @pl.when(k == nk - 1)
        def _():
            tot = acc_ref[...]
            if r_ref is not None:
                tot = tot + r_ref[...]
            o_ref[...] = tot.astype(out_dtype)

    grid = (nm, nn, nk) if outer == "i" else (nn, nm, nk)
    scratch = [] if nk == 1 else [pltpu.VMEM((tm, tn), F32)]
    return pl.pallas_call(
        body, out_shape=o_shape, grid=grid, in_specs=in_specs, out_specs=o_spec, scratch_shapes=scratch,
        name=name, compiler_params=_cp(("parallel", "parallel", "arbitrary"), VMEM_LIMIT),
    )(*args)


def _aug(lane, terms):
    out = 0.0
    for j, t in enumerate(terms):
        out = jnp.where(lane == HEAD_DIM + j, t, out)
    return out


def _split3f(x):
    hi, mid, lo = _split3(x)
    return [hi.astype(F32), mid.astype(F32), lo.astype(F32)]


def _inproj_fwd(x, g_mix, w_pad, bf_pad):
    S = x.shape[0]
    tm = _pick(S, (512, 256))
    tri = (lax.broadcasted_iota(jnp.int32, (tm, tm), 0) >= lax.broadcasted_iota(jnp.int32, (tm, tm), 1)).astype(BF16)

    def body(x_ref, g_ref, w_ref, bf_ref, tri_ref, put_ref, one_ref, xn_ref, qa_ref, ka_ref, va_ref, ug_ref, zf_ref,
             carry_ref):
        i = pl.program_id(0)

        @pl.when(i == 0)
        def _():
            carry_ref[...] = jnp.zeros_like(carry_ref)

        xf = x_ref[...]
        r = lax.rsqrt(jnp.mean(xf * xf, axis=-1, keepdims=True) + EPS)
        xn = ((xf * r) * g_ref[...]).astype(BF16)
        xn_ref[...] = xn
        proj = lax.dot_general(xn, w_ref[...], _NT, preferred_element_type=F32)
        ug_ref[...] = proj[:, QKV:UG_END]
        zf = proj[:, UG_END:] + bf_ref[...]
        zf_ref[...] = zf
        lf = jnp.minimum(zf, 0.0) - jnp.log(1.0 + jnp.exp(-jnp.abs(zf)))
        c = _dot3l(tri_ref[...], lf) + carry_ref[0:1, :]
        carry_ref[0:1, :] = c[tm - 1:tm, :]
        c3 = jnp.concatenate(_split3(c), axis=1)
        aug_q = jnp.dot(c3, put_ref[0], preferred_element_type=F32) + one_ref[0:1, :]
        aug_k = jnp.dot(c3, put_ref[1], preferred_element_type=F32) + one_ref[1:2, :]
        lane = lax.broadcasted_iota(jnp.int32, (tm, LANES), 1)
        for h in range(N_HEADS):
            p, odd = h // 2, h % 2

            def head(base, scale=None, p=p, odd=odd):
                blk = proj[:, base + p * LANES:base + (p + 1) * LANES]
                if scale is not None:
                    blk = blk * scale
                return pltpu.roll(blk, HEAD_DIM, 1) if odd else blk

            cols = slice(h * LANES, (h + 1) * LANES)
            qa_ref[:, cols] = jnp.where(lane < HEAD_DIM, head(0, HEAD_DIM ** -0.5), aug_q[:, cols]).astype(BF16)
            ka_ref[:, cols] = jnp.where(lane < HEAD_DIM, head(ATT_WIDTH), aug_k[:, cols]).astype(BF16)
            va_ref[:, cols] = jnp.where(lane < HEAD_DIM, head(2 * ATT_WIDTH), one_ref[2:3, cols]).astype(BF16)

    wide = N_HEADS * LANES
    src = lax.broadcasted_iota(jnp.int32, (3 * LANES, wide), 0)
    col = lax.broadcasted_iota(jnp.int32, (3 * LANES, wide), 1)
    hd, term = src % LANES, src // LANES
    to_q = (col == hd * LANES + HEAD_DIM + term) & (hd < N_HEADS)
    to_k = (col == hd * LANES + HEAD_DIM + 3 + term) & (hd < N_HEADS)
    put = jnp.stack([to_q.astype(BF16), -to_k.astype(BF16)])
    off = lax.broadcasted_iota(jnp.int32, (8, wide), 1) % LANES - HEAD_DIM
    row = lax.broadcasted_iota(jnp.int32, (8, wide), 0)
    q_one = (off >= 3) & (off < 6)
    k_one = ((off >= 0) & (off < 3)) | ((off >= 6) & (off < 9))
    v_one = (off >= 0) & (off < 3)
    ones = jnp.where(row == 0, q_one, jnp.where(row == 1, k_one, (row == 2) & v_one)).astype(F32)
    return pl.pallas_call(
        body,
        out_shape=(jax.ShapeDtypeStruct((S, D_MODEL), BF16), jax.ShapeDtypeStruct((S, wide), BF16),
                   jax.ShapeDtypeStruct((S, wide), BF16), jax.ShapeDtypeStruct((S, wide), BF16),
                   jax.ShapeDtypeStruct((S, 2 * GMLP_WIDTH), F32), jax.ShapeDtypeStruct((S, LANES), F32)),
        grid=(S // tm,),
        in_specs=[pl.BlockSpec((tm, D_MODEL), lambda i: (i, 0)), pl.BlockSpec((1, D_MODEL), lambda i: (0, 0)),
                  pl.BlockSpec((IN_PAD, D_MODEL), lambda i: (0, 0)), pl.BlockSpec((1, LANES), lambda i: (0, 0)),
                  pl.BlockSpec((tm, tm), lambda i: (0, 0)), pl.BlockSpec((2, 3 * LANES, wide), lambda i: (0, 0, 0)),
                  pl.BlockSpec((8, wide), lambda i: (0, 0))],
        out_specs=(pl.BlockSpec((tm, D_MODEL), lambda i: (i, 0)), pl.BlockSpec((tm, wide), lambda i: (i, 0)),
                   pl.BlockSpec((tm, wide), lambda i: (i, 0)), pl.BlockSpec((tm, wide), lambda i: (i, 0)),
                   pl.BlockSpec((tm, 2 * GMLP_WIDTH), lambda i: (i, 0)), pl.BlockSpec((tm, LANES), lambda i: (i, 0))),
        scratch_shapes=[pltpu.VMEM((8, LANES), F32)],
        name="inproj_fwd", compiler_params=_cp(("arbitrary",), VMEM_LIMIT),
    )(x, g_mix, w_pad, bf_pad, tri, put, ones)


def _group_ones():
    r = lax.broadcasted_iota(jnp.int32, (GMLP_WIDTH, GMLP_WIDTH), 0) // GROUP_DIM
    c = lax.broadcasted_iota(jnp.int32, (GMLP_WIDTH, GMLP_WIDTH), 1) // GROUP_DIM
    return (r == c).astype(BF16)


def _gmlp_mixed(vn_bf, w_ref, bias, n_chunks):
    lane = lax.broadcasted_iota(jnp.int32, (CHUNK, LANES), 1)
    row = lax.broadcasted_iota(jnp.int32, (CHUNK, CHUNK), 0)
    col = lax.broadcasted_iota(jnp.int32, (CHUNK, CHUNK), 1)
    ws = [jnp.where(row >= col, w_ref[g], 0.0).astype(BF16) for g in range(N_GROUPS)]
    rows = []
    for ci in range(n_chunks):
        cols = []
        for pp in range(N_GROUPS // 2):
            v = vn_bf[ci * CHUNK:(ci + 1) * CHUNK, pp * LANES:(pp + 1) * LANES]
            v_lo = jnp.where(lane < GROUP_DIM, v, jnp.zeros_like(v))
            v_hi = jnp.where(lane >= GROUP_DIM, v, jnp.zeros_like(v))
            m = (jnp.dot(ws[2 * pp], v_lo, preferred_element_type=F32)
                 + jnp.dot(ws[2 * pp + 1], v_hi, preferred_element_type=F32))
            cols.append(m + bias[:, pp * LANES:(pp + 1) * LANES])
        rows.append(jnp.concatenate(cols, axis=1))
    return jnp.concatenate(rows, axis=0)


def _gmlp_fwd(ug, gain, w_s, bias_full):
    S = ug.shape[0]
    tm = _pick(S, (512, 256, 128))
    ones = _group_ones()

    def body(ug_ref, gain_ref, w_ref, bias_ref, ones_ref, sg_ref):
        u = _gelu(ug_ref[:, :GMLP_WIDTH])
        vr = _gelu(ug_ref[:, GMLP_WIDTH:])
        ms = _dot3(vr * vr, ones_ref[...]) * (1.0 / GROUP_DIM)
        vn = ((vr * lax.rsqrt(ms + EPS)) * gain_ref[...]).astype(BF16)
        mixed = _gmlp_mixed(vn, w_ref, bias_ref[...], tm // CHUNK)
        sg_ref[...] = (u * mixed).astype(BF16)

    return pl.pallas_call(
        body, out_shape=jax.ShapeDtypeStruct((S, GMLP_WIDTH), BF16), grid=(S // tm,),
        in_specs=[pl.BlockSpec((tm, 2 * GMLP_WIDTH), lambda i: (i, 0)), pl.BlockSpec((1, GMLP_WIDTH), lambda i: (0, 0)),
                  pl.BlockSpec((N_GROUPS, CHUNK, CHUNK), lambda i: (0, 0, 0)),
                  pl.BlockSpec((CHUNK, GMLP_WIDTH), lambda i: (0, 0)),
                  pl.BlockSpec((GMLP_WIDTH, GMLP_WIDTH), lambda i: (0, 0))],
        out_specs=pl.BlockSpec((tm, GMLP_WIDTH), lambda i: (i, 0)),
        name="gmlp_fwd", compiler_params=_cp(("parallel",), VMEM_LIMIT),
    )(ug, gain, w_s, bias_full, ones)


_NT = (((1,), (1,)), ((), ()))
_TN = (((0,), (0,)), ((), ()))


def _attn_fwd(qa, ka, va):
    S = qa.shape[0]
    tq = _pick(S, (ATT_TQ, 256))
    tk = min(ATT_TK, tq)
    nq = S // tq
    assert tq == tk, "the diagonal block is handled as one tq x tq tile"
    per_q = 1

    def body(q_ref, k_ref, v_ref, o_ref, lse_ref, ob_ref):
        qi = pl.program_id(1)
        lane = lax.broadcasted_iota(jnp.int32, (tq, LANES), 1)
        qs = [q_ref[:, :LANES], q_ref[:, LANES:]]

        def update(q, ks, k_len, h, m, acc, first_row):
            cols = slice(h * LANES, (h + 1) * LANES)
            s = lax.dot_general(q, k_ref[pl.ds(ks, k_len), cols], _NT, preferred_element_type=F32)
            if first_row is not None:
                rid = lax.broadcasted_iota(jnp.int32, s.shape, 0) + first_row
                s = jnp.where(rid >= lax.broadcasted_iota(jnp.int32, s.shape, 1), s, -jnp.inf)
            m_new = jnp.maximum(m, jnp.max(s, axis=-1, keepdims=True))
            p = jnp.exp(s - m_new).astype(BF16)
            acc = jnp.exp(m - m_new) * acc + jnp.dot(p, v_ref[pl.ds(ks, k_len), cols], preferred_element_type=F32)
            return m_new, acc

        def step(kb, carry):
            ks = pl.multiple_of(kb * tk, tk)
            return tuple(update(qs[h], ks, tk, h, *carry[h], None) for h in range(2))

        one = (jnp.full((tq, 1), -jnp.inf, F32), jnp.zeros((tq, LANES), F32))
        carry = lax.fori_loop(0, qi * per_q, step, (one, one))
        outs, lses = [], []
        strip = tq // 2
        diag = pl.multiple_of(qi * tq, tq)
        for h in range(2):
            ms, accs = [], []
            for r in range(2):
                rows = slice(r * strip, (r + 1) * strip)
                m, acc = update(qs[h][rows], diag, (r + 1) * strip, h, carry[h][0][rows], carry[h][1][rows], r * strip)
                ms.append(m)
                accs.append(acc)
            m, acc = jnp.concatenate(ms, axis=0), jnp.concatenate(accs, axis=0)
            l = acc[:, HEAD_DIM:HEAD_DIM + 1]
            outs.append(acc / l)
            lses.append(m + jnp.log(l))
        o = jnp.where(lane < HEAD_DIM, outs[0], pltpu.roll(outs[1], HEAD_DIM, 1))
        o_ref[...] = o
        ob_ref[...] = o.astype(BF16)
        lse_ref[...] = jnp.where(lane < HEAD_DIM, lses[0], lses[1])

    return pl.pallas_call(
        body,
        out_shape=(jax.ShapeDtypeStruct((S, ATT_WIDTH), F32), jax.ShapeDtypeStruct((S, ATT_WIDTH), F32),
                   jax.ShapeDtypeStruct((S, ATT_WIDTH), BF16)),
        grid=(N_PAIRS, nq),
        in_specs=[pl.BlockSpec((tq, 2 * LANES), lambda p, i: (i, p)),
                  pl.BlockSpec((S, 2 * LANES), lambda p, i: (0, p)),
                  pl.BlockSpec((S, 2 * LANES), lambda p, i: (0, p))],
        out_specs=(pl.BlockSpec((tq, LANES), lambda p, i: (i, p)), pl.BlockSpec((tq, LANES), lambda p, i: (i, p)),
                   pl.BlockSpec((tq, LANES), lambda p, i: (i, p))),
        name="attn_fwd", compiler_params=_cp(("parallel", "parallel"), VMEM_LIMIT),
    )(qa, ka, va)


def _shift_rows(x, prev, n):
    rid = lax.broadcasted_iota(jnp.int32, x.shape, 0)
    y = pltpu.roll(x, n, 0)
    if n == 1:
        return jnp.where(rid == 0, prev[7:8, :], y)
    return jnp.where(rid == 0, prev[6:7, :], jnp.where(rid == 1, prev[7:8, :], y))


def _shift_rows_up(x, nxt, n):
    rows = x.shape[0]
    rid = lax.broadcasted_iota(jnp.int32, x.shape, 0)
    y = pltpu.roll(x, rows - n, 0)
    if n == 1:
        return jnp.where(rid == rows - 1, nxt[0:1, :], y)
    return jnp.where(rid == rows - 2, nxt[0:1, :], jnp.where(rid == rows - 1, nxt[1:2, :], y))


def _conv3(cur, prev, w, b):
    return (w[0:1, :] * _shift_rows(cur, prev, 2) + w[1:2, :] * _shift_rows(cur, prev, 1)
            + w[2:3, :] * cur + b)


def _ffn_up_conv(hn, w_up_bf, cw, cb):
    S = hn.shape[0]
    F = D_FF
    tm = _pick(S, (FFN_TM, 256))
    tn = _pick(F, (FFN_TN, 256, 128))
    nj = F // tn

    def body(hn_ref, wa_ref, wg_ref, cw_ref, cb_ref, hu_ref, hc_ref, act_ref, tail_ref):
        i = pl.program_id(1)

        @pl.when(i == 0)
        def _():
            tail_ref[...] = jnp.zeros_like(tail_ref)

        hn_v = hn_ref[...]
        halves = []
        for h, w_ref in enumerate((wa_ref, wg_ref)):
            hu = lax.dot_general(hn_v, w_ref[...], _NT, preferred_element_type=F32)
            hu_ref[h] = hu.astype(BF16)
            hc = _conv3(hu, tail_ref[h], cw_ref[h], cb_ref[h])
            hc_ref[h] = hc
            halves.append(hc)
            tail_ref[h] = hu[tm - 8:, :]
        a, g = halves
        act_ref[...] = (g * _sigmoid(g) * a).astype(BF16)

    both = pl.BlockSpec((2, tm, tn), lambda j, i: (0, i, j))
    return pl.pallas_call(
        body, out_shape=(jax.ShapeDtypeStruct((2, S, F), BF16), jax.ShapeDtypeStruct((2, S, F), F32),
                         jax.ShapeDtypeStruct((S, F), BF16)),
        grid=(nj, S // tm),
        in_specs=[pl.BlockSpec((tm, D_MODEL), lambda j, i: (i, 0)),
                  pl.BlockSpec((tn, D_MODEL), lambda j, i: (j, 0)),
                  pl.BlockSpec((tn, D_MODEL), lambda j, i: (nj + j, 0)),
                  pl.BlockSpec((2, 8, tn), lambda j, i: (0, 0, j)),
                  pl.BlockSpec((2, 1, tn), lambda j, i: (0, 0, j))],
        out_specs=(both, both, pl.BlockSpec((tm, tn), lambda j, i: (i, j))),
        scratch_shapes=[pltpu.VMEM((2, 8, tn), F32)],
        name="ffn_up_conv", compiler_params=_cp(("parallel", "arbitrary"), VMEM_LIMIT),
    )(hn, w_up_bf, w_up_bf, cw, cb)


def _ffn_down_loss(act, w_down_bf, h1, g_final, target):
    S = h1.shape[0]
    tm = _pick(S, (512, 256))

    def body(a_ref, w_ref, h1_ref, g_ref, t_ref, loss_ref, dh_ref, dhb_ref, dg_ref):
        i = pl.program_id(0)

        @pl.when(i == 0)
        def _():
            loss_ref[...] = jnp.zeros_like(loss_ref)
            dg_ref[...] = jnp.zeros_like(dg_ref)

        hf = h1_ref[...] + jnp.dot(a_ref[...], w_ref[...], preferred_element_type=F32)
        g = g_ref[...]
        r = lax.rsqrt(jnp.mean(hf * hf, axis=-1, keepdims=True) + EPS)
        hhat = hf * r
        err = hhat * g - t_ref[...]
        loss_ref[...] += 0.5 * jnp.sum(jnp.mean(err * err, axis=-1, keepdims=True))
        dy = err * (1.0 / D_MODEL)
        dg_ref[0:1, :] += jnp.sum(dy * hhat, axis=0, keepdims=True)
        dhat = dy * g
        dh = r * (dhat - hhat * jnp.mean(dhat * hhat, axis=-1, keepdims=True))
        dh_ref[...] = dh
        dhb_ref[...] = dh.astype(BF16)

    row = pl.BlockSpec((tm, D_MODEL), lambda i: (i, 0))
    return pl.pallas_call(
        body,
        out_shape=(jax.ShapeDtypeStruct((8, LANES), F32), jax.ShapeDtypeStruct((S, D_MODEL), F32),
                   jax.ShapeDtypeStruct((S, D_MODEL), BF16), jax.ShapeDtypeStruct((8, D_MODEL), F32)),
        grid=(S // tm,),
        in_specs=[pl.BlockSpec((tm, D_FF), lambda i: (i, 0)), pl.BlockSpec((D_FF, D_MODEL), lambda i: (0, 0)), row,
                  pl.BlockSpec((1, D_MODEL), lambda i: (0, 0)), row],
        out_specs=(pl.BlockSpec((8, LANES), lambda i: (0, 0)), row, row, pl.BlockSpec((8, D_MODEL), lambda i: (0, 0))),
        name="ffn_down_loss", compiler_params=_cp(("arbitrary",), VMEM_LIMIT),
    )(act, w_down_bf, h1, g_final, target)


def _ffn_up_dx_rms(dhu, w_up_bf, h1, g_ffn, dh2):
    _, S, F = dhu.shape
    tm = _pick(S, (512, 256))

    def body(a_ref, b_ref, h_ref, g_ref, r_ref, dh_ref, dhb_ref, dg_ref):
        i = pl.program_id(0)

        @pl.when(i == 0)
        def _():
            dg_ref[...] = jnp.zeros_like(dg_ref)

        dyv = (jnp.dot(a_ref[0], b_ref[:F, :], preferred_element_type=F32)
               + jnp.dot(a_ref[1], b_ref[F:, :], preferred_element_type=F32))
        hf = h_ref[...]
        r = lax.rsqrt(jnp.mean(hf * hf, axis=-1, keepdims=True) + EPS)
        hhat = hf * r
        dg_ref[0:1, :] += jnp.sum(dyv * hhat, axis=0, keepdims=True)
        dhat = dyv * g_ref[...]
        dh = r_ref[...] + r * (dhat - hhat * jnp.mean(dhat * hhat, axis=-1, keepdims=True))
        dh_ref[...] = dh
        dhb_ref[...] = dh.astype(BF16)

    row = pl.BlockSpec((tm, D_MODEL), lambda i: (i, 0))
    return pl.pallas_call(
        body,
        out_shape=(jax.ShapeDtypeStruct((S, D_MODEL), F32), jax.ShapeDtypeStruct((S, D_MODEL), BF16),
                   jax.ShapeDtypeStruct((8, D_MODEL), F32)),
        grid=(S // tm,),
        in_specs=[pl.BlockSpec((2, tm, F), lambda i: (0, i, 0)), pl.BlockSpec((2 * F, D_MODEL), lambda i: (0, 0)),
                  row, pl.BlockSpec((1, D_MODEL), lambda i: (0, 0)), row],
        out_specs=(row, row, pl.BlockSpec((8, D_MODEL), lambda i: (0, 0))),
        name="ffn_up_dx_rms", compiler_params=_cp(("arbitrary",), VMEM_LIMIT),
    )(dhu, w_up_bf, h1, g_ffn, dh2)


def _conv_gate_bwd(hc, hu, dact, cw):
    _, S, F = hu.shape
    tm = _pick(S, (CONV_TM, 128))
    tn = _pick(F, (CONV_TN, 256, 128))
    r8 = tm // 8
    n_i = S // tm
    last8 = S // 8 - 1

    def body(hc_ref, hcn_ref, hu_ref, da_ref, dan_ref, w_ref, dhu_ref, dcw_ref):
        i = pl.program_id(1)

        @pl.when(i == 0)
        def _():
            dcw_ref[...] = jnp.zeros_like(dcw_ref)

        rid8 = lax.broadcasted_iota(jnp.int32, (8, tn), 0)

        def gate_grads(a, g, d):
            sg = _sigmoid(g)
            return d * (g * sg), d * a * (sg * (1.0 + g * (1.0 - sg)))

        dhc = gate_grads(hc_ref[0], hc_ref[1], da_ref[...])
        dhc_n = gate_grads(hcn_ref[0], hcn_ref[1], dan_ref[...])
        for h in range(2):
            w = w_ref[h]
            d = dhc[h]
            dn = jnp.where(i < n_i - 1, dhc_n[h], 0.0)
            u1 = _shift_rows_up(d, dn, 1)
            u2 = _shift_rows_up(d, dn, 2)
            dhu_ref[h] = (w[2:3, :] * d + w[1:2, :] * u1 + w[0:1, :] * u2).astype(BF16)
            x = hu_ref[h].astype(F32)
            t0, t1, t2, t3 = [jnp.sum(t, axis=0, keepdims=True) for t in (u2 * x, u1 * x, d * x, d)]
            dcw_ref[h] += jnp.where(rid8 == 0, t0, jnp.where(rid8 == 1, t1, jnp.where(rid8 == 2, t2, jnp.where(rid8 == 3, t3, 0.0))))

    cur = pl.BlockSpec((2, tm, tn), lambda j, i: (0, i, j))
    return pl.pallas_call(
        body,
        out_shape=(jax.ShapeDtypeStruct((2, S, F), BF16), jax.ShapeDtypeStruct((2, 8, F), F32)),
        grid=(F // tn, n_i),
        in_specs=[cur, pl.BlockSpec((2, 8, tn), lambda j, i: (0, jnp.minimum((i + 1) * r8, last8), j)), cur,
                  pl.BlockSpec((tm, tn), lambda j, i: (i, j)),
                  pl.BlockSpec((8, tn), lambda j, i: (jnp.minimum((i + 1) * r8, last8), j)),
                  pl.BlockSpec((2, 8, tn), lambda j, i: (0, 0, j))],
        out_specs=(cur, pl.BlockSpec((2, 8, tn), lambda j, i: (0, 0, j))),
        name="conv_gate_bwd", compiler_params=_cp(("parallel", "arbitrary"), VMEM_LIMIT),
    )(hc, hc, hu, dact, dact, cw)


def _out_proj_dx_prep(dh1_bf, w_out_bf, att, lse, qa):
    S = att.shape[0]
    tm = _pick(S, (256,))

    def body(dh_ref, w_ref, o_ref, lse_ref, q_ref, dsg_ref, qb_ref, doa_ref):
        lane = lax.broadcasted_iota(jnp.int32, (tm, LANES), 1)
        dh = dh_ref[...]
        dsg_ref[...] = lax.dot_general(dh, w_ref[ATT_WIDTH:, :], _NT, preferred_element_type=F32)
        datt = lax.dot_general(dh, w_ref[:ATT_WIDTH, :], _NT, preferred_element_type=F32)
        for p in range(N_PAIRS):
            pc = slice(p * LANES, (p + 1) * LANES)
            do = datt[:, pc]
            prod = o_ref[:, pc] * do
            for hh in range(2):
                sel = (lane >= HEAD_DIM) if hh else (lane < HEAD_DIM)
                delta = jnp.sum(jnp.where(sel, prod, 0.0), axis=-1, keepdims=True)
                dod = pltpu.roll(do, HEAD_DIM, 1) if hh else do
                cols = slice((2 * p + hh) * LANES, (2 * p + hh + 1) * LANES)
                doa_ref[:, cols] = jnp.where(lane < HEAD_DIM, dod, _aug(lane, _split3f(-delta))).astype(BF16)
                lcol = p * LANES + hh * HEAD_DIM
                l3 = _split3f(-lse_ref[:, lcol:lcol + 1])
                augl = jnp.where(lane == HEAD_DIM + 6, l3[0], jnp.where(lane == HEAD_DIM + 7, l3[1], l3[2])).astype(BF16)
                qb_ref[:, cols] = jnp.where((lane >= HEAD_DIM + 6) & (lane < HEAD_DIM + 9), augl, q_ref[:, cols])

    half = pl.BlockSpec((tm, ATT_WIDTH), lambda i: (i, 0))
    wide = pl.BlockSpec((tm, N_HEADS * LANES), lambda i: (i, 0))
    return pl.pallas_call(
        body,
        out_shape=(jax.ShapeDtypeStruct((S, GMLP_WIDTH), F32), jax.ShapeDtypeStruct(qa.shape, BF16),
                   jax.ShapeDtypeStruct(qa.shape, BF16)),
        grid=(S // tm,),
        in_specs=[pl.BlockSpec((tm, D_MODEL), lambda i: (i, 0)), pl.BlockSpec((D_MODEL, D_MODEL), lambda i: (0, 0)),
                  half, half, wide],
        out_specs=(half, wide, wide),
        name="out_proj_dx_prep", compiler_params=_cp(("parallel",), VMEM_LIMIT),
    )(dh1_bf, w_out_bf, att, lse, qa)


def _attn_bwd(qb, ka, va, doa):
    S = qb.shape[0]
    tk = _pick(S, (512, 256))
    tq = tk
    nq = S // tq

    def pair(a, scale=None):
        lane = lax.broadcasted_iota(jnp.int32, (a.shape[0], LANES), 1)
        out = jnp.where(lane < HEAD_DIM, a[:, :LANES], pltpu.roll(a[:, LANES:], HEAD_DIM, 1))
        return out if scale is None else out * scale

    def head_lanes(a, col, sign, first):
        lane = lax.broadcasted_iota(jnp.int32, (a.shape[0], LANES), 1)
        return jnp.where(lane == first, sign * a[:, col:col + 1],
                         jnp.where(lane == first + 1, sign * a[:, LANES + col:LANES + col + 1], 0.0))

    def body(q_ref, do_ref, k_ref, v_ref, dqc_ref, dkc_ref, dvc_ref, dcq_ref, dck_ref, dq_ref, dka_ref, dva_ref):
        kb = pl.program_id(1)

        @pl.when(kb == 0)
        def _():
            dq_ref[...] = jnp.zeros_like(dq_ref)

        dka_ref[...] = jnp.zeros_like(dka_ref)
        dva_ref[...] = jnp.zeros_like(dva_ref)
        def sub_tile(qs, q_len, k_off, k_len, masked):
            keys = slice(k_off, k_off + k_len)
            for h in range(2):
                cols = slice(h * LANES, (h + 1) * LANES)
                qblk = q_ref[pl.ds(qs, q_len), cols]
                doblk = do_ref[pl.ds(qs, q_len), cols]
                kh = k_ref[keys, cols]
                p = jnp.exp(lax.dot_general(kh, qblk, _NT, preferred_element_type=F32))
                if masked:
                    p = jnp.where(lax.broadcasted_iota(jnp.int32, p.shape, 1) >= lax.broadcasted_iota(jnp.int32, p.shape, 0),
                                  p, 0.0)
                ds = (p * lax.dot_general(v_ref[keys, cols], doblk, _NT, preferred_element_type=F32)).astype(BF16)
                dva_ref[keys, cols] += jnp.dot(p.astype(BF16), doblk, preferred_element_type=F32)
                dka_ref[keys, cols] += jnp.dot(ds, qblk, preferred_element_type=F32)
                dq_ref[pl.ds(qs, q_len), cols] += lax.dot_general(ds, kh, _TN, preferred_element_type=F32)

        half = tk // 2
        sub_tile(pl.multiple_of(kb * tq, tq), tq, 0, half, True)
        sub_tile(pl.multiple_of(kb * tq + half, half), half, half, half, True)

        def step(qi, carry):
            sub_tile(pl.multiple_of(qi * tq, tq), tq, 0, tk, False)
            return carry

        lax.fori_loop(kb + 1, nq, step, 0)
        dka = dka_ref[...]
        dkc_ref[...] = pair(dka).astype(BF16)
        dvc_ref[...] = pair(dva_ref[...]).astype(BF16)
        first = 2 * pl.program_id(0)
        dck_ref[...] = head_lanes(dka, HEAD_DIM + 3, -1.0, first)

        @pl.when(kb == nq - 1)
        def _():
            dqa = dq_ref[...]
            dqc_ref[...] = pair(dqa, HEAD_DIM ** -0.5).astype(BF16)
            dcq_ref[...] = head_lanes(dqa, HEAD_DIM, 1.0, first)

    wide = 2 * LANES
    half = jax.ShapeDtypeStruct((S, ATT_WIDTH), BF16)
    slabs = jax.ShapeDtypeStruct((N_PAIRS, S, LANES), F32)
    return pl.pallas_call(
        body,
        out_shape=(half, half, half, slabs, slabs),
        grid=(N_PAIRS, nq),
        in_specs=[pl.BlockSpec((S, wide), lambda p, j: (0, p)), pl.BlockSpec((S, wide), lambda p, j: (0, p)),
                  pl.BlockSpec((tk, wide), lambda p, j: (j, p)), pl.BlockSpec((tk, wide), lambda p, j: (j, p))],
        out_specs=(pl.BlockSpec((S, LANES), lambda p, j: (0, p)), pl.BlockSpec((tk, LANES), lambda p, j: (j, p)),
                   pl.BlockSpec((tk, LANES), lambda p, j: (j, p)), pl.BlockSpec((None, S, LANES), lambda p, j: (p, 0, 0)),
                   pl.BlockSpec((None, tk, LANES), lambda p, j: (p, j, 0))),
        scratch_shapes=[pltpu.VMEM((S, wide), F32), pltpu.VMEM((tk, wide), F32), pltpu.VMEM((tk, wide), F32)],
        name="attn_bwd", compiler_params=_cp(("parallel", "arbitrary"), VMEM_LIMIT),
    )(qb, doa, ka, va)


def _gmlp_bwd(ug, dsg, gain, w_s, wt_s, bias_full):
    S = ug.shape[0]
    tm = _pick(S, (512, 256, 128))
    n_chunks = tm // CHUNK
    n_i = S // tm
    ones = _group_ones()
    nt = (((1,), (1,)), ((), ()))

    def body(ug_ref, dsg_ref, gain_ref, w_ref, wt_ref, bias_ref, ones_ref, dug_ref, dw_ref, dgain_ref, dbias_ref,
             dbacc_ref):
        i = pl.program_id(0)

        @pl.when(i == 0)
        def _():
            dw_ref[...] = jnp.zeros_like(dw_ref)
            dgain_ref[...] = jnp.zeros_like(dgain_ref)
            dbacc_ref[...] = jnp.zeros_like(dbacc_ref)

        ones_m = ones_ref[...]
        pu = ug_ref[:, :GMLP_WIDTH]
        pg = ug_ref[:, GMLP_WIDTH:]
        u = _gelu(pu)
        vr = _gelu(pg)
        ms = _dot3(vr * vr, ones_m) * (1.0 / GROUP_DIM)
        rinv = lax.rsqrt(ms + EPS)
        vhat = vr * rinv
        gain_v = gain_ref[...]
        vn = (vhat * gain_v).astype(BF16)
        mixed = _gmlp_mixed(vn, w_ref, bias_ref[...], n_chunks)
        dsg_v = dsg_ref[...]
        du = dsg_v * mixed
        dmixed = dsg_v * u
        dm_bf = dmixed.astype(BF16)
        lane = lax.broadcasted_iota(jnp.int32, (CHUNK, LANES), 1)
        row = lax.broadcasted_iota(jnp.int32, (CHUNK, CHUNK), 0)
        col = lax.broadcasted_iota(jnp.int32, (CHUNK, CHUNK), 1)
        wts = [jnp.where(col >= row, wt_ref[g], 0.0).astype(BF16) for g in range(N_GROUPS)]
        dvn_rows = []
        dbsum = jnp.zeros((CHUNK, GMLP_WIDTH), F32)
        for ci in range(n_chunks):
            rs = slice(ci * CHUNK, (ci + 1) * CHUNK)
            dbsum = dbsum + dmixed[rs, :]
            cols = []
            for pp in range(N_GROUPS // 2):
                cs = slice(pp * LANES, (pp + 1) * LANES)
                dm = dm_bf[rs, cs]
                dm_lo = jnp.where(lane < GROUP_DIM, dm, jnp.zeros_like(dm))
                dm_hi = jnp.where(lane >= GROUP_DIM, dm, jnp.zeros_like(dm))
                vb = vn[rs, cs]
                dw_ref[2 * pp] += lax.dot_general(dm_lo, vb, nt, preferred_element_type=F32)
                dw_ref[2 * pp + 1] += lax.dot_general(dm_hi, vb, nt, preferred_element_type=F32)
                cols.append(jnp.dot(wts[2 * pp], dm_lo, preferred_element_type=F32)
                            + jnp.dot(wts[2 * pp + 1], dm_hi, preferred_element_type=F32))
            dvn_rows.append(jnp.concatenate(cols, axis=1))
        dvn = jnp.concatenate(dvn_rows, axis=0)
        dbacc_ref[...] += dbsum
        dgain_ref[0:1, :] += jnp.sum(dvn * vhat, axis=0, keepdims=True)
        dvhat = dvn * gain_v
        gm = _dot3(dvhat * vhat, ones_m) * (1.0 / GROUP_DIM)
        dvr = rinv * (dvhat - vhat * gm)
        dug_ref[:, :GMLP_WIDTH] = (du * _gelu_grad(pu)).astype(BF16)
        dug_ref[:, GMLP_WIDTH:] = (dvr * _gelu_grad(pg)).astype(BF16)

        @pl.when(i == n_i - 1)
        def _():
            for g in range(N_GROUPS):
                dw_ref[g] = jnp.where(row >= col, dw_ref[g], 0.0)
            dbias_ref[...] = _dot3(dbacc_ref[...], ones_m)

    return pl.pallas_call(
        body,
        out_shape=(jax.ShapeDtypeStruct((S, 2 * GMLP_WIDTH), BF16), jax.ShapeDtypeStruct((N_GROUPS, CHUNK, CHUNK), F32),
                   jax.ShapeDtypeStruct((8, GMLP_WIDTH), F32), jax.ShapeDtypeStruct((CHUNK, GMLP_WIDTH), F32)),
        grid=(n_i,),
        in_specs=[pl.BlockSpec((tm, 2 * GMLP_WIDTH), lambda i: (i, 0)), pl.BlockSpec((tm, GMLP_WIDTH), lambda i: (i, 0)),
                  pl.BlockSpec((1, GMLP_WIDTH), lambda i: (0, 0)),
                  pl.BlockSpec((N_GROUPS, CHUNK, CHUNK), lambda i: (0, 0, 0)),
                  pl.BlockSpec((N_GROUPS, CHUNK, CHUNK), lambda i: (0, 0, 0)),
                  pl.BlockSpec((CHUNK, GMLP_WIDTH), lambda i: (0, 0)),
                  pl.BlockSpec((GMLP_WIDTH, GMLP_WIDTH), lambda i: (0, 0))],
        out_specs=(pl.BlockSpec((tm, 2 * GMLP_WIDTH), lambda i: (i, 0)),
                   pl.BlockSpec((N_GROUPS, CHUNK, CHUNK), lambda i: (0, 0, 0)),
                   pl.BlockSpec((8, GMLP_WIDTH), lambda i: (0, 0)),
                   pl.BlockSpec((CHUNK, GMLP_WIDTH), lambda i: (0, 0))),
        scratch_shapes=[pltpu.VMEM((CHUNK, GMLP_WIDTH), F32)],
        name="gmlp_bwd", compiler_params=_cp(("arbitrary",), VMEM_LIMIT),
    )(ug, dsg, gain, w_s, wt_s, bias_full, ones)


def _gate_bwd(dcq, dck, zf):
    S = zf.shape[0]
    tm = _pick(S, (256,))
    n_i = S // tm
    triu = (lax.broadcasted_iota(jnp.int32, (tm, tm), 0) <= lax.broadcasted_iota(jnp.int32, (tm, tm), 1)).astype(BF16)

    def body(dcq_ref, dck_ref, zf_ref, tri_ref, dzf_ref, dbf_ref, carry_ref):
        i = pl.program_id(0)

        @pl.when(i == 0)
        def _():
            carry_ref[...] = jnp.zeros_like(carry_ref)
            dbf_ref[...] = jnp.zeros_like(dbf_ref)

        lane = lax.broadcasted_iota(jnp.int32, (tm, LANES), 1)
        dc = dcq_ref[0] + dck_ref[0]
        for p in range(1, N_PAIRS):
            dc = dc + (dcq_ref[p] + dck_ref[p])
        dlf = _dot3l(tri_ref[...], dc) + carry_ref[0:1, :]
        carry_ref[0:1, :] = dlf[0:1, :]
        dz = jnp.where(lane < N_HEADS, dlf * _sigmoid(-zf_ref[...]), 0.0)
        dzf_ref[...] = dz.astype(BF16)
        dbf_ref[0:1, :] += jnp.sum(dz, axis=0, keepdims=True)

    return pl.pallas_call(
        body,
        out_shape=(jax.ShapeDtypeStruct((S, LANES), BF16), jax.ShapeDtypeStruct((8, LANES), F32)),
        grid=(n_i,),
        in_specs=[pl.BlockSpec((N_PAIRS, tm, LANES), lambda i: (0, n_i - 1 - i, 0)),
                  pl.BlockSpec((N_PAIRS, tm, LANES), lambda i: (0, n_i - 1 - i, 0)),
                  pl.BlockSpec((tm, LANES), lambda i: (n_i - 1 - i, 0)),
                  pl.BlockSpec((tm, tm), lambda i: (0, 0))],
        out_specs=(pl.BlockSpec((tm, LANES), lambda i: (n_i - 1 - i, 0)), pl.BlockSpec((8, LANES), lambda i: (0, 0))),
        scratch_shapes=[pltpu.VMEM((8, LANES), F32)],
        name="gate_bwd", compiler_params=_cp(("arbitrary",), VMEM_LIMIT),
    )(dcq, dck, zf, triu)


def _out_proj_fwd(att_bf, sg, w_out_bf, x, g_ffn):
    S = x.shape[0]
    tm = _pick(S, (512, 256))

    def body(a_ref, s_ref, w_ref, x_ref, g_ref, h_ref, hn_ref):
        h = (x_ref[...] + jnp.dot(a_ref[...], w_ref[:ATT_WIDTH, :], preferred_element_type=F32)
             + jnp.dot(s_ref[...], w_ref[ATT_WIDTH:, :], preferred_element_type=F32))
        h_ref[...] = h
        r = lax.rsqrt(jnp.mean(h * h, axis=-1, keepdims=True) + EPS)
        hn_ref[...] = ((h * r) * g_ref[...]).astype(BF16)

    row = pl.BlockSpec((tm, D_MODEL), lambda i: (i, 0))
    half = pl.BlockSpec((tm, ATT_WIDTH), lambda i: (i, 0))
    return pl.pallas_call(
        body, out_shape=(jax.ShapeDtypeStruct((S, D_MODEL), F32), jax.ShapeDtypeStruct((S, D_MODEL), BF16)),
        grid=(S // tm,),
        in_specs=[half, half, pl.BlockSpec((D_MODEL, D_MODEL), lambda i: (0, 0)), row,
                  pl.BlockSpec((1, D_MODEL), lambda i: (0, 0))],
        out_specs=(row, row), name="out_proj", compiler_params=_cp(("parallel",), VMEM_LIMIT),
    )(att_bf, sg, w_out_bf, x, g_ffn)


def _out_proj_dw(att_bf, sg, dh1_bf):
    S = att_bf.shape[0]
    tk = _pick(S, (1024, 512))

    def body(a_ref, s_ref, d_ref, o_ref):
        k = pl.program_id(0)

        @pl.when(k == 0)
        def _():
            o_ref[...] = jnp.zeros_like(o_ref)

        d = d_ref[...]
        o_ref[:ATT_WIDTH, :] += lax.dot_general(a_ref[...], d, _TN, preferred_element_type=F32)
        o_ref[ATT_WIDTH:, :] += lax.dot_general(s_ref[...], d, _TN, preferred_element_type=F32)

    half = pl.BlockSpec((tk, ATT_WIDTH), lambda k: (k, 0))
    return pl.pallas_call(
        body, out_shape=jax.ShapeDtypeStruct((D_MODEL, D_MODEL), F32), grid=(S // tk,),
        in_specs=[half, half, pl.BlockSpec((tk, D_MODEL), lambda k: (k, 0))],
        out_specs=pl.BlockSpec((D_MODEL, D_MODEL), lambda k: (0, 0)),
        name="out_proj_dw", compiler_params=_cp(("arbitrary",), VMEM_LIMIT),
    )(att_bf, sg, dh1_bf)


_IN_PIECES = ((0, ATT_WIDTH), (ATT_WIDTH, ATT_WIDTH), (2 * ATT_WIDTH, ATT_WIDTH), (QKV, 2 * GMLP_WIDTH), (UG_END, LANES))


def _inproj_bwd_dx(pieces, w_pad, x, g_mix, dh1):
    S = x.shape[0]
    tm = _pick(S, (512, 256))

    def body(*refs):
        p_refs, (w_ref, x_ref, g_ref, r_ref, dx_ref, dg_ref) = refs[:5], refs[5:]
        i = pl.program_id(0)

        @pl.when(i == 0)
        def _():
            dg_ref[...] = jnp.zeros_like(dg_ref)

        dxn = None
        for p_ref, (c0, width) in zip(p_refs, _IN_PIECES):
            part = jnp.dot(p_ref[...], w_ref[c0:c0 + width, :], preferred_element_type=F32)
            dxn = part if dxn is None else dxn + part
        xf = x_ref[...]
        r = lax.rsqrt(jnp.mean(xf * xf, axis=-1, keepdims=True) + EPS)
        xhat = xf * r
        dg_ref[0:1, :] += jnp.sum(dxn * xhat, axis=0, keepdims=True)
        dhat = dxn * g_ref[...]
        dx_ref[...] = r_ref[...] + r * (dhat - xhat * jnp.mean(dhat * xhat, axis=-1, keepdims=True))

    row = pl.BlockSpec((tm, D_MODEL), lambda i: (i, 0))
    return pl.pallas_call(
        body, out_shape=(jax.ShapeDtypeStruct((S, D_MODEL), F32), jax.ShapeDtypeStruct((8, D_MODEL), F32)),
        grid=(S // tm,),
        in_specs=[pl.BlockSpec((tm, width), lambda i: (i, 0)) for _, width in _IN_PIECES]
        + [pl.BlockSpec((IN_PAD, D_MODEL), lambda i: (0, 0)), row, pl.BlockSpec((1, D_MODEL), lambda i: (0, 0)), row],
        out_specs=(row, pl.BlockSpec((8, D_MODEL), lambda i: (0, 0))),
        name="in_proj_dx", compiler_params=_cp(("arbitrary",), VMEM_LIMIT),
    )(*pieces, w_pad, x, g_mix, dh1)


def _inproj_bwd_dw(xn, pieces):
    S = xn.shape[0]
    tk = _pick(S, (1024, 512))

    def body(*refs):
        x_ref, p_refs, o_ref = refs[0], refs[1:6], refs[6]
        k = pl.program_id(0)

        @pl.when(k == 0)
        def _():
            o_ref[...] = jnp.zeros_like(o_ref)

        xb = x_ref[...]
        for p_ref, (c0, width) in zip(p_refs, _IN_PIECES):
            o_ref[:, c0:c0 + width] += lax.dot_general(xb, p_ref[...], _TN, preferred_element_type=F32)

    return pl.pallas_call(
        body, out_shape=jax.ShapeDtypeStruct((D_MODEL, IN_PAD), F32), grid=(S // tk,),
        in_specs=[pl.BlockSpec((tk, D_MODEL), lambda k: (k, 0))]
        + [pl.BlockSpec((tk, width), lambda k: (k, 0)) for _, width in _IN_PIECES],
        out_specs=pl.BlockSpec((D_MODEL, IN_PAD), lambda k: (0, 0)),
        name="in_proj_dw", compiler_params=_cp(("arbitrary",), VMEM_LIMIT),
    )(xn, *pieces)


def _adamw(w, m, v, parts, name):
    R, C = w.shape[-2:]
    tr = R
    for cand in (256, 128, 64, 32, 16, 8):
        if R % cand == 0 and R > cand:
            tr = cand
            break
    c1 = 1.0 / (1.0 - ADAM_B1 ** ADAM_STEP)
    c2 = 1.0 / (1.0 - ADAM_B2 ** ADAM_STEP)

    def body(w_ref, m_ref, v_ref, p_ref, g_ref, d_ref, nm_ref, nv_ref):
        g = p_ref[0].astype(F32)
        for j in range(1, N_DEV):
            g = g + p_ref[j].astype(F32)
        g_ref[...] = g
        nm = ADAM_B1 * m_ref[...] + (1.0 - ADAM_B1) * g
        nv = ADAM_B2 * v_ref[...] + (1.0 - ADAM_B2) * (g * g)
        nm_ref[...] = nm
        nv_ref[...] = nv
        d_ref[...] = -ADAM_LR * ((nm * c1) / (jnp.sqrt(nv * c2) + ADAM_EPS) + ADAM_WD * w_ref[...])

    if w.ndim == 3:
        spec = pl.BlockSpec((None, tr, C), lambda i: (0, i, 0))
    else:
        spec = pl.BlockSpec((tr, C), lambda i: (i, 0))
    shp = jax.ShapeDtypeStruct(w.shape, F32)
    return pl.pallas_call(
        body, out_shape=(shp, shp, shp, shp), grid=(R // tr,),
        in_specs=[spec, spec, spec, pl.BlockSpec((N_DEV, tr, C), lambda i: (0, i, 0))],
        out_specs=(spec, spec, spec, spec),
        name=name, compiler_params=_cp(("parallel",), VMEM_LIMIT),
    )(w, m, v, parts)


def _place():
    x, y, c = lax.axis_index("x"), lax.axis_index("y"), lax.axis_index("c")
    return x, y, c


def _all_gather(blocks, name):
    n = len(blocks)

    def body(*refs):
        ins, outs = refs[:n], refs[n:2 * n]
        send_sems, recv_sems, local_sems = refs[2 * n:]
        x, y, c = _place()
        me, sibling = (x, y, c), (x, y, 1 - c)
        chips = [(1 - x, y), (x, 1 - y), (1 - x, 1 - y)]
        sends = []
        for a in range(n):
            out = outs[a]

            def slot(px, py, pc, out=out):
                return out.at[4 * px + 2 * py + pc]

            def copy(k, block, to, src=None, a=a, slot=slot):
                return pltpu.make_async_remote_copy(
                    src_ref=slot(*block) if src is None else src, dst_ref=slot(*block),
                    send_sem=send_sems.at[a, k], recv_sem=recv_sems.at[a, k], device_id=to, device_id_type=MESH)

            mine = pltpu.make_async_copy(ins[a], slot(*me), local_sems.at[a])
            mine.start()
            first = [copy(0, me, sibling, src=ins[a])]
            first += [copy(1 + j, me, (*chip, c), src=ins[a]) for j, chip in enumerate(chips)]
            for cp in first:
                cp.start()
            sends.append((mine, first, copy))
        for a in range(n):
            mine, first, copy = sends[a]
            passed = [copy(4 + j, (*chip, c), sibling) for j, chip in enumerate(chips)]
            for j, chip in enumerate(chips):
                copy(1 + j, (*chip, c), me).wait_recv()
                passed[j].start()
            copy(0, sibling, me).wait_recv()
            for j, chip in enumerate(chips):
                copy(4 + j, (*chip, 1 - c), me).wait_recv()
            for cp in first + passed:
                cp.wait_send()
            mine.wait()

    any_spec = pl.BlockSpec(memory_space=pl.ANY)
    return pl.pallas_call(
        body, out_shape=tuple(jax.ShapeDtypeStruct((N_DEV,) + b.shape, b.dtype) for b in blocks),
        in_specs=[any_spec] * n, out_specs=tuple([any_spec] * n),
        scratch_shapes=[pltpu.SemaphoreType.DMA((n, 7)), pltpu.SemaphoreType.DMA((n, 7)), pltpu.SemaphoreType.DMA((n,))],
        name=name,
    )(*blocks)


_HBM = pl.BlockSpec(memory_space=pltpu.HBM)
_SEM = pl.BlockSpec(memory_space=pltpu.SEMAPHORE)
_EFFECT = pltpu.SideEffectType.DATAFLOW_SIDE_EFFECTING


def _peers(x, y, c):
    out = []
    for k in range(1, N_DEV):
        px, py, pc = x ^ ((k >> 2) & 1), y ^ ((k >> 1) & 1), c ^ (k & 1)
        out.append((k, (px, py, pc), 4 * px + 2 * py + pc))
    return out


def _xchg_copies(src_refs, land_refs, send_sems, recv_sems, scatter):
    x, y, c = _place()
    me = 4 * x + 2 * y + c
    copies = []
    for a, (src, land) in enumerate(zip(src_refs, land_refs)):
        for k, place, idx in _peers(x, y, c):
            j = a * (N_DEV - 1) + k - 1
            copies.append(pltpu.make_async_remote_copy(
                src_ref=src.at[idx] if scatter[a] else src, dst_ref=land.at[me],
                send_sem=send_sems[j], recv_sem=recv_sems[j], device_id=place, device_id_type=MESH))
    return copies


def _xchg_start(srcs, scatter, name):
    n = len(srcs)
    lands = [lax.empty((N_DEV,) + (s.shape[1:] if sc else s.shape), s.dtype) for s, sc in zip(srcs, scatter)]

    ns = n * (N_DEV - 1)

    def body(*refs):
        sems = refs[2 * n:2 * n + 2 * ns]
        for cp in _xchg_copies(refs[:n], refs[n:2 * n], sems[:ns], sems[ns:], scatter):
            cp.start()
        token = refs[-1]
        token[...] = jnp.zeros_like(token)

    both = list(srcs) + lands
    res = pl.pallas_call(
        body, name=name,
        out_shape=(*[pltpu.SemaphoreType.DMA(())] * (2 * ns),
                   *[pltpu.HBM(a.shape, a.dtype) for a in both], jax.ShapeDtypeStruct((8, LANES), F32)),
        in_specs=[_HBM] * (2 * n),
        out_specs=(*([_SEM] * (2 * ns)), *([_HBM] * (2 * n)), pl.BlockSpec(memory_space=pltpu.VMEM)),
        input_output_aliases={i: 2 * ns + i for i in range(2 * n)},
        compiler_params=pltpu.CompilerParams(has_side_effects=_EFFECT),
    )(*[pltpu.with_memory_space_constraint(a, pltpu.HBM) for a in both])
    return (tuple(res[:2 * ns]), tuple(res[2 * ns:2 * ns + 2 * n])), res[-1]


def _xchg_wait(handle, scatter, after, name):
    sems, thru = handle
    n = len(thru) // 2
    ns = len(sems) // 2

    def body(*refs):
        got = refs[2 * n:2 * n + 2 * ns]
        for cp in _xchg_copies(refs[:n], refs[n:2 * n], got[:ns], got[ns:], scatter):
            cp.wait_send()
            cp.wait_recv()

    outs = pl.pallas_call(
        body, name=name, out_shape=tuple(pltpu.HBM(a.shape, a.dtype) for a in thru),
        in_specs=[_HBM] * (2 * n) + [_SEM] * (2 * ns) + [pl.BlockSpec(memory_space=pl.ANY)],
        out_specs=tuple([_HBM] * (2 * n)), input_output_aliases={i: i for i in range(2 * n)},
        compiler_params=pltpu.CompilerParams(has_side_effects=_EFFECT),
    )(*thru, *sems, after)
    return outs[:n], outs[n:]


def _tie(a, token):
    return a if token is None else a + token[0, 0].astype(a.dtype)


def _rows128(a):
    flat = a.reshape(-1)
    rows = -(-flat.shape[0] // LANES)
    rows = -(-rows // 8) * 8
    return jnp.pad(flat, (0, rows * LANES - flat.shape[0])).reshape(rows, LANES)


def _local_step(x, target, norm_mix_g, w_in_t, b_forget, gmlp_norm_g, w_spatial, b_spatial, norm_ffn_g, conv_b,
                norm_final_g, rest_fn, send_fn, small_fn, token=None):
    f = D_FF
    g_mix = norm_mix_g.reshape(1, D_MODEL)
    w_pad = jnp.pad(w_in_t, ((0, IN_PAD - IN_COLS), (0, 0)))
    bf_pad = jnp.pad(b_forget.reshape(1, N_HEADS), ((0, 0), (0, LANES - N_HEADS)))
    xn, qa, ka, va, ug, zf = _inproj_fwd(x, _tie(g_mix, token), w_pad, bf_pad)
    bias_full = jnp.repeat(b_spatial.reshape(N_GROUPS, CHUNK).T, GROUP_DIM, axis=1)
    w_s = w_spatial.reshape(N_GROUPS, CHUNK, CHUNK)
    gain = gmlp_norm_g.reshape(1, GMLP_WIDTH)
    sg = _gmlp_fwd(ug, gain, w_s, bias_full)
    att, lse, att_bf = _attn_fwd(qa, ka, va)
    w_out_bf, w_up_bf, conv_w, w_down_bf = rest_fn(att_bf)
    g_ffn = norm_ffn_g.reshape(1, D_MODEL)
    h1, hn = _out_proj_fwd(att_bf, sg, w_out_bf, x, g_ffn)
    cw = jnp.pad(conv_w.reshape(3, 2, f).transpose(1, 0, 2), ((0, 0), (0, 5), (0, 0)))
    cb = conv_b.reshape(2, 1, f)
    hu, hc, act = _ffn_up_conv(hn, w_up_bf, cw, cb)
    loss_blk, dh2, dh2_bf, dg_final = _ffn_down_loss(act, w_down_bf, h1, norm_final_g.reshape(1, D_MODEL), target)
    dw_down = _mm(act, dh2_bf, mode="tn", out_dtype=F32, tm=1408, tn=1024, tk=2048, name="ffn_down_dw")
    dact = _mm(dh2_bf, w_down_bf, mode="nt", out_dtype=F32, tm=1024, tn=1408, tk=1024, outer="j", name="ffn_down_dx")
    dhu, dcw = _conv_gate_bwd(hc, hu, dact, _tie(cw, send_fn("w_down", dw_down)))
    dw_up = _mm(hn, dhu, mode="tn", out_dtype=F32, tm=1024, tn=1408, tk=2048, b_halves=True, outer="j", name="ffn_up_dw")
    dh1, dh1_bf, dg_ffn = _ffn_up_dx_rms(dhu, w_up_bf, h1, _tie(g_ffn, send_fn("w_up", dw_up)), dh2)
    dsg, qb, doa = _out_proj_dx_prep(dh1_bf, w_out_bf, att, lse, qa)
    dw_out = _out_proj_dw(att_bf, sg, dh1_bf)
    dq, dk, dv, dcq, dck = _attn_bwd(qb, ka, va, doa)
    wt_s = w_s.transpose(0, 2, 1)
    dug, dw_s, dgain, dbias = _gmlp_bwd(ug, dsg, _tie(gain, send_fn("w_out", dw_out)), w_s, wt_s, bias_full)
    dzf, dbf = _gate_bwd(dcq, dck, zf)
    grad_x, dg_mix = _inproj_bwd_dx((dq, dk, dv, dug, dzf), w_pad, x, g_mix, dh1)
    grads = dict(
        norm_mix_g=dg_mix[0:1, :],
        b_forget=dbf[0:1, :N_HEADS],
        gmlp_norm_g=dgain[0:1, :],
        w_spatial=dw_s,
        b_spatial=dbias[:, ::GROUP_DIM].T,
        norm_ffn_g=dg_ffn[0:1, :],
        conv_w=dcw[:, 0:3, :].transpose(1, 0, 2).reshape(3, 2 * f),
        conv_b=dcw[:, 3, :].reshape(1, 2 * f),
        norm_final_g=dg_final[0, :],
    )
    token = small_fn(loss_blk[0, 0], grads)
    dw_in = _inproj_bwd_dw(xn, (dq, dk, dv, dug, _tie(dzf, token)))
    return grad_x, send_fn("w_in", dw_in[:, :IN_COLS])


SMALL = ("norm_mix_g", "b_forget", "gmlp_norm_g", "w_spatial", "b_spatial", "norm_ffn_g", "conv_b", "norm_final_g")


def kernel(x, norm_mix_g, w_in, b_forget, gmlp_norm_g, w_spatial, b_spatial, w_out, norm_ffn_g, w_up, conv_w, conv_b, w_down, norm_final_g, loss_target, m_norm_mix_g, m_w_in, m_b_forget, m_gmlp_norm_g, m_w_spatial, m_b_spatial, m_w_out, m_norm_ffn_g, m_w_up, m_conv_w, m_conv_b, m_w_down, m_norm_final_g, v_norm_mix_g, v_w_in, v_b_forget, v_gmlp_norm_g, v_w_spatial, v_b_spatial, v_w_out, v_norm_ffn_g, v_w_up, v_conv_w, v_conv_b, v_w_down, v_norm_final_g):
    weights = dict(norm_mix_g=norm_mix_g, w_in=w_in, b_forget=b_forget, gmlp_norm_g=gmlp_norm_g, w_spatial=w_spatial,
                   b_spatial=b_spatial, w_out=w_out, norm_ffn_g=norm_ffn_g, w_up=w_up, conv_w=conv_w, conv_b=conv_b,
                   w_down=w_down, norm_final_g=norm_final_g)
    m_in = dict(norm_mix_g=m_norm_mix_g, w_in=m_w_in, b_forget=m_b_forget, gmlp_norm_g=m_gmlp_norm_g,
                w_spatial=m_w_spatial, b_spatial=m_b_spatial, w_out=m_w_out, norm_ffn_g=m_norm_ffn_g, w_up=m_w_up,
                conv_w=m_conv_w, conv_b=m_conv_b, w_down=m_w_down, norm_final_g=m_norm_final_g)
    v_in = dict(norm_mix_g=v_norm_mix_g, w_in=v_w_in, b_forget=v_b_forget, gmlp_norm_g=v_gmlp_norm_g,
                w_spatial=v_w_spatial, b_spatial=v_b_spatial, w_out=v_w_out, norm_ffn_g=v_norm_ffn_g, w_up=v_w_up,
                conv_w=v_conv_w, conv_b=v_conv_b, w_down=v_w_down, norm_final_g=v_norm_final_g)
    order = list(weights)
    me = 4 * lax.axis_index("x") + 2 * lax.axis_index("y") + lax.axis_index("c")
    n_in, n_up = w_in.shape[2], w_up.shape[2]
    r_out, r_down = w_out.shape[1], w_down.shape[1]

    def with_mine(landed, mine):
        return lax.dynamic_update_index_in_dim(landed, mine, me, 0)

    up_blk = w_up[0].T.astype(BF16)
    out_blk = w_out[0].astype(BF16)
    down_blk = w_down[0].astype(BF16)
    taps_blk = jnp.pad(conv_w[0], ((0, 5), (0, 0)))
    (in_all,) = _all_gather([w_in[0].T.astype(BF16)], "gather_w_in")
    in_all, rest_blocks = lax.optimization_barrier((in_all, [up_blk, out_blk, down_blk, taps_blk]))
    rest_handle, token = _xchg_start(rest_blocks, [False] * 4, "gather_rest_start")
    w_in_t = in_all.reshape(N_DEV * n_in, D_MODEL)

    def rest_fn(after):
        mine, landed = _xchg_wait(rest_handle, [False] * 4, after, "gather_rest_wait")
        up_all, out_all, down_all, taps_all = [with_mine(l, b) for l, b in zip(landed, mine)]
        return (out_all.reshape(N_DEV * r_out, D_MODEL), up_all.reshape(N_DEV * n_up, D_MODEL),
                taps_all[:, :3, :].transpose(1, 0, 2).reshape(3, N_DEV * n_up),
                down_all.reshape(N_DEV * r_down, D_MODEL))

    sent = {}

    def send_fn(name, grad):
        if name == "w_in":
            parts = grad.reshape(D_MODEL, N_DEV, -1).transpose(1, 0, 2).astype(BF16)
        elif name == "w_up":
            parts = grad.reshape(D_MODEL, N_DEV, -1).transpose(1, 0, 2)
        else:
            parts = grad.reshape(N_DEV, -1, D_MODEL)
        sent[name], tok = _xchg_start([parts], [True], "scatter_" + name + "_start")
        return tok

    small = {}

    def small_fn(loss_local, g):
        loss_rows = jnp.pad(loss_local.reshape(1, 1), ((0, 31), (0, LANES - 1)))
        packed = [_rows128(g[k]) for k in SMALL] + [loss_rows, _rows128(g["conv_w"])]
        small["sizes"] = [p.shape[0] for p in packed]
        small["handle"], tok = _xchg_start([jnp.concatenate(packed, axis=0)], [False], "gather_small_start")
        return tok

    grad_x, after = _local_step(
        x[0], loss_target[0], norm_mix_g, w_in_t, b_forget, gmlp_norm_g, w_spatial, b_spatial, norm_ffn_g, conv_b,
        norm_final_g, rest_fn, send_fn, small_fn, token)

    outs = {}

    def update_big(name, after):
        (parts,), (landed,) = _xchg_wait(sent[name], [True], after, "scatter_" + name + "_wait")
        got = with_mine(landed, lax.dynamic_index_in_dim(parts, me, 0, keepdims=False))
        outs[name] = tuple(_adamw(weights[name], m_in[name], v_in[name], got, "adamw_" + name))
        return outs[name][0]

    for name in ("w_down", "w_up", "w_out"):
        after = update_big(name, after)

    (mine,), (landed,) = _xchg_wait(small["handle"], [False], after, "gather_small_wait")
    small_all = with_mine(landed, mine)
    sizes = small["sizes"]
    n_small_rows = sum(sizes[:-2])

    def pack(src):
        return jnp.concatenate([_rows128(src[k]) for k in SMALL] + [jnp.zeros((sizes[-2], LANES), F32)], axis=0)

    n_adam_rows = n_small_rows + sizes[-2]
    sg_, sd_, sm_, sv_ = _adamw(pack(weights), pack(m_in), pack(v_in), small_all[:, :n_adam_rows, :], "adamw_small")
    loss = sg_[n_small_rows, 0]
    off = 0
    for k, rows in zip(SMALL, sizes[:-2]):
        shp = weights[k].shape
        cnt = math.prod(shp)
        outs[k] = tuple(a[off:off + rows].reshape(-1)[:cnt].reshape(shp) for a in (sg_, sd_, sm_, sv_))
        off += rows
    taps_parts = small_all[:, n_adam_rows:, :].reshape(N_DEV, -1)[:, :3 * N_DEV * n_up].reshape(N_DEV, 3, N_DEV * n_up)
    taps_mine = lax.dynamic_slice_in_dim(taps_parts, me * n_up, n_up, axis=2)
    taps_mine = jnp.pad(taps_mine, ((0, 0), (0, 5), (0, 0)))

    def pad8(a):
        return jnp.pad(a[0], ((0, 5), (0, 0)))

    res = _adamw(pad8(conv_w), pad8(m_conv_w), pad8(v_conv_w), taps_mine, "adamw_conv_w")
    outs["conv_w"] = tuple(a[:3][None] for a in res)
    update_big("w_in", sg_)

    return (loss, grad_x[None], *[outs[k][0] for k in order], *[outs[k][1] for k in order],
            *[outs[k][2] for k in order], *[outs[k][3] for k in order])
```

```python
import functools
import math

import jax
import jax.numpy as jnp
from jax import lax
from jax.experimental import pallas as pl
from jax.experimental.pallas import tpu as pltpu

F32 = jnp.float32
BF16 = jnp.bfloat16

N_DEV = 8
D_MODEL = 1024
ATT_WIDTH = 512
GMLP_WIDTH = 512
HEAD_DIM = 64
N_HEADS = 8
N_PAIRS = 4
N_GROUPS = 8
GROUP_DIM = 64
CHUNK = 128
D_FF = 2816
IN_COLS = 2568
IN_PAD = 2688
QKV = 1536
UG_END = 2560
EPS = 1e-6
LANES = 128

ADAM_LR = 0.001
ADAM_B1 = 0.9
ADAM_B2 = 0.999
ADAM_EPS = 1e-08
ADAM_WD = 0.01
ADAM_STEP = 10

ATT_TQ = 1024
ATT_TK = 1024
FFN_TM, FFN_TN = 512, 1408
CONV_TM, CONV_TN = 512, 1408
VMEM_LIMIT = 56 * 1024 * 1024
MESH = pl.DeviceIdType.MESH


def _cp(sem, vmem=None):
    return pltpu.CompilerParams(dimension_semantics=sem, vmem_limit_bytes=vmem)


def _pick(n, prefs):
    for p in prefs:
        if n % p == 0:
            return p
    return n


def _split3(x):
    hi = x.astype(BF16)
    r1 = x - hi.astype(F32)
    mid = r1.astype(BF16)
    lo = (r1 - mid.astype(F32)).astype(BF16)
    return hi, mid, lo


def _dot3(x, ones_bf):
    d = functools.partial(jnp.dot, preferred_element_type=F32)
    out = []
    for c in range(0, x.shape[1], 2 * LANES):
        blk = ones_bf[c:c + 2 * LANES, c:c + 2 * LANES]
        hi, mid, lo = _split3(x[:, c:c + 2 * LANES])
        out.append(d(hi, blk) + d(mid, blk) + d(lo, blk))
    return jnp.concatenate(out, axis=1)


def _dot3l(ones_bf, x):
    n = x.shape[1]
    y = jnp.dot(ones_bf, jnp.concatenate(_split3(x), axis=1), preferred_element_type=F32)
    return y[:, :n] + y[:, n:2 * n] + y[:, 2 * n:]


def _gelu(x):
    k = math.sqrt(2.0 / math.pi)
    t = jnp.tanh(k * (x + 0.044715 * (x * x * x)))
    return 0.5 * x * (1.0 + t)


def _gelu_grad(x):
    k = math.sqrt(2.0 / math.pi)
    x2 = x * x
    t = jnp.tanh(k * (x + 0.044715 * (x2 * x)))
    return 0.5 * (1.0 + t) + 0.5 * x * (1.0 - t * t) * (k * (1.0 + 3.0 * 0.044715 * x2))


def _sigmoid(x):
    return 1.0 / (1.0 + jnp.exp(-x))


def _mm(a, b, *, mode, out_dtype, tm, tn, tk, name, res=None, a_halves=False, b_halves=False,
        out_halves=False, outer="i"):
    if mode == "tn":
        K, M = a.shape[-2], a.shape[-1] * (2 if a_halves else 1)
    else:
        M, K = a.shape[-2], a.shape[-1] * (2 if a_halves else 1)
    if mode == "nt":
        N = b.shape[-2]
        assert b.shape[-1] == K
    else:
        N = b.shape[-1] * (2 if b_halves else 1)
    tm, tn, tk = min(tm, M), min(tn, N), min(tk, K)
    assert M % tm == 0 and N % tn == 0 and K % tk == 0, (name, M, N, K, tm, tn, tk)
    nm, nn, nk = M // tm, N // tn, K // tk

    def ij(g0, g1):
        return (g0, g1) if outer == "i" else (g1, g0)

    if mode == "nn":
        dims = (((1,), (0,)), ((), ()))
        if a_halves:
            nkh = nk // 2
            a_spec = pl.BlockSpec((None, tm, tk), lambda g0, g1, k: (k // nkh, ij(g0, g1)[0], k % nkh))
        else:
            a_spec = pl.BlockSpec((tm, tk), lambda g0, g1, k: (ij(g0, g1)[0], k))
        b_spec = pl.BlockSpec((tk, tn), lambda g0, g1, k: (k, ij(g0, g1)[1]))
    elif mode == "nt":
        dims = (((1,), (1,)), ((), ()))
        if a_halves:
            nkh = nk // 2
            a_spec = pl.BlockSpec((None, tm, tk), lambda g0, g1, k: (k // nkh, ij(g0, g1)[0], k % nkh))
        else:
            a_spec = pl.BlockSpec((tm, tk), lambda g0, g1, k: (ij(g0, g1)[0], k))
        b_spec = pl.BlockSpec((tn, tk), lambda g0, g1, k: (ij(g0, g1)[1], k))
    else:
        dims = (((0,), (0,)), ((), ()))
        if a_halves:
            nmh = nm // 2
            a_spec = pl.BlockSpec((None, tk, tm), lambda g0, g1, k: (ij(g0, g1)[0] // nmh, k, ij(g0, g1)[0] % nmh))
        else:
            a_spec = pl.BlockSpec((tk, tm), lambda g0, g1, k: (k, ij(g0, g1)[0]))
        if b_halves:
            nnh = nn // 2
            b_spec = pl.BlockSpec((None, tk, tn), lambda g0, g1, k: (ij(g0, g1)[1] // nnh, k, ij(g0, g1)[1] % nnh))
        else:
            b_spec = pl.BlockSpec((tk, tn), lambda g0, g1, k: (k, ij(g0, g1)[1]))
    if out_halves:
        nnh = nn // 2
        o_spec = pl.BlockSpec((None, tm, tn), lambda g0, g1, k: (ij(g0, g1)[1] // nnh, ij(g0, g1)[0], ij(g0, g1)[1] % nnh))
        o_shape = jax.ShapeDtypeStruct((2, M, N // 2), out_dtype)
    else:
        o_spec = pl.BlockSpec((tm, tn), lambda g0, g1, k: ij(g0, g1))
        o_shape = jax.ShapeDtypeStruct((M, N), out_dtype)
    in_specs = [a_spec, b_spec]
    args = [a, b]
    if res is not None:
        in_specs.append(pl.BlockSpec((tm, tn), lambda g0, g1, k: ij(g0, g1)))
        args.append(res)

    def body(*refs):
        if res is not None:
            a_ref, b_ref, r_ref, o_ref = refs[:4]
        else:
            a_ref, b_ref, o_ref = refs[:3]
            r_ref = None
        part = lax.dot_general(a_ref[...], b_ref[...], dims, preferred_element_type=F32)
        if nk == 1:
            if r_ref is not None:
                part = part + r_ref[...]
            o_ref[...] = part.astype(out_dtype)
            return
        acc_ref = refs[-1]
        k = pl.program_id(2)

        @pl.when(k == 0)
        def _():
            acc_ref[...] = part

        @pl.when(k > 0)
        def _():
            acc_ref[...] += part

        @pl.when(k == nk - 1)
        def _():
            tot = acc_ref[...]
            if r_ref is not None:
                tot = tot + r_ref[...]
            o_ref[...] = tot.astype(out_dtype)

    grid = (nm, nn, nk) if outer == "i" else (nn, nm, nk)
    scratch = [] if nk == 1 else [pltpu.VMEM((tm, tn), F32)]
    return pl.pallas_call(
        body, out_shape=o_shape, grid=grid, in_specs=in_specs, out_specs=o_spec, scratch_shapes=scratch,
        name=name, compiler_params=_cp(("parallel", "parallel", "arbitrary"), VMEM_LIMIT),
    )(*args)


def _aug(lane, terms):
    out = 0.0
    for j, t in enumerate(terms):
        out = jnp.where(lane == HEAD_DIM + j, t, out)
    return out


def _split3f(x):
    hi, mid, lo = _split3(x)
    return [hi.astype(F32), mid.astype(F32), lo.astype(F32)]


def _inproj_fwd(x, g_mix, w_pad, bf_pad):
    S = x.shape[0]
    tm = _pick(S, (512, 256))
    tri = (lax.broadcasted_iota(jnp.int32, (tm, tm), 0) >= lax.broadcasted_iota(jnp.int32, (tm, tm), 1)).astype(BF16)

    def body(x_ref, g_ref, w_ref, bf_ref, tri_ref, put_ref, one_ref, xn_ref, qa_ref, ka_ref, va_ref, ug_ref, zf_ref,
             carry_ref):
        i = pl.program_id(0)

        @pl.when(i == 0)
        def _():
            carry_ref[...] = jnp.zeros_like(carry_ref)

        xf = x_ref[...]
        r = lax.rsqrt(jnp.mean(xf * xf, axis=-1, keepdims=True) + EPS)
        xn = ((xf * r) * g_ref[...]).astype(BF16)
        xn_ref[...] = xn
        proj = lax.dot_general(xn, w_ref[...], _NT, preferred_element_type=F32)
        ug_ref[...] = proj[:, QKV:UG_END]
        zf = proj[:, UG_END:] + bf_ref[...]
        zf_ref[...] = zf
        lf = jnp.minimum(zf, 0.0) - jnp.log(1.0 + jnp.exp(-jnp.abs(zf)))
        c = _dot3l(tri_ref[...], lf) + carry_ref[0:1, :]
        carry_ref[0:1, :] = c[tm - 1:tm, :]
        c3 = jnp.concatenate(_split3(c), axis=1)
        aug_q = jnp.dot(c3, put_ref[0], preferred_element_type=F32) + one_ref[0:1, :]
        aug_k = jnp.dot(c3, put_ref[1], preferred_element_type=F32) + one_ref[1:2, :]
        lane = lax.broadcasted_iota(jnp.int32, (tm, LANES), 1)
        for h in range(N_HEADS):
            p, odd = h // 2, h % 2

            def head(base, scale=None, p=p, odd=odd):
                blk = proj[:, base + p * LANES:base + (p + 1) * LANES]
                if scale is not None:
                    blk = blk * scale
                return pltpu.roll(blk, HEAD_DIM, 1) if odd else blk

            cols = slice(h * LANES, (h + 1) * LANES)
            qa_ref[:, cols] = jnp.where(lane < HEAD_DIM, head(0, HEAD_DIM ** -0.5), aug_q[:, cols]).astype(BF16)
            ka_ref[:, cols] = jnp.where(lane < HEAD_DIM, head(ATT_WIDTH), aug_k[:, cols]).astype(BF16)
            va_ref[:, cols] = jnp.where(lane < HEAD_DIM, head(2 * ATT_WIDTH), one_ref[2:3, cols]).astype(BF16)

    wide = N_HEADS * LANES
    src = lax.broadcasted_iota(jnp.int32, (3 * LANES, wide), 0)
    col = lax.broadcasted_iota(jnp.int32, (3 * LANES, wide), 1)
    hd, term = src % LANES, src // LANES
    to_q = (col == hd * LANES + HEAD_DIM + term) & (hd < N_HEADS)
    to_k = (col == hd * LANES + HEAD_DIM + 3 + term) & (hd < N_HEADS)
    put = jnp.stack([to_q.astype(BF16), -to_k.astype(BF16)])
    off = lax.broadcasted_iota(jnp.int32, (8, wide), 1) % LANES - HEAD_DIM
    row = lax.broadcasted_iota(jnp.int32, (8, wide), 0)
    q_one = (off >= 3) & (off < 6)
    k_one = ((off >= 0) & (off < 3)) | ((off >= 6) & (off < 9))
    v_one = (off >= 0) & (off < 3)
    ones = jnp.where(row == 0, q_one, jnp.where(row == 1, k_one, (row == 2) & v_one)).astype(F32)
    return pl.pallas_call(
        body,
        out_shape=(jax.ShapeDtypeStruct((S, D_MODEL), BF16), jax.ShapeDtypeStruct((S, wide), BF16),
                   jax.ShapeDtypeStruct((S, wide), BF16), jax.ShapeDtypeStruct((S, wide), BF16),
                   jax.ShapeDtypeStruct((S, 2 * GMLP_WIDTH), F32), jax.ShapeDtypeStruct((S, LANES), F32)),
        grid=(S // tm,),
        in_specs=[pl.BlockSpec((tm, D_MODEL), lambda i: (i, 0)), pl.BlockSpec((1, D_MODEL), lambda i: (0, 0)),
                  pl.BlockSpec((IN_PAD, D_MODEL), lambda i: (0, 0)), pl.BlockSpec((1, LANES), lambda i: (0, 0)),
                  pl.BlockSpec((tm, tm), lambda i: (0, 0)), pl.BlockSpec((2, 3 * LANES, wide), lambda i: (0, 0, 0)),
                  pl.BlockSpec((8, wide), lambda i: (0, 0))],
        out_specs=(pl.BlockSpec((tm, D_MODEL), lambda i: (i, 0)), pl.BlockSpec((tm, wide), lambda i: (i, 0)),
                   pl.BlockSpec((tm, wide), lambda i: (i, 0)), pl.BlockSpec((tm, wide), lambda i: (i, 0)),
                   pl.BlockSpec((tm, 2 * GMLP_WIDTH), lambda i: (i, 0)), pl.BlockSpec((tm, LANES), lambda i: (i, 0))),
        scratch_shapes=[pltpu.VMEM((8, LANES), F32)],
        name="inproj_fwd", compiler_params=_cp(("arbitrary",), VMEM_LIMIT),
    )(x, g_mix, w_pad, bf_pad, tri, put, ones)


def _group_ones():
    r = lax.broadcasted_iota(jnp.int32, (GMLP_WIDTH, GMLP_WIDTH), 0) // GROUP_DIM
    c = lax.broadcasted_iota(jnp.int32, (GMLP_WIDTH, GMLP_WIDTH), 1) // GROUP_DIM
    return (r == c).astype(BF16)


def _gmlp_mixed(vn_bf, w_ref, bias, n_chunks):
    lane = lax.broadcasted_iota(jnp.int32, (CHUNK, LANES), 1)
    row = lax.broadcasted_iota(jnp.int32, (CHUNK, CHUNK), 0)
    col = lax.broadcasted_iota(jnp.int32, (CHUNK, CHUNK), 1)
    ws = [jnp.where(row >= col, w_ref[g], 0.0).astype(BF16) for g in range(N_GROUPS)]
    rows = []
    for ci in range(n_chunks):
        cols = []
        for pp in range(N_GROUPS // 2):
            v = vn_bf[ci * CHUNK:(ci + 1) * CHUNK, pp * LANES:(pp + 1) * LANES]
            v_lo = jnp.where(lane < GROUP_DIM, v, jnp.zeros_like(v))
            v_hi = jnp.where(lane >= GROUP_DIM, v, jnp.zeros_like(v))
            m = (jnp.dot(ws[2 * pp], v_lo, preferred_element_type=F32)
                 + jnp.dot(ws[2 * pp + 1], v_hi, preferred_element_type=F32))
            cols.append(m + bias[:, pp * LANES:(pp + 1) * LANES])
        rows.append(jnp.concatenate(cols, axis=1))
    return jnp.concatenate(rows, axis=0)


def _gmlp_fwd(ug, gain, w_s, bias_full):
    S = ug.shape[0]
    tm = _pick(S, (512, 256, 128))
    ones = _group_ones()

    def body(ug_ref, gain_ref, w_ref, bias_ref, ones_ref, sg_ref):
        u = _gelu(ug_ref[:, :GMLP_WIDTH])
        vr = _gelu(ug_ref[:, GMLP_WIDTH:])
        ms = _dot3(vr * vr, ones_ref[...]) * (1.0 / GROUP_DIM)
        vn = ((vr * lax.rsqrt(ms + EPS)) * gain_ref[...]).astype(BF16)
        mixed = _gmlp_mixed(vn, w_ref, bias_ref[...], tm // CHUNK)
        sg_ref[...] = (u * mixed).astype(BF16)

    return pl.pallas_call(
        body, out_shape=jax.ShapeDtypeStruct((S, GMLP_WIDTH), BF16), grid=(S // tm,),
        in_specs=[pl.BlockSpec((tm, 2 * GMLP_WIDTH), lambda i: (i, 0)), pl.BlockSpec((1, GMLP_WIDTH), lambda i: (0, 0)),
                  pl.BlockSpec((N_GROUPS, CHUNK, CHUNK), lambda i: (0, 0, 0)),
                  pl.BlockSpec((CHUNK, GMLP_WIDTH), lambda i: (0, 0)),
                  pl.BlockSpec((GMLP_WIDTH, GMLP_WIDTH), lambda i: (0, 0))],
        out_specs=pl.BlockSpec((tm, GMLP_WIDTH), lambda i: (i, 0)),
        name="gmlp_fwd", compiler_params=_cp(("parallel",), VMEM_LIMIT),
    )(ug, gain, w_s, bias_full, ones)


_NT = (((1,), (1,)), ((), ()))
_TN = (((0,), (0,)), ((), ()))


def _attn_fwd(qa, ka, va):
    S = qa.shape[0]
    tq = _pick(S, (ATT_TQ, 256))
    tk = min(ATT_TK, tq)
    nq = S // tq
    assert tq == tk, "the diagonal block is handled as one tq x tq tile"
    per_q = 1

    def body(q_ref, k_ref, v_ref, o_ref, lse_ref, ob_ref):
        qi = pl.program_id(1)
        lane = lax.broadcasted_iota(jnp.int32, (tq, LANES), 1)
        qs = [q_ref[:, :LANES], q_ref[:, LANES:]]

        def update(q, ks, k_len, h, m, acc, first_row):
            cols = slice(h * LANES, (h + 1) * LANES)
            s = lax.dot_general(q, k_ref[pl.ds(ks, k_len), cols], _NT, preferred_element_type=F32)
            if first_row is not None:
                rid = lax.broadcasted_iota(jnp.int32, s.shape, 0) + first_row
                s = jnp.where(rid >= lax.broadcasted_iota(jnp.int32, s.shape, 1), s, -jnp.inf)
            m_new = jnp.maximum(m, jnp.max(s, axis=-1, keepdims=True))
            p = jnp.exp(s - m_new).astype(BF16)
            acc = jnp.exp(m - m_new) * acc + jnp.dot(p, v_ref[pl.ds(ks, k_len), cols], preferred_element_type=F32)
            return m_new, acc

        def step(kb, carry):
            ks = pl.multiple_of(kb * tk, tk)
            return tuple(update(qs[h], ks, tk, h, *carry[h], None) for h in range(2))

        one = (jnp.full((tq, 1), -jnp.inf, F32), jnp.zeros((tq, LANES), F32))
        carry = lax.fori_loop(0, qi * per_q, step, (one, one))
        outs, lses = [], []
        strip = tq // 2
        diag = pl.multiple_of(qi * tq, tq)
        for h in range(2):
            ms, accs = [], []
            for r in range(2):
                rows = slice(r * strip, (r + 1) * strip)
                m, acc = update(qs[h][rows], diag, (r + 1) * strip, h, carry[h][0][rows], carry[h][1][rows], r * strip)
                ms.append(m)
                accs.append(acc)
            m, acc = jnp.concatenate(ms, axis=0), jnp.concatenate(accs, axis=0)
            l = acc[:, HEAD_DIM:HEAD_DIM + 1]
            outs.append(acc / l)
            lses.append(m + jnp.log(l))
        o = jnp.where(lane < HEAD_DIM, outs[0], pltpu.roll(outs[1], HEAD_DIM, 1))
        o_ref[...] = o
        ob_ref[...] = o.astype(BF16)
        lse_ref[...] = jnp.where(lane < HEAD_DIM, lses[0], lses[1])

    return pl.pallas_call(
        body,
        out_shape=(jax.ShapeDtypeStruct((S, ATT_WIDTH), F32), jax.ShapeDtypeStruct((S, ATT_WIDTH), F32),
                   jax.ShapeDtypeStruct((S, ATT_WIDTH), BF16)),
        grid=(N_PAIRS, nq),
        in_specs=[pl.BlockSpec((tq, 2 * LANES), lambda p, i: (i, p)),
                  pl.BlockSpec((S, 2 * LANES), lambda p, i: (0, p)),
                  pl.BlockSpec((S, 2 * LANES), lambda p, i: (0, p))],
        out_specs=(pl.BlockSpec((tq, LANES), lambda p, i: (i, p)), pl.BlockSpec((tq, LANES), lambda p, i: (i, p)),
                   pl.BlockSpec((tq, LANES), lambda p, i: (i, p))),
        name="attn_fwd", compiler_params=_cp(("parallel", "parallel"), VMEM_LIMIT),
    )(qa, ka, va)


def _shift_rows(x, prev, n):
    rid = lax.broadcasted_iota(jnp.int32, x.shape, 0)
    y = pltpu.roll(x, n, 0)
    if n == 1:
        return jnp.where(rid == 0, prev[7:8, :], y)
    return jnp.where(rid == 0, prev[6:7, :], jnp.where(rid == 1, prev[7:8, :], y))


def _shift_rows_up(x, nxt, n):
    rows = x.shape[0]
    rid = lax.broadcasted_iota(jnp.int32, x.shape, 0)
    y = pltpu.roll(x, rows - n, 0)
    if n == 1:
        return jnp.where(rid == rows - 1, nxt[0:1, :], y)
    return jnp.where(rid == rows - 2, nxt[0:1, :], jnp.where(rid == rows - 1, nxt[1:2, :], y))


def _conv3(cur, prev, w, b):
    return (w[0:1, :] * _shift_rows(cur, prev, 2) + w[1:2, :] * _shift_rows(cur, prev, 1)
            + w[2:3, :] * cur + b)


def _ffn_up_conv(hn, w_up_bf, cw, cb):
    S = hn.shape[0]
    F = D_FF
    tm = _pick(S, (FFN_TM, 256))
    tn = _pick(F, (FFN_TN, 256, 128))
    nj = F // tn

    def body(hn_ref, wa_ref, wg_ref, cw_ref, cb_ref, hu_ref, hc_ref, act_ref, tail_ref):
        i = pl.program_id(1)

        @pl.when(i == 0)
        def _():
            tail_ref[...] = jnp.zeros_like(tail_ref)

        hn_v = hn_ref[...]
        halves = []
        for h, w_ref in enumerate((wa_ref, wg_ref)):
            hu = lax.dot_general(hn_v, w_ref[...], _NT, preferred_element_type=F32)
            hu_ref[h] = hu.astype(BF16)
            hc = _conv3(hu, tail_ref[h], cw_ref[h], cb_ref[h])
            hc_ref[h] = hc
            halves.append(hc)
            tail_ref[h] = hu[tm - 8:, :]
        a, g = halves
        act_ref[...] = (g * _sigmoid(g) * a).astype(BF16)

    both = pl.BlockSpec((2, tm, tn), lambda j, i: (0, i, j))
    return pl.pallas_call(
        body, out_shape=(jax.ShapeDtypeStruct((2, S, F), BF16), jax.ShapeDtypeStruct((2, S, F), F32),
                         jax.ShapeDtypeStruct((S, F), BF16)),
        grid=(nj, S // tm),
        in_specs=[pl.BlockSpec((tm, D_MODEL), lambda j, i: (i, 0)),
                  pl.BlockSpec((tn, D_MODEL), lambda j, i: (j, 0)),
                  pl.BlockSpec((tn, D_MODEL), lambda j, i: (nj + j, 0)),
                  pl.BlockSpec((2, 8, tn), lambda j, i: (0, 0, j)),
                  pl.BlockSpec((2, 1, tn), lambda j, i: (0, 0, j))],
        out_specs=(both, both, pl.BlockSpec((tm, tn), lambda j, i: (i, j))),
        scratch_shapes=[pltpu.VMEM((2, 8, tn), F32)],
        name="ffn_up_conv", compiler_params=_cp(("parallel", "arbitrary"), VMEM_LIMIT),
    )(hn, w_up_bf, w_up_bf, cw, cb)


def _ffn_down_loss(act, w_down_bf, h1, g_final, target):
    S = h1.shape[0]
    tm = _pick(S, (512, 256))

    def body(a_ref, w_ref, h1_ref, g_ref, t_ref, loss_ref, dh_ref, dhb_ref, dg_ref):
        i = pl.program_id(0)

        @pl.when(i == 0)
        def _():
            loss_ref[...] = jnp.zeros_like(loss_ref)
            dg_ref[...] = jnp.zeros_like(dg_ref)

        hf = h1_ref[...] + jnp.dot(a_ref[...], w_ref[...], preferred_element_type=F32)
        g = g_ref[...]
        r = lax.rsqrt(jnp.mean(hf * hf, axis=-1, keepdims=True) + EPS)
        hhat = hf * r
        err = hhat * g - t_ref[...]
        loss_ref[...] += 0.5 * jnp.sum(jnp.mean(err * err, axis=-1, keepdims=True))
        dy = err * (1.0 / D_MODEL)
        dg_ref[0:1, :] += jnp.sum(dy * hhat, axis=0, keepdims=True)
        dhat = dy * g
        dh = r * (dhat - hhat * jnp.mean(dhat * hhat, axis=-1, keepdims=True))
        dh_ref[...] = dh
        dhb_ref[...] = dh.astype(BF16)

    row = pl.BlockSpec((tm, D_MODEL), lambda i: (i, 0))
    return pl.pallas_call(
        body,
        out_shape=(jax.ShapeDtypeStruct((8, LANES), F32), jax.ShapeDtypeStruct((S, D_MODEL), F32),
                   jax.ShapeDtypeStruct((S, D_MODEL), BF16), jax.ShapeDtypeStruct((8, D_MODEL), F32)),
        grid=(S // tm,),
        in_specs=[pl.BlockSpec((tm, D_FF), lambda i: (i, 0)), pl.BlockSpec((D_FF, D_MODEL), lambda i: (0, 0)), row,
                  pl.BlockSpec((1, D_MODEL), lambda i: (0, 0)), row],
        out_specs=(pl.BlockSpec((8, LANES), lambda i: (0, 0)), row, row, pl.BlockSpec((8, D_MODEL), lambda i: (0, 0))),
        name="ffn_down_loss", compiler_params=_cp(("arbitrary",), VMEM_LIMIT),
    )(act, w_down_bf, h1, g_final, target)


def _ffn_up_dx_rms(dhu, w_up_bf, h1, g_ffn, dh2):
    _, S, F = dhu.shape
    tm = _pick(S, (512, 256))

    def body(a_ref, b_ref, h_ref, g_ref, r_ref, dh_ref, dhb_ref, dg_ref):
        i = pl.program_id(0)

        @pl.when(i == 0)
        def _():
            dg_ref[...] = jnp.zeros_like(dg_ref)

        dyv = (jnp.dot(a_ref[0], b_ref[:F, :], preferred_element_type=F32)
               + jnp.dot(a_ref[1], b_ref[F:, :], preferred_element_type=F32))
        hf = h_ref[...]
        r = lax.rsqrt(jnp.mean(hf * hf, axis=-1, keepdims=True) + EPS)
        hhat = hf * r
        dg_ref[0:1, :] += jnp.sum(dyv * hhat, axis=0, keepdims=True)
        dhat = dyv * g_ref[...]
        dh = r_ref[...] + r * (dhat - hhat * jnp.mean(dhat * hhat, axis=-1, keepdims=True))
        dh_ref[...] = dh
        dhb_ref[...] = dh.astype(BF16)

    row = pl.BlockSpec((tm, D_MODEL), lambda i: (i, 0))
    return pl.pallas_call(
        body,
        out_shape=(jax.ShapeDtypeStruct((S, D_MODEL), F32), jax.ShapeDtypeStruct((S, D_MODEL), BF16),
                   jax.ShapeDtypeStruct((8, D_MODEL), F32)),
        grid=(S // tm,),
        in_specs=[pl.BlockSpec((2, tm, F), lambda i: (0, i, 0)), pl.BlockSpec((2 * F, D_MODEL), lambda i: (0, 0)),
                  row, pl.BlockSpec((1, D_MODEL), lambda i: (0, 0)), row],
        out_specs=(row, row, pl.BlockSpec((8, D_MODEL), lambda i: (0, 0))),
        name="ffn_up_dx_rms", compiler_params=_cp(("arbitrary",), VMEM_LIMIT),
    )(dhu, w_up_bf, h1, g_ffn, dh2)


def _conv_gate_bwd(hc, hu, dact, cw):
    _, S, F = hu.shape
    tm = _pick(S, (CONV_TM, 128))
    tn = _pick(F, (CONV_TN, 256, 128))
    r8 = tm // 8
    n_i = S // tm
    last8 = S // 8 - 1

    def body(hc_ref, hcn_ref, hu_ref, da_ref, dan_ref, w_ref, dhu_ref, dcw_ref):
        i = pl.program_id(1)

        @pl.when(i == 0)
        def _():
            dcw_ref[...] = jnp.zeros_like(dcw_ref)

        rid8 = lax.broadcasted_iota(jnp.int32, (8, tn), 0)

        def gate_grads(a, g, d):
            sg = _sigmoid(g)
            return d * (g * sg), d * a * (sg * (1.0 + g * (1.0 - sg)))

        dhc = gate_grads(hc_ref[0], hc_ref[1], da_ref[...])
        dhc_n = gate_grads(hcn_ref[0], hcn_ref[1], dan_ref[...])
        for h in range(2):
            w = w_ref[h]
            d = dhc[h]
            dn = jnp.where(i < n_i - 1, dhc_n[h], 0.0)
            u1 = _shift_rows_up(d, dn, 1)
            u2 = _shift_rows_up(d, dn, 2)
            dhu_ref[h] = (w[2:3, :] * d + w[1:2, :] * u1 + w[0:1, :] * u2).astype(BF16)
            x = hu_ref[h].astype(F32)
            t0, t1, t2, t3 = [jnp.sum(t, axis=0, keepdims=True) for t in (u2 * x, u1 * x, d * x, d)]
            dcw_ref[h] += jnp.where(rid8 == 0, t0, jnp.where(rid8 == 1, t1, jnp.where(rid8 == 2, t2, jnp.where(rid8 == 3, t3, 0.0))))

    cur = pl.BlockSpec((2, tm, tn), lambda j, i: (0, i, j))
    return pl.pallas_call(
        body,
        out_shape=(jax.ShapeDtypeStruct((2, S, F), BF16), jax.ShapeDtypeStruct((2, 8, F), F32)),
        grid=(F // tn, n_i),
        in_specs=[cur, pl.BlockSpec((2, 8, tn), lambda j, i: (0, jnp.minimum((i + 1) * r8, last8), j)), cur,
                  pl.BlockSpec((tm, tn), lambda j, i: (i, j)),
                  pl.BlockSpec((8, tn), lambda j, i: (jnp.minimum((i + 1) * r8, last8), j)),
                  pl.BlockSpec((2, 8, tn), lambda j, i: (0, 0, j))],
        out_specs=(cur, pl.BlockSpec((2, 8, tn), lambda j, i: (0, 0, j))),
        name="conv_gate_bwd", compiler_params=_cp(("parallel", "arbitrary"), VMEM_LIMIT),
    )(hc, hc, hu, dact, dact, cw)


def _out_proj_dx_prep(dh1_bf, w_out_bf, att, lse, qa):
    S = att.shape[0]
    tm = _pick(S, (256,))

    def body(dh_ref, w_ref, o_ref, lse_ref, q_ref, dsg_ref, qb_ref, doa_ref):
        lane = lax.broadcasted_iota(jnp.int32, (tm, LANES), 1)
        dh = dh_ref[...]
        dsg_ref[...] = lax.dot_general(dh, w_ref[ATT_WIDTH:, :], _NT, preferred_element_type=F32)
        datt = lax.dot_general(dh, w_ref[:ATT_WIDTH, :], _NT, preferred_element_type=F32)
        for p in range(N_PAIRS):
            pc = slice(p * LANES, (p + 1) * LANES)
            do = datt[:, pc]
            prod = o_ref[:, pc] * do
            for hh in range(2):
                sel = (lane >= HEAD_DIM) if hh else (lane < HEAD_DIM)
                delta = jnp.sum(jnp.where(sel, prod, 0.0), axis=-1, keepdims=True)
                dod = pltpu.roll(do, HEAD_DIM, 1) if hh else do
                cols = slice((2 * p + hh) * LANES, (2 * p + hh + 1) * LANES)
                doa_ref[:, cols] = jnp.where(lane < HEAD_DIM, dod, _aug(lane, _split3f(-delta))).astype(BF16)
                lcol = p * LANES + hh * HEAD_DIM
                l3 = _split3f(-lse_ref[:, lcol:lcol + 1])
                augl = jnp.where(lane == HEAD_DIM + 6, l3[0], jnp.where(lane == HEAD_DIM + 7, l3[1], l3[2])).astype(BF16)
                qb_ref[:, cols] = jnp.where((lane >= HEAD_DIM + 6) & (lane < HEAD_DIM + 9), augl, q_ref[:, cols])

    half = pl.BlockSpec((tm, ATT_WIDTH), lambda i: (i, 0))
    wide = pl.BlockSpec((tm, N_HEADS * LANES), lambda i: (i, 0))
    return pl.pallas_call(
        body,
        out_shape=(jax.ShapeDtypeStruct((S, GMLP_WIDTH), F32), jax.ShapeDtypeStruct(qa.shape, BF16),
                   jax.ShapeDtypeStruct(qa.shape, BF16)),
        grid=(S // tm,),
        in_specs=[pl.BlockSpec((tm, D_MODEL), lambda i: (i, 0)), pl.BlockSpec((D_MODEL, D_MODEL), lambda i: (0, 0)),
                  half, half, wide],
        out_specs=(half, wide, wide),
        name="out_proj_dx_prep", compiler_params=_cp(("parallel",), VMEM_LIMIT),
    )(dh1_bf, w_out_bf, att, lse, qa)


def _attn_bwd(qb, ka, va, doa):
    S = qb.shape[0]
    tk = _pick(S, (512, 256))
    tq = tk
    nq = S // tq

    def pair(a, scale=None):
        lane = lax.broadcasted_iota(jnp.int32, (a.shape[0], LANES), 1)
        out = jnp.where(lane < HEAD_DIM, a[:, :LANES], pltpu.roll(a[:, LANES:], HEAD_DIM, 1))
        return out if scale is None else out * scale

    def head_lanes(a, col, sign, first):
        lane = lax.broadcasted_iota(jnp.int32, (a.shape[0], LANES), 1)
        return jnp.where(lane == first, sign * a[:, col:col + 1],
                         jnp.where(lane == first + 1, sign * a[:, LANES + col:LANES + col + 1], 0.0))

    def body(q_ref, do_ref, k_ref, v_ref, dqc_ref, dkc_ref, dvc_ref, dcq_ref, dck_ref, dq_ref, dka_ref, dva_ref):
        kb = pl.program_id(1)

        @pl.when(kb == 0)
        def _():
            dq_ref[...] = jnp.zeros_like(dq_ref)

        dka_ref[...] = jnp.zeros_like(dka_ref)
        dva_ref[...] = jnp.zeros_like(dva_ref)
        def sub_tile(qs, q_len, k_off, k_len, masked):
            keys = slice(k_off, k_off + k_len)
            for h in range(2):
                cols = slice(h * LANES, (h + 1) * LANES)
                qblk = q_ref[pl.ds(qs, q_len), cols]
                doblk = do_ref[pl.ds(qs, q_len), cols]
                kh = k_ref[keys, cols]
                p = jnp.exp(lax.dot_general(kh, qblk, _NT, preferred_element_type=F32))
                if masked:
                    p = jnp.where(lax.broadcasted_iota(jnp.int32, p.shape, 1) >= lax.broadcasted_iota(jnp.int32, p.shape, 0),
                                  p, 0.0)
                ds = (p * lax.dot_general(v_ref[keys, cols], doblk, _NT, preferred_element_type=F32)).astype(BF16)
                dva_ref[keys, cols] += jnp.dot(p.astype(BF16), doblk, preferred_element_type=F32)
                dka_ref[keys, cols] += jnp.dot(ds, qblk, preferred_element_type=F32)
                dq_ref[pl.ds(qs, q_len), cols] += lax.dot_general(ds, kh, _TN, preferred_element_type=F32)

        half = tk // 2
        sub_tile(pl.multiple_of(kb * tq, tq), tq, 0, half, True)
        sub_tile(pl.multiple_of(kb * tq + half, half), half, half, half, True)

        def step(qi, carry):
            sub_tile(pl.multiple_of(qi * tq, tq), tq, 0, tk, False)
            return carry

        lax.fori_loop(kb + 1, nq, step, 0)
        dka = dka_ref[...]
        dkc_ref[...] = pair(dka).astype(BF16)
        dvc_ref[...] = pair(dva_ref[...]).astype(BF16)
        first = 2 * pl.program_id(0)
        dck_ref[...] = head_lanes(dka, HEAD_DIM + 3, -1.0, first)

        @pl.when(kb == nq - 1)
        def _():
            dqa = dq_ref[...]
            dqc_ref[...] = pair(dqa, HEAD_DIM ** -0.5).astype(BF16)
            dcq_ref[...] = head_lanes(dqa, HEAD_DIM, 1.0, first)

    wide = 2 * LANES
    half = jax.ShapeDtypeStruct((S, ATT_WIDTH), BF16)
    slabs = jax.ShapeDtypeStruct((N_PAIRS, S, LANES), F32)
    return pl.pallas_call(
        body,
        out_shape=(half, half, half, slabs, slabs),
        grid=(N_PAIRS, nq),
        in_specs=[pl.BlockSpec((S, wide), lambda p, j: (0, p)), pl.BlockSpec((S, wide), lambda p, j: (0, p)),
                  pl.BlockSpec((tk, wide), lambda p, j: (j, p)), pl.BlockSpec((tk, wide), lambda p, j: (j, p))],
        out_specs=(pl.BlockSpec((S, LANES), lambda p, j: (0, p)), pl.BlockSpec((tk, LANES), lambda p, j: (j, p)),
                   pl.BlockSpec((tk, LANES), lambda p, j: (j, p)), pl.BlockSpec((None, S, LANES), lambda p, j: (p, 0, 0)),
                   pl.BlockSpec((None, tk, LANES), lambda p, j: (p, j, 0))),
        scratch_shapes=[pltpu.VMEM((S, wide), F32), pltpu.VMEM((tk, wide), F32), pltpu.VMEM((tk, wide), F32)],
        name="attn_bwd", compiler_params=_cp(("parallel", "arbitrary"), VMEM_LIMIT),
    )(qb, doa, ka, va)


def _gmlp_bwd(ug, dsg, gain, w_s, wt_s, bias_full):
    S = ug.shape[0]
    tm = _pick(S, (512, 256, 128))
    n_chunks = tm // CHUNK
    n_i = S // tm
    ones = _group_ones()
    nt = (((1,), (1,)), ((), ()))

    def body(ug_ref, dsg_ref, gain_ref, w_ref, wt_ref, bias_ref, ones_ref, dug_ref, dw_ref, dgain_ref, dbias_ref,
             dbacc_ref):
        i = pl.program_id(0)

        @pl.when(i == 0)
        def _():
            dw_ref[...] = jnp.zeros_like(dw_ref)
            dgain_ref[...] = jnp.zeros_like(dgain_ref)
            dbacc_ref[...] = jnp.zeros_like(dbacc_ref)

        ones_m = ones_ref[...]
        pu = ug_ref[:, :GMLP_WIDTH]
        pg = ug_ref[:, GMLP_WIDTH:]
        u = _gelu(pu)
        vr = _gelu(pg)
        ms = _dot3(vr * vr, ones_m) * (1.0 / GROUP_DIM)
        rinv = lax.rsqrt(ms + EPS)
        vhat = vr * rinv
        gain_v = gain_ref[...]
        vn = (vhat * gain_v).astype(BF16)
        mixed = _gmlp_mixed(vn, w_ref, bias_ref[...], n_chunks)
        dsg_v = dsg_ref[...]
        du = dsg_v * mixed
        dmixed = dsg_v * u
        dm_bf = dmixed.astype(BF16)
        lane = lax.broadcasted_iota(jnp.int32, (CHUNK, LANES), 1)
        row = lax.broadcasted_iota(jnp.int32, (CHUNK, CHUNK), 0)
        col = lax.broadcasted_iota(jnp.int32, (CHUNK, CHUNK), 1)
        wts = [jnp.where(col >= row, wt_ref[g], 0.0).astype(BF16) for g in range(N_GROUPS)]
        dvn_rows = []
        dbsum = jnp.zeros((CHUNK, GMLP_WIDTH), F32)
        for ci in range(n_chunks):
            rs = slice(ci * CHUNK, (ci + 1) * CHUNK)
            dbsum = dbsum + dmixed[rs, :]
            cols = []
            for pp in range(N_GROUPS // 2):
                cs = slice(pp * LANES, (pp + 1) * LANES)
                dm = dm_bf[rs, cs]
                dm_lo = jnp.where(lane < GROUP_DIM, dm, jnp.zeros_like(dm))
                dm_hi = jnp.where(lane >= GROUP_DIM, dm, jnp.zeros_like(dm))
                vb = vn[rs, cs]
                dw_ref[2 * pp] += lax.dot_general(dm_lo, vb, nt, preferred_element_type=F32)
                dw_ref[2 * pp + 1] += lax.dot_general(dm_hi, vb, nt, preferred_element_type=F32)
                cols.append(jnp.dot(wts[2 * pp], dm_lo, preferred_element_type=F32)
                            + jnp.dot(wts[2 * pp + 1], dm_hi, preferred_element_type=F32))
            dvn_rows.append(jnp.concatenate(cols, axis=1))
        dvn = jnp.concatenate(dvn_rows, axis=0)
        dbacc_ref[...] += dbsum
        dgain_ref[0:1, :] += jnp.sum(dvn * vhat, axis=0, keepdims=True)
        dvhat = dvn * gain_v
        gm = _dot3(dvhat * vhat, ones_m) * (1.0 / GROUP_DIM)
        dvr = rinv * (dvhat - vhat * gm)
        dug_ref[:, :GMLP_WIDTH] = (du * _gelu_grad(pu)).astype(BF16)
        dug_ref[:, GMLP_WIDTH:] = (dvr * _gelu_grad(pg)).astype(BF16)

        @pl.when(i == n_i - 1)
        def _():
            for g in range(N_GROUPS):
                dw_ref[g] = jnp.where(row >= col, dw_ref[g], 0.0)
            dbias_ref[...] = _dot3(dbacc_ref[...], ones_m)

    return pl.pallas_call(
        body,
        out_shape=(jax.ShapeDtypeStruct((S, 2 * GMLP_WIDTH), BF16), jax.ShapeDtypeStruct((N_GROUPS, CHUNK, CHUNK), F32),
                   jax.ShapeDtypeStruct((8, GMLP_WIDTH), F32), jax.ShapeDtypeStruct((CHUNK, GMLP_WIDTH), F32)),
        grid=(n_i,),
        in_specs=[pl.BlockSpec((tm, 2 * GMLP_WIDTH), lambda i: (i, 0)), pl.BlockSpec((tm, GMLP_WIDTH), lambda i: (i, 0)),
                  pl.BlockSpec((1, GMLP_WIDTH), lambda i: (0, 0)),
                  pl.BlockSpec((N_GROUPS, CHUNK, CHUNK), lambda i: (0, 0, 0)),
                  pl.BlockSpec((N_GROUPS, CHUNK, CHUNK), lambda i: (0, 0, 0)),
                  pl.BlockSpec((CHUNK, GMLP_WIDTH), lambda i: (0, 0)),
                  pl.BlockSpec((GMLP_WIDTH, GMLP_WIDTH), lambda i: (0, 0))],
        out_specs=(pl.BlockSpec((tm, 2 * GMLP_WIDTH), lambda i: (i, 0)),
                   pl.BlockSpec((N_GROUPS, CHUNK, CHUNK), lambda i: (0, 0, 0)),
                   pl.BlockSpec((8, GMLP_WIDTH), lambda i: (0, 0)),
                   pl.BlockSpec((CHUNK, GMLP_WIDTH), lambda i: (0, 0))),
        scratch_shapes=[pltpu.VMEM((CHUNK, GMLP_WIDTH), F32)],
        name="gmlp_bwd", compiler_params=_cp(("arbitrary",), VMEM_LIMIT),
    )(ug, dsg, gain, w_s, wt_s, bias_full, ones)


def _gate_bwd(dcq, dck, zf):
    S = zf.shape[0]
    tm = _pick(S, (256,))
    n_i = S // tm
    triu = (lax.broadcasted_iota(jnp.int32, (tm, tm), 0) <= lax.broadcasted_iota(jnp.int32, (tm, tm), 1)).astype(BF16)

    def body(dcq_ref, dck_ref, zf_ref, tri_ref, dzf_ref, dbf_ref, carry_ref):
        i = pl.program_id(0)

        @pl.when(i == 0)
        def _():
            carry_ref[...] = jnp.zeros_like(carry_ref)
            dbf_ref[...] = jnp.zeros_like(dbf_ref)

        lane = lax.broadcasted_iota(jnp.int32, (tm, LANES), 1)
        dc = dcq_ref[0] + dck_ref[0]
        for p in range(1, N_PAIRS):
            dc = dc + (dcq_ref[p] + dck_ref[p])
        dlf = _dot3l(tri_ref[...], dc) + carry_ref[0:1, :]
        carry_ref[0:1, :] = dlf[0:1, :]
        dz = jnp.where(lane < N_HEADS, dlf * _sigmoid(-zf_ref[...]), 0.0)
        dzf_ref[...] = dz.astype(BF16)
        dbf_ref[0:1, :] += jnp.sum(dz, axis=0, keepdims=True)

    return pl.pallas_call(
        body,
        out_shape=(jax.ShapeDtypeStruct((S, LANES), BF16), jax.ShapeDtypeStruct((8, LANES), F32)),
        grid=(n_i,),
        in_specs=[pl.BlockSpec((N_PAIRS, tm, LANES), lambda i: (0, n_i - 1 - i, 0)),
                  pl.BlockSpec((N_PAIRS, tm, LANES), lambda i: (0, n_i - 1 - i, 0)),
                  pl.BlockSpec((tm, LANES), lambda i: (n_i - 1 - i, 0)),
                  pl.BlockSpec((tm, tm), lambda i: (0, 0))],
        out_specs=(pl.BlockSpec((tm, LANES), lambda i: (n_i - 1 - i, 0)), pl.BlockSpec((8, LANES), lambda i: (0, 0))),
        scratch_shapes=[pltpu.VMEM((8, LANES), F32)],
        name="gate_bwd", compiler_params=_cp(("arbitrary",), VMEM_LIMIT),
    )(dcq, dck, zf, triu)


def _out_proj_fwd(att_bf, sg, w_out_bf, x, g_ffn):
    S = x.shape[0]
    tm = _pick(S, (512, 256))

    def body(a_ref, s_ref, w_ref, x_ref, g_ref, h_ref, hn_ref):
        h = (x_ref[...] + jnp.dot(a_ref[...], w_ref[:ATT_WIDTH, :], preferred_element_type=F32)
             + jnp.dot(s_ref[...], w_ref[ATT_WIDTH:, :], preferred_element_type=F32))
        h_ref[...] = h
        r = lax.rsqrt(jnp.mean(h * h, axis=-1, keepdims=True) + EPS)
        hn_ref[...] = ((h * r) * g_ref[...]).astype(BF16)

    row = pl.BlockSpec((tm, D_MODEL), lambda i: (i, 0))
    half = pl.BlockSpec((tm, ATT_WIDTH), lambda i: (i, 0))
    return pl.pallas_call(
        body, out_shape=(jax.ShapeDtypeStruct((S, D_MODEL), F32), jax.ShapeDtypeStruct((S, D_MODEL), BF16)),
        grid=(S // tm,),
        in_specs=[half, half, pl.BlockSpec((D_MODEL, D_MODEL), lambda i: (0, 0)), row,
                  pl.BlockSpec((1, D_MODEL), lambda i: (0, 0))],
        out_specs=(row, row), name="out_proj", compiler_params=_cp(("parallel",), VMEM_LIMIT),
    )(att_bf, sg, w_out_bf, x, g_ffn)


def _out_proj_dw(att_bf, sg, dh1_bf):
    S = att_bf.shape[0]
    tk = _pick(S, (1024, 512))

    def body(a_ref, s_ref, d_ref, o_ref):
        k = pl.program_id(0)

        @pl.when(k == 0)
        def _():
            o_ref[...] = jnp.zeros_like(o_ref)

        d = d_ref[...]
        o_ref[:ATT_WIDTH, :] += lax.dot_general(a_ref[...], d, _TN, preferred_element_type=F32)
        o_ref[ATT_WIDTH:, :] += lax.dot_general(s_ref[...], d, _TN, preferred_element_type=F32)

    half = pl.BlockSpec((tk, ATT_WIDTH), lambda k: (k, 0))
    return pl.pallas_call(
        body, out_shape=jax.ShapeDtypeStruct((D_MODEL, D_MODEL), F32), grid=(S // tk,),
        in_specs=[half, half, pl.BlockSpec((tk, D_MODEL), lambda k: (k, 0))],
        out_specs=pl.BlockSpec((D_MODEL, D_MODEL), lambda k: (0, 0)),
        name="out_proj_dw", compiler_params=_cp(("arbitrary",), VMEM_LIMIT),
    )(att_bf, sg, dh1_bf)


_IN_PIECES = ((0, ATT_WIDTH), (ATT_WIDTH, ATT_WIDTH), (2 * ATT_WIDTH, ATT_WIDTH), (QKV, 2 * GMLP_WIDTH), (UG_END, LANES))


def _inproj_bwd_dx(pieces, w_pad, x, g_mix, dh1):
    S = x.shape[0]
    tm = _pick(S, (512, 256))

    def body(*refs):
        p_refs, (w_ref, x_ref, g_ref, r_ref, dx_ref, dg_ref) = refs[:5], refs[5:]
        i = pl.program_id(0)

        @pl.when(i == 0)
        def _():
            dg_ref[...] = jnp.zeros_like(dg_ref)

        dxn = None
        for p_ref, (c0, width) in zip(p_refs, _IN_PIECES):
            part = jnp.dot(p_ref[...], w_ref[c0:c0 + width, :], preferred_element_type=F32)
            dxn = part if dxn is None else dxn + part
        xf = x_ref[...]
        r = lax.rsqrt(jnp.mean(xf * xf, axis=-1, keepdims=True) + EPS)
        xhat = xf * r
        dg_ref[0:1, :] += jnp.sum(dxn * xhat, axis=0, keepdims=True)
        dhat = dxn * g_ref[...]
        dx_ref[...] = r_ref[...] + r * (dhat - xhat * jnp.mean(dhat * xhat, axis=-1, keepdims=True))

    row = pl.BlockSpec((tm, D_MODEL), lambda i: (i, 0))
    return pl.pallas_call(
        body, out_shape=(jax.ShapeDtypeStruct((S, D_MODEL), F32), jax.ShapeDtypeStruct((8, D_MODEL), F32)),
        grid=(S // tm,),
        in_specs=[pl.BlockSpec((tm, width), lambda i: (i, 0)) for _, width in _IN_PIECES]
        + [pl.BlockSpec((IN_PAD, D_MODEL), lambda i: (0, 0)), row, pl.BlockSpec((1, D_MODEL), lambda i: (0, 0)), row],
        out_specs=(row, pl.BlockSpec((8, D_MODEL), lambda i: (0, 0))),
        name="in_proj_dx", compiler_params=_cp(("arbitrary",), VMEM_LIMIT),
    )(*pieces, w_pad, x, g_mix, dh1)


def _inproj_bwd_dw(xn, pieces):
    S = xn.shape[0]
    tk = _pick(S, (1024, 512))

    def body(*refs):
        x_ref, p_refs, o_ref = refs[0], refs[1:6], refs[6]
        k = pl.program_id(0)

        @pl.when(k == 0)
        def _():
            o_ref[...] = jnp.zeros_like(o_ref)

        xb = x_ref[...]
        for p_ref, (c0, width) in zip(p_refs, _IN_PIECES):
            o_ref[:, c0:c0 + width] += lax.dot_general(xb, p_ref[...], _TN, preferred_element_type=F32)

    return pl.pallas_call(
        body, out_shape=jax.ShapeDtypeStruct((D_MODEL, IN_PAD), F32), grid=(S // tk,),
        in_specs=[pl.BlockSpec((tk, D_MODEL), lambda k: (k, 0))]
        + [pl.BlockSpec((tk, width), lambda k: (k, 0)) for _, width in _IN_PIECES],
        out_specs=pl.BlockSpec((D_MODEL, IN_PAD), lambda k: (0, 0)),
        name="in_proj_dw", compiler_params=_cp(("arbitrary",), VMEM_LIMIT),
    )(xn, *pieces)


def _adamw(w, m, v, parts, name):
    R, C = w.shape[-2:]
    tr = R
    for cand in (256, 128, 64, 32, 16, 8):
        if R % cand == 0 and R > cand:
            tr = cand
            break
    c1 = 1.0 / (1.0 - ADAM_B1 ** ADAM_STEP)
    c2 = 1.0 / (1.0 - ADAM_B2 ** ADAM_STEP)

    def body(w_ref, m_ref, v_ref, p_ref, g_ref, d_ref, nm_ref, nv_ref):
        g = p_ref[0].astype(F32)
        for j in range(1, N_DEV):
            g = g + p_ref[j].astype(F32)
        g_ref[...] = g
        nm = ADAM_B1 * m_ref[...] + (1.0 - ADAM_B1) * g
        nv = ADAM_B2 * v_ref[...] + (1.0 - ADAM_B2) * (g * g)
        nm_ref[...] = nm
        nv_ref[...] = nv
        d_ref[...] = -ADAM_LR * ((nm * c1) / (jnp.sqrt(nv * c2) + ADAM_EPS) + ADAM_WD * w_ref[...])

    if w.ndim == 3:
        spec = pl.BlockSpec((None, tr, C), lambda i: (0, i, 0))
    else:
        spec = pl.BlockSpec((tr, C), lambda i: (i, 0))
    shp = jax.ShapeDtypeStruct(w.shape, F32)
    return pl.pallas_call(
        body, out_shape=(shp, shp, shp, shp), grid=(R // tr,),
        in_specs=[spec, spec, spec, pl.BlockSpec((N_DEV, tr, C), lambda i: (0, i, 0))],
        out_specs=(spec, spec, spec, spec),
        name=name, compiler_params=_cp(("parallel",), VMEM_LIMIT),
    )(w, m, v, parts)


def _adamw_packed(w, m, v, parts, sizes, name):
    R = w.shape[0]
    assert R == sum(sizes)
    c1 = 1.0 / (1.0 - ADAM_B1 ** ADAM_STEP)
    c2 = 1.0 / (1.0 - ADAM_B2 ** ADAM_STEP)
    n = len(sizes)

    def body(w_ref, m_ref, v_ref, p_ref, *out_refs):
        g = p_ref[0]
        for j in range(1, N_DEV):
            g = g + p_ref[j]
        nm = ADAM_B1 * m_ref[...] + (1.0 - ADAM_B1) * g
        nv = ADAM_B2 * v_ref[...] + (1.0 - ADAM_B2) * (g * g)
        d = -ADAM_LR * ((nm * c1) / (jnp.sqrt(nv * c2) + ADAM_EPS) + ADAM_WD * w_ref[...])
        for kind, val in enumerate((g, d, nm, nv)):
            off = 0
            for k, rows in enumerate(sizes):
                out_refs[kind * n + k][...] = val[off:off + rows, :]
                off += rows

    whole = pl.BlockSpec((R, LANES), lambda i: (0, 0))
    shapes = [jax.ShapeDtypeStruct((rows, LANES), F32) for rows in sizes] * 4
    res = pl.pallas_call(
        body, out_shape=tuple(shapes), grid=(1,),
        in_specs=[whole, whole, whole, pl.BlockSpec((N_DEV, R, LANES), lambda i: (0, 0, 0))],
        out_specs=tuple(pl.BlockSpec((rows, LANES), lambda i: (0, 0)) for rows in sizes) * 4,
        name=name, compiler_params=_cp(("arbitrary",), VMEM_LIMIT),
    )(w, m, v, parts)
    return [list(res[kind * n:(kind + 1) * n]) for kind in range(4)]


def _place():
    x, y, c = lax.axis_index("x"), lax.axis_index("y"), lax.axis_index("c")
    return x, y, c


def _all_gather(blocks, name):
    n = len(blocks)

    def body(*refs):
        ins, outs = refs[:n], refs[n:2 * n]
        send_sems, recv_sems, local_sems = refs[2 * n:]
        x, y, c = _place()
        me, sibling = (x, y, c), (x, y, 1 - c)
        chips = [(1 - x, y), (x, 1 - y), (1 - x, 1 - y)]
        sends = []
        for a in range(n):
            out = outs[a]

            def slot(px, py, pc, out=out):
                return out.at[4 * px + 2 * py + pc]

            def copy(k, block, to, src=None, a=a, slot=slot):
                return pltpu.make_async_remote_copy(
                    src_ref=slot(*block) if src is None else src, dst_ref=slot(*block),
                    send_sem=send_sems.at[a, k], recv_sem=recv_sems.at[a, k], device_id=to, device_id_type=MESH)

            mine = pltpu.make_async_copy(ins[a], slot(*me), local_sems.at[a])
            mine.start()
            first = [copy(0, me, sibling, src=ins[a])]
            first += [copy(1 + j, me, (*chip, c), src=ins[a]) for j, chip in enumerate(chips)]
            for cp in first:
                cp.start()
            sends.append((mine, first, copy))
        for a in range(n):
            mine, first, copy = sends[a]
            passed = [copy(4 + j, (*chip, c), sibling) for j, chip in enumerate(chips)]
            for j, chip in enumerate(chips):
                copy(1 + j, (*chip, c), me).wait_recv()
                passed[j].start()
            copy(0, sibling, me).wait_recv()
            for j, chip in enumerate(chips):
                copy(4 + j, (*chip, 1 - c), me).wait_recv()
            for cp in first + passed:
                cp.wait_send()
            mine.wait()

    any_spec = pl.BlockSpec(memory_space=pl.ANY)
    return pl.pallas_call(
        body, out_shape=tuple(jax.ShapeDtypeStruct((N_DEV,) + b.shape, b.dtype) for b in blocks),
        in_specs=[any_spec] * n, out_specs=tuple([any_spec] * n),
        scratch_shapes=[pltpu.SemaphoreType.DMA((n, 7)), pltpu.SemaphoreType.DMA((n, 7)), pltpu.SemaphoreType.DMA((n,))],
        name=name,
    )(*blocks)


_HBM = pl.BlockSpec(memory_space=pltpu.HBM)
_SEM = pl.BlockSpec(memory_space=pltpu.SEMAPHORE)
_EFFECT = pltpu.SideEffectType.DATAFLOW_SIDE_EFFECTING


def _peers(x, y, c):
    out = []
    for k in range(1, N_DEV):
        px, py, pc = x ^ ((k >> 2) & 1), y ^ ((k >> 1) & 1), c ^ (k & 1)
        out.append((k, (px, py, pc), 4 * px + 2 * py + pc))
    return out


def _xchg_copies(src_refs, land_refs, send_sems, recv_sems, scatter):
    x, y, c = _place()
    me = 4 * x + 2 * y + c
    copies = []
    for a, (src, land) in enumerate(zip(src_refs, land_refs)):
        for k, place, idx in _peers(x, y, c):
            j = a * (N_DEV - 1) + k - 1
            copies.append(pltpu.make_async_remote_copy(
                src_ref=src.at[idx] if scatter[a] else src, dst_ref=land.at[me],
                send_sem=send_sems[j], recv_sem=recv_sems[j], device_id=place, device_id_type=MESH))
    return copies


def _xchg_start(srcs, scatter, name):
    n = len(srcs)
    lands = [lax.empty((N_DEV,) + (s.shape[1:] if sc else s.shape), s.dtype) for s, sc in zip(srcs, scatter)]

    ns = n * (N_DEV - 1)

    def body(*refs):
        sems = refs[2 * n:2 * n + 2 * ns]
        for cp in _xchg_copies(refs[:n], refs[n:2 * n], sems[:ns], sems[ns:], scatter):
            cp.start()
        token = refs[-1]
        token[...] = jnp.zeros_like(token)

    both = list(srcs) + lands
    res = pl.pallas_call(
        body, name=name,
        out_shape=(*[pltpu.SemaphoreType.DMA(())] * (2 * ns),
                   *[pltpu.HBM(a.shape, a.dtype) for a in both], jax.ShapeDtypeStruct((8, LANES), F32)),
        in_specs=[_HBM] * (2 * n),
        out_specs=(*([_SEM] * (2 * ns)), *([_HBM] * (2 * n)), pl.BlockSpec(memory_space=pltpu.VMEM)),
        input_output_aliases={i: 2 * ns + i for i in range(2 * n)},
        compiler_params=pltpu.CompilerParams(has_side_effects=_EFFECT),
    )(*[pltpu.with_memory_space_constraint(a, pltpu.HBM) for a in both])
    return (tuple(res[:2 * ns]), tuple(res[2 * ns:2 * ns + 2 * n])), res[-1]


def _xchg_wait(handle, scatter, after, name):
    sems, thru = handle
    n = len(thru) // 2
    ns = len(sems) // 2

    def body(*refs):
        got = refs[2 * n:2 * n + 2 * ns]
        for cp in _xchg_copies(refs[:n], refs[n:2 * n], got[:ns], got[ns:], scatter):
            cp.wait_send()
            cp.wait_recv()

    outs = pl.pallas_call(
        body, name=name, out_shape=tuple(pltpu.HBM(a.shape, a.dtype) for a in thru),
        in_specs=[_HBM] * (2 * n) + [_SEM] * (2 * ns) + [pl.BlockSpec(memory_space=pl.ANY)],
        out_specs=tuple([_HBM] * (2 * n)), input_output_aliases={i: i for i in range(2 * n)},
        compiler_params=pltpu.CompilerParams(has_side_effects=_EFFECT),
    )(*thru, *sems, after)
    return outs[:n], outs[n:]


def _tie(a, token):
    return a if token is None else a + token[0, 0].astype(a.dtype)


def _rows128(a):
    flat = a.reshape(-1)
    rows = -(-flat.shape[0] // LANES)
    rows = -(-rows // 8) * 8
    return jnp.pad(flat, (0, rows * LANES - flat.shape[0])).reshape(rows, LANES)


def _local_step(x, target, norm_mix_g, w_in_t, b_forget, gmlp_norm_g, w_spatial, b_spatial, norm_ffn_g, conv_b,
                norm_final_g, rest_fn, send_fn, small_fn, token=None):
    f = D_FF
    g_mix = norm_mix_g.reshape(1, D_MODEL)
    w_pad = jnp.pad(w_in_t, ((0, IN_PAD - IN_COLS), (0, 0)))
    bf_pad = jnp.pad(b_forget.reshape(1, N_HEADS), ((0, 0), (0, LANES - N_HEADS)))
    xn, qa, ka, va, ug, zf = _inproj_fwd(x, _tie(g_mix, token), w_pad, bf_pad)
    bias_full = jnp.repeat(b_spatial.reshape(N_GROUPS, CHUNK).T, GROUP_DIM, axis=1)
    w_s = w_spatial.reshape(N_GROUPS, CHUNK, CHUNK)
    gain = gmlp_norm_g.reshape(1, GMLP_WIDTH)
    sg = _gmlp_fwd(ug, gain, w_s, bias_full)
    att, lse, att_bf = _attn_fwd(qa, ka, va)
    w_out_bf, w_up_bf, conv_w, w_down_bf = rest_fn(att_bf)
    g_ffn = norm_ffn_g.reshape(1, D_MODEL)
    h1, hn = _out_proj_fwd(att_bf, sg, w_out_bf, x, g_ffn)
    cw = jnp.pad(conv_w.reshape(3, 2, f).transpose(1, 0, 2), ((0, 0), (0, 5), (0, 0)))
    cb = conv_b.reshape(2, 1, f)
    hu, hc, act = _ffn_up_conv(hn, w_up_bf, cw, cb)
    loss_blk, dh2, dh2_bf, dg_final = _ffn_down_loss(act, w_down_bf, h1, norm_final_g.reshape(1, D_MODEL), target)
    dw_down = _mm(act, dh2_bf, mode="tn", out_dtype=F32, tm=1408, tn=1024, tk=2048, name="ffn_down_dw")
    dact = _mm(dh2_bf, w_down_bf, mode="nt", out_dtype=F32, tm=1024, tn=1408, tk=1024, outer="j", name="ffn_down_dx")
    dhu, dcw = _conv_gate_bwd(hc, hu, dact, _tie(cw, send_fn("w_down", dw_down)))
    dw_up = _mm(hn, dhu, mode="tn", out_dtype=F32, tm=1024, tn=1408, tk=2048, b_halves=True, outer="j", name="ffn_up_dw")
    dh1, dh1_bf, dg_ffn = _ffn_up_dx_rms(dhu, w_up_bf, h1, _tie(g_ffn, send_fn("w_up", dw_up)), dh2)
    dsg, qb, doa = _out_proj_dx_prep(dh1_bf, w_out_bf, att, lse, qa)
    dw_out = _out_proj_dw(att_bf, sg, dh1_bf)
    dq, dk, dv, dcq, dck = _attn_bwd(qb, ka, va, doa)
    wt_s = w_s.transpose(0, 2, 1)
    dug, dw_s, dgain, dbias = _gmlp_bwd(ug, dsg, _tie(gain, send_fn("w_out", dw_out)), w_s, wt_s, bias_full)
    dzf, dbf = _gate_bwd(dcq, dck, zf)
    grad_x, dg_mix = _inproj_bwd_dx((dq, dk, dv, dug, dzf), w_pad, x, g_mix, dh1)
    grads = dict(
        norm_mix_g=dg_mix[0:1, :],
        b_forget=dbf[0:1, :N_HEADS],
        gmlp_norm_g=dgain[0:1, :],
        w_spatial=dw_s,
        b_spatial=dbias[:, ::GROUP_DIM].T,
        norm_ffn_g=dg_ffn[0:1, :],
        conv_w=dcw[:, 0:3, :].transpose(1, 0, 2).reshape(3, 2 * f),
        conv_b=dcw[:, 3, :].reshape(1, 2 * f),
        norm_final_g=dg_final[0, :],
    )
    token = small_fn(loss_blk[0, 0], grads)
    dw_in = _inproj_bwd_dw(xn, (dq, dk, dv, dug, _tie(dzf, token)))
    return grad_x, send_fn("w_in", dw_in[:, :IN_COLS])


SMALL = ("norm_mix_g", "b_forget", "gmlp_norm_g", "w_spatial", "b_spatial", "norm_ffn_g", "conv_b", "norm_final_g")


def kernel(x, norm_mix_g, w_in, b_forget, gmlp_norm_g, w_spatial, b_spatial, w_out, norm_ffn_g, w_up, conv_w, conv_b, w_down, norm_final_g, loss_target, m_norm_mix_g, m_w_in, m_b_forget, m_gmlp_norm_g, m_w_spatial, m_b_spatial, m_w_out, m_norm_ffn_g, m_w_up, m_conv_w, m_conv_b, m_w_down, m_norm_final_g, v_norm_mix_g, v_w_in, v_b_forget, v_gmlp_norm_g, v_w_spatial, v_b_spatial, v_w_out, v_norm_ffn_g, v_w_up, v_conv_w, v_conv_b, v_w_down, v_norm_final_g):
    weights = dict(norm_mix_g=norm_mix_g, w_in=w_in, b_forget=b_forget, gmlp_norm_g=gmlp_norm_g, w_spatial=w_spatial,
                   b_spatial=b_spatial, w_out=w_out, norm_ffn_g=norm_ffn_g, w_up=w_up, conv_w=conv_w, conv_b=conv_b,
                   w_down=w_down, norm_final_g=norm_final_g)
    m_in = dict(norm_mix_g=m_norm_mix_g, w_in=m_w_in, b_forget=m_b_forget, gmlp_norm_g=m_gmlp_norm_g,
                w_spatial=m_w_spatial, b_spatial=m_b_spatial, w_out=m_w_out, norm_ffn_g=m_norm_ffn_g, w_up=m_w_up,
                conv_w=m_conv_w, conv_b=m_conv_b, w_down=m_w_down, norm_final_g=m_norm_final_g)
    v_in = dict(norm_mix_g=v_norm_mix_g, w_in=v_w_in, b_forget=v_b_forget, gmlp_norm_g=v_gmlp_norm_g,
                w_spatial=v_w_spatial, b_spatial=v_b_spatial, w_out=v_w_out, norm_ffn_g=v_norm_ffn_g, w_up=v_w_up,
                conv_w=v_conv_w, conv_b=v_conv_b, w_down=v_w_down, norm_final_g=v_norm_final_g)
    order = list(weights)
    me = 4 * lax.axis_index("x") + 2 * lax.axis_index("y") + lax.axis_index("c")
    n_in, n_up = w_in.shape[2], w_up.shape[2]
    r_out, r_down = w_out.shape[1], w_down.shape[1]

    def with_mine(landed, mine):
        return lax.dynamic_update_index_in_dim(landed, mine, me, 0)

    up_blk = w_up[0].T.astype(BF16)
    out_blk = w_out[0].astype(BF16)
    down_blk = w_down[0].astype(BF16)
    taps_blk = jnp.pad(conv_w[0], ((0, 5), (0, 0)))
    (in_all,) = _all_gather([w_in[0].T.astype(BF16)], "gather_w_in")
    in_all, rest_blocks = lax.optimization_barrier((in_all, [up_blk, out_blk, down_blk, taps_blk]))
    rest_handle, token = _xchg_start(rest_blocks, [False] * 4, "gather_rest_start")
    w_in_t = in_all.reshape(N_DEV * n_in, D_MODEL)

    def rest_fn(after):
        mine, landed = _xchg_wait(rest_handle, [False] * 4, after, "gather_rest_wait")
        up_all, out_all, down_all, taps_all = [with_mine(l, b) for l, b in zip(landed, mine)]
        return (out_all.reshape(N_DEV * r_out, D_MODEL), up_all.reshape(N_DEV * n_up, D_MODEL),
                taps_all[:, :3, :].transpose(1, 0, 2).reshape(3, N_DEV * n_up),
                down_all.reshape(N_DEV * r_down, D_MODEL))

    sent = {}

    def send_fn(name, grad):
        if name == "w_in":
            parts = grad.reshape(D_MODEL, N_DEV, -1).transpose(1, 0, 2).astype(BF16)
        elif name == "w_up":
            parts = grad.reshape(D_MODEL, N_DEV, -1).transpose(1, 0, 2)
        else:
            parts = grad.reshape(N_DEV, -1, D_MODEL)
        sent[name], tok = _xchg_start([parts], [True], "scatter_" + name + "_start")
        return tok

    small = {}

    def small_fn(loss_local, g):
        loss_rows = jnp.pad(loss_local.reshape(1, 1), ((0, 31), (0, LANES - 1)))
        packed = [_rows128(g[k]) for k in SMALL] + [loss_rows, _rows128(g["conv_w"])]
        small["sizes"] = [p.shape[0] for p in packed]
        small["handle"], tok = _xchg_start([jnp.concatenate(packed, axis=0)], [False], "gather_small_start")
        return tok

    grad_x, after = _local_step(
        x[0], loss_target[0], norm_mix_g, w_in_t, b_forget, gmlp_norm_g, w_spatial, b_spatial, norm_ffn_g, conv_b,
        norm_final_g, rest_fn, send_fn, small_fn, token)

    outs = {}

    def update_big(name, after):
        (parts,), (landed,) = _xchg_wait(sent[name], [True], after, "scatter_" + name + "_wait")
        got = with_mine(landed, lax.dynamic_index_in_dim(parts, me, 0, keepdims=False))
        outs[name] = tuple(_adamw(weights[name], m_in[name], v_in[name], got, "adamw_" + name))
        return outs[name][0]

    for name in ("w_down", "w_up", "w_out"):
        after = update_big(name, after)

    (mine,), (landed,) = _xchg_wait(small["handle"], [False], after, "gather_small_wait")
    small_all = with_mine(landed, mine)
    sizes = small["sizes"]
    n_small_rows = sum(sizes[:-2])

    def pack(src):
        return jnp.concatenate([_rows128(src[k]) for k in SMALL] + [jnp.zeros((sizes[-2], LANES), F32)], axis=0)

    n_adam_rows = n_small_rows + sizes[-2]
    per_kind = _adamw_packed(pack(weights), pack(m_in), pack(v_in), small_all, sizes[:-1], "adamw_small")
    loss = per_kind[0][-1][0, 0]
    for j, k in enumerate(SMALL):
        shp = weights[k].shape
        cnt = math.prod(shp)
        outs[k] = tuple(kind[j].reshape(-1)[:cnt].reshape(shp) for kind in per_kind)
    sg_ = per_kind[0][0]
    taps_parts = small_all[:, n_adam_rows:, :].reshape(N_DEV, -1)[:, :3 * N_DEV * n_up].reshape(N_DEV, 3, N_DEV * n_up)
    taps_mine = lax.dynamic_slice_in_dim(taps_parts, me * n_up, n_up, axis=2)
    taps_mine = jnp.pad(taps_mine, ((0, 0), (0, 5), (0, 0)))

    def pad8(a):
        return jnp.pad(a[0], ((0, 5), (0, 0)))

    res = _adamw(pad8(conv_w), pad8(m_conv_w), pad8(v_conv_w), taps_mine, "adamw_conv_w")
    outs["conv_w"] = tuple(a[:3][None] for a in res)
    update_big("w_in", sg_)

    return (loss, grad_x[None], *[outs[k][0] for k in order], *[outs[k][1] for k in order],
            *[outs[k][2] for k in order], *[outs[k][3] for k in order])
```

```python
import functools
import math

import jax
import jax.numpy as jnp
from jax import lax
from jax.experimental import pallas as pl
from jax.experimental.pallas import tpu as pltpu

F32 = jnp.float32
BF16 = jnp.bfloat16

N_DEV = 8
D_MODEL = 1024
ATT_WIDTH = 512
GMLP_WIDTH = 512
HEAD_DIM = 64
N_HEADS = 8
N_PAIRS = 4
N_GROUPS = 8
GROUP_DIM = 64
CHUNK = 128
D_FF = 2816
IN_COLS = 2568
IN_PAD = 2688
QKV = 1536
UG_END = 2560
EPS = 1e-6
LANES = 128

ADAM_LR = 0.001
ADAM_B1 = 0.9
ADAM_B2 = 0.999
ADAM_EPS = 1e-08
ADAM_WD = 0.01
ADAM_STEP = 10

ATT_TQ = 1024
ATT_TK = 1024
FFN_TM, FFN_TN = 512, 1408
CONV_TM, CONV_TN = 512, 1408
VMEM_LIMIT = 56 * 1024 * 1024
MESH = pl.DeviceIdType.MESH


def _cp(sem, vmem=None):
    return pltpu.CompilerParams(dimension_semantics=sem, vmem_limit_bytes=vmem)


def _pick(n, prefs):
    for p in prefs:
        if n % p == 0:
            return p
    return n


def _split3(x):
    hi = x.astype(BF16)
    r1 = x - hi.astype(F32)
    mid = r1.astype(BF16)
    lo = (r1 - mid.astype(F32)).astype(BF16)
    return hi, mid, lo


def _dot3(x, ones_bf):
    d = functools.partial(jnp.dot, preferred_element_type=F32)
    out = []
    for c in range(0, x.shape[1], 2 * LANES):
        blk = ones_bf[c:c + 2 * LANES, c:c + 2 * LANES]
        hi, mid, lo = _split3(x[:, c:c + 2 * LANES])
        out.append(d(hi, blk) + d(mid, blk) + d(lo, blk))
    return jnp.concatenate(out, axis=1)


def _dot3l(ones_bf, x):
    n = x.shape[1]
    y = jnp.dot(ones_bf, jnp.concatenate(_split3(x), axis=1), preferred_element_type=F32)
    return y[:, :n] + y[:, n:2 * n] + y[:, 2 * n:]


def _gelu(x):
    k = math.sqrt(2.0 / math.pi)
    t = jnp.tanh(k * (x + 0.044715 * (x * x * x)))
    return 0.5 * x * (1.0 + t)


def _gelu_grad(x):
    k = math.sqrt(2.0 / math.pi)
    x2 = x * x
    t = jnp.tanh(k * (x + 0.044715 * (x2 * x)))
    return 0.5 * (1.0 + t) + 0.5 * x * (1.0 - t * t) * (k * (1.0 + 3.0 * 0.044715 * x2))


def _sigmoid(x):
    return 1.0 / (1.0 + jnp.exp(-x))


def _mm(a, b, *, mode, out_dtype, tm, tn, tk, name, res=None, a_halves=False, b_halves=False,
        out_halves=False, outer="i"):
    if mode == "tn":
        K, M = a.shape[-2], a.shape[-1] * (2 if a_halves else 1)
    else:
        M, K = a.shape[-2], a.shape[-1] * (2 if a_halves else 1)
    if mode == "nt":
        N = b.shape[-2]
        assert b.shape[-1] == K
    else:
        N = b.shape[-1] * (2 if b_halves else 1)
    tm, tn, tk = min(tm, M), min(tn, N), min(tk, K)
    assert M % tm == 0 and N % tn == 0 and K % tk == 0, (name, M, N, K, tm, tn, tk)
    nm, nn, nk = M // tm, N // tn, K // tk

    def ij(g0, g1):
        return (g0, g1) if outer == "i" else (g1, g0)

    if mode == "nn":
        dims = (((1,), (0,)), ((), ()))
        if a_halves:
            nkh = nk // 2
            a_spec = pl.BlockSpec((None, tm, tk), lambda g0, g1, k: (k // nkh, ij(g0, g1)[0], k % nkh))
        else:
            a_spec = pl.BlockSpec((tm, tk), lambda g0, g1, k: (ij(g0, g1)[0], k))
        b_spec = pl.BlockSpec((tk, tn), lambda g0, g1, k: (k, ij(g0, g1)[1]))
    elif mode == "nt":
        dims = (((1,), (1,)), ((), ()))
        if a_halves:
            nkh = nk // 2
            a_spec = pl.BlockSpec((None, tm, tk), lambda g0, g1, k: (k // nkh, ij(g0, g1)[0], k % nkh))
        else:
            a_spec = pl.BlockSpec((tm, tk), lambda g0, g1, k: (ij(g0, g1)[0], k))
        b_spec = pl.BlockSpec((tn, tk), lambda g0, g1, k: (ij(g0, g1)[1], k))
    else:
        dims = (((0,), (0,)), ((), ()))
        if a_halves:
            nmh = nm // 2
            a_spec = pl.BlockSpec((None, tk, tm), lambda g0, g1, k: (ij(g0, g1)[0] // nmh, k, ij(g0, g1)[0] % nmh))
        else:
            a_spec = pl.BlockSpec((tk, tm), lambda g0, g1, k: (k, ij(g0, g1)[0]))
        if b_halves:
            nnh = nn // 2
            b_spec = pl.BlockSpec((None, tk, tn), lambda g0, g1, k: (ij(g0, g1)[1] // nnh, k, ij(g0, g1)[1] % nnh))
        else:
            b_spec = pl.BlockSpec((tk, tn), lambda g0, g1, k: (k, ij(g0, g1)[1]))
    if out_halves:
        nnh = nn // 2
        o_spec = pl.BlockSpec((None, tm, tn), lambda g0, g1, k: (ij(g0, g1)[1] // nnh, ij(g0, g1)[0], ij(g0, g1)[1] % nnh))
        o_shape = jax.ShapeDtypeStruct((2, M, N // 2), out_dtype)
    else:
        o_spec = pl.BlockSpec((tm, tn), lambda g0, g1, k: ij(g0, g1))
        o_shape = jax.ShapeDtypeStruct((M, N), out_dtype)
    in_specs = [a_spec, b_spec]
    args = [a, b]
    if res is not None:
        in_specs.append(pl.BlockSpec((tm, tn), lambda g0, g1, k: ij(g0, g1)))
        args.append(res)

    def body(*refs):
        if res is not None:
            a_ref, b_ref, r_ref, o_ref = refs[:4]
        else:
            a_ref, b_ref, o_ref = refs[:3]
            r_ref = None
        part = lax.dot_general(a_ref[...], b_ref[...], dims, preferred_element_type=F32)
        if nk == 1:
            if r_ref is not None:
                part = part + r_ref[...]
            o_ref[...] = part.astype(out_dtype)
            return
        acc_ref = refs[-1]
        k = pl.program_id(2)

        @pl.when(k == 0)
        def _():
            acc_ref[...] = part

        @pl.when(k > 0)
        def _():
            acc_ref[...] += part

        @pl.when(k == nk - 1)
        def _():
            tot = acc_ref[...]
            if r_ref is not None:
                tot = tot + r_ref[...]
            o_ref[...] = tot.astype(out_dtype)

    grid = (nm, nn, nk) if outer == "i" else (nn, nm, nk)
    scratch = [] if nk == 1 else [pltpu.VMEM((tm, tn), F32)]
    return pl.pallas_call(
        body, out_shape=o_shape, grid=grid, in_specs=in_specs, out_specs=o_spec, scratch_shapes=scratch,
        name=name, compiler_params=_cp(("parallel", "parallel", "arbitrary"), VMEM_LIMIT),
    )(*args)


def _aug(lane, terms):
    out = 0.0
    for j, t in enumerate(terms):
        out = jnp.where(lane == HEAD_DIM + j, t, out)
    return out


def _split3f(x):
    hi, mid, lo = _split3(x)
    return [hi.astype(F32), mid.astype(F32), lo.astype(F32)]


def _inproj_fwd(x, g_mix, w_pad, bf_pad):
    S = x.shape[0]
    tm = _pick(S, (512, 256))
    tri = (lax.broadcasted_iota(jnp.int32, (tm, tm), 0) >= lax.broadcasted_iota(jnp.int32, (tm, tm), 1)).astype(BF16)

    def body(x_ref, g_ref, w_ref, bf_ref, tri_ref, put_ref, one_ref, xn_ref, qa_ref, ka_ref, va_ref, ug_ref, zf_ref,
             carry_ref):
        i = pl.program_id(0)

        @pl.when(i == 0)
        def _():
            carry_ref[...] = jnp.zeros_like(carry_ref)

        xf = x_ref[...]
        r = lax.rsqrt(jnp.mean(xf * xf, axis=-1, keepdims=True) + EPS)
        xn = ((xf * r) * g_ref[...]).astype(BF16)
        xn_ref[...] = xn
        proj = lax.dot_general(xn, w_ref[...], _NT, preferred_element_type=F32)
        ug_ref[...] = proj[:, QKV:UG_END]
        zf = proj[:, UG_END:] + bf_ref[...]
        zf_ref[...] = zf
        lf = jnp.minimum(zf, 0.0) - jnp.log(1.0 + jnp.exp(-jnp.abs(zf)))
        c = _dot3l(tri_ref[...], lf) + carry_ref[0:1, :]
        carry_ref[0:1, :] = c[tm - 1:tm, :]
        c3 = jnp.concatenate(_split3(c), axis=1)
        aug_q = jnp.dot(c3, put_ref[0], preferred_element_type=F32) + one_ref[0:1, :]
        aug_k = jnp.dot(c3, put_ref[1], preferred_element_type=F32) + one_ref[1:2, :]
        lane = lax.broadcasted_iota(jnp.int32, (tm, LANES), 1)
        for h in range(N_HEADS):
            p, odd = h // 2, h % 2

            def head(base, scale=None, p=p, odd=odd):
                blk = proj[:, base + p * LANES:base + (p + 1) * LANES]
                if scale is not None:
                    blk = blk * scale
                return pltpu.roll(blk, HEAD_DIM, 1) if odd else blk

            cols = slice(h * LANES, (h + 1) * LANES)
            qa_ref[:, cols] = jnp.where(lane < HEAD_DIM, head(0, HEAD_DIM ** -0.5), aug_q[:, cols]).astype(BF16)
            ka_ref[:, cols] = jnp.where(lane < HEAD_DIM, head(ATT_WIDTH), aug_k[:, cols]).astype(BF16)
            va_ref[:, cols] = jnp.where(lane < HEAD_DIM, head(2 * ATT_WIDTH), one_ref[2:3, cols]).astype(BF16)

    wide = N_HEADS * LANES
    src = lax.broadcasted_iota(jnp.int32, (3 * LANES, wide), 0)
    col = lax.broadcasted_iota(jnp.int32, (3 * LANES, wide), 1)
    hd, term = src % LANES, src // LANES
    to_q = (col == hd * LANES + HEAD_DIM + term) & (hd < N_HEADS)
    to_k = (col == hd * LANES + HEAD_DIM + 3 + term) & (hd < N_HEADS)
    put = jnp.stack([to_q.astype(BF16), -to_k.astype(BF16)])
    off = lax.broadcasted_iota(jnp.int32, (8, wide), 1) % LANES - HEAD_DIM
    row = lax.broadcasted_iota(jnp.int32, (8, wide), 0)
    q_one = (off >= 3) & (off < 6)
    k_one = ((off >= 0) & (off < 3)) | ((off >= 6) & (off < 9))
    v_one = (off >= 0) & (off < 3)
    ones = jnp.where(row == 0, q_one, jnp.where(row == 1, k_one, (row == 2) & v_one)).astype(F32)
    return pl.pallas_call(
        body,
        out_shape=(jax.ShapeDtypeStruct((S, D_MODEL), BF16), jax.ShapeDtypeStruct((S, wide), BF16),
                   jax.ShapeDtypeStruct((S, wide), BF16), jax.ShapeDtypeStruct((S, wide), BF16),
                   jax.ShapeDtypeStruct((S, 2 * GMLP_WIDTH), F32), jax.ShapeDtypeStruct((S, LANES), F32)),
        grid=(S // tm,),
        in_specs=[pl.BlockSpec((tm, D_MODEL), lambda i: (i, 0)), pl.BlockSpec((1, D_MODEL), lambda i: (0, 0)),
                  pl.BlockSpec((IN_PAD, D_MODEL), lambda i: (0, 0)), pl.BlockSpec((1, LANES), lambda i: (0, 0)),
                  pl.BlockSpec((tm, tm), lambda i: (0, 0)), pl.BlockSpec((2, 3 * LANES, wide), lambda i: (0, 0, 0)),
                  pl.BlockSpec((8, wide), lambda i: (0, 0))],
        out_specs=(pl.BlockSpec((tm, D_MODEL), lambda i: (i, 0)), pl.BlockSpec((tm, wide), lambda i: (i, 0)),
                   pl.BlockSpec((tm, wide), lambda i: (i, 0)), pl.BlockSpec((tm, wide), lambda i: (i, 0)),
                   pl.BlockSpec((tm, 2 * GMLP_WIDTH), lambda i: (i, 0)), pl.BlockSpec((tm, LANES), lambda i: (i, 0))),
        scratch_shapes=[pltpu.VMEM((8, LANES), F32)],
        name="inproj_fwd", compiler_params=_cp(("arbitrary",), VMEM_LIMIT),
    )(x, g_mix, w_pad, bf_pad, tri, put, ones)


def _group_ones():
    r = lax.broadcasted_iota(jnp.int32, (GMLP_WIDTH, GMLP_WIDTH), 0) // GROUP_DIM
    c = lax.broadcasted_iota(jnp.int32, (GMLP_WIDTH, GMLP_WIDTH), 1) // GROUP_DIM
    return (r == c).astype(BF16)


def _gmlp_mixed(vn_bf, w_ref, bias, n_chunks):
    lane = lax.broadcasted_iota(jnp.int32, (CHUNK, LANES), 1)
    row = lax.broadcasted_iota(jnp.int32, (CHUNK, CHUNK), 0)
    col = lax.broadcasted_iota(jnp.int32, (CHUNK, CHUNK), 1)
    ws = [jnp.where(row >= col, w_ref[g], 0.0).astype(BF16) for g in range(N_GROUPS)]
    rows = []
    for ci in range(n_chunks):
        cols = []
        for pp in range(N_GROUPS // 2):
            v = vn_bf[ci * CHUNK:(ci + 1) * CHUNK, pp * LANES:(pp + 1) * LANES]
            v_lo = jnp.where(lane < GROUP_DIM, v, jnp.zeros_like(v))
            v_hi = jnp.where(lane >= GROUP_DIM, v, jnp.zeros_like(v))
            m = (jnp.dot(ws[2 * pp], v_lo, preferred_element_type=F32)
                 + jnp.dot(ws[2 * pp + 1], v_hi, preferred_element_type=F32))
            cols.append(m + bias[:, pp * LANES:(pp + 1) * LANES])
        rows.append(jnp.concatenate(cols, axis=1))
    return jnp.concatenate(rows, axis=0)


def _gmlp_fwd(ug, gain, w_s, bias_full):
    S = ug.shape[0]
    tm = _pick(S, (512, 256, 128))
    ones = _group_ones()

    def body(ug_ref, gain_ref, w_ref, bias_ref, ones_ref, sg_ref):
        u = _gelu(ug_ref[:, :GMLP_WIDTH])
        vr = _gelu(ug_ref[:, GMLP_WIDTH:])
        ms = _dot3(vr * vr, ones_ref[...]) * (1.0 / GROUP_DIM)
        vn = ((vr * lax.rsqrt(ms + EPS)) * gain_ref[...]).astype(BF16)
        mixed = _gmlp_mixed(vn, w_ref, bias_ref[...], tm // CHUNK)
        sg_ref[...] = (u * mixed).astype(BF16)

    return pl.pallas_call(
        body, out_shape=jax.ShapeDtypeStruct((S, GMLP_WIDTH), BF16), grid=(S // tm,),
        in_specs=[pl.BlockSpec((tm, 2 * GMLP_WIDTH), lambda i: (i, 0)), pl.BlockSpec((1, GMLP_WIDTH), lambda i: (0, 0)),
                  pl.BlockSpec((N_GROUPS, CHUNK, CHUNK), lambda i: (0, 0, 0)),
                  pl.BlockSpec((CHUNK, GMLP_WIDTH), lambda i: (0, 0)),
                  pl.BlockSpec((GMLP_WIDTH, GMLP_WIDTH), lambda i: (0, 0))],
        out_specs=pl.BlockSpec((tm, GMLP_WIDTH), lambda i: (i, 0)),
        name="gmlp_fwd", compiler_params=_cp(("parallel",), VMEM_LIMIT),
    )(ug, gain, w_s, bias_full, ones)


_NT = (((1,), (1,)), ((), ()))
_TN = (((0,), (0,)), ((), ()))


def _attn_fwd(qa, ka, va):
    S = qa.shape[0]
    tq = _pick(S, (ATT_TQ, 256))
    tk = min(ATT_TK, tq)
    nq = S // tq
    assert tq == tk, "the diagonal block is handled as one tq x tq tile"
    per_q = 1

    def body(q_ref, k_ref, v_ref, o_ref, lse_ref, ob_ref):
        qi = pl.program_id(1)
        lane = lax.broadcasted_iota(jnp.int32, (tq, LANES), 1)
        qs = [q_ref[:, :LANES], q_ref[:, LANES:]]

        def update(q, ks, k_len, h, m, acc, first_row):
            cols = slice(h * LANES, (h + 1) * LANES)
            s = lax.dot_general(q, k_ref[pl.ds(ks, k_len), cols], _NT, preferred_element_type=F32)
            if first_row is not None:
                rid = lax.broadcasted_iota(jnp.int32, s.shape, 0) + first_row
                s = jnp.where(rid >= lax.broadcasted_iota(jnp.int32, s.shape, 1), s, -jnp.inf)
            m_new = jnp.maximum(m, jnp.max(s, axis=-1, keepdims=True))
            p = jnp.exp(s - m_new).astype(BF16)
            acc = jnp.exp(m - m_new) * acc + jnp.dot(p, v_ref[pl.ds(ks, k_len), cols], preferred_element_type=F32)
            return m_new, acc

        def step(kb, carry):
            ks = pl.multiple_of(kb * tk, tk)
            return tuple(update(qs[h], ks, tk, h, *carry[h], None) for h in range(2))

        one = (jnp.full((tq, 1), -jnp.inf, F32), jnp.zeros((tq, LANES), F32))
        carry = lax.fori_loop(0, qi * per_q, step, (one, one))
        outs, lses = [], []
        strip = tq // 2
        diag = pl.multiple_of(qi * tq, tq)
        for h in range(2):
            ms, accs = [], []
            for r in range(2):
                rows = slice(r * strip, (r + 1) * strip)
                m, acc = update(qs[h][rows], diag, (r + 1) * strip, h, carry[h][0][rows], carry[h][1][rows], r * strip)
                ms.append(m)
                accs.append(acc)
            m, acc = jnp.concatenate(ms, axis=0), jnp.concatenate(accs, axis=0)
            l = acc[:, HEAD_DIM:HEAD_DIM + 1]
            outs.append(acc / l)
            lses.append(m + jnp.log(l))
        o = jnp.where(lane < HEAD_DIM, outs[0], pltpu.roll(outs[1], HEAD_DIM, 1))
        o_ref[...] = o
        ob_ref[...] = o.astype(BF16)
        lse_ref[...] = jnp.where(lane < HEAD_DIM, lses[0], lses[1])

    return pl.pallas_call(
        body,
        out_shape=(jax.ShapeDtypeStruct((S, ATT_WIDTH), F32), jax.ShapeDtypeStruct((S, ATT_WIDTH), F32),
                   jax.ShapeDtypeStruct((S, ATT_WIDTH), BF16)),
        grid=(N_PAIRS, nq),
        in_specs=[pl.BlockSpec((tq, 2 * LANES), lambda p, i: (i, p)),
                  pl.BlockSpec((S, 2 * LANES), lambda p, i: (0, p)),
                  pl.BlockSpec((S, 2 * LANES), lambda p, i: (0, p))],
        out_specs=(pl.BlockSpec((tq, LANES), lambda p, i: (i, p)), pl.BlockSpec((tq, LANES), lambda p, i: (i, p)),
                   pl.BlockSpec((tq, LANES), lambda p, i: (i, p))),
        name="attn_fwd", compiler_params=_cp(("parallel", "parallel"), VMEM_LIMIT),
    )(qa, ka, va)


def _shift_rows(x, prev, n):
    rid = lax.broadcasted_iota(jnp.int32, x.shape, 0)
    y = pltpu.roll(x, n, 0)
    if n == 1:
        return jnp.where(rid == 0, prev[7:8, :], y)
    return jnp.where(rid == 0, prev[6:7, :], jnp.where(rid == 1, prev[7:8, :], y))


def _shift_rows_up(x, nxt, n):
    rows = x.shape[0]
    rid = lax.broadcasted_iota(jnp.int32, x.shape, 0)
    y = pltpu.roll(x, rows - n, 0)
    if n == 1:
        return jnp.where(rid == rows - 1, nxt[0:1, :], y)
    return jnp.where(rid == rows - 2, nxt[0:1, :], jnp.where(rid == rows - 1, nxt[1:2, :], y))


def _conv3(cur, prev, w, b):
    return (w[0:1, :] * _shift_rows(cur, prev, 2) + w[1:2, :] * _shift_rows(cur, prev, 1)
            + w[2:3, :] * cur + b)


def _ffn_up_conv(hn, w_up_bf, cw, cb):
    S = hn.shape[0]
    F = D_FF
    tm = _pick(S, (FFN_TM, 256))
    tn = _pick(F, (FFN_TN, 256, 128))
    nj = F // tn

    def body(hn_ref, wa_ref, wg_ref, cw_ref, cb_ref, hu_ref, hc_ref, act_ref, tail_ref):
        i = pl.program_id(1)

        @pl.when(i == 0)
        def _():
            tail_ref[...] = jnp.zeros_like(tail_ref)

        hn_v = hn_ref[...]
        halves = []
        for h, w_ref in enumerate((wa_ref, wg_ref)):
            hu = lax.dot_general(hn_v, w_ref[...], _NT, preferred_element_type=F32)
            hu_ref[h] = hu.astype(BF16)
            hc = _conv3(hu, tail_ref[h], cw_ref[h], cb_ref[h])
            hc_ref[h] = hc
            halves.append(hc)
            tail_ref[h] = hu[tm - 8:, :]
        a, g = halves
        act_ref[...] = (g * _sigmoid(g) * a).astype(BF16)

    both = pl.BlockSpec((2, tm, tn), lambda j, i: (0, i, j))
    return pl.pallas_call(
        body, out_shape=(jax.ShapeDtypeStruct((2, S, F), BF16), jax.ShapeDtypeStruct((2, S, F), F32),
                         jax.ShapeDtypeStruct((S, F), BF16)),
        grid=(nj, S // tm),
        in_specs=[pl.BlockSpec((tm, D_MODEL), lambda j, i: (i, 0)),
                  pl.BlockSpec((tn, D_MODEL), lambda j, i: (j, 0)),
                  pl.BlockSpec((tn, D_MODEL), lambda j, i: (nj + j, 0)),
                  pl.BlockSpec((2, 8, tn), lambda j, i: (0, 0, j)),
                  pl.BlockSpec((2, 1, tn), lambda j, i: (0, 0, j))],
        out_specs=(both, both, pl.BlockSpec((tm, tn), lambda j, i: (i, j))),
        scratch_shapes=[pltpu.VMEM((2, 8, tn), F32)],
        name="ffn_up_conv", compiler_params=_cp(("parallel", "arbitrary"), VMEM_LIMIT),
    )(hn, w_up_bf, w_up_bf, cw, cb)


def _ffn_down_loss(act, w_down_bf, h1, g_final, target):
    S = h1.shape[0]
    tm = _pick(S, (512, 256))

    def body(a_ref, w_ref, h1_ref, g_ref, t_ref, loss_ref, dh_ref, dhb_ref, dg_ref):
        i = pl.program_id(0)

        @pl.when(i == 0)
        def _():
            loss_ref[...] = jnp.zeros_like(loss_ref)
            dg_ref[...] = jnp.zeros_like(dg_ref)

        hf = h1_ref[...] + jnp.dot(a_ref[...], w_ref[...], preferred_element_type=F32)
        g = g_ref[...]
        r = lax.rsqrt(jnp.mean(hf * hf, axis=-1, keepdims=True) + EPS)
        hhat = hf * r
        err = hhat * g - t_ref[...]
        loss_ref[...] += 0.5 * jnp.sum(jnp.mean(err * err, axis=-1, keepdims=True))
        dy = err * (1.0 / D_MODEL)
        dg_ref[0:1, :] += jnp.sum(dy * hhat, axis=0, keepdims=True)
        dhat = dy * g
        dh = r * (dhat - hhat * jnp.mean(dhat * hhat, axis=-1, keepdims=True))
        dh_ref[...] = dh
        dhb_ref[...] = dh.astype(BF16)

    row = pl.BlockSpec((tm, D_MODEL), lambda i: (i, 0))
    return pl.pallas_call(
        body,
        out_shape=(jax.ShapeDtypeStruct((8, LANES), F32), jax.ShapeDtypeStruct((S, D_MODEL), F32),
                   jax.ShapeDtypeStruct((S, D_MODEL), BF16), jax.ShapeDtypeStruct((8, D_MODEL), F32)),
        grid=(S // tm,),
        in_specs=[pl.BlockSpec((tm, D_FF), lambda i: (i, 0)), pl.BlockSpec((D_FF, D_MODEL), lambda i: (0, 0)), row,
                  pl.BlockSpec((1, D_MODEL), lambda i: (0, 0)), row],
        out_specs=(pl.BlockSpec((8, LANES), lambda i: (0, 0)), row, row, pl.BlockSpec((8, D_MODEL), lambda i: (0, 0))),
        name="ffn_down_loss", compiler_params=_cp(("arbitrary",), VMEM_LIMIT),
    )(act, w_down_bf, h1, g_final, target)


def _ffn_up_dx_rms(dhu, w_up_bf, h1, g_ffn, dh2):
    _, S, F = dhu.shape
    tm = _pick(S, (512, 256))

    def body(a_ref, b_ref, h_ref, g_ref, r_ref, dh_ref, dhb_ref, dg_ref):
        i = pl.program_id(0)

        @pl.when(i == 0)
        def _():
            dg_ref[...] = jnp.zeros_like(dg_ref)

        dyv = (jnp.dot(a_ref[0], b_ref[:F, :], preferred_element_type=F32)
               + jnp.dot(a_ref[1], b_ref[F:, :], preferred_element_type=F32))
        hf = h_ref[...]
        r = lax.rsqrt(jnp.mean(hf * hf, axis=-1, keepdims=True) + EPS)
        hhat = hf * r
        dg_ref[0:1, :] += jnp.sum(dyv * hhat, axis=0, keepdims=True)
        dhat = dyv * g_ref[...]
        dh = r_ref[...] + r * (dhat - hhat * jnp.mean(dhat * hhat, axis=-1, keepdims=True))
        dh_ref[...] = dh
        dhb_ref[...] = dh.astype(BF16)

    row = pl.BlockSpec((tm, D_MODEL), lambda i: (i, 0))
    return pl.pallas_call(
        body,
        out_shape=(jax.ShapeDtypeStruct((S, D_MODEL), F32), jax.ShapeDtypeStruct((S, D_MODEL), BF16),
                   jax.ShapeDtypeStruct((8, D_MODEL), F32)),
        grid=(S // tm,),
        in_specs=[pl.BlockSpec((2, tm, F), lambda i: (0, i, 0)), pl.BlockSpec((2 * F, D_MODEL), lambda i: (0, 0)),
                  row, pl.BlockSpec((1, D_MODEL), lambda i: (0, 0)), row],
        out_specs=(row, row, pl.BlockSpec((8, D_MODEL), lambda i: (0, 0))),
        name="ffn_up_dx_rms", compiler_params=_cp(("arbitrary",), VMEM_LIMIT),
    )(dhu, w_up_bf, h1, g_ffn, dh2)


def _conv_gate_bwd(hc, hu, dact, cw):
    _, S, F = hu.shape
    tm = _pick(S, (CONV_TM, 128))
    tn = _pick(F, (CONV_TN, 256, 128))
    r8 = tm // 8
    n_i = S // tm
    last8 = S // 8 - 1

    def body(hc_ref, hcn_ref, hu_ref, da_ref, dan_ref, w_ref, dhu_ref, dcw_ref):
        i = pl.program_id(1)

        @pl.when(i == 0)
        def _():
            dcw_ref[...] = jnp.zeros_like(dcw_ref)

        rid8 = lax.broadcasted_iota(jnp.int32, (8, tn), 0)

        def gate_grads(a, g, d):
            sg = _sigmoid(g)
            return d * (g * sg), d * a * (sg * (1.0 + g * (1.0 - sg)))

        dhc = gate_grads(hc_ref[0], hc_ref[1], da_ref[...])
        dhc_n = gate_grads(hcn_ref[0], hcn_ref[1], dan_ref[...])
        for h in range(2):
            w = w_ref[h]
            d = dhc[h]
            dn = jnp.where(i < n_i - 1, dhc_n[h], 0.0)
            u1 = _shift_rows_up(d, dn, 1)
            u2 = _shift_rows_up(d, dn, 2)
            dhu_ref[h] = (w[2:3, :] * d + w[1:2, :] * u1 + w[0:1, :] * u2).astype(BF16)
            x = hu_ref[h].astype(F32)
            t0, t1, t2, t3 = [jnp.sum(t, axis=0, keepdims=True) for t in (u2 * x, u1 * x, d * x, d)]
            dcw_ref[h] += jnp.where(rid8 == 0, t0, jnp.where(rid8 == 1, t1, jnp.where(rid8 == 2, t2, jnp.where(rid8 == 3, t3, 0.0))))

    cur = pl.BlockSpec((2, tm, tn), lambda j, i: (0, i, j))
    return pl.pallas_call(
        body,
        out_shape=(jax.ShapeDtypeStruct((2, S, F), BF16), jax.ShapeDtypeStruct((2, 8, F), F32)),
        grid=(F // tn, n_i),
        in_specs=[cur, pl.BlockSpec((2, 8, tn), lambda j, i: (0, jnp.minimum((i + 1) * r8, last8), j)), cur,
                  pl.BlockSpec((tm, tn), lambda j, i: (i, j)),
                  pl.BlockSpec((8, tn), lambda j, i: (jnp.minimum((i + 1) * r8, last8), j)),
                  pl.BlockSpec((2, 8, tn), lambda j, i: (0, 0, j))],
        out_specs=(cur, pl.BlockSpec((2, 8, tn), lambda j, i: (0, 0, j))),
        name="conv_gate_bwd", compiler_params=_cp(("parallel", "arbitrary"), VMEM_LIMIT),
    )(hc, hc, hu, dact, dact, cw)


def _out_proj_dx_prep(dh1_bf, w_out_bf, att, lse, qa):
    S = att.shape[0]
    tm = _pick(S, (256,))

    def body(dh_ref, w_ref, o_ref, lse_ref, q_ref, dsg_ref, qb_ref, doa_ref):
        lane = lax.broadcasted_iota(jnp.int32, (tm, LANES), 1)
        dh = dh_ref[...]
        dsg_ref[...] = lax.dot_general(dh, w_ref[ATT_WIDTH:, :], _NT, preferred_element_type=F32)
        datt = lax.dot_general(dh, w_ref[:ATT_WIDTH, :], _NT, preferred_element_type=F32)
        for p in range(N_PAIRS):
            pc = slice(p * LANES, (p + 1) * LANES)
            do = datt[:, pc]
            prod = o_ref[:, pc] * do
            for hh in range(2):
                sel = (lane >= HEAD_DIM) if hh else (lane < HEAD_DIM)
                delta = jnp.sum(jnp.where(sel, prod, 0.0), axis=-1, keepdims=True)
                dod = pltpu.roll(do, HEAD_DIM, 1) if hh else do
                cols = slice((2 * p + hh) * LANES, (2 * p + hh + 1) * LANES)
                doa_ref[:, cols] = jnp.where(lane < HEAD_DIM, dod, _aug(lane, _split3f(-delta))).astype(BF16)
                lcol = p * LANES + hh * HEAD_DIM
                l3 = _split3f(-lse_ref[:, lcol:lcol + 1])
                augl = jnp.where(lane == HEAD_DIM + 6, l3[0], jnp.where(lane == HEAD_DIM + 7, l3[1], l3[2])).astype(BF16)
                qb_ref[:, cols] = jnp.where((lane >= HEAD_DIM + 6) & (lane < HEAD_DIM + 9), augl, q_ref[:, cols])

    half = pl.BlockSpec((tm, ATT_WIDTH), lambda i: (i, 0))
    wide = pl.BlockSpec((tm, N_HEADS * LANES), lambda i: (i, 0))
    return pl.pallas_call(
        body,
        out_shape=(jax.ShapeDtypeStruct((S, GMLP_WIDTH), F32), jax.ShapeDtypeStruct(qa.shape, BF16),
                   jax.ShapeDtypeStruct(qa.shape, BF16)),
        grid=(S // tm,),
        in_specs=[pl.BlockSpec((tm, D_MODEL), lambda i: (i, 0)), pl.BlockSpec((D_MODEL, D_MODEL), lambda i: (0, 0)),
                  half, half, wide],
        out_specs=(half, wide, wide),
        name="out_proj_dx_prep", compiler_params=_cp(("parallel",), VMEM_LIMIT),
    )(dh1_bf, w_out_bf, att, lse, qa)


def _attn_bwd(qb, ka, va, doa):
    S = qb.shape[0]
    tk = _pick(S, (512, 256))
    tq = tk
    nq = S // tq

    def pair(a, scale=None):
        lane = lax.broadcasted_iota(jnp.int32, (a.shape[0], LANES), 1)
        out = jnp.where(lane < HEAD_DIM, a[:, :LANES], pltpu.roll(a[:, LANES:], HEAD_DIM, 1))
        return out if scale is None else out * scale

    def head_lanes(a, col, sign, first):
        lane = lax.broadcasted_iota(jnp.int32, (a.shape[0], LANES), 1)
        return jnp.where(lane == first, sign * a[:, col:col + 1],
                         jnp.where(lane == first + 1, sign * a[:, LANES + col:LANES + col + 1], 0.0))

    def body(q_ref, do_ref, k_ref, v_ref, dqc_ref, dkc_ref, dvc_ref, dcq_ref, dck_ref, dq_ref, dka_ref, dva_ref):
        kb = pl.program_id(1)

        @pl.when(kb == 0)
        def _():
            dq_ref[...] = jnp.zeros_like(dq_ref)

        dka_ref[...] = jnp.zeros_like(dka_ref)
        dva_ref[...] = jnp.zeros_like(dva_ref)
        def sub_tile(qs, q_len, k_off, k_len, masked):
            keys = slice(k_off, k_off + k_len)
            for h in range(2):
                cols = slice(h * LANES, (h + 1) * LANES)
                qblk = q_ref[pl.ds(qs, q_len), cols]
                doblk = do_ref[pl.ds(qs, q_len), cols]
                kh = k_ref[keys, cols]
                p = jnp.exp(lax.dot_general(kh, qblk, _NT, preferred_element_type=F32))
                if masked:
                    p = jnp.where(lax.broadcasted_iota(jnp.int32, p.shape, 1) >= lax.broadcasted_iota(jnp.int32, p.shape, 0),
                                  p, 0.0)
                ds = (p * lax.dot_general(v_ref[keys, cols], doblk, _NT, preferred_element_type=F32)).astype(BF16)
                dva_ref[keys, cols] += jnp.dot(p.astype(BF16), doblk, preferred_element_type=F32)
                dka_ref[keys, cols] += jnp.dot(ds, qblk, preferred_element_type=F32)
                dq_ref[pl.ds(qs, q_len), cols] += lax.dot_general(ds, kh, _TN, preferred_element_type=F32)

        half = tk // 2
        sub_tile(pl.multiple_of(kb * tq, tq), tq, 0, half, True)
        sub_tile(pl.multiple_of(kb * tq + half, half), half, half, half, True)

        def step(qi, carry):
            sub_tile(pl.multiple_of(qi * tq, tq), tq, 0, tk, False)
            return carry

        lax.fori_loop(kb + 1, nq, step, 0)
        dka = dka_ref[...]
        dkc_ref[...] = pair(dka).astype(BF16)
        dvc_ref[...] = pair(dva_ref[...]).astype(BF16)
        first = 2 * pl.program_id(0)
        dck_ref[...] = head_lanes(dka, HEAD_DIM + 3, -1.0, first)

        @pl.when(kb == nq - 1)
        def _():
            dqa = dq_ref[...]
            dqc_ref[...] = pair(dqa, HEAD_DIM ** -0.5).astype(BF16)
            dcq_ref[...] = head_lanes(dqa, HEAD_DIM, 1.0, first)

    wide = 2 * LANES
    half = jax.ShapeDtypeStruct((S, ATT_WIDTH), BF16)
    slabs = jax.ShapeDtypeStruct((N_PAIRS, S, LANES), F32)
    return pl.pallas_call(
        body,
        out_shape=(half, half, half, slabs, slabs),
        grid=(N_PAIRS, nq),
        in_specs=[pl.BlockSpec((S, wide), lambda p, j: (0, p)), pl.BlockSpec((S, wide), lambda p, j: (0, p)),
                  pl.BlockSpec((tk, wide), lambda p, j: (j, p)), pl.BlockSpec((tk, wide), lambda p, j: (j, p))],
        out_specs=(pl.BlockSpec((S, LANES), lambda p, j: (0, p)), pl.BlockSpec((tk, LANES), lambda p, j: (j, p)),
                   pl.BlockSpec((tk, LANES), lambda p, j: (j, p)), pl.BlockSpec((None, S, LANES), lambda p, j: (p, 0, 0)),
                   pl.BlockSpec((None, tk, LANES), lambda p, j: (p, j, 0))),
        scratch_shapes=[pltpu.VMEM((S, wide), F32), pltpu.VMEM((tk, wide), F32), pltpu.VMEM((tk, wide), F32)],
        name="attn_bwd", compiler_params=_cp(("parallel", "arbitrary"), VMEM_LIMIT),
    )(qb, doa, ka, va)


def _gmlp_bwd(ug, dsg, gain, w_s, wt_s, bias_full):
    S = ug.shape[0]
    tm = _pick(S, (512, 256, 128))
    n_chunks = tm // CHUNK
    n_i = S // tm
    ones = _group_ones()
    nt = (((1,), (1,)), ((), ()))

    def body(ug_ref, dsg_ref, gain_ref, w_ref, wt_ref, bias_ref, ones_ref, dug_ref, dw_ref, dgain_ref, dbias_ref,
             dbacc_ref):
        i = pl.program_id(0)

        @pl.when(i == 0)
        def _():
            dw_ref[...] = jnp.zeros_like(dw_ref)
            dgain_ref[...] = jnp.zeros_like(dgain_ref)
            dbacc_ref[...] = jnp.zeros_like(dbacc_ref)

        ones_m = ones_ref[...]
        pu = ug_ref[:, :GMLP_WIDTH]
        pg = ug_ref[:, GMLP_WIDTH:]
        u = _gelu(pu)
        vr = _gelu(pg)
        ms = _dot3(vr * vr, ones_m) * (1.0 / GROUP_DIM)
        rinv = lax.rsqrt(ms + EPS)
        vhat = vr * rinv
        gain_v = gain_ref[...]
        vn = (vhat * gain_v).astype(BF16)
        mixed = _gmlp_mixed(vn, w_ref, bias_ref[...], n_chunks)
        dsg_v = dsg_ref[...]
        du = dsg_v * mixed
        dmixed = dsg_v * u
        dm_bf = dmixed.astype(BF16)
        lane = lax.broadcasted_iota(jnp.int32, (CHUNK, LANES), 1)
        row = lax.broadcasted_iota(jnp.int32, (CHUNK, CHUNK), 0)
        col = lax.broadcasted_iota(jnp.int32, (CHUNK, CHUNK), 1)
        wts = [jnp.where(col >= row, wt_ref[g], 0.0).astype(BF16) for g in range(N_GROUPS)]
        dvn_rows = []
        dbsum = jnp.zeros((CHUNK, GMLP_WIDTH), F32)
        for ci in range(n_chunks):
            rs = slice(ci * CHUNK, (ci + 1) * CHUNK)
            dbsum = dbsum + dmixed[rs, :]
            cols = []
            for pp in range(N_GROUPS // 2):
                cs = slice(pp * LANES, (pp + 1) * LANES)
                dm = dm_bf[rs, cs]
                dm_lo = jnp.where(lane < GROUP_DIM, dm, jnp.zeros_like(dm))
                dm_hi = jnp.where(lane >= GROUP_DIM, dm, jnp.zeros_like(dm))
                vb = vn[rs, cs]
                dw_ref[2 * pp] += lax.dot_general(dm_lo, vb, nt, preferred_element_type=F32)
                dw_ref[2 * pp + 1] += lax.dot_general(dm_hi, vb, nt, preferred_element_type=F32)
                cols.append(jnp.dot(wts[2 * pp], dm_lo, preferred_element_type=F32)
                            + jnp.dot(wts[2 * pp + 1], dm_hi, preferred_element_type=F32))
            dvn_rows.append(jnp.concatenate(cols, axis=1))
        dvn = jnp.concatenate(dvn_rows, axis=0)
        dbacc_ref[...] += dbsum
        dgain_ref[0:1, :] += jnp.sum(dvn * vhat, axis=0, keepdims=True)
        dvhat = dvn * gain_v
        gm = _dot3(dvhat * vhat, ones_m) * (1.0 / GROUP_DIM)
        dvr = rinv * (dvhat - vhat * gm)
        dug_ref[:, :GMLP_WIDTH] = (du * _gelu_grad(pu)).astype(BF16)
        dug_ref[:, GMLP_WIDTH:] = (dvr * _gelu_grad(pg)).astype(BF16)

        @pl.when(i == n_i - 1)
        def _():
            for g in range(N_GROUPS):
                dw_ref[g] = jnp.where(row >= col, dw_ref[g], 0.0)
            dbias_ref[...] = _dot3(dbacc_ref[...], ones_m)

    return pl.pallas_call(
        body,
        out_shape=(jax.ShapeDtypeStruct((S, 2 * GMLP_WIDTH), BF16), jax.ShapeDtypeStruct((N_GROUPS, CHUNK, CHUNK), F32),
                   jax.ShapeDtypeStruct((8, GMLP_WIDTH), F32), jax.ShapeDtypeStruct((CHUNK, GMLP_WIDTH), F32)),
        grid=(n_i,),
        in_specs=[pl.BlockSpec((tm, 2 * GMLP_WIDTH), lambda i: (i, 0)), pl.BlockSpec((tm, GMLP_WIDTH), lambda i: (i, 0)),
                  pl.BlockSpec((1, GMLP_WIDTH), lambda i: (0, 0)),
                  pl.BlockSpec((N_GROUPS, CHUNK, CHUNK), lambda i: (0, 0, 0)),
                  pl.BlockSpec((N_GROUPS, CHUNK, CHUNK), lambda i: (0, 0, 0)),
                  pl.BlockSpec((CHUNK, GMLP_WIDTH), lambda i: (0, 0)),
                  pl.BlockSpec((GMLP_WIDTH, GMLP_WIDTH), lambda i: (0, 0))],
        out_specs=(pl.BlockSpec((tm, 2 * GMLP_WIDTH), lambda i: (i, 0)),
                   pl.BlockSpec((N_GROUPS, CHUNK, CHUNK), lambda i: (0, 0, 0)),
                   pl.BlockSpec((8, GMLP_WIDTH), lambda i: (0, 0)),
                   pl.BlockSpec((CHUNK, GMLP_WIDTH), lambda i: (0, 0))),
        scratch_shapes=[pltpu.VMEM((CHUNK, GMLP_WIDTH), F32)],
        name="gmlp_bwd", compiler_params=_cp(("arbitrary",), VMEM_LIMIT),
    )(ug, dsg, gain, w_s, wt_s, bias_full, ones)


def _gate_bwd(dcq, dck, zf):
    S = zf.shape[0]
    tm = _pick(S, (256,))
    n_i = S // tm
    triu = (lax.broadcasted_iota(jnp.int32, (tm, tm), 0) <= lax.broadcasted_iota(jnp.int32, (tm, tm), 1)).astype(BF16)

    def body(dcq_ref, dck_ref, zf_ref, tri_ref, dzf_ref, dbf_ref, carry_ref):
        i = pl.program_id(0)

        @pl.when(i == 0)
        def _():
            carry_ref[...] = jnp.zeros_like(carry_ref)
            dbf_ref[...] = jnp.zeros_like(dbf_ref)

        lane = lax.broadcasted_iota(jnp.int32, (tm, LANES), 1)
        dc = dcq_ref[0] + dck_ref[0]
        for p in range(1, N_PAIRS):
            dc = dc + (dcq_ref[p] + dck_ref[p])
        dlf = _dot3l(tri_ref[...], dc) + carry_ref[0:1, :]
        carry_ref[0:1, :] = dlf[0:1, :]
        dz = jnp.where(lane < N_HEADS, dlf * _sigmoid(-zf_ref[...]), 0.0)
        dzf_ref[...] = dz.astype(BF16)
        dbf_ref[0:1, :] += jnp.sum(dz, axis=0, keepdims=True)

    return pl.pallas_call(
        body,
        out_shape=(jax.ShapeDtypeStruct((S, LANES), BF16), jax.ShapeDtypeStruct((8, LANES), F32)),
        grid=(n_i,),
        in_specs=[pl.BlockSpec((N_PAIRS, tm, LANES), lambda i: (0, n_i - 1 - i, 0)),
                  pl.BlockSpec((N_PAIRS, tm, LANES), lambda i: (0, n_i - 1 - i, 0)),
                  pl.BlockSpec((tm, LANES), lambda i: (n_i - 1 - i, 0)),
                  pl.BlockSpec((tm, tm), lambda i: (0, 0))],
        out_specs=(pl.BlockSpec((tm, LANES), lambda i: (n_i - 1 - i, 0)), pl.BlockSpec((8, LANES), lambda i: (0, 0))),
        scratch_shapes=[pltpu.VMEM((8, LANES), F32)],
        name="gate_bwd", compiler_params=_cp(("arbitrary",), VMEM_LIMIT),
    )(dcq, dck, zf, triu)


def _out_proj_fwd(att_bf, sg, w_out_bf, x, g_ffn):
    S = x.shape[0]
    tm = _pick(S, (512, 256))

    def body(a_ref, s_ref, w_ref, x_ref, g_ref, h_ref, hn_ref):
        h = (x_ref[...] + jnp.dot(a_ref[...], w_ref[:ATT_WIDTH, :], preferred_element_type=F32)
             + jnp.dot(s_ref[...], w_ref[ATT_WIDTH:, :], preferred_element_type=F32))
        h_ref[...] = h
        r = lax.rsqrt(jnp.mean(h * h, axis=-1, keepdims=True) + EPS)
        hn_ref[...] = ((h * r) * g_ref[...]).astype(BF16)

    row = pl.BlockSpec((tm, D_MODEL), lambda i: (i, 0))
    half = pl.BlockSpec((tm, ATT_WIDTH), lambda i: (i, 0))
    return pl.pallas_call(
        body, out_shape=(jax.ShapeDtypeStruct((S, D_MODEL), F32), jax.ShapeDtypeStruct((S, D_MODEL), BF16)),
        grid=(S // tm,),
        in_specs=[half, half, pl.BlockSpec((D_MODEL, D_MODEL), lambda i: (0, 0)), row,
                  pl.BlockSpec((1, D_MODEL), lambda i: (0, 0))],
        out_specs=(row, row), name="out_proj", compiler_params=_cp(("parallel",), VMEM_LIMIT),
    )(att_bf, sg, w_out_bf, x, g_ffn)


def _out_proj_dw(att_bf, sg, dh1_bf):
    S = att_bf.shape[0]
    tk = _pick(S, (1024, 512))

    def body(a_ref, s_ref, d_ref, o_ref):
        k = pl.program_id(0)

        @pl.when(k == 0)
        def _():
            o_ref[...] = jnp.zeros_like(o_ref)

        d = d_ref[...]
        o_ref[:ATT_WIDTH, :] += lax.dot_general(a_ref[...], d, _TN, preferred_element_type=F32)
        o_ref[ATT_WIDTH:, :] += lax.dot_general(s_ref[...], d, _TN, preferred_element_type=F32)

    half = pl.BlockSpec((tk, ATT_WIDTH), lambda k: (k, 0))
    return pl.pallas_call(
        body, out_shape=jax.ShapeDtypeStruct((D_MODEL, D_MODEL), F32), grid=(S // tk,),
        in_specs=[half, half, pl.BlockSpec((tk, D_MODEL), lambda k: (k, 0))],
        out_specs=pl.BlockSpec((D_MODEL, D_MODEL), lambda k: (0, 0)),
        name="out_proj_dw", compiler_params=_cp(("arbitrary",), VMEM_LIMIT),
    )(att_bf, sg, dh1_bf)


_IN_PIECES = ((0, ATT_WIDTH), (ATT_WIDTH, ATT_WIDTH), (2 * ATT_WIDTH, ATT_WIDTH), (QKV, 2 * GMLP_WIDTH), (UG_END, LANES))


def _inproj_bwd_dx(pieces, w_pad, x, g_mix, dh1):
    S = x.shape[0]
    tm = _pick(S, (512, 256))

    def body(*refs):
        p_refs, (w_ref, x_ref, g_ref, r_ref, dx_ref, dg_ref) = refs[:5], refs[5:]
        i = pl.program_id(0)

        @pl.when(i == 0)
        def _():
            dg_ref[...] = jnp.zeros_like(dg_ref)

        dxn = None
        for p_ref, (c0, width) in zip(p_refs, _IN_PIECES):
            part = jnp.dot(p_ref[...], w_ref[c0:c0 + width, :], preferred_element_type=F32)
            dxn = part if dxn is None else dxn + part
        xf = x_ref[...]
        r = lax.rsqrt(jnp.mean(xf * xf, axis=-1, keepdims=True) + EPS)
        xhat = xf * r
        dg_ref[0:1, :] += jnp.sum(dxn * xhat, axis=0, keepdims=True)
        dhat = dxn * g_ref[...]
        dx_ref[...] = r_ref[...] + r * (dhat - xhat * jnp.mean(dhat * xhat, axis=-1, keepdims=True))

    row = pl.BlockSpec((tm, D_MODEL), lambda i: (i, 0))
    return pl.pallas_call(
        body, out_shape=(jax.ShapeDtypeStruct((S, D_MODEL), F32), jax.ShapeDtypeStruct((8, D_MODEL), F32)),
        grid=(S // tm,),
        in_specs=[pl.BlockSpec((tm, width), lambda i: (i, 0)) for _, width in _IN_PIECES]
        + [pl.BlockSpec((IN_PAD, D_MODEL), lambda i: (0, 0)), row, pl.BlockSpec((1, D_MODEL), lambda i: (0, 0)), row],
        out_specs=(row, pl.BlockSpec((8, D_MODEL), lambda i: (0, 0))),
        name="in_proj_dx", compiler_params=_cp(("arbitrary",), VMEM_LIMIT),
    )(*pieces, w_pad, x, g_mix, dh1)


def _inproj_bwd_dw(xn, pieces):
    S = xn.shape[0]
    tk = _pick(S, (1024, 512))

    def body(*refs):
        x_ref, p_refs, o_ref = refs[0], refs[1:6], refs[6]
        k = pl.program_id(0)

        @pl.when(k == 0)
        def _():
            o_ref[...] = jnp.zeros_like(o_ref)

        xb = x_ref[...]
        for p_ref, (c0, width) in zip(p_refs, _IN_PIECES):
            o_ref[:, c0:c0 + width] += lax.dot_general(xb, p_ref[...], _TN, preferred_element_type=F32)

    return pl.pallas_call(
        body, out_shape=jax.ShapeDtypeStruct((D_MODEL, IN_PAD), F32), grid=(S // tk,),
        in_specs=[pl.BlockSpec((tk, D_MODEL), lambda k: (k, 0))]
        + [pl.BlockSpec((tk, width), lambda k: (k, 0)) for _, width in _IN_PIECES],
        out_specs=pl.BlockSpec((D_MODEL, IN_PAD), lambda k: (0, 0)),
        name="in_proj_dw", compiler_params=_cp(("arbitrary",), VMEM_LIMIT),
    )(xn, *pieces)


def _adamw(w, m, v, parts, name):
    R, C = w.shape[-2:]
    tr = R
    for cand in (256, 128, 64, 32, 16, 8):
        if R % cand == 0 and R > cand:
            tr = cand
            break
    c1 = 1.0 / (1.0 - ADAM_B1 ** ADAM_STEP)
    c2 = 1.0 / (1.0 - ADAM_B2 ** ADAM_STEP)

    def body(w_ref, m_ref, v_ref, p_ref, g_ref, d_ref, nm_ref, nv_ref):
        g = p_ref[0].astype(F32)
        for j in range(1, N_DEV):
            g = g + p_ref[j].astype(F32)
        g_ref[...] = g
        nm = ADAM_B1 * m_ref[...] + (1.0 - ADAM_B1) * g
        nv = ADAM_B2 * v_ref[...] + (1.0 - ADAM_B2) * (g * g)
        nm_ref[...] = nm
        nv_ref[...] = nv
        d_ref[...] = -ADAM_LR * ((nm * c1) / (jnp.sqrt(nv * c2) + ADAM_EPS) + ADAM_WD * w_ref[...])

    if w.ndim == 3:
        spec = pl.BlockSpec((None, tr, C), lambda i: (0, i, 0))
    else:
        spec = pl.BlockSpec((tr, C), lambda i: (i, 0))
    shp = jax.ShapeDtypeStruct(w.shape, F32)
    return pl.pallas_call(
        body, out_shape=(shp, shp, shp, shp), grid=(R // tr,),
        in_specs=[spec, spec, spec, pl.BlockSpec((N_DEV, tr, C), lambda i: (0, i, 0))],
        out_specs=(spec, spec, spec, spec),
        name=name, compiler_params=_cp(("parallel",), VMEM_LIMIT),
    )(w, m, v, parts)


def _adamw_owner(w, m, v, landed, sent, me, name):
    R, C = w.shape[-2:]
    tr = R
    for cand in (256, 128, 64, 32, 16, 8):
        if R % cand == 0 and R > cand:
            tr = cand
            break
    c1 = 1.0 / (1.0 - ADAM_B1 ** ADAM_STEP)
    c2 = 1.0 / (1.0 - ADAM_B2 ** ADAM_STEP)

    def body(me_ref, w_ref, m_ref, v_ref, p_ref, own_ref, g_ref, d_ref, nm_ref, nv_ref):
        mine = me_ref[0]
        own = own_ref[...].astype(F32)
        g = jnp.where(mine == 0, own, p_ref[0].astype(F32))
        for j in range(1, N_DEV):
            g = g + jnp.where(mine == j, own, p_ref[j].astype(F32))
        g_ref[...] = g
        nm = ADAM_B1 * m_ref[...] + (1.0 - ADAM_B1) * g
        nv = ADAM_B2 * v_ref[...] + (1.0 - ADAM_B2) * (g * g)
        nm_ref[...] = nm
        nv_ref[...] = nv
        d_ref[...] = -ADAM_LR * ((nm * c1) / (jnp.sqrt(nv * c2) + ADAM_EPS) + ADAM_WD * w_ref[...])

    spec = pl.BlockSpec((None, tr, C), lambda i, me_ref: (0, i, 0))
    shp = jax.ShapeDtypeStruct(w.shape, F32)
    return pl.pallas_call(
        body, out_shape=(shp, shp, shp, shp),
        grid_spec=pltpu.PrefetchScalarGridSpec(
            num_scalar_prefetch=1, grid=(R // tr,),
            in_specs=[spec, spec, spec, pl.BlockSpec((N_DEV, tr, C), lambda i, me_ref: (0, i, 0)),
                      pl.BlockSpec((None, tr, C), lambda i, me_ref: (me_ref[0], i, 0))],
            out_specs=(spec, spec, spec, spec)),
        name=name, compiler_params=_cp(("parallel",), VMEM_LIMIT),
    )(jnp.reshape(me, (1,)).astype(jnp.int32), w, m, v, landed, sent)


def _adamw_packed(w, m, v, parts, sizes, name):
    R = w.shape[0]
    assert R == sum(sizes)
    c1 = 1.0 / (1.0 - ADAM_B1 ** ADAM_STEP)
    c2 = 1.0 / (1.0 - ADAM_B2 ** ADAM_STEP)
    n = len(sizes)

    def body(w_ref, m_ref, v_ref, p_ref, *out_refs):
        g = p_ref[0]
        for j in range(1, N_DEV):
            g = g + p_ref[j]
        nm = ADAM_B1 * m_ref[...] + (1.0 - ADAM_B1) * g
        nv = ADAM_B2 * v_ref[...] + (1.0 - ADAM_B2) * (g * g)
        d = -ADAM_LR * ((nm * c1) / (jnp.sqrt(nv * c2) + ADAM_EPS) + ADAM_WD * w_ref[...])
        for kind, val in enumerate((g, d, nm, nv)):
            off = 0
            for k, rows in enumerate(sizes):
                out_refs[kind * n + k][...] = val[off:off + rows, :]
                off += rows

    whole = pl.BlockSpec((R, LANES), lambda i: (0, 0))
    shapes = [jax.ShapeDtypeStruct((rows, LANES), F32) for rows in sizes] * 4
    res = pl.pallas_call(
        body, out_shape=tuple(shapes), grid=(1,),
        in_specs=[whole, whole, whole, pl.BlockSpec((N_DEV, R, LANES), lambda i: (0, 0, 0))],
        out_specs=tuple(pl.BlockSpec((rows, LANES), lambda i: (0, 0)) for rows in sizes) * 4,
        name=name, compiler_params=_cp(("arbitrary",), VMEM_LIMIT),
    )(w, m, v, parts)
    return [list(res[kind * n:(kind + 1) * n]) for kind in range(4)]


def _place():
    x, y, c = lax.axis_index("x"), lax.axis_index("y"), lax.axis_index("c")
    return x, y, c


def _all_gather(blocks, name):
    n = len(blocks)

    def body(*refs):
        ins, outs = refs[:n], refs[n:2 * n]
        send_sems, recv_sems, local_sems = refs[2 * n:]
        x, y, c = _place()
        me, sibling = (x, y, c), (x, y, 1 - c)
        chips = [(1 - x, y), (x, 1 - y), (1 - x, 1 - y)]
        sends = []
        for a in range(n):
            out = outs[a]

            def slot(px, py, pc, out=out):
                return out.at[4 * px + 2 * py + pc]

            def copy(k, block, to, src=None, a=a, slot=slot):
                return pltpu.make_async_remote_copy(
                    src_ref=slot(*block) if src is None else src, dst_ref=slot(*block),
                    send_sem=send_sems.at[a, k], recv_sem=recv_sems.at[a, k], device_id=to, device_id_type=MESH)

            mine = pltpu.make_async_copy(ins[a], slot(*me), local_sems.at[a])
            mine.start()
            first = [copy(0, me, sibling, src=ins[a])]
            first += [copy(1 + j, me, (*chip, c), src=ins[a]) for j, chip in enumerate(chips)]
            for cp in first:
                cp.start()
            sends.append((mine, first, copy))
        for a in range(n):
            mine, first, copy = sends[a]
            passed = [copy(4 + j, (*chip, c), sibling) for j, chip in enumerate(chips)]
            for j, chip in enumerate(chips):
                copy(1 + j, (*chip, c), me).wait_recv()
                passed[j].start()
            copy(0, sibling, me).wait_recv()
            for j, chip in enumerate(chips):
                copy(4 + j, (*chip, 1 - c), me).wait_recv()
            for cp in first + passed:
                cp.wait_send()
            mine.wait()

    any_spec = pl.BlockSpec(memory_space=pl.ANY)
    return pl.pallas_call(
        body, out_shape=tuple(jax.ShapeDtypeStruct((N_DEV,) + b.shape, b.dtype) for b in blocks),
        in_specs=[any_spec] * n, out_specs=tuple([any_spec] * n),
        scratch_shapes=[pltpu.SemaphoreType.DMA((n, 7)), pltpu.SemaphoreType.DMA((n, 7)), pltpu.SemaphoreType.DMA((n,))],
        name=name,
    )(*blocks)


_HBM = pl.BlockSpec(memory_space=pltpu.HBM)
_SEM = pl.BlockSpec(memory_space=pltpu.SEMAPHORE)
_EFFECT = pltpu.SideEffectType.DATAFLOW_SIDE_EFFECTING


def _peers(x, y, c):
    out = []
    for k in range(1, N_DEV):
        px, py, pc = x ^ ((k >> 2) & 1), y ^ ((k >> 1) & 1), c ^ (k & 1)
        out.append((k, (px, py, pc), 4 * px + 2 * py + pc))
    return out


def _xchg_copies(src_refs, land_refs, send_sems, recv_sems, scatter):
    x, y, c = _place()
    me = 4 * x + 2 * y + c
    copies = []
    for a, (src, land) in enumerate(zip(src_refs, land_refs)):
        for k, place, idx in _peers(x, y, c):
            j = a * (N_DEV - 1) + k - 1
            copies.append(pltpu.make_async_remote_copy(
                src_ref=src.at[idx] if scatter[a] else src, dst_ref=land.at[me],
                send_sem=send_sems[j], recv_sem=recv_sems[j], device_id=place, device_id_type=MESH))
    return copies


def _xchg_start(srcs, scatter, name):
    n = len(srcs)
    lands = [lax.empty((N_DEV,) + (s.shape[1:] if sc else s.shape), s.dtype) for s, sc in zip(srcs, scatter)]

    ns = n * (N_DEV - 1)

    def body(*refs):
        sems = refs[2 * n:2 * n + 2 * ns]
        for cp in _xchg_copies(refs[:n], refs[n:2 * n], sems[:ns], sems[ns:], scatter):
            cp.start()
        token = refs[-1]
        token[...] = jnp.zeros_like(token)

    both = list(srcs) + lands
    res = pl.pallas_call(
        body, name=name,
        out_shape=(*[pltpu.SemaphoreType.DMA(())] * (2 * ns),
                   *[pltpu.HBM(a.shape, a.dtype) for a in both], jax.ShapeDtypeStruct((8, LANES), F32)),
        in_specs=[_HBM] * (2 * n),
        out_specs=(*([_SEM] * (2 * ns)), *([_HBM] * (2 * n)), pl.BlockSpec(memory_space=pltpu.VMEM)),
        input_output_aliases={i: 2 * ns + i for i in range(2 * n)},
        compiler_params=pltpu.CompilerParams(has_side_effects=_EFFECT),
    )(*[pltpu.with_memory_space_constraint(a, pltpu.HBM) for a in both])
    return (tuple(res[:2 * ns]), tuple(res[2 * ns:2 * ns + 2 * n])), res[-1]


def _xchg_wait(handle, scatter, after, name):
    sems, thru = handle
    n = len(thru) // 2
    ns = len(sems) // 2

    def body(*refs):
        got = refs[2 * n:2 * n + 2 * ns]
        for cp in _xchg_copies(refs[:n], refs[n:2 * n], got[:ns], got[ns:], scatter):
            cp.wait_send()
            cp.wait_recv()

    outs = pl.pallas_call(
        body, name=name, out_shape=tuple(pltpu.HBM(a.shape, a.dtype) for a in thru),
        in_specs=[_HBM] * (2 * n) + [_SEM] * (2 * ns) + [pl.BlockSpec(memory_space=pl.ANY)],
        out_specs=tuple([_HBM] * (2 * n)), input_output_aliases={i: i for i in range(2 * n)},
        compiler_params=pltpu.CompilerParams(has_side_effects=_EFFECT),
    )(*thru, *sems, after)
    return outs[:n], outs[n:]


def _tie(a, token):
    return a if token is None else a + token[0, 0].astype(a.dtype)


def _rows128(a):
    flat = a.reshape(-1)
    rows = -(-flat.shape[0] // LANES)
    rows = -(-rows // 8) * 8
    return jnp.pad(flat, (0, rows * LANES - flat.shape[0])).reshape(rows, LANES)


def _local_step(x, target, norm_mix_g, w_in_t, b_forget, gmlp_norm_g, w_spatial, b_spatial, norm_ffn_g, conv_b,
                norm_final_g, rest_fn, send_fn, small_fn, token=None):
    f = D_FF
    g_mix = norm_mix_g.reshape(1, D_MODEL)
    w_pad = jnp.pad(w_in_t, ((0, IN_PAD - IN_COLS), (0, 0)))
    bf_pad = jnp.pad(b_forget.reshape(1, N_HEADS), ((0, 0), (0, LANES - N_HEADS)))
    xn, qa, ka, va, ug, zf = _inproj_fwd(x, _tie(g_mix, token), w_pad, bf_pad)
    bias_full = jnp.repeat(b_spatial.reshape(N_GROUPS, CHUNK).T, GROUP_DIM, axis=1)
    w_s = w_spatial.reshape(N_GROUPS, CHUNK, CHUNK)
    gain = gmlp_norm_g.reshape(1, GMLP_WIDTH)
    sg = _gmlp_fwd(ug, gain, w_s, bias_full)
    att, lse, att_bf = _attn_fwd(qa, ka, va)
    w_out_bf, w_up_bf, conv_w, w_down_bf = rest_fn(att_bf)
    g_ffn = norm_ffn_g.reshape(1, D_MODEL)
    h1, hn = _out_proj_fwd(att_bf, sg, w_out_bf, x, g_ffn)
    cw = jnp.pad(conv_w.reshape(3, 2, f).transpose(1, 0, 2), ((0, 0), (0, 5), (0, 0)))
    cb = conv_b.reshape(2, 1, f)
    hu, hc, act = _ffn_up_conv(hn, w_up_bf, cw, cb)
    loss_blk, dh2, dh2_bf, dg_final = _ffn_down_loss(act, w_down_bf, h1, norm_final_g.reshape(1, D_MODEL), target)
    dw_down = _mm(act, dh2_bf, mode="tn", out_dtype=F32, tm=1408, tn=1024, tk=2048, name="ffn_down_dw")
    dact = _mm(dh2_bf, w_down_bf, mode="nt", out_dtype=F32, tm=1024, tn=1408, tk=1024, outer="j", name="ffn_down_dx")
    dhu, dcw = _conv_gate_bwd(hc, hu, dact, _tie(cw, send_fn("w_down", dw_down)))
    dw_up = _mm(hn, dhu, mode="tn", out_dtype=F32, tm=1024, tn=1408, tk=2048, b_halves=True, outer="j", name="ffn_up_dw")
    dh1, dh1_bf, dg_ffn = _ffn_up_dx_rms(dhu, w_up_bf, h1, _tie(g_ffn, send_fn("w_up", dw_up)), dh2)
    dsg, qb, doa = _out_proj_dx_prep(dh1_bf, w_out_bf, att, lse, qa)
    dw_out = _out_proj_dw(att_bf, sg, dh1_bf)
    dq, dk, dv, dcq, dck = _attn_bwd(qb, ka, va, doa)
    wt_s = w_s.transpose(0, 2, 1)
    dug, dw_s, dgain, dbias = _gmlp_bwd(ug, dsg, _tie(gain, send_fn("w_out", dw_out)), w_s, wt_s, bias_full)
    dzf, dbf = _gate_bwd(dcq, dck, zf)
    grad_x, dg_mix = _inproj_bwd_dx((dq, dk, dv, dug, dzf), w_pad, x, g_mix, dh1)
    grads = dict(
        norm_mix_g=dg_mix[0:1, :],
        b_forget=dbf[0:1, :N_HEADS],
        gmlp_norm_g=dgain[0:1, :],
        w_spatial=dw_s,
        b_spatial=dbias[:, ::GROUP_DIM].T,
        norm_ffn_g=dg_ffn[0:1, :],
        conv_w=dcw[:, 0:3, :].transpose(1, 0, 2).reshape(3, 2 * f),
        conv_b=dcw[:, 3, :].reshape(1, 2 * f),
        norm_final_g=dg_final[0, :],
    )
    token = small_fn(loss_blk[0, 0], grads)
    dw_in = _inproj_bwd_dw(xn, (dq, dk, dv, dug, _tie(dzf, token)))
    return grad_x, send_fn("w_in", dw_in[:, :IN_COLS])


SMALL = ("norm_mix_g", "b_forget", "gmlp_norm_g", "w_spatial", "b_spatial", "norm_ffn_g", "conv_b", "norm_final_g")


def kernel(x, norm_mix_g, w_in, b_forget, gmlp_norm_g, w_spatial, b_spatial, w_out, norm_ffn_g, w_up, conv_w, conv_b, w_down, norm_final_g, loss_target, m_norm_mix_g, m_w_in, m_b_forget, m_gmlp_norm_g, m_w_spatial, m_b_spatial, m_w_out, m_norm_ffn_g, m_w_up, m_conv_w, m_conv_b, m_w_down, m_norm_final_g, v_norm_mix_g, v_w_in, v_b_forget, v_gmlp_norm_g, v_w_spatial, v_b_spatial, v_w_out, v_norm_ffn_g, v_w_up, v_conv_w, v_conv_b, v_w_down, v_norm_final_g):
    weights = dict(norm_mix_g=norm_mix_g, w_in=w_in, b_forget=b_forget, gmlp_norm_g=gmlp_norm_g, w_spatial=w_spatial,
                   b_spatial=b_spatial, w_out=w_out, norm_ffn_g=norm_ffn_g, w_up=w_up, conv_w=conv_w, conv_b=conv_b,
                   w_down=w_down, norm_final_g=norm_final_g)
    m_in = dict(norm_mix_g=m_norm_mix_g, w_in=m_w_in, b_forget=m_b_forget, gmlp_norm_g=m_gmlp_norm_g,
                w_spatial=m_w_spatial, b_spatial=m_b_spatial, w_out=m_w_out, norm_ffn_g=m_norm_ffn_g, w_up=m_w_up,
                conv_w=m_conv_w, conv_b=m_conv_b, w_down=m_w_down, norm_final_g=m_norm_final_g)
    v_in = dict(norm_mix_g=v_norm_mix_g, w_in=v_w_in, b_forget=v_b_forget, gmlp_norm_g=v_gmlp_norm_g,
                w_spatial=v_w_spatial, b_spatial=v_b_spatial, w_out=v_w_out, norm_ffn_g=v_norm_ffn_g, w_up=v_w_up,
                conv_w=v_conv_w, conv_b=v_conv_b, w_down=v_w_down, norm_final_g=v_norm_final_g)
    order = list(weights)
    me = 4 * lax.axis_index("x") + 2 * lax.axis_index("y") + lax.axis_index("c")
    n_in, n_up = w_in.shape[2], w_up.shape[2]
    r_out, r_down = w_out.shape[1], w_down.shape[1]

    def with_mine(landed, mine):
        return lax.dynamic_update_index_in_dim(landed, mine, me, 0)

    up_blk = w_up[0].T.astype(BF16)
    out_blk = w_out[0].astype(BF16)
    down_blk = w_down[0].astype(BF16)
    taps_blk = jnp.pad(conv_w[0], ((0, 5), (0, 0)))
    (in_all,) = _all_gather([w_in[0].T.astype(BF16)], "gather_w_in")
    in_all, rest_blocks = lax.optimization_barrier((in_all, [up_blk, out_blk, down_blk, taps_blk]))
    rest_handle, token = _xchg_start(rest_blocks, [False] * 4, "gather_rest_start")
    w_in_t = in_all.reshape(N_DEV * n_in, D_MODEL)

    def rest_fn(after):
        mine, landed = _xchg_wait(rest_handle, [False] * 4, after, "gather_rest_wait")
        up_all, out_all, down_all, taps_all = [with_mine(l, b) for l, b in zip(landed, mine)]
        return (out_all.reshape(N_DEV * r_out, D_MODEL), up_all.reshape(N_DEV * n_up, D_MODEL),
                taps_all[:, :3, :].transpose(1, 0, 2).reshape(3, N_DEV * n_up),
                down_all.reshape(N_DEV * r_down, D_MODEL))

    sent = {}

    def send_fn(name, grad):
        if name == "w_in":
            parts = grad.reshape(D_MODEL, N_DEV, -1).transpose(1, 0, 2).astype(BF16)
        elif name == "w_up":
            parts = grad.reshape(D_MODEL, N_DEV, -1).transpose(1, 0, 2)
        else:
            parts = grad.reshape(N_DEV, -1, D_MODEL)
        sent[name], tok = _xchg_start([parts], [True], "scatter_" + name + "_start")
        return tok

    small = {}

    def small_fn(loss_local, g):
        loss_rows = jnp.pad(loss_local.reshape(1, 1), ((0, 31), (0, LANES - 1)))
        packed = [_rows128(g[k]) for k in SMALL] + [loss_rows, _rows128(g["conv_w"])]
        small["sizes"] = [p.shape[0] for p in packed]
        small["handle"], tok = _xchg_start([jnp.concatenate(packed, axis=0)], [False], "gather_small_start")
        return tok

    grad_x, after = _local_step(
        x[0], loss_target[0], norm_mix_g, w_in_t, b_forget, gmlp_norm_g, w_spatial, b_spatial, norm_ffn_g, conv_b,
        norm_final_g, rest_fn, send_fn, small_fn, token)

    outs = {}

    def update_big(name, after):
        (parts,), (landed,) = _xchg_wait(sent[name], [True], after, "scatter_" + name + "_wait")
        outs[name] = tuple(_adamw_owner(weights[name], m_in[name], v_in[name], landed, parts, me, "adamw_" + name))
        return outs[name][0]

    for name in ("w_down", "w_up", "w_out"):
        after = update_big(name, after)

    (mine,), (landed,) = _xchg_wait(small["handle"], [False], after, "gather_small_wait")
    small_all = with_mine(landed, mine)
    sizes = small["sizes"]
    n_small_rows = sum(sizes[:-2])

    def pack(src):
        return jnp.concatenate([_rows128(src[k]) for k in SMALL] + [jnp.zeros((sizes[-2], LANES), F32)], axis=0)

    n_adam_rows = n_small_rows + sizes[-2]
    per_kind = _adamw_packed(pack(weights), pack(m_in), pack(v_in), small_all, sizes[:-1], "adamw_small")
    loss = per_kind[0][-1][0, 0]
    for j, k in enumerate(SMALL):
        shp = weights[k].shape
        cnt = math.prod(shp)
        outs[k] = tuple(kind[j].reshape(-1)[:cnt].reshape(shp) for kind in per_kind)
    sg_ = per_kind[0][0]
    taps_parts = small_all[:, n_adam_rows:, :].reshape(N_DEV, -1)[:, :3 * N_DEV * n_up].reshape(N_DEV, 3, N_DEV * n_up)
    taps_mine = lax.dynamic_slice_in_dim(taps_parts, me * n_up, n_up, axis=2)
    taps_mine = jnp.pad(taps_mine, ((0, 0), (0, 5), (0, 0)))

    def pad8(a):
        return jnp.pad(a[0], ((0, 5), (0, 0)))

    res = _adamw(pad8(conv_w), pad8(m_conv_w), pad8(v_conv_w), taps_mine, "adamw_conv_w")
    outs["conv_w"] = tuple(a[:3][None] for a in res)
    update_big("w_in", sg_)

    return (loss, grad_x[None], *[outs[k][0] for k in order], *[outs[k][1] for k in order],
            *[outs[k][2] for k in order], *[outs[k][3] for k in order])
```

```python
import functools
import math

import jax
import jax.numpy as jnp
from jax import lax
from jax.experimental import pallas as pl
from jax.experimental.pallas import tpu as pltpu

F32 = jnp.float32
BF16 = jnp.bfloat16

N_DEV = 8
D_MODEL = 1024
ATT_WIDTH = 512
GMLP_WIDTH = 512
HEAD_DIM = 64
N_HEADS = 8
N_PAIRS = 4
N_GROUPS = 8
GROUP_DIM = 64
CHUNK = 128
D_FF = 2816
IN_COLS = 2568
IN_PAD = 2688
QKV = 1536
UG_END = 2560
EPS = 1e-6
LANES = 128

ADAM_LR = 0.001
ADAM_B1 = 0.9
ADAM_B2 = 0.999
ADAM_EPS = 1e-08
ADAM_WD = 0.01
ADAM_STEP = 10

ATT_TQ = 1024
ATT_TK = 1024
FFN_TM, FFN_TN = 512, 1408
CONV_TM, CONV_TN = 512, 1408
VMEM_LIMIT = 56 * 1024 * 1024
MESH = pl.DeviceIdType.MESH


def _cp(sem, vmem=None):
    return pltpu.CompilerParams(dimension_semantics=sem, vmem_limit_bytes=vmem)


def _pick(n, prefs):
    for p in prefs:
        if n % p == 0:
            return p
    return n


def _split3(x):
    hi = x.astype(BF16)
    r1 = x - hi.astype(F32)
    mid = r1.astype(BF16)
    lo = (r1 - mid.astype(F32)).astype(BF16)
    return hi, mid, lo


def _dot3(x, ones_bf):
    d = functools.partial(jnp.dot, preferred_element_type=F32)
    out = []
    for c in range(0, x.shape[1], 2 * LANES):
        blk = ones_bf[c:c + 2 * LANES, c:c + 2 * LANES]
        hi, mid, lo = _split3(x[:, c:c + 2 * LANES])
        out.append(d(hi, blk) + d(mid, blk) + d(lo, blk))
    return jnp.concatenate(out, axis=1)


def _dot3l(ones_bf, x):
    n = x.shape[1]
    y = jnp.dot(ones_bf, jnp.concatenate(_split3(x), axis=1), preferred_element_type=F32)
    return y[:, :n] + y[:, n:2 * n] + y[:, 2 * n:]


def _gelu(x):
    k = math.sqrt(2.0 / math.pi)
    t = jnp.tanh(k * (x + 0.044715 * (x * x * x)))
    return 0.5 * x * (1.0 + t)


def _gelu_grad(x):
    k = math.sqrt(2.0 / math.pi)
    x2 = x * x
    t = jnp.tanh(k * (x + 0.044715 * (x2 * x)))
    return 0.5 * (1.0 + t) + 0.5 * x * (1.0 - t * t) * (k * (1.0 + 3.0 * 0.044715 * x2))


def _sigmoid(x):
    return 1.0 / (1.0 + jnp.exp(-x))


def _mm(a, b, *, mode, out_dtype, tm, tn, tk, name, res=None, a_halves=False, b_halves=False,
        out_halves=False, outer="i"):
    if mode == "tn":
        K, M = a.shape[-2], a.shape[-1] * (2 if a_halves else 1)
    else:
        M, K = a.shape[-2], a.shape[-1] * (2 if a_halves else 1)
    if mode == "nt":
        N = b.shape[-2]
        assert b.shape[-1] == K
    else:
        N = b.shape[-1] * (2 if b_halves else 1)
    tm, tn, tk = min(tm, M), min(tn, N), min(tk, K)
    assert M % tm == 0 and N % tn == 0 and K % tk == 0, (name, M, N, K, tm, tn, tk)
    nm, nn, nk = M // tm, N // tn, K // tk

    def ij(g0, g1):
        return (g0, g1) if outer == "i" else (g1, g0)

    if mode == "nn":
        dims = (((1,), (0,)), ((), ()))
        if a_halves:
            nkh = nk // 2
            a_spec = pl.BlockSpec((None, tm, tk), lambda g0, g1, k: (k // nkh, ij(g0, g1)[0], k % nkh))
        else:
            a_spec = pl.BlockSpec((tm, tk), lambda g0, g1, k: (ij(g0, g1)[0], k))
        b_spec = pl.BlockSpec((tk, tn), lambda g0, g1, k: (k, ij(g0, g1)[1]))
    elif mode == "nt":
        dims = (((1,), (1,)), ((), ()))
        if a_halves:
            nkh = nk // 2
            a_spec = pl.BlockSpec((None, tm, tk), lambda g0, g1, k: (k // nkh, ij(g0, g1)[0], k % nkh))
        else:
            a_spec = pl.BlockSpec((tm, tk), lambda g0, g1, k: (ij(g0, g1)[0], k))
        b_spec = pl.BlockSpec((tn, tk), lambda g0, g1, k: (ij(g0, g1)[1], k))
    else:
        dims = (((0,), (0,)), ((), ()))
        if a_halves:
            nmh = nm // 2
            a_spec = pl.BlockSpec((None, tk, tm), lambda g0, g1, k: (ij(g0, g1)[0] // nmh, k, ij(g0, g1)[0] % nmh))
        else:
            a_spec = pl.BlockSpec((tk, tm), lambda g0, g1, k: (k, ij(g0, g1)[0]))
        if b_halves:
            nnh = nn // 2
            b_spec = pl.BlockSpec((None, tk, tn), lambda g0, g1, k: (ij(g0, g1)[1] // nnh, k, ij(g0, g1)[1] % nnh))
        else:
            b_spec = pl.BlockSpec((tk, tn), lambda g0, g1, k: (k, ij(g0, g1)[1]))
    if out_halves:
        nnh = nn // 2
        o_spec = pl.BlockSpec((None, tm, tn), lambda g0, g1, k: (ij(g0, g1)[1] // nnh, ij(g0, g1)[0], ij(g0, g1)[1] % nnh))
        o_shape = jax.ShapeDtypeStruct((2, M, N // 2), out_dtype)
    else:
        o_spec = pl.BlockSpec((tm, tn), lambda g0, g1, k: ij(g0, g1))
        o_shape = jax.ShapeDtypeStruct((M, N), out_dtype)
    in_specs = [a_spec, b_spec]
    args = [a, b]
    if res is not None:
        in_specs.append(pl.BlockSpec((tm, tn), lambda g0, g1, k: ij(g0, g1)))
        args.append(res)

    def body(*refs):
        if res is not None:
            a_ref, b_ref, r_ref, o_ref = refs[:4]
        else:
            a_ref, b_ref, o_ref = refs[:3]
            r_ref = None
        part = lax.dot_general(a_ref[...], b_ref[...], dims, preferred_element_type=F32)
        if nk == 1:
            if r_ref is not None:
                part = part + r_ref[...]
            o_ref[...] = part.astype(out_dtype)
            return
        acc_ref = refs[-1]
        k = pl.program_id(2)

        @pl.when(k == 0)
        def _():
            acc_ref[...] = part

        @pl.when(k > 0)
        def _():
            acc_ref[...] += part

        @pl.when(k == nk - 1)
        def _():
            tot = acc_ref[...]
            if r_ref is not None:
                tot = tot + r_ref[...]
            o_ref[...] = tot.astype(out_dtype)

    grid = (nm, nn, nk) if outer == "i" else (nn, nm, nk)
    scratch = [] if nk == 1 else [pltpu.VMEM((tm, tn), F32)]
    return pl.pallas_call(
        body, out_shape=o_shape, grid=grid, in_specs=in_specs, out_specs=o_spec, scratch_shapes=scratch,
        name=name, compiler_params=_cp(("parallel", "parallel", "arbitrary"), VMEM_LIMIT),
    )(*args)


def _aug(lane, terms):
    out = 0.0
    for j, t in enumerate(terms):
        out = jnp.where(lane == HEAD_DIM + j, t, out)
    return out


def _split3f(x):
    hi, mid, lo = _split3(x)
    return [hi.astype(F32), mid.astype(F32), lo.astype(F32)]


def _inproj_fwd(x, g_mix, w_pad, bf_pad):
    S = x.shape[0]
    tm = _pick(S, (512, 256))
    tri = (lax.broadcasted_iota(jnp.int32, (tm, tm), 0) >= lax.broadcasted_iota(jnp.int32, (tm, tm), 1)).astype(BF16)

    def body(x_ref, g_ref, w_ref, bf_ref, tri_ref, put_ref, one_ref, xn_ref, qa_ref, ka_ref, va_ref, ug_ref, zf_ref,
             carry_ref):
        i = pl.program_id(0)

        @pl.when(i == 0)
        def _():
            carry_ref[...] = jnp.zeros_like(carry_ref)

        xf = x_ref[...]
        r = lax.rsqrt(jnp.mean(xf * xf, axis=-1, keepdims=True) + EPS)
        xn = ((xf * r) * g_ref[...]).astype(BF16)
        xn_ref[...] = xn
        proj = lax.dot_general(xn, w_ref[...], _NT, preferred_element_type=F32)
        ug_ref[...] = proj[:, QKV:UG_END]
        zf = proj[:, UG_END:] + bf_ref[...]
        zf_ref[...] = zf
        lf = jnp.minimum(zf, 0.0) - jnp.log(1.0 + jnp.exp(-jnp.abs(zf)))
        c = _dot3l(tri_ref[...], lf) + carry_ref[0:1, :]
        carry_ref[0:1, :] = c[tm - 1:tm, :]
        c3 = jnp.concatenate(_split3(c), axis=1)
        aug_q = jnp.dot(c3, put_ref[0], preferred_element_type=F32) + one_ref[0:1, :]
        aug_k = jnp.dot(c3, put_ref[1], preferred_element_type=F32) + one_ref[1:2, :]
        lane = lax.broadcasted_iota(jnp.int32, (tm, LANES), 1)
        for h in range(N_HEADS):
            p, odd = h // 2, h % 2

            def head(base, scale=None, p=p, odd=odd):
                blk = proj[:, base + p * LANES:base + (p + 1) * LANES]
                if scale is not None:
                    blk = blk * scale
                return pltpu.roll(blk, HEAD_DIM, 1) if odd else blk

            cols = slice(h * LANES, (h + 1) * LANES)
            qa_ref[:, cols] = jnp.where(lane < HEAD_DIM, head(0, HEAD_DIM ** -0.5), aug_q[:, cols]).astype(BF16)
            ka_ref[:, cols] = jnp.where(lane < HEAD_DIM, head(ATT_WIDTH), aug_k[:, cols]).astype(BF16)
            va_ref[:, cols] = jnp.where(lane < HEAD_DIM, head(2 * ATT_WIDTH), one_ref[2:3, cols]).astype(BF16)

    wide = N_HEADS * LANES
    src = lax.broadcasted_iota(jnp.int32, (3 * LANES, wide), 0)
    col = lax.broadcasted_iota(jnp.int32, (3 * LANES, wide), 1)
    hd, term = src % LANES, src // LANES
    to_q = (col == hd * LANES + HEAD_DIM + term) & (hd < N_HEADS)
    to_k = (col == hd * LANES + HEAD_DIM + 3 + term) & (hd < N_HEADS)
    put = jnp.stack([to_q.astype(BF16), -to_k.astype(BF16)])
    off = lax.broadcasted_iota(jnp.int32, (8, wide), 1) % LANES - HEAD_DIM
    row = lax.broadcasted_iota(jnp.int32, (8, wide), 0)
    q_one = (off >= 3) & (off < 6)
    k_one = ((off >= 0) & (off < 3)) | ((off >= 6) & (off < 9))
    v_one = (off >= 0) & (off < 3)
    ones = jnp.where(row == 0, q_one, jnp.where(row == 1, k_one, (row == 2) & v_one)).astype(F32)
    return pl.pallas_call(
        body,
        out_shape=(jax.ShapeDtypeStruct((S, D_MODEL), BF16), jax.ShapeDtypeStruct((S, wide), BF16),
                   jax.ShapeDtypeStruct((S, wide), BF16), jax.ShapeDtypeStruct((S, wide), BF16),
                   jax.ShapeDtypeStruct((S, 2 * GMLP_WIDTH), F32), jax.ShapeDtypeStruct((S, LANES), F32)),
        grid=(S // tm,),
        in_specs=[pl.BlockSpec((tm, D_MODEL), lambda i: (i, 0)), pl.BlockSpec((1, D_MODEL), lambda i: (0, 0)),
                  pl.BlockSpec((IN_PAD, D_MODEL), lambda i: (0, 0)), pl.BlockSpec((1, LANES), lambda i: (0, 0)),
                  pl.BlockSpec((tm, tm), lambda i: (0, 0)), pl.BlockSpec((2, 3 * LANES, wide), lambda i: (0, 0, 0)),
                  pl.BlockSpec((8, wide), lambda i: (0, 0))],
        out_specs=(pl.BlockSpec((tm, D_MODEL), lambda i: (i, 0)), pl.BlockSpec((tm, wide), lambda i: (i, 0)),
                   pl.BlockSpec((tm, wide), lambda i: (i, 0)), pl.BlockSpec((tm, wide), lambda i: (i, 0)),
                   pl.BlockSpec((tm, 2 * GMLP_WIDTH), lambda i: (i, 0)), pl.BlockSpec((tm, LANES), lambda i: (i, 0))),
        scratch_shapes=[pltpu.VMEM((8, LANES), F32)],
        name="inproj_fwd", compiler_params=_cp(("arbitrary",), VMEM_LIMIT),
    )(x, g_mix, w_pad, bf_pad, tri, put, ones)


def _group_ones():
    r = lax.broadcasted_iota(jnp.int32, (GMLP_WIDTH, GMLP_WIDTH), 0) // GROUP_DIM
    c = lax.broadcasted_iota(jnp.int32, (GMLP_WIDTH, GMLP_WIDTH), 1) // GROUP_DIM
    return (r == c).astype(BF16)


def _gmlp_mixed(vn_bf, w_ref, bias, n_chunks):
    lane = lax.broadcasted_iota(jnp.int32, (CHUNK, LANES), 1)
    row = lax.broadcasted_iota(jnp.int32, (CHUNK, CHUNK), 0)
    col = lax.broadcasted_iota(jnp.int32, (CHUNK, CHUNK), 1)
    ws = [jnp.where(row >= col, w_ref[g], 0.0).astype(BF16) for g in range(N_GROUPS)]
    rows = []
    for ci in range(n_chunks):
        cols = []
        for pp in range(N_GROUPS // 2):
            v = vn_bf[ci * CHUNK:(ci + 1) * CHUNK, pp * LANES:(pp + 1) * LANES]
            v_lo = jnp.where(lane < GROUP_DIM, v, jnp.zeros_like(v))
            v_hi = jnp.where(lane >= GROUP_DIM, v, jnp.zeros_like(v))
            m = (jnp.dot(ws[2 * pp], v_lo, preferred_element_type=F32)
                 + jnp.dot(ws[2 * pp + 1], v_hi, preferred_element_type=F32))
            cols.append(m + bias[:, pp * LANES:(pp + 1) * LANES])
        rows.append(jnp.concatenate(cols, axis=1))
    return jnp.concatenate(rows, axis=0)


def _gmlp_fwd(ug, gain, w_s, bias_full):
    S = ug.shape[0]
    tm = _pick(S, (512, 256, 128))
    ones = _group_ones()

    def body(ug_ref, gain_ref, w_ref, bias_ref, ones_ref, sg_ref):
        u = _gelu(ug_ref[:, :GMLP_WIDTH])
        vr = _gelu(ug_ref[:, GMLP_WIDTH:])
        ms = _dot3(vr * vr, ones_ref[...]) * (1.0 / GROUP_DIM)
        vn = ((vr * lax.rsqrt(ms + EPS)) * gain_ref[...]).astype(BF16)
        mixed = _gmlp_mixed(vn, w_ref, bias_ref[...], tm // CHUNK)
        sg_ref[...] = (u * mixed).astype(BF16)

    return pl.pallas_call(
        body, out_shape=jax.ShapeDtypeStruct((S, GMLP_WIDTH), BF16), grid=(S // tm,),
        in_specs=[pl.BlockSpec((tm, 2 * GMLP_WIDTH), lambda i: (i, 0)), pl.BlockSpec((1, GMLP_WIDTH), lambda i: (0, 0)),
                  pl.BlockSpec((N_GROUPS, CHUNK, CHUNK), lambda i: (0, 0, 0)),
                  pl.BlockSpec((CHUNK, GMLP_WIDTH), lambda i: (0, 0)),
                  pl.BlockSpec((GMLP_WIDTH, GMLP_WIDTH), lambda i: (0, 0))],
        out_specs=pl.BlockSpec((tm, GMLP_WIDTH), lambda i: (i, 0)),
        name="gmlp_fwd", compiler_params=_cp(("parallel",), VMEM_LIMIT),
    )(ug, gain, w_s, bias_full, ones)


_NT = (((1,), (1,)), ((), ()))
_TN = (((0,), (0,)), ((), ()))


def _attn_fwd(qa, ka, va):
    S = qa.shape[0]
    tq = _pick(S, (ATT_TQ, 256))
    tk = min(ATT_TK, tq)
    nq = S // tq
    assert tq == tk, "the diagonal block is handled as one tq x tq tile"
    per_q = 1

    def body(q_ref, k_ref, v_ref, o_ref, lse_ref, ob_ref):
        qi = pl.program_id(1)
        lane = lax.broadcasted_iota(jnp.int32, (tq, LANES), 1)
        qs = [q_ref[:, :LANES], q_ref[:, LANES:]]

        def update(q, ks, k_len, h, m, acc, first_row):
            cols = slice(h * LANES, (h + 1) * LANES)
            s = lax.dot_general(q, k_ref[pl.ds(ks, k_len), cols], _NT, preferred_element_type=F32)
            if first_row is not None:
                rid = lax.broadcasted_iota(jnp.int32, s.shape, 0) + first_row
                s = jnp.where(rid >= lax.broadcasted_iota(jnp.int32, s.shape, 1), s, -jnp.inf)
            m_new = jnp.maximum(m, jnp.max(s, axis=-1, keepdims=True))
            p = jnp.exp(s - m_new).astype(BF16)
            acc = jnp.exp(m - m_new) * acc + jnp.dot(p, v_ref[pl.ds(ks, k_len), cols], preferred_element_type=F32)
            return m_new, acc

        def step(kb, carry):
            ks = pl.multiple_of(kb * tk, tk)
            return tuple(update(qs[h], ks, tk, h, *carry[h], None) for h in range(2))

        one = (jnp.full((tq, 1), -jnp.inf, F32), jnp.zeros((tq, LANES), F32))
        carry = lax.fori_loop(0, qi * per_q, step, (one, one))
        outs, lses = [], []
        strip = tq // 2
        diag = pl.multiple_of(qi * tq, tq)
        for h in range(2):
            ms, accs = [], []
            for r in range(2):
                rows = slice(r * strip, (r + 1) * strip)
                m, acc = update(qs[h][rows], diag, (r + 1) * strip, h, carry[h][0][rows], carry[h][1][rows], r * strip)
                ms.append(m)
                accs.append(acc)
            m, acc = jnp.concatenate(ms, axis=0), jnp.concatenate(accs, axis=0)
            l = acc[:, HEAD_DIM:HEAD_DIM + 1]
            outs.append(acc / l)
            lses.append(m + jnp.log(l))
        o = jnp.where(lane < HEAD_DIM, outs[0], pltpu.roll(outs[1], HEAD_DIM, 1))
        o_ref[...] = o
        ob_ref[...] = o.astype(BF16)
        lse_ref[...] = jnp.where(lane < HEAD_DIM, lses[0], lses[1])

    return pl.pallas_call(
        body,
        out_shape=(jax.ShapeDtypeStruct((S, ATT_WIDTH), F32), jax.ShapeDtypeStruct((S, ATT_WIDTH), F32),
                   jax.ShapeDtypeStruct((S, ATT_WIDTH), BF16)),
        grid=(N_PAIRS, nq),
        in_specs=[pl.BlockSpec((tq, 2 * LANES), lambda p, i: (i, p)),
                  pl.BlockSpec((S, 2 * LANES), lambda p, i: (0, p)),
                  pl.BlockSpec((S, 2 * LANES), lambda p, i: (0, p))],
        out_specs=(pl.BlockSpec((tq, LANES), lambda p, i: (i, p)), pl.BlockSpec((tq, LANES), lambda p, i: (i, p)),
                   pl.BlockSpec((tq, LANES), lambda p, i: (i, p))),
        name="attn_fwd", compiler_params=_cp(("parallel", "parallel"), VMEM_LIMIT),
    )(qa, ka, va)


def _shift_rows(x, prev, n):
    rid = lax.broadcasted_iota(jnp.int32, x.shape, 0)
    y = pltpu.roll(x, n, 0)
    if n == 1:
        return jnp.where(rid == 0, prev[7:8, :], y)
    return jnp.where(rid == 0, prev[6:7, :], jnp.where(rid == 1, prev[7:8, :], y))


def _shift_rows_up(x, nxt, n):
    rows = x.shape[0]
    rid = lax.broadcasted_iota(jnp.int32, x.shape, 0)
    y = pltpu.roll(x, rows - n, 0)
    if n == 1:
        return jnp.where(rid == rows - 1, nxt[0:1, :], y)
    return jnp.where(rid == rows - 2, nxt[0:1, :], jnp.where(rid == rows - 1, nxt[1:2, :], y))


def _conv3(cur, prev, w, b):
    return (w[0:1, :] * _shift_rows(cur, prev, 2) + w[1:2, :] * _shift_rows(cur, prev, 1)
            + w[2:3, :] * cur + b)


def _ffn_up_conv(hn, w_up_bf, cw, cb):
    S = hn.shape[0]
    F = D_FF
    tm = _pick(S, (FFN_TM, 256))
    tn = _pick(F, (FFN_TN, 256, 128))
    nj = F // tn

    def body(hn_ref, wa_ref, wg_ref, cw_ref, cb_ref, hu_ref, hc_ref, act_ref, tail_ref):
        i = pl.program_id(1)

        @pl.when(i == 0)
        def _():
            tail_ref[...] = jnp.zeros_like(tail_ref)

        hn_v = hn_ref[...]
        halves = []
        for h, w_ref in enumerate((wa_ref, wg_ref)):
            hu = lax.dot_general(hn_v, w_ref[...], _NT, preferred_element_type=F32)
            hu_ref[h] = hu.astype(BF16)
            hc = _conv3(hu, tail_ref[h], cw_ref[h], cb_ref[h])
            hc_ref[h] = hc
            halves.append(hc)
            tail_ref[h] = hu[tm - 8:, :]
        a, g = halves
        act_ref[...] = (g * _sigmoid(g) * a).astype(BF16)

    both = pl.BlockSpec((2, tm, tn), lambda j, i: (0, i, j))
    return pl.pallas_call(
        body, out_shape=(jax.ShapeDtypeStruct((2, S, F), BF16), jax.ShapeDtypeStruct((2, S, F), F32),
                         jax.ShapeDtypeStruct((S, F), BF16)),
        grid=(nj, S // tm),
        in_specs=[pl.BlockSpec((tm, D_MODEL), lambda j, i: (i, 0)),
                  pl.BlockSpec((tn, D_MODEL), lambda j, i: (j, 0)),
                  pl.BlockSpec((tn, D_MODEL), lambda j, i: (nj + j, 0)),
                  pl.BlockSpec((2, 8, tn), lambda j, i: (0, 0, j)),
                  pl.BlockSpec((2, 1, tn), lambda j, i: (0, 0, j))],
        out_specs=(both, both, pl.BlockSpec((tm, tn), lambda j, i: (i, j))),
        scratch_shapes=[pltpu.VMEM((2, 8, tn), F32)],
        name="ffn_up_conv", compiler_params=_cp(("parallel", "arbitrary"), VMEM_LIMIT),
    )(hn, w_up_bf, w_up_bf, cw, cb)


def _ffn_down_loss(act, w_down_bf, h1, g_final, target):
    S = h1.shape[0]
    tm = _pick(S, (512, 256))

    def body(a_ref, w_ref, h1_ref, g_ref, t_ref, loss_ref, dh_ref, dhb_ref, dg_ref):
        i = pl.program_id(0)

        @pl.when(i == 0)
        def _():
            loss_ref[...] = jnp.zeros_like(loss_ref)
            dg_ref[...] = jnp.zeros_like(dg_ref)

        hf = h1_ref[...] + jnp.dot(a_ref[...], w_ref[...], preferred_element_type=F32)
        g = g_ref[...]
        r = lax.rsqrt(jnp.mean(hf * hf, axis=-1, keepdims=True) + EPS)
        hhat = hf * r
        err = hhat * g - t_ref[...]
        loss_ref[...] += 0.5 * jnp.sum(jnp.mean(err * err, axis=-1, keepdims=True))
        dy = err * (1.0 / D_MODEL)
        dg_ref[0:1, :] += jnp.sum(dy * hhat, axis=0, keepdims=True)
        dhat = dy * g
        dh = r * (dhat - hhat * jnp.mean(dhat * hhat, axis=-1, keepdims=True))
        dh_ref[...] = dh
        dhb_ref[...] = dh.astype(BF16)

    row = pl.BlockSpec((tm, D_MODEL), lambda i: (i, 0))
    return pl.pallas_call(
        body,
        out_shape=(jax.ShapeDtypeStruct((8, LANES), F32), jax.ShapeDtypeStruct((S, D_MODEL), F32),
                   jax.ShapeDtypeStruct((S, D_MODEL), BF16), jax.ShapeDtypeStruct((8, D_MODEL), F32)),
        grid=(S // tm,),
        in_specs=[pl.BlockSpec((tm, D_FF), lambda i: (i, 0)), pl.BlockSpec((D_FF, D_MODEL), lambda i: (0, 0)), row,
                  pl.BlockSpec((1, D_MODEL), lambda i: (0, 0)), row],
        out_specs=(pl.BlockSpec((8, LANES), lambda i: (0, 0)), row, row, pl.BlockSpec((8, D_MODEL), lambda i: (0, 0))),
        name="ffn_down_loss", compiler_params=_cp(("arbitrary",), VMEM_LIMIT),
    )(act, w_down_bf, h1, g_final, target)


def _ffn_up_dx_rms(dhu, w_up_bf, h1, g_ffn, dh2):
    _, S, F = dhu.shape
    tm = _pick(S, (512, 256))

    def body(a_ref, b_ref, h_ref, g_ref, r_ref, dh_ref, dhb_ref, dg_ref):
        i = pl.program_id(0)

        @pl.when(i == 0)
        def _():
            dg_ref[...] = jnp.zeros_like(dg_ref)

        dyv = (jnp.dot(a_ref[0], b_ref[:F, :], preferred_element_type=F32)
               + jnp.dot(a_ref[1], b_ref[F:, :], preferred_element_type=F32))
        hf = h_ref[...]
        r = lax.rsqrt(jnp.mean(hf * hf, axis=-1, keepdims=True) + EPS)
        hhat = hf * r
        dg_ref[0:1, :] += jnp.sum(dyv * hhat, axis=0, keepdims=True)
        dhat = dyv * g_ref[...]
        dh = r_ref[...] + r * (dhat - hhat * jnp.mean(dhat * hhat, axis=-1, keepdims=True))
        dh_ref[...] = dh
        dhb_ref[...] = dh.astype(BF16)

    row = pl.BlockSpec((tm, D_MODEL), lambda i: (i, 0))
    return pl.pallas_call(
        body,
        out_shape=(jax.ShapeDtypeStruct((S, D_MODEL), F32), jax.ShapeDtypeStruct((S, D_MODEL), BF16),
                   jax.ShapeDtypeStruct((8, D_MODEL), F32)),
        grid=(S // tm,),
        in_specs=[pl.BlockSpec((2, tm, F), lambda i: (0, i, 0)), pl.BlockSpec((2 * F, D_MODEL), lambda i: (0, 0)),
                  row, pl.BlockSpec((1, D_MODEL), lambda i: (0, 0)), row],
        out_specs=(row, row, pl.BlockSpec((8, D_MODEL), lambda i: (0, 0))),
        name="ffn_up_dx_rms", compiler_params=_cp(("arbitrary",), VMEM_LIMIT),
    )(dhu, w_up_bf, h1, g_ffn, dh2)


def _conv_gate_bwd(hc, hu, dact, cw):
    _, S, F = hu.shape
    tm = _pick(S, (CONV_TM, 128))
    tn = _pick(F, (CONV_TN, 256, 128))
    r8 = tm // 8
    n_i = S // tm
    last8 = S // 8 - 1

    def body(hc_ref, hcn_ref, hu_ref, da_ref, dan_ref, w_ref, dhu_ref, dcw_ref):
        i = pl.program_id(1)

        @pl.when(i == 0)
        def _():
            dcw_ref[...] = jnp.zeros_like(dcw_ref)

        rid8 = lax.broadcasted_iota(jnp.int32, (8, tn), 0)

        def gate_grads(a, g, d):
            sg = _sigmoid(g)
            return d * (g * sg), d * a * (sg * (1.0 + g * (1.0 - sg)))

        dhc = gate_grads(hc_ref[0], hc_ref[1], da_ref[...])
        dhc_n = gate_grads(hcn_ref[0], hcn_ref[1], dan_ref[...])
        for h in range(2):
            w = w_ref[h]
            d = dhc[h]
            dn = jnp.where(i < n_i - 1, dhc_n[h], 0.0)
            u1 = _shift_rows_up(d, dn, 1)
            u2 = _shift_rows_up(d, dn, 2)
            dhu_ref[h] = (w[2:3, :] * d + w[1:2, :] * u1 + w[0:1, :] * u2).astype(BF16)
            x = hu_ref[h].astype(F32)
            t0, t1, t2, t3 = [jnp.sum(t, axis=0, keepdims=True) for t in (u2 * x, u1 * x, d * x, d)]
            dcw_ref[h] += jnp.where(rid8 == 0, t0, jnp.where(rid8 == 1, t1, jnp.where(rid8 == 2, t2, jnp.where(rid8 == 3, t3, 0.0))))

    cur = pl.BlockSpec((2, tm, tn), lambda j, i: (0, i, j))
    return pl.pallas_call(
        body,
        out_shape=(jax.ShapeDtypeStruct((2, S, F), BF16), jax.ShapeDtypeStruct((2, 8, F), F32)),
        grid=(F // tn, n_i),
        in_specs=[cur, pl.BlockSpec((2, 8, tn), lambda j, i: (0, jnp.minimum((i + 1) * r8, last8), j)), cur,
                  pl.BlockSpec((tm, tn), lambda j, i: (i, j)),
                  pl.BlockSpec((8, tn), lambda j, i: (jnp.minimum((i + 1) * r8, last8), j)),
                  pl.BlockSpec((2, 8, tn), lambda j, i: (0, 0, j))],
        out_specs=(cur, pl.BlockSpec((2, 8, tn), lambda j, i: (0, 0, j))),
        name="conv_gate_bwd", compiler_params=_cp(("parallel", "arbitrary"), VMEM_LIMIT),
    )(hc, hc, hu, dact, dact, cw)


def _out_proj_dx_prep(dh1_bf, w_out_bf, att, lse, qa):
    S = att.shape[0]
    tm = _pick(S, (256,))

    def body(dh_ref, w_ref, o_ref, lse_ref, q_ref, dsg_ref, qb_ref, doa_ref):
        lane = lax.broadcasted_iota(jnp.int32, (tm, LANES), 1)
        dh = dh_ref[...]
        dsg_ref[...] = lax.dot_general(dh, w_ref[ATT_WIDTH:, :], _NT, preferred_element_type=F32)
        datt = lax.dot_general(dh, w_ref[:ATT_WIDTH, :], _NT, preferred_element_type=F32)
        for p in range(N_PAIRS):
            pc = slice(p * LANES, (p + 1) * LANES)
            do = datt[:, pc]
            prod = o_ref[:, pc] * do
            for hh in range(2):
                sel = (lane >= HEAD_DIM) if hh else (lane < HEAD_DIM)
                delta = jnp.sum(jnp.where(sel, prod, 0.0), axis=-1, keepdims=True)
                dod = pltpu.roll(do, HEAD_DIM, 1) if hh else do
                cols = slice((2 * p + hh) * LANES, (2 * p + hh + 1) * LANES)
                doa_ref[:, cols] = jnp.where(lane < HEAD_DIM, dod, _aug(lane, _split3f(-delta))).astype(BF16)
                lcol = p * LANES + hh * HEAD_DIM
                l3 = _split3f(-lse_ref[:, lcol:lcol + 1])
                augl = jnp.where(lane == HEAD_DIM + 6, l3[0], jnp.where(lane == HEAD_DIM + 7, l3[1], l3[2])).astype(BF16)
                qb_ref[:, cols] = jnp.where((lane >= HEAD_DIM + 6) & (lane < HEAD_DIM + 9), augl, q_ref[:, cols])

    half = pl.BlockSpec((tm, ATT_WIDTH), lambda i: (i, 0))
    wide = pl.BlockSpec((tm, N_HEADS * LANES), lambda i: (i, 0))
    return pl.pallas_call(
        body,
        out_shape=(jax.ShapeDtypeStruct((S, GMLP_WIDTH), F32), jax.ShapeDtypeStruct(qa.shape, BF16),
                   jax.ShapeDtypeStruct(qa.shape, BF16)),
        grid=(S // tm,),
        in_specs=[pl.BlockSpec((tm, D_MODEL), lambda i: (i, 0)), pl.BlockSpec((D_MODEL, D_MODEL), lambda i: (0, 0)),
                  half, half, wide],
        out_specs=(half, wide, wide),
        name="out_proj_dx_prep", compiler_params=_cp(("parallel",), VMEM_LIMIT),
    )(dh1_bf, w_out_bf, att, lse, qa)


def _attn_bwd(qb, ka, va, doa):
    S = qb.shape[0]
    tk = _pick(S, (512, 256))
    tq = tk
    nq = S // tq

    def pair(a, scale=None):
        lane = lax.broadcasted_iota(jnp.int32, (a.shape[0], LANES), 1)
        out = jnp.where(lane < HEAD_DIM, a[:, :LANES], pltpu.roll(a[:, LANES:], HEAD_DIM, 1))
        return out if scale is None else out * scale

    def head_lanes(a, col, sign, first):
        lane = lax.broadcasted_iota(jnp.int32, (a.shape[0], LANES), 1)
        return jnp.where(lane == first, sign * a[:, col:col + 1],
                         jnp.where(lane == first + 1, sign * a[:, LANES + col:LANES + col + 1], 0.0))

    def body(q_ref, do_ref, k_ref, v_ref, dqc_ref, dkc_ref, dvc_ref, dcq_ref, dck_ref, dq_ref, dka_ref, dva_ref):
        kb = pl.program_id(1)

        @pl.when(kb == 0)
        def _():
            dq_ref[...] = jnp.zeros_like(dq_ref)

        dka_ref[...] = jnp.zeros_like(dka_ref)
        dva_ref[...] = jnp.zeros_like(dva_ref)
        def sub_tile(qs, q_len, k_off, k_len, masked):
            keys = slice(k_off, k_off + k_len)
            for h in range(2):
                cols = slice(h * LANES, (h + 1) * LANES)
                qblk = q_ref[pl.ds(qs, q_len), cols]
                doblk = do_ref[pl.ds(qs, q_len), cols]
                kh = k_ref[keys, cols]
                p = jnp.exp(lax.dot_general(kh, qblk, _NT, preferred_element_type=F32))
                if masked:
                    p = jnp.where(lax.broadcasted_iota(jnp.int32, p.shape, 1) >= lax.broadcasted_iota(jnp.int32, p.shape, 0),
                                  p, 0.0)
                ds = (p * lax.dot_general(v_ref[keys, cols], doblk, _NT, preferred_element_type=F32)).astype(BF16)
                dva_ref[keys, cols] += jnp.dot(p.astype(BF16), doblk, preferred_element_type=F32)
                dka_ref[keys, cols] += jnp.dot(ds, qblk, preferred_element_type=F32)
                dq_ref[pl.ds(qs, q_len), cols] += lax.dot_general(ds, kh, _TN, preferred_element_type=F32)

        half = tk // 2
        sub_tile(pl.multiple_of(kb * tq, tq), tq, 0, half, True)
        sub_tile(pl.multiple_of(kb * tq + half, half), half, half, half, True)

        rest = nq - 1 - kb
        odd = rest % 2

        @pl.when(odd == 1)
        def _():
            sub_tile(pl.multiple_of((kb + 1) * tq, tq), tq, 0, tk, False)

        def step(t, carry):
            sub_tile(pl.multiple_of((kb + 1 + odd + 2 * t) * tq, tq), 2 * tq, 0, tk, False)
            return carry

        lax.fori_loop(0, rest // 2, step, 0)
        dka = dka_ref[...]
        dkc_ref[...] = pair(dka).astype(BF16)
        dvc_ref[...] = pair(dva_ref[...]).astype(BF16)
        first = 2 * pl.program_id(0)
        dck_ref[...] = head_lanes(dka, HEAD_DIM + 3, -1.0, first)

        @pl.when(kb == nq - 1)
        def _():
            dqa = dq_ref[...]
            dqc_ref[...] = pair(dqa, HEAD_DIM ** -0.5).astype(BF16)
            dcq_ref[...] = head_lanes(dqa, HEAD_DIM, 1.0, first)

    wide = 2 * LANES
    half = jax.ShapeDtypeStruct((S, ATT_WIDTH), BF16)
    slabs = jax.ShapeDtypeStruct((N_PAIRS, S, LANES), F32)
    return pl.pallas_call(
        body,
        out_shape=(half, half, half, slabs, slabs),
        grid=(N_PAIRS, nq),
        in_specs=[pl.BlockSpec((S, wide), lambda p, j: (0, p)), pl.BlockSpec((S, wide), lambda p, j: (0, p)),
                  pl.BlockSpec((tk, wide), lambda p, j: (j, p)), pl.BlockSpec((tk, wide), lambda p, j: (j, p))],
        out_specs=(pl.BlockSpec((S, LANES), lambda p, j: (0, p)), pl.BlockSpec((tk, LANES), lambda p, j: (j, p)),
                   pl.BlockSpec((tk, LANES), lambda p, j: (j, p)), pl.BlockSpec((None, S, LANES), lambda p, j: (p, 0, 0)),
                   pl.BlockSpec((None, tk, LANES), lambda p, j: (p, j, 0))),
        scratch_shapes=[pltpu.VMEM((S, wide), F32), pltpu.VMEM((tk, wide), F32), pltpu.VMEM((tk, wide), F32)],
        name="attn_bwd", compiler_params=_cp(("parallel", "arbitrary"), VMEM_LIMIT),
    )(qb, doa, ka, va)


def _gmlp_bwd(ug, dsg, gain, w_s, wt_s, bias_full):
    S = ug.shape[0]
    tm = _pick(S, (512, 256, 128))
    n_chunks = tm // CHUNK
    n_i = S // tm
    ones = _group_ones()
    nt = (((1,), (1,)), ((), ()))

    def body(ug_ref, dsg_ref, gain_ref, w_ref, wt_ref, bias_ref, ones_ref, dug_ref, dw_ref, dgain_ref, dbias_ref,
             dbacc_ref):
        i = pl.program_id(0)

        @pl.when(i == 0)
        def _():
            dw_ref[...] = jnp.zeros_like(dw_ref)
            dgain_ref[...] = jnp.zeros_like(dgain_ref)
            dbacc_ref[...] = jnp.zeros_like(dbacc_ref)

        ones_m = ones_ref[...]
        pu = ug_ref[:, :GMLP_WIDTH]
        pg = ug_ref[:, GMLP_WIDTH:]
        u = _gelu(pu)
        vr = _gelu(pg)
        ms = _dot3(vr * vr, ones_m) * (1.0 / GROUP_DIM)
        rinv = lax.rsqrt(ms + EPS)
        vhat = vr * rinv
        gain_v = gain_ref[...]
        vn = (vhat * gain_v).astype(BF16)
        mixed = _gmlp_mixed(vn, w_ref, bias_ref[...], n_chunks)
        dsg_v = dsg_ref[...]
        du = dsg_v * mixed
        dmixed = dsg_v * u
        dm_bf = dmixed.astype(BF16)
        lane = lax.broadcasted_iota(jnp.int32, (CHUNK, LANES), 1)
        row = lax.broadcasted_iota(jnp.int32, (CHUNK, CHUNK), 0)
        col = lax.broadcasted_iota(jnp.int32, (CHUNK, CHUNK), 1)
        wts = [jnp.where(col >= row, wt_ref[g], 0.0).astype(BF16) for g in range(N_GROUPS)]
        dvn_rows = []
        dbsum = jnp.zeros((CHUNK, GMLP_WIDTH), F32)
        for ci in range(n_chunks):
            rs = slice(ci * CHUNK, (ci + 1) * CHUNK)
            dbsum = dbsum + dmixed[rs, :]
            cols = []
            for pp in range(N_GROUPS // 2):
                cs = slice(pp * LANES, (pp + 1) * LANES)
                dm = dm_bf[rs, cs]
                dm_lo = jnp.where(lane < GROUP_DIM, dm, jnp.zeros_like(dm))
                dm_hi = jnp.where(lane >= GROUP_DIM, dm, jnp.zeros_like(dm))
                vb = vn[rs, cs]
                dw_ref[2 * pp] += lax.dot_general(dm_lo, vb, nt, preferred_element_type=F32)
                dw_ref[2 * pp + 1] += lax.dot_general(dm_hi, vb, nt, preferred_element_type=F32)
                cols.append(jnp.dot(wts[2 * pp], dm_lo, preferred_element_type=F32)
                            + jnp.dot(wts[2 * pp + 1], dm_hi, preferred_element_type=F32))
            dvn_rows.append(jnp.concatenate(cols, axis=1))
        dvn = jnp.concatenate(dvn_rows, axis=0)
        dbacc_ref[...] += dbsum
        dgain_ref[0:1, :] += jnp.sum(dvn * vhat, axis=0, keepdims=True)
        dvhat = dvn * gain_v
        gm = _dot3(dvhat * vhat, ones_m) * (1.0 / GROUP_DIM)
        dvr = rinv * (dvhat - vhat * gm)
        dug_ref[:, :GMLP_WIDTH] = (du * _gelu_grad(pu)).astype(BF16)
        dug_ref[:, GMLP_WIDTH:] = (dvr * _gelu_grad(pg)).astype(BF16)

        @pl.when(i == n_i - 1)
        def _():
            for g in range(N_GROUPS):
                dw_ref[g] = jnp.where(row >= col, dw_ref[g], 0.0)
            dbias_ref[...] = _dot3(dbacc_ref[...], ones_m)

    return pl.pallas_call(
        body,
        out_shape=(jax.ShapeDtypeStruct((S, 2 * GMLP_WIDTH), BF16), jax.ShapeDtypeStruct((N_GROUPS, CHUNK, CHUNK), F32),
                   jax.ShapeDtypeStruct((8, GMLP_WIDTH), F32), jax.ShapeDtypeStruct((CHUNK, GMLP_WIDTH), F32)),
        grid=(n_i,),
        in_specs=[pl.BlockSpec((tm, 2 * GMLP_WIDTH), lambda i: (i, 0)), pl.BlockSpec((tm, GMLP_WIDTH), lambda i: (i, 0)),
                  pl.BlockSpec((1, GMLP_WIDTH), lambda i: (0, 0)),
                  pl.BlockSpec((N_GROUPS, CHUNK, CHUNK), lambda i: (0, 0, 0)),
                  pl.BlockSpec((N_GROUPS, CHUNK, CHUNK), lambda i: (0, 0, 0)),
                  pl.BlockSpec((CHUNK, GMLP_WIDTH), lambda i: (0, 0)),
                  pl.BlockSpec((GMLP_WIDTH, GMLP_WIDTH), lambda i: (0, 0))],
        out_specs=(pl.BlockSpec((tm, 2 * GMLP_WIDTH), lambda i: (i, 0)),
                   pl.BlockSpec((N_GROUPS, CHUNK, CHUNK), lambda i: (0, 0, 0)),
                   pl.BlockSpec((8, GMLP_WIDTH), lambda i: (0, 0)),
                   pl.BlockSpec((CHUNK, GMLP_WIDTH), lambda i: (0, 0))),
        scratch_shapes=[pltpu.VMEM((CHUNK, GMLP_WIDTH), F32)],
        name="gmlp_bwd", compiler_params=_cp(("arbitrary",), VMEM_LIMIT),
    )(ug, dsg, gain, w_s, wt_s, bias_full, ones)


def _gate_bwd(dcq, dck, zf):
    S = zf.shape[0]
    tm = _pick(S, (256,))
    n_i = S // tm
    triu = (lax.broadcasted_iota(jnp.int32, (tm, tm), 0) <= lax.broadcasted_iota(jnp.int32, (tm, tm), 1)).astype(BF16)

    def body(dcq_ref, dck_ref, zf_ref, tri_ref, dzf_ref, dbf_ref, carry_ref):
        i = pl.program_id(0)

        @pl.when(i == 0)
        def _():
            carry_ref[...] = jnp.zeros_like(carry_ref)
            dbf_ref[...] = jnp.zeros_like(dbf_ref)

        lane = lax.broadcasted_iota(jnp.int32, (tm, LANES), 1)
        dc = dcq_ref[0] + dck_ref[0]
        for p in range(1, N_PAIRS):
            dc = dc + (dcq_ref[p] + dck_ref[p])
        dlf = _dot3l(tri_ref[...], dc) + carry_ref[0:1, :]
        carry_ref[0:1, :] = dlf[0:1, :]
        dz = jnp.where(lane < N_HEADS, dlf * _sigmoid(-zf_ref[...]), 0.0)
        dzf_ref[...] = dz.astype(BF16)
        dbf_ref[0:1, :] += jnp.sum(dz, axis=0, keepdims=True)

    return pl.pallas_call(
        body,
        out_shape=(jax.ShapeDtypeStruct((S, LANES), BF16), jax.ShapeDtypeStruct((8, LANES), F32)),
        grid=(n_i,),
        in_specs=[pl.BlockSpec((N_PAIRS, tm, LANES), lambda i: (0, n_i - 1 - i, 0)),
                  pl.BlockSpec((N_PAIRS, tm, LANES), lambda i: (0, n_i - 1 - i, 0)),
                  pl.BlockSpec((tm, LANES), lambda i: (n_i - 1 - i, 0)),
                  pl.BlockSpec((tm, tm), lambda i: (0, 0))],
        out_specs=(pl.BlockSpec((tm, LANES), lambda i: (n_i - 1 - i, 0)), pl.BlockSpec((8, LANES), lambda i: (0, 0))),
        scratch_shapes=[pltpu.VMEM((8, LANES), F32)],
        name="gate_bwd", compiler_params=_cp(("arbitrary",), VMEM_LIMIT),
    )(dcq, dck, zf, triu)


def _out_proj_fwd(att_bf, sg, w_out_bf, x, g_ffn):
    S = x.shape[0]
    tm = _pick(S, (512, 256))

    def body(a_ref, s_ref, w_ref, x_ref, g_ref, h_ref, hn_ref):
        h = (x_ref[...] + jnp.dot(a_ref[...], w_ref[:ATT_WIDTH, :], preferred_element_type=F32)
             + jnp.dot(s_ref[...], w_ref[ATT_WIDTH:, :], preferred_element_type=F32))
        h_ref[...] = h
        r = lax.rsqrt(jnp.mean(h * h, axis=-1, keepdims=True) + EPS)
        hn_ref[...] = ((h * r) * g_ref[...]).astype(BF16)

    row = pl.BlockSpec((tm, D_MODEL), lambda i: (i, 0))
    half = pl.BlockSpec((tm, ATT_WIDTH), lambda i: (i, 0))
    return pl.pallas_call(
        body, out_shape=(jax.ShapeDtypeStruct((S, D_MODEL), F32), jax.ShapeDtypeStruct((S, D_MODEL), BF16)),
        grid=(S // tm,),
        in_specs=[half, half, pl.BlockSpec((D_MODEL, D_MODEL), lambda i: (0, 0)), row,
                  pl.BlockSpec((1, D_MODEL), lambda i: (0, 0))],
        out_specs=(row, row), name="out_proj", compiler_params=_cp(("parallel",), VMEM_LIMIT),
    )(att_bf, sg, w_out_bf, x, g_ffn)


def _out_proj_dw(att_bf, sg, dh1_bf):
    S = att_bf.shape[0]
    tk = _pick(S, (1024, 512))

    def body(a_ref, s_ref, d_ref, o_ref):
        k = pl.program_id(0)

        @pl.when(k == 0)
        def _():
            o_ref[...] = jnp.zeros_like(o_ref)

        d = d_ref[...]
        o_ref[:ATT_WIDTH, :] += lax.dot_general(a_ref[...], d, _TN, preferred_element_type=F32)
        o_ref[ATT_WIDTH:, :] += lax.dot_general(s_ref[...], d, _TN, preferred_element_type=F32)

    half = pl.BlockSpec((tk, ATT_WIDTH), lambda k: (k, 0))
    return pl.pallas_call(
        body, out_shape=jax.ShapeDtypeStruct((D_MODEL, D_MODEL), F32), grid=(S // tk,),
        in_specs=[half, half, pl.BlockSpec((tk, D_MODEL), lambda k: (k, 0))],
        out_specs=pl.BlockSpec((D_MODEL, D_MODEL), lambda k: (0, 0)),
        name="out_proj_dw", compiler_params=_cp(("arbitrary",), VMEM_LIMIT),
    )(att_bf, sg, dh1_bf)


_IN_PIECES = ((0, ATT_WIDTH), (ATT_WIDTH, ATT_WIDTH), (2 * ATT_WIDTH, ATT_WIDTH), (QKV, 2 * GMLP_WIDTH), (UG_END, LANES))


def _inproj_bwd_dx(pieces, w_pad, x, g_mix, dh1):
    S = x.shape[0]
    tm = _pick(S, (512, 256))

    def body(*refs):
        p_refs, (w_ref, x_ref, g_ref, r_ref, dx_ref, dg_ref) = refs[:5], refs[5:]
        i = pl.program_id(0)

        @pl.when(i == 0)
        def _():
            dg_ref[...] = jnp.zeros_like(dg_ref)

        dxn = None
        for p_ref, (c0, width) in zip(p_refs, _IN_PIECES):
            part = jnp.dot(p_ref[...], w_ref[c0:c0 + width, :], preferred_element_type=F32)
            dxn = part if dxn is None else dxn + part
        xf = x_ref[...]
        r = lax.rsqrt(jnp.mean(xf * xf, axis=-1, keepdims=True) + EPS)
        xhat = xf * r
        dg_ref[0:1, :] += jnp.sum(dxn * xhat, axis=0, keepdims=True)
        dhat = dxn * g_ref[...]
        dx_ref[...] = r_ref[...] + r * (dhat - xhat * jnp.mean(dhat * xhat, axis=-1, keepdims=True))

    row = pl.BlockSpec((tm, D_MODEL), lambda i: (i, 0))
    return pl.pallas_call(
        body, out_shape=(jax.ShapeDtypeStruct((S, D_MODEL), F32), jax.ShapeDtypeStruct((8, D_MODEL), F32)),
        grid=(S // tm,),
        in_specs=[pl.BlockSpec((tm, width), lambda i: (i, 0)) for _, width in _IN_PIECES]
        + [pl.BlockSpec((IN_PAD, D_MODEL), lambda i: (0, 0)), row, pl.BlockSpec((1, D_MODEL), lambda i: (0, 0)), row],
        out_specs=(row, pl.BlockSpec((8, D_MODEL), lambda i: (0, 0))),
        name="in_proj_dx", compiler_params=_cp(("arbitrary",), VMEM_LIMIT),
    )(*pieces, w_pad, x, g_mix, dh1)


def _inproj_bwd_dw(xn, pieces):
    S = xn.shape[0]
    tk = _pick(S, (1024, 512))

    def body(*refs):
        x_ref, p_refs, o_ref = refs[0], refs[1:6], refs[6]
        k = pl.program_id(0)

        @pl.when(k == 0)
        def _():
            o_ref[...] = jnp.zeros_like(o_ref)

        xb = x_ref[...]
        for p_ref, (c0, width) in zip(p_refs, _IN_PIECES):
            o_ref[:, c0:c0 + width] += lax.dot_general(xb, p_ref[...], _TN, preferred_element_type=F32)

    return pl.pallas_call(
        body, out_shape=jax.ShapeDtypeStruct((D_MODEL, IN_PAD), F32), grid=(S // tk,),
        in_specs=[pl.BlockSpec((tk, D_MODEL), lambda k: (k, 0))]
        + [pl.BlockSpec((tk, width), lambda k: (k, 0)) for _, width in _IN_PIECES],
        out_specs=pl.BlockSpec((D_MODEL, IN_PAD), lambda k: (0, 0)),
        name="in_proj_dw", compiler_params=_cp(("arbitrary",), VMEM_LIMIT),
    )(xn, *pieces)


def _adamw(w, m, v, parts, name):
    R, C = w.shape[-2:]
    tr = R
    for cand in (256, 128, 64, 32, 16, 8):
        if R % cand == 0 and R > cand:
            tr = cand
            break
    c1 = 1.0 / (1.0 - ADAM_B1 ** ADAM_STEP)
    c2 = 1.0 / (1.0 - ADAM_B2 ** ADAM_STEP)

    def body(w_ref, m_ref, v_ref, p_ref, g_ref, d_ref, nm_ref, nv_ref):
        g = p_ref[0].astype(F32)
        for j in range(1, N_DEV):
            g = g + p_ref[j].astype(F32)
        g_ref[...] = g
        nm = ADAM_B1 * m_ref[...] + (1.0 - ADAM_B1) * g
        nv = ADAM_B2 * v_ref[...] + (1.0 - ADAM_B2) * (g * g)
        nm_ref[...] = nm
        nv_ref[...] = nv
        d_ref[...] = -ADAM_LR * ((nm * c1) / (jnp.sqrt(nv * c2) + ADAM_EPS) + ADAM_WD * w_ref[...])

    if w.ndim == 3:
        spec = pl.BlockSpec((None, tr, C), lambda i: (0, i, 0))
    else:
        spec = pl.BlockSpec((tr, C), lambda i: (i, 0))
    shp = jax.ShapeDtypeStruct(w.shape, F32)
    return pl.pallas_call(
        body, out_shape=(shp, shp, shp, shp), grid=(R // tr,),
        in_specs=[spec, spec, spec, pl.BlockSpec((N_DEV, tr, C), lambda i: (0, i, 0))],
        out_specs=(spec, spec, spec, spec),
        name=name, compiler_params=_cp(("parallel",), VMEM_LIMIT),
    )(w, m, v, parts)


def _adamw_owner(w, m, v, landed, sent, me, name):
    R, C = w.shape[-2:]
    tr = R
    for cand in (256, 128, 64, 32, 16, 8):
        if R % cand == 0 and R > cand:
            tr = cand
            break
    c1 = 1.0 / (1.0 - ADAM_B1 ** ADAM_STEP)
    c2 = 1.0 / (1.0 - ADAM_B2 ** ADAM_STEP)

    def body(me_ref, w_ref, m_ref, v_ref, p_ref, own_ref, g_ref, d_ref, nm_ref, nv_ref):
        mine = me_ref[0]
        own = own_ref[...].astype(F32)
        g = jnp.where(mine == 0, own, p_ref[0].astype(F32))
        for j in range(1, N_DEV):
            g = g + jnp.where(mine == j, own, p_ref[j].astype(F32))
        g_ref[...] = g
        nm = ADAM_B1 * m_ref[...] + (1.0 - ADAM_B1) * g
        nv = ADAM_B2 * v_ref[...] + (1.0 - ADAM_B2) * (g * g)
        nm_ref[...] = nm
        nv_ref[...] = nv
        d_ref[...] = -ADAM_LR * ((nm * c1) / (jnp.sqrt(nv * c2) + ADAM_EPS) + ADAM_WD * w_ref[...])

    spec = pl.BlockSpec((None, tr, C), lambda i, me_ref: (0, i, 0))
    shp = jax.ShapeDtypeStruct(w.shape, F32)
    return pl.pallas_call(
        body, out_shape=(shp, shp, shp, shp),
        grid_spec=pltpu.PrefetchScalarGridSpec(
            num_scalar_prefetch=1, grid=(R // tr,),
            in_specs=[spec, spec, spec, pl.BlockSpec((N_DEV, tr, C), lambda i, me_ref: (0, i, 0)),
                      pl.BlockSpec((None, tr, C), lambda i, me_ref: (me_ref[0], i, 0))],
            out_specs=(spec, spec, spec, spec)),
        name=name, compiler_params=_cp(("parallel",), VMEM_LIMIT),
    )(jnp.reshape(me, (1,)).astype(jnp.int32), w, m, v, landed, sent)


def _adamw_packed(w, m, v, parts, sizes, name):
    R = w.shape[0]
    assert R == sum(sizes)
    c1 = 1.0 / (1.0 - ADAM_B1 ** ADAM_STEP)
    c2 = 1.0 / (1.0 - ADAM_B2 ** ADAM_STEP)
    n = len(sizes)

    def body(w_ref, m_ref, v_ref, p_ref, *out_refs):
        g = p_ref[0]
        for j in range(1, N_DEV):
            g = g + p_ref[j]
        nm = ADAM_B1 * m_ref[...] + (1.0 - ADAM_B1) * g
        nv = ADAM_B2 * v_ref[...] + (1.0 - ADAM_B2) * (g * g)
        d = -ADAM_LR * ((nm * c1) / (jnp.sqrt(nv * c2) + ADAM_EPS) + ADAM_WD * w_ref[...])
        for kind, val in enumerate((g, d, nm, nv)):
            off = 0
            for k, rows in enumerate(sizes):
                out_refs[kind * n + k][...] = val[off:off + rows, :]
                off += rows

    whole = pl.BlockSpec((R, LANES), lambda i: (0, 0))
    shapes = [jax.ShapeDtypeStruct((rows, LANES), F32) for rows in sizes] * 4
    res = pl.pallas_call(
        body, out_shape=tuple(shapes), grid=(1,),
        in_specs=[whole, whole, whole, pl.BlockSpec((N_DEV, R, LANES), lambda i: (0, 0, 0))],
        out_specs=tuple(pl.BlockSpec((rows, LANES), lambda i: (0, 0)) for rows in sizes) * 4,
        name=name, compiler_params=_cp(("arbitrary",), VMEM_LIMIT),
    )(w, m, v, parts)
    return [list(res[kind * n:(kind + 1) * n]) for kind in range(4)]


def _place():
    x, y, c = lax.axis_index("x"), lax.axis_index("y"), lax.axis_index("c")
    return x, y, c


def _all_gather(blocks, name):
    n = len(blocks)

    def body(*refs):
        ins, outs = refs[:n], refs[n:2 * n]
        send_sems, recv_sems, local_sems = refs[2 * n:]
        x, y, c = _place()
        me, sibling = (x, y, c), (x, y, 1 - c)
        chips = [(1 - x, y), (x, 1 - y), (1 - x, 1 - y)]
        sends = []
        for a in range(n):
            out = outs[a]

            def slot(px, py, pc, out=out):
                return out.at[4 * px + 2 * py + pc]

            def copy(k, block, to, src=None, a=a, slot=slot):
                return pltpu.make_async_remote_copy(
                    src_ref=slot(*block) if src is None else src, dst_ref=slot(*block),
                    send_sem=send_sems.at[a, k], recv_sem=recv_sems.at[a, k], device_id=to, device_id_type=MESH)

            mine = pltpu.make_async_copy(ins[a], slot(*me), local_sems.at[a])
            mine.start()
            first = [copy(0, me, sibling, src=ins[a])]
            first += [copy(1 + j, me, (*chip, c), src=ins[a]) for j, chip in enumerate(chips)]
            for cp in first:
                cp.start()
            sends.append((mine, first, copy))
        for a in range(n):
            mine, first, copy = sends[a]
            passed = [copy(4 + j, (*chip, c), sibling) for j, chip in enumerate(chips)]
            for j, chip in enumerate(chips):
                copy(1 + j, (*chip, c), me).wait_recv()
                passed[j].start()
            copy(0, sibling, me).wait_recv()
            for j, chip in enumerate(chips):
                copy(4 + j, (*chip, 1 - c), me).wait_recv()
            for cp in first + passed:
                cp.wait_send()
            mine.wait()

    any_spec = pl.BlockSpec(memory_space=pl.ANY)
    return pl.pallas_call(
        body, out_shape=tuple(jax.ShapeDtypeStruct((N_DEV,) + b.shape, b.dtype) for b in blocks),
        in_specs=[any_spec] * n, out_specs=tuple([any_spec] * n),
        scratch_shapes=[pltpu.SemaphoreType.DMA((n, 7)), pltpu.SemaphoreType.DMA((n, 7)), pltpu.SemaphoreType.DMA((n,))],
        name=name,
    )(*blocks)


_HBM = pl.BlockSpec(memory_space=pltpu.HBM)
_SEM = pl.BlockSpec(memory_space=pltpu.SEMAPHORE)
_EFFECT = pltpu.SideEffectType.DATAFLOW_SIDE_EFFECTING


def _peers(x, y, c):
    out = []
    for k in range(1, N_DEV):
        px, py, pc = x ^ ((k >> 2) & 1), y ^ ((k >> 1) & 1), c ^ (k & 1)
        out.append((k, (px, py, pc), 4 * px + 2 * py + pc))
    return out


def _xchg_copies(src_refs, land_refs, send_sems, recv_sems, scatter):
    x, y, c = _place()
    me = 4 * x + 2 * y + c
    copies = []
    for a, (src, land) in enumerate(zip(src_refs, land_refs)):
        for k, place, idx in _peers(x, y, c):
            j = a * (N_DEV - 1) + k - 1
            copies.append(pltpu.make_async_remote_copy(
                src_ref=src.at[idx] if scatter[a] else src, dst_ref=land.at[me],
                send_sem=send_sems[j], recv_sem=recv_sems[j], device_id=place, device_id_type=MESH))
    return copies


def _xchg_start(srcs, scatter, name):
    n = len(srcs)
    lands = [lax.empty((N_DEV,) + (s.shape[1:] if sc else s.shape), s.dtype) for s, sc in zip(srcs, scatter)]

    ns = n * (N_DEV - 1)

    def body(*refs):
        sems = refs[2 * n:2 * n + 2 * ns]
        for cp in _xchg_copies(refs[:n], refs[n:2 * n], sems[:ns], sems[ns:], scatter):
            cp.start()
        token = refs[-1]
        token[...] = jnp.zeros_like(token)

    both = list(srcs) + lands
    res = pl.pallas_call(
        body, name=name,
        out_shape=(*[pltpu.SemaphoreType.DMA(())] * (2 * ns),
                   *[pltpu.HBM(a.shape, a.dtype) for a in both], jax.ShapeDtypeStruct((8, LANES), F32)),
        in_specs=[_HBM] * (2 * n),
        out_specs=(*([_SEM] * (2 * ns)), *([_HBM] * (2 * n)), pl.BlockSpec(memory_space=pltpu.VMEM)),
        input_output_aliases={i: 2 * ns + i for i in range(2 * n)},
        compiler_params=pltpu.CompilerParams(has_side_effects=_EFFECT),
    )(*[pltpu.with_memory_space_constraint(a, pltpu.HBM) for a in both])
    return (tuple(res[:2 * ns]), tuple(res[2 * ns:2 * ns + 2 * n])), res[-1]


def _xchg_wait(handle, scatter, after, name):
    sems, thru = handle
    n = len(thru) // 2
    ns = len(sems) // 2

    def body(*refs):
        got = refs[2 * n:2 * n + 2 * ns]
        for cp in _xchg_copies(refs[:n], refs[n:2 * n], got[:ns], got[ns:], scatter):
            cp.wait_send()
            cp.wait_recv()

    outs = pl.pallas_call(
        body, name=name, out_shape=tuple(pltpu.HBM(a.shape, a.dtype) for a in thru),
        in_specs=[_HBM] * (2 * n) + [_SEM] * (2 * ns) + [pl.BlockSpec(memory_space=pl.ANY)],
        out_specs=tuple([_HBM] * (2 * n)), input_output_aliases={i: i for i in range(2 * n)},
        compiler_params=pltpu.CompilerParams(has_side_effects=_EFFECT),
    )(*thru, *sems, after)
    return outs[:n], outs[n:]


def _tie(a, token):
    return a if token is None else a + token[0, 0].astype(a.dtype)


def _rows128(a):
    flat = a.reshape(-1)
    rows = -(-flat.shape[0] // LANES)
    rows = -(-rows // 8) * 8
    return jnp.pad(flat, (0, rows * LANES - flat.shape[0])).reshape(rows, LANES)


def _local_step(x, target, norm_mix_g, w_in_t, b_forget, gmlp_norm_g, w_spatial, b_spatial, norm_ffn_g, conv_b,
                norm_final_g, rest_fn, send_fn, small_fn, token=None):
    f = D_FF
    g_mix = norm_mix_g.reshape(1, D_MODEL)
    w_pad = jnp.pad(w_in_t, ((0, IN_PAD - IN_COLS), (0, 0)))
    bf_pad = jnp.pad(b_forget.reshape(1, N_HEADS), ((0, 0), (0, LANES - N_HEADS)))
    xn, qa, ka, va, ug, zf = _inproj_fwd(x, _tie(g_mix, token), w_pad, bf_pad)
    bias_full = jnp.repeat(b_spatial.reshape(N_GROUPS, CHUNK).T, GROUP_DIM, axis=1)
    w_s = w_spatial.reshape(N_GROUPS, CHUNK, CHUNK)
    gain = gmlp_norm_g.reshape(1, GMLP_WIDTH)
    sg = _gmlp_fwd(ug, gain, w_s, bias_full)
    att, lse, att_bf = _attn_fwd(qa, ka, va)
    w_out_bf, w_up_bf, conv_w, w_down_bf = rest_fn(att_bf)
    g_ffn = norm_ffn_g.reshape(1, D_MODEL)
    h1, hn = _out_proj_fwd(att_bf, sg, w_out_bf, x, g_ffn)
    cw = jnp.pad(conv_w.reshape(3, 2, f).transpose(1, 0, 2), ((0, 0), (0, 5), (0, 0)))
    cb = conv_b.reshape(2, 1, f)
    hu, hc, act = _ffn_up_conv(hn, w_up_bf, cw, cb)
    loss_blk, dh2, dh2_bf, dg_final = _ffn_down_loss(act, w_down_bf, h1, norm_final_g.reshape(1, D_MODEL), target)
    dw_down = _mm(act, dh2_bf, mode="tn", out_dtype=F32, tm=1408, tn=1024, tk=2048, name="ffn_down_dw")
    dact = _mm(dh2_bf, w_down_bf, mode="nt", out_dtype=F32, tm=1024, tn=1408, tk=1024, outer="j", name="ffn_down_dx")
    dhu, dcw = _conv_gate_bwd(hc, hu, dact, _tie(cw, send_fn("w_down", dw_down)))
    dw_up = _mm(hn, dhu, mode="tn", out_dtype=F32, tm=1024, tn=1408, tk=2048, b_halves=True, outer="j", name="ffn_up_dw")
    dh1, dh1_bf, dg_ffn = _ffn_up_dx_rms(dhu, w_up_bf, h1, _tie(g_ffn, send_fn("w_up", dw_up)), dh2)
    dsg, qb, doa = _out_proj_dx_prep(dh1_bf, w_out_bf, att, lse, qa)
    dw_out = _out_proj_dw(att_bf, sg, dh1_bf)
    dq, dk, dv, dcq, dck = _attn_bwd(qb, ka, va, doa)
    wt_s = w_s.transpose(0, 2, 1)
    dug, dw_s, dgain, dbias = _gmlp_bwd(ug, dsg, _tie(gain, send_fn("w_out", dw_out)), w_s, wt_s, bias_full)
    dzf, dbf = _gate_bwd(dcq, dck, zf)
    grad_x, dg_mix = _inproj_bwd_dx((dq, dk, dv, dug, dzf), w_pad, x, g_mix, dh1)
    grads = dict(
        norm_mix_g=dg_mix[0:1, :],
        b_forget=dbf[0:1, :N_HEADS],
        gmlp_norm_g=dgain[0:1, :],
        w_spatial=dw_s,
        b_spatial=dbias[:, ::GROUP_DIM].T,
        norm_ffn_g=dg_ffn[0:1, :],
        conv_w=dcw[:, 0:3, :].transpose(1, 0, 2).reshape(3, 2 * f),
        conv_b=dcw[:, 3, :].reshape(1, 2 * f),
        norm_final_g=dg_final[0, :],
    )
    token = small_fn(loss_blk[0, 0], grads)
    dw_in = _inproj_bwd_dw(xn, (dq, dk, dv, dug, _tie(dzf, token)))
    return grad_x, send_fn("w_in", dw_in[:, :IN_COLS])


SMALL = ("norm_mix_g", "b_forget", "gmlp_norm_g", "w_spatial", "b_spatial", "norm_ffn_g", "conv_b", "norm_final_g")


def kernel(x, norm_mix_g, w_in, b_forget, gmlp_norm_g, w_spatial, b_spatial, w_out, norm_ffn_g, w_up, conv_w, conv_b, w_down, norm_final_g, loss_target, m_norm_mix_g, m_w_in, m_b_forget, m_gmlp_norm_g, m_w_spatial, m_b_spatial, m_w_out, m_norm_ffn_g, m_w_up, m_conv_w, m_conv_b, m_w_down, m_norm_final_g, v_norm_mix_g, v_w_in, v_b_forget, v_gmlp_norm_g, v_w_spatial, v_b_spatial, v_w_out, v_norm_ffn_g, v_w_up, v_conv_w, v_conv_b, v_w_down, v_norm_final_g):
    weights = dict(norm_mix_g=norm_mix_g, w_in=w_in, b_forget=b_forget, gmlp_norm_g=gmlp_norm_g, w_spatial=w_spatial,
                   b_spatial=b_spatial, w_out=w_out, norm_ffn_g=norm_ffn_g, w_up=w_up, conv_w=conv_w, conv_b=conv_b,
                   w_down=w_down, norm_final_g=norm_final_g)
    m_in = dict(norm_mix_g=m_norm_mix_g, w_in=m_w_in, b_forget=m_b_forget, gmlp_norm_g=m_gmlp_norm_g,
                w_spatial=m_w_spatial, b_spatial=m_b_spatial, w_out=m_w_out, norm_ffn_g=m_norm_ffn_g, w_up=m_w_up,
                conv_w=m_conv_w, conv_b=m_conv_b, w_down=m_w_down, norm_final_g=m_norm_final_g)
    v_in = dict(norm_mix_g=v_norm_mix_g, w_in=v_w_in, b_forget=v_b_forget, gmlp_norm_g=v_gmlp_norm_g,
                w_spatial=v_w_spatial, b_spatial=v_b_spatial, w_out=v_w_out, norm_ffn_g=v_norm_ffn_g, w_up=v_w_up,
                conv_w=v_conv_w, conv_b=v_conv_b, w_down=v_w_down, norm_final_g=v_norm_final_g)
    order = list(weights)
    me = 4 * lax.axis_index("x") + 2 * lax.axis_index("y") + lax.axis_index("c")
    n_in, n_up = w_in.shape[2], w_up.shape[2]
    r_out, r_down = w_out.shape[1], w_down.shape[1]

    def with_mine(landed, mine):
        return lax.dynamic_update_index_in_dim(landed, mine, me, 0)

    up_blk = w_up[0].T.astype(BF16)
    out_blk = w_out[0].astype(BF16)
    down_blk = w_down[0].astype(BF16)
    taps_blk = jnp.pad(conv_w[0], ((0, 5), (0, 0)))
    (in_all,) = _all_gather([w_in[0].T.astype(BF16)], "gather_w_in")
    in_all, rest_blocks = lax.optimization_barrier((in_all, [up_blk, out_blk, down_blk, taps_blk]))
    rest_handle, token = _xchg_start(rest_blocks, [False] * 4, "gather_rest_start")
    w_in_t = in_all.reshape(N_DEV * n_in, D_MODEL)

    def rest_fn(after):
        mine, landed = _xchg_wait(rest_handle, [False] * 4, after, "gather_rest_wait")
        up_all, out_all, down_all, taps_all = [with_mine(l, b) for l, b in zip(landed, mine)]
        return (out_all.reshape(N_DEV * r_out, D_MODEL), up_all.reshape(N_DEV * n_up, D_MODEL),
                taps_all[:, :3, :].transpose(1, 0, 2).reshape(3, N_DEV * n_up),
                down_all.reshape(N_DEV * r_down, D_MODEL))

    sent = {}

    def send_fn(name, grad):
        if name == "w_in":
            parts = grad.reshape(D_MODEL, N_DEV, -1).transpose(1, 0, 2).astype(BF16)
        elif name == "w_up":
            parts = grad.reshape(D_MODEL, N_DEV, -1).transpose(1, 0, 2)
        else:
            parts = grad.reshape(N_DEV, -1, D_MODEL)
        sent[name], tok = _xchg_start([parts], [True], "scatter_" + name + "_start")
        return tok

    small = {}

    def small_fn(loss_local, g):
        loss_rows = jnp.pad(loss_local.reshape(1, 1), ((0, 31), (0, LANES - 1)))
        packed = [_rows128(g[k]) for k in SMALL] + [loss_rows, _rows128(g["conv_w"])]
        small["sizes"] = [p.shape[0] for p in packed]
        small["handle"], tok = _xchg_start([jnp.concatenate(packed, axis=0)], [False], "gather_small_start")
        return tok

    grad_x, after = _local_step(
        x[0], loss_target[0], norm_mix_g, w_in_t, b_forget, gmlp_norm_g, w_spatial, b_spatial, norm_ffn_g, conv_b,
        norm_final_g, rest_fn, send_fn, small_fn, token)

    outs = {}

    def update_big(name, after):
        (parts,), (landed,) = _xchg_wait(sent[name], [True], after, "scatter_" + name + "_wait")
        outs[name] = tuple(_adamw_owner(weights[name], m_in[name], v_in[name], landed, parts, me, "adamw_" + name))
        return outs[name][0]

    for name in ("w_down", "w_up", "w_out"):
        after = update_big(name, after)

    (mine,), (landed,) = _xchg_wait(small["handle"], [False], after, "gather_small_wait")
    small_all = with_mine(landed, mine)
    sizes = small["sizes"]
    n_small_rows = sum(sizes[:-2])

    def pack(src):
        return jnp.concatenate([_rows128(src[k]) for k in SMALL] + [jnp.zeros((sizes[-2], LANES), F32)], axis=0)

    n_adam_rows = n_small_rows + sizes[-2]
    per_kind = _adamw_packed(pack(weights), pack(m_in), pack(v_in), small_all, sizes[:-1], "adamw_small")
    loss = per_kind[0][-1][0, 0]
    for j, k in enumerate(SMALL):
        shp = weights[k].shape
        cnt = math.prod(shp)
        outs[k] = tuple(kind[j].reshape(-1)[:cnt].reshape(shp) for kind in per_kind)
    sg_ = per_kind[0][0]
    taps_parts = small_all[:, n_adam_rows:, :].reshape(N_DEV, -1)[:, :3 * N_DEV * n_up].reshape(N_DEV, 3, N_DEV * n_up)
    taps_mine = lax.dynamic_slice_in_dim(taps_parts, me * n_up, n_up, axis=2)
    taps_mine = jnp.pad(taps_mine, ((0, 0), (0, 5), (0, 0)))

    def pad8(a):
        return jnp.pad(a[0], ((0, 5), (0, 0)))

    res = _adamw(pad8(conv_w), pad8(m_conv_w), pad8(v_conv_w), taps_mine, "adamw_conv_w")
    outs["conv_w"] = tuple(a[:3][None] for a in res)
    update_big("w_in", sg_)

    return (loss, grad_x[None], *[outs[k][0] for k in order], *[outs[k][1] for k in order],
            *[outs[k][2] for k in order], *[outs[k][3] for k in order])
```

```python
import functools
import math

import jax
import jax.numpy as jnp
from jax import lax
from jax.experimental import pallas as pl
from jax.experimental.pallas import tpu as pltpu

F32 = jnp.float32
BF16 = jnp.bfloat16

N_DEV = 8
D_MODEL = 1024
ATT_WIDTH = 512
GMLP_WIDTH = 512
HEAD_DIM = 64
N_HEADS = 8
N_PAIRS = 4
N_GROUPS = 8
GROUP_DIM = 64
CHUNK = 128
D_FF = 2816
IN_COLS = 2568
IN_PAD = 2688
QKV = 1536
UG_END = 2560
EPS = 1e-6
LANES = 128

ADAM_LR = 0.001
ADAM_B1 = 0.9
ADAM_B2 = 0.999
ADAM_EPS = 1e-08
ADAM_WD = 0.01
ADAM_STEP = 10

ATT_TQ = 1024
ATT_TK = 1024
FFN_TM, FFN_TN = 512, 1408
CONV_TM, CONV_TN = 512, 1408
VMEM_LIMIT = 56 * 1024 * 1024
MESH = pl.DeviceIdType.MESH


def _cp(sem, vmem=None):
    return pltpu.CompilerParams(dimension_semantics=sem, vmem_limit_bytes=vmem)


def _pick(n, prefs):
    for p in prefs:
        if n % p == 0:
            return p
    return n


def _split3(x):
    hi = x.astype(BF16)
    r1 = x - hi.astype(F32)
    mid = r1.astype(BF16)
    lo = (r1 - mid.astype(F32)).astype(BF16)
    return hi, mid, lo


def _dot3(x, ones_bf):
    d = functools.partial(jnp.dot, preferred_element_type=F32)
    out = []
    for c in range(0, x.shape[1], 2 * LANES):
        blk = ones_bf[c:c + 2 * LANES, c:c + 2 * LANES]
        hi, mid, lo = _split3(x[:, c:c + 2 * LANES])
        out.append(d(hi, blk) + d(mid, blk) + d(lo, blk))
    return jnp.concatenate(out, axis=1)


def _dot3l(ones_bf, x):
    n = x.shape[1]
    y = jnp.dot(ones_bf, jnp.concatenate(_split3(x), axis=1), preferred_element_type=F32)
    return y[:, :n] + y[:, n:2 * n] + y[:, 2 * n:]


def _gelu(x):
    k = math.sqrt(2.0 / math.pi)
    t = jnp.tanh(k * (x + 0.044715 * (x * x * x)))
    return 0.5 * x * (1.0 + t)


def _gelu_grad(x):
    k = math.sqrt(2.0 / math.pi)
    x2 = x * x
    t = jnp.tanh(k * (x + 0.044715 * (x2 * x)))
    return 0.5 * (1.0 + t) + 0.5 * x * (1.0 - t * t) * (k * (1.0 + 3.0 * 0.044715 * x2))


def _sigmoid(x):
    return 1.0 / (1.0 + jnp.exp(-x))


def _mm(a, b, *, mode, out_dtype, tm, tn, tk, name, res=None, a_halves=False, b_halves=False,
        out_halves=False, outer="i"):
    if mode == "tn":
        K, M = a.shape[-2], a.shape[-1] * (2 if a_halves else 1)
    else:
        M, K = a.shape[-2], a.shape[-1] * (2 if a_halves else 1)
    if mode == "nt":
        N = b.shape[-2]
        assert b.shape[-1] == K
    else:
        N = b.shape[-1] * (2 if b_halves else 1)
    tm, tn, tk = min(tm, M), min(tn, N), min(tk, K)
    assert M % tm == 0 and N % tn == 0 and K % tk == 0, (name, M, N, K, tm, tn, tk)
    nm, nn, nk = M // tm, N // tn, K // tk

    def ij(g0, g1):
        return (g0, g1) if outer == "i" else (g1, g0)

    if mode == "nn":
        dims = (((1,), (0,)), ((), ()))
        if a_halves:
            nkh = nk // 2
            a_spec = pl.BlockSpec((None, tm, tk), lambda g0, g1, k: (k // nkh, ij(g0, g1)[0], k % nkh))
        else:
            a_spec = pl.BlockSpec((tm, tk), lambda g0, g1, k: (ij(g0, g1)[0], k))
        b_spec = pl.BlockSpec((tk, tn), lambda g0, g1, k: (k, ij(g0, g1)[1]))
    elif mode == "nt":
        dims = (((1,), (1,)), ((), ()))
        if a_halves:
            nkh = nk // 2
            a_spec = pl.BlockSpec((None, tm, tk), lambda g0, g1, k: (k // nkh, ij(g0, g1)[0], k % nkh))
        else:
            a_spec = pl.BlockSpec((tm, tk), lambda g0, g1, k: (ij(g0, g1)[0], k))
        b_spec = pl.BlockSpec((tn, tk), lambda g0, g1, k: (ij(g0, g1)[1], k))
    else:
        dims = (((0,), (0,)), ((), ()))
        if a_halves:
            nmh = nm // 2
            a_spec = pl.BlockSpec((None, tk, tm), lambda g0, g1, k: (ij(g0, g1)[0] // nmh, k, ij(g0, g1)[0] % nmh))
        else:
            a_spec = pl.BlockSpec((tk, tm), lambda g0, g1, k: (k, ij(g0, g1)[0]))
        if b_halves:
            nnh = nn // 2
            b_spec = pl.BlockSpec((None, tk, tn), lambda g0, g1, k: (ij(g0, g1)[1] // nnh, k, ij(g0, g1)[1] % nnh))
        else:
            b_spec = pl.BlockSpec((tk, tn), lambda g0, g1, k: (k, ij(g0, g1)[1]))
    if out_halves:
        nnh = nn // 2
        o_spec = pl.BlockSpec((None, tm, tn), lambda g0, g1, k: (ij(g0, g1)[1] // nnh, ij(g0, g1)[0], ij(g0, g1)[1] % nnh))
        o_shape = jax.ShapeDtypeStruct((2, M, N // 2), out_dtype)
    else:
        o_spec = pl.BlockSpec((tm, tn), lambda g0, g1, k: ij(g0, g1))
        o_shape = jax.ShapeDtypeStruct((M, N), out_dtype)
    in_specs = [a_spec, b_spec]
    args = [a, b]
    if res is not None:
        in_specs.append(pl.BlockSpec((tm, tn), lambda g0, g1, k: ij(g0, g1)))
        args.append(res)

    def body(*refs):
        if res is not None:
            a_ref, b_ref, r_ref, o_ref = refs[:4]
        else:
            a_ref, b_ref, o_ref = refs[:3]
            r_ref = None
        part = lax.dot_general(a_ref[...], b_ref[...], dims, preferred_element_type=F32)
        if nk == 1:
            if r_ref is not None:
                part = part + r_ref[...]
            o_ref[...] = part.astype(out_dtype)
            return
        acc_ref = refs[-1]
        k = pl.program_id(2)

        @pl.when(k == 0)
        def _():
            acc_ref[...] = part

        @pl.when(k > 0)
        def _():
            acc_ref[...] += part

        @pl.when(k == nk - 1)
        def _():
            tot = acc_ref[...]
            if r_ref is not None:
                tot = tot + r_ref[...]
            o_ref[...] = tot.astype(out_dtype)

    grid = (nm, nn, nk) if outer == "i" else (nn, nm, nk)
    scratch = [] if nk == 1 else [pltpu.VMEM((tm, tn), F32)]
    return pl.pallas_call(
        body, out_shape=o_shape, grid=grid, in_specs=in_specs, out_specs=o_spec, scratch_shapes=scratch,
        name=name, compiler_params=_cp(("parallel", "parallel", "arbitrary"), VMEM_LIMIT),
    )(*args)


def _aug(lane, terms):
    out = 0.0
    for j, t in enumerate(terms):
        out = jnp.where(lane == HEAD_DIM + j, t, out)
    return out


def _split3f(x):
    hi, mid, lo = _split3(x)
    return [hi.astype(F32), mid.astype(F32), lo.astype(F32)]


def _inproj_fwd(x, g_mix, w_pad, bf_pad):
    S = x.shape[0]
    tm = _pick(S, (512, 256))
    tri = (lax.broadcasted_iota(jnp.int32, (tm, tm), 0) >= lax.broadcasted_iota(jnp.int32, (tm, tm), 1)).astype(BF16)

    def body(x_ref, g_ref, w_ref, bf_ref, tri_ref, put_ref, one_ref, xn_ref, qa_ref, ka_ref, va_ref, ug_ref, zf_ref,
             carry_ref):
        i = pl.program_id(0)

        @pl.when(i == 0)
        def _():
            carry_ref[...] = jnp.zeros_like(carry_ref)

        xf = x_ref[...]
        r = lax.rsqrt(jnp.mean(xf * xf, axis=-1, keepdims=True) + EPS)
        xn = ((xf * r) * g_ref[...]).astype(BF16)
        xn_ref[...] = xn
        proj = lax.dot_general(xn, w_ref[...], _NT, preferred_element_type=F32)
        ug_ref[...] = proj[:, QKV:UG_END]
        zf = proj[:, UG_END:] + bf_ref[...]
        zf_ref[...] = zf
        lf = jnp.minimum(zf, 0.0) - jnp.log(1.0 + jnp.exp(-jnp.abs(zf)))
        c = _dot3l(tri_ref[...], lf) + carry_ref[0:1, :]
        carry_ref[0:1, :] = c[tm - 1:tm, :]
        c3 = jnp.concatenate(_split3(c), axis=1)
        aug_q = jnp.dot(c3, put_ref[0], preferred_element_type=F32) + one_ref[0:1, :]
        aug_k = jnp.dot(c3, put_ref[1], preferred_element_type=F32) + one_ref[1:2, :]
        lane = lax.broadcasted_iota(jnp.int32, (tm, LANES), 1)
        for h in range(N_HEADS):
            p, odd = h // 2, h % 2

            def head(base, scale=None, p=p, odd=odd):
                blk = proj[:, base + p * LANES:base + (p + 1) * LANES]
                if scale is not None:
                    blk = blk * scale
                return pltpu.roll(blk, HEAD_DIM, 1) if odd else blk

            cols = slice(h * LANES, (h + 1) * LANES)
            qa_ref[:, cols] = jnp.where(lane < HEAD_DIM, head(0, HEAD_DIM ** -0.5), aug_q[:, cols]).astype(BF16)
            ka_ref[:, cols] = jnp.where(lane < HEAD_DIM, head(ATT_WIDTH), aug_k[:, cols]).astype(BF16)
            va_ref[:, cols] = jnp.where(lane < HEAD_DIM, head(2 * ATT_WIDTH), one_ref[2:3, cols]).astype(BF16)

    wide = N_HEADS * LANES
    src = lax.broadcasted_iota(jnp.int32, (3 * LANES, wide), 0)
    col = lax.broadcasted_iota(jnp.int32, (3 * LANES, wide), 1)
    hd, term = src % LANES, src // LANES
    to_q = (col == hd * LANES + HEAD_DIM + term) & (hd < N_HEADS)
    to_k = (col == hd * LANES + HEAD_DIM + 3 + term) & (hd < N_HEADS)
    put = jnp.stack([to_q.astype(BF16), -to_k.astype(BF16)])
    off = lax.broadcasted_iota(jnp.int32, (8, wide), 1) % LANES - HEAD_DIM
    row = lax.broadcasted_iota(jnp.int32, (8, wide), 0)
    q_one = (off >= 3) & (off < 6)
    k_one = ((off >= 0) & (off < 3)) | ((off >= 6) & (off < 9))
    v_one = (off >= 0) & (off < 3)
    ones = jnp.where(row == 0, q_one, jnp.where(row == 1, k_one, (row == 2) & v_one)).astype(F32)
    return pl.pallas_call(
        body,
        out_shape=(jax.ShapeDtypeStruct((S, D_MODEL), BF16), jax.ShapeDtypeStruct((S, wide), BF16),
                   jax.ShapeDtypeStruct((S, wide), BF16), jax.ShapeDtypeStruct((S, wide), BF16),
                   jax.ShapeDtypeStruct((S, 2 * GMLP_WIDTH), F32), jax.ShapeDtypeStruct((S, LANES), F32)),
        grid=(S // tm,),
        in_specs=[pl.BlockSpec((tm, D_MODEL), lambda i: (i, 0)), pl.BlockSpec((1, D_MODEL), lambda i: (0, 0)),
                  pl.BlockSpec((IN_PAD, D_MODEL), lambda i: (0, 0)), pl.BlockSpec((1, LANES), lambda i: (0, 0)),
                  pl.BlockSpec((tm, tm), lambda i: (0, 0)), pl.BlockSpec((2, 3 * LANES, wide), lambda i: (0, 0, 0)),
                  pl.BlockSpec((8, wide), lambda i: (0, 0))],
        out_specs=(pl.BlockSpec((tm, D_MODEL), lambda i: (i, 0)), pl.BlockSpec((tm, wide), lambda i: (i, 0)),
                   pl.BlockSpec((tm, wide), lambda i: (i, 0)), pl.BlockSpec((tm, wide), lambda i: (i, 0)),
                   pl.BlockSpec((tm, 2 * GMLP_WIDTH), lambda i: (i, 0)), pl.BlockSpec((tm, LANES), lambda i: (i, 0))),
        scratch_shapes=[pltpu.VMEM((8, LANES), F32)],
        name="inproj_fwd", compiler_params=_cp(("arbitrary",), VMEM_LIMIT),
    )(x, g_mix, w_pad, bf_pad, tri, put, ones)


def _group_ones():
    r = lax.broadcasted_iota(jnp.int32, (GMLP_WIDTH, GMLP_WIDTH), 0) // GROUP_DIM
    c = lax.broadcasted_iota(jnp.int32, (GMLP_WIDTH, GMLP_WIDTH), 1) // GROUP_DIM
    return (r == c).astype(BF16)


def _gmlp_mixed(vn_bf, w_ref, bias, n_chunks):
    lane = lax.broadcasted_iota(jnp.int32, (CHUNK, LANES), 1)
    row = lax.broadcasted_iota(jnp.int32, (CHUNK, CHUNK), 0)
    col = lax.broadcasted_iota(jnp.int32, (CHUNK, CHUNK), 1)
    ws = [jnp.where(row >= col, w_ref[g], 0.0).astype(BF16) for g in range(N_GROUPS)]
    rows = []
    for ci in range(n_chunks):
        cols = []
        for pp in range(N_GROUPS // 2):
            v = vn_bf[ci * CHUNK:(ci + 1) * CHUNK, pp * LANES:(pp + 1) * LANES]
            v_lo = jnp.where(lane < GROUP_DIM, v, jnp.zeros_like(v))
            v_hi = jnp.where(lane >= GROUP_DIM, v, jnp.zeros_like(v))
            m = (jnp.dot(ws[2 * pp], v_lo, preferred_element_type=F32)
                 + jnp.dot(ws[2 * pp + 1], v_hi, preferred_element_type=F32))
            cols.append(m + bias[:, pp * LANES:(pp + 1) * LANES])
        rows.append(jnp.concatenate(cols, axis=1))
    return jnp.concatenate(rows, axis=0)


def _gmlp_fwd(ug, gain, w_s, bias_full):
    S = ug.shape[0]
    tm = _pick(S, (512, 256, 128))
    ones = _group_ones()

    def body(ug_ref, gain_ref, w_ref, bias_ref, ones_ref, sg_ref):
        u = _gelu(ug_ref[:, :GMLP_WIDTH])
        vr = _gelu(ug_ref[:, GMLP_WIDTH:])
        ms = _dot3(vr * vr, ones_ref[...]) * (1.0 / GROUP_DIM)
        vn = ((vr * lax.rsqrt(ms + EPS)) * gain_ref[...]).astype(BF16)
        mixed = _gmlp_mixed(vn, w_ref, bias_ref[...], tm // CHUNK)
        sg_ref[...] = (u * mixed).astype(BF16)

    return pl.pallas_call(
        body, out_shape=jax.ShapeDtypeStruct((S, GMLP_WIDTH), BF16), grid=(S // tm,),
        in_specs=[pl.BlockSpec((tm, 2 * GMLP_WIDTH), lambda i: (i, 0)), pl.BlockSpec((1, GMLP_WIDTH), lambda i: (0, 0)),
                  pl.BlockSpec((N_GROUPS, CHUNK, CHUNK), lambda i: (0, 0, 0)),
                  pl.BlockSpec((CHUNK, GMLP_WIDTH), lambda i: (0, 0)),
                  pl.BlockSpec((GMLP_WIDTH, GMLP_WIDTH), lambda i: (0, 0))],
        out_specs=pl.BlockSpec((tm, GMLP_WIDTH), lambda i: (i, 0)),
        name="gmlp_fwd", compiler_params=_cp(("parallel",), VMEM_LIMIT),
    )(ug, gain, w_s, bias_full, ones)


_NT = (((1,), (1,)), ((), ()))
_TN = (((0,), (0,)), ((), ()))


def _attn_fwd(qa, ka, va):
    S = qa.shape[0]
    tq = _pick(S, (ATT_TQ, 256))
    tk = min(ATT_TK, tq)
    nq = S // tq
    assert tq == tk, "the diagonal block is handled as one tq x tq tile"
    per_q = 1

    def body(q_ref, k_ref, v_ref, o_ref, lse_ref, ob_ref):
        qi = pl.program_id(1)
        lane = lax.broadcasted_iota(jnp.int32, (tq, LANES), 1)
        qs = [q_ref[:, :LANES], q_ref[:, LANES:]]

        def update(q, ks, k_len, h, m, acc, first_row):
            cols = slice(h * LANES, (h + 1) * LANES)
            s = lax.dot_general(q, k_ref[pl.ds(ks, k_len), cols], _NT, preferred_element_type=F32)
            if first_row is not None:
                rid = lax.broadcasted_iota(jnp.int32, s.shape, 0) + first_row
                s = jnp.where(rid >= lax.broadcasted_iota(jnp.int32, s.shape, 1), s, -jnp.inf)
            m_new = jnp.maximum(m, jnp.max(s, axis=-1, keepdims=True))
            p = jnp.exp(s - m_new).astype(BF16)
            acc = jnp.exp(m - m_new) * acc + jnp.dot(p, v_ref[pl.ds(ks, k_len), cols], preferred_element_type=F32)
            return m_new, acc

        def step(kb, carry):
            ks = pl.multiple_of(kb * tk, tk)
            return tuple(update(qs[h], ks, tk, h, *carry[h], None) for h in range(2))

        def two_steps(t, carry):
            return step(qi % 2 + 2 * t + 1, step(qi % 2 + 2 * t, carry))

        one = (jnp.full((tq, 1), -jnp.inf, F32), jnp.zeros((tq, LANES), F32))
        carry = lax.fori_loop(0, qi % 2, step, (one, one))
        carry = lax.fori_loop(0, qi // 2, two_steps, carry)
        outs, lses = [], []
        strip = tq // 2
        diag = pl.multiple_of(qi * tq, tq)
        for h in range(2):
            ms, accs = [], []
            for r in range(2):
                rows = slice(r * strip, (r + 1) * strip)
                m, acc = update(qs[h][rows], diag, (r + 1) * strip, h, carry[h][0][rows], carry[h][1][rows], r * strip)
                ms.append(m)
                accs.append(acc)
            m, acc = jnp.concatenate(ms, axis=0), jnp.concatenate(accs, axis=0)
            l = acc[:, HEAD_DIM:HEAD_DIM + 1]
            outs.append(acc / l)
            lses.append(m + jnp.log(l))
        o = jnp.where(lane < HEAD_DIM, outs[0], pltpu.roll(outs[1], HEAD_DIM, 1))
        o_ref[...] = o
        ob_ref[...] = o.astype(BF16)
        lse_ref[...] = jnp.where(lane < HEAD_DIM, lses[0], lses[1])

    return pl.pallas_call(
        body,
        out_shape=(jax.ShapeDtypeStruct((S, ATT_WIDTH), F32), jax.ShapeDtypeStruct((S, ATT_WIDTH), F32),
                   jax.ShapeDtypeStruct((S, ATT_WIDTH), BF16)),
        grid=(N_PAIRS, nq),
        in_specs=[pl.BlockSpec((tq, 2 * LANES), lambda p, i: (i, p)),
                  pl.BlockSpec((S, 2 * LANES), lambda p, i: (0, p)),
                  pl.BlockSpec((S, 2 * LANES), lambda p, i: (0, p))],
        out_specs=(pl.BlockSpec((tq, LANES), lambda p, i: (i, p)), pl.BlockSpec((tq, LANES), lambda p, i: (i, p)),
                   pl.BlockSpec((tq, LANES), lambda p, i: (i, p))),
        name="attn_fwd", compiler_params=_cp(("parallel", "parallel"), VMEM_LIMIT),
    )(qa, ka, va)


def _shift_rows(x, prev, n):
    rid = lax.broadcasted_iota(jnp.int32, x.shape, 0)
    y = pltpu.roll(x, n, 0)
    if n == 1:
        return jnp.where(rid == 0, prev[7:8, :], y)
    return jnp.where(rid == 0, prev[6:7, :], jnp.where(rid == 1, prev[7:8, :], y))


def _shift_rows_up(x, nxt, n):
    rows = x.shape[0]
    rid = lax.broadcasted_iota(jnp.int32, x.shape, 0)
    y = pltpu.roll(x, rows - n, 0)
    if n == 1:
        return jnp.where(rid == rows - 1, nxt[0:1, :], y)
    return jnp.where(rid == rows - 2, nxt[0:1, :], jnp.where(rid == rows - 1, nxt[1:2, :], y))


def _conv3(cur, prev, w, b):
    return (w[0:1, :] * _shift_rows(cur, prev, 2) + w[1:2, :] * _shift_rows(cur, prev, 1)
            + w[2:3, :] * cur + b)


def _ffn_up_conv(hn, w_up_bf, cw, cb):
    S = hn.shape[0]
    F = D_FF
    tm = _pick(S, (FFN_TM, 256))
    tn = _pick(F, (FFN_TN, 256, 128))
    nj = F // tn

    def body(hn_ref, wa_ref, wg_ref, cw_ref, cb_ref, hu_ref, hc_ref, act_ref, tail_ref):
        i = pl.program_id(1)

        @pl.when(i == 0)
        def _():
            tail_ref[...] = jnp.zeros_like(tail_ref)

        hn_v = hn_ref[...]
        halves = []
        for h, w_ref in enumerate((wa_ref, wg_ref)):
            hu = lax.dot_general(hn_v, w_ref[...], _NT, preferred_element_type=F32)
            hu_ref[h] = hu.astype(BF16)
            hc = _conv3(hu, tail_ref[h], cw_ref[h], cb_ref[h])
            hc_ref[h] = hc
            halves.append(hc)
            tail_ref[h] = hu[tm - 8:, :]
        a, g = halves
        act_ref[...] = (g * _sigmoid(g) * a).astype(BF16)

    both = pl.BlockSpec((2, tm, tn), lambda j, i: (0, i, j))
    return pl.pallas_call(
        body, out_shape=(jax.ShapeDtypeStruct((2, S, F), BF16), jax.ShapeDtypeStruct((2, S, F), F32),
                         jax.ShapeDtypeStruct((S, F), BF16)),
        grid=(nj, S // tm),
        in_specs=[pl.BlockSpec((tm, D_MODEL), lambda j, i: (i, 0)),
                  pl.BlockSpec((tn, D_MODEL), lambda j, i: (j, 0)),
                  pl.BlockSpec((tn, D_MODEL), lambda j, i: (nj + j, 0)),
                  pl.BlockSpec((2, 8, tn), lambda j, i: (0, 0, j)),
                  pl.BlockSpec((2, 1, tn), lambda j, i: (0, 0, j))],
        out_specs=(both, both, pl.BlockSpec((tm, tn), lambda j, i: (i, j))),
        scratch_shapes=[pltpu.VMEM((2, 8, tn), F32)],
        name="ffn_up_conv", compiler_params=_cp(("parallel", "arbitrary"), VMEM_LIMIT),
    )(hn, w_up_bf, w_up_bf, cw, cb)


def _ffn_down_loss(act, w_down_bf, h1, g_final, target):
    S = h1.shape[0]
    tm = _pick(S, (512, 256))

    def body(a_ref, w_ref, h1_ref, g_ref, t_ref, loss_ref, dh_ref, dhb_ref, dg_ref):
        i = pl.program_id(0)

        @pl.when(i == 0)
        def _():
            loss_ref[...] = jnp.zeros_like(loss_ref)
            dg_ref[...] = jnp.zeros_like(dg_ref)

        hf = h1_ref[...] + jnp.dot(a_ref[...], w_ref[...], preferred_element_type=F32)
        g = g_ref[...]
        r = lax.rsqrt(jnp.mean(hf * hf, axis=-1, keepdims=True) + EPS)
        hhat = hf * r
        err = hhat * g - t_ref[...]
        loss_ref[...] += 0.5 * jnp.sum(jnp.mean(err * err, axis=-1, keepdims=True))
        dy = err * (1.0 / D_MODEL)
        dg_ref[0:1, :] += jnp.sum(dy * hhat, axis=0, keepdims=True)
        dhat = dy * g
        dh = r * (dhat - hhat * jnp.mean(dhat * hhat, axis=-1, keepdims=True))
        dh_ref[...] = dh
        dhb_ref[...] = dh.astype(BF16)

    row = pl.BlockSpec((tm, D_MODEL), lambda i: (i, 0))
    return pl.pallas_call(
        body,
        out_shape=(jax.ShapeDtypeStruct((8, LANES), F32), jax.ShapeDtypeStruct((S, D_MODEL), F32),
                   jax.ShapeDtypeStruct((S, D_MODEL), BF16), jax.ShapeDtypeStruct((8, D_MODEL), F32)),
        grid=(S // tm,),
        in_specs=[pl.BlockSpec((tm, D_FF), lambda i: (i, 0)), pl.BlockSpec((D_FF, D_MODEL), lambda i: (0, 0)), row,
                  pl.BlockSpec((1, D_MODEL), lambda i: (0, 0)), row],
        out_specs=(pl.BlockSpec((8, LANES), lambda i: (0, 0)), row, row, pl.BlockSpec((8, D_MODEL), lambda i: (0, 0))),
        name="ffn_down_loss", compiler_params=_cp(("arbitrary",), VMEM_LIMIT),
    )(act, w_down_bf, h1, g_final, target)


def _ffn_up_dx_rms(dhu, w_up_bf, h1, g_ffn, dh2):
    _, S, F = dhu.shape
    tm = _pick(S, (512, 256))

    def body(a_ref, b_ref, h_ref, g_ref, r_ref, dh_ref, dhb_ref, dg_ref):
        i = pl.program_id(0)

        @pl.when(i == 0)
        def _():
            dg_ref[...] = jnp.zeros_like(dg_ref)

        dyv = (jnp.dot(a_ref[0], b_ref[:F, :], preferred_element_type=F32)
               + jnp.dot(a_ref[1], b_ref[F:, :], preferred_element_type=F32))
        hf = h_ref[...]
        r = lax.rsqrt(jnp.mean(hf * hf, axis=-1, keepdims=True) + EPS)
        hhat = hf * r
        dg_ref[0:1, :] += jnp.sum(dyv * hhat, axis=0, keepdims=True)
        dhat = dyv * g_ref[...]
        dh = r_ref[...] + r * (dhat - hhat * jnp.mean(dhat * hhat, axis=-1, keepdims=True))
        dh_ref[...] = dh
        dhb_ref[...] = dh.astype(BF16)

    row = pl.BlockSpec((tm, D_MODEL), lambda i: (i, 0))
    return pl.pallas_call(
        body,
        out_shape=(jax.ShapeDtypeStruct((S, D_MODEL), F32), jax.ShapeDtypeStruct((S, D_MODEL), BF16),
                   jax.ShapeDtypeStruct((8, D_MODEL), F32)),
        grid=(S // tm,),
        in_specs=[pl.BlockSpec((2, tm, F), lambda i: (0, i, 0)), pl.BlockSpec((2 * F, D_MODEL), lambda i: (0, 0)),
                  row, pl.BlockSpec((1, D_MODEL), lambda i: (0, 0)), row],
        out_specs=(row, row, pl.BlockSpec((8, D_MODEL), lambda i: (0, 0))),
        name="ffn_up_dx_rms", compiler_params=_cp(("arbitrary",), VMEM_LIMIT),
    )(dhu, w_up_bf, h1, g_ffn, dh2)


def _conv_gate_bwd(hc, hu, dact, cw):
    _, S, F = hu.shape
    tm = _pick(S, (CONV_TM, 128))
    tn = _pick(F, (CONV_TN, 256, 128))
    r8 = tm // 8
    n_i = S // tm
    last8 = S // 8 - 1

    def body(hc_ref, hcn_ref, hu_ref, da_ref, dan_ref, w_ref, dhu_ref, dcw_ref):
        i = pl.program_id(1)

        @pl.when(i == 0)
        def _():
            dcw_ref[...] = jnp.zeros_like(dcw_ref)

        rid8 = lax.broadcasted_iota(jnp.int32, (8, tn), 0)

        def gate_grads(a, g, d):
            sg = _sigmoid(g)
            return d * (g * sg), d * a * (sg * (1.0 + g * (1.0 - sg)))

        dhc = gate_grads(hc_ref[0], hc_ref[1], da_ref[...])
        dhc_n = gate_grads(hcn_ref[0], hcn_ref[1], dan_ref[...])
        for h in range(2):
            w = w_ref[h]
            d = dhc[h]
            dn = jnp.where(i < n_i - 1, dhc_n[h], 0.0)
            u1 = _shift_rows_up(d, dn, 1)
            u2 = _shift_rows_up(d, dn, 2)
            dhu_ref[h] = (w[2:3, :] * d + w[1:2, :] * u1 + w[0:1, :] * u2).astype(BF16)
            x = hu_ref[h].astype(F32)
            t0, t1, t2, t3 = [jnp.sum(t, axis=0, keepdims=True) for t in (u2 * x, u1 * x, d * x, d)]
            dcw_ref[h] += jnp.where(rid8 == 0, t0, jnp.where(rid8 == 1, t1, jnp.where(rid8 == 2, t2, jnp.where(rid8 == 3, t3, 0.0))))

    cur = pl.BlockSpec((2, tm, tn), lambda j, i: (0, i, j))
    return pl.pallas_call(
        body,
        out_shape=(jax.ShapeDtypeStruct((2, S, F), BF16), jax.ShapeDtypeStruct((2, 8, F), F32)),
        grid=(F // tn, n_i),
        in_specs=[cur, pl.BlockSpec((2, 8, tn), lambda j, i: (0, jnp.minimum((i + 1) * r8, last8), j)), cur,
                  pl.BlockSpec((tm, tn), lambda j, i: (i, j)),
                  pl.BlockSpec((8, tn), lambda j, i: (jnp.minimum((i + 1) * r8, last8), j)),
                  pl.BlockSpec((2, 8, tn), lambda j, i: (0, 0, j))],
        out_specs=(cur, pl.BlockSpec((2, 8, tn), lambda j, i: (0, 0, j))),
        name="conv_gate_bwd", compiler_params=_cp(("parallel", "arbitrary"), VMEM_LIMIT),
    )(hc, hc, hu, dact, dact, cw)


def _out_proj_dx_prep(dh1_bf, w_out_bf, att, lse, qa):
    S = att.shape[0]
    tm = _pick(S, (256,))

    def body(dh_ref, w_ref, o_ref, lse_ref, q_ref, dsg_ref, qb_ref, doa_ref):
        lane = lax.broadcasted_iota(jnp.int32, (tm, LANES), 1)
        dh = dh_ref[...]
        dsg_ref[...] = lax.dot_general(dh, w_ref[ATT_WIDTH:, :], _NT, preferred_element_type=F32)
        datt = lax.dot_general(dh, w_ref[:ATT_WIDTH, :], _NT, preferred_element_type=F32)
        for p in range(N_PAIRS):
            pc = slice(p * LANES, (p + 1) * LANES)
            do = datt[:, pc]
            prod = o_ref[:, pc] * do
            for hh in range(2):
                sel = (lane >= HEAD_DIM) if hh else (lane < HEAD_DIM)
                delta = jnp.sum(jnp.where(sel, prod, 0.0), axis=-1, keepdims=True)
                dod = pltpu.roll(do, HEAD_DIM, 1) if hh else do
                cols = slice((2 * p + hh) * LANES, (2 * p + hh + 1) * LANES)
                doa_ref[:, cols] = jnp.where(lane < HEAD_DIM, dod, _aug(lane, _split3f(-delta))).astype(BF16)
                lcol = p * LANES + hh * HEAD_DIM
                l3 = _split3f(-lse_ref[:, lcol:lcol + 1])
                augl = jnp.where(lane == HEAD_DIM + 6, l3[0], jnp.where(lane == HEAD_DIM + 7, l3[1], l3[2])).astype(BF16)
                qb_ref[:, cols] = jnp.where((lane >= HEAD_DIM + 6) & (lane < HEAD_DIM + 9), augl, q_ref[:, cols])

    half = pl.BlockSpec((tm, ATT_WIDTH), lambda i: (i, 0))
    wide = pl.BlockSpec((tm, N_HEADS * LANES), lambda i: (i, 0))
    return pl.pallas_call(
        body,
        out_shape=(jax.ShapeDtypeStruct((S, GMLP_WIDTH), F32), jax.ShapeDtypeStruct(qa.shape, BF16),
                   jax.ShapeDtypeStruct(qa.shape, BF16)),
        grid=(S // tm,),
        in_specs=[pl.BlockSpec((tm, D_MODEL), lambda i: (i, 0)), pl.BlockSpec((D_MODEL, D_MODEL), lambda i: (0, 0)),
                  half, half, wide],
        out_specs=(half, wide, wide),
        name="out_proj_dx_prep", compiler_params=_cp(("parallel",), VMEM_LIMIT),
    )(dh1_bf, w_out_bf, att, lse, qa)


def _attn_bwd(qb, ka, va, doa):
    S = qb.shape[0]
    tk = _pick(S, (512, 256))
    tq = tk
    nq = S // tq

    def pair(a, scale=None):
        lane = lax.broadcasted_iota(jnp.int32, (a.shape[0], LANES), 1)
        out = jnp.where(lane < HEAD_DIM, a[:, :LANES], pltpu.roll(a[:, LANES:], HEAD_DIM, 1))
        return out if scale is None else out * scale

    def head_lanes(a, col, sign, first):
        lane = lax.broadcasted_iota(jnp.int32, (a.shape[0], LANES), 1)
        return jnp.where(lane == first, sign * a[:, col:col + 1],
                         jnp.where(lane == first + 1, sign * a[:, LANES + col:LANES + col + 1], 0.0))

    def body(q_ref, do_ref, k_ref, v_ref, dqc_ref, dkc_ref, dvc_ref, dcq_ref, dck_ref, dq_ref, dka_ref, dva_ref):
        kb = pl.program_id(1)

        @pl.when(kb == 0)
        def _():
            dq_ref[...] = jnp.zeros_like(dq_ref)

        dka_ref[...] = jnp.zeros_like(dka_ref)
        dva_ref[...] = jnp.zeros_like(dva_ref)
        def sub_tile(qs, q_len, k_off, k_len, masked):
            keys = slice(k_off, k_off + k_len)
            for h in range(2):
                cols = slice(h * LANES, (h + 1) * LANES)
                qblk = q_ref[pl.ds(qs, q_len), cols]
                doblk = do_ref[pl.ds(qs, q_len), cols]
                kh = k_ref[keys, cols]
                p = jnp.exp(lax.dot_general(kh, qblk, _NT, preferred_element_type=F32))
                if masked:
                    p = jnp.where(lax.broadcasted_iota(jnp.int32, p.shape, 1) >= lax.broadcasted_iota(jnp.int32, p.shape, 0),
                                  p, 0.0)
                ds = (p * lax.dot_general(v_ref[keys, cols], doblk, _NT, preferred_element_type=F32)).astype(BF16)
                dva_ref[keys, cols] += jnp.dot(p.astype(BF16), doblk, preferred_element_type=F32)
                dka_ref[keys, cols] += jnp.dot(ds, qblk, preferred_element_type=F32)
                dq_ref[pl.ds(qs, q_len), cols] += lax.dot_general(ds, kh, _TN, preferred_element_type=F32)

        half = tk // 2
        sub_tile(pl.multiple_of(kb * tq, tq), tq, 0, half, True)
        sub_tile(pl.multiple_of(kb * tq + half, half), half, half, half, True)

        rest = nq - 1 - kb
        odd = rest % 2

        @pl.when(odd == 1)
        def _():
            sub_tile(pl.multiple_of((kb + 1) * tq, tq), tq, 0, tk, False)

        def step(t, carry):
            sub_tile(pl.multiple_of((kb + 1 + odd + 2 * t) * tq, tq), 2 * tq, 0, tk, False)
            return carry

        lax.fori_loop(0, rest // 2, step, 0)
        dka = dka_ref[...]
        dkc_ref[...] = pair(dka).astype(BF16)
        dvc_ref[...] = pair(dva_ref[...]).astype(BF16)
        first = 2 * pl.program_id(0)
        dck_ref[...] = head_lanes(dka, HEAD_DIM + 3, -1.0, first)

        @pl.when(kb == nq - 1)
        def _():
            dqa = dq_ref[...]
            dqc_ref[...] = pair(dqa, HEAD_DIM ** -0.5).astype(BF16)
            dcq_ref[...] = head_lanes(dqa, HEAD_DIM, 1.0, first)

    wide = 2 * LANES
    half = jax.ShapeDtypeStruct((S, ATT_WIDTH), BF16)
    slabs = jax.ShapeDtypeStruct((N_PAIRS, S, LANES), F32)
    return pl.pallas_call(
        body,
        out_shape=(half, half, half, slabs, slabs),
        grid=(N_PAIRS, nq),
        in_specs=[pl.BlockSpec((S, wide), lambda p, j: (0, p)), pl.BlockSpec((S, wide), lambda p, j: (0, p)),
                  pl.BlockSpec((tk, wide), lambda p, j: (j, p)), pl.BlockSpec((tk, wide), lambda p, j: (j, p))],
        out_specs=(pl.BlockSpec((S, LANES), lambda p, j: (0, p)), pl.BlockSpec((tk, LANES), lambda p, j: (j, p)),
                   pl.BlockSpec((tk, LANES), lambda p, j: (j, p)), pl.BlockSpec((None, S, LANES), lambda p, j: (p, 0, 0)),
                   pl.BlockSpec((None, tk, LANES), lambda p, j: (p, j, 0))),
        scratch_shapes=[pltpu.VMEM((S, wide), F32), pltpu.VMEM((tk, wide), F32), pltpu.VMEM((tk, wide), F32)],
        name="attn_bwd", compiler_params=_cp(("parallel", "arbitrary"), VMEM_LIMIT),
    )(qb, doa, ka, va)


def _gmlp_bwd(ug, dsg, gain, w_s, wt_s, bias_full):
    S = ug.shape[0]
    tm = _pick(S, (512, 256, 128))
    n_chunks = tm // CHUNK
    n_i = S // tm
    ones = _group_ones()
    nt = (((1,), (1,)), ((), ()))

    def body(ug_ref, dsg_ref, gain_ref, w_ref, wt_ref, bias_ref, ones_ref, dug_ref, dw_ref, dgain_ref, dbias_ref,
             dbacc_ref):
        i = pl.program_id(0)

        @pl.when(i == 0)
        def _():
            dw_ref[...] = jnp.zeros_like(dw_ref)
            dgain_ref[...] = jnp.zeros_like(dgain_ref)
            dbacc_ref[...] = jnp.zeros_like(dbacc_ref)

        ones_m = ones_ref[...]
        pu = ug_ref[:, :GMLP_WIDTH]
        pg = ug_ref[:, GMLP_WIDTH:]
        u = _gelu(pu)
        vr = _gelu(pg)
        ms = _dot3(vr * vr, ones_m) * (1.0 / GROUP_DIM)
        rinv = lax.rsqrt(ms + EPS)
        vhat = vr * rinv
        gain_v = gain_ref[...]
        vn = (vhat * gain_v).astype(BF16)
        mixed = _gmlp_mixed(vn, w_ref, bias_ref[...], n_chunks)
        dsg_v = dsg_ref[...]
        du = dsg_v * mixed
        dmixed = dsg_v * u
        dm_bf = dmixed.astype(BF16)
        lane = lax.broadcasted_iota(jnp.int32, (CHUNK, LANES), 1)
        row = lax.broadcasted_iota(jnp.int32, (CHUNK, CHUNK), 0)
        col = lax.broadcasted_iota(jnp.int32, (CHUNK, CHUNK), 1)
        wts = [jnp.where(col >= row, wt_ref[g], 0.0).astype(BF16) for g in range(N_GROUPS)]
        dvn_rows = []
        dbsum = jnp.zeros((CHUNK, GMLP_WIDTH), F32)
        for ci in range(n_chunks):
            rs = slice(ci * CHUNK, (ci + 1) * CHUNK)
            dbsum = dbsum + dmixed[rs, :]
            cols = []
            for pp in range(N_GROUPS // 2):
                cs = slice(pp * LANES, (pp + 1) * LANES)
                dm = dm_bf[rs, cs]
                dm_lo = jnp.where(lane < GROUP_DIM, dm, jnp.zeros_like(dm))
                dm_hi = jnp.where(lane >= GROUP_DIM, dm, jnp.zeros_like(dm))
                vb = vn[rs, cs]
                dw_ref[2 * pp] += lax.dot_general(dm_lo, vb, nt, preferred_element_type=F32)
                dw_ref[2 * pp + 1] += lax.dot_general(dm_hi, vb, nt, preferred_element_type=F32)
                cols.append(jnp.dot(wts[2 * pp], dm_lo, preferred_element_type=F32)
                            + jnp.dot(wts[2 * pp + 1], dm_hi, preferred_element_type=F32))
            dvn_rows.append(jnp.concatenate(cols, axis=1))
        dvn = jnp.concatenate(dvn_rows, axis=0)
        dbacc_ref[...] += dbsum
        dgain_ref[0:1, :] += jnp.sum(dvn * vhat, axis=0, keepdims=True)
        dvhat = dvn * gain_v
        gm = _dot3(dvhat * vhat, ones_m) * (1.0 / GROUP_DIM)
        dvr = rinv * (dvhat - vhat * gm)
        dug_ref[:, :GMLP_WIDTH] = (du * _gelu_grad(pu)).astype(BF16)
        dug_ref[:, GMLP_WIDTH:] = (dvr * _gelu_grad(pg)).astype(BF16)

        @pl.when(i == n_i - 1)
        def _():
            for g in range(N_GROUPS):
                dw_ref[g] = jnp.where(row >= col, dw_ref[g], 0.0)
            dbias_ref[...] = _dot3(dbacc_ref[...], ones_m)

    return pl.pallas_call(
        body,
        out_shape=(jax.ShapeDtypeStruct((S, 2 * GMLP_WIDTH), BF16), jax.ShapeDtypeStruct((N_GROUPS, CHUNK, CHUNK), F32),
                   jax.ShapeDtypeStruct((8, GMLP_WIDTH), F32), jax.ShapeDtypeStruct((CHUNK, GMLP_WIDTH), F32)),
        grid=(n_i,),
        in_specs=[pl.BlockSpec((tm, 2 * GMLP_WIDTH), lambda i: (i, 0)), pl.BlockSpec((tm, GMLP_WIDTH), lambda i: (i, 0)),
                  pl.BlockSpec((1, GMLP_WIDTH), lambda i: (0, 0)),
                  pl.BlockSpec((N_GROUPS, CHUNK, CHUNK), lambda i: (0, 0, 0)),
                  pl.BlockSpec((N_GROUPS, CHUNK, CHUNK), lambda i: (0, 0, 0)),
                  pl.BlockSpec((CHUNK, GMLP_WIDTH), lambda i: (0, 0)),
                  pl.BlockSpec((GMLP_WIDTH, GMLP_WIDTH), lambda i: (0, 0))],
        out_specs=(pl.BlockSpec((tm, 2 * GMLP_WIDTH), lambda i: (i, 0)),
                   pl.BlockSpec((N_GROUPS, CHUNK, CHUNK), lambda i: (0, 0, 0)),
                   pl.BlockSpec((8, GMLP_WIDTH), lambda i: (0, 0)),
                   pl.BlockSpec((CHUNK, GMLP_WIDTH), lambda i: (0, 0))),
        scratch_shapes=[pltpu.VMEM((CHUNK, GMLP_WIDTH), F32)],
        name="gmlp_bwd", compiler_params=_cp(("arbitrary",), VMEM_LIMIT),
    )(ug, dsg, gain, w_s, wt_s, bias_full, ones)


def _gate_bwd(dcq, dck, zf):
    S = zf.shape[0]
    tm = _pick(S, (256,))
    n_i = S // tm
    triu = (lax.broadcasted_iota(jnp.int32, (tm, tm), 0) <= lax.broadcasted_iota(jnp.int32, (tm, tm), 1)).astype(BF16)

    def body(dcq_ref, dck_ref, zf_ref, tri_ref, dzf_ref, dbf_ref, carry_ref):
        i = pl.program_id(0)

        @pl.when(i == 0)
        def _():
            carry_ref[...] = jnp.zeros_like(carry_ref)
            dbf_ref[...] = jnp.zeros_like(dbf_ref)

        lane = lax.broadcasted_iota(jnp.int32, (tm, LANES), 1)
        dc = dcq_ref[0] + dck_ref[0]
        for p in range(1, N_PAIRS):
            dc = dc + (dcq_ref[p] + dck_ref[p])
        dlf = _dot3l(tri_ref[...], dc) + carry_ref[0:1, :]
        carry_ref[0:1, :] = dlf[0:1, :]
        dz = jnp.where(lane < N_HEADS, dlf * _sigmoid(-zf_ref[...]), 0.0)
        dzf_ref[...] = dz.astype(BF16)
        dbf_ref[0:1, :] += jnp.sum(dz, axis=0, keepdims=True)

    return pl.pallas_call(
        body,
        out_shape=(jax.ShapeDtypeStruct((S, LANES), BF16), jax.ShapeDtypeStruct((8, LANES), F32)),
        grid=(n_i,),
        in_specs=[pl.BlockSpec((N_PAIRS, tm, LANES), lambda i: (0, n_i - 1 - i, 0)),
                  pl.BlockSpec((N_PAIRS, tm, LANES), lambda i: (0, n_i - 1 - i, 0)),
                  pl.BlockSpec((tm, LANES), lambda i: (n_i - 1 - i, 0)),
                  pl.BlockSpec((tm, tm), lambda i: (0, 0))],
        out_specs=(pl.BlockSpec((tm, LANES), lambda i: (n_i - 1 - i, 0)), pl.BlockSpec((8, LANES), lambda i: (0, 0))),
        scratch_shapes=[pltpu.VMEM((8, LANES), F32)],
        name="gate_bwd", compiler_params=_cp(("arbitrary",), VMEM_LIMIT),
    )(dcq, dck, zf, triu)


def _out_proj_fwd(att_bf, sg, w_out_bf, x, g_ffn):
    S = x.shape[0]
    tm = _pick(S, (512, 256))

    def body(a_ref, s_ref, w_ref, x_ref, g_ref, h_ref, hn_ref):
        h = (x_ref[...] + jnp.dot(a_ref[...], w_ref[:ATT_WIDTH, :], preferred_element_type=F32)
             + jnp.dot(s_ref[...], w_ref[ATT_WIDTH:, :], preferred_element_type=F32))
        h_ref[...] = h
        r = lax.rsqrt(jnp.mean(h * h, axis=-1, keepdims=True) + EPS)
        hn_ref[...] = ((h * r) * g_ref[...]).astype(BF16)

    row = pl.BlockSpec((tm, D_MODEL), lambda i: (i, 0))
    half = pl.BlockSpec((tm, ATT_WIDTH), lambda i: (i, 0))
    return pl.pallas_call(
        body, out_shape=(jax.ShapeDtypeStruct((S, D_MODEL), F32), jax.ShapeDtypeStruct((S, D_MODEL), BF16)),
        grid=(S // tm,),
        in_specs=[half, half, pl.BlockSpec((D_MODEL, D_MODEL), lambda i: (0, 0)), row,
                  pl.BlockSpec((1, D_MODEL), lambda i: (0, 0))],
        out_specs=(row, row), name="out_proj", compiler_params=_cp(("parallel",), VMEM_LIMIT),
    )(att_bf, sg, w_out_bf, x, g_ffn)


def _out_proj_dw(att_bf, sg, dh1_bf):
    S = att_bf.shape[0]
    tk = _pick(S, (1024, 512))

    def body(a_ref, s_ref, d_ref, o_ref):
        k = pl.program_id(0)

        @pl.when(k == 0)
        def _():
            o_ref[...] = jnp.zeros_like(o_ref)

        d = d_ref[...]
        o_ref[:ATT_WIDTH, :] += lax.dot_general(a_ref[...], d, _TN, preferred_element_type=F32)
        o_ref[ATT_WIDTH:, :] += lax.dot_general(s_ref[...], d, _TN, preferred_element_type=F32)

    half = pl.BlockSpec((tk, ATT_WIDTH), lambda k: (k, 0))
    return pl.pallas_call(
        body, out_shape=jax.ShapeDtypeStruct((D_MODEL, D_MODEL), F32), grid=(S // tk,),
        in_specs=[half, half, pl.BlockSpec((tk, D_MODEL), lambda k: (k, 0))],
        out_specs=pl.BlockSpec((D_MODEL, D_MODEL), lambda k: (0, 0)),
        name="out_proj_dw", compiler_params=_cp(("arbitrary",), VMEM_LIMIT),
    )(att_bf, sg, dh1_bf)


_IN_PIECES = ((0, ATT_WIDTH), (ATT_WIDTH, ATT_WIDTH), (2 * ATT_WIDTH, ATT_WIDTH), (QKV, 2 * GMLP_WIDTH), (UG_END, LANES))


def _inproj_bwd_dx(pieces, w_pad, x, g_mix, dh1):
    S = x.shape[0]
    tm = _pick(S, (512, 256))

    def body(*refs):
        p_refs, (w_ref, x_ref, g_ref, r_ref, dx_ref, dg_ref) = refs[:5], refs[5:]
        i = pl.program_id(0)

        @pl.when(i == 0)
        def _():
            dg_ref[...] = jnp.zeros_like(dg_ref)

        dxn = None
        for p_ref, (c0, width) in zip(p_refs, _IN_PIECES):
            part = jnp.dot(p_ref[...], w_ref[c0:c0 + width, :], preferred_element_type=F32)
            dxn = part if dxn is None else dxn + part
        xf = x_ref[...]
        r = lax.rsqrt(jnp.mean(xf * xf, axis=-1, keepdims=True) + EPS)
        xhat = xf * r
        dg_ref[0:1, :] += jnp.sum(dxn * xhat, axis=0, keepdims=True)
        dhat = dxn * g_ref[...]
        dx_ref[...] = r_ref[...] + r * (dhat - xhat * jnp.mean(dhat * xhat, axis=-1, keepdims=True))

    row = pl.BlockSpec((tm, D_MODEL), lambda i: (i, 0))
    return pl.pallas_call(
        body, out_shape=(jax.ShapeDtypeStruct((S, D_MODEL), F32), jax.ShapeDtypeStruct((8, D_MODEL), F32)),
        grid=(S // tm,),
        in_specs=[pl.BlockSpec((tm, width), lambda i: (i, 0)) for _, width in _IN_PIECES]
        + [pl.BlockSpec((IN_PAD, D_MODEL), lambda i: (0, 0)), row, pl.BlockSpec((1, D_MODEL), lambda i: (0, 0)), row],
        out_specs=(row, pl.BlockSpec((8, D_MODEL), lambda i: (0, 0))),
        name="in_proj_dx", compiler_params=_cp(("arbitrary",), VMEM_LIMIT),
    )(*pieces, w_pad, x, g_mix, dh1)


def _inproj_bwd_dw(xn, pieces):
    S = xn.shape[0]
    tk = _pick(S, (1024, 512))

    def body(*refs):
        x_ref, p_refs, o_ref = refs[0], refs[1:6], refs[6]
        k = pl.program_id(0)

        @pl.when(k == 0)
        def _():
            o_ref[...] = jnp.zeros_like(o_ref)

        xb = x_ref[...]
        for p_ref, (c0, width) in zip(p_refs, _IN_PIECES):
            o_ref[:, c0:c0 + width] += lax.dot_general(xb, p_ref[...], _TN, preferred_element_type=F32)

    return pl.pallas_call(
        body, out_shape=jax.ShapeDtypeStruct((D_MODEL, IN_PAD), F32), grid=(S // tk,),
        in_specs=[pl.BlockSpec((tk, D_MODEL), lambda k: (k, 0))]
        + [pl.BlockSpec((tk, width), lambda k: (k, 0)) for _, width in _IN_PIECES],
        out_specs=pl.BlockSpec((D_MODEL, IN_PAD), lambda k: (0, 0)),
        name="in_proj_dw", compiler_params=_cp(("arbitrary",), VMEM_LIMIT),
    )(xn, *pieces)


def _adamw(w, m, v, parts, name):
    R, C = w.shape[-2:]
    tr = R
    for cand in (256, 128, 64, 32, 16, 8):
        if R % cand == 0 and R > cand:
            tr = cand
            break
    c1 = 1.0 / (1.0 - ADAM_B1 ** ADAM_STEP)
    c2 = 1.0 / (1.0 - ADAM_B2 ** ADAM_STEP)

    def body(w_ref, m_ref, v_ref, p_ref, g_ref, d_ref, nm_ref, nv_ref):
        g = p_ref[0].astype(F32)
        for j in range(1, N_DEV):
            g = g + p_ref[j].astype(F32)
        g_ref[...] = g
        nm = ADAM_B1 * m_ref[...] + (1.0 - ADAM_B1) * g
        nv = ADAM_B2 * v_ref[...] + (1.0 - ADAM_B2) * (g * g)
        nm_ref[...] = nm
        nv_ref[...] = nv
        d_ref[...] = -ADAM_LR * ((nm * c1) / (jnp.sqrt(nv * c2) + ADAM_EPS) + ADAM_WD * w_ref[...])

    if w.ndim == 3:
        spec = pl.BlockSpec((None, tr, C), lambda i: (0, i, 0))
    else:
        spec = pl.BlockSpec((tr, C), lambda i: (i, 0))
    shp = jax.ShapeDtypeStruct(w.shape, F32)
    return pl.pallas_call(
        body, out_shape=(shp, shp, shp, shp), grid=(R // tr,),
        in_specs=[spec, spec, spec, pl.BlockSpec((N_DEV, tr, C), lambda i: (0, i, 0))],
        out_specs=(spec, spec, spec, spec),
        name=name, compiler_params=_cp(("parallel",), VMEM_LIMIT),
    )(w, m, v, parts)


def _adamw_owner(w, m, v, landed, sent, me, name):
    R, C = w.shape[-2:]
    tr = R
    for cand in (256, 128, 64, 32, 16, 8):
        if R % cand == 0 and R > cand:
            tr = cand
            break
    c1 = 1.0 / (1.0 - ADAM_B1 ** ADAM_STEP)
    c2 = 1.0 / (1.0 - ADAM_B2 ** ADAM_STEP)

    def body(me_ref, w_ref, m_ref, v_ref, p_ref, own_ref, g_ref, d_ref, nm_ref, nv_ref):
        mine = me_ref[0]
        own = own_ref[...].astype(F32)
        g = jnp.where(mine == 0, own, p_ref[0].astype(F32))
        for j in range(1, N_DEV):
            g = g + jnp.where(mine == j, own, p_ref[j].astype(F32))
        g_ref[...] = g
        nm = ADAM_B1 * m_ref[...] + (1.0 - ADAM_B1) * g
        nv = ADAM_B2 * v_ref[...] + (1.0 - ADAM_B2) * (g * g)
        nm_ref[...] = nm
        nv_ref[...] = nv
        d_ref[...] = -ADAM_LR * ((nm * c1) / (jnp.sqrt(nv * c2) + ADAM_EPS) + ADAM_WD * w_ref[...])

    spec = pl.BlockSpec((None, tr, C), lambda i, me_ref: (0, i, 0))
    shp = jax.ShapeDtypeStruct(w.shape, F32)
    return pl.pallas_call(
        body, out_shape=(shp, shp, shp, shp),
        grid_spec=pltpu.PrefetchScalarGridSpec(
            num_scalar_prefetch=1, grid=(R // tr,),
            in_specs=[spec, spec, spec, pl.BlockSpec((N_DEV, tr, C), lambda i, me_ref: (0, i, 0)),
                      pl.BlockSpec((None, tr, C), lambda i, me_ref: (me_ref[0], i, 0))],
            out_specs=(spec, spec, spec, spec)),
        name=name, compiler_params=_cp(("parallel",), VMEM_LIMIT),
    )(jnp.reshape(me, (1,)).astype(jnp.int32), w, m, v, landed, sent)


def _adamw_packed(w, m, v, parts, sizes, name):
    R = w.shape[0]
    assert R == sum(sizes)
    c1 = 1.0 / (1.0 - ADAM_B1 ** ADAM_STEP)
    c2 = 1.0 / (1.0 - ADAM_B2 ** ADAM_STEP)
    n = len(sizes)

    def body(w_ref, m_ref, v_ref, p_ref, *out_refs):
        g = p_ref[0]
        for j in range(1, N_DEV):
            g = g + p_ref[j]
        nm = ADAM_B1 * m_ref[...] + (1.0 - ADAM_B1) * g
        nv = ADAM_B2 * v_ref[...] + (1.0 - ADAM_B2) * (g * g)
        d = -ADAM_LR * ((nm * c1) / (jnp.sqrt(nv * c2) + ADAM_EPS) + ADAM_WD * w_ref[...])
        for kind, val in enumerate((g, d, nm, nv)):
            off = 0
            for k, rows in enumerate(sizes):
                out_refs[kind * n + k][...] = val[off:off + rows, :]
                off += rows

    whole = pl.BlockSpec((R, LANES), lambda i: (0, 0))
    shapes = [jax.ShapeDtypeStruct((rows, LANES), F32) for rows in sizes] * 4
    res = pl.pallas_call(
        body, out_shape=tuple(shapes), grid=(1,),
        in_specs=[whole, whole, whole, pl.BlockSpec((N_DEV, R, LANES), lambda i: (0, 0, 0))],
        out_specs=tuple(pl.BlockSpec((rows, LANES), lambda i: (0, 0)) for rows in sizes) * 4,
        name=name, compiler_params=_cp(("arbitrary",), VMEM_LIMIT),
    )(w, m, v, parts)
    return [list(res[kind * n:(kind + 1) * n]) for kind in range(4)]


def _place():
    x, y, c = lax.axis_index("x"), lax.axis_index("y"), lax.axis_index("c")
    return x, y, c


def _all_gather(blocks, name):
    n = len(blocks)

    def body(*refs):
        ins, outs = refs[:n], refs[n:2 * n]
        send_sems, recv_sems, local_sems = refs[2 * n:]
        x, y, c = _place()
        me, sibling = (x, y, c), (x, y, 1 - c)
        chips = [(1 - x, y), (x, 1 - y), (1 - x, 1 - y)]
        sends = []
        for a in range(n):
            out = outs[a]

            def slot(px, py, pc, out=out):
                return out.at[4 * px + 2 * py + pc]

            def copy(k, block, to, src=None, a=a, slot=slot):
                return pltpu.make_async_remote_copy(
                    src_ref=slot(*block) if src is None else src, dst_ref=slot(*block),
                    send_sem=send_sems.at[a, k], recv_sem=recv_sems.at[a, k], device_id=to, device_id_type=MESH)

            mine = pltpu.make_async_copy(ins[a], slot(*me), local_sems.at[a])
            mine.start()
            first = [copy(0, me, sibling, src=ins[a])]
            first += [copy(1 + j, me, (*chip, c), src=ins[a]) for j, chip in enumerate(chips)]
            for cp in first:
                cp.start()
            sends.append((mine, first, copy))
        for a in range(n):
            mine, first, copy = sends[a]
            passed = [copy(4 + j, (*chip, c), sibling) for j, chip in enumerate(chips)]
            for j, chip in enumerate(chips):
                copy(1 + j, (*chip, c), me).wait_recv()
                passed[j].start()
            copy(0, sibling, me).wait_recv()
            for j, chip in enumerate(chips):
                copy(4 + j, (*chip, 1 - c), me).wait_recv()
            for cp in first + passed:
                cp.wait_send()
            mine.wait()

    any_spec = pl.BlockSpec(memory_space=pl.ANY)
    return pl.pallas_call(
        body, out_shape=tuple(jax.ShapeDtypeStruct((N_DEV,) + b.shape, b.dtype) for b in blocks),
        in_specs=[any_spec] * n, out_specs=tuple([any_spec] * n),
        scratch_shapes=[pltpu.SemaphoreType.DMA((n, 7)), pltpu.SemaphoreType.DMA((n, 7)), pltpu.SemaphoreType.DMA((n,))],
        name=name,
    )(*blocks)


_HBM = pl.BlockSpec(memory_space=pltpu.HBM)
_SEM = pl.BlockSpec(memory_space=pltpu.SEMAPHORE)
_EFFECT = pltpu.SideEffectType.DATAFLOW_SIDE_EFFECTING


def _peers(x, y, c):
    out = []
    for k in range(1, N_DEV):
        px, py, pc = x ^ ((k >> 2) & 1), y ^ ((k >> 1) & 1), c ^ (k & 1)
        out.append((k, (px, py, pc), 4 * px + 2 * py + pc))
    return out


def _xchg_copies(src_refs, land_refs, send_sems, recv_sems, scatter):
    x, y, c = _place()
    me = 4 * x + 2 * y + c
    copies = []
    for a, (src, land) in enumerate(zip(src_refs, land_refs)):
        for k, place, idx in _peers(x, y, c):
            j = a * (N_DEV - 1) + k - 1
            copies.append(pltpu.make_async_remote_copy(
                src_ref=src.at[idx] if scatter[a] else src, dst_ref=land.at[me],
                send_sem=send_sems[j], recv_sem=recv_sems[j], device_id=place, device_id_type=MESH))
    return copies


def _xchg_start(srcs, scatter, name):
    n = len(srcs)
    lands = [lax.empty((N_DEV,) + (s.shape[1:] if sc else s.shape), s.dtype) for s, sc in zip(srcs, scatter)]

    ns = n * (N_DEV - 1)

    def body(*refs):
        sems = refs[2 * n:2 * n + 2 * ns]
        for cp in _xchg_copies(refs[:n], refs[n:2 * n], sems[:ns], sems[ns:], scatter):
            cp.start()
        token = refs[-1]
        token[...] = jnp.zeros_like(token)

    both = list(srcs) + lands
    res = pl.pallas_call(
        body, name=name,
        out_shape=(*[pltpu.SemaphoreType.DMA(())] * (2 * ns),
                   *[pltpu.HBM(a.shape, a.dtype) for a in both], jax.ShapeDtypeStruct((8, LANES), F32)),
        in_specs=[_HBM] * (2 * n),
        out_specs=(*([_SEM] * (2 * ns)), *([_HBM] * (2 * n)), pl.BlockSpec(memory_space=pltpu.VMEM)),
        input_output_aliases={i: 2 * ns + i for i in range(2 * n)},
        compiler_params=pltpu.CompilerParams(has_side_effects=_EFFECT),
    )(*[pltpu.with_memory_space_constraint(a, pltpu.HBM) for a in both])
    return (tuple(res[:2 * ns]), tuple(res[2 * ns:2 * ns + 2 * n])), res[-1]


def _xchg_wait(handle, scatter, after, name):
    sems, thru = handle
    n = len(thru) // 2
    ns = len(sems) // 2

    def body(*refs):
        got = refs[2 * n:2 * n + 2 * ns]
        for cp in _xchg_copies(refs[:n], refs[n:2 * n], got[:ns], got[ns:], scatter):
            cp.wait_send()
            cp.wait_recv()

    outs = pl.pallas_call(
        body, name=name, out_shape=tuple(pltpu.HBM(a.shape, a.dtype) for a in thru),
        in_specs=[_HBM] * (2 * n) + [_SEM] * (2 * ns) + [pl.BlockSpec(memory_space=pl.ANY)],
        out_specs=tuple([_HBM] * (2 * n)), input_output_aliases={i: i for i in range(2 * n)},
        compiler_params=pltpu.CompilerParams(has_side_effects=_EFFECT),
    )(*thru, *sems, after)
    return outs[:n], outs[n:]


def _tie(a, token):
    return a if token is None else a + token[0, 0].astype(a.dtype)


def _rows128(a):
    flat = a.reshape(-1)
    rows = -(-flat.shape[0] // LANES)
    rows = -(-rows // 8) * 8
    return jnp.pad(flat, (0, rows * LANES - flat.shape[0])).reshape(rows, LANES)


def _local_step(x, target, norm_mix_g, w_in_t, b_forget, gmlp_norm_g, w_spatial, b_spatial, norm_ffn_g, conv_b,
                norm_final_g, rest_fn, send_fn, small_fn, token=None):
    f = D_FF
    g_mix = norm_mix_g.reshape(1, D_MODEL)
    w_pad = jnp.pad(w_in_t, ((0, IN_PAD - IN_COLS), (0, 0)))
    bf_pad = jnp.pad(b_forget.reshape(1, N_HEADS), ((0, 0), (0, LANES - N_HEADS)))
    xn, qa, ka, va, ug, zf = _inproj_fwd(x, _tie(g_mix, token), w_pad, bf_pad)
    bias_full = jnp.repeat(b_spatial.reshape(N_GROUPS, CHUNK).T, GROUP_DIM, axis=1)
    w_s = w_spatial.reshape(N_GROUPS, CHUNK, CHUNK)
    gain = gmlp_norm_g.reshape(1, GMLP_WIDTH)
    sg = _gmlp_fwd(ug, gain, w_s, bias_full)
    att, lse, att_bf = _attn_fwd(qa, ka, va)
    w_out_bf, w_up_bf, conv_w, w_down_bf = rest_fn(att_bf)
    g_ffn = norm_ffn_g.reshape(1, D_MODEL)
    h1, hn = _out_proj_fwd(att_bf, sg, w_out_bf, x, g_ffn)
    cw = jnp.pad(conv_w.reshape(3, 2, f).transpose(1, 0, 2), ((0, 0), (0, 5), (0, 0)))
    cb = conv_b.reshape(2, 1, f)
    hu, hc, act = _ffn_up_conv(hn, w_up_bf, cw, cb)
    loss_blk, dh2, dh2_bf, dg_final = _ffn_down_loss(act, w_down_bf, h1, norm_final_g.reshape(1, D_MODEL), target)
    dw_down = _mm(act, dh2_bf, mode="tn", out_dtype=F32, tm=1408, tn=1024, tk=2048, name="ffn_down_dw")
    dact = _mm(dh2_bf, w_down_bf, mode="nt", out_dtype=F32, tm=1024, tn=1408, tk=1024, outer="j", name="ffn_down_dx")
    dhu, dcw = _conv_gate_bwd(hc, hu, dact, _tie(cw, send_fn("w_down", dw_down)))
    dw_up = _mm(hn, dhu, mode="tn", out_dtype=F32, tm=1024, tn=1408, tk=2048, b_halves=True, outer="j", name="ffn_up_dw")
    dh1, dh1_bf, dg_ffn = _ffn_up_dx_rms(dhu, w_up_bf, h1, _tie(g_ffn, send_fn("w_up", dw_up)), dh2)
    dsg, qb, doa = _out_proj_dx_prep(dh1_bf, w_out_bf, att, lse, qa)
    dw_out = _out_proj_dw(att_bf, sg, dh1_bf)
    dq, dk, dv, dcq, dck = _attn_bwd(qb, ka, va, doa)
    wt_s = w_s.transpose(0, 2, 1)
    dug, dw_s, dgain, dbias = _gmlp_bwd(ug, dsg, _tie(gain, send_fn("w_out", dw_out)), w_s, wt_s, bias_full)
    dzf, dbf = _gate_bwd(dcq, dck, zf)
    grad_x, dg_mix = _inproj_bwd_dx((dq, dk, dv, dug, dzf), w_pad, x, g_mix, dh1)
    grads = dict(
        norm_mix_g=dg_mix[0:1, :],
        b_forget=dbf[0:1, :N_HEADS],
        gmlp_norm_g=dgain[0:1, :],
        w_spatial=dw_s,
        b_spatial=dbias[:, ::GROUP_DIM].T,
        norm_ffn_g=dg_ffn[0:1, :],
        conv_w=dcw[:, 0:3, :].transpose(1, 0, 2).reshape(3, 2 * f),
        conv_b=dcw[:, 3, :].reshape(1, 2 * f),
        norm_final_g=dg_final[0, :],
    )
    token = small_fn(loss_blk[0, 0], grads)
    dw_in = _inproj_bwd_dw(xn, (dq, dk, dv, dug, _tie(dzf, token)))
    return grad_x, send_fn("w_in", dw_in[:, :IN_COLS])


SMALL = ("norm_mix_g", "b_forget", "gmlp_norm_g", "w_spatial", "b_spatial", "norm_ffn_g", "conv_b", "norm_final_g")


def kernel(x, norm_mix_g, w_in, b_forget, gmlp_norm_g, w_spatial, b_spatial, w_out, norm_ffn_g, w_up, conv_w, conv_b, w_down, norm_final_g, loss_target, m_norm_mix_g, m_w_in, m_b_forget, m_gmlp_norm_g, m_w_spatial, m_b_spatial, m_w_out, m_norm_ffn_g, m_w_up, m_conv_w, m_conv_b, m_w_down, m_norm_final_g, v_norm_mix_g, v_w_in, v_b_forget, v_gmlp_norm_g, v_w_spatial, v_b_spatial, v_w_out, v_norm_ffn_g, v_w_up, v_conv_w, v_conv_b, v_w_down, v_norm_final_g):
    weights = dict(norm_mix_g=norm_mix_g, w_in=w_in, b_forget=b_forget, gmlp_norm_g=gmlp_norm_g, w_spatial=w_spatial,
                   b_spatial=b_spatial, w_out=w_out, norm_ffn_g=norm_ffn_g, w_up=w_up, conv_w=conv_w, conv_b=conv_b,
                   w_down=w_down, norm_final_g=norm_final_g)
    m_in = dict(norm_mix_g=m_norm_mix_g, w_in=m_w_in, b_forget=m_b_forget, gmlp_norm_g=m_gmlp_norm_g,
                w_spatial=m_w_spatial, b_spatial=m_b_spatial, w_out=m_w_out, norm_ffn_g=m_norm_ffn_g, w_up=m_w_up,
                conv_w=m_conv_w, conv_b=m_conv_b, w_down=m_w_down, norm_final_g=m_norm_final_g)
    v_in = dict(norm_mix_g=v_norm_mix_g, w_in=v_w_in, b_forget=v_b_forget, gmlp_norm_g=v_gmlp_norm_g,
                w_spatial=v_w_spatial, b_spatial=v_b_spatial, w_out=v_w_out, norm_ffn_g=v_norm_ffn_g, w_up=v_w_up,
                conv_w=v_conv_w, conv_b=v_conv_b, w_down=v_w_down, norm_final_g=v_norm_final_g)
    order = list(weights)
    me = 4 * lax.axis_index("x") + 2 * lax.axis_index("y") + lax.axis_index("c")
    n_in, n_up = w_in.shape[2], w_up.shape[2]
    r_out, r_down = w_out.shape[1], w_down.shape[1]

    def with_mine(landed, mine):
        return lax.dynamic_update_index_in_dim(landed, mine, me, 0)

    up_blk = w_up[0].T.astype(BF16)
    out_blk = w_out[0].astype(BF16)
    down_blk = w_down[0].astype(BF16)
    taps_blk = jnp.pad(conv_w[0], ((0, 5), (0, 0)))
    (in_all,) = _all_gather([w_in[0].T.astype(BF16)], "gather_w_in")
    in_all, rest_blocks = lax.optimization_barrier((in_all, [up_blk, out_blk, down_blk, taps_blk]))
    rest_handle, token = _xchg_start(rest_blocks, [False] * 4, "gather_rest_start")
    w_in_t = in_all.reshape(N_DEV * n_in, D_MODEL)

    def rest_fn(after):
        mine, landed = _xchg_wait(rest_handle, [False] * 4, after, "gather_rest_wait")
        up_all, out_all, down_all, taps_all = [with_mine(l, b) for l, b in zip(landed, mine)]
        return (out_all.reshape(N_DEV * r_out, D_MODEL), up_all.reshape(N_DEV * n_up, D_MODEL),
                taps_all[:, :3, :].transpose(1, 0, 2).reshape(3, N_DEV * n_up),
                down_all.reshape(N_DEV * r_down, D_MODEL))

    sent = {}

    def send_fn(name, grad):
        if name == "w_in":
            parts = grad.reshape(D_MODEL, N_DEV, -1).transpose(1, 0, 2).astype(BF16)
        elif name == "w_up":
            parts = grad.reshape(D_MODEL, N_DEV, -1).transpose(1, 0, 2)
        else:
            parts = grad.reshape(N_DEV, -1, D_MODEL)
        sent[name], tok = _xchg_start([parts], [True], "scatter_" + name + "_start")
        return tok

    small = {}

    def small_fn(loss_local, g):
        loss_rows = jnp.pad(loss_local.reshape(1, 1), ((0, 31), (0, LANES - 1)))
        packed = [_rows128(g[k]) for k in SMALL] + [loss_rows, _rows128(g["conv_w"])]
        small["sizes"] = [p.shape[0] for p in packed]
        small["handle"], tok = _xchg_start([jnp.concatenate(packed, axis=0)], [False], "gather_small_start")
        return tok

    grad_x, after = _local_step(
        x[0], loss_target[0], norm_mix_g, w_in_t, b_forget, gmlp_norm_g, w_spatial, b_spatial, norm_ffn_g, conv_b,
        norm_final_g, rest_fn, send_fn, small_fn, token)

    outs = {}

    def update_big(name, after):
        (parts,), (landed,) = _xchg_wait(sent[name], [True], after, "scatter_" + name + "_wait")
        outs[name] = tuple(_adamw_owner(weights[name], m_in[name], v_in[name], landed, parts, me, "adamw_" + name))
        return outs[name][0]

    for name in ("w_down", "w_up", "w_out"):
        after = update_big(name, after)

    (mine,), (landed,) = _xchg_wait(small["handle"], [False], after, "gather_small_wait")
    small_all = with_mine(landed, mine)
    sizes = small["sizes"]
    n_small_rows = sum(sizes[:-2])

    def pack(src):
        return jnp.concatenate([_rows128(src[k]) for k in SMALL] + [jnp.zeros((sizes[-2], LANES), F32)], axis=0)

    n_adam_rows = n_small_rows + sizes[-2]
    per_kind = _adamw_packed(pack(weights), pack(m_in), pack(v_in), small_all, sizes[:-1], "adamw_small")
    loss = per_kind[0][-1][0, 0]
    for j, k in enumerate(SMALL):
        shp = weights[k].shape
        cnt = math.prod(shp)
        outs[k] = tuple(kind[j].reshape(-1)[:cnt].reshape(shp) for kind in per_kind)
    sg_ = per_kind[0][0]
    taps_parts = small_all[:, n_adam_rows:, :].reshape(N_DEV, -1)[:, :3 * N_DEV * n_up].reshape(N_DEV, 3, N_DEV * n_up)
    taps_mine = lax.dynamic_slice_in_dim(taps_parts, me * n_up, n_up, axis=2)
    taps_mine = jnp.pad(taps_mine, ((0, 0), (0, 5), (0, 0)))

    def pad8(a):
        return jnp.pad(a[0], ((0, 5), (0, 0)))

    res = _adamw(pad8(conv_w), pad8(m_conv_w), pad8(v_conv_w), taps_mine, "adamw_conv_w")
    outs["conv_w"] = tuple(a[:3][None] for a in res)
    update_big("w_in", sg_)

    return (loss, grad_x[None], *[outs[k][0] for k in order], *[outs[k][1] for k in order],
            *[outs[k][2] for k in order], *[outs[k][3] for k in order])
```

```python
import functools
import math

import jax
import jax.numpy as jnp
from jax import lax
from jax.experimental import pallas as pl
from jax.experimental.pallas import tpu as pltpu

F32 = jnp.float32
BF16 = jnp.bfloat16

N_DEV = 8
D_MODEL = 1024
ATT_WIDTH = 512
GMLP_WIDTH = 512
HEAD_DIM = 64
N_HEADS = 8
N_PAIRS = 4
N_GROUPS = 8
GROUP_DIM = 64
CHUNK = 128
D_FF = 2816
IN_COLS = 2568
IN_PAD = 2688
QKV = 1536
UG_END = 2560
EPS = 1e-6
LANES = 128

ADAM_LR = 0.001
ADAM_B1 = 0.9
ADAM_B2 = 0.999
ADAM_EPS = 1e-08
ADAM_WD = 0.01
ADAM_STEP = 10

ATT_TQ = 1024
ATT_TK = 1024
FFN_TM, FFN_TN = 512, 1408
CONV_TM, CONV_TN = 512, 1408
VMEM_LIMIT = 56 * 1024 * 1024
MESH = pl.DeviceIdType.MESH


def _cp(sem, vmem=None):
    return pltpu.CompilerParams(dimension_semantics=sem, vmem_limit_bytes=vmem)


def _pick(n, prefs):
    for p in prefs:
        if n % p == 0:
            return p
    return n


def _split3(x):
    hi = x.astype(BF16)
    r1 = x - hi.astype(F32)
    mid = r1.astype(BF16)
    lo = (r1 - mid.astype(F32)).astype(BF16)
    return hi, mid, lo


def _dot3(x, ones_bf):
    d = functools.partial(jnp.dot, preferred_element_type=F32)
    out = []
    for c in range(0, x.shape[1], 2 * LANES):
        blk = ones_bf[c:c + 2 * LANES, c:c + 2 * LANES]
        hi, mid, lo = _split3(x[:, c:c + 2 * LANES])
        out.append(d(hi, blk) + d(mid, blk) + d(lo, blk))
    return jnp.concatenate(out, axis=1)


def _dot3l(ones_bf, x):
    n = x.shape[1]
    y = jnp.dot(ones_bf, jnp.concatenate(_split3(x), axis=1), preferred_element_type=F32)
    return y[:, :n] + y[:, n:2 * n] + y[:, 2 * n:]


def _gelu(x):
    k = math.sqrt(2.0 / math.pi)
    t = jnp.tanh(k * (x + 0.044715 * (x * x * x)))
    return 0.5 * x * (1.0 + t)


def _gelu_grad(x):
    k = math.sqrt(2.0 / math.pi)
    x2 = x * x
    t = jnp.tanh(k * (x + 0.044715 * (x2 * x)))
    return 0.5 * (1.0 + t) + 0.5 * x * (1.0 - t * t) * (k * (1.0 + 3.0 * 0.044715 * x2))


def _sigmoid(x):
    return 1.0 / (1.0 + jnp.exp(-x))


def _mm(a, b, *, mode, out_dtype, tm, tn, tk, name, res=None, a_halves=False, b_halves=False,
        out_halves=False, outer="i"):
    if mode == "tn":
        K, M = a.shape[-2], a.shape[-1] * (2 if a_halves else 1)
    else:
        M, K = a.shape[-2], a.shape[-1] * (2 if a_halves else 1)
    if mode == "nt":
        N = b.shape[-2]
        assert b.shape[-1] == K
    else:
        N = b.shape[-1] * (2 if b_halves else 1)
    tm, tn, tk = min(tm, M), min(tn, N), min(tk, K)
    assert M % tm == 0 and N % tn == 0 and K % tk == 0, (name, M, N, K, tm, tn, tk)
    nm, nn, nk = M // tm, N // tn, K // tk

    def ij(g0, g1):
        return (g0, g1) if outer == "i" else (g1, g0)

    if mode == "nn":
        dims = (((1,), (0,)), ((), ()))
        if a_halves:
            nkh = nk // 2
            a_spec = pl.BlockSpec((None, tm, tk), lambda g0, g1, k: (k // nkh, ij(g0, g1)[0], k % nkh))
        else:
            a_spec = pl.BlockSpec((tm, tk), lambda g0, g1, k: (ij(g0, g1)[0], k))
        b_spec = pl.BlockSpec((tk, tn), lambda g0, g1, k: (k, ij(g0, g1)[1]))
    elif mode == "nt":
        dims = (((1,), (1,)), ((), ()))
        if a_halves:
            nkh = nk // 2
            a_spec = pl.BlockSpec((None, tm, tk), lambda g0, g1, k: (k // nkh, ij(g0, g1)[0], k % nkh))
        else:
            a_spec = pl.BlockSpec((tm, tk), lambda g0, g1, k: (ij(g0, g1)[0], k))
        b_spec = pl.BlockSpec((tn, tk), lambda g0, g1, k: (ij(g0, g1)[1], k))
    else:
        dims = (((0,), (0,)), ((), ()))
        if a_halves:
            nmh = nm // 2
            a_spec = pl.BlockSpec((None, tk, tm), lambda g0, g1, k: (ij(g0, g1)[0] // nmh, k, ij(g0, g1)[0] % nmh))
        else:
            a_spec = pl.BlockSpec((tk, tm), lambda g0, g1, k: (k, ij(g0, g1)[0]))
        if b_halves:
            nnh = nn // 2
            b_spec = pl.BlockSpec((None, tk, tn), lambda g0, g1, k: (ij(g0, g1)[1] // nnh, k, ij(g0, g1)[1] % nnh))
        else:
            b_spec = pl.BlockSpec((tk, tn), lambda g0, g1, k: (k, ij(g0, g1)[1]))
    if out_halves:
        nnh = nn // 2
        o_spec = pl.BlockSpec((None, tm, tn), lambda g0, g1, k: (ij(g0, g1)[1] // nnh, ij(g0, g1)[0], ij(g0, g1)[1] % nnh))
        o_shape = jax.ShapeDtypeStruct((2, M, N // 2), out_dtype)
    else:
        o_spec = pl.BlockSpec((tm, tn), lambda g0, g1, k: ij(g0, g1))
        o_shape = jax.ShapeDtypeStruct((M, N), out_dtype)
    in_specs = [a_spec, b_spec]
    args = [a, b]
    if res is not None:
        in_specs.append(pl.BlockSpec((tm, tn), lambda g0, g1, k: ij(g0, g1)))
        args.append(res)

    def body(*refs):
        if res is not None:
            a_ref, b_ref, r_ref, o_ref = refs[:4]
        else:
            a_ref, b_ref, o_ref = refs[:3]
            r_ref = None
        part = lax.dot_general(a_ref[...], b_ref[...], dims, preferred_element_type=F32)
        if nk == 1:
            if r_ref is not None:
                part = part + r_ref[...]
            o_ref[...] = part.astype(out_dtype)
            return
        acc_ref = refs[-1]
        k = pl.program_id(2)

        @pl.when(k == 0)
        def _():
            acc_ref[...] = part

        @pl.when(k > 0)
        def _():
            acc_ref[...] += part

        @pl.when(k == nk - 1)
        def _():
            tot = acc_ref[...]
            if r_ref is not None:
                tot = tot + r_ref[...]
            o_ref[...] = tot.astype(out_dtype)

    grid = (nm, nn, nk) if outer == "i" else (nn, nm, nk)
    scratch = [] if nk == 1 else [pltpu.VMEM((tm, tn), F32)]
    return pl.pallas_call(
        body, out_shape=o_shape, grid=grid, in_specs=in_specs, out_specs=o_spec, scratch_shapes=scratch,
        name=name, compiler_params=_cp(("parallel", "parallel", "arbitrary"), VMEM_LIMIT),
    )(*args)


def _aug(lane, terms):
    out = 0.0
    for j, t in enumerate(terms):
        out = jnp.where(lane == HEAD_DIM + j, t, out)
    return out


def _split3f(x):
    hi, mid, lo = _split3(x)
    return [hi.astype(F32), mid.astype(F32), lo.astype(F32)]


def _inproj_fwd(x, g_mix, w_pad, bf_pad):
    S = x.shape[0]
    tm = _pick(S, (512, 256))
    tri = (lax.broadcasted_iota(jnp.int32, (tm, tm), 0) >= lax.broadcasted_iota(jnp.int32, (tm, tm), 1)).astype(BF16)

    def body(x_ref, g_ref, w_ref, bf_ref, tri_ref, put_ref, one_ref, xn_ref, qa_ref, ka_ref, va_ref, ug_ref, zf_ref,
             carry_ref):
        i = pl.program_id(0)

        @pl.when(i == 0)
        def _():
            carry_ref[...] = jnp.zeros_like(carry_ref)

        xf = x_ref[...]
        r = lax.rsqrt(jnp.mean(xf * xf, axis=-1, keepdims=True) + EPS)
        xn = ((xf * r) * g_ref[...]).astype(BF16)
        xn_ref[...] = xn
        proj = lax.dot_general(xn, w_ref[...], _NT, preferred_element_type=F32)
        ug_ref[...] = proj[:, QKV:UG_END]
        zf = proj[:, UG_END:] + bf_ref[...]
        zf_ref[...] = zf
        lf = jnp.minimum(zf, 0.0) - jnp.log(1.0 + jnp.exp(-jnp.abs(zf)))
        c = _dot3l(tri_ref[...], lf) + carry_ref[0:1, :]
        carry_ref[0:1, :] = c[tm - 1:tm, :]
        c3 = jnp.concatenate(_split3(c), axis=1)
        aug_q = jnp.dot(c3, put_ref[0], preferred_element_type=F32) + one_ref[0:1, :]
        aug_k = jnp.dot(c3, put_ref[1], preferred_element_type=F32) + one_ref[1:2, :]
        lane = lax.broadcasted_iota(jnp.int32, (tm, LANES), 1)
        for h in range(N_HEADS):
            p, odd = h // 2, h % 2

            def head(base, scale=None, p=p, odd=odd):
                blk = proj[:, base + p * LANES:base + (p + 1) * LANES]
                if scale is not None:
                    blk = blk * scale
                return pltpu.roll(blk, HEAD_DIM, 1) if odd else blk

            cols = slice(h * LANES, (h + 1) * LANES)
            qa_ref[:, cols] = jnp.where(lane < HEAD_DIM, head(0, HEAD_DIM ** -0.5), aug_q[:, cols]).astype(BF16)
            ka_ref[:, cols] = jnp.where(lane < HEAD_DIM, head(ATT_WIDTH), aug_k[:, cols]).astype(BF16)
            va_ref[:, cols] = jnp.where(lane < HEAD_DIM, head(2 * ATT_WIDTH), one_ref[2:3, cols]).astype(BF16)

    wide = N_HEADS * LANES
    src = lax.broadcasted_iota(jnp.int32, (3 * LANES, wide), 0)
    col = lax.broadcasted_iota(jnp.int32, (3 * LANES, wide), 1)
    hd, term = src % LANES, src // LANES
    to_q = (col == hd * LANES + HEAD_DIM + term) & (hd < N_HEADS)
    to_k = (col == hd * LANES + HEAD_DIM + 3 + term) & (hd < N_HEADS)
    put = jnp.stack([to_q.astype(BF16), -to_k.astype(BF16)])
    off = lax.broadcasted_iota(jnp.int32, (8, wide), 1) % LANES - HEAD_DIM
    row = lax.broadcasted_iota(jnp.int32, (8, wide), 0)
    q_one = (off >= 3) & (off < 6)
    k_one = ((off >= 0) & (off < 3)) | ((off >= 6) & (off < 9))
    v_one = (off >= 0) & (off < 3)
    ones = jnp.where(row == 0, q_one, jnp.where(row == 1, k_one, (row == 2) & v_one)).astype(F32)
    return pl.pallas_call(
        body,
        out_shape=(jax.ShapeDtypeStruct((S, D_MODEL), BF16), jax.ShapeDtypeStruct((S, wide), BF16),
                   jax.ShapeDtypeStruct((S, wide), BF16), jax.ShapeDtypeStruct((S, wide), BF16),
                   jax.ShapeDtypeStruct((S, 2 * GMLP_WIDTH), F32), jax.ShapeDtypeStruct((S, LANES), F32)),
        grid=(S // tm,),
        in_specs=[pl.BlockSpec((tm, D_MODEL), lambda i: (i, 0)), pl.BlockSpec((1, D_MODEL), lambda i: (0, 0)),
                  pl.BlockSpec((IN_PAD, D_MODEL), lambda i: (0, 0)), pl.BlockSpec((1, LANES), lambda i: (0, 0)),
                  pl.BlockSpec((tm, tm), lambda i: (0, 0)), pl.BlockSpec((2, 3 * LANES, wide), lambda i: (0, 0, 0)),
                  pl.BlockSpec((8, wide), lambda i: (0, 0))],
        out_specs=(pl.BlockSpec((tm, D_MODEL), lambda i: (i, 0)), pl.BlockSpec((tm, wide), lambda i: (i, 0)),
                   pl.BlockSpec((tm, wide), lambda i: (i, 0)), pl.BlockSpec((tm, wide), lambda i: (i, 0)),
                   pl.BlockSpec((tm, 2 * GMLP_WIDTH), lambda i: (i, 0)), pl.BlockSpec((tm, LANES), lambda i: (i, 0))),
        scratch_shapes=[pltpu.VMEM((8, LANES), F32)],
        name="inproj_fwd", compiler_params=_cp(("arbitrary",), VMEM_LIMIT),
    )(x, g_mix, w_pad, bf_pad, tri, put, ones)


def _group_ones():
    r = lax.broadcasted_iota(jnp.int32, (GMLP_WIDTH, GMLP_WIDTH), 0) // GROUP_DIM
    c = lax.broadcasted_iota(jnp.int32, (GMLP_WIDTH, GMLP_WIDTH), 1) // GROUP_DIM
    return (r == c).astype(BF16)


def _gmlp_mixed(vn_bf, w_ref, bias, n_chunks):
    lane = lax.broadcasted_iota(jnp.int32, (CHUNK, LANES), 1)
    row = lax.broadcasted_iota(jnp.int32, (CHUNK, CHUNK), 0)
    col = lax.broadcasted_iota(jnp.int32, (CHUNK, CHUNK), 1)
    ws = [jnp.where(row >= col, w_ref[g], 0.0).astype(BF16) for g in range(N_GROUPS)]
    rows = []
    for ci in range(n_chunks):
        cols = []
        for pp in range(N_GROUPS // 2):
            v = vn_bf[ci * CHUNK:(ci + 1) * CHUNK, pp * LANES:(pp + 1) * LANES]
            v_lo = jnp.where(lane < GROUP_DIM, v, jnp.zeros_like(v))
            v_hi = jnp.where(lane >= GROUP_DIM, v, jnp.zeros_like(v))
            m = (jnp.dot(ws[2 * pp], v_lo, preferred_element_type=F32)
                 + jnp.dot(ws[2 * pp + 1], v_hi, preferred_element_type=F32))
            cols.append(m + bias[:, pp * LANES:(pp + 1) * LANES])
        rows.append(jnp.concatenate(cols, axis=1))
    return jnp.concatenate(rows, axis=0)


def _gmlp_fwd(ug, gain, w_s, bias_full):
    S = ug.shape[0]
    tm = _pick(S, (512, 256, 128))
    ones = _group_ones()

    def body(ug_ref, gain_ref, w_ref, bias_ref, ones_ref, sg_ref):
        u = _gelu(ug_ref[:, :GMLP_WIDTH])
        vr = _gelu(ug_ref[:, GMLP_WIDTH:])
        ms = _dot3(vr * vr, ones_ref[...]) * (1.0 / GROUP_DIM)
        vn = ((vr * lax.rsqrt(ms + EPS)) * gain_ref[...]).astype(BF16)
        mixed = _gmlp_mixed(vn, w_ref, bias_ref[...], tm // CHUNK)
        sg_ref[...] = (u * mixed).astype(BF16)

    return pl.pallas_call(
        body, out_shape=jax.ShapeDtypeStruct((S, GMLP_WIDTH), BF16), grid=(S // tm,),
        in_specs=[pl.BlockSpec((tm, 2 * GMLP_WIDTH), lambda i: (i, 0)), pl.BlockSpec((1, GMLP_WIDTH), lambda i: (0, 0)),
                  pl.BlockSpec((N_GROUPS, CHUNK, CHUNK), lambda i: (0, 0, 0)),
                  pl.BlockSpec((CHUNK, GMLP_WIDTH), lambda i: (0, 0)),
                  pl.BlockSpec((GMLP_WIDTH, GMLP_WIDTH), lambda i: (0, 0))],
        out_specs=pl.BlockSpec((tm, GMLP_WIDTH), lambda i: (i, 0)),
        name="gmlp_fwd", compiler_params=_cp(("parallel",), VMEM_LIMIT),
    )(ug, gain, w_s, bias_full, ones)


_NT = (((1,), (1,)), ((), ()))
_TN = (((0,), (0,)), ((), ()))


def _attn_fwd(qa, ka, va):
    S = qa.shape[0]
    tq = _pick(S, (ATT_TQ, 256))
    tk = min(ATT_TK, tq)
    nq = S // tq
    assert tq == tk, "the diagonal block is handled as one tq x tq tile"
    per_q = 1

    def body(q_ref, k_ref, v_ref, o_ref, lse_ref, ob_ref):
        qi = pl.program_id(1)
        lane = lax.broadcasted_iota(jnp.int32, (tq, LANES), 1)
        qs = [q_ref[:, :LANES], q_ref[:, LANES:]]

        def update(q, ks, k_len, h, m, acc, first_row):
            cols = slice(h * LANES, (h + 1) * LANES)
            s = lax.dot_general(q, k_ref[pl.ds(ks, k_len), cols], _NT, preferred_element_type=F32)
            if first_row is not None:
                rid = lax.broadcasted_iota(jnp.int32, s.shape, 0) + first_row
                s = jnp.where(rid >= lax.broadcasted_iota(jnp.int32, s.shape, 1), s, -jnp.inf)
            m_new = jnp.maximum(m, jnp.max(s, axis=-1, keepdims=True))
            p = jnp.exp(s - m_new).astype(BF16)
            acc = jnp.exp(m - m_new) * acc + jnp.dot(p, v_ref[pl.ds(ks, k_len), cols], preferred_element_type=F32)
            return m_new, acc

        def step(kb, carry):
            ks = pl.multiple_of(kb * tk, tk)
            return tuple(update(qs[h], ks, tk, h, *carry[h], None) for h in range(2))

        def steps(n, first):
            def body(t, carry):
                for u in range(n):
                    carry = step(first + n * t + u, carry)
                return carry
            return body

        one = (jnp.full((tq, 1), -jnp.inf, F32), jnp.zeros((tq, LANES), F32))
        carry = lax.fori_loop(0, qi % 2, step, (one, one))
        carry = lax.fori_loop(0, (qi % 4) // 2, steps(2, qi % 2), carry)
        carry = lax.fori_loop(0, qi // 4, steps(4, qi % 4), carry)
        outs, lses = [], []
        strip = tq // 2
        diag = pl.multiple_of(qi * tq, tq)
        for h in range(2):
            ms, accs = [], []
            for r in range(2):
                rows = slice(r * strip, (r + 1) * strip)
                m, acc = update(qs[h][rows], diag, (r + 1) * strip, h, carry[h][0][rows], carry[h][1][rows], r * strip)
                ms.append(m)
                accs.append(acc)
            m, acc = jnp.concatenate(ms, axis=0), jnp.concatenate(accs, axis=0)
            l = acc[:, HEAD_DIM:HEAD_DIM + 1]
            outs.append(acc / l)
            lses.append(m + jnp.log(l))
        o = jnp.where(lane < HEAD_DIM, outs[0], pltpu.roll(outs[1], HEAD_DIM, 1))
        o_ref[...] = o
        ob_ref[...] = o.astype(BF16)
        lse_ref[...] = jnp.where(lane < HEAD_DIM, lses[0], lses[1])

    return pl.pallas_call(
        body,
        out_shape=(jax.ShapeDtypeStruct((S, ATT_WIDTH), F32), jax.ShapeDtypeStruct((S, ATT_WIDTH), F32),
                   jax.ShapeDtypeStruct((S, ATT_WIDTH), BF16)),
        grid=(N_PAIRS, nq),
        in_specs=[pl.BlockSpec((tq, 2 * LANES), lambda p, i: (i, p)),
                  pl.BlockSpec((S, 2 * LANES), lambda p, i: (0, p)),
                  pl.BlockSpec((S, 2 * LANES), lambda p, i: (0, p))],
        out_specs=(pl.BlockSpec((tq, LANES), lambda p, i: (i, p)), pl.BlockSpec((tq, LANES), lambda p, i: (i, p)),
                   pl.BlockSpec((tq, LANES), lambda p, i: (i, p))),
        name="attn_fwd", compiler_params=_cp(("parallel", "parallel"), VMEM_LIMIT),
    )(qa, ka, va)


def _shift_rows(x, prev, n):
    rid = lax.broadcasted_iota(jnp.int32, x.shape, 0)
    y = pltpu.roll(x, n, 0)
    if n == 1:
        return jnp.where(rid == 0, prev[7:8, :], y)
    return jnp.where(rid == 0, prev[6:7, :], jnp.where(rid == 1, prev[7:8, :], y))


def _shift_rows_up(x, nxt, n):
    rows = x.shape[0]
    rid = lax.broadcasted_iota(jnp.int32, x.shape, 0)
    y = pltpu.roll(x, rows - n, 0)
    if n == 1:
        return jnp.where(rid == rows - 1, nxt[0:1, :], y)
    return jnp.where(rid == rows - 2, nxt[0:1, :], jnp.where(rid == rows - 1, nxt[1:2, :], y))


def _conv3(cur, prev, w, b):
    return (w[0:1, :] * _shift_rows(cur, prev, 2) + w[1:2, :] * _shift_rows(cur, prev, 1)
            + w[2:3, :] * cur + b)


def _ffn_up_conv(hn, w_up_bf, cw, cb):
    S = hn.shape[0]
    F = D_FF
    tm = _pick(S, (FFN_TM, 256))
    tn = _pick(F, (FFN_TN, 256, 128))
    nj = F // tn

    def body(hn_ref, wa_ref, wg_ref, cw_ref, cb_ref, hu_ref, hc_ref, act_ref, tail_ref):
        i = pl.program_id(1)

        @pl.when(i == 0)
        def _():
            tail_ref[...] = jnp.zeros_like(tail_ref)

        hn_v = hn_ref[...]
        halves = []
        for h, w_ref in enumerate((wa_ref, wg_ref)):
            hu = lax.dot_general(hn_v, w_ref[...], _NT, preferred_element_type=F32)
            hu_ref[h] = hu.astype(BF16)
            hc = _conv3(hu, tail_ref[h], cw_ref[h], cb_ref[h])
            hc_ref[h] = hc
            halves.append(hc)
            tail_ref[h] = hu[tm - 8:, :]
        a, g = halves
        act_ref[...] = (g * _sigmoid(g) * a).astype(BF16)

    both = pl.BlockSpec((2, tm, tn), lambda j, i: (0, i, j))
    return pl.pallas_call(
        body, out_shape=(jax.ShapeDtypeStruct((2, S, F), BF16), jax.ShapeDtypeStruct((2, S, F), F32),
                         jax.ShapeDtypeStruct((S, F), BF16)),
        grid=(nj, S // tm),
        in_specs=[pl.BlockSpec((tm, D_MODEL), lambda j, i: (i, 0)),
                  pl.BlockSpec((tn, D_MODEL), lambda j, i: (j, 0)),
                  pl.BlockSpec((tn, D_MODEL), lambda j, i: (nj + j, 0)),
                  pl.BlockSpec((2, 8, tn), lambda j, i: (0, 0, j)),
                  pl.BlockSpec((2, 1, tn), lambda j, i: (0, 0, j))],
        out_specs=(both, both, pl.BlockSpec((tm, tn), lambda j, i: (i, j))),
        scratch_shapes=[pltpu.VMEM((2, 8, tn), F32)],
        name="ffn_up_conv", compiler_params=_cp(("parallel", "arbitrary"), VMEM_LIMIT),
    )(hn, w_up_bf, w_up_bf, cw, cb)


def _ffn_down_loss(act, w_down_bf, h1, g_final, target):
    S = h1.shape[0]
    tm = _pick(S, (512, 256))

    def body(a_ref, w_ref, h1_ref, g_ref, t_ref, loss_ref, dh_ref, dhb_ref, dg_ref):
        i = pl.program_id(0)

        @pl.when(i == 0)
        def _():
            loss_ref[...] = jnp.zeros_like(loss_ref)
            dg_ref[...] = jnp.zeros_like(dg_ref)

        hf = h1_ref[...] + jnp.dot(a_ref[...], w_ref[...], preferred_element_type=F32)
        g = g_ref[...]
        r = lax.rsqrt(jnp.mean(hf * hf, axis=-1, keepdims=True) + EPS)
        hhat = hf * r
        err = hhat * g - t_ref[...]
        loss_ref[...] += 0.5 * jnp.sum(jnp.mean(err * err, axis=-1, keepdims=True))
        dy = err * (1.0 / D_MODEL)
        dg_ref[0:1, :] += jnp.sum(dy * hhat, axis=0, keepdims=True)
        dhat = dy * g
        dh = r * (dhat - hhat * jnp.mean(dhat * hhat, axis=-1, keepdims=True))
        dh_ref[...] = dh
        dhb_ref[...] = dh.astype(BF16)

    row = pl.BlockSpec((tm, D_MODEL), lambda i: (i, 0))
    return pl.pallas_call(
        body,
        out_shape=(jax.ShapeDtypeStruct((8, LANES), F32), jax.ShapeDtypeStruct((S, D_MODEL), F32),
                   jax.ShapeDtypeStruct((S, D_MODEL), BF16), jax.ShapeDtypeStruct((8, D_MODEL), F32)),
        grid=(S // tm,),
        in_specs=[pl.BlockSpec((tm, D_FF), lambda i: (i, 0)), pl.BlockSpec((D_FF, D_MODEL), lambda i: (0, 0)), row,
                  pl.BlockSpec((1, D_MODEL), lambda i: (0, 0)), row],
        out_specs=(pl.BlockSpec((8, LANES), lambda i: (0, 0)), row, row, pl.BlockSpec((8, D_MODEL), lambda i: (0, 0))),
        name="ffn_down_loss", compiler_params=_cp(("arbitrary",), VMEM_LIMIT),
    )(act, w_down_bf, h1, g_final, target)


def _ffn_up_dx_rms(dhu, w_up_bf, h1, g_ffn, dh2):
    _, S, F = dhu.shape
    tm = _pick(S, (512, 256))

    def body(a_ref, b_ref, h_ref, g_ref, r_ref, dh_ref, dhb_ref, dg_ref):
        i = pl.program_id(0)

        @pl.when(i == 0)
        def _():
            dg_ref[...] = jnp.zeros_like(dg_ref)

        dyv = (jnp.dot(a_ref[0], b_ref[:F, :], preferred_element_type=F32)
               + jnp.dot(a_ref[1], b_ref[F:, :], preferred_element_type=F32))
        hf = h_ref[...]
        r = lax.rsqrt(jnp.mean(hf * hf, axis=-1, keepdims=True) + EPS)
        hhat = hf * r
        dg_ref[0:1, :] += jnp.sum(dyv * hhat, axis=0, keepdims=True)
        dhat = dyv * g_ref[...]
        dh = r_ref[...] + r * (dhat - hhat * jnp.mean(dhat * hhat, axis=-1, keepdims=True))
        dh_ref[...] = dh
        dhb_ref[...] = dh.astype(BF16)

    row = pl.BlockSpec((tm, D_MODEL), lambda i: (i, 0))
    return pl.pallas_call(
        body,
        out_shape=(jax.ShapeDtypeStruct((S, D_MODEL), F32), jax.ShapeDtypeStruct((S, D_MODEL), BF16),
                   jax.ShapeDtypeStruct((8, D_MODEL), F32)),
        grid=(S // tm,),
        in_specs=[pl.BlockSpec((2, tm, F), lambda i: (0, i, 0)), pl.BlockSpec((2 * F, D_MODEL), lambda i: (0, 0)),
                  row, pl.BlockSpec((1, D_MODEL), lambda i: (0, 0)), row],
        out_specs=(row, row, pl.BlockSpec((8, D_MODEL), lambda i: (0, 0))),
        name="ffn_up_dx_rms", compiler_params=_cp(("arbitrary",), VMEM_LIMIT),
    )(dhu, w_up_bf, h1, g_ffn, dh2)


def _conv_gate_bwd(hc, hu, dact, cw):
    _, S, F = hu.shape
    tm = _pick(S, (CONV_TM, 128))
    tn = _pick(F, (CONV_TN, 256, 128))
    r8 = tm // 8
    n_i = S // tm
    last8 = S // 8 - 1

    def body(hc_ref, hcn_ref, hu_ref, da_ref, dan_ref, w_ref, dhu_ref, dcw_ref):
        i = pl.program_id(1)

        @pl.when(i == 0)
        def _():
            dcw_ref[...] = jnp.zeros_like(dcw_ref)

        rid8 = lax.broadcasted_iota(jnp.int32, (8, tn), 0)

        def gate_grads(a, g, d):
            sg = _sigmoid(g)
            return d * (g * sg), d * a * (sg * (1.0 + g * (1.0 - sg)))

        dhc = gate_grads(hc_ref[0], hc_ref[1], da_ref[...])
        dhc_n = gate_grads(hcn_ref[0], hcn_ref[1], dan_ref[...])
        for h in range(2):
            w = w_ref[h]
            d = dhc[h]
            dn = jnp.where(i < n_i - 1, dhc_n[h], 0.0)
            u1 = _shift_rows_up(d, dn, 1)
            u2 = _shift_rows_up(d, dn, 2)
            dhu_ref[h] = (w[2:3, :] * d + w[1:2, :] * u1 + w[0:1, :] * u2).astype(BF16)
            x = hu_ref[h].astype(F32)
            t0, t1, t2, t3 = [jnp.sum(t, axis=0, keepdims=True) for t in (u2 * x, u1 * x, d * x, d)]
            dcw_ref[h] += jnp.where(rid8 == 0, t0, jnp.where(rid8 == 1, t1, jnp.where(rid8 == 2, t2, jnp.where(rid8 == 3, t3, 0.0))))

    cur = pl.BlockSpec((2, tm, tn), lambda j, i: (0, i, j))
    return pl.pallas_call(
        body,
        out_shape=(jax.ShapeDtypeStruct((2, S, F), BF16), jax.ShapeDtypeStruct((2, 8, F), F32)),
        grid=(F // tn, n_i),
        in_specs=[cur, pl.BlockSpec((2, 8, tn), lambda j, i: (0, jnp.minimum((i + 1) * r8, last8), j)), cur,
                  pl.BlockSpec((tm, tn), lambda j, i: (i, j)),
                  pl.BlockSpec((8, tn), lambda j, i: (jnp.minimum((i + 1) * r8, last8), j)),
                  pl.BlockSpec((2, 8, tn), lambda j, i: (0, 0, j))],
        out_specs=(cur, pl.BlockSpec((2, 8, tn), lambda j, i: (0, 0, j))),
        name="conv_gate_bwd", compiler_params=_cp(("parallel", "arbitrary"), VMEM_LIMIT),
    )(hc, hc, hu, dact, dact, cw)


def _out_proj_dx_prep(dh1_bf, w_out_bf, att, lse, qa):
    S = att.shape[0]
    tm = _pick(S, (256,))

    def body(dh_ref, w_ref, o_ref, lse_ref, q_ref, dsg_ref, qb_ref, doa_ref):
        lane = lax.broadcasted_iota(jnp.int32, (tm, LANES), 1)
        dh = dh_ref[...]
        dsg_ref[...] = lax.dot_general(dh, w_ref[ATT_WIDTH:, :], _NT, preferred_element_type=F32)
        datt = lax.dot_general(dh, w_ref[:ATT_WIDTH, :], _NT, preferred_element_type=F32)
        for p in range(N_PAIRS):
            pc = slice(p * LANES, (p + 1) * LANES)
            do = datt[:, pc]
            prod = o_ref[:, pc] * do
            for hh in range(2):
                sel = (lane >= HEAD_DIM) if hh else (lane < HEAD_DIM)
                delta = jnp.sum(jnp.where(sel, prod, 0.0), axis=-1, keepdims=True)
                dod = pltpu.roll(do, HEAD_DIM, 1) if hh else do
                cols = slice((2 * p + hh) * LANES, (2 * p + hh + 1) * LANES)
                doa_ref[:, cols] = jnp.where(lane < HEAD_DIM, dod, _aug(lane, _split3f(-delta))).astype(BF16)
                lcol = p * LANES + hh * HEAD_DIM
                l3 = _split3f(-lse_ref[:, lcol:lcol + 1])
                augl = jnp.where(lane == HEAD_DIM + 6, l3[0], jnp.where(lane == HEAD_DIM + 7, l3[1], l3[2])).astype(BF16)
                qb_ref[:, cols] = jnp.where((lane >= HEAD_DIM + 6) & (lane < HEAD_DIM + 9), augl, q_ref[:, cols])

    half = pl.BlockSpec((tm, ATT_WIDTH), lambda i: (i, 0))
    wide = pl.BlockSpec((tm, N_HEADS * LANES), lambda i: (i, 0))
    return pl.pallas_call(
        body,
        out_shape=(jax.ShapeDtypeStruct((S, GMLP_WIDTH), F32), jax.ShapeDtypeStruct(qa.shape, BF16),
                   jax.ShapeDtypeStruct(qa.shape, BF16)),
        grid=(S // tm,),
        in_specs=[pl.BlockSpec((tm, D_MODEL), lambda i: (i, 0)), pl.BlockSpec((D_MODEL, D_MODEL), lambda i: (0, 0)),
                  half, half, wide],
        out_specs=(half, wide, wide),
        name="out_proj_dx_prep", compiler_params=_cp(("parallel",), VMEM_LIMIT),
    )(dh1_bf, w_out_bf, att, lse, qa)


def _attn_bwd(qb, ka, va, doa):
    S = qb.shape[0]
    tk = _pick(S, (512, 256))
    tq = tk
    nq = S // tq

    def pair(a, scale=None):
        lane = lax.broadcasted_iota(jnp.int32, (a.shape[0], LANES), 1)
        out = jnp.where(lane < HEAD_DIM, a[:, :LANES], pltpu.roll(a[:, LANES:], HEAD_DIM, 1))
        return out if scale is None else out * scale

    def head_lanes(a, col, sign, first):
        lane = lax.broadcasted_iota(jnp.int32, (a.shape[0], LANES), 1)
        return jnp.where(lane == first, sign * a[:, col:col + 1],
                         jnp.where(lane == first + 1, sign * a[:, LANES + col:LANES + col + 1], 0.0))

    def body(q_ref, do_ref, k_ref, v_ref, dqc_ref, dkc_ref, dvc_ref, dcq_ref, dck_ref, dq_ref, dka_ref, dva_ref):
        kb = pl.program_id(1)

        @pl.when(kb == 0)
        def _():
            dq_ref[...] = jnp.zeros_like(dq_ref)

        dka_ref[...] = jnp.zeros_like(dka_ref)
        dva_ref[...] = jnp.zeros_like(dva_ref)
        def sub_tile(qs, q_len, k_off, k_len, masked):
            keys = slice(k_off, k_off + k_len)
            for h in range(2):
                cols = slice(h * LANES, (h + 1) * LANES)
                qblk = q_ref[pl.ds(qs, q_len), cols]
                doblk = do_ref[pl.ds(qs, q_len), cols]
                kh = k_ref[keys, cols]
                p = jnp.exp(lax.dot_general(kh, qblk, _NT, preferred_element_type=F32))
                if masked:
                    p = jnp.where(lax.broadcasted_iota(jnp.int32, p.shape, 1) >= lax.broadcasted_iota(jnp.int32, p.shape, 0),
                                  p, 0.0)
                ds = (p * lax.dot_general(v_ref[keys, cols], doblk, _NT, preferred_element_type=F32)).astype(BF16)
                dva_ref[keys, cols] += jnp.dot(p.astype(BF16), doblk, preferred_element_type=F32)
                dka_ref[keys, cols] += jnp.dot(ds, qblk, preferred_element_type=F32)
                dq_ref[pl.ds(qs, q_len), cols] += lax.dot_general(ds, kh, _TN, preferred_element_type=F32)

        half = tk // 2
        sub_tile(pl.multiple_of(kb * tq, tq), tq, 0, half, True)
        sub_tile(pl.multiple_of(kb * tq + half, half), half, half, half, True)

        rest = nq - 1 - kb
        odd = rest % 2

        @pl.when(odd == 1)
        def _():
            sub_tile(pl.multiple_of((kb + 1) * tq, tq), tq, 0, tk, False)

        def step(t, carry):
            sub_tile(pl.multiple_of((kb + 1 + odd + 2 * t) * tq, tq), 2 * tq, 0, tk, False)
            return carry

        lax.fori_loop(0, rest // 2, step, 0)
        dka = dka_ref[...]
        dkc_ref[...] = pair(dka).astype(BF16)
        dvc_ref[...] = pair(dva_ref[...]).astype(BF16)
        first = 2 * pl.program_id(0)
        dck_ref[...] = head_lanes(dka, HEAD_DIM + 3, -1.0, first)

        @pl.when(kb == nq - 1)
        def _():
            dqa = dq_ref[...]
            dqc_ref[...] = pair(dqa, HEAD_DIM ** -0.5).astype(BF16)
            dcq_ref[...] = head_lanes(dqa, HEAD_DIM, 1.0, first)

    wide = 2 * LANES
    half = jax.ShapeDtypeStruct((S, ATT_WIDTH), BF16)
    slabs = jax.ShapeDtypeStruct((N_PAIRS, S, LANES), F32)
    return pl.pallas_call(
        body,
        out_shape=(half, half, half, slabs, slabs),
        grid=(N_PAIRS, nq),
        in_specs=[pl.BlockSpec((S, wide), lambda p, j: (0, p)), pl.BlockSpec((S, wide), lambda p, j: (0, p)),
                  pl.BlockSpec((tk, wide), lambda p, j: (j, p)), pl.BlockSpec((tk, wide), lambda p, j: (j, p))],
        out_specs=(pl.BlockSpec((S, LANES), lambda p, j: (0, p)), pl.BlockSpec((tk, LANES), lambda p, j: (j, p)),
                   pl.BlockSpec((tk, LANES), lambda p, j: (j, p)), pl.BlockSpec((None, S, LANES), lambda p, j: (p, 0, 0)),
                   pl.BlockSpec((None, tk, LANES), lambda p, j: (p, j, 0))),
        scratch_shapes=[pltpu.VMEM((S, wide), F32), pltpu.VMEM((tk, wide), F32), pltpu.VMEM((tk, wide), F32)],
        name="attn_bwd", compiler_params=_cp(("parallel", "arbitrary"), VMEM_LIMIT),
    )(qb, doa, ka, va)


def _gmlp_bwd(ug, dsg, gain, w_s, wt_s, bias_full):
    S = ug.shape[0]
    tm = _pick(S, (512, 256, 128))
    n_chunks = tm // CHUNK
    n_i = S // tm
    ones = _group_ones()
    nt = (((1,), (1,)), ((), ()))

    def body(ug_ref, dsg_ref, gain_ref, w_ref, wt_ref, bias_ref, ones_ref, dug_ref, dw_ref, dgain_ref, dbias_ref,
             dbacc_ref):
        i = pl.program_id(0)

        @pl.when(i == 0)
        def _():
            dw_ref[...] = jnp.zeros_like(dw_ref)
            dgain_ref[...] = jnp.zeros_like(dgain_ref)
            dbacc_ref[...] = jnp.zeros_like(dbacc_ref)

        ones_m = ones_ref[...]
        pu = ug_ref[:, :GMLP_WIDTH]
        pg = ug_ref[:, GMLP_WIDTH:]
        u = _gelu(pu)
        vr = _gelu(pg)
        ms = _dot3(vr * vr, ones_m) * (1.0 / GROUP_DIM)
        rinv = lax.rsqrt(ms + EPS)
        vhat = vr * rinv
        gain_v = gain_ref[...]
        vn = (vhat * gain_v).astype(BF16)
        mixed = _gmlp_mixed(vn, w_ref, bias_ref[...], n_chunks)
        dsg_v = dsg_ref[...]
        du = dsg_v * mixed
        dmixed = dsg_v * u
        dm_bf = dmixed.astype(BF16)
        lane = lax.broadcasted_iota(jnp.int32, (CHUNK, LANES), 1)
        row = lax.broadcasted_iota(jnp.int32, (CHUNK, CHUNK), 0)
        col = lax.broadcasted_iota(jnp.int32, (CHUNK, CHUNK), 1)
        wts = [jnp.where(col >= row, wt_ref[g], 0.0).astype(BF16) for g in range(N_GROUPS)]
        dvn_rows = []
        dbsum = jnp.zeros((CHUNK, GMLP_WIDTH), F32)
        for ci in range(n_chunks):
            rs = slice(ci * CHUNK, (ci + 1) * CHUNK)
            dbsum = dbsum + dmixed[rs, :]
            cols = []
            for pp in range(N_GROUPS // 2):
                cs = slice(pp * LANES, (pp + 1) * LANES)
                dm = dm_bf[rs, cs]
                dm_lo = jnp.where(lane < GROUP_DIM, dm, jnp.zeros_like(dm))
                dm_hi = jnp.where(lane >= GROUP_DIM, dm, jnp.zeros_like(dm))
                vb = vn[rs, cs]
                dw_ref[2 * pp] += lax.dot_general(dm_lo, vb, nt, preferred_element_type=F32)
                dw_ref[2 * pp + 1] += lax.dot_general(dm_hi, vb, nt, preferred_element_type=F32)
                cols.append(jnp.dot(wts[2 * pp], dm_lo, preferred_element_type=F32)
                            + jnp.dot(wts[2 * pp + 1], dm_hi, preferred_element_type=F32))
            dvn_rows.append(jnp.concatenate(cols, axis=1))
        dvn = jnp.concatenate(dvn_rows, axis=0)
        dbacc_ref[...] += dbsum
        dgain_ref[0:1, :] += jnp.sum(dvn * vhat, axis=0, keepdims=True)
        dvhat = dvn * gain_v
        gm = _dot3(dvhat * vhat, ones_m) * (1.0 / GROUP_DIM)
        dvr = rinv * (dvhat - vhat * gm)
        dug_ref[:, :GMLP_WIDTH] = (du * _gelu_grad(pu)).astype(BF16)
        dug_ref[:, GMLP_WIDTH:] = (dvr * _gelu_grad(pg)).astype(BF16)

        @pl.when(i == n_i - 1)
        def _():
            for g in range(N_GROUPS):
                dw_ref[g] = jnp.where(row >= col, dw_ref[g], 0.0)
            dbias_ref[...] = _dot3(dbacc_ref[...], ones_m)

    return pl.pallas_call(
        body,
        out_shape=(jax.ShapeDtypeStruct((S, 2 * GMLP_WIDTH), BF16), jax.ShapeDtypeStruct((N_GROUPS, CHUNK, CHUNK), F32),
                   jax.ShapeDtypeStruct((8, GMLP_WIDTH), F32), jax.ShapeDtypeStruct((CHUNK, GMLP_WIDTH), F32)),
        grid=(n_i,),
        in_specs=[pl.BlockSpec((tm, 2 * GMLP_WIDTH), lambda i: (i, 0)), pl.BlockSpec((tm, GMLP_WIDTH), lambda i: (i, 0)),
                  pl.BlockSpec((1, GMLP_WIDTH), lambda i: (0, 0)),
                  pl.BlockSpec((N_GROUPS, CHUNK, CHUNK), lambda i: (0, 0, 0)),
                  pl.BlockSpec((N_GROUPS, CHUNK, CHUNK), lambda i: (0, 0, 0)),
                  pl.BlockSpec((CHUNK, GMLP_WIDTH), lambda i: (0, 0)),
                  pl.BlockSpec((GMLP_WIDTH, GMLP_WIDTH), lambda i: (0, 0))],
        out_specs=(pl.BlockSpec((tm, 2 * GMLP_WIDTH), lambda i: (i, 0)),
                   pl.BlockSpec((N_GROUPS, CHUNK, CHUNK), lambda i: (0, 0, 0)),
                   pl.BlockSpec((8, GMLP_WIDTH), lambda i: (0, 0)),
                   pl.BlockSpec((CHUNK, GMLP_WIDTH), lambda i: (0, 0))),
        scratch_shapes=[pltpu.VMEM((CHUNK, GMLP_WIDTH), F32)],
        name="gmlp_bwd", compiler_params=_cp(("arbitrary",), VMEM_LIMIT),
    )(ug, dsg, gain, w_s, wt_s, bias_full, ones)


def _gate_bwd(dcq, dck, zf):
    S = zf.shape[0]
    tm = _pick(S, (256,))
    n_i = S // tm
    triu = (lax.broadcasted_iota(jnp.int32, (tm, tm), 0) <= lax.broadcasted_iota(jnp.int32, (tm, tm), 1)).astype(BF16)

    def body(dcq_ref, dck_ref, zf_ref, tri_ref, dzf_ref, dbf_ref, carry_ref):
        i = pl.program_id(0)

        @pl.when(i == 0)
        def _():
            carry_ref[...] = jnp.zeros_like(carry_ref)
            dbf_ref[...] = jnp.zeros_like(dbf_ref)

        lane = lax.broadcasted_iota(jnp.int32, (tm, LANES), 1)
        dc = dcq_ref[0] + dck_ref[0]
        for p in range(1, N_PAIRS):
            dc = dc + (dcq_ref[p] + dck_ref[p])
        dlf = _dot3l(tri_ref[...], dc) + carry_ref[0:1, :]
        carry_ref[0:1, :] = dlf[0:1, :]
        dz = jnp.where(lane < N_HEADS, dlf * _sigmoid(-zf_ref[...]), 0.0)
        dzf_ref[...] = dz.astype(BF16)
        dbf_ref[0:1, :] += jnp.sum(dz, axis=0, keepdims=True)

    return pl.pallas_call(
        body,
        out_shape=(jax.ShapeDtypeStruct((S, LANES), BF16), jax.ShapeDtypeStruct((8, LANES), F32)),
        grid=(n_i,),
        in_specs=[pl.BlockSpec((N_PAIRS, tm, LANES), lambda i: (0, n_i - 1 - i, 0)),
                  pl.BlockSpec((N_PAIRS, tm, LANES), lambda i: (0, n_i - 1 - i, 0)),
                  pl.BlockSpec((tm, LANES), lambda i: (n_i - 1 - i, 0)),
                  pl.BlockSpec((tm, tm), lambda i: (0, 0))],
        out_specs=(pl.BlockSpec((tm, LANES), lambda i: (n_i - 1 - i, 0)), pl.BlockSpec((8, LANES), lambda i: (0, 0))),
        scratch_shapes=[pltpu.VMEM((8, LANES), F32)],
        name="gate_bwd", compiler_params=_cp(("arbitrary",), VMEM_LIMIT),
    )(dcq, dck, zf, triu)


def _out_proj_fwd(att_bf, sg, w_out_bf, x, g_ffn):
    S = x.shape[0]
    tm = _pick(S, (512, 256))

    def body(a_ref, s_ref, w_ref, x_ref, g_ref, h_ref, hn_ref):
        h = (x_ref[...] + jnp.dot(a_ref[...], w_ref[:ATT_WIDTH, :], preferred_element_type=F32)
             + jnp.dot(s_ref[...], w_ref[ATT_WIDTH:, :], preferred_element_type=F32))
        h_ref[...] = h
        r = lax.rsqrt(jnp.mean(h * h, axis=-1, keepdims=True) + EPS)
        hn_ref[...] = ((h * r) * g_ref[...]).astype(BF16)

    row = pl.BlockSpec((tm, D_MODEL), lambda i: (i, 0))
    half = pl.BlockSpec((tm, ATT_WIDTH), lambda i: (i, 0))
    return pl.pallas_call(
        body, out_shape=(jax.ShapeDtypeStruct((S, D_MODEL), F32), jax.ShapeDtypeStruct((S, D_MODEL), BF16)),
        grid=(S // tm,),
        in_specs=[half, half, pl.BlockSpec((D_MODEL, D_MODEL), lambda i: (0, 0)), row,
                  pl.BlockSpec((1, D_MODEL), lambda i: (0, 0))],
        out_specs=(row, row), name="out_proj", compiler_params=_cp(("parallel",), VMEM_LIMIT),
    )(att_bf, sg, w_out_bf, x, g_ffn)


def _out_proj_dw(att_bf, sg, dh1_bf):
    S = att_bf.shape[0]
    tk = _pick(S, (1024, 512))

    def body(a_ref, s_ref, d_ref, o_ref):
        k = pl.program_id(0)

        @pl.when(k == 0)
        def _():
            o_ref[...] = jnp.zeros_like(o_ref)

        d = d_ref[...]
        o_ref[:ATT_WIDTH, :] += lax.dot_general(a_ref[...], d, _TN, preferred_element_type=F32)
        o_ref[ATT_WIDTH:, :] += lax.dot_general(s_ref[...], d, _TN, preferred_element_type=F32)

    half = pl.BlockSpec((tk, ATT_WIDTH), lambda k: (k, 0))
    return pl.pallas_call(
        body, out_shape=jax.ShapeDtypeStruct((D_MODEL, D_MODEL), F32), grid=(S // tk,),
        in_specs=[half, half, pl.BlockSpec((tk, D_MODEL), lambda k: (k, 0))],
        out_specs=pl.BlockSpec((D_MODEL, D_MODEL), lambda k: (0, 0)),
        name="out_proj_dw", compiler_params=_cp(("arbitrary",), VMEM_LIMIT),
    )(att_bf, sg, dh1_bf)


_IN_PIECES = ((0, ATT_WIDTH), (ATT_WIDTH, ATT_WIDTH), (2 * ATT_WIDTH, ATT_WIDTH), (QKV, 2 * GMLP_WIDTH), (UG_END, LANES))


def _inproj_bwd_dx(pieces, w_pad, x, g_mix, dh1):
    S = x.shape[0]
    tm = _pick(S, (512, 256))

    def body(*refs):
        p_refs, (w_ref, x_ref, g_ref, r_ref, dx_ref, dg_ref) = refs[:5], refs[5:]
        i = pl.program_id(0)

        @pl.when(i == 0)
        def _():
            dg_ref[...] = jnp.zeros_like(dg_ref)

        dxn = None
        for p_ref, (c0, width) in zip(p_refs, _IN_PIECES):
            part = jnp.dot(p_ref[...], w_ref[c0:c0 + width, :], preferred_element_type=F32)
            dxn = part if dxn is None else dxn + part
        xf = x_ref[...]
        r = lax.rsqrt(jnp.mean(xf * xf, axis=-1, keepdims=True) + EPS)
        xhat = xf * r
        dg_ref[0:1, :] += jnp.sum(dxn * xhat, axis=0, keepdims=True)
        dhat = dxn * g_ref[...]
        dx_ref[...] = r_ref[...] + r * (dhat - xhat * jnp.mean(dhat * xhat, axis=-1, keepdims=True))

    row = pl.BlockSpec((tm, D_MODEL), lambda i: (i, 0))
    return pl.pallas_call(
        body, out_shape=(jax.ShapeDtypeStruct((S, D_MODEL), F32), jax.ShapeDtypeStruct((8, D_MODEL), F32)),
        grid=(S // tm,),
        in_specs=[pl.BlockSpec((tm, width), lambda i: (i, 0)) for _, width in _IN_PIECES]
        + [pl.BlockSpec((IN_PAD, D_MODEL), lambda i: (0, 0)), row, pl.BlockSpec((1, D_MODEL), lambda i: (0, 0)), row],
        out_specs=(row, pl.BlockSpec((8, D_MODEL), lambda i: (0, 0))),
        name="in_proj_dx", compiler_params=_cp(("arbitrary",), VMEM_LIMIT),
    )(*pieces, w_pad, x, g_mix, dh1)


def _inproj_bwd_dw(xn, pieces):
    S = xn.shape[0]
    tk = _pick(S, (1024, 512))

    def body(*refs):
        x_ref, p_refs, o_ref = refs[0], refs[1:6], refs[6]
        k = pl.program_id(0)

        @pl.when(k == 0)
        def _():
            o_ref[...] = jnp.zeros_like(o_ref)

        xb = x_ref[...]
        for p_ref, (c0, width) in zip(p_refs, _IN_PIECES):
            o_ref[:, c0:c0 + width] += lax.dot_general(xb, p_ref[...], _TN, preferred_element_type=F32)

    return pl.pallas_call(
        body, out_shape=jax.ShapeDtypeStruct((D_MODEL, IN_PAD), F32), grid=(S // tk,),
        in_specs=[pl.BlockSpec((tk, D_MODEL), lambda k: (k, 0))]
        + [pl.BlockSpec((tk, width), lambda k: (k, 0)) for _, width in _IN_PIECES],
        out_specs=pl.BlockSpec((D_MODEL, IN_PAD), lambda k: (0, 0)),
        name="in_proj_dw", compiler_params=_cp(("arbitrary",), VMEM_LIMIT),
    )(xn, *pieces)


def _adamw(w, m, v, parts, name):
    R, C = w.shape[-2:]
    tr = R
    for cand in (256, 128, 64, 32, 16, 8):
        if R % cand == 0 and R > cand:
            tr = cand
            break
    c1 = 1.0 / (1.0 - ADAM_B1 ** ADAM_STEP)
    c2 = 1.0 / (1.0 - ADAM_B2 ** ADAM_STEP)

    def body(w_ref, m_ref, v_ref, p_ref, g_ref, d_ref, nm_ref, nv_ref):
        g = p_ref[0].astype(F32)
        for j in range(1, N_DEV):
            g = g + p_ref[j].astype(F32)
        g_ref[...] = g
        nm = ADAM_B1 * m_ref[...] + (1.0 - ADAM_B1) * g
        nv = ADAM_B2 * v_ref[...] + (1.0 - ADAM_B2) * (g * g)
        nm_ref[...] = nm
        nv_ref[...] = nv
        d_ref[...] = -ADAM_LR * ((nm * c1) / (jnp.sqrt(nv * c2) + ADAM_EPS) + ADAM_WD * w_ref[...])

    if w.ndim == 3:
        spec = pl.BlockSpec((None, tr, C), lambda i: (0, i, 0))
    else:
        spec = pl.BlockSpec((tr, C), lambda i: (i, 0))
    shp = jax.ShapeDtypeStruct(w.shape, F32)
    return pl.pallas_call(
        body, out_shape=(shp, shp, shp, shp), grid=(R // tr,),
        in_specs=[spec, spec, spec, pl.BlockSpec((N_DEV, tr, C), lambda i: (0, i, 0))],
        out_specs=(spec, spec, spec, spec),
        name=name, compiler_params=_cp(("parallel",), VMEM_LIMIT),
    )(w, m, v, parts)


def _adamw_owner(w, m, v, landed, sent, me, name):
    R, C = w.shape[-2:]
    tr = R
    for cand in (256, 128, 64, 32, 16, 8):
        if R % cand == 0 and R > cand:
            tr = cand
            break
    c1 = 1.0 / (1.0 - ADAM_B1 ** ADAM_STEP)
    c2 = 1.0 / (1.0 - ADAM_B2 ** ADAM_STEP)

    def body(me_ref, w_ref, m_ref, v_ref, p_ref, own_ref, g_ref, d_ref, nm_ref, nv_ref):
        mine = me_ref[0]
        own = own_ref[...].astype(F32)
        g = jnp.where(mine == 0, own, p_ref[0].astype(F32))
        for j in range(1, N_DEV):
            g = g + jnp.where(mine == j, own, p_ref[j].astype(F32))
        g_ref[...] = g
        nm = ADAM_B1 * m_ref[...] + (1.0 - ADAM_B1) * g
        nv = ADAM_B2 * v_ref[...] + (1.0 - ADAM_B2) * (g * g)
        nm_ref[...] = nm
        nv_ref[...] = nv
        d_ref[...] = -ADAM_LR * ((nm * c1) / (jnp.sqrt(nv * c2) + ADAM_EPS) + ADAM_WD * w_ref[...])

    spec = pl.BlockSpec((None, tr, C), lambda i, me_ref: (0, i, 0))
    shp = jax.ShapeDtypeStruct(w.shape, F32)
    return pl.pallas_call(
        body, out_shape=(shp, shp, shp, shp),
        grid_spec=pltpu.PrefetchScalarGridSpec(
            num_scalar_prefetch=1, grid=(R // tr,),
            in_specs=[spec, spec, spec, pl.BlockSpec((N_DEV, tr, C), lambda i, me_ref: (0, i, 0)),
                      pl.BlockSpec((None, tr, C), lambda i, me_ref: (me_ref[0], i, 0))],
            out_specs=(spec, spec, spec, spec)),
        name=name, compiler_params=_cp(("parallel",), VMEM_LIMIT),
    )(jnp.reshape(me, (1,)).astype(jnp.int32), w, m, v, landed, sent)


def _adamw_packed(w, m, v, parts, sizes, name):
    R = w.shape[0]
    assert R == sum(sizes)
    c1 = 1.0 / (1.0 - ADAM_B1 ** ADAM_STEP)
    c2 = 1.0 / (1.0 - ADAM_B2 ** ADAM_STEP)
    n = len(sizes)

    def body(w_ref, m_ref, v_ref, p_ref, *out_refs):
        g = p_ref[0]
        for j in range(1, N_DEV):
            g = g + p_ref[j]
        nm = ADAM_B1 * m_ref[...] + (1.0 - ADAM_B1) * g
        nv = ADAM_B2 * v_ref[...] + (1.0 - ADAM_B2) * (g * g)
        d = -ADAM_LR * ((nm * c1) / (jnp.sqrt(nv * c2) + ADAM_EPS) + ADAM_WD * w_ref[...])
        for kind, val in enumerate((g, d, nm, nv)):
            off = 0
            for k, rows in enumerate(sizes):
                out_refs[kind * n + k][...] = val[off:off + rows, :]
                off += rows

    whole = pl.BlockSpec((R, LANES), lambda i: (0, 0))
    shapes = [jax.ShapeDtypeStruct((rows, LANES), F32) for rows in sizes] * 4
    res = pl.pallas_call(
        body, out_shape=tuple(shapes), grid=(1,),
        in_specs=[whole, whole, whole, pl.BlockSpec((N_DEV, R, LANES), lambda i: (0, 0, 0))],
        out_specs=tuple(pl.BlockSpec((rows, LANES), lambda i: (0, 0)) for rows in sizes) * 4,
        name=name, compiler_params=_cp(("arbitrary",), VMEM_LIMIT),
    )(w, m, v, parts)
    return [list(res[kind * n:(kind + 1) * n]) for kind in range(4)]


def _place():
    x, y, c = lax.axis_index("x"), lax.axis_index("y"), lax.axis_index("c")
    return x, y, c


def _all_gather(blocks, name):
    n = len(blocks)

    def body(*refs):
        ins, outs = refs[:n], refs[n:2 * n]
        send_sems, recv_sems, local_sems = refs[2 * n:]
        x, y, c = _place()
        me, sibling = (x, y, c), (x, y, 1 - c)
        chips = [(1 - x, y), (x, 1 - y), (1 - x, 1 - y)]
        sends = []
        for a in range(n):
            out = outs[a]

            def slot(px, py, pc, out=out):
                return out.at[4 * px + 2 * py + pc]

            def copy(k, block, to, src=None, a=a, slot=slot):
                return pltpu.make_async_remote_copy(
                    src_ref=slot(*block) if src is None else src, dst_ref=slot(*block),
                    send_sem=send_sems.at[a, k], recv_sem=recv_sems.at[a, k], device_id=to, device_id_type=MESH)

            mine = pltpu.make_async_copy(ins[a], slot(*me), local_sems.at[a])
            mine.start()
            first = [copy(0, me, sibling, src=ins[a])]
            first += [copy(1 + j, me, (*chip, c), src=ins[a]) for j, chip in enumerate(chips)]
            for cp in first:
                cp.start()
            sends.append((mine, first, copy))
        for a in range(n):
            mine, first, copy = sends[a]
            passed = [copy(4 + j, (*chip, c), sibling) for j, chip in enumerate(chips)]
            for j, chip in enumerate(chips):
                copy(1 + j, (*chip, c), me).wait_recv()
                passed[j].start()
            copy(0, sibling, me).wait_recv()
            for j, chip in enumerate(chips):
                copy(4 + j, (*chip, 1 - c), me).wait_recv()
            for cp in first + passed:
                cp.wait_send()
            mine.wait()

    any_spec = pl.BlockSpec(memory_space=pl.ANY)
    return pl.pallas_call(
        body, out_shape=tuple(jax.ShapeDtypeStruct((N_DEV,) + b.shape, b.dtype) for b in blocks),
        in_specs=[any_spec] * n, out_specs=tuple([any_spec] * n),
        scratch_shapes=[pltpu.SemaphoreType.DMA((n, 7)), pltpu.SemaphoreType.DMA((n, 7)), pltpu.SemaphoreType.DMA((n,))],
        name=name,
    )(*blocks)


_HBM = pl.BlockSpec(memory_space=pltpu.HBM)
_SEM = pl.BlockSpec(memory_space=pltpu.SEMAPHORE)
_EFFECT = pltpu.SideEffectType.DATAFLOW_SIDE_EFFECTING


def _peers(x, y, c):
    out = []
    for k in range(1, N_DEV):
        px, py, pc = x ^ ((k >> 2) & 1), y ^ ((k >> 1) & 1), c ^ (k & 1)
        out.append((k, (px, py, pc), 4 * px + 2 * py + pc))
    return out


def _xchg_copies(src_refs, land_refs, send_sems, recv_sems, scatter):
    x, y, c = _place()
    me = 4 * x + 2 * y + c
    copies = []
    for a, (src, land) in enumerate(zip(src_refs, land_refs)):
        for k, place, idx in _peers(x, y, c):
            j = a * (N_DEV - 1) + k - 1
            copies.append(pltpu.make_async_remote_copy(
                src_ref=src.at[idx] if scatter[a] else src, dst_ref=land.at[me],
                send_sem=send_sems[j], recv_sem=recv_sems[j], device_id=place, device_id_type=MESH))
    return copies


def _xchg_start(srcs, scatter, name):
    n = len(srcs)
    lands = [lax.empty((N_DEV,) + (s.shape[1:] if sc else s.shape), s.dtype) for s, sc in zip(srcs, scatter)]

    ns = n * (N_DEV - 1)

    def body(*refs):
        sems = refs[2 * n:2 * n + 2 * ns]
        for cp in _xchg_copies(refs[:n], refs[n:2 * n], sems[:ns], sems[ns:], scatter):
            cp.start()
        token = refs[-1]
        token[...] = jnp.zeros_like(token)

    both = list(srcs) + lands
    res = pl.pallas_call(
        body, name=name,
        out_shape=(*[pltpu.SemaphoreType.DMA(())] * (2 * ns),
                   *[pltpu.HBM(a.shape, a.dtype) for a in both], jax.ShapeDtypeStruct((8, LANES), F32)),
        in_specs=[_HBM] * (2 * n),
        out_specs=(*([_SEM] * (2 * ns)), *([_HBM] * (2 * n)), pl.BlockSpec(memory_space=pltpu.VMEM)),
        input_output_aliases={i: 2 * ns + i for i in range(2 * n)},
        compiler_params=pltpu.CompilerParams(has_side_effects=_EFFECT),
    )(*[pltpu.with_memory_space_constraint(a, pltpu.HBM) for a in both])
    return (tuple(res[:2 * ns]), tuple(res[2 * ns:2 * ns + 2 * n])), res[-1]


def _xchg_wait(handle, scatter, after, name):
    sems, thru = handle
    n = len(thru) // 2
    ns = len(sems) // 2

    def body(*refs):
        got = refs[2 * n:2 * n + 2 * ns]
        for cp in _xchg_copies(refs[:n], refs[n:2 * n], got[:ns], got[ns:], scatter):
            cp.wait_send()
            cp.wait_recv()

    outs = pl.pallas_call(
        body, name=name, out_shape=tuple(pltpu.HBM(a.shape, a.dtype) for a in thru),
        in_specs=[_HBM] * (2 * n) + [_SEM] * (2 * ns) + [pl.BlockSpec(memory_space=pl.ANY)],
        out_specs=tuple([_HBM] * (2 * n)), input_output_aliases={i: i for i in range(2 * n)},
        compiler_params=pltpu.CompilerParams(has_side_effects=_EFFECT),
    )(*thru, *sems, after)
    return outs[:n], outs[n:]


def _tie(a, token):
    return a if token is None else a + token[0, 0].astype(a.dtype)


def _rows128(a):
    flat = a.reshape(-1)
    rows = -(-flat.shape[0] // LANES)
    rows = -(-rows // 8) * 8
    return jnp.pad(flat, (0, rows * LANES - flat.shape[0])).reshape(rows, LANES)


def _local_step(x, target, norm_mix_g, w_in_t, b_forget, gmlp_norm_g, w_spatial, b_spatial, norm_ffn_g, conv_b,
                norm_final_g, rest_fn, send_fn, small_fn, token=None):
    f = D_FF
    g_mix = norm_mix_g.reshape(1, D_MODEL)
    w_pad = jnp.pad(w_in_t, ((0, IN_PAD - IN_COLS), (0, 0)))
    bf_pad = jnp.pad(b_forget.reshape(1, N_HEADS), ((0, 0), (0, LANES - N_HEADS)))
    xn, qa, ka, va, ug, zf = _inproj_fwd(x, _tie(g_mix, token), w_pad, bf_pad)
    bias_full = jnp.repeat(b_spatial.reshape(N_GROUPS, CHUNK).T, GROUP_DIM, axis=1)
    w_s = w_spatial.reshape(N_GROUPS, CHUNK, CHUNK)
    gain = gmlp_norm_g.reshape(1, GMLP_WIDTH)
    sg = _gmlp_fwd(ug, gain, w_s, bias_full)
    att, lse, att_bf = _attn_fwd(qa, ka, va)
    w_out_bf, w_up_bf, conv_w, w_down_bf = rest_fn(att_bf)
    g_ffn = norm_ffn_g.reshape(1, D_MODEL)
    h1, hn = _out_proj_fwd(att_bf, sg, w_out_bf, x, g_ffn)
    cw = jnp.pad(conv_w.reshape(3, 2, f).transpose(1, 0, 2), ((0, 0), (0, 5), (0, 0)))
    cb = conv_b.reshape(2, 1, f)
    hu, hc, act = _ffn_up_conv(hn, w_up_bf, cw, cb)
    loss_blk, dh2, dh2_bf, dg_final = _ffn_down_loss(act, w_down_bf, h1, norm_final_g.reshape(1, D_MODEL), target)
    dw_down = _mm(act, dh2_bf, mode="tn", out_dtype=F32, tm=1408, tn=1024, tk=2048, name="ffn_down_dw")
    dact = _mm(dh2_bf, w_down_bf, mode="nt", out_dtype=F32, tm=1024, tn=1408, tk=1024, outer="j", name="ffn_down_dx")
    dhu, dcw = _conv_gate_bwd(hc, hu, dact, _tie(cw, send_fn("w_down", dw_down)))
    dw_up = _mm(hn, dhu, mode="tn", out_dtype=F32, tm=1024, tn=1408, tk=2048, b_halves=True, outer="j", name="ffn_up_dw")
    dh1, dh1_bf, dg_ffn = _ffn_up_dx_rms(dhu, w_up_bf, h1, _tie(g_ffn, send_fn("w_up", dw_up)), dh2)
    dsg, qb, doa = _out_proj_dx_prep(dh1_bf, w_out_bf, att, lse, qa)
    dw_out = _out_proj_dw(att_bf, sg, dh1_bf)
    dq, dk, dv, dcq, dck = _attn_bwd(qb, ka, va, doa)
    wt_s = w_s.transpose(0, 2, 1)
    dug, dw_s, dgain, dbias = _gmlp_bwd(ug, dsg, _tie(gain, send_fn("w_out", dw_out)), w_s, wt_s, bias_full)
    dzf, dbf = _gate_bwd(dcq, dck, zf)
    grad_x, dg_mix = _inproj_bwd_dx((dq, dk, dv, dug, dzf), w_pad, x, g_mix, dh1)
    grads = dict(
        norm_mix_g=dg_mix[0:1, :],
        b_forget=dbf[0:1, :N_HEADS],
        gmlp_norm_g=dgain[0:1, :],
        w_spatial=dw_s,
        b_spatial=dbias[:, ::GROUP_DIM].T,
        norm_ffn_g=dg_ffn[0:1, :],
        conv_w=dcw[:, 0:3, :].transpose(1, 0, 2).reshape(3, 2 * f),
        conv_b=dcw[:, 3, :].reshape(1, 2 * f),
        norm_final_g=dg_final[0, :],
    )
    token = small_fn(loss_blk[0, 0], grads)
    dw_in = _inproj_bwd_dw(xn, (dq, dk, dv, dug, _tie(dzf, token)))
    return grad_x, send_fn("w_in", dw_in[:, :IN_COLS])


SMALL = ("norm_mix_g", "b_forget", "gmlp_norm_g", "w_spatial", "b_spatial", "norm_ffn_g", "conv_b", "norm_final_g")


def kernel(x, norm_mix_g, w_in, b_forget, gmlp_norm_g, w_spatial, b_spatial, w_out, norm_ffn_g, w_up, conv_w, conv_b, w_down, norm_final_g, loss_target, m_norm_mix_g, m_w_in, m_b_forget, m_gmlp_norm_g, m_w_spatial, m_b_spatial, m_w_out, m_norm_ffn_g, m_w_up, m_conv_w, m_conv_b, m_w_down, m_norm_final_g, v_norm_mix_g, v_w_in, v_b_forget, v_gmlp_norm_g, v_w_spatial, v_b_spatial, v_w_out, v_norm_ffn_g, v_w_up, v_conv_w, v_conv_b, v_w_down, v_norm_final_g):
    weights = dict(norm_mix_g=norm_mix_g, w_in=w_in, b_forget=b_forget, gmlp_norm_g=gmlp_norm_g, w_spatial=w_spatial,
                   b_spatial=b_spatial, w_out=w_out, norm_ffn_g=norm_ffn_g, w_up=w_up, conv_w=conv_w, conv_b=conv_b,
                   w_down=w_down, norm_final_g=norm_final_g)
    m_in = dict(norm_mix_g=m_norm_mix_g, w_in=m_w_in, b_forget=m_b_forget, gmlp_norm_g=m_gmlp_norm_g,
                w_spatial=m_w_spatial, b_spatial=m_b_spatial, w_out=m_w_out, norm_ffn_g=m_norm_ffn_g, w_up=m_w_up,
                conv_w=m_conv_w, conv_b=m_conv_b, w_down=m_w_down, norm_final_g=m_norm_final_g)
    v_in = dict(norm_mix_g=v_norm_mix_g, w_in=v_w_in, b_forget=v_b_forget, gmlp_norm_g=v_gmlp_norm_g,
                w_spatial=v_w_spatial, b_spatial=v_b_spatial, w_out=v_w_out, norm_ffn_g=v_norm_ffn_g, w_up=v_w_up,
                conv_w=v_conv_w, conv_b=v_conv_b, w_down=v_w_down, norm_final_g=v_norm_final_g)
    order = list(weights)
    me = 4 * lax.axis_index("x") + 2 * lax.axis_index("y") + lax.axis_index("c")
    n_in, n_up = w_in.shape[2], w_up.shape[2]
    r_out, r_down = w_out.shape[1], w_down.shape[1]

    def with_mine(landed, mine):
        return lax.dynamic_update_index_in_dim(landed, mine, me, 0)

    up_blk = w_up[0].T.astype(BF16)
    out_blk = w_out[0].astype(BF16)
    down_blk = w_down[0].astype(BF16)
    taps_blk = jnp.pad(conv_w[0], ((0, 5), (0, 0)))
    (in_all,) = _all_gather([w_in[0].T.astype(BF16)], "gather_w_in")
    in_all, rest_blocks = lax.optimization_barrier((in_all, [up_blk, out_blk, down_blk, taps_blk]))
    rest_handle, token = _xchg_start(rest_blocks, [False] * 4, "gather_rest_start")
    w_in_t = in_all.reshape(N_DEV * n_in, D_MODEL)

    def rest_fn(after):
        mine, landed = _xchg_wait(rest_handle, [False] * 4, after, "gather_rest_wait")
        up_all, out_all, down_all, taps_all = [with_mine(l, b) for l, b in zip(landed, mine)]
        return (out_all.reshape(N_DEV * r_out, D_MODEL), up_all.reshape(N_DEV * n_up, D_MODEL),
                taps_all[:, :3, :].transpose(1, 0, 2).reshape(3, N_DEV * n_up),
                down_all.reshape(N_DEV * r_down, D_MODEL))

    sent = {}

    def send_fn(name, grad):
        if name == "w_in":
            parts = grad.reshape(D_MODEL, N_DEV, -1).transpose(1, 0, 2).astype(BF16)
        elif name == "w_up":
            parts = grad.reshape(D_MODEL, N_DEV, -1).transpose(1, 0, 2)
        else:
            parts = grad.reshape(N_DEV, -1, D_MODEL)
        sent[name], tok = _xchg_start([parts], [True], "scatter_" + name + "_start")
        return tok

    small = {}

    def small_fn(loss_local, g):
        loss_rows = jnp.pad(loss_local.reshape(1, 1), ((0, 31), (0, LANES - 1)))
        packed = [_rows128(g[k]) for k in SMALL] + [loss_rows, _rows128(g["conv_w"])]
        small["sizes"] = [p.shape[0] for p in packed]
        small["handle"], tok = _xchg_start([jnp.concatenate(packed, axis=0)], [False], "gather_small_start")
        return tok

    grad_x, after = _local_step(
        x[0], loss_target[0], norm_mix_g, w_in_t, b_forget, gmlp_norm_g, w_spatial, b_spatial, norm_ffn_g, conv_b,
        norm_final_g, rest_fn, send_fn, small_fn, token)

    outs = {}

    def update_big(name, after):
        (parts,), (landed,) = _xchg_wait(sent[name], [True], after, "scatter_" + name + "_wait")
        outs[name] = tuple(_adamw_owner(weights[name], m_in[name], v_in[name], landed, parts, me, "adamw_" + name))
        return outs[name][0]

    for name in ("w_down", "w_up", "w_out"):
        after = update_big(name, after)

    (mine,), (landed,) = _xchg_wait(small["handle"], [False], after, "gather_small_wait")
    small_all = with_mine(landed, mine)
    sizes = small["sizes"]
    n_small_rows = sum(sizes[:-2])

    def pack(src):
        return jnp.concatenate([_rows128(src[k]) for k in SMALL] + [jnp.zeros((sizes[-2], LANES), F32)], axis=0)

    n_adam_rows = n_small_rows + sizes[-2]
    per_kind = _adamw_packed(pack(weights), pack(m_in), pack(v_in), small_all, sizes[:-1], "adamw_small")
    loss = per_kind[0][-1][0, 0]
    for j, k in enumerate(SMALL):
        shp = weights[k].shape
        cnt = math.prod(shp)
        outs[k] = tuple(kind[j].reshape(-1)[:cnt].reshape(shp) for kind in per_kind)
    sg_ = per_kind[0][0]
    taps_parts = small_all[:, n_adam_rows:, :].reshape(N_DEV, -1)[:, :3 * N_DEV * n_up].reshape(N_DEV, 3, N_DEV * n_up)
    taps_mine = lax.dynamic_slice_in_dim(taps_parts, me * n_up, n_up, axis=2)
    taps_mine = jnp.pad(taps_mine, ((0, 0), (0, 5), (0, 0)))

    def pad8(a):
        return jnp.pad(a[0], ((0, 5), (0, 0)))

    res = _adamw(pad8(conv_w), pad8(m_conv_w), pad8(v_conv_w), taps_mine, "adamw_conv_w")
    outs["conv_w"] = tuple(a[:3][None] for a in res)
    update_big("w_in", sg_)

    return (loss, grad_x[None], *[outs[k][0] for k in order], *[outs[k][1] for k in order],
            *[outs[k][2] for k in order], *[outs[k][3] for k in order])
```

```python
import functools
import math

import jax
import jax.numpy as jnp
from jax import lax
from jax.experimental import pallas as pl
from jax.experimental.pallas import tpu as pltpu

F32 = jnp.float32
BF16 = jnp.bfloat16

N_DEV = 8
D_MODEL = 1024
ATT_WIDTH = 512
GMLP_WIDTH = 512
HEAD_DIM = 64
N_HEADS = 8
N_PAIRS = 4
N_GROUPS = 8
GROUP_DIM = 64
CHUNK = 128
D_FF = 2816
IN_COLS = 2568
IN_PAD = 2688
QKV = 1536
UG_END = 2560
EPS = 1e-6
LANES = 128

ADAM_LR = 0.001
ADAM_B1 = 0.9
ADAM_B2 = 0.999
ADAM_EPS = 1e-08
ADAM_WD = 0.01
ADAM_STEP = 10

ATT_TQ = 1024
ATT_TK = 1024
FFN_TM, FFN_TN = 512, 1408
CONV_TM, CONV_TN = 512, 1408
VMEM_LIMIT = 56 * 1024 * 1024
MESH = pl.DeviceIdType.MESH


def _cp(sem, vmem=None):
    return pltpu.CompilerParams(dimension_semantics=sem, vmem_limit_bytes=vmem)


def _pick(n, prefs):
    for p in prefs:
        if n % p == 0:
            return p
    return n


def _split3(x):
    hi = x.astype(BF16)
    r1 = x - hi.astype(F32)
    mid = r1.astype(BF16)
    lo = (r1 - mid.astype(F32)).astype(BF16)
    return hi, mid, lo


def _dot3(x, ones_bf):
    d = functools.partial(jnp.dot, preferred_element_type=F32)
    out = []
    for c in range(0, x.shape[1], 2 * LANES):
        blk = ones_bf[c:c + 2 * LANES, c:c + 2 * LANES]
        hi, mid, lo = _split3(x[:, c:c + 2 * LANES])
        out.append(d(hi, blk) + d(mid, blk) + d(lo, blk))
    return jnp.concatenate(out, axis=1)


def _dot3l(ones_bf, x):
    n = x.shape[1]
    y = jnp.dot(ones_bf, jnp.concatenate(_split3(x), axis=1), preferred_element_type=F32)
    return y[:, :n] + y[:, n:2 * n] + y[:, 2 * n:]


def _gelu(x):
    k = math.sqrt(2.0 / math.pi)
    t = jnp.tanh(k * (x + 0.044715 * (x * x * x)))
    return 0.5 * x * (1.0 + t)


def _gelu_grad(x):
    k = math.sqrt(2.0 / math.pi)
    x2 = x * x
    t = jnp.tanh(k * (x + 0.044715 * (x2 * x)))
    return 0.5 * (1.0 + t) + 0.5 * x * (1.0 - t * t) * (k * (1.0 + 3.0 * 0.044715 * x2))


def _sigmoid(x):
    return 1.0 / (1.0 + jnp.exp(-x))


def _mm(a, b, *, mode, out_dtype, tm, tn, tk, name, res=None, a_halves=False, b_halves=False,
        out_halves=False, outer="i"):
    if mode == "tn":
        K, M = a.shape[-2], a.shape[-1] * (2 if a_halves else 1)
    else:
        M, K = a.shape[-2], a.shape[-1] * (2 if a_halves else 1)
    if mode == "nt":
        N = b.shape[-2]
        assert b.shape[-1] == K
    else:
        N = b.shape[-1] * (2 if b_halves else 1)
    tm, tn, tk = min(tm, M), min(tn, N), min(tk, K)
    assert M % tm == 0 and N % tn == 0 and K % tk == 0, (name, M, N, K, tm, tn, tk)
    nm, nn, nk = M // tm, N // tn, K // tk

    def ij(g0, g1):
        return (g0, g1) if outer == "i" else (g1, g0)

    if mode == "nn":
        dims = (((1,), (0,)), ((), ()))
        if a_halves:
            nkh = nk // 2
            a_spec = pl.BlockSpec((None, tm, tk), lambda g0, g1, k: (k // nkh, ij(g0, g1)[0], k % nkh))
        else:
            a_spec = pl.BlockSpec((tm, tk), lambda g0, g1, k: (ij(g0, g1)[0], k))
        b_spec = pl.BlockSpec((tk, tn), lambda g0, g1, k: (k, ij(g0, g1)[1]))
    elif mode == "nt":
        dims = (((1,), (1,)), ((), ()))
        if a_halves:
            nkh = nk // 2
            a_spec = pl.BlockSpec((None, tm, tk), lambda g0, g1, k: (k // nkh, ij(g0, g1)[0], k % nkh))
        else:
            a_spec = pl.BlockSpec((tm, tk), lambda g0, g1, k: (ij(g0, g1)[0], k))
        b_spec = pl.BlockSpec((tn, tk), lambda g0, g1, k: (ij(g0, g1)[1], k))
    else:
        dims = (((0,), (0,)), ((), ()))
        if a_halves:
            nmh = nm // 2
            a_spec = pl.BlockSpec((None, tk, tm), lambda g0, g1, k: (ij(g0, g1)[0] // nmh, k, ij(g0, g1)[0] % nmh))
        else:
            a_spec = pl.BlockSpec((tk, tm), lambda g0, g1, k: (k, ij(g0, g1)[0]))
        if b_halves:
            nnh = nn // 2
            b_spec = pl.BlockSpec((None, tk, tn), lambda g0, g1, k: (ij(g0, g1)[1] // nnh, k, ij(g0, g1)[1] % nnh))
        else:
            b_spec = pl.BlockSpec((tk, tn), lambda g0, g1, k: (k, ij(g0, g1)[1]))
    if out_halves:
        nnh = nn // 2
        o_spec = pl.BlockSpec((None, tm, tn), lambda g0, g1, k: (ij(g0, g1)[1] // nnh, ij(g0, g1)[0], ij(g0, g1)[1] % nnh))
        o_shape = jax.ShapeDtypeStruct((2, M, N // 2), out_dtype)
    else:
        o_spec = pl.BlockSpec((tm, tn), lambda g0, g1, k: ij(g0, g1))
        o_shape = jax.ShapeDtypeStruct((M, N), out_dtype)
    in_specs = [a_spec, b_spec]
    args = [a, b]
    if res is not None:
        in_specs.append(pl.BlockSpec((tm, tn), lambda g0, g1, k: ij(g0, g1)))
        args.append(res)

    def body(*refs):
        if res is not None:
            a_ref, b_ref, r_ref, o_ref = refs[:4]
        else:
            a_ref, b_ref, o_ref = refs[:3]
            r_ref = None
        part = lax.dot_general(a_ref[...], b_ref[...], dims, preferred_element_type=F32)
        if nk == 1:
            if r_ref is not None:
                part = part + r_ref[...]
            o_ref[...] = part.astype(out_dtype)
            return
        acc_ref = refs[-1]
        k = pl.program_id(2)

        @pl.when(k == 0)
        def _():
            acc_ref[...] = part

        @pl.when(k > 0)
        def _():
            acc_ref[...] += part

        @pl.when(k == nk - 1)
        def _():
            tot = acc_ref[...]
            if r_ref is not None:
                tot = tot + r_ref[...]
            o_ref[...] = tot.astype(out_dtype)

    grid = (nm, nn, nk) if outer == "i" else (nn, nm, nk)
    scratch = [] if nk == 1 else [pltpu.VMEM((tm, tn), F32)]
    return pl.pallas_call(
        body, out_shape=o_shape, grid=grid, in_specs=in_specs, out_specs=o_spec, scratch_shapes=scratch,
        name=name, compiler_params=_cp(("parallel", "parallel", "arbitrary"), VMEM_LIMIT),
    )(*args)


def _aug(lane, terms):
    out = 0.0
    for j, t in enumerate(terms):
        out = jnp.where(lane == HEAD_DIM + j, t, out)
    return out


def _split3f(x):
    hi, mid, lo = _split3(x)
    return [hi.astype(F32), mid.astype(F32), lo.astype(F32)]


def _inproj_fwd(x, g_mix, w_pad, bf_pad):
    S = x.shape[0]
    tm = _pick(S, (512, 256))
    tri = (lax.broadcasted_iota(jnp.int32, (tm, tm), 0) >= lax.broadcasted_iota(jnp.int32, (tm, tm), 1)).astype(BF16)

    def body(x_ref, g_ref, w_ref, bf_ref, tri_ref, put_ref, one_ref, xn_ref, qa_ref, ka_ref, va_ref, ug_ref, zf_ref,
             carry_ref):
        i = pl.program_id(0)

        @pl.when(i == 0)
        def _():
            carry_ref[...] = jnp.zeros_like(carry_ref)

        xf = x_ref[...]
        r = lax.rsqrt(jnp.mean(xf * xf, axis=-1, keepdims=True) + EPS)
        xn = ((xf * r) * g_ref[...]).astype(BF16)
        xn_ref[...] = xn
        proj = lax.dot_general(xn, w_ref[...], _NT, preferred_element_type=F32)
        ug_ref[...] = proj[:, QKV:UG_END]
        zf = proj[:, UG_END:] + bf_ref[...]
        zf_ref[...] = zf
        lf = jnp.minimum(zf, 0.0) - jnp.log(1.0 + jnp.exp(-jnp.abs(zf)))
        c = _dot3l(tri_ref[...], lf) + carry_ref[0:1, :]
        carry_ref[0:1, :] = c[tm - 1:tm, :]
        c3 = jnp.concatenate(_split3(c), axis=1)
        aug_q = jnp.dot(c3, put_ref[0], preferred_element_type=F32) + one_ref[0:1, :]
        aug_k = jnp.dot(c3, put_ref[1], preferred_element_type=F32) + one_ref[1:2, :]
        lane = lax.broadcasted_iota(jnp.int32, (tm, LANES), 1)
        for h in range(N_HEADS):
            p, odd = h // 2, h % 2

            def head(base, scale=None, p=p, odd=odd):
                blk = proj[:, base + p * LANES:base + (p + 1) * LANES]
                if scale is not None:
                    blk = blk * scale
                return pltpu.roll(blk, HEAD_DIM, 1) if odd else blk

            cols = slice(h * LANES, (h + 1) * LANES)
            qa_ref[:, cols] = jnp.where(lane < HEAD_DIM, head(0, HEAD_DIM ** -0.5), aug_q[:, cols]).astype(BF16)
            ka_ref[:, cols] = jnp.where(lane < HEAD_DIM, head(ATT_WIDTH), aug_k[:, cols]).astype(BF16)
            va_ref[:, cols] = jnp.where(lane < HEAD_DIM, head(2 * ATT_WIDTH), one_ref[2:3, cols]).astype(BF16)

    wide = N_HEADS * LANES
    src = lax.broadcasted_iota(jnp.int32, (3 * LANES, wide), 0)
    col = lax.broadcasted_iota(jnp.int32, (3 * LANES, wide), 1)
    hd, term = src % LANES, src // LANES
    to_q = (col == hd * LANES + HEAD_DIM + term) & (hd < N_HEADS)
    to_k = (col == hd * LANES + HEAD_DIM + 3 + term) & (hd < N_HEADS)
    put = jnp.stack([to_q.astype(BF16), -to_k.astype(BF16)])
    off = lax.broadcasted_iota(jnp.int32, (8, wide), 1) % LANES - HEAD_DIM
    row = lax.broadcasted_iota(jnp.int32, (8, wide), 0)
    q_one = (off >= 3) & (off < 6)
    k_one = ((off >= 0) & (off < 3)) | ((off >= 6) & (off < 9))
    v_one = (off >= 0) & (off < 3)
    ones = jnp.where(row == 0, q_one, jnp.where(row == 1, k_one, (row == 2) & v_one)).astype(F32)
    return pl.pallas_call(
        body,
        out_shape=(jax.ShapeDtypeStruct((S, D_MODEL), BF16), jax.ShapeDtypeStruct((S, wide), BF16),
                   jax.ShapeDtypeStruct((S, wide), BF16), jax.ShapeDtypeStruct((S, wide), BF16),
                   jax.ShapeDtypeStruct((S, 2 * GMLP_WIDTH), F32), jax.ShapeDtypeStruct((S, LANES), F32)),
        grid=(S // tm,),
        in_specs=[pl.BlockSpec((tm, D_MODEL), lambda i: (i, 0)), pl.BlockSpec((1, D_MODEL), lambda i: (0, 0)),
                  pl.BlockSpec((IN_PAD, D_MODEL), lambda i: (0, 0)), pl.BlockSpec((1, LANES), lambda i: (0, 0)),
                  pl.BlockSpec((tm, tm), lambda i: (0, 0)), pl.BlockSpec((2, 3 * LANES, wide), lambda i: (0, 0, 0)),
                  pl.BlockSpec((8, wide), lambda i: (0, 0))],
        out_specs=(pl.BlockSpec((tm, D_MODEL), lambda i: (i, 0)), pl.BlockSpec((tm, wide), lambda i: (i, 0)),
                   pl.BlockSpec((tm, wide), lambda i: (i, 0)), pl.BlockSpec((tm, wide), lambda i: (i, 0)),
                   pl.BlockSpec((tm, 2 * GMLP_WIDTH), lambda i: (i, 0)), pl.BlockSpec((tm, LANES), lambda i: (i, 0))),
        scratch_shapes=[pltpu.VMEM((8, LANES), F32)],
        name="inproj_fwd", compiler_params=_cp(("arbitrary",), VMEM_LIMIT),
    )(x, g_mix, w_pad, bf_pad, tri, put, ones)


def _group_ones():
    r = lax.broadcasted_iota(jnp.int32, (GMLP_WIDTH, GMLP_WIDTH), 0) // GROUP_DIM
    c = lax.broadcasted_iota(jnp.int32, (GMLP_WIDTH, GMLP_WIDTH), 1) // GROUP_DIM
    return (r == c).astype(BF16)


def _gmlp_mixed(vn_bf, w_ref, bias, n_chunks):
    lane = lax.broadcasted_iota(jnp.int32, (CHUNK, LANES), 1)
    row = lax.broadcasted_iota(jnp.int32, (CHUNK, CHUNK), 0)
    col = lax.broadcasted_iota(jnp.int32, (CHUNK, CHUNK), 1)
    ws = [jnp.where(row >= col, w_ref[g], 0.0).astype(BF16) for g in range(N_GROUPS)]
    rows = []
    for ci in range(n_chunks):
        cols = []
        for pp in range(N_GROUPS // 2):
            v = vn_bf[ci * CHUNK:(ci + 1) * CHUNK, pp * LANES:(pp + 1) * LANES]
            v_lo = jnp.where(lane < GROUP_DIM, v, jnp.zeros_like(v))
            v_hi = jnp.where(lane >= GROUP_DIM, v, jnp.zeros_like(v))
            m = (jnp.dot(ws[2 * pp], v_lo, preferred_element_type=F32)
                 + jnp.dot(ws[2 * pp + 1], v_hi, preferred_element_type=F32))
            cols.append(m + bias[:, pp * LANES:(pp + 1) * LANES])
        rows.append(jnp.concatenate(cols, axis=1))
    return jnp.concatenate(rows, axis=0)


def _gmlp_fwd(ug, gain, w_s, bias_full):
    S = ug.shape[0]
    tm = _pick(S, (512, 256, 128))
    ones = _group_ones()

    def body(ug_ref, gain_ref, w_ref, bias_ref, ones_ref, sg_ref):
        u = _gelu(ug_ref[:, :GMLP_WIDTH])
        vr = _gelu(ug_ref[:, GMLP_WIDTH:])
        ms = _dot3(vr * vr, ones_ref[...]) * (1.0 / GROUP_DIM)
        vn = ((vr * lax.rsqrt(ms + EPS)) * gain_ref[...]).astype(BF16)
        mixed = _gmlp_mixed(vn, w_ref, bias_ref[...], tm // CHUNK)
        sg_ref[...] = (u * mixed).astype(BF16)

    return pl.pallas_call(
        body, out_shape=jax.ShapeDtypeStruct((S, GMLP_WIDTH), BF16), grid=(S // tm,),
        in_specs=[pl.BlockSpec((tm, 2 * GMLP_WIDTH), lambda i: (i, 0)), pl.BlockSpec((1, GMLP_WIDTH), lambda i: (0, 0)),
                  pl.BlockSpec((N_GROUPS, CHUNK, CHUNK), lambda i: (0, 0, 0)),
                  pl.BlockSpec((CHUNK, GMLP_WIDTH), lambda i: (0, 0)),
                  pl.BlockSpec((GMLP_WIDTH, GMLP_WIDTH), lambda i: (0, 0))],
        out_specs=pl.BlockSpec((tm, GMLP_WIDTH), lambda i: (i, 0)),
        name="gmlp_fwd", compiler_params=_cp(("parallel",), VMEM_LIMIT),
    )(ug, gain, w_s, bias_full, ones)


_NT = (((1,), (1,)), ((), ()))
_TN = (((0,), (0,)), ((), ()))


def _attn_fwd(qa, ka, va):
    S = qa.shape[0]
    tq = _pick(S, (ATT_TQ, 256))
    tk = min(ATT_TK, tq)
    nq = S // tq
    assert tq == tk, "the diagonal block is handled as one tq x tq tile"
    per_q = 1

    def body(q_ref, k_ref, v_ref, o_ref, lse_ref, ob_ref):
        qi = pl.program_id(1)
        lane = lax.broadcasted_iota(jnp.int32, (tq, LANES), 1)
        qs = [q_ref[:, :LANES], q_ref[:, LANES:]]

        def update(q, ks, k_len, h, m, acc, first_row):
            cols = slice(h * LANES, (h + 1) * LANES)
            s = lax.dot_general(q, k_ref[pl.ds(ks, k_len), cols], _NT, preferred_element_type=F32)
            if first_row is not None:
                rid = lax.broadcasted_iota(jnp.int32, s.shape, 0) + first_row
                s = jnp.where(rid >= lax.broadcasted_iota(jnp.int32, s.shape, 1), s, -jnp.inf)
            m_new = jnp.maximum(m, jnp.max(s, axis=-1, keepdims=True))
            p = jnp.exp(s - m_new).astype(BF16)
            acc = jnp.exp(m - m_new) * acc + jnp.dot(p, v_ref[pl.ds(ks, k_len), cols], preferred_element_type=F32)
            return m_new, acc

        def step(kb, carry):
            ks = pl.multiple_of(kb * tk, tk)
            return tuple(update(qs[h], ks, tk, h, *carry[h], None) for h in range(2))

        def two_steps(t, carry):
            return step(qi % 2 + 2 * t + 1, step(qi % 2 + 2 * t, carry))

        one = (jnp.full((tq, 1), -jnp.inf, F32), jnp.zeros((tq, LANES), F32))
        carry = lax.fori_loop(0, qi % 2, step, (one, one))
        carry = lax.fori_loop(0, qi // 2, two_steps, carry)
        outs, lses = [], []
        diag = pl.multiple_of(qi * tq, tq)
        for h in range(2):
            m, acc = update(qs[h], diag, tk, h, *carry[h], 0)
            l = acc[:, HEAD_DIM:HEAD_DIM + 1]
            outs.append(acc / l)
            lses.append(m + jnp.log(l))
        o = jnp.where(lane < HEAD_DIM, outs[0], pltpu.roll(outs[1], HEAD_DIM, 1))
        o_ref[...] = o
        ob_ref[...] = o.astype(BF16)
        lse_ref[...] = jnp.where(lane < HEAD_DIM, lses[0], lses[1])

    return pl.pallas_call(
        body,
        out_shape=(jax.ShapeDtypeStruct((S, ATT_WIDTH), F32), jax.ShapeDtypeStruct((S, ATT_WIDTH), F32),
                   jax.ShapeDtypeStruct((S, ATT_WIDTH), BF16)),
        grid=(N_PAIRS, nq),
        in_specs=[pl.BlockSpec((tq, 2 * LANES), lambda p, i: (i, p)),
                  pl.BlockSpec((S, 2 * LANES), lambda p, i: (0, p)),
                  pl.BlockSpec((S, 2 * LANES), lambda p, i: (0, p))],
        out_specs=(pl.BlockSpec((tq, LANES), lambda p, i: (i, p)), pl.BlockSpec((tq, LANES), lambda p, i: (i, p)),
                   pl.BlockSpec((tq, LANES), lambda p, i: (i, p))),
        name="attn_fwd", compiler_params=_cp(("parallel", "parallel"), VMEM_LIMIT),
    )(qa, ka, va)


def _shift_rows(x, prev, n):
    rid = lax.broadcasted_iota(jnp.int32, x.shape, 0)
    y = pltpu.roll(x, n, 0)
    if n == 1:
        return jnp.where(rid == 0, prev[7:8, :], y)
    return jnp.where(rid == 0, prev[6:7, :], jnp.where(rid == 1, prev[7:8, :], y))


def _shift_rows_up(x, nxt, n):
    rows = x.shape[0]
    rid = lax.broadcasted_iota(jnp.int32, x.shape, 0)
    y = pltpu.roll(x, rows - n, 0)
    if n == 1:
        return jnp.where(rid == rows - 1, nxt[0:1, :], y)
    return jnp.where(rid == rows - 2, nxt[0:1, :], jnp.where(rid == rows - 1, nxt[1:2, :], y))


def _conv3(cur, prev, w, b):
    return (w[0:1, :] * _shift_rows(cur, prev, 2) + w[1:2, :] * _shift_rows(cur, prev, 1)
            + w[2:3, :] * cur + b)


def _ffn_up_conv(hn, w_up_bf, cw, cb):
    S = hn.shape[0]
    F = D_FF
    tm = _pick(S, (FFN_TM, 256))
    tn = _pick(F, (FFN_TN, 256, 128))
    nj = F // tn

    def body(hn_ref, wa_ref, wg_ref, cw_ref, cb_ref, hu_ref, hc_ref, act_ref, tail_ref):
        i = pl.program_id(1)

        @pl.when(i == 0)
        def _():
            tail_ref[...] = jnp.zeros_like(tail_ref)

        hn_v = hn_ref[...]
        halves = []
        for h, w_ref in enumerate((wa_ref, wg_ref)):
            hu = lax.dot_general(hn_v, w_ref[...], _NT, preferred_element_type=F32)
            hu_ref[h] = hu.astype(BF16)
            hc = _conv3(hu, tail_ref[h], cw_ref[h], cb_ref[h])
            hc_ref[h] = hc
            halves.append(hc)
            tail_ref[h] = hu[tm - 8:, :]
        a, g = halves
        act_ref[...] = (g * _sigmoid(g) * a).astype(BF16)

    both = pl.BlockSpec((2, tm, tn), lambda j, i: (0, i, j))
    return pl.pallas_call(
        body, out_shape=(jax.ShapeDtypeStruct((2, S, F), BF16), jax.ShapeDtypeStruct((2, S, F), F32),
                         jax.ShapeDtypeStruct((S, F), BF16)),
        grid=(nj, S // tm),
        in_specs=[pl.BlockSpec((tm, D_MODEL), lambda j, i: (i, 0)),
                  pl.BlockSpec((tn, D_MODEL), lambda j, i: (j, 0)),
                  pl.BlockSpec((tn, D_MODEL), lambda j, i: (nj + j, 0)),
                  pl.BlockSpec((2, 8, tn), lambda j, i: (0, 0, j)),
                  pl.BlockSpec((2, 1, tn), lambda j, i: (0, 0, j))],
        out_specs=(both, both, pl.BlockSpec((tm, tn), lambda j, i: (i, j))),
        scratch_shapes=[pltpu.VMEM((2, 8, tn), F32)],
        name="ffn_up_conv", compiler_params=_cp(("parallel", "arbitrary"), VMEM_LIMIT),
    )(hn, w_up_bf, w_up_bf, cw, cb)


def _ffn_down_loss(act, w_down_bf, h1, g_final, target):
    S = h1.shape[0]
    tm = _pick(S, (512, 256))

    def body(a_ref, w_ref, h1_ref, g_ref, t_ref, loss_ref, dh_ref, dhb_ref, dg_ref):
        i = pl.program_id(0)

        @pl.when(i == 0)
        def _():
            loss_ref[...] = jnp.zeros_like(loss_ref)
            dg_ref[...] = jnp.zeros_like(dg_ref)

        hf = h1_ref[...] + jnp.dot(a_ref[...], w_ref[...], preferred_element_type=F32)
        g = g_ref[...]
        r = lax.rsqrt(jnp.mean(hf * hf, axis=-1, keepdims=True) + EPS)
        hhat = hf * r
        err = hhat * g - t_ref[...]
        loss_ref[...] += 0.5 * jnp.sum(jnp.mean(err * err, axis=-1, keepdims=True))
        dy = err * (1.0 / D_MODEL)
        dg_ref[0:1, :] += jnp.sum(dy * hhat, axis=0, keepdims=True)
        dhat = dy * g
        dh = r * (dhat - hhat * jnp.mean(dhat * hhat, axis=-1, keepdims=True))
        dh_ref[...] = dh
        dhb_ref[...] = dh.astype(BF16)

    row = pl.BlockSpec((tm, D_MODEL), lambda i: (i, 0))
    return pl.pallas_call(
        body,
        out_shape=(jax.ShapeDtypeStruct((8, LANES), F32), jax.ShapeDtypeStruct((S, D_MODEL), F32),
                   jax.ShapeDtypeStruct((S, D_MODEL), BF16), jax.ShapeDtypeStruct((8, D_MODEL), F32)),
        grid=(S // tm,),
        in_specs=[pl.BlockSpec((tm, D_FF), lambda i: (i, 0)), pl.BlockSpec((D_FF, D_MODEL), lambda i: (0, 0)), row,
                  pl.BlockSpec((1, D_MODEL), lambda i: (0, 0)), row],
        out_specs=(pl.BlockSpec((8, LANES), lambda i: (0, 0)), row, row, pl.BlockSpec((8, D_MODEL), lambda i: (0, 0))),
        name="ffn_down_loss", compiler_params=_cp(("arbitrary",), VMEM_LIMIT),
    )(act, w_down_bf, h1, g_final, target)


def _ffn_up_dx_rms(dhu, w_up_bf, h1, g_ffn, dh2):
    _, S, F = dhu.shape
    tm = _pick(S, (512, 256))

    def body(a_ref, b_ref, h_ref, g_ref, r_ref, dh_ref, dhb_ref, dg_ref):
        i = pl.program_id(0)

        @pl.when(i == 0)
        def _():
            dg_ref[...] = jnp.zeros_like(dg_ref)

        dyv = (jnp.dot(a_ref[0], b_ref[:F, :], preferred_element_type=F32)
               + jnp.dot(a_ref[1], b_ref[F:, :], preferred_element_type=F32))
        hf = h_ref[...]
        r = lax.rsqrt(jnp.mean(hf * hf, axis=-1, keepdims=True) + EPS)
        hhat = hf * r
        dg_ref[0:1, :] += jnp.sum(dyv * hhat, axis=0, keepdims=True)
        dhat = dyv * g_ref[...]
        dh = r_ref[...] + r * (dhat - hhat * jnp.mean(dhat * hhat, axis=-1, keepdims=True))
        dh_ref[...] = dh
        dhb_ref[...] = dh.astype(BF16)

    row = pl.BlockSpec((tm, D_MODEL), lambda i: (i, 0))
    return pl.pallas_call(
        body,
        out_shape=(jax.ShapeDtypeStruct((S, D_MODEL), F32), jax.ShapeDtypeStruct((S, D_MODEL), BF16),
                   jax.ShapeDtypeStruct((8, D_MODEL), F32)),
        grid=(S // tm,),
        in_specs=[pl.BlockSpec((2, tm, F), lambda i: (0, i, 0)), pl.BlockSpec((2 * F, D_MODEL), lambda i: (0, 0)),
                  row, pl.BlockSpec((1, D_MODEL), lambda i: (0, 0)), row],
        out_specs=(row, row, pl.BlockSpec((8, D_MODEL), lambda i: (0, 0))),
        name="ffn_up_dx_rms", compiler_params=_cp(("arbitrary",), VMEM_LIMIT),
    )(dhu, w_up_bf, h1, g_ffn, dh2)


def _conv_gate_bwd(hc, hu, dact, cw):
    _, S, F = hu.shape
    tm = _pick(S, (CONV_TM, 128))
    tn = _pick(F, (CONV_TN, 256, 128))
    r8 = tm // 8
    n_i = S // tm
    last8 = S // 8 - 1

    def body(hc_ref, hcn_ref, hu_ref, da_ref, dan_ref, w_ref, dhu_ref, dcw_ref):
        i = pl.program_id(1)

        @pl.when(i == 0)
        def _():
            dcw_ref[...] = jnp.zeros_like(dcw_ref)

        rid8 = lax.broadcasted_iota(jnp.int32, (8, tn), 0)

        def gate_grads(a, g, d):
            sg = _sigmoid(g)
            return d * (g * sg), d * a * (sg * (1.0 + g * (1.0 - sg)))

        dhc = gate_grads(hc_ref[0], hc_ref[1], da_ref[...])
        dhc_n = gate_grads(hcn_ref[0], hcn_ref[1], dan_ref[...])
        for h in range(2):
            w = w_ref[h]
            d = dhc[h]
            dn = jnp.where(i < n_i - 1, dhc_n[h], 0.0)
            u1 = _shift_rows_up(d, dn, 1)
            u2 = _shift_rows_up(d, dn, 2)
            dhu_ref[h] = (w[2:3, :] * d + w[1:2, :] * u1 + w[0:1, :] * u2).astype(BF16)
            x = hu_ref[h].astype(F32)
            t0, t1, t2, t3 = [jnp.sum(t, axis=0, keepdims=True) for t in (u2 * x, u1 * x, d * x, d)]
            dcw_ref[h] += jnp.where(rid8 == 0, t0, jnp.where(rid8 == 1, t1, jnp.where(rid8 == 2, t2, jnp.where(rid8 == 3, t3, 0.0))))

    cur = pl.BlockSpec((2, tm, tn), lambda j, i: (0, i, j))
    return pl.pallas_call(
        body,
        out_shape=(jax.ShapeDtypeStruct((2, S, F), BF16), jax.ShapeDtypeStruct((2, 8, F), F32)),
        grid=(F // tn, n_i),
        in_specs=[cur, pl.BlockSpec((2, 8, tn), lambda j, i: (0, jnp.minimum((i + 1) * r8, last8), j)), cur,
                  pl.BlockSpec((tm, tn), lambda j, i: (i, j)),
                  pl.BlockSpec((8, tn), lambda j, i: (jnp.minimum((i + 1) * r8, last8), j)),
                  pl.BlockSpec((2, 8, tn), lambda j, i: (0, 0, j))],
        out_specs=(cur, pl.BlockSpec((2, 8, tn), lambda j, i: (0, 0, j))),
        name="conv_gate_bwd", compiler_params=_cp(("parallel", "arbitrary"), VMEM_LIMIT),
    )(hc, hc, hu, dact, dact, cw)


def _out_proj_dx_prep(dh1_bf, w_out_bf, att, lse, qa):
    S = att.shape[0]
    tm = _pick(S, (256,))

    def body(dh_ref, w_ref, o_ref, lse_ref, q_ref, dsg_ref, qb_ref, doa_ref):
        lane = lax.broadcasted_iota(jnp.int32, (tm, LANES), 1)
        dh = dh_ref[...]
        dsg_ref[...] = lax.dot_general(dh, w_ref[ATT_WIDTH:, :], _NT, preferred_element_type=F32)
        datt = lax.dot_general(dh, w_ref[:ATT_WIDTH, :], _NT, preferred_element_type=F32)
        for p in range(N_PAIRS):
            pc = slice(p * LANES, (p + 1) * LANES)
            do = datt[:, pc]
            prod = o_ref[:, pc] * do
            for hh in range(2):
                sel = (lane >= HEAD_DIM) if hh else (lane < HEAD_DIM)
                delta = jnp.sum(jnp.where(sel, prod, 0.0), axis=-1, keepdims=True)
                dod = pltpu.roll(do, HEAD_DIM, 1) if hh else do
                cols = slice((2 * p + hh) * LANES, (2 * p + hh + 1) * LANES)
                doa_ref[:, cols] = jnp.where(lane < HEAD_DIM, dod, _aug(lane, _split3f(-delta))).astype(BF16)
                lcol = p * LANES + hh * HEAD_DIM
                l3 = _split3f(-lse_ref[:, lcol:lcol + 1])
                augl = jnp.where(lane == HEAD_DIM + 6, l3[0], jnp.where(lane == HEAD_DIM + 7, l3[1], l3[2])).astype(BF16)
                qb_ref[:, cols] = jnp.where((lane >= HEAD_DIM + 6) & (lane < HEAD_DIM + 9), augl, q_ref[:, cols])

    half = pl.BlockSpec((tm, ATT_WIDTH), lambda i: (i, 0))
    wide = pl.BlockSpec((tm, N_HEADS * LANES), lambda i: (i, 0))
    return pl.pallas_call(
        body,
        out_shape=(jax.ShapeDtypeStruct((S, GMLP_WIDTH), F32), jax.ShapeDtypeStruct(qa.shape, BF16),
                   jax.ShapeDtypeStruct(qa.shape, BF16)),
        grid=(S // tm,),
        in_specs=[pl.BlockSpec((tm, D_MODEL), lambda i: (i, 0)), pl.BlockSpec((D_MODEL, D_MODEL), lambda i: (0, 0)),
                  half, half, wide],
        out_specs=(half, wide, wide),
        name="out_proj_dx_prep", compiler_params=_cp(("parallel",), VMEM_LIMIT),
    )(dh1_bf, w_out_bf, att, lse, qa)


def _attn_bwd(qb, ka, va, doa):
    S = qb.shape[0]
    tk = _pick(S, (512, 256))
    tq = tk
    nq = S // tq

    def pair(a, scale=None):
        lane = lax.broadcasted_iota(jnp.int32, (a.shape[0], LANES), 1)
        out = jnp.where(lane < HEAD_DIM, a[:, :LANES], pltpu.roll(a[:, LANES:], HEAD_DIM, 1))
        return out if scale is None else out * scale

    def head_lanes(a, col, sign, first):
        lane = lax.broadcasted_iota(jnp.int32, (a.shape[0], LANES), 1)
        return jnp.where(lane == first, sign * a[:, col:col + 1],
                         jnp.where(lane == first + 1, sign * a[:, LANES + col:LANES + col + 1], 0.0))

    def body(q_ref, do_ref, k_ref, v_ref, dqc_ref, dkc_ref, dvc_ref, dcq_ref, dck_ref, dq_ref, dka_ref, dva_ref):
        kb = pl.program_id(1)

        @pl.when(kb == 0)
        def _():
            dq_ref[...] = jnp.zeros_like(dq_ref)

        dka_ref[...] = jnp.zeros_like(dka_ref)
        dva_ref[...] = jnp.zeros_like(dva_ref)
        def sub_tile(qs, q_len, k_off, k_len, masked):
            keys = slice(k_off, k_off + k_len)
            for h in range(2):
                cols = slice(h * LANES, (h + 1) * LANES)
                qblk = q_ref[pl.ds(qs, q_len), cols]
                doblk = do_ref[pl.ds(qs, q_len), cols]
                kh = k_ref[keys, cols]
                p = jnp.exp(lax.dot_general(kh, qblk, _NT, preferred_element_type=F32))
                if masked:
                    p = jnp.where(lax.broadcasted_iota(jnp.int32, p.shape, 1) >= lax.broadcasted_iota(jnp.int32, p.shape, 0),
                                  p, 0.0)
                ds = (p * lax.dot_general(v_ref[keys, cols], doblk, _NT, preferred_element_type=F32)).astype(BF16)
                dva_ref[keys, cols] += jnp.dot(p.astype(BF16), doblk, preferred_element_type=F32)
                dka_ref[keys, cols] += jnp.dot(ds, qblk, preferred_element_type=F32)
                dq_ref[pl.ds(qs, q_len), cols] += lax.dot_general(ds, kh, _TN, preferred_element_type=F32)

        half = tk // 2
        sub_tile(pl.multiple_of(kb * tq, tq), tq, 0, half, True)
        sub_tile(pl.multiple_of(kb * tq + half, half), half, half, half, True)

        rest = nq - 1 - kb
        odd = rest % 2

        @pl.when(odd == 1)
        def _():
            sub_tile(pl.multiple_of((kb + 1) * tq, tq), tq, 0, tk, False)

        def step(t, carry):
            sub_tile(pl.multiple_of((kb + 1 + odd + 2 * t) * tq, tq), 2 * tq, 0, tk, False)
            return carry

        lax.fori_loop(0, rest // 2, step, 0)
        dka = dka_ref[...]
        dkc_ref[...] = pair(dka).astype(BF16)
        dvc_ref[...] = pair(dva_ref[...]).astype(BF16)
        first = 2 * pl.program_id(0)
        dck_ref[...] = head_lanes(dka, HEAD_DIM + 3, -1.0, first)

        @pl.when(kb == nq - 1)
        def _():
            dqa = dq_ref[...]
            dqc_ref[...] = pair(dqa, HEAD_DIM ** -0.5).astype(BF16)
            dcq_ref[...] = head_lanes(dqa, HEAD_DIM, 1.0, first)

    wide = 2 * LANES
    half = jax.ShapeDtypeStruct((S, ATT_WIDTH), BF16)
    slabs = jax.ShapeDtypeStruct((N_PAIRS, S, LANES), F32)
    return pl.pallas_call(
        body,
        out_shape=(half, half, half, slabs, slabs),
        grid=(N_PAIRS, nq),
        in_specs=[pl.BlockSpec((S, wide), lambda p, j: (0, p)), pl.BlockSpec((S, wide), lambda p, j: (0, p)),
                  pl.BlockSpec((tk, wide), lambda p, j: (j, p)), pl.BlockSpec((tk, wide), lambda p, j: (j, p))],
        out_specs=(pl.BlockSpec((S, LANES), lambda p, j: (0, p)), pl.BlockSpec((tk, LANES), lambda p, j: (j, p)),
                   pl.BlockSpec((tk, LANES), lambda p, j: (j, p)), pl.BlockSpec((None, S, LANES), lambda p, j: (p, 0, 0)),
                   pl.BlockSpec((None, tk, LANES), lambda p, j: (p, j, 0))),
        scratch_shapes=[pltpu.VMEM((S, wide), F32), pltpu.VMEM((tk, wide), F32), pltpu.VMEM((tk, wide), F32)],
        name="attn_bwd", compiler_params=_cp(("parallel", "arbitrary"), VMEM_LIMIT),
    )(qb, doa, ka, va)


def _gmlp_bwd(ug, dsg, gain, w_s, wt_s, bias_full):
    S = ug.shape[0]
    tm = _pick(S, (512, 256, 128))
    n_chunks = tm // CHUNK
    n_i = S // tm
    ones = _group_ones()
    nt = (((1,), (1,)), ((), ()))

    def body(ug_ref, dsg_ref, gain_ref, w_ref, wt_ref, bias_ref, ones_ref, dug_ref, dw_ref, dgain_ref, dbias_ref,
             dbacc_ref):
        i = pl.program_id(0)

        @pl.when(i == 0)
        def _():
            dw_ref[...] = jnp.zeros_like(dw_ref)
            dgain_ref[...] = jnp.zeros_like(dgain_ref)
            dbacc_ref[...] = jnp.zeros_like(dbacc_ref)

        ones_m = ones_ref[...]
        pu = ug_ref[:, :GMLP_WIDTH]
        pg = ug_ref[:, GMLP_WIDTH:]
        u = _gelu(pu)
        vr = _gelu(pg)
        ms = _dot3(vr * vr, ones_m) * (1.0 / GROUP_DIM)
        rinv = lax.rsqrt(ms + EPS)
        vhat = vr * rinv
        gain_v = gain_ref[...]
        vn = (vhat * gain_v).astype(BF16)
        mixed = _gmlp_mixed(vn, w_ref, bias_ref[...], n_chunks)
        dsg_v = dsg_ref[...]
        du = dsg_v * mixed
        dmixed = dsg_v * u
        dm_bf = dmixed.astype(BF16)
        lane = lax.broadcasted_iota(jnp.int32, (CHUNK, LANES), 1)
        row = lax.broadcasted_iota(jnp.int32, (CHUNK, CHUNK), 0)
        col = lax.broadcasted_iota(jnp.int32, (CHUNK, CHUNK), 1)
        wts = [jnp.where(col >= row, wt_ref[g], 0.0).astype(BF16) for g in range(N_GROUPS)]
        dvn_rows = []
        dbsum = jnp.zeros((CHUNK, GMLP_WIDTH), F32)
        for ci in range(n_chunks):
            rs = slice(ci * CHUNK, (ci + 1) * CHUNK)
            dbsum = dbsum + dmixed[rs, :]
            cols = []
            for pp in range(N_GROUPS // 2):
                cs = slice(pp * LANES, (pp + 1) * LANES)
                dm = dm_bf[rs, cs]
                dm_lo = jnp.where(lane < GROUP_DIM, dm, jnp.zeros_like(dm))
                dm_hi = jnp.where(lane >= GROUP_DIM, dm, jnp.zeros_like(dm))
                vb = vn[rs, cs]
                dw_ref[2 * pp] += lax.dot_general(dm_lo, vb, nt, preferred_element_type=F32)
                dw_ref[2 * pp + 1] += lax.dot_general(dm_hi, vb, nt, preferred_element_type=F32)
                cols.append(jnp.dot(wts[2 * pp], dm_lo, preferred_element_type=F32)
                            + jnp.dot(wts[2 * pp + 1], dm_hi, preferred_element_type=F32))
            dvn_rows.append(jnp.concatenate(cols, axis=1))
        dvn = jnp.concatenate(dvn_rows, axis=0)
        dbacc_ref[...] += dbsum
        dgain_ref[0:1, :] += jnp.sum(dvn * vhat, axis=0, keepdims=True)
        dvhat = dvn * gain_v
        gm = _dot3(dvhat * vhat, ones_m) * (1.0 / GROUP_DIM)
        dvr = rinv * (dvhat - vhat * gm)
        dug_ref[:, :GMLP_WIDTH] = (du * _gelu_grad(pu)).astype(BF16)
        dug_ref[:, GMLP_WIDTH:] = (dvr * _gelu_grad(pg)).astype(BF16)

        @pl.when(i == n_i - 1)
        def _():
            for g in range(N_GROUPS):
                dw_ref[g] = jnp.where(row >= col, dw_ref[g], 0.0)
            dbias_ref[...] = _dot3(dbacc_ref[...], ones_m)

    return pl.pallas_call(
        body,
        out_shape=(jax.ShapeDtypeStruct((S, 2 * GMLP_WIDTH), BF16), jax.ShapeDtypeStruct((N_GROUPS, CHUNK, CHUNK), F32),
                   jax.ShapeDtypeStruct((8, GMLP_WIDTH), F32), jax.ShapeDtypeStruct((CHUNK, GMLP_WIDTH), F32)),
        grid=(n_i,),
        in_specs=[pl.BlockSpec((tm, 2 * GMLP_WIDTH), lambda i: (i, 0)), pl.BlockSpec((tm, GMLP_WIDTH), lambda i: (i, 0)),
                  pl.BlockSpec((1, GMLP_WIDTH), lambda i: (0, 0)),
                  pl.BlockSpec((N_GROUPS, CHUNK, CHUNK), lambda i: (0, 0, 0)),
                  pl.BlockSpec((N_GROUPS, CHUNK, CHUNK), lambda i: (0, 0, 0)),
                  pl.BlockSpec((CHUNK, GMLP_WIDTH), lambda i: (0, 0)),
                  pl.BlockSpec((GMLP_WIDTH, GMLP_WIDTH), lambda i: (0, 0))],
        out_specs=(pl.BlockSpec((tm, 2 * GMLP_WIDTH), lambda i: (i, 0)),
                   pl.BlockSpec((N_GROUPS, CHUNK, CHUNK), lambda i: (0, 0, 0)),
                   pl.BlockSpec((8, GMLP_WIDTH), lambda i: (0, 0)),
                   pl.BlockSpec((CHUNK, GMLP_WIDTH), lambda i: (0, 0))),
        scratch_shapes=[pltpu.VMEM((CHUNK, GMLP_WIDTH), F32)],
        name="gmlp_bwd", compiler_params=_cp(("arbitrary",), VMEM_LIMIT),
    )(ug, dsg, gain, w_s, wt_s, bias_full, ones)


def _gate_bwd(dcq, dck, zf):
    S = zf.shape[0]
    tm = _pick(S, (256,))
    n_i = S // tm
    triu = (lax.broadcasted_iota(jnp.int32, (tm, tm), 0) <= lax.broadcasted_iota(jnp.int32, (tm, tm), 1)).astype(BF16)

    def body(dcq_ref, dck_ref, zf_ref, tri_ref, dzf_ref, dbf_ref, carry_ref):
        i = pl.program_id(0)

        @pl.when(i == 0)
        def _():
            carry_ref[...] = jnp.zeros_like(carry_ref)
            dbf_ref[...] = jnp.zeros_like(dbf_ref)

        lane = lax.broadcasted_iota(jnp.int32, (tm, LANES), 1)
        dc = dcq_ref[0] + dck_ref[0]
        for p in range(1, N_PAIRS):
            dc = dc + (dcq_ref[p] + dck_ref[p])
        dlf = _dot3l(tri_ref[...], dc) + carry_ref[0:1, :]
        carry_ref[0:1, :] = dlf[0:1, :]
        dz = jnp.where(lane < N_HEADS, dlf * _sigmoid(-zf_ref[...]), 0.0)
        dzf_ref[...] = dz.astype(BF16)
        dbf_ref[0:1, :] += jnp.sum(dz, axis=0, keepdims=True)

    return pl.pallas_call(
        body,
        out_shape=(jax.ShapeDtypeStruct((S, LANES), BF16), jax.ShapeDtypeStruct((8, LANES), F32)),
        grid=(n_i,),
        in_specs=[pl.BlockSpec((N_PAIRS, tm, LANES), lambda i: (0, n_i - 1 - i, 0)),
                  pl.BlockSpec((N_PAIRS, tm, LANES), lambda i: (0, n_i - 1 - i, 0)),
                  pl.BlockSpec((tm, LANES), lambda i: (n_i - 1 - i, 0)),
                  pl.BlockSpec((tm, tm), lambda i: (0, 0))],
        out_specs=(pl.BlockSpec((tm, LANES), lambda i: (n_i - 1 - i, 0)), pl.BlockSpec((8, LANES), lambda i: (0, 0))),
        scratch_shapes=[pltpu.VMEM((8, LANES), F32)],
        name="gate_bwd", compiler_params=_cp(("arbitrary",), VMEM_LIMIT),
    )(dcq, dck, zf, triu)


def _out_proj_fwd(att_bf, sg, w_out_bf, x, g_ffn):
    S = x.shape[0]
    tm = _pick(S, (512, 256))

    def body(a_ref, s_ref, w_ref, x_ref, g_ref, h_ref, hn_ref):
        h = (x_ref[...] + jnp.dot(a_ref[...], w_ref[:ATT_WIDTH, :], preferred_element_type=F32)
             + jnp.dot(s_ref[...], w_ref[ATT_WIDTH:, :], preferred_element_type=F32))
        h_ref[...] = h
        r = lax.rsqrt(jnp.mean(h * h, axis=-1, keepdims=True) + EPS)
        hn_ref[...] = ((h * r) * g_ref[...]).astype(BF16)

    row = pl.BlockSpec((tm, D_MODEL), lambda i: (i, 0))
    half = pl.BlockSpec((tm, ATT_WIDTH), lambda i: (i, 0))
    return pl.pallas_call(
        body, out_shape=(jax.ShapeDtypeStruct((S, D_MODEL), F32), jax.ShapeDtypeStruct((S, D_MODEL), BF16)),
        grid=(S // tm,),
        in_specs=[half, half, pl.BlockSpec((D_MODEL, D_MODEL), lambda i: (0, 0)), row,
                  pl.BlockSpec((1, D_MODEL), lambda i: (0, 0))],
        out_specs=(row, row), name="out_proj", compiler_params=_cp(("parallel",), VMEM_LIMIT),
    )(att_bf, sg, w_out_bf, x, g_ffn)


def _out_proj_dw(att_bf, sg, dh1_bf):
    S = att_bf.shape[0]
    tk = _pick(S, (1024, 512))

    def body(a_ref, s_ref, d_ref, o_ref):
        k = pl.program_id(0)

        @pl.when(k == 0)
        def _():
            o_ref[...] = jnp.zeros_like(o_ref)

        d = d_ref[...]
        o_ref[:ATT_WIDTH, :] += lax.dot_general(a_ref[...], d, _TN, preferred_element_type=F32)
        o_ref[ATT_WIDTH:, :] += lax.dot_general(s_ref[...], d, _TN, preferred_element_type=F32)

    half = pl.BlockSpec((tk, ATT_WIDTH), lambda k: (k, 0))
    return pl.pallas_call(
        body, out_shape=jax.ShapeDtypeStruct((D_MODEL, D_MODEL), F32), grid=(S // tk,),
        in_specs=[half, half, pl.BlockSpec((tk, D_MODEL), lambda k: (k, 0))],
        out_specs=pl.BlockSpec((D_MODEL, D_MODEL), lambda k: (0, 0)),
        name="out_proj_dw", compiler_params=_cp(("arbitrary",), VMEM_LIMIT),
    )(att_bf, sg, dh1_bf)


_IN_PIECES = ((0, ATT_WIDTH), (ATT_WIDTH, ATT_WIDTH), (2 * ATT_WIDTH, ATT_WIDTH), (QKV, 2 * GMLP_WIDTH), (UG_END, LANES))


def _inproj_bwd_dx(pieces, w_pad, x, g_mix, dh1):
    S = x.shape[0]
    tm = _pick(S, (512, 256))

    def body(*refs):
        p_refs, (w_ref, x_ref, g_ref, r_ref, dx_ref, dg_ref) = refs[:5], refs[5:]
        i = pl.program_id(0)

        @pl.when(i == 0)
        def _():
            dg_ref[...] = jnp.zeros_like(dg_ref)

        dxn = None
        for p_ref, (c0, width) in zip(p_refs, _IN_PIECES):
            part = jnp.dot(p_ref[...], w_ref[c0:c0 + width, :], preferred_element_type=F32)
            dxn = part if dxn is None else dxn + part
        xf = x_ref[...]
        r = lax.rsqrt(jnp.mean(xf * xf, axis=-1, keepdims=True) + EPS)
        xhat = xf * r
        dg_ref[0:1, :] += jnp.sum(dxn * xhat, axis=0, keepdims=True)
        dhat = dxn * g_ref[...]
        dx_ref[...] = r_ref[...] + r * (dhat - xhat * jnp.mean(dhat * xhat, axis=-1, keepdims=True))

    row = pl.BlockSpec((tm, D_MODEL), lambda i: (i, 0))
    return pl.pallas_call(
        body, out_shape=(jax.ShapeDtypeStruct((S, D_MODEL), F32), jax.ShapeDtypeStruct((8, D_MODEL), F32)),
        grid=(S // tm,),
        in_specs=[pl.BlockSpec((tm, width), lambda i: (i, 0)) for _, width in _IN_PIECES]
        + [pl.BlockSpec((IN_PAD, D_MODEL), lambda i: (0, 0)), row, pl.BlockSpec((1, D_MODEL), lambda i: (0, 0)), row],
        out_specs=(row, pl.BlockSpec((8, D_MODEL), lambda i: (0, 0))),
        name="in_proj_dx", compiler_params=_cp(("arbitrary",), VMEM_LIMIT),
    )(*pieces, w_pad, x, g_mix, dh1)


def _inproj_bwd_dw(xn, pieces):
    S = xn.shape[0]
    tk = _pick(S, (1024, 512))

    def body(*refs):
        x_ref, p_refs, o_ref = refs[0], refs[1:6], refs[6]
        k = pl.program_id(0)

        @pl.when(k == 0)
        def _():
            o_ref[...] = jnp.zeros_like(o_ref)

        xb = x_ref[...]
        for p_ref, (c0, width) in zip(p_refs, _IN_PIECES):
            o_ref[:, c0:c0 + width] += lax.dot_general(xb, p_ref[...], _TN, preferred_element_type=F32)

    return pl.pallas_call(
        body, out_shape=jax.ShapeDtypeStruct((D_MODEL, IN_PAD), F32), grid=(S // tk,),
        in_specs=[pl.BlockSpec((tk, D_MODEL), lambda k: (k, 0))]
        + [pl.BlockSpec((tk, width), lambda k: (k, 0)) for _, width in _IN_PIECES],
        out_specs=pl.BlockSpec((D_MODEL, IN_PAD), lambda k: (0, 0)),
        name="in_proj_dw", compiler_params=_cp(("arbitrary",), VMEM_LIMIT),
    )(xn, *pieces)


def _adamw(w, m, v, parts, name):
    R, C = w.shape[-2:]
    tr = R
    for cand in (256, 128, 64, 32, 16, 8):
        if R % cand == 0 and R > cand:
            tr = cand
            break
    c1 = 1.0 / (1.0 - ADAM_B1 ** ADAM_STEP)
    c2 = 1.0 / (1.0 - ADAM_B2 ** ADAM_STEP)

    def body(w_ref, m_ref, v_ref, p_ref, g_ref, d_ref, nm_ref, nv_ref):
        g = p_ref[0].astype(F32)
        for j in range(1, N_DEV):
            g = g + p_ref[j].astype(F32)
        g_ref[...] = g
        nm = ADAM_B1 * m_ref[...] + (1.0 - ADAM_B1) * g
        nv = ADAM_B2 * v_ref[...] + (1.0 - ADAM_B2) * (g * g)
        nm_ref[...] = nm
        nv_ref[...] = nv
        d_ref[...] = -ADAM_LR * ((nm * c1) / (jnp.sqrt(nv * c2) + ADAM_EPS) + ADAM_WD * w_ref[...])

    if w.ndim == 3:
        spec = pl.BlockSpec((None, tr, C), lambda i: (0, i, 0))
    else:
        spec = pl.BlockSpec((tr, C), lambda i: (i, 0))
    shp = jax.ShapeDtypeStruct(w.shape, F32)
    return pl.pallas_call(
        body, out_shape=(shp, shp, shp, shp), grid=(R // tr,),
        in_specs=[spec, spec, spec, pl.BlockSpec((N_DEV, tr, C), lambda i: (0, i, 0))],
        out_specs=(spec, spec, spec, spec),
        name=name, compiler_params=_cp(("parallel",), VMEM_LIMIT),
    )(w, m, v, parts)


def _adamw_owner(w, m, v, landed, sent, me, name):
    R, C = w.shape[-2:]
    tr = R
    for cand in (256, 128, 64, 32, 16, 8):
        if R % cand == 0 and R > cand:
            tr = cand
            break
    c1 = 1.0 / (1.0 - ADAM_B1 ** ADAM_STEP)
    c2 = 1.0 / (1.0 - ADAM_B2 ** ADAM_STEP)

    def body(me_ref, w_ref, m_ref, v_ref, p_ref, own_ref, g_ref, d_ref, nm_ref, nv_ref):
        mine = me_ref[0]
        own = own_ref[...].astype(F32)
        g = jnp.where(mine == 0, own, p_ref[0].astype(F32))
        for j in range(1, N_DEV):
            g = g + jnp.where(mine == j, own, p_ref[j].astype(F32))
        g_ref[...] = g
        nm = ADAM_B1 * m_ref[...] + (1.0 - ADAM_B1) * g
        nv = ADAM_B2 * v_ref[...] + (1.0 - ADAM_B2) * (g * g)
        nm_ref[...] = nm
        nv_ref[...] = nv
        d_ref[...] = -ADAM_LR * ((nm * c1) / (jnp.sqrt(nv * c2) + ADAM_EPS) + ADAM_WD * w_ref[...])

    spec = pl.BlockSpec((None, tr, C), lambda i, me_ref: (0, i, 0))
    shp = jax.ShapeDtypeStruct(w.shape, F32)
    return pl.pallas_call(
        body, out_shape=(shp, shp, shp, shp),
        grid_spec=pltpu.PrefetchScalarGridSpec(
            num_scalar_prefetch=1, grid=(R // tr,),
            in_specs=[spec, spec, spec, pl.BlockSpec((N_DEV, tr, C), lambda i, me_ref: (0, i, 0)),
                      pl.BlockSpec((None, tr, C), lambda i, me_ref: (me_ref[0], i, 0))],
            out_specs=(spec, spec, spec, spec)),
        name=name, compiler_params=_cp(("parallel",), VMEM_LIMIT),
    )(jnp.reshape(me, (1,)).astype(jnp.int32), w, m, v, landed, sent)


def _adamw_packed(w, m, v, parts, sizes, name):
    R = w.shape[0]
    assert R == sum(sizes)
    c1 = 1.0 / (1.0 - ADAM_B1 ** ADAM_STEP)
    c2 = 1.0 / (1.0 - ADAM_B2 ** ADAM_STEP)
    n = len(sizes)

    def body(w_ref, m_ref, v_ref, p_ref, *out_refs):
        g = p_ref[0]
        for j in range(1, N_DEV):
            g = g + p_ref[j]
        nm = ADAM_B1 * m_ref[...] + (1.0 - ADAM_B1) * g
        nv = ADAM_B2 * v_ref[...] + (1.0 - ADAM_B2) * (g * g)
        d = -ADAM_LR * ((nm * c1) / (jnp.sqrt(nv * c2) + ADAM_EPS) + ADAM_WD * w_ref[...])
        for kind, val in enumerate((g, d, nm, nv)):
            off = 0
            for k, rows in enumerate(sizes):
                out_refs[kind * n + k][...] = val[off:off + rows, :]
                off += rows

    whole = pl.BlockSpec((R, LANES), lambda i: (0, 0))
    shapes = [jax.ShapeDtypeStruct((rows, LANES), F32) for rows in sizes] * 4
    res = pl.pallas_call(
        body, out_shape=tuple(shapes), grid=(1,),
        in_specs=[whole, whole, whole, pl.BlockSpec((N_DEV, R, LANES), lambda i: (0, 0, 0))],
        out_specs=tuple(pl.BlockSpec((rows, LANES), lambda i: (0, 0)) for rows in sizes) * 4,
        name=name, compiler_params=_cp(("arbitrary",), VMEM_LIMIT),
    )(w, m, v, parts)
    return [list(res[kind * n:(kind + 1) * n]) for kind in range(4)]


def _place():
    x, y, c = lax.axis_index("x"), lax.axis_index("y"), lax.axis_index("c")
    return x, y, c


def _all_gather(blocks, name):
    n = len(blocks)

    def body(*refs):
        ins, outs = refs[:n], refs[n:2 * n]
        send_sems, recv_sems, local_sems = refs[2 * n:]
        x, y, c = _place()
        me, sibling = (x, y, c), (x, y, 1 - c)
        chips = [(1 - x, y), (x, 1 - y), (1 - x, 1 - y)]
        sends = []
        for a in range(n):
            out = outs[a]

            def slot(px, py, pc, out=out):
                return out.at[4 * px + 2 * py + pc]

            def copy(k, block, to, src=None, a=a, slot=slot):
                return pltpu.make_async_remote_copy(
                    src_ref=slot(*block) if src is None else src, dst_ref=slot(*block),
                    send_sem=send_sems.at[a, k], recv_sem=recv_sems.at[a, k], device_id=to, device_id_type=MESH)

            mine = pltpu.make_async_copy(ins[a], slot(*me), local_sems.at[a])
            mine.start()
            first = [copy(0, me, sibling, src=ins[a])]
            first += [copy(1 + j, me, (*chip, c), src=ins[a]) for j, chip in enumerate(chips)]
            for cp in first:
                cp.start()
            sends.append((mine, first, copy))
        for a in range(n):
            mine, first, copy = sends[a]
            passed = [copy(4 + j, (*chip, c), sibling) for j, chip in enumerate(chips)]
            for j, chip in enumerate(chips):
                copy(1 + j, (*chip, c), me).wait_recv()
                passed[j].start()
            copy(0, sibling, me).wait_recv()
            for j, chip in enumerate(chips):
                copy(4 + j, (*chip, 1 - c), me).wait_recv()
            for cp in first + passed:
                cp.wait_send()
            mine.wait()

    any_spec = pl.BlockSpec(memory_space=pl.ANY)
    return pl.pallas_call(
        body, out_shape=tuple(jax.ShapeDtypeStruct((N_DEV,) + b.shape, b.dtype) for b in blocks),
        in_specs=[any_spec] * n, out_specs=tuple([any_spec] * n),
        scratch_shapes=[pltpu.SemaphoreType.DMA((n, 7)), pltpu.SemaphoreType.DMA((n, 7)), pltpu.SemaphoreType.DMA((n,))],
        name=name,
    )(*blocks)


_HBM = pl.BlockSpec(memory_space=pltpu.HBM)
_SEM = pl.BlockSpec(memory_space=pltpu.SEMAPHORE)
_EFFECT = pltpu.SideEffectType.DATAFLOW_SIDE_EFFECTING


def _peers(x, y, c):
    out = []
    for k in range(1, N_DEV):
        px, py, pc = x ^ ((k >> 2) & 1), y ^ ((k >> 1) & 1), c ^ (k & 1)
        out.append((k, (px, py, pc), 4 * px + 2 * py + pc))
    return out


def _xchg_copies(src_refs, land_refs, send_sems, recv_sems, scatter):
    x, y, c = _place()
    me = 4 * x + 2 * y + c
    copies = []
    for a, (src, land) in enumerate(zip(src_refs, land_refs)):
        for k, place, idx in _peers(x, y, c):
            j = a * (N_DEV - 1) + k - 1
            copies.append(pltpu.make_async_remote_copy(
                src_ref=src.at[idx] if scatter[a] else src, dst_ref=land.at[me],
                send_sem=send_sems[j], recv_sem=recv_sems[j], device_id=place, device_id_type=MESH))
    return copies


def _xchg_start(srcs, scatter, name):
    n = len(srcs)
    lands = [lax.empty((N_DEV,) + (s.shape[1:] if sc else s.shape), s.dtype) for s, sc in zip(srcs, scatter)]

    ns = n * (N_DEV - 1)

    def body(*refs):
        sems = refs[2 * n:2 * n + 2 * ns]
        for cp in _xchg_copies(refs[:n], refs[n:2 * n], sems[:ns], sems[ns:], scatter):
            cp.start()
        token = refs[-1]
        token[...] = jnp.zeros_like(token)

    both = list(srcs) + lands
    res = pl.pallas_call(
        body, name=name,
        out_shape=(*[pltpu.SemaphoreType.DMA(())] * (2 * ns),
                   *[pltpu.HBM(a.shape, a.dtype) for a in both], jax.ShapeDtypeStruct((8, LANES), F32)),
        in_specs=[_HBM] * (2 * n),
        out_specs=(*([_SEM] * (2 * ns)), *([_HBM] * (2 * n)), pl.BlockSpec(memory_space=pltpu.VMEM)),
        input_output_aliases={i: 2 * ns + i for i in range(2 * n)},
        compiler_params=pltpu.CompilerParams(has_side_effects=_EFFECT),
    )(*[pltpu.with_memory_space_constraint(a, pltpu.HBM) for a in both])
    return (tuple(res[:2 * ns]), tuple(res[2 * ns:2 * ns + 2 * n])), res[-1]


def _xchg_wait(handle, scatter, after, name):
    sems, thru = handle
    n = len(thru) // 2
    ns = len(sems) // 2

    def body(*refs):
        got = refs[2 * n:2 * n + 2 * ns]
        for cp in _xchg_copies(refs[:n], refs[n:2 * n], got[:ns], got[ns:], scatter):
            cp.wait_send()
            cp.wait_recv()

    outs = pl.pallas_call(
        body, name=name, out_shape=tuple(pltpu.HBM(a.shape, a.dtype) for a in thru),
        in_specs=[_HBM] * (2 * n) + [_SEM] * (2 * ns) + [pl.BlockSpec(memory_space=pl.ANY)],
        out_specs=tuple([_HBM] * (2 * n)), input_output_aliases={i: i for i in range(2 * n)},
        compiler_params=pltpu.CompilerParams(has_side_effects=_EFFECT),
    )(*thru, *sems, after)
    return outs[:n], outs[n:]


def _tie(a, token):
    return a if token is None else a + token[0, 0].astype(a.dtype)


def _rows128(a):
    flat = a.reshape(-1)
    rows = -(-flat.shape[0] // LANES)
    rows = -(-rows // 8) * 8
    return jnp.pad(flat, (0, rows * LANES - flat.shape[0])).reshape(rows, LANES)


def _local_step(x, target, norm_mix_g, w_in_t, b_forget, gmlp_norm_g, w_spatial, b_spatial, norm_ffn_g, conv_b,
                norm_final_g, rest_fn, send_fn, small_fn, token=None):
    f = D_FF
    g_mix = norm_mix_g.reshape(1, D_MODEL)
    w_pad = jnp.pad(w_in_t, ((0, IN_PAD - IN_COLS), (0, 0)))
    bf_pad = jnp.pad(b_forget.reshape(1, N_HEADS), ((0, 0), (0, LANES - N_HEADS)))
    xn, qa, ka, va, ug, zf = _inproj_fwd(x, _tie(g_mix, token), w_pad, bf_pad)
    bias_full = jnp.repeat(b_spatial.reshape(N_GROUPS, CHUNK).T, GROUP_DIM, axis=1)
    w_s = w_spatial.reshape(N_GROUPS, CHUNK, CHUNK)
    gain = gmlp_norm_g.reshape(1, GMLP_WIDTH)
    sg = _gmlp_fwd(ug, gain, w_s, bias_full)
    att, lse, att_bf = _attn_fwd(qa, ka, va)
    w_out_bf, w_up_bf, conv_w, w_down_bf = rest_fn(att_bf)
    g_ffn = norm_ffn_g.reshape(1, D_MODEL)
    h1, hn = _out_proj_fwd(att_bf, sg, w_out_bf, x, g_ffn)
    cw = jnp.pad(conv_w.reshape(3, 2, f).transpose(1, 0, 2), ((0, 0), (0, 5), (0, 0)))
    cb = conv_b.reshape(2, 1, f)
    hu, hc, act = _ffn_up_conv(hn, w_up_bf, cw, cb)
    loss_blk, dh2, dh2_bf, dg_final = _ffn_down_loss(act, w_down_bf, h1, norm_final_g.reshape(1, D_MODEL), target)
    dw_down = _mm(act, dh2_bf, mode="tn", out_dtype=F32, tm=1408, tn=1024, tk=2048, name="ffn_down_dw")
    dact = _mm(dh2_bf, w_down_bf, mode="nt", out_dtype=F32, tm=1024, tn=1408, tk=1024, outer="j", name="ffn_down_dx")
    dhu, dcw = _conv_gate_bwd(hc, hu, dact, _tie(cw, send_fn("w_down", dw_down)))
    dw_up = _mm(hn, dhu, mode="tn", out_dtype=F32, tm=1024, tn=1408, tk=2048, b_halves=True, outer="j", name="ffn_up_dw")
    dh1, dh1_bf, dg_ffn = _ffn_up_dx_rms(dhu, w_up_bf, h1, _tie(g_ffn, send_fn("w_up", dw_up)), dh2)
    dsg, qb, doa = _out_proj_dx_prep(dh1_bf, w_out_bf, att, lse, qa)
    dw_out = _out_proj_dw(att_bf, sg, dh1_bf)
    dq, dk, dv, dcq, dck = _attn_bwd(qb, ka, va, doa)
    wt_s = w_s.transpose(0, 2, 1)
    dug, dw_s, dgain, dbias = _gmlp_bwd(ug, dsg, _tie(gain, send_fn("w_out", dw_out)), w_s, wt_s, bias_full)
    dzf, dbf = _gate_bwd(dcq, dck, zf)
    grad_x, dg_mix = _inproj_bwd_dx((dq, dk, dv, dug, dzf), w_pad, x, g_mix, dh1)
    grads = dict(
        norm_mix_g=dg_mix[0:1, :],
        b_forget=dbf[0:1, :N_HEADS],
        gmlp_norm_g=dgain[0:1, :],
        w_spatial=dw_s,
        b_spatial=dbias[:, ::GROUP_DIM].T,
        norm_ffn_g=dg_ffn[0:1, :],
        conv_w=dcw[:, 0:3, :].transpose(1, 0, 2).reshape(3, 2 * f),
        conv_b=dcw[:, 3, :].reshape(1, 2 * f),
        norm_final_g=dg_final[0, :],
    )
    token = small_fn(loss_blk[0, 0], grads)
    dw_in = _inproj_bwd_dw(xn, (dq, dk, dv, dug, _tie(dzf, token)))
    return grad_x, send_fn("w_in", dw_in[:, :IN_COLS])


SMALL = ("norm_mix_g", "b_forget", "gmlp_norm_g", "w_spatial", "b_spatial", "norm_ffn_g", "conv_b", "norm_final_g")


def kernel(x, norm_mix_g, w_in, b_forget, gmlp_norm_g, w_spatial, b_spatial, w_out, norm_ffn_g, w_up, conv_w, conv_b, w_down, norm_final_g, loss_target, m_norm_mix_g, m_w_in, m_b_forget, m_gmlp_norm_g, m_w_spatial, m_b_spatial, m_w_out, m_norm_ffn_g, m_w_up, m_conv_w, m_conv_b, m_w_down, m_norm_final_g, v_norm_mix_g, v_w_in, v_b_forget, v_gmlp_norm_g, v_w_spatial, v_b_spatial, v_w_out, v_norm_ffn_g, v_w_up, v_conv_w, v_conv_b, v_w_down, v_norm_final_g):
    weights = dict(norm_mix_g=norm_mix_g, w_in=w_in, b_forget=b_forget, gmlp_norm_g=gmlp_norm_g, w_spatial=w_spatial,
                   b_spatial=b_spatial, w_out=w_out, norm_ffn_g=norm_ffn_g, w_up=w_up, conv_w=conv_w, conv_b=conv_b,
                   w_down=w_down, norm_final_g=norm_final_g)
    m_in = dict(norm_mix_g=m_norm_mix_g, w_in=m_w_in, b_forget=m_b_forget, gmlp_norm_g=m_gmlp_norm_g,
                w_spatial=m_w_spatial, b_spatial=m_b_spatial, w_out=m_w_out, norm_ffn_g=m_norm_ffn_g, w_up=m_w_up,
                conv_w=m_conv_w, conv_b=m_conv_b, w_down=m_w_down, norm_final_g=m_norm_final_g)
    v_in = dict(norm_mix_g=v_norm_mix_g, w_in=v_w_in, b_forget=v_b_forget, gmlp_norm_g=v_gmlp_norm_g,
                w_spatial=v_w_spatial, b_spatial=v_b_spatial, w_out=v_w_out, norm_ffn_g=v_norm_ffn_g, w_up=v_w_up,
                conv_w=v_conv_w, conv_b=v_conv_b, w_down=v_w_down, norm_final_g=v_norm_final_g)
    order = list(weights)
    me = 4 * lax.axis_index("x") + 2 * lax.axis_index("y") + lax.axis_index("c")
    n_in, n_up = w_in.shape[2], w_up.shape[2]
    r_out, r_down = w_out.shape[1], w_down.shape[1]

    def with_mine(landed, mine):
        return lax.dynamic_update_index_in_dim(landed, mine, me, 0)

    up_blk = w_up[0].T.astype(BF16)
    out_blk = w_out[0].astype(BF16)
    down_blk = w_down[0].astype(BF16)
    taps_blk = jnp.pad(conv_w[0], ((0, 5), (0, 0)))
    (in_all,) = _all_gather([w_in[0].T.astype(BF16)], "gather_w_in")
    in_all, rest_blocks = lax.optimization_barrier((in_all, [up_blk, out_blk, down_blk, taps_blk]))
    rest_handle, token = _xchg_start(rest_blocks, [False] * 4, "gather_rest_start")
    w_in_t = in_all.reshape(N_DEV * n_in, D_MODEL)

    def rest_fn(after):
        mine, landed = _xchg_wait(rest_handle, [False] * 4, after, "gather_rest_wait")
        up_all, out_all, down_all, taps_all = [with_mine(l, b) for l, b in zip(landed, mine)]
        return (out_all.reshape(N_DEV * r_out, D_MODEL), up_all.reshape(N_DEV * n_up, D_MODEL),
                taps_all[:, :3, :].transpose(1, 0, 2).reshape(3, N_DEV * n_up),
                down_all.reshape(N_DEV * r_down, D_MODEL))

    sent = {}

    def send_fn(name, grad):
        if name == "w_in":
            parts = grad.reshape(D_MODEL, N_DEV, -1).transpose(1, 0, 2).astype(BF16)
        elif name == "w_up":
            parts = grad.reshape(D_MODEL, N_DEV, -1).transpose(1, 0, 2)
        else:
            parts = grad.reshape(N_DEV, -1, D_MODEL)
        sent[name], tok = _xchg_start([parts], [True], "scatter_" + name + "_start")
        return tok

    small = {}

    def small_fn(loss_local, g):
        loss_rows = jnp.pad(loss_local.reshape(1, 1), ((0, 31), (0, LANES - 1)))
        packed = [_rows128(g[k]) for k in SMALL] + [loss_rows, _rows128(g["conv_w"])]
        small["sizes"] = [p.shape[0] for p in packed]
        small["handle"], tok = _xchg_start([jnp.concatenate(packed, axis=0)], [False], "gather_small_start")
        return tok

    grad_x, after = _local_step(
        x[0], loss_target[0], norm_mix_g, w_in_t, b_forget, gmlp_norm_g, w_spatial, b_spatial, norm_ffn_g, conv_b,
        norm_final_g, rest_fn, send_fn, small_fn, token)

    outs = {}

    def update_big(name, after):
        (parts,), (landed,) = _xchg_wait(sent[name], [True], after, "scatter_" + name + "_wait")
        outs[name] = tuple(_adamw_owner(weights[name], m_in[name], v_in[name], landed, parts, me, "adamw_" + name))
        return outs[name][0]

    for name in ("w_down", "w_up", "w_out"):
        after = update_big(name, after)

    (mine,), (landed,) = _xchg_wait(small["handle"], [False], after, "gather_small_wait")
    small_all = with_mine(landed, mine)
    sizes = small["sizes"]
    n_small_rows = sum(sizes[:-2])

    def pack(src):
        return jnp.concatenate([_rows128(src[k]) for k in SMALL] + [jnp.zeros((sizes[-2], LANES), F32)], axis=0)

    n_adam_rows = n_small_rows + sizes[-2]
    per_kind = _adamw_packed(pack(weights), pack(m_in), pack(v_in), small_all, sizes[:-1], "adamw_small")
    loss = per_kind[0][-1][0, 0]
    for j, k in enumerate(SMALL):
        shp = weights[k].shape
        cnt = math.prod(shp)
        outs[k] = tuple(kind[j].reshape(-1)[:cnt].reshape(shp) for kind in per_kind)
    sg_ = per_kind[0][0]
    taps_parts = small_all[:, n_adam_rows:, :].reshape(N_DEV, -1)[:, :3 * N_DEV * n_up].reshape(N_DEV, 3, N_DEV * n_up)
    taps_mine = lax.dynamic_slice_in_dim(taps_parts, me * n_up, n_up, axis=2)
    taps_mine = jnp.pad(taps_mine, ((0, 0), (0, 5), (0, 0)))

    def pad8(a):
        return jnp.pad(a[0], ((0, 5), (0, 0)))

    res = _adamw(pad8(conv_w), pad8(m_conv_w), pad8(v_conv_w), taps_mine, "adamw_conv_w")
    outs["conv_w"] = tuple(a[:3][None] for a in res)
    update_big("w_in", sg_)

    return (loss, grad_x[None], *[outs[k][0] for k in order], *[outs[k][1] for k in order],
            *[outs[k][2] for k in order], *[outs[k][3] for k in order])
```

```python
import functools
import math

import jax
import jax.numpy as jnp
from jax import lax
from jax.experimental import pallas as pl
from jax.experimental.pallas import tpu as pltpu

F32 = jnp.float32
BF16 = jnp.bfloat16

N_DEV = 8
D_MODEL = 1024
ATT_WIDTH = 512
GMLP_WIDTH = 512
HEAD_DIM = 64
N_HEADS = 8
N_PAIRS = 4
N_GROUPS = 8
GROUP_DIM = 64
CHUNK = 128
D_FF = 2816
IN_COLS = 2568
IN_PAD = 2688
QKV = 1536
UG_END = 2560
EPS = 1e-6
LANES = 128

ADAM_LR = 0.001
ADAM_B1 = 0.9
ADAM_B2 = 0.999
ADAM_EPS = 1e-08
ADAM_WD = 0.01
ADAM_STEP = 10

ATT_TQ = 1024
ATT_TK = 1024
FFN_TM, FFN_TN = 512, 1408
CONV_TM, CONV_TN = 512, 1408
VMEM_LIMIT = 56 * 1024 * 1024
MESH = pl.DeviceIdType.MESH


def _cp(sem, vmem=None):
    return pltpu.CompilerParams(dimension_semantics=sem, vmem_limit_bytes=vmem)


def _pick(n, prefs):
    for p in prefs:
        if n % p == 0:
            return p
    return n


def _split3(x):
    hi = x.astype(BF16)
    r1 = x - hi.astype(F32)
    mid = r1.astype(BF16)
    lo = (r1 - mid.astype(F32)).astype(BF16)
    return hi, mid, lo


def _dot3(x, ones_bf):
    d = functools.partial(jnp.dot, preferred_element_type=F32)
    out = []
    for c in range(0, x.shape[1], 2 * LANES):
        blk = ones_bf[c:c + 2 * LANES, c:c + 2 * LANES]
        hi, mid, lo = _split3(x[:, c:c + 2 * LANES])
        out.append(d(hi, blk) + d(mid, blk) + d(lo, blk))
    return jnp.concatenate(out, axis=1)


def _dot3l(ones_bf, x):
    n = x.shape[1]
    y = jnp.dot(ones_bf, jnp.concatenate(_split3(x), axis=1), preferred_element_type=F32)
    return y[:, :n] + y[:, n:2 * n] + y[:, 2 * n:]


def _gelu(x):
    k = math.sqrt(2.0 / math.pi)
    t = jnp.tanh(k * (x + 0.044715 * (x * x * x)))
    return 0.5 * x * (1.0 + t)


def _gelu_grad(x):
    k = math.sqrt(2.0 / math.pi)
    x2 = x * x
    t = jnp.tanh(k * (x + 0.044715 * (x2 * x)))
    return 0.5 * (1.0 + t) + 0.5 * x * (1.0 - t * t) * (k * (1.0 + 3.0 * 0.044715 * x2))


def _sigmoid(x):
    return 1.0 / (1.0 + jnp.exp(-x))


def _mm(a, b, *, mode, out_dtype, tm, tn, tk, name, res=None, a_halves=False, b_halves=False,
        out_halves=False, outer="i"):
    if mode == "tn":
        K, M = a.shape[-2], a.shape[-1] * (2 if a_halves else 1)
    else:
        M, K = a.shape[-2], a.shape[-1] * (2 if a_halves else 1)
    if mode == "nt":
        N = b.shape[-2]
        assert b.shape[-1] == K
    else:
        N = b.shape[-1] * (2 if b_halves else 1)
    tm, tn, tk = min(tm, M), min(tn, N), min(tk, K)
    assert M % tm == 0 and N % tn == 0 and K % tk == 0, (name, M, N, K, tm, tn, tk)
    nm, nn, nk = M // tm, N // tn, K // tk

    def ij(g0, g1):
        return (g0, g1) if outer == "i" else (g1, g0)

    if mode == "nn":
        dims = (((1,), (0,)), ((), ()))
        if a_halves:
            nkh = nk // 2
            a_spec = pl.BlockSpec((None, tm, tk), lambda g0, g1, k: (k // nkh, ij(g0, g1)[0], k % nkh))
        else:
            a_spec = pl.BlockSpec((tm, tk), lambda g0, g1, k: (ij(g0, g1)[0], k))
        b_spec = pl.BlockSpec((tk, tn), lambda g0, g1, k: (k, ij(g0, g1)[1]))
    elif mode == "nt":
        dims = (((1,), (1,)), ((), ()))
        if a_halves:
            nkh = nk // 2
            a_spec = pl.BlockSpec((None, tm, tk), lambda g0, g1, k: (k // nkh, ij(g0, g1)[0], k % nkh))
        else:
            a_spec = pl.BlockSpec((tm, tk), lambda g0, g1, k: (ij(g0, g1)[0], k))
        b_spec = pl.BlockSpec((tn, tk), lambda g0, g1, k: (ij(g0, g1)[1], k))
    else:
        dims = (((0,), (0,)), ((), ()))
        if a_halves:
            nmh = nm // 2
            a_spec = pl.BlockSpec((None, tk, tm), lambda g0, g1, k: (ij(g0, g1)[0] // nmh, k, ij(g0, g1)[0] % nmh))
        else:
            a_spec = pl.BlockSpec((tk, tm), lambda g0, g1, k: (k, ij(g0, g1)[0]))
        if b_halves:
            nnh = nn // 2
            b_spec = pl.BlockSpec((None, tk, tn), lambda g0, g1, k: (ij(g0, g1)[1] // nnh, k, ij(g0, g1)[1] % nnh))
        else:
            b_spec = pl.BlockSpec((tk, tn), lambda g0, g1, k: (k, ij(g0, g1)[1]))
    if out_halves:
        nnh = nn // 2
        o_spec = pl.BlockSpec((None, tm, tn), lambda g0, g1, k: (ij(g0, g1)[1] // nnh, ij(g0, g1)[0], ij(g0, g1)[1] % nnh))
        o_shape = jax.ShapeDtypeStruct((2, M, N // 2), out_dtype)
    else:
        o_spec = pl.BlockSpec((tm, tn), lambda g0, g1, k: ij(g0, g1))
        o_shape = jax.ShapeDtypeStruct((M, N), out_dtype)
    in_specs = [a_spec, b_spec]
    args = [a, b]
    if res is not None:
        in_specs.append(pl.BlockSpec((tm, tn), lambda g0, g1, k: ij(g0, g1)))
        args.append(res)

    def body(*refs):
        if res is not None:
            a_ref, b_ref, r_ref, o_ref = refs[:4]
        else:
            a_ref, b_ref, o_ref = refs[:3]
            r_ref = None
        part = lax.dot_general(a_ref[...], b_ref[...], dims, preferred_element_type=F32)
        if nk == 1:
            if r_ref is not None:
                part = part + r_ref[...]
            o_ref[...] = part.astype(out_dtype)
            return
        acc_ref = refs[-1]
        k = pl.program_id(2)

        @pl.when(k == 0)
        def _():
            acc_ref[...] = part

        @pl.when(k > 0)
        def _():
            acc_ref[...] += part

        @pl.when(k == nk - 1)
        def _():
            tot = acc_ref[...]
            if r_ref is not None:
                tot = tot + r_ref[...]
            o_ref[...] = tot.astype(out_dtype)

    grid = (nm, nn, nk) if outer == "i" else (nn, nm, nk)
    scratch = [] if nk == 1 else [pltpu.VMEM((tm, tn), F32)]
    return pl.pallas_call(
        body, out_shape=o_shape, grid=grid, in_specs=in_specs, out_specs=o_spec, scratch_shapes=scratch,
        name=name, compiler_params=_cp(("parallel", "parallel", "arbitrary"), VMEM_LIMIT),
    )(*args)


def _aug(lane, terms):
    out = 0.0
    for j, t in enumerate(terms):
        out = jnp.where(lane == HEAD_DIM + j, t, out)
    return out


def _split3f(x):
    hi, mid, lo = _split3(x)
    return [hi.astype(F32), mid.astype(F32), lo.astype(F32)]


def _inproj_fwd(x, g_mix, w_pad, bf_pad):
    S = x.shape[0]
    tm = _pick(S, (512, 256))
    tri = (lax.broadcasted_iota(jnp.int32, (tm, tm), 0) >= lax.broadcasted_iota(jnp.int32, (tm, tm), 1)).astype(BF16)

    def body(x_ref, g_ref, w_ref, bf_ref, tri_ref, put_ref, one_ref, xn_ref, qa_ref, ka_ref, va_ref, ug_ref, zf_ref,
             carry_ref):
        i = pl.program_id(0)

        @pl.when(i == 0)
        def _():
            carry_ref[...] = jnp.zeros_like(carry_ref)

        xf = x_ref[...]
        r = lax.rsqrt(jnp.mean(xf * xf, axis=-1, keepdims=True) + EPS)
        xn = ((xf * r) * g_ref[...]).astype(BF16)
        xn_ref[...] = xn
        proj = lax.dot_general(xn, w_ref[...], _NT, preferred_element_type=F32)
        ug_ref[...] = proj[:, QKV:UG_END]
        zf = proj[:, UG_END:] + bf_ref[...]
        zf_ref[...] = zf
        lf = jnp.minimum(zf, 0.0) - jnp.log(1.0 + jnp.exp(-jnp.abs(zf)))
        c = _dot3l(tri_ref[...], lf) + carry_ref[0:1, :]
        carry_ref[0:1, :] = c[tm - 1:tm, :]
        c3 = jnp.concatenate(_split3(c), axis=1)
        aug_q = jnp.dot(c3, put_ref[0], preferred_element_type=F32) + one_ref[0:1, :]
        aug_k = jnp.dot(c3, put_ref[1], preferred_element_type=F32) + one_ref[1:2, :]
        lane = lax.broadcasted_iota(jnp.int32, (tm, LANES), 1)
        for h in range(N_HEADS):
            p, odd = h // 2, h % 2

            def head(base, scale=None, p=p, odd=odd):
                blk = proj[:, base + p * LANES:base + (p + 1) * LANES]
                if scale is not None:
                    blk = blk * scale
                return pltpu.roll(blk, HEAD_DIM, 1) if odd else blk

            cols = slice(h * LANES, (h + 1) * LANES)
            qa_ref[:, cols] = jnp.where(lane < HEAD_DIM, head(0, HEAD_DIM ** -0.5), aug_q[:, cols]).astype(BF16)
            ka_ref[:, cols] = jnp.where(lane < HEAD_DIM, head(ATT_WIDTH), aug_k[:, cols]).astype(BF16)
            va_ref[:, cols] = jnp.where(lane < HEAD_DIM, head(2 * ATT_WIDTH), one_ref[2:3, cols]).astype(BF16)

    wide = N_HEADS * LANES
    src = lax.broadcasted_iota(jnp.int32, (3 * LANES, wide), 0)
    col = lax.broadcasted_iota(jnp.int32, (3 * LANES, wide), 1)
    hd, term = src % LANES, src // LANES
    to_q = (col == hd * LANES + HEAD_DIM + term) & (hd < N_HEADS)
    to_k = (col == hd * LANES + HEAD_DIM + 3 + term) & (hd < N_HEADS)
    put = jnp.stack([to_q.astype(BF16), -to_k.astype(BF16)])
    off = lax.broadcasted_iota(jnp.int32, (8, wide), 1) % LANES - HEAD_DIM
    row = lax.broadcasted_iota(jnp.int32, (8, wide), 0)
    q_one = (off >= 3) & (off < 6)
    k_one = ((off >= 0) & (off < 3)) | ((off >= 6) & (off < 9))
    v_one = (off >= 0) & (off < 3)
    ones = jnp.where(row == 0, q_one, jnp.where(row == 1, k_one, (row == 2) & v_one)).astype(F32)
    return pl.pallas_call(
        body,
        out_shape=(jax.ShapeDtypeStruct((S, D_MODEL), BF16), jax.ShapeDtypeStruct((S, wide), BF16),
                   jax.ShapeDtypeStruct((S, wide), BF16), jax.ShapeDtypeStruct((S, wide), BF16),
                   jax.ShapeDtypeStruct((S, 2 * GMLP_WIDTH), F32), jax.ShapeDtypeStruct((S, LANES), F32)),
        grid=(S // tm,),
        in_specs=[pl.BlockSpec((tm, D_MODEL), lambda i: (i, 0)), pl.BlockSpec((1, D_MODEL), lambda i: (0, 0)),
                  pl.BlockSpec((IN_PAD, D_MODEL), lambda i: (0, 0)), pl.BlockSpec((1, LANES), lambda i: (0, 0)),
                  pl.BlockSpec((tm, tm), lambda i: (0, 0)), pl.BlockSpec((2, 3 * LANES, wide), lambda i: (0, 0, 0)),
                  pl.BlockSpec((8, wide), lambda i: (0, 0))],
        out_specs=(pl.BlockSpec((tm, D_MODEL), lambda i: (i, 0)), pl.BlockSpec((tm, wide), lambda i: (i, 0)),
                   pl.BlockSpec((tm, wide), lambda i: (i, 0)), pl.BlockSpec((tm, wide), lambda i: (i, 0)),
                   pl.BlockSpec((tm, 2 * GMLP_WIDTH), lambda i: (i, 0)), pl.BlockSpec((tm, LANES), lambda i: (i, 0))),
        scratch_shapes=[pltpu.VMEM((8, LANES), F32)],
        name="inproj_fwd", compiler_params=_cp(("arbitrary",), VMEM_LIMIT),
    )(x, g_mix, w_pad, bf_pad, tri, put, ones)


def _group_ones():
    r = lax.broadcasted_iota(jnp.int32, (GMLP_WIDTH, GMLP_WIDTH), 0) // GROUP_DIM
    c = lax.broadcasted_iota(jnp.int32, (GMLP_WIDTH, GMLP_WIDTH), 1) // GROUP_DIM
    return (r == c).astype(BF16)


def _gmlp_mixed(vn_bf, w_ref, bias, n_chunks):
    lane = lax.broadcasted_iota(jnp.int32, (CHUNK, LANES), 1)
    row = lax.broadcasted_iota(jnp.int32, (CHUNK, CHUNK), 0)
    col = lax.broadcasted_iota(jnp.int32, (CHUNK, CHUNK), 1)
    ws = [jnp.where(row >= col, w_ref[g], 0.0).astype(BF16) for g in range(N_GROUPS)]
    rows = []
    for ci in range(n_chunks):
        cols = []
        for pp in range(N_GROUPS // 2):
            v = vn_bf[ci * CHUNK:(ci + 1) * CHUNK, pp * LANES:(pp + 1) * LANES]
            v_lo = jnp.where(lane < GROUP_DIM, v, jnp.zeros_like(v))
            v_hi = jnp.where(lane >= GROUP_DIM, v, jnp.zeros_like(v))
            m = (jnp.dot(ws[2 * pp], v_lo, preferred_element_type=F32)
                 + jnp.dot(ws[2 * pp + 1], v_hi, preferred_element_type=F32))
            cols.append(m + bias[:, pp * LANES:(pp + 1) * LANES])
        rows.append(jnp.concatenate(cols, axis=1))
    return jnp.concatenate(rows, axis=0)


def _gmlp_fwd(ug, gain, w_s, bias_full):
    S = ug.shape[0]
    tm = _pick(S, (512, 256, 128))
    ones = _group_ones()

    def body(ug_ref, gain_ref, w_ref, bias_ref, ones_ref, sg_ref):
        u = _gelu(ug_ref[:, :GMLP_WIDTH])
        vr = _gelu(ug_ref[:, GMLP_WIDTH:])
        ms = _dot3(vr * vr, ones_ref[...]) * (1.0 / GROUP_DIM)
        vn = ((vr * lax.rsqrt(ms + EPS)) * gain_ref[...]).astype(BF16)
        mixed = _gmlp_mixed(vn, w_ref, bias_ref[...], tm // CHUNK)
        sg_ref[...] = (u * mixed).astype(BF16)

    return pl.pallas_call(
        body, out_shape=jax.ShapeDtypeStruct((S, GMLP_WIDTH), BF16), grid=(S // tm,),
        in_specs=[pl.BlockSpec((tm, 2 * GMLP_WIDTH), lambda i: (i, 0)), pl.BlockSpec((1, GMLP_WIDTH), lambda i: (0, 0)),
                  pl.BlockSpec((N_GROUPS, CHUNK, CHUNK), lambda i: (0, 0, 0)),
                  pl.BlockSpec((CHUNK, GMLP_WIDTH), lambda i: (0, 0)),
                  pl.BlockSpec((GMLP_WIDTH, GMLP_WIDTH), lambda i: (0, 0))],
        out_specs=pl.BlockSpec((tm, GMLP_WIDTH), lambda i: (i, 0)),
        name="gmlp_fwd", compiler_params=_cp(("parallel",), VMEM_LIMIT),
    )(ug, gain, w_s, bias_full, ones)


_NT = (((1,), (1,)), ((), ()))
_TN = (((0,), (0,)), ((), ()))


def _attn_fwd(qa, ka, va):
    S = qa.shape[0]
    tq = _pick(S, (ATT_TQ, 256))
    tk = min(ATT_TK, tq)
    nq = S // tq
    assert tq == tk, "the diagonal block is handled as one tq x tq tile"
    per_q = 1

    def body(q_ref, k_ref, v_ref, o_ref, lse_ref, ob_ref):
        qi = pl.program_id(1)
        lane = lax.broadcasted_iota(jnp.int32, (tq, LANES), 1)
        qs = [q_ref[:, :LANES], q_ref[:, LANES:]]

        def update(q, ks, k_len, h, m, acc, first_row):
            cols = slice(h * LANES, (h + 1) * LANES)
            s = lax.dot_general(q, k_ref[pl.ds(ks, k_len), cols], _NT, preferred_element_type=F32)
            if first_row is not None:
                rid = lax.broadcasted_iota(jnp.int32, s.shape, 0) + first_row
                s = jnp.where(rid >= lax.broadcasted_iota(jnp.int32, s.shape, 1), s, -jnp.inf)
            m_new = jnp.maximum(m, jnp.max(s, axis=-1, keepdims=True))
            p = jnp.exp(s - m_new).astype(BF16)
            acc = jnp.exp(m - m_new) * acc + jnp.dot(p, v_ref[pl.ds(ks, k_len), cols], preferred_element_type=F32)
            return m_new, acc

        def step(kb, carry):
            ks = pl.multiple_of(kb * tk, tk)
            return tuple(update(qs[h], ks, tk, h, *carry[h], None) for h in range(2))

        def two_steps(t, carry):
            return step(qi % 2 + 2 * t + 1, step(qi % 2 + 2 * t, carry))

        one = (jnp.full((tq, 1), -jnp.inf, F32), jnp.zeros((tq, LANES), F32))
        carry = lax.fori_loop(0, qi % 2, step, (one, one))
        carry = lax.fori_loop(0, qi // 2, two_steps, carry)
        outs, lses = [], []
        diag = pl.multiple_of(qi * tq, tq)
        for h in range(2):
            m, acc = update(qs[h], diag, tk, h, *carry[h], 0)
            l = acc[:, HEAD_DIM:HEAD_DIM + 1]
            outs.append(acc / l)
            lses.append(m + jnp.log(l))
        o = jnp.where(lane < HEAD_DIM, outs[0], pltpu.roll(outs[1], HEAD_DIM, 1))
        o_ref[...] = o
        ob_ref[...] = o.astype(BF16)
        lse_ref[...] = jnp.where(lane < HEAD_DIM, lses[0], lses[1])

    return pl.pallas_call(
        body,
        out_shape=(jax.ShapeDtypeStruct((S, ATT_WIDTH), F32), jax.ShapeDtypeStruct((S, ATT_WIDTH), F32),
                   jax.ShapeDtypeStruct((S, ATT_WIDTH), BF16)),
        grid=(N_PAIRS, nq),
        in_specs=[pl.BlockSpec((tq, 2 * LANES), lambda p, i: (i, p)),
                  pl.BlockSpec((S, 2 * LANES), lambda p, i: (0, p)),
                  pl.BlockSpec((S, 2 * LANES), lambda p, i: (0, p))],
        out_specs=(pl.BlockSpec((tq, LANES), lambda p, i: (i, p)), pl.BlockSpec((tq, LANES), lambda p, i: (i, p)),
                   pl.BlockSpec((tq, LANES), lambda p, i: (i, p))),
        name="attn_fwd", compiler_params=_cp(("parallel", "parallel"), VMEM_LIMIT),
    )(qa, ka, va)


def _shift_rows(x, prev, n):
    rid = lax.broadcasted_iota(jnp.int32, x.shape, 0)
    y = pltpu.roll(x, n, 0)
    if n == 1:
        return jnp.where(rid == 0, prev[7:8, :], y)
    return jnp.where(rid == 0, prev[6:7, :], jnp.where(rid == 1, prev[7:8, :], y))


def _shift_rows_up(x, nxt, n):
    rows = x.shape[0]
    rid = lax.broadcasted_iota(jnp.int32, x.shape, 0)
    y = pltpu.roll(x, rows - n, 0)
    if n == 1:
        return jnp.where(rid == rows - 1, nxt[0:1, :], y)
    return jnp.where(rid == rows - 2, nxt[0:1, :], jnp.where(rid == rows - 1, nxt[1:2, :], y))


def _conv3(cur, prev, w, b):
    return (w[0:1, :] * _shift_rows(cur, prev, 2) + w[1:2, :] * _shift_rows(cur, prev, 1)
            + w[2:3, :] * cur + b)


def _ffn_up_conv(hn, w_up_bf, cw, cb):
    S = hn.shape[0]
    F = D_FF
    tm = _pick(S, (FFN_TM, 256))
    tn = _pick(F, (FFN_TN, 256, 128))
    nj = F // tn

    def body(hn_ref, wa_ref, wg_ref, cw_ref, cb_ref, hu_ref, hc_ref, act_ref, tail_ref):
        i = pl.program_id(1)

        @pl.when(i == 0)
        def _():
            tail_ref[...] = jnp.zeros_like(tail_ref)

        hn_v = hn_ref[...]
        halves = []
        for h, w_ref in enumerate((wa_ref, wg_ref)):
            hu = lax.dot_general(hn_v, w_ref[...], _NT, preferred_element_type=F32)
            hu_ref[h] = hu.astype(BF16)
            hc = _conv3(hu, tail_ref[h], cw_ref[h], cb_ref[h])
            hc_ref[h] = hc
            halves.append(hc)
            tail_ref[h] = hu[tm - 8:, :]
        a, g = halves
        act_ref[...] = (g * _sigmoid(g) * a).astype(BF16)

    both = pl.BlockSpec((2, tm, tn), lambda j, i: (0, i, j))
    return pl.pallas_call(
        body, out_shape=(jax.ShapeDtypeStruct((2, S, F), BF16), jax.ShapeDtypeStruct((2, S, F), F32),
                         jax.ShapeDtypeStruct((S, F), BF16)),
        grid=(nj, S // tm),
        in_specs=[pl.BlockSpec((tm, D_MODEL), lambda j, i: (i, 0)),
                  pl.BlockSpec((tn, D_MODEL), lambda j, i: (j, 0)),
                  pl.BlockSpec((tn, D_MODEL), lambda j, i: (nj + j, 0)),
                  pl.BlockSpec((2, 8, tn), lambda j, i: (0, 0, j)),
                  pl.BlockSpec((2, 1, tn), lambda j, i: (0, 0, j))],
        out_specs=(both, both, pl.BlockSpec((tm, tn), lambda j, i: (i, j))),
        scratch_shapes=[pltpu.VMEM((2, 8, tn), F32)],
        name="ffn_up_conv", compiler_params=_cp(("parallel", "arbitrary"), VMEM_LIMIT),
    )(hn, w_up_bf, w_up_bf, cw, cb)


def _ffn_down_loss(act, w_down_bf, h1, g_final, target):
    S = h1.shape[0]
    tm = _pick(S, (512, 256))

    def body(a_ref, w_ref, h1_ref, g_ref, t_ref, loss_ref, dh_ref, dhb_ref, dg_ref):
        i = pl.program_id(0)

        @pl.when(i == 0)
        def _():
            loss_ref[...] = jnp.zeros_like(loss_ref)
            dg_ref[...] = jnp.zeros_like(dg_ref)

        hf = h1_ref[...] + jnp.dot(a_ref[...], w_ref[...], preferred_element_type=F32)
        g = g_ref[...]
        r = lax.rsqrt(jnp.mean(hf * hf, axis=-1, keepdims=True) + EPS)
        hhat = hf * r
        err = hhat * g - t_ref[...]
        loss_ref[...] += 0.5 * jnp.sum(jnp.mean(err * err, axis=-1, keepdims=True))
        dy = err * (1.0 / D_MODEL)
        dg_ref[0:1, :] += jnp.sum(dy * hhat, axis=0, keepdims=True)
        dhat = dy * g
        dh = r * (dhat - hhat * jnp.mean(dhat * hhat, axis=-1, keepdims=True))
        dh_ref[...] = dh
        dhb_ref[...] = dh.astype(BF16)

    row = pl.BlockSpec((tm, D_MODEL), lambda i: (i, 0))
    return pl.pallas_call(
        body,
        out_shape=(jax.ShapeDtypeStruct((8, LANES), F32), jax.ShapeDtypeStruct((S, D_MODEL), F32),
                   jax.ShapeDtypeStruct((S, D_MODEL), BF16), jax.ShapeDtypeStruct((8, D_MODEL), F32)),
        grid=(S // tm,),
        in_specs=[pl.BlockSpec((tm, D_FF), lambda i: (i, 0)), pl.BlockSpec((D_FF, D_MODEL), lambda i: (0, 0)), row,
                  pl.BlockSpec((1, D_MODEL), lambda i: (0, 0)), row],
        out_specs=(pl.BlockSpec((8, LANES), lambda i: (0, 0)), row, row, pl.BlockSpec((8, D_MODEL), lambda i: (0, 0))),
        name="ffn_down_loss", compiler_params=_cp(("arbitrary",), VMEM_LIMIT),
    )(act, w_down_bf, h1, g_final, target)


def _ffn_up_dx_rms(dhu, w_up_bf, h1, g_ffn, dh2):
    _, S, F = dhu.shape
    tm = _pick(S, (512, 256))

    def body(a_ref, b_ref, h_ref, g_ref, r_ref, dh_ref, dhb_ref, dg_ref):
        i = pl.program_id(0)

        @pl.when(i == 0)
        def _():
            dg_ref[...] = jnp.zeros_like(dg_ref)

        dyv = (jnp.dot(a_ref[0], b_ref[:F, :], preferred_element_type=F32)
               + jnp.dot(a_ref[1], b_ref[F:, :], preferred_element_type=F32))
        hf = h_ref[...]
        r = lax.rsqrt(jnp.mean(hf * hf, axis=-1, keepdims=True) + EPS)
        hhat = hf * r
        dg_ref[0:1, :] += jnp.sum(dyv * hhat, axis=0, keepdims=True)
        dhat = dyv * g_ref[...]
        dh = r_ref[...] + r * (dhat - hhat * jnp.mean(dhat * hhat, axis=-1, keepdims=True))
        dh_ref[...] = dh
        dhb_ref[...] = dh.astype(BF16)

    row = pl.BlockSpec((tm, D_MODEL), lambda i: (i, 0))
    return pl.pallas_call(
        body,
        out_shape=(jax.ShapeDtypeStruct((S, D_MODEL), F32), jax.ShapeDtypeStruct((S, D_MODEL), BF16),
                   jax.ShapeDtypeStruct((8, D_MODEL), F32)),
        grid=(S // tm,),
        in_specs=[pl.BlockSpec((2, tm, F), lambda i: (0, i, 0)), pl.BlockSpec((2 * F, D_MODEL), lambda i: (0, 0)),
                  row, pl.BlockSpec((1, D_MODEL), lambda i: (0, 0)), row],
        out_specs=(row, row, pl.BlockSpec((8, D_MODEL), lambda i: (0, 0))),
        name="ffn_up_dx_rms", compiler_params=_cp(("arbitrary",), VMEM_LIMIT),
    )(dhu, w_up_bf, h1, g_ffn, dh2)


def _conv_gate_bwd(hc, hu, dact, cw):
    _, S, F = hu.shape
    tm = _pick(S, (CONV_TM, 128))
    tn = _pick(F, (CONV_TN, 256, 128))
    r8 = tm // 8
    n_i = S // tm
    last8 = S // 8 - 1

    def body(hc_ref, hcn_ref, hu_ref, da_ref, dan_ref, w_ref, dhu_ref, dcw_ref):
        i = pl.program_id(1)

        @pl.when(i == 0)
        def _():
            dcw_ref[...] = jnp.zeros_like(dcw_ref)

        rid8 = lax.broadcasted_iota(jnp.int32, (8, tn), 0)

        def gate_grads(a, g, d):
            sg = _sigmoid(g)
            return d * (g * sg), d * a * (sg * (1.0 + g * (1.0 - sg)))

        dhc = gate_grads(hc_ref[0], hc_ref[1], da_ref[...])
        dhc_n = gate_grads(hcn_ref[0], hcn_ref[1], dan_ref[...])
        for h in range(2):
            w = w_ref[h]
            d = dhc[h]
            dn = jnp.where(i < n_i - 1, dhc_n[h], 0.0)
            u1 = _shift_rows_up(d, dn, 1)
            u2 = _shift_rows_up(d, dn, 2)
            dhu_ref[h] = (w[2:3, :] * d + w[1:2, :] * u1 + w[0:1, :] * u2).astype(BF16)
            x = hu_ref[h].astype(F32)
            t0, t1, t2, t3 = [jnp.sum(t, axis=0, keepdims=True) for t in (u2 * x, u1 * x, d * x, d)]
            dcw_ref[h] += jnp.where(rid8 == 0, t0, jnp.where(rid8 == 1, t1, jnp.where(rid8 == 2, t2, jnp.where(rid8 == 3, t3, 0.0))))

    cur = pl.BlockSpec((2, tm, tn), lambda j, i: (0, i, j))
    return pl.pallas_call(
        body,
        out_shape=(jax.ShapeDtypeStruct((2, S, F), BF16), jax.ShapeDtypeStruct((2, 8, F), F32)),
        grid=(F // tn, n_i),
        in_specs=[cur, pl.BlockSpec((2, 8, tn), lambda j, i: (0, jnp.minimum((i + 1) * r8, last8), j)), cur,
                  pl.BlockSpec((tm, tn), lambda j, i: (i, j)),
                  pl.BlockSpec((8, tn), lambda j, i: (jnp.minimum((i + 1) * r8, last8), j)),
                  pl.BlockSpec((2, 8, tn), lambda j, i: (0, 0, j))],
        out_specs=(cur, pl.BlockSpec((2, 8, tn), lambda j, i: (0, 0, j))),
        name="conv_gate_bwd", compiler_params=_cp(("parallel", "arbitrary"), VMEM_LIMIT),
    )(hc, hc, hu, dact, dact, cw)


def _out_proj_dx_prep(dh1_bf, w_out_bf, att, lse, qa):
    S = att.shape[0]
    tm = _pick(S, (256,))

    def body(dh_ref, w_ref, o_ref, lse_ref, q_ref, dsg_ref, qb_ref, doa_ref):
        lane = lax.broadcasted_iota(jnp.int32, (tm, LANES), 1)
        dh = dh_ref[...]
        dsg_ref[...] = lax.dot_general(dh, w_ref[ATT_WIDTH:, :], _NT, preferred_element_type=F32)
        datt = lax.dot_general(dh, w_ref[:ATT_WIDTH, :], _NT, preferred_element_type=F32)
        for p in range(N_PAIRS):
            pc = slice(p * LANES, (p + 1) * LANES)
            do = datt[:, pc]
            prod = o_ref[:, pc] * do
            for hh in range(2):
                sel = (lane >= HEAD_DIM) if hh else (lane < HEAD_DIM)
                delta = jnp.sum(jnp.where(sel, prod, 0.0), axis=-1, keepdims=True)
                dod = pltpu.roll(do, HEAD_DIM, 1) if hh else do
                cols = slice((2 * p + hh) * LANES, (2 * p + hh + 1) * LANES)
                doa_ref[:, cols] = jnp.where(lane < HEAD_DIM, dod, _aug(lane, _split3f(-delta))).astype(BF16)
                lcol = p * LANES + hh * HEAD_DIM
                l3 = _split3f(-lse_ref[:, lcol:lcol + 1])
                augl = jnp.where(lane == HEAD_DIM + 6, l3[0], jnp.where(lane == HEAD_DIM + 7, l3[1], l3[2])).astype(BF16)
                qb_ref[:, cols] = jnp.where((lane >= HEAD_DIM + 6) & (lane < HEAD_DIM + 9), augl, q_ref[:, cols])

    half = pl.BlockSpec((tm, ATT_WIDTH), lambda i: (i, 0))
    wide = pl.BlockSpec((tm, N_HEADS * LANES), lambda i: (i, 0))
    return pl.pallas_call(
        body,
        out_shape=(jax.ShapeDtypeStruct((S, GMLP_WIDTH), F32), jax.ShapeDtypeStruct(qa.shape, BF16),
                   jax.ShapeDtypeStruct(qa.shape, BF16)),
        grid=(S // tm,),
        in_specs=[pl.BlockSpec((tm, D_MODEL), lambda i: (i, 0)), pl.BlockSpec((D_MODEL, D_MODEL), lambda i: (0, 0)),
                  half, half, wide],
        out_specs=(half, wide, wide),
        name="out_proj_dx_prep", compiler_params=_cp(("parallel",), VMEM_LIMIT),
    )(dh1_bf, w_out_bf, att, lse, qa)


def _attn_bwd(qb, ka, va, doa):
    S = qb.shape[0]
    tk = _pick(S, (512, 256))
    tq = tk
    nq = S // tq

    def pair(a, scale=None):
        lane = lax.broadcasted_iota(jnp.int32, (a.shape[0], LANES), 1)
        out = jnp.where(lane < HEAD_DIM, a[:, :LANES], pltpu.roll(a[:, LANES:], HEAD_DIM, 1))
        return out if scale is None else out * scale

    def head_lanes(a, col, sign, first):
        lane = lax.broadcasted_iota(jnp.int32, (a.shape[0], LANES), 1)
        return jnp.where(lane == first, sign * a[:, col:col + 1],
                         jnp.where(lane == first + 1, sign * a[:, LANES + col:LANES + col + 1], 0.0))

    def body(q_ref, do_ref, k_ref, v_ref, dqc_ref, dkc_ref, dvc_ref, dcq_ref, dck_ref, dq_ref, dka_ref, dva_ref):
        kb = pl.program_id(1)

        @pl.when(kb == 0)
        def _():
            dq_ref[...] = jnp.zeros_like(dq_ref)

        dka_ref[...] = jnp.zeros_like(dka_ref)
        dva_ref[...] = jnp.zeros_like(dva_ref)
        def sub_tile(qs, q_len, k_off, k_len, masked):
            keys = slice(k_off, k_off + k_len)
            for h in range(2):
                cols = slice(h * LANES, (h + 1) * LANES)
                qblk = q_ref[pl.ds(qs, q_len), cols]
                doblk = do_ref[pl.ds(qs, q_len), cols]
                kh = k_ref[keys, cols]
                p = jnp.exp(lax.dot_general(kh, qblk, _NT, preferred_element_type=F32))
                if masked:
                    p = jnp.where(lax.broadcasted_iota(jnp.int32, p.shape, 1) >= lax.broadcasted_iota(jnp.int32, p.shape, 0),
                                  p, 0.0)
                ds = (p * lax.dot_general(v_ref[keys, cols], doblk, _NT, preferred_element_type=F32)).astype(BF16)
                dva_ref[keys, cols] += jnp.dot(p.astype(BF16), doblk, preferred_element_type=F32)
                dka_ref[keys, cols] += jnp.dot(ds, qblk, preferred_element_type=F32)
                dq_ref[pl.ds(qs, q_len), cols] += lax.dot_general(ds, kh, _TN, preferred_element_type=F32)

        half = tk // 2
        sub_tile(pl.multiple_of(kb * tq, tq), tq, 0, half, True)
        sub_tile(pl.multiple_of(kb * tq + half, half), half, half, half, True)

        rest = nq - 1 - kb
        odd = rest % 2

        @pl.when(odd == 1)
        def _():
            sub_tile(pl.multiple_of((kb + 1) * tq, tq), tq, 0, tk, False)

        def step(t, carry):
            sub_tile(pl.multiple_of((kb + 1 + odd + 2 * t) * tq, tq), 2 * tq, 0, tk, False)
            return carry

        lax.fori_loop(0, rest // 2, step, 0)
        dka = dka_ref[...]
        dkc_ref[...] = pair(dka).astype(BF16)
        dvc_ref[...] = pair(dva_ref[...]).astype(BF16)
        first = 2 * pl.program_id(0)
        dck_ref[...] = head_lanes(dka, HEAD_DIM + 3, -1.0, first)

        @pl.when(kb == nq - 1)
        def _():
            dqa = dq_ref[...]
            dqc_ref[...] = pair(dqa, HEAD_DIM ** -0.5).astype(BF16)
            dcq_ref[...] = head_lanes(dqa, HEAD_DIM, 1.0, first)

    wide = 2 * LANES
    half = jax.ShapeDtypeStruct((S, ATT_WIDTH), BF16)
    slabs = jax.ShapeDtypeStruct((N_PAIRS, S, LANES), F32)
    return pl.pallas_call(
        body,
        out_shape=(half, half, half, slabs, slabs),
        grid=(N_PAIRS, nq),
        in_specs=[pl.BlockSpec((S, wide), lambda p, j: (0, p)), pl.BlockSpec((S, wide), lambda p, j: (0, p)),
                  pl.BlockSpec((tk, wide), lambda p, j: (j, p)), pl.BlockSpec((tk, wide), lambda p, j: (j, p))],
        out_specs=(pl.BlockSpec((S, LANES), lambda p, j: (0, p)), pl.BlockSpec((tk, LANES), lambda p, j: (j, p)),
                   pl.BlockSpec((tk, LANES), lambda p, j: (j, p)), pl.BlockSpec((None, S, LANES), lambda p, j: (p, 0, 0)),
                   pl.BlockSpec((None, tk, LANES), lambda p, j: (p, j, 0))),
        scratch_shapes=[pltpu.VMEM((S, wide), F32), pltpu.VMEM((tk, wide), F32), pltpu.VMEM((tk, wide), F32)],
        name="attn_bwd", compiler_params=_cp(("parallel", "arbitrary"), VMEM_LIMIT),
    )(qb, doa, ka, va)


def _gmlp_bwd(ug, dsg, gain, w_s, wt_s, bias_full):
    S = ug.shape[0]
    tm = _pick(S, (512, 256, 128))
    n_chunks = tm // CHUNK
    n_i = S // tm
    ones = _group_ones()
    nt = (((1,), (1,)), ((), ()))

    def body(ug_ref, dsg_ref, gain_ref, w_ref, wt_ref, bias_ref, ones_ref, dug_ref, dw_ref, dgain_ref, dbias_ref,
             dbacc_ref):
        i = pl.program_id(0)

        @pl.when(i == 0)
        def _():
            dw_ref[...] = jnp.zeros_like(dw_ref)
            dgain_ref[...] = jnp.zeros_like(dgain_ref)
            dbacc_ref[...] = jnp.zeros_like(dbacc_ref)

        ones_m = ones_ref[...]
        pu = ug_ref[:, :GMLP_WIDTH]
        pg = ug_ref[:, GMLP_WIDTH:]
        u = _gelu(pu)
        vr = _gelu(pg)
        ms = _dot3(vr * vr, ones_m) * (1.0 / GROUP_DIM)
        rinv = lax.rsqrt(ms + EPS)
        vhat = vr * rinv
        gain_v = gain_ref[...]
        vn = (vhat * gain_v).astype(BF16)
        mixed = _gmlp_mixed(vn, w_ref, bias_ref[...], n_chunks)
        dsg_v = dsg_ref[...]
        du = dsg_v * mixed
        dmixed = dsg_v * u
        dm_bf = dmixed.astype(BF16)
        lane = lax.broadcasted_iota(jnp.int32, (CHUNK, LANES), 1)
        row = lax.broadcasted_iota(jnp.int32, (CHUNK, CHUNK), 0)
        col = lax.broadcasted_iota(jnp.int32, (CHUNK, CHUNK), 1)
        wts = [jnp.where(col >= row, wt_ref[g], 0.0).astype(BF16) for g in range(N_GROUPS)]
        dvn_rows = []
        dbsum = jnp.zeros((CHUNK, GMLP_WIDTH), F32)
        for ci in range(n_chunks):
            rs = slice(ci * CHUNK, (ci + 1) * CHUNK)
            dbsum = dbsum + dmixed[rs, :]
            cols = []
            for pp in range(N_GROUPS // 2):
                cs = slice(pp * LANES, (pp + 1) * LANES)
                dm = dm_bf[rs, cs]
                dm_lo = jnp.where(lane < GROUP_DIM, dm, jnp.zeros_like(dm))
                dm_hi = jnp.where(lane >= GROUP_DIM, dm, jnp.zeros_like(dm))
                vb = vn[rs, cs]
                dw_ref[2 * pp] += lax.dot_general(dm_lo, vb, nt, preferred_element_type=F32)
                dw_ref[2 * pp + 1] += lax.dot_general(dm_hi, vb, nt, preferred_element_type=F32)
                cols.append(jnp.dot(wts[2 * pp], dm_lo, preferred_element_type=F32)
                            + jnp.dot(wts[2 * pp + 1], dm_hi, preferred_element_type=F32))
            dvn_rows.append(jnp.concatenate(cols, axis=1))
        dvn = jnp.concatenate(dvn_rows, axis=0)
        dbacc_ref[...] += dbsum
        dgain_ref[0:1, :] += jnp.sum(dvn * vhat, axis=0, keepdims=True)
        dvhat = dvn * gain_v
        gm = _dot3(dvhat * vhat, ones_m) * (1.0 / GROUP_DIM)
        dvr = rinv * (dvhat - vhat * gm)
        dug_ref[:, :GMLP_WIDTH] = (du * _gelu_grad(pu)).astype(BF16)
        dug_ref[:, GMLP_WIDTH:] = (dvr * _gelu_grad(pg)).astype(BF16)

        @pl.when(i == n_i - 1)
        def _():
            for g in range(N_GROUPS):
                dw_ref[g] = jnp.where(row >= col, dw_ref[g], 0.0)
            dbias_ref[...] = _dot3(dbacc_ref[...], ones_m)

    return pl.pallas_call(
        body,
        out_shape=(jax.ShapeDtypeStruct((S, 2 * GMLP_WIDTH), BF16), jax.ShapeDtypeStruct((N_GROUPS, CHUNK, CHUNK), F32),
                   jax.ShapeDtypeStruct((8, GMLP_WIDTH), F32), jax.ShapeDtypeStruct((CHUNK, GMLP_WIDTH), F32)),
        grid=(n_i,),
        in_specs=[pl.BlockSpec((tm, 2 * GMLP_WIDTH), lambda i: (i, 0)), pl.BlockSpec((tm, GMLP_WIDTH), lambda i: (i, 0)),
                  pl.BlockSpec((1, GMLP_WIDTH), lambda i: (0, 0)),
                  pl.BlockSpec((N_GROUPS, CHUNK, CHUNK), lambda i: (0, 0, 0)),
                  pl.BlockSpec((N_GROUPS, CHUNK, CHUNK), lambda i: (0, 0, 0)),
                  pl.BlockSpec((CHUNK, GMLP_WIDTH), lambda i: (0, 0)),
                  pl.BlockSpec((GMLP_WIDTH, GMLP_WIDTH), lambda i: (0, 0))],
        out_specs=(pl.BlockSpec((tm, 2 * GMLP_WIDTH), lambda i: (i, 0)),
                   pl.BlockSpec((N_GROUPS, CHUNK, CHUNK), lambda i: (0, 0, 0)),
                   pl.BlockSpec((8, GMLP_WIDTH), lambda i: (0, 0)),
                   pl.BlockSpec((CHUNK, GMLP_WIDTH), lambda i: (0, 0))),
        scratch_shapes=[pltpu.VMEM((CHUNK, GMLP_WIDTH), F32)],
        name="gmlp_bwd", compiler_params=_cp(("arbitrary",), VMEM_LIMIT),
    )(ug, dsg, gain, w_s, wt_s, bias_full, ones)


def _gate_bwd(dcq, dck, zf):
    S = zf.shape[0]
    tm = _pick(S, (256,))
    n_i = S // tm
    triu = (lax.broadcasted_iota(jnp.int32, (tm, tm), 0) <= lax.broadcasted_iota(jnp.int32, (tm, tm), 1)).astype(BF16)

    def body(dcq_ref, dck_ref, zf_ref, tri_ref, dzf_ref, dbf_ref, carry_ref):
        i = pl.program_id(0)

        @pl.when(i == 0)
        def _():
            carry_ref[...] = jnp.zeros_like(carry_ref)
            dbf_ref[...] = jnp.zeros_like(dbf_ref)

        lane = lax.broadcasted_iota(jnp.int32, (tm, LANES), 1)
        dc = dcq_ref[0] + dck_ref[0]
        for p in range(1, N_PAIRS):
            dc = dc + (dcq_ref[p] + dck_ref[p])
        dlf = _dot3l(tri_ref[...], dc) + carry_ref[0:1, :]
        carry_ref[0:1, :] = dlf[0:1, :]
        dz = jnp.where(lane < N_HEADS, dlf * _sigmoid(-zf_ref[...]), 0.0)
        dzf_ref[...] = dz.astype(BF16)
        dbf_ref[0:1, :] += jnp.sum(dz, axis=0, keepdims=True)

    return pl.pallas_call(
        body,
        out_shape=(jax.ShapeDtypeStruct((S, LANES), BF16), jax.ShapeDtypeStruct((8, LANES), F32)),
        grid=(n_i,),
        in_specs=[pl.BlockSpec((N_PAIRS, tm, LANES), lambda i: (0, n_i - 1 - i, 0)),
                  pl.BlockSpec((N_PAIRS, tm, LANES), lambda i: (0, n_i - 1 - i, 0)),
                  pl.BlockSpec((tm, LANES), lambda i: (n_i - 1 - i, 0)),
                  pl.BlockSpec((tm, tm), lambda i: (0, 0))],
        out_specs=(pl.BlockSpec((tm, LANES), lambda i: (n_i - 1 - i, 0)), pl.BlockSpec((8, LANES), lambda i: (0, 0))),
        scratch_shapes=[pltpu.VMEM((8, LANES), F32)],
        name="gate_bwd", compiler_params=_cp(("arbitrary",), VMEM_LIMIT),
    )(dcq, dck, zf, triu)


def _out_proj_fwd(att_bf, sg, w_out_bf, x, g_ffn):
    S = x.shape[0]
    tm = _pick(S, (512, 256))

    def body(a_ref, s_ref, w_ref, x_ref, g_ref, h_ref, hn_ref):
        h = (x_ref[...] + jnp.dot(a_ref[...], w_ref[:ATT_WIDTH, :], preferred_element_type=F32)
             + jnp.dot(s_ref[...], w_ref[ATT_WIDTH:, :], preferred_element_type=F32))
        h_ref[...] = h
        r = lax.rsqrt(jnp.mean(h * h, axis=-1, keepdims=True) + EPS)
        hn_ref[...] = ((h * r) * g_ref[...]).astype(BF16)

    row = pl.BlockSpec((tm, D_MODEL), lambda i: (i, 0))
    half = pl.BlockSpec((tm, ATT_WIDTH), lambda i: (i, 0))
    return pl.pallas_call(
        body, out_shape=(jax.ShapeDtypeStruct((S, D_MODEL), F32), jax.ShapeDtypeStruct((S, D_MODEL), BF16)),
        grid=(S // tm,),
        in_specs=[half, half, pl.BlockSpec((D_MODEL, D_MODEL), lambda i: (0, 0)), row,
                  pl.BlockSpec((1, D_MODEL), lambda i: (0, 0))],
        out_specs=(row, row), name="out_proj", compiler_params=_cp(("parallel",), VMEM_LIMIT),
    )(att_bf, sg, w_out_bf, x, g_ffn)


def _out_proj_dw(att_bf, sg, dh1_bf):
    S = att_bf.shape[0]
    tk = _pick(S, (1024, 512))

    def body(a_ref, s_ref, d_ref, o_ref):
        k = pl.program_id(0)

        @pl.when(k == 0)
        def _():
            o_ref[...] = jnp.zeros_like(o_ref)

        d = d_ref[...]
        o_ref[:ATT_WIDTH, :] += lax.dot_general(a_ref[...], d, _TN, preferred_element_type=F32)
        o_ref[ATT_WIDTH:, :] += lax.dot_general(s_ref[...], d, _TN, preferred_element_type=F32)

    half = pl.BlockSpec((tk, ATT_WIDTH), lambda k: (k, 0))
    return pl.pallas_call(
        body, out_shape=jax.ShapeDtypeStruct((D_MODEL, D_MODEL), F32), grid=(S // tk,),
        in_specs=[half, half, pl.BlockSpec((tk, D_MODEL), lambda k: (k, 0))],
        out_specs=pl.BlockSpec((D_MODEL, D_MODEL), lambda k: (0, 0)),
        name="out_proj_dw", compiler_params=_cp(("arbitrary",), VMEM_LIMIT),
    )(att_bf, sg, dh1_bf)


_IN_PIECES = ((0, ATT_WIDTH), (ATT_WIDTH, ATT_WIDTH), (2 * ATT_WIDTH, ATT_WIDTH), (QKV, 2 * GMLP_WIDTH), (UG_END, LANES))


def _inproj_bwd_dx(pieces, w_pad, x, g_mix, dh1):
    S = x.shape[0]
    tm = _pick(S, (512, 256))

    def body(*refs):
        p_refs, (w_ref, x_ref, g_ref, r_ref, dx_ref, dg_ref) = refs[:5], refs[5:]
        i = pl.program_id(0)

        @pl.when(i == 0)
        def _():
            dg_ref[...] = jnp.zeros_like(dg_ref)

        dxn = None
        for p_ref, (c0, width) in zip(p_refs, _IN_PIECES):
            part = jnp.dot(p_ref[...], w_ref[c0:c0 + width, :], preferred_element_type=F32)
            dxn = part if dxn is None else dxn + part
        xf = x_ref[...]
        r = lax.rsqrt(jnp.mean(xf * xf, axis=-1, keepdims=True) + EPS)
        xhat = xf * r
        dg_ref[0:1, :] += jnp.sum(dxn * xhat, axis=0, keepdims=True)
        dhat = dxn * g_ref[...]
        dx_ref[...] = r_ref[...] + r * (dhat - xhat * jnp.mean(dhat * xhat, axis=-1, keepdims=True))

    row = pl.BlockSpec((tm, D_MODEL), lambda i: (i, 0))
    return pl.pallas_call(
        body, out_shape=(jax.ShapeDtypeStruct((S, D_MODEL), F32), jax.ShapeDtypeStruct((8, D_MODEL), F32)),
        grid=(S // tm,),
        in_specs=[pl.BlockSpec((tm, width), lambda i: (i, 0)) for _, width in _IN_PIECES]
        + [pl.BlockSpec((IN_PAD, D_MODEL), lambda i: (0, 0)), row, pl.BlockSpec((1, D_MODEL), lambda i: (0, 0)), row],
        out_specs=(row, pl.BlockSpec((8, D_MODEL), lambda i: (0, 0))),
        name="in_proj_dx", compiler_params=_cp(("arbitrary",), VMEM_LIMIT),
    )(*pieces, w_pad, x, g_mix, dh1)


def _inproj_bwd_dw(xn, pieces):
    S = xn.shape[0]
    tk = _pick(S, (1024, 512))

    def body(*refs):
        x_ref, p_refs, o_ref = refs[0], refs[1:6], refs[6]
        k = pl.program_id(0)

        @pl.when(k == 0)
        def _():
            o_ref[...] = jnp.zeros_like(o_ref)

        xb = x_ref[...]
        for p_ref, (c0, width) in zip(p_refs, _IN_PIECES):
            o_ref[:, c0:c0 + width] += lax.dot_general(xb, p_ref[...], _TN, preferred_element_type=F32)

    return pl.pallas_call(
        body, out_shape=jax.ShapeDtypeStruct((D_MODEL, IN_PAD), F32), grid=(S // tk,),
        in_specs=[pl.BlockSpec((tk, D_MODEL), lambda k: (k, 0))]
        + [pl.BlockSpec((tk, width), lambda k: (k, 0)) for _, width in _IN_PIECES],
        out_specs=pl.BlockSpec((D_MODEL, IN_PAD), lambda k: (0, 0)),
        name="in_proj_dw", compiler_params=_cp(("arbitrary",), VMEM_LIMIT),
    )(xn, *pieces)


def _adamw(w, m, v, parts, name):
    R, C = w.shape[-2:]
    tr = R
    for cand in (256, 128, 64, 32, 16, 8):
        if R % cand == 0 and R > cand:
            tr = cand
            break
    c1 = 1.0 / (1.0 - ADAM_B1 ** ADAM_STEP)
    c2 = 1.0 / (1.0 - ADAM_B2 ** ADAM_STEP)

    def body(w_ref, m_ref, v_ref, p_ref, g_ref, d_ref, nm_ref, nv_ref):
        g = p_ref[0].astype(F32)
        for j in range(1, N_DEV):
            g = g + p_ref[j].astype(F32)
        g_ref[...] = g
        nm = ADAM_B1 * m_ref[...] + (1.0 - ADAM_B1) * g
        nv = ADAM_B2 * v_ref[...] + (1.0 - ADAM_B2) * (g * g)
        nm_ref[...] = nm
        nv_ref[...] = nv
        d_ref[...] = -ADAM_LR * ((nm * c1) / (jnp.sqrt(nv * c2) + ADAM_EPS) + ADAM_WD * w_ref[...])

    if w.ndim == 3:
        spec = pl.BlockSpec((None, tr, C), lambda i: (0, i, 0))
    else:
        spec = pl.BlockSpec((tr, C), lambda i: (i, 0))
    shp = jax.ShapeDtypeStruct(w.shape, F32)
    return pl.pallas_call(
        body, out_shape=(shp, shp, shp, shp), grid=(R // tr,),
        in_specs=[spec, spec, spec, pl.BlockSpec((N_DEV, tr, C), lambda i: (0, i, 0))],
        out_specs=(spec, spec, spec, spec),
        name=name, compiler_params=_cp(("parallel",), VMEM_LIMIT),
    )(w, m, v, parts)


def _adamw_owner(w, m, v, landed, sent, me, name):
    R, C = w.shape[-2:]
    tr = R
    for cand in (256, 128, 64, 32, 16, 8):
        if R % cand == 0 and R > cand:
            tr = cand
            break
    c1 = 1.0 / (1.0 - ADAM_B1 ** ADAM_STEP)
    c2 = 1.0 / (1.0 - ADAM_B2 ** ADAM_STEP)

    def body(me_ref, w_ref, m_ref, v_ref, p_ref, own_ref, g_ref, d_ref, nm_ref, nv_ref):
        mine = me_ref[0]
        own = own_ref[...].astype(F32)
        g = jnp.where(mine == 0, own, p_ref[0].astype(F32))
        for j in range(1, N_DEV):
            g = g + jnp.where(mine == j, own, p_ref[j].astype(F32))
        g_ref[...] = g
        nm = ADAM_B1 * m_ref[...] + (1.0 - ADAM_B1) * g
        nv = ADAM_B2 * v_ref[...] + (1.0 - ADAM_B2) * (g * g)
        nm_ref[...] = nm
        nv_ref[...] = nv
        d_ref[...] = -ADAM_LR * ((nm * c1) / (jnp.sqrt(nv * c2) + ADAM_EPS) + ADAM_WD * w_ref[...])

    spec = pl.BlockSpec((None, tr, C), lambda i, me_ref: (0, i, 0))
    shp = jax.ShapeDtypeStruct(w.shape, F32)
    return pl.pallas_call(
        body, out_shape=(shp, shp, shp, shp),
        grid_spec=pltpu.PrefetchScalarGridSpec(
            num_scalar_prefetch=1, grid=(R // tr,),
            in_specs=[spec, spec, spec, pl.BlockSpec((N_DEV, tr, C), lambda i, me_ref: (0, i, 0)),
                      pl.BlockSpec((None, tr, C), lambda i, me_ref: (me_ref[0], i, 0))],
            out_specs=(spec, spec, spec, spec)),
        name=name, compiler_params=_cp(("parallel",), VMEM_LIMIT),
    )(jnp.reshape(me, (1,)).astype(jnp.int32), w, m, v, landed, sent)


def _adamw_packed(w, m, v, parts, sizes, name):
    R = w.shape[0]
    assert R == sum(sizes)
    c1 = 1.0 / (1.0 - ADAM_B1 ** ADAM_STEP)
    c2 = 1.0 / (1.0 - ADAM_B2 ** ADAM_STEP)
    n = len(sizes)

    def body(w_ref, m_ref, v_ref, p_ref, *out_refs):
        g = p_ref[0]
        for j in range(1, N_DEV):
            g = g + p_ref[j]
        nm = ADAM_B1 * m_ref[...] + (1.0 - ADAM_B1) * g
        nv = ADAM_B2 * v_ref[...] + (1.0 - ADAM_B2) * (g * g)
        d = -ADAM_LR * ((nm * c1) / (jnp.sqrt(nv * c2) + ADAM_EPS) + ADAM_WD * w_ref[...])
        for kind, val in enumerate((g, d, nm, nv)):
            off = 0
            for k, rows in enumerate(sizes):
                out_refs[kind * n + k][...] = val[off:off + rows, :]
                off += rows

    whole = pl.BlockSpec((R, LANES), lambda i: (0, 0))
    shapes = [jax.ShapeDtypeStruct((rows, LANES), F32) for rows in sizes] * 4
    res = pl.pallas_call(
        body, out_shape=tuple(shapes), grid=(1,),
        in_specs=[whole, whole, whole, pl.BlockSpec((N_DEV, R, LANES), lambda i: (0, 0, 0))],
        out_specs=tuple(pl.BlockSpec((rows, LANES), lambda i: (0, 0)) for rows in sizes) * 4,
        name=name, compiler_params=_cp(("arbitrary",), VMEM_LIMIT),
    )(w, m, v, parts)
    return [list(res[kind * n:(kind + 1) * n]) for kind in range(4)]


def _place():
    x, y, c = lax.axis_index("x"), lax.axis_index("y"), lax.axis_index("c")
    return x, y, c


def _all_gather(blocks, name):
    n = len(blocks)

    def body(*refs):
        ins, outs = refs[:n], refs[n:2 * n]
        send_sems, recv_sems, local_sems = refs[2 * n:]
        x, y, c = _place()
        me, sibling = (x, y, c), (x, y, 1 - c)
        chips = [(1 - x, y), (x, 1 - y), (1 - x, 1 - y)]
        sends = []
        for a in range(n):
            out = outs[a]

            def slot(px, py, pc, out=out):
                return out.at[4 * px + 2 * py + pc]

            def copy(k, block, to, src=None, a=a, slot=slot):
                return pltpu.make_async_remote_copy(
                    src_ref=slot(*block) if src is None else src, dst_ref=slot(*block),
                    send_sem=send_sems.at[a, k], recv_sem=recv_sems.at[a, k], device_id=to, device_id_type=MESH)

            mine = pltpu.make_async_copy(ins[a], slot(*me), local_sems.at[a])
            mine.start()
            first = [copy(0, me, sibling, src=ins[a])]
            first += [copy(1 + j, me, (*chip, c), src=ins[a]) for j, chip in enumerate(chips)]
            for cp in first:
                cp.start()
            sends.append((mine, first, copy))
        for a in range(n):
            mine, first, copy = sends[a]
            passed = [copy(4 + j, (*chip, c), sibling) for j, chip in enumerate(chips)]
            for j, chip in enumerate(chips):
                copy(1 + j, (*chip, c), me).wait_recv()
                passed[j].start()
            copy(0, sibling, me).wait_recv()
            for j, chip in enumerate(chips):
                copy(4 + j, (*chip, 1 - c), me).wait_recv()
            for cp in first + passed:
                cp.wait_send()
            mine.wait()

    any_spec = pl.BlockSpec(memory_space=pl.ANY)
    return pl.pallas_call(
        body, out_shape=tuple(jax.ShapeDtypeStruct((N_DEV,) + b.shape, b.dtype) for b in blocks),
        in_specs=[any_spec] * n, out_specs=tuple([any_spec] * n),
        scratch_shapes=[pltpu.SemaphoreType.DMA((n, 7)), pltpu.SemaphoreType.DMA((n, 7)), pltpu.SemaphoreType.DMA((n,))],
        name=name,
    )(*blocks)


_HBM = pl.BlockSpec(memory_space=pltpu.HBM)
_SEM = pl.BlockSpec(memory_space=pltpu.SEMAPHORE)
_EFFECT = pltpu.SideEffectType.DATAFLOW_SIDE_EFFECTING


def _peers(x, y, c):
    out = []
    for k in range(1, N_DEV):
        px, py, pc = x ^ ((k >> 2) & 1), y ^ ((k >> 1) & 1), c ^ (k & 1)
        out.append((k, (px, py, pc), 4 * px + 2 * py + pc))
    return out


def _xchg_copies(src_refs, land_refs, send_sems, recv_sems, scatter):
    x, y, c = _place()
    me = 4 * x + 2 * y + c
    copies = []
    for a, (src, land) in enumerate(zip(src_refs, land_refs)):
        for k, place, idx in _peers(x, y, c):
            j = a * (N_DEV - 1) + k - 1
            copies.append(pltpu.make_async_remote_copy(
                src_ref=src.at[idx] if scatter[a] else src, dst_ref=land.at[me],
                send_sem=send_sems[j], recv_sem=recv_sems[j], device_id=place, device_id_type=MESH))
    return copies


def _xchg_start(srcs, scatter, name):
    n = len(srcs)
    lands = [lax.empty((N_DEV,) + (s.shape[1:] if sc else s.shape), s.dtype) for s, sc in zip(srcs, scatter)]

    ns = n * (N_DEV - 1)

    def body(*refs):
        sems = refs[2 * n:2 * n + 2 * ns]
        for cp in _xchg_copies(refs[:n], refs[n:2 * n], sems[:ns], sems[ns:], scatter):
            cp.start()
        token = refs[-1]
        token[...] = jnp.zeros_like(token)

    both = list(srcs) + lands
    res = pl.pallas_call(
        body, name=name,
        out_shape=(*[pltpu.SemaphoreType.DMA(())] * (2 * ns),
                   *[pltpu.HBM(a.shape, a.dtype) for a in both], jax.ShapeDtypeStruct((8, LANES), F32)),
        in_specs=[_HBM] * (2 * n),
        out_specs=(*([_SEM] * (2 * ns)), *([_HBM] * (2 * n)), pl.BlockSpec(memory_space=pltpu.VMEM)),
        input_output_aliases={i: 2 * ns + i for i in range(2 * n)},
        compiler_params=pltpu.CompilerParams(has_side_effects=_EFFECT),
    )(*[pltpu.with_memory_space_constraint(a, pltpu.HBM) for a in both])
    return (tuple(res[:2 * ns]), tuple(res[2 * ns:2 * ns + 2 * n])), res[-1]


def _xchg_wait(handle, scatter, after, name):
    sems, thru = handle
    n = len(thru) // 2
    ns = len(sems) // 2

    def body(*refs):
        got = refs[2 * n:2 * n + 2 * ns]
        for cp in _xchg_copies(refs[:n], refs[n:2 * n], got[:ns], got[ns:], scatter):
            cp.wait_send()
            cp.wait_recv()

    outs = pl.pallas_call(
        body, name=name, out_shape=tuple(pltpu.HBM(a.shape, a.dtype) for a in thru),
        in_specs=[_HBM] * (2 * n) + [_SEM] * (2 * ns) + [pl.BlockSpec(memory_space=pl.ANY)],
        out_specs=tuple([_HBM] * (2 * n)), input_output_aliases={i: i for i in range(2 * n)},
        compiler_params=pltpu.CompilerParams(has_side_effects=_EFFECT),
    )(*thru, *sems, after)
    return outs[:n], outs[n:]


def _tie(a, token):
    return a if token is None else a + token[0, 0].astype(a.dtype)


def _rows128(a):
    flat = a.reshape(-1)
    rows = -(-flat.shape[0] // LANES)
    rows = -(-rows // 8) * 8
    return jnp.pad(flat, (0, rows * LANES - flat.shape[0])).reshape(rows, LANES)


def _local_step(x, target, norm_mix_g, w_in_t, b_forget, gmlp_norm_g, w_spatial, b_spatial, norm_ffn_g, conv_b,
                norm_final_g, rest_fn, send_fn, small_fn, token=None):
    f = D_FF
    g_mix = norm_mix_g.reshape(1, D_MODEL)
    w_pad = jnp.pad(w_in_t, ((0, IN_PAD - IN_COLS), (0, 0)))
    bf_pad = jnp.pad(b_forget.reshape(1, N_HEADS), ((0, 0), (0, LANES - N_HEADS)))
    xn, qa, ka, va, ug, zf = _inproj_fwd(x, _tie(g_mix, token), w_pad, bf_pad)
    bias_full = jnp.repeat(b_spatial.reshape(N_GROUPS, CHUNK).T, GROUP_DIM, axis=1)
    w_s = w_spatial.reshape(N_GROUPS, CHUNK, CHUNK)
    gain = gmlp_norm_g.reshape(1, GMLP_WIDTH)
    sg = _gmlp_fwd(ug, gain, w_s, bias_full)
    att, lse, att_bf = _attn_fwd(qa, ka, va)
    w_out_bf, w_up_bf, conv_w, w_down_bf = rest_fn(att_bf)
    g_ffn = norm_ffn_g.reshape(1, D_MODEL)
    h1, hn = _out_proj_fwd(att_bf, sg, w_out_bf, x, g_ffn)
    cw = jnp.pad(conv_w.reshape(3, 2, f).transpose(1, 0, 2), ((0, 0), (0, 5), (0, 0)))
    cb = conv_b.reshape(2, 1, f)
    hu, hc, act = _ffn_up_conv(hn, w_up_bf, cw, cb)
    loss_blk, dh2, dh2_bf, dg_final = _ffn_down_loss(act, w_down_bf, h1, norm_final_g.reshape(1, D_MODEL), target)
    dw_down = _mm(act, dh2_bf, mode="tn", out_dtype=F32, tm=1408, tn=1024, tk=2048, name="ffn_down_dw")
    dact = _mm(dh2_bf, w_down_bf, mode="nt", out_dtype=F32, tm=1024, tn=1408, tk=1024, outer="j", name="ffn_down_dx")
    dhu, dcw = _conv_gate_bwd(hc, hu, dact, _tie(cw, send_fn("w_down", dw_down)))
    dw_up = _mm(hn, dhu, mode="tn", out_dtype=F32, tm=1024, tn=1408, tk=2048, b_halves=True, outer="j", name="ffn_up_dw")
    dh1, dh1_bf, dg_ffn = _ffn_up_dx_rms(dhu, w_up_bf, h1, _tie(g_ffn, send_fn("w_up", dw_up)), dh2)
    dsg, qb, doa = _out_proj_dx_prep(dh1_bf, w_out_bf, att, lse, qa)
    dw_out = _out_proj_dw(att_bf, sg, dh1_bf)
    dq, dk, dv, dcq, dck = _attn_bwd(qb, ka, va, doa)
    wt_s = w_s.transpose(0, 2, 1)
    dug, dw_s, dgain, dbias = _gmlp_bwd(ug, dsg, _tie(gain, send_fn("w_out", dw_out)), w_s, wt_s, bias_full)
    dzf, dbf = _gate_bwd(dcq, dck, zf)
    dw_in = _inproj_bwd_dw(xn, (dq, dk, dv, dug, dzf))
    sent_in = send_fn("w_in", dw_in[:, :IN_COLS])
    grad_x, dg_mix = _inproj_bwd_dx((dq, dk, dv, dug, _tie(dzf, sent_in)), w_pad, x, g_mix, dh1)
    grads = dict(
        norm_mix_g=dg_mix[0:1, :],
        b_forget=dbf[0:1, :N_HEADS],
        gmlp_norm_g=dgain[0:1, :],
        w_spatial=dw_s,
        b_spatial=dbias[:, ::GROUP_DIM].T,
        norm_ffn_g=dg_ffn[0:1, :],
        conv_w=dcw[:, 0:3, :].transpose(1, 0, 2).reshape(3, 2 * f),
        conv_b=dcw[:, 3, :].reshape(1, 2 * f),
        norm_final_g=dg_final[0, :],
    )
    return grad_x, small_fn(loss_blk[0, 0], grads)


SMALL = ("norm_mix_g", "b_forget", "gmlp_norm_g", "w_spatial", "b_spatial", "norm_ffn_g", "conv_b", "norm_final_g")


def kernel(x, norm_mix_g, w_in, b_forget, gmlp_norm_g, w_spatial, b_spatial, w_out, norm_ffn_g, w_up, conv_w, conv_b, w_down, norm_final_g, loss_target, m_norm_mix_g, m_w_in, m_b_forget, m_gmlp_norm_g, m_w_spatial, m_b_spatial, m_w_out, m_norm_ffn_g, m_w_up, m_conv_w, m_conv_b, m_w_down, m_norm_final_g, v_norm_mix_g, v_w_in, v_b_forget, v_gmlp_norm_g, v_w_spatial, v_b_spatial, v_w_out, v_norm_ffn_g, v_w_up, v_conv_w, v_conv_b, v_w_down, v_norm_final_g):
    weights = dict(norm_mix_g=norm_mix_g, w_in=w_in, b_forget=b_forget, gmlp_norm_g=gmlp_norm_g, w_spatial=w_spatial,
                   b_spatial=b_spatial, w_out=w_out, norm_ffn_g=norm_ffn_g, w_up=w_up, conv_w=conv_w, conv_b=conv_b,
                   w_down=w_down, norm_final_g=norm_final_g)
    m_in = dict(norm_mix_g=m_norm_mix_g, w_in=m_w_in, b_forget=m_b_forget, gmlp_norm_g=m_gmlp_norm_g,
                w_spatial=m_w_spatial, b_spatial=m_b_spatial, w_out=m_w_out, norm_ffn_g=m_norm_ffn_g, w_up=m_w_up,
                conv_w=m_conv_w, conv_b=m_conv_b, w_down=m_w_down, norm_final_g=m_norm_final_g)
    v_in = dict(norm_mix_g=v_norm_mix_g, w_in=v_w_in, b_forget=v_b_forget, gmlp_norm_g=v_gmlp_norm_g,
                w_spatial=v_w_spatial, b_spatial=v_b_spatial, w_out=v_w_out, norm_ffn_g=v_norm_ffn_g, w_up=v_w_up,
                conv_w=v_conv_w, conv_b=v_conv_b, w_down=v_w_down, norm_final_g=v_norm_final_g)
    order = list(weights)
    me = 4 * lax.axis_index("x") + 2 * lax.axis_index("y") + lax.axis_index("c")
    n_in, n_up = w_in.shape[2], w_up.shape[2]
    r_out, r_down = w_out.shape[1], w_down.shape[1]

    def with_mine(landed, mine):
        return lax.dynamic_update_index_in_dim(landed, mine, me, 0)

    up_blk = w_up[0].T.astype(BF16)
    out_blk = w_out[0].astype(BF16)
    down_blk = w_down[0].astype(BF16)
    taps_blk = jnp.pad(conv_w[0], ((0, 5), (0, 0)))
    (in_all,) = _all_gather([w_in[0].T.astype(BF16)], "gather_w_in")
    in_all, rest_blocks = lax.optimization_barrier((in_all, [up_blk, out_blk, down_blk, taps_blk]))
    rest_handle, token = _xchg_start(rest_blocks, [False] * 4, "gather_rest_start")
    w_in_t = in_all.reshape(N_DEV * n_in, D_MODEL)

    def rest_fn(after):
        mine, landed = _xchg_wait(rest_handle, [False] * 4, after, "gather_rest_wait")
        up_all, out_all, down_all, taps_all = [with_mine(l, b) for l, b in zip(landed, mine)]
        return (out_all.reshape(N_DEV * r_out, D_MODEL), up_all.reshape(N_DEV * n_up, D_MODEL),
                taps_all[:, :3, :].transpose(1, 0, 2).reshape(3, N_DEV * n_up),
                down_all.reshape(N_DEV * r_down, D_MODEL))

    sent = {}

    def send_fn(name, grad):
        if name == "w_in":
            parts = grad.reshape(D_MODEL, N_DEV, -1).transpose(1, 0, 2).astype(BF16)
        elif name == "w_up":
            parts = grad.reshape(D_MODEL, N_DEV, -1).transpose(1, 0, 2)
        else:
            parts = grad.reshape(N_DEV, -1, D_MODEL)
        sent[name], tok = _xchg_start([parts], [True], "scatter_" + name + "_start")
        return tok

    small = {}

    def small_fn(loss_local, g):
        loss_rows = jnp.pad(loss_local.reshape(1, 1), ((0, 31), (0, LANES - 1)))
        packed = [_rows128(g[k]) for k in SMALL] + [loss_rows, _rows128(g["conv_w"])]
        small["sizes"] = [p.shape[0] for p in packed]
        small["handle"], tok = _xchg_start([jnp.concatenate(packed, axis=0)], [False], "gather_small_start")
        return tok

    grad_x, after = _local_step(
        x[0], loss_target[0], norm_mix_g, w_in_t, b_forget, gmlp_norm_g, w_spatial, b_spatial, norm_ffn_g, conv_b,
        norm_final_g, rest_fn, send_fn, small_fn, token)

    outs = {}

    def update_big(name, after):
        (parts,), (landed,) = _xchg_wait(sent[name], [True], after, "scatter_" + name + "_wait")
        outs[name] = tuple(_adamw_owner(weights[name], m_in[name], v_in[name], landed, parts, me, "adamw_" + name))
        return outs[name][0]

    for name in ("w_down", "w_up", "w_out"):
        after = update_big(name, after)

    (mine,), (landed,) = _xchg_wait(small["handle"], [False], after, "gather_small_wait")
    small_all = with_mine(landed, mine)
    sizes = small["sizes"]
    n_small_rows = sum(sizes[:-2])

    def pack(src):
        return jnp.concatenate([_rows128(src[k]) for k in SMALL] + [jnp.zeros((sizes[-2], LANES), F32)], axis=0)

    n_adam_rows = n_small_rows + sizes[-2]
    per_kind = _adamw_packed(pack(weights), pack(m_in), pack(v_in), small_all, sizes[:-1], "adamw_small")
    loss = per_kind[0][-1][0, 0]
    for j, k in enumerate(SMALL):
        shp = weights[k].shape
        cnt = math.prod(shp)
        outs[k] = tuple(kind[j].reshape(-1)[:cnt].reshape(shp) for kind in per_kind)
    sg_ = per_kind[0][0]
    taps_parts = small_all[:, n_adam_rows:, :].reshape(N_DEV, -1)[:, :3 * N_DEV * n_up].reshape(N_DEV, 3, N_DEV * n_up)
    taps_mine = lax.dynamic_slice_in_dim(taps_parts, me * n_up, n_up, axis=2)
    taps_mine = jnp.pad(taps_mine, ((0, 0), (0, 5), (0, 0)))

    def pad8(a):
        return jnp.pad(a[0], ((0, 5), (0, 0)))

    res = _adamw(pad8(conv_w), pad8(m_conv_w), pad8(v_conv_w), taps_mine, "adamw_conv_w")
    outs["conv_w"] = tuple(a[:3][None] for a in res)
    update_big("w_in", sg_)

    return (loss, grad_x[None], *[outs[k][0] for k in order], *[outs[k][1] for k in order],
            *[outs[k][2] for k in order], *[outs[k][3] for k in order])
```

```python
import functools
import math

import jax
import jax.numpy as jnp
from jax import lax
from jax.experimental import pallas as pl
from jax.experimental.pallas import tpu as pltpu

F32 = jnp.float32
BF16 = jnp.bfloat16

N_DEV = 8
D_MODEL = 1024
ATT_WIDTH = 512
GMLP_WIDTH = 512
HEAD_DIM = 64
N_HEADS = 8
N_PAIRS = 4
N_GROUPS = 8
GROUP_DIM = 64
CHUNK = 128
D_FF = 2816
IN_COLS = 2568
IN_PAD = 2688
QKV = 1536
UG_END = 2560
EPS = 1e-6
LANES = 128

ADAM_LR = 0.001
ADAM_B1 = 0.9
ADAM_B2 = 0.999
ADAM_EPS = 1e-08
ADAM_WD = 0.01
ADAM_STEP = 10

ATT_TQ = 1024
ATT_TK = 1024
FFN_TM, FFN_TN = 512, 1408
CONV_TM, CONV_TN = 512, 1408
VMEM_LIMIT = 56 * 1024 * 1024
MESH = pl.DeviceIdType.MESH


def _cp(sem, vmem=None):
    return pltpu.CompilerParams(dimension_semantics=sem, vmem_limit_bytes=vmem)


def _pick(n, prefs):
    for p in prefs:
        if n % p == 0:
            return p
    return n


def _split3(x):
    hi = x.astype(BF16)
    r1 = x - hi.astype(F32)
    mid = r1.astype(BF16)
    lo = (r1 - mid.astype(F32)).astype(BF16)
    return hi, mid, lo


def _dot3(x, ones_bf):
    d = functools.partial(jnp.dot, preferred_element_type=F32)
    out = []
    for c in range(0, x.shape[1], 2 * LANES):
        blk = ones_bf[c:c + 2 * LANES, c:c + 2 * LANES]
        hi, mid, lo = _split3(x[:, c:c + 2 * LANES])
        out.append(d(hi, blk) + d(mid, blk) + d(lo, blk))
    return jnp.concatenate(out, axis=1)


def _dot3l(ones_bf, x):
    n = x.shape[1]
    y = jnp.dot(ones_bf, jnp.concatenate(_split3(x), axis=1), preferred_element_type=F32)
    return y[:, :n] + y[:, n:2 * n] + y[:, 2 * n:]


def _gelu(x):
    k = math.sqrt(2.0 / math.pi)
    t = jnp.tanh(k * (x + 0.044715 * (x * x * x)))
    return 0.5 * x * (1.0 + t)


def _gelu_grad(x):
    k = math.sqrt(2.0 / math.pi)
    x2 = x * x
    t = jnp.tanh(k * (x + 0.044715 * (x2 * x)))
    return 0.5 * (1.0 + t) + 0.5 * x * (1.0 - t * t) * (k * (1.0 + 3.0 * 0.044715 * x2))


def _sigmoid(x):
    return 1.0 / (1.0 + jnp.exp(-x))


def _mm(a, b, *, mode, out_dtype, tm, tn, tk, name, res=None, a_halves=False, b_halves=False,
        out_halves=False, outer="i"):
    if mode == "tn":
        K, M = a.shape[-2], a.shape[-1] * (2 if a_halves else 1)
    else:
        M, K = a.shape[-2], a.shape[-1] * (2 if a_halves else 1)
    if mode == "nt":
        N = b.shape[-2]
        assert b.shape[-1] == K
    else:
        N = b.shape[-1] * (2 if b_halves else 1)
    tm, tn, tk = min(tm, M), min(tn, N), min(tk, K)
    assert M % tm == 0 and N % tn == 0 and K % tk == 0, (name, M, N, K, tm, tn, tk)
    nm, nn, nk = M // tm, N // tn, K // tk

    def ij(g0, g1):
        return (g0, g1) if outer == "i" else (g1, g0)

    if mode == "nn":
        dims = (((1,), (0,)), ((), ()))
        if a_halves:
            nkh = nk // 2
            a_spec = pl.BlockSpec((None, tm, tk), lambda g0, g1, k: (k // nkh, ij(g0, g1)[0], k % nkh))
        else:
            a_spec = pl.BlockSpec((tm, tk), lambda g0, g1, k: (ij(g0, g1)[0], k))
        b_spec = pl.BlockSpec((tk, tn), lambda g0, g1, k: (k, ij(g0, g1)[1]))
    elif mode == "nt":
        dims = (((1,), (1,)), ((), ()))
        if a_halves:
            nkh = nk // 2
            a_spec = pl.BlockSpec((None, tm, tk), lambda g0, g1, k: (k // nkh, ij(g0, g1)[0], k % nkh))
        else:
            a_spec = pl.BlockSpec((tm, tk), lambda g0, g1, k: (ij(g0, g1)[0], k))
        b_spec = pl.BlockSpec((tn, tk), lambda g0, g1, k: (ij(g0, g1)[1], k))
    else:
        dims = (((0,), (0,)), ((), ()))
        if a_halves:
            nmh = nm // 2
            a_spec = pl.BlockSpec((None, tk, tm), lambda g0, g1, k: (ij(g0, g1)[0] // nmh, k, ij(g0, g1)[0] % nmh))
        else:
            a_spec = pl.BlockSpec((tk, tm), lambda g0, g1, k: (k, ij(g0, g1)[0]))
        if b_halves:
            nnh = nn // 2
            b_spec = pl.BlockSpec((None, tk, tn), lambda g0, g1, k: (ij(g0, g1)[1] // nnh, k, ij(g0, g1)[1] % nnh))
        else:
            b_spec = pl.BlockSpec((tk, tn), lambda g0, g1, k: (k, ij(g0, g1)[1]))
    if out_halves:
        nnh = nn // 2
        o_spec = pl.BlockSpec((None, tm, tn), lambda g0, g1, k: (ij(g0, g1)[1] // nnh, ij(g0, g1)[0], ij(g0, g1)[1] % nnh))
        o_shape = jax.ShapeDtypeStruct((2, M, N // 2), out_dtype)
    else:
        o_spec = pl.BlockSpec((tm, tn), lambda g0, g1, k: ij(g0, g1))
        o_shape = jax.ShapeDtypeStruct((M, N), out_dtype)
    in_specs = [a_spec, b_spec]
    args = [a, b]
    if res is not None:
        in_specs.append(pl.BlockSpec((tm, tn), lambda g0, g1, k: ij(g0, g1)))
        args.append(res)

    def body(*refs):
        if res is not None:
            a_ref, b_ref, r_ref, o_ref = refs[:4]
        else:
            a_ref, b_ref, o_ref = refs[:3]
            r_ref = None
        part = lax.dot_general(a_ref[...], b_ref[...], dims, preferred_element_type=F32)
        if nk == 1:
            if r_ref is not None:
                part = part + r_ref[...]
            o_ref[...] = part.astype(out_dtype)
            return
        acc_ref = refs[-1]
        k = pl.program_id(2)

        @pl.when(k == 0)
        def _():
            acc_ref[...] = part

        @pl.when(k > 0)
        def _():
            acc_ref[...] += part

        @pl.when(k == nk - 1)
        def _():
            tot = acc_ref[...]
            if r_ref is not None:
                tot = tot + r_ref[...]
            o_ref[...] = tot.astype(out_dtype)

    grid = (nm, nn, nk) if outer == "i" else (nn, nm, nk)
    scratch = [] if nk == 1 else [pltpu.VMEM((tm, tn), F32)]
    return pl.pallas_call(
        body, out_shape=o_shape, grid=grid, in_specs=in_specs, out_specs=o_spec, scratch_shapes=scratch,
        name=name, compiler_params=_cp(("parallel", "parallel", "arbitrary"), VMEM_LIMIT),
    )(*args)


def _aug(lane, terms):
    out = 0.0
    for j, t in enumerate(terms):
        out = jnp.where(lane == HEAD_DIM + j, t, out)
    return out


def _split3f(x):
    hi, mid, lo = _split3(x)
    return [hi.astype(F32), mid.astype(F32), lo.astype(F32)]


def _inproj_fwd(x, g_mix, w_pad, bf_pad):
    S = x.shape[0]
    tm = _pick(S, (512, 256))
    tri = (lax.broadcasted_iota(jnp.int32, (tm, tm), 0) >= lax.broadcasted_iota(jnp.int32, (tm, tm), 1)).astype(BF16)

    def body(x_ref, g_ref, w_ref, bf_ref, tri_ref, put_ref, one_ref, xn_ref, qa_ref, ka_ref, va_ref, ug_ref, zf_ref,
             carry_ref):
        i = pl.program_id(0)

        @pl.when(i == 0)
        def _():
            carry_ref[...] = jnp.zeros_like(carry_ref)

        xf = x_ref[...]
        r = lax.rsqrt(jnp.mean(xf * xf, axis=-1, keepdims=True) + EPS)
        xn = ((xf * r) * g_ref[...]).astype(BF16)
        xn_ref[...] = xn
        proj = lax.dot_general(xn, w_ref[...], _NT, preferred_element_type=F32)
        ug_ref[...] = proj[:, QKV:UG_END]
        zf = proj[:, UG_END:] + bf_ref[...]
        zf_ref[...] = zf
        lf = jnp.minimum(zf, 0.0) - jnp.log(1.0 + jnp.exp(-jnp.abs(zf)))
        c = _dot3l(tri_ref[...], lf) + carry_ref[0:1, :]
        carry_ref[0:1, :] = c[tm - 1:tm, :]
        c3 = jnp.concatenate(_split3(c), axis=1)
        aug_q = jnp.dot(c3, put_ref[0], preferred_element_type=F32) + one_ref[0:1, :]
        aug_k = jnp.dot(c3, put_ref[1], preferred_element_type=F32) + one_ref[1:2, :]
        lane = lax.broadcasted_iota(jnp.int32, (tm, LANES), 1)
        for h in range(N_HEADS):
            p, odd = h // 2, h % 2

            def head(base, scale=None, p=p, odd=odd):
                blk = proj[:, base + p * LANES:base + (p + 1) * LANES]
                if scale is not None:
                    blk = blk * scale
                return pltpu.roll(blk, HEAD_DIM, 1) if odd else blk

            cols = slice(h * LANES, (h + 1) * LANES)
            qa_ref[:, cols] = jnp.where(lane < HEAD_DIM, head(0, HEAD_DIM ** -0.5), aug_q[:, cols]).astype(BF16)
            ka_ref[:, cols] = jnp.where(lane < HEAD_DIM, head(ATT_WIDTH), aug_k[:, cols]).astype(BF16)
            va_ref[:, cols] = jnp.where(lane < HEAD_DIM, head(2 * ATT_WIDTH), one_ref[2:3, cols]).astype(BF16)

    wide = N_HEADS * LANES
    src = lax.broadcasted_iota(jnp.int32, (3 * LANES, wide), 0)
    col = lax.broadcasted_iota(jnp.int32, (3 * LANES, wide), 1)
    hd, term = src % LANES, src // LANES
    to_q = (col == hd * LANES + HEAD_DIM + term) & (hd < N_HEADS)
    to_k = (col == hd * LANES + HEAD_DIM + 3 + term) & (hd < N_HEADS)
    put = jnp.stack([to_q.astype(BF16), -to_k.astype(BF16)])
    off = lax.broadcasted_iota(jnp.int32, (8, wide), 1) % LANES - HEAD_DIM
    row = lax.broadcasted_iota(jnp.int32, (8, wide), 0)
    q_one = (off >= 3) & (off < 6)
    k_one = ((off >= 0) & (off < 3)) | ((off >= 6) & (off < 9))
    v_one = (off >= 0) & (off < 3)
    ones = jnp.where(row == 0, q_one, jnp.where(row == 1, k_one, (row == 2) & v_one)).astype(F32)
    return pl.pallas_call(
        body,
        out_shape=(jax.ShapeDtypeStruct((S, D_MODEL), BF16), jax.ShapeDtypeStruct((S, wide), BF16),
                   jax.ShapeDtypeStruct((S, wide), BF16), jax.ShapeDtypeStruct((S, wide), BF16),
                   jax.ShapeDtypeStruct((S, 2 * GMLP_WIDTH), F32), jax.ShapeDtypeStruct((S, LANES), F32)),
        grid=(S // tm,),
        in_specs=[pl.BlockSpec((tm, D_MODEL), lambda i: (i, 0)), pl.BlockSpec((1, D_MODEL), lambda i: (0, 0)),
                  pl.BlockSpec((IN_PAD, D_MODEL), lambda i: (0, 0)), pl.BlockSpec((1, LANES), lambda i: (0, 0)),
                  pl.BlockSpec((tm, tm), lambda i: (0, 0)), pl.BlockSpec((2, 3 * LANES, wide), lambda i: (0, 0, 0)),
                  pl.BlockSpec((8, wide), lambda i: (0, 0))],
        out_specs=(pl.BlockSpec((tm, D_MODEL), lambda i: (i, 0)), pl.BlockSpec((tm, wide), lambda i: (i, 0)),
                   pl.BlockSpec((tm, wide), lambda i: (i, 0)), pl.BlockSpec((tm, wide), lambda i: (i, 0)),
                   pl.BlockSpec((tm, 2 * GMLP_WIDTH), lambda i: (i, 0)), pl.BlockSpec((tm, LANES), lambda i: (i, 0))),
        scratch_shapes=[pltpu.VMEM((8, LANES), F32)],
        name="inproj_fwd", compiler_params=_cp(("arbitrary",), VMEM_LIMIT),
    )(x, g_mix, w_pad, bf_pad, tri, put, ones)


def _group_ones():
    r = lax.broadcasted_iota(jnp.int32, (GMLP_WIDTH, GMLP_WIDTH), 0) // GROUP_DIM
    c = lax.broadcasted_iota(jnp.int32, (GMLP_WIDTH, GMLP_WIDTH), 1) // GROUP_DIM
    return (r == c).astype(BF16)


def _gmlp_mixed(vn_bf, w_ref, bias, n_chunks):
    lane = lax.broadcasted_iota(jnp.int32, (CHUNK, LANES), 1)
    row = lax.broadcasted_iota(jnp.int32, (CHUNK, CHUNK), 0)
    col = lax.broadcasted_iota(jnp.int32, (CHUNK, CHUNK), 1)
    ws = [jnp.where(row >= col, w_ref[g], 0.0).astype(BF16) for g in range(N_GROUPS)]
    rows = []
    for ci in range(n_chunks):
        cols = []
        for pp in range(N_GROUPS // 2):
            v = vn_bf[ci * CHUNK:(ci + 1) * CHUNK, pp * LANES:(pp + 1) * LANES]
            v_lo = jnp.where(lane < GROUP_DIM, v, jnp.zeros_like(v))
            v_hi = jnp.where(lane >= GROUP_DIM, v, jnp.zeros_like(v))
            m = (jnp.dot(ws[2 * pp], v_lo, preferred_element_type=F32)
                 + jnp.dot(ws[2 * pp + 1], v_hi, preferred_element_type=F32))
            cols.append(m + bias[:, pp * LANES:(pp + 1) * LANES])
        rows.append(jnp.concatenate(cols, axis=1))
    return jnp.concatenate(rows, axis=0)


def _gmlp_fwd(ug, gain, w_s, bias_full):
    S = ug.shape[0]
    tm = _pick(S, (512, 256, 128))
    ones = _group_ones()

    def body(ug_ref, gain_ref, w_ref, bias_ref, ones_ref, sg_ref):
        u = _gelu(ug_ref[:, :GMLP_WIDTH])
        vr = _gelu(ug_ref[:, GMLP_WIDTH:])
        ms = _dot3(vr * vr, ones_ref[...]) * (1.0 / GROUP_DIM)
        vn = ((vr * lax.rsqrt(ms + EPS)) * gain_ref[...]).astype(BF16)
        mixed = _gmlp_mixed(vn, w_ref, bias_ref[...], tm // CHUNK)
        sg_ref[...] = (u * mixed).astype(BF16)

    return pl.pallas_call(
        body, out_shape=jax.ShapeDtypeStruct((S, GMLP_WIDTH), BF16), grid=(S // tm,),
        in_specs=[pl.BlockSpec((tm, 2 * GMLP_WIDTH), lambda i: (i, 0)), pl.BlockSpec((1, GMLP_WIDTH), lambda i: (0, 0)),
                  pl.BlockSpec((N_GROUPS, CHUNK, CHUNK), lambda i: (0, 0, 0)),
                  pl.BlockSpec((CHUNK, GMLP_WIDTH), lambda i: (0, 0)),
                  pl.BlockSpec((GMLP_WIDTH, GMLP_WIDTH), lambda i: (0, 0))],
        out_specs=pl.BlockSpec((tm, GMLP_WIDTH), lambda i: (i, 0)),
        name="gmlp_fwd", compiler_params=_cp(("parallel",), VMEM_LIMIT),
    )(ug, gain, w_s, bias_full, ones)


_NT = (((1,), (1,)), ((), ()))
_TN = (((0,), (0,)), ((), ()))


def _attn_fwd(qa, ka, va):
    S = qa.shape[0]
    tq = _pick(S, (ATT_TQ, 256))
    tk = min(ATT_TK, tq)
    nq = S // tq
    assert tq == tk, "the diagonal block is handled as one tq x tq tile"
    per_q = 1

    def body(q_ref, k_ref, v_ref, o_ref, lse_ref, ob_ref):
        qi = pl.program_id(1)
        lane = lax.broadcasted_iota(jnp.int32, (tq, LANES), 1)
        qs = [q_ref[:, :LANES], q_ref[:, LANES:]]

        def update(q, ks, k_len, h, m, acc, first_row):
            cols = slice(h * LANES, (h + 1) * LANES)
            s = lax.dot_general(q, k_ref[pl.ds(ks, k_len), cols], _NT, preferred_element_type=F32)
            if first_row is not None:
                rid = lax.broadcasted_iota(jnp.int32, s.shape, 0) + first_row
                s = jnp.where(rid >= lax.broadcasted_iota(jnp.int32, s.shape, 1), s, -jnp.inf)
            m_new = jnp.maximum(m, jnp.max(s, axis=-1, keepdims=True))
            p = jnp.exp(s - m_new).astype(BF16)
            acc = jnp.exp(m - m_new) * acc + jnp.dot(p, v_ref[pl.ds(ks, k_len), cols], preferred_element_type=F32)
            return m_new, acc

        def step(kb, carry):
            ks = pl.multiple_of(kb * tk, tk)
            return tuple(update(qs[h], ks, tk, h, *carry[h], None) for h in range(2))

        def two_steps(t, carry):
            return step(qi % 2 + 2 * t + 1, step(qi % 2 + 2 * t, carry))

        one = (jnp.full((tq, 1), -jnp.inf, F32), jnp.zeros((tq, LANES), F32))
        carry = lax.fori_loop(0, qi % 2, step, (one, one))
        carry = lax.fori_loop(0, qi // 2, two_steps, carry)
        outs, lses = [], []
        diag = pl.multiple_of(qi * tq, tq)
        for h in range(2):
            m, acc = update(qs[h], diag, tk, h, *carry[h], 0)
            l = acc[:, HEAD_DIM:HEAD_DIM + 1]
            outs.append(acc / l)
            lses.append(m + jnp.log(l))
        o = jnp.where(lane < HEAD_DIM, outs[0], pltpu.roll(outs[1], HEAD_DIM, 1))
        o_ref[...] = o
        ob_ref[...] = o.astype(BF16)
        lse_ref[...] = jnp.where(lane < HEAD_DIM, lses[0], lses[1])

    return pl.pallas_call(
        body,
        out_shape=(jax.ShapeDtypeStruct((S, ATT_WIDTH), F32), jax.ShapeDtypeStruct((S, ATT_WIDTH), F32),
                   jax.ShapeDtypeStruct((S, ATT_WIDTH), BF16)),
        grid=(N_PAIRS, nq),
        in_specs=[pl.BlockSpec((tq, 2 * LANES), lambda p, i: (i, p)),
                  pl.BlockSpec((S, 2 * LANES), lambda p, i: (0, p)),
                  pl.BlockSpec((S, 2 * LANES), lambda p, i: (0, p))],
        out_specs=(pl.BlockSpec((tq, LANES), lambda p, i: (i, p)), pl.BlockSpec((tq, LANES), lambda p, i: (i, p)),
                   pl.BlockSpec((tq, LANES), lambda p, i: (i, p))),
        name="attn_fwd", compiler_params=_cp(("parallel", "parallel"), VMEM_LIMIT),
    )(qa, ka, va)


def _shift_rows(x, prev, n):
    rid = lax.broadcasted_iota(jnp.int32, x.shape, 0)
    y = pltpu.roll(x, n, 0)
    if n == 1:
        return jnp.where(rid == 0, prev[7:8, :], y)
    return jnp.where(rid == 0, prev[6:7, :], jnp.where(rid == 1, prev[7:8, :], y))


def _shift_rows_up(x, nxt, n):
    rows = x.shape[0]
    rid = lax.broadcasted_iota(jnp.int32, x.shape, 0)
    y = pltpu.roll(x, rows - n, 0)
    if n == 1:
        return jnp.where(rid == rows - 1, nxt[0:1, :], y)
    return jnp.where(rid == rows - 2, nxt[0:1, :], jnp.where(rid == rows - 1, nxt[1:2, :], y))


def _conv3(cur, prev, w, b):
    return (w[0:1, :] * _shift_rows(cur, prev, 2) + w[1:2, :] * _shift_rows(cur, prev, 1)
            + w[2:3, :] * cur + b)


def _ffn_up_conv(hn, w_up_bf, cw, cb):
    S = hn.shape[0]
    F = D_FF
    tm = _pick(S, (FFN_TM, 256))
    tn = _pick(F, (FFN_TN, 256, 128))
    nj = F // tn

    def body(hn_ref, wa_ref, wg_ref, cw_ref, cb_ref, hu_ref, hc_ref, act_ref, tail_ref):
        i = pl.program_id(1)

        @pl.when(i == 0)
        def _():
            tail_ref[...] = jnp.zeros_like(tail_ref)

        hn_v = hn_ref[...]
        halves = []
        for h, w_ref in enumerate((wa_ref, wg_ref)):
            hu = lax.dot_general(hn_v, w_ref[...], _NT, preferred_element_type=F32)
            hu_ref[h] = hu.astype(BF16)
            hc = _conv3(hu, tail_ref[h], cw_ref[h], cb_ref[h])
            hc_ref[h] = hc
            halves.append(hc)
            tail_ref[h] = hu[tm - 8:, :]
        a, g = halves
        act_ref[...] = (g * _sigmoid(g) * a).astype(BF16)

    both = pl.BlockSpec((2, tm, tn), lambda j, i: (0, i, j))
    return pl.pallas_call(
        body, out_shape=(jax.ShapeDtypeStruct((2, S, F), BF16), jax.ShapeDtypeStruct((2, S, F), F32),
                         jax.ShapeDtypeStruct((S, F), BF16)),
        grid=(nj, S // tm),
        in_specs=[pl.BlockSpec((tm, D_MODEL), lambda j, i: (i, 0)),
                  pl.BlockSpec((tn, D_MODEL), lambda j, i: (j, 0)),
                  pl.BlockSpec((tn, D_MODEL), lambda j, i: (nj + j, 0)),
                  pl.BlockSpec((2, 8, tn), lambda j, i: (0, 0, j)),
                  pl.BlockSpec((2, 1, tn), lambda j, i: (0, 0, j))],
        out_specs=(both, both, pl.BlockSpec((tm, tn), lambda j, i: (i, j))),
        scratch_shapes=[pltpu.VMEM((2, 8, tn), F32)],
        name="ffn_up_conv", compiler_params=_cp(("parallel", "arbitrary"), VMEM_LIMIT),
    )(hn, w_up_bf, w_up_bf, cw, cb)


def _ffn_down_loss(act, w_down_bf, h1, g_final, target):
    S = h1.shape[0]
    tm = _pick(S, (512, 256))

    def body(a_ref, w_ref, h1_ref, g_ref, t_ref, loss_ref, dh_ref, dhb_ref, dg_ref):
        i = pl.program_id(0)

        @pl.when(i == 0)
        def _():
            loss_ref[...] = jnp.zeros_like(loss_ref)
            dg_ref[...] = jnp.zeros_like(dg_ref)

        hf = h1_ref[...] + jnp.dot(a_ref[...], w_ref[...], preferred_element_type=F32)
        g = g_ref[...]
        r = lax.rsqrt(jnp.mean(hf * hf, axis=-1, keepdims=True) + EPS)
        hhat = hf * r
        err = hhat * g - t_ref[...]
        loss_ref[...] += 0.5 * jnp.sum(jnp.mean(err * err, axis=-1, keepdims=True))
        dy = err * (1.0 / D_MODEL)
        dg_ref[0:1, :] += jnp.sum(dy * hhat, axis=0, keepdims=True)
        dhat = dy * g
        dh = r * (dhat - hhat * jnp.mean(dhat * hhat, axis=-1, keepdims=True))
        dh_ref[...] = dh
        dhb_ref[...] = dh.astype(BF16)

    row = pl.BlockSpec((tm, D_MODEL), lambda i: (i, 0))
    return pl.pallas_call(
        body,
        out_shape=(jax.ShapeDtypeStruct((8, LANES), F32), jax.ShapeDtypeStruct((S, D_MODEL), F32),
                   jax.ShapeDtypeStruct((S, D_MODEL), BF16), jax.ShapeDtypeStruct((8, D_MODEL), F32)),
        grid=(S // tm,),
        in_specs=[pl.BlockSpec((tm, D_FF), lambda i: (i, 0)), pl.BlockSpec((D_FF, D_MODEL), lambda i: (0, 0)), row,
                  pl.BlockSpec((1, D_MODEL), lambda i: (0, 0)), row],
        out_specs=(pl.BlockSpec((8, LANES), lambda i: (0, 0)), row, row, pl.BlockSpec((8, D_MODEL), lambda i: (0, 0))),
        name="ffn_down_loss", compiler_params=_cp(("arbitrary",), VMEM_LIMIT),
    )(act, w_down_bf, h1, g_final, target)


def _ffn_up_dx_rms(dhu, w_up_bf, h1, g_ffn, dh2):
    _, S, F = dhu.shape
    tm = _pick(S, (512, 256))

    def body(a_ref, b_ref, h_ref, g_ref, r_ref, dh_ref, dhb_ref, dg_ref):
        i = pl.program_id(0)

        @pl.when(i == 0)
        def _():
            dg_ref[...] = jnp.zeros_like(dg_ref)

        dyv = (jnp.dot(a_ref[0], b_ref[:F, :], preferred_element_type=F32)
               + jnp.dot(a_ref[1], b_ref[F:, :], preferred_element_type=F32))
        hf = h_ref[...]
        r = lax.rsqrt(jnp.mean(hf * hf, axis=-1, keepdims=True) + EPS)
        hhat = hf * r
        dg_ref[0:1, :] += jnp.sum(dyv * hhat, axis=0, keepdims=True)
        dhat = dyv * g_ref[...]
        dh = r_ref[...] + r * (dhat - hhat * jnp.mean(dhat * hhat, axis=-1, keepdims=True))
        dh_ref[...] = dh
        dhb_ref[...] = dh.astype(BF16)

    row = pl.BlockSpec((tm, D_MODEL), lambda i: (i, 0))
    return pl.pallas_call(
        body,
        out_shape=(jax.ShapeDtypeStruct((S, D_MODEL), F32), jax.ShapeDtypeStruct((S, D_MODEL), BF16),
                   jax.ShapeDtypeStruct((8, D_MODEL), F32)),
        grid=(S // tm,),
        in_specs=[pl.BlockSpec((2, tm, F), lambda i: (0, i, 0)), pl.BlockSpec((2 * F, D_MODEL), lambda i: (0, 0)),
                  row, pl.BlockSpec((1, D_MODEL), lambda i: (0, 0)), row],
        out_specs=(row, row, pl.BlockSpec((8, D_MODEL), lambda i: (0, 0))),
        name="ffn_up_dx_rms", compiler_params=_cp(("arbitrary",), VMEM_LIMIT),
    )(dhu, w_up_bf, h1, g_ffn, dh2)


def _conv_gate_bwd(hc, hu, dact, cw):
    _, S, F = hu.shape
    tm = _pick(S, (CONV_TM, 128))
    tn = _pick(F, (CONV_TN, 256, 128))
    r8 = tm // 8
    n_i = S // tm
    last8 = S // 8 - 1

    def body(hc_ref, hcn_ref, hu_ref, da_ref, dan_ref, w_ref, dhu_ref, dcw_ref):
        i = pl.program_id(1)

        @pl.when(i == 0)
        def _():
            dcw_ref[...] = jnp.zeros_like(dcw_ref)

        rid8 = lax.broadcasted_iota(jnp.int32, (8, tn), 0)

        def gate_grads(a, g, d):
            sg = _sigmoid(g)
            return d * (g * sg), d * a * (sg * (1.0 + g * (1.0 - sg)))

        dhc = gate_grads(hc_ref[0], hc_ref[1], da_ref[...])
        dhc_n = gate_grads(hcn_ref[0], hcn_ref[1], dan_ref[...])
        for h in range(2):
            w = w_ref[h]
            d = dhc[h]
            dn = jnp.where(i < n_i - 1, dhc_n[h], 0.0)
            u1 = _shift_rows_up(d, dn, 1)
            u2 = _shift_rows_up(d, dn, 2)
            dhu_ref[h] = (w[2:3, :] * d + w[1:2, :] * u1 + w[0:1, :] * u2).astype(BF16)
            x = hu_ref[h].astype(F32)
            t0, t1, t2, t3 = [jnp.sum(t, axis=0, keepdims=True) for t in (u2 * x, u1 * x, d * x, d)]
            dcw_ref[h] += jnp.where(rid8 == 0, t0, jnp.where(rid8 == 1, t1, jnp.where(rid8 == 2, t2, jnp.where(rid8 == 3, t3, 0.0))))

    cur = pl.BlockSpec((2, tm, tn), lambda j, i: (0, i, j))
    return pl.pallas_call(
        body,
        out_shape=(jax.ShapeDtypeStruct((2, S, F), BF16), jax.ShapeDtypeStruct((2, 8, F), F32)),
        grid=(F // tn, n_i),
        in_specs=[cur, pl.BlockSpec((2, 8, tn), lambda j, i: (0, jnp.minimum((i + 1) * r8, last8), j)), cur,
                  pl.BlockSpec((tm, tn), lambda j, i: (i, j)),
                  pl.BlockSpec((8, tn), lambda j, i: (jnp.minimum((i + 1) * r8, last8), j)),
                  pl.BlockSpec((2, 8, tn), lambda j, i: (0, 0, j))],
        out_specs=(cur, pl.BlockSpec((2, 8, tn), lambda j, i: (0, 0, j))),
        name="conv_gate_bwd", compiler_params=_cp(("parallel", "arbitrary"), VMEM_LIMIT),
    )(hc, hc, hu, dact, dact, cw)


def _out_proj_dx_prep(dh1_bf, w_out_bf, att, lse, qa):
    S = att.shape[0]
    tm = _pick(S, (256,))

    def body(dh_ref, w_ref, o_ref, lse_ref, q_ref, dsg_ref, qb_ref, doa_ref):
        lane = lax.broadcasted_iota(jnp.int32, (tm, LANES), 1)
        dh = dh_ref[...]
        dsg_ref[...] = lax.dot_general(dh, w_ref[ATT_WIDTH:, :], _NT, preferred_element_type=F32)
        datt = lax.dot_general(dh, w_ref[:ATT_WIDTH, :], _NT, preferred_element_type=F32)
        for p in range(N_PAIRS):
            pc = slice(p * LANES, (p + 1) * LANES)
            do = datt[:, pc]
            prod = o_ref[:, pc] * do
            for hh in range(2):
                sel = (lane >= HEAD_DIM) if hh else (lane < HEAD_DIM)
                delta = jnp.sum(jnp.where(sel, prod, 0.0), axis=-1, keepdims=True)
                dod = pltpu.roll(do, HEAD_DIM, 1) if hh else do
                cols = slice((2 * p + hh) * LANES, (2 * p + hh + 1) * LANES)
                doa_ref[:, cols] = jnp.where(lane < HEAD_DIM, dod, _aug(lane, _split3f(-delta))).astype(BF16)
                lcol = p * LANES + hh * HEAD_DIM
                l3 = _split3f(-lse_ref[:, lcol:lcol + 1])
                augl = jnp.where(lane == HEAD_DIM + 6, l3[0], jnp.where(lane == HEAD_DIM + 7, l3[1], l3[2])).astype(BF16)
                qb_ref[:, cols] = jnp.where((lane >= HEAD_DIM + 6) & (lane < HEAD_DIM + 9), augl, q_ref[:, cols])

    half = pl.BlockSpec((tm, ATT_WIDTH), lambda i: (i, 0))
    wide = pl.BlockSpec((tm, N_HEADS * LANES), lambda i: (i, 0))
    return pl.pallas_call(
        body,
        out_shape=(jax.ShapeDtypeStruct((S, GMLP_WIDTH), F32), jax.ShapeDtypeStruct(qa.shape, BF16),
                   jax.ShapeDtypeStruct(qa.shape, BF16)),
        grid=(S // tm,),
        in_specs=[pl.BlockSpec((tm, D_MODEL), lambda i: (i, 0)), pl.BlockSpec((D_MODEL, D_MODEL), lambda i: (0, 0)),
                  half, half, wide],
        out_specs=(half, wide, wide),
        name="out_proj_dx_prep", compiler_params=_cp(("parallel",), VMEM_LIMIT),
    )(dh1_bf, w_out_bf, att, lse, qa)


def _attn_bwd(qb, ka, va, doa):
    S = qb.shape[0]
    tk = _pick(S, (512, 256))
    tq = tk
    nq = S // tq

    def pair(a, scale=None):
        lane = lax.broadcasted_iota(jnp.int32, (a.shape[0], LANES), 1)
        out = jnp.where(lane < HEAD_DIM, a[:, :LANES], pltpu.roll(a[:, LANES:], HEAD_DIM, 1))
        return out if scale is None else out * scale

    def head_lanes(a, col, sign, first):
        lane = lax.broadcasted_iota(jnp.int32, (a.shape[0], LANES), 1)
        return jnp.where(lane == first, sign * a[:, col:col + 1],
                         jnp.where(lane == first + 1, sign * a[:, LANES + col:LANES + col + 1], 0.0))

    def body(q_ref, do_ref, k_ref, v_ref, dqc_ref, dkc_ref, dvc_ref, dcq_ref, dck_ref, dq_ref, dka_ref, dva_ref):
        kb = pl.program_id(1)

        @pl.when(kb == 0)
        def _():
            dq_ref[...] = jnp.zeros_like(dq_ref)

        dka_ref[...] = jnp.zeros_like(dka_ref)
        dva_ref[...] = jnp.zeros_like(dva_ref)
        def sub_tile(qs, q_len, k_off, k_len, masked):
            keys = slice(k_off, k_off + k_len)
            for h in range(2):
                cols = slice(h * LANES, (h + 1) * LANES)
                qblk = q_ref[pl.ds(qs, q_len), cols]
                doblk = do_ref[pl.ds(qs, q_len), cols]
                kh = k_ref[keys, cols]
                p = jnp.exp(lax.dot_general(kh, qblk, _NT, preferred_element_type=F32))
                if masked:
                    p = jnp.where(lax.broadcasted_iota(jnp.int32, p.shape, 1) >= lax.broadcasted_iota(jnp.int32, p.shape, 0),
                                  p, 0.0)
                ds = (p * lax.dot_general(v_ref[keys, cols], doblk, _NT, preferred_element_type=F32)).astype(BF16)
                dva_ref[keys, cols] += jnp.dot(p.astype(BF16), doblk, preferred_element_type=F32)
                dka_ref[keys, cols] += jnp.dot(ds, qblk, preferred_element_type=F32)
                dq_ref[pl.ds(qs, q_len), cols] += lax.dot_general(ds, kh, _TN, preferred_element_type=F32)

        half = tk // 2
        sub_tile(pl.multiple_of(kb * tq, tq), tq, 0, half, True)
        sub_tile(pl.multiple_of(kb * tq + half, half), half, half, half, True)

        rest = nq - 1 - kb
        odd = rest % 2

        @pl.when(odd == 1)
        def _():
            sub_tile(pl.multiple_of((kb + 1) * tq, tq), tq, 0, tk, False)

        def step(t, carry):
            sub_tile(pl.multiple_of((kb + 1 + odd + 2 * t) * tq, tq), 2 * tq, 0, tk, False)
            return carry

        lax.fori_loop(0, rest // 2, step, 0)
        dka = dka_ref[...]
        dkc_ref[...] = pair(dka).astype(BF16)
        dvc_ref[...] = pair(dva_ref[...]).astype(BF16)
        first = 2 * pl.program_id(0)
        dck_ref[...] = head_lanes(dka, HEAD_DIM + 3, -1.0, first)

        @pl.when(kb == nq - 1)
        def _():
            dqa = dq_ref[...]
            dqc_ref[...] = pair(dqa, HEAD_DIM ** -0.5).astype(BF16)
            dcq_ref[...] = head_lanes(dqa, HEAD_DIM, 1.0, first)

    wide = 2 * LANES
    half = jax.ShapeDtypeStruct((S, ATT_WIDTH), BF16)
    slabs = jax.ShapeDtypeStruct((N_PAIRS, S, LANES), F32)
    return pl.pallas_call(
        body,
        out_shape=(half, half, half, slabs, slabs),
        grid=(N_PAIRS, nq),
        in_specs=[pl.BlockSpec((S, wide), lambda p, j: (0, p)), pl.BlockSpec((S, wide), lambda p, j: (0, p)),
                  pl.BlockSpec((tk, wide), lambda p, j: (j, p)), pl.BlockSpec((tk, wide), lambda p, j: (j, p))],
        out_specs=(pl.BlockSpec((S, LANES), lambda p, j: (0, p)), pl.BlockSpec((tk, LANES), lambda p, j: (j, p)),
                   pl.BlockSpec((tk, LANES), lambda p, j: (j, p)), pl.BlockSpec((None, S, LANES), lambda p, j: (p, 0, 0)),
                   pl.BlockSpec((None, tk, LANES), lambda p, j: (p, j, 0))),
        scratch_shapes=[pltpu.VMEM((S, wide), F32), pltpu.VMEM((tk, wide), F32), pltpu.VMEM((tk, wide), F32)],
        name="attn_bwd", compiler_params=_cp(("parallel", "arbitrary"), VMEM_LIMIT),
    )(qb, doa, ka, va)


def _gmlp_bwd(ug, dsg, gain, w_s, wt_s, bias_full):
    S = ug.shape[0]
    tm = _pick(S, (512, 256, 128))
    n_chunks = tm // CHUNK
    n_i = S // tm
    ones = _group_ones()
    nt = (((1,), (1,)), ((), ()))

    def body(ug_ref, dsg_ref, gain_ref, w_ref, wt_ref, bias_ref, ones_ref, dug_ref, dw_ref, dgain_ref, dbias_ref,
             dbacc_ref):
        i = pl.program_id(0)

        @pl.when(i == 0)
        def _():
            dw_ref[...] = jnp.zeros_like(dw_ref)
            dgain_ref[...] = jnp.zeros_like(dgain_ref)
            dbacc_ref[...] = jnp.zeros_like(dbacc_ref)

        ones_m = ones_ref[...]
        pu = ug_ref[:, :GMLP_WIDTH]
        pg = ug_ref[:, GMLP_WIDTH:]
        u = _gelu(pu)
        vr = _gelu(pg)
        ms = _dot3(vr * vr, ones_m) * (1.0 / GROUP_DIM)
        rinv = lax.rsqrt(ms + EPS)
        vhat = vr * rinv
        gain_v = gain_ref[...]
        vn = (vhat * gain_v).astype(BF16)
        mixed = _gmlp_mixed(vn, w_ref, bias_ref[...], n_chunks)
        dsg_v = dsg_ref[...]
        du = dsg_v * mixed
        dmixed = dsg_v * u
        dm_bf = dmixed.astype(BF16)
        lane = lax.broadcasted_iota(jnp.int32, (CHUNK, LANES), 1)
        row = lax.broadcasted_iota(jnp.int32, (CHUNK, CHUNK), 0)
        col = lax.broadcasted_iota(jnp.int32, (CHUNK, CHUNK), 1)
        wts = [jnp.where(col >= row, wt_ref[g], 0.0).astype(BF16) for g in range(N_GROUPS)]
        dvn_rows = []
        dbsum = jnp.zeros((CHUNK, GMLP_WIDTH), F32)
        for ci in range(n_chunks):
            rs = slice(ci * CHUNK, (ci + 1) * CHUNK)
            dbsum = dbsum + dmixed[rs, :]
            cols = []
            for pp in range(N_GROUPS // 2):
                cs = slice(pp * LANES, (pp + 1) * LANES)
                dm = dm_bf[rs, cs]
                dm_lo = jnp.where(lane < GROUP_DIM, dm, jnp.zeros_like(dm))
                dm_hi = jnp.where(lane >= GROUP_DIM, dm, jnp.zeros_like(dm))
                vb = vn[rs, cs]
                dw_ref[2 * pp] += lax.dot_general(dm_lo, vb, nt, preferred_element_type=F32)
                dw_ref[2 * pp + 1] += lax.dot_general(dm_hi, vb, nt, preferred_element_type=F32)
                cols.append(jnp.dot(wts[2 * pp], dm_lo, preferred_element_type=F32)
                            + jnp.dot(wts[2 * pp + 1], dm_hi, preferred_element_type=F32))
            dvn_rows.append(jnp.concatenate(cols, axis=1))
        dvn = jnp.concatenate(dvn_rows, axis=0)
        dbacc_ref[...] += dbsum
        dgain_ref[0:1, :] += jnp.sum(dvn * vhat, axis=0, keepdims=True)
        dvhat = dvn * gain_v
        gm = _dot3(dvhat * vhat, ones_m) * (1.0 / GROUP_DIM)
        dvr = rinv * (dvhat - vhat * gm)
        dug_ref[:, :GMLP_WIDTH] = (du * _gelu_grad(pu)).astype(BF16)
        dug_ref[:, GMLP_WIDTH:] = (dvr * _gelu_grad(pg)).astype(BF16)

        @pl.when(i == n_i - 1)
        def _():
            for g in range(N_GROUPS):
                dw_ref[g] = jnp.where(row >= col, dw_ref[g], 0.0)
            dbias_ref[...] = _dot3(dbacc_ref[...], ones_m)

    return pl.pallas_call(
        body,
        out_shape=(jax.ShapeDtypeStruct((S, 2 * GMLP_WIDTH), BF16), jax.ShapeDtypeStruct((N_GROUPS, CHUNK, CHUNK), F32),
                   jax.ShapeDtypeStruct((8, GMLP_WIDTH), F32), jax.ShapeDtypeStruct((CHUNK, GMLP_WIDTH), F32)),
        grid=(n_i,),
        in_specs=[pl.BlockSpec((tm, 2 * GMLP_WIDTH), lambda i: (i, 0)), pl.BlockSpec((tm, GMLP_WIDTH), lambda i: (i, 0)),
                  pl.BlockSpec((1, GMLP_WIDTH), lambda i: (0, 0)),
                  pl.BlockSpec((N_GROUPS, CHUNK, CHUNK), lambda i: (0, 0, 0)),
                  pl.BlockSpec((N_GROUPS, CHUNK, CHUNK), lambda i: (0, 0, 0)),
                  pl.BlockSpec((CHUNK, GMLP_WIDTH), lambda i: (0, 0)),
                  pl.BlockSpec((GMLP_WIDTH, GMLP_WIDTH), lambda i: (0, 0))],
        out_specs=(pl.BlockSpec((tm, 2 * GMLP_WIDTH), lambda i: (i, 0)),
                   pl.BlockSpec((N_GROUPS, CHUNK, CHUNK), lambda i: (0, 0, 0)),
                   pl.BlockSpec((8, GMLP_WIDTH), lambda i: (0, 0)),
                   pl.BlockSpec((CHUNK, GMLP_WIDTH), lambda i: (0, 0))),
        scratch_shapes=[pltpu.VMEM((CHUNK, GMLP_WIDTH), F32)],
        name="gmlp_bwd", compiler_params=_cp(("arbitrary",), VMEM_LIMIT),
    )(ug, dsg, gain, w_s, wt_s, bias_full, ones)


def _gate_bwd(dcq, dck, zf):
    S = zf.shape[0]
    tm = _pick(S, (256,))
    n_i = S // tm
    triu = (lax.broadcasted_iota(jnp.int32, (tm, tm), 0) <= lax.broadcasted_iota(jnp.int32, (tm, tm), 1)).astype(BF16)

    def body(dcq_ref, dck_ref, zf_ref, tri_ref, dzf_ref, dbf_ref, carry_ref):
        i = pl.program_id(0)

        @pl.when(i == 0)
        def _():
            carry_ref[...] = jnp.zeros_like(carry_ref)
            dbf_ref[...] = jnp.zeros_like(dbf_ref)

        lane = lax.broadcasted_iota(jnp.int32, (tm, LANES), 1)
        dc = dcq_ref[0] + dck_ref[0]
        for p in range(1, N_PAIRS):
            dc = dc + (dcq_ref[p] + dck_ref[p])
        dlf = _dot3l(tri_ref[...], dc) + carry_ref[0:1, :]
        carry_ref[0:1, :] = dlf[0:1, :]
        dz = jnp.where(lane < N_HEADS, dlf * _sigmoid(-zf_ref[...]), 0.0)
        dzf_ref[...] = dz.astype(BF16)
        dbf_ref[0:1, :] += jnp.sum(dz, axis=0, keepdims=True)

    return pl.pallas_call(
        body,
        out_shape=(jax.ShapeDtypeStruct((S, LANES), BF16), jax.ShapeDtypeStruct((8, LANES), F32)),
        grid=(n_i,),
        in_specs=[pl.BlockSpec((N_PAIRS, tm, LANES), lambda i: (0, n_i - 1 - i, 0)),
                  pl.BlockSpec((N_PAIRS, tm, LANES), lambda i: (0, n_i - 1 - i, 0)),
                  pl.BlockSpec((tm, LANES), lambda i: (n_i - 1 - i, 0)),
                  pl.BlockSpec((tm, tm), lambda i: (0, 0))],
        out_specs=(pl.BlockSpec((tm, LANES), lambda i: (n_i - 1 - i, 0)), pl.BlockSpec((8, LANES), lambda i: (0, 0))),
        scratch_shapes=[pltpu.VMEM((8, LANES), F32)],
        name="gate_bwd", compiler_params=_cp(("arbitrary",), VMEM_LIMIT),
    )(dcq, dck, zf, triu)


def _out_proj_fwd(att_bf, sg, w_out_bf, x, g_ffn):
    S = x.shape[0]
    tm = _pick(S, (512, 256))

    def body(a_ref, s_ref, w_ref, x_ref, g_ref, h_ref, hn_ref):
        h = (x_ref[...] + jnp.dot(a_ref[...], w_ref[:ATT_WIDTH, :], preferred_element_type=F32)
             + jnp.dot(s_ref[...], w_ref[ATT_WIDTH:, :], preferred_element_type=F32))
        h_ref[...] = h
        r = lax.rsqrt(jnp.mean(h * h, axis=-1, keepdims=True) + EPS)
        hn_ref[...] = ((h * r) * g_ref[...]).astype(BF16)

    row = pl.BlockSpec((tm, D_MODEL), lambda i: (i, 0))
    half = pl.BlockSpec((tm, ATT_WIDTH), lambda i: (i, 0))
    return pl.pallas_call(
        body, out_shape=(jax.ShapeDtypeStruct((S, D_MODEL), F32), jax.ShapeDtypeStruct((S, D_MODEL), BF16)),
        grid=(S // tm,),
        in_specs=[half, half, pl.BlockSpec((D_MODEL, D_MODEL), lambda i: (0, 0)), row,
                  pl.BlockSpec((1, D_MODEL), lambda i: (0, 0))],
        out_specs=(row, row), name="out_proj", compiler_params=_cp(("parallel",), VMEM_LIMIT),
    )(att_bf, sg, w_out_bf, x, g_ffn)


def _out_proj_dw(att_bf, sg, dh1_bf):
    S = att_bf.shape[0]
    tk = _pick(S, (1024, 512))

    def body(a_ref, s_ref, d_ref, o_ref):
        k = pl.program_id(0)

        @pl.when(k == 0)
        def _():
            o_ref[...] = jnp.zeros_like(o_ref)

        d = d_ref[...]
        o_ref[:ATT_WIDTH, :] += lax.dot_general(a_ref[...], d, _TN, preferred_element_type=F32)
        o_ref[ATT_WIDTH:, :] += lax.dot_general(s_ref[...], d, _TN, preferred_element_type=F32)

    half = pl.BlockSpec((tk, ATT_WIDTH), lambda k: (k, 0))
    return pl.pallas_call(
        body, out_shape=jax.ShapeDtypeStruct((D_MODEL, D_MODEL), F32), grid=(S // tk,),
        in_specs=[half, half, pl.BlockSpec((tk, D_MODEL), lambda k: (k, 0))],
        out_specs=pl.BlockSpec((D_MODEL, D_MODEL), lambda k: (0, 0)),
        name="out_proj_dw", compiler_params=_cp(("arbitrary",), VMEM_LIMIT),
    )(att_bf, sg, dh1_bf)


_IN_PIECES = ((0, ATT_WIDTH), (ATT_WIDTH, ATT_WIDTH), (2 * ATT_WIDTH, ATT_WIDTH), (QKV, 2 * GMLP_WIDTH), (UG_END, LANES))


def _inproj_bwd_dx(pieces, w_pad, x, g_mix, dh1):
    S = x.shape[0]
    tm = _pick(S, (512, 256))

    def body(*refs):
        p_refs, (w_ref, x_ref, g_ref, r_ref, dx_ref, dg_ref) = refs[:5], refs[5:]
        i = pl.program_id(0)

        @pl.when(i == 0)
        def _():
            dg_ref[...] = jnp.zeros_like(dg_ref)

        dxn = None
        for p_ref, (c0, width) in zip(p_refs, _IN_PIECES):
            part = jnp.dot(p_ref[...], w_ref[c0:c0 + width, :], preferred_element_type=F32)
            dxn = part if dxn is None else dxn + part
        xf = x_ref[...]
        r = lax.rsqrt(jnp.mean(xf * xf, axis=-1, keepdims=True) + EPS)
        xhat = xf * r
        dg_ref[0:1, :] += jnp.sum(dxn * xhat, axis=0, keepdims=True)
        dhat = dxn * g_ref[...]
        dx_ref[...] = r_ref[...] + r * (dhat - xhat * jnp.mean(dhat * xhat, axis=-1, keepdims=True))

    row = pl.BlockSpec((tm, D_MODEL), lambda i: (i, 0))
    return pl.pallas_call(
        body, out_shape=(jax.ShapeDtypeStruct((S, D_MODEL), F32), jax.ShapeDtypeStruct((8, D_MODEL), F32)),
        grid=(S // tm,),
        in_specs=[pl.BlockSpec((tm, width), lambda i: (i, 0)) for _, width in _IN_PIECES]
        + [pl.BlockSpec((IN_PAD, D_MODEL), lambda i: (0, 0)), row, pl.BlockSpec((1, D_MODEL), lambda i: (0, 0)), row],
        out_specs=(row, pl.BlockSpec((8, D_MODEL), lambda i: (0, 0))),
        name="in_proj_dx", compiler_params=_cp(("arbitrary",), VMEM_LIMIT),
    )(*pieces, w_pad, x, g_mix, dh1)


def _inproj_bwd_dw(xn, pieces):
    S = xn.shape[0]
    tk = _pick(S, (1024, 512))

    def body(*refs):
        x_ref, p_refs, o_ref = refs[0], refs[1:6], refs[6]
        k = pl.program_id(0)

        @pl.when(k == 0)
        def _():
            o_ref[...] = jnp.zeros_like(o_ref)

        xb = x_ref[...]
        for p_ref, (c0, width) in zip(p_refs, _IN_PIECES):
            o_ref[:, c0:c0 + width] += lax.dot_general(xb, p_ref[...], _TN, preferred_element_type=F32)

    return pl.pallas_call(
        body, out_shape=jax.ShapeDtypeStruct((D_MODEL, IN_PAD), F32), grid=(S // tk,),
        in_specs=[pl.BlockSpec((tk, D_MODEL), lambda k: (k, 0))]
        + [pl.BlockSpec((tk, width), lambda k: (k, 0)) for _, width in _IN_PIECES],
        out_specs=pl.BlockSpec((D_MODEL, IN_PAD), lambda k: (0, 0)),
        name="in_proj_dw", compiler_params=_cp(("arbitrary",), VMEM_LIMIT),
    )(xn, *pieces)


def _adamw(w, m, v, parts, name):
    R, C = w.shape[-2:]
    tr = R
    for cand in (256, 128, 64, 32, 16, 8):
        if R % cand == 0 and R > cand:
            tr = cand
            break
    c1 = 1.0 / (1.0 - ADAM_B1 ** ADAM_STEP)
    c2 = 1.0 / (1.0 - ADAM_B2 ** ADAM_STEP)

    def body(w_ref, m_ref, v_ref, p_ref, g_ref, d_ref, nm_ref, nv_ref):
        g = p_ref[0].astype(F32)
        for j in range(1, N_DEV):
            g = g + p_ref[j].astype(F32)
        g_ref[...] = g
        nm = ADAM_B1 * m_ref[...] + (1.0 - ADAM_B1) * g
        nv = ADAM_B2 * v_ref[...] + (1.0 - ADAM_B2) * (g * g)
        nm_ref[...] = nm
        nv_ref[...] = nv
        d_ref[...] = -ADAM_LR * ((nm * c1) / (jnp.sqrt(nv * c2) + ADAM_EPS) + ADAM_WD * w_ref[...])

    if w.ndim == 3:
        spec = pl.BlockSpec((None, tr, C), lambda i: (0, i, 0))
    else:
        spec = pl.BlockSpec((tr, C), lambda i: (i, 0))
    shp = jax.ShapeDtypeStruct(w.shape, F32)
    return pl.pallas_call(
        body, out_shape=(shp, shp, shp, shp), grid=(R // tr,),
        in_specs=[spec, spec, spec, pl.BlockSpec((N_DEV, tr, C), lambda i: (0, i, 0))],
        out_specs=(spec, spec, spec, spec),
        name=name, compiler_params=_cp(("parallel",), VMEM_LIMIT),
    )(w, m, v, parts)


def _adamw_owner(w, m, v, landed, sent, me, name):
    R, C = w.shape[-2:]
    tr = R
    for cand in (256, 128, 64, 32, 16, 8):
        if R % cand == 0 and R > cand:
            tr = cand
            break
    c1 = 1.0 / (1.0 - ADAM_B1 ** ADAM_STEP)
    c2 = 1.0 / (1.0 - ADAM_B2 ** ADAM_STEP)

    def body(me_ref, w_ref, m_ref, v_ref, p_ref, own_ref, g_ref, d_ref, nm_ref, nv_ref):
        mine = me_ref[0]
        own = own_ref[...].astype(F32)
        g = jnp.where(mine == 0, own, p_ref[0].astype(F32))
        for j in range(1, N_DEV):
            g = g + jnp.where(mine == j, own, p_ref[j].astype(F32))
        g_ref[...] = g
        nm = ADAM_B1 * m_ref[...] + (1.0 - ADAM_B1) * g
        nv = ADAM_B2 * v_ref[...] + (1.0 - ADAM_B2) * (g * g)
        nm_ref[...] = nm
        nv_ref[...] = nv
        d_ref[...] = -ADAM_LR * ((nm * c1) / (jnp.sqrt(nv * c2) + ADAM_EPS) + ADAM_WD * w_ref[...])

    spec = pl.BlockSpec((None, tr, C), lambda i, me_ref: (0, i, 0))
    shp = jax.ShapeDtypeStruct(w.shape, F32)
    return pl.pallas_call(
        body, out_shape=(shp, shp, shp, shp),
        grid_spec=pltpu.PrefetchScalarGridSpec(
            num_scalar_prefetch=1, grid=(R // tr,),
            in_specs=[spec, spec, spec, pl.BlockSpec((N_DEV, tr, C), lambda i, me_ref: (0, i, 0)),
                      pl.BlockSpec((None, tr, C), lambda i, me_ref: (me_ref[0], i, 0))],
            out_specs=(spec, spec, spec, spec)),
        name=name, compiler_params=_cp(("parallel",), VMEM_LIMIT),
    )(jnp.reshape(me, (1,)).astype(jnp.int32), w, m, v, landed, sent)


def _adamw_packed(w, m, v, parts, sizes, name):
    R = w.shape[0]
    assert R == sum(sizes)
    c1 = 1.0 / (1.0 - ADAM_B1 ** ADAM_STEP)
    c2 = 1.0 / (1.0 - ADAM_B2 ** ADAM_STEP)
    n = len(sizes)

    def body(w_ref, m_ref, v_ref, p_ref, *out_refs):
        g = p_ref[0]
        for j in range(1, N_DEV):
            g = g + p_ref[j]
        nm = ADAM_B1 * m_ref[...] + (1.0 - ADAM_B1) * g
        nv = ADAM_B2 * v_ref[...] + (1.0 - ADAM_B2) * (g * g)
        d = -ADAM_LR * ((nm * c1) / (jnp.sqrt(nv * c2) + ADAM_EPS) + ADAM_WD * w_ref[...])
        for kind, val in enumerate((g, d, nm, nv)):
            off = 0
            for k, rows in enumerate(sizes):
                out_refs[kind * n + k][...] = val[off:off + rows, :]
                off += rows

    whole = pl.BlockSpec((R, LANES), lambda i: (0, 0))
    shapes = [jax.ShapeDtypeStruct((rows, LANES), F32) for rows in sizes] * 4
    res = pl.pallas_call(
        body, out_shape=tuple(shapes), grid=(1,),
        in_specs=[whole, whole, whole, pl.BlockSpec((N_DEV, R, LANES), lambda i: (0, 0, 0))],
        out_specs=tuple(pl.BlockSpec((rows, LANES), lambda i: (0, 0)) for rows in sizes) * 4,
        name=name, compiler_params=_cp(("arbitrary",), VMEM_LIMIT),
    )(w, m, v, parts)
    return [list(res[kind * n:(kind + 1) * n]) for kind in range(4)]


def _place():
    x, y, c = lax.axis_index("x"), lax.axis_index("y"), lax.axis_index("c")
    return x, y, c


def _all_gather(blocks, name):
    n = len(blocks)

    def body(*refs):
        ins, outs = refs[:n], refs[n:2 * n]
        send_sems, recv_sems, local_sems = refs[2 * n:]
        x, y, c = _place()
        me, sibling = (x, y, c), (x, y, 1 - c)
        chips = [(1 - x, y), (x, 1 - y), (1 - x, 1 - y)]
        sends = []
        for a in range(n):
            out = outs[a]

            def slot(px, py, pc, out=out):
                return out.at[4 * px + 2 * py + pc]

            def copy(k, block, to, src=None, a=a, slot=slot):
                return pltpu.make_async_remote_copy(
                    src_ref=slot(*block) if src is None else src, dst_ref=slot(*block),
                    send_sem=send_sems.at[a, k], recv_sem=recv_sems.at[a, k], device_id=to, device_id_type=MESH)

            mine = pltpu.make_async_copy(ins[a], slot(*me), local_sems.at[a])
            mine.start()
            first = [copy(0, me, sibling, src=ins[a])]
            first += [copy(1 + j, me, (*chip, c), src=ins[a]) for j, chip in enumerate(chips)]
            for cp in first:
                cp.start()
            sends.append((mine, first, copy))
        for a in range(n):
            mine, first, copy = sends[a]
            passed = [copy(4 + j, (*chip, c), sibling) for j, chip in enumerate(chips)]
            for j, chip in enumerate(chips):
                copy(1 + j, (*chip, c), me).wait_recv()
                passed[j].start()
            copy(0, sibling, me).wait_recv()
            for j, chip in enumerate(chips):
                copy(4 + j, (*chip, 1 - c), me).wait_recv()
            for cp in first + passed:
                cp.wait_send()
            mine.wait()

    any_spec = pl.BlockSpec(memory_space=pl.ANY)
    return pl.pallas_call(
        body, out_shape=tuple(jax.ShapeDtypeStruct((N_DEV,) + b.shape, b.dtype) for b in blocks),
        in_specs=[any_spec] * n, out_specs=tuple([any_spec] * n),
        scratch_shapes=[pltpu.SemaphoreType.DMA((n, 7)), pltpu.SemaphoreType.DMA((n, 7)), pltpu.SemaphoreType.DMA((n,))],
        name=name,
    )(*blocks)


_HBM = pl.BlockSpec(memory_space=pltpu.HBM)
_SEM = pl.BlockSpec(memory_space=pltpu.SEMAPHORE)
_EFFECT = pltpu.SideEffectType.DATAFLOW_SIDE_EFFECTING


def _peers(x, y, c):
    out = []
    for k in range(1, N_DEV):
        px, py, pc = x ^ ((k >> 2) & 1), y ^ ((k >> 1) & 1), c ^ (k & 1)
        out.append((k, (px, py, pc), 4 * px + 2 * py + pc))
    return out


def _xchg_copies(src_refs, land_refs, send_sems, recv_sems, scatter):
    x, y, c = _place()
    me = 4 * x + 2 * y + c
    copies = []
    for a, (src, land) in enumerate(zip(src_refs, land_refs)):
        for k, place, idx in _peers(x, y, c):
            j = a * (N_DEV - 1) + k - 1
            copies.append(pltpu.make_async_remote_copy(
                src_ref=src.at[idx] if scatter[a] else src, dst_ref=land.at[me],
                send_sem=send_sems[j], recv_sem=recv_sems[j], device_id=place, device_id_type=MESH))
    return copies


def _xchg_start(srcs, scatter, name):
    n = len(srcs)
    lands = [lax.empty((N_DEV,) + (s.shape[1:] if sc else s.shape), s.dtype) for s, sc in zip(srcs, scatter)]

    ns = n * (N_DEV - 1)

    def body(*refs):
        sems = refs[2 * n:2 * n + 2 * ns]
        for cp in _xchg_copies(refs[:n], refs[n:2 * n], sems[:ns], sems[ns:], scatter):
            cp.start()
        token = refs[-1]
        token[...] = jnp.zeros_like(token)

    both = list(srcs) + lands
    res = pl.pallas_call(
        body, name=name,
        out_shape=(*[pltpu.SemaphoreType.DMA(())] * (2 * ns),
                   *[pltpu.HBM(a.shape, a.dtype) for a in both], jax.ShapeDtypeStruct((8, LANES), F32)),
        in_specs=[_HBM] * (2 * n),
        out_specs=(*([_SEM] * (2 * ns)), *([_HBM] * (2 * n)), pl.BlockSpec(memory_space=pltpu.VMEM)),
        input_output_aliases={i: 2 * ns + i for i in range(2 * n)},
        compiler_params=pltpu.CompilerParams(has_side_effects=_EFFECT),
    )(*[pltpu.with_memory_space_constraint(a, pltpu.HBM) for a in both])
    return (tuple(res[:2 * ns]), tuple(res[2 * ns:2 * ns + 2 * n])), res[-1]


def _xchg_wait(handle, scatter, after, name):
    sems, thru = handle
    n = len(thru) // 2
    ns = len(sems) // 2

    def body(*refs):
        got = refs[2 * n:2 * n + 2 * ns]
        for cp in _xchg_copies(refs[:n], refs[n:2 * n], got[:ns], got[ns:], scatter):
            cp.wait_send()
            cp.wait_recv()

    outs = pl.pallas_call(
        body, name=name, out_shape=tuple(pltpu.HBM(a.shape, a.dtype) for a in thru),
        in_specs=[_HBM] * (2 * n) + [_SEM] * (2 * ns) + [pl.BlockSpec(memory_space=pl.ANY)],
        out_specs=tuple([_HBM] * (2 * n)), input_output_aliases={i: i for i in range(2 * n)},
        compiler_params=pltpu.CompilerParams(has_side_effects=_EFFECT),
    )(*thru, *sems, after)
    return outs[:n], outs[n:]


def _tie(a, token):
    return a if token is None else a + token[0, 0].astype(a.dtype)


def _rows128(a):
    flat = a.reshape(-1)
    rows = -(-flat.shape[0] // LANES)
    rows = -(-rows // 8) * 8
    return jnp.pad(flat, (0, rows * LANES - flat.shape[0])).reshape(rows, LANES)


def _local_step(x, target, norm_mix_g, w_in_t, b_forget, gmlp_norm_g, w_spatial, b_spatial, norm_ffn_g, conv_b,
                norm_final_g, rest_fn, send_fn, small_fn, token=None):
    f = D_FF
    g_mix = norm_mix_g.reshape(1, D_MODEL)
    w_pad = jnp.pad(w_in_t, ((0, IN_PAD - IN_COLS), (0, 0)))
    bf_pad = jnp.pad(b_forget.reshape(1, N_HEADS), ((0, 0), (0, LANES - N_HEADS)))
    xn, qa, ka, va, ug, zf = _inproj_fwd(x, _tie(g_mix, token), w_pad, bf_pad)
    bias_full = jnp.repeat(b_spatial.reshape(N_GROUPS, CHUNK).T, GROUP_DIM, axis=1)
    w_s = w_spatial.reshape(N_GROUPS, CHUNK, CHUNK)
    gain = gmlp_norm_g.reshape(1, GMLP_WIDTH)
    sg = _gmlp_fwd(ug, gain, w_s, bias_full)
    att, lse, att_bf = _attn_fwd(qa, ka, va)
    w_out_bf, w_up_bf, conv_w, w_down_bf = rest_fn(att_bf)
    g_ffn = norm_ffn_g.reshape(1, D_MODEL)
    h1, hn = _out_proj_fwd(att_bf, sg, w_out_bf, x, g_ffn)
    cw = jnp.pad(conv_w.reshape(3, 2, f).transpose(1, 0, 2), ((0, 0), (0, 5), (0, 0)))
    cb = conv_b.reshape(2, 1, f)
    hu, hc, act = _ffn_up_conv(hn, w_up_bf, cw, cb)
    loss_blk, dh2, dh2_bf, dg_final = _ffn_down_loss(act, w_down_bf, h1, norm_final_g.reshape(1, D_MODEL), target)
    dw_down = _mm(act, dh2_bf, mode="tn", out_dtype=F32, tm=1408, tn=1024, tk=2048, name="ffn_down_dw")
    dact = _mm(dh2_bf, w_down_bf, mode="nt", out_dtype=F32, tm=1024, tn=1408, tk=1024, outer="j", name="ffn_down_dx")
    dhu, dcw = _conv_gate_bwd(hc, hu, dact, _tie(cw, send_fn("w_down", dw_down)))
    dw_up = _mm(hn, dhu, mode="tn", out_dtype=F32, tm=1024, tn=1408, tk=2048, b_halves=True, outer="j", name="ffn_up_dw")
    dh1, dh1_bf, dg_ffn = _ffn_up_dx_rms(dhu, w_up_bf, h1, _tie(g_ffn, send_fn("w_up", dw_up)), dh2)
    dsg, qb, doa = _out_proj_dx_prep(dh1_bf, w_out_bf, att, lse, qa)
    dw_out = _out_proj_dw(att_bf, sg, dh1_bf)
    dq, dk, dv, dcq, dck = _attn_bwd(qb, ka, va, doa)
    wt_s = w_s.transpose(0, 2, 1)
    dug, dw_s, dgain, dbias = _gmlp_bwd(ug, dsg, _tie(gain, send_fn("w_out", dw_out)), w_s, wt_s, bias_full)
    dzf, dbf = _gate_bwd(dcq, dck, _tie(zf, send_fn("w_spatial", dw_s)))
    dw_in = _inproj_bwd_dw(xn, (dq, dk, dv, dug, dzf))
    sent_in = send_fn("w_in", dw_in[:, :IN_COLS])
    grad_x, dg_mix = _inproj_bwd_dx((dq, dk, dv, dug, _tie(dzf, sent_in)), w_pad, x, g_mix, dh1)
    grads = dict(
        norm_mix_g=dg_mix[0:1, :],
        b_forget=dbf[0:1, :N_HEADS],
        gmlp_norm_g=dgain[0:1, :],
        w_spatial=dw_s,
        b_spatial=dbias[:, ::GROUP_DIM].T,
        norm_ffn_g=dg_ffn[0:1, :],
        conv_w=dcw[:, 0:3, :].transpose(1, 0, 2).reshape(3, 2 * f),
        conv_b=dcw[:, 3, :].reshape(1, 2 * f),
        norm_final_g=dg_final[0, :],
    )
    return grad_x, small_fn(loss_blk[0, 0], grads)


SMALL = ("norm_mix_g", "b_forget", "gmlp_norm_g", "b_spatial", "norm_ffn_g", "conv_b", "norm_final_g")
PACKED = ("w_spatial",) + SMALL


def kernel(x, norm_mix_g, w_in, b_forget, gmlp_norm_g, w_spatial, b_spatial, w_out, norm_ffn_g, w_up, conv_w, conv_b, w_down, norm_final_g, loss_target, m_norm_mix_g, m_w_in, m_b_forget, m_gmlp_norm_g, m_w_spatial, m_b_spatial, m_w_out, m_norm_ffn_g, m_w_up, m_conv_w, m_conv_b, m_w_down, m_norm_final_g, v_norm_mix_g, v_w_in, v_b_forget, v_gmlp_norm_g, v_w_spatial, v_b_spatial, v_w_out, v_norm_ffn_g, v_w_up, v_conv_w, v_conv_b, v_w_down, v_norm_final_g):
    weights = dict(norm_mix_g=norm_mix_g, w_in=w_in, b_forget=b_forget, gmlp_norm_g=gmlp_norm_g, w_spatial=w_spatial,
                   b_spatial=b_spatial, w_out=w_out, norm_ffn_g=norm_ffn_g, w_up=w_up, conv_w=conv_w, conv_b=conv_b,
                   w_down=w_down, norm_final_g=norm_final_g)
    m_in = dict(norm_mix_g=m_norm_mix_g, w_in=m_w_in, b_forget=m_b_forget, gmlp_norm_g=m_gmlp_norm_g,
                w_spatial=m_w_spatial, b_spatial=m_b_spatial, w_out=m_w_out, norm_ffn_g=m_norm_ffn_g, w_up=m_w_up,
                conv_w=m_conv_w, conv_b=m_conv_b, w_down=m_w_down, norm_final_g=m_norm_final_g)
    v_in = dict(norm_mix_g=v_norm_mix_g, w_in=v_w_in, b_forget=v_b_forget, gmlp_norm_g=v_gmlp_norm_g,
                w_spatial=v_w_spatial, b_spatial=v_b_spatial, w_out=v_w_out, norm_ffn_g=v_norm_ffn_g, w_up=v_w_up,
                conv_w=v_conv_w, conv_b=v_conv_b, w_down=v_w_down, norm_final_g=v_norm_final_g)
    order = list(weights)
    me = 4 * lax.axis_index("x") + 2 * lax.axis_index("y") + lax.axis_index("c")
    n_in, n_up = w_in.shape[2], w_up.shape[2]
    r_out, r_down = w_out.shape[1], w_down.shape[1]

    def with_mine(landed, mine):
        return lax.dynamic_update_index_in_dim(landed, mine, me, 0)

    up_blk = w_up[0].T.astype(BF16)
    out_blk = w_out[0].astype(BF16)
    down_blk = w_down[0].astype(BF16)
    taps_blk = jnp.pad(conv_w[0], ((0, 5), (0, 0)))
    (in_all,) = _all_gather([w_in[0].T.astype(BF16)], "gather_w_in")
    in_all, rest_blocks = lax.optimization_barrier((in_all, [up_blk, out_blk, down_blk, taps_blk]))
    rest_handle, token = _xchg_start(rest_blocks, [False] * 4, "gather_rest_start")
    w_in_t = in_all.reshape(N_DEV * n_in, D_MODEL)

    def rest_fn(after):
        mine, landed = _xchg_wait(rest_handle, [False] * 4, after, "gather_rest_wait")
        up_all, out_all, down_all, taps_all = [with_mine(l, b) for l, b in zip(landed, mine)]
        return (out_all.reshape(N_DEV * r_out, D_MODEL), up_all.reshape(N_DEV * n_up, D_MODEL),
                taps_all[:, :3, :].transpose(1, 0, 2).reshape(3, N_DEV * n_up),
                down_all.reshape(N_DEV * r_down, D_MODEL))

    sent = {}

    def send_fn(name, grad):
        if name == "w_in":
            parts = grad.reshape(D_MODEL, N_DEV, -1).transpose(1, 0, 2).astype(BF16)
        elif name == "w_up":
            parts = grad.reshape(D_MODEL, N_DEV, -1).transpose(1, 0, 2)
        elif name == "w_spatial":
            sent[name], tok = _xchg_start([_rows128(grad)], [False], "gather_w_spatial_start")
            return tok
        else:
            parts = grad.reshape(N_DEV, -1, D_MODEL)
        sent[name], tok = _xchg_start([parts], [True], "scatter_" + name + "_start")
        return tok

    small = {}

    def small_fn(loss_local, g):
        loss_rows = jnp.pad(loss_local.reshape(1, 1), ((0, 31), (0, LANES - 1)))
        packed = [_rows128(g[k]) for k in SMALL] + [loss_rows, _rows128(g["conv_w"])]
        small["sizes"] = [p.shape[0] for p in packed]
        small["handle"], tok = _xchg_start([jnp.concatenate(packed, axis=0)], [False], "gather_small_start")
        return tok

    grad_x, after = _local_step(
        x[0], loss_target[0], norm_mix_g, w_in_t, b_forget, gmlp_norm_g, w_spatial, b_spatial, norm_ffn_g, conv_b,
        norm_final_g, rest_fn, send_fn, small_fn, token)

    outs = {}

    def update_big(name, after):
        (parts,), (landed,) = _xchg_wait(sent[name], [True], after, "scatter_" + name + "_wait")
        outs[name] = tuple(_adamw_owner(weights[name], m_in[name], v_in[name], landed, parts, me, "adamw_" + name))
        return outs[name][0]

    for name in ("w_down", "w_up", "w_out"):
        after = update_big(name, after)

    (ws_mine,), (ws_landed,) = _xchg_wait(sent["w_spatial"], [False], after, "gather_w_spatial_wait")
    (mine,), (landed,) = _xchg_wait(small["handle"], [False], ws_landed, "gather_small_wait")
    small_all = jnp.concatenate([with_mine(ws_landed, ws_mine), with_mine(landed, mine)], axis=1)
    sizes = [ws_mine.shape[0]] + small["sizes"]
    n_small_rows = sum(sizes[:-2])

    def pack(src):
        return jnp.concatenate([_rows128(src[k]) for k in PACKED] + [jnp.zeros((sizes[-2], LANES), F32)], axis=0)

    n_adam_rows = n_small_rows + sizes[-2]
    per_kind = _adamw_packed(pack(weights), pack(m_in), pack(v_in), small_all, sizes[:-1], "adamw_small")
    loss = per_kind[0][-1][0, 0]
    for j, k in enumerate(PACKED):
        shp = weights[k].shape
        cnt = math.prod(shp)
        outs[k] = tuple(kind[j].reshape(-1)[:cnt].reshape(shp) for kind in per_kind)
    sg_ = per_kind[0][0]
    taps_parts = small_all[:, n_adam_rows:, :].reshape(N_DEV, -1)[:, :3 * N_DEV * n_up].reshape(N_DEV, 3, N_DEV * n_up)
    taps_mine = lax.dynamic_slice_in_dim(taps_parts, me * n_up, n_up, axis=2)
    taps_mine = jnp.pad(taps_mine, ((0, 0), (0, 5), (0, 0)))

    def pad8(a):
        return jnp.pad(a[0], ((0, 5), (0, 0)))

    res = _adamw(pad8(conv_w), pad8(m_conv_w), pad8(v_conv_w), taps_mine, "adamw_conv_w")
    outs["conv_w"] = tuple(a[:3][None] for a in res)
    update_big("w_in", sg_)

    return (loss, grad_x[None], *[outs[k][0] for k in order], *[outs[k][1] for k in order],
            *[outs[k][2] for k in order], *[outs[k][3] for k in order])
```
